```python
import math
import jax, jax.numpy as jnp
from jax import lax
import numpy as np

D_MODEL = 1024
BATCH = 8
SEQ = 4096
DEPTH = 1

HEAD_DIM = 64
MIX_WIDTH = D_MODEL
POOL_WIDTH = MIX_WIDTH // 4
POOL_GROUPS = 4
POOL_GROUP_DIM = POOL_WIDTH // POOL_GROUPS
POOL_WINDOWS = (2, 4, 8, 16)
FOX_WIDTH = MIX_WIDTH // 2
FOX_HEADS = FOX_WIDTH // HEAD_DIM
MEM_HEADS = 4
MEM_WIDTH = MEM_HEADS * HEAD_DIM
N_MEM = 256
Q_BLOCK = 128
EPS = 1e-6
SPLIT_SIZES = (POOL_WIDTH, POOL_WIDTH, FOX_WIDTH, FOX_WIDTH, FOX_WIDTH, FOX_HEADS, FOX_WIDTH, MEM_WIDTH, MEM_WIDTH)
IN_WIDTH = sum(SPLIT_SIZES)

kernel_name = "hymba_pool_fox_memory_layer"


def rms_norm(x, g):
    xf = x.astype(jnp.float32)
    y = xf * lax.rsqrt(jnp.mean(xf * xf, axis=-1, keepdims=True) + EPS)
    return (y * g.astype(jnp.float32)).astype(x.dtype)


def split_cols(proj):
    parts, off = [], 0
    for w in SPLIT_SIZES:
        parts.append(proj[..., off:off + w])
        off += w
    return parts


def to_heads(t, n_heads):
    b, s, _ = t.shape
    return t.reshape(b, s, n_heads, HEAD_DIM).transpose(0, 2, 1, 3)


def from_heads(t):
    b, h, s, d = t.shape
    return t.transpose(0, 2, 1, 3).reshape(b, s, h * d)


def pool_mixer(u, w_pool, scale):
    b, s, _ = u.shape
    uf = u.reshape(b, s, POOL_GROUPS, POOL_GROUP_DIM).astype(jnp.float32)
    csum = jnp.cumsum(uf, axis=1)
    pos = jnp.arange(1, s + 1, dtype=jnp.float32)
    pooled = []
    for g, w in enumerate(POOL_WINDOWS):
        cg = csum[:, :, g]
        lag = jnp.pad(cg, ((0, 0), (w, 0), (0, 0)))[:, :s]
        count = jnp.minimum(pos, float(w))
        pooled.append((cg - lag) / count[None, :, None])
    pooled = jnp.stack(pooled, axis=2)
    d = (pooled - uf).astype(u.dtype)
    y = jnp.einsum('bsgc,gce->bsge', d, w_pool).reshape(b, s, POOL_WIDTH)
    return y * scale


def fox_attention(q, k, v, logf):
    b, h, s, dh = q.shape
    n_blocks = s // Q_BLOCK
    F = jnp.cumsum(logf.astype(jnp.float32), axis=-1)
    qb = q.reshape(b, h, n_blocks, Q_BLOCK, dh).transpose(2, 0, 1, 3, 4)
    Fb = F.reshape(b, h, n_blocks, Q_BLOCK).transpose(2, 0, 1, 3)
    pos_k = jnp.arange(s)
    scale = 1.0 / math.sqrt(dh)

    def block(args):
        qi, Fi, i = args
        logits = jnp.einsum('bhqd,bhkd->bhqk', qi, k).astype(jnp.float32) * scale
        logits = logits + Fi[..., None] - F[:, :, None, :]
        pos_q = i * Q_BLOCK + jnp.arange(Q_BLOCK)
        causal = pos_k[None, :] <= pos_q[:, None]
        logits = jnp.where(causal, logits, -1e30)
        p = jax.nn.softmax(logits, axis=-1)
        return jnp.einsum('bhqk,bhkd->bhqd', p.astype(v.dtype), v)

    out = lax.map(block, (qb, Fb, jnp.arange(n_blocks)))
    return out.transpose(1, 2, 0, 3, 4).reshape(b, h, s, dh)


def memory_attention(q, k, v):
    scale = 1.0 / math.sqrt(q.shape[-1])
    logits = jnp.einsum('bhqd,bhmd->bhqm', q, k).astype(jnp.float32) * scale
    p = jax.nn.softmax(logits, axis=-1)
    return jnp.einsum('bhqm,bhmd->bhqd', p.astype(v.dtype), v)


def _fwd_setup_inputs(seed: int = 0) -> dict:
    key = jax.random.key(seed)
    ks = jax.random.split(key, 16)
    f32 = jnp.float32
    x = jax.random.normal(ks[0], (BATCH, SEQ, D_MODEL), f32)
    mem = jax.random.normal(ks[1], (BATCH, N_MEM, D_MODEL), f32)
    norm_g = 1.0 + 0.02 * jax.random.normal(ks[2], (DEPTH, D_MODEL), f32)
    w_in = jax.random.normal(ks[3], (DEPTH, D_MODEL, IN_WIDTH), f32) * D_MODEL ** -0.5
    b_f = 1.0 + 4.0 * jax.random.uniform(ks[4], (DEPTH, FOX_HEADS), f32)
    w_pool = jax.random.normal(ks[5], (DEPTH, POOL_GROUPS, POOL_GROUP_DIM, POOL_GROUP_DIM), f32) * POOL_GROUP_DIM ** -0.5
    pool_scale = 1.0 + 0.02 * jax.random.normal(ks[6], (DEPTH, POOL_WIDTH), f32)
    fox_q_g = 1.0 + 0.02 * jax.random.normal(ks[7], (DEPTH, HEAD_DIM), f32)
    fox_k_g = 1.0 + 0.02 * jax.random.normal(ks[8], (DEPTH, HEAD_DIM), f32)
    mem_norm_g = 1.0 + 0.02 * jax.random.normal(ks[9], (DEPTH, D_MODEL), f32)
    w_mem_kv = jax.random.normal(ks[10], (DEPTH, D_MODEL, 2 * MEM_WIDTH), f32) * D_MODEL ** -0.5
    mem_q_g = 1.0 + 0.02 * jax.random.normal(ks[11], (DEPTH, HEAD_DIM), f32)
    mem_k_g = 1.0 + 0.02 * jax.random.normal(ks[12], (DEPTH, HEAD_DIM), f32)
    w_out = jax.random.normal(ks[13], (DEPTH, MIX_WIDTH, D_MODEL), f32) * MIX_WIDTH ** -0.5
    return {"x": x, "mem": mem, "norm_g": norm_g, "w_in": w_in, "b_f": b_f,
            "w_pool": w_pool, "pool_scale": pool_scale, "fox_q_g": fox_q_g,
            "fox_k_g": fox_k_g, "mem_norm_g": mem_norm_g, "w_mem_kv": w_mem_kv,
            "mem_q_g": mem_q_g, "mem_k_g": mem_k_g, "w_out": w_out}


def _fwd_reference(x, mem, norm_g, w_in, b_f, w_pool, pool_scale, fox_q_g, fox_k_g,
              mem_norm_g, w_mem_kv, mem_q_g, mem_k_g, w_out):
    for l in range(DEPTH):
        h = rms_norm(x, norm_g[l])
        proj = jnp.einsum('bsd,de->bse', h, w_in[l])
        u_a, g_a, q_b, k_b, v_b, f_b, g_b, q_m, g_m = split_cols(proj)

        y_a = pool_mixer(u_a, w_pool[l], pool_scale[l])

        q = rms_norm(to_heads(q_b, FOX_HEADS), fox_q_g[l])
        k = rms_norm(to_heads(k_b, FOX_HEADS), fox_k_g[l])
        v = to_heads(v_b, FOX_HEADS)
        logf = jax.nn.log_sigmoid((f_b + b_f[l]).astype(jnp.float32)).transpose(0, 2, 1)
        y_b = from_heads(fox_attention(q, k, v, logf))

        mem_n = rms_norm(mem, mem_norm_g[l])
        kv = jnp.einsum('bmd,de->bme', mem_n, w_mem_kv[l])
        k_m = rms_norm(to_heads(kv[..., :MEM_WIDTH], MEM_HEADS), mem_k_g[l])
        v_m = to_heads(kv[..., MEM_WIDTH:], MEM_HEADS)
        q_mh = rms_norm(to_heads(q_m, MEM_HEADS), mem_q_g[l])
        y_m = from_heads(memory_attention(q_mh, k_m, v_m))

        mixed = jnp.concatenate([y_a * jax.nn.silu(g_a),
                                 y_b * jax.nn.silu(g_b),
                                 y_m * jax.nn.silu(g_m)], axis=-1)
        x = x + jnp.einsum('bse,ed->bsd', mixed, w_out[l])
    return x


import jax as _jax
import jax.numpy as _jnp

TWIN_FORMAT = 'train_step'
FWD_PARAMS = ['x', 'mem', 'norm_g', 'w_in', 'b_f', 'w_pool', 'pool_scale', 'fox_q_g', 'fox_k_g', 'mem_norm_g', 'w_mem_kv', 'mem_q_g', 'mem_k_g', 'w_out']
TWIN_WEIGHTS = ['norm_g', 'w_in', 'b_f', 'w_pool', 'pool_scale', 'fox_q_g', 'fox_k_g', 'mem_norm_g', 'w_mem_kv', 'mem_q_g', 'mem_k_g', 'w_out']
TWIN_DIFF_INPUT = 'x'
TWIN_INPUTS = ['x', 'mem', 'norm_g', 'w_in', 'b_f', 'w_pool', 'pool_scale', 'fox_q_g', 'fox_k_g', 'mem_norm_g', 'w_mem_kv', 'mem_q_g', 'mem_k_g', 'w_out', 'loss_target', 'm_norm_g', 'm_w_in', 'm_b_f', 'm_w_pool', 'm_pool_scale', 'm_fox_q_g', 'm_fox_k_g', 'm_mem_norm_g', 'm_w_mem_kv', 'm_mem_q_g', 'm_mem_k_g', 'm_w_out', 'v_norm_g', 'v_w_in', 'v_b_f', 'v_w_pool', 'v_pool_scale', 'v_fox_q_g', 'v_fox_k_g', 'v_mem_norm_g', 'v_w_mem_kv', 'v_mem_q_g', 'v_mem_k_g', 'v_w_out']
TWIN_OUTPUTS = ['loss', 'grad_x', 'grad_norm_g', 'grad_w_in', 'grad_b_f', 'grad_w_pool', 'grad_pool_scale', 'grad_fox_q_g', 'grad_fox_k_g', 'grad_mem_norm_g', 'grad_w_mem_kv', 'grad_mem_q_g', 'grad_mem_k_g', 'grad_w_out', 'delta_norm_g', 'delta_w_in', 'delta_b_f', 'delta_w_pool', 'delta_pool_scale', 'delta_fox_q_g', 'delta_fox_k_g', 'delta_mem_norm_g', 'delta_w_mem_kv', 'delta_mem_q_g', 'delta_mem_k_g', 'delta_w_out', 'new_m_norm_g', 'new_m_w_in', 'new_m_b_f', 'new_m_w_pool', 'new_m_pool_scale', 'new_m_fox_q_g', 'new_m_fox_k_g', 'new_m_mem_norm_g', 'new_m_w_mem_kv', 'new_m_mem_q_g', 'new_m_mem_k_g', 'new_m_w_out', 'new_v_norm_g', 'new_v_w_in', 'new_v_b_f', 'new_v_w_pool', 'new_v_pool_scale', 'new_v_fox_q_g', 'new_v_fox_k_g', 'new_v_mem_norm_g', 'new_v_w_mem_kv', 'new_v_mem_q_g', 'new_v_mem_k_g', 'new_v_w_out']
TWIN_LEAF_KINDS = {'loss': 'loss', 'grad_x': 'grad_x', 'grad_norm_g': 'grad_w', 'grad_w_in': 'grad_w', 'grad_b_f': 'grad_w', 'grad_w_pool': 'grad_w', 'grad_pool_scale': 'grad_w', 'grad_fox_q_g': 'grad_w', 'grad_fox_k_g': 'grad_w', 'grad_mem_norm_g': 'grad_w', 'grad_w_mem_kv': 'grad_w', 'grad_mem_q_g': 'grad_w', 'grad_mem_k_g': 'grad_w', 'grad_w_out': 'grad_w', 'delta_norm_g': 'delta_w', 'delta_w_in': 'delta_w', 'delta_b_f': 'delta_w', 'delta_w_pool': 'delta_w', 'delta_pool_scale': 'delta_w', 'delta_fox_q_g': 'delta_w', 'delta_fox_k_g': 'delta_w', 'delta_mem_norm_g': 'delta_w', 'delta_w_mem_kv': 'delta_w', 'delta_mem_q_g': 'delta_w', 'delta_mem_k_g': 'delta_w', 'delta_w_out': 'delta_w', 'new_m_norm_g': 'new_m', 'new_m_w_in': 'new_m', 'new_m_b_f': 'new_m', 'new_m_w_pool': 'new_m', 'new_m_pool_scale': 'new_m', 'new_m_fox_q_g': 'new_m', 'new_m_fox_k_g': 'new_m', 'new_m_mem_norm_g': 'new_m', 'new_m_w_mem_kv': 'new_m', 'new_m_mem_q_g': 'new_m', 'new_m_mem_k_g': 'new_m', 'new_m_w_out': 'new_m', 'new_v_norm_g': 'new_v', 'new_v_w_in': 'new_v', 'new_v_b_f': 'new_v', 'new_v_w_pool': 'new_v', 'new_v_pool_scale': 'new_v', 'new_v_fox_q_g': 'new_v', 'new_v_fox_k_g': 'new_v', 'new_v_mem_norm_g': 'new_v', 'new_v_w_mem_kv': 'new_v', 'new_v_mem_q_g': 'new_v', 'new_v_mem_k_g': 'new_v', 'new_v_w_out': 'new_v'}


def _forward(args):
    return _fwd_reference(*[args[k] for k in FWD_PARAMS])


def _output_shape():
    out = _jax.eval_shape(lambda: _forward(_fwd_setup_inputs(0)))
    return out.shape, out.dtype

N_MICROBATCH = 1
ADAM_LR = 0.001
ADAM_B1 = 0.9
ADAM_B2 = 0.999
ADAM_EPS = 1e-08
ADAM_WD = 0.01
ADAM_STEP = 10
PER_EXAMPLE_BATCH_AXIS = {'x': 0, 'mem': 0, 'loss_target': 0}
SHARED_INPUTS = []
_WEIGHT_DTYPES = {'norm_g': _jnp.float32, 'w_in': _jnp.float32, 'b_f': _jnp.float32, 'w_pool': _jnp.float32, 'pool_scale': _jnp.float32, 'fox_q_g': _jnp.float32, 'fox_k_g': _jnp.float32, 'mem_norm_g': _jnp.float32, 'w_mem_kv': _jnp.float32, 'mem_q_g': _jnp.float32, 'mem_k_g': _jnp.float32, 'w_out': _jnp.float32}
MOMENT_SCALE = {'norm_g': 6.264293e+00, 'w_in': 1.301425e-01, 'b_f': 3.127945e+01, 'w_pool': 9.244533e-01, 'pool_scale': 8.798907e+00, 'fox_q_g': 4.023393e+00, 'fox_k_g': 4.027224e+00, 'mem_norm_g': 2.691007e-02, 'w_mem_kv': 1.984339e-02, 'mem_q_g': 4.414029e-01, 'mem_k_g': 4.378973e-01, 'w_out': 1.233164e-01}


def _to_microbatches(a, axis):
    t = _jnp.moveaxis(a, axis, 0)
    t = t.reshape((N_MICROBATCH, t.shape[0] // N_MICROBATCH) + t.shape[1:])
    return _jnp.moveaxis(t, 1, axis + 1)


def setup_inputs(seed: int = 0) -> dict:
    inp = _fwd_setup_inputs(seed)
    key = _jax.random.fold_in(_jax.random.key(seed), 7919)
    shape, _ = _output_shape()
    out = dict(inp)
    out["loss_target"] = _jax.random.normal(_jax.random.fold_in(key, 0), shape, _jnp.float32)
    for i, name in enumerate(TWIN_WEIGHTS):
        w = inp[name].astype(_jnp.float32)
        if MOMENT_SCALE is None:
            s = _jnp.sqrt(_jnp.mean(_jnp.square(w)) + 1e-30)
        else:
            s = MOMENT_SCALE[name]
        km, kv = _jax.random.split(_jax.random.fold_in(key, i + 1))
        out[name] = w
        out["m_" + name] = s * _jax.random.normal(km, w.shape, _jnp.float32)
        out["v_" + name] = (s * s) * _jax.random.uniform(kv, w.shape, _jnp.float32, 0.5, 1.5)
    if N_MICROBATCH > 1:
        for name, axis in PER_EXAMPLE_BATCH_AXIS.items():
            out[name] = _to_microbatches(out[name], axis)
    return {'x': out['x'], 'mem': out['mem'], 'norm_g': out['norm_g'], 'w_in': out['w_in'], 'b_f': out['b_f'], 'w_pool': out['w_pool'], 'pool_scale': out['pool_scale'], 'fox_q_g': out['fox_q_g'], 'fox_k_g': out['fox_k_g'], 'mem_norm_g': out['mem_norm_g'], 'w_mem_kv': out['w_mem_kv'], 'mem_q_g': out['mem_q_g'], 'mem_k_g': out['mem_k_g'], 'w_out': out['w_out'], 'loss_target': out['loss_target'], 'm_norm_g': out['m_norm_g'], 'm_w_in': out['m_w_in'], 'm_b_f': out['m_b_f'], 'm_w_pool': out['m_w_pool'], 'm_pool_scale': out['m_pool_scale'], 'm_fox_q_g': out['m_fox_q_g'], 'm_fox_k_g': out['m_fox_k_g'], 'm_mem_norm_g': out['m_mem_norm_g'], 'm_w_mem_kv': out['m_w_mem_kv'], 'm_mem_q_g': out['m_mem_q_g'], 'm_mem_k_g': out['m_mem_k_g'], 'm_w_out': out['m_w_out'], 'v_norm_g': out['v_norm_g'], 'v_w_in': out['v_w_in'], 'v_b_f': out['v_b_f'], 'v_w_pool': out['v_w_pool'], 'v_pool_scale': out['v_pool_scale'], 'v_fox_q_g': out['v_fox_q_g'], 'v_fox_k_g': out['v_fox_k_g'], 'v_mem_norm_g': out['v_mem_norm_g'], 'v_w_mem_kv': out['v_w_mem_kv'], 'v_mem_q_g': out['v_mem_q_g'], 'v_mem_k_g': out['v_mem_k_g'], 'v_w_out': out['v_w_out']}


def _loss(weights, diff, rest, loss_target):
    with _jax.named_scope("forward"):
        args = {**rest, TWIN_DIFF_INPUT: diff, **{k: w.astype(_WEIGHT_DTYPES[k]) for k, w in weights.items()}}
        y = _forward(args)
    with _jax.named_scope("loss_head"):
        err = _jnp.square(y.astype(_jnp.float32) - loss_target)
        return 0.5 * _jnp.sum(_jnp.mean(err, axis=-1)) if err.ndim else 0.5 * err


def _adamw(w, g, m, v):
    m = ADAM_B1 * m + (1.0 - ADAM_B1) * g
    v = ADAM_B2 * v + (1.0 - ADAM_B2) * _jnp.square(g)
    m_hat = m / (1.0 - ADAM_B1 ** ADAM_STEP)
    v_hat = v / (1.0 - ADAM_B2 ** ADAM_STEP)
    delta = -ADAM_LR * (m_hat / (_jnp.sqrt(v_hat) + ADAM_EPS) + ADAM_WD * w)
    return delta, m, v


def reference(x, mem, norm_g, w_in, b_f, w_pool, pool_scale, fox_q_g, fox_k_g, mem_norm_g, w_mem_kv, mem_q_g, mem_k_g, w_out, loss_target, m_norm_g, m_w_in, m_b_f, m_w_pool, m_pool_scale, m_fox_q_g, m_fox_k_g, m_mem_norm_g, m_w_mem_kv, m_mem_q_g, m_mem_k_g, m_w_out, v_norm_g, v_w_in, v_b_f, v_w_pool, v_pool_scale, v_fox_q_g, v_fox_k_g, v_mem_norm_g, v_w_mem_kv, v_mem_q_g, v_mem_k_g, v_w_out):
    given = dict(x=x, mem=mem, norm_g=norm_g, w_in=w_in, b_f=b_f, w_pool=w_pool, pool_scale=pool_scale, fox_q_g=fox_q_g, fox_k_g=fox_k_g, mem_norm_g=mem_norm_g, w_mem_kv=w_mem_kv, mem_q_g=mem_q_g, mem_k_g=mem_k_g, w_out=w_out, loss_target=loss_target, m_norm_g=m_norm_g, m_w_in=m_w_in, m_b_f=m_b_f, m_w_pool=m_w_pool, m_pool_scale=m_pool_scale, m_fox_q_g=m_fox_q_g, m_fox_k_g=m_fox_k_g, m_mem_norm_g=m_mem_norm_g, m_w_mem_kv=m_w_mem_kv, m_mem_q_g=m_mem_q_g, m_mem_k_g=m_mem_k_g, m_w_out=m_w_out, v_norm_g=v_norm_g, v_w_in=v_w_in, v_b_f=v_b_f, v_w_pool=v_w_pool, v_pool_scale=v_pool_scale, v_fox_q_g=v_fox_q_g, v_fox_k_g=v_fox_k_g, v_mem_norm_g=v_mem_norm_g, v_w_mem_kv=v_w_mem_kv, v_mem_q_g=v_mem_q_g, v_mem_k_g=v_mem_k_g, v_w_out=v_w_out)
    weights = {n: given[n] for n in TWIN_WEIGHTS}
    shared = {n: given[n] for n in SHARED_INPUTS}
    per_example = {n: given[n] for n in ['x', 'mem']}
    grad_fn = _jax.value_and_grad(_loss, argnums=(0, 1))

    def one_microbatch(ex, loss_target):
        ex = dict(ex)
        diff = ex.pop(TWIN_DIFF_INPUT)
        return grad_fn(weights, diff, {**shared, **ex}, loss_target)

    if N_MICROBATCH == 1:
        loss, (grad_w, grad_x) = one_microbatch(per_example, given["loss_target"])
    else:
        def body(carry, xs):
            loss_sum, grad_sum = carry
            l_k, (gw_k, gx_k) = one_microbatch(xs[0], xs[1])
            with _jax.named_scope("update"):
                return (loss_sum + l_k, _jax.tree.map(_jnp.add, grad_sum, gw_k)), gx_k

        init = (_jnp.zeros((), _jnp.float32), _jax.tree.map(_jnp.zeros_like, weights))
        (loss, grad_w), grad_x = _jax.lax.scan(body, init, (per_example, given["loss_target"]))
    with _jax.named_scope("update"):
        delta_w, new_m, new_v = {}, {}, {}
        for n in TWIN_WEIGHTS:
            delta_w[n], new_m[n], new_v[n] = _adamw(weights[n], grad_w[n], given["m_" + n], given["v_" + n])
    return (loss, grad_x, *[grad_w[n] for n in TWIN_WEIGHTS], *[delta_w[n] for n in TWIN_WEIGHTS],
            *[new_m[n] for n in TWIN_WEIGHTS], *[new_v[n] for n in TWIN_WEIGHTS])
```

```python
import functools

import jax
import jax.numpy as jnp
from jax import lax
from jax.experimental import pallas as pl
from jax.experimental.pallas import tpu as pltpu

F32 = jnp.float32
BF16 = jnp.bfloat16
MESH = pl.DeviceIdType.MESH

N_DEV = 8
D_MODEL = 1024
HEAD_DIM = 64
FOX_HEADS = 8
MEM_HEADS = 4
N_MEM = 256
POOL_WIDTH = 256
EPS = 1e-6
IN_WIDTH = 3080
IN_CHUNK = IN_WIDTH // N_DEV
F_OFF = 2048
MY_WIDTH = 3200
MY_F_OFF = 3072
PIECE = 512
PACK_ROWS = 1408
PACK_W = 512
KV_ROW0 = 1024
OUT_ROW0 = 1152
SMALL_ROWS = 152
TM = 256
TA = 256
HALO = 16
VMEM_LIMIT = 56 * 1024 * 1024

ADAM_LR = 0.001
ADAM_B1 = 0.9
ADAM_B2 = 0.999
ADAM_EPS = 1e-08
ADAM_WD = 0.01
ADAM_STEP = 10


def _dot(a, b):
    return jnp.dot(a, b, preferred_element_type=F32)


def _dot_nt(a, b):
    return lax.dot_general(a, b, (((1,), (1,)), ((), ())), preferred_element_type=F32)


def _dot_tn(a, b):
    return lax.dot_general(a, b, (((0,), (0,)), ((), ())), preferred_element_type=F32)


def _sigmoid(g):
    return 1.0 / (1.0 + jnp.exp(-g))


def _split3(v):
    hi = v.astype(BF16)
    r1 = v - hi.astype(F32)
    mid = r1.astype(BF16)
    lo = (r1 - mid.astype(F32)).astype(BF16)
    return hi, mid, lo


def _rms(v):
    return lax.rsqrt(jnp.mean(v * v, axis=-1, keepdims=True) + EPS)


def _rms_bwd(a, r, g, dy):
    da = dy * g
    return r * (da - a * jnp.mean(da * a, axis=-1, keepdims=True)), dy * a


def _params(sem, limit=VMEM_LIMIT):
    return pltpu.CompilerParams(dimension_semantics=sem, vmem_limit_bytes=limit)


def _full(shape):
    return pl.BlockSpec(shape, lambda *_: (0,) * len(shape))


def _my_place():
    x, y, c = lax.axis_index("x"), lax.axis_index("y"), lax.axis_index("c")
    return x, y, c, 4 * x + 2 * y + c


def _gather_weights(w_in, w_kv, w_out):
    def body(win_ref, wkv_ref, wout_ref, out_ref, pay_ref, send_sems, recv_sems):
        x, y, c, me = _my_place()
        sibling = (x, y, 1 - c)
        chips = [(1 - x, y), (x, 1 - y), (1 - x, 1 - y)]

        pay_ref[...] = jnp.zeros((PACK_ROWS, PACK_W), BF16)
        pay_ref[0:KV_ROW0, 0:IN_CHUNK] = win_ref[...].astype(BF16)
        pay_ref[KV_ROW0:OUT_ROW0, :] = wkv_ref[...].astype(BF16)
        pay_ref[OUT_ROW0:OUT_ROW0 + 128, :] = wout_ref[:, 0:PACK_W].astype(BF16)
        pay_ref[OUT_ROW0 + 128:PACK_ROWS, :] = wout_ref[:, PACK_W:2 * PACK_W].astype(BF16)

        def slab(px, py, pc):
            return out_ref.at[4 * px + 2 * py + pc]

        def copy(k, block, to, src=None):
            return pltpu.make_async_remote_copy(
                src_ref=slab(*block) if src is None else src,
                dst_ref=slab(*block),
                send_sem=send_sems.at[k],
                recv_sem=recv_sems.at[k],
                device_id=to,
                device_id_type=MESH,
            )

        first = [copy(0, (x, y, c), sibling, src=pay_ref)]
        first += [copy(1 + j, (x, y, c), (*chip, c), src=pay_ref) for j, chip in enumerate(chips)]
        for cp in first:
            cp.start()
        out_ref[me] = pay_ref[...]
        passed = [copy(4 + j, (*chip, c), sibling) for j, chip in enumerate(chips)]
        for j, chip in enumerate(chips):
            copy(1 + j, (*chip, c), (x, y, c)).wait_recv()
            passed[j].start()
        copy(0, sibling, (x, y, c)).wait_recv()
        for j, chip in enumerate(chips):
            copy(4 + j, (*chip, 1 - c), (x, y, c)).wait_recv()
        for cp in first + passed:
            cp.wait_send()

    return pl.pallas_call(
        body,
        name="gather_weights",
        out_shape=jax.ShapeDtypeStruct((N_DEV, PACK_ROWS, PACK_W), BF16),
        in_specs=[pl.BlockSpec(memory_space=pltpu.VMEM)] * 3,
        out_specs=pl.BlockSpec(memory_space=pltpu.VMEM),
        scratch_shapes=[
            pltpu.VMEM((PACK_ROWS, PACK_W), BF16),
            pltpu.SemaphoreType.DMA((7,)),
            pltpu.SemaphoreType.DMA((7,)),
        ],
        compiler_params=pltpu.CompilerParams(vmem_limit_bytes=VMEM_LIMIT),
    )(w_in, w_kv, w_out)


def _exchange_grads(big, small):
    def body(big_ref, small_ref, red_ref, sred_ref, land_ref, sland_ref, send_sems, recv_sems, ssend_sems, srecv_sems):
        x, y, c, me = _my_place()
        copies = []
        for k in range(1, N_DEV):
            px, py, pc = x ^ (k >> 2), y ^ ((k >> 1) & 1), c ^ (k & 1)
            peer = 4 * px + 2 * py + pc
            copies.append(pltpu.make_async_remote_copy(
                src_ref=big_ref.at[peer], dst_ref=land_ref.at[me],
                send_sem=send_sems.at[k - 1], recv_sem=recv_sems.at[k - 1],
                device_id=(px, py, pc), device_id_type=MESH))
            copies.append(pltpu.make_async_remote_copy(
                src_ref=small_ref, dst_ref=sland_ref.at[me],
                send_sem=ssend_sems.at[k - 1], recv_sem=srecv_sems.at[k - 1],
                device_id=(px, py, pc), device_id_type=MESH))
        for cp in copies:
            cp.start()
        land_ref[me] = big_ref[me]
        sland_ref[me] = small_ref[...]
        for cp in copies:
            cp.wait_recv()
        for cp in copies:
            cp.wait_send()
        acc = land_ref[0].astype(F32)
        sacc = sland_ref[0]
        for d in range(1, N_DEV):
            acc = acc + land_ref[d].astype(F32)
            sacc = sacc + sland_ref[d]
        red_ref[...] = acc
        sred_ref[...] = sacc

    return pl.pallas_call(
        body,
        name="exchange_grads",
        out_shape=(jax.ShapeDtypeStruct((PACK_ROWS, PACK_W), F32), jax.ShapeDtypeStruct((SMALL_ROWS, 128), F32)),
        in_specs=[pl.BlockSpec(memory_space=pltpu.VMEM)] * 2,
        out_specs=(pl.BlockSpec(memory_space=pltpu.VMEM),) * 2,
        scratch_shapes=[
            pltpu.VMEM((N_DEV, PACK_ROWS, PACK_W), BF16),
            pltpu.VMEM((N_DEV, SMALL_ROWS, 128), F32),
            pltpu.SemaphoreType.DMA((7,)),
            pltpu.SemaphoreType.DMA((7,)),
            pltpu.SemaphoreType.DMA((7,)),
            pltpu.SemaphoreType.DMA((7,)),
        ],
        compiler_params=pltpu.CompilerParams(vmem_limit_bytes=VMEM_LIMIT),
    )(big, small)


def _fwd_proj(x, norm_g, w_my):
    s_len = x.shape[0]

    def body(x_ref, g_ref, w_ref, proj_ref, h_ref):
        xv = x_ref[...]
        h = (xv * _rms(xv) * g_ref[...]).astype(BF16)
        h_ref[...] = h
        proj_ref[...] = _dot(h, w_ref[...])

    return pl.pallas_call(
        body,
        name="fwd_proj",
        grid=(s_len // TM,),
        in_specs=[pl.BlockSpec((TM, D_MODEL), lambda i: (i, 0)), _full((1, D_MODEL)), _full((D_MODEL, MY_WIDTH))],
        out_specs=[pl.BlockSpec((TM, MY_WIDTH), lambda i: (i, 0)), pl.BlockSpec((TM, D_MODEL), lambda i: (i, 0))],
        out_shape=[jax.ShapeDtypeStruct((s_len, MY_WIDTH), F32), jax.ShapeDtypeStruct((s_len, D_MODEL), BF16)],
        compiler_params=_params(("parallel",)),
    )(x, norm_g, w_my)


def _log_sigmoid(z):
    return jnp.minimum(z, 0.0) - jnp.log(1.0 + jnp.exp(-jnp.abs(z)))


def _fox_prep(proj, bf_pad, gq, gk):
    s_len = proj.shape[0]

    def body(q_ref, k_ref, v_ref, f_ref, bf_ref, gq_ref, gk_ref, qa_ref, ka_ref, vh_ref, carry_ref):
        @pl.when(pl.program_id(0) == 0)
        def _():
            carry_ref[...] = jnp.zeros((1, 128), F32)

        lane = lax.broadcasted_iota(jnp.int32, (TM, 128), 1)
        logf = jnp.where(lane < FOX_HEADS, _log_sigmoid(f_ref[...] + bf_ref[...]), 0.0)
        row = lax.broadcasted_iota(jnp.int32, (TM, TM), 0)
        col = lax.broadcasted_iota(jnp.int32, (TM, TM), 1)
        tri = jnp.where(col <= row, 1.0, 0.0).astype(BF16)
        hi, mid, lo = _split3(logf)
        cum = _dot(tri, hi) + _dot(tri, mid) + _dot(tri, lo) + carry_ref[...]
        carry_ref[...] = cum[TM - 1:TM, :]
        f_hi, f_mid, f_lo = [t.astype(F32) for t in _split3(cum)]
        zeros = jnp.zeros((TM, HEAD_DIM), F32)
        for h in range(FOX_HEADS):
            sl = slice(HEAD_DIM * h, HEAD_DIM * (h + 1))
            qh = q_ref[:, sl]
            kh = k_ref[:, sl]
            qn = qh * _rms(qh) * gq_ref[...] * 0.125
            kn = kh * _rms(kh) * gk_ref[...]
            a, b, c = f_hi[:, h:h + 1], f_mid[:, h:h + 1], f_lo[:, h:h + 1]
            qa = jnp.where(lane < 64, jnp.concatenate([qn, zeros], axis=1),
                           jnp.where(lane == 64, a, jnp.where(lane == 65, b, jnp.where(lane == 66, c,
                                                                                         jnp.where(lane < 70, 1.0, 0.0)))))
            ka = jnp.where(lane < 64, jnp.concatenate([kn, zeros], axis=1),
                           jnp.where(lane < 67, 1.0, jnp.where(lane == 67, -a, jnp.where(lane == 68, -b,
                                                                                          jnp.where(lane == 69, -c, 0.0)))))
            qa_ref[h] = qa.astype(BF16)
            ka_ref[h] = ka.astype(BF16)
            vh_ref[h] = v_ref[:, sl].astype(BF16)

    col_blk = lambda j: pl.BlockSpec((TM, PIECE), lambda i: (i, j))
    head_blk = lambda w: pl.BlockSpec((FOX_HEADS, TM, w), lambda i: (0, i, 0))
    return pl.pallas_call(
        body,
        name="fox_prep",
        grid=(s_len // TM,),
        in_specs=[col_blk(1), col_blk(2), col_blk(3), pl.BlockSpec((TM, 128), lambda i: (i, MY_F_OFF // 128)),
                  _full((1, 128)), _full((1, HEAD_DIM)), _full((1, HEAD_DIM))],
        out_specs=[head_blk(128), head_blk(128), head_blk(HEAD_DIM)],
        out_shape=[jax.ShapeDtypeStruct((FOX_HEADS, s_len, 128), BF16), jax.ShapeDtypeStruct((FOX_HEADS, s_len, 128), BF16),
                   jax.ShapeDtypeStruct((FOX_HEADS, s_len, HEAD_DIM), BF16)],
        scratch_shapes=[pltpu.VMEM((1, 128), F32)],
        compiler_params=_params(("arbitrary",)),
    )(proj, proj, proj, proj, bf_pad, gq, gk)


def _mem_prep(mem, g, w_kv, gk):
    def body(mem_ref, g_ref, w_ref, gk_ref, km_ref, vm_ref):
        m = mem_ref[...]
        kv = _dot((m * _rms(m) * g_ref[...]).astype(BF16), w_ref[...])
        for h in range(MEM_HEADS):
            sl = slice(HEAD_DIM * h, HEAD_DIM * (h + 1))
            kh = kv[:, sl]
            km_ref[:, sl] = (kh * _rms(kh) * gk_ref[...]).astype(BF16)
        vm_ref[...] = kv[:, 256:512].astype(BF16)

    return pl.pallas_call(
        body,
        name="mem_prep",
        out_shape=(jax.ShapeDtypeStruct((N_MEM, 256), BF16),) * 2,
        in_specs=[pl.BlockSpec(memory_space=pltpu.VMEM)] * 4,
        out_specs=(pl.BlockSpec(memory_space=pltpu.VMEM),) * 2,
        compiler_params=pltpu.CompilerParams(vmem_limit_bytes=VMEM_LIMIT),
    )(mem, g, w_kv, gk)


def _causal_mask():
    row = lax.broadcasted_iota(jnp.int32, (TA, TA), 0)
    col = lax.broadcasted_iota(jnp.int32, (TA, TA), 1)
    return col <= row


def _fox_fwd(q_aug, k_aug, v_h):
    s_len = q_aug.shape[1]

    def body(q_ref, k_ref, v_ref, o_ref):
        qi = pl.program_id(1)
        q = q_ref[0]

        def step(j, carry, masked):
            m, l, acc = carry
            rows = pl.ds(pl.multiple_of(j * TA, TA), TA)
            s = _dot_nt(q, k_ref[0, rows, :])
            if masked:
                s = jnp.where(_causal_mask(), s, -1e30)
            m_new = jnp.maximum(m, jnp.max(s, axis=-1, keepdims=True))
            alpha = jnp.exp(m - m_new)
            p = jnp.exp(s - m_new)
            l = alpha * l + jnp.sum(p, axis=-1, keepdims=True)
            acc = alpha * acc + _dot(p.astype(BF16), v_ref[0, rows, :])
            return m_new, l, acc

        init = (jnp.full((TA, 1), -1e30, F32), jnp.zeros((TA, 1), F32), jnp.zeros((TA, HEAD_DIM), F32))
        carry = lax.fori_loop(0, qi, lambda j, cr: step(j, cr, False), init)
        m, l, acc = step(qi, carry, True)
        lse = m + jnp.log(l)
        o_ref[0] = jnp.concatenate([acc / l, jnp.broadcast_to(lse, (TA, HEAD_DIM))], axis=1)

    return pl.pallas_call(
        body,
        name="fox_fwd",
        grid=(FOX_HEADS, s_len // TA),
        in_specs=[pl.BlockSpec((1, TA, 128), lambda h, i: (h, i, 0)), pl.BlockSpec((1, s_len, 128), lambda h, i: (h, 0, 0)),
                  pl.BlockSpec((1, s_len, HEAD_DIM), lambda h, i: (h, 0, 0))],
        out_specs=pl.BlockSpec((1, TA, 128), lambda h, i: (h, i, 0)),
        out_shape=jax.ShapeDtypeStruct((FOX_HEADS, s_len, 128), F32),
        compiler_params=_params(("parallel", "parallel")),
    )(q_aug, k_aug, v_h)


def _lane_group(lane, a, b, c, d):
    return jnp.where(lane < 64, a, jnp.where(lane < 128, b, jnp.where(lane < 192, c, d)))


def _pool_count(row0):
    lane = lax.broadcasted_iota(jnp.int32, (TM, POOL_WIDTH), 1)
    t1 = row0 + lax.broadcasted_iota(jnp.int32, (TM, POOL_WIDTH), 0) + 1
    return jnp.minimum(t1, _lane_group(lane, 2, 4, 8, 16)).astype(F32), lane


def _pool_delta(u, halo, cnt, lane):
    xe = jnp.concatenate([halo, u], axis=0)
    s2 = xe + pltpu.roll(xe, 1, 0)
    s4 = s2 + pltpu.roll(s2, 2, 0)
    s8 = s4 + pltpu.roll(s4, 4, 0)
    s16 = s8 + pltpu.roll(s8, 8, 0)
    win = _lane_group(lane, s2[HALO:], s4[HALO:], s8[HALO:], s16[HALO:])
    return win / cnt - u


def _pool_delta_bwd(dd, dd_next, cnt, cnt_next, lane):
    ee = jnp.concatenate([dd / cnt, dd_next / cnt_next], axis=0)
    n = TM + HALO
    r2 = ee + pltpu.roll(ee, n - 1, 0)
    r4 = r2 + pltpu.roll(r2, n - 2, 0)
    r8 = r4 + pltpu.roll(r4, n - 4, 0)
    r16 = r8 + pltpu.roll(r8, n - 8, 0)
    return _lane_group(lane, r2[:TM], r4[:TM], r8[:TM], r16[:TM]) - dd


def _mem_attn(qm, gq, km_ref, vm_ref):
    heads = []
    for h in range(MEM_HEADS):
        sl = slice(HEAD_DIM * h, HEAD_DIM * (h + 1))
        qh = qm[:, sl]
        r = _rms(qh)
        a = qh * r
        q16 = (a * gq * 0.125).astype(BF16)
        s = _dot_nt(q16, km_ref[:, sl])
        e = jnp.exp(s - jnp.max(s, axis=-1, keepdims=True))
        p = e / jnp.sum(e, axis=-1, keepdims=True)
        heads.append((a, r, q16, p, _dot(p.astype(BF16), vm_ref[:, sl])))
    return heads


def _row_specs(s_len):
    n_halo = s_len // HALO
    piece = lambda j: pl.BlockSpec((TM, PIECE), lambda i: (i, j))
    before = lambda j: pl.BlockSpec((HALO, 256), lambda i: (jnp.maximum(i * (TM // HALO) - 1, 0), j))
    after = lambda j: pl.BlockSpec((HALO, 256), lambda i: (jnp.minimum((i + 1) * (TM // HALO), n_halo - 1), j))
    return piece, before, after


def _mix_fwd(proj, olse, km, vm, wp_bd, pool_scale, gq_m):
    s_len = proj.shape[0]

    def body(ua_ref, halo_ref, gb_ref, qm_ref, ol_ref, km_ref, vm_ref, wp_ref, sc_ref, gq_ref, out_ref):
        i = pl.program_id(0)
        u = ua_ref[:, 0:256]
        g_a = ua_ref[:, 256:512]
        halo = jnp.where(i > 0, halo_ref[...], 0.0)
        cnt, lane = _pool_count(i * TM)
        d = _pool_delta(u, halo, cnt, lane).astype(BF16)
        y_a = _dot(d, wp_ref[...]) * sc_ref[...]
        out_ref[:, 0:256] = (y_a * (g_a * _sigmoid(g_a))).astype(BF16)
        for h in range(FOX_HEADS):
            sl = slice(HEAD_DIM * h, HEAD_DIM * (h + 1))
            g = gb_ref[:, sl]
            out_ref[:, 256 + HEAD_DIM * h:256 + HEAD_DIM * (h + 1)] = (ol_ref[h][:, 0:HEAD_DIM] * (g * _sigmoid(g))).astype(BF16)
        heads = _mem_attn(qm_ref[:, 0:256], gq_ref[...], km_ref, vm_ref)
        for h in range(MEM_HEADS):
            g = qm_ref[:, 256 + HEAD_DIM * h:256 + HEAD_DIM * (h + 1)]
            out_ref[:, 768 + HEAD_DIM * h:768 + HEAD_DIM * (h + 1)] = (heads[h][4] * (g * _sigmoid(g))).astype(BF16)

    piece, before, _ = _row_specs(s_len)
    return pl.pallas_call(
        body,
        name="mix_fwd",
        grid=(s_len // TM,),
        in_specs=[piece(0), before(0), piece(4), piece(5), pl.BlockSpec((FOX_HEADS, TM, 128), lambda i: (0, i, 0)),
                  _full((N_MEM, 256)), _full((N_MEM, 256)), _full((256, 256)), _full((1, 256)), _full((1, HEAD_DIM))],
        out_specs=pl.BlockSpec((TM, D_MODEL), lambda i: (i, 0)),
        out_shape=jax.ShapeDtypeStruct((s_len, D_MODEL), BF16),
        compiler_params=_params(("parallel",)),
    )(proj, proj, proj, proj, olse, km, vm, wp_bd, pool_scale, gq_m)


def _out_loss(mixed, x, target, w_out):
    s_len = x.shape[0]

    def body(mix_ref, x_ref, t_ref, w_ref, dout_ref, dmix_ref, dw_ref, loss_ref):
        @pl.when(pl.program_id(0) == 0)
        def _():
            dw_ref[...] = jnp.zeros((D_MODEL, D_MODEL), F32)
            loss_ref[...] = jnp.zeros((8, 128), F32)

        mixed16 = mix_ref[...]
        err = x_ref[...] + _dot(mixed16, w_ref[...]) - t_ref[...]
        loss_ref[...] += 0.5 * jnp.sum(jnp.mean(err * err, axis=-1, keepdims=True))
        d_out = err / float(D_MODEL)
        dout_ref[...] = d_out
        d16 = d_out.astype(BF16)
        dmix_ref[...] = _dot_nt(d16, w_ref[...])
        dw_ref[...] += _dot_tn(mixed16, d16)

    row = pl.BlockSpec((TM, D_MODEL), lambda i: (i, 0))
    return pl.pallas_call(
        body,
        name="out_loss",
        grid=(s_len // TM,),
        in_specs=[row, row, row, _full((D_MODEL, D_MODEL))],
        out_specs=[row, row, _full((D_MODEL, D_MODEL)), _full((8, 128))],
        out_shape=[jax.ShapeDtypeStruct((s_len, D_MODEL), F32), jax.ShapeDtypeStruct((s_len, D_MODEL), F32),
                   jax.ShapeDtypeStruct((D_MODEL, D_MODEL), F32), jax.ShapeDtypeStruct((8, 128), F32)],
        compiler_params=_params(("arbitrary",)),
    )(mixed, x, target, w_out)


def _silu_pair(g):
    s = _sigmoid(g)
    return g * s, s * (1.0 + g * (1.0 - s))


def _mix_bwd(proj, olse, d_mixed, km, vm, wp_bd, pool_scale, gq_m):
    s_len = proj.shape[0]
    n_tiles = s_len // TM

    def body(ua_ref, halo_ref, ga_next_ref, gb_ref, qm_ref, ol_ref, dm_ref, dm_next_ref, km_ref, vm_ref, wp_ref, sc_ref, gq_ref,
             dua_ref, dgb_ref, dqm_ref, do_ref, dwp_ref, dsc_ref, dkm_ref, dvm_ref, dgq_ref):
        i = pl.program_id(0)

        @pl.when(i == 0)
        def _():
            dwp_ref[...] = jnp.zeros((256, 256), F32)
            dsc_ref[...] = jnp.zeros((1, 256), F32)
            dkm_ref[...] = jnp.zeros((N_MEM, 256), F32)
            dvm_ref[...] = jnp.zeros((N_MEM, 256), F32)
            dgq_ref[...] = jnp.zeros((1, HEAD_DIM), F32)

        u = ua_ref[:, 0:256]
        g_a = ua_ref[:, 256:512]
        halo = jnp.where(i > 0, halo_ref[...], 0.0)
        cnt, lane = _pool_count(i * TM)
        d16 = _pool_delta(u, halo, cnt, lane).astype(BF16)
        t = _dot(d16, wp_ref[...])
        y_a = t * sc_ref[...]
        silu_a, dsilu_a = _silu_pair(g_a)
        dm_a = dm_ref[:, 0:256]
        dy_a = dm_a * silu_a
        dsc_ref[...] += jnp.sum(dy_a * t, axis=0, keepdims=True)
        dt16 = (dy_a * sc_ref[...]).astype(BF16)
        dwp_ref[...] += _dot_tn(d16, dt16)
        dd = _dot_nt(dt16, wp_ref[...])
        g_next = ga_next_ref[...]
        dt_next = (dm_next_ref[...] * (g_next * _sigmoid(g_next)) * sc_ref[...]).astype(BF16)
        dd_next = jnp.where(i < n_tiles - 1, _dot_nt(dt_next, wp_ref[...]), 0.0)
        cnt_next = _pool_count((i + 1) * TM)[0][0:HALO]
        dua_ref[:, 0:256] = _pool_delta_bwd(dd, dd_next, cnt, cnt_next, lane).astype(BF16)
        dua_ref[:, 256:512] = (dm_a * y_a * dsilu_a).astype(BF16)

        for h in range(FOX_HEADS):
            sl = slice(HEAD_DIM * h, HEAD_DIM * (h + 1))
            silu_b, dsilu_b = _silu_pair(gb_ref[:, sl])
            dm_b = dm_ref[:, 256 + HEAD_DIM * h:256 + HEAD_DIM * (h + 1)]
            do_ref[h] = (dm_b * silu_b).astype(BF16)
            dgb_ref[:, sl] = (dm_b * ol_ref[h][:, 0:HEAD_DIM] * dsilu_b).astype(BF16)

        heads = _mem_attn(qm_ref[:, 0:256], gq_ref[...], km_ref, vm_ref)
        dgq = jnp.zeros((TM, HEAD_DIM), F32)
        for h in range(MEM_HEADS):
            sl = slice(HEAD_DIM * h, HEAD_DIM * (h + 1))
            a, r, q16, p, y_m = heads[h]
            silu_m, dsilu_m = _silu_pair(qm_ref[:, 256 + HEAD_DIM * h:256 + HEAD_DIM * (h + 1)])
            dm_m = dm_ref[:, 768 + HEAD_DIM * h:768 + HEAD_DIM * (h + 1)]
            dqm_ref[:, 256 + HEAD_DIM * h:256 + HEAD_DIM * (h + 1)] = (dm_m * y_m * dsilu_m).astype(BF16)
            dy16 = (dm_m * silu_m).astype(BF16)
            dp = _dot_nt(dy16, vm_ref[:, sl])
            dvm_ref[:, sl] += _dot_tn(p.astype(BF16), dy16)
            ds16 = (p * (dp - jnp.sum(dp * p, axis=-1, keepdims=True))).astype(BF16)
            dkm_ref[:, sl] += _dot_tn(ds16, q16)
            dq, dg_rows = _rms_bwd(a, r, gq_ref[...], _dot(ds16, km_ref[:, sl]) * 0.125)
            dqm_ref[:, sl] = dq.astype(BF16)
            dgq = dgq + dg_rows
        dgq_ref[...] += jnp.sum(dgq, axis=0, keepdims=True)

    piece, before, after = _row_specs(s_len)
    n_halo = s_len // HALO
    row = pl.BlockSpec((TM, D_MODEL), lambda i: (i, 0))
    dm_after = pl.BlockSpec((HALO, 256), lambda i: (jnp.minimum((i + 1) * (TM // HALO), n_halo - 1), 0))
    out_piece = pl.BlockSpec((TM, PIECE), lambda i: (i, 0))
    return pl.pallas_call(
        body,
        name="mix_bwd",
        grid=(n_tiles,),
        in_specs=[piece(0), before(0), after(1), piece(4), piece(5), pl.BlockSpec((FOX_HEADS, TM, 128), lambda i: (0, i, 0)),
                  row, dm_after, _full((N_MEM, 256)), _full((N_MEM, 256)), _full((256, 256)), _full((1, 256)), _full((1, HEAD_DIM))],
        out_specs=[out_piece, out_piece, out_piece, pl.BlockSpec((FOX_HEADS, TM, HEAD_DIM), lambda i: (0, i, 0)),
                   _full((256, 256)), _full((1, 256)), _full((N_MEM, 256)), _full((N_MEM, 256)), _full((1, HEAD_DIM))],
        out_shape=[jax.ShapeDtypeStruct((s_len, PIECE), BF16)] * 3 + [
            jax.ShapeDtypeStruct((FOX_HEADS, s_len, HEAD_DIM), BF16),
            jax.ShapeDtypeStruct((256, 256), F32), jax.ShapeDtypeStruct((1, 256), F32),
            jax.ShapeDtypeStruct((N_MEM, 256), F32), jax.ShapeDtypeStruct((N_MEM, 256), F32),
            jax.ShapeDtypeStruct((1, HEAD_DIM), F32)],
        compiler_params=_params(("arbitrary",)),
    )(proj, proj, proj, proj, proj, olse, d_mixed, d_mixed, km, vm, wp_bd, pool_scale, gq_m)


def _fox_bwd(q_aug, k_aug, v_h, d_o, olse):
    s_len = q_aug.shape[1]
    n_tiles = s_len // TA

    def body(q_ref, k_ref, v_ref, do_ref, ol_ref, dq_ref, dk_ref, dv_ref):
        dq_ref[...] = jnp.zeros((1, s_len, 128), F32)

        def kv_tile(j, _):
            cols = pl.ds(pl.multiple_of(j * TA, TA), TA)
            k = k_ref[0, cols, :]
            v = v_ref[0, cols, :]

            def pair(i, acc, masked):
                dk, dv = acc
                rows = pl.ds(pl.multiple_of(i * TA, TA), TA)
                q = q_ref[0, rows, :]
                d_o16 = do_ref[0, rows, :]
                ol = ol_ref[0, rows, :]
                delta = jnp.sum(d_o16.astype(F32) * ol[:, 0:HEAD_DIM], axis=-1, keepdims=True)
                s = _dot_nt(q, k)
                if masked:
                    s = jnp.where(_causal_mask(), s, -1e30)
                p = jnp.exp(s - ol[:, HEAD_DIM:HEAD_DIM + 1])
                ds16 = (p * (_dot_nt(d_o16, v) - delta)).astype(BF16)
                dq_ref[0, rows, :] += _dot(ds16, k)
                return dk + _dot_tn(ds16, q), dv + _dot_tn(p.astype(BF16), d_o16)

            acc = pair(j, (jnp.zeros((TA, 128), F32), jnp.zeros((TA, HEAD_DIM), F32)), True)
            dk, dv = lax.fori_loop(j + 1, n_tiles, lambda i, a: pair(i, a, False), acc)
            dk_ref[0, cols, :] = dk
            dv_ref[0, cols, :] = dv
            return 0

        lax.fori_loop(0, n_tiles, kv_tile, 0)

    head = lambda w: pl.BlockSpec((1, s_len, w), lambda h: (h, 0, 0))
    return pl.pallas_call(
        body,
        name="fox_bwd",
        grid=(FOX_HEADS,),
        in_specs=[head(128), head(128), head(HEAD_DIM), head(HEAD_DIM), head(128)],
        out_specs=[head(128), head(128), head(HEAD_DIM)],
        out_shape=[jax.ShapeDtypeStruct((FOX_HEADS, s_len, 128), F32), jax.ShapeDtypeStruct((FOX_HEADS, s_len, 128), F32),
                   jax.ShapeDtypeStruct((FOX_HEADS, s_len, HEAD_DIM), F32)],
        compiler_params=_params(("parallel",)),
    )(q_aug, k_aug, v_h, d_o, olse)


def _fox_post(proj, bf_pad, gq, gk, dq_aug, dk_aug, dv_h):
    s_len = proj.shape[0]
    n_tiles = s_len // TM

    def body(q_ref, k_ref, f_ref, bf_ref, gq_ref, gk_ref, dqa_ref, dka_ref, dvh_ref,
             dq_ref, dk_ref, dv_ref, df_ref, dgq_ref, dgk_ref, dbf_ref, carry_ref):
        @pl.when(pl.program_id(0) == 0)
        def _():
            carry_ref[...] = jnp.zeros((1, 128), F32)
            dgq_ref[...] = jnp.zeros((1, HEAD_DIM), F32)
            dgk_ref[...] = jnp.zeros((1, HEAD_DIM), F32)
            dbf_ref[...] = jnp.zeros((1, 128), F32)

        lane = lax.broadcasted_iota(jnp.int32, (TM, 128), 1)
        d_cum = jnp.zeros((TM, 128), F32)
        dgq = jnp.zeros((TM, HEAD_DIM), F32)
        dgk = jnp.zeros((TM, HEAD_DIM), F32)
        for h in range(FOX_HEADS):
            sl = slice(HEAD_DIM * h, HEAD_DIM * (h + 1))
            dqa = dqa_ref[h]
            dka = dka_ref[h]
            qh = q_ref[:, sl]
            kh = k_ref[:, sl]
            rq = _rms(qh)
            rk = _rms(kh)
            dq, dgq_rows = _rms_bwd(qh * rq, rq, gq_ref[...], dqa[:, 0:HEAD_DIM] * 0.125)
            dk, dgk_rows = _rms_bwd(kh * rk, rk, gk_ref[...], dka[:, 0:HEAD_DIM])
            dq_ref[:, sl] = dq.astype(BF16)
            dk_ref[:, sl] = dk.astype(BF16)
            dv_ref[:, sl] = dvh_ref[h].astype(BF16)
            dgq = dgq + dgq_rows
            dgk = dgk + dgk_rows
            d_cum = jnp.where(lane == h, dqa[:, 64:65] - dka[:, 67:68], d_cum)
        dgq_ref[...] += jnp.sum(dgq, axis=0, keepdims=True)
        dgk_ref[...] += jnp.sum(dgk, axis=0, keepdims=True)

        row = lax.broadcasted_iota(jnp.int32, (TM, TM), 0)
        col = lax.broadcasted_iota(jnp.int32, (TM, TM), 1)
        tri = jnp.where(col >= row, 1.0, 0.0).astype(BF16)
        hi, mid, lo = _split3(d_cum)
        d_logf = _dot(tri, hi) + _dot(tri, mid) + _dot(tri, lo) + carry_ref[...]
        carry_ref[...] = d_logf[0:1, :]
        z = f_ref[...] + bf_ref[...]
        d_f = jnp.where(lane < FOX_HEADS, d_logf * _sigmoid(-z), 0.0)
        df_ref[...] = d_f.astype(BF16)
        dbf_ref[...] += jnp.sum(d_f, axis=0, keepdims=True)

    rev = lambda i: n_tiles - 1 - i
    col_blk = lambda j: pl.BlockSpec((TM, PIECE), lambda i: (rev(i), j))
    head_blk = lambda w: pl.BlockSpec((FOX_HEADS, TM, w), lambda i: (0, rev(i), 0))
    out_piece = pl.BlockSpec((TM, PIECE), lambda i: (rev(i), 0))
    return pl.pallas_call(
        body,
        name="fox_post",
        grid=(n_tiles,),
        in_specs=[col_blk(1), col_blk(2), pl.BlockSpec((TM, 128), lambda i: (rev(i), MY_F_OFF // 128)),
                  _full((1, 128)), _full((1, HEAD_DIM)), _full((1, HEAD_DIM)), head_blk(128), head_blk(128), head_blk(HEAD_DIM)],
        out_specs=[out_piece, out_piece, out_piece, pl.BlockSpec((TM, 128), lambda i: (rev(i), 0)),
                   _full((1, HEAD_DIM)), _full((1, HEAD_DIM)), _full((1, 128))],
        out_shape=[jax.ShapeDtypeStruct((s_len, PIECE), BF16)] * 3 + [
            jax.ShapeDtypeStruct((s_len, 128), BF16), jax.ShapeDtypeStruct((1, HEAD_DIM), F32),
            jax.ShapeDtypeStruct((1, HEAD_DIM), F32), jax.ShapeDtypeStruct((1, 128), F32)],
        scratch_shapes=[pltpu.VMEM((1, 128), F32)],
        compiler_params=_params(("arbitrary",)),
    )(proj, proj, proj, bf_pad, gq, gk, dq_aug, dk_aug, dv_h)


def _mem_bwd(mem, g, w_kv, gk, dkm, dvm):
    def body(mem_ref, g_ref, w_ref, gk_ref, dkm_ref, dvm_ref, dw_ref, dg_ref, dgk_ref, dkv_ref):
        m = mem_ref[...]
        r = _rms(m)
        a = m * r
        mn16 = (a * g_ref[...]).astype(BF16)
        kv = _dot(mn16, w_ref[...])
        dgk = jnp.zeros((N_MEM, HEAD_DIM), F32)
        for h in range(MEM_HEADS):
            sl = slice(HEAD_DIM * h, HEAD_DIM * (h + 1))
            kh = kv[:, sl]
            rk = _rms(kh)
            dk, dg_rows = _rms_bwd(kh * rk, rk, gk_ref[...], dkm_ref[:, sl])
            dkv_ref[:, sl] = dk.astype(BF16)
            dgk = dgk + dg_rows
        dkv_ref[:, 256:512] = dvm_ref[...].astype(BF16)
        dgk_ref[...] = jnp.sum(dgk, axis=0, keepdims=True)
        dkv16 = dkv_ref[...]
        dw_ref[...] = _dot_tn(mn16, dkv16)
        dg_ref[...] = jnp.sum(_dot_nt(dkv16, w_ref[...]) * a, axis=0, keepdims=True)

    return pl.pallas_call(
        body,
        name="mem_bwd",
        out_shape=(jax.ShapeDtypeStruct((D_MODEL, 512), F32), jax.ShapeDtypeStruct((1, D_MODEL), F32),
                   jax.ShapeDtypeStruct((1, HEAD_DIM), F32)),
        in_specs=[pl.BlockSpec(memory_space=pltpu.VMEM)] * 6,
        out_specs=(pl.BlockSpec(memory_space=pltpu.VMEM),) * 3,
        scratch_shapes=[pltpu.VMEM((N_MEM, 512), BF16)],
        compiler_params=pltpu.CompilerParams(vmem_limit_bytes=VMEM_LIMIT),
    )(mem, g, w_kv, gk, dkm, dvm)


def _grad_x(pieces, d_f, w_my, x, norm_g, d_out):
    s_len = x.shape[0]

    def body(p0, p1, p2, p3, p4, p5, df_ref, w_ref, x_ref, g_ref, dout_ref, gx_ref, dg_ref):
        @pl.when(pl.program_id(0) == 0)
        def _():
            dg_ref[...] = jnp.zeros((1, D_MODEL), F32)

        dh = _dot_nt(df_ref[...], w_ref[:, MY_F_OFF:MY_WIDTH])
        for j, p in enumerate((p0, p1, p2, p3, p4, p5)):
            dh = dh + _dot_nt(p[...], w_ref[:, PIECE * j:PIECE * (j + 1)])
        xv = x_ref[...]
        r = _rms(xv)
        dx, dg_rows = _rms_bwd(xv * r, r, g_ref[...], dh)
        gx_ref[...] = dout_ref[...] + dx
        dg_ref[...] += jnp.sum(dg_rows, axis=0, keepdims=True)

    piece = pl.BlockSpec((TM, PIECE), lambda i: (i, 0))
    row = pl.BlockSpec((TM, D_MODEL), lambda i: (i, 0))
    return pl.pallas_call(
        body,
        name="grad_x",
        grid=(s_len // TM,),
        in_specs=[piece] * 6 + [pl.BlockSpec((TM, 128), lambda i: (i, 0)), _full((D_MODEL, MY_WIDTH)), row,
                                _full((1, D_MODEL)), row],
        out_specs=[row, _full((1, D_MODEL))],
        out_shape=[jax.ShapeDtypeStruct((s_len, D_MODEL), F32), jax.ShapeDtypeStruct((1, D_MODEL), F32)],
        compiler_params=_params(("arbitrary",)),
    )(*pieces, d_f, w_my, x, norm_g, d_out)


def _grad_w_in(h16, pieces, d_f):
    s_len = h16.shape[0]
    tk = 512

    def body(h_ref, p0, p1, p2, p3, p4, p5, df_ref, dw_ref):
        @pl.when(pl.program_id(0) == 0)
        def _():
            dw_ref[...] = jnp.zeros((D_MODEL, MY_WIDTH), F32)

        h = h_ref[...]
        for j, p in enumerate((p0, p1, p2, p3, p4, p5)):
            dw_ref[:, PIECE * j:PIECE * (j + 1)] += _dot_tn(h, p[...])
        dw_ref[:, MY_F_OFF:MY_WIDTH] += _dot_tn(h, df_ref[...])

    piece = pl.BlockSpec((tk, PIECE), lambda i: (i, 0))
    return pl.pallas_call(
        body,
        name="grad_w_in",
        grid=(s_len // tk,),
        in_specs=[pl.BlockSpec((tk, D_MODEL), lambda i: (i, 0))] + [piece] * 6 + [pl.BlockSpec((tk, 128), lambda i: (i, 0))],
        out_specs=_full((D_MODEL, MY_WIDTH)),
        out_shape=jax.ShapeDtypeStruct((D_MODEL, MY_WIDTH), F32),
        compiler_params=_params(("arbitrary",)),
    )(h16, *pieces, d_f)


def _adamw(w, g, m, v):
    m = ADAM_B1 * m + (1.0 - ADAM_B1) * g
    v = ADAM_B2 * v + (1.0 - ADAM_B2) * (g * g)
    m_hat = m / (1.0 - ADAM_B1 ** ADAM_STEP)
    v_hat = v / (1.0 - ADAM_B2 ** ADAM_STEP)
    return -ADAM_LR * (m_hat / (jnp.sqrt(v_hat) + ADAM_EPS) + ADAM_WD * w), m, v


def _update(red, sred, big, small):
    def body(red_ref, sred_ref, wi, mi, vi, wk, mk, vk, wo, mo, vo, ws, ms, vs, *outs):
        groups = [
            (lambda: red_ref[0:KV_ROW0, 0:IN_CHUNK], wi, mi, vi, outs[0:4], None),
            (lambda: red_ref[KV_ROW0:OUT_ROW0, :], wk, mk, vk, outs[4:8], None),
            (lambda: red_ref[OUT_ROW0:OUT_ROW0 + 128, :], wo, mo, vo, outs[8:12], slice(0, PACK_W)),
            (lambda: red_ref[OUT_ROW0 + 128:PACK_ROWS, :], wo, mo, vo, outs[8:12], slice(PACK_W, 2 * PACK_W)),
            (lambda: sred_ref[...], ws, ms, vs, outs[12:16], None),
        ]
        for grad, w_ref, m_ref, v_ref, (g_out, d_out, m_out, v_out), cols in groups:
            g = grad()
            if cols is None:
                delta, m_new, v_new = _adamw(w_ref[...], g, m_ref[...], v_ref[...])
                g_out[...], d_out[...], m_out[...], v_out[...] = g, delta, m_new, v_new
            else:
                delta, m_new, v_new = _adamw(w_ref[:, cols], g, m_ref[:, cols], v_ref[:, cols])
                g_out[:, cols], d_out[:, cols], m_out[:, cols], v_out[:, cols] = g, delta, m_new, v_new

    shapes = [big[0][0].shape] * 4 + [big[1][0].shape] * 4 + [big[2][0].shape] * 4 + [small[0].shape] * 4
    flat = [red, sred, *big[0], *big[1], *big[2], *small]
    return pl.pallas_call(
        body,
        name="adamw_update",
        out_shape=tuple(jax.ShapeDtypeStruct(s, F32) for s in shapes),
        in_specs=[pl.BlockSpec(memory_space=pltpu.VMEM)] * len(flat),
        out_specs=(pl.BlockSpec(memory_space=pltpu.VMEM),) * len(shapes),
        compiler_params=pltpu.CompilerParams(vmem_limit_bytes=VMEM_LIMIT),
    )(*flat)


SMALL_LAYOUT = (("norm_g", 8), ("mem_norm_g", 8), ("w_pool", 128), ("pool_scale", 2), ("b_f", 1),
                ("fox_q_g", 1), ("fox_k_g", 1), ("mem_q_g", 1), ("mem_k_g", 1))


def _pack_small(parts):
    rows = []
    for name, n_rows in SMALL_LAYOUT:
        flat = parts[name].reshape(-1)
        rows.append(jnp.pad(flat, (0, n_rows * 128 - flat.shape[0])).reshape(n_rows, 128))
    used = sum(n for _, n in SMALL_LAYOUT)
    rows.append(jnp.zeros((SMALL_ROWS - used, 128), F32))
    return jnp.concatenate(rows, axis=0)


def _unpack_small(packed, like):
    out, r0 = {}, 0
    for name, n_rows in SMALL_LAYOUT:
        size = like[name].size
        out[name] = packed[r0:r0 + n_rows].reshape(-1)[:size].reshape(like[name].shape)
        r0 += n_rows
    return out


def _to_my_columns(w):
    pad = jnp.zeros(w.shape[:-1] + (MY_WIDTH - IN_WIDTH,), w.dtype)
    return jnp.concatenate([w[..., :F_OFF], w[..., F_OFF + FOX_HEADS:], w[..., F_OFF:F_OFF + FOX_HEADS], pad], axis=-1)


def _from_my_columns(w):
    return jnp.concatenate([w[..., :F_OFF], w[..., MY_F_OFF:MY_F_OFF + FOX_HEADS], w[..., F_OFF:MY_F_OFF]], axis=-1)


def kernel(x, mem, norm_g, w_in, b_f, w_pool, pool_scale, fox_q_g, fox_k_g, mem_norm_g, w_mem_kv, mem_q_g, mem_k_g, w_out, loss_target, m_norm_g, m_w_in, m_b_f, m_w_pool, m_pool_scale, m_fox_q_g, m_fox_k_g, m_mem_norm_g, m_w_mem_kv, m_mem_q_g, m_mem_k_g, m_w_out, v_norm_g, v_w_in, v_b_f, v_w_pool, v_pool_scale, v_fox_q_g, v_fox_k_g, v_mem_norm_g, v_w_mem_kv, v_mem_q_g, v_mem_k_g, v_w_out):
    small_w = dict(norm_g=norm_g, mem_norm_g=mem_norm_g, w_pool=w_pool, pool_scale=pool_scale, b_f=b_f,
                   fox_q_g=fox_q_g, fox_k_g=fox_k_g, mem_q_g=mem_q_g, mem_k_g=mem_k_g)
    small_m = dict(norm_g=m_norm_g, mem_norm_g=m_mem_norm_g, w_pool=m_w_pool, pool_scale=m_pool_scale, b_f=m_b_f,
                   fox_q_g=m_fox_q_g, fox_k_g=m_fox_k_g, mem_q_g=m_mem_q_g, mem_k_g=m_mem_k_g)
    small_v = dict(norm_g=v_norm_g, mem_norm_g=v_mem_norm_g, w_pool=v_w_pool, pool_scale=v_pool_scale, b_f=v_b_f,
                   fox_q_g=v_fox_q_g, fox_k_g=v_fox_k_g, mem_q_g=v_mem_q_g, mem_k_g=v_mem_k_g)
    x2, mem2, target2 = x[0], mem[0], loss_target[0]

    packed = _gather_weights(w_in[0], w_mem_kv[0], w_out[0])
    w_in_all = packed[:, :KV_ROW0, :IN_CHUNK].transpose(1, 0, 2).reshape(D_MODEL, IN_WIDTH)
    w_my = _to_my_columns(w_in_all)
    w_kv_all = packed[:, KV_ROW0:OUT_ROW0, :].reshape(D_MODEL, 512)
    w_out_all = jnp.concatenate([packed[:, OUT_ROW0:OUT_ROW0 + 128, :], packed[:, OUT_ROW0 + 128:, :]], axis=2).reshape(D_MODEL, D_MODEL)
    wp_bd = jnp.zeros((POOL_WIDTH, POOL_WIDTH), F32)
    for g in range(4):
        wp_bd = wp_bd.at[64 * g:64 * (g + 1), 64 * g:64 * (g + 1)].set(w_pool[0, g])
    wp_bd = wp_bd.astype(BF16)
    bf_pad = jnp.pad(b_f, ((0, 0), (0, 128 - FOX_HEADS)))

    proj, h16 = _fwd_proj(x2, norm_g, w_my)
    q_aug, k_aug, v_h = _fox_prep(proj, bf_pad, fox_q_g, fox_k_g)
    km, vm = _mem_prep(mem2, mem_norm_g, w_kv_all, mem_k_g)
    olse = _fox_fwd(q_aug, k_aug, v_h)
    mixed = _mix_fwd(proj, olse, km, vm, wp_bd, pool_scale, mem_q_g)
    d_out, d_mixed, dw_out, loss_blk = _out_loss(mixed, x2, target2, w_out_all)
    loss = lax.psum(loss_blk[0, 0], ("x", "y", "c"))

    d_ua, d_gb, d_qm, d_o, dwp_bd, d_scale, dkm, dvm, d_gqm = _mix_bwd(proj, olse, d_mixed, km, vm, wp_bd, pool_scale, mem_q_g)
    dq_aug, dk_aug, dv_h = _fox_bwd(q_aug, k_aug, v_h, d_o, olse)
    d_q, d_k, d_v, d_f, d_gq, d_gk, d_bf = _fox_post(proj, bf_pad, fox_q_g, fox_k_g, dq_aug, dk_aug, dv_h)
    dw_kv, d_mem_g, d_gkm = _mem_bwd(mem2, mem_norm_g, w_kv_all, mem_k_g, dkm, dvm)
    pieces = (d_ua, d_q, d_k, d_v, d_gb, d_qm)
    grad_x, d_norm_g = _grad_x(pieces, d_f, w_my, x2, norm_g, d_out)
    dw_my = _grad_w_in(h16, pieces, d_f)

    dw_in_all = _from_my_columns(dw_my).reshape(D_MODEL, N_DEV, IN_CHUNK).transpose(1, 0, 2)
    dw_in_all = jnp.pad(dw_in_all, ((0, 0), (0, 0), (0, PACK_W - IN_CHUNK)))
    dw_out_all = dw_out.reshape(N_DEV, 128, D_MODEL)
    big = jnp.concatenate([dw_in_all, dw_kv.reshape(N_DEV, 128, 512), dw_out_all[:, :, :PACK_W], dw_out_all[:, :, PACK_W:]],
                          axis=1).astype(BF16)
    d_wpool = jnp.stack([dwp_bd[64 * g:64 * (g + 1), 64 * g:64 * (g + 1)] for g in range(4)])
    small_g = _pack_small(dict(norm_g=d_norm_g, mem_norm_g=d_mem_g, w_pool=d_wpool, pool_scale=d_scale, b_f=d_bf[:, :FOX_HEADS],
                               fox_q_g=d_gq, fox_k_g=d_gk, mem_q_g=d_gqm, mem_k_g=d_gkm))
    red, sred = _exchange_grads(big, small_g)

    outs = _update(red, sred, ((w_in[0], m_w_in[0], v_w_in[0]), (w_mem_kv[0], m_w_mem_kv[0], v_w_mem_kv[0]),
                               (w_out[0], m_w_out[0], v_w_out[0])),
                   (_pack_small(small_w), _pack_small(small_m), _pack_small(small_v)))
    big_out = {"w_in": outs[0:4], "w_mem_kv": outs[4:8], "w_out": outs[8:12]}
    small_out = [_unpack_small(o, small_w) for o in outs[12:16]]
    order = ["norm_g", "w_in", "b_f", "w_pool", "pool_scale", "fox_q_g", "fox_k_g", "mem_norm_g", "w_mem_kv", "mem_q_g", "mem_k_g", "w_out"]
    result = [loss, grad_x[None]]
    for kind in range(4):
        for name in order:
            result.append(big_out[name][kind][None] if name in big_out else small_out[kind][name])
    return tuple(result)
```

```python
import functools

import jax
import jax.numpy as jnp
from jax import lax
from jax.experimental import pallas as pl
from jax.experimental.pallas import tpu as pltpu

F32 = jnp.float32
BF16 = jnp.bfloat16
MESH = pl.DeviceIdType.MESH

N_DEV = 8
D_MODEL = 1024
HEAD_DIM = 64
FOX_HEADS = 8
MEM_HEADS = 4
N_MEM = 256
POOL_WIDTH = 256
EPS = 1e-6
IN_WIDTH = 3080
IN_CHUNK = IN_WIDTH // N_DEV
F_OFF = 2048
MY_WIDTH = 3200
MY_F_OFF = 3072
PIECE = 512
PACK_ROWS = 1408
PACK_W = 512
KV_ROW0 = 1024
OUT_ROW0 = 1152
SMALL_ROWS = 152
TM = 256
TA = 256
HB = 4
HALO = 16
VMEM_LIMIT = 56 * 1024 * 1024

ADAM_LR = 0.001
ADAM_B1 = 0.9
ADAM_B2 = 0.999
ADAM_EPS = 1e-08
ADAM_WD = 0.01
ADAM_STEP = 10


def _dot(a, b):
    return jnp.dot(a, b, preferred_element_type=F32)


def _dot_nt(a, b):
    return lax.dot_general(a, b, (((1,), (1,)), ((), ())), preferred_element_type=F32)


def _dot_tn(a, b):
    return lax.dot_general(a, b, (((0,), (0,)), ((), ())), preferred_element_type=F32)


def _sigmoid(g):
    return 1.0 / (1.0 + jnp.exp(-g))


def _split3(v):
    hi = v.astype(BF16)
    r1 = v - hi.astype(F32)
    mid = r1.astype(BF16)
    lo = (r1 - mid.astype(F32)).astype(BF16)
    return hi, mid, lo


def _rms(v):
    return lax.rsqrt(jnp.mean(v * v, axis=-1, keepdims=True) + EPS)


def _rms_bwd(a, r, g, dy):
    da = dy * g
    return r * (da - a * jnp.mean(da * a, axis=-1, keepdims=True)), dy * a


def _params(sem, limit=VMEM_LIMIT):
    return pltpu.CompilerParams(dimension_semantics=sem, vmem_limit_bytes=limit)


def _full(shape):
    return pl.BlockSpec(shape, lambda *_: (0,) * len(shape))


def _my_place():
    x, y, c = lax.axis_index("x"), lax.axis_index("y"), lax.axis_index("c")
    return x, y, c, 4 * x + 2 * y + c


def _gather_weights(w_in, w_kv, w_out):
    def body(win_ref, wkv_ref, wout_ref, out_ref, pay_ref, send_sems, recv_sems):
        x, y, c, me = _my_place()
        sibling = (x, y, 1 - c)
        chips = [(1 - x, y), (x, 1 - y), (1 - x, 1 - y)]

        pay_ref[...] = jnp.zeros((PACK_ROWS, PACK_W), BF16)
        pay_ref[0:KV_ROW0, 0:IN_CHUNK] = win_ref[...].astype(BF16)
        pay_ref[KV_ROW0:OUT_ROW0, :] = wkv_ref[...].astype(BF16)
        pay_ref[OUT_ROW0:OUT_ROW0 + 128, :] = wout_ref[:, 0:PACK_W].astype(BF16)
        pay_ref[OUT_ROW0 + 128:PACK_ROWS, :] = wout_ref[:, PACK_W:2 * PACK_W].astype(BF16)

        def slab(px, py, pc):
            return out_ref.at[4 * px + 2 * py + pc]

        def copy(k, block, to, src=None):
            return pltpu.make_async_remote_copy(
                src_ref=slab(*block) if src is None else src,
                dst_ref=slab(*block),
                send_sem=send_sems.at[k],
                recv_sem=recv_sems.at[k],
                device_id=to,
                device_id_type=MESH,
            )

        first = [copy(0, (x, y, c), sibling, src=pay_ref)]
        first += [copy(1 + j, (x, y, c), (*chip, c), src=pay_ref) for j, chip in enumerate(chips)]
        for cp in first:
            cp.start()
        out_ref[me] = pay_ref[...]
        passed = [copy(4 + j, (*chip, c), sibling) for j, chip in enumerate(chips)]
        for j, chip in enumerate(chips):
            copy(1 + j, (*chip, c), (x, y, c)).wait_recv()
            passed[j].start()
        copy(0, sibling, (x, y, c)).wait_recv()
        for j, chip in enumerate(chips):
            copy(4 + j, (*chip, 1 - c), (x, y, c)).wait_recv()
        for cp in first + passed:
            cp.wait_send()

    return pl.pallas_call(
        body,
        name="gather_weights",
        out_shape=jax.ShapeDtypeStruct((N_DEV, PACK_ROWS, PACK_W), BF16),
        in_specs=[pl.BlockSpec(memory_space=pltpu.VMEM)] * 3,
        out_specs=pl.BlockSpec(memory_space=pltpu.VMEM),
        scratch_shapes=[
            pltpu.VMEM((PACK_ROWS, PACK_W), BF16),
            pltpu.SemaphoreType.DMA((7,)),
            pltpu.SemaphoreType.DMA((7,)),
        ],
        compiler_params=pltpu.CompilerParams(vmem_limit_bytes=VMEM_LIMIT),
    )(w_in, w_kv, w_out)


def _exchange_grads(big, small):
    def body(big_ref, small_ref, red_ref, sred_ref, land_ref, sland_ref, send_sems, recv_sems, ssend_sems, srecv_sems):
        x, y, c, me = _my_place()
        copies = []
        for k in range(1, N_DEV):
            px, py, pc = x ^ (k >> 2), y ^ ((k >> 1) & 1), c ^ (k & 1)
            peer = 4 * px + 2 * py + pc
            copies.append(pltpu.make_async_remote_copy(
                src_ref=big_ref.at[peer], dst_ref=land_ref.at[me],
                send_sem=send_sems.at[k - 1], recv_sem=recv_sems.at[k - 1],
                device_id=(px, py, pc), device_id_type=MESH))
            copies.append(pltpu.make_async_remote_copy(
                src_ref=small_ref, dst_ref=sland_ref.at[me],
                send_sem=ssend_sems.at[k - 1], recv_sem=srecv_sems.at[k - 1],
                device_id=(px, py, pc), device_id_type=MESH))
        for cp in copies:
            cp.start()
        land_ref[me] = big_ref[me]
        sland_ref[me] = small_ref[...]
        for cp in copies:
            cp.wait_recv()
        for cp in copies:
            cp.wait_send()
        acc = land_ref[0].astype(F32)
        sacc = sland_ref[0]
        for d in range(1, N_DEV):
            acc = acc + land_ref[d].astype(F32)
            sacc = sacc + sland_ref[d]
        red_ref[...] = acc
        sred_ref[...] = sacc

    return pl.pallas_call(
        body,
        name="exchange_grads",
        out_shape=(jax.ShapeDtypeStruct((PACK_ROWS, PACK_W), F32), jax.ShapeDtypeStruct((SMALL_ROWS, 128), F32)),
        in_specs=[pl.BlockSpec(memory_space=pltpu.VMEM)] * 2,
        out_specs=(pl.BlockSpec(memory_space=pltpu.VMEM),) * 2,
        scratch_shapes=[
            pltpu.VMEM((N_DEV, PACK_ROWS, PACK_W), BF16),
            pltpu.VMEM((N_DEV, SMALL_ROWS, 128), F32),
            pltpu.SemaphoreType.DMA((7,)),
            pltpu.SemaphoreType.DMA((7,)),
            pltpu.SemaphoreType.DMA((7,)),
            pltpu.SemaphoreType.DMA((7,)),
        ],
        compiler_params=pltpu.CompilerParams(vmem_limit_bytes=VMEM_LIMIT),
    )(big, small)


def _fwd_proj(x, norm_g, w_my):
    s_len = x.shape[0]

    def body(x_ref, g_ref, w_ref, proj_ref, h_ref):
        xv = x_ref[...]
        h = (xv * _rms(xv) * g_ref[...]).astype(BF16)
        h_ref[...] = h
        proj_ref[...] = _dot(h, w_ref[...])

    return pl.pallas_call(
        body,
        name="fwd_proj",
        grid=(s_len // TM,),
        in_specs=[pl.BlockSpec((TM, D_MODEL), lambda i: (i, 0)), _full((1, D_MODEL)), _full((D_MODEL, MY_WIDTH))],
        out_specs=[pl.BlockSpec((TM, MY_WIDTH), lambda i: (i, 0)), pl.BlockSpec((TM, D_MODEL), lambda i: (i, 0))],
        out_shape=[jax.ShapeDtypeStruct((s_len, MY_WIDTH), F32), jax.ShapeDtypeStruct((s_len, D_MODEL), BF16)],
        compiler_params=_params(("parallel",)),
    )(x, norm_g, w_my)


def _log_sigmoid(z):
    return jnp.minimum(z, 0.0) - jnp.log(1.0 + jnp.exp(-jnp.abs(z)))


def _fox_prep(proj, bf_pad, gq, gk):
    s_len = proj.shape[0]

    def body(q_ref, k_ref, v_ref, f_ref, bf_ref, gq_ref, gk_ref, qa_ref, ka_ref, vh_ref, kt_ref, vt_ref, carry_ref):
        @pl.when(pl.program_id(0) == 0)
        def _():
            carry_ref[...] = jnp.zeros((1, 128), F32)

        lane = lax.broadcasted_iota(jnp.int32, (TM, 128), 1)
        logf = jnp.where(lane < FOX_HEADS, _log_sigmoid(f_ref[...] + bf_ref[...]), 0.0)
        row = lax.broadcasted_iota(jnp.int32, (TM, TM), 0)
        col = lax.broadcasted_iota(jnp.int32, (TM, TM), 1)
        tri = jnp.where(col <= row, 1.0, 0.0).astype(BF16)
        hi, mid, lo = _split3(logf)
        cum = _dot(tri, hi) + _dot(tri, mid) + _dot(tri, lo) + carry_ref[...]
        carry_ref[...] = cum[TM - 1:TM, :]
        f_hi, f_mid, f_lo = [t.astype(F32) for t in _split3(cum)]
        zeros = jnp.zeros((TM, HEAD_DIM), F32)
        for h in range(FOX_HEADS):
            sl = slice(HEAD_DIM * h, HEAD_DIM * (h + 1))
            qh = q_ref[:, sl]
            kh = k_ref[:, sl]
            qn = qh * _rms(qh) * gq_ref[...] * 0.125
            kn = kh * _rms(kh) * gk_ref[...]
            a, b, c = f_hi[:, h:h + 1], f_mid[:, h:h + 1], f_lo[:, h:h + 1]
            qa = jnp.where(lane < 64, jnp.concatenate([qn, zeros], axis=1),
                           jnp.where(lane == 64, a, jnp.where(lane == 65, b, jnp.where(lane == 66, c,
                                                                                         jnp.where(lane < 70, 1.0, 0.0)))))
            ka = jnp.where(lane < 64, jnp.concatenate([kn, zeros], axis=1),
                           jnp.where(lane < 67, 1.0, jnp.where(lane == 67, -a, jnp.where(lane == 68, -b,
                                                                                          jnp.where(lane == 69, -c, 0.0)))))
            vh = v_ref[:, sl]
            v_aug = jnp.where(lane < 64, jnp.concatenate([vh, zeros], axis=1), jnp.where(lane == 64, 1.0, 0.0))
            qa_ref[h] = qa.astype(BF16)
            ka_ref[h] = ka.astype(BF16)
            vh_ref[h] = vh.astype(BF16)
            kt_ref[h, 0] = ka.T.astype(BF16)
            vt_ref[h, 0] = v_aug.T.astype(BF16)

    col_blk = lambda j: pl.BlockSpec((TM, PIECE), lambda i: (i, j))
    head_blk = lambda w: pl.BlockSpec((FOX_HEADS, TM, w), lambda i: (0, i, 0))
    tile_blk = pl.BlockSpec((FOX_HEADS, 1, 128, TM), lambda i: (0, i, 0, 0))
    tiled = jax.ShapeDtypeStruct((FOX_HEADS, s_len // TM, 128, TM), BF16)
    return pl.pallas_call(
        body,
        name="fox_prep",
        grid=(s_len // TM,),
        in_specs=[col_blk(1), col_blk(2), col_blk(3), pl.BlockSpec((TM, 128), lambda i: (i, MY_F_OFF // 128)),
                  _full((1, 128)), _full((1, HEAD_DIM)), _full((1, HEAD_DIM))],
        out_specs=[head_blk(128), head_blk(128), head_blk(HEAD_DIM), tile_blk, tile_blk],
        out_shape=[jax.ShapeDtypeStruct((FOX_HEADS, s_len, 128), BF16), jax.ShapeDtypeStruct((FOX_HEADS, s_len, 128), BF16),
                   jax.ShapeDtypeStruct((FOX_HEADS, s_len, HEAD_DIM), BF16), tiled, tiled],
        scratch_shapes=[pltpu.VMEM((1, 128), F32)],
        compiler_params=_params(("arbitrary",)),
    )(proj, proj, proj, proj, bf_pad, gq, gk)


def _mem_prep(mem, g, w_kv, gk):
    def body(mem_ref, g_ref, w_ref, gk_ref, km_ref, vm_ref):
        m = mem_ref[...]
        kv = _dot((m * _rms(m) * g_ref[...]).astype(BF16), w_ref[...])
        for h in range(MEM_HEADS):
            sl = slice(HEAD_DIM * h, HEAD_DIM * (h + 1))
            kh = kv[:, sl]
            km_ref[:, sl] = (kh * _rms(kh) * gk_ref[...]).astype(BF16)
        vm_ref[...] = kv[:, 256:512].astype(BF16)

    return pl.pallas_call(
        body,
        name="mem_prep",
        out_shape=(jax.ShapeDtypeStruct((N_MEM, 256), BF16),) * 2,
        in_specs=[pl.BlockSpec(memory_space=pltpu.VMEM)] * 4,
        out_specs=(pl.BlockSpec(memory_space=pltpu.VMEM),) * 2,
        compiler_params=pltpu.CompilerParams(vmem_limit_bytes=VMEM_LIMIT),
    )(mem, g, w_kv, gk)


def _causal_mask():
    row = lax.broadcasted_iota(jnp.int32, (TA, TA), 0)
    col = lax.broadcasted_iota(jnp.int32, (TA, TA), 1)
    return col <= row


def _causal_mask_t():
    row = lax.broadcasted_iota(jnp.int32, (TA, TA), 0)
    col = lax.broadcasted_iota(jnp.int32, (TA, TA), 1)
    return row <= col


def _fox_fwd(q_aug, k_aug, vt_aug):
    s_len = q_aug.shape[1]
    n_tiles = s_len // TA

    def body(q_ref, k_ref, vt_ref, o_ref, st_ref):
        qi = pl.program_id(1)
        qs = [q_ref[h] for h in range(HB)]

        def step(j, carry, masked):
            rows = pl.ds(pl.multiple_of(j * TA, TA), TA)
            scores = [_dot_nt(k_ref[h, rows, :], qs[h]) for h in range(HB)]
            soft = []
            for h in range(HB):
                m, _ = carry[h]
                s = jnp.where(_causal_mask_t(), scores[h], -1e30) if masked else scores[h]
                m_new = jnp.maximum(m, jnp.max(s, axis=0, keepdims=True))
                soft.append((m_new, jnp.exp(m - m_new), jnp.exp(s - m_new).astype(BF16)))
            return tuple((m_new, alpha * carry[h][1] + _dot(vt_ref[h, j], p)) for h, (m_new, alpha, p) in enumerate(soft))

        init = tuple((jnp.full((1, TA), -1e30, F32), jnp.zeros((128, TA), F32)) for _ in range(HB))
        carry = lax.fori_loop(0, qi, lambda j, cr: step(j, cr, False), init)
        carry = step(qi, carry, True)
        for h in range(HB):
            m, acc = carry[h]
            l = acc[HEAD_DIM:HEAD_DIM + 1, :]
            lse = m + jnp.log(l)
            o_ref[h] = jnp.concatenate([acc[0:HEAD_DIM] / l, jnp.broadcast_to(lse, (HEAD_DIM, TA))], axis=0).T
            st_ref[h, 0] = jnp.broadcast_to(lse, (8, TA))

    return pl.pallas_call(
        body,
        name="fox_fwd",
        grid=(FOX_HEADS // HB, n_tiles),
        in_specs=[pl.BlockSpec((HB, TA, 128), lambda g, i: (g, i, 0)), pl.BlockSpec((HB, s_len, 128), lambda g, i: (g, 0, 0)),
                  pl.BlockSpec((HB, n_tiles, 128, TA), lambda g, i: (g, 0, 0, 0))],
        out_specs=[pl.BlockSpec((HB, TA, 128), lambda g, i: (g, i, 0)), pl.BlockSpec((HB, 1, 8, TA), lambda g, i: (g, i, 0, 0))],
        out_shape=[jax.ShapeDtypeStruct((FOX_HEADS, s_len, 128), F32), jax.ShapeDtypeStruct((FOX_HEADS, n_tiles, 8, TA), F32)],
        compiler_params=_params(("parallel", "parallel")),
    )(q_aug, k_aug, vt_aug)


def _lane_group(lane, a, b, c, d):
    return jnp.where(lane < 64, a, jnp.where(lane < 128, b, jnp.where(lane < 192, c, d)))


def _pool_count(row0):
    lane = lax.broadcasted_iota(jnp.int32, (TM, POOL_WIDTH), 1)
    t1 = row0 + lax.broadcasted_iota(jnp.int32, (TM, POOL_WIDTH), 0) + 1
    return jnp.minimum(t1, _lane_group(lane, 2, 4, 8, 16)).astype(F32), lane


def _pool_delta(u, halo, cnt, lane):
    xe = jnp.concatenate([halo, u], axis=0)
    s2 = xe + pltpu.roll(xe, 1, 0)
    s4 = s2 + pltpu.roll(s2, 2, 0)
    s8 = s4 + pltpu.roll(s4, 4, 0)
    s16 = s8 + pltpu.roll(s8, 8, 0)
    win = _lane_group(lane, s2[HALO:], s4[HALO:], s8[HALO:], s16[HALO:])
    return win / cnt - u


def _pool_delta_bwd(dd, dd_next, cnt, cnt_next, lane):
    ee = jnp.concatenate([dd / cnt, dd_next / cnt_next], axis=0)
    n = TM + HALO
    r2 = ee + pltpu.roll(ee, n - 1, 0)
    r4 = r2 + pltpu.roll(r2, n - 2, 0)
    r8 = r4 + pltpu.roll(r4, n - 4, 0)
    r16 = r8 + pltpu.roll(r8, n - 8, 0)
    return _lane_group(lane, r2[:TM], r4[:TM], r8[:TM], r16[:TM]) - dd


def _mem_attn(qm, gq, km_ref, vm_ref):
    heads = []
    for h in range(MEM_HEADS):
        sl = slice(HEAD_DIM * h, HEAD_DIM * (h + 1))
        qh = qm[:, sl]
        r = _rms(qh)
        a = qh * r
        q16 = (a * gq * 0.125).astype(BF16)
        s = _dot_nt(q16, km_ref[:, sl])
        e = jnp.exp(s - jnp.max(s, axis=-1, keepdims=True))
        p = e / jnp.sum(e, axis=-1, keepdims=True)
        heads.append((a, r, q16, p, _dot(p.astype(BF16), vm_ref[:, sl])))
    return heads


def _row_specs(s_len):
    n_halo = s_len // HALO
    piece = lambda j: pl.BlockSpec((TM, PIECE), lambda i: (i, j))
    before = lambda j: pl.BlockSpec((HALO, 256), lambda i: (jnp.maximum(i * (TM // HALO) - 1, 0), j))
    after = lambda j: pl.BlockSpec((HALO, 256), lambda i: (jnp.minimum((i + 1) * (TM // HALO), n_halo - 1), j))
    return piece, before, after


def _mix_fwd(proj, olse, km, vm, wp_bd, pool_scale, gq_m):
    s_len = proj.shape[0]

    def body(ua_ref, halo_ref, gb_ref, qm_ref, ol_ref, km_ref, vm_ref, wp_ref, sc_ref, gq_ref, out_ref):
        i = pl.program_id(0)
        u = ua_ref[:, 0:256]
        g_a = ua_ref[:, 256:512]
        halo = jnp.where(i > 0, halo_ref[...], 0.0)
        cnt, lane = _pool_count(i * TM)
        d = _pool_delta(u, halo, cnt, lane).astype(BF16)
        y_a = _dot(d, wp_ref[...]) * sc_ref[...]
        out_ref[:, 0:256] = (y_a * (g_a * _sigmoid(g_a))).astype(BF16)
        for h in range(FOX_HEADS):
            sl = slice(HEAD_DIM * h, HEAD_DIM * (h + 1))
            g = gb_ref[:, sl]
            out_ref[:, 256 + HEAD_DIM * h:256 + HEAD_DIM * (h + 1)] = (ol_ref[h][:, 0:HEAD_DIM] * (g * _sigmoid(g))).astype(BF16)
        heads = _mem_attn(qm_ref[:, 0:256], gq_ref[...], km_ref, vm_ref)
        for h in range(MEM_HEADS):
            g = qm_ref[:, 256 + HEAD_DIM * h:256 + HEAD_DIM * (h + 1)]
            out_ref[:, 768 + HEAD_DIM * h:768 + HEAD_DIM * (h + 1)] = (heads[h][4] * (g * _sigmoid(g))).astype(BF16)

    piece, before, _ = _row_specs(s_len)
    return pl.pallas_call(
        body,
        name="mix_fwd",
        grid=(s_len // TM,),
        in_specs=[piece(0), before(0), piece(4), piece(5), pl.BlockSpec((FOX_HEADS, TM, 128), lambda i: (0, i, 0)),
                  _full((N_MEM, 256)), _full((N_MEM, 256)), _full((256, 256)), _full((1, 256)), _full((1, HEAD_DIM))],
        out_specs=pl.BlockSpec((TM, D_MODEL), lambda i: (i, 0)),
        out_shape=jax.ShapeDtypeStruct((s_len, D_MODEL), BF16),
        compiler_params=_params(("parallel",)),
    )(proj, proj, proj, proj, olse, km, vm, wp_bd, pool_scale, gq_m)


def _out_loss(mixed, x, target, w_out):
    s_len = x.shape[0]

    def body(mix_ref, x_ref, t_ref, w_ref, dout_ref, dmix_ref, dw_ref, loss_ref):
        @pl.when(pl.program_id(0) == 0)
        def _():
            dw_ref[...] = jnp.zeros((D_MODEL, D_MODEL), F32)
            loss_ref[...] = jnp.zeros((8, 128), F32)

        mixed16 = mix_ref[...]
        err = x_ref[...] + _dot(mixed16, w_ref[...]) - t_ref[...]
        loss_ref[...] += 0.5 * jnp.sum(jnp.mean(err * err, axis=-1, keepdims=True))
        d_out = err / float(D_MODEL)
        dout_ref[...] = d_out
        d16 = d_out.astype(BF16)
        dmix_ref[...] = _dot_nt(d16, w_ref[...])
        dw_ref[...] += _dot_tn(mixed16, d16)

    row = pl.BlockSpec((TM, D_MODEL), lambda i: (i, 0))
    return pl.pallas_call(
        body,
        name="out_loss",
        grid=(s_len // TM,),
        in_specs=[row, row, row, _full((D_MODEL, D_MODEL))],
        out_specs=[row, row, _full((D_MODEL, D_MODEL)), _full((8, 128))],
        out_shape=[jax.ShapeDtypeStruct((s_len, D_MODEL), F32), jax.ShapeDtypeStruct((s_len, D_MODEL), F32),
                   jax.ShapeDtypeStruct((D_MODEL, D_MODEL), F32), jax.ShapeDtypeStruct((8, 128), F32)],
        compiler_params=_params(("arbitrary",)),
    )(mixed, x, target, w_out)


def _silu_pair(g):
    s = _sigmoid(g)
    return g * s, s * (1.0 + g * (1.0 - s))


def _mix_bwd(proj, olse, d_mixed, km, vm, wp_bd, pool_scale, gq_m):
    s_len = proj.shape[0]
    n_tiles = s_len // TM

    def body(ua_ref, halo_ref, ga_next_ref, gb_ref, qm_ref, ol_ref, dm_ref, dm_next_ref, km_ref, vm_ref, wp_ref, sc_ref, gq_ref,
             dua_ref, dgb_ref, dqm_ref, do_ref, dl_ref, dwp_ref, dsc_ref, dkm_ref, dvm_ref, dgq_ref):
        i = pl.program_id(0)

        @pl.when(i == 0)
        def _():
            dwp_ref[...] = jnp.zeros((256, 256), F32)
            dsc_ref[...] = jnp.zeros((1, 256), F32)
            dkm_ref[...] = jnp.zeros((N_MEM, 256), F32)
            dvm_ref[...] = jnp.zeros((N_MEM, 256), F32)
            dgq_ref[...] = jnp.zeros((1, HEAD_DIM), F32)

        u = ua_ref[:, 0:256]
        g_a = ua_ref[:, 256:512]
        halo = jnp.where(i > 0, halo_ref[...], 0.0)
        cnt, lane = _pool_count(i * TM)
        d16 = _pool_delta(u, halo, cnt, lane).astype(BF16)
        t = _dot(d16, wp_ref[...])
        y_a = t * sc_ref[...]
        silu_a, dsilu_a = _silu_pair(g_a)
        dm_a = dm_ref[:, 0:256]
        dy_a = dm_a * silu_a
        dsc_ref[...] += jnp.sum(dy_a * t, axis=0, keepdims=True)
        dt16 = (dy_a * sc_ref[...]).astype(BF16)
        dwp_ref[...] += _dot_tn(d16, dt16)
        dd = _dot_nt(dt16, wp_ref[...])
        g_next = ga_next_ref[...]
        dt_next = (dm_next_ref[...] * (g_next * _sigmoid(g_next)) * sc_ref[...]).astype(BF16)
        dd_next = jnp.where(i < n_tiles - 1, _dot_nt(dt_next, wp_ref[...]), 0.0)
        cnt_next = _pool_count((i + 1) * TM)[0][0:HALO]
        dua_ref[:, 0:256] = _pool_delta_bwd(dd, dd_next, cnt, cnt_next, lane).astype(BF16)
        dua_ref[:, 256:512] = (dm_a * y_a * dsilu_a).astype(BF16)

        ones = jnp.ones((8, HEAD_DIM), BF16)
        for h in range(FOX_HEADS):
            sl = slice(HEAD_DIM * h, HEAD_DIM * (h + 1))
            silu_b, dsilu_b = _silu_pair(gb_ref[:, sl])
            dm_b = dm_ref[:, 256 + HEAD_DIM * h:256 + HEAD_DIM * (h + 1)]
            o_h = ol_ref[h][:, 0:HEAD_DIM]
            d_o = dm_b * silu_b
            do_ref[h] = d_o.astype(BF16)
            dgb_ref[:, sl] = (dm_b * o_h * dsilu_b).astype(BF16)
            prod = d_o * o_h
            hi = prod.astype(BF16)
            lo = (prod - hi.astype(F32)).astype(BF16)
            dl_ref[h, 0] = _dot_nt(ones, hi) + _dot_nt(ones, lo)

        heads = _mem_attn(qm_ref[:, 0:256], gq_ref[...], km_ref, vm_ref)
        dgq = jnp.zeros((TM, HEAD_DIM), F32)
        for h in range(MEM_HEADS):
            sl = slice(HEAD_DIM * h, HEAD_DIM * (h + 1))
            a, r, q16, p, y_m = heads[h]
            silu_m, dsilu_m = _silu_pair(qm_ref[:, 256 + HEAD_DIM * h:256 + HEAD_DIM * (h + 1)])
            dm_m = dm_ref[:, 768 + HEAD_DIM * h:768 + HEAD_DIM * (h + 1)]
            dqm_ref[:, 256 + HEAD_DIM * h:256 + HEAD_DIM * (h + 1)] = (dm_m * y_m * dsilu_m).astype(BF16)
            dy16 = (dm_m * silu_m).astype(BF16)
            dp = _dot_nt(dy16, vm_ref[:, sl])
            dvm_ref[:, sl] += _dot_tn(p.astype(BF16), dy16)
            ds16 = (p * (dp - jnp.sum(dp * p, axis=-1, keepdims=True))).astype(BF16)
            dkm_ref[:, sl] += _dot_tn(ds16, q16)
            dq, dg_rows = _rms_bwd(a, r, gq_ref[...], _dot(ds16, km_ref[:, sl]) * 0.125)
            dqm_ref[:, sl] = dq.astype(BF16)
            dgq = dgq + dg_rows
        dgq_ref[...] += jnp.sum(dgq, axis=0, keepdims=True)

    piece, before, after = _row_specs(s_len)
    n_halo = s_len // HALO
    row = pl.BlockSpec((TM, D_MODEL), lambda i: (i, 0))
    dm_after = pl.BlockSpec((HALO, 256), lambda i: (jnp.minimum((i + 1) * (TM // HALO), n_halo - 1), 0))
    out_piece = pl.BlockSpec((TM, PIECE), lambda i: (i, 0))
    return pl.pallas_call(
        body,
        name="mix_bwd",
        grid=(n_tiles,),
        in_specs=[piece(0), before(0), after(1), piece(4), piece(5), pl.BlockSpec((FOX_HEADS, TM, 128), lambda i: (0, i, 0)),
                  row, dm_after, _full((N_MEM, 256)), _full((N_MEM, 256)), _full((256, 256)), _full((1, 256)), _full((1, HEAD_DIM))],
        out_specs=[out_piece, out_piece, out_piece, pl.BlockSpec((FOX_HEADS, TM, HEAD_DIM), lambda i: (0, i, 0)),
                   pl.BlockSpec((FOX_HEADS, 1, 8, TM), lambda i: (0, i, 0, 0)),
                   _full((256, 256)), _full((1, 256)), _full((N_MEM, 256)), _full((N_MEM, 256)), _full((1, HEAD_DIM))],
        out_shape=[jax.ShapeDtypeStruct((s_len, PIECE), BF16)] * 3 + [
            jax.ShapeDtypeStruct((FOX_HEADS, s_len, HEAD_DIM), BF16),
            jax.ShapeDtypeStruct((FOX_HEADS, n_tiles, 8, TM), F32),
            jax.ShapeDtypeStruct((256, 256), F32), jax.ShapeDtypeStruct((1, 256), F32),
            jax.ShapeDtypeStruct((N_MEM, 256), F32), jax.ShapeDtypeStruct((N_MEM, 256), F32),
            jax.ShapeDtypeStruct((1, HEAD_DIM), F32)],
        compiler_params=_params(("arbitrary",)),
    )(proj, proj, proj, proj, proj, olse, d_mixed, d_mixed, km, vm, wp_bd, pool_scale, gq_m)


def _fox_bwd(q_aug, k_aug, kt_aug, v_h, d_o, lse_rows, delta_rows):
    s_len = q_aug.shape[1]
    n_tiles = s_len // TA

    def body(q_ref, k_ref, kt_ref, v_ref, do_ref, st_ref, dl_ref, dqt_ref, dk_ref, dv_ref):
        j = pl.program_id(1)

        @pl.when(j == 0)
        def _():
            dqt_ref[...] = jnp.zeros((HB, n_tiles, 128, TA), F32)

        ks = [k_ref[h] for h in range(HB)]
        kts = [kt_ref[h, 0] for h in range(HB)]
        vs = [v_ref[h] for h in range(HB)]

        def pair(i, acc, masked):
            rows = pl.ds(pl.multiple_of(i * TA, TA), TA)
            qs = [q_ref[h, rows, :] for h in range(HB)]
            d_os = [do_ref[h, rows, :] for h in range(HB)]
            scores = [(_dot_nt(ks[h], qs[h]), _dot_nt(vs[h], d_os[h])) for h in range(HB)]
            probs = []
            for h in range(HB):
                s, dp = scores[h]
                if masked:
                    s = jnp.where(_causal_mask_t(), s, -1e30)
                p = jnp.exp(s - st_ref[h, i, 0:1, :])
                probs.append((p.astype(BF16), (p * (dp - dl_ref[h, i, 0:1, :])).astype(BF16)))
            out = []
            for h in range(HB):
                p16, ds16 = probs[h]
                dqt_ref[h, i] += _dot(kts[h], ds16)
                out.append((acc[h][0] + _dot(ds16, qs[h]), acc[h][1] + _dot(p16, d_os[h])))
            return tuple(out)

        zero = tuple((jnp.zeros((TA, 128), F32), jnp.zeros((TA, HEAD_DIM), F32)) for _ in range(HB))
        acc = pair(j, zero, True)
        acc = lax.fori_loop(j + 1, n_tiles, lambda i, a: pair(i, a, False), acc)
        for h in range(HB):
            dk_ref[h] = acc[h][0]
            dv_ref[h] = acc[h][1]

    whole = lambda w: pl.BlockSpec((HB, s_len, w), lambda g, j: (g, 0, 0))
    tile = lambda w: pl.BlockSpec((HB, TA, w), lambda g, j: (g, j, 0))
    rows_blk = lambda r: pl.BlockSpec((HB, n_tiles, r, TA), lambda g, j: (g, 0, 0, 0))
    return pl.pallas_call(
        body,
        name="fox_bwd",
        grid=(FOX_HEADS // HB, n_tiles),
        in_specs=[whole(128), tile(128), pl.BlockSpec((HB, 1, 128, TA), lambda g, j: (g, j, 0, 0)), tile(HEAD_DIM),
                  whole(HEAD_DIM), rows_blk(8), rows_blk(8)],
        out_specs=[rows_blk(128), tile(128), tile(HEAD_DIM)],
        out_shape=[jax.ShapeDtypeStruct((FOX_HEADS, n_tiles, 128, TA), F32), jax.ShapeDtypeStruct((FOX_HEADS, s_len, 128), F32),
                   jax.ShapeDtypeStruct((FOX_HEADS, s_len, HEAD_DIM), F32)],
        compiler_params=_params(("parallel", "arbitrary")),
    )(q_aug, k_aug, kt_aug, v_h, d_o, lse_rows, delta_rows)


def _fox_post(proj, bf_pad, gq, gk, dq_aug, dk_aug, dv_h):
    s_len = proj.shape[0]
    n_tiles = s_len // TM

    def body(q_ref, k_ref, f_ref, bf_ref, gq_ref, gk_ref, dqa_ref, dka_ref, dvh_ref,
             dq_ref, dk_ref, dv_ref, df_ref, dgq_ref, dgk_ref, dbf_ref, carry_ref):
        @pl.when(pl.program_id(0) == 0)
        def _():
            carry_ref[...] = jnp.zeros((1, 128), F32)
            dgq_ref[...] = jnp.zeros((1, HEAD_DIM), F32)
            dgk_ref[...] = jnp.zeros((1, HEAD_DIM), F32)
            dbf_ref[...] = jnp.zeros((1, 128), F32)

        lane = lax.broadcasted_iota(jnp.int32, (TM, 128), 1)
        d_cum = jnp.zeros((TM, 128), F32)
        dgq = jnp.zeros((TM, HEAD_DIM), F32)
        dgk = jnp.zeros((TM, HEAD_DIM), F32)
        for h in range(FOX_HEADS):
            sl = slice(HEAD_DIM * h, HEAD_DIM * (h + 1))
            dqa = dqa_ref[h, 0].T
            dka = dka_ref[h]
            qh = q_ref[:, sl]
            kh = k_ref[:, sl]
            rq = _rms(qh)
            rk = _rms(kh)
            dq, dgq_rows = _rms_bwd(qh * rq, rq, gq_ref[...], dqa[:, 0:HEAD_DIM] * 0.125)
            dk, dgk_rows = _rms_bwd(kh * rk, rk, gk_ref[...], dka[:, 0:HEAD_DIM])
            dq_ref[:, sl] = dq.astype(BF16)
            dk_ref[:, sl] = dk.astype(BF16)
            dv_ref[:, sl] = dvh_ref[h].astype(BF16)
            dgq = dgq + dgq_rows
            dgk = dgk + dgk_rows
            d_cum = jnp.where(lane == h, dqa[:, 64:65] - dka[:, 67:68], d_cum)
        dgq_ref[...] += jnp.sum(dgq, axis=0, keepdims=True)
        dgk_ref[...] += jnp.sum(dgk, axis=0, keepdims=True)

        row = lax.broadcasted_iota(jnp.int32, (TM, TM), 0)
        col = lax.broadcasted_iota(jnp.int32, (TM, TM), 1)
        tri = jnp.where(col >= row, 1.0, 0.0).astype(BF16)
        hi, mid, lo = _split3(d_cum)
        d_logf = _dot(tri, hi) + _dot(tri, mid) + _dot(tri, lo) + carry_ref[...]
        carry_ref[...] = d_logf[0:1, :]
        z = f_ref[...] + bf_ref[...]
        d_f = jnp.where(lane < FOX_HEADS, d_logf * _sigmoid(-z), 0.0)
        df_ref[...] = d_f.astype(BF16)
        dbf_ref[...] += jnp.sum(d_f, axis=0, keepdims=True)

    rev = lambda i: n_tiles - 1 - i
    col_blk = lambda j: pl.BlockSpec((TM, PIECE), lambda i: (rev(i), j))
    head_blk = lambda w: pl.BlockSpec((FOX_HEADS, TM, w), lambda i: (0, rev(i), 0))
    out_piece = pl.BlockSpec((TM, PIECE), lambda i: (rev(i), 0))
    return pl.pallas_call(
        body,
        name="fox_post",
        grid=(n_tiles,),
        in_specs=[col_blk(1), col_blk(2), pl.BlockSpec((TM, 128), lambda i: (rev(i), MY_F_OFF // 128)),
                  _full((1, 128)), _full((1, HEAD_DIM)), _full((1, HEAD_DIM)),
                  pl.BlockSpec((FOX_HEADS, 1, 128, TM), lambda i: (0, rev(i), 0, 0)), head_blk(128), head_blk(HEAD_DIM)],
        out_specs=[out_piece, out_piece, out_piece, pl.BlockSpec((TM, 128), lambda i: (rev(i), 0)),
                   _full((1, HEAD_DIM)), _full((1, HEAD_DIM)), _full((1, 128))],
        out_shape=[jax.ShapeDtypeStruct((s_len, PIECE), BF16)] * 3 + [
            jax.ShapeDtypeStruct((s_len, 128), BF16), jax.ShapeDtypeStruct((1, HEAD_DIM), F32),
            jax.ShapeDtypeStruct((1, HEAD_DIM), F32), jax.ShapeDtypeStruct((1, 128), F32)],
        scratch_shapes=[pltpu.VMEM((1, 128), F32)],
        compiler_params=_params(("arbitrary",)),
    )(proj, proj, proj, bf_pad, gq, gk, dq_aug, dk_aug, dv_h)


def _mem_bwd(mem, g, w_kv, gk, dkm, dvm):
    def body(mem_ref, g_ref, w_ref, gk_ref, dkm_ref, dvm_ref, dw_ref, dg_ref, dgk_ref, dkv_ref):
        m = mem_ref[...]
        r = _rms(m)
        a = m * r
        mn16 = (a * g_ref[...]).astype(BF16)
        kv = _dot(mn16, w_ref[...])
        dgk = jnp.zeros((N_MEM, HEAD_DIM), F32)
        for h in range(MEM_HEADS):
            sl = slice(HEAD_DIM * h, HEAD_DIM * (h + 1))
            kh = kv[:, sl]
            rk = _rms(kh)
            dk, dg_rows = _rms_bwd(kh * rk, rk, gk_ref[...], dkm_ref[:, sl])
            dkv_ref[:, sl] = dk.astype(BF16)
            dgk = dgk + dg_rows
        dkv_ref[:, 256:512] = dvm_ref[...].astype(BF16)
        dgk_ref[...] = jnp.sum(dgk, axis=0, keepdims=True)
        dkv16 = dkv_ref[...]
        dw_ref[...] = _dot_tn(mn16, dkv16)
        dg_ref[...] = jnp.sum(_dot_nt(dkv16, w_ref[...]) * a, axis=0, keepdims=True)

    return pl.pallas_call(
        body,
        name="mem_bwd",
        out_shape=(jax.ShapeDtypeStruct((D_MODEL, 512), F32), jax.ShapeDtypeStruct((1, D_MODEL), F32),
                   jax.ShapeDtypeStruct((1, HEAD_DIM), F32)),
        in_specs=[pl.BlockSpec(memory_space=pltpu.VMEM)] * 6,
        out_specs=(pl.BlockSpec(memory_space=pltpu.VMEM),) * 3,
        scratch_shapes=[pltpu.VMEM((N_MEM, 512), BF16)],
        compiler_params=pltpu.CompilerParams(vmem_limit_bytes=VMEM_LIMIT),
    )(mem, g, w_kv, gk, dkm, dvm)


def _grad_x(pieces, d_f, w_my, x, norm_g, d_out):
    s_len = x.shape[0]

    def body(p0, p1, p2, p3, p4, p5, df_ref, w_ref, x_ref, g_ref, dout_ref, gx_ref, dg_ref):
        @pl.when(pl.program_id(0) == 0)
        def _():
            dg_ref[...] = jnp.zeros((1, D_MODEL), F32)

        dh = _dot_nt(df_ref[...], w_ref[:, MY_F_OFF:MY_WIDTH])
        for j, p in enumerate((p0, p1, p2, p3, p4, p5)):
            dh = dh + _dot_nt(p[...], w_ref[:, PIECE * j:PIECE * (j + 1)])
        xv = x_ref[...]
        r = _rms(xv)
        dx, dg_rows = _rms_bwd(xv * r, r, g_ref[...], dh)
        gx_ref[...] = dout_ref[...] + dx
        dg_ref[...] += jnp.sum(dg_rows, axis=0, keepdims=True)

    piece = pl.BlockSpec((TM, PIECE), lambda i: (i, 0))
    row = pl.BlockSpec((TM, D_MODEL), lambda i: (i, 0))
    return pl.pallas_call(
        body,
        name="grad_x",
        grid=(s_len // TM,),
        in_specs=[piece] * 6 + [pl.BlockSpec((TM, 128), lambda i: (i, 0)), _full((D_MODEL, MY_WIDTH)), row,
                                _full((1, D_MODEL)), row],
        out_specs=[row, _full((1, D_MODEL))],
        out_shape=[jax.ShapeDtypeStruct((s_len, D_MODEL), F32), jax.ShapeDtypeStruct((1, D_MODEL), F32)],
        compiler_params=_params(("arbitrary",)),
    )(*pieces, d_f, w_my, x, norm_g, d_out)


def _grad_w_in(h16, pieces, d_f):
    s_len = h16.shape[0]
    tk = 512

    def body(h_ref, p0, p1, p2, p3, p4, p5, df_ref, dw_ref):
        @pl.when(pl.program_id(0) == 0)
        def _():
            dw_ref[...] = jnp.zeros((D_MODEL, MY_WIDTH), F32)

        h = h_ref[...]
        for j, p in enumerate((p0, p1, p2, p3, p4, p5)):
            dw_ref[:, PIECE * j:PIECE * (j + 1)] += _dot_tn(h, p[...])
        dw_ref[:, MY_F_OFF:MY_WIDTH] += _dot_tn(h, df_ref[...])

    piece = pl.BlockSpec((tk, PIECE), lambda i: (i, 0))
    return pl.pallas_call(
        body,
        name="grad_w_in",
        grid=(s_len // tk,),
        in_specs=[pl.BlockSpec((tk, D_MODEL), lambda i: (i, 0))] + [piece] * 6 + [pl.BlockSpec((tk, 128), lambda i: (i, 0))],
        out_specs=_full((D_MODEL, MY_WIDTH)),
        out_shape=jax.ShapeDtypeStruct((D_MODEL, MY_WIDTH), F32),
        compiler_params=_params(("arbitrary",)),
    )(h16, *pieces, d_f)


def _adamw(w, g, m, v):
    m = ADAM_B1 * m + (1.0 - ADAM_B1) * g
    v = ADAM_B2 * v + (1.0 - ADAM_B2) * (g * g)
    m_hat = m / (1.0 - ADAM_B1 ** ADAM_STEP)
    v_hat = v / (1.0 - ADAM_B2 ** ADAM_STEP)
    return -ADAM_LR * (m_hat / (jnp.sqrt(v_hat) + ADAM_EPS) + ADAM_WD * w), m, v


def _update(red, sred, big, small):
    def body(red_ref, sred_ref, wi, mi, vi, wk, mk, vk, wo, mo, vo, ws, ms, vs, *outs):
        groups = [
            (lambda: red_ref[0:KV_ROW0, 0:IN_CHUNK], wi, mi, vi, outs[0:4], None),
            (lambda: red_ref[KV_ROW0:OUT_ROW0, :], wk, mk, vk, outs[4:8], None),
            (lambda: red_ref[OUT_ROW0:OUT_ROW0 + 128, :], wo, mo, vo, outs[8:12], slice(0, PACK_W)),
            (lambda: red_ref[OUT_ROW0 + 128:PACK_ROWS, :], wo, mo, vo, outs[8:12], slice(PACK_W, 2 * PACK_W)),
            (lambda: sred_ref[...], ws, ms, vs, outs[12:16], None),
        ]
        for grad, w_ref, m_ref, v_ref, (g_out, d_out, m_out, v_out), cols in groups:
            g = grad()
            if cols is None:
                delta, m_new, v_new = _adamw(w_ref[...], g, m_ref[...], v_ref[...])
                g_out[...], d_out[...], m_out[...], v_out[...] = g, delta, m_new, v_new
            else:
                delta, m_new, v_new = _adamw(w_ref[:, cols], g, m_ref[:, cols], v_ref[:, cols])
                g_out[:, cols], d_out[:, cols], m_out[:, cols], v_out[:, cols] = g, delta, m_new, v_new

    shapes = [big[0][0].shape] * 4 + [big[1][0].shape] * 4 + [big[2][0].shape] * 4 + [small[0].shape] * 4
    flat = [red, sred, *big[0], *big[1], *big[2], *small]
    return pl.pallas_call(
        body,
        name="adamw_update",
        out_shape=tuple(jax.ShapeDtypeStruct(s, F32) for s in shapes),
        in_specs=[pl.BlockSpec(memory_space=pltpu.VMEM)] * len(flat),
        out_specs=(pl.BlockSpec(memory_space=pltpu.VMEM),) * len(shapes),
        compiler_params=pltpu.CompilerParams(vmem_limit_bytes=VMEM_LIMIT),
    )(*flat)


SMALL_LAYOUT = (("norm_g", 8), ("mem_norm_g", 8), ("w_pool", 128), ("pool_scale", 2), ("b_f", 1),
                ("fox_q_g", 1), ("fox_k_g", 1), ("mem_q_g", 1), ("mem_k_g", 1))


def _pack_small(parts):
    rows = []
    for name, n_rows in SMALL_LAYOUT:
        flat = parts[name].reshape(-1)
        rows.append(jnp.pad(flat, (0, n_rows * 128 - flat.shape[0])).reshape(n_rows, 128))
    used = sum(n for _, n in SMALL_LAYOUT)
    rows.append(jnp.zeros((SMALL_ROWS - used, 128), F32))
    return jnp.concatenate(rows, axis=0)


def _unpack_small(packed, like):
    out, r0 = {}, 0
    for name, n_rows in SMALL_LAYOUT:
        size = like[name].size
        out[name] = packed[r0:r0 + n_rows].reshape(-1)[:size].reshape(like[name].shape)
        r0 += n_rows
    return out


def _to_my_columns(w):
    pad = jnp.zeros(w.shape[:-1] + (MY_WIDTH - IN_WIDTH,), w.dtype)
    return jnp.concatenate([w[..., :F_OFF], w[..., F_OFF + FOX_HEADS:], w[..., F_OFF:F_OFF + FOX_HEADS], pad], axis=-1)


def _from_my_columns(w):
    return jnp.concatenate([w[..., :F_OFF], w[..., MY_F_OFF:MY_F_OFF + FOX_HEADS], w[..., F_OFF:MY_F_OFF]], axis=-1)


def kernel(x, mem, norm_g, w_in, b_f, w_pool, pool_scale, fox_q_g, fox_k_g, mem_norm_g, w_mem_kv, mem_q_g, mem_k_g, w_out, loss_target, m_norm_g, m_w_in, m_b_f, m_w_pool, m_pool_scale, m_fox_q_g, m_fox_k_g, m_mem_norm_g, m_w_mem_kv, m_mem_q_g, m_mem_k_g, m_w_out, v_norm_g, v_w_in, v_b_f, v_w_pool, v_pool_scale, v_fox_q_g, v_fox_k_g, v_mem_norm_g, v_w_mem_kv, v_mem_q_g, v_mem_k_g, v_w_out):
    small_w = dict(norm_g=norm_g, mem_norm_g=mem_norm_g, w_pool=w_pool, pool_scale=pool_scale, b_f=b_f,
                   fox_q_g=fox_q_g, fox_k_g=fox_k_g, mem_q_g=mem_q_g, mem_k_g=mem_k_g)
    small_m = dict(norm_g=m_norm_g, mem_norm_g=m_mem_norm_g, w_pool=m_w_pool, pool_scale=m_pool_scale, b_f=m_b_f,
                   fox_q_g=m_fox_q_g, fox_k_g=m_fox_k_g, mem_q_g=m_mem_q_g, mem_k_g=m_mem_k_g)
    small_v = dict(norm_g=v_norm_g, mem_norm_g=v_mem_norm_g, w_pool=v_w_pool, pool_scale=v_pool_scale, b_f=v_b_f,
                   fox_q_g=v_fox_q_g, fox_k_g=v_fox_k_g, mem_q_g=v_mem_q_g, mem_k_g=v_mem_k_g)
    x2, mem2, target2 = x[0], mem[0], loss_target[0]

    packed = _gather_weights(w_in[0], w_mem_kv[0], w_out[0])
    w_in_all = packed[:, :KV_ROW0, :IN_CHUNK].transpose(1, 0, 2).reshape(D_MODEL, IN_WIDTH)
    w_my = _to_my_columns(w_in_all)
    w_kv_all = packed[:, KV_ROW0:OUT_ROW0, :].reshape(D_MODEL, 512)
    w_out_all = jnp.concatenate([packed[:, OUT_ROW0:OUT_ROW0 + 128, :], packed[:, OUT_ROW0 + 128:, :]], axis=2).reshape(D_MODEL, D_MODEL)
    wp_bd = jnp.zeros((POOL_WIDTH, POOL_WIDTH), F32)
    for g in range(4):
        wp_bd = wp_bd.at[64 * g:64 * (g + 1), 64 * g:64 * (g + 1)].set(w_pool[0, g])
    wp_bd = wp_bd.astype(BF16)
    bf_pad = jnp.pad(b_f, ((0, 0), (0, 128 - FOX_HEADS)))

    proj, h16 = _fwd_proj(x2, norm_g, w_my)
    q_aug, k_aug, v_h, kt_aug, vt_aug = _fox_prep(proj, bf_pad, fox_q_g, fox_k_g)
    km, vm = _mem_prep(mem2, mem_norm_g, w_kv_all, mem_k_g)
    olse, lse_rows = _fox_fwd(q_aug, k_aug, vt_aug)
    mixed = _mix_fwd(proj, olse, km, vm, wp_bd, pool_scale, mem_q_g)
    d_out, d_mixed, dw_out, loss_blk = _out_loss(mixed, x2, target2, w_out_all)
    loss = lax.psum(loss_blk[0, 0], ("x", "y", "c"))

    d_ua, d_gb, d_qm, d_o, delta_rows, dwp_bd, d_scale, dkm, dvm, d_gqm = _mix_bwd(proj, olse, d_mixed, km, vm, wp_bd, pool_scale, mem_q_g)
    dq_aug, dk_aug, dv_h = _fox_bwd(q_aug, k_aug, kt_aug, v_h, d_o, lse_rows, delta_rows)
    d_q, d_k, d_v, d_f, d_gq, d_gk, d_bf = _fox_post(proj, bf_pad, fox_q_g, fox_k_g, dq_aug, dk_aug, dv_h)
    dw_kv, d_mem_g, d_gkm = _mem_bwd(mem2, mem_norm_g, w_kv_all, mem_k_g, dkm, dvm)
    pieces = (d_ua, d_q, d_k, d_v, d_gb, d_qm)
    grad_x, d_norm_g = _grad_x(pieces, d_f, w_my, x2, norm_g, d_out)
    dw_my = _grad_w_in(h16, pieces, d_f)

    dw_in_all = _from_my_columns(dw_my).reshape(D_MODEL, N_DEV, IN_CHUNK).transpose(1, 0, 2)
    dw_in_all = jnp.pad(dw_in_all, ((0, 0), (0, 0), (0, PACK_W - IN_CHUNK)))
    dw_out_all = dw_out.reshape(N_DEV, 128, D_MODEL)
    big = jnp.concatenate([dw_in_all, dw_kv.reshape(N_DEV, 128, 512), dw_out_all[:, :, :PACK_W], dw_out_all[:, :, PACK_W:]],
                          axis=1).astype(BF16)
    d_wpool = jnp.stack([dwp_bd[64 * g:64 * (g + 1), 64 * g:64 * (g + 1)] for g in range(4)])
    small_g = _pack_small(dict(norm_g=d_norm_g, mem_norm_g=d_mem_g, w_pool=d_wpool, pool_scale=d_scale, b_f=d_bf[:, :FOX_HEADS],
                               fox_q_g=d_gq, fox_k_g=d_gk, mem_q_g=d_gqm, mem_k_g=d_gkm))
    red, sred = _exchange_grads(big, small_g)

    outs = _update(red, sred, ((w_in[0], m_w_in[0], v_w_in[0]), (w_mem_kv[0], m_w_mem_kv[0], v_w_mem_kv[0]),
                               (w_out[0], m_w_out[0], v_w_out[0])),
                   (_pack_small(small_w), _pack_small(small_m), _pack_small(small_v)))
    big_out = {"w_in": outs[0:4], "w_mem_kv": outs[4:8], "w_out": outs[8:12]}
    small_out = [_unpack_small(o, small_w) for o in outs[12:16]]
    order = ["norm_g", "w_in", "b_f", "w_pool", "pool_scale", "fox_q_g", "fox_k_g", "mem_norm_g", "w_mem_kv", "mem_q_g", "mem_k_g", "w_out"]
    result = [loss, grad_x[None]]
    for kind in range(4):
        for name in order:
            result.append(big_out[name][kind][None] if name in big_out else small_out[kind][name])
    return tuple(result)
```

```python
import functools

import jax
import jax.numpy as jnp
from jax import lax
from jax.experimental import pallas as pl
from jax.experimental.pallas import tpu as pltpu

F32 = jnp.float32
BF16 = jnp.bfloat16
MESH = pl.DeviceIdType.MESH

N_DEV = 8
D_MODEL = 1024
HEAD_DIM = 64
FOX_HEADS = 8
MEM_HEADS = 4
N_MEM = 256
POOL_WIDTH = 256
EPS = 1e-6
IN_WIDTH = 3080
IN_CHUNK = IN_WIDTH // N_DEV
F_OFF = 2048
MY_WIDTH = 3200
MY_F_OFF = 3072
PIECE = 512
TM = 256
TA = 256
HB = 4
HALO = 16
VMEM_LIMIT = 56 * 1024 * 1024

ADAM_LR = 0.001
ADAM_B1 = 0.9
ADAM_B2 = 0.999
ADAM_EPS = 1e-08
ADAM_WD = 0.01
ADAM_STEP = 10


def _dot(a, b):
    return jnp.dot(a, b, preferred_element_type=F32)


def _dot_nt(a, b):
    return lax.dot_general(a, b, (((1,), (1,)), ((), ())), preferred_element_type=F32)


def _dot_tn(a, b):
    return lax.dot_general(a, b, (((0,), (0,)), ((), ())), preferred_element_type=F32)


def _sigmoid(g):
    return 1.0 / (1.0 + jnp.exp(-g))


def _split3(v):
    hi = v.astype(BF16)
    r1 = v - hi.astype(F32)
    mid = r1.astype(BF16)
    lo = (r1 - mid.astype(F32)).astype(BF16)
    return hi, mid, lo


def _rms(v):
    return lax.rsqrt(jnp.mean(v * v, axis=-1, keepdims=True) + EPS)


def _rms_bwd(a, r, g, dy):
    da = dy * g
    return r * (da - a * jnp.mean(da * a, axis=-1, keepdims=True)), dy * a


def _params(sem, limit=VMEM_LIMIT):
    return pltpu.CompilerParams(dimension_semantics=sem, vmem_limit_bytes=limit)


def _full(shape):
    return pl.BlockSpec(shape, lambda *_: (0,) * len(shape))


def _chunk_segments(d):
    runs = []
    for lo, hi, shift in ((0, F_OFF, 0), (F_OFF, F_OFF + FOX_HEADS, MY_F_OFF - F_OFF), (F_OFF + FOX_HEADS, IN_WIDTH, -FOX_HEADS)):
        a, b = max(lo, IN_CHUNK * d), min(hi, IN_CHUNK * (d + 1))
        if a < b:
            runs.append((a + shift, b - a))
    return runs


def _my_place():
    x, y, c = lax.axis_index("x"), lax.axis_index("y"), lax.axis_index("c")
    return x, y, c, 4 * x + 2 * y + c


def _shard_rows(dev):
    return pl.ds(pl.multiple_of(128 * dev, 128), 128)


def _gather_weights(w_in, w_kv, w_out):
    n_fam = 3

    def body(win_ref, wkv_ref, wout_ref, wmy_ref, kv_all, out_all, in_all, pay_in, pay_kv, pay_out, send_sems, recv_sems):
        x, y, c, me = _my_place()
        sibling = (x, y, 1 - c)
        chips = [(1 - x, y), (x, 1 - y), (1 - x, 1 - y)]

        pay_in[...] = win_ref[...].astype(BF16)
        pay_kv[...] = wkv_ref[...].astype(BF16)
        pay_out[...] = wout_ref[...].astype(BF16)
        pays = (pay_in, pay_kv, pay_out)

        def place(fam, px, py, pc):
            dev = 4 * px + 2 * py + pc
            return (in_all.at[dev], kv_all.at[_shard_rows(dev)], out_all.at[_shard_rows(dev)])[fam]

        def copy(fam, k, block, to, own=False):
            return pltpu.make_async_remote_copy(
                src_ref=pays[fam] if own else place(fam, *block),
                dst_ref=place(fam, *block),
                send_sem=send_sems.at[7 * fam + k],
                recv_sem=recv_sems.at[7 * fam + k],
                device_id=to,
                device_id_type=MESH,
            )

        fams = range(n_fam)
        first = [copy(f, 0, (x, y, c), sibling, own=True) for f in fams]
        first += [copy(f, 1 + j, (x, y, c), (*chip, c), own=True) for j, chip in enumerate(chips) for f in fams]
        for cp in first:
            cp.start()
        in_all[me] = pay_in[...]
        kv_all[_shard_rows(me), :] = pay_kv[...]
        out_all[_shard_rows(me), :] = pay_out[...]
        passed = []
        for j, chip in enumerate(chips):
            for f in fams:
                copy(f, 1 + j, (*chip, c), (x, y, c)).wait_recv()
                passed.append(copy(f, 4 + j, (*chip, c), sibling))
                passed[-1].start()
        for f in fams:
            copy(f, 0, sibling, (x, y, c)).wait_recv()
        for j, chip in enumerate(chips):
            for f in fams:
                copy(f, 4 + j, (*chip, 1 - c), (x, y, c)).wait_recv()
        for cp in first + passed:
            cp.wait_send()

        for r0 in range(0, D_MODEL, 256):
            ch = [in_all[d, r0:r0 + 256, :].astype(F32) for d in range(N_DEV)]
            lo = F_OFF - 5 * IN_CHUNK
            full = jnp.concatenate(ch[:5] + [ch[5][:, :lo], ch[5][:, lo + FOX_HEADS:], ch[6], ch[7], ch[5][:, lo:lo + FOX_HEADS],
                                             jnp.zeros((256, MY_WIDTH - IN_WIDTH), F32)], axis=1)
            wmy_ref[r0:r0 + 256, :] = full.astype(BF16)

    return pl.pallas_call(
        body,
        name="gather_weights",
        out_shape=(jax.ShapeDtypeStruct((D_MODEL, MY_WIDTH), BF16), jax.ShapeDtypeStruct((D_MODEL, 512), BF16),
                   jax.ShapeDtypeStruct((D_MODEL, D_MODEL), BF16)),
        in_specs=[pl.BlockSpec(memory_space=pltpu.VMEM)] * 3,
        out_specs=(pl.BlockSpec(memory_space=pltpu.VMEM),) * 3,
        scratch_shapes=[
            pltpu.VMEM((N_DEV, D_MODEL, IN_CHUNK), BF16),
            pltpu.VMEM((D_MODEL, IN_CHUNK), BF16),
            pltpu.VMEM((128, 512), BF16),
            pltpu.VMEM((128, D_MODEL), BF16),
            pltpu.SemaphoreType.DMA((7 * n_fam,)),
            pltpu.SemaphoreType.DMA((7 * n_fam,)),
        ],
        compiler_params=pltpu.CompilerParams(vmem_limit_bytes=VMEM_LIMIT),
    )(w_in, w_kv, w_out)


SB_ROWS, SB_W = 80, 256
SB_NORM_G, SB_MEM_NORM_G, SB_POOL_SCALE, SB_B_F, SB_FOX_Q, SB_FOX_K, SB_MEM_Q, SB_MEM_K, SB_LOSS, SB_W_POOL = 0, 4, 8, 9, 10, 11, 12, 13, 14, 16


def _exchange_grads(dw_in, dw_kv, dw_out, d_norm_g, d_mem_g, d_scale, d_bf, d_gq, d_gk, d_gqm, d_gkm, dwp_bd, loss_blk):
    def body(in_ref, kv_ref, out_ref, dng, dmg, dsc, dbf, dgq, dgk, dgqm, dgkm, dwp, loss_ref,
             rin_ref, rkv_ref, rout_ref, sred_ref, land_in, land_kv, land_out, sb_ref, sland, send_sems, recv_sems):
        x, y, c, me = _my_place()

        sb_ref[...] = jnp.zeros((SB_ROWS, SB_W), F32)
        for k in range(4):
            sb_ref[SB_NORM_G + k:SB_NORM_G + k + 1, :] = dng[:, SB_W * k:SB_W * (k + 1)]
            sb_ref[SB_MEM_NORM_G + k:SB_MEM_NORM_G + k + 1, :] = dmg[:, SB_W * k:SB_W * (k + 1)]
        sb_ref[SB_POOL_SCALE:SB_POOL_SCALE + 1, :] = dsc[...]
        sb_ref[SB_B_F:SB_B_F + 1, 0:128] = dbf[...]
        for row, ref in ((SB_FOX_Q, dgq), (SB_FOX_K, dgk), (SB_MEM_Q, dgqm), (SB_MEM_K, dgkm)):
            sb_ref[row:row + 1, 0:HEAD_DIM] = ref[...]
        sb_ref[SB_LOSS:SB_LOSS + 1, 0:128] = loss_ref[0:1, :]
        for g in range(4):
            sl = slice(64 * g, 64 * (g + 1))
            sb_ref[SB_W_POOL:SB_W_POOL + 64, sl] = dwp[sl, sl]

        copies = []
        for k in range(1, N_DEV):
            px, py, pc = x ^ (k >> 2), y ^ ((k >> 1) & 1), c ^ (k & 1)
            peer = 4 * px + 2 * py + pc
            pairs = ((in_ref.at[peer], land_in.at[me]), (kv_ref.at[_shard_rows(peer)], land_kv.at[me]),
                     (out_ref.at[_shard_rows(peer)], land_out.at[me]), (sb_ref, sland.at[me]))
            for fam, (src, dst) in enumerate(pairs):
                copies.append(pltpu.make_async_remote_copy(
                    src_ref=src, dst_ref=dst, send_sem=send_sems.at[7 * fam + k - 1], recv_sem=recv_sems.at[7 * fam + k - 1],
                    device_id=(px, py, pc), device_id_type=MESH))
        for cp in copies:
            cp.start()
        land_in[me] = in_ref[me]
        land_kv[me] = kv_ref[_shard_rows(me), :]
        land_out[me] = out_ref[_shard_rows(me), :]
        sland[me] = sb_ref[...]
        for cp in copies:
            cp.wait_recv()
        for cp in copies:
            cp.wait_send()
        for land, red in ((land_in, rin_ref), (land_kv, rkv_ref), (land_out, rout_ref), (sland, sred_ref)):
            acc = land[0].astype(F32)
            for d in range(1, N_DEV):
                acc = acc + land[d].astype(F32)
            red[...] = acc

    args = (dw_in, dw_kv, dw_out, d_norm_g, d_mem_g, d_scale, d_bf, d_gq, d_gk, d_gqm, d_gkm, dwp_bd, loss_blk)
    return pl.pallas_call(
        body,
        name="exchange_grads",
        out_shape=(jax.ShapeDtypeStruct((D_MODEL, IN_CHUNK), F32), jax.ShapeDtypeStruct((128, 512), F32),
                   jax.ShapeDtypeStruct((128, D_MODEL), F32), jax.ShapeDtypeStruct((SB_ROWS, SB_W), F32)),
        in_specs=[pl.BlockSpec(memory_space=pltpu.VMEM)] * len(args),
        out_specs=(pl.BlockSpec(memory_space=pltpu.VMEM),) * 4,
        scratch_shapes=[
            pltpu.VMEM((N_DEV, D_MODEL, IN_CHUNK), BF16),
            pltpu.VMEM((N_DEV, 128, 512), BF16),
            pltpu.VMEM((N_DEV, 128, D_MODEL), BF16),
            pltpu.VMEM((SB_ROWS, SB_W), F32),
            pltpu.VMEM((N_DEV, SB_ROWS, SB_W), F32),
            pltpu.SemaphoreType.DMA((28,)),
            pltpu.SemaphoreType.DMA((28,)),
        ],
        compiler_params=pltpu.CompilerParams(vmem_limit_bytes=VMEM_LIMIT),
    )(*args)


def _fwd_proj(x, norm_g, w_my):
    s_len = x.shape[0]

    def body(x_ref, g_ref, w_ref, proj_ref, h_ref):
        xv = x_ref[...]
        h = (xv * _rms(xv) * g_ref[...]).astype(BF16)
        h_ref[...] = h
        proj_ref[...] = _dot(h, w_ref[...])

    return pl.pallas_call(
        body,
        name="fwd_proj",
        grid=(s_len // TM,),
        in_specs=[pl.BlockSpec((TM, D_MODEL), lambda i: (i, 0)), _full((1, D_MODEL)), _full((D_MODEL, MY_WIDTH))],
        out_specs=[pl.BlockSpec((TM, MY_WIDTH), lambda i: (i, 0)), pl.BlockSpec((TM, D_MODEL), lambda i: (i, 0))],
        out_shape=[jax.ShapeDtypeStruct((s_len, MY_WIDTH), F32), jax.ShapeDtypeStruct((s_len, D_MODEL), BF16)],
        compiler_params=_params(("parallel",)),
    )(x, norm_g, w_my)


def _log_sigmoid(z):
    return jnp.minimum(z, 0.0) - jnp.log(1.0 + jnp.exp(-jnp.abs(z)))


def _fox_prep(proj, bf_pad, gq, gk):
    s_len = proj.shape[0]

    def body(q_ref, k_ref, v_ref, f_ref, bf_ref, gq_ref, gk_ref, qa_ref, ka_ref, vh_ref, kt_ref, vt_ref, carry_ref):
        @pl.when(pl.program_id(0) == 0)
        def _():
            carry_ref[...] = jnp.zeros((1, 128), F32)

        lane = lax.broadcasted_iota(jnp.int32, (TM, 128), 1)
        logf = jnp.where(lane < FOX_HEADS, _log_sigmoid(f_ref[...] + bf_ref[...]), 0.0)
        row = lax.broadcasted_iota(jnp.int32, (TM, TM), 0)
        col = lax.broadcasted_iota(jnp.int32, (TM, TM), 1)
        tri = jnp.where(col <= row, 1.0, 0.0).astype(BF16)
        hi, mid, lo = _split3(logf)
        cum = _dot(tri, hi) + _dot(tri, mid) + _dot(tri, lo) + carry_ref[...]
        carry_ref[...] = cum[TM - 1:TM, :]
        f_hi, f_mid, f_lo = [t.astype(F32) for t in _split3(cum)]
        zeros = jnp.zeros((TM, HEAD_DIM), F32)
        for h in range(FOX_HEADS):
            sl = slice(HEAD_DIM * h, HEAD_DIM * (h + 1))
            qh = q_ref[:, sl]
            kh = k_ref[:, sl]
            qn = qh * _rms(qh) * gq_ref[...] * 0.125
            kn = kh * _rms(kh) * gk_ref[...]
            a, b, c = f_hi[:, h:h + 1], f_mid[:, h:h + 1], f_lo[:, h:h + 1]
            qa = jnp.where(lane < 64, jnp.concatenate([qn, zeros], axis=1),
                           jnp.where(lane == 64, a, jnp.where(lane == 65, b, jnp.where(lane == 66, c,
                                                                                         jnp.where(lane < 70, 1.0, 0.0)))))
            ka = jnp.where(lane < 64, jnp.concatenate([kn, zeros], axis=1),
                           jnp.where(lane < 67, 1.0, jnp.where(lane == 67, -a, jnp.where(lane == 68, -b,
                                                                                          jnp.where(lane == 69, -c, 0.0)))))
            vh = v_ref[:, sl]
            v_aug = jnp.where(lane < 64, jnp.concatenate([vh, zeros], axis=1), jnp.where(lane == 64, 1.0, 0.0))
            qa_ref[h] = qa.astype(BF16)
            ka_ref[h] = ka.astype(BF16)
            vh_ref[h] = vh.astype(BF16)
            kt_ref[h, 0] = ka.T.astype(BF16)
            vt_ref[h, 0] = v_aug.T.astype(BF16)

    col_blk = lambda j: pl.BlockSpec((TM, PIECE), lambda i: (i, j))
    head_blk = lambda w: pl.BlockSpec((FOX_HEADS, TM, w), lambda i: (0, i, 0))
    tile_blk = pl.BlockSpec((FOX_HEADS, 1, 128, TM), lambda i: (0, i, 0, 0))
    tiled = jax.ShapeDtypeStruct((FOX_HEADS, s_len // TM, 128, TM), BF16)
    return pl.pallas_call(
        body,
        name="fox_prep",
        grid=(s_len // TM,),
        in_specs=[col_blk(1), col_blk(2), col_blk(3), pl.BlockSpec((TM, 128), lambda i: (i, MY_F_OFF // 128)),
                  _full((1, 128)), _full((1, HEAD_DIM)), _full((1, HEAD_DIM))],
        out_specs=[head_blk(128), head_blk(128), head_blk(HEAD_DIM), tile_blk, tile_blk],
        out_shape=[jax.ShapeDtypeStruct((FOX_HEADS, s_len, 128), BF16), jax.ShapeDtypeStruct((FOX_HEADS, s_len, 128), BF16),
                   jax.ShapeDtypeStruct((FOX_HEADS, s_len, HEAD_DIM), BF16), tiled, tiled],
        scratch_shapes=[pltpu.VMEM((1, 128), F32)],
        compiler_params=_params(("arbitrary",)),
    )(proj, proj, proj, proj, bf_pad, gq, gk)


def _mem_prep(mem, g, w_kv, gk, w_pool, b_f):
    def body(mem_ref, g_ref, w_ref, gk_ref, wp_ref, bf_ref, km_ref, vm_ref, wpbd_ref, bfpad_ref):
        m = mem_ref[...]
        kv = _dot((m * _rms(m) * g_ref[...]).astype(BF16), w_ref[...])
        for h in range(MEM_HEADS):
            sl = slice(HEAD_DIM * h, HEAD_DIM * (h + 1))
            kh = kv[:, sl]
            km_ref[:, sl] = (kh * _rms(kh) * gk_ref[...]).astype(BF16)
        vm_ref[...] = kv[:, 256:512].astype(BF16)
        wpbd_ref[...] = jnp.zeros((POOL_WIDTH, POOL_WIDTH), BF16)
        for grp in range(4):
            sl = slice(64 * grp, 64 * (grp + 1))
            wpbd_ref[sl, sl] = wp_ref[grp].astype(BF16)
        bfpad_ref[...] = jnp.zeros((1, 128), F32)
        bfpad_ref[:, 0:FOX_HEADS] = bf_ref[...]

    return pl.pallas_call(
        body,
        name="mem_prep",
        out_shape=(jax.ShapeDtypeStruct((N_MEM, 256), BF16), jax.ShapeDtypeStruct((N_MEM, 256), BF16),
                   jax.ShapeDtypeStruct((POOL_WIDTH, POOL_WIDTH), BF16), jax.ShapeDtypeStruct((1, 128), F32)),
        in_specs=[pl.BlockSpec(memory_space=pltpu.VMEM)] * 6,
        out_specs=(pl.BlockSpec(memory_space=pltpu.VMEM),) * 4,
        compiler_params=pltpu.CompilerParams(vmem_limit_bytes=VMEM_LIMIT),
    )(mem, g, w_kv, gk, w_pool, b_f)


def _causal_mask():
    row = lax.broadcasted_iota(jnp.int32, (TA, TA), 0)
    col = lax.broadcasted_iota(jnp.int32, (TA, TA), 1)
    return col <= row


def _causal_mask_t():
    row = lax.broadcasted_iota(jnp.int32, (TA, TA), 0)
    col = lax.broadcasted_iota(jnp.int32, (TA, TA), 1)
    return row <= col


def _fox_fwd(q_aug, k_aug, vt_aug):
    s_len = q_aug.shape[1]
    n_tiles = s_len // TA

    def body(q_ref, k_ref, vt_ref, o_ref, st_ref):
        qi = pl.program_id(1)
        qs = [q_ref[h] for h in range(HB)]

        def step(j, carry, masked):
            rows = pl.ds(pl.multiple_of(j * TA, TA), TA)
            scores = [_dot_nt(k_ref[h, rows, :], qs[h]) for h in range(HB)]
            soft = []
            for h in range(HB):
                m, _ = carry[h]
                s = jnp.where(_causal_mask_t(), scores[h], -1e30) if masked else scores[h]
                m_new = jnp.maximum(m, jnp.max(s, axis=0, keepdims=True))
                soft.append((m_new, jnp.exp(m - m_new), jnp.exp(s - m_new).astype(BF16)))
            return tuple((m_new, alpha * carry[h][1] + _dot(vt_ref[h, j], p)) for h, (m_new, alpha, p) in enumerate(soft))

        init = tuple((jnp.full((1, TA), -1e30, F32), jnp.zeros((128, TA), F32)) for _ in range(HB))
        carry = lax.fori_loop(0, qi, lambda j, cr: step(j, cr, False), init)
        carry = step(qi, carry, True)
        for h in range(HB):
            m, acc = carry[h]
            l = acc[HEAD_DIM:HEAD_DIM + 1, :]
            lse = m + jnp.log(l)
            o_ref[h] = jnp.concatenate([acc[0:HEAD_DIM] / l, jnp.broadcast_to(lse, (HEAD_DIM, TA))], axis=0).T
            st_ref[h, 0] = jnp.broadcast_to(lse, (8, TA))

    return pl.pallas_call(
        body,
        name="fox_fwd",
        grid=(FOX_HEADS // HB, n_tiles),
        in_specs=[pl.BlockSpec((HB, TA, 128), lambda g, i: (g, i, 0)), pl.BlockSpec((HB, s_len, 128), lambda g, i: (g, 0, 0)),
                  pl.BlockSpec((HB, n_tiles, 128, TA), lambda g, i: (g, 0, 0, 0))],
        out_specs=[pl.BlockSpec((HB, TA, 128), lambda g, i: (g, i, 0)), pl.BlockSpec((HB, 1, 8, TA), lambda g, i: (g, i, 0, 0))],
        out_shape=[jax.ShapeDtypeStruct((FOX_HEADS, s_len, 128), F32), jax.ShapeDtypeStruct((FOX_HEADS, n_tiles, 8, TA), F32)],
        compiler_params=_params(("parallel", "parallel")),
    )(q_aug, k_aug, vt_aug)


def _lane_group(lane, a, b, c, d):
    return jnp.where(lane < 64, a, jnp.where(lane < 128, b, jnp.where(lane < 192, c, d)))


def _pool_count(row0):
    lane = lax.broadcasted_iota(jnp.int32, (TM, POOL_WIDTH), 1)
    t1 = row0 + lax.broadcasted_iota(jnp.int32, (TM, POOL_WIDTH), 0) + 1
    return jnp.minimum(t1, _lane_group(lane, 2, 4, 8, 16)).astype(F32), lane


def _pool_delta(u, halo, cnt, lane):
    xe = jnp.concatenate([halo, u], axis=0)
    s2 = xe + pltpu.roll(xe, 1, 0)
    s4 = s2 + pltpu.roll(s2, 2, 0)
    s8 = s4 + pltpu.roll(s4, 4, 0)
    s16 = s8 + pltpu.roll(s8, 8, 0)
    win = _lane_group(lane, s2[HALO:], s4[HALO:], s8[HALO:], s16[HALO:])
    return win / cnt - u


def _pool_delta_bwd(dd, dd_next, cnt, cnt_next, lane):
    ee = jnp.concatenate([dd / cnt, dd_next / cnt_next], axis=0)
    n = TM + HALO
    r2 = ee + pltpu.roll(ee, n - 1, 0)
    r4 = r2 + pltpu.roll(r2, n - 2, 0)
    r8 = r4 + pltpu.roll(r4, n - 4, 0)
    r16 = r8 + pltpu.roll(r8, n - 8, 0)
    return _lane_group(lane, r2[:TM], r4[:TM], r8[:TM], r16[:TM]) - dd


def _mem_attn(qm, gq, km_ref, vm_ref):
    heads = []
    for h in range(MEM_HEADS):
        sl = slice(HEAD_DIM * h, HEAD_DIM * (h + 1))
        qh = qm[:, sl]
        r = _rms(qh)
        a = qh * r
        q16 = (a * gq * 0.125).astype(BF16)
        s = _dot_nt(q16, km_ref[:, sl])
        e = jnp.exp(s - jnp.max(s, axis=-1, keepdims=True))
        p = e / jnp.sum(e, axis=-1, keepdims=True)
        heads.append((a, r, q16, p, _dot(p.astype(BF16), vm_ref[:, sl])))
    return heads


def _row_specs(s_len):
    n_halo = s_len // HALO
    piece = lambda j: pl.BlockSpec((TM, PIECE), lambda i: (i, j))
    before = lambda j: pl.BlockSpec((HALO, 256), lambda i: (jnp.maximum(i * (TM // HALO) - 1, 0), j))
    after = lambda j: pl.BlockSpec((HALO, 256), lambda i: (jnp.minimum((i + 1) * (TM // HALO), n_halo - 1), j))
    return piece, before, after


def _mix_fwd(proj, olse, km, vm, wp_bd, pool_scale, gq_m):
    s_len = proj.shape[0]

    def body(ua_ref, halo_ref, gb_ref, qm_ref, ol_ref, km_ref, vm_ref, wp_ref, sc_ref, gq_ref, out_ref):
        i = pl.program_id(0)
        u = ua_ref[:, 0:256]
        g_a = ua_ref[:, 256:512]
        halo = jnp.where(i > 0, halo_ref[...], 0.0)
        cnt, lane = _pool_count(i * TM)
        d = _pool_delta(u, halo, cnt, lane).astype(BF16)
        y_a = _dot(d, wp_ref[...]) * sc_ref[...]
        out_ref[:, 0:256] = (y_a * (g_a * _sigmoid(g_a))).astype(BF16)
        for h in range(FOX_HEADS):
            sl = slice(HEAD_DIM * h, HEAD_DIM * (h + 1))
            g = gb_ref[:, sl]
            out_ref[:, 256 + HEAD_DIM * h:256 + HEAD_DIM * (h + 1)] = (ol_ref[h][:, 0:HEAD_DIM] * (g * _sigmoid(g))).astype(BF16)
        heads = _mem_attn(qm_ref[:, 0:256], gq_ref[...], km_ref, vm_ref)
        for h in range(MEM_HEADS):
            g = qm_ref[:, 256 + HEAD_DIM * h:256 + HEAD_DIM * (h + 1)]
            out_ref[:, 768 + HEAD_DIM * h:768 + HEAD_DIM * (h + 1)] = (heads[h][4] * (g * _sigmoid(g))).astype(BF16)

    piece, before, _ = _row_specs(s_len)
    return pl.pallas_call(
        body,
        name="mix_fwd",
        grid=(s_len // TM,),
        in_specs=[piece(0), before(0), piece(4), piece(5), pl.BlockSpec((FOX_HEADS, TM, 128), lambda i: (0, i, 0)),
                  _full((N_MEM, 256)), _full((N_MEM, 256)), _full((256, 256)), _full((1, 256)), _full((1, HEAD_DIM))],
        out_specs=pl.BlockSpec((TM, D_MODEL), lambda i: (i, 0)),
        out_shape=jax.ShapeDtypeStruct((s_len, D_MODEL), BF16),
        compiler_params=_params(("parallel",)),
    )(proj, proj, proj, proj, olse, km, vm, wp_bd, pool_scale, gq_m)


def _out_loss(mixed, x, target, w_out):
    s_len = x.shape[0]

    def body(mix_ref, x_ref, t_ref, w_ref, dout_ref, dmix_ref, dw16_ref, loss_ref, dw_ref):
        @pl.when(pl.program_id(0) == 0)
        def _():
            dw_ref[...] = jnp.zeros((D_MODEL, D_MODEL), F32)
            loss_ref[...] = jnp.zeros((8, 128), F32)

        mixed16 = mix_ref[...]
        err = x_ref[...] + _dot(mixed16, w_ref[...]) - t_ref[...]
        loss_ref[...] += 0.5 * jnp.sum(jnp.mean(err * err, axis=-1, keepdims=True))
        d_out = err / float(D_MODEL)
        dout_ref[...] = d_out
        d16 = d_out.astype(BF16)
        dmix_ref[...] = _dot_nt(d16, w_ref[...])
        dw_ref[...] += _dot_tn(mixed16, d16)

        @pl.when(pl.program_id(0) == pl.num_programs(0) - 1)
        def _():
            dw16_ref[...] = dw_ref[...].astype(BF16)

    row = pl.BlockSpec((TM, D_MODEL), lambda i: (i, 0))
    return pl.pallas_call(
        body,
        name="out_loss",
        grid=(s_len // TM,),
        in_specs=[row, row, row, _full((D_MODEL, D_MODEL))],
        out_specs=[row, row, _full((D_MODEL, D_MODEL)), _full((8, 128))],
        out_shape=[jax.ShapeDtypeStruct((s_len, D_MODEL), F32), jax.ShapeDtypeStruct((s_len, D_MODEL), F32),
                   jax.ShapeDtypeStruct((D_MODEL, D_MODEL), BF16), jax.ShapeDtypeStruct((8, 128), F32)],
        scratch_shapes=[pltpu.VMEM((D_MODEL, D_MODEL), F32)],
        compiler_params=_params(("arbitrary",)),
    )(mixed, x, target, w_out)


def _silu_pair(g):
    s = _sigmoid(g)
    return g * s, s * (1.0 + g * (1.0 - s))


def _mix_bwd(proj, olse, d_mixed, km, vm, wp_bd, pool_scale, gq_m):
    s_len = proj.shape[0]
    n_tiles = s_len // TM

    def body(ua_ref, halo_ref, ga_next_ref, gb_ref, qm_ref, ol_ref, dm_ref, dm_next_ref, km_ref, vm_ref, wp_ref, sc_ref, gq_ref,
             dua_ref, dgb_ref, dqm_ref, do_ref, dl_ref, dwp_ref, dsc_ref, dkm_ref, dvm_ref, dgq_ref):
        i = pl.program_id(0)

        @pl.when(i == 0)
        def _():
            dwp_ref[...] = jnp.zeros((256, 256), F32)
            dsc_ref[...] = jnp.zeros((1, 256), F32)
            dkm_ref[...] = jnp.zeros((N_MEM, 256), F32)
            dvm_ref[...] = jnp.zeros((N_MEM, 256), F32)
            dgq_ref[...] = jnp.zeros((1, HEAD_DIM), F32)

        u = ua_ref[:, 0:256]
        g_a = ua_ref[:, 256:512]
        halo = jnp.where(i > 0, halo_ref[...], 0.0)
        cnt, lane = _pool_count(i * TM)
        d16 = _pool_delta(u, halo, cnt, lane).astype(BF16)
        t = _dot(d16, wp_ref[...])
        y_a = t * sc_ref[...]
        silu_a, dsilu_a = _silu_pair(g_a)
        dm_a = dm_ref[:, 0:256]
        dy_a = dm_a * silu_a
        dsc_ref[...] += jnp.sum(dy_a * t, axis=0, keepdims=True)
        dt16 = (dy_a * sc_ref[...]).astype(BF16)
        dwp_ref[...] += _dot_tn(d16, dt16)
        dd = _dot_nt(dt16, wp_ref[...])
        g_next = ga_next_ref[...]
        dt_next = (dm_next_ref[...] * (g_next * _sigmoid(g_next)) * sc_ref[...]).astype(BF16)
        dd_next = jnp.where(i < n_tiles - 1, _dot_nt(dt_next, wp_ref[...]), 0.0)
        cnt_next = _pool_count((i + 1) * TM)[0][0:HALO]
        dua_ref[:, 0:256] = _pool_delta_bwd(dd, dd_next, cnt, cnt_next, lane).astype(BF16)
        dua_ref[:, 256:512] = (dm_a * y_a * dsilu_a).astype(BF16)

        ones = jnp.ones((8, HEAD_DIM), BF16)
        for h in range(FOX_HEADS):
            sl = slice(HEAD_DIM * h, HEAD_DIM * (h + 1))
            silu_b, dsilu_b = _silu_pair(gb_ref[:, sl])
            dm_b = dm_ref[:, 256 + HEAD_DIM * h:256 + HEAD_DIM * (h + 1)]
            o_h = ol_ref[h][:, 0:HEAD_DIM]
            d_o = dm_b * silu_b
            do_ref[h] = d_o.astype(BF16)
            dgb_ref[:, sl] = (dm_b * o_h * dsilu_b).astype(BF16)
            prod = d_o * o_h
            hi = prod.astype(BF16)
            lo = (prod - hi.astype(F32)).astype(BF16)
            dl_ref[h, 0] = _dot_nt(ones, hi) + _dot_nt(ones, lo)

        heads = _mem_attn(qm_ref[:, 0:256], gq_ref[...], km_ref, vm_ref)
        dgq = jnp.zeros((TM, HEAD_DIM), F32)
        for h in range(MEM_HEADS):
            sl = slice(HEAD_DIM * h, HEAD_DIM * (h + 1))
            a, r, q16, p, y_m = heads[h]
            silu_m, dsilu_m = _silu_pair(qm_ref[:, 256 + HEAD_DIM * h:256 + HEAD_DIM * (h + 1)])
            dm_m = dm_ref[:, 768 + HEAD_DIM * h:768 + HEAD_DIM * (h + 1)]
            dqm_ref[:, 256 + HEAD_DIM * h:256 + HEAD_DIM * (h + 1)] = (dm_m * y_m * dsilu_m).astype(BF16)
            dy16 = (dm_m * silu_m).astype(BF16)
            dp = _dot_nt(dy16, vm_ref[:, sl])
            dvm_ref[:, sl] += _dot_tn(p.astype(BF16), dy16)
            ds16 = (p * (dp - jnp.sum(dp * p, axis=-1, keepdims=True))).astype(BF16)
            dkm_ref[:, sl] += _dot_tn(ds16, q16)
            dq, dg_rows = _rms_bwd(a, r, gq_ref[...], _dot(ds16, km_ref[:, sl]) * 0.125)
            dqm_ref[:, sl] = dq.astype(BF16)
            dgq = dgq + dg_rows
        dgq_ref[...] += jnp.sum(dgq, axis=0, keepdims=True)

    piece, before, after = _row_specs(s_len)
    n_halo = s_len // HALO
    row = pl.BlockSpec((TM, D_MODEL), lambda i: (i, 0))
    dm_after = pl.BlockSpec((HALO, 256), lambda i: (jnp.minimum((i + 1) * (TM // HALO), n_halo - 1), 0))
    out_piece = pl.BlockSpec((TM, PIECE), lambda i: (i, 0))
    return pl.pallas_call(
        body,
        name="mix_bwd",
        grid=(n_tiles,),
        in_specs=[piece(0), before(0), after(1), piece(4), piece(5), pl.BlockSpec((FOX_HEADS, TM, 128), lambda i: (0, i, 0)),
                  row, dm_after, _full((N_MEM, 256)), _full((N_MEM, 256)), _full((256, 256)), _full((1, 256)), _full((1, HEAD_DIM))],
        out_specs=[out_piece, out_piece, out_piece, pl.BlockSpec((FOX_HEADS, TM, HEAD_DIM), lambda i: (0, i, 0)),
                   pl.BlockSpec((FOX_HEADS, 1, 8, TM), lambda i: (0, i, 0, 0)),
                   _full((256, 256)), _full((1, 256)), _full((N_MEM, 256)), _full((N_MEM, 256)), _full((1, HEAD_DIM))],
        out_shape=[jax.ShapeDtypeStruct((s_len, PIECE), BF16)] * 3 + [
            jax.ShapeDtypeStruct((FOX_HEADS, s_len, HEAD_DIM), BF16),
            jax.ShapeDtypeStruct((FOX_HEADS, n_tiles, 8, TM), F32),
            jax.ShapeDtypeStruct((256, 256), F32), jax.ShapeDtypeStruct((1, 256), F32),
            jax.ShapeDtypeStruct((N_MEM, 256), F32), jax.ShapeDtypeStruct((N_MEM, 256), F32),
            jax.ShapeDtypeStruct((1, HEAD_DIM), F32)],
        compiler_params=_params(("arbitrary",)),
    )(proj, proj, proj, proj, proj, olse, d_mixed, d_mixed, km, vm, wp_bd, pool_scale, gq_m)


def _fox_bwd(q_aug, k_aug, kt_aug, v_h, d_o, lse_rows, delta_rows):
    s_len = q_aug.shape[1]
    n_tiles = s_len // TA

    def body(q_ref, k_ref, kt_ref, v_ref, do_ref, st_ref, dl_ref, dqt_ref, dk_ref, dv_ref):
        j = pl.program_id(1)

        @pl.when(j == 0)
        def _():
            dqt_ref[...] = jnp.zeros((HB, n_tiles, 128, TA), F32)

        ks = [k_ref[h] for h in range(HB)]
        kts = [kt_ref[h, 0] for h in range(HB)]
        vs = [v_ref[h] for h in range(HB)]

        def pair(i, acc, masked):
            rows = pl.ds(pl.multiple_of(i * TA, TA), TA)
            qs = [q_ref[h, rows, :] for h in range(HB)]
            d_os = [do_ref[h, rows, :] for h in range(HB)]
            scores = [(_dot_nt(ks[h], qs[h]), _dot_nt(vs[h], d_os[h])) for h in range(HB)]
            probs = []
            for h in range(HB):
                s, dp = scores[h]
                if masked:
                    s = jnp.where(_causal_mask_t(), s, -1e30)
                p = jnp.exp(s - st_ref[h, i, 0:1, :])
                probs.append((p.astype(BF16), (p * (dp - dl_ref[h, i, 0:1, :])).astype(BF16)))
            out = []
            for h in range(HB):
                p16, ds16 = probs[h]
                dqt_ref[h, i] += _dot(kts[h], ds16)
                out.append((acc[h][0] + _dot(ds16, qs[h]), acc[h][1] + _dot(p16, d_os[h])))
            return tuple(out)

        zero = tuple((jnp.zeros((TA, 128), F32), jnp.zeros((TA, HEAD_DIM), F32)) for _ in range(HB))
        acc = pair(j, zero, True)
        acc = lax.fori_loop(j + 1, n_tiles, lambda i, a: pair(i, a, False), acc)
        for h in range(HB):
            dk_ref[h] = acc[h][0]
            dv_ref[h] = acc[h][1]

    whole = lambda w: pl.BlockSpec((HB, s_len, w), lambda g, j: (g, 0, 0))
    tile = lambda w: pl.BlockSpec((HB, TA, w), lambda g, j: (g, j, 0))
    rows_blk = lambda r: pl.BlockSpec((HB, n_tiles, r, TA), lambda g, j: (g, 0, 0, 0))
    return pl.pallas_call(
        body,
        name="fox_bwd",
        grid=(FOX_HEADS // HB, n_tiles),
        in_specs=[whole(128), tile(128), pl.BlockSpec((HB, 1, 128, TA), lambda g, j: (g, j, 0, 0)), tile(HEAD_DIM),
                  whole(HEAD_DIM), rows_blk(8), rows_blk(8)],
        out_specs=[rows_blk(128), tile(128), tile(HEAD_DIM)],
        out_shape=[jax.ShapeDtypeStruct((FOX_HEADS, n_tiles, 128, TA), F32), jax.ShapeDtypeStruct((FOX_HEADS, s_len, 128), F32),
                   jax.ShapeDtypeStruct((FOX_HEADS, s_len, HEAD_DIM), F32)],
        compiler_params=_params(("parallel", "arbitrary")),
    )(q_aug, k_aug, kt_aug, v_h, d_o, lse_rows, delta_rows)


def _fox_post(proj, bf_pad, gq, gk, dq_aug, dk_aug, dv_h):
    s_len = proj.shape[0]
    n_tiles = s_len // TM

    def body(q_ref, k_ref, f_ref, bf_ref, gq_ref, gk_ref, dqa_ref, dka_ref, dvh_ref,
             dq_ref, dk_ref, dv_ref, df_ref, dgq_ref, dgk_ref, dbf_ref, carry_ref):
        @pl.when(pl.program_id(0) == 0)
        def _():
            carry_ref[...] = jnp.zeros((1, 128), F32)
            dgq_ref[...] = jnp.zeros((1, HEAD_DIM), F32)
            dgk_ref[...] = jnp.zeros((1, HEAD_DIM), F32)
            dbf_ref[...] = jnp.zeros((1, 128), F32)

        lane = lax.broadcasted_iota(jnp.int32, (TM, 128), 1)
        d_cum = jnp.zeros((TM, 128), F32)
        dgq = jnp.zeros((TM, HEAD_DIM), F32)
        dgk = jnp.zeros((TM, HEAD_DIM), F32)
        for h in range(FOX_HEADS):
            sl = slice(HEAD_DIM * h, HEAD_DIM * (h + 1))
            dqa = dqa_ref[h, 0].T
            dka = dka_ref[h]
            qh = q_ref[:, sl]
            kh = k_ref[:, sl]
            rq = _rms(qh)
            rk = _rms(kh)
            dq, dgq_rows = _rms_bwd(qh * rq, rq, gq_ref[...], dqa[:, 0:HEAD_DIM] * 0.125)
            dk, dgk_rows = _rms_bwd(kh * rk, rk, gk_ref[...], dka[:, 0:HEAD_DIM])
            dq_ref[:, sl] = dq.astype(BF16)
            dk_ref[:, sl] = dk.astype(BF16)
            dv_ref[:, sl] = dvh_ref[h].astype(BF16)
            dgq = dgq + dgq_rows
            dgk = dgk + dgk_rows
            d_cum = jnp.where(lane == h, dqa[:, 64:65] - dka[:, 67:68], d_cum)
        dgq_ref[...] += jnp.sum(dgq, axis=0, keepdims=True)
        dgk_ref[...] += jnp.sum(dgk, axis=0, keepdims=True)

        row = lax.broadcasted_iota(jnp.int32, (TM, TM), 0)
        col = lax.broadcasted_iota(jnp.int32, (TM, TM), 1)
        tri = jnp.where(col >= row, 1.0, 0.0).astype(BF16)
        hi, mid, lo = _split3(d_cum)
        d_logf = _dot(tri, hi) + _dot(tri, mid) + _dot(tri, lo) + carry_ref[...]
        carry_ref[...] = d_logf[0:1, :]
        z = f_ref[...] + bf_ref[...]
        d_f = jnp.where(lane < FOX_HEADS, d_logf * _sigmoid(-z), 0.0)
        df_ref[...] = d_f.astype(BF16)
        dbf_ref[...] += jnp.sum(d_f, axis=0, keepdims=True)

    rev = lambda i: n_tiles - 1 - i
    col_blk = lambda j: pl.BlockSpec((TM, PIECE), lambda i: (rev(i), j))
    head_blk = lambda w: pl.BlockSpec((FOX_HEADS, TM, w), lambda i: (0, rev(i), 0))
    out_piece = pl.BlockSpec((TM, PIECE), lambda i: (rev(i), 0))
    return pl.pallas_call(
        body,
        name="fox_post",
        grid=(n_tiles,),
        in_specs=[col_blk(1), col_blk(2), pl.BlockSpec((TM, 128), lambda i: (rev(i), MY_F_OFF // 128)),
                  _full((1, 128)), _full((1, HEAD_DIM)), _full((1, HEAD_DIM)),
                  pl.BlockSpec((FOX_HEADS, 1, 128, TM), lambda i: (0, rev(i), 0, 0)), head_blk(128), head_blk(HEAD_DIM)],
        out_specs=[out_piece, out_piece, out_piece, pl.BlockSpec((TM, 128), lambda i: (rev(i), 0)),
                   _full((1, HEAD_DIM)), _full((1, HEAD_DIM)), _full((1, 128))],
        out_shape=[jax.ShapeDtypeStruct((s_len, PIECE), BF16)] * 3 + [
            jax.ShapeDtypeStruct((s_len, 128), BF16), jax.ShapeDtypeStruct((1, HEAD_DIM), F32),
            jax.ShapeDtypeStruct((1, HEAD_DIM), F32), jax.ShapeDtypeStruct((1, 128), F32)],
        scratch_shapes=[pltpu.VMEM((1, 128), F32)],
        compiler_params=_params(("arbitrary",)),
    )(proj, proj, proj, bf_pad, gq, gk, dq_aug, dk_aug, dv_h)


def _mem_bwd(mem, g, w_kv, gk, dkm, dvm):
    def body(mem_ref, g_ref, w_ref, gk_ref, dkm_ref, dvm_ref, dw_ref, dg_ref, dgk_ref, dkv_ref):
        m = mem_ref[...]
        r = _rms(m)
        a = m * r
        mn16 = (a * g_ref[...]).astype(BF16)
        kv = _dot(mn16, w_ref[...])
        dgk = jnp.zeros((N_MEM, HEAD_DIM), F32)
        for h in range(MEM_HEADS):
            sl = slice(HEAD_DIM * h, HEAD_DIM * (h + 1))
            kh = kv[:, sl]
            rk = _rms(kh)
            dk, dg_rows = _rms_bwd(kh * rk, rk, gk_ref[...], dkm_ref[:, sl])
            dkv_ref[:, sl] = dk.astype(BF16)
            dgk = dgk + dg_rows
        dkv_ref[:, 256:512] = dvm_ref[...].astype(BF16)
        dgk_ref[...] = jnp.sum(dgk, axis=0, keepdims=True)
        dkv16 = dkv_ref[...]
        dw_ref[...] = _dot_tn(mn16, dkv16).astype(BF16)
        dg_ref[...] = jnp.sum(_dot_nt(dkv16, w_ref[...]) * a, axis=0, keepdims=True)

    return pl.pallas_call(
        body,
        name="mem_bwd",
        out_shape=(jax.ShapeDtypeStruct((D_MODEL, 512), BF16), jax.ShapeDtypeStruct((1, D_MODEL), F32),
                   jax.ShapeDtypeStruct((1, HEAD_DIM), F32)),
        in_specs=[pl.BlockSpec(memory_space=pltpu.VMEM)] * 6,
        out_specs=(pl.BlockSpec(memory_space=pltpu.VMEM),) * 3,
        scratch_shapes=[pltpu.VMEM((N_MEM, 512), BF16)],
        compiler_params=pltpu.CompilerParams(vmem_limit_bytes=VMEM_LIMIT),
    )(mem, g, w_kv, gk, dkm, dvm)


def _grad_x(pieces, d_f, w_my, x, norm_g, d_out):
    s_len = x.shape[0]

    def body(p0, p1, p2, p3, p4, p5, df_ref, w_ref, x_ref, g_ref, dout_ref, gx_ref, dg_ref):
        @pl.when(pl.program_id(0) == 0)
        def _():
            dg_ref[...] = jnp.zeros((1, D_MODEL), F32)

        dh = _dot_nt(df_ref[...], w_ref[:, MY_F_OFF:MY_WIDTH])
        for j, p in enumerate((p0, p1, p2, p3, p4, p5)):
            dh = dh + _dot_nt(p[...], w_ref[:, PIECE * j:PIECE * (j + 1)])
        xv = x_ref[...]
        r = _rms(xv)
        dx, dg_rows = _rms_bwd(xv * r, r, g_ref[...], dh)
        gx_ref[...] = dout_ref[...] + dx
        dg_ref[...] += jnp.sum(dg_rows, axis=0, keepdims=True)

    piece = pl.BlockSpec((TM, PIECE), lambda i: (i, 0))
    row = pl.BlockSpec((TM, D_MODEL), lambda i: (i, 0))
    return pl.pallas_call(
        body,
        name="grad_x",
        grid=(s_len // TM,),
        in_specs=[piece] * 6 + [pl.BlockSpec((TM, 128), lambda i: (i, 0)), _full((D_MODEL, MY_WIDTH)), row,
                                _full((1, D_MODEL)), row],
        out_specs=[row, _full((1, D_MODEL))],
        out_shape=[jax.ShapeDtypeStruct((s_len, D_MODEL), F32), jax.ShapeDtypeStruct((1, D_MODEL), F32)],
        compiler_params=_params(("arbitrary",)),
    )(*pieces, d_f, w_my, x, norm_g, d_out)


def _grad_w_in(h16, pieces, d_f):
    s_len = h16.shape[0]
    tk = 512

    def body(h_ref, p0, p1, p2, p3, p4, p5, df_ref, out_ref, dw_ref):
        @pl.when(pl.program_id(0) == 0)
        def _():
            dw_ref[...] = jnp.zeros((D_MODEL, MY_WIDTH), F32)

        h = h_ref[...]
        for j, p in enumerate((p0, p1, p2, p3, p4, p5)):
            dw_ref[:, PIECE * j:PIECE * (j + 1)] += _dot_tn(h, p[...])
        dw_ref[:, MY_F_OFF:MY_WIDTH] += _dot_tn(h, df_ref[...])

        @pl.when(pl.program_id(0) == pl.num_programs(0) - 1)
        def _():
            for d in range(N_DEV):
                parts = [dw_ref[:, start:start + width] for start, width in _chunk_segments(d)]
                out_ref[d] = (parts[0] if len(parts) == 1 else jnp.concatenate(parts, axis=1)).astype(BF16)

    piece = pl.BlockSpec((tk, PIECE), lambda i: (i, 0))
    return pl.pallas_call(
        body,
        name="grad_w_in",
        grid=(s_len // tk,),
        in_specs=[pl.BlockSpec((tk, D_MODEL), lambda i: (i, 0))] + [piece] * 6 + [pl.BlockSpec((tk, 128), lambda i: (i, 0))],
        out_specs=_full((N_DEV, D_MODEL, IN_CHUNK)),
        out_shape=jax.ShapeDtypeStruct((N_DEV, D_MODEL, IN_CHUNK), BF16),
        scratch_shapes=[pltpu.VMEM((D_MODEL, MY_WIDTH), F32)],
        compiler_params=_params(("arbitrary",)),
    )(h16, *pieces, d_f)


def _adamw(w, g, m, v):
    m = ADAM_B1 * m + (1.0 - ADAM_B1) * g
    v = ADAM_B2 * v + (1.0 - ADAM_B2) * (g * g)
    m_hat = m / (1.0 - ADAM_B1 ** ADAM_STEP)
    v_hat = v / (1.0 - ADAM_B2 ** ADAM_STEP)
    return -ADAM_LR * (m_hat / (jnp.sqrt(v_hat) + ADAM_EPS) + ADAM_WD * w), m, v


PARAM_ORDER = ("norm_g", "w_in", "b_f", "w_pool", "pool_scale", "fox_q_g", "fox_k_g", "mem_norm_g", "w_mem_kv", "mem_q_g", "mem_k_g", "w_out")


def _update(red_in, red_kv, red_out, sred, params):
    def body(rin_ref, rkv_ref, rout_ref, sred_ref, *refs):
        ins, outs = refs[:3 * len(PARAM_ORDER)], refs[3 * len(PARAM_ORDER):]
        wide = lambda row: jnp.concatenate([sred_ref[row + k:row + k + 1, :] for k in range(4)], axis=1)
        head = lambda row: sred_ref[row:row + 1, 0:HEAD_DIM]
        grads = {
            "norm_g": lambda: wide(SB_NORM_G), "w_in": lambda: rin_ref[...], "b_f": lambda: sred_ref[SB_B_F:SB_B_F + 1, 0:FOX_HEADS],
            "pool_scale": lambda: sred_ref[SB_POOL_SCALE:SB_POOL_SCALE + 1, :], "fox_q_g": lambda: head(SB_FOX_Q),
            "fox_k_g": lambda: head(SB_FOX_K), "mem_norm_g": lambda: wide(SB_MEM_NORM_G), "w_mem_kv": lambda: rkv_ref[...],
            "mem_q_g": lambda: head(SB_MEM_Q), "mem_k_g": lambda: head(SB_MEM_K), "w_out": lambda: rout_ref[...],
        }
        for p, name in enumerate(PARAM_ORDER):
            w_ref, m_ref, v_ref = ins[3 * p:3 * p + 3]
            g_out, d_out, m_out, v_out = outs[4 * p:4 * p + 4]
            if name == "w_pool":
                for grp in range(4):
                    g = sred_ref[SB_W_POOL:SB_W_POOL + 64, 64 * grp:64 * (grp + 1)]
                    delta, m_new, v_new = _adamw(w_ref[grp], g, m_ref[grp], v_ref[grp])
                    g_out[grp], d_out[grp], m_out[grp], v_out[grp] = g, delta, m_new, v_new
            else:
                g = grads[name]()
                delta, m_new, v_new = _adamw(w_ref[...], g, m_ref[...], v_ref[...])
                g_out[...], d_out[...], m_out[...], v_out[...] = g, delta, m_new, v_new

    flat = [t for name in PARAM_ORDER for t in params[name]]
    shapes = [params[name][0].shape for name in PARAM_ORDER for _ in range(4)]
    outs = pl.pallas_call(
        body,
        name="adamw_update",
        out_shape=tuple(jax.ShapeDtypeStruct(s, F32) for s in shapes),
        in_specs=[pl.BlockSpec(memory_space=pltpu.VMEM)] * (4 + len(flat)),
        out_specs=(pl.BlockSpec(memory_space=pltpu.VMEM),) * len(shapes),
        compiler_params=pltpu.CompilerParams(vmem_limit_bytes=VMEM_LIMIT),
    )(red_in, red_kv, red_out, sred, *flat)
    return {name: outs[4 * p:4 * p + 4] for p, name in enumerate(PARAM_ORDER)}


def kernel(x, mem, norm_g, w_in, b_f, w_pool, pool_scale, fox_q_g, fox_k_g, mem_norm_g, w_mem_kv, mem_q_g, mem_k_g, w_out, loss_target, m_norm_g, m_w_in, m_b_f, m_w_pool, m_pool_scale, m_fox_q_g, m_fox_k_g, m_mem_norm_g, m_w_mem_kv, m_mem_q_g, m_mem_k_g, m_w_out, v_norm_g, v_w_in, v_b_f, v_w_pool, v_pool_scale, v_fox_q_g, v_fox_k_g, v_mem_norm_g, v_w_mem_kv, v_mem_q_g, v_mem_k_g, v_w_out):
    given = dict(norm_g=(norm_g, m_norm_g, v_norm_g), w_in=(w_in, m_w_in, v_w_in), b_f=(b_f, m_b_f, v_b_f),
                 w_pool=(w_pool, m_w_pool, v_w_pool), pool_scale=(pool_scale, m_pool_scale, v_pool_scale),
                 fox_q_g=(fox_q_g, m_fox_q_g, v_fox_q_g), fox_k_g=(fox_k_g, m_fox_k_g, v_fox_k_g),
                 mem_norm_g=(mem_norm_g, m_mem_norm_g, v_mem_norm_g), w_mem_kv=(w_mem_kv, m_w_mem_kv, v_w_mem_kv),
                 mem_q_g=(mem_q_g, m_mem_q_g, v_mem_q_g), mem_k_g=(mem_k_g, m_mem_k_g, v_mem_k_g), w_out=(w_out, m_w_out, v_w_out))
    params = {name: tuple(t[0] if t.ndim > 2 else t for t in wmv) for name, wmv in given.items()}
    x2, mem2, target2 = x[0], mem[0], loss_target[0]

    w_my, w_kv_all, w_out_all = _gather_weights(params["w_in"][0], params["w_mem_kv"][0], params["w_out"][0])
    proj, h16 = _fwd_proj(x2, norm_g, w_my)
    km, vm, wp_bd, bf_pad = _mem_prep(mem2, mem_norm_g, w_kv_all, mem_k_g, params["w_pool"][0], b_f)
    q_aug, k_aug, v_h, kt_aug, vt_aug = _fox_prep(proj, bf_pad, fox_q_g, fox_k_g)
    olse, lse_rows = _fox_fwd(q_aug, k_aug, vt_aug)
    mixed = _mix_fwd(proj, olse, km, vm, wp_bd, pool_scale, mem_q_g)
    d_out, d_mixed, dw_out, loss_blk = _out_loss(mixed, x2, target2, w_out_all)

    d_ua, d_gb, d_qm, d_o, delta_rows, dwp_bd, d_scale, dkm, dvm, d_gqm = _mix_bwd(proj, olse, d_mixed, km, vm, wp_bd, pool_scale, mem_q_g)
    dq_aug, dk_aug, dv_h = _fox_bwd(q_aug, k_aug, kt_aug, v_h, d_o, lse_rows, delta_rows)
    d_q, d_k, d_v, d_f, d_gq, d_gk, d_bf = _fox_post(proj, bf_pad, fox_q_g, fox_k_g, dq_aug, dk_aug, dv_h)
    dw_kv, d_mem_g, d_gkm = _mem_bwd(mem2, mem_norm_g, w_kv_all, mem_k_g, dkm, dvm)
    pieces = (d_ua, d_q, d_k, d_v, d_gb, d_qm)
    grad_x, d_norm_g = _grad_x(pieces, d_f, w_my, x2, norm_g, d_out)
    dw_in = _grad_w_in(h16, pieces, d_f)

    red_in, red_kv, red_out, sred = _exchange_grads(dw_in, dw_kv, dw_out, d_norm_g, d_mem_g, d_scale, d_bf, d_gq, d_gk, d_gqm,
                                                    d_gkm, dwp_bd, loss_blk)
    new = _update(red_in, red_kv, red_out, sred, params)
    result = [sred[SB_LOSS, 0], grad_x[None]]
    for kind in range(4):
        for name in PARAM_ORDER:
            out = new[name][kind]
            result.append(out[None] if given[name][0].ndim > 2 else out)
    return tuple(result)
```

```python
import functools

import jax
import jax.numpy as jnp
from jax import lax
from jax.experimental import pallas as pl
from jax.experimental.pallas import tpu as pltpu

F32 = jnp.float32
BF16 = jnp.bfloat16
MESH = pl.DeviceIdType.MESH

N_DEV = 8
D_MODEL = 1024
HEAD_DIM = 64
FOX_HEADS = 8
MEM_HEADS = 4
N_MEM = 256
POOL_WIDTH = 256
EPS = 1e-6
IN_WIDTH = 3080
IN_CHUNK = IN_WIDTH // N_DEV
F_OFF = 2048
MY_WIDTH = 3200
MY_F_OFF = 3072
PIECE = 512
TM = 256
TA = 256
HB = 4
HALO = 16
VMEM_LIMIT = 56 * 1024 * 1024

ADAM_LR = 0.001
ADAM_B1 = 0.9
ADAM_B2 = 0.999
ADAM_EPS = 1e-08
ADAM_WD = 0.01
ADAM_STEP = 10


def _dot(a, b):
    return jnp.dot(a, b, preferred_element_type=F32)


def _dot_nt(a, b):
    return lax.dot_general(a, b, (((1,), (1,)), ((), ())), preferred_element_type=F32)


def _dot_tn(a, b):
    return lax.dot_general(a, b, (((0,), (0,)), ((), ())), preferred_element_type=F32)


def _sigmoid(g):
    return 1.0 / (1.0 + jnp.exp(-g))


def _split3(v):
    hi = v.astype(BF16)
    r1 = v - hi.astype(F32)
    mid = r1.astype(BF16)
    lo = (r1 - mid.astype(F32)).astype(BF16)
    return hi, mid, lo


def _rms(v):
    return lax.rsqrt(jnp.mean(v * v, axis=-1, keepdims=True) + EPS)


def _rms_bwd(a, r, g, dy):
    da = dy * g
    return r * (da - a * jnp.mean(da * a, axis=-1, keepdims=True)), dy * a


def _params(sem, limit=VMEM_LIMIT):
    return pltpu.CompilerParams(dimension_semantics=sem, vmem_limit_bytes=limit)


def _full(shape):
    return pl.BlockSpec(shape, lambda *_: (0,) * len(shape))


def _chunk_segments(d):
    runs = []
    for lo, hi, shift in ((0, F_OFF, 0), (F_OFF, F_OFF + FOX_HEADS, MY_F_OFF - F_OFF), (F_OFF + FOX_HEADS, IN_WIDTH, -FOX_HEADS)):
        a, b = max(lo, IN_CHUNK * d), min(hi, IN_CHUNK * (d + 1))
        if a < b:
            runs.append((a + shift, b - a))
    return runs


def _my_place():
    x, y, c = lax.axis_index("x"), lax.axis_index("y"), lax.axis_index("c")
    return x, y, c, 4 * x + 2 * y + c


def _shard_rows(dev):
    return pl.ds(pl.multiple_of(128 * dev, 128), 128)


def _gather_w_in(w_in, w_kv, w_out, w_pool, b_f):
    def body(win_ref, wkv_ref, wout_ref, wp_ref, bf_ref, wmy_ref, kv16_ref, out16_ref, wpbd_ref, bfpad_ref,
             in_all, pay_in, send_sems, recv_sems):
        x, y, c, me = _my_place()
        sibling = (x, y, 1 - c)
        chips = [(1 - x, y), (x, 1 - y), (1 - x, 1 - y)]

        pay_in[...] = win_ref[...].astype(BF16)

        def copy(k, block, to, own=False):
            px, py, pc = block
            dst = in_all.at[4 * px + 2 * py + pc]
            return pltpu.make_async_remote_copy(src_ref=pay_in if own else dst, dst_ref=dst, send_sem=send_sems.at[k],
                                                recv_sem=recv_sems.at[k], device_id=to, device_id_type=MESH)

        first = [copy(0, (x, y, c), sibling, own=True)]
        first += [copy(1 + j, (x, y, c), (*chip, c), own=True) for j, chip in enumerate(chips)]
        for cp in first:
            cp.start()
        in_all[me] = pay_in[...]
        kv16_ref[...] = wkv_ref[...].astype(BF16)
        out16_ref[...] = wout_ref[...].astype(BF16)
        wpbd_ref[...] = jnp.zeros((POOL_WIDTH, POOL_WIDTH), BF16)
        for grp in range(4):
            sl = slice(64 * grp, 64 * (grp + 1))
            wpbd_ref[sl, sl] = wp_ref[grp].astype(BF16)
        bfpad_ref[...] = jnp.zeros((1, 128), F32)
        bfpad_ref[:, 0:FOX_HEADS] = bf_ref[...]
        passed = []
        for j, chip in enumerate(chips):
            copy(1 + j, (*chip, c), (x, y, c)).wait_recv()
            passed.append(copy(4 + j, (*chip, c), sibling))
            passed[-1].start()
        copy(0, sibling, (x, y, c)).wait_recv()
        for j, chip in enumerate(chips):
            copy(4 + j, (*chip, 1 - c), (x, y, c)).wait_recv()
        for cp in first + passed:
            cp.wait_send()

        for r0 in range(0, D_MODEL, 256):
            ch = [in_all[d, r0:r0 + 256, :].astype(F32) for d in range(N_DEV)]
            lo = F_OFF - 5 * IN_CHUNK
            full = jnp.concatenate(ch[:5] + [ch[5][:, :lo], ch[5][:, lo + FOX_HEADS:], ch[6], ch[7], ch[5][:, lo:lo + FOX_HEADS],
                                             jnp.zeros((256, MY_WIDTH - IN_WIDTH), F32)], axis=1)
            wmy_ref[r0:r0 + 256, :] = full.astype(BF16)

    return pl.pallas_call(
        body,
        name="gather_w_in",
        out_shape=(jax.ShapeDtypeStruct((D_MODEL, MY_WIDTH), BF16), jax.ShapeDtypeStruct((128, 512), BF16),
                   jax.ShapeDtypeStruct((128, D_MODEL), BF16), jax.ShapeDtypeStruct((POOL_WIDTH, POOL_WIDTH), BF16),
                   jax.ShapeDtypeStruct((1, 128), F32)),
        in_specs=[pl.BlockSpec(memory_space=pltpu.VMEM)] * 5,
        out_specs=(pl.BlockSpec(memory_space=pltpu.VMEM),) * 5,
        scratch_shapes=[
            pltpu.VMEM((N_DEV, D_MODEL, IN_CHUNK), BF16),
            pltpu.VMEM((D_MODEL, IN_CHUNK), BF16),
            pltpu.SemaphoreType.DMA((7,)),
            pltpu.SemaphoreType.DMA((7,)),
        ],
        compiler_params=pltpu.CompilerParams(vmem_limit_bytes=VMEM_LIMIT),
    )(w_in, w_kv, w_out, w_pool, b_f)


def _row_block_exchange(kv_ref, out_ref, kv_dst, out_dst, send_sems, recv_sems, local_sems, gather):
    x, y, c, me = _my_place()
    remote = []
    for k in range(1, N_DEV):
        px, py, pc = x ^ (k >> 2), y ^ ((k >> 1) & 1), c ^ (k & 1)
        peer = 4 * px + 2 * py + pc
        for fam, (src, dst) in enumerate(((kv_ref, kv_dst), (out_ref, out_dst))):
            remote.append(pltpu.make_async_remote_copy(
                src_ref=src if gather else src.at[_shard_rows(peer)], dst_ref=dst.at[_shard_rows(me)] if gather else dst.at[me],
                send_sem=send_sems.at[7 * fam + k - 1], recv_sem=recv_sems.at[7 * fam + k - 1],
                device_id=(px, py, pc), device_id_type=MESH))
    local = [pltpu.make_async_copy(src if gather else src.at[_shard_rows(me)], dst.at[_shard_rows(me)] if gather else dst.at[me],
                                   local_sems.at[fam]) for fam, (src, dst) in enumerate(((kv_ref, kv_dst), (out_ref, out_dst)))]
    return remote, local


SB_ROWS, SB_W = 80, 256
SB_NORM_G, SB_MEM_NORM_G, SB_POOL_SCALE, SB_B_F, SB_FOX_Q, SB_FOX_K, SB_MEM_Q, SB_MEM_K, SB_LOSS, SB_W_POOL = 0, 4, 8, 9, 10, 11, 12, 13, 14, 16


def _exchange_grads(dw_in, land_kv, land_out, d_norm_g, d_mem_g, d_scale, d_bf, d_gq, d_gk, d_gqm, d_gkm, dwp_bd, loss_blk):
    half = D_MODEL // 2

    def body(in_ref, lkv_ref, lout_ref, dng, dmg, dsc, dbf, dgq, dgk, dgqm, dgkm, dwp, loss_ref,
             rin_ref, rkv_ref, rout_ref, sred_ref, land1, send2a, send2b, keep2a, keep2b, land2a, land2b,
             send3a, send3b, land3a, land3b, sb_ref, sland, send_sems, recv_sems):
        x, y, c, me = _my_place()
        sibling, x_nbr, y_nbr = (x, y, 1 - c), (1 - x, y, c), (x, 1 - y, c)

        def rcopy(src, dst, sem, to):
            return pltpu.make_async_remote_copy(src_ref=src, dst_ref=dst, send_sem=send_sems.at[sem], recv_sem=recv_sems.at[sem],
                                                device_id=to, device_id_type=MESH)

        sb_ref[...] = jnp.zeros((SB_ROWS, SB_W), F32)
        for k in range(4):
            sb_ref[SB_NORM_G + k:SB_NORM_G + k + 1, :] = dng[:, SB_W * k:SB_W * (k + 1)]
            sb_ref[SB_MEM_NORM_G + k:SB_MEM_NORM_G + k + 1, :] = dmg[:, SB_W * k:SB_W * (k + 1)]
        sb_ref[SB_POOL_SCALE:SB_POOL_SCALE + 1, :] = dsc[...]
        sb_ref[SB_B_F:SB_B_F + 1, 0:128] = dbf[...]
        for row, ref in ((SB_FOX_Q, dgq), (SB_FOX_K, dgk), (SB_MEM_Q, dgqm), (SB_MEM_K, dgkm)):
            sb_ref[row:row + 1, 0:HEAD_DIM] = ref[...]
        sb_ref[SB_LOSS:SB_LOSS + 1, 0:128] = loss_ref[0:1, :]
        for g in range(4):
            sl = slice(64 * g, 64 * (g + 1))
            sb_ref[SB_W_POOL:SB_W_POOL + 64, sl] = dwp[sl, sl]

        small = []
        for k in range(1, N_DEV):
            px, py, pc = x ^ (k >> 2), y ^ ((k >> 1) & 1), c ^ (k & 1)
            small.append(rcopy(sb_ref, sland.at[me], k - 1, (px, py, pc)))
        stage1 = [rcopy(in_ref.at[2 * k + 1 - c], land1.at[k], 7 + k, sibling) for k in range(4)]
        for cp in small + stage1:
            cp.start()
        sland[me] = sb_ref[...]
        for cp in stage1:
            cp.wait_recv()

        top, bot = slice(0, half), slice(half, D_MODEL)

        def chip_sum(k, rows):
            return in_ref[2 * k + c, rows, :].astype(F32) + land1[k, rows, :].astype(F32)

        for j in range(2):
            send2a[j] = chip_sum(2 * (1 - x) + j, top).astype(BF16)
            keep2a[j] = chip_sum(2 * x + j, top)
            send2b[j] = chip_sum(2 * j + 1 - y, bot).astype(BF16)
            keep2b[j] = chip_sum(2 * j + y, bot)
        stage2 = [rcopy(send2a.at[j], land2a.at[j], 11 + j, x_nbr) for j in range(2)]
        stage2 += [rcopy(send2b.at[j], land2b.at[j], 13 + j, y_nbr) for j in range(2)]
        for cp in stage2:
            cp.start()
        for cp in stage2:
            cp.wait_recv()
        for j in range(2):
            keep2a[j] = keep2a[j] + land2a[j].astype(F32)
            keep2b[j] = keep2b[j] + land2b[j].astype(F32)
        send3a[...] = keep2a[1 - y].astype(BF16)
        send3b[...] = keep2b[1 - x].astype(BF16)
        stage3 = [rcopy(send3a, land3a, 15, y_nbr), rcopy(send3b, land3b, 16, x_nbr)]
        for cp in stage3:
            cp.start()
        for cp in stage3:
            cp.wait_recv()
        rin_ref[top, :] = keep2a[y] + land3a[...].astype(F32)
        rin_ref[bot, :] = keep2b[x] + land3b[...].astype(F32)

        for cp in small:
            cp.wait_recv()
        for cp in small + stage1 + stage2 + stage3:
            cp.wait_send()
        for land, red in ((lkv_ref, rkv_ref), (lout_ref, rout_ref), (sland, sred_ref)):
            acc = land[0].astype(F32)
            for d in range(1, N_DEV):
                acc = acc + land[d].astype(F32)
            red[...] = acc

    args = (dw_in, land_kv, land_out, d_norm_g, d_mem_g, d_scale, d_bf, d_gq, d_gk, d_gqm, d_gkm, dwp_bd, loss_blk)
    half_chunk = lambda n, dt: pltpu.VMEM((n, half, IN_CHUNK), dt)
    return pl.pallas_call(
        body,
        name="exchange_grads",
        out_shape=(jax.ShapeDtypeStruct((D_MODEL, IN_CHUNK), F32), jax.ShapeDtypeStruct((128, 512), F32),
                   jax.ShapeDtypeStruct((128, D_MODEL), F32), jax.ShapeDtypeStruct((SB_ROWS, SB_W), F32)),
        in_specs=[pl.BlockSpec(memory_space=pltpu.VMEM)] * len(args),
        out_specs=(pl.BlockSpec(memory_space=pltpu.VMEM),) * 4,
        scratch_shapes=[
            pltpu.VMEM((4, D_MODEL, IN_CHUNK), BF16),
            half_chunk(2, BF16), half_chunk(2, BF16), half_chunk(2, F32), half_chunk(2, F32), half_chunk(2, BF16), half_chunk(2, BF16),
            pltpu.VMEM((half, IN_CHUNK), BF16), pltpu.VMEM((half, IN_CHUNK), BF16),
            pltpu.VMEM((half, IN_CHUNK), BF16), pltpu.VMEM((half, IN_CHUNK), BF16),
            pltpu.VMEM((SB_ROWS, SB_W), F32),
            pltpu.VMEM((N_DEV, SB_ROWS, SB_W), F32),
            pltpu.SemaphoreType.DMA((17,)),
            pltpu.SemaphoreType.DMA((17,)),
        ],
        compiler_params=pltpu.CompilerParams(vmem_limit_bytes=VMEM_LIMIT),
    )(*args)


def _fwd_proj(x, norm_g, w_my):
    s_len = x.shape[0]

    def body(x_ref, g_ref, w_ref, proj_ref, h_ref):
        xv = x_ref[...]
        h = (xv * _rms(xv) * g_ref[...]).astype(BF16)
        h_ref[...] = h
        proj_ref[...] = _dot(h, w_ref[...])

    return pl.pallas_call(
        body,
        name="fwd_proj",
        grid=(s_len // TM,),
        in_specs=[pl.BlockSpec((TM, D_MODEL), lambda i: (i, 0)), _full((1, D_MODEL)), _full((D_MODEL, MY_WIDTH))],
        out_specs=[pl.BlockSpec((TM, MY_WIDTH), lambda i: (i, 0)), pl.BlockSpec((TM, D_MODEL), lambda i: (i, 0))],
        out_shape=[jax.ShapeDtypeStruct((s_len, MY_WIDTH), F32), jax.ShapeDtypeStruct((s_len, D_MODEL), BF16)],
        compiler_params=_params(("parallel",)),
    )(x, norm_g, w_my)


def _log_sigmoid(z):
    return jnp.minimum(z, 0.0) - jnp.log(1.0 + jnp.exp(-jnp.abs(z)))


def _fox_prep(proj, bf_pad, gq, gk):
    s_len = proj.shape[0]

    def body(q_ref, k_ref, v_ref, f_ref, bf_ref, gq_ref, gk_ref, qa_ref, ka_ref, vh_ref, kt_ref, vt_ref, carry_ref):
        @pl.when(pl.program_id(0) == 0)
        def _():
            carry_ref[...] = jnp.zeros((1, 128), F32)

        lane = lax.broadcasted_iota(jnp.int32, (TM, 128), 1)
        logf = jnp.where(lane < FOX_HEADS, _log_sigmoid(f_ref[...] + bf_ref[...]), 0.0)
        row = lax.broadcasted_iota(jnp.int32, (TM, TM), 0)
        col = lax.broadcasted_iota(jnp.int32, (TM, TM), 1)
        tri = jnp.where(col <= row, 1.0, 0.0).astype(BF16)
        hi, mid, lo = _split3(logf)
        cum = _dot(tri, hi) + _dot(tri, mid) + _dot(tri, lo) + carry_ref[...]
        carry_ref[...] = cum[TM - 1:TM, :]
        f_hi, f_mid, f_lo = [t.astype(F32) for t in _split3(cum)]
        zeros = jnp.zeros((TM, HEAD_DIM), F32)
        for h in range(FOX_HEADS):
            sl = slice(HEAD_DIM * h, HEAD_DIM * (h + 1))
            qh = q_ref[:, sl]
            kh = k_ref[:, sl]
            qn = qh * _rms(qh) * gq_ref[...] * 0.125
            kn = kh * _rms(kh) * gk_ref[...]
            a, b, c = f_hi[:, h:h + 1], f_mid[:, h:h + 1], f_lo[:, h:h + 1]
            qa = jnp.where(lane < 64, jnp.concatenate([qn, zeros], axis=1),
                           jnp.where(lane == 64, a, jnp.where(lane == 65, b, jnp.where(lane == 66, c,
                                                                                         jnp.where(lane < 70, 1.0, 0.0)))))
            ka = jnp.where(lane < 64, jnp.concatenate([kn, zeros], axis=1),
                           jnp.where(lane < 67, 1.0, jnp.where(lane == 67, -a, jnp.where(lane == 68, -b,
                                                                                          jnp.where(lane == 69, -c, 0.0)))))
            vh = v_ref[:, sl]
            v_aug = jnp.where(lane < 64, jnp.concatenate([vh, zeros], axis=1), jnp.where(lane == 64, 1.0, 0.0))
            qa_ref[h] = qa.astype(BF16)
            ka_ref[h] = ka.astype(BF16)
            vh_ref[h] = vh.astype(BF16)
            kt_ref[h, 0] = ka.T.astype(BF16)
            vt_ref[h, 0] = v_aug.T.astype(BF16)

    col_blk = lambda j: pl.BlockSpec((TM, PIECE), lambda i: (i, j))
    head_blk = lambda w: pl.BlockSpec((FOX_HEADS, TM, w), lambda i: (0, i, 0))
    tile_blk = pl.BlockSpec((FOX_HEADS, 1, 128, TM), lambda i: (0, i, 0, 0))
    tiled = jax.ShapeDtypeStruct((FOX_HEADS, s_len // TM, 128, TM), BF16)
    return pl.pallas_call(
        body,
        name="fox_prep",
        grid=(s_len // TM,),
        in_specs=[col_blk(1), col_blk(2), col_blk(3), pl.BlockSpec((TM, 128), lambda i: (i, MY_F_OFF // 128)),
                  _full((1, 128)), _full((1, HEAD_DIM)), _full((1, HEAD_DIM))],
        out_specs=[head_blk(128), head_blk(128), head_blk(HEAD_DIM), tile_blk, tile_blk],
        out_shape=[jax.ShapeDtypeStruct((FOX_HEADS, s_len, 128), BF16), jax.ShapeDtypeStruct((FOX_HEADS, s_len, 128), BF16),
                   jax.ShapeDtypeStruct((FOX_HEADS, s_len, HEAD_DIM), BF16), tiled, tiled],
        scratch_shapes=[pltpu.VMEM((1, 128), F32)],
        compiler_params=_params(("arbitrary",)),
    )(proj, proj, proj, proj, bf_pad, gq, gk)


def _mem_prep(mem, g, w_kv, gk):
    def body(mem_ref, g_ref, w_ref, gk_ref, km_ref, vm_ref):
        m = mem_ref[...]
        kv = _dot((m * _rms(m) * g_ref[...]).astype(BF16), w_ref[...])
        for h in range(MEM_HEADS):
            sl = slice(HEAD_DIM * h, HEAD_DIM * (h + 1))
            kh = kv[:, sl]
            km_ref[:, sl] = (kh * _rms(kh) * gk_ref[...]).astype(BF16)
        vm_ref[...] = kv[:, 256:512].astype(BF16)

    return pl.pallas_call(
        body,
        name="mem_prep",
        out_shape=(jax.ShapeDtypeStruct((N_MEM, 256), BF16),) * 2,
        in_specs=[pl.BlockSpec(memory_space=pltpu.VMEM)] * 4,
        out_specs=(pl.BlockSpec(memory_space=pltpu.VMEM),) * 2,
        compiler_params=pltpu.CompilerParams(vmem_limit_bytes=VMEM_LIMIT),
    )(mem, g, w_kv, gk)


def _causal_mask():
    row = lax.broadcasted_iota(jnp.int32, (TA, TA), 0)
    col = lax.broadcasted_iota(jnp.int32, (TA, TA), 1)
    return col <= row


def _causal_mask_t():
    row = lax.broadcasted_iota(jnp.int32, (TA, TA), 0)
    col = lax.broadcasted_iota(jnp.int32, (TA, TA), 1)
    return row <= col


def _fox_fwd(q_aug, k_aug, vt_aug, w_kv16, w_out16):
    s_len = q_aug.shape[1]
    n_tiles = s_len // TA
    n_groups = FOX_HEADS // HB

    def body(q_ref, k_ref, vt_ref, kv16_ref, out16_ref, o_ref, st_ref, kv_all, out_all, send_sems, recv_sems, local_sems):
        qi = pl.program_id(1)
        remote, local = _row_block_exchange(kv16_ref, out16_ref, kv_all, out_all, send_sems, recv_sems, local_sems, gather=True)

        @pl.when((pl.program_id(0) == 0) & (qi == 0))
        def _():
            for cp in remote + local:
                cp.start()

        qs = [q_ref[h] for h in range(HB)]

        def step(j, carry, masked):
            rows = pl.ds(pl.multiple_of(j * TA, TA), TA)
            scores = [_dot_nt(k_ref[h, rows, :], qs[h]) for h in range(HB)]
            soft = []
            for h in range(HB):
                m, _ = carry[h]
                s = jnp.where(_causal_mask_t(), scores[h], -1e30) if masked else scores[h]
                m_new = jnp.maximum(m, jnp.max(s, axis=0, keepdims=True))
                soft.append((m_new, jnp.exp(m - m_new), jnp.exp(s - m_new).astype(BF16)))
            return tuple((m_new, alpha * carry[h][1] + _dot(vt_ref[h, j], p)) for h, (m_new, alpha, p) in enumerate(soft))

        init = tuple((jnp.full((1, TA), -1e30, F32), jnp.zeros((128, TA), F32)) for _ in range(HB))
        carry = lax.fori_loop(0, qi, lambda j, cr: step(j, cr, False), init)
        carry = step(qi, carry, True)
        for h in range(HB):
            m, acc = carry[h]
            l = acc[HEAD_DIM:HEAD_DIM + 1, :]
            lse = m + jnp.log(l)
            o_ref[h] = jnp.concatenate([acc[0:HEAD_DIM] / l, jnp.broadcast_to(lse, (HEAD_DIM, TA))], axis=0).T
            st_ref[h, 0] = jnp.broadcast_to(lse, (8, TA))

        @pl.when((pl.program_id(0) == n_groups - 1) & (qi == n_tiles - 1))
        def _():
            for cp in remote:
                cp.wait_recv()
            for cp in remote:
                cp.wait_send()
            for cp in local:
                cp.wait()

    hbm = pl.BlockSpec(memory_space=pl.ANY)
    return pl.pallas_call(
        body,
        name="fox_fwd",
        grid=(n_groups, n_tiles),
        in_specs=[pl.BlockSpec((HB, TA, 128), lambda g, i: (g, i, 0)), pl.BlockSpec((HB, s_len, 128), lambda g, i: (g, 0, 0)),
                  pl.BlockSpec((HB, n_tiles, 128, TA), lambda g, i: (g, 0, 0, 0)), hbm, hbm],
        out_specs=[pl.BlockSpec((HB, TA, 128), lambda g, i: (g, i, 0)), pl.BlockSpec((HB, 1, 8, TA), lambda g, i: (g, i, 0, 0)),
                   hbm, hbm],
        out_shape=[jax.ShapeDtypeStruct((FOX_HEADS, s_len, 128), F32), jax.ShapeDtypeStruct((FOX_HEADS, n_tiles, 8, TA), F32),
                   jax.ShapeDtypeStruct((D_MODEL, 512), BF16), jax.ShapeDtypeStruct((D_MODEL, D_MODEL), BF16)],
        scratch_shapes=[pltpu.SemaphoreType.DMA((14,)), pltpu.SemaphoreType.DMA((14,)), pltpu.SemaphoreType.DMA((2,))],
        compiler_params=_params(("arbitrary", "arbitrary")),
    )(q_aug, k_aug, vt_aug, w_kv16, w_out16)


def _lane_group(lane, a, b, c, d):
    return jnp.where(lane < 64, a, jnp.where(lane < 128, b, jnp.where(lane < 192, c, d)))


def _pool_count(row0):
    lane = lax.broadcasted_iota(jnp.int32, (TM, POOL_WIDTH), 1)
    t1 = row0 + lax.broadcasted_iota(jnp.int32, (TM, POOL_WIDTH), 0) + 1
    return jnp.minimum(t1, _lane_group(lane, 2, 4, 8, 16)).astype(F32), lane


def _pool_delta(u, halo, cnt, lane):
    xe = jnp.concatenate([halo, u], axis=0)
    s2 = xe + pltpu.roll(xe, 1, 0)
    s4 = s2 + pltpu.roll(s2, 2, 0)
    s8 = s4 + pltpu.roll(s4, 4, 0)
    s16 = s8 + pltpu.roll(s8, 8, 0)
    win = _lane_group(lane, s2[HALO:], s4[HALO:], s8[HALO:], s16[HALO:])
    return win / cnt - u


def _pool_delta_bwd(dd, dd_next, cnt, cnt_next, lane):
    ee = jnp.concatenate([dd / cnt, dd_next / cnt_next], axis=0)
    n = TM + HALO
    r2 = ee + pltpu.roll(ee, n - 1, 0)
    r4 = r2 + pltpu.roll(r2, n - 2, 0)
    r8 = r4 + pltpu.roll(r4, n - 4, 0)
    r16 = r8 + pltpu.roll(r8, n - 8, 0)
    return _lane_group(lane, r2[:TM], r4[:TM], r8[:TM], r16[:TM]) - dd


def _mem_attn(qm, gq, km_ref, vm_ref):
    heads = []
    for h in range(MEM_HEADS):
        sl = slice(HEAD_DIM * h, HEAD_DIM * (h + 1))
        qh = qm[:, sl]
        r = _rms(qh)
        a = qh * r
        q16 = (a * gq * 0.125).astype(BF16)
        s = _dot_nt(q16, km_ref[:, sl])
        e = jnp.exp(s - jnp.max(s, axis=-1, keepdims=True))
        p = e / jnp.sum(e, axis=-1, keepdims=True)
        heads.append((a, r, q16, p, _dot(p.astype(BF16), vm_ref[:, sl])))
    return heads


def _row_specs(s_len):
    n_halo = s_len // HALO
    piece = lambda j: pl.BlockSpec((TM, PIECE), lambda i: (i, j))
    before = lambda j: pl.BlockSpec((HALO, 256), lambda i: (jnp.maximum(i * (TM // HALO) - 1, 0), j))
    after = lambda j: pl.BlockSpec((HALO, 256), lambda i: (jnp.minimum((i + 1) * (TM // HALO), n_halo - 1), j))
    return piece, before, after


def _mix_fwd(proj, olse, km, vm, wp_bd, pool_scale, gq_m):
    s_len = proj.shape[0]

    def body(ua_ref, halo_ref, gb_ref, qm_ref, ol_ref, km_ref, vm_ref, wp_ref, sc_ref, gq_ref, out_ref):
        i = pl.program_id(0)
        u = ua_ref[:, 0:256]
        g_a = ua_ref[:, 256:512]
        halo = jnp.where(i > 0, halo_ref[...], 0.0)
        cnt, lane = _pool_count(i * TM)
        d = _pool_delta(u, halo, cnt, lane).astype(BF16)
        y_a = _dot(d, wp_ref[...]) * sc_ref[...]
        out_ref[:, 0:256] = (y_a * (g_a * _sigmoid(g_a))).astype(BF16)
        for h in range(FOX_HEADS):
            sl = slice(HEAD_DIM * h, HEAD_DIM * (h + 1))
            g = gb_ref[:, sl]
            out_ref[:, 256 + HEAD_DIM * h:256 + HEAD_DIM * (h + 1)] = (ol_ref[h][:, 0:HEAD_DIM] * (g * _sigmoid(g))).astype(BF16)
        heads = _mem_attn(qm_ref[:, 0:256], gq_ref[...], km_ref, vm_ref)
        for h in range(MEM_HEADS):
            g = qm_ref[:, 256 + HEAD_DIM * h:256 + HEAD_DIM * (h + 1)]
            out_ref[:, 768 + HEAD_DIM * h:768 + HEAD_DIM * (h + 1)] = (heads[h][4] * (g * _sigmoid(g))).astype(BF16)

    piece, before, _ = _row_specs(s_len)
    return pl.pallas_call(
        body,
        name="mix_fwd",
        grid=(s_len // TM,),
        in_specs=[piece(0), before(0), piece(4), piece(5), pl.BlockSpec((FOX_HEADS, TM, 128), lambda i: (0, i, 0)),
                  _full((N_MEM, 256)), _full((N_MEM, 256)), _full((256, 256)), _full((1, 256)), _full((1, HEAD_DIM))],
        out_specs=pl.BlockSpec((TM, D_MODEL), lambda i: (i, 0)),
        out_shape=jax.ShapeDtypeStruct((s_len, D_MODEL), BF16),
        compiler_params=_params(("parallel",)),
    )(proj, proj, proj, proj, olse, km, vm, wp_bd, pool_scale, gq_m)


def _out_loss(mixed, x, target, w_out):
    s_len = x.shape[0]

    def body(mix_ref, x_ref, t_ref, w_ref, dout_ref, dmix_ref, dw16_ref, loss_ref, dw_ref):
        @pl.when(pl.program_id(0) == 0)
        def _():
            dw_ref[...] = jnp.zeros((D_MODEL, D_MODEL), F32)
            loss_ref[...] = jnp.zeros((8, 128), F32)

        mixed16 = mix_ref[...]
        err = x_ref[...] + _dot(mixed16, w_ref[...]) - t_ref[...]
        loss_ref[...] += 0.5 * jnp.sum(jnp.mean(err * err, axis=-1, keepdims=True))
        d_out = err / float(D_MODEL)
        dout_ref[...] = d_out
        d16 = d_out.astype(BF16)
        dmix_ref[...] = _dot_nt(d16, w_ref[...])
        dw_ref[...] += _dot_tn(mixed16, d16)

        @pl.when(pl.program_id(0) == pl.num_programs(0) - 1)
        def _():
            dw16_ref[...] = dw_ref[...].astype(BF16)

    row = pl.BlockSpec((TM, D_MODEL), lambda i: (i, 0))
    return pl.pallas_call(
        body,
        name="out_loss",
        grid=(s_len // TM,),
        in_specs=[row, row, row, _full((D_MODEL, D_MODEL))],
        out_specs=[row, row, _full((D_MODEL, D_MODEL)), _full((8, 128))],
        out_shape=[jax.ShapeDtypeStruct((s_len, D_MODEL), F32), jax.ShapeDtypeStruct((s_len, D_MODEL), F32),
                   jax.ShapeDtypeStruct((D_MODEL, D_MODEL), BF16), jax.ShapeDtypeStruct((8, 128), F32)],
        scratch_shapes=[pltpu.VMEM((D_MODEL, D_MODEL), F32)],
        compiler_params=_params(("arbitrary",)),
    )(mixed, x, target, w_out)


def _silu_pair(g):
    s = _sigmoid(g)
    return g * s, s * (1.0 + g * (1.0 - s))


def _mix_bwd(proj, olse, d_mixed, km, vm, wp_bd, pool_scale, gq_m):
    s_len = proj.shape[0]
    n_tiles = s_len // TM

    def body(ua_ref, halo_ref, ga_next_ref, gb_ref, qm_ref, ol_ref, dm_ref, dm_next_ref, km_ref, vm_ref, wp_ref, sc_ref, gq_ref,
             dua_ref, dgb_ref, dqm_ref, do_ref, dl_ref, dwp_ref, dsc_ref, dkm_ref, dvm_ref, dgq_ref):
        i = pl.program_id(0)

        @pl.when(i == 0)
        def _():
            dwp_ref[...] = jnp.zeros((256, 256), F32)
            dsc_ref[...] = jnp.zeros((1, 256), F32)
            dkm_ref[...] = jnp.zeros((N_MEM, 256), F32)
            dvm_ref[...] = jnp.zeros((N_MEM, 256), F32)
            dgq_ref[...] = jnp.zeros((1, HEAD_DIM), F32)

        u = ua_ref[:, 0:256]
        g_a = ua_ref[:, 256:512]
        halo = jnp.where(i > 0, halo_ref[...], 0.0)
        cnt, lane = _pool_count(i * TM)
        d16 = _pool_delta(u, halo, cnt, lane).astype(BF16)
        t = _dot(d16, wp_ref[...])
        y_a = t * sc_ref[...]
        silu_a, dsilu_a = _silu_pair(g_a)
        dm_a = dm_ref[:, 0:256]
        dy_a = dm_a * silu_a
        dsc_ref[...] += jnp.sum(dy_a * t, axis=0, keepdims=True)
        dt16 = (dy_a * sc_ref[...]).astype(BF16)
        dwp_ref[...] += _dot_tn(d16, dt16)
        dd = _dot_nt(dt16, wp_ref[...])
        g_next = ga_next_ref[...]
        dt_next = (dm_next_ref[...] * (g_next * _sigmoid(g_next)) * sc_ref[...]).astype(BF16)
        dd_next = jnp.where(i < n_tiles - 1, _dot_nt(dt_next, wp_ref[...]), 0.0)
        cnt_next = _pool_count((i + 1) * TM)[0][0:HALO]
        dua_ref[:, 0:256] = _pool_delta_bwd(dd, dd_next, cnt, cnt_next, lane).astype(BF16)
        dua_ref[:, 256:512] = (dm_a * y_a * dsilu_a).astype(BF16)

        ones = jnp.ones((8, HEAD_DIM), BF16)
        for h in range(FOX_HEADS):
            sl = slice(HEAD_DIM * h, HEAD_DIM * (h + 1))
            silu_b, dsilu_b = _silu_pair(gb_ref[:, sl])
            dm_b = dm_ref[:, 256 + HEAD_DIM * h:256 + HEAD_DIM * (h + 1)]
            o_h = ol_ref[h][:, 0:HEAD_DIM]
            d_o = dm_b * silu_b
            do_ref[h] = d_o.astype(BF16)
            dgb_ref[:, sl] = (dm_b * o_h * dsilu_b).astype(BF16)
            prod = d_o * o_h
            hi = prod.astype(BF16)
            lo = (prod - hi.astype(F32)).astype(BF16)
            dl_ref[h, 0] = _dot_nt(ones, hi) + _dot_nt(ones, lo)

        heads = _mem_attn(qm_ref[:, 0:256], gq_ref[...], km_ref, vm_ref)
        dgq = jnp.zeros((TM, HEAD_DIM), F32)
        for h in range(MEM_HEADS):
            sl = slice(HEAD_DIM * h, HEAD_DIM * (h + 1))
            a, r, q16, p, y_m = heads[h]
            silu_m, dsilu_m = _silu_pair(qm_ref[:, 256 + HEAD_DIM * h:256 + HEAD_DIM * (h + 1)])
            dm_m = dm_ref[:, 768 + HEAD_DIM * h:768 + HEAD_DIM * (h + 1)]
            dqm_ref[:, 256 + HEAD_DIM * h:256 + HEAD_DIM * (h + 1)] = (dm_m * y_m * dsilu_m).astype(BF16)
            dy16 = (dm_m * silu_m).astype(BF16)
            dp = _dot_nt(dy16, vm_ref[:, sl])
            dvm_ref[:, sl] += _dot_tn(p.astype(BF16), dy16)
            ds16 = (p * (dp - jnp.sum(dp * p, axis=-1, keepdims=True))).astype(BF16)
            dkm_ref[:, sl] += _dot_tn(ds16, q16)
            dq, dg_rows = _rms_bwd(a, r, gq_ref[...], _dot(ds16, km_ref[:, sl]) * 0.125)
            dqm_ref[:, sl] = dq.astype(BF16)
            dgq = dgq + dg_rows
        dgq_ref[...] += jnp.sum(dgq, axis=0, keepdims=True)

    piece, before, after = _row_specs(s_len)
    n_halo = s_len // HALO
    row = pl.BlockSpec((TM, D_MODEL), lambda i: (i, 0))
    dm_after = pl.BlockSpec((HALO, 256), lambda i: (jnp.minimum((i + 1) * (TM // HALO), n_halo - 1), 0))
    out_piece = pl.BlockSpec((TM, PIECE), lambda i: (i, 0))
    return pl.pallas_call(
        body,
        name="mix_bwd",
        grid=(n_tiles,),
        in_specs=[piece(0), before(0), after(1), piece(4), piece(5), pl.BlockSpec((FOX_HEADS, TM, 128), lambda i: (0, i, 0)),
                  row, dm_after, _full((N_MEM, 256)), _full((N_MEM, 256)), _full((256, 256)), _full((1, 256)), _full((1, HEAD_DIM))],
        out_specs=[out_piece, out_piece, out_piece, pl.BlockSpec((FOX_HEADS, TM, HEAD_DIM), lambda i: (0, i, 0)),
                   pl.BlockSpec((FOX_HEADS, 1, 8, TM), lambda i: (0, i, 0, 0)),
                   _full((256, 256)), _full((1, 256)), _full((N_MEM, 256)), _full((N_MEM, 256)), _full((1, HEAD_DIM))],
        out_shape=[jax.ShapeDtypeStruct((s_len, PIECE), BF16)] * 3 + [
            jax.ShapeDtypeStruct((FOX_HEADS, s_len, HEAD_DIM), BF16),
            jax.ShapeDtypeStruct((FOX_HEADS, n_tiles, 8, TM), F32),
            jax.ShapeDtypeStruct((256, 256), F32), jax.ShapeDtypeStruct((1, 256), F32),
            jax.ShapeDtypeStruct((N_MEM, 256), F32), jax.ShapeDtypeStruct((N_MEM, 256), F32),
            jax.ShapeDtypeStruct((1, HEAD_DIM), F32)],
        compiler_params=_params(("arbitrary",)),
    )(proj, proj, proj, proj, proj, olse, d_mixed, d_mixed, km, vm, wp_bd, pool_scale, gq_m)


def _fox_bwd(q_aug, k_aug, kt_aug, v_h, d_o, lse_rows, delta_rows, dw_kv16, dw_out16):
    s_len = q_aug.shape[1]
    n_tiles = s_len // TA
    n_groups = FOX_HEADS // HB

    def body(q_ref, k_ref, kt_ref, v_ref, do_ref, st_ref, dl_ref, dkv16_ref, dout16_ref, dqt_ref, dk_ref, dv_ref, land_kv, land_out,
             send_sems, recv_sems, local_sems):
        j = pl.program_id(1)
        remote, local = _row_block_exchange(dkv16_ref, dout16_ref, land_kv, land_out, send_sems, recv_sems, local_sems, gather=False)

        @pl.when((pl.program_id(0) == 0) & (j == 0))
        def _():
            for cp in remote + local:
                cp.start()

        @pl.when(j == 0)
        def _():
            dqt_ref[...] = jnp.zeros((HB, n_tiles, 128, TA), F32)

        ks = [k_ref[h] for h in range(HB)]
        kts = [kt_ref[h, 0] for h in range(HB)]
        vs = [v_ref[h] for h in range(HB)]

        def pair(i, acc, masked):
            rows = pl.ds(pl.multiple_of(i * TA, TA), TA)
            qs = [q_ref[h, rows, :] for h in range(HB)]
            d_os = [do_ref[h, rows, :] for h in range(HB)]
            scores = [(_dot_nt(ks[h], qs[h]), _dot_nt(vs[h], d_os[h])) for h in range(HB)]
            probs = []
            for h in range(HB):
                s, dp = scores[h]
                if masked:
                    s = jnp.where(_causal_mask_t(), s, -1e30)
                p = jnp.exp(s - st_ref[h, i, 0:1, :])
                probs.append((p.astype(BF16), (p * (dp - dl_ref[h, i, 0:1, :])).astype(BF16)))
            out = []
            for h in range(HB):
                p16, ds16 = probs[h]
                dqt_ref[h, i] += _dot(kts[h], ds16)
                out.append((acc[h][0] + _dot(ds16, qs[h]), acc[h][1] + _dot(p16, d_os[h])))
            return tuple(out)

        zero = tuple((jnp.zeros((TA, 128), F32), jnp.zeros((TA, HEAD_DIM), F32)) for _ in range(HB))
        acc = pair(j, zero, True)
        acc = lax.fori_loop(j + 1, n_tiles, lambda i, a: pair(i, a, False), acc)
        for h in range(HB):
            dk_ref[h] = acc[h][0]
            dv_ref[h] = acc[h][1]

        @pl.when((pl.program_id(0) == n_groups - 1) & (j == n_tiles - 1))
        def _():
            for cp in remote:
                cp.wait_recv()
            for cp in remote:
                cp.wait_send()
            for cp in local:
                cp.wait()

    whole = lambda w: pl.BlockSpec((HB, s_len, w), lambda g, j: (g, 0, 0))
    tile = lambda w: pl.BlockSpec((HB, TA, w), lambda g, j: (g, j, 0))
    rows_blk = lambda r: pl.BlockSpec((HB, n_tiles, r, TA), lambda g, j: (g, 0, 0, 0))
    hbm = pl.BlockSpec(memory_space=pl.ANY)
    return pl.pallas_call(
        body,
        name="fox_bwd",
        grid=(n_groups, n_tiles),
        in_specs=[whole(128), tile(128), pl.BlockSpec((HB, 1, 128, TA), lambda g, j: (g, j, 0, 0)), tile(HEAD_DIM),
                  whole(HEAD_DIM), rows_blk(8), rows_blk(8), hbm, hbm],
        out_specs=[rows_blk(128), tile(128), tile(HEAD_DIM), hbm, hbm],
        out_shape=[jax.ShapeDtypeStruct((FOX_HEADS, n_tiles, 128, TA), F32), jax.ShapeDtypeStruct((FOX_HEADS, s_len, 128), F32),
                   jax.ShapeDtypeStruct((FOX_HEADS, s_len, HEAD_DIM), F32),
                   jax.ShapeDtypeStruct((N_DEV, 128, 512), BF16), jax.ShapeDtypeStruct((N_DEV, 128, D_MODEL), BF16)],
        scratch_shapes=[pltpu.SemaphoreType.DMA((14,)), pltpu.SemaphoreType.DMA((14,)), pltpu.SemaphoreType.DMA((2,))],
        compiler_params=_params(("arbitrary", "arbitrary")),
    )(q_aug, k_aug, kt_aug, v_h, d_o, lse_rows, delta_rows, dw_kv16, dw_out16)


def _fox_post(proj, bf_pad, gq, gk, dq_aug, dk_aug, dv_h):
    s_len = proj.shape[0]
    n_tiles = s_len // TM

    def body(q_ref, k_ref, f_ref, bf_ref, gq_ref, gk_ref, dqa_ref, dka_ref, dvh_ref,
             dq_ref, dk_ref, dv_ref, df_ref, dgq_ref, dgk_ref, dbf_ref, carry_ref):
        @pl.when(pl.program_id(0) == 0)
        def _():
            carry_ref[...] = jnp.zeros((1, 128), F32)
            dgq_ref[...] = jnp.zeros((1, HEAD_DIM), F32)
            dgk_ref[...] = jnp.zeros((1, HEAD_DIM), F32)
            dbf_ref[...] = jnp.zeros((1, 128), F32)

        lane = lax.broadcasted_iota(jnp.int32, (TM, 128), 1)
        d_cum = jnp.zeros((TM, 128), F32)
        dgq = jnp.zeros((TM, HEAD_DIM), F32)
        dgk = jnp.zeros((TM, HEAD_DIM), F32)
        for h in range(FOX_HEADS):
            sl = slice(HEAD_DIM * h, HEAD_DIM * (h + 1))
            dqa = dqa_ref[h, 0].T
            dka = dka_ref[h]
            qh = q_ref[:, sl]
            kh = k_ref[:, sl]
            rq = _rms(qh)
            rk = _rms(kh)
            dq, dgq_rows = _rms_bwd(qh * rq, rq, gq_ref[...], dqa[:, 0:HEAD_DIM] * 0.125)
            dk, dgk_rows = _rms_bwd(kh * rk, rk, gk_ref[...], dka[:, 0:HEAD_DIM])
            dq_ref[:, sl] = dq.astype(BF16)
            dk_ref[:, sl] = dk.astype(BF16)
            dv_ref[:, sl] = dvh_ref[h].astype(BF16)
            dgq = dgq + dgq_rows
            dgk = dgk + dgk_rows
            d_cum = jnp.where(lane == h, dqa[:, 64:65] - dka[:, 67:68], d_cum)
        dgq_ref[...] += jnp.sum(dgq, axis=0, keepdims=True)
        dgk_ref[...] += jnp.sum(dgk, axis=0, keepdims=True)

        row = lax.broadcasted_iota(jnp.int32, (TM, TM), 0)
        col = lax.broadcasted_iota(jnp.int32, (TM, TM), 1)
        tri = jnp.where(col >= row, 1.0, 0.0).astype(BF16)
        hi, mid, lo = _split3(d_cum)
        d_logf = _dot(tri, hi) + _dot(tri, mid) + _dot(tri, lo) + carry_ref[...]
        carry_ref[...] = d_logf[0:1, :]
        z = f_ref[...] + bf_ref[...]
        d_f = jnp.where(lane < FOX_HEADS, d_logf * _sigmoid(-z), 0.0)
        df_ref[...] = d_f.astype(BF16)
        dbf_ref[...] += jnp.sum(d_f, axis=0, keepdims=True)

    rev = lambda i: n_tiles - 1 - i
    col_blk = lambda j: pl.BlockSpec((TM, PIECE), lambda i: (rev(i), j))
    head_blk = lambda w: pl.BlockSpec((FOX_HEADS, TM, w), lambda i: (0, rev(i), 0))
    out_piece = pl.BlockSpec((TM, PIECE), lambda i: (rev(i), 0))
    return pl.pallas_call(
        body,
        name="fox_post",
        grid=(n_tiles,),
        in_specs=[col_blk(1), col_blk(2), pl.BlockSpec((TM, 128), lambda i: (rev(i), MY_F_OFF // 128)),
                  _full((1, 128)), _full((1, HEAD_DIM)), _full((1, HEAD_DIM)),
                  pl.BlockSpec((FOX_HEADS, 1, 128, TM), lambda i: (0, rev(i), 0, 0)), head_blk(128), head_blk(HEAD_DIM)],
        out_specs=[out_piece, out_piece, out_piece, pl.BlockSpec((TM, 128), lambda i: (rev(i), 0)),
                   _full((1, HEAD_DIM)), _full((1, HEAD_DIM)), _full((1, 128))],
        out_shape=[jax.ShapeDtypeStruct((s_len, PIECE), BF16)] * 3 + [
            jax.ShapeDtypeStruct((s_len, 128), BF16), jax.ShapeDtypeStruct((1, HEAD_DIM), F32),
            jax.ShapeDtypeStruct((1, HEAD_DIM), F32), jax.ShapeDtypeStruct((1, 128), F32)],
        scratch_shapes=[pltpu.VMEM((1, 128), F32)],
        compiler_params=_params(("arbitrary",)),
    )(proj, proj, proj, bf_pad, gq, gk, dq_aug, dk_aug, dv_h)


def _mem_bwd(mem, g, w_kv, gk, dkm, dvm):
    def body(mem_ref, g_ref, w_ref, gk_ref, dkm_ref, dvm_ref, dw_ref, dg_ref, dgk_ref, dkv_ref):
        m = mem_ref[...]
        r = _rms(m)
        a = m * r
        mn16 = (a * g_ref[...]).astype(BF16)
        kv = _dot(mn16, w_ref[...])
        dgk = jnp.zeros((N_MEM, HEAD_DIM), F32)
        for h in range(MEM_HEADS):
            sl = slice(HEAD_DIM * h, HEAD_DIM * (h + 1))
            kh = kv[:, sl]
            rk = _rms(kh)
            dk, dg_rows = _rms_bwd(kh * rk, rk, gk_ref[...], dkm_ref[:, sl])
            dkv_ref[:, sl] = dk.astype(BF16)
            dgk = dgk + dg_rows
        dkv_ref[:, 256:512] = dvm_ref[...].astype(BF16)
        dgk_ref[...] = jnp.sum(dgk, axis=0, keepdims=True)
        dkv16 = dkv_ref[...]
        dw_ref[...] = _dot_tn(mn16, dkv16).astype(BF16)
        dg_ref[...] = jnp.sum(_dot_nt(dkv16, w_ref[...]) * a, axis=0, keepdims=True)

    return pl.pallas_call(
        body,
        name="mem_bwd",
        out_shape=(jax.ShapeDtypeStruct((D_MODEL, 512), BF16), jax.ShapeDtypeStruct((1, D_MODEL), F32),
                   jax.ShapeDtypeStruct((1, HEAD_DIM), F32)),
        in_specs=[pl.BlockSpec(memory_space=pltpu.VMEM)] * 6,
        out_specs=(pl.BlockSpec(memory_space=pltpu.VMEM),) * 3,
        scratch_shapes=[pltpu.VMEM((N_MEM, 512), BF16)],
        compiler_params=pltpu.CompilerParams(vmem_limit_bytes=VMEM_LIMIT),
    )(mem, g, w_kv, gk, dkm, dvm)


def _grad_x(pieces, d_f, w_my, x, norm_g, d_out):
    s_len = x.shape[0]

    def body(p0, p1, p2, p3, p4, p5, df_ref, w_ref, x_ref, g_ref, dout_ref, gx_ref, dg_ref):
        @pl.when(pl.program_id(0) == 0)
        def _():
            dg_ref[...] = jnp.zeros((1, D_MODEL), F32)

        dh = _dot_nt(df_ref[...], w_ref[:, MY_F_OFF:MY_WIDTH])
        for j, p in enumerate((p0, p1, p2, p3, p4, p5)):
            dh = dh + _dot_nt(p[...], w_ref[:, PIECE * j:PIECE * (j + 1)])
        xv = x_ref[...]
        r = _rms(xv)
        dx, dg_rows = _rms_bwd(xv * r, r, g_ref[...], dh)
        gx_ref[...] = dout_ref[...] + dx
        dg_ref[...] += jnp.sum(dg_rows, axis=0, keepdims=True)

    piece = pl.BlockSpec((TM, PIECE), lambda i: (i, 0))
    row = pl.BlockSpec((TM, D_MODEL), lambda i: (i, 0))
    return pl.pallas_call(
        body,
        name="grad_x",
        grid=(s_len // TM,),
        in_specs=[piece] * 6 + [pl.BlockSpec((TM, 128), lambda i: (i, 0)), _full((D_MODEL, MY_WIDTH)), row,
                                _full((1, D_MODEL)), row],
        out_specs=[row, _full((1, D_MODEL))],
        out_shape=[jax.ShapeDtypeStruct((s_len, D_MODEL), F32), jax.ShapeDtypeStruct((1, D_MODEL), F32)],
        compiler_params=_params(("arbitrary",)),
    )(*pieces, d_f, w_my, x, norm_g, d_out)


def _grad_w_in(h16, pieces, d_f):
    s_len = h16.shape[0]
    tk = 512

    def body(h_ref, p0, p1, p2, p3, p4, p5, df_ref, out_ref, dw_ref):
        @pl.when(pl.program_id(0) == 0)
        def _():
            dw_ref[...] = jnp.zeros((D_MODEL, MY_WIDTH), F32)

        h = h_ref[...]
        for j, p in enumerate((p0, p1, p2, p3, p4, p5)):
            dw_ref[:, PIECE * j:PIECE * (j + 1)] += _dot_tn(h, p[...])
        dw_ref[:, MY_F_OFF:MY_WIDTH] += _dot_tn(h, df_ref[...])

        @pl.when(pl.program_id(0) == pl.num_programs(0) - 1)
        def _():
            for d in range(N_DEV):
                parts = [dw_ref[:, start:start + width] for start, width in _chunk_segments(d)]
                out_ref[d] = (parts[0] if len(parts) == 1 else jnp.concatenate(parts, axis=1)).astype(BF16)

    piece = pl.BlockSpec((tk, PIECE), lambda i: (i, 0))
    return pl.pallas_call(
        body,
        name="grad_w_in",
        grid=(s_len // tk,),
        in_specs=[pl.BlockSpec((tk, D_MODEL), lambda i: (i, 0))] + [piece] * 6 + [pl.BlockSpec((tk, 128), lambda i: (i, 0))],
        out_specs=_full((N_DEV, D_MODEL, IN_CHUNK)),
        out_shape=jax.ShapeDtypeStruct((N_DEV, D_MODEL, IN_CHUNK), BF16),
        scratch_shapes=[pltpu.VMEM((D_MODEL, MY_WIDTH), F32)],
        compiler_params=_params(("arbitrary",)),
    )(h16, *pieces, d_f)


def _adamw(w, g, m, v):
    m = ADAM_B1 * m + (1.0 - ADAM_B1) * g
    v = ADAM_B2 * v + (1.0 - ADAM_B2) * (g * g)
    m_hat = m / (1.0 - ADAM_B1 ** ADAM_STEP)
    v_hat = v / (1.0 - ADAM_B2 ** ADAM_STEP)
    return -ADAM_LR * (m_hat / (jnp.sqrt(v_hat) + ADAM_EPS) + ADAM_WD * w), m, v


PARAM_ORDER = ("norm_g", "w_in", "b_f", "w_pool", "pool_scale", "fox_q_g", "fox_k_g", "mem_norm_g", "w_mem_kv", "mem_q_g", "mem_k_g", "w_out")


def _update(red_in, red_kv, red_out, sred, params):
    def body(rin_ref, rkv_ref, rout_ref, sred_ref, *refs):
        ins, outs = refs[:3 * len(PARAM_ORDER)], refs[3 * len(PARAM_ORDER):]
        wide = lambda row: jnp.concatenate([sred_ref[row + k:row + k + 1, :] for k in range(4)], axis=1)
        head = lambda row: sred_ref[row:row + 1, 0:HEAD_DIM]
        grads = {
            "norm_g": lambda: wide(SB_NORM_G), "w_in": lambda: rin_ref[...], "b_f": lambda: sred_ref[SB_B_F:SB_B_F + 1, 0:FOX_HEADS],
            "pool_scale": lambda: sred_ref[SB_POOL_SCALE:SB_POOL_SCALE + 1, :], "fox_q_g": lambda: head(SB_FOX_Q),
            "fox_k_g": lambda: head(SB_FOX_K), "mem_norm_g": lambda: wide(SB_MEM_NORM_G), "w_mem_kv": lambda: rkv_ref[...],
            "mem_q_g": lambda: head(SB_MEM_Q), "mem_k_g": lambda: head(SB_MEM_K), "w_out": lambda: rout_ref[...],
        }
        for p, name in enumerate(PARAM_ORDER):
            w_ref, m_ref, v_ref = ins[3 * p:3 * p + 3]
            g_out, d_out, m_out, v_out = outs[4 * p:4 * p + 4]
            if name == "w_pool":
                for grp in range(4):
                    g = sred_ref[SB_W_POOL:SB_W_POOL + 64, 64 * grp:64 * (grp + 1)]
                    delta, m_new, v_new = _adamw(w_ref[grp], g, m_ref[grp], v_ref[grp])
                    g_out[grp], d_out[grp], m_out[grp], v_out[grp] = g, delta, m_new, v_new
            else:
                g = grads[name]()
                delta, m_new, v_new = _adamw(w_ref[...], g, m_ref[...], v_ref[...])
                g_out[...], d_out[...], m_out[...], v_out[...] = g, delta, m_new, v_new

    flat = [t for name in PARAM_ORDER for t in params[name]]
    shapes = [params[name][0].shape for name in PARAM_ORDER for _ in range(4)]
    outs = pl.pallas_call(
        body,
        name="adamw_update",
        out_shape=tuple(jax.ShapeDtypeStruct(s, F32) for s in shapes),
        in_specs=[pl.BlockSpec(memory_space=pltpu.VMEM)] * (4 + len(flat)),
        out_specs=(pl.BlockSpec(memory_space=pltpu.VMEM),) * len(shapes),
        compiler_params=pltpu.CompilerParams(vmem_limit_bytes=VMEM_LIMIT),
    )(red_in, red_kv, red_out, sred, *flat)
    return {name: outs[4 * p:4 * p + 4] for p, name in enumerate(PARAM_ORDER)}


def kernel(x, mem, norm_g, w_in, b_f, w_pool, pool_scale, fox_q_g, fox_k_g, mem_norm_g, w_mem_kv, mem_q_g, mem_k_g, w_out, loss_target, m_norm_g, m_w_in, m_b_f, m_w_pool, m_pool_scale, m_fox_q_g, m_fox_k_g, m_mem_norm_g, m_w_mem_kv, m_mem_q_g, m_mem_k_g, m_w_out, v_norm_g, v_w_in, v_b_f, v_w_pool, v_pool_scale, v_fox_q_g, v_fox_k_g, v_mem_norm_g, v_w_mem_kv, v_mem_q_g, v_mem_k_g, v_w_out):
    given = dict(norm_g=(norm_g, m_norm_g, v_norm_g), w_in=(w_in, m_w_in, v_w_in), b_f=(b_f, m_b_f, v_b_f),
                 w_pool=(w_pool, m_w_pool, v_w_pool), pool_scale=(pool_scale, m_pool_scale, v_pool_scale),
                 fox_q_g=(fox_q_g, m_fox_q_g, v_fox_q_g), fox_k_g=(fox_k_g, m_fox_k_g, v_fox_k_g),
                 mem_norm_g=(mem_norm_g, m_mem_norm_g, v_mem_norm_g), w_mem_kv=(w_mem_kv, m_w_mem_kv, v_w_mem_kv),
                 mem_q_g=(mem_q_g, m_mem_q_g, v_mem_q_g), mem_k_g=(mem_k_g, m_mem_k_g, v_mem_k_g), w_out=(w_out, m_w_out, v_w_out))
    params = {name: tuple(t[0] if t.ndim > 2 else t for t in wmv) for name, wmv in given.items()}
    x2, mem2, target2 = x[0], mem[0], loss_target[0]

    w_my, w_kv16, w_out16, wp_bd, bf_pad = _gather_w_in(params["w_in"][0], params["w_mem_kv"][0], params["w_out"][0],
                                                         params["w_pool"][0], b_f)
    proj, h16 = _fwd_proj(x2, norm_g, w_my)
    q_aug, k_aug, v_h, kt_aug, vt_aug = _fox_prep(proj, bf_pad, fox_q_g, fox_k_g)
    olse, lse_rows, w_kv_all, w_out_all = _fox_fwd(q_aug, k_aug, vt_aug, w_kv16, w_out16)
    km, vm = _mem_prep(mem2, mem_norm_g, w_kv_all, mem_k_g)
    mixed = _mix_fwd(proj, olse, km, vm, wp_bd, pool_scale, mem_q_g)
    d_out, d_mixed, dw_out, loss_blk = _out_loss(mixed, x2, target2, w_out_all)

    d_ua, d_gb, d_qm, d_o, delta_rows, dwp_bd, d_scale, dkm, dvm, d_gqm = _mix_bwd(proj, olse, d_mixed, km, vm, wp_bd, pool_scale, mem_q_g)
    dw_kv, d_mem_g, d_gkm = _mem_bwd(mem2, mem_norm_g, w_kv_all, mem_k_g, dkm, dvm)
    dq_aug, dk_aug, dv_h, land_kv, land_out = _fox_bwd(q_aug, k_aug, kt_aug, v_h, d_o, lse_rows, delta_rows, dw_kv, dw_out)
    d_q, d_k, d_v, d_f, d_gq, d_gk, d_bf = _fox_post(proj, bf_pad, fox_q_g, fox_k_g, dq_aug, dk_aug, dv_h)
    pieces = (d_ua, d_q, d_k, d_v, d_gb, d_qm)
    dw_in = _grad_w_in(h16, pieces, d_f)
    grad_x, d_norm_g = _grad_x(pieces, d_f, w_my, x2, norm_g, d_out)

    red_in, red_kv, red_out, sred = _exchange_grads(dw_in, land_kv, land_out, d_norm_g, d_mem_g, d_scale, d_bf, d_gq, d_gk, d_gqm,
                                                    d_gkm, dwp_bd, loss_blk)
    new = _update(red_in, red_kv, red_out, sred, params)
    result = [sred[SB_LOSS, 0], grad_x[None]]
    for kind in range(4):
        for name in PARAM_ORDER:
            out = new[name][kind]
            result.append(out[None] if given[name][0].ndim > 2 else out)
    return tuple(result)
```

```python
import functools

import jax
import jax.numpy as jnp
from jax import lax
from jax.experimental import pallas as pl
from jax.experimental.pallas import tpu as pltpu

F32 = jnp.float32
BF16 = jnp.bfloat16
MESH = pl.DeviceIdType.MESH

N_DEV = 8
D_MODEL = 1024
HEAD_DIM = 64
FOX_HEADS = 8
MEM_HEADS = 4
N_MEM = 256
POOL_WIDTH = 256
EPS = 1e-6
IN_WIDTH = 3080
IN_CHUNK = IN_WIDTH // N_DEV
F_OFF = 2048
MY_WIDTH = 3200
MY_F_OFF = 3072
PIECE = 512
TM = 256
TA = 256
HB = 4
HALO = 16
VMEM_LIMIT = 56 * 1024 * 1024

ADAM_LR = 0.001
ADAM_B1 = 0.9
ADAM_B2 = 0.999
ADAM_EPS = 1e-08
ADAM_WD = 0.01
ADAM_STEP = 10


def _dot(a, b):
    return jnp.dot(a, b, preferred_element_type=F32)


def _dot_nt(a, b):
    return lax.dot_general(a, b, (((1,), (1,)), ((), ())), preferred_element_type=F32)


def _dot_tn(a, b):
    return lax.dot_general(a, b, (((0,), (0,)), ((), ())), preferred_element_type=F32)


def _sigmoid(g):
    return 1.0 / (1.0 + jnp.exp(-g))


def _split3(v):
    hi = v.astype(BF16)
    r1 = v - hi.astype(F32)
    mid = r1.astype(BF16)
    lo = (r1 - mid.astype(F32)).astype(BF16)
    return hi, mid, lo


def _rms(v):
    return lax.rsqrt(jnp.mean(v * v, axis=-1, keepdims=True) + EPS)


def _rms_bwd(a, r, g, dy):
    da = dy * g
    return r * (da - a * jnp.mean(da * a, axis=-1, keepdims=True)), dy * a


def _params(sem, limit=VMEM_LIMIT):
    return pltpu.CompilerParams(dimension_semantics=sem, vmem_limit_bytes=limit)


def _full(shape):
    return pl.BlockSpec(shape, lambda *_: (0,) * len(shape))


def _chunk_segments(d):
    runs = []
    for lo, hi, shift in ((0, F_OFF, 0), (F_OFF, F_OFF + FOX_HEADS, MY_F_OFF - F_OFF), (F_OFF + FOX_HEADS, IN_WIDTH, -FOX_HEADS)):
        a, b = max(lo, IN_CHUNK * d), min(hi, IN_CHUNK * (d + 1))
        if a < b:
            runs.append((a + shift, b - a))
    return runs


def _my_place():
    x, y, c = lax.axis_index("x"), lax.axis_index("y"), lax.axis_index("c")
    return x, y, c, 4 * x + 2 * y + c


def _shard_rows(dev):
    return pl.ds(pl.multiple_of(128 * dev, 128), 128)


def _gather_w_in(w_in, w_kv, w_out, w_pool, b_f):
    def body(win_ref, wkv_ref, wout_ref, wp_ref, bf_ref, wmy_ref, kv16_ref, out16_ref, wpbd_ref, bfpad_ref,
             in_all, pay_in, send_sems, recv_sems):
        x, y, c, me = _my_place()
        sibling = (x, y, 1 - c)
        chips = [(1 - x, y), (x, 1 - y), (1 - x, 1 - y)]

        pay_in[...] = win_ref[...].astype(BF16)

        def copy(k, block, to, own=False):
            px, py, pc = block
            dst = in_all.at[4 * px + 2 * py + pc]
            return pltpu.make_async_remote_copy(src_ref=pay_in if own else dst, dst_ref=dst, send_sem=send_sems.at[k],
                                                recv_sem=recv_sems.at[k], device_id=to, device_id_type=MESH)

        first = [copy(0, (x, y, c), sibling, own=True)]
        first += [copy(1 + j, (x, y, c), (*chip, c), own=True) for j, chip in enumerate(chips)]
        for cp in first:
            cp.start()
        in_all[me] = pay_in[...]
        kv16_ref[...] = wkv_ref[...].astype(BF16)
        out16_ref[...] = wout_ref[...].astype(BF16)
        wpbd_ref[...] = jnp.zeros((POOL_WIDTH, POOL_WIDTH), BF16)
        for grp in range(4):
            sl = slice(64 * grp, 64 * (grp + 1))
            wpbd_ref[sl, sl] = wp_ref[grp].astype(BF16)
        bfpad_ref[...] = jnp.zeros((1, 128), F32)
        bfpad_ref[:, 0:FOX_HEADS] = bf_ref[...]
        passed = []
        for j, chip in enumerate(chips):
            copy(1 + j, (*chip, c), (x, y, c)).wait_recv()
            passed.append(copy(4 + j, (*chip, c), sibling))
            passed[-1].start()
        copy(0, sibling, (x, y, c)).wait_recv()
        for j, chip in enumerate(chips):
            copy(4 + j, (*chip, 1 - c), (x, y, c)).wait_recv()
        for cp in first + passed:
            cp.wait_send()

        for r0 in range(0, D_MODEL, 256):
            ch = [in_all[d, r0:r0 + 256, :].astype(F32) for d in range(N_DEV)]
            lo = F_OFF - 5 * IN_CHUNK
            full = jnp.concatenate(ch[:5] + [ch[5][:, :lo], ch[5][:, lo + FOX_HEADS:], ch[6], ch[7], ch[5][:, lo:lo + FOX_HEADS],
                                             jnp.zeros((256, MY_WIDTH - IN_WIDTH), F32)], axis=1)
            wmy_ref[r0:r0 + 256, :] = full.astype(BF16)

    return pl.pallas_call(
        body,
        name="gather_w_in",
        out_shape=(jax.ShapeDtypeStruct((D_MODEL, MY_WIDTH), BF16), jax.ShapeDtypeStruct((128, 512), BF16),
                   jax.ShapeDtypeStruct((128, D_MODEL), BF16), jax.ShapeDtypeStruct((POOL_WIDTH, POOL_WIDTH), BF16),
                   jax.ShapeDtypeStruct((1, 128), F32)),
        in_specs=[pl.BlockSpec(memory_space=pltpu.VMEM)] * 5,
        out_specs=(pl.BlockSpec(memory_space=pltpu.VMEM),) * 5,
        scratch_shapes=[
            pltpu.VMEM((N_DEV, D_MODEL, IN_CHUNK), BF16),
            pltpu.VMEM((D_MODEL, IN_CHUNK), BF16),
            pltpu.SemaphoreType.DMA((7,)),
            pltpu.SemaphoreType.DMA((7,)),
        ],
        compiler_params=pltpu.CompilerParams(vmem_limit_bytes=VMEM_LIMIT),
    )(w_in, w_kv, w_out, w_pool, b_f)


def _row_block_exchange(kv_ref, out_ref, kv_dst, out_dst, send_sems, recv_sems, local_sems, gather):
    x, y, c, me = _my_place()
    remote = []
    for k in range(1, N_DEV):
        px, py, pc = x ^ (k >> 2), y ^ ((k >> 1) & 1), c ^ (k & 1)
        peer = 4 * px + 2 * py + pc
        for fam, (src, dst) in enumerate(((kv_ref, kv_dst), (out_ref, out_dst))):
            remote.append(pltpu.make_async_remote_copy(
                src_ref=src if gather else src.at[_shard_rows(peer)], dst_ref=dst.at[_shard_rows(me)] if gather else dst.at[me],
                send_sem=send_sems.at[7 * fam + k - 1], recv_sem=recv_sems.at[7 * fam + k - 1],
                device_id=(px, py, pc), device_id_type=MESH))
    local = [pltpu.make_async_copy(src if gather else src.at[_shard_rows(me)], dst.at[_shard_rows(me)] if gather else dst.at[me],
                                   local_sems.at[fam]) for fam, (src, dst) in enumerate(((kv_ref, kv_dst), (out_ref, out_dst)))]
    return remote, local


SB_ROWS, SB_W = 80, 256
SB_NORM_G, SB_MEM_NORM_G, SB_POOL_SCALE, SB_B_F, SB_FOX_Q, SB_FOX_K, SB_MEM_Q, SB_MEM_K, SB_LOSS, SB_W_POOL = 0, 4, 8, 9, 10, 11, 12, 13, 14, 16


def _exchange_grads(dw_in, land_kv, land_out, d_norm_g, d_mem_g, d_scale, d_bf, d_gq, d_gk, d_gqm, d_gkm, dwp_bd, loss_blk):
    half = D_MODEL // 2

    def body(in_ref, lkv_ref, lout_ref, dng, dmg, dsc, dbf, dgq, dgk, dgqm, dgkm, dwp, loss_ref,
             rin_ref, rkv_ref, rout_ref, sred_ref, land1, send2a, send2b, keep2a, keep2b, land2a, land2b,
             send3a, send3b, land3a, land3b, sb_ref, sland, send_sems, recv_sems):
        x, y, c, me = _my_place()
        sibling, x_nbr, y_nbr = (x, y, 1 - c), (1 - x, y, c), (x, 1 - y, c)

        def rcopy(src, dst, sem, to):
            return pltpu.make_async_remote_copy(src_ref=src, dst_ref=dst, send_sem=send_sems.at[sem], recv_sem=recv_sems.at[sem],
                                                device_id=to, device_id_type=MESH)

        sb_ref[...] = jnp.zeros((SB_ROWS, SB_W), F32)
        for k in range(4):
            sb_ref[SB_NORM_G + k:SB_NORM_G + k + 1, :] = dng[:, SB_W * k:SB_W * (k + 1)]
            sb_ref[SB_MEM_NORM_G + k:SB_MEM_NORM_G + k + 1, :] = dmg[:, SB_W * k:SB_W * (k + 1)]
        sb_ref[SB_POOL_SCALE:SB_POOL_SCALE + 1, :] = dsc[...]
        sb_ref[SB_B_F:SB_B_F + 1, 0:128] = dbf[...]
        for row, ref in ((SB_FOX_Q, dgq), (SB_FOX_K, dgk), (SB_MEM_Q, dgqm), (SB_MEM_K, dgkm)):
            sb_ref[row:row + 1, 0:HEAD_DIM] = ref[...]
        sb_ref[SB_LOSS:SB_LOSS + 1, 0:128] = loss_ref[0:1, :]
        for g in range(4):
            sl = slice(64 * g, 64 * (g + 1))
            sb_ref[SB_W_POOL:SB_W_POOL + 64, sl] = dwp[sl, sl]

        small = []
        for k in range(1, N_DEV):
            px, py, pc = x ^ (k >> 2), y ^ ((k >> 1) & 1), c ^ (k & 1)
            small.append(rcopy(sb_ref, sland.at[me], k - 1, (px, py, pc)))
        stage1 = [rcopy(in_ref.at[2 * k + 1 - c], land1.at[k], 7 + k, sibling) for k in range(4)]
        for cp in small + stage1:
            cp.start()
        sland[me] = sb_ref[...]
        for cp in stage1:
            cp.wait_recv()

        top, bot = slice(0, half), slice(half, D_MODEL)

        def chip_sum(k, rows):
            return in_ref[2 * k + c, rows, :].astype(F32) + land1[k, rows, :].astype(F32)

        for j in range(2):
            send2a[j] = chip_sum(2 * (1 - x) + j, top).astype(BF16)
            keep2a[j] = chip_sum(2 * x + j, top)
            send2b[j] = chip_sum(2 * j + 1 - y, bot).astype(BF16)
            keep2b[j] = chip_sum(2 * j + y, bot)
        stage2 = [rcopy(send2a.at[j], land2a.at[j], 11 + j, x_nbr) for j in range(2)]
        stage2 += [rcopy(send2b.at[j], land2b.at[j], 13 + j, y_nbr) for j in range(2)]
        for cp in stage2:
            cp.start()
        for cp in stage2:
            cp.wait_recv()
        for j in range(2):
            keep2a[j] = keep2a[j] + land2a[j].astype(F32)
            keep2b[j] = keep2b[j] + land2b[j].astype(F32)
        send3a[...] = keep2a[1 - y].astype(BF16)
        send3b[...] = keep2b[1 - x].astype(BF16)
        stage3 = [rcopy(send3a, land3a, 15, y_nbr), rcopy(send3b, land3b, 16, x_nbr)]
        for cp in stage3:
            cp.start()
        for cp in stage3:
            cp.wait_recv()
        rin_ref[top, :] = keep2a[y] + land3a[...].astype(F32)
        rin_ref[bot, :] = keep2b[x] + land3b[...].astype(F32)

        for cp in small:
            cp.wait_recv()
        for cp in small + stage1 + stage2 + stage3:
            cp.wait_send()
        for land, red in ((lkv_ref, rkv_ref), (lout_ref, rout_ref), (sland, sred_ref)):
            acc = land[0].astype(F32)
            for d in range(1, N_DEV):
                acc = acc + land[d].astype(F32)
            red[...] = acc

    args = (dw_in, land_kv, land_out, d_norm_g, d_mem_g, d_scale, d_bf, d_gq, d_gk, d_gqm, d_gkm, dwp_bd, loss_blk)
    half_chunk = lambda n, dt: pltpu.VMEM((n, half, IN_CHUNK), dt)
    return pl.pallas_call(
        body,
        name="exchange_grads",
        out_shape=(jax.ShapeDtypeStruct((D_MODEL, IN_CHUNK), F32), jax.ShapeDtypeStruct((128, 512), F32),
                   jax.ShapeDtypeStruct((128, D_MODEL), F32), jax.ShapeDtypeStruct((SB_ROWS, SB_W), F32)),
        in_specs=[pl.BlockSpec(memory_space=pltpu.VMEM)] * len(args),
        out_specs=(pl.BlockSpec(memory_space=pltpu.VMEM),) * 4,
        scratch_shapes=[
            pltpu.VMEM((4, D_MODEL, IN_CHUNK), BF16),
            half_chunk(2, BF16), half_chunk(2, BF16), half_chunk(2, F32), half_chunk(2, F32), half_chunk(2, BF16), half_chunk(2, BF16),
            pltpu.VMEM((half, IN_CHUNK), BF16), pltpu.VMEM((half, IN_CHUNK), BF16),
            pltpu.VMEM((half, IN_CHUNK), BF16), pltpu.VMEM((half, IN_CHUNK), BF16),
            pltpu.VMEM((SB_ROWS, SB_W), F32),
            pltpu.VMEM((N_DEV, SB_ROWS, SB_W), F32),
            pltpu.SemaphoreType.DMA((17,)),
            pltpu.SemaphoreType.DMA((17,)),
        ],
        compiler_params=pltpu.CompilerParams(vmem_limit_bytes=VMEM_LIMIT),
    )(*args)


def _fwd_proj(x, norm_g, w_my):
    s_len = x.shape[0]

    def body(x_ref, g_ref, w_ref, proj_ref, h_ref):
        xv = x_ref[...]
        h = (xv * _rms(xv) * g_ref[...]).astype(BF16)
        h_ref[...] = h
        proj_ref[...] = _dot(h, w_ref[...])

    return pl.pallas_call(
        body,
        name="fwd_proj",
        grid=(s_len // TM,),
        in_specs=[pl.BlockSpec((TM, D_MODEL), lambda i: (i, 0)), _full((1, D_MODEL)), _full((D_MODEL, MY_WIDTH))],
        out_specs=[pl.BlockSpec((TM, MY_WIDTH), lambda i: (i, 0)), pl.BlockSpec((TM, D_MODEL), lambda i: (i, 0))],
        out_shape=[jax.ShapeDtypeStruct((s_len, MY_WIDTH), F32), jax.ShapeDtypeStruct((s_len, D_MODEL), BF16)],
        compiler_params=_params(("parallel",)),
    )(x, norm_g, w_my)


def _log_sigmoid(z):
    return jnp.minimum(z, 0.0) - jnp.log(1.0 + jnp.exp(-jnp.abs(z)))


def _fox_prep(proj, bf_pad, gq, gk):
    s_len = proj.shape[0]

    def body(q_ref, k_ref, v_ref, f_ref, bf_ref, gq_ref, gk_ref, qa_ref, ka_ref, vh_ref, kt_ref, vt_ref, carry_ref):
        @pl.when(pl.program_id(0) == 0)
        def _():
            carry_ref[...] = jnp.zeros((1, 128), F32)

        lane = lax.broadcasted_iota(jnp.int32, (TM, 128), 1)
        logf = jnp.where(lane < FOX_HEADS, _log_sigmoid(f_ref[...] + bf_ref[...]), 0.0)
        row = lax.broadcasted_iota(jnp.int32, (TM, TM), 0)
        col = lax.broadcasted_iota(jnp.int32, (TM, TM), 1)
        tri = jnp.where(col <= row, 1.0, 0.0).astype(BF16)
        hi, mid, lo = _split3(logf)
        cum = _dot(tri, hi) + _dot(tri, mid) + _dot(tri, lo) + carry_ref[...]
        carry_ref[...] = cum[TM - 1:TM, :]
        f_hi, f_mid, f_lo = [t.astype(F32) for t in _split3(cum)]
        zeros = jnp.zeros((TM, HEAD_DIM), F32)
        for h in range(FOX_HEADS):
            sl = slice(HEAD_DIM * h, HEAD_DIM * (h + 1))
            qh = q_ref[:, sl]
            kh = k_ref[:, sl]
            qn = qh * _rms(qh) * gq_ref[...] * 0.125
            kn = kh * _rms(kh) * gk_ref[...]
            a, b, c = f_hi[:, h:h + 1], f_mid[:, h:h + 1], f_lo[:, h:h + 1]
            qa = jnp.where(lane < 64, jnp.concatenate([qn, zeros], axis=1),
                           jnp.where(lane == 64, a, jnp.where(lane == 65, b, jnp.where(lane == 66, c,
                                                                                         jnp.where(lane < 70, 1.0, 0.0)))))
            ka = jnp.where(lane < 64, jnp.concatenate([kn, zeros], axis=1),
                           jnp.where(lane < 67, 1.0, jnp.where(lane == 67, -a, jnp.where(lane == 68, -b,
                                                                                          jnp.where(lane == 69, -c, 0.0)))))
            vh = v_ref[:, sl]
            v_aug = jnp.where(lane < 64, jnp.concatenate([vh, zeros], axis=1), jnp.where(lane == 64, 1.0, 0.0))
            qa_ref[h] = qa.astype(BF16)
            ka_ref[h] = ka.astype(BF16)
            vh_ref[h] = vh.astype(BF16)
            kt_ref[h, 0] = ka.T.astype(BF16)
            vt_ref[h, 0] = v_aug.T.astype(BF16)

    col_blk = lambda j: pl.BlockSpec((TM, PIECE), lambda i: (i, j))
    head_blk = lambda w: pl.BlockSpec((FOX_HEADS, TM, w), lambda i: (0, i, 0))
    tile_blk = pl.BlockSpec((FOX_HEADS, 1, 128, TM), lambda i: (0, i, 0, 0))
    tiled = jax.ShapeDtypeStruct((FOX_HEADS, s_len // TM, 128, TM), BF16)
    return pl.pallas_call(
        body,
        name="fox_prep",
        grid=(s_len // TM,),
        in_specs=[col_blk(1), col_blk(2), col_blk(3), pl.BlockSpec((TM, 128), lambda i: (i, MY_F_OFF // 128)),
                  _full((1, 128)), _full((1, HEAD_DIM)), _full((1, HEAD_DIM))],
        out_specs=[head_blk(128), head_blk(128), head_blk(HEAD_DIM), tile_blk, tile_blk],
        out_shape=[jax.ShapeDtypeStruct((FOX_HEADS, s_len, 128), BF16), jax.ShapeDtypeStruct((FOX_HEADS, s_len, 128), BF16),
                   jax.ShapeDtypeStruct((FOX_HEADS, s_len, HEAD_DIM), BF16), tiled, tiled],
        scratch_shapes=[pltpu.VMEM((1, 128), F32)],
        compiler_params=_params(("arbitrary",)),
    )(proj, proj, proj, proj, bf_pad, gq, gk)


def _mem_prep(mem, g, w_kv, gk):
    def body(mem_ref, g_ref, w_ref, gk_ref, km_ref, vm_ref):
        m = mem_ref[...]
        kv = _dot((m * _rms(m) * g_ref[...]).astype(BF16), w_ref[...])
        for h in range(MEM_HEADS):
            sl = slice(HEAD_DIM * h, HEAD_DIM * (h + 1))
            kh = kv[:, sl]
            km_ref[:, sl] = (kh * _rms(kh) * gk_ref[...]).astype(BF16)
        vm_ref[...] = kv[:, 256:512].astype(BF16)

    return pl.pallas_call(
        body,
        name="mem_prep",
        out_shape=(jax.ShapeDtypeStruct((N_MEM, 256), BF16),) * 2,
        in_specs=[pl.BlockSpec(memory_space=pltpu.VMEM)] * 4,
        out_specs=(pl.BlockSpec(memory_space=pltpu.VMEM),) * 2,
        compiler_params=pltpu.CompilerParams(vmem_limit_bytes=VMEM_LIMIT),
    )(mem, g, w_kv, gk)


def _causal_mask():
    row = lax.broadcasted_iota(jnp.int32, (TA, TA), 0)
    col = lax.broadcasted_iota(jnp.int32, (TA, TA), 1)
    return col <= row


def _causal_mask_t():
    row = lax.broadcasted_iota(jnp.int32, (TA, TA), 0)
    col = lax.broadcasted_iota(jnp.int32, (TA, TA), 1)
    return row <= col


def _fox_fwd(q_aug, k_aug, vt_aug, w_kv16, w_out16):
    s_len = q_aug.shape[1]
    n_tiles = s_len // TA
    n_groups = FOX_HEADS // HB

    def body(q_ref, k_ref, vt_ref, kv16_ref, out16_ref, o_ref, st_ref, kv_all, out_all, send_sems, recv_sems, local_sems):
        qi = pl.program_id(1)
        remote, local = _row_block_exchange(kv16_ref, out16_ref, kv_all, out_all, send_sems, recv_sems, local_sems, gather=True)

        @pl.when((pl.program_id(0) == 0) & (qi == 0))
        def _():
            for cp in remote + local:
                cp.start()

        qs = [q_ref[h] for h in range(HB)]

        def step(t0, carry, masked, width):
            rows = pl.ds(pl.multiple_of(t0 * TA, TA), width * TA)
            scores = [_dot_nt(k_ref[h, rows, :], qs[h]) for h in range(HB)]
            soft = []
            for h in range(HB):
                m, _ = carry[h]
                s = jnp.where(_causal_mask_t(), scores[h], -1e30) if masked else scores[h]
                m_new = jnp.maximum(m, jnp.max(s, axis=0, keepdims=True))
                soft.append((m_new, jnp.exp(m - m_new), jnp.exp(s - m_new).astype(BF16)))
            out = []
            for h, (m_new, alpha, p) in enumerate(soft):
                acc = alpha * carry[h][1]
                for t in range(width):
                    acc = acc + _dot(vt_ref[h, t0 + t], p[t * TA:(t + 1) * TA])
                out.append((m_new, acc))
            return tuple(out)

        init = tuple((jnp.full((1, TA), -1e30, F32), jnp.zeros((128, TA), F32)) for _ in range(HB))
        carry = lax.fori_loop(0, qi // 2, lambda j, cr: step(2 * j, cr, False, 2), init)
        carry = lax.fori_loop(2 * (qi // 2), qi, lambda j, cr: step(j, cr, False, 1), carry)
        carry = step(qi, carry, True, 1)
        for h in range(HB):
            m, acc = carry[h]
            l = acc[HEAD_DIM:HEAD_DIM + 1, :]
            lse = m + jnp.log(l)
            o_ref[h] = jnp.concatenate([acc[0:HEAD_DIM] / l, jnp.broadcast_to(lse, (HEAD_DIM, TA))], axis=0).T
            st_ref[h, 0] = jnp.broadcast_to(lse, (8, TA))

        @pl.when((pl.program_id(0) == n_groups - 1) & (qi == n_tiles - 1))
        def _():
            for cp in remote:
                cp.wait_recv()
            for cp in remote:
                cp.wait_send()
            for cp in local:
                cp.wait()

    hbm = pl.BlockSpec(memory_space=pl.ANY)
    return pl.pallas_call(
        body,
        name="fox_fwd",
        grid=(n_groups, n_tiles),
        in_specs=[pl.BlockSpec((HB, TA, 128), lambda g, i: (g, i, 0)), pl.BlockSpec((HB, s_len, 128), lambda g, i: (g, 0, 0)),
                  pl.BlockSpec((HB, n_tiles, 128, TA), lambda g, i: (g, 0, 0, 0)), hbm, hbm],
        out_specs=[pl.BlockSpec((HB, TA, 128), lambda g, i: (g, i, 0)), pl.BlockSpec((HB, 1, 8, TA), lambda g, i: (g, i, 0, 0)),
                   hbm, hbm],
        out_shape=[jax.ShapeDtypeStruct((FOX_HEADS, s_len, 128), F32), jax.ShapeDtypeStruct((FOX_HEADS, n_tiles, 8, TA), F32),
                   jax.ShapeDtypeStruct((D_MODEL, 512), BF16), jax.ShapeDtypeStruct((D_MODEL, D_MODEL), BF16)],
        scratch_shapes=[pltpu.SemaphoreType.DMA((14,)), pltpu.SemaphoreType.DMA((14,)), pltpu.SemaphoreType.DMA((2,))],
        compiler_params=_params(("arbitrary", "arbitrary")),
    )(q_aug, k_aug, vt_aug, w_kv16, w_out16)


def _lane_group(lane, a, b, c, d):
    return jnp.where(lane < 64, a, jnp.where(lane < 128, b, jnp.where(lane < 192, c, d)))


def _pool_count(row0):
    lane = lax.broadcasted_iota(jnp.int32, (TM, POOL_WIDTH), 1)
    t1 = row0 + lax.broadcasted_iota(jnp.int32, (TM, POOL_WIDTH), 0) + 1
    return jnp.minimum(t1, _lane_group(lane, 2, 4, 8, 16)).astype(F32), lane


def _pool_delta(u, halo, cnt, lane):
    xe = jnp.concatenate([halo, u], axis=0)
    s2 = xe + pltpu.roll(xe, 1, 0)
    s4 = s2 + pltpu.roll(s2, 2, 0)
    s8 = s4 + pltpu.roll(s4, 4, 0)
    s16 = s8 + pltpu.roll(s8, 8, 0)
    win = _lane_group(lane, s2[HALO:], s4[HALO:], s8[HALO:], s16[HALO:])
    return win / cnt - u


def _pool_delta_bwd(dd, dd_next, cnt, cnt_next, lane):
    ee = jnp.concatenate([dd / cnt, dd_next / cnt_next], axis=0)
    n = TM + HALO
    r2 = ee + pltpu.roll(ee, n - 1, 0)
    r4 = r2 + pltpu.roll(r2, n - 2, 0)
    r8 = r4 + pltpu.roll(r4, n - 4, 0)
    r16 = r8 + pltpu.roll(r8, n - 8, 0)
    return _lane_group(lane, r2[:TM], r4[:TM], r8[:TM], r16[:TM]) - dd


def _mem_attn(qm, gq, km_ref, vm_ref):
    heads = []
    for h in range(MEM_HEADS):
        sl = slice(HEAD_DIM * h, HEAD_DIM * (h + 1))
        qh = qm[:, sl]
        r = _rms(qh)
        a = qh * r
        q16 = (a * gq * 0.125).astype(BF16)
        s = _dot_nt(q16, km_ref[:, sl])
        e = jnp.exp(s - jnp.max(s, axis=-1, keepdims=True))
        p = e / jnp.sum(e, axis=-1, keepdims=True)
        heads.append((a, r, q16, p, _dot(p.astype(BF16), vm_ref[:, sl])))
    return heads


def _row_specs(s_len):
    n_halo = s_len // HALO
    piece = lambda j: pl.BlockSpec((TM, PIECE), lambda i: (i, j))
    before = lambda j: pl.BlockSpec((HALO, 256), lambda i: (jnp.maximum(i * (TM // HALO) - 1, 0), j))
    after = lambda j: pl.BlockSpec((HALO, 256), lambda i: (jnp.minimum((i + 1) * (TM // HALO), n_halo - 1), j))
    return piece, before, after


def _mix_fwd(proj, olse, km, vm, wp_bd, pool_scale, gq_m):
    s_len = proj.shape[0]

    def body(ua_ref, halo_ref, gb_ref, qm_ref, ol_ref, km_ref, vm_ref, wp_ref, sc_ref, gq_ref, out_ref):
        i = pl.program_id(0)
        u = ua_ref[:, 0:256]
        g_a = ua_ref[:, 256:512]
        halo = jnp.where(i > 0, halo_ref[...], 0.0)
        cnt, lane = _pool_count(i * TM)
        d = _pool_delta(u, halo, cnt, lane).astype(BF16)
        y_a = _dot(d, wp_ref[...]) * sc_ref[...]
        out_ref[:, 0:256] = (y_a * (g_a * _sigmoid(g_a))).astype(BF16)
        for h in range(FOX_HEADS):
            sl = slice(HEAD_DIM * h, HEAD_DIM * (h + 1))
            g = gb_ref[:, sl]
            out_ref[:, 256 + HEAD_DIM * h:256 + HEAD_DIM * (h + 1)] = (ol_ref[h][:, 0:HEAD_DIM] * (g * _sigmoid(g))).astype(BF16)
        heads = _mem_attn(qm_ref[:, 0:256], gq_ref[...], km_ref, vm_ref)
        for h in range(MEM_HEADS):
            g = qm_ref[:, 256 + HEAD_DIM * h:256 + HEAD_DIM * (h + 1)]
            out_ref[:, 768 + HEAD_DIM * h:768 + HEAD_DIM * (h + 1)] = (heads[h][4] * (g * _sigmoid(g))).astype(BF16)

    piece, before, _ = _row_specs(s_len)
    return pl.pallas_call(
        body,
        name="mix_fwd",
        grid=(s_len // TM,),
        in_specs=[piece(0), before(0), piece(4), piece(5), pl.BlockSpec((FOX_HEADS, TM, 128), lambda i: (0, i, 0)),
                  _full((N_MEM, 256)), _full((N_MEM, 256)), _full((256, 256)), _full((1, 256)), _full((1, HEAD_DIM))],
        out_specs=pl.BlockSpec((TM, D_MODEL), lambda i: (i, 0)),
        out_shape=jax.ShapeDtypeStruct((s_len, D_MODEL), BF16),
        compiler_params=_params(("parallel",)),
    )(proj, proj, proj, proj, olse, km, vm, wp_bd, pool_scale, gq_m)


def _out_loss(mixed, x, target, w_out):
    s_len = x.shape[0]

    def body(mix_ref, x_ref, t_ref, w_ref, dout_ref, dmix_ref, dw16_ref, loss_ref, dw_ref):
        @pl.when(pl.program_id(0) == 0)
        def _():
            dw_ref[...] = jnp.zeros((D_MODEL, D_MODEL), F32)
            loss_ref[...] = jnp.zeros((8, 128), F32)

        mixed16 = mix_ref[...]
        err = x_ref[...] + _dot(mixed16, w_ref[...]) - t_ref[...]
        loss_ref[...] += 0.5 * jnp.sum(jnp.mean(err * err, axis=-1, keepdims=True))
        d_out = err / float(D_MODEL)
        dout_ref[...] = d_out
        d16 = d_out.astype(BF16)
        dmix_ref[...] = _dot_nt(d16, w_ref[...])
        dw_ref[...] += _dot_tn(mixed16, d16)

        @pl.when(pl.program_id(0) == pl.num_programs(0) - 1)
        def _():
            dw16_ref[...] = dw_ref[...].astype(BF16)

    row = pl.BlockSpec((TM, D_MODEL), lambda i: (i, 0))
    return pl.pallas_call(
        body,
        name="out_loss",
        grid=(s_len // TM,),
        in_specs=[row, row, row, _full((D_MODEL, D_MODEL))],
        out_specs=[row, row, _full((D_MODEL, D_MODEL)), _full((8, 128))],
        out_shape=[jax.ShapeDtypeStruct((s_len, D_MODEL), F32), jax.ShapeDtypeStruct((s_len, D_MODEL), F32),
                   jax.ShapeDtypeStruct((D_MODEL, D_MODEL), BF16), jax.ShapeDtypeStruct((8, 128), F32)],
        scratch_shapes=[pltpu.VMEM((D_MODEL, D_MODEL), F32)],
        compiler_params=_params(("arbitrary",)),
    )(mixed, x, target, w_out)


def _silu_pair(g):
    s = _sigmoid(g)
    return g * s, s * (1.0 + g * (1.0 - s))


def _mix_bwd(proj, olse, d_mixed, km, vm, wp_bd, pool_scale, gq_m):
    s_len = proj.shape[0]
    n_tiles = s_len // TM

    def body(ua_ref, halo_ref, ga_next_ref, gb_ref, qm_ref, ol_ref, dm_ref, dm_next_ref, km_ref, vm_ref, wp_ref, sc_ref, gq_ref,
             dua_ref, dgb_ref, dqm_ref, do_ref, dl_ref, dwp_ref, dsc_ref, dkm_ref, dvm_ref, dgq_ref):
        i = pl.program_id(0)

        @pl.when(i == 0)
        def _():
            dwp_ref[...] = jnp.zeros((256, 256), F32)
            dsc_ref[...] = jnp.zeros((1, 256), F32)
            dkm_ref[...] = jnp.zeros((N_MEM, 256), F32)
            dvm_ref[...] = jnp.zeros((N_MEM, 256), F32)
            dgq_ref[...] = jnp.zeros((1, HEAD_DIM), F32)

        u = ua_ref[:, 0:256]
        g_a = ua_ref[:, 256:512]
        halo = jnp.where(i > 0, halo_ref[...], 0.0)
        cnt, lane = _pool_count(i * TM)
        d16 = _pool_delta(u, halo, cnt, lane).astype(BF16)
        t = _dot(d16, wp_ref[...])
        y_a = t * sc_ref[...]
        silu_a, dsilu_a = _silu_pair(g_a)
        dm_a = dm_ref[:, 0:256]
        dy_a = dm_a * silu_a
        dsc_ref[...] += jnp.sum(dy_a * t, axis=0, keepdims=True)
        dt16 = (dy_a * sc_ref[...]).astype(BF16)
        dwp_ref[...] += _dot_tn(d16, dt16)
        dd = _dot_nt(dt16, wp_ref[...])
        g_next = ga_next_ref[...]
        dt_next = (dm_next_ref[...] * (g_next * _sigmoid(g_next)) * sc_ref[...]).astype(BF16)
        dd_next = jnp.where(i < n_tiles - 1, _dot_nt(dt_next, wp_ref[...]), 0.0)
        cnt_next = _pool_count((i + 1) * TM)[0][0:HALO]
        dua_ref[:, 0:256] = _pool_delta_bwd(dd, dd_next, cnt, cnt_next, lane).astype(BF16)
        dua_ref[:, 256:512] = (dm_a * y_a * dsilu_a).astype(BF16)

        ones = jnp.ones((8, HEAD_DIM), BF16)
        for h in range(FOX_HEADS):
            sl = slice(HEAD_DIM * h, HEAD_DIM * (h + 1))
            silu_b, dsilu_b = _silu_pair(gb_ref[:, sl])
            dm_b = dm_ref[:, 256 + HEAD_DIM * h:256 + HEAD_DIM * (h + 1)]
            o_h = ol_ref[h][:, 0:HEAD_DIM]
            d_o = dm_b * silu_b
            do_ref[h] = d_o.astype(BF16)
            dgb_ref[:, sl] = (dm_b * o_h * dsilu_b).astype(BF16)
            prod = d_o * o_h
            hi = prod.astype(BF16)
            lo = (prod - hi.astype(F32)).astype(BF16)
            dl_ref[h, 0] = _dot_nt(ones, hi) + _dot_nt(ones, lo)

        heads = _mem_attn(qm_ref[:, 0:256], gq_ref[...], km_ref, vm_ref)
        dgq = jnp.zeros((TM, HEAD_DIM), F32)
        for h in range(MEM_HEADS):
            sl = slice(HEAD_DIM * h, HEAD_DIM * (h + 1))
            a, r, q16, p, y_m = heads[h]
            silu_m, dsilu_m = _silu_pair(qm_ref[:, 256 + HEAD_DIM * h:256 + HEAD_DIM * (h + 1)])
            dm_m = dm_ref[:, 768 + HEAD_DIM * h:768 + HEAD_DIM * (h + 1)]
            dqm_ref[:, 256 + HEAD_DIM * h:256 + HEAD_DIM * (h + 1)] = (dm_m * y_m * dsilu_m).astype(BF16)
            dy16 = (dm_m * silu_m).astype(BF16)
            dp = _dot_nt(dy16, vm_ref[:, sl])
            dvm_ref[:, sl] += _dot_tn(p.astype(BF16), dy16)
            ds16 = (p * (dp - jnp.sum(dp * p, axis=-1, keepdims=True))).astype(BF16)
            dkm_ref[:, sl] += _dot_tn(ds16, q16)
            dq, dg_rows = _rms_bwd(a, r, gq_ref[...], _dot(ds16, km_ref[:, sl]) * 0.125)
            dqm_ref[:, sl] = dq.astype(BF16)
            dgq = dgq + dg_rows
        dgq_ref[...] += jnp.sum(dgq, axis=0, keepdims=True)

    piece, before, after = _row_specs(s_len)
    n_halo = s_len // HALO
    row = pl.BlockSpec((TM, D_MODEL), lambda i: (i, 0))
    dm_after = pl.BlockSpec((HALO, 256), lambda i: (jnp.minimum((i + 1) * (TM // HALO), n_halo - 1), 0))
    out_piece = pl.BlockSpec((TM, PIECE), lambda i: (i, 0))
    return pl.pallas_call(
        body,
        name="mix_bwd",
        grid=(n_tiles,),
        in_specs=[piece(0), before(0), after(1), piece(4), piece(5), pl.BlockSpec((FOX_HEADS, TM, 128), lambda i: (0, i, 0)),
                  row, dm_after, _full((N_MEM, 256)), _full((N_MEM, 256)), _full((256, 256)), _full((1, 256)), _full((1, HEAD_DIM))],
        out_specs=[out_piece, out_piece, out_piece, pl.BlockSpec((FOX_HEADS, TM, HEAD_DIM), lambda i: (0, i, 0)),
                   pl.BlockSpec((FOX_HEADS, 1, 8, TM), lambda i: (0, i, 0, 0)),
                   _full((256, 256)), _full((1, 256)), _full((N_MEM, 256)), _full((N_MEM, 256)), _full((1, HEAD_DIM))],
        out_shape=[jax.ShapeDtypeStruct((s_len, PIECE), BF16)] * 3 + [
            jax.ShapeDtypeStruct((FOX_HEADS, s_len, HEAD_DIM), BF16),
            jax.ShapeDtypeStruct((FOX_HEADS, n_tiles, 8, TM), F32),
            jax.ShapeDtypeStruct((256, 256), F32), jax.ShapeDtypeStruct((1, 256), F32),
            jax.ShapeDtypeStruct((N_MEM, 256), F32), jax.ShapeDtypeStruct((N_MEM, 256), F32),
            jax.ShapeDtypeStruct((1, HEAD_DIM), F32)],
        compiler_params=_params(("arbitrary",)),
    )(proj, proj, proj, proj, proj, olse, d_mixed, d_mixed, km, vm, wp_bd, pool_scale, gq_m)


def _fox_bwd(q_aug, k_aug, kt_aug, v_h, d_o, lse_rows, delta_rows, dw_kv16, dw_out16):
    s_len = q_aug.shape[1]
    n_tiles = s_len // TA
    n_groups = FOX_HEADS // HB

    def body(q_ref, k_ref, kt_ref, v_ref, do_ref, st_ref, dl_ref, dkv16_ref, dout16_ref, dqt_ref, dk_ref, dv_ref, land_kv, land_out,
             send_sems, recv_sems, local_sems):
        j = pl.program_id(1)
        remote, local = _row_block_exchange(dkv16_ref, dout16_ref, land_kv, land_out, send_sems, recv_sems, local_sems, gather=False)

        @pl.when((pl.program_id(0) == 0) & (j == 0))
        def _():
            for cp in remote + local:
                cp.start()

        @pl.when(j == 0)
        def _():
            dqt_ref[...] = jnp.zeros((HB, n_tiles, 128, TA), F32)

        ks = [k_ref[h] for h in range(HB)]
        kts = [kt_ref[h, 0] for h in range(HB)]
        vs = [v_ref[h] for h in range(HB)]

        def pair(i, acc, masked):
            rows = pl.ds(pl.multiple_of(i * TA, TA), TA)
            qs = [q_ref[h, rows, :] for h in range(HB)]
            d_os = [do_ref[h, rows, :] for h in range(HB)]
            scores = [(_dot_nt(ks[h], qs[h]), _dot_nt(vs[h], d_os[h])) for h in range(HB)]
            probs = []
            for h in range(HB):
                s, dp = scores[h]
                if masked:
                    s = jnp.where(_causal_mask_t(), s, -1e30)
                p = jnp.exp(s - st_ref[h, i, 0:1, :])
                probs.append((p.astype(BF16), (p * (dp - dl_ref[h, i, 0:1, :])).astype(BF16)))
            out = []
            for h in range(HB):
                p16, ds16 = probs[h]
                dqt_ref[h, i] += _dot(kts[h], ds16)
                out.append((acc[h][0] + _dot(ds16, qs[h]), acc[h][1] + _dot(p16, d_os[h])))
            return tuple(out)

        zero = tuple((jnp.zeros((TA, 128), F32), jnp.zeros((TA, HEAD_DIM), F32)) for _ in range(HB))
        acc = pair(j, zero, True)
        acc = lax.fori_loop(j + 1, n_tiles, lambda i, a: pair(i, a, False), acc)
        for h in range(HB):
            dk_ref[h] = acc[h][0]
            dv_ref[h] = acc[h][1]

        @pl.when((pl.program_id(0) == n_groups - 1) & (j == n_tiles - 1))
        def _():
            for cp in remote:
                cp.wait_recv()
            for cp in remote:
                cp.wait_send()
            for cp in local:
                cp.wait()

    whole = lambda w: pl.BlockSpec((HB, s_len, w), lambda g, j: (g, 0, 0))
    tile = lambda w: pl.BlockSpec((HB, TA, w), lambda g, j: (g, j, 0))
    rows_blk = lambda r: pl.BlockSpec((HB, n_tiles, r, TA), lambda g, j: (g, 0, 0, 0))
    hbm = pl.BlockSpec(memory_space=pl.ANY)
    return pl.pallas_call(
        body,
        name="fox_bwd",
        grid=(n_groups, n_tiles),
        in_specs=[whole(128), tile(128), pl.BlockSpec((HB, 1, 128, TA), lambda g, j: (g, j, 0, 0)), tile(HEAD_DIM),
                  whole(HEAD_DIM), rows_blk(8), rows_blk(8), hbm, hbm],
        out_specs=[rows_blk(128), tile(128), tile(HEAD_DIM), hbm, hbm],
        out_shape=[jax.ShapeDtypeStruct((FOX_HEADS, n_tiles, 128, TA), F32), jax.ShapeDtypeStruct((FOX_HEADS, s_len, 128), F32),
                   jax.ShapeDtypeStruct((FOX_HEADS, s_len, HEAD_DIM), F32),
                   jax.ShapeDtypeStruct((N_DEV, 128, 512), BF16), jax.ShapeDtypeStruct((N_DEV, 128, D_MODEL), BF16)],
        scratch_shapes=[pltpu.SemaphoreType.DMA((14,)), pltpu.SemaphoreType.DMA((14,)), pltpu.SemaphoreType.DMA((2,))],
        compiler_params=_params(("arbitrary", "arbitrary")),
    )(q_aug, k_aug, kt_aug, v_h, d_o, lse_rows, delta_rows, dw_kv16, dw_out16)


def _fox_post(proj, bf_pad, gq, gk, dq_aug, dk_aug, dv_h):
    s_len = proj.shape[0]
    n_tiles = s_len // TM

    def body(q_ref, k_ref, f_ref, bf_ref, gq_ref, gk_ref, dqa_ref, dka_ref, dvh_ref,
             dq_ref, dk_ref, dv_ref, df_ref, dgq_ref, dgk_ref, dbf_ref, carry_ref):
        @pl.when(pl.program_id(0) == 0)
        def _():
            carry_ref[...] = jnp.zeros((1, 128), F32)
            dgq_ref[...] = jnp.zeros((1, HEAD_DIM), F32)
            dgk_ref[...] = jnp.zeros((1, HEAD_DIM), F32)
            dbf_ref[...] = jnp.zeros((1, 128), F32)

        lane = lax.broadcasted_iota(jnp.int32, (TM, 128), 1)
        d_cum = jnp.zeros((TM, 128), F32)
        dgq = jnp.zeros((TM, HEAD_DIM), F32)
        dgk = jnp.zeros((TM, HEAD_DIM), F32)
        for h in range(FOX_HEADS):
            sl = slice(HEAD_DIM * h, HEAD_DIM * (h + 1))
            dqa = dqa_ref[h, 0].T
            dka = dka_ref[h]
            qh = q_ref[:, sl]
            kh = k_ref[:, sl]
            rq = _rms(qh)
            rk = _rms(kh)
            dq, dgq_rows = _rms_bwd(qh * rq, rq, gq_ref[...], dqa[:, 0:HEAD_DIM] * 0.125)
            dk, dgk_rows = _rms_bwd(kh * rk, rk, gk_ref[...], dka[:, 0:HEAD_DIM])
            dq_ref[:, sl] = dq.astype(BF16)
            dk_ref[:, sl] = dk.astype(BF16)
            dv_ref[:, sl] = dvh_ref[h].astype(BF16)
            dgq = dgq + dgq_rows
            dgk = dgk + dgk_rows
            d_cum = jnp.where(lane == h, dqa[:, 64:65] - dka[:, 67:68], d_cum)
        dgq_ref[...] += jnp.sum(dgq, axis=0, keepdims=True)
        dgk_ref[...] += jnp.sum(dgk, axis=0, keepdims=True)

        row = lax.broadcasted_iota(jnp.int32, (TM, TM), 0)
        col = lax.broadcasted_iota(jnp.int32, (TM, TM), 1)
        tri = jnp.where(col >= row, 1.0, 0.0).astype(BF16)
        hi, mid, lo = _split3(d_cum)
        d_logf = _dot(tri, hi) + _dot(tri, mid) + _dot(tri, lo) + carry_ref[...]
        carry_ref[...] = d_logf[0:1, :]
        z = f_ref[...] + bf_ref[...]
        d_f = jnp.where(lane < FOX_HEADS, d_logf * _sigmoid(-z), 0.0)
        df_ref[...] = d_f.astype(BF16)
        dbf_ref[...] += jnp.sum(d_f, axis=0, keepdims=True)

    rev = lambda i: n_tiles - 1 - i
    col_blk = lambda j: pl.BlockSpec((TM, PIECE), lambda i: (rev(i), j))
    head_blk = lambda w: pl.BlockSpec((FOX_HEADS, TM, w), lambda i: (0, rev(i), 0))
    out_piece = pl.BlockSpec((TM, PIECE), lambda i: (rev(i), 0))
    return pl.pallas_call(
        body,
        name="fox_post",
        grid=(n_tiles,),
        in_specs=[col_blk(1), col_blk(2), pl.BlockSpec((TM, 128), lambda i: (rev(i), MY_F_OFF // 128)),
                  _full((1, 128)), _full((1, HEAD_DIM)), _full((1, HEAD_DIM)),
                  pl.BlockSpec((FOX_HEADS, 1, 128, TM), lambda i: (0, rev(i), 0, 0)), head_blk(128), head_blk(HEAD_DIM)],
        out_specs=[out_piece, out_piece, out_piece, pl.BlockSpec((TM, 128), lambda i: (rev(i), 0)),
                   _full((1, HEAD_DIM)), _full((1, HEAD_DIM)), _full((1, 128))],
        out_shape=[jax.ShapeDtypeStruct((s_len, PIECE), BF16)] * 3 + [
            jax.ShapeDtypeStruct((s_len, 128), BF16), jax.ShapeDtypeStruct((1, HEAD_DIM), F32),
            jax.ShapeDtypeStruct((1, HEAD_DIM), F32), jax.ShapeDtypeStruct((1, 128), F32)],
        scratch_shapes=[pltpu.VMEM((1, 128), F32)],
        compiler_params=_params(("arbitrary",)),
    )(proj, proj, proj, bf_pad, gq, gk, dq_aug, dk_aug, dv_h)


def _mem_bwd(mem, g, w_kv, gk, dkm, dvm):
    def body(mem_ref, g_ref, w_ref, gk_ref, dkm_ref, dvm_ref, dw_ref, dg_ref, dgk_ref, dkv_ref):
        m = mem_ref[...]
        r = _rms(m)
        a = m * r
        mn16 = (a * g_ref[...]).astype(BF16)
        kv = _dot(mn16, w_ref[...])
        dgk = jnp.zeros((N_MEM, HEAD_DIM), F32)
        for h in range(MEM_HEADS):
            sl = slice(HEAD_DIM * h, HEAD_DIM * (h + 1))
            kh = kv[:, sl]
            rk = _rms(kh)
            dk, dg_rows = _rms_bwd(kh * rk, rk, gk_ref[...], dkm_ref[:, sl])
            dkv_ref[:, sl] = dk.astype(BF16)
            dgk = dgk + dg_rows
        dkv_ref[:, 256:512] = dvm_ref[...].astype(BF16)
        dgk_ref[...] = jnp.sum(dgk, axis=0, keepdims=True)
        dkv16 = dkv_ref[...]
        dw_ref[...] = _dot_tn(mn16, dkv16).astype(BF16)
        dg_ref[...] = jnp.sum(_dot_nt(dkv16, w_ref[...]) * a, axis=0, keepdims=True)

    return pl.pallas_call(
        body,
        name="mem_bwd",
        out_shape=(jax.ShapeDtypeStruct((D_MODEL, 512), BF16), jax.ShapeDtypeStruct((1, D_MODEL), F32),
                   jax.ShapeDtypeStruct((1, HEAD_DIM), F32)),
        in_specs=[pl.BlockSpec(memory_space=pltpu.VMEM)] * 6,
        out_specs=(pl.BlockSpec(memory_space=pltpu.VMEM),) * 3,
        scratch_shapes=[pltpu.VMEM((N_MEM, 512), BF16)],
        compiler_params=pltpu.CompilerParams(vmem_limit_bytes=VMEM_LIMIT),
    )(mem, g, w_kv, gk, dkm, dvm)


def _grad_x(pieces, d_f, w_my, x, norm_g, d_out):
    s_len = x.shape[0]

    def body(p0, p1, p2, p3, p4, p5, df_ref, w_ref, x_ref, g_ref, dout_ref, gx_ref, dg_ref):
        @pl.when(pl.program_id(0) == 0)
        def _():
            dg_ref[...] = jnp.zeros((1, D_MODEL), F32)

        dh = _dot_nt(df_ref[...], w_ref[:, MY_F_OFF:MY_WIDTH])
        for j, p in enumerate((p0, p1, p2, p3, p4, p5)):
            dh = dh + _dot_nt(p[...], w_ref[:, PIECE * j:PIECE * (j + 1)])
        xv = x_ref[...]
        r = _rms(xv)
        dx, dg_rows = _rms_bwd(xv * r, r, g_ref[...], dh)
        gx_ref[...] = dout_ref[...] + dx
        dg_ref[...] += jnp.sum(dg_rows, axis=0, keepdims=True)

    piece = pl.BlockSpec((TM, PIECE), lambda i: (i, 0))
    row = pl.BlockSpec((TM, D_MODEL), lambda i: (i, 0))
    return pl.pallas_call(
        body,
        name="grad_x",
        grid=(s_len // TM,),
        in_specs=[piece] * 6 + [pl.BlockSpec((TM, 128), lambda i: (i, 0)), _full((D_MODEL, MY_WIDTH)), row,
                                _full((1, D_MODEL)), row],
        out_specs=[row, _full((1, D_MODEL))],
        out_shape=[jax.ShapeDtypeStruct((s_len, D_MODEL), F32), jax.ShapeDtypeStruct((1, D_MODEL), F32)],
        compiler_params=_params(("arbitrary",)),
    )(*pieces, d_f, w_my, x, norm_g, d_out)


def _grad_w_in(h16, pieces, d_f):
    s_len = h16.shape[0]
    tk = 512

    def body(h_ref, p0, p1, p2, p3, p4, p5, df_ref, out_ref, dw_ref):
        @pl.when(pl.program_id(0) == 0)
        def _():
            dw_ref[...] = jnp.zeros((D_MODEL, MY_WIDTH), F32)

        h = h_ref[...]
        for j, p in enumerate((p0, p1, p2, p3, p4, p5)):
            dw_ref[:, PIECE * j:PIECE * (j + 1)] += _dot_tn(h, p[...])
        dw_ref[:, MY_F_OFF:MY_WIDTH] += _dot_tn(h, df_ref[...])

        @pl.when(pl.program_id(0) == pl.num_programs(0) - 1)
        def _():
            for d in range(N_DEV):
                parts = [dw_ref[:, start:start + width] for start, width in _chunk_segments(d)]
                out_ref[d] = (parts[0] if len(parts) == 1 else jnp.concatenate(parts, axis=1)).astype(BF16)

    piece = pl.BlockSpec((tk, PIECE), lambda i: (i, 0))
    return pl.pallas_call(
        body,
        name="grad_w_in",
        grid=(s_len // tk,),
        in_specs=[pl.BlockSpec((tk, D_MODEL), lambda i: (i, 0))] + [piece] * 6 + [pl.BlockSpec((tk, 128), lambda i: (i, 0))],
        out_specs=_full((N_DEV, D_MODEL, IN_CHUNK)),
        out_shape=jax.ShapeDtypeStruct((N_DEV, D_MODEL, IN_CHUNK), BF16),
        scratch_shapes=[pltpu.VMEM((D_MODEL, MY_WIDTH), F32)],
        compiler_params=_params(("arbitrary",)),
    )(h16, *pieces, d_f)


def _adamw(w, g, m, v):
    m = ADAM_B1 * m + (1.0 - ADAM_B1) * g
    v = ADAM_B2 * v + (1.0 - ADAM_B2) * (g * g)
    m_hat = m / (1.0 - ADAM_B1 ** ADAM_STEP)
    v_hat = v / (1.0 - ADAM_B2 ** ADAM_STEP)
    return -ADAM_LR * (m_hat / (jnp.sqrt(v_hat) + ADAM_EPS) + ADAM_WD * w), m, v


PARAM_ORDER = ("norm_g", "w_in", "b_f", "w_pool", "pool_scale", "fox_q_g", "fox_k_g", "mem_norm_g", "w_mem_kv", "mem_q_g", "mem_k_g", "w_out")


def _update(red_in, red_kv, red_out, sred, params):
    def body(rin_ref, rkv_ref, rout_ref, sred_ref, *refs):
        ins, outs = refs[:3 * len(PARAM_ORDER)], refs[3 * len(PARAM_ORDER):]
        wide = lambda row: jnp.concatenate([sred_ref[row + k:row + k + 1, :] for k in range(4)], axis=1)
        head = lambda row: sred_ref[row:row + 1, 0:HEAD_DIM]
        grads = {
            "norm_g": lambda: wide(SB_NORM_G), "w_in": lambda: rin_ref[...], "b_f": lambda: sred_ref[SB_B_F:SB_B_F + 1, 0:FOX_HEADS],
            "pool_scale": lambda: sred_ref[SB_POOL_SCALE:SB_POOL_SCALE + 1, :], "fox_q_g": lambda: head(SB_FOX_Q),
            "fox_k_g": lambda: head(SB_FOX_K), "mem_norm_g": lambda: wide(SB_MEM_NORM_G), "w_mem_kv": lambda: rkv_ref[...],
            "mem_q_g": lambda: head(SB_MEM_Q), "mem_k_g": lambda: head(SB_MEM_K), "w_out": lambda: rout_ref[...],
        }
        for p, name in enumerate(PARAM_ORDER):
            w_ref, m_ref, v_ref = ins[3 * p:3 * p + 3]
            g_out, d_out, m_out, v_out = outs[4 * p:4 * p + 4]
            if name == "w_pool":
                for grp in range(4):
                    g = sred_ref[SB_W_POOL:SB_W_POOL + 64, 64 * grp:64 * (grp + 1)]
                    delta, m_new, v_new = _adamw(w_ref[grp], g, m_ref[grp], v_ref[grp])
                    g_out[grp], d_out[grp], m_out[grp], v_out[grp] = g, delta, m_new, v_new
            else:
                g = grads[name]()
                delta, m_new, v_new = _adamw(w_ref[...], g, m_ref[...], v_ref[...])
                g_out[...], d_out[...], m_out[...], v_out[...] = g, delta, m_new, v_new

    flat = [t for name in PARAM_ORDER for t in params[name]]
    shapes = [params[name][0].shape for name in PARAM_ORDER for _ in range(4)]
    outs = pl.pallas_call(
        body,
        name="adamw_update",
        out_shape=tuple(jax.ShapeDtypeStruct(s, F32) for s in shapes),
        in_specs=[pl.BlockSpec(memory_space=pltpu.VMEM)] * (4 + len(flat)),
        out_specs=(pl.BlockSpec(memory_space=pltpu.VMEM),) * len(shapes),
        compiler_params=pltpu.CompilerParams(vmem_limit_bytes=VMEM_LIMIT),
    )(red_in, red_kv, red_out, sred, *flat)
    return {name: outs[4 * p:4 * p + 4] for p, name in enumerate(PARAM_ORDER)}


def kernel(x, mem, norm_g, w_in, b_f, w_pool, pool_scale, fox_q_g, fox_k_g, mem_norm_g, w_mem_kv, mem_q_g, mem_k_g, w_out, loss_target, m_norm_g, m_w_in, m_b_f, m_w_pool, m_pool_scale, m_fox_q_g, m_fox_k_g, m_mem_norm_g, m_w_mem_kv, m_mem_q_g, m_mem_k_g, m_w_out, v_norm_g, v_w_in, v_b_f, v_w_pool, v_pool_scale, v_fox_q_g, v_fox_k_g, v_mem_norm_g, v_w_mem_kv, v_mem_q_g, v_mem_k_g, v_w_out):
    given = dict(norm_g=(norm_g, m_norm_g, v_norm_g), w_in=(w_in, m_w_in, v_w_in), b_f=(b_f, m_b_f, v_b_f),
                 w_pool=(w_pool, m_w_pool, v_w_pool), pool_scale=(pool_scale, m_pool_scale, v_pool_scale),
                 fox_q_g=(fox_q_g, m_fox_q_g, v_fox_q_g), fox_k_g=(fox_k_g, m_fox_k_g, v_fox_k_g),
                 mem_norm_g=(mem_norm_g, m_mem_norm_g, v_mem_norm_g), w_mem_kv=(w_mem_kv, m_w_mem_kv, v_w_mem_kv),
                 mem_q_g=(mem_q_g, m_mem_q_g, v_mem_q_g), mem_k_g=(mem_k_g, m_mem_k_g, v_mem_k_g), w_out=(w_out, m_w_out, v_w_out))
    params = {name: tuple(t[0] if t.ndim > 2 else t for t in wmv) for name, wmv in given.items()}
    x2, mem2, target2 = x[0], mem[0], loss_target[0]

    w_my, w_kv16, w_out16, wp_bd, bf_pad = _gather_w_in(params["w_in"][0], params["w_mem_kv"][0], params["w_out"][0],
                                                         params["w_pool"][0], b_f)
    proj, h16 = _fwd_proj(x2, norm_g, w_my)
    q_aug, k_aug, v_h, kt_aug, vt_aug = _fox_prep(proj, bf_pad, fox_q_g, fox_k_g)
    olse, lse_rows, w_kv_all, w_out_all = _fox_fwd(q_aug, k_aug, vt_aug, w_kv16, w_out16)
    km, vm = _mem_prep(mem2, mem_norm_g, w_kv_all, mem_k_g)
    mixed = _mix_fwd(proj, olse, km, vm, wp_bd, pool_scale, mem_q_g)
    d_out, d_mixed, dw_out, loss_blk = _out_loss(mixed, x2, target2, w_out_all)

    d_ua, d_gb, d_qm, d_o, delta_rows, dwp_bd, d_scale, dkm, dvm, d_gqm = _mix_bwd(proj, olse, d_mixed, km, vm, wp_bd, pool_scale, mem_q_g)
    dw_kv, d_mem_g, d_gkm = _mem_bwd(mem2, mem_norm_g, w_kv_all, mem_k_g, dkm, dvm)
    dq_aug, dk_aug, dv_h, land_kv, land_out = _fox_bwd(q_aug, k_aug, kt_aug, v_h, d_o, lse_rows, delta_rows, dw_kv, dw_out)
    d_q, d_k, d_v, d_f, d_gq, d_gk, d_bf = _fox_post(proj, bf_pad, fox_q_g, fox_k_g, dq_aug, dk_aug, dv_h)
    pieces = (d_ua, d_q, d_k, d_v, d_gb, d_qm)
    dw_in = _grad_w_in(h16, pieces, d_f)
    grad_x, d_norm_g = _grad_x(pieces, d_f, w_my, x2, norm_g, d_out)

    red_in, red_kv, red_out, sred = _exchange_grads(dw_in, land_kv, land_out, d_norm_g, d_mem_g, d_scale, d_bf, d_gq, d_gk, d_gqm,
                                                    d_gkm, dwp_bd, loss_blk)
    new = _update(red_in, red_kv, red_out, sred, params)
    result = [sred[SB_LOSS, 0], grad_x[None]]
    for kind in range(4):
        for name in PARAM_ORDER:
            out = new[name][kind]
            result.append(out[None] if given[name][0].ndim > 2 else out)
    return tuple(result)
```

```python
import functools

import jax
import jax.numpy as jnp
from jax import lax
from jax.experimental import pallas as pl
from jax.experimental.pallas import tpu as pltpu

F32 = jnp.float32
BF16 = jnp.bfloat16
MESH = pl.DeviceIdType.MESH

N_DEV = 8
D_MODEL = 1024
HEAD_DIM = 64
FOX_HEADS = 8
MEM_HEADS = 4
N_MEM = 256
POOL_WIDTH = 256
EPS = 1e-6
IN_WIDTH = 3080
IN_CHUNK = IN_WIDTH // N_DEV
F_OFF = 2048
MY_WIDTH = 3200
MY_F_OFF = 3072
PIECE = 512
TM = 256
TMM = 512
TA = 256
HB = 4
HALO = 16
VMEM_LIMIT = 56 * 1024 * 1024

ADAM_LR = 0.001
ADAM_B1 = 0.9
ADAM_B2 = 0.999
ADAM_EPS = 1e-08
ADAM_WD = 0.01
ADAM_STEP = 10


def _dot(a, b):
    return jnp.dot(a, b, preferred_element_type=F32)


def _dot_nt(a, b):
    return lax.dot_general(a, b, (((1,), (1,)), ((), ())), preferred_element_type=F32)


def _dot_tn(a, b):
    return lax.dot_general(a, b, (((0,), (0,)), ((), ())), preferred_element_type=F32)


def _sigmoid(g):
    return 1.0 / (1.0 + jnp.exp(-g))


def _split3(v):
    hi = v.astype(BF16)
    r1 = v - hi.astype(F32)
    mid = r1.astype(BF16)
    lo = (r1 - mid.astype(F32)).astype(BF16)
    return hi, mid, lo


def _rms(v):
    return lax.rsqrt(jnp.mean(v * v, axis=-1, keepdims=True) + EPS)


def _rms_bwd(a, r, g, dy):
    da = dy * g
    return r * (da - a * jnp.mean(da * a, axis=-1, keepdims=True)), dy * a


def _params(sem, limit=VMEM_LIMIT):
    return pltpu.CompilerParams(dimension_semantics=sem, vmem_limit_bytes=limit)


def _full(shape):
    return pl.BlockSpec(shape, lambda *_: (0,) * len(shape))


def _chunk_segments(d):
    runs = []
    for lo, hi, shift in ((0, F_OFF, 0), (F_OFF, F_OFF + FOX_HEADS, MY_F_OFF - F_OFF), (F_OFF + FOX_HEADS, IN_WIDTH, -FOX_HEADS)):
        a, b = max(lo, IN_CHUNK * d), min(hi, IN_CHUNK * (d + 1))
        if a < b:
            runs.append((a + shift, b - a))
    return runs


def _my_place():
    x, y, c = lax.axis_index("x"), lax.axis_index("y"), lax.axis_index("c")
    return x, y, c, 4 * x + 2 * y + c


def _shard_rows(dev):
    return pl.ds(pl.multiple_of(128 * dev, 128), 128)


def _gather_w_in(w_in, w_kv, w_out, w_pool, b_f):
    def body(win_ref, wkv_ref, wout_ref, wp_ref, bf_ref, wmy_ref, kv16_ref, out16_ref, wpbd_ref, bfpad_ref,
             in_all, pay_in, send_sems, recv_sems):
        x, y, c, me = _my_place()
        sibling = (x, y, 1 - c)
        chips = [(1 - x, y), (x, 1 - y), (1 - x, 1 - y)]

        pay_in[...] = win_ref[...].astype(BF16)

        def copy(k, block, to, own=False):
            px, py, pc = block
            dst = in_all.at[4 * px + 2 * py + pc]
            return pltpu.make_async_remote_copy(src_ref=pay_in if own else dst, dst_ref=dst, send_sem=send_sems.at[k],
                                                recv_sem=recv_sems.at[k], device_id=to, device_id_type=MESH)

        first = [copy(0, (x, y, c), sibling, own=True)]
        first += [copy(1 + j, (x, y, c), (*chip, c), own=True) for j, chip in enumerate(chips)]
        for cp in first:
            cp.start()
        in_all[me] = pay_in[...]
        kv16_ref[...] = wkv_ref[...].astype(BF16)
        out16_ref[...] = wout_ref[...].astype(BF16)
        wpbd_ref[...] = jnp.zeros((POOL_WIDTH, POOL_WIDTH), BF16)
        for grp in range(4):
            sl = slice(64 * grp, 64 * (grp + 1))
            wpbd_ref[sl, sl] = wp_ref[grp].astype(BF16)
        bfpad_ref[...] = jnp.zeros((1, 128), F32)
        bfpad_ref[:, 0:FOX_HEADS] = bf_ref[...]
        passed = []
        for j, chip in enumerate(chips):
            copy(1 + j, (*chip, c), (x, y, c)).wait_recv()
            passed.append(copy(4 + j, (*chip, c), sibling))
            passed[-1].start()
        copy(0, sibling, (x, y, c)).wait_recv()
        for j, chip in enumerate(chips):
            copy(4 + j, (*chip, 1 - c), (x, y, c)).wait_recv()
        for cp in first + passed:
            cp.wait_send()

        for r0 in range(0, D_MODEL, 256):
            ch = [in_all[d, r0:r0 + 256, :].astype(F32) for d in range(N_DEV)]
            lo = F_OFF - 5 * IN_CHUNK
            full = jnp.concatenate(ch[:5] + [ch[5][:, :lo], ch[5][:, lo + FOX_HEADS:], ch[6], ch[7], ch[5][:, lo:lo + FOX_HEADS],
                                             jnp.zeros((256, MY_WIDTH - IN_WIDTH), F32)], axis=1)
            wmy_ref[r0:r0 + 256, :] = full.astype(BF16)

    return pl.pallas_call(
        body,
        name="gather_w_in",
        out_shape=(jax.ShapeDtypeStruct((D_MODEL, MY_WIDTH), BF16), jax.ShapeDtypeStruct((128, 512), BF16),
                   jax.ShapeDtypeStruct((128, D_MODEL), BF16), jax.ShapeDtypeStruct((POOL_WIDTH, POOL_WIDTH), BF16),
                   jax.ShapeDtypeStruct((1, 128), F32)),
        in_specs=[pl.BlockSpec(memory_space=pltpu.VMEM)] * 5,
        out_specs=(pl.BlockSpec(memory_space=pltpu.VMEM),) * 5,
        scratch_shapes=[
            pltpu.VMEM((N_DEV, D_MODEL, IN_CHUNK), BF16),
            pltpu.VMEM((D_MODEL, IN_CHUNK), BF16),
            pltpu.SemaphoreType.DMA((7,)),
            pltpu.SemaphoreType.DMA((7,)),
        ],
        compiler_params=pltpu.CompilerParams(vmem_limit_bytes=VMEM_LIMIT),
    )(w_in, w_kv, w_out, w_pool, b_f)


def _row_block_exchange(kv_ref, out_ref, kv_dst, out_dst, send_sems, recv_sems, local_sems, gather):
    x, y, c, me = _my_place()
    remote = []
    for k in range(1, N_DEV):
        px, py, pc = x ^ (k >> 2), y ^ ((k >> 1) & 1), c ^ (k & 1)
        peer = 4 * px + 2 * py + pc
        for fam, (src, dst) in enumerate(((kv_ref, kv_dst), (out_ref, out_dst))):
            remote.append(pltpu.make_async_remote_copy(
                src_ref=src if gather else src.at[_shard_rows(peer)], dst_ref=dst.at[_shard_rows(me)] if gather else dst.at[me],
                send_sem=send_sems.at[7 * fam + k - 1], recv_sem=recv_sems.at[7 * fam + k - 1],
                device_id=(px, py, pc), device_id_type=MESH))
    local = [pltpu.make_async_copy(src if gather else src.at[_shard_rows(me)], dst.at[_shard_rows(me)] if gather else dst.at[me],
                                   local_sems.at[fam]) for fam, (src, dst) in enumerate(((kv_ref, kv_dst), (out_ref, out_dst)))]
    return remote, local


SB_ROWS, SB_W = 80, 256
SB_NORM_G, SB_MEM_NORM_G, SB_POOL_SCALE, SB_B_F, SB_FOX_Q, SB_FOX_K, SB_MEM_Q, SB_MEM_K, SB_LOSS, SB_W_POOL = 0, 4, 8, 9, 10, 11, 12, 13, 14, 16


def _exchange_grads(dw_in, land_kv, land_out, d_norm_g, d_mem_g, d_scale, d_bf, d_gq, d_gk, d_gqm, d_gkm, dwp_bd, loss_blk):
    half = D_MODEL // 2

    def body(in_ref, lkv_ref, lout_ref, dng, dmg, dsc, dbf, dgq, dgk, dgqm, dgkm, dwp, loss_ref,
             rin_ref, rkv_ref, rout_ref, sred_ref, land1, send2a, send2b, keep2a, keep2b, land2a, land2b,
             send3a, send3b, land3a, land3b, sb_ref, sland, send_sems, recv_sems):
        x, y, c, me = _my_place()
        sibling, x_nbr, y_nbr = (x, y, 1 - c), (1 - x, y, c), (x, 1 - y, c)

        def rcopy(src, dst, sem, to):
            return pltpu.make_async_remote_copy(src_ref=src, dst_ref=dst, send_sem=send_sems.at[sem], recv_sem=recv_sems.at[sem],
                                                device_id=to, device_id_type=MESH)

        sb_ref[...] = jnp.zeros((SB_ROWS, SB_W), F32)
        for k in range(4):
            sb_ref[SB_NORM_G + k:SB_NORM_G + k + 1, :] = dng[:, SB_W * k:SB_W * (k + 1)]
            sb_ref[SB_MEM_NORM_G + k:SB_MEM_NORM_G + k + 1, :] = dmg[:, SB_W * k:SB_W * (k + 1)]
        sb_ref[SB_POOL_SCALE:SB_POOL_SCALE + 1, :] = dsc[...]
        sb_ref[SB_B_F:SB_B_F + 1, 0:128] = dbf[...]
        for row, ref in ((SB_FOX_Q, dgq), (SB_FOX_K, dgk), (SB_MEM_Q, dgqm), (SB_MEM_K, dgkm)):
            sb_ref[row:row + 1, 0:HEAD_DIM] = ref[...]
        sb_ref[SB_LOSS:SB_LOSS + 1, 0:128] = loss_ref[0:1, :]
        for g in range(4):
            sl = slice(64 * g, 64 * (g + 1))
            sb_ref[SB_W_POOL:SB_W_POOL + 64, sl] = dwp[sl, sl]

        small = []
        for k in range(1, N_DEV):
            px, py, pc = x ^ (k >> 2), y ^ ((k >> 1) & 1), c ^ (k & 1)
            small.append(rcopy(sb_ref, sland.at[me], k - 1, (px, py, pc)))
        stage1 = [rcopy(in_ref.at[2 * k + 1 - c], land1.at[k], 7 + k, sibling) for k in range(4)]
        for cp in small + stage1:
            cp.start()
        sland[me] = sb_ref[...]
        for cp in stage1:
            cp.wait_recv()

        top, bot = slice(0, half), slice(half, D_MODEL)

        def chip_sum(k, rows):
            return in_ref[2 * k + c, rows, :].astype(F32) + land1[k, rows, :].astype(F32)

        for j in range(2):
            send2a[j] = chip_sum(2 * (1 - x) + j, top).astype(BF16)
            keep2a[j] = chip_sum(2 * x + j, top)
            send2b[j] = chip_sum(2 * j + 1 - y, bot).astype(BF16)
            keep2b[j] = chip_sum(2 * j + y, bot)
        stage2 = [rcopy(send2a.at[j], land2a.at[j], 11 + j, x_nbr) for j in range(2)]
        stage2 += [rcopy(send2b.at[j], land2b.at[j], 13 + j, y_nbr) for j in range(2)]
        for cp in stage2:
            cp.start()
        for cp in stage2:
            cp.wait_recv()
        for j in range(2):
            keep2a[j] = keep2a[j] + land2a[j].astype(F32)
            keep2b[j] = keep2b[j] + land2b[j].astype(F32)
        send3a[...] = keep2a[1 - y].astype(BF16)
        send3b[...] = keep2b[1 - x].astype(BF16)
        stage3 = [rcopy(send3a, land3a, 15, y_nbr), rcopy(send3b, land3b, 16, x_nbr)]
        for cp in stage3:
            cp.start()
        for cp in stage3:
            cp.wait_recv()
        rin_ref[top, :] = keep2a[y] + land3a[...].astype(F32)
        rin_ref[bot, :] = keep2b[x] + land3b[...].astype(F32)

        for cp in small:
            cp.wait_recv()
        for cp in small + stage1 + stage2 + stage3:
            cp.wait_send()
        for land, red in ((lkv_ref, rkv_ref), (lout_ref, rout_ref), (sland, sred_ref)):
            acc = land[0].astype(F32)
            for d in range(1, N_DEV):
                acc = acc + land[d].astype(F32)
            red[...] = acc

    args = (dw_in, land_kv, land_out, d_norm_g, d_mem_g, d_scale, d_bf, d_gq, d_gk, d_gqm, d_gkm, dwp_bd, loss_blk)
    half_chunk = lambda n, dt: pltpu.VMEM((n, half, IN_CHUNK), dt)
    return pl.pallas_call(
        body,
        name="exchange_grads",
        out_shape=(jax.ShapeDtypeStruct((D_MODEL, IN_CHUNK), F32), jax.ShapeDtypeStruct((128, 512), F32),
                   jax.ShapeDtypeStruct((128, D_MODEL), F32), jax.ShapeDtypeStruct((SB_ROWS, SB_W), F32)),
        in_specs=[pl.BlockSpec(memory_space=pltpu.VMEM)] * len(args),
        out_specs=(pl.BlockSpec(memory_space=pltpu.VMEM),) * 4,
        scratch_shapes=[
            pltpu.VMEM((4, D_MODEL, IN_CHUNK), BF16),
            half_chunk(2, BF16), half_chunk(2, BF16), half_chunk(2, F32), half_chunk(2, F32), half_chunk(2, BF16), half_chunk(2, BF16),
            pltpu.VMEM((half, IN_CHUNK), BF16), pltpu.VMEM((half, IN_CHUNK), BF16),
            pltpu.VMEM((half, IN_CHUNK), BF16), pltpu.VMEM((half, IN_CHUNK), BF16),
            pltpu.VMEM((SB_ROWS, SB_W), F32),
            pltpu.VMEM((N_DEV, SB_ROWS, SB_W), F32),
            pltpu.SemaphoreType.DMA((17,)),
            pltpu.SemaphoreType.DMA((17,)),
        ],
        compiler_params=pltpu.CompilerParams(vmem_limit_bytes=VMEM_LIMIT),
    )(*args)


def _fwd_proj(x, norm_g, w_my):
    s_len = x.shape[0]

    def body(x_ref, g_ref, w_ref, proj_ref, h_ref):
        xv = x_ref[...]
        h = (xv * _rms(xv) * g_ref[...]).astype(BF16)
        h_ref[...] = h
        proj_ref[...] = _dot(h, w_ref[...])

    return pl.pallas_call(
        body,
        name="fwd_proj",
        grid=(s_len // TMM,),
        in_specs=[pl.BlockSpec((TMM, D_MODEL), lambda i: (i, 0)), _full((1, D_MODEL)), _full((D_MODEL, MY_WIDTH))],
        out_specs=[pl.BlockSpec((TMM, MY_WIDTH), lambda i: (i, 0)), pl.BlockSpec((TMM, D_MODEL), lambda i: (i, 0))],
        out_shape=[jax.ShapeDtypeStruct((s_len, MY_WIDTH), F32), jax.ShapeDtypeStruct((s_len, D_MODEL), BF16)],
        compiler_params=_params(("parallel",)),
    )(x, norm_g, w_my)


def _log_sigmoid(z):
    return jnp.minimum(z, 0.0) - jnp.log(1.0 + jnp.exp(-jnp.abs(z)))


def _fox_prep(proj, bf_pad, gq, gk):
    s_len = proj.shape[0]

    def body(q_ref, k_ref, v_ref, f_ref, bf_ref, gq_ref, gk_ref, qa_ref, ka_ref, vh_ref, kt_ref, vt_ref, carry_ref):
        @pl.when(pl.program_id(0) == 0)
        def _():
            carry_ref[...] = jnp.zeros((1, 128), F32)

        lane = lax.broadcasted_iota(jnp.int32, (TM, 128), 1)
        logf = jnp.where(lane < FOX_HEADS, _log_sigmoid(f_ref[...] + bf_ref[...]), 0.0)
        row = lax.broadcasted_iota(jnp.int32, (TM, TM), 0)
        col = lax.broadcasted_iota(jnp.int32, (TM, TM), 1)
        tri = jnp.where(col <= row, 1.0, 0.0).astype(BF16)
        hi, mid, lo = _split3(logf)
        cum = _dot(tri, hi) + _dot(tri, mid) + _dot(tri, lo) + carry_ref[...]
        carry_ref[...] = cum[TM - 1:TM, :]
        f_hi, f_mid, f_lo = [t.astype(F32) for t in _split3(cum)]
        zeros = jnp.zeros((TM, HEAD_DIM), F32)
        for h in range(FOX_HEADS):
            sl = slice(HEAD_DIM * h, HEAD_DIM * (h + 1))
            qh = q_ref[:, sl]
            kh = k_ref[:, sl]
            qn = qh * _rms(qh) * gq_ref[...] * 0.125
            kn = kh * _rms(kh) * gk_ref[...]
            a, b, c = f_hi[:, h:h + 1], f_mid[:, h:h + 1], f_lo[:, h:h + 1]
            qa = jnp.where(lane < 64, jnp.concatenate([qn, zeros], axis=1),
                           jnp.where(lane == 64, a, jnp.where(lane == 65, b, jnp.where(lane == 66, c,
                                                                                         jnp.where(lane < 70, 1.0, 0.0)))))
            ka = jnp.where(lane < 64, jnp.concatenate([kn, zeros], axis=1),
                           jnp.where(lane < 67, 1.0, jnp.where(lane == 67, -a, jnp.where(lane == 68, -b,
                                                                                          jnp.where(lane == 69, -c, 0.0)))))
            vh = v_ref[:, sl]
            v_aug = jnp.where(lane < 64, jnp.concatenate([vh, zeros], axis=1), jnp.where(lane == 64, 1.0, 0.0))
            qa_ref[h] = qa.astype(BF16)
            ka_ref[h] = ka.astype(BF16)
            vh_ref[h] = vh.astype(BF16)
            kt_ref[h, 0] = ka.T.astype(BF16)
            vt_ref[h, 0] = v_aug.T.astype(BF16)

    col_blk = lambda j: pl.BlockSpec((TM, PIECE), lambda i: (i, j))
    head_blk = lambda w: pl.BlockSpec((FOX_HEADS, TM, w), lambda i: (0, i, 0))
    tile_blk = pl.BlockSpec((FOX_HEADS, 1, 128, TM), lambda i: (0, i, 0, 0))
    tiled = jax.ShapeDtypeStruct((FOX_HEADS, s_len // TM, 128, TM), BF16)
    return pl.pallas_call(
        body,
        name="fox_prep",
        grid=(s_len // TM,),
        in_specs=[col_blk(1), col_blk(2), col_blk(3), pl.BlockSpec((TM, 128), lambda i: (i, MY_F_OFF // 128)),
                  _full((1, 128)), _full((1, HEAD_DIM)), _full((1, HEAD_DIM))],
        out_specs=[head_blk(128), head_blk(128), head_blk(HEAD_DIM), tile_blk, tile_blk],
        out_shape=[jax.ShapeDtypeStruct((FOX_HEADS, s_len, 128), BF16), jax.ShapeDtypeStruct((FOX_HEADS, s_len, 128), BF16),
                   jax.ShapeDtypeStruct((FOX_HEADS, s_len, HEAD_DIM), BF16), tiled, tiled],
        scratch_shapes=[pltpu.VMEM((1, 128), F32)],
        compiler_params=_params(("arbitrary",)),
    )(proj, proj, proj, proj, bf_pad, gq, gk)


def _mem_prep(mem, g, w_kv, gk):
    def body(mem_ref, g_ref, w_ref, gk_ref, km_ref, vm_ref):
        m = mem_ref[...]
        kv = _dot((m * _rms(m) * g_ref[...]).astype(BF16), w_ref[...])
        for h in range(MEM_HEADS):
            sl = slice(HEAD_DIM * h, HEAD_DIM * (h + 1))
            kh = kv[:, sl]
            km_ref[:, sl] = (kh * _rms(kh) * gk_ref[...]).astype(BF16)
        vm_ref[...] = kv[:, 256:512].astype(BF16)

    return pl.pallas_call(
        body,
        name="mem_prep",
        out_shape=(jax.ShapeDtypeStruct((N_MEM, 256), BF16),) * 2,
        in_specs=[pl.BlockSpec(memory_space=pltpu.VMEM)] * 4,
        out_specs=(pl.BlockSpec(memory_space=pltpu.VMEM),) * 2,
        compiler_params=pltpu.CompilerParams(vmem_limit_bytes=VMEM_LIMIT),
    )(mem, g, w_kv, gk)


def _causal_mask():
    row = lax.broadcasted_iota(jnp.int32, (TA, TA), 0)
    col = lax.broadcasted_iota(jnp.int32, (TA, TA), 1)
    return col <= row


def _causal_mask_t():
    row = lax.broadcasted_iota(jnp.int32, (TA, TA), 0)
    col = lax.broadcasted_iota(jnp.int32, (TA, TA), 1)
    return row <= col


def _fox_fwd(q_aug, k_aug, vt_aug, w_kv16, w_out16):
    s_len = q_aug.shape[1]
    n_tiles = s_len // TA
    n_groups = FOX_HEADS // HB

    def body(q_ref, k_ref, vt_ref, kv16_ref, out16_ref, o_ref, st_ref, kv_all, out_all, send_sems, recv_sems, local_sems):
        qi = pl.program_id(1)
        remote, local = _row_block_exchange(kv16_ref, out16_ref, kv_all, out_all, send_sems, recv_sems, local_sems, gather=True)

        @pl.when((pl.program_id(0) == 0) & (qi == 0))
        def _():
            for cp in remote + local:
                cp.start()

        qs = [q_ref[h] for h in range(HB)]

        def step(t0, carry, masked, width):
            rows = pl.ds(pl.multiple_of(t0 * TA, TA), width * TA)
            scores = [_dot_nt(k_ref[h, rows, :], qs[h]) for h in range(HB)]
            soft = []
            for h in range(HB):
                m, _ = carry[h]
                s = jnp.where(_causal_mask_t(), scores[h], -1e30) if masked else scores[h]
                m_new = jnp.maximum(m, jnp.max(s, axis=0, keepdims=True))
                soft.append((m_new, jnp.exp(m - m_new), jnp.exp(s - m_new).astype(BF16)))
            out = []
            for h, (m_new, alpha, p) in enumerate(soft):
                acc = alpha * carry[h][1]
                for t in range(width):
                    acc = acc + _dot(vt_ref[h, t0 + t], p[t * TA:(t + 1) * TA])
                out.append((m_new, acc))
            return tuple(out)

        init = tuple((jnp.full((1, TA), -1e30, F32), jnp.zeros((128, TA), F32)) for _ in range(HB))
        carry = lax.fori_loop(0, qi // 2, lambda j, cr: step(2 * j, cr, False, 2), init)
        carry = lax.fori_loop(2 * (qi // 2), qi, lambda j, cr: step(j, cr, False, 1), carry)
        carry = step(qi, carry, True, 1)
        for h in range(HB):
            m, acc = carry[h]
            l = acc[HEAD_DIM:HEAD_DIM + 1, :]
            lse = m + jnp.log(l)
            o_ref[h] = jnp.concatenate([acc[0:HEAD_DIM] / l, jnp.broadcast_to(lse, (HEAD_DIM, TA))], axis=0).T
            st_ref[h, 0] = jnp.broadcast_to(lse, (8, TA))

        @pl.when((pl.program_id(0) == n_groups - 1) & (qi == n_tiles - 1))
        def _():
            for cp in remote:
                cp.wait_recv()
            for cp in remote:
                cp.wait_send()
            for cp in local:
                cp.wait()

    hbm = pl.BlockSpec(memory_space=pl.ANY)
    return pl.pallas_call(
        body,
        name="fox_fwd",
        grid=(n_groups, n_tiles),
        in_specs=[pl.BlockSpec((HB, TA, 128), lambda g, i: (g, i, 0)), pl.BlockSpec((HB, s_len, 128), lambda g, i: (g, 0, 0)),
                  pl.BlockSpec((HB, n_tiles, 128, TA), lambda g, i: (g, 0, 0, 0)), hbm, hbm],
        out_specs=[pl.BlockSpec((HB, TA, 128), lambda g, i: (g, i, 0)), pl.BlockSpec((HB, 1, 8, TA), lambda g, i: (g, i, 0, 0)),
                   hbm, hbm],
        out_shape=[jax.ShapeDtypeStruct((FOX_HEADS, s_len, 128), F32), jax.ShapeDtypeStruct((FOX_HEADS, n_tiles, 8, TA), F32),
                   jax.ShapeDtypeStruct((D_MODEL, 512), BF16), jax.ShapeDtypeStruct((D_MODEL, D_MODEL), BF16)],
        scratch_shapes=[pltpu.SemaphoreType.DMA((14,)), pltpu.SemaphoreType.DMA((14,)), pltpu.SemaphoreType.DMA((2,))],
        compiler_params=_params(("arbitrary", "arbitrary")),
    )(q_aug, k_aug, vt_aug, w_kv16, w_out16)


def _lane_group(lane, a, b, c, d):
    return jnp.where(lane < 64, a, jnp.where(lane < 128, b, jnp.where(lane < 192, c, d)))


def _pool_count(row0):
    lane = lax.broadcasted_iota(jnp.int32, (TM, POOL_WIDTH), 1)
    t1 = row0 + lax.broadcasted_iota(jnp.int32, (TM, POOL_WIDTH), 0) + 1
    return jnp.minimum(t1, _lane_group(lane, 2, 4, 8, 16)).astype(F32), lane


def _pool_delta(u, halo, cnt, lane):
    xe = jnp.concatenate([halo, u], axis=0)
    s2 = xe + pltpu.roll(xe, 1, 0)
    s4 = s2 + pltpu.roll(s2, 2, 0)
    s8 = s4 + pltpu.roll(s4, 4, 0)
    s16 = s8 + pltpu.roll(s8, 8, 0)
    win = _lane_group(lane, s2[HALO:], s4[HALO:], s8[HALO:], s16[HALO:])
    return win / cnt - u


def _pool_delta_bwd(dd, dd_next, cnt, cnt_next, lane):
    ee = jnp.concatenate([dd / cnt, dd_next / cnt_next], axis=0)
    n = TM + HALO
    r2 = ee + pltpu.roll(ee, n - 1, 0)
    r4 = r2 + pltpu.roll(r2, n - 2, 0)
    r8 = r4 + pltpu.roll(r4, n - 4, 0)
    r16 = r8 + pltpu.roll(r8, n - 8, 0)
    return _lane_group(lane, r2[:TM], r4[:TM], r8[:TM], r16[:TM]) - dd


def _mem_attn(qm, gq, km_ref, vm_ref):
    heads = []
    for h in range(MEM_HEADS):
        sl = slice(HEAD_DIM * h, HEAD_DIM * (h + 1))
        qh = qm[:, sl]
        r = _rms(qh)
        a = qh * r
        q16 = (a * gq * 0.125).astype(BF16)
        s = _dot_nt(q16, km_ref[:, sl])
        e = jnp.exp(s - jnp.max(s, axis=-1, keepdims=True))
        p = e / jnp.sum(e, axis=-1, keepdims=True)
        heads.append((a, r, q16, p, _dot(p.astype(BF16), vm_ref[:, sl])))
    return heads


def _row_specs(s_len):
    n_halo = s_len // HALO
    piece = lambda j: pl.BlockSpec((TM, PIECE), lambda i: (i, j))
    before = lambda j: pl.BlockSpec((HALO, 256), lambda i: (jnp.maximum(i * (TM // HALO) - 1, 0), j))
    after = lambda j: pl.BlockSpec((HALO, 256), lambda i: (jnp.minimum((i + 1) * (TM // HALO), n_halo - 1), j))
    return piece, before, after


def _mix_fwd(proj, olse, km, vm, wp_bd, pool_scale, gq_m):
    s_len = proj.shape[0]

    def body(ua_ref, halo_ref, gb_ref, qm_ref, ol_ref, km_ref, vm_ref, wp_ref, sc_ref, gq_ref, out_ref):
        i = pl.program_id(0)
        u = ua_ref[:, 0:256]
        g_a = ua_ref[:, 256:512]
        halo = jnp.where(i > 0, halo_ref[...], 0.0)
        cnt, lane = _pool_count(i * TM)
        d = _pool_delta(u, halo, cnt, lane).astype(BF16)
        y_a = _dot(d, wp_ref[...]) * sc_ref[...]
        out_ref[:, 0:256] = (y_a * (g_a * _sigmoid(g_a))).astype(BF16)
        for h in range(FOX_HEADS):
            sl = slice(HEAD_DIM * h, HEAD_DIM * (h + 1))
            g = gb_ref[:, sl]
            out_ref[:, 256 + HEAD_DIM * h:256 + HEAD_DIM * (h + 1)] = (ol_ref[h][:, 0:HEAD_DIM] * (g * _sigmoid(g))).astype(BF16)
        heads = _mem_attn(qm_ref[:, 0:256], gq_ref[...], km_ref, vm_ref)
        for h in range(MEM_HEADS):
            g = qm_ref[:, 256 + HEAD_DIM * h:256 + HEAD_DIM * (h + 1)]
            out_ref[:, 768 + HEAD_DIM * h:768 + HEAD_DIM * (h + 1)] = (heads[h][4] * (g * _sigmoid(g))).astype(BF16)

    piece, before, _ = _row_specs(s_len)
    return pl.pallas_call(
        body,
        name="mix_fwd",
        grid=(s_len // TM,),
        in_specs=[piece(0), before(0), piece(4), piece(5), pl.BlockSpec((FOX_HEADS, TM, 128), lambda i: (0, i, 0)),
                  _full((N_MEM, 256)), _full((N_MEM, 256)), _full((256, 256)), _full((1, 256)), _full((1, HEAD_DIM))],
        out_specs=pl.BlockSpec((TM, D_MODEL), lambda i: (i, 0)),
        out_shape=jax.ShapeDtypeStruct((s_len, D_MODEL), BF16),
        compiler_params=_params(("parallel",)),
    )(proj, proj, proj, proj, olse, km, vm, wp_bd, pool_scale, gq_m)


def _out_loss(mixed, x, target, w_out):
    s_len = x.shape[0]

    def body(mix_ref, x_ref, t_ref, w_ref, dout_ref, dmix_ref, dw16_ref, loss_ref, dw_ref):
        @pl.when(pl.program_id(0) == 0)
        def _():
            dw_ref[...] = jnp.zeros((D_MODEL, D_MODEL), F32)
            loss_ref[...] = jnp.zeros((8, 128), F32)

        mixed16 = mix_ref[...]
        err = x_ref[...] + _dot(mixed16, w_ref[...]) - t_ref[...]
        loss_ref[...] += 0.5 * jnp.sum(jnp.mean(err * err, axis=-1, keepdims=True))
        d_out = err / float(D_MODEL)
        dout_ref[...] = d_out
        d16 = d_out.astype(BF16)
        dmix_ref[...] = _dot_nt(d16, w_ref[...])
        dw_ref[...] += _dot_tn(mixed16, d16)

        @pl.when(pl.program_id(0) == pl.num_programs(0) - 1)
        def _():
            dw16_ref[...] = dw_ref[...].astype(BF16)

    row = pl.BlockSpec((TMM, D_MODEL), lambda i: (i, 0))
    return pl.pallas_call(
        body,
        name="out_loss",
        grid=(s_len // TMM,),
        in_specs=[row, row, row, _full((D_MODEL, D_MODEL))],
        out_specs=[row, row, _full((D_MODEL, D_MODEL)), _full((8, 128))],
        out_shape=[jax.ShapeDtypeStruct((s_len, D_MODEL), F32), jax.ShapeDtypeStruct((s_len, D_MODEL), F32),
                   jax.ShapeDtypeStruct((D_MODEL, D_MODEL), BF16), jax.ShapeDtypeStruct((8, 128), F32)],
        scratch_shapes=[pltpu.VMEM((D_MODEL, D_MODEL), F32)],
        compiler_params=_params(("arbitrary",)),
    )(mixed, x, target, w_out)


def _silu_pair(g):
    s = _sigmoid(g)
    return g * s, s * (1.0 + g * (1.0 - s))


def _mix_bwd(proj, olse, d_mixed, km, vm, wp_bd, pool_scale, gq_m):
    s_len = proj.shape[0]
    n_tiles = s_len // TM

    def body(ua_ref, halo_ref, ga_next_ref, gb_ref, qm_ref, ol_ref, dm_ref, dm_next_ref, km_ref, vm_ref, wp_ref, sc_ref, gq_ref,
             dua_ref, dgb_ref, dqm_ref, do_ref, dl_ref, dwp_ref, dsc_ref, dkm_ref, dvm_ref, dgq_ref):
        i = pl.program_id(0)

        @pl.when(i == 0)
        def _():
            dwp_ref[...] = jnp.zeros((256, 256), F32)
            dsc_ref[...] = jnp.zeros((1, 256), F32)
            dkm_ref[...] = jnp.zeros((N_MEM, 256), F32)
            dvm_ref[...] = jnp.zeros((N_MEM, 256), F32)
            dgq_ref[...] = jnp.zeros((1, HEAD_DIM), F32)

        u = ua_ref[:, 0:256]
        g_a = ua_ref[:, 256:512]
        halo = jnp.where(i > 0, halo_ref[...], 0.0)
        cnt, lane = _pool_count(i * TM)
        d16 = _pool_delta(u, halo, cnt, lane).astype(BF16)
        t = _dot(d16, wp_ref[...])
        y_a = t * sc_ref[...]
        silu_a, dsilu_a = _silu_pair(g_a)
        dm_a = dm_ref[:, 0:256]
        dy_a = dm_a * silu_a
        dsc_ref[...] += jnp.sum(dy_a * t, axis=0, keepdims=True)
        dt16 = (dy_a * sc_ref[...]).astype(BF16)
        dwp_ref[...] += _dot_tn(d16, dt16)
        dd = _dot_nt(dt16, wp_ref[...])
        g_next = ga_next_ref[...]
        dt_next = (dm_next_ref[...] * (g_next * _sigmoid(g_next)) * sc_ref[...]).astype(BF16)
        dd_next = jnp.where(i < n_tiles - 1, _dot_nt(dt_next, wp_ref[...]), 0.0)
        cnt_next = _pool_count((i + 1) * TM)[0][0:HALO]
        dua_ref[:, 0:256] = _pool_delta_bwd(dd, dd_next, cnt, cnt_next, lane).astype(BF16)
        dua_ref[:, 256:512] = (dm_a * y_a * dsilu_a).astype(BF16)

        ones = jnp.ones((8, HEAD_DIM), BF16)
        for h in range(FOX_HEADS):
            sl = slice(HEAD_DIM * h, HEAD_DIM * (h + 1))
            silu_b, dsilu_b = _silu_pair(gb_ref[:, sl])
            dm_b = dm_ref[:, 256 + HEAD_DIM * h:256 + HEAD_DIM * (h + 1)]
            o_h = ol_ref[h][:, 0:HEAD_DIM]
            d_o = dm_b * silu_b
            do_ref[h] = d_o.astype(BF16)
            dgb_ref[:, sl] = (dm_b * o_h * dsilu_b).astype(BF16)
            prod = d_o * o_h
            hi = prod.astype(BF16)
            lo = (prod - hi.astype(F32)).astype(BF16)
            dl_ref[h, 0] = _dot_nt(ones, hi) + _dot_nt(ones, lo)

        heads = _mem_attn(qm_ref[:, 0:256], gq_ref[...], km_ref, vm_ref)
        dgq = jnp.zeros((TM, HEAD_DIM), F32)
        for h in range(MEM_HEADS):
            sl = slice(HEAD_DIM * h, HEAD_DIM * (h + 1))
            a, r, q16, p, y_m = heads[h]
            silu_m, dsilu_m = _silu_pair(qm_ref[:, 256 + HEAD_DIM * h:256 + HEAD_DIM * (h + 1)])
            dm_m = dm_ref[:, 768 + HEAD_DIM * h:768 + HEAD_DIM * (h + 1)]
            dqm_ref[:, 256 + HEAD_DIM * h:256 + HEAD_DIM * (h + 1)] = (dm_m * y_m * dsilu_m).astype(BF16)
            dy16 = (dm_m * silu_m).astype(BF16)
            dp = _dot_nt(dy16, vm_ref[:, sl])
            dvm_ref[:, sl] += _dot_tn(p.astype(BF16), dy16)
            ds16 = (p * (dp - jnp.sum(dp * p, axis=-1, keepdims=True))).astype(BF16)
            dkm_ref[:, sl] += _dot_tn(ds16, q16)
            dq, dg_rows = _rms_bwd(a, r, gq_ref[...], _dot(ds16, km_ref[:, sl]) * 0.125)
            dqm_ref[:, sl] = dq.astype(BF16)
            dgq = dgq + dg_rows
        dgq_ref[...] += jnp.sum(dgq, axis=0, keepdims=True)

    piece, before, after = _row_specs(s_len)
    n_halo = s_len // HALO
    row = pl.BlockSpec((TM, D_MODEL), lambda i: (i, 0))
    dm_after = pl.BlockSpec((HALO, 256), lambda i: (jnp.minimum((i + 1) * (TM // HALO), n_halo - 1), 0))
    out_piece = pl.BlockSpec((TM, PIECE), lambda i: (i, 0))
    return pl.pallas_call(
        body,
        name="mix_bwd",
        grid=(n_tiles,),
        in_specs=[piece(0), before(0), after(1), piece(4), piece(5), pl.BlockSpec((FOX_HEADS, TM, 128), lambda i: (0, i, 0)),
                  row, dm_after, _full((N_MEM, 256)), _full((N_MEM, 256)), _full((256, 256)), _full((1, 256)), _full((1, HEAD_DIM))],
        out_specs=[out_piece, out_piece, out_piece, pl.BlockSpec((FOX_HEADS, TM, HEAD_DIM), lambda i: (0, i, 0)),
                   pl.BlockSpec((FOX_HEADS, 1, 8, TM), lambda i: (0, i, 0, 0)),
                   _full((256, 256)), _full((1, 256)), _full((N_MEM, 256)), _full((N_MEM, 256)), _full((1, HEAD_DIM))],
        out_shape=[jax.ShapeDtypeStruct((s_len, PIECE), BF16)] * 3 + [
            jax.ShapeDtypeStruct((FOX_HEADS, s_len, HEAD_DIM), BF16),
            jax.ShapeDtypeStruct((FOX_HEADS, n_tiles, 8, TM), F32),
            jax.ShapeDtypeStruct((256, 256), F32), jax.ShapeDtypeStruct((1, 256), F32),
            jax.ShapeDtypeStruct((N_MEM, 256), F32), jax.ShapeDtypeStruct((N_MEM, 256), F32),
            jax.ShapeDtypeStruct((1, HEAD_DIM), F32)],
        compiler_params=_params(("arbitrary",)),
    )(proj, proj, proj, proj, proj, olse, d_mixed, d_mixed, km, vm, wp_bd, pool_scale, gq_m)


def _fox_bwd(q_aug, k_aug, kt_aug, v_h, d_o, lse_rows, delta_rows, dw_kv16, dw_out16):
    s_len = q_aug.shape[1]
    n_tiles = s_len // TA
    n_groups = FOX_HEADS // HB

    def body(q_ref, k_ref, kt_ref, v_ref, do_ref, st_ref, dl_ref, dkv16_ref, dout16_ref, dqt_ref, dk_ref, dv_ref, land_kv, land_out,
             send_sems, recv_sems, local_sems):
        j = pl.program_id(1)
        remote, local = _row_block_exchange(dkv16_ref, dout16_ref, land_kv, land_out, send_sems, recv_sems, local_sems, gather=False)

        @pl.when((pl.program_id(0) == 0) & (j == 0))
        def _():
            for cp in remote + local:
                cp.start()

        @pl.when(j == 0)
        def _():
            dqt_ref[...] = jnp.zeros((HB, n_tiles, 128, TA), F32)

        ks = [k_ref[h] for h in range(HB)]
        kts = [kt_ref[h, 0] for h in range(HB)]
        vs = [v_ref[h] for h in range(HB)]

        def pair(i0, acc, masked, width):
            rows = pl.ds(pl.multiple_of(i0 * TA, TA), width * TA)
            qs = [q_ref[h, rows, :] for h in range(HB)]
            d_os = [do_ref[h, rows, :] for h in range(HB)]
            row_stat = lambda ref, h: jnp.concatenate([ref[h, i0 + t, 0:1, :] for t in range(width)], axis=1)
            scores = [(_dot_nt(ks[h], qs[h]), _dot_nt(vs[h], d_os[h])) for h in range(HB)]
            probs = []
            for h in range(HB):
                s, dp = scores[h]
                if masked:
                    s = jnp.where(_causal_mask_t(), s, -1e30)
                p = jnp.exp(s - row_stat(st_ref, h))
                probs.append((p.astype(BF16), (p * (dp - row_stat(dl_ref, h))).astype(BF16)))
            out = []
            for h in range(HB):
                p16, ds16 = probs[h]
                dqt = _dot(kts[h], ds16)
                for t in range(width):
                    dqt_ref[h, i0 + t] += dqt[:, t * TA:(t + 1) * TA]
                out.append((acc[h][0] + _dot(ds16, qs[h]), acc[h][1] + _dot(p16, d_os[h])))
            return tuple(out)

        zero = tuple((jnp.zeros((TA, 128), F32), jnp.zeros((TA, HEAD_DIM), F32)) for _ in range(HB))
        acc = pair(j, zero, True, 1)
        odd = (n_tiles - 1 - j) % 2
        acc = lax.fori_loop(j + 1, j + 1 + odd, lambda i, a: pair(i, a, False, 1), acc)
        acc = lax.fori_loop(0, (n_tiles - 1 - j) // 2, lambda t, a: pair(j + 1 + odd + 2 * t, a, False, 2), acc)
        for h in range(HB):
            dk_ref[h] = acc[h][0]
            dv_ref[h] = acc[h][1]

        @pl.when((pl.program_id(0) == n_groups - 1) & (j == n_tiles - 1))
        def _():
            for cp in remote:
                cp.wait_recv()
            for cp in remote:
                cp.wait_send()
            for cp in local:
                cp.wait()

    whole = lambda w: pl.BlockSpec((HB, s_len, w), lambda g, j: (g, 0, 0))
    tile = lambda w: pl.BlockSpec((HB, TA, w), lambda g, j: (g, j, 0))
    rows_blk = lambda r: pl.BlockSpec((HB, n_tiles, r, TA), lambda g, j: (g, 0, 0, 0))
    hbm = pl.BlockSpec(memory_space=pl.ANY)
    return pl.pallas_call(
        body,
        name="fox_bwd",
        grid=(n_groups, n_tiles),
        in_specs=[whole(128), tile(128), pl.BlockSpec((HB, 1, 128, TA), lambda g, j: (g, j, 0, 0)), tile(HEAD_DIM),
                  whole(HEAD_DIM), rows_blk(8), rows_blk(8), hbm, hbm],
        out_specs=[rows_blk(128), tile(128), tile(HEAD_DIM), hbm, hbm],
        out_shape=[jax.ShapeDtypeStruct((FOX_HEADS, n_tiles, 128, TA), F32), jax.ShapeDtypeStruct((FOX_HEADS, s_len, 128), F32),
                   jax.ShapeDtypeStruct((FOX_HEADS, s_len, HEAD_DIM), F32),
                   jax.ShapeDtypeStruct((N_DEV, 128, 512), BF16), jax.ShapeDtypeStruct((N_DEV, 128, D_MODEL), BF16)],
        scratch_shapes=[pltpu.SemaphoreType.DMA((14,)), pltpu.SemaphoreType.DMA((14,)), pltpu.SemaphoreType.DMA((2,))],
        compiler_params=_params(("arbitrary", "arbitrary")),
    )(q_aug, k_aug, kt_aug, v_h, d_o, lse_rows, delta_rows, dw_kv16, dw_out16)


def _fox_post(proj, bf_pad, gq, gk, dq_aug, dk_aug, dv_h):
    s_len = proj.shape[0]
    n_tiles = s_len // TM

    def body(q_ref, k_ref, f_ref, bf_ref, gq_ref, gk_ref, dqa_ref, dka_ref, dvh_ref,
             dq_ref, dk_ref, dv_ref, df_ref, dgq_ref, dgk_ref, dbf_ref, carry_ref):
        @pl.when(pl.program_id(0) == 0)
        def _():
            carry_ref[...] = jnp.zeros((1, 128), F32)
            dgq_ref[...] = jnp.zeros((1, HEAD_DIM), F32)
            dgk_ref[...] = jnp.zeros((1, HEAD_DIM), F32)
            dbf_ref[...] = jnp.zeros((1, 128), F32)

        lane = lax.broadcasted_iota(jnp.int32, (TM, 128), 1)
        d_cum = jnp.zeros((TM, 128), F32)
        dgq = jnp.zeros((TM, HEAD_DIM), F32)
        dgk = jnp.zeros((TM, HEAD_DIM), F32)
        for h in range(FOX_HEADS):
            sl = slice(HEAD_DIM * h, HEAD_DIM * (h + 1))
            dqa = dqa_ref[h, 0].T
            dka = dka_ref[h]
            qh = q_ref[:, sl]
            kh = k_ref[:, sl]
            rq = _rms(qh)
            rk = _rms(kh)
            dq, dgq_rows = _rms_bwd(qh * rq, rq, gq_ref[...], dqa[:, 0:HEAD_DIM] * 0.125)
            dk, dgk_rows = _rms_bwd(kh * rk, rk, gk_ref[...], dka[:, 0:HEAD_DIM])
            dq_ref[:, sl] = dq.astype(BF16)
            dk_ref[:, sl] = dk.astype(BF16)
            dv_ref[:, sl] = dvh_ref[h].astype(BF16)
            dgq = dgq + dgq_rows
            dgk = dgk + dgk_rows
            d_cum = jnp.where(lane == h, dqa[:, 64:65] - dka[:, 67:68], d_cum)
        dgq_ref[...] += jnp.sum(dgq, axis=0, keepdims=True)
        dgk_ref[...] += jnp.sum(dgk, axis=0, keepdims=True)

        row = lax.broadcasted_iota(jnp.int32, (TM, TM), 0)
        col = lax.broadcasted_iota(jnp.int32, (TM, TM), 1)
        tri = jnp.where(col >= row, 1.0, 0.0).astype(BF16)
        hi, mid, lo = _split3(d_cum)
        d_logf = _dot(tri, hi) + _dot(tri, mid) + _dot(tri, lo) + carry_ref[...]
        carry_ref[...] = d_logf[0:1, :]
        z = f_ref[...] + bf_ref[...]
        d_f = jnp.where(lane < FOX_HEADS, d_logf * _sigmoid(-z), 0.0)
        df_ref[...] = d_f.astype(BF16)
        dbf_ref[...] += jnp.sum(d_f, axis=0, keepdims=True)

    rev = lambda i: n_tiles - 1 - i
    col_blk = lambda j: pl.BlockSpec((TM, PIECE), lambda i: (rev(i), j))
    head_blk = lambda w: pl.BlockSpec((FOX_HEADS, TM, w), lambda i: (0, rev(i), 0))
    out_piece = pl.BlockSpec((TM, PIECE), lambda i: (rev(i), 0))
    return pl.pallas_call(
        body,
        name="fox_post",
        grid=(n_tiles,),
        in_specs=[col_blk(1), col_blk(2), pl.BlockSpec((TM, 128), lambda i: (rev(i), MY_F_OFF // 128)),
                  _full((1, 128)), _full((1, HEAD_DIM)), _full((1, HEAD_DIM)),
                  pl.BlockSpec((FOX_HEADS, 1, 128, TM), lambda i: (0, rev(i), 0, 0)), head_blk(128), head_blk(HEAD_DIM)],
        out_specs=[out_piece, out_piece, out_piece, pl.BlockSpec((TM, 128), lambda i: (rev(i), 0)),
                   _full((1, HEAD_DIM)), _full((1, HEAD_DIM)), _full((1, 128))],
        out_shape=[jax.ShapeDtypeStruct((s_len, PIECE), BF16)] * 3 + [
            jax.ShapeDtypeStruct((s_len, 128), BF16), jax.ShapeDtypeStruct((1, HEAD_DIM), F32),
            jax.ShapeDtypeStruct((1, HEAD_DIM), F32), jax.ShapeDtypeStruct((1, 128), F32)],
        scratch_shapes=[pltpu.VMEM((1, 128), F32)],
        compiler_params=_params(("arbitrary",)),
    )(proj, proj, proj, bf_pad, gq, gk, dq_aug, dk_aug, dv_h)


def _mem_bwd(mem, g, w_kv, gk, dkm, dvm):
    def body(mem_ref, g_ref, w_ref, gk_ref, dkm_ref, dvm_ref, dw_ref, dg_ref, dgk_ref, dkv_ref):
        m = mem_ref[...]
        r = _rms(m)
        a = m * r
        mn16 = (a * g_ref[...]).astype(BF16)
        kv = _dot(mn16, w_ref[...])
        dgk = jnp.zeros((N_MEM, HEAD_DIM), F32)
        for h in range(MEM_HEADS):
            sl = slice(HEAD_DIM * h, HEAD_DIM * (h + 1))
            kh = kv[:, sl]
            rk = _rms(kh)
            dk, dg_rows = _rms_bwd(kh * rk, rk, gk_ref[...], dkm_ref[:, sl])
            dkv_ref[:, sl] = dk.astype(BF16)
            dgk = dgk + dg_rows
        dkv_ref[:, 256:512] = dvm_ref[...].astype(BF16)
        dgk_ref[...] = jnp.sum(dgk, axis=0, keepdims=True)
        dkv16 = dkv_ref[...]
        dw_ref[...] = _dot_tn(mn16, dkv16).astype(BF16)
        dg_ref[...] = jnp.sum(_dot_nt(dkv16, w_ref[...]) * a, axis=0, keepdims=True)

    return pl.pallas_call(
        body,
        name="mem_bwd",
        out_shape=(jax.ShapeDtypeStruct((D_MODEL, 512), BF16), jax.ShapeDtypeStruct((1, D_MODEL), F32),
                   jax.ShapeDtypeStruct((1, HEAD_DIM), F32)),
        in_specs=[pl.BlockSpec(memory_space=pltpu.VMEM)] * 6,
        out_specs=(pl.BlockSpec(memory_space=pltpu.VMEM),) * 3,
        scratch_shapes=[pltpu.VMEM((N_MEM, 512), BF16)],
        compiler_params=pltpu.CompilerParams(vmem_limit_bytes=VMEM_LIMIT),
    )(mem, g, w_kv, gk, dkm, dvm)


def _grad_x(pieces, d_f, w_my, x, norm_g, d_out):
    s_len = x.shape[0]

    def body(p0, p1, p2, p3, p4, p5, df_ref, w_ref, x_ref, g_ref, dout_ref, gx_ref, dg_ref):
        @pl.when(pl.program_id(0) == 0)
        def _():
            dg_ref[...] = jnp.zeros((1, D_MODEL), F32)

        dh = _dot_nt(df_ref[...], w_ref[:, MY_F_OFF:MY_WIDTH])
        for j, p in enumerate((p0, p1, p2, p3, p4, p5)):
            dh = dh + _dot_nt(p[...], w_ref[:, PIECE * j:PIECE * (j + 1)])
        xv = x_ref[...]
        r = _rms(xv)
        dx, dg_rows = _rms_bwd(xv * r, r, g_ref[...], dh)
        gx_ref[...] = dout_ref[...] + dx
        dg_ref[...] += jnp.sum(dg_rows, axis=0, keepdims=True)

    piece = pl.BlockSpec((TMM, PIECE), lambda i: (i, 0))
    row = pl.BlockSpec((TMM, D_MODEL), lambda i: (i, 0))
    return pl.pallas_call(
        body,
        name="grad_x",
        grid=(s_len // TMM,),
        in_specs=[piece] * 6 + [pl.BlockSpec((TMM, 128), lambda i: (i, 0)), _full((D_MODEL, MY_WIDTH)), row,
                                _full((1, D_MODEL)), row],
        out_specs=[row, _full((1, D_MODEL))],
        out_shape=[jax.ShapeDtypeStruct((s_len, D_MODEL), F32), jax.ShapeDtypeStruct((1, D_MODEL), F32)],
        compiler_params=_params(("arbitrary",)),
    )(*pieces, d_f, w_my, x, norm_g, d_out)


def _grad_w_in(h16, pieces, d_f):
    s_len = h16.shape[0]
    tk = 512

    def body(h_ref, p0, p1, p2, p3, p4, p5, df_ref, out_ref, dw_ref):
        @pl.when(pl.program_id(0) == 0)
        def _():
            dw_ref[...] = jnp.zeros((D_MODEL, MY_WIDTH), F32)

        h = h_ref[...]
        for j, p in enumerate((p0, p1, p2, p3, p4, p5)):
            dw_ref[:, PIECE * j:PIECE * (j + 1)] += _dot_tn(h, p[...])
        dw_ref[:, MY_F_OFF:MY_WIDTH] += _dot_tn(h, df_ref[...])

        @pl.when(pl.program_id(0) == pl.num_programs(0) - 1)
        def _():
            for d in range(N_DEV):
                parts = [dw_ref[:, start:start + width] for start, width in _chunk_segments(d)]
                out_ref[d] = (parts[0] if len(parts) == 1 else jnp.concatenate(parts, axis=1)).astype(BF16)

    piece = pl.BlockSpec((tk, PIECE), lambda i: (i, 0))
    return pl.pallas_call(
        body,
        name="grad_w_in",
        grid=(s_len // tk,),
        in_specs=[pl.BlockSpec((tk, D_MODEL), lambda i: (i, 0))] + [piece] * 6 + [pl.BlockSpec((tk, 128), lambda i: (i, 0))],
        out_specs=_full((N_DEV, D_MODEL, IN_CHUNK)),
        out_shape=jax.ShapeDtypeStruct((N_DEV, D_MODEL, IN_CHUNK), BF16),
        scratch_shapes=[pltpu.VMEM((D_MODEL, MY_WIDTH), F32)],
        compiler_params=_params(("arbitrary",)),
    )(h16, *pieces, d_f)


def _adamw(w, g, m, v):
    m = ADAM_B1 * m + (1.0 - ADAM_B1) * g
    v = ADAM_B2 * v + (1.0 - ADAM_B2) * (g * g)
    m_hat = m / (1.0 - ADAM_B1 ** ADAM_STEP)
    v_hat = v / (1.0 - ADAM_B2 ** ADAM_STEP)
    return -ADAM_LR * (m_hat / (jnp.sqrt(v_hat) + ADAM_EPS) + ADAM_WD * w), m, v


PARAM_ORDER = ("norm_g", "w_in", "b_f", "w_pool", "pool_scale", "fox_q_g", "fox_k_g", "mem_norm_g", "w_mem_kv", "mem_q_g", "mem_k_g", "w_out")


def _update(red_in, red_kv, red_out, sred, params):
    def body(rin_ref, rkv_ref, rout_ref, sred_ref, *refs):
        ins, outs = refs[:3 * len(PARAM_ORDER)], refs[3 * len(PARAM_ORDER):]
        wide = lambda row: jnp.concatenate([sred_ref[row + k:row + k + 1, :] for k in range(4)], axis=1)
        head = lambda row: sred_ref[row:row + 1, 0:HEAD_DIM]
        grads = {
            "norm_g": lambda: wide(SB_NORM_G), "w_in": lambda: rin_ref[...], "b_f": lambda: sred_ref[SB_B_F:SB_B_F + 1, 0:FOX_HEADS],
            "pool_scale": lambda: sred_ref[SB_POOL_SCALE:SB_POOL_SCALE + 1, :], "fox_q_g": lambda: head(SB_FOX_Q),
            "fox_k_g": lambda: head(SB_FOX_K), "mem_norm_g": lambda: wide(SB_MEM_NORM_G), "w_mem_kv": lambda: rkv_ref[...],
            "mem_q_g": lambda: head(SB_MEM_Q), "mem_k_g": lambda: head(SB_MEM_K), "w_out": lambda: rout_ref[...],
        }
        for p, name in enumerate(PARAM_ORDER):
            w_ref, m_ref, v_ref = ins[3 * p:3 * p + 3]
            g_out, d_out, m_out, v_out = outs[4 * p:4 * p + 4]
            if name == "w_pool":
                for grp in range(4):
                    g = sred_ref[SB_W_POOL:SB_W_POOL + 64, 64 * grp:64 * (grp + 1)]
                    delta, m_new, v_new = _adamw(w_ref[grp], g, m_ref[grp], v_ref[grp])
                    g_out[grp], d_out[grp], m_out[grp], v_out[grp] = g, delta, m_new, v_new
            else:
                g = grads[name]()
                delta, m_new, v_new = _adamw(w_ref[...], g, m_ref[...], v_ref[...])
                g_out[...], d_out[...], m_out[...], v_out[...] = g, delta, m_new, v_new

    flat = [t for name in PARAM_ORDER for t in params[name]]
    shapes = [params[name][0].shape for name in PARAM_ORDER for _ in range(4)]
    outs = pl.pallas_call(
        body,
        name="adamw_update",
        out_shape=tuple(jax.ShapeDtypeStruct(s, F32) for s in shapes),
        in_specs=[pl.BlockSpec(memory_space=pltpu.VMEM)] * (4 + len(flat)),
        out_specs=(pl.BlockSpec(memory_space=pltpu.VMEM),) * len(shapes),
        compiler_params=pltpu.CompilerParams(vmem_limit_bytes=VMEM_LIMIT),
    )(red_in, red_kv, red_out, sred, *flat)
    return {name: outs[4 * p:4 * p + 4] for p, name in enumerate(PARAM_ORDER)}


def kernel(x, mem, norm_g, w_in, b_f, w_pool, pool_scale, fox_q_g, fox_k_g, mem_norm_g, w_mem_kv, mem_q_g, mem_k_g, w_out, loss_target, m_norm_g, m_w_in, m_b_f, m_w_pool, m_pool_scale, m_fox_q_g, m_fox_k_g, m_mem_norm_g, m_w_mem_kv, m_mem_q_g, m_mem_k_g, m_w_out, v_norm_g, v_w_in, v_b_f, v_w_pool, v_pool_scale, v_fox_q_g, v_fox_k_g, v_mem_norm_g, v_w_mem_kv, v_mem_q_g, v_mem_k_g, v_w_out):
    given = dict(norm_g=(norm_g, m_norm_g, v_norm_g), w_in=(w_in, m_w_in, v_w_in), b_f=(b_f, m_b_f, v_b_f),
                 w_pool=(w_pool, m_w_pool, v_w_pool), pool_scale=(pool_scale, m_pool_scale, v_pool_scale),
                 fox_q_g=(fox_q_g, m_fox_q_g, v_fox_q_g), fox_k_g=(fox_k_g, m_fox_k_g, v_fox_k_g),
                 mem_norm_g=(mem_norm_g, m_mem_norm_g, v_mem_norm_g), w_mem_kv=(w_mem_kv, m_w_mem_kv, v_w_mem_kv),
                 mem_q_g=(mem_q_g, m_mem_q_g, v_mem_q_g), mem_k_g=(mem_k_g, m_mem_k_g, v_mem_k_g), w_out=(w_out, m_w_out, v_w_out))
    params = {name: tuple(t[0] if t.ndim > 2 else t for t in wmv) for name, wmv in given.items()}
    x2, mem2, target2 = x[0], mem[0], loss_target[0]

    w_my, w_kv16, w_out16, wp_bd, bf_pad = _gather_w_in(params["w_in"][0], params["w_mem_kv"][0], params["w_out"][0],
                                                         params["w_pool"][0], b_f)
    proj, h16 = _fwd_proj(x2, norm_g, w_my)
    q_aug, k_aug, v_h, kt_aug, vt_aug = _fox_prep(proj, bf_pad, fox_q_g, fox_k_g)
    olse, lse_rows, w_kv_all, w_out_all = _fox_fwd(q_aug, k_aug, vt_aug, w_kv16, w_out16)
    km, vm = _mem_prep(mem2, mem_norm_g, w_kv_all, mem_k_g)
    mixed = _mix_fwd(proj, olse, km, vm, wp_bd, pool_scale, mem_q_g)
    d_out, d_mixed, dw_out, loss_blk = _out_loss(mixed, x2, target2, w_out_all)

    d_ua, d_gb, d_qm, d_o, delta_rows, dwp_bd, d_scale, dkm, dvm, d_gqm = _mix_bwd(proj, olse, d_mixed, km, vm, wp_bd, pool_scale, mem_q_g)
    dw_kv, d_mem_g, d_gkm = _mem_bwd(mem2, mem_norm_g, w_kv_all, mem_k_g, dkm, dvm)
    dq_aug, dk_aug, dv_h, land_kv, land_out = _fox_bwd(q_aug, k_aug, kt_aug, v_h, d_o, lse_rows, delta_rows, dw_kv, dw_out)
    d_q, d_k, d_v, d_f, d_gq, d_gk, d_bf = _fox_post(proj, bf_pad, fox_q_g, fox_k_g, dq_aug, dk_aug, dv_h)
    pieces = (d_ua, d_q, d_k, d_v, d_gb, d_qm)
    dw_in = _grad_w_in(h16, pieces, d_f)
    grad_x, d_norm_g = _grad_x(pieces, d_f, w_my, x2, norm_g, d_out)

    red_in, red_kv, red_out, sred = _exchange_grads(dw_in, land_kv, land_out, d_norm_g, d_mem_g, d_scale, d_bf, d_gq, d_gk, d_gqm,
                                                    d_gkm, dwp_bd, loss_blk)
    new = _update(red_in, red_kv, red_out, sred, params)
    result = [sred[SB_LOSS, 0], grad_x[None]]
    for kind in range(4):
        for name in PARAM_ORDER:
            out = new[name][kind]
            result.append(out[None] if given[name][0].ndim > 2 else out)
    return tuple(result)
```

```python
import functools

import jax
import jax.numpy as jnp
import numpy as np
from jax import lax
from jax.experimental import pallas as pl
from jax.experimental.pallas import tpu as pltpu

F32 = jnp.float32
BF16 = jnp.bfloat16
MESH = pl.DeviceIdType.MESH

N_DEV = 8
D_MODEL = 1024
HEAD_DIM = 64
FOX_HEADS = 8
MEM_HEADS = 4
N_MEM = 256
POOL_WIDTH = 256
EPS = 1e-6
IN_WIDTH = 3080
IN_CHUNK = IN_WIDTH // N_DEV
F_OFF = 2048
MY_WIDTH = 3200
MY_F_OFF = 3072
PIECE = 512
TM = 256
TMM = 512
TA = 256
HB = 4
HALO = 16
VMEM_LIMIT = 56 * 1024 * 1024

ADAM_LR = 0.001
ADAM_B1 = 0.9
ADAM_B2 = 0.999
ADAM_EPS = 1e-08
ADAM_WD = 0.01
ADAM_STEP = 10


def _dot(a, b):
    return jnp.dot(a, b, preferred_element_type=F32)


def _dot_nt(a, b):
    return lax.dot_general(a, b, (((1,), (1,)), ((), ())), preferred_element_type=F32)


def _dot_tn(a, b):
    return lax.dot_general(a, b, (((0,), (0,)), ((), ())), preferred_element_type=F32)


def _sigmoid(g):
    return 0.5 + 0.5 * jnp.tanh(0.5 * g)


def _split3(v):
    hi = v.astype(BF16)
    r1 = v - hi.astype(F32)
    mid = r1.astype(BF16)
    lo = (r1 - mid.astype(F32)).astype(BF16)
    return hi, mid, lo


def _rms(v):
    return lax.rsqrt(jnp.mean(v * v, axis=-1, keepdims=True) + EPS)


def _rms_bwd(a, r, g, dy):
    da = dy * g
    return r * (da - a * jnp.mean(da * a, axis=-1, keepdims=True)), dy * a


def _head_mean(z, seg):
    hi = z.astype(BF16)
    lo = (z - hi.astype(F32)).astype(BF16)
    return _dot(hi, seg) + _dot(lo, seg)


def _head_rms(v, seg):
    return lax.rsqrt(_head_mean(v * v, seg) + EPS)


def _head_rms_bwd(a, r, g, dy, seg):
    da = dy * g
    return r * (da - a * _head_mean(da * a, seg)), dy * a


def _fold_heads(row, n_heads):
    out = row[:, 0:HEAD_DIM]
    for h in range(1, n_heads):
        out = out + row[:, HEAD_DIM * h:HEAD_DIM * (h + 1)]
    return out


def _params(sem, limit=VMEM_LIMIT):
    return pltpu.CompilerParams(dimension_semantics=sem, vmem_limit_bytes=limit)


def _full(shape):
    return pl.BlockSpec(shape, lambda *_: (0,) * len(shape))


def _chunk_segments(d):
    runs = []
    for lo, hi, shift in ((0, F_OFF, 0), (F_OFF, F_OFF + FOX_HEADS, MY_F_OFF - F_OFF), (F_OFF + FOX_HEADS, IN_WIDTH, -FOX_HEADS)):
        a, b = max(lo, IN_CHUNK * d), min(hi, IN_CHUNK * (d + 1))
        if a < b:
            runs.append((a + shift, b - a))
    return runs


def _my_place():
    x, y, c = lax.axis_index("x"), lax.axis_index("y"), lax.axis_index("c")
    return x, y, c, 4 * x + 2 * y + c


def _shard_rows(dev):
    return pl.ds(pl.multiple_of(128 * dev, 128), 128)


def _gather_w_in(w_in, w_kv, w_out, w_pool, b_f, gq, gk, gqm):
    def body(win_ref, wkv_ref, wout_ref, wp_ref, bf_ref, gq_ref, gk_ref, gqm_ref, wmy_ref, kv16_ref, out16_ref, wpbd_ref, bfpad_ref,
             gq8_ref, gk8_ref, gqm4_ref, in_all, pay_in, send_sems, recv_sems):
        x, y, c, me = _my_place()
        sibling = (x, y, 1 - c)
        chips = [(1 - x, y), (x, 1 - y), (1 - x, 1 - y)]

        pay_in[...] = win_ref[...].astype(BF16)

        def copy(k, block, to, own=False):
            px, py, pc = block
            dst = in_all.at[4 * px + 2 * py + pc]
            return pltpu.make_async_remote_copy(src_ref=pay_in if own else dst, dst_ref=dst, send_sem=send_sems.at[k],
                                                recv_sem=recv_sems.at[k], device_id=to, device_id_type=MESH)

        first = [copy(0, (x, y, c), sibling, own=True)]
        first += [copy(1 + j, (x, y, c), (*chip, c), own=True) for j, chip in enumerate(chips)]
        for cp in first:
            cp.start()
        in_all[me] = pay_in[...]
        kv16_ref[...] = wkv_ref[...].astype(BF16)
        out16_ref[...] = wout_ref[...].astype(BF16)
        wpbd_ref[...] = jnp.zeros((POOL_WIDTH, POOL_WIDTH), BF16)
        for grp in range(4):
            sl = slice(64 * grp, 64 * (grp + 1))
            wpbd_ref[sl, sl] = wp_ref[grp].astype(BF16)
        bfpad_ref[...] = jnp.zeros((1, 128), F32)
        bfpad_ref[:, 0:FOX_HEADS] = bf_ref[...]
        gq8_ref[...] = jnp.concatenate([gq_ref[...]] * FOX_HEADS, axis=1)
        gk8_ref[...] = jnp.concatenate([gk_ref[...]] * FOX_HEADS, axis=1)
        gqm4_ref[...] = jnp.concatenate([gqm_ref[...]] * MEM_HEADS, axis=1)
        passed = []
        for j, chip in enumerate(chips):
            copy(1 + j, (*chip, c), (x, y, c)).wait_recv()
            passed.append(copy(4 + j, (*chip, c), sibling))
            passed[-1].start()
        copy(0, sibling, (x, y, c)).wait_recv()
        for j, chip in enumerate(chips):
            copy(4 + j, (*chip, 1 - c), (x, y, c)).wait_recv()
        for cp in first + passed:
            cp.wait_send()

        for r0 in range(0, D_MODEL, 256):
            ch = [in_all[d, r0:r0 + 256, :].astype(F32) for d in range(N_DEV)]
            lo = F_OFF - 5 * IN_CHUNK
            full = jnp.concatenate(ch[:5] + [ch[5][:, :lo], ch[5][:, lo + FOX_HEADS:], ch[6], ch[7], ch[5][:, lo:lo + FOX_HEADS],
                                             jnp.zeros((256, MY_WIDTH - IN_WIDTH), F32)], axis=1)
            wmy_ref[r0:r0 + 256, :] = full.astype(BF16)

    return pl.pallas_call(
        body,
        name="gather_w_in",
        out_shape=(jax.ShapeDtypeStruct((D_MODEL, MY_WIDTH), BF16), jax.ShapeDtypeStruct((128, 512), BF16),
                   jax.ShapeDtypeStruct((128, D_MODEL), BF16), jax.ShapeDtypeStruct((POOL_WIDTH, POOL_WIDTH), BF16),
                   jax.ShapeDtypeStruct((1, 128), F32), jax.ShapeDtypeStruct((1, PIECE), F32), jax.ShapeDtypeStruct((1, PIECE), F32),
                   jax.ShapeDtypeStruct((1, 256), F32)),
        in_specs=[pl.BlockSpec(memory_space=pltpu.VMEM)] * 8,
        out_specs=(pl.BlockSpec(memory_space=pltpu.VMEM),) * 8,
        scratch_shapes=[
            pltpu.VMEM((N_DEV, D_MODEL, IN_CHUNK), BF16),
            pltpu.VMEM((D_MODEL, IN_CHUNK), BF16),
            pltpu.SemaphoreType.DMA((7,)),
            pltpu.SemaphoreType.DMA((7,)),
        ],
        compiler_params=pltpu.CompilerParams(vmem_limit_bytes=VMEM_LIMIT),
    )(w_in, w_kv, w_out, w_pool, b_f, gq, gk, gqm)


def _row_block_exchange(kv_ref, out_ref, kv_dst, out_dst, send_sems, recv_sems, local_sems, gather):
    x, y, c, me = _my_place()
    remote = []
    for k in range(1, N_DEV):
        px, py, pc = x ^ (k >> 2), y ^ ((k >> 1) & 1), c ^ (k & 1)
        peer = 4 * px + 2 * py + pc
        for fam, (src, dst) in enumerate(((kv_ref, kv_dst), (out_ref, out_dst))):
            remote.append(pltpu.make_async_remote_copy(
                src_ref=src if gather else src.at[_shard_rows(peer)], dst_ref=dst.at[_shard_rows(me)] if gather else dst.at[me],
                send_sem=send_sems.at[7 * fam + k - 1], recv_sem=recv_sems.at[7 * fam + k - 1],
                device_id=(px, py, pc), device_id_type=MESH))
    local = [pltpu.make_async_copy(src if gather else src.at[_shard_rows(me)], dst.at[_shard_rows(me)] if gather else dst.at[me],
                                   local_sems.at[fam]) for fam, (src, dst) in enumerate(((kv_ref, kv_dst), (out_ref, out_dst)))]
    return remote, local


SB_ROWS, SB_W = 80, 256
SB_NORM_G, SB_MEM_NORM_G, SB_POOL_SCALE, SB_B_F, SB_FOX_Q, SB_FOX_K, SB_MEM_Q, SB_MEM_K, SB_LOSS, SB_W_POOL = 0, 4, 8, 9, 10, 11, 12, 13, 14, 16


def _exchange_grads(dw_in, land_kv, land_out, d_norm_g, d_mem_g, d_scale, d_bf, d_gq, d_gk, d_gqm, d_gkm, dwp_bd, loss_blk):
    half = D_MODEL // 2

    def body(in_ref, lkv_ref, lout_ref, dng, dmg, dsc, dbf, dgq, dgk, dgqm, dgkm, dwp, loss_ref,
             rin_ref, rkv_ref, rout_ref, sred_ref, land1, send2a, send2b, keep2a, keep2b, land2a, land2b,
             send3a, send3b, land3a, land3b, sb_ref, sland, send_sems, recv_sems):
        x, y, c, me = _my_place()
        sibling, x_nbr, y_nbr = (x, y, 1 - c), (1 - x, y, c), (x, 1 - y, c)

        def rcopy(src, dst, sem, to):
            return pltpu.make_async_remote_copy(src_ref=src, dst_ref=dst, send_sem=send_sems.at[sem], recv_sem=recv_sems.at[sem],
                                                device_id=to, device_id_type=MESH)

        sb_ref[...] = jnp.zeros((SB_ROWS, SB_W), F32)
        for k in range(4):
            sb_ref[SB_NORM_G + k:SB_NORM_G + k + 1, :] = dng[:, SB_W * k:SB_W * (k + 1)]
            sb_ref[SB_MEM_NORM_G + k:SB_MEM_NORM_G + k + 1, :] = dmg[:, SB_W * k:SB_W * (k + 1)]
        sb_ref[SB_POOL_SCALE:SB_POOL_SCALE + 1, :] = dsc[...]
        sb_ref[SB_B_F:SB_B_F + 1, 0:128] = dbf[...]
        for row, ref in ((SB_FOX_Q, dgq), (SB_FOX_K, dgk), (SB_MEM_Q, dgqm), (SB_MEM_K, dgkm)):
            sb_ref[row:row + 1, 0:HEAD_DIM] = ref[...]
        sb_ref[SB_LOSS:SB_LOSS + 1, 0:128] = loss_ref[0:1, :]
        for g in range(4):
            sl = slice(64 * g, 64 * (g + 1))
            sb_ref[SB_W_POOL:SB_W_POOL + 64, sl] = dwp[sl, sl]

        small = []
        for k in range(1, N_DEV):
            px, py, pc = x ^ (k >> 2), y ^ ((k >> 1) & 1), c ^ (k & 1)
            small.append(rcopy(sb_ref, sland.at[me], k - 1, (px, py, pc)))
        stage1 = [rcopy(in_ref.at[2 * k + 1 - c], land1.at[k], 7 + k, sibling) for k in range(4)]
        for cp in small + stage1:
            cp.start()
        sland[me] = sb_ref[...]
        for cp in stage1:
            cp.wait_recv()

        top, bot = slice(0, half), slice(half, D_MODEL)

        def chip_sum(k, rows):
            return in_ref[2 * k + c, rows, :].astype(F32) + land1[k, rows, :].astype(F32)

        for j in range(2):
            send2a[j] = chip_sum(2 * (1 - x) + j, top).astype(BF16)
            keep2a[j] = chip_sum(2 * x + j, top)
            send2b[j] = chip_sum(2 * j + 1 - y, bot).astype(BF16)
            keep2b[j] = chip_sum(2 * j + y, bot)
        stage2 = [rcopy(send2a.at[j], land2a.at[j], 11 + j, x_nbr) for j in range(2)]
        stage2 += [rcopy(send2b.at[j], land2b.at[j], 13 + j, y_nbr) for j in range(2)]
        for cp in stage2:
            cp.start()
        for cp in stage2:
            cp.wait_recv()
        for j in range(2):
            keep2a[j] = keep2a[j] + land2a[j].astype(F32)
            keep2b[j] = keep2b[j] + land2b[j].astype(F32)
        send3a[...] = keep2a[1 - y].astype(BF16)
        send3b[...] = keep2b[1 - x].astype(BF16)
        stage3 = [rcopy(send3a, land3a, 15, y_nbr), rcopy(send3b, land3b, 16, x_nbr)]
        for cp in stage3:
            cp.start()
        for cp in stage3:
            cp.wait_recv()
        rin_ref[top, :] = keep2a[y] + land3a[...].astype(F32)
        rin_ref[bot, :] = keep2b[x] + land3b[...].astype(F32)

        for cp in small:
            cp.wait_recv()
        for cp in small + stage1 + stage2 + stage3:
            cp.wait_send()
        for land, red in ((lkv_ref, rkv_ref), (lout_ref, rout_ref), (sland, sred_ref)):
            acc = land[0].astype(F32)
            for d in range(1, N_DEV):
                acc = acc + land[d].astype(F32)
            red[...] = acc

    args = (dw_in, land_kv, land_out, d_norm_g, d_mem_g, d_scale, d_bf, d_gq, d_gk, d_gqm, d_gkm, dwp_bd, loss_blk)
    half_chunk = lambda n, dt: pltpu.VMEM((n, half, IN_CHUNK), dt)
    return pl.pallas_call(
        body,
        name="exchange_grads",
        out_shape=(jax.ShapeDtypeStruct((D_MODEL, IN_CHUNK), F32), jax.ShapeDtypeStruct((128, 512), F32),
                   jax.ShapeDtypeStruct((128, D_MODEL), F32), jax.ShapeDtypeStruct((SB_ROWS, SB_W), F32)),
        in_specs=[pl.BlockSpec(memory_space=pltpu.VMEM)] * len(args),
        out_specs=(pl.BlockSpec(memory_space=pltpu.VMEM),) * 4,
        scratch_shapes=[
            pltpu.VMEM((4, D_MODEL, IN_CHUNK), BF16),
            half_chunk(2, BF16), half_chunk(2, BF16), half_chunk(2, F32), half_chunk(2, F32), half_chunk(2, BF16), half_chunk(2, BF16),
            pltpu.VMEM((half, IN_CHUNK), BF16), pltpu.VMEM((half, IN_CHUNK), BF16),
            pltpu.VMEM((half, IN_CHUNK), BF16), pltpu.VMEM((half, IN_CHUNK), BF16),
            pltpu.VMEM((SB_ROWS, SB_W), F32),
            pltpu.VMEM((N_DEV, SB_ROWS, SB_W), F32),
            pltpu.SemaphoreType.DMA((17,)),
            pltpu.SemaphoreType.DMA((17,)),
        ],
        compiler_params=pltpu.CompilerParams(vmem_limit_bytes=VMEM_LIMIT),
    )(*args)


def _fwd_proj(x, norm_g, w_my):
    s_len = x.shape[0]

    def body(x_ref, g_ref, w_ref, proj_ref, h_ref):
        xv = x_ref[...]
        h = (xv * _rms(xv) * g_ref[...]).astype(BF16)
        h_ref[...] = h
        proj_ref[...] = _dot(h, w_ref[...])

    return pl.pallas_call(
        body,
        name="fwd_proj",
        grid=(s_len // TMM,),
        in_specs=[pl.BlockSpec((TMM, D_MODEL), lambda i: (i, 0)), _full((1, D_MODEL)), _full((D_MODEL, MY_WIDTH))],
        out_specs=[pl.BlockSpec((TMM, MY_WIDTH), lambda i: (i, 0)), pl.BlockSpec((TMM, D_MODEL), lambda i: (i, 0))],
        out_shape=[jax.ShapeDtypeStruct((s_len, MY_WIDTH), F32), jax.ShapeDtypeStruct((s_len, D_MODEL), BF16)],
        compiler_params=_params(("parallel",)),
    )(x, norm_g, w_my)


def _log_sigmoid(z):
    return jnp.minimum(z, 0.0) - jnp.log(1.0 + jnp.exp(-jnp.abs(z)))


def _fox_prep(proj, bf_pad, gq, gk, seg):
    s_len = proj.shape[0]

    def body(q_ref, k_ref, v_ref, f_ref, bf_ref, gq_ref, gk_ref, seg_ref, qa_ref, ka_ref, vh_ref, kt_ref, vt_ref, carry_ref):
        @pl.when(pl.program_id(0) == 0)
        def _():
            carry_ref[...] = jnp.zeros((1, 128), F32)

        lane = lax.broadcasted_iota(jnp.int32, (TM, 128), 1)
        logf = jnp.where(lane < FOX_HEADS, _log_sigmoid(f_ref[...] + bf_ref[...]), 0.0)
        row = lax.broadcasted_iota(jnp.int32, (TM, TM), 0)
        col = lax.broadcasted_iota(jnp.int32, (TM, TM), 1)
        tri = jnp.where(col <= row, 1.0, 0.0).astype(BF16)
        hi, mid, lo = _split3(logf)
        cum = _dot(tri, hi) + _dot(tri, mid) + _dot(tri, lo) + carry_ref[...]
        carry_ref[...] = cum[TM - 1:TM, :]
        f_hi, f_mid, f_lo = [t.astype(F32) for t in _split3(cum)]
        zeros = jnp.zeros((TM, HEAD_DIM), F32)
        q_all = q_ref[...]
        k_all = k_ref[...]
        qn_all = q_all * _head_rms(q_all, seg_ref[...]) * gq_ref[...] * 0.125
        kn_all = k_all * _head_rms(k_all, seg_ref[...]) * gk_ref[...]
        for h in range(FOX_HEADS):
            sl = slice(HEAD_DIM * h, HEAD_DIM * (h + 1))
            qn = qn_all[:, sl]
            kn = kn_all[:, sl]
            a, b, c = f_hi[:, h:h + 1], f_mid[:, h:h + 1], f_lo[:, h:h + 1]
            qa = jnp.where(lane < 64, jnp.concatenate([qn, zeros], axis=1),
                           jnp.where(lane == 64, a, jnp.where(lane == 65, b, jnp.where(lane == 66, c,
                                                                                         jnp.where(lane < 70, 1.0, 0.0)))))
            ka = jnp.where(lane < 64, jnp.concatenate([kn, zeros], axis=1),
                           jnp.where(lane < 67, 1.0, jnp.where(lane == 67, -a, jnp.where(lane == 68, -b,
                                                                                          jnp.where(lane == 69, -c, 0.0)))))
            vh = v_ref[:, sl]
            v_aug = jnp.where(lane < 64, jnp.concatenate([vh, zeros], axis=1), jnp.where(lane == 64, 1.0, 0.0))
            qa_ref[h] = qa.astype(BF16)
            ka_ref[h] = ka.astype(BF16)
            vh_ref[h] = vh.astype(BF16)
            kt_ref[h, 0] = ka.T.astype(BF16)
            vt_ref[h, 0] = v_aug.T.astype(BF16)

    col_blk = lambda j: pl.BlockSpec((TM, PIECE), lambda i: (i, j))
    head_blk = lambda w: pl.BlockSpec((FOX_HEADS, TM, w), lambda i: (0, i, 0))
    tile_blk = pl.BlockSpec((FOX_HEADS, 1, 128, TM), lambda i: (0, i, 0, 0))
    tiled = jax.ShapeDtypeStruct((FOX_HEADS, s_len // TM, 128, TM), BF16)
    return pl.pallas_call(
        body,
        name="fox_prep",
        grid=(s_len // TM,),
        in_specs=[col_blk(1), col_blk(2), col_blk(3), pl.BlockSpec((TM, 128), lambda i: (i, MY_F_OFF // 128)),
                  _full((1, 128)), _full((1, PIECE)), _full((1, PIECE)), _full((PIECE, PIECE))],
        out_specs=[head_blk(128), head_blk(128), head_blk(HEAD_DIM), tile_blk, tile_blk],
        out_shape=[jax.ShapeDtypeStruct((FOX_HEADS, s_len, 128), BF16), jax.ShapeDtypeStruct((FOX_HEADS, s_len, 128), BF16),
                   jax.ShapeDtypeStruct((FOX_HEADS, s_len, HEAD_DIM), BF16), tiled, tiled],
        scratch_shapes=[pltpu.VMEM((1, 128), F32)],
        compiler_params=_params(("arbitrary",)),
    )(proj, proj, proj, proj, bf_pad, gq, gk, seg)


def _mem_prep(mem, g, w_kv, gk):
    def body(mem_ref, g_ref, w_ref, gk_ref, km_ref, vm_ref):
        m = mem_ref[...]
        kv = _dot((m * _rms(m) * g_ref[...]).astype(BF16), w_ref[...])
        for h in range(MEM_HEADS):
            sl = slice(HEAD_DIM * h, HEAD_DIM * (h + 1))
            kh = kv[:, sl]
            km_ref[:, sl] = (kh * _rms(kh) * gk_ref[...]).astype(BF16)
        vm_ref[...] = kv[:, 256:512].astype(BF16)

    return pl.pallas_call(
        body,
        name="mem_prep",
        out_shape=(jax.ShapeDtypeStruct((N_MEM, 256), BF16),) * 2,
        in_specs=[pl.BlockSpec(memory_space=pltpu.VMEM)] * 4,
        out_specs=(pl.BlockSpec(memory_space=pltpu.VMEM),) * 2,
        compiler_params=pltpu.CompilerParams(vmem_limit_bytes=VMEM_LIMIT),
    )(mem, g, w_kv, gk)


def _causal_mask():
    row = lax.broadcasted_iota(jnp.int32, (TA, TA), 0)
    col = lax.broadcasted_iota(jnp.int32, (TA, TA), 1)
    return col <= row


def _causal_mask_t():
    row = lax.broadcasted_iota(jnp.int32, (TA, TA), 0)
    col = lax.broadcasted_iota(jnp.int32, (TA, TA), 1)
    return row <= col


def _fox_fwd(q_aug, k_aug, vt_aug, w_kv16, w_out16):
    s_len = q_aug.shape[1]
    n_tiles = s_len // TA
    n_groups = FOX_HEADS // HB

    def body(q_ref, k_ref, vt_ref, kv16_ref, out16_ref, o_ref, st_ref, kv_all, out_all, send_sems, recv_sems, local_sems):
        qi = pl.program_id(1)
        remote, local = _row_block_exchange(kv16_ref, out16_ref, kv_all, out_all, send_sems, recv_sems, local_sems, gather=True)

        @pl.when((pl.program_id(0) == 0) & (qi == 0))
        def _():
            for cp in remote + local:
                cp.start()

        qs = [q_ref[h] for h in range(HB)]

        def step(t0, carry, masked, width):
            rows = pl.ds(pl.multiple_of(t0 * TA, TA), width * TA)
            scores = [_dot_nt(k_ref[h, rows, :], qs[h]) for h in range(HB)]
            soft = []
            for h in range(HB):
                m, _ = carry[h]
                s = jnp.where(_causal_mask_t(), scores[h], -1e30) if masked else scores[h]
                m_new = jnp.maximum(m, jnp.max(s, axis=0, keepdims=True))
                soft.append((m_new, jnp.exp(m - m_new), jnp.exp(s - m_new).astype(BF16)))
            out = []
            for h, (m_new, alpha, p) in enumerate(soft):
                acc = alpha * carry[h][1]
                for t in range(width):
                    acc = acc + _dot(vt_ref[h, t0 + t], p[t * TA:(t + 1) * TA])
                out.append((m_new, acc))
            return tuple(out)

        init = tuple((jnp.full((1, TA), -1e30, F32), jnp.zeros((128, TA), F32)) for _ in range(HB))
        carry = lax.fori_loop(0, qi // 2, lambda j, cr: step(2 * j, cr, False, 2), init)
        carry = lax.fori_loop(2 * (qi // 2), qi, lambda j, cr: step(j, cr, False, 1), carry)
        carry = step(qi, carry, True, 1)
        for h in range(HB):
            m, acc = carry[h]
            l = acc[HEAD_DIM:HEAD_DIM + 1, :]
            lse = m + jnp.log(l)
            o_ref[h] = jnp.concatenate([acc[0:HEAD_DIM] / l, jnp.broadcast_to(lse, (HEAD_DIM, TA))], axis=0).T
            st_ref[h, 0] = jnp.broadcast_to(lse, (8, TA))

        @pl.when((pl.program_id(0) == n_groups - 1) & (qi == n_tiles - 1))
        def _():
            for cp in remote:
                cp.wait_recv()
            for cp in remote:
                cp.wait_send()
            for cp in local:
                cp.wait()

    hbm = pl.BlockSpec(memory_space=pl.ANY)
    return pl.pallas_call(
        body,
        name="fox_fwd",
        grid=(n_groups, n_tiles),
        in_specs=[pl.BlockSpec((HB, TA, 128), lambda g, i: (g, i, 0)), pl.BlockSpec((HB, s_len, 128), lambda g, i: (g, 0, 0)),
                  pl.BlockSpec((HB, n_tiles, 128, TA), lambda g, i: (g, 0, 0, 0)), hbm, hbm],
        out_specs=[pl.BlockSpec((HB, TA, 128), lambda g, i: (g, i, 0)), pl.BlockSpec((HB, 1, 8, TA), lambda g, i: (g, i, 0, 0)),
                   hbm, hbm],
        out_shape=[jax.ShapeDtypeStruct((FOX_HEADS, s_len, 128), F32), jax.ShapeDtypeStruct((FOX_HEADS, n_tiles, 8, TA), F32),
                   jax.ShapeDtypeStruct((D_MODEL, 512), BF16), jax.ShapeDtypeStruct((D_MODEL, D_MODEL), BF16)],
        scratch_shapes=[pltpu.SemaphoreType.DMA((14,)), pltpu.SemaphoreType.DMA((14,)), pltpu.SemaphoreType.DMA((2,))],
        compiler_params=_params(("arbitrary", "arbitrary")),
    )(q_aug, k_aug, vt_aug, w_kv16, w_out16)


def _lane_group(lane, a, b, c, d):
    return jnp.where(lane < 64, a, jnp.where(lane < 128, b, jnp.where(lane < 192, c, d)))


def _pool_count(row0):
    lane = lax.broadcasted_iota(jnp.int32, (TM, POOL_WIDTH), 1)
    t1 = row0 + lax.broadcasted_iota(jnp.int32, (TM, POOL_WIDTH), 0) + 1
    return 1.0 / jnp.minimum(t1, _lane_group(lane, 2, 4, 8, 16)).astype(F32), lane


def _pool_delta(u, halo, cnt, lane):
    xe = jnp.concatenate([halo, u], axis=0)
    s2 = xe + pltpu.roll(xe, 1, 0)
    s4 = s2 + pltpu.roll(s2, 2, 0)
    s8 = s4 + pltpu.roll(s4, 4, 0)
    s16 = s8 + pltpu.roll(s8, 8, 0)
    win = _lane_group(lane, s2[HALO:], s4[HALO:], s8[HALO:], s16[HALO:])
    return win * cnt - u


def _pool_delta_bwd(dd, dd_next, cnt, cnt_next, lane):
    ee = jnp.concatenate([dd * cnt, dd_next * cnt_next], axis=0)
    n = TM + HALO
    r2 = ee + pltpu.roll(ee, n - 1, 0)
    r4 = r2 + pltpu.roll(r2, n - 2, 0)
    r8 = r4 + pltpu.roll(r4, n - 4, 0)
    r16 = r8 + pltpu.roll(r8, n - 8, 0)
    return _lane_group(lane, r2[:TM], r4[:TM], r8[:TM], r16[:TM]) - dd


def _mem_attn(qm, gq, seg, km_ref, vm_ref):
    r = _head_rms(qm, seg)
    a = qm * r
    q16 = (a * gq * 0.125).astype(BF16)
    heads = []
    for h in range(MEM_HEADS):
        sl = slice(HEAD_DIM * h, HEAD_DIM * (h + 1))
        s = _dot_nt(q16[:, sl], km_ref[:, sl])
        e = jnp.exp(s - jnp.max(s, axis=-1, keepdims=True))
        p = e * (1.0 / jnp.sum(e, axis=-1, keepdims=True))
        heads.append((p, _dot(p.astype(BF16), vm_ref[:, sl])))
    return a, r, q16, heads


def _row_specs(s_len):
    n_halo = s_len // HALO
    piece = lambda j: pl.BlockSpec((TM, PIECE), lambda i: (i, j))
    before = lambda j: pl.BlockSpec((HALO, 256), lambda i: (jnp.maximum(i * (TM // HALO) - 1, 0), j))
    after = lambda j: pl.BlockSpec((HALO, 256), lambda i: (jnp.minimum((i + 1) * (TM // HALO), n_halo - 1), j))
    return piece, before, after


def _mix_fwd(proj, olse, km, vm, wp_bd, pool_scale, gq_m, seg):
    s_len = proj.shape[0]

    def body(ua_ref, halo_ref, gb_ref, qm_ref, ol_ref, km_ref, vm_ref, wp_ref, sc_ref, gq_ref, seg_ref, out_ref):
        i = pl.program_id(0)
        u = ua_ref[:, 0:256]
        g_a = ua_ref[:, 256:512]
        halo = jnp.where(i > 0, halo_ref[...], 0.0)
        cnt, lane = _pool_count(i * TM)
        d = _pool_delta(u, halo, cnt, lane).astype(BF16)
        y_a = _dot(d, wp_ref[...]) * sc_ref[...]
        out_ref[:, 0:256] = (y_a * (g_a * _sigmoid(g_a))).astype(BF16)
        g_b = gb_ref[...]
        y_b = jnp.concatenate([ol_ref[h][:, 0:HEAD_DIM] for h in range(FOX_HEADS)], axis=1)
        out_ref[:, 256:768] = (y_b * (g_b * _sigmoid(g_b))).astype(BF16)
        _, _, _, heads = _mem_attn(qm_ref[:, 0:256], gq_ref[...], seg_ref[0:256, 0:256], km_ref, vm_ref)
        g_m = qm_ref[:, 256:512]
        y_m = jnp.concatenate([y for _, y in heads], axis=1)
        out_ref[:, 768:1024] = (y_m * (g_m * _sigmoid(g_m))).astype(BF16)

    piece, before, _ = _row_specs(s_len)
    return pl.pallas_call(
        body,
        name="mix_fwd",
        grid=(s_len // TM,),
        in_specs=[piece(0), before(0), piece(4), piece(5), pl.BlockSpec((FOX_HEADS, TM, 128), lambda i: (0, i, 0)),
                  _full((N_MEM, 256)), _full((N_MEM, 256)), _full((256, 256)), _full((1, 256)), _full((1, 256)),
                  _full((PIECE, PIECE))],
        out_specs=pl.BlockSpec((TM, D_MODEL), lambda i: (i, 0)),
        out_shape=jax.ShapeDtypeStruct((s_len, D_MODEL), BF16),
        compiler_params=_params(("parallel",)),
    )(proj, proj, proj, proj, olse, km, vm, wp_bd, pool_scale, gq_m, seg)


def _out_loss(mixed, x, target, w_out):
    s_len = x.shape[0]

    def body(mix_ref, x_ref, t_ref, w_ref, dout_ref, dmix_ref, dw16_ref, loss_ref, dw_ref):
        @pl.when(pl.program_id(0) == 0)
        def _():
            dw_ref[...] = jnp.zeros((D_MODEL, D_MODEL), F32)
            loss_ref[...] = jnp.zeros((8, 128), F32)

        mixed16 = mix_ref[...]
        err = x_ref[...] + _dot(mixed16, w_ref[...]) - t_ref[...]
        loss_ref[...] += 0.5 * jnp.sum(jnp.mean(err * err, axis=-1, keepdims=True))
        d_out = err / float(D_MODEL)
        dout_ref[...] = d_out
        d16 = d_out.astype(BF16)
        dmix_ref[...] = _dot_nt(d16, w_ref[...])
        dw_ref[...] += _dot_tn(mixed16, d16)

        @pl.when(pl.program_id(0) == pl.num_programs(0) - 1)
        def _():
            dw16_ref[...] = dw_ref[...].astype(BF16)

    row = pl.BlockSpec((TMM, D_MODEL), lambda i: (i, 0))
    return pl.pallas_call(
        body,
        name="out_loss",
        grid=(s_len // TMM,),
        in_specs=[row, row, row, _full((D_MODEL, D_MODEL))],
        out_specs=[row, row, _full((D_MODEL, D_MODEL)), _full((8, 128))],
        out_shape=[jax.ShapeDtypeStruct((s_len, D_MODEL), F32), jax.ShapeDtypeStruct((s_len, D_MODEL), F32),
                   jax.ShapeDtypeStruct((D_MODEL, D_MODEL), BF16), jax.ShapeDtypeStruct((8, 128), F32)],
        scratch_shapes=[pltpu.VMEM((D_MODEL, D_MODEL), F32)],
        compiler_params=_params(("arbitrary",)),
    )(mixed, x, target, w_out)


def _silu_pair(g):
    s = _sigmoid(g)
    return g * s, s * (1.0 + g * (1.0 - s))


def _mix_bwd(proj, olse, d_mixed, km, vm, wp_bd, pool_scale, gq_m, seg):
    s_len = proj.shape[0]
    n_tiles = s_len // TM

    def body(ua_ref, halo_ref, ga_next_ref, gb_ref, qm_ref, ol_ref, dm_ref, dm_next_ref, km_ref, vm_ref, wp_ref, sc_ref, gq_ref,
             seg_ref, dua_ref, dgb_ref, dqm_ref, do_ref, dl_ref, dwp_ref, dsc_ref, dkm_ref, dvm_ref, dgq_ref):
        i = pl.program_id(0)

        @pl.when(i == 0)
        def _():
            dwp_ref[...] = jnp.zeros((256, 256), F32)
            dsc_ref[...] = jnp.zeros((1, 256), F32)
            dkm_ref[...] = jnp.zeros((N_MEM, 256), F32)
            dvm_ref[...] = jnp.zeros((N_MEM, 256), F32)
            dgq_ref[...] = jnp.zeros((1, HEAD_DIM), F32)

        u = ua_ref[:, 0:256]
        g_a = ua_ref[:, 256:512]
        halo = jnp.where(i > 0, halo_ref[...], 0.0)
        cnt, lane = _pool_count(i * TM)
        d16 = _pool_delta(u, halo, cnt, lane).astype(BF16)
        t = _dot(d16, wp_ref[...])
        y_a = t * sc_ref[...]
        silu_a, dsilu_a = _silu_pair(g_a)
        dm_a = dm_ref[:, 0:256]
        dy_a = dm_a * silu_a
        dsc_ref[...] += jnp.sum(dy_a * t, axis=0, keepdims=True)
        dt16 = (dy_a * sc_ref[...]).astype(BF16)
        dwp_ref[...] += _dot_tn(d16, dt16)
        dd = _dot_nt(dt16, wp_ref[...])
        g_next = ga_next_ref[...]
        dt_next = (dm_next_ref[...] * (g_next * _sigmoid(g_next)) * sc_ref[...]).astype(BF16)
        dd_next = jnp.where(i < n_tiles - 1, _dot_nt(dt_next, wp_ref[...]), 0.0)
        cnt_next = _pool_count((i + 1) * TM)[0][0:HALO]
        dua_ref[:, 0:256] = _pool_delta_bwd(dd, dd_next, cnt, cnt_next, lane).astype(BF16)
        dua_ref[:, 256:512] = (dm_a * y_a * dsilu_a).astype(BF16)

        silu_b, dsilu_b = _silu_pair(gb_ref[...])
        dm_b = dm_ref[:, 256:768]
        o_all = jnp.concatenate([ol_ref[h][:, 0:HEAD_DIM] for h in range(FOX_HEADS)], axis=1)
        d_o = dm_b * silu_b
        dgb_ref[...] = (dm_b * o_all * dsilu_b).astype(BF16)
        for h in range(FOX_HEADS):
            do_ref[h] = d_o[:, HEAD_DIM * h:HEAD_DIM * (h + 1)].astype(BF16)
        prod = d_o * o_all
        hi = prod.astype(BF16)
        lo = (prod - hi.astype(F32)).astype(BF16)
        head_of_lane = lax.broadcasted_iota(jnp.int32, (FOX_HEADS, PIECE), 1) // HEAD_DIM
        pick = jnp.where(head_of_lane == lax.broadcasted_iota(jnp.int32, (FOX_HEADS, PIECE), 0), 1.0, 0.0).astype(BF16)
        delta = _dot_nt(pick, hi) + _dot_nt(pick, lo)
        for h in range(FOX_HEADS):
            dl_ref[h, 0] = jnp.broadcast_to(delta[h:h + 1, :], (8, TM))

        seg = seg_ref[0:256, 0:256]
        a, r, q16, heads = _mem_attn(qm_ref[:, 0:256], gq_ref[...], seg, km_ref, vm_ref)
        silu_m, dsilu_m = _silu_pair(qm_ref[:, 256:512])
        dm_m = dm_ref[:, 768:1024]
        y_m = jnp.concatenate([y for _, y in heads], axis=1)
        dqm_ref[:, 256:512] = (dm_m * y_m * dsilu_m).astype(BF16)
        dy16_all = (dm_m * silu_m).astype(BF16)
        d_scores = []
        for h in range(MEM_HEADS):
            sl = slice(HEAD_DIM * h, HEAD_DIM * (h + 1))
            p = heads[h][0]
            dy16 = dy16_all[:, sl]
            dp = _dot_nt(dy16, vm_ref[:, sl])
            dvm_ref[:, sl] += _dot_tn(p.astype(BF16), dy16)
            ds16 = (p * (dp - jnp.sum(dp * p, axis=-1, keepdims=True))).astype(BF16)
            dkm_ref[:, sl] += _dot_tn(ds16, q16[:, sl])
            d_scores.append(_dot(ds16, km_ref[:, sl]))
        dq, dg_rows = _head_rms_bwd(a, r, gq_ref[...], jnp.concatenate(d_scores, axis=1) * 0.125, seg)
        dqm_ref[:, 0:256] = dq.astype(BF16)
        dgq_ref[...] += _fold_heads(jnp.sum(dg_rows, axis=0, keepdims=True), MEM_HEADS)

    piece, before, after = _row_specs(s_len)
    n_halo = s_len // HALO
    row = pl.BlockSpec((TM, D_MODEL), lambda i: (i, 0))
    dm_after = pl.BlockSpec((HALO, 256), lambda i: (jnp.minimum((i + 1) * (TM // HALO), n_halo - 1), 0))
    out_piece = pl.BlockSpec((TM, PIECE), lambda i: (i, 0))
    return pl.pallas_call(
        body,
        name="mix_bwd",
        grid=(n_tiles,),
        in_specs=[piece(0), before(0), after(1), piece(4), piece(5), pl.BlockSpec((FOX_HEADS, TM, 128), lambda i: (0, i, 0)),
                  row, dm_after, _full((N_MEM, 256)), _full((N_MEM, 256)), _full((256, 256)), _full((1, 256)), _full((1, 256)),
                  _full((PIECE, PIECE))],
        out_specs=[out_piece, out_piece, out_piece, pl.BlockSpec((FOX_HEADS, TM, HEAD_DIM), lambda i: (0, i, 0)),
                   pl.BlockSpec((FOX_HEADS, 1, 8, TM), lambda i: (0, i, 0, 0)),
                   _full((256, 256)), _full((1, 256)), _full((N_MEM, 256)), _full((N_MEM, 256)), _full((1, HEAD_DIM))],
        out_shape=[jax.ShapeDtypeStruct((s_len, PIECE), BF16)] * 3 + [
            jax.ShapeDtypeStruct((FOX_HEADS, s_len, HEAD_DIM), BF16),
            jax.ShapeDtypeStruct((FOX_HEADS, n_tiles, 8, TM), F32),
            jax.ShapeDtypeStruct((256, 256), F32), jax.ShapeDtypeStruct((1, 256), F32),
            jax.ShapeDtypeStruct((N_MEM, 256), F32), jax.ShapeDtypeStruct((N_MEM, 256), F32),
            jax.ShapeDtypeStruct((1, HEAD_DIM), F32)],
        compiler_params=_params(("arbitrary",)),
    )(proj, proj, proj, proj, proj, olse, d_mixed, d_mixed, km, vm, wp_bd, pool_scale, gq_m, seg)


def _fox_bwd(q_aug, k_aug, kt_aug, v_h, d_o, lse_rows, delta_rows, dw_kv16, dw_out16):
    s_len = q_aug.shape[1]
    n_tiles = s_len // TA
    n_groups = FOX_HEADS // HB

    def body(q_ref, k_ref, kt_ref, v_ref, do_ref, st_ref, dl_ref, dkv16_ref, dout16_ref, dqt_ref, dk_ref, dv_ref, land_kv, land_out,
             send_sems, recv_sems, local_sems):
        j = pl.program_id(1)
        remote, local = _row_block_exchange(dkv16_ref, dout16_ref, land_kv, land_out, send_sems, recv_sems, local_sems, gather=False)

        @pl.when((pl.program_id(0) == 0) & (j == 0))
        def _():
            for cp in remote + local:
                cp.start()

        @pl.when(j == 0)
        def _():
            dqt_ref[...] = jnp.zeros((HB, n_tiles, 128, TA), F32)

        ks = [k_ref[h] for h in range(HB)]
        kts = [kt_ref[h, 0] for h in range(HB)]
        vs = [v_ref[h] for h in range(HB)]

        def pair(i0, acc, masked, width):
            rows = pl.ds(pl.multiple_of(i0 * TA, TA), width * TA)
            qs = [q_ref[h, rows, :] for h in range(HB)]
            d_os = [do_ref[h, rows, :] for h in range(HB)]
            row_stat = lambda ref, h: jnp.concatenate([ref[h, i0 + t, 0:1, :] for t in range(width)], axis=1)
            scores = [(_dot_nt(ks[h], qs[h]), _dot_nt(vs[h], d_os[h])) for h in range(HB)]
            probs = []
            for h in range(HB):
                s, dp = scores[h]
                if masked:
                    s = jnp.where(_causal_mask_t(), s, -1e30)
                p = jnp.exp(s - row_stat(st_ref, h))
                probs.append((p.astype(BF16), (p * (dp - row_stat(dl_ref, h))).astype(BF16)))
            out = []
            for h in range(HB):
                p16, ds16 = probs[h]
                dqt = _dot(kts[h], ds16)
                for t in range(width):
                    dqt_ref[h, i0 + t] += dqt[:, t * TA:(t + 1) * TA]
                out.append((acc[h][0] + _dot(ds16, qs[h]), acc[h][1] + _dot(p16, d_os[h])))
            return tuple(out)

        zero = tuple((jnp.zeros((TA, 128), F32), jnp.zeros((TA, HEAD_DIM), F32)) for _ in range(HB))
        acc = pair(j, zero, True, 1)
        odd = (n_tiles - 1 - j) % 2
        acc = lax.fori_loop(j + 1, j + 1 + odd, lambda i, a: pair(i, a, False, 1), acc)
        acc = lax.fori_loop(0, (n_tiles - 1 - j) // 2, lambda t, a: pair(j + 1 + odd + 2 * t, a, False, 2), acc)
        for h in range(HB):
            dk_ref[h] = acc[h][0]
            dv_ref[h] = acc[h][1]

        @pl.when((pl.program_id(0) == n_groups - 1) & (j == n_tiles - 1))
        def _():
            for cp in remote:
                cp.wait_recv()
            for cp in remote:
                cp.wait_send()
            for cp in local:
                cp.wait()

    whole = lambda w: pl.BlockSpec((HB, s_len, w), lambda g, j: (g, 0, 0))
    tile = lambda w: pl.BlockSpec((HB, TA, w), lambda g, j: (g, j, 0))
    rows_blk = lambda r: pl.BlockSpec((HB, n_tiles, r, TA), lambda g, j: (g, 0, 0, 0))
    hbm = pl.BlockSpec(memory_space=pl.ANY)
    return pl.pallas_call(
        body,
        name="fox_bwd",
        grid=(n_groups, n_tiles),
        in_specs=[whole(128), tile(128), pl.BlockSpec((HB, 1, 128, TA), lambda g, j: (g, j, 0, 0)), tile(HEAD_DIM),
                  whole(HEAD_DIM), rows_blk(8), rows_blk(8), hbm, hbm],
        out_specs=[rows_blk(128), tile(128), tile(HEAD_DIM), hbm, hbm],
        out_shape=[jax.ShapeDtypeStruct((FOX_HEADS, n_tiles, 128, TA), F32), jax.ShapeDtypeStruct((FOX_HEADS, s_len, 128), F32),
                   jax.ShapeDtypeStruct((FOX_HEADS, s_len, HEAD_DIM), F32),
                   jax.ShapeDtypeStruct((N_DEV, 128, 512), BF16), jax.ShapeDtypeStruct((N_DEV, 128, D_MODEL), BF16)],
        scratch_shapes=[pltpu.SemaphoreType.DMA((14,)), pltpu.SemaphoreType.DMA((14,)), pltpu.SemaphoreType.DMA((2,))],
        compiler_params=_params(("arbitrary", "arbitrary")),
    )(q_aug, k_aug, kt_aug, v_h, d_o, lse_rows, delta_rows, dw_kv16, dw_out16)


def _fox_post(proj, bf_pad, gq, gk, seg, dq_aug, dk_aug, dv_h):
    s_len = proj.shape[0]
    n_tiles = s_len // TM

    def body(q_ref, k_ref, f_ref, bf_ref, gq_ref, gk_ref, seg_ref, dqa_ref, dka_ref, dvh_ref,
             dq_ref, dk_ref, dv_ref, df_ref, dgq_ref, dgk_ref, dbf_ref, carry_ref):
        @pl.when(pl.program_id(0) == 0)
        def _():
            carry_ref[...] = jnp.zeros((1, 128), F32)
            dgq_ref[...] = jnp.zeros((1, HEAD_DIM), F32)
            dgk_ref[...] = jnp.zeros((1, HEAD_DIM), F32)
            dbf_ref[...] = jnp.zeros((1, 128), F32)

        lane = lax.broadcasted_iota(jnp.int32, (TM, 128), 1)
        seg = seg_ref[...]
        dqas = [dqa_ref[h, 0].T for h in range(FOX_HEADS)]
        dkas = [dka_ref[h] for h in range(FOX_HEADS)]
        d_cum = jnp.zeros((TM, 128), F32)
        for h in range(FOX_HEADS):
            d_cum = jnp.where(lane == h, dqas[h][:, 64:65] - dkas[h][:, 67:68], d_cum)
        q_all = q_ref[...]
        k_all = k_ref[...]
        rq = _head_rms(q_all, seg)
        rk = _head_rms(k_all, seg)
        dy_q = jnp.concatenate([t[:, 0:HEAD_DIM] for t in dqas], axis=1) * 0.125
        dy_k = jnp.concatenate([t[:, 0:HEAD_DIM] for t in dkas], axis=1)
        dq, dgq_rows = _head_rms_bwd(q_all * rq, rq, gq_ref[...], dy_q, seg)
        dk, dgk_rows = _head_rms_bwd(k_all * rk, rk, gk_ref[...], dy_k, seg)
        dq_ref[...] = dq.astype(BF16)
        dk_ref[...] = dk.astype(BF16)
        dv_ref[...] = jnp.concatenate([dvh_ref[h] for h in range(FOX_HEADS)], axis=1).astype(BF16)
        dgq_ref[...] += _fold_heads(jnp.sum(dgq_rows, axis=0, keepdims=True), FOX_HEADS)
        dgk_ref[...] += _fold_heads(jnp.sum(dgk_rows, axis=0, keepdims=True), FOX_HEADS)

        row = lax.broadcasted_iota(jnp.int32, (TM, TM), 0)
        col = lax.broadcasted_iota(jnp.int32, (TM, TM), 1)
        tri = jnp.where(col >= row, 1.0, 0.0).astype(BF16)
        hi, mid, lo = _split3(d_cum)
        d_logf = _dot(tri, hi) + _dot(tri, mid) + _dot(tri, lo) + carry_ref[...]
        carry_ref[...] = d_logf[0:1, :]
        z = f_ref[...] + bf_ref[...]
        d_f = jnp.where(lane < FOX_HEADS, d_logf / (1.0 + jnp.exp(z)), 0.0)
        df_ref[...] = d_f.astype(BF16)
        dbf_ref[...] += jnp.sum(d_f, axis=0, keepdims=True)

    rev = lambda i: n_tiles - 1 - i
    col_blk = lambda j: pl.BlockSpec((TM, PIECE), lambda i: (rev(i), j))
    head_blk = lambda w: pl.BlockSpec((FOX_HEADS, TM, w), lambda i: (0, rev(i), 0))
    out_piece = pl.BlockSpec((TM, PIECE), lambda i: (rev(i), 0))
    return pl.pallas_call(
        body,
        name="fox_post",
        grid=(n_tiles,),
        in_specs=[col_blk(1), col_blk(2), pl.BlockSpec((TM, 128), lambda i: (rev(i), MY_F_OFF // 128)),
                  _full((1, 128)), _full((1, PIECE)), _full((1, PIECE)), _full((PIECE, PIECE)),
                  pl.BlockSpec((FOX_HEADS, 1, 128, TM), lambda i: (0, rev(i), 0, 0)), head_blk(128), head_blk(HEAD_DIM)],
        out_specs=[out_piece, out_piece, out_piece, pl.BlockSpec((TM, 128), lambda i: (rev(i), 0)),
                   _full((1, HEAD_DIM)), _full((1, HEAD_DIM)), _full((1, 128))],
        out_shape=[jax.ShapeDtypeStruct((s_len, PIECE), BF16)] * 3 + [
            jax.ShapeDtypeStruct((s_len, 128), BF16), jax.ShapeDtypeStruct((1, HEAD_DIM), F32),
            jax.ShapeDtypeStruct((1, HEAD_DIM), F32), jax.ShapeDtypeStruct((1, 128), F32)],
        scratch_shapes=[pltpu.VMEM((1, 128), F32)],
        compiler_params=_params(("arbitrary",)),
    )(proj, proj, proj, bf_pad, gq, gk, seg, dq_aug, dk_aug, dv_h)


def _mem_bwd(mem, g, w_kv, gk, dkm, dvm):
    def body(mem_ref, g_ref, w_ref, gk_ref, dkm_ref, dvm_ref, dw_ref, dg_ref, dgk_ref, dkv_ref):
        m = mem_ref[...]
        r = _rms(m)
        a = m * r
        mn16 = (a * g_ref[...]).astype(BF16)
        kv = _dot(mn16, w_ref[...])
        dgk = jnp.zeros((N_MEM, HEAD_DIM), F32)
        for h in range(MEM_HEADS):
            sl = slice(HEAD_DIM * h, HEAD_DIM * (h + 1))
            kh = kv[:, sl]
            rk = _rms(kh)
            dk, dg_rows = _rms_bwd(kh * rk, rk, gk_ref[...], dkm_ref[:, sl])
            dkv_ref[:, sl] = dk.astype(BF16)
            dgk = dgk + dg_rows
        dkv_ref[:, 256:512] = dvm_ref[...].astype(BF16)
        dgk_ref[...] = jnp.sum(dgk, axis=0, keepdims=True)
        dkv16 = dkv_ref[...]
        dw_ref[...] = _dot_tn(mn16, dkv16).astype(BF16)
        dg_ref[...] = jnp.sum(_dot_nt(dkv16, w_ref[...]) * a, axis=0, keepdims=True)

    return pl.pallas_call(
        body,
        name="mem_bwd",
        out_shape=(jax.ShapeDtypeStruct((D_MODEL, 512), BF16), jax.ShapeDtypeStruct((1, D_MODEL), F32),
                   jax.ShapeDtypeStruct((1, HEAD_DIM), F32)),
        in_specs=[pl.BlockSpec(memory_space=pltpu.VMEM)] * 6,
        out_specs=(pl.BlockSpec(memory_space=pltpu.VMEM),) * 3,
        scratch_shapes=[pltpu.VMEM((N_MEM, 512), BF16)],
        compiler_params=pltpu.CompilerParams(vmem_limit_bytes=VMEM_LIMIT),
    )(mem, g, w_kv, gk, dkm, dvm)


def _grad_x(pieces, d_f, w_my, x, norm_g, d_out):
    s_len = x.shape[0]

    def body(p0, p1, p2, p3, p4, p5, df_ref, w_ref, x_ref, g_ref, dout_ref, gx_ref, dg_ref):
        @pl.when(pl.program_id(0) == 0)
        def _():
            dg_ref[...] = jnp.zeros((1, D_MODEL), F32)

        dh = _dot_nt(df_ref[...], w_ref[:, MY_F_OFF:MY_WIDTH])
        for j, p in enumerate((p0, p1, p2, p3, p4, p5)):
            dh = dh + _dot_nt(p[...], w_ref[:, PIECE * j:PIECE * (j + 1)])
        xv = x_ref[...]
        r = _rms(xv)
        dx, dg_rows = _rms_bwd(xv * r, r, g_ref[...], dh)
        gx_ref[...] = dout_ref[...] + dx
        dg_ref[...] += jnp.sum(dg_rows, axis=0, keepdims=True)

    piece = pl.BlockSpec((TMM, PIECE), lambda i: (i, 0))
    row = pl.BlockSpec((TMM, D_MODEL), lambda i: (i, 0))
    return pl.pallas_call(
        body,
        name="grad_x",
        grid=(s_len // TMM,),
        in_specs=[piece] * 6 + [pl.BlockSpec((TMM, 128), lambda i: (i, 0)), _full((D_MODEL, MY_WIDTH)), row,
                                _full((1, D_MODEL)), row],
        out_specs=[row, _full((1, D_MODEL))],
        out_shape=[jax.ShapeDtypeStruct((s_len, D_MODEL), F32), jax.ShapeDtypeStruct((1, D_MODEL), F32)],
        compiler_params=_params(("arbitrary",)),
    )(*pieces, d_f, w_my, x, norm_g, d_out)


def _grad_w_in(h16, pieces, d_f):
    s_len = h16.shape[0]
    tk = 512

    def body(h_ref, p0, p1, p2, p3, p4, p5, df_ref, out_ref, dw_ref):
        @pl.when(pl.program_id(0) == 0)
        def _():
            dw_ref[...] = jnp.zeros((D_MODEL, MY_WIDTH), F32)

        h = h_ref[...]
        for j, p in enumerate((p0, p1, p2, p3, p4, p5)):
            dw_ref[:, PIECE * j:PIECE * (j + 1)] += _dot_tn(h, p[...])
        dw_ref[:, MY_F_OFF:MY_WIDTH] += _dot_tn(h, df_ref[...])

        @pl.when(pl.program_id(0) == pl.num_programs(0) - 1)
        def _():
            for d in range(N_DEV):
                parts = [dw_ref[:, start:start + width] for start, width in _chunk_segments(d)]
                out_ref[d] = (parts[0] if len(parts) == 1 else jnp.concatenate(parts, axis=1)).astype(BF16)

    piece = pl.BlockSpec((tk, PIECE), lambda i: (i, 0))
    return pl.pallas_call(
        body,
        name="grad_w_in",
        grid=(s_len // tk,),
        in_specs=[pl.BlockSpec((tk, D_MODEL), lambda i: (i, 0))] + [piece] * 6 + [pl.BlockSpec((tk, 128), lambda i: (i, 0))],
        out_specs=_full((N_DEV, D_MODEL, IN_CHUNK)),
        out_shape=jax.ShapeDtypeStruct((N_DEV, D_MODEL, IN_CHUNK), BF16),
        scratch_shapes=[pltpu.VMEM((D_MODEL, MY_WIDTH), F32)],
        compiler_params=_params(("arbitrary",)),
    )(h16, *pieces, d_f)


def _adamw(w, g, m, v):
    m = ADAM_B1 * m + (1.0 - ADAM_B1) * g
    v = ADAM_B2 * v + (1.0 - ADAM_B2) * (g * g)
    m_hat = m / (1.0 - ADAM_B1 ** ADAM_STEP)
    v_hat = v / (1.0 - ADAM_B2 ** ADAM_STEP)
    return -ADAM_LR * (m_hat / (jnp.sqrt(v_hat) + ADAM_EPS) + ADAM_WD * w), m, v


PARAM_ORDER = ("norm_g", "w_in", "b_f", "w_pool", "pool_scale", "fox_q_g", "fox_k_g", "mem_norm_g", "w_mem_kv", "mem_q_g", "mem_k_g", "w_out")


def _update(red_in, red_kv, red_out, sred, params):
    def body(rin_ref, rkv_ref, rout_ref, sred_ref, *refs):
        ins, outs = refs[:3 * len(PARAM_ORDER)], refs[3 * len(PARAM_ORDER):]
        wide = lambda row: jnp.concatenate([sred_ref[row + k:row + k + 1, :] for k in range(4)], axis=1)
        head = lambda row: sred_ref[row:row + 1, 0:HEAD_DIM]
        grads = {
            "norm_g": lambda: wide(SB_NORM_G), "w_in": lambda: rin_ref[...], "b_f": lambda: sred_ref[SB_B_F:SB_B_F + 1, 0:FOX_HEADS],
            "pool_scale": lambda: sred_ref[SB_POOL_SCALE:SB_POOL_SCALE + 1, :], "fox_q_g": lambda: head(SB_FOX_Q),
            "fox_k_g": lambda: head(SB_FOX_K), "mem_norm_g": lambda: wide(SB_MEM_NORM_G), "w_mem_kv": lambda: rkv_ref[...],
            "mem_q_g": lambda: head(SB_MEM_Q), "mem_k_g": lambda: head(SB_MEM_K), "w_out": lambda: rout_ref[...],
        }
        for p, name in enumerate(PARAM_ORDER):
            w_ref, m_ref, v_ref = ins[3 * p:3 * p + 3]
            g_out, d_out, m_out, v_out = outs[4 * p:4 * p + 4]
            if name == "w_pool":
                for grp in range(4):
                    g = sred_ref[SB_W_POOL:SB_W_POOL + 64, 64 * grp:64 * (grp + 1)]
                    delta, m_new, v_new = _adamw(w_ref[grp], g, m_ref[grp], v_ref[grp])
                    g_out[grp], d_out[grp], m_out[grp], v_out[grp] = g, delta, m_new, v_new
            else:
                g = grads[name]()
                delta, m_new, v_new = _adamw(w_ref[...], g, m_ref[...], v_ref[...])
                g_out[...], d_out[...], m_out[...], v_out[...] = g, delta, m_new, v_new

    flat = [t for name in PARAM_ORDER for t in params[name]]
    shapes = [params[name][0].shape for name in PARAM_ORDER for _ in range(4)]
    outs = pl.pallas_call(
        body,
        name="adamw_update",
        out_shape=tuple(jax.ShapeDtypeStruct(s, F32) for s in shapes),
        in_specs=[pl.BlockSpec(memory_space=pltpu.VMEM)] * (4 + len(flat)),
        out_specs=(pl.BlockSpec(memory_space=pltpu.VMEM),) * len(shapes),
        compiler_params=pltpu.CompilerParams(vmem_limit_bytes=VMEM_LIMIT),
    )(red_in, red_kv, red_out, sred, *flat)
    return {name: outs[4 * p:4 * p + 4] for p, name in enumerate(PARAM_ORDER)}


def kernel(x, mem, norm_g, w_in, b_f, w_pool, pool_scale, fox_q_g, fox_k_g, mem_norm_g, w_mem_kv, mem_q_g, mem_k_g, w_out, loss_target, m_norm_g, m_w_in, m_b_f, m_w_pool, m_pool_scale, m_fox_q_g, m_fox_k_g, m_mem_norm_g, m_w_mem_kv, m_mem_q_g, m_mem_k_g, m_w_out, v_norm_g, v_w_in, v_b_f, v_w_pool, v_pool_scale, v_fox_q_g, v_fox_k_g, v_mem_norm_g, v_w_mem_kv, v_mem_q_g, v_mem_k_g, v_w_out):
    given = dict(norm_g=(norm_g, m_norm_g, v_norm_g), w_in=(w_in, m_w_in, v_w_in), b_f=(b_f, m_b_f, v_b_f),
                 w_pool=(w_pool, m_w_pool, v_w_pool), pool_scale=(pool_scale, m_pool_scale, v_pool_scale),
                 fox_q_g=(fox_q_g, m_fox_q_g, v_fox_q_g), fox_k_g=(fox_k_g, m_fox_k_g, v_fox_k_g),
                 mem_norm_g=(mem_norm_g, m_mem_norm_g, v_mem_norm_g), w_mem_kv=(w_mem_kv, m_w_mem_kv, v_w_mem_kv),
                 mem_q_g=(mem_q_g, m_mem_q_g, v_mem_q_g), mem_k_g=(mem_k_g, m_mem_k_g, v_mem_k_g), w_out=(w_out, m_w_out, v_w_out))
    params = {name: tuple(t[0] if t.ndim > 2 else t for t in wmv) for name, wmv in given.items()}
    x2, mem2, target2 = x[0], mem[0], loss_target[0]

    seg = jnp.asarray(np.kron(np.eye(FOX_HEADS), np.full((HEAD_DIM, HEAD_DIM), 1.0 / HEAD_DIM)), BF16)
    w_my, w_kv16, w_out16, wp_bd, bf_pad, gq8, gk8, gqm4 = _gather_w_in(
        params["w_in"][0], params["w_mem_kv"][0], params["w_out"][0], params["w_pool"][0], b_f, fox_q_g, fox_k_g, mem_q_g)
    proj, h16 = _fwd_proj(x2, norm_g, w_my)
    q_aug, k_aug, v_h, kt_aug, vt_aug = _fox_prep(proj, bf_pad, gq8, gk8, seg)
    olse, lse_rows, w_kv_all, w_out_all = _fox_fwd(q_aug, k_aug, vt_aug, w_kv16, w_out16)
    km, vm = _mem_prep(mem2, mem_norm_g, w_kv_all, mem_k_g)
    mixed = _mix_fwd(proj, olse, km, vm, wp_bd, pool_scale, gqm4, seg)
    d_out, d_mixed, dw_out, loss_blk = _out_loss(mixed, x2, target2, w_out_all)

    d_ua, d_gb, d_qm, d_o, delta_rows, dwp_bd, d_scale, dkm, dvm, d_gqm = _mix_bwd(proj, olse, d_mixed, km, vm, wp_bd, pool_scale,
                                                                                    gqm4, seg)
    dw_kv, d_mem_g, d_gkm = _mem_bwd(mem2, mem_norm_g, w_kv_all, mem_k_g, dkm, dvm)
    dq_aug, dk_aug, dv_h, land_kv, land_out = _fox_bwd(q_aug, k_aug, kt_aug, v_h, d_o, lse_rows, delta_rows, dw_kv, dw_out)
    d_q, d_k, d_v, d_f, d_gq, d_gk, d_bf = _fox_post(proj, bf_pad, gq8, gk8, seg, dq_aug, dk_aug, dv_h)
    pieces = (d_ua, d_q, d_k, d_v, d_gb, d_qm)
    dw_in = _grad_w_in(h16, pieces, d_f)
    grad_x, d_norm_g = _grad_x(pieces, d_f, w_my, x2, norm_g, d_out)

    red_in, red_kv, red_out, sred = _exchange_grads(dw_in, land_kv, land_out, d_norm_g, d_mem_g, d_scale, d_bf, d_gq, d_gk, d_gqm,
                                                    d_gkm, dwp_bd, loss_blk)
    new = _update(red_in, red_kv, red_out, sred, params)
    result = [sred[SB_LOSS, 0], grad_x[None]]
    for kind in range(4):
        for name in PARAM_ORDER:
            out = new[name][kind]
            result.append(out[None] if given[name][0].ndim > 2 else out)
    return tuple(result)
```

```python
import functools

import jax
import jax.numpy as jnp
import numpy as np
from jax import lax
from jax.experimental import pallas as pl
from jax.experimental.pallas import tpu as pltpu

F32 = jnp.float32
BF16 = jnp.bfloat16
MESH = pl.DeviceIdType.MESH

N_DEV = 8
D_MODEL = 1024
HEAD_DIM = 64
FOX_HEADS = 8
MEM_HEADS = 4
N_MEM = 256
POOL_WIDTH = 256
EPS = 1e-6
IN_WIDTH = 3080
IN_CHUNK = IN_WIDTH // N_DEV
F_OFF = 2048
MY_WIDTH = 3200
MY_F_OFF = 3072
PIECE = 512
TM = 256
TMM = 512
TA = 256
HB = 4
HB_FWD = 8
AUG_ROWS = 72
HALO = 16
VMEM_LIMIT = 56 * 1024 * 1024

ADAM_LR = 0.001
ADAM_B1 = 0.9
ADAM_B2 = 0.999
ADAM_EPS = 1e-08
ADAM_WD = 0.01
ADAM_STEP = 10


def _dot(a, b):
    return jnp.dot(a, b, preferred_element_type=F32)


def _dot_nt(a, b):
    return lax.dot_general(a, b, (((1,), (1,)), ((), ())), preferred_element_type=F32)


def _dot_tn(a, b):
    return lax.dot_general(a, b, (((0,), (0,)), ((), ())), preferred_element_type=F32)


def _sigmoid(g):
    return 0.5 + 0.5 * jnp.tanh(0.5 * g)


def _split3(v):
    hi = v.astype(BF16)
    r1 = v - hi.astype(F32)
    mid = r1.astype(BF16)
    lo = (r1 - mid.astype(F32)).astype(BF16)
    return hi, mid, lo


def _rms(v):
    return lax.rsqrt(jnp.mean(v * v, axis=-1, keepdims=True) + EPS)


def _rms_bwd(a, r, g, dy):
    da = dy * g
    return r * (da - a * jnp.mean(da * a, axis=-1, keepdims=True)), dy * a


def _head_mean(z, seg):
    hi = z.astype(BF16)
    lo = (z - hi.astype(F32)).astype(BF16)
    return _dot(hi, seg) + _dot(lo, seg)


def _head_rms(v, seg):
    return lax.rsqrt(_head_mean(v * v, seg) + EPS)


def _head_rms_bwd(a, r, g, dy, seg):
    da = dy * g
    return r * (da - a * _head_mean(da * a, seg)), dy * a


def _fold_heads(row, n_heads):
    out = row[:, 0:HEAD_DIM]
    for h in range(1, n_heads):
        out = out + row[:, HEAD_DIM * h:HEAD_DIM * (h + 1)]
    return out


def _params(sem, limit=VMEM_LIMIT):
    return pltpu.CompilerParams(dimension_semantics=sem, vmem_limit_bytes=limit)


def _full(shape):
    return pl.BlockSpec(shape, lambda *_: (0,) * len(shape))


def _chunk_segments(d):
    runs = []
    for lo, hi, shift in ((0, F_OFF, 0), (F_OFF, F_OFF + FOX_HEADS, MY_F_OFF - F_OFF), (F_OFF + FOX_HEADS, IN_WIDTH, -FOX_HEADS)):
        a, b = max(lo, IN_CHUNK * d), min(hi, IN_CHUNK * (d + 1))
        if a < b:
            runs.append((a + shift, b - a))
    return runs


def _my_place():
    x, y, c = lax.axis_index("x"), lax.axis_index("y"), lax.axis_index("c")
    return x, y, c, 4 * x + 2 * y + c


def _shard_rows(dev):
    return pl.ds(pl.multiple_of(128 * dev, 128), 128)


def _gather_w_in(w_in, w_kv, w_out, w_pool, b_f, gq, gk, gqm):
    def body(win_ref, wkv_ref, wout_ref, wp_ref, bf_ref, gq_ref, gk_ref, gqm_ref, wmy_ref, kv16_ref, out16_ref, wpbd_ref, bfpad_ref,
             gq8_ref, gk8_ref, gqm4_ref, in_all, pay_in, send_sems, recv_sems):
        x, y, c, me = _my_place()
        sibling = (x, y, 1 - c)
        chips = [(1 - x, y), (x, 1 - y), (1 - x, 1 - y)]

        pay_in[...] = win_ref[...].astype(BF16)

        def copy(k, block, to, own=False):
            px, py, pc = block
            dst = in_all.at[4 * px + 2 * py + pc]
            return pltpu.make_async_remote_copy(src_ref=pay_in if own else dst, dst_ref=dst, send_sem=send_sems.at[k],
                                                recv_sem=recv_sems.at[k], device_id=to, device_id_type=MESH)

        first = [copy(0, (x, y, c), sibling, own=True)]
        first += [copy(1 + j, (x, y, c), (*chip, c), own=True) for j, chip in enumerate(chips)]
        for cp in first:
            cp.start()
        in_all[me] = pay_in[...]
        kv16_ref[...] = wkv_ref[...].astype(BF16)
        out16_ref[...] = wout_ref[...].astype(BF16)
        wpbd_ref[...] = jnp.zeros((POOL_WIDTH, POOL_WIDTH), BF16)
        for grp in range(4):
            sl = slice(64 * grp, 64 * (grp + 1))
            wpbd_ref[sl, sl] = wp_ref[grp].astype(BF16)
        bfpad_ref[...] = jnp.zeros((1, 128), F32)
        bfpad_ref[:, 0:FOX_HEADS] = bf_ref[...]
        gq8_ref[...] = jnp.concatenate([gq_ref[...]] * FOX_HEADS, axis=1)
        gk8_ref[...] = jnp.concatenate([gk_ref[...]] * FOX_HEADS, axis=1)
        gqm4_ref[...] = jnp.concatenate([gqm_ref[...]] * MEM_HEADS, axis=1)
        passed = []
        for j, chip in enumerate(chips):
            copy(1 + j, (*chip, c), (x, y, c)).wait_recv()
            passed.append(copy(4 + j, (*chip, c), sibling))
            passed[-1].start()
        copy(0, sibling, (x, y, c)).wait_recv()
        for j, chip in enumerate(chips):
            copy(4 + j, (*chip, 1 - c), (x, y, c)).wait_recv()
        for cp in first + passed:
            cp.wait_send()

        for r0 in range(0, D_MODEL, 256):
            ch = [in_all[d, r0:r0 + 256, :].astype(F32) for d in range(N_DEV)]
            lo = F_OFF - 5 * IN_CHUNK
            full = jnp.concatenate(ch[:5] + [ch[5][:, :lo], ch[5][:, lo + FOX_HEADS:], ch[6], ch[7], ch[5][:, lo:lo + FOX_HEADS],
                                             jnp.zeros((256, MY_WIDTH - IN_WIDTH), F32)], axis=1)
            wmy_ref[r0:r0 + 256, :] = full.astype(BF16)

    return pl.pallas_call(
        body,
        name="gather_w_in",
        out_shape=(jax.ShapeDtypeStruct((D_MODEL, MY_WIDTH), BF16), jax.ShapeDtypeStruct((128, 512), BF16),
                   jax.ShapeDtypeStruct((128, D_MODEL), BF16), jax.ShapeDtypeStruct((POOL_WIDTH, POOL_WIDTH), BF16),
                   jax.ShapeDtypeStruct((1, 128), F32), jax.ShapeDtypeStruct((1, PIECE), F32), jax.ShapeDtypeStruct((1, PIECE), F32),
                   jax.ShapeDtypeStruct((1, 256), F32)),
        in_specs=[pl.BlockSpec(memory_space=pltpu.VMEM)] * 8,
        out_specs=(pl.BlockSpec(memory_space=pltpu.VMEM),) * 8,
        scratch_shapes=[
            pltpu.VMEM((N_DEV, D_MODEL, IN_CHUNK), BF16),
            pltpu.VMEM((D_MODEL, IN_CHUNK), BF16),
            pltpu.SemaphoreType.DMA((7,)),
            pltpu.SemaphoreType.DMA((7,)),
        ],
        compiler_params=pltpu.CompilerParams(vmem_limit_bytes=VMEM_LIMIT),
    )(w_in, w_kv, w_out, w_pool, b_f, gq, gk, gqm)


def _row_block_exchange(kv_ref, out_ref, kv_dst, out_dst, send_sems, recv_sems, local_sems, gather):
    x, y, c, me = _my_place()
    remote = []
    for k in range(1, N_DEV):
        px, py, pc = x ^ (k >> 2), y ^ ((k >> 1) & 1), c ^ (k & 1)
        peer = 4 * px + 2 * py + pc
        for fam, (src, dst) in enumerate(((kv_ref, kv_dst), (out_ref, out_dst))):
            remote.append(pltpu.make_async_remote_copy(
                src_ref=src if gather else src.at[_shard_rows(peer)], dst_ref=dst.at[_shard_rows(me)] if gather else dst.at[me],
                send_sem=send_sems.at[7 * fam + k - 1], recv_sem=recv_sems.at[7 * fam + k - 1],
                device_id=(px, py, pc), device_id_type=MESH))
    local = [pltpu.make_async_copy(src if gather else src.at[_shard_rows(me)], dst.at[_shard_rows(me)] if gather else dst.at[me],
                                   local_sems.at[fam]) for fam, (src, dst) in enumerate(((kv_ref, kv_dst), (out_ref, out_dst)))]
    return remote, local


SB_ROWS, SB_W = 80, 256
SB_NORM_G, SB_MEM_NORM_G, SB_POOL_SCALE, SB_B_F, SB_FOX_Q, SB_FOX_K, SB_MEM_Q, SB_MEM_K, SB_LOSS, SB_W_POOL = 0, 4, 8, 9, 10, 11, 12, 13, 14, 16


def _exchange_grads(dw_in, land_kv, land_out, d_norm_g, d_mem_g, d_scale, d_bf, d_gq, d_gk, d_gqm, d_gkm, dwp_bd, loss_blk):
    half = D_MODEL // 2

    def body(in_ref, lkv_ref, lout_ref, dng, dmg, dsc, dbf, dgq, dgk, dgqm, dgkm, dwp, loss_ref,
             rin_ref, rkv_ref, rout_ref, sred_ref, land1, send2a, send2b, keep2a, keep2b, land2a, land2b,
             send3a, send3b, land3a, land3b, sb_ref, sland, send_sems, recv_sems):
        x, y, c, me = _my_place()
        sibling, x_nbr, y_nbr = (x, y, 1 - c), (1 - x, y, c), (x, 1 - y, c)

        def rcopy(src, dst, sem, to):
            return pltpu.make_async_remote_copy(src_ref=src, dst_ref=dst, send_sem=send_sems.at[sem], recv_sem=recv_sems.at[sem],
                                                device_id=to, device_id_type=MESH)

        sb_ref[...] = jnp.zeros((SB_ROWS, SB_W), F32)
        for k in range(4):
            sb_ref[SB_NORM_G + k:SB_NORM_G + k + 1, :] = dng[:, SB_W * k:SB_W * (k + 1)]
            sb_ref[SB_MEM_NORM_G + k:SB_MEM_NORM_G + k + 1, :] = dmg[:, SB_W * k:SB_W * (k + 1)]
        sb_ref[SB_POOL_SCALE:SB_POOL_SCALE + 1, :] = dsc[...]
        sb_ref[SB_B_F:SB_B_F + 1, 0:128] = dbf[...]
        for row, ref in ((SB_FOX_Q, dgq), (SB_FOX_K, dgk), (SB_MEM_Q, dgqm), (SB_MEM_K, dgkm)):
            sb_ref[row:row + 1, 0:HEAD_DIM] = ref[...]
        sb_ref[SB_LOSS:SB_LOSS + 1, 0:128] = loss_ref[0:1, :]
        for g in range(4):
            sl = slice(64 * g, 64 * (g + 1))
            sb_ref[SB_W_POOL:SB_W_POOL + 64, sl] = dwp[sl, sl]

        small = []
        for k in range(1, N_DEV):
            px, py, pc = x ^ (k >> 2), y ^ ((k >> 1) & 1), c ^ (k & 1)
            small.append(rcopy(sb_ref, sland.at[me], k - 1, (px, py, pc)))
        stage1 = [rcopy(in_ref.at[2 * k + 1 - c], land1.at[k], 7 + k, sibling) for k in range(4)]
        for cp in small + stage1:
            cp.start()
        sland[me] = sb_ref[...]
        for cp in stage1:
            cp.wait_recv()

        top, bot = slice(0, half), slice(half, D_MODEL)

        def chip_sum(k, rows):
            return in_ref[2 * k + c, rows, :].astype(F32) + land1[k, rows, :].astype(F32)

        for j in range(2):
            send2a[j] = chip_sum(2 * (1 - x) + j, top).astype(BF16)
            keep2a[j] = chip_sum(2 * x + j, top)
            send2b[j] = chip_sum(2 * j + 1 - y, bot).astype(BF16)
            keep2b[j] = chip_sum(2 * j + y, bot)
        stage2 = [rcopy(send2a.at[j], land2a.at[j], 11 + j, x_nbr) for j in range(2)]
        stage2 += [rcopy(send2b.at[j], land2b.at[j], 13 + j, y_nbr) for j in range(2)]
        for cp in stage2:
            cp.start()
        for cp in stage2:
            cp.wait_recv()
        for j in range(2):
            keep2a[j] = keep2a[j] + land2a[j].astype(F32)
            keep2b[j] = keep2b[j] + land2b[j].astype(F32)
        send3a[...] = keep2a[1 - y].astype(BF16)
        send3b[...] = keep2b[1 - x].astype(BF16)
        stage3 = [rcopy(send3a, land3a, 15, y_nbr), rcopy(send3b, land3b, 16, x_nbr)]
        for cp in stage3:
            cp.start()
        for cp in stage3:
            cp.wait_recv()
        rin_ref[top, :] = keep2a[y] + land3a[...].astype(F32)
        rin_ref[bot, :] = keep2b[x] + land3b[...].astype(F32)

        for cp in small:
            cp.wait_recv()
        for cp in small + stage1 + stage2 + stage3:
            cp.wait_send()
        for land, red in ((lkv_ref, rkv_ref), (lout_ref, rout_ref), (sland, sred_ref)):
            acc = land[0].astype(F32)
            for d in range(1, N_DEV):
                acc = acc + land[d].astype(F32)
            red[...] = acc

    args = (dw_in, land_kv, land_out, d_norm_g, d_mem_g, d_scale, d_bf, d_gq, d_gk, d_gqm, d_gkm, dwp_bd, loss_blk)
    half_chunk = lambda n, dt: pltpu.VMEM((n, half, IN_CHUNK), dt)
    return pl.pallas_call(
        body,
        name="exchange_grads",
        out_shape=(jax.ShapeDtypeStruct((D_MODEL, IN_CHUNK), F32), jax.ShapeDtypeStruct((128, 512), F32),
                   jax.ShapeDtypeStruct((128, D_MODEL), F32), jax.ShapeDtypeStruct((SB_ROWS, SB_W), F32)),
        in_specs=[pl.BlockSpec(memory_space=pltpu.VMEM)] * len(args),
        out_specs=(pl.BlockSpec(memory_space=pltpu.VMEM),) * 4,
        scratch_shapes=[
            pltpu.VMEM((4, D_MODEL, IN_CHUNK), BF16),
            half_chunk(2, BF16), half_chunk(2, BF16), half_chunk(2, F32), half_chunk(2, F32), half_chunk(2, BF16), half_chunk(2, BF16),
            pltpu.VMEM((half, IN_CHUNK), BF16), pltpu.VMEM((half, IN_CHUNK), BF16),
            pltpu.VMEM((half, IN_CHUNK), BF16), pltpu.VMEM((half, IN_CHUNK), BF16),
            pltpu.VMEM((SB_ROWS, SB_W), F32),
            pltpu.VMEM((N_DEV, SB_ROWS, SB_W), F32),
            pltpu.SemaphoreType.DMA((17,)),
            pltpu.SemaphoreType.DMA((17,)),
        ],
        compiler_params=pltpu.CompilerParams(vmem_limit_bytes=VMEM_LIMIT),
    )(*args)


def _fwd_proj(x, norm_g, w_my):
    s_len = x.shape[0]

    def body(x_ref, g_ref, w_ref, proj_ref, h_ref):
        xv = x_ref[...]
        h = (xv * _rms(xv) * g_ref[...]).astype(BF16)
        h_ref[...] = h
        proj_ref[...] = _dot(h, w_ref[...])

    return pl.pallas_call(
        body,
        name="fwd_proj",
        grid=(s_len // TMM,),
        in_specs=[pl.BlockSpec((TMM, D_MODEL), lambda i: (i, 0)), _full((1, D_MODEL)), _full((D_MODEL, MY_WIDTH))],
        out_specs=[pl.BlockSpec((TMM, MY_WIDTH), lambda i: (i, 0)), pl.BlockSpec((TMM, D_MODEL), lambda i: (i, 0))],
        out_shape=[jax.ShapeDtypeStruct((s_len, MY_WIDTH), F32), jax.ShapeDtypeStruct((s_len, D_MODEL), BF16)],
        compiler_params=_params(("parallel",)),
    )(x, norm_g, w_my)


def _log_sigmoid(z):
    return jnp.minimum(z, 0.0) - jnp.log(1.0 + jnp.exp(-jnp.abs(z)))


def _fox_prep(proj, bf_pad, gq, gk, seg):
    s_len = proj.shape[0]

    def body(q_ref, k_ref, v_ref, f_ref, bf_ref, gq_ref, gk_ref, seg_ref, qa_ref, ka_ref, vh_ref, kt_ref, vt_ref, carry_ref):
        @pl.when(pl.program_id(0) == 0)
        def _():
            carry_ref[...] = jnp.zeros((1, 128), F32)

        lane = lax.broadcasted_iota(jnp.int32, (TM, 128), 1)
        logf = jnp.where(lane < FOX_HEADS, _log_sigmoid(f_ref[...] + bf_ref[...]), 0.0)
        row = lax.broadcasted_iota(jnp.int32, (TM, TM), 0)
        col = lax.broadcasted_iota(jnp.int32, (TM, TM), 1)
        tri = jnp.where(col <= row, 1.0, 0.0).astype(BF16)
        hi, mid, lo = _split3(logf)
        cum = _dot(tri, hi) + _dot(tri, mid) + _dot(tri, lo) + carry_ref[...]
        carry_ref[...] = cum[TM - 1:TM, :]
        f_hi, f_mid, f_lo = [t.astype(F32) for t in _split3(cum)]
        zeros = jnp.zeros((TM, HEAD_DIM), F32)
        q_all = q_ref[...]
        k_all = k_ref[...]
        qn_all = q_all * _head_rms(q_all, seg_ref[...]) * gq_ref[...] * 0.125
        kn_all = k_all * _head_rms(k_all, seg_ref[...]) * gk_ref[...]
        for h in range(FOX_HEADS):
            sl = slice(HEAD_DIM * h, HEAD_DIM * (h + 1))
            qn = qn_all[:, sl]
            kn = kn_all[:, sl]
            a, b, c = f_hi[:, h:h + 1], f_mid[:, h:h + 1], f_lo[:, h:h + 1]
            qa = jnp.where(lane < 64, jnp.concatenate([qn, zeros], axis=1),
                           jnp.where(lane == 64, a, jnp.where(lane == 65, b, jnp.where(lane == 66, c,
                                                                                         jnp.where(lane < 70, 1.0, 0.0)))))
            ka = jnp.where(lane < 64, jnp.concatenate([kn, zeros], axis=1),
                           jnp.where(lane < 67, 1.0, jnp.where(lane == 67, -a, jnp.where(lane == 68, -b,
                                                                                          jnp.where(lane == 69, -c, 0.0)))))
            vh = v_ref[:, sl]
            v_aug = jnp.where(lane < 64, jnp.concatenate([vh, zeros], axis=1), jnp.where(lane == 64, 1.0, 0.0))
            qa_ref[h] = qa.astype(BF16)
            ka_ref[h] = ka.astype(BF16)
            vh_ref[h] = vh.astype(BF16)
            kt_ref[h, 0] = ka.T.astype(BF16)
            vt_ref[h, 0] = v_aug.T.astype(BF16)

    col_blk = lambda j: pl.BlockSpec((TM, PIECE), lambda i: (i, j))
    head_blk = lambda w: pl.BlockSpec((FOX_HEADS, TM, w), lambda i: (0, i, 0))
    tile_blk = pl.BlockSpec((FOX_HEADS, 1, 128, TM), lambda i: (0, i, 0, 0))
    tiled = jax.ShapeDtypeStruct((FOX_HEADS, s_len // TM, 128, TM), BF16)
    return pl.pallas_call(
        body,
        name="fox_prep",
        grid=(s_len // TM,),
        in_specs=[col_blk(1), col_blk(2), col_blk(3), pl.BlockSpec((TM, 128), lambda i: (i, MY_F_OFF // 128)),
                  _full((1, 128)), _full((1, PIECE)), _full((1, PIECE)), _full((PIECE, PIECE))],
        out_specs=[head_blk(128), head_blk(128), head_blk(HEAD_DIM), tile_blk, tile_blk],
        out_shape=[jax.ShapeDtypeStruct((FOX_HEADS, s_len, 128), BF16), jax.ShapeDtypeStruct((FOX_HEADS, s_len, 128), BF16),
                   jax.ShapeDtypeStruct((FOX_HEADS, s_len, HEAD_DIM), BF16), tiled, tiled],
        scratch_shapes=[pltpu.VMEM((1, 128), F32)],
        compiler_params=_params(("arbitrary",)),
    )(proj, proj, proj, proj, bf_pad, gq, gk, seg)


def _mem_prep(mem, g, w_kv, gk):
    def body(mem_ref, g_ref, w_ref, gk_ref, km_ref, vm_ref):
        m = mem_ref[...]
        kv = _dot((m * _rms(m) * g_ref[...]).astype(BF16), w_ref[...])
        for h in range(MEM_HEADS):
            sl = slice(HEAD_DIM * h, HEAD_DIM * (h + 1))
            kh = kv[:, sl]
            km_ref[:, sl] = (kh * _rms(kh) * gk_ref[...]).astype(BF16)
        vm_ref[...] = kv[:, 256:512].astype(BF16)

    return pl.pallas_call(
        body,
        name="mem_prep",
        out_shape=(jax.ShapeDtypeStruct((N_MEM, 256), BF16),) * 2,
        in_specs=[pl.BlockSpec(memory_space=pltpu.VMEM)] * 4,
        out_specs=(pl.BlockSpec(memory_space=pltpu.VMEM),) * 2,
        compiler_params=pltpu.CompilerParams(vmem_limit_bytes=VMEM_LIMIT),
    )(mem, g, w_kv, gk)


def _causal_mask():
    row = lax.broadcasted_iota(jnp.int32, (TA, TA), 0)
    col = lax.broadcasted_iota(jnp.int32, (TA, TA), 1)
    return col <= row


def _causal_mask_t():
    row = lax.broadcasted_iota(jnp.int32, (TA, TA), 0)
    col = lax.broadcasted_iota(jnp.int32, (TA, TA), 1)
    return row <= col


def _fox_fwd(q_aug, k_aug, vt_aug, w_kv16, w_out16):
    s_len = q_aug.shape[1]
    n_tiles = s_len // TA
    hb = HB_FWD
    n_groups = FOX_HEADS // hb

    def body(q_ref, k_ref, vt_ref, kv16_ref, out16_ref, o_ref, st_ref, kv_all, out_all, send_sems, recv_sems, local_sems):
        qi = pl.program_id(1)
        remote, local = _row_block_exchange(kv16_ref, out16_ref, kv_all, out_all, send_sems, recv_sems, local_sems, gather=True)

        @pl.when((pl.program_id(0) == 0) & (qi == 0))
        def _():
            for cp in remote + local:
                cp.start()

        qs = [q_ref[h] for h in range(hb)]

        def step(t0, carry, masked, width):
            rows = pl.ds(pl.multiple_of(t0 * TA, TA), width * TA)
            scores = [_dot_nt(k_ref[h, rows, :], qs[h]) for h in range(hb)]
            soft = []
            for h in range(hb):
                m, _ = carry[h]
                s = jnp.where(_causal_mask_t(), scores[h], -1e30) if masked else scores[h]
                m_new = jnp.maximum(m, jnp.max(s, axis=0, keepdims=True))
                soft.append((m_new, jnp.exp(m - m_new), jnp.exp(s - m_new).astype(BF16)))
            out = []
            for h, (m_new, alpha, p) in enumerate(soft):
                acc = alpha * carry[h][1]
                for t in range(width):
                    acc = acc + _dot(vt_ref[h, t0 + t, 0:AUG_ROWS, :], p[t * TA:(t + 1) * TA])
                out.append((m_new, acc))
            return tuple(out)

        init = tuple((jnp.full((1, TA), -1e30, F32), jnp.zeros((AUG_ROWS, TA), F32)) for _ in range(hb))
        carry = lax.fori_loop(0, qi // 2, lambda j, cr: step(2 * j, cr, False, 2), init)
        carry = lax.fori_loop(2 * (qi // 2), qi, lambda j, cr: step(j, cr, False, 1), carry)
        carry = step(qi, carry, True, 1)
        for h in range(hb):
            m, acc = carry[h]
            l = acc[HEAD_DIM:HEAD_DIM + 1, :]
            lse = m + jnp.log(l)
            o_ref[h] = jnp.concatenate([acc[0:HEAD_DIM] / l, jnp.broadcast_to(lse, (HEAD_DIM, TA))], axis=0).T
            st_ref[h, 0] = jnp.broadcast_to(lse, (8, TA))

        @pl.when((pl.program_id(0) == n_groups - 1) & (qi == n_tiles - 1))
        def _():
            for cp in remote:
                cp.wait_recv()
            for cp in remote:
                cp.wait_send()
            for cp in local:
                cp.wait()

    hbm = pl.BlockSpec(memory_space=pl.ANY)
    return pl.pallas_call(
        body,
        name="fox_fwd",
        grid=(n_groups, n_tiles),
        in_specs=[pl.BlockSpec((hb, TA, 128), lambda g, i: (g, i, 0)), pl.BlockSpec((hb, s_len, 128), lambda g, i: (g, 0, 0)),
                  pl.BlockSpec((hb, n_tiles, 128, TA), lambda g, i: (g, 0, 0, 0)), hbm, hbm],
        out_specs=[pl.BlockSpec((hb, TA, 128), lambda g, i: (g, i, 0)), pl.BlockSpec((hb, 1, 8, TA), lambda g, i: (g, i, 0, 0)),
                   hbm, hbm],
        out_shape=[jax.ShapeDtypeStruct((FOX_HEADS, s_len, 128), F32), jax.ShapeDtypeStruct((FOX_HEADS, n_tiles, 8, TA), F32),
                   jax.ShapeDtypeStruct((D_MODEL, 512), BF16), jax.ShapeDtypeStruct((D_MODEL, D_MODEL), BF16)],
        scratch_shapes=[pltpu.SemaphoreType.DMA((14,)), pltpu.SemaphoreType.DMA((14,)), pltpu.SemaphoreType.DMA((2,))],
        compiler_params=_params(("arbitrary", "arbitrary")),
    )(q_aug, k_aug, vt_aug, w_kv16, w_out16)


def _lane_group(lane, a, b, c, d):
    return jnp.where(lane < 64, a, jnp.where(lane < 128, b, jnp.where(lane < 192, c, d)))


def _pool_count(row0):
    lane = lax.broadcasted_iota(jnp.int32, (TM, POOL_WIDTH), 1)
    t1 = row0 + lax.broadcasted_iota(jnp.int32, (TM, POOL_WIDTH), 0) + 1
    return 1.0 / jnp.minimum(t1, _lane_group(lane, 2, 4, 8, 16)).astype(F32), lane


def _pool_delta(u, halo, cnt, lane):
    xe = jnp.concatenate([halo, u], axis=0)
    s2 = xe + pltpu.roll(xe, 1, 0)
    s4 = s2 + pltpu.roll(s2, 2, 0)
    s8 = s4 + pltpu.roll(s4, 4, 0)
    s16 = s8 + pltpu.roll(s8, 8, 0)
    win = _lane_group(lane, s2[HALO:], s4[HALO:], s8[HALO:], s16[HALO:])
    return win * cnt - u


def _pool_delta_bwd(dd, dd_next, cnt, cnt_next, lane):
    ee = jnp.concatenate([dd * cnt, dd_next * cnt_next], axis=0)
    n = TM + HALO
    r2 = ee + pltpu.roll(ee, n - 1, 0)
    r4 = r2 + pltpu.roll(r2, n - 2, 0)
    r8 = r4 + pltpu.roll(r4, n - 4, 0)
    r16 = r8 + pltpu.roll(r8, n - 8, 0)
    return _lane_group(lane, r2[:TM], r4[:TM], r8[:TM], r16[:TM]) - dd


def _mem_attn(qm, gq, seg, km_ref, vm_ref):
    r = _head_rms(qm, seg)
    a = qm * r
    q16 = (a * gq * 0.125).astype(BF16)
    heads = []
    for h in range(MEM_HEADS):
        sl = slice(HEAD_DIM * h, HEAD_DIM * (h + 1))
        s = _dot_nt(q16[:, sl], km_ref[:, sl])
        e = jnp.exp(s - jnp.max(s, axis=-1, keepdims=True))
        p = e * (1.0 / jnp.sum(e, axis=-1, keepdims=True))
        heads.append((p, _dot(p.astype(BF16), vm_ref[:, sl])))
    return a, r, q16, heads


def _row_specs(s_len):
    n_halo = s_len // HALO
    piece = lambda j: pl.BlockSpec((TM, PIECE), lambda i: (i, j))
    before = lambda j: pl.BlockSpec((HALO, 256), lambda i: (jnp.maximum(i * (TM // HALO) - 1, 0), j))
    after = lambda j: pl.BlockSpec((HALO, 256), lambda i: (jnp.minimum((i + 1) * (TM // HALO), n_halo - 1), j))
    return piece, before, after


def _mix_fwd(proj, olse, km, vm, wp_bd, pool_scale, gq_m, seg):
    s_len = proj.shape[0]

    def body(ua_ref, halo_ref, gb_ref, qm_ref, ol_ref, km_ref, vm_ref, wp_ref, sc_ref, gq_ref, seg_ref, out_ref):
        i = pl.program_id(0)
        u = ua_ref[:, 0:256]
        g_a = ua_ref[:, 256:512]
        halo = jnp.where(i > 0, halo_ref[...], 0.0)
        cnt, lane = _pool_count(i * TM)
        d = _pool_delta(u, halo, cnt, lane).astype(BF16)
        y_a = _dot(d, wp_ref[...]) * sc_ref[...]
        out_ref[:, 0:256] = (y_a * (g_a * _sigmoid(g_a))).astype(BF16)
        g_b = gb_ref[...]
        y_b = jnp.concatenate([ol_ref[h][:, 0:HEAD_DIM] for h in range(FOX_HEADS)], axis=1)
        out_ref[:, 256:768] = (y_b * (g_b * _sigmoid(g_b))).astype(BF16)
        _, _, _, heads = _mem_attn(qm_ref[:, 0:256], gq_ref[...], seg_ref[0:256, 0:256], km_ref, vm_ref)
        g_m = qm_ref[:, 256:512]
        y_m = jnp.concatenate([y for _, y in heads], axis=1)
        out_ref[:, 768:1024] = (y_m * (g_m * _sigmoid(g_m))).astype(BF16)

    piece, before, _ = _row_specs(s_len)
    return pl.pallas_call(
        body,
        name="mix_fwd",
        grid=(s_len // TM,),
        in_specs=[piece(0), before(0), piece(4), piece(5), pl.BlockSpec((FOX_HEADS, TM, 128), lambda i: (0, i, 0)),
                  _full((N_MEM, 256)), _full((N_MEM, 256)), _full((256, 256)), _full((1, 256)), _full((1, 256)),
                  _full((PIECE, PIECE))],
        out_specs=pl.BlockSpec((TM, D_MODEL), lambda i: (i, 0)),
        out_shape=jax.ShapeDtypeStruct((s_len, D_MODEL), BF16),
        compiler_params=_params(("parallel",)),
    )(proj, proj, proj, proj, olse, km, vm, wp_bd, pool_scale, gq_m, seg)


def _out_loss(mixed, x, target, w_out):
    s_len = x.shape[0]

    def body(mix_ref, x_ref, t_ref, w_ref, dout_ref, dmix_ref, dw16_ref, loss_ref, dw_ref):
        @pl.when(pl.program_id(0) == 0)
        def _():
            dw_ref[...] = jnp.zeros((D_MODEL, D_MODEL), F32)
            loss_ref[...] = jnp.zeros((8, 128), F32)

        mixed16 = mix_ref[...]
        err = x_ref[...] + _dot(mixed16, w_ref[...]) - t_ref[...]
        loss_ref[...] += 0.5 * jnp.sum(jnp.mean(err * err, axis=-1, keepdims=True))
        d_out = err / float(D_MODEL)
        dout_ref[...] = d_out
        d16 = d_out.astype(BF16)
        dmix_ref[...] = _dot_nt(d16, w_ref[...])
        dw_ref[...] += _dot_tn(mixed16, d16)

        @pl.when(pl.program_id(0) == pl.num_programs(0) - 1)
        def _():
            dw16_ref[...] = dw_ref[...].astype(BF16)

    row = pl.BlockSpec((TMM, D_MODEL), lambda i: (i, 0))
    return pl.pallas_call(
        body,
        name="out_loss",
        grid=(s_len // TMM,),
        in_specs=[row, row, row, _full((D_MODEL, D_MODEL))],
        out_specs=[row, row, _full((D_MODEL, D_MODEL)), _full((8, 128))],
        out_shape=[jax.ShapeDtypeStruct((s_len, D_MODEL), F32), jax.ShapeDtypeStruct((s_len, D_MODEL), F32),
                   jax.ShapeDtypeStruct((D_MODEL, D_MODEL), BF16), jax.ShapeDtypeStruct((8, 128), F32)],
        scratch_shapes=[pltpu.VMEM((D_MODEL, D_MODEL), F32)],
        compiler_params=_params(("arbitrary",)),
    )(mixed, x, target, w_out)


def _silu_pair(g):
    s = _sigmoid(g)
    return g * s, s * (1.0 + g * (1.0 - s))


def _mix_bwd(proj, olse, d_mixed, km, vm, wp_bd, pool_scale, gq_m, seg):
    s_len = proj.shape[0]
    n_tiles = s_len // TM

    def body(ua_ref, halo_ref, ga_next_ref, gb_ref, qm_ref, ol_ref, dm_ref, dm_next_ref, km_ref, vm_ref, wp_ref, sc_ref, gq_ref,
             seg_ref, dua_ref, dgb_ref, dqm_ref, do_ref, dl_ref, dwp_ref, dsc_ref, dkm_ref, dvm_ref, dgq_ref):
        i = pl.program_id(0)

        @pl.when(i == 0)
        def _():
            dwp_ref[...] = jnp.zeros((256, 256), F32)
            dsc_ref[...] = jnp.zeros((1, 256), F32)
            dkm_ref[...] = jnp.zeros((N_MEM, 256), F32)
            dvm_ref[...] = jnp.zeros((N_MEM, 256), F32)
            dgq_ref[...] = jnp.zeros((1, HEAD_DIM), F32)

        u = ua_ref[:, 0:256]
        g_a = ua_ref[:, 256:512]
        halo = jnp.where(i > 0, halo_ref[...], 0.0)
        cnt, lane = _pool_count(i * TM)
        d16 = _pool_delta(u, halo, cnt, lane).astype(BF16)
        t = _dot(d16, wp_ref[...])
        y_a = t * sc_ref[...]
        silu_a, dsilu_a = _silu_pair(g_a)
        dm_a = dm_ref[:, 0:256]
        dy_a = dm_a * silu_a
        dsc_ref[...] += jnp.sum(dy_a * t, axis=0, keepdims=True)
        dt16 = (dy_a * sc_ref[...]).astype(BF16)
        dwp_ref[...] += _dot_tn(d16, dt16)
        dd = _dot_nt(dt16, wp_ref[...])
        g_next = ga_next_ref[...]
        dt_next = (dm_next_ref[...] * (g_next * _sigmoid(g_next)) * sc_ref[...]).astype(BF16)
        dd_next = jnp.where(i < n_tiles - 1, _dot_nt(dt_next, wp_ref[...]), 0.0)
        cnt_next = _pool_count((i + 1) * TM)[0][0:HALO]
        dua_ref[:, 0:256] = _pool_delta_bwd(dd, dd_next, cnt, cnt_next, lane).astype(BF16)
        dua_ref[:, 256:512] = (dm_a * y_a * dsilu_a).astype(BF16)

        silu_b, dsilu_b = _silu_pair(gb_ref[...])
        dm_b = dm_ref[:, 256:768]
        o_all = jnp.concatenate([ol_ref[h][:, 0:HEAD_DIM] for h in range(FOX_HEADS)], axis=1)
        d_o = dm_b * silu_b
        dgb_ref[...] = (dm_b * o_all * dsilu_b).astype(BF16)
        for h in range(FOX_HEADS):
            do_ref[h] = d_o[:, HEAD_DIM * h:HEAD_DIM * (h + 1)].astype(BF16)
        prod = d_o * o_all
        hi = prod.astype(BF16)
        lo = (prod - hi.astype(F32)).astype(BF16)
        head_of_lane = lax.broadcasted_iota(jnp.int32, (FOX_HEADS, PIECE), 1) // HEAD_DIM
        pick = jnp.where(head_of_lane == lax.broadcasted_iota(jnp.int32, (FOX_HEADS, PIECE), 0), 1.0, 0.0).astype(BF16)
        delta = _dot_nt(pick, hi) + _dot_nt(pick, lo)
        for h in range(FOX_HEADS):
            dl_ref[h, 0] = jnp.broadcast_to(delta[h:h + 1, :], (8, TM))

        seg = seg_ref[0:256, 0:256]
        a, r, q16, heads = _mem_attn(qm_ref[:, 0:256], gq_ref[...], seg, km_ref, vm_ref)
        silu_m, dsilu_m = _silu_pair(qm_ref[:, 256:512])
        dm_m = dm_ref[:, 768:1024]
        y_m = jnp.concatenate([y for _, y in heads], axis=1)
        dqm_ref[:, 256:512] = (dm_m * y_m * dsilu_m).astype(BF16)
        dy16_all = (dm_m * silu_m).astype(BF16)
        d_scores = []
        for h in range(MEM_HEADS):
            sl = slice(HEAD_DIM * h, HEAD_DIM * (h + 1))
            p = heads[h][0]
            dy16 = dy16_all[:, sl]
            dp = _dot_nt(dy16, vm_ref[:, sl])
            dvm_ref[:, sl] += _dot_tn(p.astype(BF16), dy16)
            ds16 = (p * (dp - jnp.sum(dp * p, axis=-1, keepdims=True))).astype(BF16)
            dkm_ref[:, sl] += _dot_tn(ds16, q16[:, sl])
            d_scores.append(_dot(ds16, km_ref[:, sl]))
        dq, dg_rows = _head_rms_bwd(a, r, gq_ref[...], jnp.concatenate(d_scores, axis=1) * 0.125, seg)
        dqm_ref[:, 0:256] = dq.astype(BF16)
        dgq_ref[...] += _fold_heads(jnp.sum(dg_rows, axis=0, keepdims=True), MEM_HEADS)

    piece, before, after = _row_specs(s_len)
    n_halo = s_len // HALO
    row = pl.BlockSpec((TM, D_MODEL), lambda i: (i, 0))
    dm_after = pl.BlockSpec((HALO, 256), lambda i: (jnp.minimum((i + 1) * (TM // HALO), n_halo - 1), 0))
    out_piece = pl.BlockSpec((TM, PIECE), lambda i: (i, 0))
    return pl.pallas_call(
        body,
        name="mix_bwd",
        grid=(n_tiles,),
        in_specs=[piece(0), before(0), after(1), piece(4), piece(5), pl.BlockSpec((FOX_HEADS, TM, 128), lambda i: (0, i, 0)),
                  row, dm_after, _full((N_MEM, 256)), _full((N_MEM, 256)), _full((256, 256)), _full((1, 256)), _full((1, 256)),
                  _full((PIECE, PIECE))],
        out_specs=[out_piece, out_piece, out_piece, pl.BlockSpec((FOX_HEADS, TM, HEAD_DIM), lambda i: (0, i, 0)),
                   pl.BlockSpec((FOX_HEADS, 1, 8, TM), lambda i: (0, i, 0, 0)),
                   _full((256, 256)), _full((1, 256)), _full((N_MEM, 256)), _full((N_MEM, 256)), _full((1, HEAD_DIM))],
        out_shape=[jax.ShapeDtypeStruct((s_len, PIECE), BF16)] * 3 + [
            jax.ShapeDtypeStruct((FOX_HEADS, s_len, HEAD_DIM), BF16),
            jax.ShapeDtypeStruct((FOX_HEADS, n_tiles, 8, TM), F32),
            jax.ShapeDtypeStruct((256, 256), F32), jax.ShapeDtypeStruct((1, 256), F32),
            jax.ShapeDtypeStruct((N_MEM, 256), F32), jax.ShapeDtypeStruct((N_MEM, 256), F32),
            jax.ShapeDtypeStruct((1, HEAD_DIM), F32)],
        compiler_params=_params(("arbitrary",)),
    )(proj, proj, proj, proj, proj, olse, d_mixed, d_mixed, km, vm, wp_bd, pool_scale, gq_m, seg)


def _fox_bwd(q_aug, k_aug, kt_aug, v_h, d_o, lse_rows, delta_rows, dw_kv16, dw_out16):
    s_len = q_aug.shape[1]
    n_tiles = s_len // TA
    n_groups = FOX_HEADS // HB

    def body(q_ref, k_ref, kt_ref, v_ref, do_ref, st_ref, dl_ref, dkv16_ref, dout16_ref, dqt_ref, dk_ref, dv_ref, land_kv, land_out,
             send_sems, recv_sems, local_sems):
        j = pl.program_id(1)
        remote, local = _row_block_exchange(dkv16_ref, dout16_ref, land_kv, land_out, send_sems, recv_sems, local_sems, gather=False)

        @pl.when((pl.program_id(0) == 0) & (j == 0))
        def _():
            for cp in remote + local:
                cp.start()

        @pl.when(j == 0)
        def _():
            dqt_ref[...] = jnp.zeros((HB, n_tiles, 128, TA), F32)

        ks = [k_ref[h] for h in range(HB)]
        kts = [kt_ref[h, 0, 0:AUG_ROWS, :] for h in range(HB)]
        vs = [v_ref[h] for h in range(HB)]

        def pair(i0, acc, masked, width):
            rows = pl.ds(pl.multiple_of(i0 * TA, TA), width * TA)
            qs = [q_ref[h, rows, :] for h in range(HB)]
            d_os = [do_ref[h, rows, :] for h in range(HB)]
            row_stat = lambda ref, h: jnp.concatenate([ref[h, i0 + t, 0:1, :] for t in range(width)], axis=1)
            scores = [(_dot_nt(ks[h], qs[h]), _dot_nt(vs[h], d_os[h])) for h in range(HB)]
            probs = []
            for h in range(HB):
                s, dp = scores[h]
                if masked:
                    s = jnp.where(_causal_mask_t(), s, -1e30)
                p = jnp.exp(s - row_stat(st_ref, h))
                probs.append((p.astype(BF16), (p * (dp - row_stat(dl_ref, h))).astype(BF16)))
            out = []
            for h in range(HB):
                p16, ds16 = probs[h]
                dqt = _dot(kts[h], ds16)
                for t in range(width):
                    dqt_ref[h, i0 + t, 0:AUG_ROWS, :] += dqt[:, t * TA:(t + 1) * TA]
                out.append((acc[h][0] + _dot(ds16, qs[h]), acc[h][1] + _dot(p16, d_os[h])))
            return tuple(out)

        zero = tuple((jnp.zeros((TA, 128), F32), jnp.zeros((TA, HEAD_DIM), F32)) for _ in range(HB))
        acc = pair(j, zero, True, 1)
        odd = (n_tiles - 1 - j) % 2
        acc = lax.fori_loop(j + 1, j + 1 + odd, lambda i, a: pair(i, a, False, 1), acc)
        acc = lax.fori_loop(0, (n_tiles - 1 - j) // 2, lambda t, a: pair(j + 1 + odd + 2 * t, a, False, 2), acc)
        for h in range(HB):
            dk_ref[h] = acc[h][0]
            dv_ref[h] = acc[h][1]

        @pl.when((pl.program_id(0) == n_groups - 1) & (j == n_tiles - 1))
        def _():
            for cp in remote:
                cp.wait_recv()
            for cp in remote:
                cp.wait_send()
            for cp in local:
                cp.wait()

    whole = lambda w: pl.BlockSpec((HB, s_len, w), lambda g, j: (g, 0, 0))
    tile = lambda w: pl.BlockSpec((HB, TA, w), lambda g, j: (g, j, 0))
    rows_blk = lambda r: pl.BlockSpec((HB, n_tiles, r, TA), lambda g, j: (g, 0, 0, 0))
    hbm = pl.BlockSpec(memory_space=pl.ANY)
    return pl.pallas_call(
        body,
        name="fox_bwd",
        grid=(n_groups, n_tiles),
        in_specs=[whole(128), tile(128), pl.BlockSpec((HB, 1, 128, TA), lambda g, j: (g, j, 0, 0)), tile(HEAD_DIM),
                  whole(HEAD_DIM), rows_blk(8), rows_blk(8), hbm, hbm],
        out_specs=[rows_blk(128), tile(128), tile(HEAD_DIM), hbm, hbm],
        out_shape=[jax.ShapeDtypeStruct((FOX_HEADS, n_tiles, 128, TA), F32), jax.ShapeDtypeStruct((FOX_HEADS, s_len, 128), F32),
                   jax.ShapeDtypeStruct((FOX_HEADS, s_len, HEAD_DIM), F32),
                   jax.ShapeDtypeStruct((N_DEV, 128, 512), BF16), jax.ShapeDtypeStruct((N_DEV, 128, D_MODEL), BF16)],
        scratch_shapes=[pltpu.SemaphoreType.DMA((14,)), pltpu.SemaphoreType.DMA((14,)), pltpu.SemaphoreType.DMA((2,))],
        compiler_params=_params(("arbitrary", "arbitrary")),
    )(q_aug, k_aug, kt_aug, v_h, d_o, lse_rows, delta_rows, dw_kv16, dw_out16)


def _fox_post(proj, bf_pad, gq, gk, seg, dq_aug, dk_aug, dv_h):
    s_len = proj.shape[0]
    n_tiles = s_len // TM

    def body(q_ref, k_ref, f_ref, bf_ref, gq_ref, gk_ref, seg_ref, dqa_ref, dka_ref, dvh_ref,
             dq_ref, dk_ref, dv_ref, df_ref, dgq_ref, dgk_ref, dbf_ref, carry_ref):
        @pl.when(pl.program_id(0) == 0)
        def _():
            carry_ref[...] = jnp.zeros((1, 128), F32)
            dgq_ref[...] = jnp.zeros((1, HEAD_DIM), F32)
            dgk_ref[...] = jnp.zeros((1, HEAD_DIM), F32)
            dbf_ref[...] = jnp.zeros((1, 128), F32)

        lane = lax.broadcasted_iota(jnp.int32, (TM, 128), 1)
        seg = seg_ref[...]
        dqas = [dqa_ref[h, 0].T for h in range(FOX_HEADS)]
        dkas = [dka_ref[h] for h in range(FOX_HEADS)]
        d_cum = jnp.zeros((TM, 128), F32)
        for h in range(FOX_HEADS):
            d_cum = jnp.where(lane == h, dqas[h][:, 64:65] - dkas[h][:, 67:68], d_cum)
        q_all = q_ref[...]
        k_all = k_ref[...]
        rq = _head_rms(q_all, seg)
        rk = _head_rms(k_all, seg)
        dy_q = jnp.concatenate([t[:, 0:HEAD_DIM] for t in dqas], axis=1) * 0.125
        dy_k = jnp.concatenate([t[:, 0:HEAD_DIM] for t in dkas], axis=1)
        dq, dgq_rows = _head_rms_bwd(q_all * rq, rq, gq_ref[...], dy_q, seg)
        dk, dgk_rows = _head_rms_bwd(k_all * rk, rk, gk_ref[...], dy_k, seg)
        dq_ref[...] = dq.astype(BF16)
        dk_ref[...] = dk.astype(BF16)
        dv_ref[...] = jnp.concatenate([dvh_ref[h] for h in range(FOX_HEADS)], axis=1).astype(BF16)
        dgq_ref[...] += _fold_heads(jnp.sum(dgq_rows, axis=0, keepdims=True), FOX_HEADS)
        dgk_ref[...] += _fold_heads(jnp.sum(dgk_rows, axis=0, keepdims=True), FOX_HEADS)

        row = lax.broadcasted_iota(jnp.int32, (TM, TM), 0)
        col = lax.broadcasted_iota(jnp.int32, (TM, TM), 1)
        tri = jnp.where(col >= row, 1.0, 0.0).astype(BF16)
        hi, mid, lo = _split3(d_cum)
        d_logf = _dot(tri, hi) + _dot(tri, mid) + _dot(tri, lo) + carry_ref[...]
        carry_ref[...] = d_logf[0:1, :]
        z = f_ref[...] + bf_ref[...]
        d_f = jnp.where(lane < FOX_HEADS, d_logf / (1.0 + jnp.exp(z)), 0.0)
        df_ref[...] = d_f.astype(BF16)
        dbf_ref[...] += jnp.sum(d_f, axis=0, keepdims=True)

    rev = lambda i: n_tiles - 1 - i
    col_blk = lambda j: pl.BlockSpec((TM, PIECE), lambda i: (rev(i), j))
    head_blk = lambda w: pl.BlockSpec((FOX_HEADS, TM, w), lambda i: (0, rev(i), 0))
    out_piece = pl.BlockSpec((TM, PIECE), lambda i: (rev(i), 0))
    return pl.pallas_call(
        body,
        name="fox_post",
        grid=(n_tiles,),
        in_specs=[col_blk(1), col_blk(2), pl.BlockSpec((TM, 128), lambda i: (rev(i), MY_F_OFF // 128)),
                  _full((1, 128)), _full((1, PIECE)), _full((1, PIECE)), _full((PIECE, PIECE)),
                  pl.BlockSpec((FOX_HEADS, 1, 128, TM), lambda i: (0, rev(i), 0, 0)), head_blk(128), head_blk(HEAD_DIM)],
        out_specs=[out_piece, out_piece, out_piece, pl.BlockSpec((TM, 128), lambda i: (rev(i), 0)),
                   _full((1, HEAD_DIM)), _full((1, HEAD_DIM)), _full((1, 128))],
        out_shape=[jax.ShapeDtypeStruct((s_len, PIECE), BF16)] * 3 + [
            jax.ShapeDtypeStruct((s_len, 128), BF16), jax.ShapeDtypeStruct((1, HEAD_DIM), F32),
            jax.ShapeDtypeStruct((1, HEAD_DIM), F32), jax.ShapeDtypeStruct((1, 128), F32)],
        scratch_shapes=[pltpu.VMEM((1, 128), F32)],
        compiler_params=_params(("arbitrary",)),
    )(proj, proj, proj, bf_pad, gq, gk, seg, dq_aug, dk_aug, dv_h)


def _mem_bwd(mem, g, w_kv, gk, dkm, dvm):
    def body(mem_ref, g_ref, w_ref, gk_ref, dkm_ref, dvm_ref, dw_ref, dg_ref, dgk_ref, dkv_ref):
        m = mem_ref[...]
        r = _rms(m)
        a = m * r
        mn16 = (a * g_ref[...]).astype(BF16)
        kv = _dot(mn16, w_ref[...])
        dgk = jnp.zeros((N_MEM, HEAD_DIM), F32)
        for h in range(MEM_HEADS):
            sl = slice(HEAD_DIM * h, HEAD_DIM * (h + 1))
            kh = kv[:, sl]
            rk = _rms(kh)
            dk, dg_rows = _rms_bwd(kh * rk, rk, gk_ref[...], dkm_ref[:, sl])
            dkv_ref[:, sl] = dk.astype(BF16)
            dgk = dgk + dg_rows
        dkv_ref[:, 256:512] = dvm_ref[...].astype(BF16)
        dgk_ref[...] = jnp.sum(dgk, axis=0, keepdims=True)
        dkv16 = dkv_ref[...]
        dw_ref[...] = _dot_tn(mn16, dkv16).astype(BF16)
        dg_ref[...] = jnp.sum(_dot_nt(dkv16, w_ref[...]) * a, axis=0, keepdims=True)

    return pl.pallas_call(
        body,
        name="mem_bwd",
        out_shape=(jax.ShapeDtypeStruct((D_MODEL, 512), BF16), jax.ShapeDtypeStruct((1, D_MODEL), F32),
                   jax.ShapeDtypeStruct((1, HEAD_DIM), F32)),
        in_specs=[pl.BlockSpec(memory_space=pltpu.VMEM)] * 6,
        out_specs=(pl.BlockSpec(memory_space=pltpu.VMEM),) * 3,
        scratch_shapes=[pltpu.VMEM((N_MEM, 512), BF16)],
        compiler_params=pltpu.CompilerParams(vmem_limit_bytes=VMEM_LIMIT),
    )(mem, g, w_kv, gk, dkm, dvm)


def _grad_x(pieces, d_f, w_my, x, norm_g, d_out):
    s_len = x.shape[0]

    def body(p0, p1, p2, p3, p4, p5, df_ref, w_ref, x_ref, g_ref, dout_ref, gx_ref, dg_ref):
        @pl.when(pl.program_id(0) == 0)
        def _():
            dg_ref[...] = jnp.zeros((1, D_MODEL), F32)

        dh = _dot_nt(df_ref[...], w_ref[:, MY_F_OFF:MY_WIDTH])
        for j, p in enumerate((p0, p1, p2, p3, p4, p5)):
            dh = dh + _dot_nt(p[...], w_ref[:, PIECE * j:PIECE * (j + 1)])
        xv = x_ref[...]
        r = _rms(xv)
        dx, dg_rows = _rms_bwd(xv * r, r, g_ref[...], dh)
        gx_ref[...] = dout_ref[...] + dx
        dg_ref[...] += jnp.sum(dg_rows, axis=0, keepdims=True)

    piece = pl.BlockSpec((TMM, PIECE), lambda i: (i, 0))
    row = pl.BlockSpec((TMM, D_MODEL), lambda i: (i, 0))
    return pl.pallas_call(
        body,
        name="grad_x",
        grid=(s_len // TMM,),
        in_specs=[piece] * 6 + [pl.BlockSpec((TMM, 128), lambda i: (i, 0)), _full((D_MODEL, MY_WIDTH)), row,
                                _full((1, D_MODEL)), row],
        out_specs=[row, _full((1, D_MODEL))],
        out_shape=[jax.ShapeDtypeStruct((s_len, D_MODEL), F32), jax.ShapeDtypeStruct((1, D_MODEL), F32)],
        compiler_params=_params(("arbitrary",)),
    )(*pieces, d_f, w_my, x, norm_g, d_out)


def _grad_w_in(h16, pieces, d_f):
    s_len = h16.shape[0]
    tk = 512

    def body(h_ref, p0, p1, p2, p3, p4, p5, df_ref, out_ref, dw_ref):
        @pl.when(pl.program_id(0) == 0)
        def _():
            dw_ref[...] = jnp.zeros((D_MODEL, MY_WIDTH), F32)

        h = h_ref[...]
        for j, p in enumerate((p0, p1, p2, p3, p4, p5)):
            dw_ref[:, PIECE * j:PIECE * (j + 1)] += _dot_tn(h, p[...])
        dw_ref[:, MY_F_OFF:MY_WIDTH] += _dot_tn(h, df_ref[...])

        @pl.when(pl.program_id(0) == pl.num_programs(0) - 1)
        def _():
            for d in range(N_DEV):
                parts = [dw_ref[:, start:start + width] for start, width in _chunk_segments(d)]
                out_ref[d] = (parts[0] if len(parts) == 1 else jnp.concatenate(parts, axis=1)).astype(BF16)

    piece = pl.BlockSpec((tk, PIECE), lambda i: (i, 0))
    return pl.pallas_call(
        body,
        name="grad_w_in",
        grid=(s_len // tk,),
        in_specs=[pl.BlockSpec((tk, D_MODEL), lambda i: (i, 0))] + [piece] * 6 + [pl.BlockSpec((tk, 128), lambda i: (i, 0))],
        out_specs=_full((N_DEV, D_MODEL, IN_CHUNK)),
        out_shape=jax.ShapeDtypeStruct((N_DEV, D_MODEL, IN_CHUNK), BF16),
        scratch_shapes=[pltpu.VMEM((D_MODEL, MY_WIDTH), F32)],
        compiler_params=_params(("arbitrary",)),
    )(h16, *pieces, d_f)


def _adamw(w, g, m, v):
    m = ADAM_B1 * m + (1.0 - ADAM_B1) * g
    v = ADAM_B2 * v + (1.0 - ADAM_B2) * (g * g)
    m_hat = m / (1.0 - ADAM_B1 ** ADAM_STEP)
    v_hat = v / (1.0 - ADAM_B2 ** ADAM_STEP)
    return -ADAM_LR * (m_hat / (jnp.sqrt(v_hat) + ADAM_EPS) + ADAM_WD * w), m, v


PARAM_ORDER = ("norm_g", "w_in", "b_f", "w_pool", "pool_scale", "fox_q_g", "fox_k_g", "mem_norm_g", "w_mem_kv", "mem_q_g", "mem_k_g", "w_out")


def _update(red_in, red_kv, red_out, sred, params):
    def body(rin_ref, rkv_ref, rout_ref, sred_ref, *refs):
        ins, outs = refs[:3 * len(PARAM_ORDER)], refs[3 * len(PARAM_ORDER):]
        wide = lambda row: jnp.concatenate([sred_ref[row + k:row + k + 1, :] for k in range(4)], axis=1)
        head = lambda row: sred_ref[row:row + 1, 0:HEAD_DIM]
        grads = {
            "norm_g": lambda: wide(SB_NORM_G), "w_in": lambda: rin_ref[...], "b_f": lambda: sred_ref[SB_B_F:SB_B_F + 1, 0:FOX_HEADS],
            "pool_scale": lambda: sred_ref[SB_POOL_SCALE:SB_POOL_SCALE + 1, :], "fox_q_g": lambda: head(SB_FOX_Q),
            "fox_k_g": lambda: head(SB_FOX_K), "mem_norm_g": lambda: wide(SB_MEM_NORM_G), "w_mem_kv": lambda: rkv_ref[...],
            "mem_q_g": lambda: head(SB_MEM_Q), "mem_k_g": lambda: head(SB_MEM_K), "w_out": lambda: rout_ref[...],
        }
        for p, name in enumerate(PARAM_ORDER):
            w_ref, m_ref, v_ref = ins[3 * p:3 * p + 3]
            g_out, d_out, m_out, v_out = outs[4 * p:4 * p + 4]
            if name == "w_pool":
                for grp in range(4):
                    g = sred_ref[SB_W_POOL:SB_W_POOL + 64, 64 * grp:64 * (grp + 1)]
                    delta, m_new, v_new = _adamw(w_ref[grp], g, m_ref[grp], v_ref[grp])
                    g_out[grp], d_out[grp], m_out[grp], v_out[grp] = g, delta, m_new, v_new
            else:
                g = grads[name]()
                delta, m_new, v_new = _adamw(w_ref[...], g, m_ref[...], v_ref[...])
                g_out[...], d_out[...], m_out[...], v_out[...] = g, delta, m_new, v_new

    flat = [t for name in PARAM_ORDER for t in params[name]]
    shapes = [params[name][0].shape for name in PARAM_ORDER for _ in range(4)]
    outs = pl.pallas_call(
        body,
        name="adamw_update",
        out_shape=tuple(jax.ShapeDtypeStruct(s, F32) for s in shapes),
        in_specs=[pl.BlockSpec(memory_space=pltpu.VMEM)] * (4 + len(flat)),
        out_specs=(pl.BlockSpec(memory_space=pltpu.VMEM),) * len(shapes),
        compiler_params=pltpu.CompilerParams(vmem_limit_bytes=VMEM_LIMIT),
    )(red_in, red_kv, red_out, sred, *flat)
    return {name: outs[4 * p:4 * p + 4] for p, name in enumerate(PARAM_ORDER)}


def kernel(x, mem, norm_g, w_in, b_f, w_pool, pool_scale, fox_q_g, fox_k_g, mem_norm_g, w_mem_kv, mem_q_g, mem_k_g, w_out, loss_target, m_norm_g, m_w_in, m_b_f, m_w_pool, m_pool_scale, m_fox_q_g, m_fox_k_g, m_mem_norm_g, m_w_mem_kv, m_mem_q_g, m_mem_k_g, m_w_out, v_norm_g, v_w_in, v_b_f, v_w_pool, v_pool_scale, v_fox_q_g, v_fox_k_g, v_mem_norm_g, v_w_mem_kv, v_mem_q_g, v_mem_k_g, v_w_out):
    given = dict(norm_g=(norm_g, m_norm_g, v_norm_g), w_in=(w_in, m_w_in, v_w_in), b_f=(b_f, m_b_f, v_b_f),
                 w_pool=(w_pool, m_w_pool, v_w_pool), pool_scale=(pool_scale, m_pool_scale, v_pool_scale),
                 fox_q_g=(fox_q_g, m_fox_q_g, v_fox_q_g), fox_k_g=(fox_k_g, m_fox_k_g, v_fox_k_g),
                 mem_norm_g=(mem_norm_g, m_mem_norm_g, v_mem_norm_g), w_mem_kv=(w_mem_kv, m_w_mem_kv, v_w_mem_kv),
                 mem_q_g=(mem_q_g, m_mem_q_g, v_mem_q_g), mem_k_g=(mem_k_g, m_mem_k_g, v_mem_k_g), w_out=(w_out, m_w_out, v_w_out))
    params = {name: tuple(t[0] if t.ndim > 2 else t for t in wmv) for name, wmv in given.items()}
    x2, mem2, target2 = x[0], mem[0], loss_target[0]

    seg = jnp.asarray(np.kron(np.eye(FOX_HEADS), np.full((HEAD_DIM, HEAD_DIM), 1.0 / HEAD_DIM)), BF16)
    w_my, w_kv16, w_out16, wp_bd, bf_pad, gq8, gk8, gqm4 = _gather_w_in(
        params["w_in"][0], params["w_mem_kv"][0], params["w_out"][0], params["w_pool"][0], b_f, fox_q_g, fox_k_g, mem_q_g)
    proj, h16 = _fwd_proj(x2, norm_g, w_my)
    q_aug, k_aug, v_h, kt_aug, vt_aug = _fox_prep(proj, bf_pad, gq8, gk8, seg)
    olse, lse_rows, w_kv_all, w_out_all = _fox_fwd(q_aug, k_aug, vt_aug, w_kv16, w_out16)
    km, vm = _mem_prep(mem2, mem_norm_g, w_kv_all, mem_k_g)
    mixed = _mix_fwd(proj, olse, km, vm, wp_bd, pool_scale, gqm4, seg)
    d_out, d_mixed, dw_out, loss_blk = _out_loss(mixed, x2, target2, w_out_all)

    d_ua, d_gb, d_qm, d_o, delta_rows, dwp_bd, d_scale, dkm, dvm, d_gqm = _mix_bwd(proj, olse, d_mixed, km, vm, wp_bd, pool_scale,
                                                                                    gqm4, seg)
    dw_kv, d_mem_g, d_gkm = _mem_bwd(mem2, mem_norm_g, w_kv_all, mem_k_g, dkm, dvm)
    dq_aug, dk_aug, dv_h, land_kv, land_out = _fox_bwd(q_aug, k_aug, kt_aug, v_h, d_o, lse_rows, delta_rows, dw_kv, dw_out)
    d_q, d_k, d_v, d_f, d_gq, d_gk, d_bf = _fox_post(proj, bf_pad, gq8, gk8, seg, dq_aug, dk_aug, dv_h)
    pieces = (d_ua, d_q, d_k, d_v, d_gb, d_qm)
    dw_in = _grad_w_in(h16, pieces, d_f)
    grad_x, d_norm_g = _grad_x(pieces, d_f, w_my, x2, norm_g, d_out)

    red_in, red_kv, red_out, sred = _exchange_grads(dw_in, land_kv, land_out, d_norm_g, d_mem_g, d_scale, d_bf, d_gq, d_gk, d_gqm,
                                                    d_gkm, dwp_bd, loss_blk)
    new = _update(red_in, red_kv, red_out, sred, params)
    result = [sred[SB_LOSS, 0], grad_x[None]]
    for kind in range(4):
        for name in PARAM_ORDER:
            out = new[name][kind]
            result.append(out[None] if given[name][0].ndim > 2 else out)
    return tuple(result)
```

```python
import functools

import jax
import jax.numpy as jnp
import numpy as np
from jax import lax
from jax.experimental import pallas as pl
from jax.experimental.pallas import tpu as pltpu

F32 = jnp.float32
BF16 = jnp.bfloat16
MESH = pl.DeviceIdType.MESH

N_DEV = 8
D_MODEL = 1024
HEAD_DIM = 64
FOX_HEADS = 8
MEM_HEADS = 4
N_MEM = 256
POOL_WIDTH = 256
EPS = 1e-6
IN_WIDTH = 3080
IN_CHUNK = IN_WIDTH // N_DEV
F_OFF = 2048
MY_WIDTH = 3200
MY_F_OFF = 3072
PIECE = 512
TM = 256
TMM = 512
TA = 256
HB = 4
HB_FWD = 8
AUG_ROWS = 72
HALO = 16
VMEM_LIMIT = 56 * 1024 * 1024

ADAM_LR = 0.001
ADAM_B1 = 0.9
ADAM_B2 = 0.999
ADAM_EPS = 1e-08
ADAM_WD = 0.01
ADAM_STEP = 10


def _dot(a, b):
    return jnp.dot(a, b, preferred_element_type=F32)


def _dot_nt(a, b):
    return lax.dot_general(a, b, (((1,), (1,)), ((), ())), preferred_element_type=F32)


def _dot_tn(a, b):
    return lax.dot_general(a, b, (((0,), (0,)), ((), ())), preferred_element_type=F32)


def _sigmoid(g):
    return 0.5 + 0.5 * jnp.tanh(0.5 * g)


def _split3(v):
    hi = v.astype(BF16)
    r1 = v - hi.astype(F32)
    mid = r1.astype(BF16)
    lo = (r1 - mid.astype(F32)).astype(BF16)
    return hi, mid, lo


def _rms(v):
    return lax.rsqrt(jnp.mean(v * v, axis=-1, keepdims=True) + EPS)


def _rms_bwd(a, r, g, dy):
    da = dy * g
    return r * (da - a * jnp.mean(da * a, axis=-1, keepdims=True)), dy * a


def _head_mean(z, seg):
    hi = z.astype(BF16)
    lo = (z - hi.astype(F32)).astype(BF16)
    if z.shape[1] == 256:
        return _dot(hi, seg) + _dot(lo, seg)
    half = seg[0:256, 0:256]
    return jnp.concatenate([_dot(hi[:, 0:256], half) + _dot(lo[:, 0:256], half),
                            _dot(hi[:, 256:512], half) + _dot(lo[:, 256:512], half)], axis=1)


def _head_rms(v, seg):
    return lax.rsqrt(_head_mean(v * v, seg) + EPS)


def _head_rms_bwd(a, r, g, dy, seg):
    da = dy * g
    return r * (da - a * _head_mean(da * a, seg)), dy * a


def _fold_heads(row, n_heads):
    out = row[:, 0:HEAD_DIM]
    for h in range(1, n_heads):
        out = out + row[:, HEAD_DIM * h:HEAD_DIM * (h + 1)]
    return out


def _params(sem, limit=VMEM_LIMIT):
    return pltpu.CompilerParams(dimension_semantics=sem, vmem_limit_bytes=limit)


def _full(shape):
    return pl.BlockSpec(shape, lambda *_: (0,) * len(shape))


def _chunk_segments(d):
    runs = []
    for lo, hi, shift in ((0, F_OFF, 0), (F_OFF, F_OFF + FOX_HEADS, MY_F_OFF - F_OFF), (F_OFF + FOX_HEADS, IN_WIDTH, -FOX_HEADS)):
        a, b = max(lo, IN_CHUNK * d), min(hi, IN_CHUNK * (d + 1))
        if a < b:
            runs.append((a + shift, b - a))
    return runs


def _my_place():
    x, y, c = lax.axis_index("x"), lax.axis_index("y"), lax.axis_index("c")
    return x, y, c, 4 * x + 2 * y + c


def _shard_rows(dev):
    return pl.ds(pl.multiple_of(128 * dev, 128), 128)


def _gather_w_in(w_in, w_kv, w_out, w_pool, b_f, gq, gk, gqm):
    def body(win_ref, wkv_ref, wout_ref, wp_ref, bf_ref, gq_ref, gk_ref, gqm_ref, wmy_ref, kv16_ref, out16_ref, wpbd_ref, bfpad_ref,
             gq8_ref, gk8_ref, gqm4_ref, in_all, pay_in, send_sems, recv_sems):
        x, y, c, me = _my_place()
        sibling = (x, y, 1 - c)
        chips = [(1 - x, y), (x, 1 - y), (1 - x, 1 - y)]

        pay_in[...] = win_ref[...].astype(BF16)

        def copy(k, block, to, own=False):
            px, py, pc = block
            dst = in_all.at[4 * px + 2 * py + pc]
            return pltpu.make_async_remote_copy(src_ref=pay_in if own else dst, dst_ref=dst, send_sem=send_sems.at[k],
                                                recv_sem=recv_sems.at[k], device_id=to, device_id_type=MESH)

        first = [copy(0, (x, y, c), sibling, own=True)]
        first += [copy(1 + j, (x, y, c), (*chip, c), own=True) for j, chip in enumerate(chips)]
        for cp in first:
            cp.start()
        in_all[me] = pay_in[...]
        kv16_ref[...] = wkv_ref[...].astype(BF16)
        out16_ref[...] = wout_ref[...].astype(BF16)
        wpbd_ref[...] = jnp.zeros((POOL_WIDTH, POOL_WIDTH), BF16)
        for grp in range(4):
            sl = slice(64 * grp, 64 * (grp + 1))
            wpbd_ref[sl, sl] = wp_ref[grp].astype(BF16)
        bfpad_ref[...] = jnp.zeros((1, 128), F32)
        bfpad_ref[:, 0:FOX_HEADS] = bf_ref[...]
        gq8_ref[...] = jnp.concatenate([gq_ref[...]] * FOX_HEADS, axis=1)
        gk8_ref[...] = jnp.concatenate([gk_ref[...]] * FOX_HEADS, axis=1)
        gqm4_ref[...] = jnp.concatenate([gqm_ref[...]] * MEM_HEADS, axis=1)
        passed = []
        for j, chip in enumerate(chips):
            copy(1 + j, (*chip, c), (x, y, c)).wait_recv()
            passed.append(copy(4 + j, (*chip, c), sibling))
            passed[-1].start()
        copy(0, sibling, (x, y, c)).wait_recv()
        for j, chip in enumerate(chips):
            copy(4 + j, (*chip, 1 - c), (x, y, c)).wait_recv()
        for cp in first + passed:
            cp.wait_send()

        for r0 in range(0, D_MODEL, 256):
            ch = [in_all[d, r0:r0 + 256, :].astype(F32) for d in range(N_DEV)]
            lo = F_OFF - 5 * IN_CHUNK
            full = jnp.concatenate(ch[:5] + [ch[5][:, :lo], ch[5][:, lo + FOX_HEADS:], ch[6], ch[7], ch[5][:, lo:lo + FOX_HEADS],
                                             jnp.zeros((256, MY_WIDTH - IN_WIDTH), F32)], axis=1)
            wmy_ref[r0:r0 + 256, :] = full.astype(BF16)

    return pl.pallas_call(
        body,
        name="gather_w_in",
        out_shape=(jax.ShapeDtypeStruct((D_MODEL, MY_WIDTH), BF16), jax.ShapeDtypeStruct((128, 512), BF16),
                   jax.ShapeDtypeStruct((128, D_MODEL), BF16), jax.ShapeDtypeStruct((POOL_WIDTH, POOL_WIDTH), BF16),
                   jax.ShapeDtypeStruct((1, 128), F32), jax.ShapeDtypeStruct((1, PIECE), F32), jax.ShapeDtypeStruct((1, PIECE), F32),
                   jax.ShapeDtypeStruct((1, 256), F32)),
        in_specs=[pl.BlockSpec(memory_space=pltpu.VMEM)] * 8,
        out_specs=(pl.BlockSpec(memory_space=pltpu.VMEM),) * 8,
        scratch_shapes=[
            pltpu.VMEM((N_DEV, D_MODEL, IN_CHUNK), BF16),
            pltpu.VMEM((D_MODEL, IN_CHUNK), BF16),
            pltpu.SemaphoreType.DMA((7,)),
            pltpu.SemaphoreType.DMA((7,)),
        ],
        compiler_params=pltpu.CompilerParams(vmem_limit_bytes=VMEM_LIMIT),
    )(w_in, w_kv, w_out, w_pool, b_f, gq, gk, gqm)


def _row_block_exchange(kv_ref, out_ref, kv_dst, out_dst, send_sems, recv_sems, local_sems, gather):
    x, y, c, me = _my_place()
    remote = []
    for k in range(1, N_DEV):
        px, py, pc = x ^ (k >> 2), y ^ ((k >> 1) & 1), c ^ (k & 1)
        peer = 4 * px + 2 * py + pc
        for fam, (src, dst) in enumerate(((kv_ref, kv_dst), (out_ref, out_dst))):
            remote.append(pltpu.make_async_remote_copy(
                src_ref=src if gather else src.at[_shard_rows(peer)], dst_ref=dst.at[_shard_rows(me)] if gather else dst.at[me],
                send_sem=send_sems.at[7 * fam + k - 1], recv_sem=recv_sems.at[7 * fam + k - 1],
                device_id=(px, py, pc), device_id_type=MESH))
    local = [pltpu.make_async_copy(src if gather else src.at[_shard_rows(me)], dst.at[_shard_rows(me)] if gather else dst.at[me],
                                   local_sems.at[fam]) for fam, (src, dst) in enumerate(((kv_ref, kv_dst), (out_ref, out_dst)))]
    return remote, local


SB_ROWS, SB_W = 80, 256
SB_NORM_G, SB_MEM_NORM_G, SB_POOL_SCALE, SB_B_F, SB_FOX_Q, SB_FOX_K, SB_MEM_Q, SB_MEM_K, SB_LOSS, SB_W_POOL = 0, 4, 8, 9, 10, 11, 12, 13, 14, 16


def _exchange_grads(dw_in, land_kv, land_out, d_norm_g, d_mem_g, d_scale, d_bf, d_gq, d_gk, d_gqm, d_gkm, dwp_bd, loss_blk):
    half = D_MODEL // 2

    def body(in_ref, lkv_ref, lout_ref, dng, dmg, dsc, dbf, dgq, dgk, dgqm, dgkm, dwp, loss_ref,
             rin_ref, rkv_ref, rout_ref, sred_ref, land1, send2a, send2b, keep2a, keep2b, land2a, land2b,
             send3a, send3b, land3a, land3b, sb_ref, sland, send_sems, recv_sems):
        x, y, c, me = _my_place()
        sibling, x_nbr, y_nbr = (x, y, 1 - c), (1 - x, y, c), (x, 1 - y, c)

        def rcopy(src, dst, sem, to):
            return pltpu.make_async_remote_copy(src_ref=src, dst_ref=dst, send_sem=send_sems.at[sem], recv_sem=recv_sems.at[sem],
                                                device_id=to, device_id_type=MESH)

        sb_ref[...] = jnp.zeros((SB_ROWS, SB_W), F32)
        for k in range(4):
            sb_ref[SB_NORM_G + k:SB_NORM_G + k + 1, :] = dng[:, SB_W * k:SB_W * (k + 1)]
            sb_ref[SB_MEM_NORM_G + k:SB_MEM_NORM_G + k + 1, :] = dmg[:, SB_W * k:SB_W * (k + 1)]
        sb_ref[SB_POOL_SCALE:SB_POOL_SCALE + 1, :] = dsc[...]
        sb_ref[SB_B_F:SB_B_F + 1, 0:128] = dbf[...]
        for row, ref in ((SB_FOX_Q, dgq), (SB_FOX_K, dgk), (SB_MEM_Q, dgqm), (SB_MEM_K, dgkm)):
            sb_ref[row:row + 1, 0:HEAD_DIM] = ref[...]
        sb_ref[SB_LOSS:SB_LOSS + 1, 0:128] = loss_ref[0:1, :]
        for g in range(4):
            sl = slice(64 * g, 64 * (g + 1))
            sb_ref[SB_W_POOL:SB_W_POOL + 64, sl] = dwp[sl, sl]

        small = []
        for k in range(1, N_DEV):
            px, py, pc = x ^ (k >> 2), y ^ ((k >> 1) & 1), c ^ (k & 1)
            small.append(rcopy(sb_ref, sland.at[me], k - 1, (px, py, pc)))
        stage1 = [rcopy(in_ref.at[2 * k + 1 - c], land1.at[k], 7 + k, sibling) for k in range(4)]
        for cp in small + stage1:
            cp.start()
        sland[me] = sb_ref[...]
        for cp in stage1:
            cp.wait_recv()

        top, bot = slice(0, half), slice(half, D_MODEL)

        def chip_sum(k, rows):
            return in_ref[2 * k + c, rows, :].astype(F32) + land1[k, rows, :].astype(F32)

        for j in range(2):
            send2a[j] = chip_sum(2 * (1 - x) + j, top).astype(BF16)
            keep2a[j] = chip_sum(2 * x + j, top)
            send2b[j] = chip_sum(2 * j + 1 - y, bot).astype(BF16)
            keep2b[j] = chip_sum(2 * j + y, bot)
        stage2 = [rcopy(send2a.at[j], land2a.at[j], 11 + j, x_nbr) for j in range(2)]
        stage2 += [rcopy(send2b.at[j], land2b.at[j], 13 + j, y_nbr) for j in range(2)]
        for cp in stage2:
            cp.start()
        for cp in stage2:
            cp.wait_recv()
        for j in range(2):
            keep2a[j] = keep2a[j] + land2a[j].astype(F32)
            keep2b[j] = keep2b[j] + land2b[j].astype(F32)
        send3a[...] = keep2a[1 - y].astype(BF16)
        send3b[...] = keep2b[1 - x].astype(BF16)
        stage3 = [rcopy(send3a, land3a, 15, y_nbr), rcopy(send3b, land3b, 16, x_nbr)]
        for cp in stage3:
            cp.start()
        for cp in stage3:
            cp.wait_recv()
        rin_ref[top, :] = keep2a[y] + land3a[...].astype(F32)
        rin_ref[bot, :] = keep2b[x] + land3b[...].astype(F32)

        for cp in small:
            cp.wait_recv()
        for cp in small + stage1 + stage2 + stage3:
            cp.wait_send()
        for land, red in ((lkv_ref, rkv_ref), (lout_ref, rout_ref), (sland, sred_ref)):
            acc = land[0].astype(F32)
            for d in range(1, N_DEV):
                acc = acc + land[d].astype(F32)
            red[...] = acc

    args = (dw_in, land_kv, land_out, d_norm_g, d_mem_g, d_scale, d_bf, d_gq, d_gk, d_gqm, d_gkm, dwp_bd, loss_blk)
    half_chunk = lambda n, dt: pltpu.VMEM((n, half, IN_CHUNK), dt)
    return pl.pallas_call(
        body,
        name="exchange_grads",
        out_shape=(jax.ShapeDtypeStruct((D_MODEL, IN_CHUNK), F32), jax.ShapeDtypeStruct((128, 512), F32),
                   jax.ShapeDtypeStruct((128, D_MODEL), F32), jax.ShapeDtypeStruct((SB_ROWS, SB_W), F32)),
        in_specs=[pl.BlockSpec(memory_space=pltpu.VMEM)] * len(args),
        out_specs=(pl.BlockSpec(memory_space=pltpu.VMEM),) * 4,
        scratch_shapes=[
            pltpu.VMEM((4, D_MODEL, IN_CHUNK), BF16),
            half_chunk(2, BF16), half_chunk(2, BF16), half_chunk(2, F32), half_chunk(2, F32), half_chunk(2, BF16), half_chunk(2, BF16),
            pltpu.VMEM((half, IN_CHUNK), BF16), pltpu.VMEM((half, IN_CHUNK), BF16),
            pltpu.VMEM((half, IN_CHUNK), BF16), pltpu.VMEM((half, IN_CHUNK), BF16),
            pltpu.VMEM((SB_ROWS, SB_W), F32),
            pltpu.VMEM((N_DEV, SB_ROWS, SB_W), F32),
            pltpu.SemaphoreType.DMA((17,)),
            pltpu.SemaphoreType.DMA((17,)),
        ],
        compiler_params=pltpu.CompilerParams(vmem_limit_bytes=VMEM_LIMIT),
    )(*args)


def _fwd_proj(x, norm_g, w_my):
    s_len = x.shape[0]

    def body(x_ref, g_ref, w_ref, proj_ref, h_ref):
        xv = x_ref[...]
        h = (xv * _rms(xv) * g_ref[...]).astype(BF16)
        h_ref[...] = h
        proj_ref[...] = _dot(h, w_ref[...])

    return pl.pallas_call(
        body,
        name="fwd_proj",
        grid=(s_len // TMM,),
        in_specs=[pl.BlockSpec((TMM, D_MODEL), lambda i: (i, 0)), _full((1, D_MODEL)), _full((D_MODEL, MY_WIDTH))],
        out_specs=[pl.BlockSpec((TMM, MY_WIDTH), lambda i: (i, 0)), pl.BlockSpec((TMM, D_MODEL), lambda i: (i, 0))],
        out_shape=[jax.ShapeDtypeStruct((s_len, MY_WIDTH), F32), jax.ShapeDtypeStruct((s_len, D_MODEL), BF16)],
        compiler_params=_params(("parallel",)),
    )(x, norm_g, w_my)


def _log_sigmoid(z):
    return jnp.minimum(z, 0.0) - jnp.log(1.0 + jnp.exp(-jnp.abs(z)))


def _forget_lane_maps():
    to_q = np.zeros((128, FOX_HEADS * 128), np.float32)
    to_k = np.zeros((128, FOX_HEADS * 128), np.float32)
    for h in range(FOX_HEADS):
        for term in range(3):
            to_q[8 * term + h, 128 * h + 64 + term] = 1.0
            to_q[24, 128 * h + 67 + term] = 1.0
            to_k[24, 128 * h + 64 + term] = 1.0
            to_k[8 * term + h, 128 * h + 67 + term] = -1.0
    return jnp.asarray(to_q, BF16), jnp.asarray(to_k, BF16)


def _fox_prep(proj, bf_pad, gq, gk, seg):
    s_len = proj.shape[0]
    to_q, to_k = _forget_lane_maps()

    def body(q_ref, k_ref, v_ref, f_ref, bf_ref, gq_ref, gk_ref, seg_ref, eq_ref, ek_ref,
             qa_ref, ka_ref, vh_ref, kt_ref, vt_ref, carry_ref):
        @pl.when(pl.program_id(0) == 0)
        def _():
            carry_ref[...] = jnp.zeros((1, 128), F32)

        lane = lax.broadcasted_iota(jnp.int32, (TM, 128), 1)
        logf = jnp.where(lane < FOX_HEADS, _log_sigmoid(f_ref[...] + bf_ref[...]), 0.0)
        row = lax.broadcasted_iota(jnp.int32, (TM, TM), 0)
        col = lax.broadcasted_iota(jnp.int32, (TM, TM), 1)
        tri = jnp.where(col <= row, 1.0, 0.0).astype(BF16)
        hi, mid, lo = _split3(logf)
        cum = _dot(tri, hi) + _dot(tri, mid) + _dot(tri, lo) + carry_ref[...]
        carry_ref[...] = cum[TM - 1:TM, :]
        f_hi, f_mid, f_lo = [t.astype(F32) for t in _split3(cum)]
        packed = (f_hi + pltpu.roll(f_mid, 8, 1) + pltpu.roll(f_lo, 16, 1) + jnp.where(lane == 24, 1.0, 0.0)).astype(BF16)
        extra_q = _dot(packed, eq_ref[...])
        extra_k = _dot(packed, ek_ref[...])
        one_at_64 = jnp.where(lane == 64, 1.0, 0.0)
        zeros = jnp.zeros((TM, HEAD_DIM), F32)
        q_all = q_ref[...]
        k_all = k_ref[...]
        qn_all = q_all * _head_rms(q_all, seg_ref[...]) * gq_ref[...] * 0.125
        kn_all = k_all * _head_rms(k_all, seg_ref[...]) * gk_ref[...]
        for h in range(FOX_HEADS):
            sl = slice(HEAD_DIM * h, HEAD_DIM * (h + 1))
            tile = slice(128 * h, 128 * (h + 1))
            qa = jnp.where(lane < 64, jnp.concatenate([qn_all[:, sl], zeros], axis=1), extra_q[:, tile])
            ka = jnp.where(lane < 64, jnp.concatenate([kn_all[:, sl], zeros], axis=1), extra_k[:, tile])
            vh = v_ref[:, sl]
            v_aug = jnp.where(lane < 64, jnp.concatenate([vh, zeros], axis=1), one_at_64)
            qa_ref[h] = qa.astype(BF16)
            ka_ref[h] = ka.astype(BF16)
            vh_ref[h] = vh.astype(BF16)
            kt_ref[h, 0] = ka.T.astype(BF16)
            vt_ref[h, 0] = v_aug.T.astype(BF16)

    col_blk = lambda j: pl.BlockSpec((TM, PIECE), lambda i: (i, j))
    head_blk = lambda w: pl.BlockSpec((FOX_HEADS, TM, w), lambda i: (0, i, 0))
    tile_blk = pl.BlockSpec((FOX_HEADS, 1, 128, TM), lambda i: (0, i, 0, 0))
    tiled = jax.ShapeDtypeStruct((FOX_HEADS, s_len // TM, 128, TM), BF16)
    return pl.pallas_call(
        body,
        name="fox_prep",
        grid=(s_len // TM,),
        in_specs=[col_blk(1), col_blk(2), col_blk(3), pl.BlockSpec((TM, 128), lambda i: (i, MY_F_OFF // 128)),
                  _full((1, 128)), _full((1, PIECE)), _full((1, PIECE)), _full((PIECE, PIECE)),
                  _full((128, FOX_HEADS * 128)), _full((128, FOX_HEADS * 128))],
        out_specs=[head_blk(128), head_blk(128), head_blk(HEAD_DIM), tile_blk, tile_blk],
        out_shape=[jax.ShapeDtypeStruct((FOX_HEADS, s_len, 128), BF16), jax.ShapeDtypeStruct((FOX_HEADS, s_len, 128), BF16),
                   jax.ShapeDtypeStruct((FOX_HEADS, s_len, HEAD_DIM), BF16), tiled, tiled],
        scratch_shapes=[pltpu.VMEM((1, 128), F32)],
        compiler_params=_params(("arbitrary",)),
    )(proj, proj, proj, proj, bf_pad, gq, gk, seg, to_q, to_k)


def _mem_prep(mem, g, w_kv, gk):
    def body(mem_ref, g_ref, w_ref, gk_ref, km_ref, vm_ref):
        m = mem_ref[...]
        kv = _dot((m * _rms(m) * g_ref[...]).astype(BF16), w_ref[...])
        for h in range(MEM_HEADS):
            sl = slice(HEAD_DIM * h, HEAD_DIM * (h + 1))
            kh = kv[:, sl]
            km_ref[:, sl] = (kh * _rms(kh) * gk_ref[...]).astype(BF16)
        vm_ref[...] = kv[:, 256:512].astype(BF16)

    return pl.pallas_call(
        body,
        name="mem_prep",
        out_shape=(jax.ShapeDtypeStruct((N_MEM, 256), BF16),) * 2,
        in_specs=[pl.BlockSpec(memory_space=pltpu.VMEM)] * 4,
        out_specs=(pl.BlockSpec(memory_space=pltpu.VMEM),) * 2,
        compiler_params=pltpu.CompilerParams(vmem_limit_bytes=VMEM_LIMIT),
    )(mem, g, w_kv, gk)


def _causal_mask():
    row = lax.broadcasted_iota(jnp.int32, (TA, TA), 0)
    col = lax.broadcasted_iota(jnp.int32, (TA, TA), 1)
    return col <= row


def _causal_mask_t():
    row = lax.broadcasted_iota(jnp.int32, (TA, TA), 0)
    col = lax.broadcasted_iota(jnp.int32, (TA, TA), 1)
    return row <= col


def _fox_fwd(q_aug, k_aug, vt_aug, w_kv16, w_out16):
    s_len = q_aug.shape[1]
    n_tiles = s_len // TA
    hb = HB_FWD
    n_groups = FOX_HEADS // hb

    def body(q_ref, k_ref, vt_ref, kv16_ref, out16_ref, o_ref, st_ref, kv_all, out_all, send_sems, recv_sems, local_sems):
        qi = pl.program_id(1)
        remote, local = _row_block_exchange(kv16_ref, out16_ref, kv_all, out_all, send_sems, recv_sems, local_sems, gather=True)

        @pl.when((pl.program_id(0) == 0) & (qi == 0))
        def _():
            for cp in remote + local:
                cp.start()

        qs = [q_ref[h] for h in range(hb)]

        def step(t0, carry, masked, width):
            rows = pl.ds(pl.multiple_of(t0 * TA, TA), width * TA)
            scores = [_dot_nt(k_ref[h, rows, :], qs[h]) for h in range(hb)]
            soft = []
            for h in range(hb):
                m, _ = carry[h]
                s = jnp.where(_causal_mask_t(), scores[h], -1e30) if masked else scores[h]
                m_new = jnp.maximum(m, jnp.max(s, axis=0, keepdims=True))
                soft.append((m_new, jnp.exp(m - m_new), jnp.exp(s - m_new).astype(BF16)))
            out = []
            for h, (m_new, alpha, p) in enumerate(soft):
                acc = alpha * carry[h][1]
                for t in range(width):
                    acc = acc + _dot(vt_ref[h, t0 + t, 0:AUG_ROWS, :], p[t * TA:(t + 1) * TA])
                out.append((m_new, acc))
            return tuple(out)

        init = tuple((jnp.full((1, TA), -1e30, F32), jnp.zeros((AUG_ROWS, TA), F32)) for _ in range(hb))
        carry = lax.fori_loop(0, qi // 2, lambda j, cr: step(2 * j, cr, False, 2), init)
        carry = lax.fori_loop(2 * (qi // 2), qi, lambda j, cr: step(j, cr, False, 1), carry)
        carry = step(qi, carry, True, 1)
        for h in range(hb):
            m, acc = carry[h]
            l = acc[HEAD_DIM:HEAD_DIM + 1, :]
            lse = m + jnp.log(l)
            o_ref[h] = jnp.concatenate([acc[0:HEAD_DIM] / l, jnp.broadcast_to(lse, (HEAD_DIM, TA))], axis=0).T
            st_ref[h, 0] = jnp.broadcast_to(lse, (8, TA))

        @pl.when((pl.program_id(0) == n_groups - 1) & (qi == n_tiles - 1))
        def _():
            for cp in remote:
                cp.wait_recv()
            for cp in remote:
                cp.wait_send()
            for cp in local:
                cp.wait()

    hbm = pl.BlockSpec(memory_space=pl.ANY)
    return pl.pallas_call(
        body,
        name="fox_fwd",
        grid=(n_groups, n_tiles),
        in_specs=[pl.BlockSpec((hb, TA, 128), lambda g, i: (g, i, 0)), pl.BlockSpec((hb, s_len, 128), lambda g, i: (g, 0, 0)),
                  pl.BlockSpec((hb, n_tiles, 128, TA), lambda g, i: (g, 0, 0, 0)), hbm, hbm],
        out_specs=[pl.BlockSpec((hb, TA, 128), lambda g, i: (g, i, 0)), pl.BlockSpec((hb, 1, 8, TA), lambda g, i: (g, i, 0, 0)),
                   hbm, hbm],
        out_shape=[jax.ShapeDtypeStruct((FOX_HEADS, s_len, 128), F32), jax.ShapeDtypeStruct((FOX_HEADS, n_tiles, 8, TA), F32),
                   jax.ShapeDtypeStruct((D_MODEL, 512), BF16), jax.ShapeDtypeStruct((D_MODEL, D_MODEL), BF16)],
        scratch_shapes=[pltpu.SemaphoreType.DMA((14,)), pltpu.SemaphoreType.DMA((14,)), pltpu.SemaphoreType.DMA((2,))],
        compiler_params=_params(("arbitrary", "arbitrary")),
    )(q_aug, k_aug, vt_aug, w_kv16, w_out16)


def _lane_group(lane, a, b, c, d):
    return jnp.where(lane < 64, a, jnp.where(lane < 128, b, jnp.where(lane < 192, c, d)))


def _pool_count(row0):
    lane = lax.broadcasted_iota(jnp.int32, (TM, POOL_WIDTH), 1)
    t1 = row0 + lax.broadcasted_iota(jnp.int32, (TM, POOL_WIDTH), 0) + 1
    return 1.0 / jnp.minimum(t1, _lane_group(lane, 2, 4, 8, 16)).astype(F32), lane


def _pool_delta(u, halo, cnt, lane):
    xe = jnp.concatenate([halo, u], axis=0)
    s2 = xe + pltpu.roll(xe, 1, 0)
    s4 = s2 + pltpu.roll(s2, 2, 0)
    s8 = s4 + pltpu.roll(s4, 4, 0)
    s16 = s8 + pltpu.roll(s8, 8, 0)
    win = _lane_group(lane, s2[HALO:], s4[HALO:], s8[HALO:], s16[HALO:])
    return win * cnt - u


def _pool_delta_bwd(dd, dd_next, cnt, cnt_next, lane):
    ee = jnp.concatenate([dd * cnt, dd_next * cnt_next], axis=0)
    n = TM + HALO
    r2 = ee + pltpu.roll(ee, n - 1, 0)
    r4 = r2 + pltpu.roll(r2, n - 2, 0)
    r8 = r4 + pltpu.roll(r4, n - 4, 0)
    r16 = r8 + pltpu.roll(r8, n - 8, 0)
    return _lane_group(lane, r2[:TM], r4[:TM], r8[:TM], r16[:TM]) - dd


def _mem_attn(qm, gq, seg, km_ref, vm_ref):
    r = _head_rms(qm, seg)
    a = qm * r
    q16 = (a * gq * 0.125).astype(BF16)
    heads = []
    for h in range(MEM_HEADS):
        sl = slice(HEAD_DIM * h, HEAD_DIM * (h + 1))
        s = _dot_nt(q16[:, sl], km_ref[:, sl])
        e = jnp.exp(s - jnp.max(s, axis=-1, keepdims=True))
        p = e * (1.0 / jnp.sum(e, axis=-1, keepdims=True))
        heads.append((p, _dot(p.astype(BF16), vm_ref[:, sl])))
    return a, r, q16, heads


def _row_specs(s_len):
    n_halo = s_len // HALO
    piece = lambda j: pl.BlockSpec((TM, PIECE), lambda i: (i, j))
    before = lambda j: pl.BlockSpec((HALO, 256), lambda i: (jnp.maximum(i * (TM // HALO) - 1, 0), j))
    after = lambda j: pl.BlockSpec((HALO, 256), lambda i: (jnp.minimum((i + 1) * (TM // HALO), n_halo - 1), j))
    return piece, before, after


def _mix_fwd(proj, olse, km, vm, wp_bd, pool_scale, gq_m, seg):
    s_len = proj.shape[0]

    def body(ua_ref, halo_ref, gb_ref, qm_ref, ol_ref, km_ref, vm_ref, wp_ref, sc_ref, gq_ref, seg_ref, out_ref):
        i = pl.program_id(0)
        u = ua_ref[:, 0:256]
        g_a = ua_ref[:, 256:512]
        halo = jnp.where(i > 0, halo_ref[...], 0.0)
        cnt, lane = _pool_count(i * TM)
        d = _pool_delta(u, halo, cnt, lane).astype(BF16)
        y_a = _dot(d, wp_ref[...]) * sc_ref[...]
        out_ref[:, 0:256] = (y_a * (g_a * _sigmoid(g_a))).astype(BF16)
        g_b = gb_ref[...]
        y_b = jnp.concatenate([ol_ref[h][:, 0:HEAD_DIM] for h in range(FOX_HEADS)], axis=1)
        out_ref[:, 256:768] = (y_b * (g_b * _sigmoid(g_b))).astype(BF16)
        _, _, _, heads = _mem_attn(qm_ref[:, 0:256], gq_ref[...], seg_ref[0:256, 0:256], km_ref, vm_ref)
        g_m = qm_ref[:, 256:512]
        y_m = jnp.concatenate([y for _, y in heads], axis=1)
        out_ref[:, 768:1024] = (y_m * (g_m * _sigmoid(g_m))).astype(BF16)

    piece, before, _ = _row_specs(s_len)
    return pl.pallas_call(
        body,
        name="mix_fwd",
        grid=(s_len // TM,),
        in_specs=[piece(0), before(0), piece(4), piece(5), pl.BlockSpec((FOX_HEADS, TM, 128), lambda i: (0, i, 0)),
                  _full((N_MEM, 256)), _full((N_MEM, 256)), _full((256, 256)), _full((1, 256)), _full((1, 256)),
                  _full((PIECE, PIECE))],
        out_specs=pl.BlockSpec((TM, D_MODEL), lambda i: (i, 0)),
        out_shape=jax.ShapeDtypeStruct((s_len, D_MODEL), BF16),
        compiler_params=_params(("parallel",)),
    )(proj, proj, proj, proj, olse, km, vm, wp_bd, pool_scale, gq_m, seg)


def _out_loss(mixed, x, target, w_out):
    s_len = x.shape[0]

    def body(mix_ref, x_ref, t_ref, w_ref, dout_ref, dmix_ref, dw16_ref, loss_ref, dw_ref):
        @pl.when(pl.program_id(0) == 0)
        def _():
            dw_ref[...] = jnp.zeros((D_MODEL, D_MODEL), F32)
            loss_ref[...] = jnp.zeros((8, 128), F32)

        mixed16 = mix_ref[...]
        err = x_ref[...] + _dot(mixed16, w_ref[...]) - t_ref[...]
        loss_ref[...] += 0.5 * jnp.sum(jnp.mean(err * err, axis=-1, keepdims=True))
        d_out = err / float(D_MODEL)
        dout_ref[...] = d_out
        d16 = d_out.astype(BF16)
        dmix_ref[...] = _dot_nt(d16, w_ref[...])
        dw_ref[...] += _dot_tn(mixed16, d16)

        @pl.when(pl.program_id(0) == pl.num_programs(0) - 1)
        def _():
            dw16_ref[...] = dw_ref[...].astype(BF16)

    row = pl.BlockSpec((TMM, D_MODEL), lambda i: (i, 0))
    return pl.pallas_call(
        body,
        name="out_loss",
        grid=(s_len // TMM,),
        in_specs=[row, row, row, _full((D_MODEL, D_MODEL))],
        out_specs=[row, row, _full((D_MODEL, D_MODEL)), _full((8, 128))],
        out_shape=[jax.ShapeDtypeStruct((s_len, D_MODEL), F32), jax.ShapeDtypeStruct((s_len, D_MODEL), F32),
                   jax.ShapeDtypeStruct((D_MODEL, D_MODEL), BF16), jax.ShapeDtypeStruct((8, 128), F32)],
        scratch_shapes=[pltpu.VMEM((D_MODEL, D_MODEL), F32)],
        compiler_params=_params(("arbitrary",)),
    )(mixed, x, target, w_out)


def _silu_pair(g):
    s = _sigmoid(g)
    return g * s, s * (1.0 + g * (1.0 - s))


def _mix_bwd(proj, olse, d_mixed, km, vm, wp_bd, pool_scale, gq_m, seg):
    s_len = proj.shape[0]
    n_tiles = s_len // TM

    def body(ua_ref, halo_ref, ga_next_ref, gb_ref, qm_ref, ol_ref, dm_ref, dm_next_ref, km_ref, vm_ref, wp_ref, sc_ref, gq_ref,
             seg_ref, dua_ref, dgb_ref, dqm_ref, do_ref, dl_ref, dwp_ref, dsc_ref, dkm_ref, dvm_ref, dgq_ref):
        i = pl.program_id(0)

        @pl.when(i == 0)
        def _():
            dwp_ref[...] = jnp.zeros((256, 256), F32)
            dsc_ref[...] = jnp.zeros((1, 256), F32)
            dkm_ref[...] = jnp.zeros((N_MEM, 256), F32)
            dvm_ref[...] = jnp.zeros((N_MEM, 256), F32)
            dgq_ref[...] = jnp.zeros((1, HEAD_DIM), F32)

        u = ua_ref[:, 0:256]
        g_a = ua_ref[:, 256:512]
        halo = jnp.where(i > 0, halo_ref[...], 0.0)
        cnt, lane = _pool_count(i * TM)
        d16 = _pool_delta(u, halo, cnt, lane).astype(BF16)
        t = _dot(d16, wp_ref[...])
        y_a = t * sc_ref[...]
        silu_a, dsilu_a = _silu_pair(g_a)
        dm_a = dm_ref[:, 0:256]
        dy_a = dm_a * silu_a
        dsc_ref[...] += jnp.sum(dy_a * t, axis=0, keepdims=True)
        dt16 = (dy_a * sc_ref[...]).astype(BF16)
        dwp_ref[...] += _dot_tn(d16, dt16)
        dd = _dot_nt(dt16, wp_ref[...])
        g_next = ga_next_ref[...]
        dt_next = (dm_next_ref[...] * (g_next * _sigmoid(g_next)) * sc_ref[...]).astype(BF16)
        dd_next = jnp.where(i < n_tiles - 1, _dot_nt(dt_next, wp_ref[...]), 0.0)
        cnt_next = _pool_count((i + 1) * TM)[0][0:HALO]
        dua_ref[:, 0:256] = _pool_delta_bwd(dd, dd_next, cnt, cnt_next, lane).astype(BF16)
        dua_ref[:, 256:512] = (dm_a * y_a * dsilu_a).astype(BF16)

        silu_b, dsilu_b = _silu_pair(gb_ref[...])
        dm_b = dm_ref[:, 256:768]
        o_all = jnp.concatenate([ol_ref[h][:, 0:HEAD_DIM] for h in range(FOX_HEADS)], axis=1)
        d_o = dm_b * silu_b
        dgb_ref[...] = (dm_b * o_all * dsilu_b).astype(BF16)
        for h in range(FOX_HEADS):
            do_ref[h] = d_o[:, HEAD_DIM * h:HEAD_DIM * (h + 1)].astype(BF16)
        prod = d_o * o_all
        hi = prod.astype(BF16)
        lo = (prod - hi.astype(F32)).astype(BF16)
        head_of_lane = lax.broadcasted_iota(jnp.int32, (FOX_HEADS, PIECE), 1) // HEAD_DIM
        pick = jnp.where(head_of_lane == lax.broadcasted_iota(jnp.int32, (FOX_HEADS, PIECE), 0), 1.0, 0.0).astype(BF16)
        delta = _dot_nt(pick, hi) + _dot_nt(pick, lo)
        for h in range(FOX_HEADS):
            dl_ref[h, 0] = jnp.broadcast_to(delta[h:h + 1, :], (8, TM))

        seg = seg_ref[0:256, 0:256]
        a, r, q16, heads = _mem_attn(qm_ref[:, 0:256], gq_ref[...], seg, km_ref, vm_ref)
        silu_m, dsilu_m = _silu_pair(qm_ref[:, 256:512])
        dm_m = dm_ref[:, 768:1024]
        y_m = jnp.concatenate([y for _, y in heads], axis=1)
        dqm_ref[:, 256:512] = (dm_m * y_m * dsilu_m).astype(BF16)
        dy16_all = (dm_m * silu_m).astype(BF16)
        d_scores = []
        for h in range(MEM_HEADS):
            sl = slice(HEAD_DIM * h, HEAD_DIM * (h + 1))
            p = heads[h][0]
            dy16 = dy16_all[:, sl]
            dp = _dot_nt(dy16, vm_ref[:, sl])
            dvm_ref[:, sl] += _dot_tn(p.astype(BF16), dy16)
            ds16 = (p * (dp - jnp.sum(dp * p, axis=-1, keepdims=True))).astype(BF16)
            dkm_ref[:, sl] += _dot_tn(ds16, q16[:, sl])
            d_scores.append(_dot(ds16, km_ref[:, sl]))
        dq, dg_rows = _head_rms_bwd(a, r, gq_ref[...], jnp.concatenate(d_scores, axis=1) * 0.125, seg)
        dqm_ref[:, 0:256] = dq.astype(BF16)
        dgq_ref[...] += _fold_heads(jnp.sum(dg_rows, axis=0, keepdims=True), MEM_HEADS)

    piece, before, after = _row_specs(s_len)
    n_halo = s_len // HALO
    row = pl.BlockSpec((TM, D_MODEL), lambda i: (i, 0))
    dm_after = pl.BlockSpec((HALO, 256), lambda i: (jnp.minimum((i + 1) * (TM // HALO), n_halo - 1), 0))
    out_piece = pl.BlockSpec((TM, PIECE), lambda i: (i, 0))
    return pl.pallas_call(
        body,
        name="mix_bwd",
        grid=(n_tiles,),
        in_specs=[piece(0), before(0), after(1), piece(4), piece(5), pl.BlockSpec((FOX_HEADS, TM, 128), lambda i: (0, i, 0)),
                  row, dm_after, _full((N_MEM, 256)), _full((N_MEM, 256)), _full((256, 256)), _full((1, 256)), _full((1, 256)),
                  _full((PIECE, PIECE))],
        out_specs=[out_piece, out_piece, out_piece, pl.BlockSpec((FOX_HEADS, TM, HEAD_DIM), lambda i: (0, i, 0)),
                   pl.BlockSpec((FOX_HEADS, 1, 8, TM), lambda i: (0, i, 0, 0)),
                   _full((256, 256)), _full((1, 256)), _full((N_MEM, 256)), _full((N_MEM, 256)), _full((1, HEAD_DIM))],
        out_shape=[jax.ShapeDtypeStruct((s_len, PIECE), BF16)] * 3 + [
            jax.ShapeDtypeStruct((FOX_HEADS, s_len, HEAD_DIM), BF16),
            jax.ShapeDtypeStruct((FOX_HEADS, n_tiles, 8, TM), F32),
            jax.ShapeDtypeStruct((256, 256), F32), jax.ShapeDtypeStruct((1, 256), F32),
            jax.ShapeDtypeStruct((N_MEM, 256), F32), jax.ShapeDtypeStruct((N_MEM, 256), F32),
            jax.ShapeDtypeStruct((1, HEAD_DIM), F32)],
        compiler_params=_params(("arbitrary",)),
    )(proj, proj, proj, proj, proj, olse, d_mixed, d_mixed, km, vm, wp_bd, pool_scale, gq_m, seg)


def _fox_bwd(q_aug, k_aug, kt_aug, v_h, d_o, lse_rows, delta_rows, dw_kv16, dw_out16):
    s_len = q_aug.shape[1]
    n_tiles = s_len // TA
    n_groups = FOX_HEADS // HB

    def body(q_ref, k_ref, kt_ref, v_ref, do_ref, st_ref, dl_ref, dkv16_ref, dout16_ref, dqt_ref, dk_ref, dv_ref, land_kv, land_out,
             send_sems, recv_sems, local_sems):
        j = pl.program_id(1)
        remote, local = _row_block_exchange(dkv16_ref, dout16_ref, land_kv, land_out, send_sems, recv_sems, local_sems, gather=False)

        @pl.when((pl.program_id(0) == 0) & (j == 0))
        def _():
            for cp in remote + local:
                cp.start()

        @pl.when(j == 0)
        def _():
            dqt_ref[...] = jnp.zeros((HB, n_tiles, 128, TA), F32)

        ks = [k_ref[h] for h in range(HB)]
        kts = [kt_ref[h, 0, 0:AUG_ROWS, :] for h in range(HB)]
        vs = [v_ref[h] for h in range(HB)]

        def pair(i0, acc, masked, width):
            rows = pl.ds(pl.multiple_of(i0 * TA, TA), width * TA)
            qs = [q_ref[h, rows, :] for h in range(HB)]
            d_os = [do_ref[h, rows, :] for h in range(HB)]
            row_stat = lambda ref, h: jnp.concatenate([ref[h, i0 + t, 0:1, :] for t in range(width)], axis=1)
            scores = [(_dot_nt(ks[h], qs[h]), _dot_nt(vs[h], d_os[h])) for h in range(HB)]
            probs = []
            for h in range(HB):
                s, dp = scores[h]
                if masked:
                    s = jnp.where(_causal_mask_t(), s, -1e30)
                p = jnp.exp(s - row_stat(st_ref, h))
                probs.append((p.astype(BF16), (p * (dp - row_stat(dl_ref, h))).astype(BF16)))
            out = []
            for h in range(HB):
                p16, ds16 = probs[h]
                dqt = _dot(kts[h], ds16)
                for t in range(width):
                    dqt_ref[h, i0 + t, 0:AUG_ROWS, :] += dqt[:, t * TA:(t + 1) * TA]
                out.append((acc[h][0] + _dot(ds16, qs[h]), acc[h][1] + _dot(p16, d_os[h])))
            return tuple(out)

        zero = tuple((jnp.zeros((TA, 128), F32), jnp.zeros((TA, HEAD_DIM), F32)) for _ in range(HB))
        acc = pair(j, zero, True, 1)
        odd = (n_tiles - 1 - j) % 2
        acc = lax.fori_loop(j + 1, j + 1 + odd, lambda i, a: pair(i, a, False, 1), acc)
        acc = lax.fori_loop(0, (n_tiles - 1 - j) // 2, lambda t, a: pair(j + 1 + odd + 2 * t, a, False, 2), acc)
        for h in range(HB):
            dk_ref[h] = acc[h][0]
            dv_ref[h] = acc[h][1]

        @pl.when((pl.program_id(0) == n_groups - 1) & (j == n_tiles - 1))
        def _():
            for cp in remote:
                cp.wait_recv()
            for cp in remote:
                cp.wait_send()
            for cp in local:
                cp.wait()

    whole = lambda w: pl.BlockSpec((HB, s_len, w), lambda g, j: (g, 0, 0))
    tile = lambda w: pl.BlockSpec((HB, TA, w), lambda g, j: (g, j, 0))
    rows_blk = lambda r: pl.BlockSpec((HB, n_tiles, r, TA), lambda g, j: (g, 0, 0, 0))
    hbm = pl.BlockSpec(memory_space=pl.ANY)
    return pl.pallas_call(
        body,
        name="fox_bwd",
        grid=(n_groups, n_tiles),
        in_specs=[whole(128), tile(128), pl.BlockSpec((HB, 1, 128, TA), lambda g, j: (g, j, 0, 0)), tile(HEAD_DIM),
                  whole(HEAD_DIM), rows_blk(8), rows_blk(8), hbm, hbm],
        out_specs=[rows_blk(128), tile(128), tile(HEAD_DIM), hbm, hbm],
        out_shape=[jax.ShapeDtypeStruct((FOX_HEADS, n_tiles, 128, TA), F32), jax.ShapeDtypeStruct((FOX_HEADS, s_len, 128), F32),
                   jax.ShapeDtypeStruct((FOX_HEADS, s_len, HEAD_DIM), F32),
                   jax.ShapeDtypeStruct((N_DEV, 128, 512), BF16), jax.ShapeDtypeStruct((N_DEV, 128, D_MODEL), BF16)],
        scratch_shapes=[pltpu.SemaphoreType.DMA((14,)), pltpu.SemaphoreType.DMA((14,)), pltpu.SemaphoreType.DMA((2,))],
        compiler_params=_params(("arbitrary", "arbitrary")),
    )(q_aug, k_aug, kt_aug, v_h, d_o, lse_rows, delta_rows, dw_kv16, dw_out16)


def _fox_post(proj, bf_pad, gq, gk, seg, dq_aug, dk_aug, dv_h):
    s_len = proj.shape[0]
    n_tiles = s_len // TM

    def body(q_ref, k_ref, f_ref, bf_ref, gq_ref, gk_ref, seg_ref, dqa_ref, dka_ref, dvh_ref,
             dq_ref, dk_ref, dv_ref, df_ref, dgq_ref, dgk_ref, dbf_ref, carry_ref):
        @pl.when(pl.program_id(0) == 0)
        def _():
            carry_ref[...] = jnp.zeros((1, 128), F32)
            dgq_ref[...] = jnp.zeros((1, HEAD_DIM), F32)
            dgk_ref[...] = jnp.zeros((1, HEAD_DIM), F32)
            dbf_ref[...] = jnp.zeros((1, 128), F32)

        lane = lax.broadcasted_iota(jnp.int32, (TM, 128), 1)
        seg = seg_ref[...]
        dqas = [dqa_ref[h, 0].T for h in range(FOX_HEADS)]
        dkas = [dka_ref[h] for h in range(FOX_HEADS)]
        d_cum = jnp.zeros((TM, 128), F32)
        for h in range(FOX_HEADS):
            d_cum = jnp.where(lane == h, dqas[h][:, 64:65] - dkas[h][:, 67:68], d_cum)
        q_all = q_ref[...]
        k_all = k_ref[...]
        rq = _head_rms(q_all, seg)
        rk = _head_rms(k_all, seg)
        dy_q = jnp.concatenate([t[:, 0:HEAD_DIM] for t in dqas], axis=1) * 0.125
        dy_k = jnp.concatenate([t[:, 0:HEAD_DIM] for t in dkas], axis=1)
        dq, dgq_rows = _head_rms_bwd(q_all * rq, rq, gq_ref[...], dy_q, seg)
        dk, dgk_rows = _head_rms_bwd(k_all * rk, rk, gk_ref[...], dy_k, seg)
        dq_ref[...] = dq.astype(BF16)
        dk_ref[...] = dk.astype(BF16)
        dv_ref[...] = jnp.concatenate([dvh_ref[h] for h in range(FOX_HEADS)], axis=1).astype(BF16)
        dgq_ref[...] += _fold_heads(jnp.sum(dgq_rows, axis=0, keepdims=True), FOX_HEADS)
        dgk_ref[...] += _fold_heads(jnp.sum(dgk_rows, axis=0, keepdims=True), FOX_HEADS)

        row = lax.broadcasted_iota(jnp.int32, (TM, TM), 0)
        col = lax.broadcasted_iota(jnp.int32, (TM, TM), 1)
        tri = jnp.where(col >= row, 1.0, 0.0).astype(BF16)
        hi, mid, lo = _split3(d_cum)
        d_logf = _dot(tri, hi) + _dot(tri, mid) + _dot(tri, lo) + carry_ref[...]
        carry_ref[...] = d_logf[0:1, :]
        z = f_ref[...] + bf_ref[...]
        d_f = jnp.where(lane < FOX_HEADS, d_logf / (1.0 + jnp.exp(z)), 0.0)
        df_ref[...] = d_f.astype(BF16)
        dbf_ref[...] += jnp.sum(d_f, axis=0, keepdims=True)

    rev = lambda i: n_tiles - 1 - i
    col_blk = lambda j: pl.BlockSpec((TM, PIECE), lambda i: (rev(i), j))
    head_blk = lambda w: pl.BlockSpec((FOX_HEADS, TM, w), lambda i: (0, rev(i), 0))
    out_piece = pl.BlockSpec((TM, PIECE), lambda i: (rev(i), 0))
    return pl.pallas_call(
        body,
        name="fox_post",
        grid=(n_tiles,),
        in_specs=[col_blk(1), col_blk(2), pl.BlockSpec((TM, 128), lambda i: (rev(i), MY_F_OFF // 128)),
                  _full((1, 128)), _full((1, PIECE)), _full((1, PIECE)), _full((PIECE, PIECE)),
                  pl.BlockSpec((FOX_HEADS, 1, 128, TM), lambda i: (0, rev(i), 0, 0)), head_blk(128), head_blk(HEAD_DIM)],
        out_specs=[out_piece, out_piece, out_piece, pl.BlockSpec((TM, 128), lambda i: (rev(i), 0)),
                   _full((1, HEAD_DIM)), _full((1, HEAD_DIM)), _full((1, 128))],
        out_shape=[jax.ShapeDtypeStruct((s_len, PIECE), BF16)] * 3 + [
            jax.ShapeDtypeStruct((s_len, 128), BF16), jax.ShapeDtypeStruct((1, HEAD_DIM), F32),
            jax.ShapeDtypeStruct((1, HEAD_DIM), F32), jax.ShapeDtypeStruct((1, 128), F32)],
        scratch_shapes=[pltpu.VMEM((1, 128), F32)],
        compiler_params=_params(("arbitrary",)),
    )(proj, proj, proj, bf_pad, gq, gk, seg, dq_aug, dk_aug, dv_h)


def _mem_bwd(mem, g, w_kv, gk, dkm, dvm):
    def body(mem_ref, g_ref, w_ref, gk_ref, dkm_ref, dvm_ref, dw_ref, dg_ref, dgk_ref, dkv_ref):
        m = mem_ref[...]
        r = _rms(m)
        a = m * r
        mn16 = (a * g_ref[...]).astype(BF16)
        kv = _dot(mn16, w_ref[...])
        dgk = jnp.zeros((N_MEM, HEAD_DIM), F32)
        for h in range(MEM_HEADS):
            sl = slice(HEAD_DIM * h, HEAD_DIM * (h + 1))
            kh = kv[:, sl]
            rk = _rms(kh)
            dk, dg_rows = _rms_bwd(kh * rk, rk, gk_ref[...], dkm_ref[:, sl])
            dkv_ref[:, sl] = dk.astype(BF16)
            dgk = dgk + dg_rows
        dkv_ref[:, 256:512] = dvm_ref[...].astype(BF16)
        dgk_ref[...] = jnp.sum(dgk, axis=0, keepdims=True)
        dkv16 = dkv_ref[...]
        dw_ref[...] = _dot_tn(mn16, dkv16).astype(BF16)
        dg_ref[...] = jnp.sum(_dot_nt(dkv16, w_ref[...]) * a, axis=0, keepdims=True)

    return pl.pallas_call(
        body,
        name="mem_bwd",
        out_shape=(jax.ShapeDtypeStruct((D_MODEL, 512), BF16), jax.ShapeDtypeStruct((1, D_MODEL), F32),
                   jax.ShapeDtypeStruct((1, HEAD_DIM), F32)),
        in_specs=[pl.BlockSpec(memory_space=pltpu.VMEM)] * 6,
        out_specs=(pl.BlockSpec(memory_space=pltpu.VMEM),) * 3,
        scratch_shapes=[pltpu.VMEM((N_MEM, 512), BF16)],
        compiler_params=pltpu.CompilerParams(vmem_limit_bytes=VMEM_LIMIT),
    )(mem, g, w_kv, gk, dkm, dvm)


def _grad_x(pieces, d_f, w_my, x, norm_g, d_out):
    s_len = x.shape[0]

    def body(p0, p1, p2, p3, p4, p5, df_ref, w_ref, x_ref, g_ref, dout_ref, gx_ref, dg_ref):
        @pl.when(pl.program_id(0) == 0)
        def _():
            dg_ref[...] = jnp.zeros((1, D_MODEL), F32)

        dh = _dot_nt(df_ref[...], w_ref[:, MY_F_OFF:MY_WIDTH])
        for j, p in enumerate((p0, p1, p2, p3, p4, p5)):
            dh = dh + _dot_nt(p[...], w_ref[:, PIECE * j:PIECE * (j + 1)])
        xv = x_ref[...]
        r = _rms(xv)
        dx, dg_rows = _rms_bwd(xv * r, r, g_ref[...], dh)
        gx_ref[...] = dout_ref[...] + dx
        dg_ref[...] += jnp.sum(dg_rows, axis=0, keepdims=True)

    piece = pl.BlockSpec((TMM, PIECE), lambda i: (i, 0))
    row = pl.BlockSpec((TMM, D_MODEL), lambda i: (i, 0))
    return pl.pallas_call(
        body,
        name="grad_x",
        grid=(s_len // TMM,),
        in_specs=[piece] * 6 + [pl.BlockSpec((TMM, 128), lambda i: (i, 0)), _full((D_MODEL, MY_WIDTH)), row,
                                _full((1, D_MODEL)), row],
        out_specs=[row, _full((1, D_MODEL))],
        out_shape=[jax.ShapeDtypeStruct((s_len, D_MODEL), F32), jax.ShapeDtypeStruct((1, D_MODEL), F32)],
        compiler_params=_params(("arbitrary",)),
    )(*pieces, d_f, w_my, x, norm_g, d_out)


def _grad_w_in(h16, pieces, d_f):
    s_len = h16.shape[0]
    tk = 512

    def body(h_ref, p0, p1, p2, p3, p4, p5, df_ref, out_ref, dw_ref):
        @pl.when(pl.program_id(0) == 0)
        def _():
            dw_ref[...] = jnp.zeros((D_MODEL, MY_WIDTH), F32)

        h = h_ref[...]
        for j, p in enumerate((p0, p1, p2, p3, p4, p5)):
            dw_ref[:, PIECE * j:PIECE * (j + 1)] += _dot_tn(h, p[...])
        dw_ref[:, MY_F_OFF:MY_WIDTH] += _dot_tn(h, df_ref[...])

        @pl.when(pl.program_id(0) == pl.num_programs(0) - 1)
        def _():
            for d in range(N_DEV):
                parts = [dw_ref[:, start:start + width] for start, width in _chunk_segments(d)]
                out_ref[d] = (parts[0] if len(parts) == 1 else jnp.concatenate(parts, axis=1)).astype(BF16)

    piece = pl.BlockSpec((tk, PIECE), lambda i: (i, 0))
    return pl.pallas_call(
        body,
        name="grad_w_in",
        grid=(s_len // tk,),
        in_specs=[pl.BlockSpec((tk, D_MODEL), lambda i: (i, 0))] + [piece] * 6 + [pl.BlockSpec((tk, 128), lambda i: (i, 0))],
        out_specs=_full((N_DEV, D_MODEL, IN_CHUNK)),
        out_shape=jax.ShapeDtypeStruct((N_DEV, D_MODEL, IN_CHUNK), BF16),
        scratch_shapes=[pltpu.VMEM((D_MODEL, MY_WIDTH), F32)],
        compiler_params=_params(("arbitrary",)),
    )(h16, *pieces, d_f)


def _adamw(w, g, m, v):
    m = ADAM_B1 * m + (1.0 - ADAM_B1) * g
    v = ADAM_B2 * v + (1.0 - ADAM_B2) * (g * g)
    m_hat = m / (1.0 - ADAM_B1 ** ADAM_STEP)
    v_hat = v / (1.0 - ADAM_B2 ** ADAM_STEP)
    return -ADAM_LR * (m_hat / (jnp.sqrt(v_hat) + ADAM_EPS) + ADAM_WD * w), m, v


PARAM_ORDER = ("norm_g", "w_in", "b_f", "w_pool", "pool_scale", "fox_q_g", "fox_k_g", "mem_norm_g", "w_mem_kv", "mem_q_g", "mem_k_g", "w_out")


def _update(red_in, red_kv, red_out, sred, params):
    def body(rin_ref, rkv_ref, rout_ref, sred_ref, *refs):
        ins, outs = refs[:3 * len(PARAM_ORDER)], refs[3 * len(PARAM_ORDER):]
        wide = lambda row: jnp.concatenate([sred_ref[row + k:row + k + 1, :] for k in range(4)], axis=1)
        head = lambda row: sred_ref[row:row + 1, 0:HEAD_DIM]
        grads = {
            "norm_g": lambda: wide(SB_NORM_G), "w_in": lambda: rin_ref[...], "b_f": lambda: sred_ref[SB_B_F:SB_B_F + 1, 0:FOX_HEADS],
            "pool_scale": lambda: sred_ref[SB_POOL_SCALE:SB_POOL_SCALE + 1, :], "fox_q_g": lambda: head(SB_FOX_Q),
            "fox_k_g": lambda: head(SB_FOX_K), "mem_norm_g": lambda: wide(SB_MEM_NORM_G), "w_mem_kv": lambda: rkv_ref[...],
            "mem_q_g": lambda: head(SB_MEM_Q), "mem_k_g": lambda: head(SB_MEM_K), "w_out": lambda: rout_ref[...],
        }
        for p, name in enumerate(PARAM_ORDER):
            w_ref, m_ref, v_ref = ins[3 * p:3 * p + 3]
            g_out, d_out, m_out, v_out = outs[4 * p:4 * p + 4]
            if name == "w_pool":
                for grp in range(4):
                    g = sred_ref[SB_W_POOL:SB_W_POOL + 64, 64 * grp:64 * (grp + 1)]
                    delta, m_new, v_new = _adamw(w_ref[grp], g, m_ref[grp], v_ref[grp])
                    g_out[grp], d_out[grp], m_out[grp], v_out[grp] = g, delta, m_new, v_new
            else:
                g = grads[name]()
                delta, m_new, v_new = _adamw(w_ref[...], g, m_ref[...], v_ref[...])
                g_out[...], d_out[...], m_out[...], v_out[...] = g, delta, m_new, v_new

    flat = [t for name in PARAM_ORDER for t in params[name]]
    shapes = [params[name][0].shape for name in PARAM_ORDER for _ in range(4)]
    outs = pl.pallas_call(
        body,
        name="adamw_update",
        out_shape=tuple(jax.ShapeDtypeStruct(s, F32) for s in shapes),
        in_specs=[pl.BlockSpec(memory_space=pltpu.VMEM)] * (4 + len(flat)),
        out_specs=(pl.BlockSpec(memory_space=pltpu.VMEM),) * len(shapes),
        compiler_params=pltpu.CompilerParams(vmem_limit_bytes=VMEM_LIMIT),
    )(red_in, red_kv, red_out, sred, *flat)
    return {name: outs[4 * p:4 * p + 4] for p, name in enumerate(PARAM_ORDER)}


def kernel(x, mem, norm_g, w_in, b_f, w_pool, pool_scale, fox_q_g, fox_k_g, mem_norm_g, w_mem_kv, mem_q_g, mem_k_g, w_out, loss_target, m_norm_g, m_w_in, m_b_f, m_w_pool, m_pool_scale, m_fox_q_g, m_fox_k_g, m_mem_norm_g, m_w_mem_kv, m_mem_q_g, m_mem_k_g, m_w_out, v_norm_g, v_w_in, v_b_f, v_w_pool, v_pool_scale, v_fox_q_g, v_fox_k_g, v_mem_norm_g, v_w_mem_kv, v_mem_q_g, v_mem_k_g, v_w_out):
    given = dict(norm_g=(norm_g, m_norm_g, v_norm_g), w_in=(w_in, m_w_in, v_w_in), b_f=(b_f, m_b_f, v_b_f),
                 w_pool=(w_pool, m_w_pool, v_w_pool), pool_scale=(pool_scale, m_pool_scale, v_pool_scale),
                 fox_q_g=(fox_q_g, m_fox_q_g, v_fox_q_g), fox_k_g=(fox_k_g, m_fox_k_g, v_fox_k_g),
                 mem_norm_g=(mem_norm_g, m_mem_norm_g, v_mem_norm_g), w_mem_kv=(w_mem_kv, m_w_mem_kv, v_w_mem_kv),
                 mem_q_g=(mem_q_g, m_mem_q_g, v_mem_q_g), mem_k_g=(mem_k_g, m_mem_k_g, v_mem_k_g), w_out=(w_out, m_w_out, v_w_out))
    params = {name: tuple(t[0] if t.ndim > 2 else t for t in wmv) for name, wmv in given.items()}
    x2, mem2, target2 = x[0], mem[0], loss_target[0]

    seg = jnp.asarray(np.kron(np.eye(FOX_HEADS), np.full((HEAD_DIM, HEAD_DIM), 1.0 / HEAD_DIM)), BF16)
    w_my, w_kv16, w_out16, wp_bd, bf_pad, gq8, gk8, gqm4 = _gather_w_in(
        params["w_in"][0], params["w_mem_kv"][0], params["w_out"][0], params["w_pool"][0], b_f, fox_q_g, fox_k_g, mem_q_g)
    proj, h16 = _fwd_proj(x2, norm_g, w_my)
    q_aug, k_aug, v_h, kt_aug, vt_aug = _fox_prep(proj, bf_pad, gq8, gk8, seg)
    olse, lse_rows, w_kv_all, w_out_all = _fox_fwd(q_aug, k_aug, vt_aug, w_kv16, w_out16)
    km, vm = _mem_prep(mem2, mem_norm_g, w_kv_all, mem_k_g)
    mixed = _mix_fwd(proj, olse, km, vm, wp_bd, pool_scale, gqm4, seg)
    d_out, d_mixed, dw_out, loss_blk = _out_loss(mixed, x2, target2, w_out_all)

    d_ua, d_gb, d_qm, d_o, delta_rows, dwp_bd, d_scale, dkm, dvm, d_gqm = _mix_bwd(proj, olse, d_mixed, km, vm, wp_bd, pool_scale,
                                                                                    gqm4, seg)
    dw_kv, d_mem_g, d_gkm = _mem_bwd(mem2, mem_norm_g, w_kv_all, mem_k_g, dkm, dvm)
    dq_aug, dk_aug, dv_h, land_kv, land_out = _fox_bwd(q_aug, k_aug, kt_aug, v_h, d_o, lse_rows, delta_rows, dw_kv, dw_out)
    d_q, d_k, d_v, d_f, d_gq, d_gk, d_bf = _fox_post(proj, bf_pad, gq8, gk8, seg, dq_aug, dk_aug, dv_h)
    pieces = (d_ua, d_q, d_k, d_v, d_gb, d_qm)
    dw_in = _grad_w_in(h16, pieces, d_f)
    grad_x, d_norm_g = _grad_x(pieces, d_f, w_my, x2, norm_g, d_out)

    red_in, red_kv, red_out, sred = _exchange_grads(dw_in, land_kv, land_out, d_norm_g, d_mem_g, d_scale, d_bf, d_gq, d_gk, d_gqm,
                                                    d_gkm, dwp_bd, loss_blk)
    new = _update(red_in, red_kv, red_out, sred, params)
    result = [sred[SB_LOSS, 0], grad_x[None]]
    for kind in range(4):
        for name in PARAM_ORDER:
            out = new[name][kind]
            result.append(out[None] if given[name][0].ndim > 2 else out)
    return tuple(result)
```

```python
import functools

import jax
import jax.numpy as jnp
import numpy as np
from jax import lax
from jax.experimental import pallas as pl
from jax.experimental.pallas import tpu as pltpu

F32 = jnp.float32
BF16 = jnp.bfloat16
MESH = pl.DeviceIdType.MESH

N_DEV = 8
D_MODEL = 1024
HEAD_DIM = 64
FOX_HEADS = 8
MEM_HEADS = 4
N_MEM = 256
POOL_WIDTH = 256
EPS = 1e-6
IN_WIDTH = 3080
IN_CHUNK = IN_WIDTH // N_DEV
F_OFF = 2048
MY_WIDTH = 3200
MY_F_OFF = 3072
PIECE = 512
TM = 256
TMM = 512
TA = 256
HB = 4
HB_FWD = 8
AUG_ROWS = 72
HALO = 16
VMEM_LIMIT = 56 * 1024 * 1024

ADAM_LR = 0.001
ADAM_B1 = 0.9
ADAM_B2 = 0.999
ADAM_EPS = 1e-08
ADAM_WD = 0.01
ADAM_STEP = 10


def _dot(a, b):
    return jnp.dot(a, b, preferred_element_type=F32)


def _dot_nt(a, b):
    return lax.dot_general(a, b, (((1,), (1,)), ((), ())), preferred_element_type=F32)


def _dot_tn(a, b):
    return lax.dot_general(a, b, (((0,), (0,)), ((), ())), preferred_element_type=F32)


def _sigmoid(g):
    return 0.5 + 0.5 * jnp.tanh(0.5 * g)


def _split3(v):
    hi = v.astype(BF16)
    r1 = v - hi.astype(F32)
    mid = r1.astype(BF16)
    lo = (r1 - mid.astype(F32)).astype(BF16)
    return hi, mid, lo


def _rms(v):
    return lax.rsqrt(jnp.mean(v * v, axis=-1, keepdims=True) + EPS)


def _rms_bwd(a, r, g, dy):
    da = dy * g
    return r * (da - a * jnp.mean(da * a, axis=-1, keepdims=True)), dy * a


def _head_mean(z, seg):
    hi = z.astype(BF16)
    lo = (z - hi.astype(F32)).astype(BF16)
    if z.shape[1] == 256:
        return _dot(hi, seg) + _dot(lo, seg)
    half = seg[0:256, 0:256]
    return jnp.concatenate([_dot(hi[:, 0:256], half) + _dot(lo[:, 0:256], half),
                            _dot(hi[:, 256:512], half) + _dot(lo[:, 256:512], half)], axis=1)


def _head_rms(v, seg):
    return lax.rsqrt(_head_mean(v * v, seg) + EPS)


def _head_rms_bwd(a, r, g, dy, seg):
    da = dy * g
    return r * (da - a * _head_mean(da * a, seg)), dy * a


def _fold_heads(row, n_heads):
    out = row[:, 0:HEAD_DIM]
    for h in range(1, n_heads):
        out = out + row[:, HEAD_DIM * h:HEAD_DIM * (h + 1)]
    return out


def _params(sem, limit=VMEM_LIMIT):
    return pltpu.CompilerParams(dimension_semantics=sem, vmem_limit_bytes=limit)


def _full(shape):
    return pl.BlockSpec(shape, lambda *_: (0,) * len(shape))


def _chunk_segments(d):
    runs = []
    for lo, hi, shift in ((0, F_OFF, 0), (F_OFF, F_OFF + FOX_HEADS, MY_F_OFF - F_OFF), (F_OFF + FOX_HEADS, IN_WIDTH, -FOX_HEADS)):
        a, b = max(lo, IN_CHUNK * d), min(hi, IN_CHUNK * (d + 1))
        if a < b:
            runs.append((a + shift, b - a))
    return runs


def _my_place():
    x, y, c = lax.axis_index("x"), lax.axis_index("y"), lax.axis_index("c")
    return x, y, c, 4 * x + 2 * y + c


def _shard_rows(dev):
    return pl.ds(pl.multiple_of(128 * dev, 128), 128)


def _gather_w_in(w_in, w_kv, w_out, w_pool, b_f, gq, gk, gqm):
    def body(win_ref, wkv_ref, wout_ref, wp_ref, bf_ref, gq_ref, gk_ref, gqm_ref, wmy_ref, kv16_ref, out16_ref, wpbd_ref, bfpad_ref,
             gq8_ref, gk8_ref, gqm4_ref, in_all, pay_in, send_sems, recv_sems):
        x, y, c, me = _my_place()
        sibling = (x, y, 1 - c)
        chips = [(1 - x, y), (x, 1 - y), (1 - x, 1 - y)]

        pay_in[...] = win_ref[...].astype(BF16)

        def copy(k, block, to, own=False):
            px, py, pc = block
            dst = in_all.at[4 * px + 2 * py + pc]
            return pltpu.make_async_remote_copy(src_ref=pay_in if own else dst, dst_ref=dst, send_sem=send_sems.at[k],
                                                recv_sem=recv_sems.at[k], device_id=to, device_id_type=MESH)

        first = [copy(0, (x, y, c), sibling, own=True)]
        first += [copy(1 + j, (x, y, c), (*chip, c), own=True) for j, chip in enumerate(chips)]
        for cp in first:
            cp.start()
        in_all[me] = pay_in[...]
        kv16_ref[...] = wkv_ref[...].astype(BF16)
        out16_ref[...] = wout_ref[...].astype(BF16)
        wpbd_ref[...] = jnp.zeros((POOL_WIDTH, POOL_WIDTH), BF16)
        for grp in range(4):
            sl = slice(64 * grp, 64 * (grp + 1))
            wpbd_ref[sl, sl] = wp_ref[grp].astype(BF16)
        bfpad_ref[...] = jnp.zeros((1, 128), F32)
        bfpad_ref[:, 0:FOX_HEADS] = bf_ref[...]
        gq8_ref[...] = jnp.concatenate([gq_ref[...]] * FOX_HEADS, axis=1)
        gk8_ref[...] = jnp.concatenate([gk_ref[...]] * FOX_HEADS, axis=1)
        gqm4_ref[...] = jnp.concatenate([gqm_ref[...]] * MEM_HEADS, axis=1)
        passed = []
        for j, chip in enumerate(chips):
            copy(1 + j, (*chip, c), (x, y, c)).wait_recv()
            passed.append(copy(4 + j, (*chip, c), sibling))
            passed[-1].start()
        copy(0, sibling, (x, y, c)).wait_recv()
        for j, chip in enumerate(chips):
            copy(4 + j, (*chip, 1 - c), (x, y, c)).wait_recv()
        for cp in first + passed:
            cp.wait_send()

        for r0 in range(0, D_MODEL, 256):
            ch = [in_all[d, r0:r0 + 256, :].astype(F32) for d in range(N_DEV)]
            lo = F_OFF - 5 * IN_CHUNK
            full = jnp.concatenate(ch[:5] + [ch[5][:, :lo], ch[5][:, lo + FOX_HEADS:], ch[6], ch[7], ch[5][:, lo:lo + FOX_HEADS],
                                             jnp.zeros((256, MY_WIDTH - IN_WIDTH), F32)], axis=1)
            wmy_ref[r0:r0 + 256, :] = full.astype(BF16)

    return pl.pallas_call(
        body,
        name="gather_w_in",
        out_shape=(jax.ShapeDtypeStruct((D_MODEL, MY_WIDTH), BF16), jax.ShapeDtypeStruct((128, 512), BF16),
                   jax.ShapeDtypeStruct((128, D_MODEL), BF16), jax.ShapeDtypeStruct((POOL_WIDTH, POOL_WIDTH), BF16),
                   jax.ShapeDtypeStruct((1, 128), F32), jax.ShapeDtypeStruct((1, PIECE), F32), jax.ShapeDtypeStruct((1, PIECE), F32),
                   jax.ShapeDtypeStruct((1, 256), F32)),
        in_specs=[pl.BlockSpec(memory_space=pltpu.VMEM)] * 8,
        out_specs=(pl.BlockSpec(memory_space=pltpu.VMEM),) * 8,
        scratch_shapes=[
            pltpu.VMEM((N_DEV, D_MODEL, IN_CHUNK), BF16),
            pltpu.VMEM((D_MODEL, IN_CHUNK), BF16),
            pltpu.SemaphoreType.DMA((7,)),
            pltpu.SemaphoreType.DMA((7,)),
        ],
        compiler_params=pltpu.CompilerParams(vmem_limit_bytes=VMEM_LIMIT),
    )(w_in, w_kv, w_out, w_pool, b_f, gq, gk, gqm)


def _row_block_exchange(kv_ref, out_ref, kv_dst, out_dst, send_sems, recv_sems, local_sems, gather):
    x, y, c, me = _my_place()
    remote = []
    for k in range(1, N_DEV):
        px, py, pc = x ^ (k >> 2), y ^ ((k >> 1) & 1), c ^ (k & 1)
        peer = 4 * px + 2 * py + pc
        for fam, (src, dst) in enumerate(((kv_ref, kv_dst), (out_ref, out_dst))):
            remote.append(pltpu.make_async_remote_copy(
                src_ref=src if gather else src.at[_shard_rows(peer)], dst_ref=dst.at[_shard_rows(me)] if gather else dst.at[me],
                send_sem=send_sems.at[7 * fam + k - 1], recv_sem=recv_sems.at[7 * fam + k - 1],
                device_id=(px, py, pc), device_id_type=MESH))
    local = [pltpu.make_async_copy(src if gather else src.at[_shard_rows(me)], dst.at[_shard_rows(me)] if gather else dst.at[me],
                                   local_sems.at[fam]) for fam, (src, dst) in enumerate(((kv_ref, kv_dst), (out_ref, out_dst)))]
    return remote, local


SB_ROWS, SB_W = 80, 256
SB_NORM_G, SB_MEM_NORM_G, SB_POOL_SCALE, SB_B_F, SB_FOX_Q, SB_FOX_K, SB_MEM_Q, SB_MEM_K, SB_LOSS, SB_W_POOL = 0, 4, 8, 9, 10, 11, 12, 13, 14, 16


def _fwd_proj(x, norm_g, w_my):
    s_len = x.shape[0]

    def body(x_ref, g_ref, w_ref, proj_ref, h_ref):
        xv = x_ref[...]
        h = (xv * _rms(xv) * g_ref[...]).astype(BF16)
        h_ref[...] = h
        proj_ref[...] = _dot(h, w_ref[...])

    return pl.pallas_call(
        body,
        name="fwd_proj",
        grid=(s_len // TMM,),
        in_specs=[pl.BlockSpec((TMM, D_MODEL), lambda i: (i, 0)), _full((1, D_MODEL)), _full((D_MODEL, MY_WIDTH))],
        out_specs=[pl.BlockSpec((TMM, MY_WIDTH), lambda i: (i, 0)), pl.BlockSpec((TMM, D_MODEL), lambda i: (i, 0))],
        out_shape=[jax.ShapeDtypeStruct((s_len, MY_WIDTH), F32), jax.ShapeDtypeStruct((s_len, D_MODEL), BF16)],
        compiler_params=_params(("parallel",)),
    )(x, norm_g, w_my)


def _log_sigmoid(z):
    return jnp.minimum(z, 0.0) - jnp.log(1.0 + jnp.exp(-jnp.abs(z)))


def _forget_lane_maps():
    to_q = np.zeros((128, FOX_HEADS * 128), np.float32)
    to_k = np.zeros((128, FOX_HEADS * 128), np.float32)
    for h in range(FOX_HEADS):
        for term in range(3):
            to_q[8 * term + h, 128 * h + 64 + term] = 1.0
            to_q[24, 128 * h + 67 + term] = 1.0
            to_k[24, 128 * h + 64 + term] = 1.0
            to_k[8 * term + h, 128 * h + 67 + term] = -1.0
    return jnp.asarray(to_q, BF16), jnp.asarray(to_k, BF16)


def _fox_prep(proj, bf_pad, gq, gk, seg):
    s_len = proj.shape[0]
    to_q, to_k = _forget_lane_maps()

    def body(q_ref, k_ref, v_ref, f_ref, bf_ref, gq_ref, gk_ref, seg_ref, eq_ref, ek_ref,
             qa_ref, ka_ref, vh_ref, kt_ref, vt_ref, carry_ref):
        @pl.when(pl.program_id(0) == 0)
        def _():
            carry_ref[...] = jnp.zeros((1, 128), F32)

        lane = lax.broadcasted_iota(jnp.int32, (TM, 128), 1)
        logf = jnp.where(lane < FOX_HEADS, _log_sigmoid(f_ref[...] + bf_ref[...]), 0.0)
        row = lax.broadcasted_iota(jnp.int32, (TM, TM), 0)
        col = lax.broadcasted_iota(jnp.int32, (TM, TM), 1)
        tri = jnp.where(col <= row, 1.0, 0.0).astype(BF16)
        hi, mid, lo = _split3(logf)
        cum = _dot(tri, hi) + _dot(tri, mid) + _dot(tri, lo) + carry_ref[...]
        carry_ref[...] = cum[TM - 1:TM, :]
        f_hi, f_mid, f_lo = [t.astype(F32) for t in _split3(cum)]
        packed = (f_hi + pltpu.roll(f_mid, 8, 1) + pltpu.roll(f_lo, 16, 1) + jnp.where(lane == 24, 1.0, 0.0)).astype(BF16)
        extra_q = _dot(packed, eq_ref[...])
        extra_k = _dot(packed, ek_ref[...])
        one_at_64 = jnp.where(lane == 64, 1.0, 0.0)
        zeros = jnp.zeros((TM, HEAD_DIM), F32)
        q_all = q_ref[...]
        k_all = k_ref[...]
        qn_all = q_all * _head_rms(q_all, seg_ref[...]) * gq_ref[...] * 0.125
        kn_all = k_all * _head_rms(k_all, seg_ref[...]) * gk_ref[...]
        for h in range(FOX_HEADS):
            sl = slice(HEAD_DIM * h, HEAD_DIM * (h + 1))
            tile = slice(128 * h, 128 * (h + 1))
            qa = jnp.where(lane < 64, jnp.concatenate([qn_all[:, sl], zeros], axis=1), extra_q[:, tile])
            ka = jnp.where(lane < 64, jnp.concatenate([kn_all[:, sl], zeros], axis=1), extra_k[:, tile])
            vh = v_ref[:, sl]
            v_aug = jnp.where(lane < 64, jnp.concatenate([vh, zeros], axis=1), one_at_64)
            qa_ref[h] = qa.astype(BF16)
            ka_ref[h] = ka.astype(BF16)
            vh_ref[h] = vh.astype(BF16)
            kt_ref[h, 0] = ka.T.astype(BF16)
            vt_ref[h, 0] = v_aug.T.astype(BF16)

    col_blk = lambda j: pl.BlockSpec((TM, PIECE), lambda i: (i, j))
    head_blk = lambda w: pl.BlockSpec((FOX_HEADS, TM, w), lambda i: (0, i, 0))
    tile_blk = pl.BlockSpec((FOX_HEADS, 1, 128, TM), lambda i: (0, i, 0, 0))
    tiled = jax.ShapeDtypeStruct((FOX_HEADS, s_len // TM, 128, TM), BF16)
    return pl.pallas_call(
        body,
        name="fox_prep",
        grid=(s_len // TM,),
        in_specs=[col_blk(1), col_blk(2), col_blk(3), pl.BlockSpec((TM, 128), lambda i: (i, MY_F_OFF // 128)),
                  _full((1, 128)), _full((1, PIECE)), _full((1, PIECE)), _full((PIECE, PIECE)),
                  _full((128, FOX_HEADS * 128)), _full((128, FOX_HEADS * 128))],
        out_specs=[head_blk(128), head_blk(128), head_blk(HEAD_DIM), tile_blk, tile_blk],
        out_shape=[jax.ShapeDtypeStruct((FOX_HEADS, s_len, 128), BF16), jax.ShapeDtypeStruct((FOX_HEADS, s_len, 128), BF16),
                   jax.ShapeDtypeStruct((FOX_HEADS, s_len, HEAD_DIM), BF16), tiled, tiled],
        scratch_shapes=[pltpu.VMEM((1, 128), F32)],
        compiler_params=_params(("arbitrary",)),
    )(proj, proj, proj, proj, bf_pad, gq, gk, seg, to_q, to_k)


def _mem_prep(mem, g, w_kv, gk):
    def body(mem_ref, g_ref, w_ref, gk_ref, km_ref, vm_ref):
        m = mem_ref[...]
        kv = _dot((m * _rms(m) * g_ref[...]).astype(BF16), w_ref[...])
        for h in range(MEM_HEADS):
            sl = slice(HEAD_DIM * h, HEAD_DIM * (h + 1))
            kh = kv[:, sl]
            km_ref[:, sl] = (kh * _rms(kh) * gk_ref[...]).astype(BF16)
        vm_ref[...] = kv[:, 256:512].astype(BF16)

    return pl.pallas_call(
        body,
        name="mem_prep",
        out_shape=(jax.ShapeDtypeStruct((N_MEM, 256), BF16),) * 2,
        in_specs=[pl.BlockSpec(memory_space=pltpu.VMEM)] * 4,
        out_specs=(pl.BlockSpec(memory_space=pltpu.VMEM),) * 2,
        compiler_params=pltpu.CompilerParams(vmem_limit_bytes=VMEM_LIMIT),
    )(mem, g, w_kv, gk)


def _causal_mask():
    row = lax.broadcasted_iota(jnp.int32, (TA, TA), 0)
    col = lax.broadcasted_iota(jnp.int32, (TA, TA), 1)
    return col <= row


def _causal_mask_t():
    row = lax.broadcasted_iota(jnp.int32, (TA, TA), 0)
    col = lax.broadcasted_iota(jnp.int32, (TA, TA), 1)
    return row <= col


def _fox_fwd(q_aug, k_aug, vt_aug, w_kv16, w_out16):
    s_len = q_aug.shape[1]
    n_tiles = s_len // TA
    hb = HB_FWD
    n_groups = FOX_HEADS // hb

    def body(q_ref, k_ref, vt_ref, kv16_ref, out16_ref, o_ref, st_ref, kv_all, out_all, send_sems, recv_sems, local_sems):
        qi = pl.program_id(1)
        remote, local = _row_block_exchange(kv16_ref, out16_ref, kv_all, out_all, send_sems, recv_sems, local_sems, gather=True)

        @pl.when((pl.program_id(0) == 0) & (qi == 0))
        def _():
            for cp in remote + local:
                cp.start()

        qs = [q_ref[h] for h in range(hb)]

        def step(t0, carry, masked, width):
            rows = pl.ds(pl.multiple_of(t0 * TA, TA), width * TA)
            scores = [_dot_nt(k_ref[h, rows, :], qs[h]) for h in range(hb)]
            soft = []
            for h in range(hb):
                m, _ = carry[h]
                s = jnp.where(_causal_mask_t(), scores[h], -1e30) if masked else scores[h]
                m_new = jnp.maximum(m, jnp.max(s, axis=0, keepdims=True))
                soft.append((m_new, jnp.exp(m - m_new), jnp.exp(s - m_new).astype(BF16)))
            out = []
            for h, (m_new, alpha, p) in enumerate(soft):
                acc = alpha * carry[h][1]
                for t in range(width):
                    acc = acc + _dot(vt_ref[h, t0 + t, 0:AUG_ROWS, :], p[t * TA:(t + 1) * TA])
                out.append((m_new, acc))
            return tuple(out)

        init = tuple((jnp.full((1, TA), -1e30, F32), jnp.zeros((AUG_ROWS, TA), F32)) for _ in range(hb))
        carry = lax.fori_loop(0, qi // 2, lambda j, cr: step(2 * j, cr, False, 2), init)
        carry = lax.fori_loop(2 * (qi // 2), qi, lambda j, cr: step(j, cr, False, 1), carry)
        carry = step(qi, carry, True, 1)
        for h in range(hb):
            m, acc = carry[h]
            l = acc[HEAD_DIM:HEAD_DIM + 1, :]
            lse = m + jnp.log(l)
            o_ref[h] = jnp.concatenate([acc[0:HEAD_DIM] / l, jnp.broadcast_to(lse, (HEAD_DIM, TA))], axis=0).T
            st_ref[h, 0] = jnp.broadcast_to(lse, (8, TA))

        @pl.when((pl.program_id(0) == n_groups - 1) & (qi == n_tiles - 1))
        def _():
            for cp in remote:
                cp.wait_recv()
            for cp in remote:
                cp.wait_send()
            for cp in local:
                cp.wait()

    hbm = pl.BlockSpec(memory_space=pl.ANY)
    return pl.pallas_call(
        body,
        name="fox_fwd",
        grid=(n_groups, n_tiles),
        in_specs=[pl.BlockSpec((hb, TA, 128), lambda g, i: (g, i, 0)), pl.BlockSpec((hb, s_len, 128), lambda g, i: (g, 0, 0)),
                  pl.BlockSpec((hb, n_tiles, 128, TA), lambda g, i: (g, 0, 0, 0)), hbm, hbm],
        out_specs=[pl.BlockSpec((hb, TA, 128), lambda g, i: (g, i, 0)), pl.BlockSpec((hb, 1, 8, TA), lambda g, i: (g, i, 0, 0)),
                   hbm, hbm],
        out_shape=[jax.ShapeDtypeStruct((FOX_HEADS, s_len, 128), F32), jax.ShapeDtypeStruct((FOX_HEADS, n_tiles, 8, TA), F32),
                   jax.ShapeDtypeStruct((D_MODEL, 512), BF16), jax.ShapeDtypeStruct((D_MODEL, D_MODEL), BF16)],
        scratch_shapes=[pltpu.SemaphoreType.DMA((14,)), pltpu.SemaphoreType.DMA((14,)), pltpu.SemaphoreType.DMA((2,))],
        compiler_params=_params(("arbitrary", "arbitrary")),
    )(q_aug, k_aug, vt_aug, w_kv16, w_out16)


def _lane_group(lane, a, b, c, d):
    return jnp.where(lane < 64, a, jnp.where(lane < 128, b, jnp.where(lane < 192, c, d)))


def _pool_count(row0):
    lane = lax.broadcasted_iota(jnp.int32, (TM, POOL_WIDTH), 1)
    t1 = row0 + lax.broadcasted_iota(jnp.int32, (TM, POOL_WIDTH), 0) + 1
    return 1.0 / jnp.minimum(t1, _lane_group(lane, 2, 4, 8, 16)).astype(F32), lane


def _pool_delta(u, halo, cnt, lane):
    xe = jnp.concatenate([halo, u], axis=0)
    s2 = xe + pltpu.roll(xe, 1, 0)
    s4 = s2 + pltpu.roll(s2, 2, 0)
    s8 = s4 + pltpu.roll(s4, 4, 0)
    s16 = s8 + pltpu.roll(s8, 8, 0)
    win = _lane_group(lane, s2[HALO:], s4[HALO:], s8[HALO:], s16[HALO:])
    return win * cnt - u


def _pool_delta_bwd(dd, dd_next, cnt, cnt_next, lane):
    ee = jnp.concatenate([dd * cnt, dd_next * cnt_next], axis=0)
    n = TM + HALO
    r2 = ee + pltpu.roll(ee, n - 1, 0)
    r4 = r2 + pltpu.roll(r2, n - 2, 0)
    r8 = r4 + pltpu.roll(r4, n - 4, 0)
    r16 = r8 + pltpu.roll(r8, n - 8, 0)
    return _lane_group(lane, r2[:TM], r4[:TM], r8[:TM], r16[:TM]) - dd


def _mem_attn(qm, gq, seg, km_ref, vm_ref):
    r = _head_rms(qm, seg)
    a = qm * r
    q16 = (a * gq * 0.125).astype(BF16)
    heads = []
    for h in range(MEM_HEADS):
        sl = slice(HEAD_DIM * h, HEAD_DIM * (h + 1))
        s = _dot_nt(q16[:, sl], km_ref[:, sl])
        e = jnp.exp(s - jnp.max(s, axis=-1, keepdims=True))
        p = e * (1.0 / jnp.sum(e, axis=-1, keepdims=True))
        heads.append((p, _dot(p.astype(BF16), vm_ref[:, sl])))
    return a, r, q16, heads


def _row_specs(s_len):
    n_halo = s_len // HALO
    piece = lambda j: pl.BlockSpec((TM, PIECE), lambda i: (i, j))
    before = lambda j: pl.BlockSpec((HALO, 256), lambda i: (jnp.maximum(i * (TM // HALO) - 1, 0), j))
    after = lambda j: pl.BlockSpec((HALO, 256), lambda i: (jnp.minimum((i + 1) * (TM // HALO), n_halo - 1), j))
    return piece, before, after


def _mix_fwd(proj, olse, km, vm, wp_bd, pool_scale, gq_m, seg):
    s_len = proj.shape[0]

    def body(ua_ref, halo_ref, gb_ref, qm_ref, ol_ref, km_ref, vm_ref, wp_ref, sc_ref, gq_ref, seg_ref, out_ref):
        i = pl.program_id(0)
        u = ua_ref[:, 0:256]
        g_a = ua_ref[:, 256:512]
        halo = jnp.where(i > 0, halo_ref[...], 0.0)
        cnt, lane = _pool_count(i * TM)
        d = _pool_delta(u, halo, cnt, lane).astype(BF16)
        y_a = _dot(d, wp_ref[...]) * sc_ref[...]
        out_ref[:, 0:256] = (y_a * (g_a * _sigmoid(g_a))).astype(BF16)
        g_b = gb_ref[...]
        y_b = jnp.concatenate([ol_ref[h][:, 0:HEAD_DIM] for h in range(FOX_HEADS)], axis=1)
        out_ref[:, 256:768] = (y_b * (g_b * _sigmoid(g_b))).astype(BF16)
        _, _, _, heads = _mem_attn(qm_ref[:, 0:256], gq_ref[...], seg_ref[0:256, 0:256], km_ref, vm_ref)
        g_m = qm_ref[:, 256:512]
        y_m = jnp.concatenate([y for _, y in heads], axis=1)
        out_ref[:, 768:1024] = (y_m * (g_m * _sigmoid(g_m))).astype(BF16)

    piece, before, _ = _row_specs(s_len)
    return pl.pallas_call(
        body,
        name="mix_fwd",
        grid=(s_len // TM,),
        in_specs=[piece(0), before(0), piece(4), piece(5), pl.BlockSpec((FOX_HEADS, TM, 128), lambda i: (0, i, 0)),
                  _full((N_MEM, 256)), _full((N_MEM, 256)), _full((256, 256)), _full((1, 256)), _full((1, 256)),
                  _full((PIECE, PIECE))],
        out_specs=pl.BlockSpec((TM, D_MODEL), lambda i: (i, 0)),
        out_shape=jax.ShapeDtypeStruct((s_len, D_MODEL), BF16),
        compiler_params=_params(("parallel",)),
    )(proj, proj, proj, proj, olse, km, vm, wp_bd, pool_scale, gq_m, seg)


def _out_loss(mixed, x, target, w_out):
    s_len = x.shape[0]

    def body(mix_ref, x_ref, t_ref, w_ref, dout_ref, dmix_ref, dw16_ref, loss_ref, dw_ref):
        @pl.when(pl.program_id(0) == 0)
        def _():
            dw_ref[...] = jnp.zeros((D_MODEL, D_MODEL), F32)
            loss_ref[...] = jnp.zeros((8, 128), F32)

        mixed16 = mix_ref[...]
        err = x_ref[...] + _dot(mixed16, w_ref[...]) - t_ref[...]
        loss_ref[...] += 0.5 * jnp.sum(jnp.mean(err * err, axis=-1, keepdims=True))
        d_out = err / float(D_MODEL)
        dout_ref[...] = d_out
        d16 = d_out.astype(BF16)
        dmix_ref[...] = _dot_nt(d16, w_ref[...])
        dw_ref[...] += _dot_tn(mixed16, d16)

        @pl.when(pl.program_id(0) == pl.num_programs(0) - 1)
        def _():
            dw16_ref[...] = dw_ref[...].astype(BF16)

    row = pl.BlockSpec((TMM, D_MODEL), lambda i: (i, 0))
    return pl.pallas_call(
        body,
        name="out_loss",
        grid=(s_len // TMM,),
        in_specs=[row, row, row, _full((D_MODEL, D_MODEL))],
        out_specs=[row, row, _full((D_MODEL, D_MODEL)), _full((8, 128))],
        out_shape=[jax.ShapeDtypeStruct((s_len, D_MODEL), F32), jax.ShapeDtypeStruct((s_len, D_MODEL), F32),
                   jax.ShapeDtypeStruct((D_MODEL, D_MODEL), BF16), jax.ShapeDtypeStruct((8, 128), F32)],
        scratch_shapes=[pltpu.VMEM((D_MODEL, D_MODEL), F32)],
        compiler_params=_params(("arbitrary",)),
    )(mixed, x, target, w_out)


def _silu_pair(g):
    s = _sigmoid(g)
    return g * s, s * (1.0 + g * (1.0 - s))


def _mix_bwd(proj, olse, d_mixed, km, vm, wp_bd, pool_scale, gq_m, seg):
    s_len = proj.shape[0]
    n_tiles = s_len // TM

    def body(ua_ref, halo_ref, ga_next_ref, gb_ref, qm_ref, ol_ref, dm_ref, dm_next_ref, km_ref, vm_ref, wp_ref, sc_ref, gq_ref,
             seg_ref, dua_ref, dgb_ref, dqm_ref, do_ref, dl_ref, dwp_ref, dsc_ref, dkm_ref, dvm_ref, dgq_ref):
        i = pl.program_id(0)

        @pl.when(i == 0)
        def _():
            dwp_ref[...] = jnp.zeros((256, 256), F32)
            dsc_ref[...] = jnp.zeros((1, 256), F32)
            dkm_ref[...] = jnp.zeros((N_MEM, 256), F32)
            dvm_ref[...] = jnp.zeros((N_MEM, 256), F32)
            dgq_ref[...] = jnp.zeros((1, HEAD_DIM), F32)

        u = ua_ref[:, 0:256]
        g_a = ua_ref[:, 256:512]
        halo = jnp.where(i > 0, halo_ref[...], 0.0)
        cnt, lane = _pool_count(i * TM)
        d16 = _pool_delta(u, halo, cnt, lane).astype(BF16)
        t = _dot(d16, wp_ref[...])
        y_a = t * sc_ref[...]
        silu_a, dsilu_a = _silu_pair(g_a)
        dm_a = dm_ref[:, 0:256]
        dy_a = dm_a * silu_a
        dsc_ref[...] += jnp.sum(dy_a * t, axis=0, keepdims=True)
        dt16 = (dy_a * sc_ref[...]).astype(BF16)
        dwp_ref[...] += _dot_tn(d16, dt16)
        dd = _dot_nt(dt16, wp_ref[...])
        g_next = ga_next_ref[...]
        dt_next = (dm_next_ref[...] * (g_next * _sigmoid(g_next)) * sc_ref[...]).astype(BF16)
        dd_next = jnp.where(i < n_tiles - 1, _dot_nt(dt_next, wp_ref[...]), 0.0)
        cnt_next = _pool_count((i + 1) * TM)[0][0:HALO]
        dua_ref[:, 0:256] = _pool_delta_bwd(dd, dd_next, cnt, cnt_next, lane).astype(BF16)
        dua_ref[:, 256:512] = (dm_a * y_a * dsilu_a).astype(BF16)

        silu_b, dsilu_b = _silu_pair(gb_ref[...])
        dm_b = dm_ref[:, 256:768]
        o_all = jnp.concatenate([ol_ref[h][:, 0:HEAD_DIM] for h in range(FOX_HEADS)], axis=1)
        d_o = dm_b * silu_b
        dgb_ref[...] = (dm_b * o_all * dsilu_b).astype(BF16)
        for h in range(FOX_HEADS):
            do_ref[h] = d_o[:, HEAD_DIM * h:HEAD_DIM * (h + 1)].astype(BF16)
        prod = d_o * o_all
        hi = prod.astype(BF16)
        lo = (prod - hi.astype(F32)).astype(BF16)
        head_of_lane = lax.broadcasted_iota(jnp.int32, (FOX_HEADS, PIECE), 1) // HEAD_DIM
        pick = jnp.where(head_of_lane == lax.broadcasted_iota(jnp.int32, (FOX_HEADS, PIECE), 0), 1.0, 0.0).astype(BF16)
        delta = _dot_nt(pick, hi) + _dot_nt(pick, lo)
        for h in range(FOX_HEADS):
            dl_ref[h, 0] = jnp.broadcast_to(delta[h:h + 1, :], (8, TM))

        seg = seg_ref[0:256, 0:256]
        a, r, q16, heads = _mem_attn(qm_ref[:, 0:256], gq_ref[...], seg, km_ref, vm_ref)
        silu_m, dsilu_m = _silu_pair(qm_ref[:, 256:512])
        dm_m = dm_ref[:, 768:1024]
        y_m = jnp.concatenate([y for _, y in heads], axis=1)
        dqm_ref[:, 256:512] = (dm_m * y_m * dsilu_m).astype(BF16)
        dy16_all = (dm_m * silu_m).astype(BF16)
        d_scores = []
        for h in range(MEM_HEADS):
            sl = slice(HEAD_DIM * h, HEAD_DIM * (h + 1))
            p = heads[h][0]
            dy16 = dy16_all[:, sl]
            dp = _dot_nt(dy16, vm_ref[:, sl])
            dvm_ref[:, sl] += _dot_tn(p.astype(BF16), dy16)
            ds16 = (p * (dp - jnp.sum(dp * p, axis=-1, keepdims=True))).astype(BF16)
            dkm_ref[:, sl] += _dot_tn(ds16, q16[:, sl])
            d_scores.append(_dot(ds16, km_ref[:, sl]))
        dq, dg_rows = _head_rms_bwd(a, r, gq_ref[...], jnp.concatenate(d_scores, axis=1) * 0.125, seg)
        dqm_ref[:, 0:256] = dq.astype(BF16)
        dgq_ref[...] += _fold_heads(jnp.sum(dg_rows, axis=0, keepdims=True), MEM_HEADS)

    piece, before, after = _row_specs(s_len)
    n_halo = s_len // HALO
    row = pl.BlockSpec((TM, D_MODEL), lambda i: (i, 0))
    dm_after = pl.BlockSpec((HALO, 256), lambda i: (jnp.minimum((i + 1) * (TM // HALO), n_halo - 1), 0))
    out_piece = pl.BlockSpec((TM, PIECE), lambda i: (i, 0))
    return pl.pallas_call(
        body,
        name="mix_bwd",
        grid=(n_tiles,),
        in_specs=[piece(0), before(0), after(1), piece(4), piece(5), pl.BlockSpec((FOX_HEADS, TM, 128), lambda i: (0, i, 0)),
                  row, dm_after, _full((N_MEM, 256)), _full((N_MEM, 256)), _full((256, 256)), _full((1, 256)), _full((1, 256)),
                  _full((PIECE, PIECE))],
        out_specs=[out_piece, out_piece, out_piece, pl.BlockSpec((FOX_HEADS, TM, HEAD_DIM), lambda i: (0, i, 0)),
                   pl.BlockSpec((FOX_HEADS, 1, 8, TM), lambda i: (0, i, 0, 0)),
                   _full((256, 256)), _full((1, 256)), _full((N_MEM, 256)), _full((N_MEM, 256)), _full((1, HEAD_DIM))],
        out_shape=[jax.ShapeDtypeStruct((s_len, PIECE), BF16)] * 3 + [
            jax.ShapeDtypeStruct((FOX_HEADS, s_len, HEAD_DIM), BF16),
            jax.ShapeDtypeStruct((FOX_HEADS, n_tiles, 8, TM), F32),
            jax.ShapeDtypeStruct((256, 256), F32), jax.ShapeDtypeStruct((1, 256), F32),
            jax.ShapeDtypeStruct((N_MEM, 256), F32), jax.ShapeDtypeStruct((N_MEM, 256), F32),
            jax.ShapeDtypeStruct((1, HEAD_DIM), F32)],
        compiler_params=_params(("arbitrary",)),
    )(proj, proj, proj, proj, proj, olse, d_mixed, d_mixed, km, vm, wp_bd, pool_scale, gq_m, seg)


def _fox_bwd(q_aug, k_aug, kt_aug, v_h, d_o, lse_rows, delta_rows, dw_kv16, dw_out16):
    s_len = q_aug.shape[1]
    n_tiles = s_len // TA
    n_groups = FOX_HEADS // HB

    def body(q_ref, k_ref, kt_ref, v_ref, do_ref, st_ref, dl_ref, dkv16_ref, dout16_ref, dqt_ref, dk_ref, dv_ref, land_kv, land_out,
             send_sems, recv_sems, local_sems):
        j = pl.program_id(1)
        remote, local = _row_block_exchange(dkv16_ref, dout16_ref, land_kv, land_out, send_sems, recv_sems, local_sems, gather=False)

        @pl.when((pl.program_id(0) == 0) & (j == 0))
        def _():
            for cp in remote + local:
                cp.start()

        @pl.when(j == 0)
        def _():
            dqt_ref[...] = jnp.zeros((HB, n_tiles, 128, TA), F32)

        ks = [k_ref[h] for h in range(HB)]
        kts = [kt_ref[h, 0, 0:AUG_ROWS, :] for h in range(HB)]
        vs = [v_ref[h] for h in range(HB)]

        def pair(i0, acc, masked, width):
            rows = pl.ds(pl.multiple_of(i0 * TA, TA), width * TA)
            qs = [q_ref[h, rows, :] for h in range(HB)]
            d_os = [do_ref[h, rows, :] for h in range(HB)]
            row_stat = lambda ref, h: jnp.concatenate([ref[h, i0 + t, 0:1, :] for t in range(width)], axis=1)
            scores = [(_dot_nt(ks[h], qs[h]), _dot_nt(vs[h], d_os[h])) for h in range(HB)]
            probs = []
            for h in range(HB):
                s, dp = scores[h]
                if masked:
                    s = jnp.where(_causal_mask_t(), s, -1e30)
                p = jnp.exp(s - row_stat(st_ref, h))
                probs.append((p.astype(BF16), (p * (dp - row_stat(dl_ref, h))).astype(BF16)))
            out = []
            for h in range(HB):
                p16, ds16 = probs[h]
                dqt = _dot(kts[h], ds16)
                for t in range(width):
                    dqt_ref[h, i0 + t, 0:AUG_ROWS, :] += dqt[:, t * TA:(t + 1) * TA]
                out.append((acc[h][0] + _dot(ds16, qs[h]), acc[h][1] + _dot(p16, d_os[h])))
            return tuple(out)

        zero = tuple((jnp.zeros((TA, 128), F32), jnp.zeros((TA, HEAD_DIM), F32)) for _ in range(HB))
        acc = pair(j, zero, True, 1)
        odd = (n_tiles - 1 - j) % 2
        acc = lax.fori_loop(j + 1, j + 1 + odd, lambda i, a: pair(i, a, False, 1), acc)
        acc = lax.fori_loop(0, (n_tiles - 1 - j) // 2, lambda t, a: pair(j + 1 + odd + 2 * t, a, False, 2), acc)
        for h in range(HB):
            dk_ref[h] = acc[h][0]
            dv_ref[h] = acc[h][1]

        @pl.when((pl.program_id(0) == n_groups - 1) & (j == n_tiles - 1))
        def _():
            for cp in remote:
                cp.wait_recv()
            for cp in remote:
                cp.wait_send()
            for cp in local:
                cp.wait()

    whole = lambda w: pl.BlockSpec((HB, s_len, w), lambda g, j: (g, 0, 0))
    tile = lambda w: pl.BlockSpec((HB, TA, w), lambda g, j: (g, j, 0))
    rows_blk = lambda r: pl.BlockSpec((HB, n_tiles, r, TA), lambda g, j: (g, 0, 0, 0))
    hbm = pl.BlockSpec(memory_space=pl.ANY)
    return pl.pallas_call(
        body,
        name="fox_bwd",
        grid=(n_groups, n_tiles),
        in_specs=[whole(128), tile(128), pl.BlockSpec((HB, 1, 128, TA), lambda g, j: (g, j, 0, 0)), tile(HEAD_DIM),
                  whole(HEAD_DIM), rows_blk(8), rows_blk(8), hbm, hbm],
        out_specs=[rows_blk(128), tile(128), tile(HEAD_DIM), hbm, hbm],
        out_shape=[jax.ShapeDtypeStruct((FOX_HEADS, n_tiles, 128, TA), F32), jax.ShapeDtypeStruct((FOX_HEADS, s_len, 128), F32),
                   jax.ShapeDtypeStruct((FOX_HEADS, s_len, HEAD_DIM), F32),
                   jax.ShapeDtypeStruct((N_DEV, 128, 512), BF16), jax.ShapeDtypeStruct((N_DEV, 128, D_MODEL), BF16)],
        scratch_shapes=[pltpu.SemaphoreType.DMA((14,)), pltpu.SemaphoreType.DMA((14,)), pltpu.SemaphoreType.DMA((2,))],
        compiler_params=_params(("arbitrary", "arbitrary")),
    )(q_aug, k_aug, kt_aug, v_h, d_o, lse_rows, delta_rows, dw_kv16, dw_out16)


def _fox_post(proj, bf_pad, gq, gk, seg, dq_aug, dk_aug, dv_h):
    s_len = proj.shape[0]
    n_tiles = s_len // TM

    def body(q_ref, k_ref, f_ref, bf_ref, gq_ref, gk_ref, seg_ref, dqa_ref, dka_ref, dvh_ref,
             dq_ref, dk_ref, dv_ref, df_ref, dgq_ref, dgk_ref, dbf_ref, carry_ref):
        @pl.when(pl.program_id(0) == 0)
        def _():
            carry_ref[...] = jnp.zeros((1, 128), F32)
            dgq_ref[...] = jnp.zeros((1, HEAD_DIM), F32)
            dgk_ref[...] = jnp.zeros((1, HEAD_DIM), F32)
            dbf_ref[...] = jnp.zeros((1, 128), F32)

        lane = lax.broadcasted_iota(jnp.int32, (TM, 128), 1)
        seg = seg_ref[...]
        dqas = [dqa_ref[h, 0].T for h in range(FOX_HEADS)]
        dkas = [dka_ref[h] for h in range(FOX_HEADS)]
        d_cum = jnp.zeros((TM, 128), F32)
        for h in range(FOX_HEADS):
            d_cum = jnp.where(lane == h, dqas[h][:, 64:65] - dkas[h][:, 67:68], d_cum)
        q_all = q_ref[...]
        k_all = k_ref[...]
        rq = _head_rms(q_all, seg)
        rk = _head_rms(k_all, seg)
        dy_q = jnp.concatenate([t[:, 0:HEAD_DIM] for t in dqas], axis=1) * 0.125
        dy_k = jnp.concatenate([t[:, 0:HEAD_DIM] for t in dkas], axis=1)
        dq, dgq_rows = _head_rms_bwd(q_all * rq, rq, gq_ref[...], dy_q, seg)
        dk, dgk_rows = _head_rms_bwd(k_all * rk, rk, gk_ref[...], dy_k, seg)
        dq_ref[...] = dq.astype(BF16)
        dk_ref[...] = dk.astype(BF16)
        dv_ref[...] = jnp.concatenate([dvh_ref[h] for h in range(FOX_HEADS)], axis=1).astype(BF16)
        dgq_ref[...] += _fold_heads(jnp.sum(dgq_rows, axis=0, keepdims=True), FOX_HEADS)
        dgk_ref[...] += _fold_heads(jnp.sum(dgk_rows, axis=0, keepdims=True), FOX_HEADS)

        row = lax.broadcasted_iota(jnp.int32, (TM, TM), 0)
        col = lax.broadcasted_iota(jnp.int32, (TM, TM), 1)
        tri = jnp.where(col >= row, 1.0, 0.0).astype(BF16)
        hi, mid, lo = _split3(d_cum)
        d_logf = _dot(tri, hi) + _dot(tri, mid) + _dot(tri, lo) + carry_ref[...]
        carry_ref[...] = d_logf[0:1, :]
        z = f_ref[...] + bf_ref[...]
        d_f = jnp.where(lane < FOX_HEADS, d_logf / (1.0 + jnp.exp(z)), 0.0)
        df_ref[...] = d_f.astype(BF16)
        dbf_ref[...] += jnp.sum(d_f, axis=0, keepdims=True)

    rev = lambda i: n_tiles - 1 - i
    col_blk = lambda j: pl.BlockSpec((TM, PIECE), lambda i: (rev(i), j))
    head_blk = lambda w: pl.BlockSpec((FOX_HEADS, TM, w), lambda i: (0, rev(i), 0))
    out_piece = pl.BlockSpec((TM, PIECE), lambda i: (rev(i), 0))
    return pl.pallas_call(
        body,
        name="fox_post",
        grid=(n_tiles,),
        in_specs=[col_blk(1), col_blk(2), pl.BlockSpec((TM, 128), lambda i: (rev(i), MY_F_OFF // 128)),
                  _full((1, 128)), _full((1, PIECE)), _full((1, PIECE)), _full((PIECE, PIECE)),
                  pl.BlockSpec((FOX_HEADS, 1, 128, TM), lambda i: (0, rev(i), 0, 0)), head_blk(128), head_blk(HEAD_DIM)],
        out_specs=[out_piece, out_piece, out_piece, pl.BlockSpec((TM, 128), lambda i: (rev(i), 0)),
                   _full((1, HEAD_DIM)), _full((1, HEAD_DIM)), _full((1, 128))],
        out_shape=[jax.ShapeDtypeStruct((s_len, PIECE), BF16)] * 3 + [
            jax.ShapeDtypeStruct((s_len, 128), BF16), jax.ShapeDtypeStruct((1, HEAD_DIM), F32),
            jax.ShapeDtypeStruct((1, HEAD_DIM), F32), jax.ShapeDtypeStruct((1, 128), F32)],
        scratch_shapes=[pltpu.VMEM((1, 128), F32)],
        compiler_params=_params(("arbitrary",)),
    )(proj, proj, proj, bf_pad, gq, gk, seg, dq_aug, dk_aug, dv_h)


def _mem_bwd(mem, g, w_kv, gk, dkm, dvm):
    def body(mem_ref, g_ref, w_ref, gk_ref, dkm_ref, dvm_ref, dw_ref, dg_ref, dgk_ref, dkv_ref):
        m = mem_ref[...]
        r = _rms(m)
        a = m * r
        mn16 = (a * g_ref[...]).astype(BF16)
        kv = _dot(mn16, w_ref[...])
        dgk = jnp.zeros((N_MEM, HEAD_DIM), F32)
        for h in range(MEM_HEADS):
            sl = slice(HEAD_DIM * h, HEAD_DIM * (h + 1))
            kh = kv[:, sl]
            rk = _rms(kh)
            dk, dg_rows = _rms_bwd(kh * rk, rk, gk_ref[...], dkm_ref[:, sl])
            dkv_ref[:, sl] = dk.astype(BF16)
            dgk = dgk + dg_rows
        dkv_ref[:, 256:512] = dvm_ref[...].astype(BF16)
        dgk_ref[...] = jnp.sum(dgk, axis=0, keepdims=True)
        dkv16 = dkv_ref[...]
        dw_ref[...] = _dot_tn(mn16, dkv16).astype(BF16)
        dg_ref[...] = jnp.sum(_dot_nt(dkv16, w_ref[...]) * a, axis=0, keepdims=True)

    return pl.pallas_call(
        body,
        name="mem_bwd",
        out_shape=(jax.ShapeDtypeStruct((D_MODEL, 512), BF16), jax.ShapeDtypeStruct((1, D_MODEL), F32),
                   jax.ShapeDtypeStruct((1, HEAD_DIM), F32)),
        in_specs=[pl.BlockSpec(memory_space=pltpu.VMEM)] * 6,
        out_specs=(pl.BlockSpec(memory_space=pltpu.VMEM),) * 3,
        scratch_shapes=[pltpu.VMEM((N_MEM, 512), BF16)],
        compiler_params=pltpu.CompilerParams(vmem_limit_bytes=VMEM_LIMIT),
    )(mem, g, w_kv, gk, dkm, dvm)


def _grad_x_exchange(pieces, d_f, w_my, x, norm_g, d_out, dw_in, land_kv, land_out, d_mem_g, d_scale, d_bf, d_gq, d_gk, d_gqm,
                     d_gkm, dwp_bd, loss_blk):
    s_len = x.shape[0]
    n_steps = s_len // TM
    step2, step3 = n_steps // 4, (3 * n_steps) // 4
    half = D_MODEL // 2

    def body(p0, p1, p2, p3, p4, p5, df_ref, x_ref, dout_ref, w_ref, g_ref, in_ref, lkv_ref, lout_ref,
             dmg, dsc, dbf, dgq, dgk, dgqm, dgkm, dwp, loss_ref,
             gx_ref, rin_ref, rkv_ref, rout_ref, sred_ref,
             dng, land1, send2a, send2b, keep2a, keep2b, land2a, land2b, send3a, send3b, land3a, land3b, sb_ref, sland,
             send_sems, recv_sems):
        i = pl.program_id(0)
        x_, y_, c_, me = _my_place()
        sibling, x_nbr, y_nbr = (x_, y_, 1 - c_), (1 - x_, y_, c_), (x_, 1 - y_, c_)

        def rcopy(src, dst, sem, to):
            return pltpu.make_async_remote_copy(src_ref=src, dst_ref=dst, send_sem=send_sems.at[sem], recv_sem=recv_sems.at[sem],
                                                device_id=to, device_id_type=MESH)

        small = []
        for k in range(1, N_DEV):
            px, py, pc = x_ ^ (k >> 2), y_ ^ ((k >> 1) & 1), c_ ^ (k & 1)
            small.append(rcopy(sb_ref, sland.at[me], k - 1, (px, py, pc)))
        stage1 = [rcopy(in_ref.at[2 * k + 1 - c_], land1.at[k], 7 + k, sibling) for k in range(4)]
        stage2 = [rcopy(send2a.at[j], land2a.at[j], 11 + j, x_nbr) for j in range(2)]
        stage2 += [rcopy(send2b.at[j], land2b.at[j], 13 + j, y_nbr) for j in range(2)]
        stage3 = [rcopy(send3a, land3a, 15, y_nbr), rcopy(send3b, land3b, 16, x_nbr)]
        top, bot = slice(0, half), slice(half, D_MODEL)

        @pl.when(i == 0)
        def _():
            dng[...] = jnp.zeros((1, D_MODEL), F32)
            for cp in stage1:
                cp.start()

        @pl.when(i == step2)
        def _():
            for cp in stage1:
                cp.wait_recv()

            def chip_sum(k, rows):
                return in_ref[2 * k + c_, rows, :].astype(F32) + land1[k, rows, :].astype(F32)

            for j in range(2):
                send2a[j] = chip_sum(2 * (1 - x_) + j, top).astype(BF16)
                keep2a[j] = chip_sum(2 * x_ + j, top)
                send2b[j] = chip_sum(2 * j + 1 - y_, bot).astype(BF16)
                keep2b[j] = chip_sum(2 * j + y_, bot)
            for cp in stage2:
                cp.start()

        @pl.when(i == step3)
        def _():
            for cp in stage2:
                cp.wait_recv()
            for j in range(2):
                keep2a[j] = keep2a[j] + land2a[j].astype(F32)
                keep2b[j] = keep2b[j] + land2b[j].astype(F32)
            send3a[...] = keep2a[1 - y_].astype(BF16)
            send3b[...] = keep2b[1 - x_].astype(BF16)
            for cp in stage3:
                cp.start()

        dh = _dot_nt(df_ref[...], w_ref[:, MY_F_OFF:MY_WIDTH])
        for j, p in enumerate((p0, p1, p2, p3, p4, p5)):
            dh = dh + _dot_nt(p[...], w_ref[:, PIECE * j:PIECE * (j + 1)])
        xv = x_ref[...]
        r = _rms(xv)
        dx, dg_rows = _rms_bwd(xv * r, r, g_ref[...], dh)
        gx_ref[...] = dout_ref[...] + dx
        dng[...] += jnp.sum(dg_rows, axis=0, keepdims=True)

        @pl.when(i == n_steps - 1)
        def _():
            sb_ref[...] = jnp.zeros((SB_ROWS, SB_W), F32)
            for k in range(4):
                sb_ref[SB_NORM_G + k:SB_NORM_G + k + 1, :] = dng[:, SB_W * k:SB_W * (k + 1)]
                sb_ref[SB_MEM_NORM_G + k:SB_MEM_NORM_G + k + 1, :] = dmg[:, SB_W * k:SB_W * (k + 1)]
            sb_ref[SB_POOL_SCALE:SB_POOL_SCALE + 1, :] = dsc[...]
            sb_ref[SB_B_F:SB_B_F + 1, 0:128] = dbf[...]
            for row, ref in ((SB_FOX_Q, dgq), (SB_FOX_K, dgk), (SB_MEM_Q, dgqm), (SB_MEM_K, dgkm)):
                sb_ref[row:row + 1, 0:HEAD_DIM] = ref[...]
            sb_ref[SB_LOSS:SB_LOSS + 1, 0:128] = loss_ref[0:1, :]
            for grp in range(4):
                sl = slice(64 * grp, 64 * (grp + 1))
                sb_ref[SB_W_POOL:SB_W_POOL + 64, sl] = dwp[sl, sl]
            for cp in small:
                cp.start()
            sland[me] = sb_ref[...]
            for cp in stage3:
                cp.wait_recv()
            rin_ref[top, :] = keep2a[y_] + land3a[...].astype(F32)
            rin_ref[bot, :] = keep2b[x_] + land3b[...].astype(F32)
            for cp in small:
                cp.wait_recv()
            for cp in small + stage1 + stage2 + stage3:
                cp.wait_send()
            for land, red in ((lkv_ref, rkv_ref), (lout_ref, rout_ref), (sland, sred_ref)):
                acc = land[0].astype(F32)
                for d in range(1, N_DEV):
                    acc = acc + land[d].astype(F32)
                red[...] = acc

    piece = pl.BlockSpec((TM, PIECE), lambda i: (i, 0))
    row = pl.BlockSpec((TM, D_MODEL), lambda i: (i, 0))
    vmem = pl.BlockSpec(memory_space=pltpu.VMEM)
    half_chunk = lambda n, dt: pltpu.VMEM((n, half, IN_CHUNK), dt)
    whole = (w_my, norm_g, dw_in, land_kv, land_out, d_mem_g, d_scale, d_bf, d_gq, d_gk, d_gqm, d_gkm, dwp_bd, loss_blk)
    return pl.pallas_call(
        body,
        name="grad_x_exchange",
        grid=(n_steps,),
        in_specs=[piece] * 6 + [pl.BlockSpec((TM, 128), lambda i: (i, 0)), row, row] + [vmem] * len(whole),
        out_specs=[row, vmem, vmem, vmem, vmem],
        out_shape=[jax.ShapeDtypeStruct((s_len, D_MODEL), F32), jax.ShapeDtypeStruct((D_MODEL, IN_CHUNK), F32),
                   jax.ShapeDtypeStruct((128, 512), F32), jax.ShapeDtypeStruct((128, D_MODEL), F32),
                   jax.ShapeDtypeStruct((SB_ROWS, SB_W), F32)],
        scratch_shapes=[
            pltpu.VMEM((1, D_MODEL), F32),
            pltpu.VMEM((4, D_MODEL, IN_CHUNK), BF16),
            half_chunk(2, BF16), half_chunk(2, BF16), half_chunk(2, F32), half_chunk(2, F32), half_chunk(2, BF16), half_chunk(2, BF16),
            pltpu.VMEM((half, IN_CHUNK), BF16), pltpu.VMEM((half, IN_CHUNK), BF16),
            pltpu.VMEM((half, IN_CHUNK), BF16), pltpu.VMEM((half, IN_CHUNK), BF16),
            pltpu.VMEM((SB_ROWS, SB_W), F32),
            pltpu.VMEM((N_DEV, SB_ROWS, SB_W), F32),
            pltpu.SemaphoreType.DMA((17,)),
            pltpu.SemaphoreType.DMA((17,)),
        ],
        compiler_params=_params(("arbitrary",)),
    )(*pieces, d_f, x, d_out, *whole)


def _grad_w_in(h16, pieces, d_f):
    s_len = h16.shape[0]
    tk = 512

    def body(h_ref, p0, p1, p2, p3, p4, p5, df_ref, out_ref, dw_ref):
        @pl.when(pl.program_id(0) == 0)
        def _():
            dw_ref[...] = jnp.zeros((D_MODEL, MY_WIDTH), F32)

        h = h_ref[...]
        for j, p in enumerate((p0, p1, p2, p3, p4, p5)):
            dw_ref[:, PIECE * j:PIECE * (j + 1)] += _dot_tn(h, p[...])
        dw_ref[:, MY_F_OFF:MY_WIDTH] += _dot_tn(h, df_ref[...])

        @pl.when(pl.program_id(0) == pl.num_programs(0) - 1)
        def _():
            for d in range(N_DEV):
                parts = [dw_ref[:, start:start + width] for start, width in _chunk_segments(d)]
                out_ref[d] = (parts[0] if len(parts) == 1 else jnp.concatenate(parts, axis=1)).astype(BF16)

    piece = pl.BlockSpec((tk, PIECE), lambda i: (i, 0))
    return pl.pallas_call(
        body,
        name="grad_w_in",
        grid=(s_len // tk,),
        in_specs=[pl.BlockSpec((tk, D_MODEL), lambda i: (i, 0))] + [piece] * 6 + [pl.BlockSpec((tk, 128), lambda i: (i, 0))],
        out_specs=_full((N_DEV, D_MODEL, IN_CHUNK)),
        out_shape=jax.ShapeDtypeStruct((N_DEV, D_MODEL, IN_CHUNK), BF16),
        scratch_shapes=[pltpu.VMEM((D_MODEL, MY_WIDTH), F32)],
        compiler_params=_params(("arbitrary",)),
    )(h16, *pieces, d_f)


def _adamw(w, g, m, v):
    m = ADAM_B1 * m + (1.0 - ADAM_B1) * g
    v = ADAM_B2 * v + (1.0 - ADAM_B2) * (g * g)
    m_hat = m / (1.0 - ADAM_B1 ** ADAM_STEP)
    v_hat = v / (1.0 - ADAM_B2 ** ADAM_STEP)
    return -ADAM_LR * (m_hat / (jnp.sqrt(v_hat) + ADAM_EPS) + ADAM_WD * w), m, v


PARAM_ORDER = ("norm_g", "w_in", "b_f", "w_pool", "pool_scale", "fox_q_g", "fox_k_g", "mem_norm_g", "w_mem_kv", "mem_q_g", "mem_k_g", "w_out")


def _update(red_in, red_kv, red_out, sred, params):
    def body(rin_ref, rkv_ref, rout_ref, sred_ref, *refs):
        ins, outs = refs[:3 * len(PARAM_ORDER)], refs[3 * len(PARAM_ORDER):]
        wide = lambda row: jnp.concatenate([sred_ref[row + k:row + k + 1, :] for k in range(4)], axis=1)
        head = lambda row: sred_ref[row:row + 1, 0:HEAD_DIM]
        grads = {
            "norm_g": lambda: wide(SB_NORM_G), "w_in": lambda: rin_ref[...], "b_f": lambda: sred_ref[SB_B_F:SB_B_F + 1, 0:FOX_HEADS],
            "pool_scale": lambda: sred_ref[SB_POOL_SCALE:SB_POOL_SCALE + 1, :], "fox_q_g": lambda: head(SB_FOX_Q),
            "fox_k_g": lambda: head(SB_FOX_K), "mem_norm_g": lambda: wide(SB_MEM_NORM_G), "w_mem_kv": lambda: rkv_ref[...],
            "mem_q_g": lambda: head(SB_MEM_Q), "mem_k_g": lambda: head(SB_MEM_K), "w_out": lambda: rout_ref[...],
        }
        for p, name in enumerate(PARAM_ORDER):
            w_ref, m_ref, v_ref = ins[3 * p:3 * p + 3]
            g_out, d_out, m_out, v_out = outs[4 * p:4 * p + 4]
            if name == "w_pool":
                for grp in range(4):
                    g = sred_ref[SB_W_POOL:SB_W_POOL + 64, 64 * grp:64 * (grp + 1)]
                    delta, m_new, v_new = _adamw(w_ref[grp], g, m_ref[grp], v_ref[grp])
                    g_out[grp], d_out[grp], m_out[grp], v_out[grp] = g, delta, m_new, v_new
            else:
                g = grads[name]()
                delta, m_new, v_new = _adamw(w_ref[...], g, m_ref[...], v_ref[...])
                g_out[...], d_out[...], m_out[...], v_out[...] = g, delta, m_new, v_new

    flat = [t for name in PARAM_ORDER for t in params[name]]
    shapes = [params[name][0].shape for name in PARAM_ORDER for _ in range(4)]
    outs = pl.pallas_call(
        body,
        name="adamw_update",
        out_shape=tuple(jax.ShapeDtypeStruct(s, F32) for s in shapes),
        in_specs=[pl.BlockSpec(memory_space=pltpu.VMEM)] * (4 + len(flat)),
        out_specs=(pl.BlockSpec(memory_space=pltpu.VMEM),) * len(shapes),
        compiler_params=pltpu.CompilerParams(vmem_limit_bytes=VMEM_LIMIT),
    )(red_in, red_kv, red_out, sred, *flat)
    return {name: outs[4 * p:4 * p + 4] for p, name in enumerate(PARAM_ORDER)}


def kernel(x, mem, norm_g, w_in, b_f, w_pool, pool_scale, fox_q_g, fox_k_g, mem_norm_g, w_mem_kv, mem_q_g, mem_k_g, w_out, loss_target, m_norm_g, m_w_in, m_b_f, m_w_pool, m_pool_scale, m_fox_q_g, m_fox_k_g, m_mem_norm_g, m_w_mem_kv, m_mem_q_g, m_mem_k_g, m_w_out, v_norm_g, v_w_in, v_b_f, v_w_pool, v_pool_scale, v_fox_q_g, v_fox_k_g, v_mem_norm_g, v_w_mem_kv, v_mem_q_g, v_mem_k_g, v_w_out):
    given = dict(norm_g=(norm_g, m_norm_g, v_norm_g), w_in=(w_in, m_w_in, v_w_in), b_f=(b_f, m_b_f, v_b_f),
                 w_pool=(w_pool, m_w_pool, v_w_pool), pool_scale=(pool_scale, m_pool_scale, v_pool_scale),
                 fox_q_g=(fox_q_g, m_fox_q_g, v_fox_q_g), fox_k_g=(fox_k_g, m_fox_k_g, v_fox_k_g),
                 mem_norm_g=(mem_norm_g, m_mem_norm_g, v_mem_norm_g), w_mem_kv=(w_mem_kv, m_w_mem_kv, v_w_mem_kv),
                 mem_q_g=(mem_q_g, m_mem_q_g, v_mem_q_g), mem_k_g=(mem_k_g, m_mem_k_g, v_mem_k_g), w_out=(w_out, m_w_out, v_w_out))
    params = {name: tuple(t[0] if t.ndim > 2 else t for t in wmv) for name, wmv in given.items()}
    x2, mem2, target2 = x[0], mem[0], loss_target[0]

    seg = jnp.asarray(np.kron(np.eye(FOX_HEADS), np.full((HEAD_DIM, HEAD_DIM), 1.0 / HEAD_DIM)), BF16)
    w_my, w_kv16, w_out16, wp_bd, bf_pad, gq8, gk8, gqm4 = _gather_w_in(
        params["w_in"][0], params["w_mem_kv"][0], params["w_out"][0], params["w_pool"][0], b_f, fox_q_g, fox_k_g, mem_q_g)
    proj, h16 = _fwd_proj(x2, norm_g, w_my)
    q_aug, k_aug, v_h, kt_aug, vt_aug = _fox_prep(proj, bf_pad, gq8, gk8, seg)
    olse, lse_rows, w_kv_all, w_out_all = _fox_fwd(q_aug, k_aug, vt_aug, w_kv16, w_out16)
    km, vm = _mem_prep(mem2, mem_norm_g, w_kv_all, mem_k_g)
    mixed = _mix_fwd(proj, olse, km, vm, wp_bd, pool_scale, gqm4, seg)
    d_out, d_mixed, dw_out, loss_blk = _out_loss(mixed, x2, target2, w_out_all)

    d_ua, d_gb, d_qm, d_o, delta_rows, dwp_bd, d_scale, dkm, dvm, d_gqm = _mix_bwd(proj, olse, d_mixed, km, vm, wp_bd, pool_scale,
                                                                                    gqm4, seg)
    dw_kv, d_mem_g, d_gkm = _mem_bwd(mem2, mem_norm_g, w_kv_all, mem_k_g, dkm, dvm)
    dq_aug, dk_aug, dv_h, land_kv, land_out = _fox_bwd(q_aug, k_aug, kt_aug, v_h, d_o, lse_rows, delta_rows, dw_kv, dw_out)
    d_q, d_k, d_v, d_f, d_gq, d_gk, d_bf = _fox_post(proj, bf_pad, gq8, gk8, seg, dq_aug, dk_aug, dv_h)
    pieces = (d_ua, d_q, d_k, d_v, d_gb, d_qm)
    dw_in = _grad_w_in(h16, pieces, d_f)
    grad_x, red_in, red_kv, red_out, sred = _grad_x_exchange(pieces, d_f, w_my, x2, norm_g, d_out, dw_in, land_kv, land_out, d_mem_g,
                                                             d_scale, d_bf, d_gq, d_gk, d_gqm, d_gkm, dwp_bd, loss_blk)
    new = _update(red_in, red_kv, red_out, sred, params)
    result = [sred[SB_LOSS, 0], grad_x[None]]
    for kind in range(4):
        for name in PARAM_ORDER:
            out = new[name][kind]
            result.append(out[None] if given[name][0].ndim > 2 else out)
    return tuple(result)
```

```python
import functools

import jax
import jax.numpy as jnp
import numpy as np
from jax import lax
from jax.experimental import pallas as pl
from jax.experimental.pallas import tpu as pltpu

F32 = jnp.float32
BF16 = jnp.bfloat16
MESH = pl.DeviceIdType.MESH

N_DEV = 8
D_MODEL = 1024
HEAD_DIM = 64
FOX_HEADS = 8
MEM_HEADS = 4
N_MEM = 256
POOL_WIDTH = 256
EPS = 1e-6
IN_WIDTH = 3080
IN_CHUNK = IN_WIDTH // N_DEV
F_OFF = 2048
MY_WIDTH = 3200
MY_F_OFF = 3072
PIECE = 512
TM = 256
TMM = 512
TA = 256
HB = 8
HB_FWD = 8
AUG_ROWS = 72
HALO = 16
VMEM_LIMIT = 56 * 1024 * 1024

ADAM_LR = 0.001
ADAM_B1 = 0.9
ADAM_B2 = 0.999
ADAM_EPS = 1e-08
ADAM_WD = 0.01
ADAM_STEP = 10


def _dot(a, b):
    return jnp.dot(a, b, preferred_element_type=F32)


def _dot_nt(a, b):
    return lax.dot_general(a, b, (((1,), (1,)), ((), ())), preferred_element_type=F32)


def _dot_tn(a, b):
    return lax.dot_general(a, b, (((0,), (0,)), ((), ())), preferred_element_type=F32)


def _sigmoid(g):
    return 0.5 + 0.5 * jnp.tanh(0.5 * g)


def _split3(v):
    hi = v.astype(BF16)
    r1 = v - hi.astype(F32)
    mid = r1.astype(BF16)
    lo = (r1 - mid.astype(F32)).astype(BF16)
    return hi, mid, lo


def _rms(v):
    return lax.rsqrt(jnp.mean(v * v, axis=-1, keepdims=True) + EPS)


def _rms_bwd(a, r, g, dy):
    da = dy * g
    return r * (da - a * jnp.mean(da * a, axis=-1, keepdims=True)), dy * a


def _head_mean(z, seg):
    hi = z.astype(BF16)
    lo = (z - hi.astype(F32)).astype(BF16)
    if z.shape[1] == 256:
        return _dot(hi, seg) + _dot(lo, seg)
    half = seg[0:256, 0:256]
    return jnp.concatenate([_dot(hi[:, 0:256], half) + _dot(lo[:, 0:256], half),
                            _dot(hi[:, 256:512], half) + _dot(lo[:, 256:512], half)], axis=1)


def _head_rms(v, seg):
    return lax.rsqrt(_head_mean(v * v, seg) + EPS)


def _head_rms_bwd(a, r, g, dy, seg):
    da = dy * g
    return r * (da - a * _head_mean(da * a, seg)), dy * a


def _fold_heads(row, n_heads):
    out = row[:, 0:HEAD_DIM]
    for h in range(1, n_heads):
        out = out + row[:, HEAD_DIM * h:HEAD_DIM * (h + 1)]
    return out


def _params(sem, limit=VMEM_LIMIT):
    return pltpu.CompilerParams(dimension_semantics=sem, vmem_limit_bytes=limit)


def _full(shape):
    return pl.BlockSpec(shape, lambda *_: (0,) * len(shape))


def _chunk_segments(d):
    runs = []
    for lo, hi, shift in ((0, F_OFF, 0), (F_OFF, F_OFF + FOX_HEADS, MY_F_OFF - F_OFF), (F_OFF + FOX_HEADS, IN_WIDTH, -FOX_HEADS)):
        a, b = max(lo, IN_CHUNK * d), min(hi, IN_CHUNK * (d + 1))
        if a < b:
            runs.append((a + shift, b - a))
    return runs


def _my_place():
    x, y, c = lax.axis_index("x"), lax.axis_index("y"), lax.axis_index("c")
    return x, y, c, 4 * x + 2 * y + c


def _shard_rows(dev):
    return pl.ds(pl.multiple_of(128 * dev, 128), 128)


def _gather_w_in(w_in, w_kv, w_out, w_pool, b_f, gq, gk, gqm):
    def body(win_ref, wkv_ref, wout_ref, wp_ref, bf_ref, gq_ref, gk_ref, gqm_ref, wmy_ref, kv16_ref, out16_ref, wpbd_ref, bfpad_ref,
             gq8_ref, gk8_ref, gqm4_ref, in_all, pay_in, send_sems, recv_sems):
        x, y, c, me = _my_place()
        here, sibling = (x, y, c), (x, y, 1 - c)
        x_chip, y_chip, far_chip = (1 - x, y), (x, 1 - y), (1 - x, 1 - y)
        relay_from = (x ^ (1 - c), y ^ c)
        relay_to = (x ^ c, y ^ (1 - c))

        pay_in[...] = win_ref[...].astype(BF16)

        def copy(k, block, to, own=False):
            px, py, pc = block
            dst = in_all.at[4 * px + 2 * py + pc]
            return pltpu.make_async_remote_copy(src_ref=pay_in if own else dst, dst_ref=dst, send_sem=send_sems.at[k],
                                                recv_sem=recv_sems.at[k], device_id=to, device_id_type=MESH)

        first = [copy(0, here, sibling, own=True), copy(1, here, (*x_chip, c), own=True), copy(2, here, (*y_chip, c), own=True)]
        for cp in first:
            cp.start()
        in_all[me] = pay_in[...]
        kv16_ref[...] = wkv_ref[...].astype(BF16)
        out16_ref[...] = wout_ref[...].astype(BF16)
        wpbd_ref[...] = jnp.zeros((POOL_WIDTH, POOL_WIDTH), BF16)
        for grp in range(4):
            sl = slice(64 * grp, 64 * (grp + 1))
            wpbd_ref[sl, sl] = wp_ref[grp].astype(BF16)
        bfpad_ref[...] = jnp.zeros((1, 128), F32)
        bfpad_ref[:, 0:FOX_HEADS] = bf_ref[...]
        gq8_ref[...] = jnp.concatenate([gq_ref[...]] * FOX_HEADS, axis=1)
        gk8_ref[...] = jnp.concatenate([gk_ref[...]] * FOX_HEADS, axis=1)
        gqm4_ref[...] = jnp.concatenate([gqm_ref[...]] * MEM_HEADS, axis=1)
        copy(1 + c, (*relay_from, c), here).wait_recv()
        passed = [copy(3, (*relay_from, c), (*relay_to, c)), copy(4, (*relay_from, c), sibling)]
        for cp in passed:
            cp.start()
        copy(2 - c, (*relay_to, c), here).wait_recv()
        passed.append(copy(5, (*relay_to, c), sibling))
        passed[-1].start()
        copy(3, (*far_chip, c), here).wait_recv()
        passed.append(copy(6, (*far_chip, c), sibling))
        passed[-1].start()
        copy(0, sibling, here).wait_recv()
        for k, chip in ((4, relay_to), (5, relay_from), (6, far_chip)):
            copy(k, (*chip, 1 - c), here).wait_recv()
        for cp in first + passed:
            cp.wait_send()

        for r0 in range(0, D_MODEL, 256):
            ch = [in_all[d, r0:r0 + 256, :].astype(F32) for d in range(N_DEV)]
            lo = F_OFF - 5 * IN_CHUNK
            full = jnp.concatenate(ch[:5] + [ch[5][:, :lo], ch[5][:, lo + FOX_HEADS:], ch[6], ch[7], ch[5][:, lo:lo + FOX_HEADS],
                                             jnp.zeros((256, MY_WIDTH - IN_WIDTH), F32)], axis=1)
            wmy_ref[r0:r0 + 256, :] = full.astype(BF16)

    return pl.pallas_call(
        body,
        name="gather_w_in",
        out_shape=(jax.ShapeDtypeStruct((D_MODEL, MY_WIDTH), BF16), jax.ShapeDtypeStruct((128, 512), BF16),
                   jax.ShapeDtypeStruct((128, D_MODEL), BF16), jax.ShapeDtypeStruct((POOL_WIDTH, POOL_WIDTH), BF16),
                   jax.ShapeDtypeStruct((1, 128), F32), jax.ShapeDtypeStruct((1, PIECE), F32), jax.ShapeDtypeStruct((1, PIECE), F32),
                   jax.ShapeDtypeStruct((1, 256), F32)),
        in_specs=[pl.BlockSpec(memory_space=pltpu.VMEM)] * 8,
        out_specs=(pl.BlockSpec(memory_space=pltpu.VMEM),) * 8,
        scratch_shapes=[
            pltpu.VMEM((N_DEV, D_MODEL, IN_CHUNK), BF16),
            pltpu.VMEM((D_MODEL, IN_CHUNK), BF16),
            pltpu.SemaphoreType.DMA((7,)),
            pltpu.SemaphoreType.DMA((7,)),
        ],
        compiler_params=pltpu.CompilerParams(vmem_limit_bytes=VMEM_LIMIT),
    )(w_in, w_kv, w_out, w_pool, b_f, gq, gk, gqm)


def _row_block_exchange(kv_ref, out_ref, kv_dst, out_dst, send_sems, recv_sems, local_sems, gather):
    x, y, c, me = _my_place()
    remote = []
    for k in range(1, N_DEV):
        px, py, pc = x ^ (k >> 2), y ^ ((k >> 1) & 1), c ^ (k & 1)
        peer = 4 * px + 2 * py + pc
        for fam, (src, dst) in enumerate(((kv_ref, kv_dst), (out_ref, out_dst))):
            remote.append(pltpu.make_async_remote_copy(
                src_ref=src if gather else src.at[_shard_rows(peer)], dst_ref=dst.at[_shard_rows(me)] if gather else dst.at[me],
                send_sem=send_sems.at[7 * fam + k - 1], recv_sem=recv_sems.at[7 * fam + k - 1],
                device_id=(px, py, pc), device_id_type=MESH))
    local = [pltpu.make_async_copy(src if gather else src.at[_shard_rows(me)], dst.at[_shard_rows(me)] if gather else dst.at[me],
                                   local_sems.at[fam]) for fam, (src, dst) in enumerate(((kv_ref, kv_dst), (out_ref, out_dst)))]
    return remote, local


SB_ROWS, SB_W = 80, 256
SB_NORM_G, SB_MEM_NORM_G, SB_POOL_SCALE, SB_B_F, SB_FOX_Q, SB_FOX_K, SB_MEM_Q, SB_MEM_K, SB_LOSS, SB_W_POOL = 0, 4, 8, 9, 10, 11, 12, 13, 14, 16


def _fwd_proj(x, norm_g, w_my):
    s_len = x.shape[0]

    def body(x_ref, g_ref, w_ref, proj_ref, h_ref):
        xv = x_ref[...]
        h = (xv * _rms(xv) * g_ref[...]).astype(BF16)
        h_ref[...] = h
        proj_ref[...] = _dot(h, w_ref[...])

    return pl.pallas_call(
        body,
        name="fwd_proj",
        grid=(s_len // TMM,),
        in_specs=[pl.BlockSpec((TMM, D_MODEL), lambda i: (i, 0)), _full((1, D_MODEL)), _full((D_MODEL, MY_WIDTH))],
        out_specs=[pl.BlockSpec((TMM, MY_WIDTH), lambda i: (i, 0)), pl.BlockSpec((TMM, D_MODEL), lambda i: (i, 0))],
        out_shape=[jax.ShapeDtypeStruct((s_len, MY_WIDTH), F32), jax.ShapeDtypeStruct((s_len, D_MODEL), BF16)],
        compiler_params=_params(("parallel",)),
    )(x, norm_g, w_my)


def _log_sigmoid(z):
    return jnp.minimum(z, 0.0) - jnp.log(1.0 + jnp.exp(-jnp.abs(z)))


def _forget_lane_maps():
    to_q = np.zeros((128, FOX_HEADS * 128), np.float32)
    to_k = np.zeros((128, FOX_HEADS * 128), np.float32)
    for h in range(FOX_HEADS):
        for term in range(3):
            to_q[8 * term + h, 128 * h + 64 + term] = 1.0
            to_q[24, 128 * h + 67 + term] = 1.0
            to_k[24, 128 * h + 64 + term] = 1.0
            to_k[8 * term + h, 128 * h + 67 + term] = -1.0
    return jnp.asarray(to_q, BF16), jnp.asarray(to_k, BF16)


def _fox_prep(proj, bf_pad, gq, gk, seg):
    s_len = proj.shape[0]
    to_q, to_k = _forget_lane_maps()

    def body(q_ref, k_ref, v_ref, f_ref, bf_ref, gq_ref, gk_ref, seg_ref, eq_ref, ek_ref,
             qa_ref, ka_ref, vh_ref, kt_ref, vt_ref, carry_ref):
        @pl.when(pl.program_id(0) == 0)
        def _():
            carry_ref[...] = jnp.zeros((1, 128), F32)

        lane = lax.broadcasted_iota(jnp.int32, (TM, 128), 1)
        logf = jnp.where(lane < FOX_HEADS, _log_sigmoid(f_ref[...] + bf_ref[...]), 0.0)
        row = lax.broadcasted_iota(jnp.int32, (TM, TM), 0)
        col = lax.broadcasted_iota(jnp.int32, (TM, TM), 1)
        tri = jnp.where(col <= row, 1.0, 0.0).astype(BF16)
        hi, mid, lo = _split3(logf)
        cum = _dot(tri, hi) + _dot(tri, mid) + _dot(tri, lo) + carry_ref[...]
        carry_ref[...] = cum[TM - 1:TM, :]
        f_hi, f_mid, f_lo = [t.astype(F32) for t in _split3(cum)]
        packed = (f_hi + pltpu.roll(f_mid, 8, 1) + pltpu.roll(f_lo, 16, 1) + jnp.where(lane == 24, 1.0, 0.0)).astype(BF16)
        extra_q = _dot(packed, eq_ref[...])
        extra_k = _dot(packed, ek_ref[...])
        one_at_64 = jnp.where(lane == 64, 1.0, 0.0)
        zeros = jnp.zeros((TM, HEAD_DIM), F32)
        q_all = q_ref[...]
        k_all = k_ref[...]
        qn_all = q_all * _head_rms(q_all, seg_ref[...]) * gq_ref[...] * 0.125
        kn_all = k_all * _head_rms(k_all, seg_ref[...]) * gk_ref[...]
        for h in range(FOX_HEADS):
            sl = slice(HEAD_DIM * h, HEAD_DIM * (h + 1))
            tile = slice(128 * h, 128 * (h + 1))
            qa = jnp.where(lane < 64, jnp.concatenate([qn_all[:, sl], zeros], axis=1), extra_q[:, tile])
            ka = jnp.where(lane < 64, jnp.concatenate([kn_all[:, sl], zeros], axis=1), extra_k[:, tile])
            vh = v_ref[:, sl]
            v_aug = jnp.where(lane < 64, jnp.concatenate([vh, zeros], axis=1), one_at_64)
            qa_ref[h] = qa.astype(BF16)
            ka_ref[h] = ka.astype(BF16)
            vh_ref[h] = vh.astype(BF16)
            kt_ref[h, 0] = ka.T.astype(BF16)
            vt_ref[h, 0] = v_aug.T.astype(BF16)

    col_blk = lambda j: pl.BlockSpec((TM, PIECE), lambda i: (i, j))
    head_blk = lambda w: pl.BlockSpec((FOX_HEADS, TM, w), lambda i: (0, i, 0))
    tile_blk = pl.BlockSpec((FOX_HEADS, 1, 128, TM), lambda i: (0, i, 0, 0))
    tiled = jax.ShapeDtypeStruct((FOX_HEADS, s_len // TM, 128, TM), BF16)
    return pl.pallas_call(
        body,
        name="fox_prep",
        grid=(s_len // TM,),
        in_specs=[col_blk(1), col_blk(2), col_blk(3), pl.BlockSpec((TM, 128), lambda i: (i, MY_F_OFF // 128)),
                  _full((1, 128)), _full((1, PIECE)), _full((1, PIECE)), _full((PIECE, PIECE)),
                  _full((128, FOX_HEADS * 128)), _full((128, FOX_HEADS * 128))],
        out_specs=[head_blk(128), head_blk(128), head_blk(HEAD_DIM), tile_blk, tile_blk],
        out_shape=[jax.ShapeDtypeStruct((FOX_HEADS, s_len, 128), BF16), jax.ShapeDtypeStruct((FOX_HEADS, s_len, 128), BF16),
                   jax.ShapeDtypeStruct((FOX_HEADS, s_len, HEAD_DIM), BF16), tiled, tiled],
        scratch_shapes=[pltpu.VMEM((1, 128), F32)],
        compiler_params=_params(("arbitrary",)),
    )(proj, proj, proj, proj, bf_pad, gq, gk, seg, to_q, to_k)


def _mem_prep(mem, g, w_kv, gk):
    def body(mem_ref, g_ref, w_ref, gk_ref, km_ref, vm_ref):
        m = mem_ref[...]
        kv = _dot((m * _rms(m) * g_ref[...]).astype(BF16), w_ref[...])
        for h in range(MEM_HEADS):
            sl = slice(HEAD_DIM * h, HEAD_DIM * (h + 1))
            kh = kv[:, sl]
            km_ref[:, sl] = (kh * _rms(kh) * gk_ref[...]).astype(BF16)
        vm_ref[...] = kv[:, 256:512].astype(BF16)

    return pl.pallas_call(
        body,
        name="mem_prep",
        out_shape=(jax.ShapeDtypeStruct((N_MEM, 256), BF16),) * 2,
        in_specs=[pl.BlockSpec(memory_space=pltpu.VMEM)] * 4,
        out_specs=(pl.BlockSpec(memory_space=pltpu.VMEM),) * 2,
        compiler_params=pltpu.CompilerParams(vmem_limit_bytes=VMEM_LIMIT),
    )(mem, g, w_kv, gk)


def _causal_mask():
    row = lax.broadcasted_iota(jnp.int32, (TA, TA), 0)
    col = lax.broadcasted_iota(jnp.int32, (TA, TA), 1)
    return col <= row


def _causal_mask_t():
    row = lax.broadcasted_iota(jnp.int32, (TA, TA), 0)
    col = lax.broadcasted_iota(jnp.int32, (TA, TA), 1)
    return row <= col


def _fox_fwd(q_aug, k_aug, vt_aug, w_kv16, w_out16):
    s_len = q_aug.shape[1]
    n_tiles = s_len // TA
    hb = HB_FWD
    n_groups = FOX_HEADS // hb

    def body(q_ref, k_ref, vt_ref, kv16_ref, out16_ref, o_ref, st_ref, kv_all, out_all, send_sems, recv_sems, local_sems):
        qi = pl.program_id(1)
        remote, local = _row_block_exchange(kv16_ref, out16_ref, kv_all, out_all, send_sems, recv_sems, local_sems, gather=True)

        @pl.when((pl.program_id(0) == 0) & (qi == 0))
        def _():
            for cp in remote + local:
                cp.start()

        qs = [q_ref[h] for h in range(hb)]

        def step(t0, carry, masked, width):
            rows = pl.ds(pl.multiple_of(t0 * TA, TA), width * TA)
            scores = [_dot_nt(k_ref[h, rows, :], qs[h]) for h in range(hb)]
            soft = []
            for h in range(hb):
                m, _ = carry[h]
                s = jnp.where(_causal_mask_t(), scores[h], -1e30) if masked else scores[h]
                m_new = jnp.maximum(m, jnp.max(s, axis=0, keepdims=True))
                soft.append((m_new, jnp.exp(m - m_new), jnp.exp(s - m_new).astype(BF16)))
            out = []
            for h, (m_new, alpha, p) in enumerate(soft):
                acc = alpha * carry[h][1]
                for t in range(width):
                    acc = acc + _dot(vt_ref[h, t0 + t, 0:AUG_ROWS, :], p[t * TA:(t + 1) * TA])
                out.append((m_new, acc))
            return tuple(out)

        init = tuple((jnp.full((1, TA), -1e30, F32), jnp.zeros((AUG_ROWS, TA), F32)) for _ in range(hb))
        carry = lax.fori_loop(0, qi // 2, lambda j, cr: step(2 * j, cr, False, 2), init)
        carry = lax.fori_loop(2 * (qi // 2), qi, lambda j, cr: step(j, cr, False, 1), carry)
        carry = step(qi, carry, True, 1)
        for h in range(hb):
            m, acc = carry[h]
            l = acc[HEAD_DIM:HEAD_DIM + 1, :]
            lse = m + jnp.log(l)
            o_ref[h] = jnp.concatenate([acc[0:HEAD_DIM] / l, jnp.broadcast_to(lse, (HEAD_DIM, TA))], axis=0).T
            st_ref[h, 0] = jnp.broadcast_to(lse, (8, TA))

        @pl.when((pl.program_id(0) == n_groups - 1) & (qi == n_tiles - 1))
        def _():
            for cp in remote:
                cp.wait_recv()
            for cp in remote:
                cp.wait_send()
            for cp in local:
                cp.wait()

    hbm = pl.BlockSpec(memory_space=pl.ANY)
    return pl.pallas_call(
        body,
        name="fox_fwd",
        grid=(n_groups, n_tiles),
        in_specs=[pl.BlockSpec((hb, TA, 128), lambda g, i: (g, i, 0)), pl.BlockSpec((hb, s_len, 128), lambda g, i: (g, 0, 0)),
                  pl.BlockSpec((hb, n_tiles, 128, TA), lambda g, i: (g, 0, 0, 0)), hbm, hbm],
        out_specs=[pl.BlockSpec((hb, TA, 128), lambda g, i: (g, i, 0)), pl.BlockSpec((hb, 1, 8, TA), lambda g, i: (g, i, 0, 0)),
                   hbm, hbm],
        out_shape=[jax.ShapeDtypeStruct((FOX_HEADS, s_len, 128), F32), jax.ShapeDtypeStruct((FOX_HEADS, n_tiles, 8, TA), F32),
                   jax.ShapeDtypeStruct((D_MODEL, 512), BF16), jax.ShapeDtypeStruct((D_MODEL, D_MODEL), BF16)],
        scratch_shapes=[pltpu.SemaphoreType.DMA((14,)), pltpu.SemaphoreType.DMA((14,)), pltpu.SemaphoreType.DMA((2,))],
        compiler_params=_params(("arbitrary", "arbitrary")),
    )(q_aug, k_aug, vt_aug, w_kv16, w_out16)


def _lane_group(lane, a, b, c, d):
    return jnp.where(lane < 64, a, jnp.where(lane < 128, b, jnp.where(lane < 192, c, d)))


def _pool_count(row0):
    lane = lax.broadcasted_iota(jnp.int32, (TM, POOL_WIDTH), 1)
    t1 = row0 + lax.broadcasted_iota(jnp.int32, (TM, POOL_WIDTH), 0) + 1
    return 1.0 / jnp.minimum(t1, _lane_group(lane, 2, 4, 8, 16)).astype(F32), lane


def _pool_delta(u, halo, cnt, lane):
    xe = jnp.concatenate([halo, u], axis=0)
    s2 = xe + pltpu.roll(xe, 1, 0)
    s4 = s2 + pltpu.roll(s2, 2, 0)
    s8 = s4 + pltpu.roll(s4, 4, 0)
    s16 = s8 + pltpu.roll(s8, 8, 0)
    win = _lane_group(lane, s2[HALO:], s4[HALO:], s8[HALO:], s16[HALO:])
    return win * cnt - u


def _pool_delta_bwd(dd, dd_next, cnt, cnt_next, lane):
    ee = jnp.concatenate([dd * cnt, dd_next * cnt_next], axis=0)
    n = TM + HALO
    r2 = ee + pltpu.roll(ee, n - 1, 0)
    r4 = r2 + pltpu.roll(r2, n - 2, 0)
    r8 = r4 + pltpu.roll(r4, n - 4, 0)
    r16 = r8 + pltpu.roll(r8, n - 8, 0)
    return _lane_group(lane, r2[:TM], r4[:TM], r8[:TM], r16[:TM]) - dd


def _mem_attn(qm, gq, seg, km_ref, vm_ref):
    r = _head_rms(qm, seg)
    a = qm * r
    q16 = (a * gq * 0.125).astype(BF16)
    heads = []
    for h in range(MEM_HEADS):
        sl = slice(HEAD_DIM * h, HEAD_DIM * (h + 1))
        s = _dot_nt(q16[:, sl], km_ref[:, sl])
        e = jnp.exp(s - jnp.max(s, axis=-1, keepdims=True))
        p = e * (1.0 / jnp.sum(e, axis=-1, keepdims=True))
        heads.append((p, _dot(p.astype(BF16), vm_ref[:, sl])))
    return a, r, q16, heads


def _row_specs(s_len):
    n_halo = s_len // HALO
    piece = lambda j: pl.BlockSpec((TM, PIECE), lambda i: (i, j))
    before = lambda j: pl.BlockSpec((HALO, 256), lambda i: (jnp.maximum(i * (TM // HALO) - 1, 0), j))
    after = lambda j: pl.BlockSpec((HALO, 256), lambda i: (jnp.minimum((i + 1) * (TM // HALO), n_halo - 1), j))
    return piece, before, after


def _mix_fwd(proj, olse, km, vm, wp_bd, pool_scale, gq_m, seg):
    s_len = proj.shape[0]

    def body(ua_ref, halo_ref, gb_ref, qm_ref, ol_ref, km_ref, vm_ref, wp_ref, sc_ref, gq_ref, seg_ref, out_ref):
        i = pl.program_id(0)
        u = ua_ref[:, 0:256]
        g_a = ua_ref[:, 256:512]
        halo = jnp.where(i > 0, halo_ref[...], 0.0)
        cnt, lane = _pool_count(i * TM)
        d = _pool_delta(u, halo, cnt, lane).astype(BF16)
        y_a = _dot(d, wp_ref[...]) * sc_ref[...]
        out_ref[:, 0:256] = (y_a * (g_a * _sigmoid(g_a))).astype(BF16)
        g_b = gb_ref[...]
        y_b = jnp.concatenate([ol_ref[h][:, 0:HEAD_DIM] for h in range(FOX_HEADS)], axis=1)
        out_ref[:, 256:768] = (y_b * (g_b * _sigmoid(g_b))).astype(BF16)
        _, _, _, heads = _mem_attn(qm_ref[:, 0:256], gq_ref[...], seg_ref[0:256, 0:256], km_ref, vm_ref)
        g_m = qm_ref[:, 256:512]
        y_m = jnp.concatenate([y for _, y in heads], axis=1)
        out_ref[:, 768:1024] = (y_m * (g_m * _sigmoid(g_m))).astype(BF16)

    piece, before, _ = _row_specs(s_len)
    return pl.pallas_call(
        body,
        name="mix_fwd",
        grid=(s_len // TM,),
        in_specs=[piece(0), before(0), piece(4), piece(5), pl.BlockSpec((FOX_HEADS, TM, 128), lambda i: (0, i, 0)),
                  _full((N_MEM, 256)), _full((N_MEM, 256)), _full((256, 256)), _full((1, 256)), _full((1, 256)),
                  _full((PIECE, PIECE))],
        out_specs=pl.BlockSpec((TM, D_MODEL), lambda i: (i, 0)),
        out_shape=jax.ShapeDtypeStruct((s_len, D_MODEL), BF16),
        compiler_params=_params(("parallel",)),
    )(proj, proj, proj, proj, olse, km, vm, wp_bd, pool_scale, gq_m, seg)


def _out_loss(mixed, x, target, w_out):
    s_len = x.shape[0]

    def body(mix_ref, x_ref, t_ref, w_ref, dout_ref, dmix_ref, dw16_ref, loss_ref, dw_ref):
        @pl.when(pl.program_id(0) == 0)
        def _():
            dw_ref[...] = jnp.zeros((D_MODEL, D_MODEL), F32)
            loss_ref[...] = jnp.zeros((8, 128), F32)

        mixed16 = mix_ref[...]
        err = x_ref[...] + _dot(mixed16, w_ref[...]) - t_ref[...]
        loss_ref[...] += 0.5 * jnp.sum(jnp.mean(err * err, axis=-1, keepdims=True))
        d_out = err / float(D_MODEL)
        dout_ref[...] = d_out
        d16 = d_out.astype(BF16)
        dmix_ref[...] = _dot_nt(d16, w_ref[...])
        dw_ref[...] += _dot_tn(mixed16, d16)

        @pl.when(pl.program_id(0) == pl.num_programs(0) - 1)
        def _():
            dw16_ref[...] = dw_ref[...].astype(BF16)

    row = pl.BlockSpec((TMM, D_MODEL), lambda i: (i, 0))
    return pl.pallas_call(
        body,
        name="out_loss",
        grid=(s_len // TMM,),
        in_specs=[row, row, row, _full((D_MODEL, D_MODEL))],
        out_specs=[row, row, _full((D_MODEL, D_MODEL)), _full((8, 128))],
        out_shape=[jax.ShapeDtypeStruct((s_len, D_MODEL), F32), jax.ShapeDtypeStruct((s_len, D_MODEL), F32),
                   jax.ShapeDtypeStruct((D_MODEL, D_MODEL), BF16), jax.ShapeDtypeStruct((8, 128), F32)],
        scratch_shapes=[pltpu.VMEM((D_MODEL, D_MODEL), F32)],
        compiler_params=_params(("arbitrary",)),
    )(mixed, x, target, w_out)


def _silu_pair(g):
    s = _sigmoid(g)
    return g * s, s * (1.0 + g * (1.0 - s))


def _mix_bwd(proj, olse, d_mixed, km, vm, wp_bd, pool_scale, gq_m, seg):
    s_len = proj.shape[0]
    n_tiles = s_len // TM

    def body(ua_ref, halo_ref, ga_next_ref, gb_ref, qm_ref, ol_ref, dm_ref, dm_next_ref, km_ref, vm_ref, wp_ref, sc_ref, gq_ref,
             seg_ref, dua_ref, dgb_ref, dqm_ref, do_ref, dl_ref, dwp_ref, dsc_ref, dkm_ref, dvm_ref, dgq_ref):
        i = pl.program_id(0)

        @pl.when(i == 0)
        def _():
            dwp_ref[...] = jnp.zeros((256, 256), F32)
            dsc_ref[...] = jnp.zeros((1, 256), F32)
            dkm_ref[...] = jnp.zeros((N_MEM, 256), F32)
            dvm_ref[...] = jnp.zeros((N_MEM, 256), F32)
            dgq_ref[...] = jnp.zeros((1, HEAD_DIM), F32)

        u = ua_ref[:, 0:256]
        g_a = ua_ref[:, 256:512]
        halo = jnp.where(i > 0, halo_ref[...], 0.0)
        cnt, lane = _pool_count(i * TM)
        d16 = _pool_delta(u, halo, cnt, lane).astype(BF16)
        t = _dot(d16, wp_ref[...])
        y_a = t * sc_ref[...]
        silu_a, dsilu_a = _silu_pair(g_a)
        dm_a = dm_ref[:, 0:256]
        dy_a = dm_a * silu_a
        dsc_ref[...] += jnp.sum(dy_a * t, axis=0, keepdims=True)
        dt16 = (dy_a * sc_ref[...]).astype(BF16)
        dwp_ref[...] += _dot_tn(d16, dt16)
        dd = _dot_nt(dt16, wp_ref[...])
        g_next = ga_next_ref[...]
        dt_next = (dm_next_ref[...] * (g_next * _sigmoid(g_next)) * sc_ref[...]).astype(BF16)
        dd_next = jnp.where(i < n_tiles - 1, _dot_nt(dt_next, wp_ref[...]), 0.0)
        cnt_next = _pool_count((i + 1) * TM)[0][0:HALO]
        dua_ref[:, 0:256] = _pool_delta_bwd(dd, dd_next, cnt, cnt_next, lane).astype(BF16)
        dua_ref[:, 256:512] = (dm_a * y_a * dsilu_a).astype(BF16)

        silu_b, dsilu_b = _silu_pair(gb_ref[...])
        dm_b = dm_ref[:, 256:768]
        o_all = jnp.concatenate([ol_ref[h][:, 0:HEAD_DIM] for h in range(FOX_HEADS)], axis=1)
        d_o = dm_b * silu_b
        dgb_ref[...] = (dm_b * o_all * dsilu_b).astype(BF16)
        for h in range(FOX_HEADS):
            do_ref[h] = d_o[:, HEAD_DIM * h:HEAD_DIM * (h + 1)].astype(BF16)
        prod = d_o * o_all
        hi = prod.astype(BF16)
        lo = (prod - hi.astype(F32)).astype(BF16)
        head_of_lane = lax.broadcasted_iota(jnp.int32, (FOX_HEADS, PIECE), 1) // HEAD_DIM
        pick = jnp.where(head_of_lane == lax.broadcasted_iota(jnp.int32, (FOX_HEADS, PIECE), 0), 1.0, 0.0).astype(BF16)
        delta = _dot_nt(pick, hi) + _dot_nt(pick, lo)
        for h in range(FOX_HEADS):
            dl_ref[h, 0] = jnp.broadcast_to(delta[h:h + 1, :], (8, TM))

        seg = seg_ref[0:256, 0:256]
        a, r, q16, heads = _mem_attn(qm_ref[:, 0:256], gq_ref[...], seg, km_ref, vm_ref)
        silu_m, dsilu_m = _silu_pair(qm_ref[:, 256:512])
        dm_m = dm_ref[:, 768:1024]
        y_m = jnp.concatenate([y for _, y in heads], axis=1)
        dqm_ref[:, 256:512] = (dm_m * y_m * dsilu_m).astype(BF16)
        dy16_all = (dm_m * silu_m).astype(BF16)
        d_scores = []
        for h in range(MEM_HEADS):
            sl = slice(HEAD_DIM * h, HEAD_DIM * (h + 1))
            p = heads[h][0]
            dy16 = dy16_all[:, sl]
            dp = _dot_nt(dy16, vm_ref[:, sl])
            dvm_ref[:, sl] += _dot_tn(p.astype(BF16), dy16)
            ds16 = (p * (dp - jnp.sum(dp * p, axis=-1, keepdims=True))).astype(BF16)
            dkm_ref[:, sl] += _dot_tn(ds16, q16[:, sl])
            d_scores.append(_dot(ds16, km_ref[:, sl]))
        dq, dg_rows = _head_rms_bwd(a, r, gq_ref[...], jnp.concatenate(d_scores, axis=1) * 0.125, seg)
        dqm_ref[:, 0:256] = dq.astype(BF16)
        dgq_ref[...] += _fold_heads(jnp.sum(dg_rows, axis=0, keepdims=True), MEM_HEADS)

    piece, before, after = _row_specs(s_len)
    n_halo = s_len // HALO
    row = pl.BlockSpec((TM, D_MODEL), lambda i: (i, 0))
    dm_after = pl.BlockSpec((HALO, 256), lambda i: (jnp.minimum((i + 1) * (TM // HALO), n_halo - 1), 0))
    out_piece = pl.BlockSpec((TM, PIECE), lambda i: (i, 0))
    return pl.pallas_call(
        body,
        name="mix_bwd",
        grid=(n_tiles,),
        in_specs=[piece(0), before(0), after(1), piece(4), piece(5), pl.BlockSpec((FOX_HEADS, TM, 128), lambda i: (0, i, 0)),
                  row, dm_after, _full((N_MEM, 256)), _full((N_MEM, 256)), _full((256, 256)), _full((1, 256)), _full((1, 256)),
                  _full((PIECE, PIECE))],
        out_specs=[out_piece, out_piece, out_piece, pl.BlockSpec((FOX_HEADS, TM, HEAD_DIM), lambda i: (0, i, 0)),
                   pl.BlockSpec((FOX_HEADS, 1, 8, TM), lambda i: (0, i, 0, 0)),
                   _full((256, 256)), _full((1, 256)), _full((N_MEM, 256)), _full((N_MEM, 256)), _full((1, HEAD_DIM))],
        out_shape=[jax.ShapeDtypeStruct((s_len, PIECE), BF16)] * 3 + [
            jax.ShapeDtypeStruct((FOX_HEADS, s_len, HEAD_DIM), BF16),
            jax.ShapeDtypeStruct((FOX_HEADS, n_tiles, 8, TM), F32),
            jax.ShapeDtypeStruct((256, 256), F32), jax.ShapeDtypeStruct((1, 256), F32),
            jax.ShapeDtypeStruct((N_MEM, 256), F32), jax.ShapeDtypeStruct((N_MEM, 256), F32),
            jax.ShapeDtypeStruct((1, HEAD_DIM), F32)],
        compiler_params=_params(("arbitrary",)),
    )(proj, proj, proj, proj, proj, olse, d_mixed, d_mixed, km, vm, wp_bd, pool_scale, gq_m, seg)


def _fox_bwd(q_aug, k_aug, kt_aug, v_h, d_o, lse_rows, delta_rows, dw_kv16, dw_out16):
    s_len = q_aug.shape[1]
    n_tiles = s_len // TA
    n_groups = FOX_HEADS // HB

    def body(q_ref, k_ref, kt_ref, v_ref, do_ref, st_ref, dl_ref, dkv16_ref, dout16_ref, dqt_ref, dk_ref, dv_ref, land_kv, land_out,
             send_sems, recv_sems, local_sems):
        j = pl.program_id(1)
        remote, local = _row_block_exchange(dkv16_ref, dout16_ref, land_kv, land_out, send_sems, recv_sems, local_sems, gather=False)

        @pl.when((pl.program_id(0) == 0) & (j == 0))
        def _():
            for cp in remote + local:
                cp.start()

        @pl.when(j == 0)
        def _():
            dqt_ref[...] = jnp.zeros((HB, n_tiles, 128, TA), F32)

        ks = [k_ref[h] for h in range(HB)]
        kts = [kt_ref[h, 0, 0:AUG_ROWS, :] for h in range(HB)]
        vs = [v_ref[h] for h in range(HB)]

        def pair(i0, acc, masked, width):
            rows = pl.ds(pl.multiple_of(i0 * TA, TA), width * TA)
            qs = [q_ref[h, rows, :] for h in range(HB)]
            d_os = [do_ref[h, rows, :] for h in range(HB)]
            row_stat = lambda ref, h: jnp.concatenate([ref[h, i0 + t, 0:1, :] for t in range(width)], axis=1)
            scores = [(_dot_nt(ks[h], qs[h]), _dot_nt(vs[h], d_os[h])) for h in range(HB)]
            probs = []
            for h in range(HB):
                s, dp = scores[h]
                if masked:
                    s = jnp.where(_causal_mask_t(), s, -1e30)
                p = jnp.exp(s - row_stat(st_ref, h))
                probs.append((p.astype(BF16), (p * (dp - row_stat(dl_ref, h))).astype(BF16)))
            out = []
            for h in range(HB):
                p16, ds16 = probs[h]
                dqt = _dot(kts[h], ds16)
                for t in range(width):
                    dqt_ref[h, i0 + t, 0:AUG_ROWS, :] += dqt[:, t * TA:(t + 1) * TA]
                out.append((acc[h][0] + _dot(ds16, qs[h]), acc[h][1] + _dot(p16, d_os[h])))
            return tuple(out)

        zero = tuple((jnp.zeros((TA, 128), F32), jnp.zeros((TA, HEAD_DIM), F32)) for _ in range(HB))
        acc = pair(j, zero, True, 1)
        odd = (n_tiles - 1 - j) % 2
        acc = lax.fori_loop(j + 1, j + 1 + odd, lambda i, a: pair(i, a, False, 1), acc)
        acc = lax.fori_loop(0, (n_tiles - 1 - j) // 2, lambda t, a: pair(j + 1 + odd + 2 * t, a, False, 2), acc)
        for h in range(HB):
            dk_ref[h] = acc[h][0]
            dv_ref[h] = acc[h][1]

        @pl.when((pl.program_id(0) == n_groups - 1) & (j == n_tiles - 1))
        def _():
            for cp in remote:
                cp.wait_recv()
            for cp in remote:
                cp.wait_send()
            for cp in local:
                cp.wait()

    assert n_groups == 1
    resident = pl.BlockSpec(memory_space=pltpu.VMEM)
    whole = lambda w: resident
    rows_blk = lambda r: resident
    tile = lambda w: pl.BlockSpec((HB, TA, w), lambda g, j: (g, j, 0))
    hbm = pl.BlockSpec(memory_space=pl.ANY)
    return pl.pallas_call(
        body,
        name="fox_bwd",
        grid=(n_groups, n_tiles),
        in_specs=[whole(128), tile(128), pl.BlockSpec((HB, 1, 128, TA), lambda g, j: (g, j, 0, 0)), tile(HEAD_DIM),
                  whole(HEAD_DIM), rows_blk(8), rows_blk(8), hbm, hbm],
        out_specs=[rows_blk(128), tile(128), tile(HEAD_DIM), hbm, hbm],
        out_shape=[jax.ShapeDtypeStruct((FOX_HEADS, n_tiles, 128, TA), F32), jax.ShapeDtypeStruct((FOX_HEADS, s_len, 128), F32),
                   jax.ShapeDtypeStruct((FOX_HEADS, s_len, HEAD_DIM), F32),
                   jax.ShapeDtypeStruct((N_DEV, 128, 512), BF16), jax.ShapeDtypeStruct((N_DEV, 128, D_MODEL), BF16)],
        scratch_shapes=[pltpu.SemaphoreType.DMA((14,)), pltpu.SemaphoreType.DMA((14,)), pltpu.SemaphoreType.DMA((2,))],
        compiler_params=_params(("arbitrary", "arbitrary")),
    )(q_aug, k_aug, kt_aug, v_h, d_o, lse_rows, delta_rows, dw_kv16, dw_out16)


def _fox_post(proj, bf_pad, gq, gk, seg, dq_aug, dk_aug, dv_h):
    s_len = proj.shape[0]
    n_tiles = s_len // TM

    def body(q_ref, k_ref, f_ref, bf_ref, gq_ref, gk_ref, seg_ref, dqa_ref, dka_ref, dvh_ref,
             dq_ref, dk_ref, dv_ref, df_ref, dgq_ref, dgk_ref, dbf_ref, carry_ref):
        @pl.when(pl.program_id(0) == 0)
        def _():
            carry_ref[...] = jnp.zeros((1, 128), F32)
            dgq_ref[...] = jnp.zeros((1, HEAD_DIM), F32)
            dgk_ref[...] = jnp.zeros((1, HEAD_DIM), F32)
            dbf_ref[...] = jnp.zeros((1, 128), F32)

        lane = lax.broadcasted_iota(jnp.int32, (TM, 128), 1)
        seg = seg_ref[...]
        dqas = [dqa_ref[h, 0].T for h in range(FOX_HEADS)]
        dkas = [dka_ref[h] for h in range(FOX_HEADS)]
        d_cum = jnp.zeros((TM, 128), F32)
        for h in range(FOX_HEADS):
            d_cum = jnp.where(lane == h, dqas[h][:, 64:65] - dkas[h][:, 67:68], d_cum)
        q_all = q_ref[...]
        k_all = k_ref[...]
        rq = _head_rms(q_all, seg)
        rk = _head_rms(k_all, seg)
        dy_q = jnp.concatenate([t[:, 0:HEAD_DIM] for t in dqas], axis=1) * 0.125
        dy_k = jnp.concatenate([t[:, 0:HEAD_DIM] for t in dkas], axis=1)
        dq, dgq_rows = _head_rms_bwd(q_all * rq, rq, gq_ref[...], dy_q, seg)
        dk, dgk_rows = _head_rms_bwd(k_all * rk, rk, gk_ref[...], dy_k, seg)
        dq_ref[...] = dq.astype(BF16)
        dk_ref[...] = dk.astype(BF16)
        dv_ref[...] = jnp.concatenate([dvh_ref[h] for h in range(FOX_HEADS)], axis=1).astype(BF16)
        dgq_ref[...] += _fold_heads(jnp.sum(dgq_rows, axis=0, keepdims=True), FOX_HEADS)
        dgk_ref[...] += _fold_heads(jnp.sum(dgk_rows, axis=0, keepdims=True), FOX_HEADS)

        row = lax.broadcasted_iota(jnp.int32, (TM, TM), 0)
        col = lax.broadcasted_iota(jnp.int32, (TM, TM), 1)
        tri = jnp.where(col >= row, 1.0, 0.0).astype(BF16)
        hi, mid, lo = _split3(d_cum)
        d_logf = _dot(tri, hi) + _dot(tri, mid) + _dot(tri, lo) + carry_ref[...]
        carry_ref[...] = d_logf[0:1, :]
        z = f_ref[...] + bf_ref[...]
        d_f = jnp.where(lane < FOX_HEADS, d_logf / (1.0 + jnp.exp(z)), 0.0)
        df_ref[...] = d_f.astype(BF16)
        dbf_ref[...] += jnp.sum(d_f, axis=0, keepdims=True)

    rev = lambda i: n_tiles - 1 - i
    col_blk = lambda j: pl.BlockSpec((TM, PIECE), lambda i: (rev(i), j))
    head_blk = lambda w: pl.BlockSpec((FOX_HEADS, TM, w), lambda i: (0, rev(i), 0))
    out_piece = pl.BlockSpec((TM, PIECE), lambda i: (rev(i), 0))
    return pl.pallas_call(
        body,
        name="fox_post",
        grid=(n_tiles,),
        in_specs=[col_blk(1), col_blk(2), pl.BlockSpec((TM, 128), lambda i: (rev(i), MY_F_OFF // 128)),
                  _full((1, 128)), _full((1, PIECE)), _full((1, PIECE)), _full((PIECE, PIECE)),
                  pl.BlockSpec((FOX_HEADS, 1, 128, TM), lambda i: (0, rev(i), 0, 0)), head_blk(128), head_blk(HEAD_DIM)],
        out_specs=[out_piece, out_piece, out_piece, pl.BlockSpec((TM, 128), lambda i: (rev(i), 0)),
                   _full((1, HEAD_DIM)), _full((1, HEAD_DIM)), _full((1, 128))],
        out_shape=[jax.ShapeDtypeStruct((s_len, PIECE), BF16)] * 3 + [
            jax.ShapeDtypeStruct((s_len, 128), BF16), jax.ShapeDtypeStruct((1, HEAD_DIM), F32),
            jax.ShapeDtypeStruct((1, HEAD_DIM), F32), jax.ShapeDtypeStruct((1, 128), F32)],
        scratch_shapes=[pltpu.VMEM((1, 128), F32)],
        compiler_params=_params(("arbitrary",)),
    )(proj, proj, proj, bf_pad, gq, gk, seg, dq_aug, dk_aug, dv_h)


def _mem_bwd(mem, g, w_kv, gk, dkm, dvm):
    def body(mem_ref, g_ref, w_ref, gk_ref, dkm_ref, dvm_ref, dw_ref, dg_ref, dgk_ref, dkv_ref):
        m = mem_ref[...]
        r = _rms(m)
        a = m * r
        mn16 = (a * g_ref[...]).astype(BF16)
        kv = _dot(mn16, w_ref[...])
        dgk = jnp.zeros((N_MEM, HEAD_DIM), F32)
        for h in range(MEM_HEADS):
            sl = slice(HEAD_DIM * h, HEAD_DIM * (h + 1))
            kh = kv[:, sl]
            rk = _rms(kh)
            dk, dg_rows = _rms_bwd(kh * rk, rk, gk_ref[...], dkm_ref[:, sl])
            dkv_ref[:, sl] = dk.astype(BF16)
            dgk = dgk + dg_rows
        dkv_ref[:, 256:512] = dvm_ref[...].astype(BF16)
        dgk_ref[...] = jnp.sum(dgk, axis=0, keepdims=True)
        dkv16 = dkv_ref[...]
        dw_ref[...] = _dot_tn(mn16, dkv16).astype(BF16)
        dg_ref[...] = jnp.sum(_dot_nt(dkv16, w_ref[...]) * a, axis=0, keepdims=True)

    return pl.pallas_call(
        body,
        name="mem_bwd",
        out_shape=(jax.ShapeDtypeStruct((D_MODEL, 512), BF16), jax.ShapeDtypeStruct((1, D_MODEL), F32),
                   jax.ShapeDtypeStruct((1, HEAD_DIM), F32)),
        in_specs=[pl.BlockSpec(memory_space=pltpu.VMEM)] * 6,
        out_specs=(pl.BlockSpec(memory_space=pltpu.VMEM),) * 3,
        scratch_shapes=[pltpu.VMEM((N_MEM, 512), BF16)],
        compiler_params=pltpu.CompilerParams(vmem_limit_bytes=VMEM_LIMIT),
    )(mem, g, w_kv, gk, dkm, dvm)


def _grad_x_exchange(pieces, d_f, w_my, x, norm_g, d_out, dw_in, land_kv, land_out, d_mem_g, d_scale, d_bf, d_gq, d_gk, d_gqm,
                     d_gkm, dwp_bd, loss_blk):
    s_len = x.shape[0]
    n_steps = s_len // TM
    step2, step3 = n_steps // 4, (3 * n_steps) // 4
    half = D_MODEL // 2

    def body(p0, p1, p2, p3, p4, p5, df_ref, x_ref, dout_ref, w_ref, g_ref, in_ref, lkv_ref, lout_ref,
             dmg, dsc, dbf, dgq, dgk, dgqm, dgkm, dwp, loss_ref,
             gx_ref, rin_ref, rkv_ref, rout_ref, sred_ref,
             dng, land1, send2a, send2b, keep2a, keep2b, land2a, land2b, send3a, send3b, land3a, land3b, sb_ref, sland,
             send_sems, recv_sems):
        i = pl.program_id(0)
        x_, y_, c_, me = _my_place()
        sibling, x_nbr, y_nbr = (x_, y_, 1 - c_), (1 - x_, y_, c_), (x_, 1 - y_, c_)

        def rcopy(src, dst, sem, to):
            return pltpu.make_async_remote_copy(src_ref=src, dst_ref=dst, send_sem=send_sems.at[sem], recv_sem=recv_sems.at[sem],
                                                device_id=to, device_id_type=MESH)

        small = []
        for k in range(1, N_DEV):
            px, py, pc = x_ ^ (k >> 2), y_ ^ ((k >> 1) & 1), c_ ^ (k & 1)
            small.append(rcopy(sb_ref, sland.at[me], k - 1, (px, py, pc)))
        stage1 = [rcopy(in_ref.at[2 * k + 1 - c_], land1.at[k], 7 + k, sibling) for k in range(4)]
        stage2 = [rcopy(send2a.at[j], land2a.at[j], 11 + j, x_nbr) for j in range(2)]
        stage2 += [rcopy(send2b.at[j], land2b.at[j], 13 + j, y_nbr) for j in range(2)]
        stage3 = [rcopy(send3a, land3a, 15, y_nbr), rcopy(send3b, land3b, 16, x_nbr)]
        top, bot = slice(0, half), slice(half, D_MODEL)

        @pl.when(i == 0)
        def _():
            dng[...] = jnp.zeros((1, D_MODEL), F32)
            for cp in stage1:
                cp.start()

        @pl.when(i == step2)
        def _():
            for cp in stage1:
                cp.wait_recv()

            def chip_sum(k, rows):
                return in_ref[2 * k + c_, rows, :].astype(F32) + land1[k, rows, :].astype(F32)

            for j in range(2):
                send2a[j] = chip_sum(2 * (1 - x_) + j, top).astype(BF16)
                keep2a[j] = chip_sum(2 * x_ + j, top)
                send2b[j] = chip_sum(2 * j + 1 - y_, bot).astype(BF16)
                keep2b[j] = chip_sum(2 * j + y_, bot)
            for cp in stage2:
                cp.start()

        @pl.when(i == step3)
        def _():
            for cp in stage2:
                cp.wait_recv()
            for j in range(2):
                keep2a[j] = keep2a[j] + land2a[j].astype(F32)
                keep2b[j] = keep2b[j] + land2b[j].astype(F32)
            send3a[...] = keep2a[1 - y_].astype(BF16)
            send3b[...] = keep2b[1 - x_].astype(BF16)
            for cp in stage3:
                cp.start()

        dh = _dot_nt(df_ref[...], w_ref[:, MY_F_OFF:MY_WIDTH])
        for j, p in enumerate((p0, p1, p2, p3, p4, p5)):
            dh = dh + _dot_nt(p[...], w_ref[:, PIECE * j:PIECE * (j + 1)])
        xv = x_ref[...]
        r = _rms(xv)
        dx, dg_rows = _rms_bwd(xv * r, r, g_ref[...], dh)
        gx_ref[...] = dout_ref[...] + dx
        dng[...] += jnp.sum(dg_rows, axis=0, keepdims=True)

        @pl.when(i == n_steps - 1)
        def _():
            sb_ref[...] = jnp.zeros((SB_ROWS, SB_W), F32)
            for k in range(4):
                sb_ref[SB_NORM_G + k:SB_NORM_G + k + 1, :] = dng[:, SB_W * k:SB_W * (k + 1)]
                sb_ref[SB_MEM_NORM_G + k:SB_MEM_NORM_G + k + 1, :] = dmg[:, SB_W * k:SB_W * (k + 1)]
            sb_ref[SB_POOL_SCALE:SB_POOL_SCALE + 1, :] = dsc[...]
            sb_ref[SB_B_F:SB_B_F + 1, 0:128] = dbf[...]
            for row, ref in ((SB_FOX_Q, dgq), (SB_FOX_K, dgk), (SB_MEM_Q, dgqm), (SB_MEM_K, dgkm)):
                sb_ref[row:row + 1, 0:HEAD_DIM] = ref[...]
            sb_ref[SB_LOSS:SB_LOSS + 1, 0:128] = loss_ref[0:1, :]
            for grp in range(4):
                sl = slice(64 * grp, 64 * (grp + 1))
                sb_ref[SB_W_POOL:SB_W_POOL + 64, sl] = dwp[sl, sl]
            for cp in small:
                cp.start()
            sland[me] = sb_ref[...]
            for cp in stage3:
                cp.wait_recv()
            rin_ref[top, :] = keep2a[y_] + land3a[...].astype(F32)
            rin_ref[bot, :] = keep2b[x_] + land3b[...].astype(F32)
            for cp in small:
                cp.wait_recv()
            for cp in small + stage1 + stage2 + stage3:
                cp.wait_send()
            for land, red in ((lkv_ref, rkv_ref), (lout_ref, rout_ref), (sland, sred_ref)):
                acc = land[0].astype(F32)
                for d in range(1, N_DEV):
                    acc = acc + land[d].astype(F32)
                red[...] = acc

    piece = pl.BlockSpec((TM, PIECE), lambda i: (i, 0))
    row = pl.BlockSpec((TM, D_MODEL), lambda i: (i, 0))
    vmem = pl.BlockSpec(memory_space=pltpu.VMEM)
    half_chunk = lambda n, dt: pltpu.VMEM((n, half, IN_CHUNK), dt)
    whole = (w_my, norm_g, dw_in, land_kv, land_out, d_mem_g, d_scale, d_bf, d_gq, d_gk, d_gqm, d_gkm, dwp_bd, loss_blk)
    return pl.pallas_call(
        body,
        name="grad_x_exchange",
        grid=(n_steps,),
        in_specs=[piece] * 6 + [pl.BlockSpec((TM, 128), lambda i: (i, 0)), row, row] + [vmem] * len(whole),
        out_specs=[row, vmem, vmem, vmem, vmem],
        out_shape=[jax.ShapeDtypeStruct((s_len, D_MODEL), F32), jax.ShapeDtypeStruct((D_MODEL, IN_CHUNK), F32),
                   jax.ShapeDtypeStruct((128, 512), F32), jax.ShapeDtypeStruct((128, D_MODEL), F32),
                   jax.ShapeDtypeStruct((SB_ROWS, SB_W), F32)],
        scratch_shapes=[
            pltpu.VMEM((1, D_MODEL), F32),
            pltpu.VMEM((4, D_MODEL, IN_CHUNK), BF16),
            half_chunk(2, BF16), half_chunk(2, BF16), half_chunk(2, F32), half_chunk(2, F32), half_chunk(2, BF16), half_chunk(2, BF16),
            pltpu.VMEM((half, IN_CHUNK), BF16), pltpu.VMEM((half, IN_CHUNK), BF16),
            pltpu.VMEM((half, IN_CHUNK), BF16), pltpu.VMEM((half, IN_CHUNK), BF16),
            pltpu.VMEM((SB_ROWS, SB_W), F32),
            pltpu.VMEM((N_DEV, SB_ROWS, SB_W), F32),
            pltpu.SemaphoreType.DMA((17,)),
            pltpu.SemaphoreType.DMA((17,)),
        ],
        compiler_params=_params(("arbitrary",)),
    )(*pieces, d_f, x, d_out, *whole)


def _grad_w_in(h16, pieces, d_f):
    s_len = h16.shape[0]
    tk = 512

    def body(h_ref, p0, p1, p2, p3, p4, p5, df_ref, out_ref, dw_ref):
        @pl.when(pl.program_id(0) == 0)
        def _():
            dw_ref[...] = jnp.zeros((D_MODEL, MY_WIDTH), F32)

        h = h_ref[...]
        for j, p in enumerate((p0, p1, p2, p3, p4, p5)):
            dw_ref[:, PIECE * j:PIECE * (j + 1)] += _dot_tn(h, p[...])
        dw_ref[:, MY_F_OFF:MY_WIDTH] += _dot_tn(h, df_ref[...])

        @pl.when(pl.program_id(0) == pl.num_programs(0) - 1)
        def _():
            for d in range(N_DEV):
                parts = [dw_ref[:, start:start + width] for start, width in _chunk_segments(d)]
                out_ref[d] = (parts[0] if len(parts) == 1 else jnp.concatenate(parts, axis=1)).astype(BF16)

    piece = pl.BlockSpec((tk, PIECE), lambda i: (i, 0))
    return pl.pallas_call(
        body,
        name="grad_w_in",
        grid=(s_len // tk,),
        in_specs=[pl.BlockSpec((tk, D_MODEL), lambda i: (i, 0))] + [piece] * 6 + [pl.BlockSpec((tk, 128), lambda i: (i, 0))],
        out_specs=_full((N_DEV, D_MODEL, IN_CHUNK)),
        out_shape=jax.ShapeDtypeStruct((N_DEV, D_MODEL, IN_CHUNK), BF16),
        scratch_shapes=[pltpu.VMEM((D_MODEL, MY_WIDTH), F32)],
        compiler_params=_params(("arbitrary",)),
    )(h16, *pieces, d_f)


def _adamw(w, g, m, v):
    m = ADAM_B1 * m + (1.0 - ADAM_B1) * g
    v = ADAM_B2 * v + (1.0 - ADAM_B2) * (g * g)
    m_hat = m / (1.0 - ADAM_B1 ** ADAM_STEP)
    v_hat = v / (1.0 - ADAM_B2 ** ADAM_STEP)
    return -ADAM_LR * (m_hat / (jnp.sqrt(v_hat) + ADAM_EPS) + ADAM_WD * w), m, v


PARAM_ORDER = ("norm_g", "w_in", "b_f", "w_pool", "pool_scale", "fox_q_g", "fox_k_g", "mem_norm_g", "w_mem_kv", "mem_q_g", "mem_k_g", "w_out")


def _update(red_in, red_kv, red_out, sred, params):
    def body(rin_ref, rkv_ref, rout_ref, sred_ref, *refs):
        ins, outs = refs[:3 * len(PARAM_ORDER)], refs[3 * len(PARAM_ORDER):]
        wide = lambda row: jnp.concatenate([sred_ref[row + k:row + k + 1, :] for k in range(4)], axis=1)
        head = lambda row: sred_ref[row:row + 1, 0:HEAD_DIM]
        grads = {
            "norm_g": lambda: wide(SB_NORM_G), "w_in": lambda: rin_ref[...], "b_f": lambda: sred_ref[SB_B_F:SB_B_F + 1, 0:FOX_HEADS],
            "pool_scale": lambda: sred_ref[SB_POOL_SCALE:SB_POOL_SCALE + 1, :], "fox_q_g": lambda: head(SB_FOX_Q),
            "fox_k_g": lambda: head(SB_FOX_K), "mem_norm_g": lambda: wide(SB_MEM_NORM_G), "w_mem_kv": lambda: rkv_ref[...],
            "mem_q_g": lambda: head(SB_MEM_Q), "mem_k_g": lambda: head(SB_MEM_K), "w_out": lambda: rout_ref[...],
        }
        for p, name in enumerate(PARAM_ORDER):
            w_ref, m_ref, v_ref = ins[3 * p:3 * p + 3]
            g_out, d_out, m_out, v_out = outs[4 * p:4 * p + 4]
            if name == "w_pool":
                for grp in range(4):
                    g = sred_ref[SB_W_POOL:SB_W_POOL + 64, 64 * grp:64 * (grp + 1)]
                    delta, m_new, v_new = _adamw(w_ref[grp], g, m_ref[grp], v_ref[grp])
                    g_out[grp], d_out[grp], m_out[grp], v_out[grp] = g, delta, m_new, v_new
            else:
                g = grads[name]()
                delta, m_new, v_new = _adamw(w_ref[...], g, m_ref[...], v_ref[...])
                g_out[...], d_out[...], m_out[...], v_out[...] = g, delta, m_new, v_new

    flat = [t for name in PARAM_ORDER for t in params[name]]
    shapes = [params[name][0].shape for name in PARAM_ORDER for _ in range(4)]
    outs = pl.pallas_call(
        body,
        name="adamw_update",
        out_shape=tuple(jax.ShapeDtypeStruct(s, F32) for s in shapes),
        in_specs=[pl.BlockSpec(memory_space=pltpu.VMEM)] * (4 + len(flat)),
        out_specs=(pl.BlockSpec(memory_space=pltpu.VMEM),) * len(shapes),
        compiler_params=pltpu.CompilerParams(vmem_limit_bytes=VMEM_LIMIT),
    )(red_in, red_kv, red_out, sred, *flat)
    return {name: outs[4 * p:4 * p + 4] for p, name in enumerate(PARAM_ORDER)}


def kernel(x, mem, norm_g, w_in, b_f, w_pool, pool_scale, fox_q_g, fox_k_g, mem_norm_g, w_mem_kv, mem_q_g, mem_k_g, w_out, loss_target, m_norm_g, m_w_in, m_b_f, m_w_pool, m_pool_scale, m_fox_q_g, m_fox_k_g, m_mem_norm_g, m_w_mem_kv, m_mem_q_g, m_mem_k_g, m_w_out, v_norm_g, v_w_in, v_b_f, v_w_pool, v_pool_scale, v_fox_q_g, v_fox_k_g, v_mem_norm_g, v_w_mem_kv, v_mem_q_g, v_mem_k_g, v_w_out):
    given = dict(norm_g=(norm_g, m_norm_g, v_norm_g), w_in=(w_in, m_w_in, v_w_in), b_f=(b_f, m_b_f, v_b_f),
                 w_pool=(w_pool, m_w_pool, v_w_pool), pool_scale=(pool_scale, m_pool_scale, v_pool_scale),
                 fox_q_g=(fox_q_g, m_fox_q_g, v_fox_q_g), fox_k_g=(fox_k_g, m_fox_k_g, v_fox_k_g),
                 mem_norm_g=(mem_norm_g, m_mem_norm_g, v_mem_norm_g), w_mem_kv=(w_mem_kv, m_w_mem_kv, v_w_mem_kv),
                 mem_q_g=(mem_q_g, m_mem_q_g, v_mem_q_g), mem_k_g=(mem_k_g, m_mem_k_g, v_mem_k_g), w_out=(w_out, m_w_out, v_w_out))
    params = {name: tuple(t[0] if t.ndim > 2 else t for t in wmv) for name, wmv in given.items()}
    x2, mem2, target2 = x[0], mem[0], loss_target[0]

    seg = jnp.asarray(np.kron(np.eye(FOX_HEADS), np.full((HEAD_DIM, HEAD_DIM), 1.0 / HEAD_DIM)), BF16)
    w_my, w_kv16, w_out16, wp_bd, bf_pad, gq8, gk8, gqm4 = _gather_w_in(
        params["w_in"][0], params["w_mem_kv"][0], params["w_out"][0], params["w_pool"][0], b_f, fox_q_g, fox_k_g, mem_q_g)
    proj, h16 = _fwd_proj(x2, norm_g, w_my)
    q_aug, k_aug, v_h, kt_aug, vt_aug = _fox_prep(proj, bf_pad, gq8, gk8, seg)
    olse, lse_rows, w_kv_all, w_out_all = _fox_fwd(q_aug, k_aug, vt_aug, w_kv16, w_out16)
    km, vm = _mem_prep(mem2, mem_norm_g, w_kv_all, mem_k_g)
    mixed = _mix_fwd(proj, olse, km, vm, wp_bd, pool_scale, gqm4, seg)
    d_out, d_mixed, dw_out, loss_blk = _out_loss(mixed, x2, target2, w_out_all)

    d_ua, d_gb, d_qm, d_o, delta_rows, dwp_bd, d_scale, dkm, dvm, d_gqm = _mix_bwd(proj, olse, d_mixed, km, vm, wp_bd, pool_scale,
                                                                                    gqm4, seg)
    dw_kv, d_mem_g, d_gkm = _mem_bwd(mem2, mem_norm_g, w_kv_all, mem_k_g, dkm, dvm)
    dq_aug, dk_aug, dv_h, land_kv, land_out = _fox_bwd(q_aug, k_aug, kt_aug, v_h, d_o, lse_rows, delta_rows, dw_kv, dw_out)
    d_q, d_k, d_v, d_f, d_gq, d_gk, d_bf = _fox_post(proj, bf_pad, gq8, gk8, seg, dq_aug, dk_aug, dv_h)
    pieces = (d_ua, d_q, d_k, d_v, d_gb, d_qm)
    dw_in = _grad_w_in(h16, pieces, d_f)
    grad_x, red_in, red_kv, red_out, sred = _grad_x_exchange(pieces, d_f, w_my, x2, norm_g, d_out, dw_in, land_kv, land_out, d_mem_g,
                                                             d_scale, d_bf, d_gq, d_gk, d_gqm, d_gkm, dwp_bd, loss_blk)
    new = _update(red_in, red_kv, red_out, sred, params)
    result = [sred[SB_LOSS, 0], grad_x[None]]
    for kind in range(4):
        for name in PARAM_ORDER:
            out = new[name][kind]
            result.append(out[None] if given[name][0].ndim > 2 else out)
    return tuple(result)
```

```python
import functools

import jax
import jax.numpy as jnp
import numpy as np
from jax import lax
from jax.experimental import pallas as pl
from jax.experimental.pallas import tpu as pltpu

F32 = jnp.float32
BF16 = jnp.bfloat16
MESH = pl.DeviceIdType.MESH

N_DEV = 8
D_MODEL = 1024
HEAD_DIM = 64
FOX_HEADS = 8
MEM_HEADS = 4
N_MEM = 256
POOL_WIDTH = 256
EPS = 1e-6
IN_WIDTH = 3080
IN_CHUNK = IN_WIDTH // N_DEV
F_OFF = 2048
MY_WIDTH = 3200
MY_F_OFF = 3072
PIECE = 512
TM = 256
TMM = 512
TA = 256
HB = 8
HB_FWD = 8
AUG_ROWS = 72
HALO = 16
VMEM_LIMIT = 56 * 1024 * 1024

ADAM_LR = 0.001
ADAM_B1 = 0.9
ADAM_B2 = 0.999
ADAM_EPS = 1e-08
ADAM_WD = 0.01
ADAM_STEP = 10


def _dot(a, b):
    return jnp.dot(a, b, preferred_element_type=F32)


def _dot_nt(a, b):
    return lax.dot_general(a, b, (((1,), (1,)), ((), ())), preferred_element_type=F32)


def _dot_tn(a, b):
    return lax.dot_general(a, b, (((0,), (0,)), ((), ())), preferred_element_type=F32)


def _sigmoid(g):
    return 0.5 + 0.5 * jnp.tanh(0.5 * g)


def _split3(v):
    hi = v.astype(BF16)
    r1 = v - hi.astype(F32)
    mid = r1.astype(BF16)
    lo = (r1 - mid.astype(F32)).astype(BF16)
    return hi, mid, lo


def _rms(v):
    return lax.rsqrt(jnp.mean(v * v, axis=-1, keepdims=True) + EPS)


def _rms_bwd(a, r, g, dy):
    da = dy * g
    return r * (da - a * jnp.mean(da * a, axis=-1, keepdims=True)), dy * a


def _head_mean(z, seg):
    hi = z.astype(BF16)
    lo = (z - hi.astype(F32)).astype(BF16)
    if z.shape[1] == 256:
        return _dot(hi, seg) + _dot(lo, seg)
    half = seg[0:256, 0:256]
    return jnp.concatenate([_dot(hi[:, 0:256], half) + _dot(lo[:, 0:256], half),
                            _dot(hi[:, 256:512], half) + _dot(lo[:, 256:512], half)], axis=1)


def _head_rms(v, seg):
    return lax.rsqrt(_head_mean(v * v, seg) + EPS)


def _head_rms_bwd(a, r, g, dy, seg):
    da = dy * g
    return r * (da - a * _head_mean(da * a, seg)), dy * a


def _fold_heads(row, n_heads):
    out = row[:, 0:HEAD_DIM]
    for h in range(1, n_heads):
        out = out + row[:, HEAD_DIM * h:HEAD_DIM * (h + 1)]
    return out


def _params(sem, limit=VMEM_LIMIT):
    return pltpu.CompilerParams(dimension_semantics=sem, vmem_limit_bytes=limit)


def _full(shape):
    return pl.BlockSpec(shape, lambda *_: (0,) * len(shape))


def _chunk_segments(d):
    runs = []
    for lo, hi, shift in ((0, F_OFF, 0), (F_OFF, F_OFF + FOX_HEADS, MY_F_OFF - F_OFF), (F_OFF + FOX_HEADS, IN_WIDTH, -FOX_HEADS)):
        a, b = max(lo, IN_CHUNK * d), min(hi, IN_CHUNK * (d + 1))
        if a < b:
            runs.append((a + shift, b - a))
    return runs


def _my_place():
    x, y, c = lax.axis_index("x"), lax.axis_index("y"), lax.axis_index("c")
    return x, y, c, 4 * x + 2 * y + c


def _shard_rows(dev):
    return pl.ds(pl.multiple_of(128 * dev, 128), 128)


def _gather_w_in(w_in, w_kv, w_out, w_pool, b_f, gq, gk, gqm):
    def body(win_ref, wkv_ref, wout_ref, wp_ref, bf_ref, gq_ref, gk_ref, gqm_ref, wmy_ref, kv16_ref, out16_ref, wpbd_ref, bfpad_ref,
             gq8_ref, gk8_ref, gqm4_ref, in_all, pay_in, send_sems, recv_sems):
        x, y, c, me = _my_place()
        here, sibling = (x, y, c), (x, y, 1 - c)
        x_chip, y_chip, far_chip = (1 - x, y), (x, 1 - y), (1 - x, 1 - y)
        relay_from = (x ^ (1 - c), y ^ c)
        relay_to = (x ^ c, y ^ (1 - c))

        pay_in[...] = win_ref[...].astype(BF16)

        def copy(k, block, to, own=False):
            px, py, pc = block
            dst = in_all.at[4 * px + 2 * py + pc]
            return pltpu.make_async_remote_copy(src_ref=pay_in if own else dst, dst_ref=dst, send_sem=send_sems.at[k],
                                                recv_sem=recv_sems.at[k], device_id=to, device_id_type=MESH)

        first = [copy(0, here, sibling, own=True), copy(1, here, (*x_chip, c), own=True), copy(2, here, (*y_chip, c), own=True)]
        for cp in first:
            cp.start()
        in_all[me] = pay_in[...]
        kv16_ref[...] = wkv_ref[...].astype(BF16)
        out16_ref[...] = wout_ref[...].astype(BF16)
        wpbd_ref[...] = jnp.zeros((POOL_WIDTH, POOL_WIDTH), BF16)
        for grp in range(4):
            sl = slice(64 * grp, 64 * (grp + 1))
            wpbd_ref[sl, sl] = wp_ref[grp].astype(BF16)
        bfpad_ref[...] = jnp.zeros((1, 128), F32)
        bfpad_ref[:, 0:FOX_HEADS] = bf_ref[...]
        gq8_ref[...] = jnp.concatenate([gq_ref[...]] * FOX_HEADS, axis=1)
        gk8_ref[...] = jnp.concatenate([gk_ref[...]] * FOX_HEADS, axis=1)
        gqm4_ref[...] = jnp.concatenate([gqm_ref[...]] * MEM_HEADS, axis=1)
        copy(1 + c, (*relay_from, c), here).wait_recv()
        passed = [copy(3, (*relay_from, c), (*relay_to, c)), copy(4, (*relay_from, c), sibling)]
        for cp in passed:
            cp.start()
        copy(2 - c, (*relay_to, c), here).wait_recv()
        passed.append(copy(5, (*relay_to, c), sibling))
        passed[-1].start()
        copy(3, (*far_chip, c), here).wait_recv()
        passed.append(copy(6, (*far_chip, c), sibling))
        passed[-1].start()
        copy(0, sibling, here).wait_recv()
        for k, chip in ((4, relay_to), (5, relay_from), (6, far_chip)):
            copy(k, (*chip, 1 - c), here).wait_recv()
        for cp in first + passed:
            cp.wait_send()

        for r0 in range(0, D_MODEL, 256):
            ch = [in_all[d, r0:r0 + 256, :].astype(F32) for d in range(N_DEV)]
            lo = F_OFF - 5 * IN_CHUNK
            full = jnp.concatenate(ch[:5] + [ch[5][:, :lo], ch[5][:, lo + FOX_HEADS:], ch[6], ch[7], ch[5][:, lo:lo + FOX_HEADS],
                                             jnp.zeros((256, MY_WIDTH - IN_WIDTH), F32)], axis=1)
            wmy_ref[r0:r0 + 256, :] = full.astype(BF16)

    return pl.pallas_call(
        body,
        name="gather_w_in",
        out_shape=(jax.ShapeDtypeStruct((D_MODEL, MY_WIDTH), BF16), jax.ShapeDtypeStruct((128, 512), BF16),
                   jax.ShapeDtypeStruct((128, D_MODEL), BF16), jax.ShapeDtypeStruct((POOL_WIDTH, POOL_WIDTH), BF16),
                   jax.ShapeDtypeStruct((1, 128), F32), jax.ShapeDtypeStruct((1, PIECE), F32), jax.ShapeDtypeStruct((1, PIECE), F32),
                   jax.ShapeDtypeStruct((1, 256), F32)),
        in_specs=[pl.BlockSpec(memory_space=pltpu.VMEM)] * 8,
        out_specs=(pl.BlockSpec(memory_space=pltpu.VMEM),) * 8,
        scratch_shapes=[
            pltpu.VMEM((N_DEV, D_MODEL, IN_CHUNK), BF16),
            pltpu.VMEM((D_MODEL, IN_CHUNK), BF16),
            pltpu.SemaphoreType.DMA((7,)),
            pltpu.SemaphoreType.DMA((7,)),
        ],
        compiler_params=pltpu.CompilerParams(vmem_limit_bytes=VMEM_LIMIT),
    )(w_in, w_kv, w_out, w_pool, b_f, gq, gk, gqm)


def _row_block_exchange(kv_ref, out_ref, kv_dst, out_dst, send_sems, recv_sems, local_sems, gather):
    x, y, c, me = _my_place()
    remote = []
    for k in range(1, N_DEV):
        px, py, pc = x ^ (k >> 2), y ^ ((k >> 1) & 1), c ^ (k & 1)
        peer = 4 * px + 2 * py + pc
        for fam, (src, dst) in enumerate(((kv_ref, kv_dst), (out_ref, out_dst))):
            remote.append(pltpu.make_async_remote_copy(
                src_ref=src if gather else src.at[_shard_rows(peer)], dst_ref=dst.at[_shard_rows(me)] if gather else dst.at[me],
                send_sem=send_sems.at[7 * fam + k - 1], recv_sem=recv_sems.at[7 * fam + k - 1],
                device_id=(px, py, pc), device_id_type=MESH))
    local = [pltpu.make_async_copy(src if gather else src.at[_shard_rows(me)], dst.at[_shard_rows(me)] if gather else dst.at[me],
                                   local_sems.at[fam]) for fam, (src, dst) in enumerate(((kv_ref, kv_dst), (out_ref, out_dst)))]
    return remote, local


SB_ROWS, SB_W = 80, 256
SB_NORM_G, SB_MEM_NORM_G, SB_POOL_SCALE, SB_B_F, SB_FOX_Q, SB_FOX_K, SB_MEM_Q, SB_MEM_K, SB_LOSS, SB_W_POOL = 0, 4, 8, 9, 10, 11, 12, 13, 14, 16


def _fwd_proj(x, norm_g, w_my):
    s_len = x.shape[0]

    def body(x_ref, g_ref, w_ref, proj_ref, h_ref):
        xv = x_ref[...]
        h = (xv * _rms(xv) * g_ref[...]).astype(BF16)
        h_ref[...] = h
        proj_ref[...] = _dot(h, w_ref[...])

    return pl.pallas_call(
        body,
        name="fwd_proj",
        grid=(s_len // TMM,),
        in_specs=[pl.BlockSpec((TMM, D_MODEL), lambda i: (i, 0)), _full((1, D_MODEL)), _full((D_MODEL, MY_WIDTH))],
        out_specs=[pl.BlockSpec((TMM, MY_WIDTH), lambda i: (i, 0)), pl.BlockSpec((TMM, D_MODEL), lambda i: (i, 0))],
        out_shape=[jax.ShapeDtypeStruct((s_len, MY_WIDTH), F32), jax.ShapeDtypeStruct((s_len, D_MODEL), BF16)],
        compiler_params=_params(("parallel",)),
    )(x, norm_g, w_my)


def _log_sigmoid(z):
    return jnp.minimum(z, 0.0) - jnp.log(1.0 + jnp.exp(-jnp.abs(z)))


def _forget_lane_maps():
    to_q = np.zeros((128, FOX_HEADS * 128), np.float32)
    to_k = np.zeros((128, FOX_HEADS * 128), np.float32)
    for h in range(FOX_HEADS):
        for term in range(3):
            to_q[8 * term + h, 128 * h + 64 + term] = 1.0
            to_q[24, 128 * h + 67 + term] = 1.0
            to_k[24, 128 * h + 64 + term] = 1.0
            to_k[8 * term + h, 128 * h + 67 + term] = -1.0
    return jnp.asarray(to_q, BF16), jnp.asarray(to_k, BF16)


def _fox_prep(proj, bf_pad, gq, gk, seg):
    s_len = proj.shape[0]
    to_q, to_k = _forget_lane_maps()

    def body(q_ref, k_ref, v_ref, f_ref, bf_ref, gq_ref, gk_ref, seg_ref, eq_ref, ek_ref,
             qa_ref, ka_ref, vh_ref, kt_ref, vt_ref, carry_ref):
        @pl.when(pl.program_id(0) == 0)
        def _():
            carry_ref[...] = jnp.zeros((1, 128), F32)

        lane = lax.broadcasted_iota(jnp.int32, (TM, 128), 1)
        logf = jnp.where(lane < FOX_HEADS, _log_sigmoid(f_ref[...] + bf_ref[...]), 0.0)
        row = lax.broadcasted_iota(jnp.int32, (TM, TM), 0)
        col = lax.broadcasted_iota(jnp.int32, (TM, TM), 1)
        tri = jnp.where(col <= row, 1.0, 0.0).astype(BF16)
        hi, mid, lo = _split3(logf)
        cum = _dot(tri, hi) + _dot(tri, mid) + _dot(tri, lo) + carry_ref[...]
        carry_ref[...] = cum[TM - 1:TM, :]
        f_hi, f_mid, f_lo = [t.astype(F32) for t in _split3(cum)]
        packed = (f_hi + pltpu.roll(f_mid, 8, 1) + pltpu.roll(f_lo, 16, 1) + jnp.where(lane == 24, 1.0, 0.0)).astype(BF16)
        extra_q = _dot(packed, eq_ref[...])
        extra_k = _dot(packed, ek_ref[...])
        one_at_64 = jnp.where(lane == 64, 1.0, 0.0)
        zeros = jnp.zeros((TM, HEAD_DIM), F32)
        q_all = q_ref[...]
        k_all = k_ref[...]
        qn_all = q_all * _head_rms(q_all, seg_ref[...]) * gq_ref[...] * 0.125
        kn_all = k_all * _head_rms(k_all, seg_ref[...]) * gk_ref[...]
        for h in range(FOX_HEADS):
            sl = slice(HEAD_DIM * h, HEAD_DIM * (h + 1))
            tile = slice(128 * h, 128 * (h + 1))
            qa = jnp.where(lane < 64, jnp.concatenate([qn_all[:, sl], zeros], axis=1), extra_q[:, tile])
            ka = jnp.where(lane < 64, jnp.concatenate([kn_all[:, sl], zeros], axis=1), extra_k[:, tile])
            vh = v_ref[:, sl]
            v_aug = jnp.where(lane < 64, jnp.concatenate([vh, zeros], axis=1), one_at_64)
            qa_ref[h] = qa.astype(BF16)
            ka_ref[h] = ka.astype(BF16)
            vh_ref[h] = vh.astype(BF16)
            kt_ref[h, 0] = ka.T.astype(BF16)
            vt_ref[h, 0] = v_aug.T.astype(BF16)

    col_blk = lambda j: pl.BlockSpec((TM, PIECE), lambda i: (i, j))
    head_blk = lambda w: pl.BlockSpec((FOX_HEADS, TM, w), lambda i: (0, i, 0))
    tile_blk = pl.BlockSpec((FOX_HEADS, 1, 128, TM), lambda i: (0, i, 0, 0))
    tiled = jax.ShapeDtypeStruct((FOX_HEADS, s_len // TM, 128, TM), BF16)
    return pl.pallas_call(
        body,
        name="fox_prep",
        grid=(s_len // TM,),
        in_specs=[col_blk(1), col_blk(2), col_blk(3), pl.BlockSpec((TM, 128), lambda i: (i, MY_F_OFF // 128)),
                  _full((1, 128)), _full((1, PIECE)), _full((1, PIECE)), _full((PIECE, PIECE)),
                  _full((128, FOX_HEADS * 128)), _full((128, FOX_HEADS * 128))],
        out_specs=[head_blk(128), head_blk(128), head_blk(HEAD_DIM), tile_blk, tile_blk],
        out_shape=[jax.ShapeDtypeStruct((FOX_HEADS, s_len, 128), BF16), jax.ShapeDtypeStruct((FOX_HEADS, s_len, 128), BF16),
                   jax.ShapeDtypeStruct((FOX_HEADS, s_len, HEAD_DIM), BF16), tiled, tiled],
        scratch_shapes=[pltpu.VMEM((1, 128), F32)],
        compiler_params=_params(("arbitrary",)),
    )(proj, proj, proj, proj, bf_pad, gq, gk, seg, to_q, to_k)


def _mem_prep(mem, g, w_kv, gk):
    def body(mem_ref, g_ref, w_ref, gk_ref, km_ref, vm_ref):
        m = mem_ref[...]
        kv = _dot((m * _rms(m) * g_ref[...]).astype(BF16), w_ref[...])
        for h in range(MEM_HEADS):
            sl = slice(HEAD_DIM * h, HEAD_DIM * (h + 1))
            kh = kv[:, sl]
            km_ref[:, sl] = (kh * _rms(kh) * gk_ref[...]).astype(BF16)
        vm_ref[...] = kv[:, 256:512].astype(BF16)

    return pl.pallas_call(
        body,
        name="mem_prep",
        out_shape=(jax.ShapeDtypeStruct((N_MEM, 256), BF16),) * 2,
        in_specs=[pl.BlockSpec(memory_space=pltpu.VMEM)] * 4,
        out_specs=(pl.BlockSpec(memory_space=pltpu.VMEM),) * 2,
        compiler_params=pltpu.CompilerParams(vmem_limit_bytes=VMEM_LIMIT),
    )(mem, g, w_kv, gk)


def _causal_mask():
    row = lax.broadcasted_iota(jnp.int32, (TA, TA), 0)
    col = lax.broadcasted_iota(jnp.int32, (TA, TA), 1)
    return col <= row


def _causal_mask_t():
    row = lax.broadcasted_iota(jnp.int32, (TA, TA), 0)
    col = lax.broadcasted_iota(jnp.int32, (TA, TA), 1)
    return row <= col


def _fox_fwd(q_aug, k_aug, vt_aug, w_kv16, w_out16):
    s_len = q_aug.shape[1]
    n_tiles = s_len // TA
    hb = HB_FWD
    n_groups = FOX_HEADS // hb

    def body(q_ref, k_ref, vt_ref, kv16_ref, out16_ref, o_ref, st_ref, kv_all, out_all, send_sems, recv_sems, local_sems):
        qi = pl.program_id(1)
        remote, local = _row_block_exchange(kv16_ref, out16_ref, kv_all, out_all, send_sems, recv_sems, local_sems, gather=True)

        @pl.when((pl.program_id(0) == 0) & (qi == 0))
        def _():
            for cp in remote + local:
                cp.start()

        qs = [q_ref[h] for h in range(hb)]

        def step(t0, carry, masked, width):
            rows = pl.ds(pl.multiple_of(t0 * TA, TA), width * TA)
            scores = [_dot_nt(k_ref[h, rows, :], qs[h]) for h in range(hb)]
            soft = []
            for h in range(hb):
                m, _ = carry[h]
                s = jnp.where(_causal_mask_t(), scores[h], -1e30) if masked else scores[h]
                m_new = jnp.maximum(m, jnp.max(s, axis=0, keepdims=True))
                soft.append((m_new, jnp.exp(m - m_new), jnp.exp(s - m_new).astype(BF16)))
            out = []
            for h, (m_new, alpha, p) in enumerate(soft):
                acc = alpha * carry[h][1]
                for t in range(width):
                    acc = acc + _dot(vt_ref[h, t0 + t, 0:AUG_ROWS, :], p[t * TA:(t + 1) * TA])
                out.append((m_new, acc))
            return tuple(out)

        init = tuple((jnp.full((1, TA), -1e30, F32), jnp.zeros((AUG_ROWS, TA), F32)) for _ in range(hb))
        carry = lax.fori_loop(0, qi // 2, lambda j, cr: step(2 * j, cr, False, 2), init)
        carry = lax.fori_loop(2 * (qi // 2), qi, lambda j, cr: step(j, cr, False, 1), carry)
        carry = step(qi, carry, True, 1)
        for h in range(hb):
            m, acc = carry[h]
            l = acc[HEAD_DIM:HEAD_DIM + 1, :]
            lse = m + jnp.log(l)
            o_ref[h] = jnp.concatenate([acc[0:HEAD_DIM] / l, jnp.broadcast_to(lse, (HEAD_DIM, TA))], axis=0).T
            st_ref[h, 0] = jnp.broadcast_to(lse, (8, TA))

        @pl.when((pl.program_id(0) == n_groups - 1) & (qi == n_tiles - 1))
        def _():
            for cp in remote:
                cp.wait_recv()
            for cp in remote:
                cp.wait_send()
            for cp in local:
                cp.wait()

    hbm = pl.BlockSpec(memory_space=pl.ANY)
    return pl.pallas_call(
        body,
        name="fox_fwd",
        grid=(n_groups, n_tiles),
        in_specs=[pl.BlockSpec((hb, TA, 128), lambda g, i: (g, i, 0)), pl.BlockSpec((hb, s_len, 128), lambda g, i: (g, 0, 0)),
                  pl.BlockSpec((hb, n_tiles, 128, TA), lambda g, i: (g, 0, 0, 0)), hbm, hbm],
        out_specs=[pl.BlockSpec((hb, TA, 128), lambda g, i: (g, i, 0)), pl.BlockSpec((hb, 1, 8, TA), lambda g, i: (g, i, 0, 0)),
                   hbm, hbm],
        out_shape=[jax.ShapeDtypeStruct((FOX_HEADS, s_len, 128), F32), jax.ShapeDtypeStruct((FOX_HEADS, n_tiles, 8, TA), F32),
                   jax.ShapeDtypeStruct((D_MODEL, 512), BF16), jax.ShapeDtypeStruct((D_MODEL, D_MODEL), BF16)],
        scratch_shapes=[pltpu.SemaphoreType.DMA((14,)), pltpu.SemaphoreType.DMA((14,)), pltpu.SemaphoreType.DMA((2,))],
        compiler_params=_params(("arbitrary", "arbitrary")),
    )(q_aug, k_aug, vt_aug, w_kv16, w_out16)


def _lane_group(lane, a, b, c, d):
    return jnp.where(lane < 64, a, jnp.where(lane < 128, b, jnp.where(lane < 192, c, d)))


def _pool_count(row0):
    lane = lax.broadcasted_iota(jnp.int32, (TM, POOL_WIDTH), 1)
    t1 = row0 + lax.broadcasted_iota(jnp.int32, (TM, POOL_WIDTH), 0) + 1
    return 1.0 / jnp.minimum(t1, _lane_group(lane, 2, 4, 8, 16)).astype(F32), lane


def _pool_delta(u, halo, cnt, lane):
    xe = jnp.concatenate([halo, u], axis=0)
    s2 = xe + pltpu.roll(xe, 1, 0)
    s4 = s2 + pltpu.roll(s2, 2, 0)
    s8 = s4 + pltpu.roll(s4, 4, 0)
    s16 = s8 + pltpu.roll(s8, 8, 0)
    win = _lane_group(lane, s2[HALO:], s4[HALO:], s8[HALO:], s16[HALO:])
    return win * cnt - u


def _pool_delta_bwd(dd, dd_next, cnt, cnt_next, lane):
    ee = jnp.concatenate([dd * cnt, dd_next * cnt_next], axis=0)
    n = TM + HALO
    r2 = ee + pltpu.roll(ee, n - 1, 0)
    r4 = r2 + pltpu.roll(r2, n - 2, 0)
    r8 = r4 + pltpu.roll(r4, n - 4, 0)
    r16 = r8 + pltpu.roll(r8, n - 8, 0)
    return _lane_group(lane, r2[:TM], r4[:TM], r8[:TM], r16[:TM]) - dd


def _mem_attn(qm, gq, seg, km_ref, vm_ref):
    r = _head_rms(qm, seg)
    a = qm * r
    q16 = (a * gq * 0.125).astype(BF16)
    heads = []
    for h in range(MEM_HEADS):
        sl = slice(HEAD_DIM * h, HEAD_DIM * (h + 1))
        s = _dot_nt(q16[:, sl], km_ref[:, sl])
        e = jnp.exp(s - jnp.max(s, axis=-1, keepdims=True))
        p = e * (1.0 / jnp.sum(e, axis=-1, keepdims=True))
        heads.append((p, _dot(p.astype(BF16), vm_ref[:, sl])))
    return a, r, q16, heads


def _row_specs(s_len):
    n_halo = s_len // HALO
    piece = lambda j: pl.BlockSpec((TM, PIECE), lambda i: (i, j))
    before = lambda j: pl.BlockSpec((HALO, 256), lambda i: (jnp.maximum(i * (TM // HALO) - 1, 0), j))
    after = lambda j: pl.BlockSpec((HALO, 256), lambda i: (jnp.minimum((i + 1) * (TM // HALO), n_halo - 1), j))
    return piece, before, after


def _mix_fwd(proj, olse, km, vm, wp_bd, pool_scale, gq_m, seg):
    s_len = proj.shape[0]

    def body(ua_ref, halo_ref, gb_ref, qm_ref, ol_ref, km_ref, vm_ref, wp_ref, sc_ref, gq_ref, seg_ref, out_ref):
        i = pl.program_id(0)
        u = ua_ref[:, 0:256]
        g_a = ua_ref[:, 256:512]
        halo = jnp.where(i > 0, halo_ref[...], 0.0)
        cnt, lane = _pool_count(i * TM)
        d = _pool_delta(u, halo, cnt, lane).astype(BF16)
        y_a = _dot(d, wp_ref[...]) * sc_ref[...]
        out_ref[:, 0:256] = (y_a * (g_a * _sigmoid(g_a))).astype(BF16)
        g_b = gb_ref[...]
        y_b = jnp.concatenate([ol_ref[h][:, 0:HEAD_DIM] for h in range(FOX_HEADS)], axis=1)
        out_ref[:, 256:768] = (y_b * (g_b * _sigmoid(g_b))).astype(BF16)
        _, _, _, heads = _mem_attn(qm_ref[:, 0:256], gq_ref[...], seg_ref[0:256, 0:256], km_ref, vm_ref)
        g_m = qm_ref[:, 256:512]
        y_m = jnp.concatenate([y for _, y in heads], axis=1)
        out_ref[:, 768:1024] = (y_m * (g_m * _sigmoid(g_m))).astype(BF16)

    piece, before, _ = _row_specs(s_len)
    return pl.pallas_call(
        body,
        name="mix_fwd",
        grid=(s_len // TM,),
        in_specs=[piece(0), before(0), piece(4), piece(5), pl.BlockSpec((FOX_HEADS, TM, 128), lambda i: (0, i, 0)),
                  _full((N_MEM, 256)), _full((N_MEM, 256)), _full((256, 256)), _full((1, 256)), _full((1, 256)),
                  _full((PIECE, PIECE))],
        out_specs=pl.BlockSpec((TM, D_MODEL), lambda i: (i, 0)),
        out_shape=jax.ShapeDtypeStruct((s_len, D_MODEL), BF16),
        compiler_params=_params(("parallel",)),
    )(proj, proj, proj, proj, olse, km, vm, wp_bd, pool_scale, gq_m, seg)


def _out_loss(mixed, x, target, w_out):
    s_len = x.shape[0]

    def body(mix_ref, x_ref, t_ref, w_ref, dout_ref, dmix_ref, dw16_ref, loss_ref, dw_ref):
        @pl.when(pl.program_id(0) == 0)
        def _():
            dw_ref[...] = jnp.zeros((D_MODEL, D_MODEL), F32)
            loss_ref[...] = jnp.zeros((8, 128), F32)

        mixed16 = mix_ref[...]
        err = x_ref[...] + _dot(mixed16, w_ref[...]) - t_ref[...]
        loss_ref[...] += 0.5 * jnp.sum(jnp.mean(err * err, axis=-1, keepdims=True))
        d_out = err / float(D_MODEL)
        dout_ref[...] = d_out
        d16 = d_out.astype(BF16)
        dmix_ref[...] = _dot_nt(d16, w_ref[...])
        dw_ref[...] += _dot_tn(mixed16, d16)

        @pl.when(pl.program_id(0) == pl.num_programs(0) - 1)
        def _():
            dw16_ref[...] = dw_ref[...].astype(BF16)

    row = pl.BlockSpec((TMM, D_MODEL), lambda i: (i, 0))
    return pl.pallas_call(
        body,
        name="out_loss",
        grid=(s_len // TMM,),
        in_specs=[row, row, row, _full((D_MODEL, D_MODEL))],
        out_specs=[row, row, _full((D_MODEL, D_MODEL)), _full((8, 128))],
        out_shape=[jax.ShapeDtypeStruct((s_len, D_MODEL), F32), jax.ShapeDtypeStruct((s_len, D_MODEL), F32),
                   jax.ShapeDtypeStruct((D_MODEL, D_MODEL), BF16), jax.ShapeDtypeStruct((8, 128), F32)],
        scratch_shapes=[pltpu.VMEM((D_MODEL, D_MODEL), F32)],
        compiler_params=_params(("arbitrary",)),
    )(mixed, x, target, w_out)


def _silu_pair(g):
    s = _sigmoid(g)
    return g * s, s * (1.0 + g * (1.0 - s))


def _mix_bwd(proj, olse, d_mixed, km, vm, wp_bd, pool_scale, gq_m, seg):
    s_len = proj.shape[0]
    n_tiles = s_len // TM

    def body(ua_ref, halo_ref, ga_next_ref, gb_ref, qm_ref, ol_ref, dm_ref, dm_next_ref, km_ref, vm_ref, wp_ref, sc_ref, gq_ref,
             seg_ref, dua_ref, dgb_ref, dqm_ref, do_ref, dl_ref, dwp_ref, dsc_ref, dkm_ref, dvm_ref, dgq_ref):
        i = pl.program_id(0)

        @pl.when(i == 0)
        def _():
            dwp_ref[...] = jnp.zeros((256, 256), F32)
            dsc_ref[...] = jnp.zeros((1, 256), F32)
            dkm_ref[...] = jnp.zeros((N_MEM, 256), F32)
            dvm_ref[...] = jnp.zeros((N_MEM, 256), F32)
            dgq_ref[...] = jnp.zeros((1, HEAD_DIM), F32)

        u = ua_ref[:, 0:256]
        g_a = ua_ref[:, 256:512]
        halo = jnp.where(i > 0, halo_ref[...], 0.0)
        cnt, lane = _pool_count(i * TM)
        d16 = _pool_delta(u, halo, cnt, lane).astype(BF16)
        t = _dot(d16, wp_ref[...])
        y_a = t * sc_ref[...]
        silu_a, dsilu_a = _silu_pair(g_a)
        dm_a = dm_ref[:, 0:256]
        dy_a = dm_a * silu_a
        dsc_ref[...] += jnp.sum(dy_a * t, axis=0, keepdims=True)
        dt16 = (dy_a * sc_ref[...]).astype(BF16)
        dwp_ref[...] += _dot_tn(d16, dt16)
        dd = _dot_nt(dt16, wp_ref[...])
        g_next = ga_next_ref[...]
        dt_next = (dm_next_ref[...] * (g_next * _sigmoid(g_next)) * sc_ref[...]).astype(BF16)
        dd_next = jnp.where(i < n_tiles - 1, _dot_nt(dt_next, wp_ref[...]), 0.0)
        cnt_next = _pool_count((i + 1) * TM)[0][0:HALO]
        dua_ref[:, 0:256] = _pool_delta_bwd(dd, dd_next, cnt, cnt_next, lane).astype(BF16)
        dua_ref[:, 256:512] = (dm_a * y_a * dsilu_a).astype(BF16)

        silu_b, dsilu_b = _silu_pair(gb_ref[...])
        dm_b = dm_ref[:, 256:768]
        o_all = jnp.concatenate([ol_ref[h][:, 0:HEAD_DIM] for h in range(FOX_HEADS)], axis=1)
        d_o = dm_b * silu_b
        dgb_ref[...] = (dm_b * o_all * dsilu_b).astype(BF16)
        for h in range(FOX_HEADS):
            do_ref[h] = d_o[:, HEAD_DIM * h:HEAD_DIM * (h + 1)].astype(BF16)
        prod = d_o * o_all
        hi = prod.astype(BF16)
        lo = (prod - hi.astype(F32)).astype(BF16)
        head_of_lane = lax.broadcasted_iota(jnp.int32, (FOX_HEADS, PIECE), 1) // HEAD_DIM
        pick = jnp.where(head_of_lane == lax.broadcasted_iota(jnp.int32, (FOX_HEADS, PIECE), 0), 1.0, 0.0).astype(BF16)
        delta = _dot_nt(pick, hi) + _dot_nt(pick, lo)
        for h in range(FOX_HEADS):
            dl_ref[h, 0] = jnp.broadcast_to(delta[h:h + 1, :], (8, TM))

        seg = seg_ref[0:256, 0:256]
        a, r, q16, heads = _mem_attn(qm_ref[:, 0:256], gq_ref[...], seg, km_ref, vm_ref)
        silu_m, dsilu_m = _silu_pair(qm_ref[:, 256:512])
        dm_m = dm_ref[:, 768:1024]
        y_m = jnp.concatenate([y for _, y in heads], axis=1)
        dqm_ref[:, 256:512] = (dm_m * y_m * dsilu_m).astype(BF16)
        dy16_all = (dm_m * silu_m).astype(BF16)
        d_scores = []
        for h in range(MEM_HEADS):
            sl = slice(HEAD_DIM * h, HEAD_DIM * (h + 1))
            p = heads[h][0]
            dy16 = dy16_all[:, sl]
            dp = _dot_nt(dy16, vm_ref[:, sl])
            dvm_ref[:, sl] += _dot_tn(p.astype(BF16), dy16)
            ds16 = (p * (dp - jnp.sum(dp * p, axis=-1, keepdims=True))).astype(BF16)
            dkm_ref[:, sl] += _dot_tn(ds16, q16[:, sl])
            d_scores.append(_dot(ds16, km_ref[:, sl]))
        dq, dg_rows = _head_rms_bwd(a, r, gq_ref[...], jnp.concatenate(d_scores, axis=1) * 0.125, seg)
        dqm_ref[:, 0:256] = dq.astype(BF16)
        dgq_ref[...] += _fold_heads(jnp.sum(dg_rows, axis=0, keepdims=True), MEM_HEADS)

    piece, before, after = _row_specs(s_len)
    n_halo = s_len // HALO
    row = pl.BlockSpec((TM, D_MODEL), lambda i: (i, 0))
    dm_after = pl.BlockSpec((HALO, 256), lambda i: (jnp.minimum((i + 1) * (TM // HALO), n_halo - 1), 0))
    out_piece = pl.BlockSpec((TM, PIECE), lambda i: (i, 0))
    return pl.pallas_call(
        body,
        name="mix_bwd",
        grid=(n_tiles,),
        in_specs=[piece(0), before(0), after(1), piece(4), piece(5), pl.BlockSpec((FOX_HEADS, TM, 128), lambda i: (0, i, 0)),
                  row, dm_after, _full((N_MEM, 256)), _full((N_MEM, 256)), _full((256, 256)), _full((1, 256)), _full((1, 256)),
                  _full((PIECE, PIECE))],
        out_specs=[out_piece, out_piece, out_piece, pl.BlockSpec((FOX_HEADS, TM, HEAD_DIM), lambda i: (0, i, 0)),
                   pl.BlockSpec((FOX_HEADS, 1, 8, TM), lambda i: (0, i, 0, 0)),
                   _full((256, 256)), _full((1, 256)), _full((N_MEM, 256)), _full((N_MEM, 256)), _full((1, HEAD_DIM))],
        out_shape=[jax.ShapeDtypeStruct((s_len, PIECE), BF16)] * 3 + [
            jax.ShapeDtypeStruct((FOX_HEADS, s_len, HEAD_DIM), BF16),
            jax.ShapeDtypeStruct((FOX_HEADS, n_tiles, 8, TM), F32),
            jax.ShapeDtypeStruct((256, 256), F32), jax.ShapeDtypeStruct((1, 256), F32),
            jax.ShapeDtypeStruct((N_MEM, 256), F32), jax.ShapeDtypeStruct((N_MEM, 256), F32),
            jax.ShapeDtypeStruct((1, HEAD_DIM), F32)],
        compiler_params=_params(("arbitrary",)),
    )(proj, proj, proj, proj, proj, olse, d_mixed, d_mixed, km, vm, wp_bd, pool_scale, gq_m, seg)


def _fox_bwd(q_aug, k_aug, kt_aug, v_h, d_o, lse_rows, delta_rows, dw_kv16, dw_out16):
    s_len = q_aug.shape[1]
    n_tiles = s_len // TA
    n_groups = FOX_HEADS // HB

    def body(q_ref, k_ref, kt_ref, v_ref, do_ref, st_ref, dl_ref, dkv16_ref, dout16_ref, dq_ref, dk_ref, dv_ref, land_kv, land_out,
             dqt_ref, send_sems, recv_sems, local_sems):
        j = pl.program_id(1)
        remote, local = _row_block_exchange(dkv16_ref, dout16_ref, land_kv, land_out, send_sems, recv_sems, local_sems, gather=False)

        @pl.when((pl.program_id(0) == 0) & (j == 0))
        def _():
            for cp in remote + local:
                cp.start()

        @pl.when(j == 0)
        def _():
            dqt_ref[...] = jnp.zeros((HB, n_tiles, AUG_ROWS, TA), F32)

        ks = [k_ref[h] for h in range(HB)]
        kts = [kt_ref[h, 0, 0:AUG_ROWS, :] for h in range(HB)]
        vs = [v_ref[h] for h in range(HB)]

        def pair(i0, acc, masked, width):
            rows = pl.ds(pl.multiple_of(i0 * TA, TA), width * TA)
            qs = [q_ref[h, rows, :] for h in range(HB)]
            d_os = [do_ref[h, rows, :] for h in range(HB)]
            row_stat = lambda ref, h: jnp.concatenate([ref[h, i0 + t, 0:1, :] for t in range(width)], axis=1)
            scores = [(_dot_nt(ks[h], qs[h]), _dot_nt(vs[h], d_os[h])) for h in range(HB)]
            probs = []
            for h in range(HB):
                s, dp = scores[h]
                if masked:
                    s = jnp.where(_causal_mask_t(), s, -1e30)
                p = jnp.exp(s - row_stat(st_ref, h))
                probs.append((p.astype(BF16), (p * (dp - row_stat(dl_ref, h))).astype(BF16)))
            out = []
            for h in range(HB):
                p16, ds16 = probs[h]
                dqt = _dot(kts[h], ds16)
                for t in range(width):
                    dqt_ref[h, i0 + t] += dqt[:, t * TA:(t + 1) * TA]
                out.append((acc[h][0] + _dot(ds16, qs[h]), acc[h][1] + _dot(p16, d_os[h])))
            return tuple(out)

        zero = tuple((jnp.zeros((TA, 128), F32), jnp.zeros((TA, HEAD_DIM), F32)) for _ in range(HB))
        acc = pair(j, zero, True, 1)
        odd = (n_tiles - 1 - j) % 2
        acc = lax.fori_loop(j + 1, j + 1 + odd, lambda i, a: pair(i, a, False, 1), acc)
        acc = lax.fori_loop(0, (n_tiles - 1 - j) // 2, lambda t, a: pair(j + 1 + odd + 2 * t, a, False, 2), acc)
        pad = jnp.zeros((128 - AUG_ROWS, TA), F32)
        for h in range(HB):
            dk_ref[h] = acc[h][0]
            dv_ref[h] = acc[h][1]
            dq_ref[h] = jnp.concatenate([dqt_ref[h, j], pad], axis=0).T

        @pl.when((pl.program_id(0) == n_groups - 1) & (j == n_tiles - 1))
        def _():
            for cp in remote:
                cp.wait_recv()
            for cp in remote:
                cp.wait_send()
            for cp in local:
                cp.wait()

    assert n_groups == 1
    resident = pl.BlockSpec(memory_space=pltpu.VMEM)
    whole = lambda w: resident
    rows_blk = lambda r: resident
    tile = lambda w: pl.BlockSpec((HB, TA, w), lambda g, j: (g, j, 0))
    hbm = pl.BlockSpec(memory_space=pl.ANY)
    return pl.pallas_call(
        body,
        name="fox_bwd",
        grid=(n_groups, n_tiles),
        in_specs=[whole(128), tile(128), pl.BlockSpec((HB, 1, 128, TA), lambda g, j: (g, j, 0, 0)), tile(HEAD_DIM),
                  whole(HEAD_DIM), rows_blk(8), rows_blk(8), hbm, hbm],
        out_specs=[tile(128), tile(128), tile(HEAD_DIM), hbm, hbm],
        out_shape=[jax.ShapeDtypeStruct((FOX_HEADS, s_len, 128), F32), jax.ShapeDtypeStruct((FOX_HEADS, s_len, 128), F32),
                   jax.ShapeDtypeStruct((FOX_HEADS, s_len, HEAD_DIM), F32),
                   jax.ShapeDtypeStruct((N_DEV, 128, 512), BF16), jax.ShapeDtypeStruct((N_DEV, 128, D_MODEL), BF16)],
        scratch_shapes=[pltpu.VMEM((HB, n_tiles, AUG_ROWS, TA), F32),
                        pltpu.SemaphoreType.DMA((14,)), pltpu.SemaphoreType.DMA((14,)), pltpu.SemaphoreType.DMA((2,))],
        compiler_params=_params(("arbitrary", "arbitrary")),
    )(q_aug, k_aug, kt_aug, v_h, d_o, lse_rows, delta_rows, dw_kv16, dw_out16)


def _fox_post(proj, bf_pad, gq, gk, seg, dq_aug, dk_aug, dv_h):
    s_len = proj.shape[0]
    n_tiles = s_len // TM

    def body(q_ref, k_ref, f_ref, bf_ref, gq_ref, gk_ref, seg_ref, dqa_ref, dka_ref, dvh_ref,
             dq_ref, dk_ref, dv_ref, df_ref, dgq_ref, dgk_ref, dbf_ref, carry_ref):
        @pl.when(pl.program_id(0) == 0)
        def _():
            carry_ref[...] = jnp.zeros((1, 128), F32)
            dgq_ref[...] = jnp.zeros((1, HEAD_DIM), F32)
            dgk_ref[...] = jnp.zeros((1, HEAD_DIM), F32)
            dbf_ref[...] = jnp.zeros((1, 128), F32)

        lane = lax.broadcasted_iota(jnp.int32, (TM, 128), 1)
        seg = seg_ref[...]
        dqas = [dqa_ref[h] for h in range(FOX_HEADS)]
        dkas = [dka_ref[h] for h in range(FOX_HEADS)]
        d_cum = jnp.zeros((TM, 128), F32)
        for h in range(FOX_HEADS):
            d_cum = jnp.where(lane == h, dqas[h][:, 64:65] - dkas[h][:, 67:68], d_cum)
        q_all = q_ref[...]
        k_all = k_ref[...]
        rq = _head_rms(q_all, seg)
        rk = _head_rms(k_all, seg)
        dy_q = jnp.concatenate([t[:, 0:HEAD_DIM] for t in dqas], axis=1) * 0.125
        dy_k = jnp.concatenate([t[:, 0:HEAD_DIM] for t in dkas], axis=1)
        dq, dgq_rows = _head_rms_bwd(q_all * rq, rq, gq_ref[...], dy_q, seg)
        dk, dgk_rows = _head_rms_bwd(k_all * rk, rk, gk_ref[...], dy_k, seg)
        dq_ref[...] = dq.astype(BF16)
        dk_ref[...] = dk.astype(BF16)
        dv_ref[...] = jnp.concatenate([dvh_ref[h] for h in range(FOX_HEADS)], axis=1).astype(BF16)
        dgq_ref[...] += _fold_heads(jnp.sum(dgq_rows, axis=0, keepdims=True), FOX_HEADS)
        dgk_ref[...] += _fold_heads(jnp.sum(dgk_rows, axis=0, keepdims=True), FOX_HEADS)

        row = lax.broadcasted_iota(jnp.int32, (TM, TM), 0)
        col = lax.broadcasted_iota(jnp.int32, (TM, TM), 1)
        tri = jnp.where(col >= row, 1.0, 0.0).astype(BF16)
        hi, mid, lo = _split3(d_cum)
        d_logf = _dot(tri, hi) + _dot(tri, mid) + _dot(tri, lo) + carry_ref[...]
        carry_ref[...] = d_logf[0:1, :]
        z = f_ref[...] + bf_ref[...]
        d_f = jnp.where(lane < FOX_HEADS, d_logf / (1.0 + jnp.exp(z)), 0.0)
        df_ref[...] = d_f.astype(BF16)
        dbf_ref[...] += jnp.sum(d_f, axis=0, keepdims=True)

    rev = lambda i: n_tiles - 1 - i
    col_blk = lambda j: pl.BlockSpec((TM, PIECE), lambda i: (rev(i), j))
    head_blk = lambda w: pl.BlockSpec((FOX_HEADS, TM, w), lambda i: (0, rev(i), 0))
    out_piece = pl.BlockSpec((TM, PIECE), lambda i: (rev(i), 0))
    return pl.pallas_call(
        body,
        name="fox_post",
        grid=(n_tiles,),
        in_specs=[col_blk(1), col_blk(2), pl.BlockSpec((TM, 128), lambda i: (rev(i), MY_F_OFF // 128)),
                  _full((1, 128)), _full((1, PIECE)), _full((1, PIECE)), _full((PIECE, PIECE)),
                  head_blk(128), head_blk(128), head_blk(HEAD_DIM)],
        out_specs=[out_piece, out_piece, out_piece, pl.BlockSpec((TM, 128), lambda i: (rev(i), 0)),
                   _full((1, HEAD_DIM)), _full((1, HEAD_DIM)), _full((1, 128))],
        out_shape=[jax.ShapeDtypeStruct((s_len, PIECE), BF16)] * 3 + [
            jax.ShapeDtypeStruct((s_len, 128), BF16), jax.ShapeDtypeStruct((1, HEAD_DIM), F32),
            jax.ShapeDtypeStruct((1, HEAD_DIM), F32), jax.ShapeDtypeStruct((1, 128), F32)],
        scratch_shapes=[pltpu.VMEM((1, 128), F32)],
        compiler_params=_params(("arbitrary",)),
    )(proj, proj, proj, bf_pad, gq, gk, seg, dq_aug, dk_aug, dv_h)


def _mem_bwd(mem, g, w_kv, gk, dkm, dvm):
    def body(mem_ref, g_ref, w_ref, gk_ref, dkm_ref, dvm_ref, dw_ref, dg_ref, dgk_ref, dkv_ref):
        m = mem_ref[...]
        r = _rms(m)
        a = m * r
        mn16 = (a * g_ref[...]).astype(BF16)
        kv = _dot(mn16, w_ref[...])
        dgk = jnp.zeros((N_MEM, HEAD_DIM), F32)
        for h in range(MEM_HEADS):
            sl = slice(HEAD_DIM * h, HEAD_DIM * (h + 1))
            kh = kv[:, sl]
            rk = _rms(kh)
            dk, dg_rows = _rms_bwd(kh * rk, rk, gk_ref[...], dkm_ref[:, sl])
            dkv_ref[:, sl] = dk.astype(BF16)
            dgk = dgk + dg_rows
        dkv_ref[:, 256:512] = dvm_ref[...].astype(BF16)
        dgk_ref[...] = jnp.sum(dgk, axis=0, keepdims=True)
        dkv16 = dkv_ref[...]
        dw_ref[...] = _dot_tn(mn16, dkv16).astype(BF16)
        dg_ref[...] = jnp.sum(_dot_nt(dkv16, w_ref[...]) * a, axis=0, keepdims=True)

    return pl.pallas_call(
        body,
        name="mem_bwd",
        out_shape=(jax.ShapeDtypeStruct((D_MODEL, 512), BF16), jax.ShapeDtypeStruct((1, D_MODEL), F32),
                   jax.ShapeDtypeStruct((1, HEAD_DIM), F32)),
        in_specs=[pl.BlockSpec(memory_space=pltpu.VMEM)] * 6,
        out_specs=(pl.BlockSpec(memory_space=pltpu.VMEM),) * 3,
        scratch_shapes=[pltpu.VMEM((N_MEM, 512), BF16)],
        compiler_params=pltpu.CompilerParams(vmem_limit_bytes=VMEM_LIMIT),
    )(mem, g, w_kv, gk, dkm, dvm)


def _grad_x_exchange(pieces, d_f, w_my, x, norm_g, d_out, dw_in, land_kv, land_out, d_mem_g, d_scale, d_bf, d_gq, d_gk, d_gqm,
                     d_gkm, dwp_bd, loss_blk):
    s_len = x.shape[0]
    n_steps = s_len // TM
    step2, step3 = n_steps // 4, (3 * n_steps) // 4
    half = D_MODEL // 2

    def body(p0, p1, p2, p3, p4, p5, df_ref, x_ref, dout_ref, w_ref, g_ref, in_ref, lkv_ref, lout_ref,
             dmg, dsc, dbf, dgq, dgk, dgqm, dgkm, dwp, loss_ref,
             gx_ref, rin_ref, rkv_ref, rout_ref, sred_ref,
             dng, land1, send2a, send2b, keep2a, keep2b, land2a, land2b, send3a, send3b, land3a, land3b, sb_ref, sland,
             send_sems, recv_sems):
        i = pl.program_id(0)
        x_, y_, c_, me = _my_place()
        sibling, x_nbr, y_nbr = (x_, y_, 1 - c_), (1 - x_, y_, c_), (x_, 1 - y_, c_)

        def rcopy(src, dst, sem, to):
            return pltpu.make_async_remote_copy(src_ref=src, dst_ref=dst, send_sem=send_sems.at[sem], recv_sem=recv_sems.at[sem],
                                                device_id=to, device_id_type=MESH)

        small = []
        for k in range(1, N_DEV):
            px, py, pc = x_ ^ (k >> 2), y_ ^ ((k >> 1) & 1), c_ ^ (k & 1)
            small.append(rcopy(sb_ref, sland.at[me], k - 1, (px, py, pc)))
        stage1 = [rcopy(in_ref.at[2 * k + 1 - c_], land1.at[k], 7 + k, sibling) for k in range(4)]
        stage2 = [rcopy(send2a.at[j], land2a.at[j], 11 + j, x_nbr) for j in range(2)]
        stage2 += [rcopy(send2b.at[j], land2b.at[j], 13 + j, y_nbr) for j in range(2)]
        stage3 = [rcopy(send3a, land3a, 15, y_nbr), rcopy(send3b, land3b, 16, x_nbr)]
        top, bot = slice(0, half), slice(half, D_MODEL)

        @pl.when(i == 0)
        def _():
            dng[...] = jnp.zeros((1, D_MODEL), F32)
            for cp in stage1:
                cp.start()

        @pl.when(i == step2)
        def _():
            for cp in stage1:
                cp.wait_recv()

            def chip_sum(k, rows):
                return in_ref[2 * k + c_, rows, :].astype(F32) + land1[k, rows, :].astype(F32)

            for j in range(2):
                send2a[j] = chip_sum(2 * (1 - x_) + j, top).astype(BF16)
                keep2a[j] = chip_sum(2 * x_ + j, top)
                send2b[j] = chip_sum(2 * j + 1 - y_, bot).astype(BF16)
                keep2b[j] = chip_sum(2 * j + y_, bot)
            for cp in stage2:
                cp.start()

        @pl.when(i == step3)
        def _():
            for cp in stage2:
                cp.wait_recv()
            for j in range(2):
                keep2a[j] = keep2a[j] + land2a[j].astype(F32)
                keep2b[j] = keep2b[j] + land2b[j].astype(F32)
            send3a[...] = keep2a[1 - y_].astype(BF16)
            send3b[...] = keep2b[1 - x_].astype(BF16)
            for cp in stage3:
                cp.start()

        dh = _dot_nt(df_ref[...], w_ref[:, MY_F_OFF:MY_WIDTH])
        for j, p in enumerate((p0, p1, p2, p3, p4, p5)):
            dh = dh + _dot_nt(p[...], w_ref[:, PIECE * j:PIECE * (j + 1)])
        xv = x_ref[...]
        r = _rms(xv)
        dx, dg_rows = _rms_bwd(xv * r, r, g_ref[...], dh)
        gx_ref[...] = dout_ref[...] + dx
        dng[...] += jnp.sum(dg_rows, axis=0, keepdims=True)

        @pl.when(i == n_steps - 1)
        def _():
            sb_ref[...] = jnp.zeros((SB_ROWS, SB_W), F32)
            for k in range(4):
                sb_ref[SB_NORM_G + k:SB_NORM_G + k + 1, :] = dng[:, SB_W * k:SB_W * (k + 1)]
                sb_ref[SB_MEM_NORM_G + k:SB_MEM_NORM_G + k + 1, :] = dmg[:, SB_W * k:SB_W * (k + 1)]
            sb_ref[SB_POOL_SCALE:SB_POOL_SCALE + 1, :] = dsc[...]
            sb_ref[SB_B_F:SB_B_F + 1, 0:128] = dbf[...]
            for row, ref in ((SB_FOX_Q, dgq), (SB_FOX_K, dgk), (SB_MEM_Q, dgqm), (SB_MEM_K, dgkm)):
                sb_ref[row:row + 1, 0:HEAD_DIM] = ref[...]
            sb_ref[SB_LOSS:SB_LOSS + 1, 0:128] = loss_ref[0:1, :]
            for grp in range(4):
                sl = slice(64 * grp, 64 * (grp + 1))
                sb_ref[SB_W_POOL:SB_W_POOL + 64, sl] = dwp[sl, sl]
            for cp in small:
                cp.start()
            sland[me] = sb_ref[...]
            for cp in stage3:
                cp.wait_recv()
            rin_ref[top, :] = keep2a[y_] + land3a[...].astype(F32)
            rin_ref[bot, :] = keep2b[x_] + land3b[...].astype(F32)
            for cp in small:
                cp.wait_recv()
            for cp in small + stage1 + stage2 + stage3:
                cp.wait_send()
            for land, red in ((lkv_ref, rkv_ref), (lout_ref, rout_ref), (sland, sred_ref)):
                acc = land[0].astype(F32)
                for d in range(1, N_DEV):
                    acc = acc + land[d].astype(F32)
                red[...] = acc

    piece = pl.BlockSpec((TM, PIECE), lambda i: (i, 0))
    row = pl.BlockSpec((TM, D_MODEL), lambda i: (i, 0))
    vmem = pl.BlockSpec(memory_space=pltpu.VMEM)
    half_chunk = lambda n, dt: pltpu.VMEM((n, half, IN_CHUNK), dt)
    whole = (w_my, norm_g, dw_in, land_kv, land_out, d_mem_g, d_scale, d_bf, d_gq, d_gk, d_gqm, d_gkm, dwp_bd, loss_blk)
    return pl.pallas_call(
        body,
        name="grad_x_exchange",
        grid=(n_steps,),
        in_specs=[piece] * 6 + [pl.BlockSpec((TM, 128), lambda i: (i, 0)), row, row] + [vmem] * len(whole),
        out_specs=[row, vmem, vmem, vmem, vmem],
        out_shape=[jax.ShapeDtypeStruct((s_len, D_MODEL), F32), jax.ShapeDtypeStruct((D_MODEL, IN_CHUNK), F32),
                   jax.ShapeDtypeStruct((128, 512), F32), jax.ShapeDtypeStruct((128, D_MODEL), F32),
                   jax.ShapeDtypeStruct((SB_ROWS, SB_W), F32)],
        scratch_shapes=[
            pltpu.VMEM((1, D_MODEL), F32),
            pltpu.VMEM((4, D_MODEL, IN_CHUNK), BF16),
            half_chunk(2, BF16), half_chunk(2, BF16), half_chunk(2, F32), half_chunk(2, F32), half_chunk(2, BF16), half_chunk(2, BF16),
            pltpu.VMEM((half, IN_CHUNK), BF16), pltpu.VMEM((half, IN_CHUNK), BF16),
            pltpu.VMEM((half, IN_CHUNK), BF16), pltpu.VMEM((half, IN_CHUNK), BF16),
            pltpu.VMEM((SB_ROWS, SB_W), F32),
            pltpu.VMEM((N_DEV, SB_ROWS, SB_W), F32),
            pltpu.SemaphoreType.DMA((17,)),
            pltpu.SemaphoreType.DMA((17,)),
        ],
        compiler_params=_params(("arbitrary",)),
    )(*pieces, d_f, x, d_out, *whole)


def _grad_w_in(h16, pieces, d_f):
    s_len = h16.shape[0]
    tk = 512

    def body(h_ref, p0, p1, p2, p3, p4, p5, df_ref, out_ref, dw_ref):
        @pl.when(pl.program_id(0) == 0)
        def _():
            dw_ref[...] = jnp.zeros((D_MODEL, MY_WIDTH), F32)

        h = h_ref[...]
        for j, p in enumerate((p0, p1, p2, p3, p4, p5)):
            dw_ref[:, PIECE * j:PIECE * (j + 1)] += _dot_tn(h, p[...])
        dw_ref[:, MY_F_OFF:MY_WIDTH] += _dot_tn(h, df_ref[...])

        @pl.when(pl.program_id(0) == pl.num_programs(0) - 1)
        def _():
            for d in range(N_DEV):
                parts = [dw_ref[:, start:start + width] for start, width in _chunk_segments(d)]
                out_ref[d] = (parts[0] if len(parts) == 1 else jnp.concatenate(parts, axis=1)).astype(BF16)

    piece = pl.BlockSpec((tk, PIECE), lambda i: (i, 0))
    return pl.pallas_call(
        body,
        name="grad_w_in",
        grid=(s_len // tk,),
        in_specs=[pl.BlockSpec((tk, D_MODEL), lambda i: (i, 0))] + [piece] * 6 + [pl.BlockSpec((tk, 128), lambda i: (i, 0))],
        out_specs=_full((N_DEV, D_MODEL, IN_CHUNK)),
        out_shape=jax.ShapeDtypeStruct((N_DEV, D_MODEL, IN_CHUNK), BF16),
        scratch_shapes=[pltpu.VMEM((D_MODEL, MY_WIDTH), F32)],
        compiler_params=_params(("arbitrary",)),
    )(h16, *pieces, d_f)


def _adamw(w, g, m, v):
    m = ADAM_B1 * m + (1.0 - ADAM_B1) * g
    v = ADAM_B2 * v + (1.0 - ADAM_B2) * (g * g)
    m_hat = m / (1.0 - ADAM_B1 ** ADAM_STEP)
    v_hat = v / (1.0 - ADAM_B2 ** ADAM_STEP)
    return -ADAM_LR * (m_hat / (jnp.sqrt(v_hat) + ADAM_EPS) + ADAM_WD * w), m, v


PARAM_ORDER = ("norm_g", "w_in", "b_f", "w_pool", "pool_scale", "fox_q_g", "fox_k_g", "mem_norm_g", "w_mem_kv", "mem_q_g", "mem_k_g", "w_out")


def _update(red_in, red_kv, red_out, sred, params):
    def body(rin_ref, rkv_ref, rout_ref, sred_ref, *refs):
        ins, outs = refs[:3 * len(PARAM_ORDER)], refs[3 * len(PARAM_ORDER):]
        wide = lambda row: jnp.concatenate([sred_ref[row + k:row + k + 1, :] for k in range(4)], axis=1)
        head = lambda row: sred_ref[row:row + 1, 0:HEAD_DIM]
        grads = {
            "norm_g": lambda: wide(SB_NORM_G), "w_in": lambda: rin_ref[...], "b_f": lambda: sred_ref[SB_B_F:SB_B_F + 1, 0:FOX_HEADS],
            "pool_scale": lambda: sred_ref[SB_POOL_SCALE:SB_POOL_SCALE + 1, :], "fox_q_g": lambda: head(SB_FOX_Q),
            "fox_k_g": lambda: head(SB_FOX_K), "mem_norm_g": lambda: wide(SB_MEM_NORM_G), "w_mem_kv": lambda: rkv_ref[...],
            "mem_q_g": lambda: head(SB_MEM_Q), "mem_k_g": lambda: head(SB_MEM_K), "w_out": lambda: rout_ref[...],
        }
        for p, name in enumerate(PARAM_ORDER):
            w_ref, m_ref, v_ref = ins[3 * p:3 * p + 3]
            g_out, d_out, m_out, v_out = outs[4 * p:4 * p + 4]
            if name == "w_pool":
                for grp in range(4):
                    g = sred_ref[SB_W_POOL:SB_W_POOL + 64, 64 * grp:64 * (grp + 1)]
                    delta, m_new, v_new = _adamw(w_ref[grp], g, m_ref[grp], v_ref[grp])
                    g_out[grp], d_out[grp], m_out[grp], v_out[grp] = g, delta, m_new, v_new
            else:
                g = grads[name]()
                delta, m_new, v_new = _adamw(w_ref[...], g, m_ref[...], v_ref[...])
                g_out[...], d_out[...], m_out[...], v_out[...] = g, delta, m_new, v_new

    flat = [t for name in PARAM_ORDER for t in params[name]]
    shapes = [params[name][0].shape for name in PARAM_ORDER for _ in range(4)]
    outs = pl.pallas_call(
        body,
        name="adamw_update",
        out_shape=tuple(jax.ShapeDtypeStruct(s, F32) for s in shapes),
        in_specs=[pl.BlockSpec(memory_space=pltpu.VMEM)] * (4 + len(flat)),
        out_specs=(pl.BlockSpec(memory_space=pltpu.VMEM),) * len(shapes),
        compiler_params=pltpu.CompilerParams(vmem_limit_bytes=VMEM_LIMIT),
    )(red_in, red_kv, red_out, sred, *flat)
    return {name: outs[4 * p:4 * p + 4] for p, name in enumerate(PARAM_ORDER)}


def kernel(x, mem, norm_g, w_in, b_f, w_pool, pool_scale, fox_q_g, fox_k_g, mem_norm_g, w_mem_kv, mem_q_g, mem_k_g, w_out, loss_target, m_norm_g, m_w_in, m_b_f, m_w_pool, m_pool_scale, m_fox_q_g, m_fox_k_g, m_mem_norm_g, m_w_mem_kv, m_mem_q_g, m_mem_k_g, m_w_out, v_norm_g, v_w_in, v_b_f, v_w_pool, v_pool_scale, v_fox_q_g, v_fox_k_g, v_mem_norm_g, v_w_mem_kv, v_mem_q_g, v_mem_k_g, v_w_out):
    given = dict(norm_g=(norm_g, m_norm_g, v_norm_g), w_in=(w_in, m_w_in, v_w_in), b_f=(b_f, m_b_f, v_b_f),
                 w_pool=(w_pool, m_w_pool, v_w_pool), pool_scale=(pool_scale, m_pool_scale, v_pool_scale),
                 fox_q_g=(fox_q_g, m_fox_q_g, v_fox_q_g), fox_k_g=(fox_k_g, m_fox_k_g, v_fox_k_g),
                 mem_norm_g=(mem_norm_g, m_mem_norm_g, v_mem_norm_g), w_mem_kv=(w_mem_kv, m_w_mem_kv, v_w_mem_kv),
                 mem_q_g=(mem_q_g, m_mem_q_g, v_mem_q_g), mem_k_g=(mem_k_g, m_mem_k_g, v_mem_k_g), w_out=(w_out, m_w_out, v_w_out))
    params = {name: tuple(t[0] if t.ndim > 2 else t for t in wmv) for name, wmv in given.items()}
    x2, mem2, target2 = x[0], mem[0], loss_target[0]

    seg = jnp.asarray(np.kron(np.eye(FOX_HEADS), np.full((HEAD_DIM, HEAD_DIM), 1.0 / HEAD_DIM)), BF16)
    w_my, w_kv16, w_out16, wp_bd, bf_pad, gq8, gk8, gqm4 = _gather_w_in(
        params["w_in"][0], params["w_mem_kv"][0], params["w_out"][0], params["w_pool"][0], b_f, fox_q_g, fox_k_g, mem_q_g)
    proj, h16 = _fwd_proj(x2, norm_g, w_my)
    q_aug, k_aug, v_h, kt_aug, vt_aug = _fox_prep(proj, bf_pad, gq8, gk8, seg)
    olse, lse_rows, w_kv_all, w_out_all = _fox_fwd(q_aug, k_aug, vt_aug, w_kv16, w_out16)
    km, vm = _mem_prep(mem2, mem_norm_g, w_kv_all, mem_k_g)
    mixed = _mix_fwd(proj, olse, km, vm, wp_bd, pool_scale, gqm4, seg)
    d_out, d_mixed, dw_out, loss_blk = _out_loss(mixed, x2, target2, w_out_all)

    d_ua, d_gb, d_qm, d_o, delta_rows, dwp_bd, d_scale, dkm, dvm, d_gqm = _mix_bwd(proj, olse, d_mixed, km, vm, wp_bd, pool_scale,
                                                                                    gqm4, seg)
    dw_kv, d_mem_g, d_gkm = _mem_bwd(mem2, mem_norm_g, w_kv_all, mem_k_g, dkm, dvm)
    dq_aug, dk_aug, dv_h, land_kv, land_out = _fox_bwd(q_aug, k_aug, kt_aug, v_h, d_o, lse_rows, delta_rows, dw_kv, dw_out)
    d_q, d_k, d_v, d_f, d_gq, d_gk, d_bf = _fox_post(proj, bf_pad, gq8, gk8, seg, dq_aug, dk_aug, dv_h)
    pieces = (d_ua, d_q, d_k, d_v, d_gb, d_qm)
    dw_in = _grad_w_in(h16, pieces, d_f)
    grad_x, red_in, red_kv, red_out, sred = _grad_x_exchange(pieces, d_f, w_my, x2, norm_g, d_out, dw_in, land_kv, land_out, d_mem_g,
                                                             d_scale, d_bf, d_gq, d_gk, d_gqm, d_gkm, dwp_bd, loss_blk)
    new = _update(red_in, red_kv, red_out, sred, params)
    result = [sred[SB_LOSS, 0], grad_x[None]]
    for kind in range(4):
        for name in PARAM_ORDER:
            out = new[name][kind]
            result.append(out[None] if given[name][0].ndim > 2 else out)
    return tuple(result)
```

```python
import functools

import jax
import jax.numpy as jnp
import numpy as np
from jax import lax
from jax.experimental import pallas as pl
from jax.experimental.pallas import tpu as pltpu

F32 = jnp.float32
BF16 = jnp.bfloat16
MESH = pl.DeviceIdType.MESH

N_DEV = 8
D_MODEL = 1024
HEAD_DIM = 64
FOX_HEADS = 8
MEM_HEADS = 4
N_MEM = 256
POOL_WIDTH = 256
EPS = 1e-6
IN_WIDTH = 3080
IN_CHUNK = IN_WIDTH // N_DEV
F_OFF = 2048
MY_WIDTH = 3200
MY_F_OFF = 3072
PIECE = 512
TM = 256
TMM = 512
TA = 256
HB = 8
HB_FWD = 8
AUG_ROWS = 72
HALO = 16
VMEM_LIMIT = 56 * 1024 * 1024

ADAM_LR = 0.001
ADAM_B1 = 0.9
ADAM_B2 = 0.999
ADAM_EPS = 1e-08
ADAM_WD = 0.01
ADAM_STEP = 10


def _dot(a, b):
    return jnp.dot(a, b, preferred_element_type=F32)


def _dot_nt(a, b):
    return lax.dot_general(a, b, (((1,), (1,)), ((), ())), preferred_element_type=F32)


def _dot_tn(a, b):
    return lax.dot_general(a, b, (((0,), (0,)), ((), ())), preferred_element_type=F32)


def _sigmoid(g):
    return 0.5 + 0.5 * jnp.tanh(0.5 * g)


def _split3(v):
    hi = v.astype(BF16)
    r1 = v - hi.astype(F32)
    mid = r1.astype(BF16)
    lo = (r1 - mid.astype(F32)).astype(BF16)
    return hi, mid, lo


def _rms(v):
    return lax.rsqrt(jnp.mean(v * v, axis=-1, keepdims=True) + EPS)


def _rms_bwd(a, r, g, dy):
    da = dy * g
    return r * (da - a * jnp.mean(da * a, axis=-1, keepdims=True)), dy * a


def _head_mean(z, seg):
    hi = z.astype(BF16)
    lo = (z - hi.astype(F32)).astype(BF16)
    if z.shape[1] == 256:
        return _dot(hi, seg) + _dot(lo, seg)
    half = seg[0:256, 0:256]
    return jnp.concatenate([_dot(hi[:, 0:256], half) + _dot(lo[:, 0:256], half),
                            _dot(hi[:, 256:512], half) + _dot(lo[:, 256:512], half)], axis=1)


def _head_rms(v, seg):
    return lax.rsqrt(_head_mean(v * v, seg) + EPS)


def _head_rms_bwd(a, r, g, dy, seg):
    da = dy * g
    return r * (da - a * _head_mean(da * a, seg)), dy * a


def _fold_heads(row, n_heads):
    out = row[:, 0:HEAD_DIM]
    for h in range(1, n_heads):
        out = out + row[:, HEAD_DIM * h:HEAD_DIM * (h + 1)]
    return out


def _params(sem, limit=VMEM_LIMIT):
    return pltpu.CompilerParams(dimension_semantics=sem, vmem_limit_bytes=limit)


def _full(shape):
    return pl.BlockSpec(shape, lambda *_: (0,) * len(shape))


def _chunk_segments(d):
    runs = []
    for lo, hi, shift in ((0, F_OFF, 0), (F_OFF, F_OFF + FOX_HEADS, MY_F_OFF - F_OFF), (F_OFF + FOX_HEADS, IN_WIDTH, -FOX_HEADS)):
        a, b = max(lo, IN_CHUNK * d), min(hi, IN_CHUNK * (d + 1))
        if a < b:
            runs.append((a + shift, b - a))
    return runs


def _my_place():
    x, y, c = lax.axis_index("x"), lax.axis_index("y"), lax.axis_index("c")
    return x, y, c, 4 * x + 2 * y + c


def _shard_rows(dev):
    return pl.ds(pl.multiple_of(128 * dev, 128), 128)


def _gather_w_in(w_in, w_kv, w_out, w_pool, b_f, gq, gk, gqm):
    def body(win_ref, wkv_ref, wout_ref, wp_ref, bf_ref, gq_ref, gk_ref, gqm_ref, wmy_ref, kv16_ref, out16_ref, wpbd_ref, bfpad_ref,
             gq8_ref, gk8_ref, gqm4_ref, in_all, pay_in, win_rows, send_sems, recv_sems, load_sem):
        x, y, c, me = _my_place()
        here, sibling = (x, y, c), (x, y, 1 - c)
        x_chip, y_chip, far_chip = (1 - x, y), (x, 1 - y), (1 - x, 1 - y)
        relay_from = (x ^ (1 - c), y ^ c)
        relay_to = (x ^ c, y ^ (1 - c))

        load = pltpu.make_async_copy(win_ref, win_rows, load_sem)
        load.start()
        load.wait()
        for j in range(8):
            cols = jnp.concatenate([win_rows[pl.ds(j, IN_CHUNK, stride=8), :], jnp.zeros((512 - IN_CHUNK, 128), F32)], axis=0)
            pay_in[128 * j:128 * (j + 1), :] = cols.T[:, 0:IN_CHUNK].astype(BF16)

        def copy(k, block, to, own=False):
            px, py, pc = block
            dst = in_all.at[4 * px + 2 * py + pc]
            return pltpu.make_async_remote_copy(src_ref=pay_in if own else dst, dst_ref=dst, send_sem=send_sems.at[k],
                                                recv_sem=recv_sems.at[k], device_id=to, device_id_type=MESH)

        first = [copy(0, here, sibling, own=True), copy(1, here, (*x_chip, c), own=True), copy(2, here, (*y_chip, c), own=True)]
        for cp in first:
            cp.start()
        in_all[me] = pay_in[...]
        kv16_ref[...] = wkv_ref[...].astype(BF16)
        out16_ref[...] = wout_ref[...].astype(BF16)
        wpbd_ref[...] = jnp.zeros((POOL_WIDTH, POOL_WIDTH), BF16)
        for grp in range(4):
            sl = slice(64 * grp, 64 * (grp + 1))
            wpbd_ref[sl, sl] = wp_ref[grp].astype(BF16)
        bfpad_ref[...] = jnp.zeros((1, 128), F32)
        bfpad_ref[:, 0:FOX_HEADS] = bf_ref[...]
        gq8_ref[...] = jnp.concatenate([gq_ref[...]] * FOX_HEADS, axis=1)
        gk8_ref[...] = jnp.concatenate([gk_ref[...]] * FOX_HEADS, axis=1)
        gqm4_ref[...] = jnp.concatenate([gqm_ref[...]] * MEM_HEADS, axis=1)
        copy(1 + c, (*relay_from, c), here).wait_recv()
        passed = [copy(3, (*relay_from, c), (*relay_to, c)), copy(4, (*relay_from, c), sibling)]
        for cp in passed:
            cp.start()
        copy(2 - c, (*relay_to, c), here).wait_recv()
        passed.append(copy(5, (*relay_to, c), sibling))
        passed[-1].start()
        copy(3, (*far_chip, c), here).wait_recv()
        passed.append(copy(6, (*far_chip, c), sibling))
        passed[-1].start()
        copy(0, sibling, here).wait_recv()
        for k, chip in ((4, relay_to), (5, relay_from), (6, far_chip)):
            copy(k, (*chip, 1 - c), here).wait_recv()
        for cp in first + passed:
            cp.wait_send()

        for r0 in range(0, D_MODEL, 256):
            ch = [in_all[d, r0:r0 + 256, :].astype(F32) for d in range(N_DEV)]
            lo = F_OFF - 5 * IN_CHUNK
            full = jnp.concatenate(ch[:5] + [ch[5][:, :lo], ch[5][:, lo + FOX_HEADS:], ch[6], ch[7], ch[5][:, lo:lo + FOX_HEADS],
                                             jnp.zeros((256, MY_WIDTH - IN_WIDTH), F32)], axis=1)
            wmy_ref[r0:r0 + 256, :] = full.astype(BF16)

    return pl.pallas_call(
        body,
        name="gather_w_in",
        out_shape=(jax.ShapeDtypeStruct((D_MODEL, MY_WIDTH), BF16), jax.ShapeDtypeStruct((128, 512), BF16),
                   jax.ShapeDtypeStruct((128, D_MODEL), BF16), jax.ShapeDtypeStruct((POOL_WIDTH, POOL_WIDTH), BF16),
                   jax.ShapeDtypeStruct((1, 128), F32), jax.ShapeDtypeStruct((1, PIECE), F32), jax.ShapeDtypeStruct((1, PIECE), F32),
                   jax.ShapeDtypeStruct((1, 256), F32)),
        in_specs=[pl.BlockSpec(memory_space=pl.ANY)] + [pl.BlockSpec(memory_space=pltpu.VMEM)] * 7,
        out_specs=(pl.BlockSpec(memory_space=pltpu.VMEM),) * 8,
        scratch_shapes=[
            pltpu.VMEM((N_DEV, D_MODEL, IN_CHUNK), BF16),
            pltpu.VMEM((D_MODEL, IN_CHUNK), BF16),
            pltpu.VMEM((IN_CHUNK * 8, 128), F32),
            pltpu.SemaphoreType.DMA((7,)),
            pltpu.SemaphoreType.DMA((7,)),
            pltpu.SemaphoreType.DMA,
        ],
        compiler_params=pltpu.CompilerParams(vmem_limit_bytes=VMEM_LIMIT),
    )(w_in, w_kv, w_out, w_pool, b_f, gq, gk, gqm)


def _row_block_exchange(kv_ref, out_ref, kv_dst, out_dst, send_sems, recv_sems, local_sems, gather):
    x, y, c, me = _my_place()
    remote = []
    for k in range(1, N_DEV):
        px, py, pc = x ^ (k >> 2), y ^ ((k >> 1) & 1), c ^ (k & 1)
        peer = 4 * px + 2 * py + pc
        for fam, (src, dst) in enumerate(((kv_ref, kv_dst), (out_ref, out_dst))):
            remote.append(pltpu.make_async_remote_copy(
                src_ref=src if gather else src.at[_shard_rows(peer)], dst_ref=dst.at[_shard_rows(me)] if gather else dst.at[me],
                send_sem=send_sems.at[7 * fam + k - 1], recv_sem=recv_sems.at[7 * fam + k - 1],
                device_id=(px, py, pc), device_id_type=MESH))
    local = [pltpu.make_async_copy(src if gather else src.at[_shard_rows(me)], dst.at[_shard_rows(me)] if gather else dst.at[me],
                                   local_sems.at[fam]) for fam, (src, dst) in enumerate(((kv_ref, kv_dst), (out_ref, out_dst)))]
    return remote, local


SB_ROWS, SB_W = 80, 256
SB_NORM_G, SB_MEM_NORM_G, SB_POOL_SCALE, SB_B_F, SB_FOX_Q, SB_FOX_K, SB_MEM_Q, SB_MEM_K, SB_LOSS, SB_W_POOL = 0, 4, 8, 9, 10, 11, 12, 13, 14, 16


def _fwd_proj(x, norm_g, w_my):
    s_len = x.shape[0]

    def body(x_ref, g_ref, w_ref, proj_ref, h_ref):
        xv = x_ref[...]
        h = (xv * _rms(xv) * g_ref[...]).astype(BF16)
        h_ref[...] = h
        proj_ref[...] = _dot(h, w_ref[...])

    return pl.pallas_call(
        body,
        name="fwd_proj",
        grid=(s_len // TMM,),
        in_specs=[pl.BlockSpec((TMM, D_MODEL), lambda i: (i, 0)), _full((1, D_MODEL)), _full((D_MODEL, MY_WIDTH))],
        out_specs=[pl.BlockSpec((TMM, MY_WIDTH), lambda i: (i, 0)), pl.BlockSpec((TMM, D_MODEL), lambda i: (i, 0))],
        out_shape=[jax.ShapeDtypeStruct((s_len, MY_WIDTH), F32), jax.ShapeDtypeStruct((s_len, D_MODEL), BF16)],
        compiler_params=_params(("parallel",)),
    )(x, norm_g, w_my)


def _log_sigmoid(z):
    return jnp.minimum(z, 0.0) - jnp.log(1.0 + jnp.exp(-jnp.abs(z)))


def _forget_lane_maps():
    to_q = np.zeros((128, FOX_HEADS * 128), np.float32)
    to_k = np.zeros((128, FOX_HEADS * 128), np.float32)
    for h in range(FOX_HEADS):
        for term in range(3):
            to_q[8 * term + h, 128 * h + 64 + term] = 1.0
            to_q[24, 128 * h + 67 + term] = 1.0
            to_k[24, 128 * h + 64 + term] = 1.0
            to_k[8 * term + h, 128 * h + 67 + term] = -1.0
    return jnp.asarray(to_q, BF16), jnp.asarray(to_k, BF16)


def _fox_prep(proj, bf_pad, gq, gk, seg):
    s_len = proj.shape[0]
    to_q, to_k = _forget_lane_maps()

    def body(q_ref, k_ref, v_ref, f_ref, bf_ref, gq_ref, gk_ref, seg_ref, eq_ref, ek_ref,
             qa_ref, ka_ref, vh_ref, kt_ref, vt_ref, carry_ref):
        @pl.when(pl.program_id(0) == 0)
        def _():
            carry_ref[...] = jnp.zeros((1, 128), F32)

        lane = lax.broadcasted_iota(jnp.int32, (TM, 128), 1)
        logf = jnp.where(lane < FOX_HEADS, _log_sigmoid(f_ref[...] + bf_ref[...]), 0.0)
        row = lax.broadcasted_iota(jnp.int32, (TM, TM), 0)
        col = lax.broadcasted_iota(jnp.int32, (TM, TM), 1)
        tri = jnp.where(col <= row, 1.0, 0.0).astype(BF16)
        hi, mid, lo = _split3(logf)
        cum = _dot(tri, hi) + _dot(tri, mid) + _dot(tri, lo) + carry_ref[...]
        carry_ref[...] = cum[TM - 1:TM, :]
        f_hi, f_mid, f_lo = [t.astype(F32) for t in _split3(cum)]
        packed = (f_hi + pltpu.roll(f_mid, 8, 1) + pltpu.roll(f_lo, 16, 1) + jnp.where(lane == 24, 1.0, 0.0)).astype(BF16)
        extra_q = _dot(packed, eq_ref[...])
        extra_k = _dot(packed, ek_ref[...])
        one_at_64 = jnp.where(lane == 64, 1.0, 0.0)
        zeros = jnp.zeros((TM, HEAD_DIM), F32)
        q_all = q_ref[...]
        k_all = k_ref[...]
        qn_all = q_all * _head_rms(q_all, seg_ref[...]) * gq_ref[...] * 0.125
        kn_all = k_all * _head_rms(k_all, seg_ref[...]) * gk_ref[...]
        for h in range(FOX_HEADS):
            sl = slice(HEAD_DIM * h, HEAD_DIM * (h + 1))
            tile = slice(128 * h, 128 * (h + 1))
            qa = jnp.where(lane < 64, jnp.concatenate([qn_all[:, sl], zeros], axis=1), extra_q[:, tile])
            ka = jnp.where(lane < 64, jnp.concatenate([kn_all[:, sl], zeros], axis=1), extra_k[:, tile])
            vh = v_ref[:, sl]
            v_aug = jnp.where(lane < 64, jnp.concatenate([vh, zeros], axis=1), one_at_64)
            qa_ref[h] = qa.astype(BF16)
            ka_ref[h] = ka.astype(BF16)
            vh_ref[h] = vh.astype(BF16)
            kt_ref[h, 0] = ka.T.astype(BF16)
            vt_ref[h, 0] = v_aug.T.astype(BF16)

    col_blk = lambda j: pl.BlockSpec((TM, PIECE), lambda i: (i, j))
    head_blk = lambda w: pl.BlockSpec((FOX_HEADS, TM, w), lambda i: (0, i, 0))
    tile_blk = pl.BlockSpec((FOX_HEADS, 1, 128, TM), lambda i: (0, i, 0, 0))
    tiled = jax.ShapeDtypeStruct((FOX_HEADS, s_len // TM, 128, TM), BF16)
    return pl.pallas_call(
        body,
        name="fox_prep",
        grid=(s_len // TM,),
        in_specs=[col_blk(1), col_blk(2), col_blk(3), pl.BlockSpec((TM, 128), lambda i: (i, MY_F_OFF // 128)),
                  _full((1, 128)), _full((1, PIECE)), _full((1, PIECE)), _full((PIECE, PIECE)),
                  _full((128, FOX_HEADS * 128)), _full((128, FOX_HEADS * 128))],
        out_specs=[head_blk(128), head_blk(128), head_blk(HEAD_DIM), tile_blk, tile_blk],
        out_shape=[jax.ShapeDtypeStruct((FOX_HEADS, s_len, 128), BF16), jax.ShapeDtypeStruct((FOX_HEADS, s_len, 128), BF16),
                   jax.ShapeDtypeStruct((FOX_HEADS, s_len, HEAD_DIM), BF16), tiled, tiled],
        scratch_shapes=[pltpu.VMEM((1, 128), F32)],
        compiler_params=_params(("arbitrary",)),
    )(proj, proj, proj, proj, bf_pad, gq, gk, seg, to_q, to_k)


def _mem_prep(mem, g, w_kv, gk):
    def body(mem_ref, g_ref, w_ref, gk_ref, km_ref, vm_ref):
        m = mem_ref[...]
        kv = _dot((m * _rms(m) * g_ref[...]).astype(BF16), w_ref[...])
        for h in range(MEM_HEADS):
            sl = slice(HEAD_DIM * h, HEAD_DIM * (h + 1))
            kh = kv[:, sl]
            km_ref[:, sl] = (kh * _rms(kh) * gk_ref[...]).astype(BF16)
        vm_ref[...] = kv[:, 256:512].astype(BF16)

    return pl.pallas_call(
        body,
        name="mem_prep",
        out_shape=(jax.ShapeDtypeStruct((N_MEM, 256), BF16),) * 2,
        in_specs=[pl.BlockSpec(memory_space=pltpu.VMEM)] * 4,
        out_specs=(pl.BlockSpec(memory_space=pltpu.VMEM),) * 2,
        compiler_params=pltpu.CompilerParams(vmem_limit_bytes=VMEM_LIMIT),
    )(mem, g, w_kv, gk)


def _causal_mask():
    row = lax.broadcasted_iota(jnp.int32, (TA, TA), 0)
    col = lax.broadcasted_iota(jnp.int32, (TA, TA), 1)
    return col <= row


def _causal_mask_t():
    row = lax.broadcasted_iota(jnp.int32, (TA, TA), 0)
    col = lax.broadcasted_iota(jnp.int32, (TA, TA), 1)
    return row <= col


def _fox_fwd(q_aug, k_aug, vt_aug, w_kv16, w_out16):
    s_len = q_aug.shape[1]
    n_tiles = s_len // TA
    hb = HB_FWD
    n_groups = FOX_HEADS // hb

    def body(q_ref, k_ref, vt_ref, kv16_ref, out16_ref, o_ref, st_ref, kv_all, out_all, send_sems, recv_sems, local_sems):
        qi = pl.program_id(1)
        remote, local = _row_block_exchange(kv16_ref, out16_ref, kv_all, out_all, send_sems, recv_sems, local_sems, gather=True)

        @pl.when((pl.program_id(0) == 0) & (qi == 0))
        def _():
            for cp in remote + local:
                cp.start()

        qs = [q_ref[h] for h in range(hb)]

        def step(t0, carry, masked, width):
            rows = pl.ds(pl.multiple_of(t0 * TA, TA), width * TA)
            scores = [_dot_nt(k_ref[h, rows, :], qs[h]) for h in range(hb)]
            soft = []
            for h in range(hb):
                m, _ = carry[h]
                s = jnp.where(_causal_mask_t(), scores[h], -1e30) if masked else scores[h]
                m_new = jnp.maximum(m, jnp.max(s, axis=0, keepdims=True))
                soft.append((m_new, jnp.exp(m - m_new), jnp.exp(s - m_new).astype(BF16)))
            out = []
            for h, (m_new, alpha, p) in enumerate(soft):
                acc = alpha * carry[h][1]
                for t in range(width):
                    acc = acc + _dot(vt_ref[h, t0 + t, 0:AUG_ROWS, :], p[t * TA:(t + 1) * TA])
                out.append((m_new, acc))
            return tuple(out)

        init = tuple((jnp.full((1, TA), -1e30, F32), jnp.zeros((AUG_ROWS, TA), F32)) for _ in range(hb))
        carry = lax.fori_loop(0, qi // 2, lambda j, cr: step(2 * j, cr, False, 2), init)
        carry = lax.fori_loop(2 * (qi // 2), qi, lambda j, cr: step(j, cr, False, 1), carry)
        carry = step(qi, carry, True, 1)
        for h in range(hb):
            m, acc = carry[h]
            l = acc[HEAD_DIM:HEAD_DIM + 1, :]
            lse = m + jnp.log(l)
            o_ref[h] = jnp.concatenate([acc[0:HEAD_DIM] / l, jnp.broadcast_to(lse, (HEAD_DIM, TA))], axis=0).T
            st_ref[h, 0] = jnp.broadcast_to(lse, (8, TA))

        @pl.when((pl.program_id(0) == n_groups - 1) & (qi == n_tiles - 1))
        def _():
            for cp in remote:
                cp.wait_recv()
            for cp in remote:
                cp.wait_send()
            for cp in local:
                cp.wait()

    hbm = pl.BlockSpec(memory_space=pl.ANY)
    return pl.pallas_call(
        body,
        name="fox_fwd",
        grid=(n_groups, n_tiles),
        in_specs=[pl.BlockSpec((hb, TA, 128), lambda g, i: (g, i, 0)), pl.BlockSpec((hb, s_len, 128), lambda g, i: (g, 0, 0)),
                  pl.BlockSpec((hb, n_tiles, 128, TA), lambda g, i: (g, 0, 0, 0)), hbm, hbm],
        out_specs=[pl.BlockSpec((hb, TA, 128), lambda g, i: (g, i, 0)), pl.BlockSpec((hb, 1, 8, TA), lambda g, i: (g, i, 0, 0)),
                   hbm, hbm],
        out_shape=[jax.ShapeDtypeStruct((FOX_HEADS, s_len, 128), F32), jax.ShapeDtypeStruct((FOX_HEADS, n_tiles, 8, TA), F32),
                   jax.ShapeDtypeStruct((D_MODEL, 512), BF16), jax.ShapeDtypeStruct((D_MODEL, D_MODEL), BF16)],
        scratch_shapes=[pltpu.SemaphoreType.DMA((14,)), pltpu.SemaphoreType.DMA((14,)), pltpu.SemaphoreType.DMA((2,))],
        compiler_params=_params(("arbitrary", "arbitrary")),
    )(q_aug, k_aug, vt_aug, w_kv16, w_out16)


def _lane_group(lane, a, b, c, d):
    return jnp.where(lane < 64, a, jnp.where(lane < 128, b, jnp.where(lane < 192, c, d)))


def _pool_count(row0):
    lane = lax.broadcasted_iota(jnp.int32, (TM, POOL_WIDTH), 1)
    t1 = row0 + lax.broadcasted_iota(jnp.int32, (TM, POOL_WIDTH), 0) + 1
    return 1.0 / jnp.minimum(t1, _lane_group(lane, 2, 4, 8, 16)).astype(F32), lane


def _pool_delta(u, halo, cnt, lane):
    xe = jnp.concatenate([halo, u], axis=0)
    s2 = xe + pltpu.roll(xe, 1, 0)
    s4 = s2 + pltpu.roll(s2, 2, 0)
    s8 = s4 + pltpu.roll(s4, 4, 0)
    s16 = s8 + pltpu.roll(s8, 8, 0)
    win = _lane_group(lane, s2[HALO:], s4[HALO:], s8[HALO:], s16[HALO:])
    return win * cnt - u


def _pool_delta_bwd(dd, dd_next, cnt, cnt_next, lane):
    ee = jnp.concatenate([dd * cnt, dd_next * cnt_next], axis=0)
    n = TM + HALO
    r2 = ee + pltpu.roll(ee, n - 1, 0)
    r4 = r2 + pltpu.roll(r2, n - 2, 0)
    r8 = r4 + pltpu.roll(r4, n - 4, 0)
    r16 = r8 + pltpu.roll(r8, n - 8, 0)
    return _lane_group(lane, r2[:TM], r4[:TM], r8[:TM], r16[:TM]) - dd


def _mem_attn(qm, gq, seg, km_ref, vm_ref):
    r = _head_rms(qm, seg)
    a = qm * r
    q16 = (a * gq * 0.125).astype(BF16)
    heads = []
    for h in range(MEM_HEADS):
        sl = slice(HEAD_DIM * h, HEAD_DIM * (h + 1))
        s = _dot_nt(q16[:, sl], km_ref[:, sl])
        e = jnp.exp(s - jnp.max(s, axis=-1, keepdims=True))
        p = e * (1.0 / jnp.sum(e, axis=-1, keepdims=True))
        heads.append((p, _dot(p.astype(BF16), vm_ref[:, sl])))
    return a, r, q16, heads


def _row_specs(s_len):
    n_halo = s_len // HALO
    piece = lambda j: pl.BlockSpec((TM, PIECE), lambda i: (i, j))
    before = lambda j: pl.BlockSpec((HALO, 256), lambda i: (jnp.maximum(i * (TM // HALO) - 1, 0), j))
    after = lambda j: pl.BlockSpec((HALO, 256), lambda i: (jnp.minimum((i + 1) * (TM // HALO), n_halo - 1), j))
    return piece, before, after


def _mix_fwd(proj, olse, km, vm, wp_bd, pool_scale, gq_m, seg):
    s_len = proj.shape[0]

    def body(ua_ref, halo_ref, gb_ref, qm_ref, ol_ref, km_ref, vm_ref, wp_ref, sc_ref, gq_ref, seg_ref, out_ref):
        i = pl.program_id(0)
        u = ua_ref[:, 0:256]
        g_a = ua_ref[:, 256:512]
        halo = jnp.where(i > 0, halo_ref[...], 0.0)
        cnt, lane = _pool_count(i * TM)
        d = _pool_delta(u, halo, cnt, lane).astype(BF16)
        y_a = _dot(d, wp_ref[...]) * sc_ref[...]
        out_ref[:, 0:256] = (y_a * (g_a * _sigmoid(g_a))).astype(BF16)
        g_b = gb_ref[...]
        y_b = jnp.concatenate([ol_ref[h][:, 0:HEAD_DIM] for h in range(FOX_HEADS)], axis=1)
        out_ref[:, 256:768] = (y_b * (g_b * _sigmoid(g_b))).astype(BF16)
        _, _, _, heads = _mem_attn(qm_ref[:, 0:256], gq_ref[...], seg_ref[0:256, 0:256], km_ref, vm_ref)
        g_m = qm_ref[:, 256:512]
        y_m = jnp.concatenate([y for _, y in heads], axis=1)
        out_ref[:, 768:1024] = (y_m * (g_m * _sigmoid(g_m))).astype(BF16)

    piece, before, _ = _row_specs(s_len)
    return pl.pallas_call(
        body,
        name="mix_fwd",
        grid=(s_len // TM,),
        in_specs=[piece(0), before(0), piece(4), piece(5), pl.BlockSpec((FOX_HEADS, TM, 128), lambda i: (0, i, 0)),
                  _full((N_MEM, 256)), _full((N_MEM, 256)), _full((256, 256)), _full((1, 256)), _full((1, 256)),
                  _full((PIECE, PIECE))],
        out_specs=pl.BlockSpec((TM, D_MODEL), lambda i: (i, 0)),
        out_shape=jax.ShapeDtypeStruct((s_len, D_MODEL), BF16),
        compiler_params=_params(("parallel",)),
    )(proj, proj, proj, proj, olse, km, vm, wp_bd, pool_scale, gq_m, seg)


def _out_loss(mixed, x, target, w_out):
    s_len = x.shape[0]

    def body(mix_ref, x_ref, t_ref, w_ref, dout_ref, dmix_ref, dw16_ref, loss_ref, dw_ref):
        @pl.when(pl.program_id(0) == 0)
        def _():
            dw_ref[...] = jnp.zeros((D_MODEL, D_MODEL), F32)
            loss_ref[...] = jnp.zeros((8, 128), F32)

        mixed16 = mix_ref[...]
        err = x_ref[...] + _dot(mixed16, w_ref[...]) - t_ref[...]
        loss_ref[...] += 0.5 * jnp.sum(jnp.mean(err * err, axis=-1, keepdims=True))
        d_out = err / float(D_MODEL)
        dout_ref[...] = d_out
        d16 = d_out.astype(BF16)
        dmix_ref[...] = _dot_nt(d16, w_ref[...])
        dw_ref[...] += _dot_tn(mixed16, d16)

        @pl.when(pl.program_id(0) == pl.num_programs(0) - 1)
        def _():
            dw16_ref[...] = dw_ref[...].astype(BF16)

    row = pl.BlockSpec((TMM, D_MODEL), lambda i: (i, 0))
    return pl.pallas_call(
        body,
        name="out_loss",
        grid=(s_len // TMM,),
        in_specs=[row, row, row, _full((D_MODEL, D_MODEL))],
        out_specs=[row, row, _full((D_MODEL, D_MODEL)), _full((8, 128))],
        out_shape=[jax.ShapeDtypeStruct((s_len, D_MODEL), F32), jax.ShapeDtypeStruct((s_len, D_MODEL), F32),
                   jax.ShapeDtypeStruct((D_MODEL, D_MODEL), BF16), jax.ShapeDtypeStruct((8, 128), F32)],
        scratch_shapes=[pltpu.VMEM((D_MODEL, D_MODEL), F32)],
        compiler_params=_params(("arbitrary",)),
    )(mixed, x, target, w_out)


def _silu_pair(g):
    s = _sigmoid(g)
    return g * s, s * (1.0 + g * (1.0 - s))


def _mix_bwd(proj, olse, d_mixed, km, vm, wp_bd, pool_scale, gq_m, seg):
    s_len = proj.shape[0]
    n_tiles = s_len // TM

    def body(ua_ref, halo_ref, ga_next_ref, gb_ref, qm_ref, ol_ref, dm_ref, dm_next_ref, km_ref, vm_ref, wp_ref, sc_ref, gq_ref,
             seg_ref, dua_ref, dgb_ref, dqm_ref, do_ref, dl_ref, dwp_ref, dsc_ref, dkm_ref, dvm_ref, dgq_ref):
        i = pl.program_id(0)

        @pl.when(i == 0)
        def _():
            dwp_ref[...] = jnp.zeros((256, 256), F32)
            dsc_ref[...] = jnp.zeros((1, 256), F32)
            dkm_ref[...] = jnp.zeros((N_MEM, 256), F32)
            dvm_ref[...] = jnp.zeros((N_MEM, 256), F32)
            dgq_ref[...] = jnp.zeros((1, HEAD_DIM), F32)

        u = ua_ref[:, 0:256]
        g_a = ua_ref[:, 256:512]
        halo = jnp.where(i > 0, halo_ref[...], 0.0)
        cnt, lane = _pool_count(i * TM)
        d16 = _pool_delta(u, halo, cnt, lane).astype(BF16)
        t = _dot(d16, wp_ref[...])
        y_a = t * sc_ref[...]
        silu_a, dsilu_a = _silu_pair(g_a)
        dm_a = dm_ref[:, 0:256]
        dy_a = dm_a * silu_a
        dsc_ref[...] += jnp.sum(dy_a * t, axis=0, keepdims=True)
        dt16 = (dy_a * sc_ref[...]).astype(BF16)
        dwp_ref[...] += _dot_tn(d16, dt16)
        dd = _dot_nt(dt16, wp_ref[...])
        g_next = ga_next_ref[...]
        dt_next = (dm_next_ref[...] * (g_next * _sigmoid(g_next)) * sc_ref[...]).astype(BF16)
        dd_next = jnp.where(i < n_tiles - 1, _dot_nt(dt_next, wp_ref[...]), 0.0)
        cnt_next = _pool_count((i + 1) * TM)[0][0:HALO]
        dua_ref[:, 0:256] = _pool_delta_bwd(dd, dd_next, cnt, cnt_next, lane).astype(BF16)
        dua_ref[:, 256:512] = (dm_a * y_a * dsilu_a).astype(BF16)

        silu_b, dsilu_b = _silu_pair(gb_ref[...])
        dm_b = dm_ref[:, 256:768]
        o_all = jnp.concatenate([ol_ref[h][:, 0:HEAD_DIM] for h in range(FOX_HEADS)], axis=1)
        d_o = dm_b * silu_b
        dgb_ref[...] = (dm_b * o_all * dsilu_b).astype(BF16)
        for h in range(FOX_HEADS):
            do_ref[h] = d_o[:, HEAD_DIM * h:HEAD_DIM * (h + 1)].astype(BF16)
        prod = d_o * o_all
        hi = prod.astype(BF16)
        lo = (prod - hi.astype(F32)).astype(BF16)
        head_of_lane = lax.broadcasted_iota(jnp.int32, (FOX_HEADS, PIECE), 1) // HEAD_DIM
        pick = jnp.where(head_of_lane == lax.broadcasted_iota(jnp.int32, (FOX_HEADS, PIECE), 0), 1.0, 0.0).astype(BF16)
        delta = _dot_nt(pick, hi) + _dot_nt(pick, lo)
        for h in range(FOX_HEADS):
            dl_ref[h, 0] = jnp.broadcast_to(delta[h:h + 1, :], (8, TM))

        seg = seg_ref[0:256, 0:256]
        a, r, q16, heads = _mem_attn(qm_ref[:, 0:256], gq_ref[...], seg, km_ref, vm_ref)
        silu_m, dsilu_m = _silu_pair(qm_ref[:, 256:512])
        dm_m = dm_ref[:, 768:1024]
        y_m = jnp.concatenate([y for _, y in heads], axis=1)
        dqm_ref[:, 256:512] = (dm_m * y_m * dsilu_m).astype(BF16)
        dy16_all = (dm_m * silu_m).astype(BF16)
        d_scores = []
        for h in range(MEM_HEADS):
            sl = slice(HEAD_DIM * h, HEAD_DIM * (h + 1))
            p = heads[h][0]
            dy16 = dy16_all[:, sl]
            dp = _dot_nt(dy16, vm_ref[:, sl])
            dvm_ref[:, sl] += _dot_tn(p.astype(BF16), dy16)
            ds16 = (p * (dp - jnp.sum(dp * p, axis=-1, keepdims=True))).astype(BF16)
            dkm_ref[:, sl] += _dot_tn(ds16, q16[:, sl])
            d_scores.append(_dot(ds16, km_ref[:, sl]))
        dq, dg_rows = _head_rms_bwd(a, r, gq_ref[...], jnp.concatenate(d_scores, axis=1) * 0.125, seg)
        dqm_ref[:, 0:256] = dq.astype(BF16)
        dgq_ref[...] += _fold_heads(jnp.sum(dg_rows, axis=0, keepdims=True), MEM_HEADS)

    piece, before, after = _row_specs(s_len)
    n_halo = s_len // HALO
    row = pl.BlockSpec((TM, D_MODEL), lambda i: (i, 0))
    dm_after = pl.BlockSpec((HALO, 256), lambda i: (jnp.minimum((i + 1) * (TM // HALO), n_halo - 1), 0))
    out_piece = pl.BlockSpec((TM, PIECE), lambda i: (i, 0))
    return pl.pallas_call(
        body,
        name="mix_bwd",
        grid=(n_tiles,),
        in_specs=[piece(0), before(0), after(1), piece(4), piece(5), pl.BlockSpec((FOX_HEADS, TM, 128), lambda i: (0, i, 0)),
                  row, dm_after, _full((N_MEM, 256)), _full((N_MEM, 256)), _full((256, 256)), _full((1, 256)), _full((1, 256)),
                  _full((PIECE, PIECE))],
        out_specs=[out_piece, out_piece, out_piece, pl.BlockSpec((FOX_HEADS, TM, HEAD_DIM), lambda i: (0, i, 0)),
                   pl.BlockSpec((FOX_HEADS, 1, 8, TM), lambda i: (0, i, 0, 0)),
                   _full((256, 256)), _full((1, 256)), _full((N_MEM, 256)), _full((N_MEM, 256)), _full((1, HEAD_DIM))],
        out_shape=[jax.ShapeDtypeStruct((s_len, PIECE), BF16)] * 3 + [
            jax.ShapeDtypeStruct((FOX_HEADS, s_len, HEAD_DIM), BF16),
            jax.ShapeDtypeStruct((FOX_HEADS, n_tiles, 8, TM), F32),
            jax.ShapeDtypeStruct((256, 256), F32), jax.ShapeDtypeStruct((1, 256), F32),
            jax.ShapeDtypeStruct((N_MEM, 256), F32), jax.ShapeDtypeStruct((N_MEM, 256), F32),
            jax.ShapeDtypeStruct((1, HEAD_DIM), F32)],
        compiler_params=_params(("arbitrary",)),
    )(proj, proj, proj, proj, proj, olse, d_mixed, d_mixed, km, vm, wp_bd, pool_scale, gq_m, seg)


def _fox_bwd(q_aug, k_aug, kt_aug, v_h, d_o, lse_rows, delta_rows, dw_kv16, dw_out16):
    s_len = q_aug.shape[1]
    n_tiles = s_len // TA
    n_groups = FOX_HEADS // HB

    def body(q_ref, k_ref, kt_ref, v_ref, do_ref, st_ref, dl_ref, dkv16_ref, dout16_ref, dq_ref, dk_ref, dv_ref, land_kv, land_out,
             dqt_ref, send_sems, recv_sems, local_sems):
        j = pl.program_id(1)
        remote, local = _row_block_exchange(dkv16_ref, dout16_ref, land_kv, land_out, send_sems, recv_sems, local_sems, gather=False)

        @pl.when((pl.program_id(0) == 0) & (j == 0))
        def _():
            for cp in remote + local:
                cp.start()

        @pl.when(j == 0)
        def _():
            dqt_ref[...] = jnp.zeros((HB, n_tiles, AUG_ROWS, TA), F32)

        ks = [k_ref[h] for h in range(HB)]
        kts = [kt_ref[h, 0, 0:AUG_ROWS, :] for h in range(HB)]
        vs = [v_ref[h] for h in range(HB)]

        def pair(i0, acc, masked, width):
            rows = pl.ds(pl.multiple_of(i0 * TA, TA), width * TA)
            qs = [q_ref[h, rows, :] for h in range(HB)]
            d_os = [do_ref[h, rows, :] for h in range(HB)]
            row_stat = lambda ref, h: jnp.concatenate([ref[h, i0 + t, 0:1, :] for t in range(width)], axis=1)
            scores = [(_dot_nt(ks[h], qs[h]), _dot_nt(vs[h], d_os[h])) for h in range(HB)]
            probs = []
            for h in range(HB):
                s, dp = scores[h]
                if masked:
                    s = jnp.where(_causal_mask_t(), s, -1e30)
                p = jnp.exp(s - row_stat(st_ref, h))
                probs.append((p.astype(BF16), (p * (dp - row_stat(dl_ref, h))).astype(BF16)))
            out = []
            for h in range(HB):
                p16, ds16 = probs[h]
                dqt = _dot(kts[h], ds16)
                for t in range(width):
                    dqt_ref[h, i0 + t] += dqt[:, t * TA:(t + 1) * TA]
                out.append((acc[h][0] + _dot(ds16, qs[h]), acc[h][1] + _dot(p16, d_os[h])))
            return tuple(out)

        zero = tuple((jnp.zeros((TA, 128), F32), jnp.zeros((TA, HEAD_DIM), F32)) for _ in range(HB))
        acc = pair(j, zero, True, 1)
        odd = (n_tiles - 1 - j) % 2
        acc = lax.fori_loop(j + 1, j + 1 + odd, lambda i, a: pair(i, a, False, 1), acc)
        acc = lax.fori_loop(0, (n_tiles - 1 - j) // 2, lambda t, a: pair(j + 1 + odd + 2 * t, a, False, 2), acc)
        pad = jnp.zeros((128 - AUG_ROWS, TA), F32)
        for h in range(HB):
            dk_ref[h] = acc[h][0]
            dv_ref[h] = acc[h][1]
            dq_ref[h] = jnp.concatenate([dqt_ref[h, j], pad], axis=0).T

        @pl.when((pl.program_id(0) == n_groups - 1) & (j == n_tiles - 1))
        def _():
            for cp in remote:
                cp.wait_recv()
            for cp in remote:
                cp.wait_send()
            for cp in local:
                cp.wait()

    assert n_groups == 1
    resident = pl.BlockSpec(memory_space=pltpu.VMEM)
    whole = lambda w: resident
    rows_blk = lambda r: resident
    tile = lambda w: pl.BlockSpec((HB, TA, w), lambda g, j: (g, j, 0))
    hbm = pl.BlockSpec(memory_space=pl.ANY)
    return pl.pallas_call(
        body,
        name="fox_bwd",
        grid=(n_groups, n_tiles),
        in_specs=[whole(128), tile(128), pl.BlockSpec((HB, 1, 128, TA), lambda g, j: (g, j, 0, 0)), tile(HEAD_DIM),
                  whole(HEAD_DIM), rows_blk(8), rows_blk(8), hbm, hbm],
        out_specs=[tile(128), tile(128), tile(HEAD_DIM), hbm, hbm],
        out_shape=[jax.ShapeDtypeStruct((FOX_HEADS, s_len, 128), F32), jax.ShapeDtypeStruct((FOX_HEADS, s_len, 128), F32),
                   jax.ShapeDtypeStruct((FOX_HEADS, s_len, HEAD_DIM), F32),
                   jax.ShapeDtypeStruct((N_DEV, 128, 512), BF16), jax.ShapeDtypeStruct((N_DEV, 128, D_MODEL), BF16)],
        scratch_shapes=[pltpu.VMEM((HB, n_tiles, AUG_ROWS, TA), F32),
                        pltpu.SemaphoreType.DMA((14,)), pltpu.SemaphoreType.DMA((14,)), pltpu.SemaphoreType.DMA((2,))],
        compiler_params=_params(("arbitrary", "arbitrary")),
    )(q_aug, k_aug, kt_aug, v_h, d_o, lse_rows, delta_rows, dw_kv16, dw_out16)


def _fox_post(proj, bf_pad, gq, gk, seg, dq_aug, dk_aug, dv_h):
    s_len = proj.shape[0]
    n_tiles = s_len // TM

    def body(q_ref, k_ref, f_ref, bf_ref, gq_ref, gk_ref, seg_ref, dqa_ref, dka_ref, dvh_ref,
             dq_ref, dk_ref, dv_ref, df_ref, dgq_ref, dgk_ref, dbf_ref, carry_ref):
        @pl.when(pl.program_id(0) == 0)
        def _():
            carry_ref[...] = jnp.zeros((1, 128), F32)
            dgq_ref[...] = jnp.zeros((1, HEAD_DIM), F32)
            dgk_ref[...] = jnp.zeros((1, HEAD_DIM), F32)
            dbf_ref[...] = jnp.zeros((1, 128), F32)

        lane = lax.broadcasted_iota(jnp.int32, (TM, 128), 1)
        seg = seg_ref[...]
        dqas = [dqa_ref[h] for h in range(FOX_HEADS)]
        dkas = [dka_ref[h] for h in range(FOX_HEADS)]
        d_cum = jnp.zeros((TM, 128), F32)
        for h in range(FOX_HEADS):
            d_cum = jnp.where(lane == h, dqas[h][:, 64:65] - dkas[h][:, 67:68], d_cum)
        q_all = q_ref[...]
        k_all = k_ref[...]
        rq = _head_rms(q_all, seg)
        rk = _head_rms(k_all, seg)
        dy_q = jnp.concatenate([t[:, 0:HEAD_DIM] for t in dqas], axis=1) * 0.125
        dy_k = jnp.concatenate([t[:, 0:HEAD_DIM] for t in dkas], axis=1)
        dq, dgq_rows = _head_rms_bwd(q_all * rq, rq, gq_ref[...], dy_q, seg)
        dk, dgk_rows = _head_rms_bwd(k_all * rk, rk, gk_ref[...], dy_k, seg)
        dq_ref[...] = dq.astype(BF16)
        dk_ref[...] = dk.astype(BF16)
        dv_ref[...] = jnp.concatenate([dvh_ref[h] for h in range(FOX_HEADS)], axis=1).astype(BF16)
        dgq_ref[...] += _fold_heads(jnp.sum(dgq_rows, axis=0, keepdims=True), FOX_HEADS)
        dgk_ref[...] += _fold_heads(jnp.sum(dgk_rows, axis=0, keepdims=True), FOX_HEADS)

        row = lax.broadcasted_iota(jnp.int32, (TM, TM), 0)
        col = lax.broadcasted_iota(jnp.int32, (TM, TM), 1)
        tri = jnp.where(col >= row, 1.0, 0.0).astype(BF16)
        hi, mid, lo = _split3(d_cum)
        d_logf = _dot(tri, hi) + _dot(tri, mid) + _dot(tri, lo) + carry_ref[...]
        carry_ref[...] = d_logf[0:1, :]
        z = f_ref[...] + bf_ref[...]
        d_f = jnp.where(lane < FOX_HEADS, d_logf / (1.0 + jnp.exp(z)), 0.0)
        df_ref[...] = d_f.astype(BF16)
        dbf_ref[...] += jnp.sum(d_f, axis=0, keepdims=True)

    rev = lambda i: n_tiles - 1 - i
    col_blk = lambda j: pl.BlockSpec((TM, PIECE), lambda i: (rev(i), j))
    head_blk = lambda w: pl.BlockSpec((FOX_HEADS, TM, w), lambda i: (0, rev(i), 0))
    out_piece = pl.BlockSpec((TM, PIECE), lambda i: (rev(i), 0))
    return pl.pallas_call(
        body,
        name="fox_post",
        grid=(n_tiles,),
        in_specs=[col_blk(1), col_blk(2), pl.BlockSpec((TM, 128), lambda i: (rev(i), MY_F_OFF // 128)),
                  _full((1, 128)), _full((1, PIECE)), _full((1, PIECE)), _full((PIECE, PIECE)),
                  head_blk(128), head_blk(128), head_blk(HEAD_DIM)],
        out_specs=[out_piece, out_piece, out_piece, pl.BlockSpec((TM, 128), lambda i: (rev(i), 0)),
                   _full((1, HEAD_DIM)), _full((1, HEAD_DIM)), _full((1, 128))],
        out_shape=[jax.ShapeDtypeStruct((s_len, PIECE), BF16)] * 3 + [
            jax.ShapeDtypeStruct((s_len, 128), BF16), jax.ShapeDtypeStruct((1, HEAD_DIM), F32),
            jax.ShapeDtypeStruct((1, HEAD_DIM), F32), jax.ShapeDtypeStruct((1, 128), F32)],
        scratch_shapes=[pltpu.VMEM((1, 128), F32)],
        compiler_params=_params(("arbitrary",)),
    )(proj, proj, proj, bf_pad, gq, gk, seg, dq_aug, dk_aug, dv_h)


def _mem_bwd(mem, g, w_kv, gk, dkm, dvm):
    def body(mem_ref, g_ref, w_ref, gk_ref, dkm_ref, dvm_ref, dw_ref, dg_ref, dgk_ref, dkv_ref):
        m = mem_ref[...]
        r = _rms(m)
        a = m * r
        mn16 = (a * g_ref[...]).astype(BF16)
        kv = _dot(mn16, w_ref[...])
        dgk = jnp.zeros((N_MEM, HEAD_DIM), F32)
        for h in range(MEM_HEADS):
            sl = slice(HEAD_DIM * h, HEAD_DIM * (h + 1))
            kh = kv[:, sl]
            rk = _rms(kh)
            dk, dg_rows = _rms_bwd(kh * rk, rk, gk_ref[...], dkm_ref[:, sl])
            dkv_ref[:, sl] = dk.astype(BF16)
            dgk = dgk + dg_rows
        dkv_ref[:, 256:512] = dvm_ref[...].astype(BF16)
        dgk_ref[...] = jnp.sum(dgk, axis=0, keepdims=True)
        dkv16 = dkv_ref[...]
        dw_ref[...] = _dot_tn(mn16, dkv16).astype(BF16)
        dg_ref[...] = jnp.sum(_dot_nt(dkv16, w_ref[...]) * a, axis=0, keepdims=True)

    return pl.pallas_call(
        body,
        name="mem_bwd",
        out_shape=(jax.ShapeDtypeStruct((D_MODEL, 512), BF16), jax.ShapeDtypeStruct((1, D_MODEL), F32),
                   jax.ShapeDtypeStruct((1, HEAD_DIM), F32)),
        in_specs=[pl.BlockSpec(memory_space=pltpu.VMEM)] * 6,
        out_specs=(pl.BlockSpec(memory_space=pltpu.VMEM),) * 3,
        scratch_shapes=[pltpu.VMEM((N_MEM, 512), BF16)],
        compiler_params=pltpu.CompilerParams(vmem_limit_bytes=VMEM_LIMIT),
    )(mem, g, w_kv, gk, dkm, dvm)


def _grad_x_exchange(pieces, d_f, w_my, x, norm_g, d_out, dw_in, land_kv, land_out, d_mem_g, d_scale, d_bf, d_gq, d_gk, d_gqm,
                     d_gkm, dwp_bd, loss_blk):
    s_len = x.shape[0]
    n_steps = s_len // TM
    step2, step3 = n_steps // 4, (3 * n_steps) // 4
    half = D_MODEL // 2

    def body(p0, p1, p2, p3, p4, p5, df_ref, x_ref, dout_ref, w_ref, g_ref, in_ref, lkv_ref, lout_ref,
             dmg, dsc, dbf, dgq, dgk, dgqm, dgkm, dwp, loss_ref,
             gx_ref, rin_ref, rkv_ref, rout_ref, sred_ref,
             dng, land1, send2a, send2b, keep2a, keep2b, land2a, land2b, send3a, send3b, land3a, land3b, sb_ref, sland,
             send_sems, recv_sems):
        i = pl.program_id(0)
        x_, y_, c_, me = _my_place()
        sibling, x_nbr, y_nbr = (x_, y_, 1 - c_), (1 - x_, y_, c_), (x_, 1 - y_, c_)

        def rcopy(src, dst, sem, to):
            return pltpu.make_async_remote_copy(src_ref=src, dst_ref=dst, send_sem=send_sems.at[sem], recv_sem=recv_sems.at[sem],
                                                device_id=to, device_id_type=MESH)

        small = []
        for k in range(1, N_DEV):
            px, py, pc = x_ ^ (k >> 2), y_ ^ ((k >> 1) & 1), c_ ^ (k & 1)
            small.append(rcopy(sb_ref, sland.at[me], k - 1, (px, py, pc)))
        stage1 = [rcopy(in_ref.at[2 * k + 1 - c_], land1.at[k], 7 + k, sibling) for k in range(4)]
        stage2 = [rcopy(send2a.at[j], land2a.at[j], 11 + j, x_nbr) for j in range(2)]
        stage2 += [rcopy(send2b.at[j], land2b.at[j], 13 + j, y_nbr) for j in range(2)]
        stage3 = [rcopy(send3a, land3a, 15, y_nbr), rcopy(send3b, land3b, 16, x_nbr)]
        top, bot = slice(0, half), slice(half, D_MODEL)

        @pl.when(i == 0)
        def _():
            dng[...] = jnp.zeros((1, D_MODEL), F32)
            for cp in stage1:
                cp.start()

        @pl.when(i == step2)
        def _():
            for cp in stage1:
                cp.wait_recv()

            def chip_sum(k, rows):
                return in_ref[2 * k + c_, rows, :].astype(F32) + land1[k, rows, :].astype(F32)

            for j in range(2):
                send2a[j] = chip_sum(2 * (1 - x_) + j, top).astype(BF16)
                keep2a[j] = chip_sum(2 * x_ + j, top)
                send2b[j] = chip_sum(2 * j + 1 - y_, bot).astype(BF16)
                keep2b[j] = chip_sum(2 * j + y_, bot)
            for cp in stage2:
                cp.start()

        @pl.when(i == step3)
        def _():
            for cp in stage2:
                cp.wait_recv()
            for j in range(2):
                keep2a[j] = keep2a[j] + land2a[j].astype(F32)
                keep2b[j] = keep2b[j] + land2b[j].astype(F32)
            send3a[...] = keep2a[1 - y_].astype(BF16)
            send3b[...] = keep2b[1 - x_].astype(BF16)
            for cp in stage3:
                cp.start()

        dh = _dot_nt(df_ref[...], w_ref[:, MY_F_OFF:MY_WIDTH])
        for j, p in enumerate((p0, p1, p2, p3, p4, p5)):
            dh = dh + _dot_nt(p[...], w_ref[:, PIECE * j:PIECE * (j + 1)])
        xv = x_ref[...]
        r = _rms(xv)
        dx, dg_rows = _rms_bwd(xv * r, r, g_ref[...], dh)
        gx_ref[...] = dout_ref[...] + dx
        dng[...] += jnp.sum(dg_rows, axis=0, keepdims=True)

        @pl.when(i == n_steps - 1)
        def _():
            sb_ref[...] = jnp.zeros((SB_ROWS, SB_W), F32)
            for k in range(4):
                sb_ref[SB_NORM_G + k:SB_NORM_G + k + 1, :] = dng[:, SB_W * k:SB_W * (k + 1)]
                sb_ref[SB_MEM_NORM_G + k:SB_MEM_NORM_G + k + 1, :] = dmg[:, SB_W * k:SB_W * (k + 1)]
            sb_ref[SB_POOL_SCALE:SB_POOL_SCALE + 1, :] = dsc[...]
            sb_ref[SB_B_F:SB_B_F + 1, 0:128] = dbf[...]
            for row, ref in ((SB_FOX_Q, dgq), (SB_FOX_K, dgk), (SB_MEM_Q, dgqm), (SB_MEM_K, dgkm)):
                sb_ref[row:row + 1, 0:HEAD_DIM] = ref[...]
            sb_ref[SB_LOSS:SB_LOSS + 1, 0:128] = loss_ref[0:1, :]
            for grp in range(4):
                sl = slice(64 * grp, 64 * (grp + 1))
                sb_ref[SB_W_POOL:SB_W_POOL + 64, sl] = dwp[sl, sl]
            for cp in small:
                cp.start()
            sland[me] = sb_ref[...]
            for cp in stage3:
                cp.wait_recv()
            rin_ref[top, :] = keep2a[y_] + land3a[...].astype(F32)
            rin_ref[bot, :] = keep2b[x_] + land3b[...].astype(F32)
            for cp in small:
                cp.wait_recv()
            for cp in small + stage1 + stage2 + stage3:
                cp.wait_send()
            for land, red in ((lkv_ref, rkv_ref), (lout_ref, rout_ref), (sland, sred_ref)):
                acc = land[0].astype(F32)
                for d in range(1, N_DEV):
                    acc = acc + land[d].astype(F32)
                red[...] = acc

    piece = pl.BlockSpec((TM, PIECE), lambda i: (i, 0))
    row = pl.BlockSpec((TM, D_MODEL), lambda i: (i, 0))
    vmem = pl.BlockSpec(memory_space=pltpu.VMEM)
    half_chunk = lambda n, dt: pltpu.VMEM((n, half, IN_CHUNK), dt)
    whole = (w_my, norm_g, dw_in, land_kv, land_out, d_mem_g, d_scale, d_bf, d_gq, d_gk, d_gqm, d_gkm, dwp_bd, loss_blk)
    return pl.pallas_call(
        body,
        name="grad_x_exchange",
        grid=(n_steps,),
        in_specs=[piece] * 6 + [pl.BlockSpec((TM, 128), lambda i: (i, 0)), row, row] + [vmem] * len(whole),
        out_specs=[row, vmem, vmem, vmem, vmem],
        out_shape=[jax.ShapeDtypeStruct((s_len, D_MODEL), F32), jax.ShapeDtypeStruct((D_MODEL, IN_CHUNK), F32),
                   jax.ShapeDtypeStruct((128, 512), F32), jax.ShapeDtypeStruct((128, D_MODEL), F32),
                   jax.ShapeDtypeStruct((SB_ROWS, SB_W), F32)],
        scratch_shapes=[
            pltpu.VMEM((1, D_MODEL), F32),
            pltpu.VMEM((4, D_MODEL, IN_CHUNK), BF16),
            half_chunk(2, BF16), half_chunk(2, BF16), half_chunk(2, F32), half_chunk(2, F32), half_chunk(2, BF16), half_chunk(2, BF16),
            pltpu.VMEM((half, IN_CHUNK), BF16), pltpu.VMEM((half, IN_CHUNK), BF16),
            pltpu.VMEM((half, IN_CHUNK), BF16), pltpu.VMEM((half, IN_CHUNK), BF16),
            pltpu.VMEM((SB_ROWS, SB_W), F32),
            pltpu.VMEM((N_DEV, SB_ROWS, SB_W), F32),
            pltpu.SemaphoreType.DMA((17,)),
            pltpu.SemaphoreType.DMA((17,)),
        ],
        compiler_params=_params(("arbitrary",)),
    )(*pieces, d_f, x, d_out, *whole)


def _grad_w_in(h16, pieces, d_f):
    s_len = h16.shape[0]
    tk = 512

    def body(h_ref, p0, p1, p2, p3, p4, p5, df_ref, out_ref, dw_ref):
        @pl.when(pl.program_id(0) == 0)
        def _():
            dw_ref[...] = jnp.zeros((D_MODEL, MY_WIDTH), F32)

        h = h_ref[...]
        for j, p in enumerate((p0, p1, p2, p3, p4, p5)):
            dw_ref[:, PIECE * j:PIECE * (j + 1)] += _dot_tn(h, p[...])
        dw_ref[:, MY_F_OFF:MY_WIDTH] += _dot_tn(h, df_ref[...])

        @pl.when(pl.program_id(0) == pl.num_programs(0) - 1)
        def _():
            for d in range(N_DEV):
                parts = [dw_ref[:, start:start + width] for start, width in _chunk_segments(d)]
                out_ref[d] = (parts[0] if len(parts) == 1 else jnp.concatenate(parts, axis=1)).astype(BF16)

    piece = pl.BlockSpec((tk, PIECE), lambda i: (i, 0))
    return pl.pallas_call(
        body,
        name="grad_w_in",
        grid=(s_len // tk,),
        in_specs=[pl.BlockSpec((tk, D_MODEL), lambda i: (i, 0))] + [piece] * 6 + [pl.BlockSpec((tk, 128), lambda i: (i, 0))],
        out_specs=_full((N_DEV, D_MODEL, IN_CHUNK)),
        out_shape=jax.ShapeDtypeStruct((N_DEV, D_MODEL, IN_CHUNK), BF16),
        scratch_shapes=[pltpu.VMEM((D_MODEL, MY_WIDTH), F32)],
        compiler_params=_params(("arbitrary",)),
    )(h16, *pieces, d_f)


def _adamw(w, g, m, v):
    m = ADAM_B1 * m + (1.0 - ADAM_B1) * g
    v = ADAM_B2 * v + (1.0 - ADAM_B2) * (g * g)
    m_hat = m / (1.0 - ADAM_B1 ** ADAM_STEP)
    v_hat = v / (1.0 - ADAM_B2 ** ADAM_STEP)
    return -ADAM_LR * (m_hat / (jnp.sqrt(v_hat) + ADAM_EPS) + ADAM_WD * w), m, v


PARAM_ORDER = ("norm_g", "w_in", "b_f", "w_pool", "pool_scale", "fox_q_g", "fox_k_g", "mem_norm_g", "w_mem_kv", "mem_q_g", "mem_k_g", "w_out")


def _update(red_in, red_kv, red_out, sred, params):
    def body(rin_ref, rkv_ref, rout_ref, sred_ref, *refs):
        ins, outs = refs[:3 * len(PARAM_ORDER)], refs[3 * len(PARAM_ORDER):]
        wide = lambda row: jnp.concatenate([sred_ref[row + k:row + k + 1, :] for k in range(4)], axis=1)
        head = lambda row: sred_ref[row:row + 1, 0:HEAD_DIM]
        grads = {
            "norm_g": lambda: wide(SB_NORM_G), "w_in": lambda: rin_ref[...], "b_f": lambda: sred_ref[SB_B_F:SB_B_F + 1, 0:FOX_HEADS],
            "pool_scale": lambda: sred_ref[SB_POOL_SCALE:SB_POOL_SCALE + 1, :], "fox_q_g": lambda: head(SB_FOX_Q),
            "fox_k_g": lambda: head(SB_FOX_K), "mem_norm_g": lambda: wide(SB_MEM_NORM_G), "w_mem_kv": lambda: rkv_ref[...],
            "mem_q_g": lambda: head(SB_MEM_Q), "mem_k_g": lambda: head(SB_MEM_K), "w_out": lambda: rout_ref[...],
        }
        for p, name in enumerate(PARAM_ORDER):
            w_ref, m_ref, v_ref = ins[3 * p:3 * p + 3]
            g_out, d_out, m_out, v_out = outs[4 * p:4 * p + 4]
            if name == "w_pool":
                for grp in range(4):
                    g = sred_ref[SB_W_POOL:SB_W_POOL + 64, 64 * grp:64 * (grp + 1)]
                    delta, m_new, v_new = _adamw(w_ref[grp], g, m_ref[grp], v_ref[grp])
                    g_out[grp], d_out[grp], m_out[grp], v_out[grp] = g, delta, m_new, v_new
            elif name == "w_in":
                g_t = jnp.concatenate([rin_ref[...], jnp.zeros((D_MODEL, 512 - IN_CHUNK), F32)], axis=1).T[0:IN_CHUNK]
                for j in range(8):
                    rows = pl.ds(j, IN_CHUNK, stride=8)
                    g = g_t[:, 128 * j:128 * (j + 1)]
                    delta, m_new, v_new = _adamw(w_ref[rows, :], g, m_ref[rows, :], v_ref[rows, :])
                    g_out[rows, :], d_out[rows, :], m_out[rows, :], v_out[rows, :] = g, delta, m_new, v_new
            else:
                g = grads[name]()
                delta, m_new, v_new = _adamw(w_ref[...], g, m_ref[...], v_ref[...])
                g_out[...], d_out[...], m_out[...], v_out[...] = g, delta, m_new, v_new

    flat = [t for name in PARAM_ORDER for t in params[name]]
    shapes = [params[name][0].shape for name in PARAM_ORDER for _ in range(4)]
    outs = pl.pallas_call(
        body,
        name="adamw_update",
        out_shape=tuple(jax.ShapeDtypeStruct(s, F32) for s in shapes),
        in_specs=[pl.BlockSpec(memory_space=pltpu.VMEM)] * (4 + len(flat)),
        out_specs=(pl.BlockSpec(memory_space=pltpu.VMEM),) * len(shapes),
        compiler_params=pltpu.CompilerParams(vmem_limit_bytes=VMEM_LIMIT),
    )(red_in, red_kv, red_out, sred, *flat)
    return {name: outs[4 * p:4 * p + 4] for p, name in enumerate(PARAM_ORDER)}


def kernel(x, mem, norm_g, w_in, b_f, w_pool, pool_scale, fox_q_g, fox_k_g, mem_norm_g, w_mem_kv, mem_q_g, mem_k_g, w_out, loss_target, m_norm_g, m_w_in, m_b_f, m_w_pool, m_pool_scale, m_fox_q_g, m_fox_k_g, m_mem_norm_g, m_w_mem_kv, m_mem_q_g, m_mem_k_g, m_w_out, v_norm_g, v_w_in, v_b_f, v_w_pool, v_pool_scale, v_fox_q_g, v_fox_k_g, v_mem_norm_g, v_w_mem_kv, v_mem_q_g, v_mem_k_g, v_w_out):
    given = dict(norm_g=(norm_g, m_norm_g, v_norm_g), w_in=(w_in, m_w_in, v_w_in), b_f=(b_f, m_b_f, v_b_f),
                 w_pool=(w_pool, m_w_pool, v_w_pool), pool_scale=(pool_scale, m_pool_scale, v_pool_scale),
                 fox_q_g=(fox_q_g, m_fox_q_g, v_fox_q_g), fox_k_g=(fox_k_g, m_fox_k_g, v_fox_k_g),
                 mem_norm_g=(mem_norm_g, m_mem_norm_g, v_mem_norm_g), w_mem_kv=(w_mem_kv, m_w_mem_kv, v_w_mem_kv),
                 mem_q_g=(mem_q_g, m_mem_q_g, v_mem_q_g), mem_k_g=(mem_k_g, m_mem_k_g, v_mem_k_g), w_out=(w_out, m_w_out, v_w_out))
    params = {name: tuple(t[0] if t.ndim > 2 else t for t in wmv) for name, wmv in given.items()}
    params["w_in"] = tuple(t.T.reshape(IN_CHUNK * 8, 128) for t in params["w_in"])
    x2, mem2, target2 = x[0], mem[0], loss_target[0]

    seg = jnp.asarray(np.kron(np.eye(FOX_HEADS), np.full((HEAD_DIM, HEAD_DIM), 1.0 / HEAD_DIM)), BF16)
    w_my, w_kv16, w_out16, wp_bd, bf_pad, gq8, gk8, gqm4 = _gather_w_in(
        params["w_in"][0], params["w_mem_kv"][0], params["w_out"][0], params["w_pool"][0], b_f, fox_q_g, fox_k_g, mem_q_g)
    proj, h16 = _fwd_proj(x2, norm_g, w_my)
    q_aug, k_aug, v_h, kt_aug, vt_aug = _fox_prep(proj, bf_pad, gq8, gk8, seg)
    olse, lse_rows, w_kv_all, w_out_all = _fox_fwd(q_aug, k_aug, vt_aug, w_kv16, w_out16)
    km, vm = _mem_prep(mem2, mem_norm_g, w_kv_all, mem_k_g)
    mixed = _mix_fwd(proj, olse, km, vm, wp_bd, pool_scale, gqm4, seg)
    d_out, d_mixed, dw_out, loss_blk = _out_loss(mixed, x2, target2, w_out_all)

    d_ua, d_gb, d_qm, d_o, delta_rows, dwp_bd, d_scale, dkm, dvm, d_gqm = _mix_bwd(proj, olse, d_mixed, km, vm, wp_bd, pool_scale,
                                                                                    gqm4, seg)
    dw_kv, d_mem_g, d_gkm = _mem_bwd(mem2, mem_norm_g, w_kv_all, mem_k_g, dkm, dvm)
    dq_aug, dk_aug, dv_h, land_kv, land_out = _fox_bwd(q_aug, k_aug, kt_aug, v_h, d_o, lse_rows, delta_rows, dw_kv, dw_out)
    d_q, d_k, d_v, d_f, d_gq, d_gk, d_bf = _fox_post(proj, bf_pad, gq8, gk8, seg, dq_aug, dk_aug, dv_h)
    pieces = (d_ua, d_q, d_k, d_v, d_gb, d_qm)
    dw_in = _grad_w_in(h16, pieces, d_f)
    grad_x, red_in, red_kv, red_out, sred = _grad_x_exchange(pieces, d_f, w_my, x2, norm_g, d_out, dw_in, land_kv, land_out, d_mem_g,
                                                             d_scale, d_bf, d_gq, d_gk, d_gqm, d_gkm, dwp_bd, loss_blk)
    new = _update(red_in, red_kv, red_out, sred, params)
    result = [sred[SB_LOSS, 0], grad_x[None]]
    for kind in range(4):
        for name in PARAM_ORDER:
            out = new[name][kind]
            if name == "w_in":
                out = out.reshape(IN_CHUNK, D_MODEL).T
            result.append(out[None] if given[name][0].ndim > 2 else out)
    return tuple(result)
```

```python
import functools

import jax
import jax.numpy as jnp
import numpy as np
from jax import lax
from jax.experimental import pallas as pl
from jax.experimental.pallas import tpu as pltpu

F32 = jnp.float32
BF16 = jnp.bfloat16
MESH = pl.DeviceIdType.MESH

N_DEV = 8
D_MODEL = 1024
HEAD_DIM = 64
FOX_HEADS = 8
MEM_HEADS = 4
N_MEM = 256
POOL_WIDTH = 256
EPS = 1e-6
IN_WIDTH = 3080
IN_CHUNK = IN_WIDTH // N_DEV
CHUNK_ROWS = 400
F_OFF = 2048
MY_WIDTH = 3200
MY_F_OFF = 3072
PIECE = 512
TM = 256
TMM = 512
TA = 256
HB = 8
HB_FWD = 8
AUG_ROWS = 72
HALO = 16
VMEM_LIMIT = 56 * 1024 * 1024

ADAM_LR = 0.001
ADAM_B1 = 0.9
ADAM_B2 = 0.999
ADAM_EPS = 1e-08
ADAM_WD = 0.01
ADAM_STEP = 10


def _dot(a, b):
    return jnp.dot(a, b, preferred_element_type=F32)


def _dot_nt(a, b):
    return lax.dot_general(a, b, (((1,), (1,)), ((), ())), preferred_element_type=F32)


def _dot_tn(a, b):
    return lax.dot_general(a, b, (((0,), (0,)), ((), ())), preferred_element_type=F32)


def _sigmoid(g):
    return 0.5 + 0.5 * jnp.tanh(0.5 * g)


def _split3(v):
    hi = v.astype(BF16)
    r1 = v - hi.astype(F32)
    mid = r1.astype(BF16)
    lo = (r1 - mid.astype(F32)).astype(BF16)
    return hi, mid, lo


def _rms(v):
    return lax.rsqrt(jnp.mean(v * v, axis=-1, keepdims=True) + EPS)


def _rms_bwd(a, r, g, dy):
    da = dy * g
    return r * (da - a * jnp.mean(da * a, axis=-1, keepdims=True)), dy * a


def _head_mean(z, seg):
    hi = z.astype(BF16)
    lo = (z - hi.astype(F32)).astype(BF16)
    if z.shape[1] == 256:
        return _dot(hi, seg) + _dot(lo, seg)
    half = seg[0:256, 0:256]
    return jnp.concatenate([_dot(hi[:, 0:256], half) + _dot(lo[:, 0:256], half),
                            _dot(hi[:, 256:512], half) + _dot(lo[:, 256:512], half)], axis=1)


def _head_rms(v, seg):
    return lax.rsqrt(_head_mean(v * v, seg) + EPS)


def _head_rms_bwd(a, r, g, dy, seg):
    da = dy * g
    return r * (da - a * _head_mean(da * a, seg)), dy * a


def _fold_heads(row, n_heads):
    out = row[:, 0:HEAD_DIM]
    for h in range(1, n_heads):
        out = out + row[:, HEAD_DIM * h:HEAD_DIM * (h + 1)]
    return out


def _params(sem, limit=VMEM_LIMIT):
    return pltpu.CompilerParams(dimension_semantics=sem, vmem_limit_bytes=limit)


def _full(shape):
    return pl.BlockSpec(shape, lambda *_: (0,) * len(shape))


def _chunk_segments(d):
    runs = []
    for lo, hi, shift in ((0, F_OFF, 0), (F_OFF, F_OFF + FOX_HEADS, MY_F_OFF - F_OFF), (F_OFF + FOX_HEADS, IN_WIDTH, -FOX_HEADS)):
        a, b = max(lo, IN_CHUNK * d), min(hi, IN_CHUNK * (d + 1))
        if a < b:
            runs.append((a + shift, b - a))
    return runs


def _my_place():
    x, y, c = lax.axis_index("x"), lax.axis_index("y"), lax.axis_index("c")
    return x, y, c, 4 * x + 2 * y + c


def _shard_rows(dev):
    return pl.ds(pl.multiple_of(128 * dev, 128), 128)


def _gather_w_in(w_in, w_kv, w_out, w_pool, b_f, gq, gk, gqm):
    def body(win_ref, wkv_ref, wout_ref, wp_ref, bf_ref, gq_ref, gk_ref, gqm_ref, wmy_ref, kv16_ref, out16_ref, wpbd_ref, bfpad_ref,
             gq8_ref, gk8_ref, gqm4_ref, in_all, pay_in, win_rows, send_sems, recv_sems, load_sem):
        x, y, c, me = _my_place()
        here, sibling = (x, y, c), (x, y, 1 - c)
        x_chip, y_chip, far_chip = (1 - x, y), (x, 1 - y), (1 - x, 1 - y)
        relay_from = (x ^ (1 - c), y ^ c)
        relay_to = (x ^ c, y ^ (1 - c))

        load = pltpu.make_async_copy(win_ref, win_rows, load_sem)
        load.start()
        load.wait()
        pay_in[...] = jnp.zeros((CHUNK_ROWS, D_MODEL), BF16)
        for j in range(8):
            pay_in[0:IN_CHUNK, 128 * j:128 * (j + 1)] = win_rows[pl.ds(j, IN_CHUNK, stride=8), :].astype(BF16)

        def copy(k, block, to, own=False):
            px, py, pc = block
            dst = in_all.at[4 * px + 2 * py + pc]
            return pltpu.make_async_remote_copy(src_ref=pay_in if own else dst, dst_ref=dst, send_sem=send_sems.at[k],
                                                recv_sem=recv_sems.at[k], device_id=to, device_id_type=MESH)

        first = [copy(0, here, sibling, own=True), copy(1, here, (*x_chip, c), own=True), copy(2, here, (*y_chip, c), own=True)]
        for cp in first:
            cp.start()
        in_all[me] = pay_in[...]
        kv16_ref[...] = wkv_ref[...].astype(BF16)
        out16_ref[...] = wout_ref[...].astype(BF16)
        wpbd_ref[...] = jnp.zeros((POOL_WIDTH, POOL_WIDTH), BF16)
        for grp in range(4):
            sl = slice(64 * grp, 64 * (grp + 1))
            wpbd_ref[sl, sl] = wp_ref[grp].astype(BF16)
        bfpad_ref[...] = jnp.zeros((1, 128), F32)
        bfpad_ref[:, 0:FOX_HEADS] = bf_ref[...]
        gq8_ref[...] = jnp.concatenate([gq_ref[...]] * FOX_HEADS, axis=1)
        gk8_ref[...] = jnp.concatenate([gk_ref[...]] * FOX_HEADS, axis=1)
        gqm4_ref[...] = jnp.concatenate([gqm_ref[...]] * MEM_HEADS, axis=1)
        copy(1 + c, (*relay_from, c), here).wait_recv()
        passed = [copy(3, (*relay_from, c), (*relay_to, c)), copy(4, (*relay_from, c), sibling)]
        for cp in passed:
            cp.start()
        copy(2 - c, (*relay_to, c), here).wait_recv()
        passed.append(copy(5, (*relay_to, c), sibling))
        passed[-1].start()
        copy(3, (*far_chip, c), here).wait_recv()
        passed.append(copy(6, (*far_chip, c), sibling))
        passed[-1].start()
        copy(0, sibling, here).wait_recv()
        for k, chip in ((4, relay_to), (5, relay_from), (6, far_chip)):
            copy(k, (*chip, 1 - c), here).wait_recv()
        for cp in first + passed:
            cp.wait_send()

        row_pad = jnp.zeros((512 - CHUNK_ROWS, 256), F32)
        for r0 in range(0, D_MODEL, 256):
            ch = [jnp.concatenate([in_all[d, :, r0:r0 + 256].astype(F32), row_pad], axis=0).T[:, 0:IN_CHUNK] for d in range(N_DEV)]
            lo = F_OFF - 5 * IN_CHUNK
            full = jnp.concatenate(ch[:5] + [ch[5][:, :lo], ch[5][:, lo + FOX_HEADS:], ch[6], ch[7], ch[5][:, lo:lo + FOX_HEADS],
                                             jnp.zeros((256, MY_WIDTH - IN_WIDTH), F32)], axis=1)
            wmy_ref[r0:r0 + 256, :] = full.astype(BF16)

    return pl.pallas_call(
        body,
        name="gather_w_in",
        out_shape=(jax.ShapeDtypeStruct((D_MODEL, MY_WIDTH), BF16), jax.ShapeDtypeStruct((128, 512), BF16),
                   jax.ShapeDtypeStruct((128, D_MODEL), BF16), jax.ShapeDtypeStruct((POOL_WIDTH, POOL_WIDTH), BF16),
                   jax.ShapeDtypeStruct((1, 128), F32), jax.ShapeDtypeStruct((1, PIECE), F32), jax.ShapeDtypeStruct((1, PIECE), F32),
                   jax.ShapeDtypeStruct((1, 256), F32)),
        in_specs=[pl.BlockSpec(memory_space=pl.ANY)] + [pl.BlockSpec(memory_space=pltpu.VMEM)] * 7,
        out_specs=(pl.BlockSpec(memory_space=pltpu.VMEM),) * 8,
        scratch_shapes=[
            pltpu.VMEM((N_DEV, CHUNK_ROWS, D_MODEL), BF16),
            pltpu.VMEM((CHUNK_ROWS, D_MODEL), BF16),
            pltpu.VMEM((IN_CHUNK * 8, 128), F32),
            pltpu.SemaphoreType.DMA((7,)),
            pltpu.SemaphoreType.DMA((7,)),
            pltpu.SemaphoreType.DMA,
        ],
        compiler_params=pltpu.CompilerParams(vmem_limit_bytes=VMEM_LIMIT),
    )(w_in, w_kv, w_out, w_pool, b_f, gq, gk, gqm)


def _row_block_exchange(kv_ref, out_ref, kv_dst, out_dst, send_sems, recv_sems, local_sems, gather):
    x, y, c, me = _my_place()
    remote = []
    for k in range(1, N_DEV):
        px, py, pc = x ^ (k >> 2), y ^ ((k >> 1) & 1), c ^ (k & 1)
        peer = 4 * px + 2 * py + pc
        for fam, (src, dst) in enumerate(((kv_ref, kv_dst), (out_ref, out_dst))):
            remote.append(pltpu.make_async_remote_copy(
                src_ref=src if gather else src.at[_shard_rows(peer)], dst_ref=dst.at[_shard_rows(me)] if gather else dst.at[me],
                send_sem=send_sems.at[7 * fam + k - 1], recv_sem=recv_sems.at[7 * fam + k - 1],
                device_id=(px, py, pc), device_id_type=MESH))
    local = [pltpu.make_async_copy(src if gather else src.at[_shard_rows(me)], dst.at[_shard_rows(me)] if gather else dst.at[me],
                                   local_sems.at[fam]) for fam, (src, dst) in enumerate(((kv_ref, kv_dst), (out_ref, out_dst)))]
    return remote, local


SB_ROWS, SB_W = 80, 256
SB_NORM_G, SB_MEM_NORM_G, SB_POOL_SCALE, SB_B_F, SB_FOX_Q, SB_FOX_K, SB_MEM_Q, SB_MEM_K, SB_LOSS, SB_W_POOL = 0, 4, 8, 9, 10, 11, 12, 13, 14, 16


def _fwd_proj(x, norm_g, w_my):
    s_len = x.shape[0]

    def body(x_ref, g_ref, w_ref, proj_ref, h_ref):
        xv = x_ref[...]
        h = (xv * _rms(xv) * g_ref[...]).astype(BF16)
        h_ref[...] = h
        proj_ref[...] = _dot(h, w_ref[...])

    return pl.pallas_call(
        body,
        name="fwd_proj",
        grid=(s_len // TMM,),
        in_specs=[pl.BlockSpec((TMM, D_MODEL), lambda i: (i, 0)), _full((1, D_MODEL)), _full((D_MODEL, MY_WIDTH))],
        out_specs=[pl.BlockSpec((TMM, MY_WIDTH), lambda i: (i, 0)), pl.BlockSpec((TMM, D_MODEL), lambda i: (i, 0))],
        out_shape=[jax.ShapeDtypeStruct((s_len, MY_WIDTH), F32), jax.ShapeDtypeStruct((s_len, D_MODEL), BF16)],
        compiler_params=_params(("parallel",)),
    )(x, norm_g, w_my)


def _log_sigmoid(z):
    return jnp.minimum(z, 0.0) - jnp.log(1.0 + jnp.exp(-jnp.abs(z)))


def _forget_lane_maps():
    to_q = np.zeros((128, FOX_HEADS * 128), np.float32)
    to_k = np.zeros((128, FOX_HEADS * 128), np.float32)
    for h in range(FOX_HEADS):
        for term in range(3):
            to_q[8 * term + h, 128 * h + 64 + term] = 1.0
            to_q[24, 128 * h + 67 + term] = 1.0
            to_k[24, 128 * h + 64 + term] = 1.0
            to_k[8 * term + h, 128 * h + 67 + term] = -1.0
    return jnp.asarray(to_q, BF16), jnp.asarray(to_k, BF16)


def _fox_prep(proj, bf_pad, gq, gk, seg):
    s_len = proj.shape[0]
    to_q, to_k = _forget_lane_maps()

    def body(q_ref, k_ref, v_ref, f_ref, bf_ref, gq_ref, gk_ref, seg_ref, eq_ref, ek_ref,
             qa_ref, ka_ref, vh_ref, kt_ref, vt_ref, carry_ref):
        @pl.when(pl.program_id(0) == 0)
        def _():
            carry_ref[...] = jnp.zeros((1, 128), F32)

        lane = lax.broadcasted_iota(jnp.int32, (TM, 128), 1)
        logf = jnp.where(lane < FOX_HEADS, _log_sigmoid(f_ref[...] + bf_ref[...]), 0.0)
        row = lax.broadcasted_iota(jnp.int32, (TM, TM), 0)
        col = lax.broadcasted_iota(jnp.int32, (TM, TM), 1)
        tri = jnp.where(col <= row, 1.0, 0.0).astype(BF16)
        hi, mid, lo = _split3(logf)
        cum = _dot(tri, hi) + _dot(tri, mid) + _dot(tri, lo) + carry_ref[...]
        carry_ref[...] = cum[TM - 1:TM, :]
        f_hi, f_mid, f_lo = [t.astype(F32) for t in _split3(cum)]
        packed = (f_hi + pltpu.roll(f_mid, 8, 1) + pltpu.roll(f_lo, 16, 1) + jnp.where(lane == 24, 1.0, 0.0)).astype(BF16)
        extra_q = _dot(packed, eq_ref[...])
        extra_k = _dot(packed, ek_ref[...])
        one_at_64 = jnp.where(lane == 64, 1.0, 0.0)
        zeros = jnp.zeros((TM, HEAD_DIM), F32)
        q_all = q_ref[...]
        k_all = k_ref[...]
        qn_all = q_all * _head_rms(q_all, seg_ref[...]) * gq_ref[...] * 0.125
        kn_all = k_all * _head_rms(k_all, seg_ref[...]) * gk_ref[...]
        for h in range(FOX_HEADS):
            sl = slice(HEAD_DIM * h, HEAD_DIM * (h + 1))
            tile = slice(128 * h, 128 * (h + 1))
            qa = jnp.where(lane < 64, jnp.concatenate([qn_all[:, sl], zeros], axis=1), extra_q[:, tile])
            ka = jnp.where(lane < 64, jnp.concatenate([kn_all[:, sl], zeros], axis=1), extra_k[:, tile])
            vh = v_ref[:, sl]
            v_aug = jnp.where(lane < 64, jnp.concatenate([vh, zeros], axis=1), one_at_64)
            qa_ref[h] = qa.astype(BF16)
            ka_ref[h] = ka.astype(BF16)
            vh_ref[h] = vh.astype(BF16)
            kt_ref[h, 0] = ka.T.astype(BF16)
            vt_ref[h, 0] = v_aug.T.astype(BF16)

    col_blk = lambda j: pl.BlockSpec((TM, PIECE), lambda i: (i, j))
    head_blk = lambda w: pl.BlockSpec((FOX_HEADS, TM, w), lambda i: (0, i, 0))
    tile_blk = pl.BlockSpec((FOX_HEADS, 1, 128, TM), lambda i: (0, i, 0, 0))
    tiled = jax.ShapeDtypeStruct((FOX_HEADS, s_len // TM, 128, TM), BF16)
    return pl.pallas_call(
        body,
        name="fox_prep",
        grid=(s_len // TM,),
        in_specs=[col_blk(1), col_blk(2), col_blk(3), pl.BlockSpec((TM, 128), lambda i: (i, MY_F_OFF // 128)),
                  _full((1, 128)), _full((1, PIECE)), _full((1, PIECE)), _full((PIECE, PIECE)),
                  _full((128, FOX_HEADS * 128)), _full((128, FOX_HEADS * 128))],
        out_specs=[head_blk(128), head_blk(128), head_blk(HEAD_DIM), tile_blk, tile_blk],
        out_shape=[jax.ShapeDtypeStruct((FOX_HEADS, s_len, 128), BF16), jax.ShapeDtypeStruct((FOX_HEADS, s_len, 128), BF16),
                   jax.ShapeDtypeStruct((FOX_HEADS, s_len, HEAD_DIM), BF16), tiled, tiled],
        scratch_shapes=[pltpu.VMEM((1, 128), F32)],
        compiler_params=_params(("arbitrary",)),
    )(proj, proj, proj, proj, bf_pad, gq, gk, seg, to_q, to_k)


def _mem_prep(mem, g, w_kv, gk):
    def body(mem_ref, g_ref, w_ref, gk_ref, km_ref, vm_ref):
        m = mem_ref[...]
        kv = _dot((m * _rms(m) * g_ref[...]).astype(BF16), w_ref[...])
        for h in range(MEM_HEADS):
            sl = slice(HEAD_DIM * h, HEAD_DIM * (h + 1))
            kh = kv[:, sl]
            km_ref[:, sl] = (kh * _rms(kh) * gk_ref[...]).astype(BF16)
        vm_ref[...] = kv[:, 256:512].astype(BF16)

    return pl.pallas_call(
        body,
        name="mem_prep",
        out_shape=(jax.ShapeDtypeStruct((N_MEM, 256), BF16),) * 2,
        in_specs=[pl.BlockSpec(memory_space=pltpu.VMEM)] * 4,
        out_specs=(pl.BlockSpec(memory_space=pltpu.VMEM),) * 2,
        compiler_params=pltpu.CompilerParams(vmem_limit_bytes=VMEM_LIMIT),
    )(mem, g, w_kv, gk)


def _causal_mask():
    row = lax.broadcasted_iota(jnp.int32, (TA, TA), 0)
    col = lax.broadcasted_iota(jnp.int32, (TA, TA), 1)
    return col <= row


def _causal_mask_t():
    row = lax.broadcasted_iota(jnp.int32, (TA, TA), 0)
    col = lax.broadcasted_iota(jnp.int32, (TA, TA), 1)
    return row <= col


def _fox_fwd(q_aug, k_aug, vt_aug, w_kv16, w_out16):
    s_len = q_aug.shape[1]
    n_tiles = s_len // TA
    hb = HB_FWD
    n_groups = FOX_HEADS // hb

    def body(q_ref, k_ref, vt_ref, kv16_ref, out16_ref, o_ref, st_ref, kv_all, out_all, send_sems, recv_sems, local_sems):
        qi = pl.program_id(1)
        remote, local = _row_block_exchange(kv16_ref, out16_ref, kv_all, out_all, send_sems, recv_sems, local_sems, gather=True)

        @pl.when((pl.program_id(0) == 0) & (qi == 0))
        def _():
            for cp in remote + local:
                cp.start()

        qs = [q_ref[h] for h in range(hb)]

        def step(t0, carry, masked, width):
            rows = pl.ds(pl.multiple_of(t0 * TA, TA), width * TA)
            scores = [_dot_nt(k_ref[h, rows, :], qs[h]) for h in range(hb)]
            soft = []
            for h in range(hb):
                m, _ = carry[h]
                s = jnp.where(_causal_mask_t(), scores[h], -1e30) if masked else scores[h]
                m_new = jnp.maximum(m, jnp.max(s, axis=0, keepdims=True))
                soft.append((m_new, jnp.exp(m - m_new), jnp.exp(s - m_new).astype(BF16)))
            out = []
            for h, (m_new, alpha, p) in enumerate(soft):
                acc = alpha * carry[h][1]
                for t in range(width):
                    acc = acc + _dot(vt_ref[h, t0 + t, 0:AUG_ROWS, :], p[t * TA:(t + 1) * TA])
                out.append((m_new, acc))
            return tuple(out)

        init = tuple((jnp.full((1, TA), -1e30, F32), jnp.zeros((AUG_ROWS, TA), F32)) for _ in range(hb))
        carry = lax.fori_loop(0, qi // 2, lambda j, cr: step(2 * j, cr, False, 2), init)
        carry = lax.fori_loop(2 * (qi // 2), qi, lambda j, cr: step(j, cr, False, 1), carry)
        carry = step(qi, carry, True, 1)
        for h in range(hb):
            m, acc = carry[h]
            l = acc[HEAD_DIM:HEAD_DIM + 1, :]
            lse = m + jnp.log(l)
            o_ref[h] = jnp.concatenate([acc[0:HEAD_DIM] / l, jnp.broadcast_to(lse, (HEAD_DIM, TA))], axis=0).T
            st_ref[h, 0] = jnp.broadcast_to(lse, (8, TA))

        @pl.when((pl.program_id(0) == n_groups - 1) & (qi == n_tiles - 1))
        def _():
            for cp in remote:
                cp.wait_recv()
            for cp in remote:
                cp.wait_send()
            for cp in local:
                cp.wait()

    hbm = pl.BlockSpec(memory_space=pl.ANY)
    return pl.pallas_call(
        body,
        name="fox_fwd",
        grid=(n_groups, n_tiles),
        in_specs=[pl.BlockSpec((hb, TA, 128), lambda g, i: (g, i, 0)), pl.BlockSpec((hb, s_len, 128), lambda g, i: (g, 0, 0)),
                  pl.BlockSpec((hb, n_tiles, 128, TA), lambda g, i: (g, 0, 0, 0)), hbm, hbm],
        out_specs=[pl.BlockSpec((hb, TA, 128), lambda g, i: (g, i, 0)), pl.BlockSpec((hb, 1, 8, TA), lambda g, i: (g, i, 0, 0)),
                   hbm, hbm],
        out_shape=[jax.ShapeDtypeStruct((FOX_HEADS, s_len, 128), F32), jax.ShapeDtypeStruct((FOX_HEADS, n_tiles, 8, TA), F32),
                   jax.ShapeDtypeStruct((D_MODEL, 512), BF16), jax.ShapeDtypeStruct((D_MODEL, D_MODEL), BF16)],
        scratch_shapes=[pltpu.SemaphoreType.DMA((14,)), pltpu.SemaphoreType.DMA((14,)), pltpu.SemaphoreType.DMA((2,))],
        compiler_params=_params(("arbitrary", "arbitrary")),
    )(q_aug, k_aug, vt_aug, w_kv16, w_out16)


def _lane_group(lane, a, b, c, d):
    return jnp.where(lane < 64, a, jnp.where(lane < 128, b, jnp.where(lane < 192, c, d)))


def _pool_count(row0):
    lane = lax.broadcasted_iota(jnp.int32, (TM, POOL_WIDTH), 1)
    t1 = row0 + lax.broadcasted_iota(jnp.int32, (TM, POOL_WIDTH), 0) + 1
    return 1.0 / jnp.minimum(t1, _lane_group(lane, 2, 4, 8, 16)).astype(F32), lane


def _pool_delta(u, halo, cnt, lane):
    xe = jnp.concatenate([halo, u], axis=0)
    s2 = xe + pltpu.roll(xe, 1, 0)
    s4 = s2 + pltpu.roll(s2, 2, 0)
    s8 = s4 + pltpu.roll(s4, 4, 0)
    s16 = s8 + pltpu.roll(s8, 8, 0)
    win = _lane_group(lane, s2[HALO:], s4[HALO:], s8[HALO:], s16[HALO:])
    return win * cnt - u


def _pool_delta_bwd(dd, dd_next, cnt, cnt_next, lane):
    ee = jnp.concatenate([dd * cnt, dd_next * cnt_next], axis=0)
    n = TM + HALO
    r2 = ee + pltpu.roll(ee, n - 1, 0)
    r4 = r2 + pltpu.roll(r2, n - 2, 0)
    r8 = r4 + pltpu.roll(r4, n - 4, 0)
    r16 = r8 + pltpu.roll(r8, n - 8, 0)
    return _lane_group(lane, r2[:TM], r4[:TM], r8[:TM], r16[:TM]) - dd


def _mem_attn(qm, gq, seg, km_ref, vm_ref):
    r = _head_rms(qm, seg)
    a = qm * r
    q16 = (a * gq * 0.125).astype(BF16)
    heads = []
    for h in range(MEM_HEADS):
        sl = slice(HEAD_DIM * h, HEAD_DIM * (h + 1))
        s = _dot_nt(q16[:, sl], km_ref[:, sl])
        e = jnp.exp(s - jnp.max(s, axis=-1, keepdims=True))
        p = e * (1.0 / jnp.sum(e, axis=-1, keepdims=True))
        heads.append((p, _dot(p.astype(BF16), vm_ref[:, sl])))
    return a, r, q16, heads


def _row_specs(s_len):
    n_halo = s_len // HALO
    piece = lambda j: pl.BlockSpec((TM, PIECE), lambda i: (i, j))
    before = lambda j: pl.BlockSpec((HALO, 256), lambda i: (jnp.maximum(i * (TM // HALO) - 1, 0), j))
    after = lambda j: pl.BlockSpec((HALO, 256), lambda i: (jnp.minimum((i + 1) * (TM // HALO), n_halo - 1), j))
    return piece, before, after


def _mix_fwd(proj, olse, km, vm, wp_bd, pool_scale, gq_m, seg):
    s_len = proj.shape[0]

    def body(ua_ref, halo_ref, gb_ref, qm_ref, ol_ref, km_ref, vm_ref, wp_ref, sc_ref, gq_ref, seg_ref, out_ref):
        i = pl.program_id(0)
        u = ua_ref[:, 0:256]
        g_a = ua_ref[:, 256:512]
        halo = jnp.where(i > 0, halo_ref[...], 0.0)
        cnt, lane = _pool_count(i * TM)
        d = _pool_delta(u, halo, cnt, lane).astype(BF16)
        y_a = _dot(d, wp_ref[...]) * sc_ref[...]
        out_ref[:, 0:256] = (y_a * (g_a * _sigmoid(g_a))).astype(BF16)
        g_b = gb_ref[...]
        y_b = jnp.concatenate([ol_ref[h][:, 0:HEAD_DIM] for h in range(FOX_HEADS)], axis=1)
        out_ref[:, 256:768] = (y_b * (g_b * _sigmoid(g_b))).astype(BF16)
        _, _, _, heads = _mem_attn(qm_ref[:, 0:256], gq_ref[...], seg_ref[0:256, 0:256], km_ref, vm_ref)
        g_m = qm_ref[:, 256:512]
        y_m = jnp.concatenate([y for _, y in heads], axis=1)
        out_ref[:, 768:1024] = (y_m * (g_m * _sigmoid(g_m))).astype(BF16)

    piece, before, _ = _row_specs(s_len)
    return pl.pallas_call(
        body,
        name="mix_fwd",
        grid=(s_len // TM,),
        in_specs=[piece(0), before(0), piece(4), piece(5), pl.BlockSpec((FOX_HEADS, TM, 128), lambda i: (0, i, 0)),
                  _full((N_MEM, 256)), _full((N_MEM, 256)), _full((256, 256)), _full((1, 256)), _full((1, 256)),
                  _full((PIECE, PIECE))],
        out_specs=pl.BlockSpec((TM, D_MODEL), lambda i: (i, 0)),
        out_shape=jax.ShapeDtypeStruct((s_len, D_MODEL), BF16),
        compiler_params=_params(("parallel",)),
    )(proj, proj, proj, proj, olse, km, vm, wp_bd, pool_scale, gq_m, seg)


def _out_loss(mixed, x, target, w_out):
    s_len = x.shape[0]

    def body(mix_ref, x_ref, t_ref, w_ref, dout_ref, dmix_ref, dw16_ref, loss_ref, dw_ref):
        @pl.when(pl.program_id(0) == 0)
        def _():
            dw_ref[...] = jnp.zeros((D_MODEL, D_MODEL), F32)
            loss_ref[...] = jnp.zeros((8, 128), F32)

        mixed16 = mix_ref[...]
        err = x_ref[...] + _dot(mixed16, w_ref[...]) - t_ref[...]
        loss_ref[...] += 0.5 * jnp.sum(jnp.mean(err * err, axis=-1, keepdims=True))
        d_out = err / float(D_MODEL)
        dout_ref[...] = d_out
        d16 = d_out.astype(BF16)
        dmix_ref[...] = _dot_nt(d16, w_ref[...])
        dw_ref[...] += _dot_tn(mixed16, d16)

        @pl.when(pl.program_id(0) == pl.num_programs(0) - 1)
        def _():
            dw16_ref[...] = dw_ref[...].astype(BF16)

    row = pl.BlockSpec((TMM, D_MODEL), lambda i: (i, 0))
    return pl.pallas_call(
        body,
        name="out_loss",
        grid=(s_len // TMM,),
        in_specs=[row, row, row, _full((D_MODEL, D_MODEL))],
        out_specs=[row, row, _full((D_MODEL, D_MODEL)), _full((8, 128))],
        out_shape=[jax.ShapeDtypeStruct((s_len, D_MODEL), F32), jax.ShapeDtypeStruct((s_len, D_MODEL), F32),
                   jax.ShapeDtypeStruct((D_MODEL, D_MODEL), BF16), jax.ShapeDtypeStruct((8, 128), F32)],
        scratch_shapes=[pltpu.VMEM((D_MODEL, D_MODEL), F32)],
        compiler_params=_params(("arbitrary",)),
    )(mixed, x, target, w_out)


def _silu_pair(g):
    s = _sigmoid(g)
    return g * s, s * (1.0 + g * (1.0 - s))


def _mix_bwd(proj, olse, d_mixed, km, vm, wp_bd, pool_scale, gq_m, seg):
    s_len = proj.shape[0]
    n_tiles = s_len // TM

    def body(ua_ref, halo_ref, ga_next_ref, gb_ref, qm_ref, ol_ref, dm_ref, dm_next_ref, km_ref, vm_ref, wp_ref, sc_ref, gq_ref,
             seg_ref, dua_ref, dgb_ref, dqm_ref, do_ref, dl_ref, dwp_ref, dsc_ref, dkm_ref, dvm_ref, dgq_ref):
        i = pl.program_id(0)

        @pl.when(i == 0)
        def _():
            dwp_ref[...] = jnp.zeros((256, 256), F32)
            dsc_ref[...] = jnp.zeros((1, 256), F32)
            dkm_ref[...] = jnp.zeros((N_MEM, 256), F32)
            dvm_ref[...] = jnp.zeros((N_MEM, 256), F32)
            dgq_ref[...] = jnp.zeros((1, HEAD_DIM), F32)

        u = ua_ref[:, 0:256]
        g_a = ua_ref[:, 256:512]
        halo = jnp.where(i > 0, halo_ref[...], 0.0)
        cnt, lane = _pool_count(i * TM)
        d16 = _pool_delta(u, halo, cnt, lane).astype(BF16)
        t = _dot(d16, wp_ref[...])
        y_a = t * sc_ref[...]
        silu_a, dsilu_a = _silu_pair(g_a)
        dm_a = dm_ref[:, 0:256]
        dy_a = dm_a * silu_a
        dsc_ref[...] += jnp.sum(dy_a * t, axis=0, keepdims=True)
        dt16 = (dy_a * sc_ref[...]).astype(BF16)
        dwp_ref[...] += _dot_tn(d16, dt16)
        dd = _dot_nt(dt16, wp_ref[...])
        g_next = ga_next_ref[...]
        dt_next = (dm_next_ref[...] * (g_next * _sigmoid(g_next)) * sc_ref[...]).astype(BF16)
        dd_next = jnp.where(i < n_tiles - 1, _dot_nt(dt_next, wp_ref[...]), 0.0)
        cnt_next = _pool_count((i + 1) * TM)[0][0:HALO]
        dua_ref[:, 0:256] = _pool_delta_bwd(dd, dd_next, cnt, cnt_next, lane).astype(BF16)
        dua_ref[:, 256:512] = (dm_a * y_a * dsilu_a).astype(BF16)

        silu_b, dsilu_b = _silu_pair(gb_ref[...])
        dm_b = dm_ref[:, 256:768]
        o_all = jnp.concatenate([ol_ref[h][:, 0:HEAD_DIM] for h in range(FOX_HEADS)], axis=1)
        d_o = dm_b * silu_b
        dgb_ref[...] = (dm_b * o_all * dsilu_b).astype(BF16)
        for h in range(FOX_HEADS):
            do_ref[h] = d_o[:, HEAD_DIM * h:HEAD_DIM * (h + 1)].astype(BF16)
        prod = d_o * o_all
        hi = prod.astype(BF16)
        lo = (prod - hi.astype(F32)).astype(BF16)
        head_of_lane = lax.broadcasted_iota(jnp.int32, (FOX_HEADS, PIECE), 1) // HEAD_DIM
        pick = jnp.where(head_of_lane == lax.broadcasted_iota(jnp.int32, (FOX_HEADS, PIECE), 0), 1.0, 0.0).astype(BF16)
        delta = _dot_nt(pick, hi) + _dot_nt(pick, lo)
        for h in range(FOX_HEADS):
            dl_ref[h, 0] = jnp.broadcast_to(delta[h:h + 1, :], (8, TM))

        seg = seg_ref[0:256, 0:256]
        a, r, q16, heads = _mem_attn(qm_ref[:, 0:256], gq_ref[...], seg, km_ref, vm_ref)
        silu_m, dsilu_m = _silu_pair(qm_ref[:, 256:512])
        dm_m = dm_ref[:, 768:1024]
        y_m = jnp.concatenate([y for _, y in heads], axis=1)
        dqm_ref[:, 256:512] = (dm_m * y_m * dsilu_m).astype(BF16)
        dy16_all = (dm_m * silu_m).astype(BF16)
        d_scores = []
        for h in range(MEM_HEADS):
            sl = slice(HEAD_DIM * h, HEAD_DIM * (h + 1))
            p = heads[h][0]
            dy16 = dy16_all[:, sl]
            dp = _dot_nt(dy16, vm_ref[:, sl])
            dvm_ref[:, sl] += _dot_tn(p.astype(BF16), dy16)
            ds16 = (p * (dp - jnp.sum(dp * p, axis=-1, keepdims=True))).astype(BF16)
            dkm_ref[:, sl] += _dot_tn(ds16, q16[:, sl])
            d_scores.append(_dot(ds16, km_ref[:, sl]))
        dq, dg_rows = _head_rms_bwd(a, r, gq_ref[...], jnp.concatenate(d_scores, axis=1) * 0.125, seg)
        dqm_ref[:, 0:256] = dq.astype(BF16)
        dgq_ref[...] += _fold_heads(jnp.sum(dg_rows, axis=0, keepdims=True), MEM_HEADS)

    piece, before, after = _row_specs(s_len)
    n_halo = s_len // HALO
    row = pl.BlockSpec((TM, D_MODEL), lambda i: (i, 0))
    dm_after = pl.BlockSpec((HALO, 256), lambda i: (jnp.minimum((i + 1) * (TM // HALO), n_halo - 1), 0))
    out_piece = pl.BlockSpec((TM, PIECE), lambda i: (i, 0))
    return pl.pallas_call(
        body,
        name="mix_bwd",
        grid=(n_tiles,),
        in_specs=[piece(0), before(0), after(1), piece(4), piece(5), pl.BlockSpec((FOX_HEADS, TM, 128), lambda i: (0, i, 0)),
                  row, dm_after, _full((N_MEM, 256)), _full((N_MEM, 256)), _full((256, 256)), _full((1, 256)), _full((1, 256)),
                  _full((PIECE, PIECE))],
        out_specs=[out_piece, out_piece, out_piece, pl.BlockSpec((FOX_HEADS, TM, HEAD_DIM), lambda i: (0, i, 0)),
                   pl.BlockSpec((FOX_HEADS, 1, 8, TM), lambda i: (0, i, 0, 0)),
                   _full((256, 256)), _full((1, 256)), _full((N_MEM, 256)), _full((N_MEM, 256)), _full((1, HEAD_DIM))],
        out_shape=[jax.ShapeDtypeStruct((s_len, PIECE), BF16)] * 3 + [
            jax.ShapeDtypeStruct((FOX_HEADS, s_len, HEAD_DIM), BF16),
            jax.ShapeDtypeStruct((FOX_HEADS, n_tiles, 8, TM), F32),
            jax.ShapeDtypeStruct((256, 256), F32), jax.ShapeDtypeStruct((1, 256), F32),
            jax.ShapeDtypeStruct((N_MEM, 256), F32), jax.ShapeDtypeStruct((N_MEM, 256), F32),
            jax.ShapeDtypeStruct((1, HEAD_DIM), F32)],
        compiler_params=_params(("arbitrary",)),
    )(proj, proj, proj, proj, proj, olse, d_mixed, d_mixed, km, vm, wp_bd, pool_scale, gq_m, seg)


def _fox_bwd(q_aug, k_aug, kt_aug, v_h, d_o, lse_rows, delta_rows, dw_kv16, dw_out16):
    s_len = q_aug.shape[1]
    n_tiles = s_len // TA
    n_groups = FOX_HEADS // HB

    def body(q_ref, k_ref, kt_ref, v_ref, do_ref, st_ref, dl_ref, dkv16_ref, dout16_ref, dq_ref, dk_ref, dv_ref, land_kv, land_out,
             dqt_ref, send_sems, recv_sems, local_sems):
        j = pl.program_id(1)
        remote, local = _row_block_exchange(dkv16_ref, dout16_ref, land_kv, land_out, send_sems, recv_sems, local_sems, gather=False)

        @pl.when((pl.program_id(0) == 0) & (j == 0))
        def _():
            for cp in remote + local:
                cp.start()

        @pl.when(j == 0)
        def _():
            dqt_ref[...] = jnp.zeros((HB, n_tiles, AUG_ROWS, TA), F32)

        ks = [k_ref[h] for h in range(HB)]
        kts = [kt_ref[h, 0, 0:AUG_ROWS, :] for h in range(HB)]
        vs = [v_ref[h] for h in range(HB)]

        def pair(i0, acc, masked, width):
            rows = pl.ds(pl.multiple_of(i0 * TA, TA), width * TA)
            qs = [q_ref[h, rows, :] for h in range(HB)]
            d_os = [do_ref[h, rows, :] for h in range(HB)]
            row_stat = lambda ref, h: jnp.concatenate([ref[h, i0 + t, 0:1, :] for t in range(width)], axis=1)
            scores = [(_dot_nt(ks[h], qs[h]), _dot_nt(vs[h], d_os[h])) for h in range(HB)]
            probs = []
            for h in range(HB):
                s, dp = scores[h]
                if masked:
                    s = jnp.where(_causal_mask_t(), s, -1e30)
                p = jnp.exp(s - row_stat(st_ref, h))
                probs.append((p.astype(BF16), (p * (dp - row_stat(dl_ref, h))).astype(BF16)))
            out = []
            for h in range(HB):
                p16, ds16 = probs[h]
                dqt = _dot(kts[h], ds16)
                for t in range(width):
                    dqt_ref[h, i0 + t] += dqt[:, t * TA:(t + 1) * TA]
                out.append((acc[h][0] + _dot(ds16, qs[h]), acc[h][1] + _dot(p16, d_os[h])))
            return tuple(out)

        zero = tuple((jnp.zeros((TA, 128), F32), jnp.zeros((TA, HEAD_DIM), F32)) for _ in range(HB))
        acc = pair(j, zero, True, 1)
        odd = (n_tiles - 1 - j) % 2
        acc = lax.fori_loop(j + 1, j + 1 + odd, lambda i, a: pair(i, a, False, 1), acc)
        acc = lax.fori_loop(0, (n_tiles - 1 - j) // 2, lambda t, a: pair(j + 1 + odd + 2 * t, a, False, 2), acc)
        pad = jnp.zeros((128 - AUG_ROWS, TA), F32)
        for h in range(HB):
            dk_ref[h] = acc[h][0]
            dv_ref[h] = acc[h][1]
            dq_ref[h] = jnp.concatenate([dqt_ref[h, j], pad], axis=0).T

        @pl.when((pl.program_id(0) == n_groups - 1) & (j == n_tiles - 1))
        def _():
            for cp in remote:
                cp.wait_recv()
            for cp in remote:
                cp.wait_send()
            for cp in local:
                cp.wait()

    assert n_groups == 1
    resident = pl.BlockSpec(memory_space=pltpu.VMEM)
    whole = lambda w: resident
    rows_blk = lambda r: resident
    tile = lambda w: pl.BlockSpec((HB, TA, w), lambda g, j: (g, j, 0))
    hbm = pl.BlockSpec(memory_space=pl.ANY)
    return pl.pallas_call(
        body,
        name="fox_bwd",
        grid=(n_groups, n_tiles),
        in_specs=[whole(128), tile(128), pl.BlockSpec((HB, 1, 128, TA), lambda g, j: (g, j, 0, 0)), tile(HEAD_DIM),
                  whole(HEAD_DIM), rows_blk(8), rows_blk(8), hbm, hbm],
        out_specs=[tile(128), tile(128), tile(HEAD_DIM), hbm, hbm],
        out_shape=[jax.ShapeDtypeStruct((FOX_HEADS, s_len, 128), F32), jax.ShapeDtypeStruct((FOX_HEADS, s_len, 128), F32),
                   jax.ShapeDtypeStruct((FOX_HEADS, s_len, HEAD_DIM), F32),
                   jax.ShapeDtypeStruct((N_DEV, 128, 512), BF16), jax.ShapeDtypeStruct((N_DEV, 128, D_MODEL), BF16)],
        scratch_shapes=[pltpu.VMEM((HB, n_tiles, AUG_ROWS, TA), F32),
                        pltpu.SemaphoreType.DMA((14,)), pltpu.SemaphoreType.DMA((14,)), pltpu.SemaphoreType.DMA((2,))],
        compiler_params=_params(("arbitrary", "arbitrary")),
    )(q_aug, k_aug, kt_aug, v_h, d_o, lse_rows, delta_rows, dw_kv16, dw_out16)


def _fox_post(proj, bf_pad, gq, gk, seg, dq_aug, dk_aug, dv_h):
    s_len = proj.shape[0]
    n_tiles = s_len // TM

    def body(q_ref, k_ref, f_ref, bf_ref, gq_ref, gk_ref, seg_ref, dqa_ref, dka_ref, dvh_ref,
             dq_ref, dk_ref, dv_ref, df_ref, dgq_ref, dgk_ref, dbf_ref, carry_ref):
        @pl.when(pl.program_id(0) == 0)
        def _():
            carry_ref[...] = jnp.zeros((1, 128), F32)
            dgq_ref[...] = jnp.zeros((1, HEAD_DIM), F32)
            dgk_ref[...] = jnp.zeros((1, HEAD_DIM), F32)
            dbf_ref[...] = jnp.zeros((1, 128), F32)

        lane = lax.broadcasted_iota(jnp.int32, (TM, 128), 1)
        seg = seg_ref[...]
        dqas = [dqa_ref[h] for h in range(FOX_HEADS)]
        dkas = [dka_ref[h] for h in range(FOX_HEADS)]
        d_cum = jnp.zeros((TM, 128), F32)
        for h in range(FOX_HEADS):
            d_cum = jnp.where(lane == h, dqas[h][:, 64:65] - dkas[h][:, 67:68], d_cum)
        q_all = q_ref[...]
        k_all = k_ref[...]
        rq = _head_rms(q_all, seg)
        rk = _head_rms(k_all, seg)
        dy_q = jnp.concatenate([t[:, 0:HEAD_DIM] for t in dqas], axis=1) * 0.125
        dy_k = jnp.concatenate([t[:, 0:HEAD_DIM] for t in dkas], axis=1)
        dq, dgq_rows = _head_rms_bwd(q_all * rq, rq, gq_ref[...], dy_q, seg)
        dk, dgk_rows = _head_rms_bwd(k_all * rk, rk, gk_ref[...], dy_k, seg)
        dq_ref[...] = dq.astype(BF16)
        dk_ref[...] = dk.astype(BF16)
        dv_ref[...] = jnp.concatenate([dvh_ref[h] for h in range(FOX_HEADS)], axis=1).astype(BF16)
        dgq_ref[...] += _fold_heads(jnp.sum(dgq_rows, axis=0, keepdims=True), FOX_HEADS)
        dgk_ref[...] += _fold_heads(jnp.sum(dgk_rows, axis=0, keepdims=True), FOX_HEADS)

        row = lax.broadcasted_iota(jnp.int32, (TM, TM), 0)
        col = lax.broadcasted_iota(jnp.int32, (TM, TM), 1)
        tri = jnp.where(col >= row, 1.0, 0.0).astype(BF16)
        hi, mid, lo = _split3(d_cum)
        d_logf = _dot(tri, hi) + _dot(tri, mid) + _dot(tri, lo) + carry_ref[...]
        carry_ref[...] = d_logf[0:1, :]
        z = f_ref[...] + bf_ref[...]
        d_f = jnp.where(lane < FOX_HEADS, d_logf / (1.0 + jnp.exp(z)), 0.0)
        df_ref[...] = d_f.astype(BF16)
        dbf_ref[...] += jnp.sum(d_f, axis=0, keepdims=True)

    rev = lambda i: n_tiles - 1 - i
    col_blk = lambda j: pl.BlockSpec((TM, PIECE), lambda i: (rev(i), j))
    head_blk = lambda w: pl.BlockSpec((FOX_HEADS, TM, w), lambda i: (0, rev(i), 0))
    out_piece = pl.BlockSpec((TM, PIECE), lambda i: (rev(i), 0))
    return pl.pallas_call(
        body,
        name="fox_post",
        grid=(n_tiles,),
        in_specs=[col_blk(1), col_blk(2), pl.BlockSpec((TM, 128), lambda i: (rev(i), MY_F_OFF // 128)),
                  _full((1, 128)), _full((1, PIECE)), _full((1, PIECE)), _full((PIECE, PIECE)),
                  head_blk(128), head_blk(128), head_blk(HEAD_DIM)],
        out_specs=[out_piece, out_piece, out_piece, pl.BlockSpec((TM, 128), lambda i: (rev(i), 0)),
                   _full((1, HEAD_DIM)), _full((1, HEAD_DIM)), _full((1, 128))],
        out_shape=[jax.ShapeDtypeStruct((s_len, PIECE), BF16)] * 3 + [
            jax.ShapeDtypeStruct((s_len, 128), BF16), jax.ShapeDtypeStruct((1, HEAD_DIM), F32),
            jax.ShapeDtypeStruct((1, HEAD_DIM), F32), jax.ShapeDtypeStruct((1, 128), F32)],
        scratch_shapes=[pltpu.VMEM((1, 128), F32)],
        compiler_params=_params(("arbitrary",)),
    )(proj, proj, proj, bf_pad, gq, gk, seg, dq_aug, dk_aug, dv_h)


def _mem_bwd(mem, g, w_kv, gk, dkm, dvm):
    def body(mem_ref, g_ref, w_ref, gk_ref, dkm_ref, dvm_ref, dw_ref, dg_ref, dgk_ref, dkv_ref):
        m = mem_ref[...]
        r = _rms(m)
        a = m * r
        mn16 = (a * g_ref[...]).astype(BF16)
        kv = _dot(mn16, w_ref[...])
        dgk = jnp.zeros((N_MEM, HEAD_DIM), F32)
        for h in range(MEM_HEADS):
            sl = slice(HEAD_DIM * h, HEAD_DIM * (h + 1))
            kh = kv[:, sl]
            rk = _rms(kh)
            dk, dg_rows = _rms_bwd(kh * rk, rk, gk_ref[...], dkm_ref[:, sl])
            dkv_ref[:, sl] = dk.astype(BF16)
            dgk = dgk + dg_rows
        dkv_ref[:, 256:512] = dvm_ref[...].astype(BF16)
        dgk_ref[...] = jnp.sum(dgk, axis=0, keepdims=True)
        dkv16 = dkv_ref[...]
        dw_ref[...] = _dot_tn(mn16, dkv16).astype(BF16)
        dg_ref[...] = jnp.sum(_dot_nt(dkv16, w_ref[...]) * a, axis=0, keepdims=True)

    return pl.pallas_call(
        body,
        name="mem_bwd",
        out_shape=(jax.ShapeDtypeStruct((D_MODEL, 512), BF16), jax.ShapeDtypeStruct((1, D_MODEL), F32),
                   jax.ShapeDtypeStruct((1, HEAD_DIM), F32)),
        in_specs=[pl.BlockSpec(memory_space=pltpu.VMEM)] * 6,
        out_specs=(pl.BlockSpec(memory_space=pltpu.VMEM),) * 3,
        scratch_shapes=[pltpu.VMEM((N_MEM, 512), BF16)],
        compiler_params=pltpu.CompilerParams(vmem_limit_bytes=VMEM_LIMIT),
    )(mem, g, w_kv, gk, dkm, dvm)


def _grad_x_exchange(pieces, d_f, w_my, x, norm_g, d_out, dw_in, land_kv, land_out, d_mem_g, d_scale, d_bf, d_gq, d_gk, d_gqm,
                     d_gkm, dwp_bd, loss_blk):
    s_len = x.shape[0]
    n_steps = s_len // TM
    step2, step3 = n_steps // 4, (3 * n_steps) // 4
    half = D_MODEL // 2

    def body(p0, p1, p2, p3, p4, p5, df_ref, x_ref, dout_ref, w_ref, g_ref, in_ref, lkv_ref, lout_ref,
             dmg, dsc, dbf, dgq, dgk, dgqm, dgkm, dwp, loss_ref,
             gx_ref, rin_ref, rkv_ref, rout_ref, sred_ref,
             dng, land1, send2a, send2b, keep2a, keep2b, land2a, land2b, send3a, send3b, land3a, land3b, sb_ref, sland,
             send_sems, recv_sems):
        i = pl.program_id(0)
        x_, y_, c_, me = _my_place()
        sibling, x_nbr, y_nbr = (x_, y_, 1 - c_), (1 - x_, y_, c_), (x_, 1 - y_, c_)

        def rcopy(src, dst, sem, to):
            return pltpu.make_async_remote_copy(src_ref=src, dst_ref=dst, send_sem=send_sems.at[sem], recv_sem=recv_sems.at[sem],
                                                device_id=to, device_id_type=MESH)

        small = []
        for k in range(1, N_DEV):
            px, py, pc = x_ ^ (k >> 2), y_ ^ ((k >> 1) & 1), c_ ^ (k & 1)
            small.append(rcopy(sb_ref, sland.at[me], k - 1, (px, py, pc)))
        stage1 = [rcopy(in_ref.at[2 * k + 1 - c_], land1.at[k], 7 + k, sibling) for k in range(4)]
        stage2 = [rcopy(send2a.at[j], land2a.at[j], 11 + j, x_nbr) for j in range(2)]
        stage2 += [rcopy(send2b.at[j], land2b.at[j], 13 + j, y_nbr) for j in range(2)]
        stage3 = [rcopy(send3a, land3a, 15, y_nbr), rcopy(send3b, land3b, 16, x_nbr)]
        top, bot = slice(0, half), slice(half, D_MODEL)

        @pl.when(i == 0)
        def _():
            dng[...] = jnp.zeros((1, D_MODEL), F32)
            for cp in stage1:
                cp.start()

        @pl.when(i == step2)
        def _():
            for cp in stage1:
                cp.wait_recv()

            def chip_sum(k, cols):
                return in_ref[2 * k + c_, :, cols].astype(F32) + land1[k, :, cols].astype(F32)

            for j in range(2):
                send2a[j] = chip_sum(2 * (1 - x_) + j, top).astype(BF16)
                keep2a[j] = chip_sum(2 * x_ + j, top)
                send2b[j] = chip_sum(2 * j + 1 - y_, bot).astype(BF16)
                keep2b[j] = chip_sum(2 * j + y_, bot)
            for cp in stage2:
                cp.start()

        @pl.when(i == step3)
        def _():
            for cp in stage2:
                cp.wait_recv()
            for j in range(2):
                keep2a[j] = keep2a[j] + land2a[j].astype(F32)
                keep2b[j] = keep2b[j] + land2b[j].astype(F32)
            send3a[...] = keep2a[1 - y_].astype(BF16)
            send3b[...] = keep2b[1 - x_].astype(BF16)
            for cp in stage3:
                cp.start()

        dh = _dot_nt(df_ref[...], w_ref[:, MY_F_OFF:MY_WIDTH])
        for j, p in enumerate((p0, p1, p2, p3, p4, p5)):
            dh = dh + _dot_nt(p[...], w_ref[:, PIECE * j:PIECE * (j + 1)])
        xv = x_ref[...]
        r = _rms(xv)
        dx, dg_rows = _rms_bwd(xv * r, r, g_ref[...], dh)
        gx_ref[...] = dout_ref[...] + dx
        dng[...] += jnp.sum(dg_rows, axis=0, keepdims=True)

        @pl.when(i == n_steps - 1)
        def _():
            sb_ref[...] = jnp.zeros((SB_ROWS, SB_W), F32)
            for k in range(4):
                sb_ref[SB_NORM_G + k:SB_NORM_G + k + 1, :] = dng[:, SB_W * k:SB_W * (k + 1)]
                sb_ref[SB_MEM_NORM_G + k:SB_MEM_NORM_G + k + 1, :] = dmg[:, SB_W * k:SB_W * (k + 1)]
            sb_ref[SB_POOL_SCALE:SB_POOL_SCALE + 1, :] = dsc[...]
            sb_ref[SB_B_F:SB_B_F + 1, 0:128] = dbf[...]
            for row, ref in ((SB_FOX_Q, dgq), (SB_FOX_K, dgk), (SB_MEM_Q, dgqm), (SB_MEM_K, dgkm)):
                sb_ref[row:row + 1, 0:HEAD_DIM] = ref[...]
            sb_ref[SB_LOSS:SB_LOSS + 1, 0:128] = loss_ref[0:1, :]
            for grp in range(4):
                sl = slice(64 * grp, 64 * (grp + 1))
                sb_ref[SB_W_POOL:SB_W_POOL + 64, sl] = dwp[sl, sl]
            for cp in small:
                cp.start()
            sland[me] = sb_ref[...]
            for cp in stage3:
                cp.wait_recv()
            rin_ref[:, top] = keep2a[y_] + land3a[...].astype(F32)
            rin_ref[:, bot] = keep2b[x_] + land3b[...].astype(F32)
            for cp in small:
                cp.wait_recv()
            for cp in small + stage1 + stage2 + stage3:
                cp.wait_send()
            for land, red in ((lkv_ref, rkv_ref), (lout_ref, rout_ref), (sland, sred_ref)):
                acc = land[0].astype(F32)
                for d in range(1, N_DEV):
                    acc = acc + land[d].astype(F32)
                red[...] = acc

    piece = pl.BlockSpec((TM, PIECE), lambda i: (i, 0))
    row = pl.BlockSpec((TM, D_MODEL), lambda i: (i, 0))
    vmem = pl.BlockSpec(memory_space=pltpu.VMEM)
    half_chunk = lambda n, dt: pltpu.VMEM((n, CHUNK_ROWS, half), dt)
    whole = (w_my, norm_g, dw_in, land_kv, land_out, d_mem_g, d_scale, d_bf, d_gq, d_gk, d_gqm, d_gkm, dwp_bd, loss_blk)
    return pl.pallas_call(
        body,
        name="grad_x_exchange",
        grid=(n_steps,),
        in_specs=[piece] * 6 + [pl.BlockSpec((TM, 128), lambda i: (i, 0)), row, row] + [vmem] * len(whole),
        out_specs=[row, vmem, vmem, vmem, vmem],
        out_shape=[jax.ShapeDtypeStruct((s_len, D_MODEL), F32), jax.ShapeDtypeStruct((CHUNK_ROWS, D_MODEL), F32),
                   jax.ShapeDtypeStruct((128, 512), F32), jax.ShapeDtypeStruct((128, D_MODEL), F32),
                   jax.ShapeDtypeStruct((SB_ROWS, SB_W), F32)],
        scratch_shapes=[
            pltpu.VMEM((1, D_MODEL), F32),
            pltpu.VMEM((4, CHUNK_ROWS, D_MODEL), BF16),
            half_chunk(2, BF16), half_chunk(2, BF16), half_chunk(2, F32), half_chunk(2, F32), half_chunk(2, BF16), half_chunk(2, BF16),
            pltpu.VMEM((CHUNK_ROWS, half), BF16), pltpu.VMEM((CHUNK_ROWS, half), BF16),
            pltpu.VMEM((CHUNK_ROWS, half), BF16), pltpu.VMEM((CHUNK_ROWS, half), BF16),
            pltpu.VMEM((SB_ROWS, SB_W), F32),
            pltpu.VMEM((N_DEV, SB_ROWS, SB_W), F32),
            pltpu.SemaphoreType.DMA((17,)),
            pltpu.SemaphoreType.DMA((17,)),
        ],
        compiler_params=_params(("arbitrary",)),
    )(*pieces, d_f, x, d_out, *whole)


def _grad_w_in(h16, pieces, d_f):
    s_len = h16.shape[0]
    tk = 512

    def body(h_ref, p0, p1, p2, p3, p4, p5, df_ref, out_ref, dw_ref):
        @pl.when(pl.program_id(0) == 0)
        def _():
            dw_ref[...] = jnp.zeros((D_MODEL, MY_WIDTH), F32)

        h = h_ref[...]
        for j, p in enumerate((p0, p1, p2, p3, p4, p5)):
            dw_ref[:, PIECE * j:PIECE * (j + 1)] += _dot_tn(h, p[...])
        dw_ref[:, MY_F_OFF:MY_WIDTH] += _dot_tn(h, df_ref[...])

        @pl.when(pl.program_id(0) == pl.num_programs(0) - 1)
        def _():
            for d in range(N_DEV):
                parts = [dw_ref[:, start:start + width] for start, width in _chunk_segments(d)]
                parts.append(jnp.zeros((D_MODEL, 512 - IN_CHUNK), F32))
                out_ref[d] = jnp.concatenate(parts, axis=1).T[0:CHUNK_ROWS].astype(BF16)

    piece = pl.BlockSpec((tk, PIECE), lambda i: (i, 0))
    return pl.pallas_call(
        body,
        name="grad_w_in",
        grid=(s_len // tk,),
        in_specs=[pl.BlockSpec((tk, D_MODEL), lambda i: (i, 0))] + [piece] * 6 + [pl.BlockSpec((tk, 128), lambda i: (i, 0))],
        out_specs=_full((N_DEV, CHUNK_ROWS, D_MODEL)),
        out_shape=jax.ShapeDtypeStruct((N_DEV, CHUNK_ROWS, D_MODEL), BF16),
        scratch_shapes=[pltpu.VMEM((D_MODEL, MY_WIDTH), F32)],
        compiler_params=_params(("arbitrary",)),
    )(h16, *pieces, d_f)


def _adamw(w, g, m, v):
    m = ADAM_B1 * m + (1.0 - ADAM_B1) * g
    v = ADAM_B2 * v + (1.0 - ADAM_B2) * (g * g)
    m_hat = m / (1.0 - ADAM_B1 ** ADAM_STEP)
    v_hat = v / (1.0 - ADAM_B2 ** ADAM_STEP)
    return -ADAM_LR * (m_hat / (jnp.sqrt(v_hat) + ADAM_EPS) + ADAM_WD * w), m, v


PARAM_ORDER = ("norm_g", "w_in", "b_f", "w_pool", "pool_scale", "fox_q_g", "fox_k_g", "mem_norm_g", "w_mem_kv", "mem_q_g", "mem_k_g", "w_out")


def _update(red_in, red_kv, red_out, sred, params):
    def body(rin_ref, rkv_ref, rout_ref, sred_ref, *refs):
        ins, outs = refs[:3 * len(PARAM_ORDER)], refs[3 * len(PARAM_ORDER):]
        wide = lambda row: jnp.concatenate([sred_ref[row + k:row + k + 1, :] for k in range(4)], axis=1)
        head = lambda row: sred_ref[row:row + 1, 0:HEAD_DIM]
        grads = {
            "norm_g": lambda: wide(SB_NORM_G), "w_in": lambda: rin_ref[...], "b_f": lambda: sred_ref[SB_B_F:SB_B_F + 1, 0:FOX_HEADS],
            "pool_scale": lambda: sred_ref[SB_POOL_SCALE:SB_POOL_SCALE + 1, :], "fox_q_g": lambda: head(SB_FOX_Q),
            "fox_k_g": lambda: head(SB_FOX_K), "mem_norm_g": lambda: wide(SB_MEM_NORM_G), "w_mem_kv": lambda: rkv_ref[...],
            "mem_q_g": lambda: head(SB_MEM_Q), "mem_k_g": lambda: head(SB_MEM_K), "w_out": lambda: rout_ref[...],
        }
        for p, name in enumerate(PARAM_ORDER):
            w_ref, m_ref, v_ref = ins[3 * p:3 * p + 3]
            g_out, d_out, m_out, v_out = outs[4 * p:4 * p + 4]
            if name == "w_pool":
                for grp in range(4):
                    g = sred_ref[SB_W_POOL:SB_W_POOL + 64, 64 * grp:64 * (grp + 1)]
                    delta, m_new, v_new = _adamw(w_ref[grp], g, m_ref[grp], v_ref[grp])
                    g_out[grp], d_out[grp], m_out[grp], v_out[grp] = g, delta, m_new, v_new
            elif name == "w_in":
                for j in range(8):
                    rows = pl.ds(j, IN_CHUNK, stride=8)
                    g = rin_ref[0:IN_CHUNK, 128 * j:128 * (j + 1)]
                    delta, m_new, v_new = _adamw(w_ref[rows, :], g, m_ref[rows, :], v_ref[rows, :])
                    g_out[rows, :], d_out[rows, :], m_out[rows, :], v_out[rows, :] = g, delta, m_new, v_new
            else:
                g = grads[name]()
                delta, m_new, v_new = _adamw(w_ref[...], g, m_ref[...], v_ref[...])
                g_out[...], d_out[...], m_out[...], v_out[...] = g, delta, m_new, v_new

    flat = [t for name in PARAM_ORDER for t in params[name]]
    shapes = [params[name][0].shape for name in PARAM_ORDER for _ in range(4)]
    outs = pl.pallas_call(
        body,
        name="adamw_update",
        out_shape=tuple(jax.ShapeDtypeStruct(s, F32) for s in shapes),
        in_specs=[pl.BlockSpec(memory_space=pltpu.VMEM)] * (4 + len(flat)),
        out_specs=(pl.BlockSpec(memory_space=pltpu.VMEM),) * len(shapes),
        compiler_params=pltpu.CompilerParams(vmem_limit_bytes=VMEM_LIMIT),
    )(red_in, red_kv, red_out, sred, *flat)
    return {name: outs[4 * p:4 * p + 4] for p, name in enumerate(PARAM_ORDER)}


def kernel(x, mem, norm_g, w_in, b_f, w_pool, pool_scale, fox_q_g, fox_k_g, mem_norm_g, w_mem_kv, mem_q_g, mem_k_g, w_out, loss_target, m_norm_g, m_w_in, m_b_f, m_w_pool, m_pool_scale, m_fox_q_g, m_fox_k_g, m_mem_norm_g, m_w_mem_kv, m_mem_q_g, m_mem_k_g, m_w_out, v_norm_g, v_w_in, v_b_f, v_w_pool, v_pool_scale, v_fox_q_g, v_fox_k_g, v_mem_norm_g, v_w_mem_kv, v_mem_q_g, v_mem_k_g, v_w_out):
    given = dict(norm_g=(norm_g, m_norm_g, v_norm_g), w_in=(w_in, m_w_in, v_w_in), b_f=(b_f, m_b_f, v_b_f),
                 w_pool=(w_pool, m_w_pool, v_w_pool), pool_scale=(pool_scale, m_pool_scale, v_pool_scale),
                 fox_q_g=(fox_q_g, m_fox_q_g, v_fox_q_g), fox_k_g=(fox_k_g, m_fox_k_g, v_fox_k_g),
                 mem_norm_g=(mem_norm_g, m_mem_norm_g, v_mem_norm_g), w_mem_kv=(w_mem_kv, m_w_mem_kv, v_w_mem_kv),
                 mem_q_g=(mem_q_g, m_mem_q_g, v_mem_q_g), mem_k_g=(mem_k_g, m_mem_k_g, v_mem_k_g), w_out=(w_out, m_w_out, v_w_out))
    params = {name: tuple(t[0] if t.ndim > 2 else t for t in wmv) for name, wmv in given.items()}
    params["w_in"] = tuple(t.T.reshape(IN_CHUNK * 8, 128) for t in params["w_in"])
    x2, mem2, target2 = x[0], mem[0], loss_target[0]

    seg = jnp.asarray(np.kron(np.eye(FOX_HEADS), np.full((HEAD_DIM, HEAD_DIM), 1.0 / HEAD_DIM)), BF16)
    w_my, w_kv16, w_out16, wp_bd, bf_pad, gq8, gk8, gqm4 = _gather_w_in(
        params["w_in"][0], params["w_mem_kv"][0], params["w_out"][0], params["w_pool"][0], b_f, fox_q_g, fox_k_g, mem_q_g)
    proj, h16 = _fwd_proj(x2, norm_g, w_my)
    q_aug, k_aug, v_h, kt_aug, vt_aug = _fox_prep(proj, bf_pad, gq8, gk8, seg)
    olse, lse_rows, w_kv_all, w_out_all = _fox_fwd(q_aug, k_aug, vt_aug, w_kv16, w_out16)
    km, vm = _mem_prep(mem2, mem_norm_g, w_kv_all, mem_k_g)
    mixed = _mix_fwd(proj, olse, km, vm, wp_bd, pool_scale, gqm4, seg)
    d_out, d_mixed, dw_out, loss_blk = _out_loss(mixed, x2, target2, w_out_all)

    d_ua, d_gb, d_qm, d_o, delta_rows, dwp_bd, d_scale, dkm, dvm, d_gqm = _mix_bwd(proj, olse, d_mixed, km, vm, wp_bd, pool_scale,
                                                                                    gqm4, seg)
    dw_kv, d_mem_g, d_gkm = _mem_bwd(mem2, mem_norm_g, w_kv_all, mem_k_g, dkm, dvm)
    dq_aug, dk_aug, dv_h, land_kv, land_out = _fox_bwd(q_aug, k_aug, kt_aug, v_h, d_o, lse_rows, delta_rows, dw_kv, dw_out)
    d_q, d_k, d_v, d_f, d_gq, d_gk, d_bf = _fox_post(proj, bf_pad, gq8, gk8, seg, dq_aug, dk_aug, dv_h)
    pieces = (d_ua, d_q, d_k, d_v, d_gb, d_qm)
    dw_in = _grad_w_in(h16, pieces, d_f)
    grad_x, red_in, red_kv, red_out, sred = _grad_x_exchange(pieces, d_f, w_my, x2, norm_g, d_out, dw_in, land_kv, land_out, d_mem_g,
                                                             d_scale, d_bf, d_gq, d_gk, d_gqm, d_gkm, dwp_bd, loss_blk)
    new = _update(red_in, red_kv, red_out, sred, params)
    result = [sred[SB_LOSS, 0], grad_x[None]]
    for kind in range(4):
        for name in PARAM_ORDER:
            out = new[name][kind]
            if name == "w_in":
                out = out.reshape(IN_CHUNK, D_MODEL).T
            result.append(out[None] if given[name][0].ndim > 2 else out)
    return tuple(result)
```

```python
import functools

import jax
import jax.numpy as jnp
import numpy as np
from jax import lax
from jax.experimental import pallas as pl
from jax.experimental.pallas import tpu as pltpu

F32 = jnp.float32
BF16 = jnp.bfloat16
MESH = pl.DeviceIdType.MESH

N_DEV = 8
D_MODEL = 1024
HEAD_DIM = 64
FOX_HEADS = 8
MEM_HEADS = 4
N_MEM = 256
POOL_WIDTH = 256
EPS = 1e-6
IN_WIDTH = 3080
IN_CHUNK = IN_WIDTH // N_DEV
CHUNK_ROWS = 400
F_OFF = 2048
MY_WIDTH = 3200
MY_F_OFF = 3072
PIECE = 512
TM = 256
TMM = 512
TA = 256
HB = 8
HB_FWD = 8
AUG_ROWS = 72
HALO = 16
VMEM_LIMIT = 56 * 1024 * 1024

ADAM_LR = 0.001
ADAM_B1 = 0.9
ADAM_B2 = 0.999
ADAM_EPS = 1e-08
ADAM_WD = 0.01
ADAM_STEP = 10


def _dot(a, b):
    return jnp.dot(a, b, preferred_element_type=F32)


def _dot_nt(a, b):
    return lax.dot_general(a, b, (((1,), (1,)), ((), ())), preferred_element_type=F32)


def _dot_tn(a, b):
    return lax.dot_general(a, b, (((0,), (0,)), ((), ())), preferred_element_type=F32)


def _sigmoid(g):
    return 0.5 + 0.5 * jnp.tanh(0.5 * g)


def _split3(v):
    hi = v.astype(BF16)
    r1 = v - hi.astype(F32)
    mid = r1.astype(BF16)
    lo = (r1 - mid.astype(F32)).astype(BF16)
    return hi, mid, lo


def _rms(v):
    return lax.rsqrt(jnp.mean(v * v, axis=-1, keepdims=True) + EPS)


def _rms_bwd(a, r, g, dy):
    da = dy * g
    return r * (da - a * jnp.mean(da * a, axis=-1, keepdims=True)), dy * a


def _head_mean(z, seg):
    hi = z.astype(BF16)
    lo = (z - hi.astype(F32)).astype(BF16)
    if z.shape[1] == 256:
        return _dot(hi, seg) + _dot(lo, seg)
    half = seg[0:256, 0:256]
    return jnp.concatenate([_dot(hi[:, 0:256], half) + _dot(lo[:, 0:256], half),
                            _dot(hi[:, 256:512], half) + _dot(lo[:, 256:512], half)], axis=1)


def _head_rms(v, seg):
    return lax.rsqrt(_head_mean(v * v, seg) + EPS)


def _head_rms_bwd(a, r, g, dy, seg):
    da = dy * g
    return r * (da - a * _head_mean(da * a, seg)), dy * a


def _fold_heads(row, n_heads):
    out = row[:, 0:HEAD_DIM]
    for h in range(1, n_heads):
        out = out + row[:, HEAD_DIM * h:HEAD_DIM * (h + 1)]
    return out


def _params(sem, limit=VMEM_LIMIT):
    return pltpu.CompilerParams(dimension_semantics=sem, vmem_limit_bytes=limit)


def _full(shape):
    return pl.BlockSpec(shape, lambda *_: (0,) * len(shape))


def _chunk_segments(d):
    runs = []
    for lo, hi, shift in ((0, F_OFF, 0), (F_OFF, F_OFF + FOX_HEADS, MY_F_OFF - F_OFF), (F_OFF + FOX_HEADS, IN_WIDTH, -FOX_HEADS)):
        a, b = max(lo, IN_CHUNK * d), min(hi, IN_CHUNK * (d + 1))
        if a < b:
            runs.append((a + shift, b - a))
    return runs


def _my_place():
    x, y, c = lax.axis_index("x"), lax.axis_index("y"), lax.axis_index("c")
    return x, y, c, 4 * x + 2 * y + c


def _shard_rows(dev):
    return pl.ds(pl.multiple_of(128 * dev, 128), 128)


def _gather_w_in(w_in, w_kv, w_out, w_pool, b_f, gq, gk, gqm):
    def body(win_ref, wkv_ref, wout_ref, wp_ref, bf_ref, gq_ref, gk_ref, gqm_ref, wmy_ref, kv16_ref, out16_ref, wpbd_ref, bfpad_ref,
             gq8_ref, gk8_ref, gqm4_ref, in_all, pay_in, win_rows, send_sems, recv_sems, load_sem):
        x, y, c, me = _my_place()
        here, sibling = (x, y, c), (x, y, 1 - c)
        x_chip, y_chip, far_chip = (1 - x, y), (x, 1 - y), (1 - x, 1 - y)
        relay_from = (x ^ (1 - c), y ^ c)
        relay_to = (x ^ c, y ^ (1 - c))

        load = pltpu.make_async_copy(win_ref, win_rows, load_sem)
        load.start()
        load.wait()
        pay_in[...] = jnp.zeros((CHUNK_ROWS, D_MODEL), BF16)
        for j in range(8):
            pay_in[0:IN_CHUNK, 128 * j:128 * (j + 1)] = win_rows[pl.ds(j, IN_CHUNK, stride=8), :].astype(BF16)

        def copy(k, block, to, own=False):
            px, py, pc = block
            dst = in_all.at[4 * px + 2 * py + pc]
            return pltpu.make_async_remote_copy(src_ref=pay_in if own else dst, dst_ref=dst, send_sem=send_sems.at[k],
                                                recv_sem=recv_sems.at[k], device_id=to, device_id_type=MESH)

        first = [copy(0, here, sibling, own=True), copy(1, here, (*x_chip, c), own=True), copy(2, here, (*y_chip, c), own=True)]
        for cp in first:
            cp.start()
        in_all[me] = pay_in[...]
        kv16_ref[...] = wkv_ref[...].astype(BF16)
        out16_ref[...] = wout_ref[...].astype(BF16)
        wpbd_ref[...] = jnp.zeros((POOL_WIDTH, POOL_WIDTH), BF16)
        for grp in range(4):
            sl = slice(64 * grp, 64 * (grp + 1))
            wpbd_ref[sl, sl] = wp_ref[grp].astype(BF16)
        bfpad_ref[...] = jnp.zeros((1, 128), F32)
        bfpad_ref[:, 0:FOX_HEADS] = bf_ref[...]
        gq8_ref[...] = jnp.concatenate([gq_ref[...]] * FOX_HEADS, axis=1)
        gk8_ref[...] = jnp.concatenate([gk_ref[...]] * FOX_HEADS, axis=1)
        gqm4_ref[...] = jnp.concatenate([gqm_ref[...]] * MEM_HEADS, axis=1)
        copy(1 + c, (*relay_from, c), here).wait_recv()
        passed = [copy(3, (*relay_from, c), (*relay_to, c)), copy(4, (*relay_from, c), sibling)]
        for cp in passed:
            cp.start()
        copy(2 - c, (*relay_to, c), here).wait_recv()
        passed.append(copy(5, (*relay_to, c), sibling))
        passed[-1].start()
        copy(3, (*far_chip, c), here).wait_recv()
        passed.append(copy(6, (*far_chip, c), sibling))
        passed[-1].start()
        copy(0, sibling, here).wait_recv()
        for k, chip in ((4, relay_to), (5, relay_from), (6, far_chip)):
            copy(k, (*chip, 1 - c), here).wait_recv()
        for cp in first + passed:
            cp.wait_send()

        row_pad = jnp.zeros((512 - CHUNK_ROWS, 256), F32)
        for r0 in range(0, D_MODEL, 256):
            ch = [jnp.concatenate([in_all[d, :, r0:r0 + 256].astype(F32), row_pad], axis=0).T[:, 0:IN_CHUNK] for d in range(N_DEV)]
            lo = F_OFF - 5 * IN_CHUNK
            full = jnp.concatenate(ch[:5] + [ch[5][:, :lo], ch[5][:, lo + FOX_HEADS:], ch[6], ch[7], ch[5][:, lo:lo + FOX_HEADS],
                                             jnp.zeros((256, MY_WIDTH - IN_WIDTH), F32)], axis=1)
            wmy_ref[r0:r0 + 256, :] = full.astype(BF16)

    return pl.pallas_call(
        body,
        name="gather_w_in",
        out_shape=(jax.ShapeDtypeStruct((D_MODEL, MY_WIDTH), BF16), jax.ShapeDtypeStruct((128, 512), BF16),
                   jax.ShapeDtypeStruct((128, D_MODEL), BF16), jax.ShapeDtypeStruct((POOL_WIDTH, POOL_WIDTH), BF16),
                   jax.ShapeDtypeStruct((1, 128), F32), jax.ShapeDtypeStruct((1, PIECE), F32), jax.ShapeDtypeStruct((1, PIECE), F32),
                   jax.ShapeDtypeStruct((1, 256), F32)),
        in_specs=[pl.BlockSpec(memory_space=pl.ANY)] + [pl.BlockSpec(memory_space=pltpu.VMEM)] * 7,
        out_specs=(pl.BlockSpec(memory_space=pltpu.VMEM),) * 8,
        scratch_shapes=[
            pltpu.VMEM((N_DEV, CHUNK_ROWS, D_MODEL), BF16),
            pltpu.VMEM((CHUNK_ROWS, D_MODEL), BF16),
            pltpu.VMEM((IN_CHUNK * 8, 128), F32),
            pltpu.SemaphoreType.DMA((7,)),
            pltpu.SemaphoreType.DMA((7,)),
            pltpu.SemaphoreType.DMA,
        ],
        compiler_params=pltpu.CompilerParams(vmem_limit_bytes=VMEM_LIMIT),
    )(w_in, w_kv, w_out, w_pool, b_f, gq, gk, gqm)


def _row_block_exchange(kv_ref, out_ref, kv_dst, out_dst, send_sems, recv_sems, local_sems, gather):
    x, y, c, me = _my_place()
    remote = []
    for k in range(1, N_DEV):
        px, py, pc = x ^ (k >> 2), y ^ ((k >> 1) & 1), c ^ (k & 1)
        peer = 4 * px + 2 * py + pc
        for fam, (src, dst) in enumerate(((kv_ref, kv_dst), (out_ref, out_dst))):
            remote.append(pltpu.make_async_remote_copy(
                src_ref=src if gather else src.at[_shard_rows(peer)], dst_ref=dst.at[_shard_rows(me)] if gather else dst.at[me],
                send_sem=send_sems.at[7 * fam + k - 1], recv_sem=recv_sems.at[7 * fam + k - 1],
                device_id=(px, py, pc), device_id_type=MESH))
    local = [pltpu.make_async_copy(src if gather else src.at[_shard_rows(me)], dst.at[_shard_rows(me)] if gather else dst.at[me],
                                   local_sems.at[fam]) for fam, (src, dst) in enumerate(((kv_ref, kv_dst), (out_ref, out_dst)))]
    return remote, local


SB_ROWS, SB_W = 80, 256
SB_NORM_G, SB_MEM_NORM_G, SB_POOL_SCALE, SB_B_F, SB_FOX_Q, SB_FOX_K, SB_MEM_Q, SB_MEM_K, SB_LOSS, SB_W_POOL = 0, 4, 8, 9, 10, 11, 12, 13, 14, 16


def _fwd_proj(x, norm_g, w_my):
    s_len = x.shape[0]

    def body(x_ref, g_ref, w_ref, proj_ref, h_ref):
        xv = x_ref[...]
        h = (xv * _rms(xv) * g_ref[...]).astype(BF16)
        h_ref[...] = h
        proj_ref[...] = _dot(h, w_ref[...])

    return pl.pallas_call(
        body,
        name="fwd_proj",
        grid=(s_len // TMM,),
        in_specs=[pl.BlockSpec((TMM, D_MODEL), lambda i: (i, 0)), _full((1, D_MODEL)), _full((D_MODEL, MY_WIDTH))],
        out_specs=[pl.BlockSpec((TMM, MY_WIDTH), lambda i: (i, 0)), pl.BlockSpec((TMM, D_MODEL), lambda i: (i, 0))],
        out_shape=[jax.ShapeDtypeStruct((s_len, MY_WIDTH), F32), jax.ShapeDtypeStruct((s_len, D_MODEL), BF16)],
        compiler_params=_params(("parallel",)),
    )(x, norm_g, w_my)


def _log_sigmoid(z):
    return jnp.minimum(z, 0.0) - jnp.log(1.0 + jnp.exp(-jnp.abs(z)))


def _forget_lane_maps():
    to_q = np.zeros((128, FOX_HEADS * 128), np.float32)
    to_k = np.zeros((128, FOX_HEADS * 128), np.float32)
    for h in range(FOX_HEADS):
        for term in range(3):
            to_q[8 * term + h, 128 * h + 64 + term] = 1.0
            to_q[24, 128 * h + 67 + term] = 1.0
            to_k[24, 128 * h + 64 + term] = 1.0
            to_k[8 * term + h, 128 * h + 67 + term] = -1.0
    return jnp.asarray(to_q, BF16), jnp.asarray(to_k, BF16)


def _fox_prep(proj, bf_pad, gq, gk, seg):
    s_len = proj.shape[0]
    to_q, to_k = _forget_lane_maps()

    def body(q_ref, k_ref, v_ref, f_ref, bf_ref, gq_ref, gk_ref, seg_ref, eq_ref, ek_ref,
             qa_ref, ka_ref, vh_ref, kt_ref, vt_ref, carry_ref):
        @pl.when(pl.program_id(0) == 0)
        def _():
            carry_ref[...] = jnp.zeros((1, 128), F32)

        lane = lax.broadcasted_iota(jnp.int32, (TM, 128), 1)
        logf = jnp.where(lane < FOX_HEADS, _log_sigmoid(f_ref[...] + bf_ref[...]), 0.0)
        row = lax.broadcasted_iota(jnp.int32, (TM, TM), 0)
        col = lax.broadcasted_iota(jnp.int32, (TM, TM), 1)
        tri = jnp.where(col <= row, 1.0, 0.0).astype(BF16)
        hi, mid, lo = _split3(logf)
        cum = _dot(tri, hi) + _dot(tri, mid) + _dot(tri, lo) + carry_ref[...]
        carry_ref[...] = cum[TM - 1:TM, :]
        f_hi, f_mid, f_lo = [t.astype(F32) for t in _split3(cum)]
        packed = (f_hi + pltpu.roll(f_mid, 8, 1) + pltpu.roll(f_lo, 16, 1) + jnp.where(lane == 24, 1.0, 0.0)).astype(BF16)
        extra_q = _dot(packed, eq_ref[...])
        extra_k = _dot(packed, ek_ref[...])
        one_at_64 = jnp.where(lane == 64, 1.0, 0.0)
        zeros = jnp.zeros((TM, HEAD_DIM), F32)
        q_all = q_ref[...]
        k_all = k_ref[...]
        qn_all = q_all * _head_rms(q_all, seg_ref[...]) * gq_ref[...] * 0.125
        kn_all = k_all * _head_rms(k_all, seg_ref[...]) * gk_ref[...]
        for h in range(FOX_HEADS):
            sl = slice(HEAD_DIM * h, HEAD_DIM * (h + 1))
            tile = slice(128 * h, 128 * (h + 1))
            qa = jnp.where(lane < 64, jnp.concatenate([qn_all[:, sl], zeros], axis=1), extra_q[:, tile])
            ka = jnp.where(lane < 64, jnp.concatenate([kn_all[:, sl], zeros], axis=1), extra_k[:, tile])
            vh = v_ref[:, sl]
            v_aug = jnp.where(lane < 64, jnp.concatenate([vh, zeros], axis=1), one_at_64)
            qa_ref[h] = qa.astype(BF16)
            ka_ref[h] = ka.astype(BF16)
            vh_ref[h] = vh.astype(BF16)
            kt_ref[h, 0] = ka.T.astype(BF16)
            vt_ref[h, 0] = v_aug.T.astype(BF16)

    col_blk = lambda j: pl.BlockSpec((TM, PIECE), lambda i: (i, j))
    head_blk = lambda w: pl.BlockSpec((FOX_HEADS, TM, w), lambda i: (0, i, 0))
    tile_blk = pl.BlockSpec((FOX_HEADS, 1, 128, TM), lambda i: (0, i, 0, 0))
    tiled = jax.ShapeDtypeStruct((FOX_HEADS, s_len // TM, 128, TM), BF16)
    return pl.pallas_call(
        body,
        name="fox_prep",
        grid=(s_len // TM,),
        in_specs=[col_blk(1), col_blk(2), col_blk(3), pl.BlockSpec((TM, 128), lambda i: (i, MY_F_OFF // 128)),
                  _full((1, 128)), _full((1, PIECE)), _full((1, PIECE)), _full((PIECE, PIECE)),
                  _full((128, FOX_HEADS * 128)), _full((128, FOX_HEADS * 128))],
        out_specs=[head_blk(128), head_blk(128), head_blk(HEAD_DIM), tile_blk, tile_blk],
        out_shape=[jax.ShapeDtypeStruct((FOX_HEADS, s_len, 128), BF16), jax.ShapeDtypeStruct((FOX_HEADS, s_len, 128), BF16),
                   jax.ShapeDtypeStruct((FOX_HEADS, s_len, HEAD_DIM), BF16), tiled, tiled],
        scratch_shapes=[pltpu.VMEM((1, 128), F32)],
        compiler_params=_params(("arbitrary",)),
    )(proj, proj, proj, proj, bf_pad, gq, gk, seg, to_q, to_k)


def _mem_prep(mem, g, w_kv, gk):
    def body(mem_ref, g_ref, w_ref, gk_ref, km_ref, vm_ref):
        m = mem_ref[...]
        kv = _dot((m * _rms(m) * g_ref[...]).astype(BF16), w_ref[...])
        for h in range(MEM_HEADS):
            sl = slice(HEAD_DIM * h, HEAD_DIM * (h + 1))
            kh = kv[:, sl]
            km_ref[:, sl] = (kh * _rms(kh) * gk_ref[...]).astype(BF16)
        vm_ref[...] = kv[:, 256:512].astype(BF16)

    return pl.pallas_call(
        body,
        name="mem_prep",
        out_shape=(jax.ShapeDtypeStruct((N_MEM, 256), BF16),) * 2,
        in_specs=[pl.BlockSpec(memory_space=pltpu.VMEM)] * 4,
        out_specs=(pl.BlockSpec(memory_space=pltpu.VMEM),) * 2,
        compiler_params=pltpu.CompilerParams(vmem_limit_bytes=VMEM_LIMIT),
    )(mem, g, w_kv, gk)


def _causal_mask():
    row = lax.broadcasted_iota(jnp.int32, (TA, TA), 0)
    col = lax.broadcasted_iota(jnp.int32, (TA, TA), 1)
    return col <= row


def _causal_mask_t():
    row = lax.broadcasted_iota(jnp.int32, (TA, TA), 0)
    col = lax.broadcasted_iota(jnp.int32, (TA, TA), 1)
    return row <= col


def _fox_fwd(q_aug, k_aug, vt_aug, w_kv16, w_out16):
    s_len = q_aug.shape[1]
    n_tiles = s_len // TA
    hb = HB_FWD
    n_groups = FOX_HEADS // hb

    def body(q_ref, k_ref, vt_ref, kv16_ref, out16_ref, o_ref, st_ref, kv_all, out_all, send_sems, recv_sems, local_sems):
        qi = pl.program_id(1)
        remote, local = _row_block_exchange(kv16_ref, out16_ref, kv_all, out_all, send_sems, recv_sems, local_sems, gather=True)

        @pl.when((pl.program_id(0) == 0) & (qi == 0))
        def _():
            for cp in remote + local:
                cp.start()

        qs = [q_ref[h] for h in range(hb)]

        def step(t0, carry, masked, width):
            rows = pl.ds(pl.multiple_of(t0 * TA, TA), width * TA)
            scores = [_dot_nt(k_ref[h, rows, :], qs[h]) for h in range(hb)]
            soft = []
            for h in range(hb):
                m, _ = carry[h]
                s = jnp.where(_causal_mask_t(), scores[h], -1e30) if masked else scores[h]
                m_new = jnp.maximum(m, jnp.max(s, axis=0, keepdims=True))
                soft.append((m_new, jnp.exp(m - m_new), jnp.exp(s - m_new).astype(BF16)))
            out = []
            for h, (m_new, alpha, p) in enumerate(soft):
                acc = alpha * carry[h][1]
                for t in range(width):
                    acc = acc + _dot(vt_ref[h, t0 + t, 0:AUG_ROWS, :], p[t * TA:(t + 1) * TA])
                out.append((m_new, acc))
            return tuple(out)

        init = tuple((jnp.full((1, TA), -1e30, F32), jnp.zeros((AUG_ROWS, TA), F32)) for _ in range(hb))
        carry = lax.fori_loop(0, qi // 2, lambda j, cr: step(2 * j, cr, False, 2), init)
        carry = lax.fori_loop(2 * (qi // 2), qi, lambda j, cr: step(j, cr, False, 1), carry)
        carry = step(qi, carry, True, 1)
        for h in range(hb):
            m, acc = carry[h]
            l = acc[HEAD_DIM:HEAD_DIM + 1, :]
            lse = m + jnp.log(l)
            o_ref[h] = jnp.concatenate([acc[0:HEAD_DIM] / l, jnp.broadcast_to(lse, (HEAD_DIM, TA))], axis=0).T
            st_ref[h, 0] = jnp.broadcast_to(lse, (8, TA))

        @pl.when((pl.program_id(0) == n_groups - 1) & (qi == n_tiles - 1))
        def _():
            for cp in remote:
                cp.wait_recv()
            for cp in remote:
                cp.wait_send()
            for cp in local:
                cp.wait()

    hbm = pl.BlockSpec(memory_space=pl.ANY)
    return pl.pallas_call(
        body,
        name="fox_fwd",
        grid=(n_groups, n_tiles),
        in_specs=[pl.BlockSpec((hb, TA, 128), lambda g, i: (g, i, 0)), pl.BlockSpec((hb, s_len, 128), lambda g, i: (g, 0, 0)),
                  pl.BlockSpec((hb, n_tiles, 128, TA), lambda g, i: (g, 0, 0, 0)), hbm, hbm],
        out_specs=[pl.BlockSpec((hb, TA, 128), lambda g, i: (g, i, 0)), pl.BlockSpec((hb, 1, 8, TA), lambda g, i: (g, i, 0, 0)),
                   hbm, hbm],
        out_shape=[jax.ShapeDtypeStruct((FOX_HEADS, s_len, 128), F32), jax.ShapeDtypeStruct((FOX_HEADS, n_tiles, 8, TA), F32),
                   jax.ShapeDtypeStruct((D_MODEL, 512), BF16), jax.ShapeDtypeStruct((D_MODEL, D_MODEL), BF16)],
        scratch_shapes=[pltpu.SemaphoreType.DMA((14,)), pltpu.SemaphoreType.DMA((14,)), pltpu.SemaphoreType.DMA((2,))],
        compiler_params=_params(("arbitrary", "arbitrary")),
    )(q_aug, k_aug, vt_aug, w_kv16, w_out16)


def _lane_group(lane, a, b, c, d):
    return jnp.where(lane < 64, a, jnp.where(lane < 128, b, jnp.where(lane < 192, c, d)))


def _pool_count(row0):
    lane = lax.broadcasted_iota(jnp.int32, (TM, POOL_WIDTH), 1)
    t1 = row0 + lax.broadcasted_iota(jnp.int32, (TM, POOL_WIDTH), 0) + 1
    return 1.0 / jnp.minimum(t1, _lane_group(lane, 2, 4, 8, 16)).astype(F32), lane


def _pool_delta(u, halo, cnt, lane):
    xe = jnp.concatenate([halo, u], axis=0)
    s2 = xe + pltpu.roll(xe, 1, 0)
    s4 = s2 + pltpu.roll(s2, 2, 0)
    s8 = s4 + pltpu.roll(s4, 4, 0)
    s16 = s8 + pltpu.roll(s8, 8, 0)
    win = _lane_group(lane, s2[HALO:], s4[HALO:], s8[HALO:], s16[HALO:])
    return win * cnt - u


def _pool_delta_bwd(dd, dd_next, cnt, cnt_next, lane):
    ee = jnp.concatenate([dd * cnt, dd_next * cnt_next], axis=0)
    n = TM + HALO
    r2 = ee + pltpu.roll(ee, n - 1, 0)
    r4 = r2 + pltpu.roll(r2, n - 2, 0)
    r8 = r4 + pltpu.roll(r4, n - 4, 0)
    r16 = r8 + pltpu.roll(r8, n - 8, 0)
    return _lane_group(lane, r2[:TM], r4[:TM], r8[:TM], r16[:TM]) - dd


def _mem_attn(qm, gq, seg, km_ref, vm_ref):
    r = _head_rms(qm, seg)
    a = qm * r
    q16 = (a * gq * 0.125).astype(BF16)
    heads = []
    for h in range(MEM_HEADS):
        sl = slice(HEAD_DIM * h, HEAD_DIM * (h + 1))
        s = _dot_nt(q16[:, sl], km_ref[:, sl])
        e = jnp.exp(s - jnp.max(s, axis=-1, keepdims=True))
        p = e * (1.0 / jnp.sum(e, axis=-1, keepdims=True))
        heads.append((p, _dot(p.astype(BF16), vm_ref[:, sl])))
    return a, r, q16, heads


def _row_specs(s_len):
    n_halo = s_len // HALO
    piece = lambda j: pl.BlockSpec((TM, PIECE), lambda i: (i, j))
    before = lambda j: pl.BlockSpec((HALO, 256), lambda i: (jnp.maximum(i * (TM // HALO) - 1, 0), j))
    after = lambda j: pl.BlockSpec((HALO, 256), lambda i: (jnp.minimum((i + 1) * (TM // HALO), n_halo - 1), j))
    return piece, before, after


def _mix_fwd(proj, olse, km, vm, wp_bd, pool_scale, gq_m, seg):
    s_len = proj.shape[0]

    def body(ua_ref, halo_ref, gb_ref, qm_ref, ol_ref, km_ref, vm_ref, wp_ref, sc_ref, gq_ref, seg_ref, out_ref):
        i = pl.program_id(0)
        u = ua_ref[:, 0:256]
        g_a = ua_ref[:, 256:512]
        halo = jnp.where(i > 0, halo_ref[...], 0.0)
        cnt, lane = _pool_count(i * TM)
        d = _pool_delta(u, halo, cnt, lane).astype(BF16)
        y_a = _dot(d, wp_ref[...]) * sc_ref[...]
        out_ref[:, 0:256] = (y_a * (g_a * _sigmoid(g_a))).astype(BF16)
        g_b = gb_ref[...]
        y_b = jnp.concatenate([ol_ref[h][:, 0:HEAD_DIM] for h in range(FOX_HEADS)], axis=1)
        out_ref[:, 256:768] = (y_b * (g_b * _sigmoid(g_b))).astype(BF16)
        _, _, _, heads = _mem_attn(qm_ref[:, 0:256], gq_ref[...], seg_ref[0:256, 0:256], km_ref, vm_ref)
        g_m = qm_ref[:, 256:512]
        y_m = jnp.concatenate([y for _, y in heads], axis=1)
        out_ref[:, 768:1024] = (y_m * (g_m * _sigmoid(g_m))).astype(BF16)

    piece, before, _ = _row_specs(s_len)
    return pl.pallas_call(
        body,
        name="mix_fwd",
        grid=(s_len // TM,),
        in_specs=[piece(0), before(0), piece(4), piece(5), pl.BlockSpec((FOX_HEADS, TM, 128), lambda i: (0, i, 0)),
                  _full((N_MEM, 256)), _full((N_MEM, 256)), _full((256, 256)), _full((1, 256)), _full((1, 256)),
                  _full((PIECE, PIECE))],
        out_specs=pl.BlockSpec((TM, D_MODEL), lambda i: (i, 0)),
        out_shape=jax.ShapeDtypeStruct((s_len, D_MODEL), BF16),
        compiler_params=_params(("parallel",)),
    )(proj, proj, proj, proj, olse, km, vm, wp_bd, pool_scale, gq_m, seg)


def _out_loss(mixed, x, target, w_out):
    s_len = x.shape[0]

    def body(mix_ref, x_ref, t_ref, w_ref, dout_ref, dmix_ref, dw16_ref, loss_ref, dw_ref):
        @pl.when(pl.program_id(0) == 0)
        def _():
            dw_ref[...] = jnp.zeros((D_MODEL, D_MODEL), F32)
            loss_ref[...] = jnp.zeros((8, 128), F32)

        mixed16 = mix_ref[...]
        err = x_ref[...] + _dot(mixed16, w_ref[...]) - t_ref[...]
        loss_ref[...] += 0.5 * jnp.sum(jnp.mean(err * err, axis=-1, keepdims=True))
        d_out = err / float(D_MODEL)
        dout_ref[...] = d_out
        d16 = d_out.astype(BF16)
        dmix_ref[...] = _dot_nt(d16, w_ref[...])
        dw_ref[...] += _dot_tn(mixed16, d16)

        @pl.when(pl.program_id(0) == pl.num_programs(0) - 1)
        def _():
            dw16_ref[...] = dw_ref[...].astype(BF16)

    row = pl.BlockSpec((TMM, D_MODEL), lambda i: (i, 0))
    return pl.pallas_call(
        body,
        name="out_loss",
        grid=(s_len // TMM,),
        in_specs=[row, row, row, _full((D_MODEL, D_MODEL))],
        out_specs=[row, row, _full((D_MODEL, D_MODEL)), _full((8, 128))],
        out_shape=[jax.ShapeDtypeStruct((s_len, D_MODEL), F32), jax.ShapeDtypeStruct((s_len, D_MODEL), F32),
                   jax.ShapeDtypeStruct((D_MODEL, D_MODEL), BF16), jax.ShapeDtypeStruct((8, 128), F32)],
        scratch_shapes=[pltpu.VMEM((D_MODEL, D_MODEL), F32)],
        compiler_params=_params(("arbitrary",)),
    )(mixed, x, target, w_out)


def _silu_pair(g):
    s = _sigmoid(g)
    return g * s, s * (1.0 + g * (1.0 - s))


def _mix_bwd(proj, olse, d_mixed, km, vm, wp_bd, pool_scale, gq_m, seg):
    s_len = proj.shape[0]
    n_tiles = s_len // TM

    def body(ua_ref, halo_ref, ga_next_ref, gb_ref, qm_ref, ol_ref, dm_ref, dm_next_ref, km_ref, vm_ref, wp_ref, sc_ref, gq_ref,
             seg_ref, dua_ref, dgb_ref, dqm_ref, do_ref, dl_ref, dwp_ref, dsc_ref, dkm_ref, dvm_ref, dgq_ref):
        i = pl.program_id(0)

        @pl.when(i == 0)
        def _():
            dwp_ref[...] = jnp.zeros((256, 256), F32)
            dsc_ref[...] = jnp.zeros((1, 256), F32)
            dkm_ref[...] = jnp.zeros((N_MEM, 256), F32)
            dvm_ref[...] = jnp.zeros((N_MEM, 256), F32)
            dgq_ref[...] = jnp.zeros((1, HEAD_DIM), F32)

        u = ua_ref[:, 0:256]
        g_a = ua_ref[:, 256:512]
        halo = jnp.where(i > 0, halo_ref[...], 0.0)
        cnt, lane = _pool_count(i * TM)
        d16 = _pool_delta(u, halo, cnt, lane).astype(BF16)
        t = _dot(d16, wp_ref[...])
        y_a = t * sc_ref[...]
        silu_a, dsilu_a = _silu_pair(g_a)
        dm_a = dm_ref[:, 0:256]
        dy_a = dm_a * silu_a
        dsc_ref[...] += jnp.sum(dy_a * t, axis=0, keepdims=True)
        dt16 = (dy_a * sc_ref[...]).astype(BF16)
        dwp_ref[...] += _dot_tn(d16, dt16)
        dd = _dot_nt(dt16, wp_ref[...])
        g_next = ga_next_ref[...]
        dt_next = (dm_next_ref[...] * (g_next * _sigmoid(g_next)) * sc_ref[...]).astype(BF16)
        dd_next = jnp.where(i < n_tiles - 1, _dot_nt(dt_next, wp_ref[...]), 0.0)
        cnt_next = _pool_count((i + 1) * TM)[0][0:HALO]
        dua_ref[:, 0:256] = _pool_delta_bwd(dd, dd_next, cnt, cnt_next, lane).astype(BF16)
        dua_ref[:, 256:512] = (dm_a * y_a * dsilu_a).astype(BF16)

        silu_b, dsilu_b = _silu_pair(gb_ref[...])
        dm_b = dm_ref[:, 256:768]
        o_all = jnp.concatenate([ol_ref[h][:, 0:HEAD_DIM] for h in range(FOX_HEADS)], axis=1)
        d_o = dm_b * silu_b
        dgb_ref[...] = (dm_b * o_all * dsilu_b).astype(BF16)
        for h in range(FOX_HEADS):
            do_ref[h] = d_o[:, HEAD_DIM * h:HEAD_DIM * (h + 1)].astype(BF16)
        prod = d_o * o_all
        hi = prod.astype(BF16)
        lo = (prod - hi.astype(F32)).astype(BF16)
        head_of_lane = lax.broadcasted_iota(jnp.int32, (FOX_HEADS, PIECE), 1) // HEAD_DIM
        pick = jnp.where(head_of_lane == lax.broadcasted_iota(jnp.int32, (FOX_HEADS, PIECE), 0), 1.0, 0.0).astype(BF16)
        delta = _dot_nt(pick, hi) + _dot_nt(pick, lo)
        for h in range(FOX_HEADS):
            dl_ref[h, 0] = jnp.broadcast_to(delta[h:h + 1, :], (8, TM))

        seg = seg_ref[0:256, 0:256]
        a, r, q16, heads = _mem_attn(qm_ref[:, 0:256], gq_ref[...], seg, km_ref, vm_ref)
        silu_m, dsilu_m = _silu_pair(qm_ref[:, 256:512])
        dm_m = dm_ref[:, 768:1024]
        y_m = jnp.concatenate([y for _, y in heads], axis=1)
        dqm_ref[:, 256:512] = (dm_m * y_m * dsilu_m).astype(BF16)
        dy16_all = (dm_m * silu_m).astype(BF16)
        d_scores = []
        for h in range(MEM_HEADS):
            sl = slice(HEAD_DIM * h, HEAD_DIM * (h + 1))
            p = heads[h][0]
            dy16 = dy16_all[:, sl]
            dp = _dot_nt(dy16, vm_ref[:, sl])
            dvm_ref[:, sl] += _dot_tn(p.astype(BF16), dy16)
            ds16 = (p * (dp - jnp.sum(dp * p, axis=-1, keepdims=True))).astype(BF16)
            dkm_ref[:, sl] += _dot_tn(ds16, q16[:, sl])
            d_scores.append(_dot(ds16, km_ref[:, sl]))
        dq, dg_rows = _head_rms_bwd(a, r, gq_ref[...], jnp.concatenate(d_scores, axis=1) * 0.125, seg)
        dqm_ref[:, 0:256] = dq.astype(BF16)
        dgq_ref[...] += _fold_heads(jnp.sum(dg_rows, axis=0, keepdims=True), MEM_HEADS)

    piece, before, after = _row_specs(s_len)
    n_halo = s_len // HALO
    row = pl.BlockSpec((TM, D_MODEL), lambda i: (i, 0))
    dm_after = pl.BlockSpec((HALO, 256), lambda i: (jnp.minimum((i + 1) * (TM // HALO), n_halo - 1), 0))
    out_piece = pl.BlockSpec((TM, PIECE), lambda i: (i, 0))
    return pl.pallas_call(
        body,
        name="mix_bwd",
        grid=(n_tiles,),
        in_specs=[piece(0), before(0), after(1), piece(4), piece(5), pl.BlockSpec((FOX_HEADS, TM, 128), lambda i: (0, i, 0)),
                  row, dm_after, _full((N_MEM, 256)), _full((N_MEM, 256)), _full((256, 256)), _full((1, 256)), _full((1, 256)),
                  _full((PIECE, PIECE))],
        out_specs=[out_piece, out_piece, out_piece, pl.BlockSpec((FOX_HEADS, TM, HEAD_DIM), lambda i: (0, i, 0)),
                   pl.BlockSpec((FOX_HEADS, 1, 8, TM), lambda i: (0, i, 0, 0)),
                   _full((256, 256)), _full((1, 256)), _full((N_MEM, 256)), _full((N_MEM, 256)), _full((1, HEAD_DIM))],
        out_shape=[jax.ShapeDtypeStruct((s_len, PIECE), BF16)] * 3 + [
            jax.ShapeDtypeStruct((FOX_HEADS, s_len, HEAD_DIM), BF16),
            jax.ShapeDtypeStruct((FOX_HEADS, n_tiles, 8, TM), F32),
            jax.ShapeDtypeStruct((256, 256), F32), jax.ShapeDtypeStruct((1, 256), F32),
            jax.ShapeDtypeStruct((N_MEM, 256), F32), jax.ShapeDtypeStruct((N_MEM, 256), F32),
            jax.ShapeDtypeStruct((1, HEAD_DIM), F32)],
        compiler_params=_params(("arbitrary",)),
    )(proj, proj, proj, proj, proj, olse, d_mixed, d_mixed, km, vm, wp_bd, pool_scale, gq_m, seg)


def _fox_bwd(q_aug, k_aug, kt_aug, v_h, d_o, lse_rows, delta_rows, dw_kv16, dw_out16):
    s_len = q_aug.shape[1]
    n_tiles = s_len // TA
    n_groups = FOX_HEADS // HB

    def body(q_ref, k_ref, kt_ref, v_ref, do_ref, st_ref, dl_ref, dkv16_ref, dout16_ref, dq_ref, dk_ref, dv_ref, land_kv, land_out,
             dqt_ref, send_sems, recv_sems, local_sems):
        j = pl.program_id(1)
        remote, local = _row_block_exchange(dkv16_ref, dout16_ref, land_kv, land_out, send_sems, recv_sems, local_sems, gather=False)

        @pl.when((pl.program_id(0) == 0) & (j == 0))
        def _():
            for cp in remote + local:
                cp.start()

        @pl.when(j == 0)
        def _():
            dqt_ref[...] = jnp.zeros((HB, n_tiles, AUG_ROWS, TA), F32)

        ks = [k_ref[h] for h in range(HB)]
        kts = [kt_ref[h, 0, 0:AUG_ROWS, :] for h in range(HB)]
        vs = [v_ref[h] for h in range(HB)]

        def pair(i0, acc, masked, width):
            rows = pl.ds(pl.multiple_of(i0 * TA, TA), width * TA)
            qs = [q_ref[h, rows, :] for h in range(HB)]
            d_os = [do_ref[h, rows, :] for h in range(HB)]
            row_stat = lambda ref, h: jnp.concatenate([ref[h, i0 + t, 0:1, :] for t in range(width)], axis=1)
            scores = [(_dot_nt(ks[h], qs[h]), _dot_nt(vs[h], d_os[h])) for h in range(HB)]
            probs = []
            for h in range(HB):
                s, dp = scores[h]
                if masked:
                    s = jnp.where(_causal_mask_t(), s, -1e30)
                p = jnp.exp(s - row_stat(st_ref, h))
                probs.append((p.astype(BF16), (p * (dp - row_stat(dl_ref, h))).astype(BF16)))
            out = []
            for h in range(HB):
                p16, ds16 = probs[h]
                dqt = _dot(kts[h], ds16)
                for t in range(width):
                    dqt_ref[h, i0 + t] += dqt[:, t * TA:(t + 1) * TA]
                out.append((acc[h][0] + _dot(ds16, qs[h]), acc[h][1] + _dot(p16, d_os[h])))
            return tuple(out)

        zero = tuple((jnp.zeros((TA, 128), F32), jnp.zeros((TA, HEAD_DIM), F32)) for _ in range(HB))
        acc = pair(j, zero, True, 1)
        odd = (n_tiles - 1 - j) % 2
        acc = lax.fori_loop(j + 1, j + 1 + odd, lambda i, a: pair(i, a, False, 1), acc)
        acc = lax.fori_loop(0, (n_tiles - 1 - j) // 2, lambda t, a: pair(j + 1 + odd + 2 * t, a, False, 2), acc)
        pad = jnp.zeros((128 - AUG_ROWS, TA), F32)
        for h in range(HB):
            dk_ref[h] = acc[h][0]
            dv_ref[h] = acc[h][1]
            dq_ref[h] = jnp.concatenate([dqt_ref[h, j], pad], axis=0).T

        @pl.when((pl.program_id(0) == n_groups - 1) & (j == n_tiles - 1))
        def _():
            for cp in remote:
                cp.wait_recv()
            for cp in remote:
                cp.wait_send()
            for cp in local:
                cp.wait()

    assert n_groups == 1
    resident = pl.BlockSpec(memory_space=pltpu.VMEM)
    whole = lambda w: resident
    rows_blk = lambda r: resident
    tile = lambda w: pl.BlockSpec((HB, TA, w), lambda g, j: (g, j, 0))
    hbm = pl.BlockSpec(memory_space=pl.ANY)
    return pl.pallas_call(
        body,
        name="fox_bwd",
        grid=(n_groups, n_tiles),
        in_specs=[whole(128), tile(128), pl.BlockSpec((HB, 1, 128, TA), lambda g, j: (g, j, 0, 0)), tile(HEAD_DIM),
                  whole(HEAD_DIM), rows_blk(8), rows_blk(8), hbm, hbm],
        out_specs=[tile(128), tile(128), tile(HEAD_DIM), hbm, hbm],
        out_shape=[jax.ShapeDtypeStruct((FOX_HEADS, s_len, 128), F32), jax.ShapeDtypeStruct((FOX_HEADS, s_len, 128), F32),
                   jax.ShapeDtypeStruct((FOX_HEADS, s_len, HEAD_DIM), F32),
                   jax.ShapeDtypeStruct((N_DEV, 128, 512), BF16), jax.ShapeDtypeStruct((N_DEV, 128, D_MODEL), BF16)],
        scratch_shapes=[pltpu.VMEM((HB, n_tiles, AUG_ROWS, TA), F32),
                        pltpu.SemaphoreType.DMA((14,)), pltpu.SemaphoreType.DMA((14,)), pltpu.SemaphoreType.DMA((2,))],
        compiler_params=_params(("arbitrary", "arbitrary")),
    )(q_aug, k_aug, kt_aug, v_h, d_o, lse_rows, delta_rows, dw_kv16, dw_out16)


def _fox_post(proj, bf_pad, gq, gk, seg, dq_aug, dk_aug, dv_h):
    s_len = proj.shape[0]
    n_tiles = s_len // TM

    def body(q_ref, k_ref, f_ref, bf_ref, gq_ref, gk_ref, seg_ref, dqa_ref, dka_ref, dvh_ref,
             dq_ref, dk_ref, dv_ref, df_ref, dgq_ref, dgk_ref, dbf_ref, carry_ref):
        @pl.when(pl.program_id(0) == 0)
        def _():
            carry_ref[...] = jnp.zeros((1, 128), F32)
            dgq_ref[...] = jnp.zeros((1, HEAD_DIM), F32)
            dgk_ref[...] = jnp.zeros((1, HEAD_DIM), F32)
            dbf_ref[...] = jnp.zeros((1, 128), F32)

        lane = lax.broadcasted_iota(jnp.int32, (TM, 128), 1)
        seg = seg_ref[...]
        dqas = [dqa_ref[h] for h in range(FOX_HEADS)]
        dkas = [dka_ref[h] for h in range(FOX_HEADS)]
        d_cum = jnp.zeros((TM, 128), F32)
        for h in range(FOX_HEADS):
            d_cum = jnp.where(lane == h, dqas[h][:, 64:65] - dkas[h][:, 67:68], d_cum)
        q_all = q_ref[...]
        k_all = k_ref[...]
        rq = _head_rms(q_all, seg)
        rk = _head_rms(k_all, seg)
        dy_q = jnp.concatenate([t[:, 0:HEAD_DIM] for t in dqas], axis=1) * 0.125
        dy_k = jnp.concatenate([t[:, 0:HEAD_DIM] for t in dkas], axis=1)
        dq, dgq_rows = _head_rms_bwd(q_all * rq, rq, gq_ref[...], dy_q, seg)
        dk, dgk_rows = _head_rms_bwd(k_all * rk, rk, gk_ref[...], dy_k, seg)
        dq_ref[...] = dq.astype(BF16)
        dk_ref[...] = dk.astype(BF16)
        dv_ref[...] = jnp.concatenate([dvh_ref[h] for h in range(FOX_HEADS)], axis=1).astype(BF16)
        dgq_ref[...] += _fold_heads(jnp.sum(dgq_rows, axis=0, keepdims=True), FOX_HEADS)
        dgk_ref[...] += _fold_heads(jnp.sum(dgk_rows, axis=0, keepdims=True), FOX_HEADS)

        row = lax.broadcasted_iota(jnp.int32, (TM, TM), 0)
        col = lax.broadcasted_iota(jnp.int32, (TM, TM), 1)
        tri = jnp.where(col >= row, 1.0, 0.0).astype(BF16)
        hi, mid, lo = _split3(d_cum)
        d_logf = _dot(tri, hi) + _dot(tri, mid) + _dot(tri, lo) + carry_ref[...]
        carry_ref[...] = d_logf[0:1, :]
        z = f_ref[...] + bf_ref[...]
        d_f = jnp.where(lane < FOX_HEADS, d_logf / (1.0 + jnp.exp(z)), 0.0)
        df_ref[...] = d_f.astype(BF16)
        dbf_ref[...] += jnp.sum(d_f, axis=0, keepdims=True)

    rev = lambda i: n_tiles - 1 - i
    col_blk = lambda j: pl.BlockSpec((TM, PIECE), lambda i: (rev(i), j))
    head_blk = lambda w: pl.BlockSpec((FOX_HEADS, TM, w), lambda i: (0, rev(i), 0))
    out_piece = pl.BlockSpec((TM, PIECE), lambda i: (rev(i), 0))
    return pl.pallas_call(
        body,
        name="fox_post",
        grid=(n_tiles,),
        in_specs=[col_blk(1), col_blk(2), pl.BlockSpec((TM, 128), lambda i: (rev(i), MY_F_OFF // 128)),
                  _full((1, 128)), _full((1, PIECE)), _full((1, PIECE)), _full((PIECE, PIECE)),
                  head_blk(128), head_blk(128), head_blk(HEAD_DIM)],
        out_specs=[out_piece, out_piece, out_piece, pl.BlockSpec((TM, 128), lambda i: (rev(i), 0)),
                   _full((1, HEAD_DIM)), _full((1, HEAD_DIM)), _full((1, 128))],
        out_shape=[jax.ShapeDtypeStruct((s_len, PIECE), BF16)] * 3 + [
            jax.ShapeDtypeStruct((s_len, 128), BF16), jax.ShapeDtypeStruct((1, HEAD_DIM), F32),
            jax.ShapeDtypeStruct((1, HEAD_DIM), F32), jax.ShapeDtypeStruct((1, 128), F32)],
        scratch_shapes=[pltpu.VMEM((1, 128), F32)],
        compiler_params=_params(("arbitrary",)),
    )(proj, proj, proj, bf_pad, gq, gk, seg, dq_aug, dk_aug, dv_h)


def _mem_bwd(mem, g, w_kv, gk, dkm, dvm):
    def body(mem_ref, g_ref, w_ref, gk_ref, dkm_ref, dvm_ref, dw_ref, dg_ref, dgk_ref, dkv_ref):
        m = mem_ref[...]
        r = _rms(m)
        a = m * r
        mn16 = (a * g_ref[...]).astype(BF16)
        kv = _dot(mn16, w_ref[...])
        dgk = jnp.zeros((N_MEM, HEAD_DIM), F32)
        for h in range(MEM_HEADS):
            sl = slice(HEAD_DIM * h, HEAD_DIM * (h + 1))
            kh = kv[:, sl]
            rk = _rms(kh)
            dk, dg_rows = _rms_bwd(kh * rk, rk, gk_ref[...], dkm_ref[:, sl])
            dkv_ref[:, sl] = dk.astype(BF16)
            dgk = dgk + dg_rows
        dkv_ref[:, 256:512] = dvm_ref[...].astype(BF16)
        dgk_ref[...] = jnp.sum(dgk, axis=0, keepdims=True)
        dkv16 = dkv_ref[...]
        dw_ref[...] = _dot_tn(mn16, dkv16).astype(BF16)
        dg_ref[...] = jnp.sum(_dot_nt(dkv16, w_ref[...]) * a, axis=0, keepdims=True)

    return pl.pallas_call(
        body,
        name="mem_bwd",
        out_shape=(jax.ShapeDtypeStruct((D_MODEL, 512), BF16), jax.ShapeDtypeStruct((1, D_MODEL), F32),
                   jax.ShapeDtypeStruct((1, HEAD_DIM), F32)),
        in_specs=[pl.BlockSpec(memory_space=pltpu.VMEM)] * 6,
        out_specs=(pl.BlockSpec(memory_space=pltpu.VMEM),) * 3,
        scratch_shapes=[pltpu.VMEM((N_MEM, 512), BF16)],
        compiler_params=pltpu.CompilerParams(vmem_limit_bytes=VMEM_LIMIT),
    )(mem, g, w_kv, gk, dkm, dvm)


def _grad_x_exchange(pieces, d_f, w_my, x, norm_g, d_out, dw_in, land_kv, land_out, d_mem_g, d_scale, d_bf, d_gq, d_gk, d_gqm,
                     d_gkm, dwp_bd, loss_blk):
    s_len = x.shape[0]
    n_steps = s_len // TM
    step2, step3 = n_steps // 4, (11 * n_steps) // 16
    half = D_MODEL // 2

    def body(p0, p1, p2, p3, p4, p5, df_ref, x_ref, dout_ref, w_ref, g_ref, in_ref, lkv_ref, lout_ref,
             dmg, dsc, dbf, dgq, dgk, dgqm, dgkm, dwp, loss_ref,
             gx_ref, rin_ref, rkv_ref, rout_ref, sred_ref,
             dng, land1, send2a, send2b, keep2a, keep2b, land2a, land2b, send3a, send3b, land3a, land3b, sb_ref, sland,
             send_sems, recv_sems):
        i = pl.program_id(0)
        x_, y_, c_, me = _my_place()
        sibling, x_nbr, y_nbr = (x_, y_, 1 - c_), (1 - x_, y_, c_), (x_, 1 - y_, c_)

        def rcopy(src, dst, sem, to):
            return pltpu.make_async_remote_copy(src_ref=src, dst_ref=dst, send_sem=send_sems.at[sem], recv_sem=recv_sems.at[sem],
                                                device_id=to, device_id_type=MESH)

        early_rows, late_rows = pl.ds(8, SB_ROWS - 8), pl.ds(0, 8)
        small_early, small_late = [], []
        for k in range(1, N_DEV):
            peer = (x_ ^ (k >> 2), y_ ^ ((k >> 1) & 1), c_ ^ (k & 1))
            small_early.append(rcopy(sb_ref.at[early_rows], sland.at[me, early_rows], k - 1, peer))
            small_late.append(rcopy(sb_ref.at[late_rows], sland.at[me, late_rows], 16 + k, peer))
        small = small_early + small_late
        stage1 = [rcopy(in_ref.at[2 * k + 1 - c_], land1.at[k], 7 + k, sibling) for k in range(4)]
        stage2 = [rcopy(send2a.at[j], land2a.at[j], 11 + j, x_nbr) for j in range(2)]
        stage2 += [rcopy(send2b.at[j], land2b.at[j], 13 + j, y_nbr) for j in range(2)]
        stage3 = [rcopy(send3a, land3a, 15, y_nbr), rcopy(send3b, land3b, 16, x_nbr)]
        top, bot = slice(0, half), slice(half, D_MODEL)

        @pl.when(i == 0)
        def _():
            dng[...] = jnp.zeros((1, D_MODEL), F32)
            for cp in stage1:
                cp.start()
            sb_ref[...] = jnp.zeros((SB_ROWS, SB_W), F32)
            sb_ref[SB_POOL_SCALE:SB_POOL_SCALE + 1, :] = dsc[...]
            sb_ref[SB_B_F:SB_B_F + 1, 0:128] = dbf[...]
            for row, ref in ((SB_FOX_Q, dgq), (SB_FOX_K, dgk), (SB_MEM_Q, dgqm), (SB_MEM_K, dgkm)):
                sb_ref[row:row + 1, 0:HEAD_DIM] = ref[...]
            sb_ref[SB_LOSS:SB_LOSS + 1, 0:128] = loss_ref[0:1, :]
            for grp in range(4):
                sl = slice(64 * grp, 64 * (grp + 1))
                sb_ref[SB_W_POOL:SB_W_POOL + 64, sl] = dwp[sl, sl]
            for cp in small_early:
                cp.start()

        @pl.when(i == step2)
        def _():
            for cp in stage1:
                cp.wait_recv()

            def chip_sum(k, cols):
                return in_ref[2 * k + c_, :, cols].astype(F32) + land1[k, :, cols].astype(F32)

            for j in range(2):
                send2a[j] = chip_sum(2 * (1 - x_) + j, top).astype(BF16)
                keep2a[j] = chip_sum(2 * x_ + j, top)
                send2b[j] = chip_sum(2 * j + 1 - y_, bot).astype(BF16)
                keep2b[j] = chip_sum(2 * j + y_, bot)
            for cp in stage2:
                cp.start()

        @pl.when(i == step3)
        def _():
            for cp in stage2:
                cp.wait_recv()
            for j in range(2):
                keep2a[j] = keep2a[j] + land2a[j].astype(F32)
                keep2b[j] = keep2b[j] + land2b[j].astype(F32)
            send3a[...] = keep2a[1 - y_].astype(BF16)
            send3b[...] = keep2b[1 - x_].astype(BF16)
            for cp in stage3:
                cp.start()

        dh = _dot_nt(df_ref[...], w_ref[:, MY_F_OFF:MY_WIDTH])
        for j, p in enumerate((p0, p1, p2, p3, p4, p5)):
            dh = dh + _dot_nt(p[...], w_ref[:, PIECE * j:PIECE * (j + 1)])
        xv = x_ref[...]
        r = _rms(xv)
        dx, dg_rows = _rms_bwd(xv * r, r, g_ref[...], dh)
        gx_ref[...] = dout_ref[...] + dx
        dng[...] += jnp.sum(dg_rows, axis=0, keepdims=True)

        @pl.when(i == n_steps - 1)
        def _():
            for k in range(4):
                sb_ref[SB_NORM_G + k:SB_NORM_G + k + 1, :] = dng[:, SB_W * k:SB_W * (k + 1)]
                sb_ref[SB_MEM_NORM_G + k:SB_MEM_NORM_G + k + 1, :] = dmg[:, SB_W * k:SB_W * (k + 1)]
            for cp in small_late:
                cp.start()
            sland[me] = sb_ref[...]
            for cp in stage3:
                cp.wait_recv()
            rin_ref[:, top] = keep2a[y_] + land3a[...].astype(F32)
            rin_ref[:, bot] = keep2b[x_] + land3b[...].astype(F32)
            for cp in small:
                cp.wait_recv()
            for cp in small + stage1 + stage2 + stage3:
                cp.wait_send()
            for land, red in ((lkv_ref, rkv_ref), (lout_ref, rout_ref), (sland, sred_ref)):
                acc = land[0].astype(F32)
                for d in range(1, N_DEV):
                    acc = acc + land[d].astype(F32)
                red[...] = acc

    piece = pl.BlockSpec((TM, PIECE), lambda i: (i, 0))
    row = pl.BlockSpec((TM, D_MODEL), lambda i: (i, 0))
    vmem = pl.BlockSpec(memory_space=pltpu.VMEM)
    half_chunk = lambda n, dt: pltpu.VMEM((n, CHUNK_ROWS, half), dt)
    whole = (w_my, norm_g, dw_in, land_kv, land_out, d_mem_g, d_scale, d_bf, d_gq, d_gk, d_gqm, d_gkm, dwp_bd, loss_blk)
    return pl.pallas_call(
        body,
        name="grad_x_exchange",
        grid=(n_steps,),
        in_specs=[piece] * 6 + [pl.BlockSpec((TM, 128), lambda i: (i, 0)), row, row] + [vmem] * len(whole),
        out_specs=[row, vmem, vmem, vmem, vmem],
        out_shape=[jax.ShapeDtypeStruct((s_len, D_MODEL), F32), jax.ShapeDtypeStruct((CHUNK_ROWS, D_MODEL), F32),
                   jax.ShapeDtypeStruct((128, 512), F32), jax.ShapeDtypeStruct((128, D_MODEL), F32),
                   jax.ShapeDtypeStruct((SB_ROWS, SB_W), F32)],
        scratch_shapes=[
            pltpu.VMEM((1, D_MODEL), F32),
            pltpu.VMEM((4, CHUNK_ROWS, D_MODEL), BF16),
            half_chunk(2, BF16), half_chunk(2, BF16), half_chunk(2, F32), half_chunk(2, F32), half_chunk(2, BF16), half_chunk(2, BF16),
            pltpu.VMEM((CHUNK_ROWS, half), BF16), pltpu.VMEM((CHUNK_ROWS, half), BF16),
            pltpu.VMEM((CHUNK_ROWS, half), BF16), pltpu.VMEM((CHUNK_ROWS, half), BF16),
            pltpu.VMEM((SB_ROWS, SB_W), F32),
            pltpu.VMEM((N_DEV, SB_ROWS, SB_W), F32),
            pltpu.SemaphoreType.DMA((24,)),
            pltpu.SemaphoreType.DMA((24,)),
        ],
        compiler_params=_params(("arbitrary",)),
    )(*pieces, d_f, x, d_out, *whole)


def _grad_w_in(h16, pieces, d_f):
    s_len = h16.shape[0]
    tk = 512

    def body(h_ref, p0, p1, p2, p3, p4, p5, df_ref, out_ref, dw_ref):
        @pl.when(pl.program_id(0) == 0)
        def _():
            dw_ref[...] = jnp.zeros((D_MODEL, MY_WIDTH), F32)

        h = h_ref[...]
        for j, p in enumerate((p0, p1, p2, p3, p4, p5)):
            dw_ref[:, PIECE * j:PIECE * (j + 1)] += _dot_tn(h, p[...])
        dw_ref[:, MY_F_OFF:MY_WIDTH] += _dot_tn(h, df_ref[...])

        @pl.when(pl.program_id(0) == pl.num_programs(0) - 1)
        def _():
            for d in range(N_DEV):
                parts = [dw_ref[:, start:start + width] for start, width in _chunk_segments(d)]
                parts.append(jnp.zeros((D_MODEL, 512 - IN_CHUNK), F32))
                out_ref[d] = jnp.concatenate(parts, axis=1).T[0:CHUNK_ROWS].astype(BF16)

    piece = pl.BlockSpec((tk, PIECE), lambda i: (i, 0))
    return pl.pallas_call(
        body,
        name="grad_w_in",
        grid=(s_len // tk,),
        in_specs=[pl.BlockSpec((tk, D_MODEL), lambda i: (i, 0))] + [piece] * 6 + [pl.BlockSpec((tk, 128), lambda i: (i, 0))],
        out_specs=_full((N_DEV, CHUNK_ROWS, D_MODEL)),
        out_shape=jax.ShapeDtypeStruct((N_DEV, CHUNK_ROWS, D_MODEL), BF16),
        scratch_shapes=[pltpu.VMEM((D_MODEL, MY_WIDTH), F32)],
        compiler_params=_params(("arbitrary",)),
    )(h16, *pieces, d_f)


def _adamw(w, g, m, v):
    m = ADAM_B1 * m + (1.0 - ADAM_B1) * g
    v = ADAM_B2 * v + (1.0 - ADAM_B2) * (g * g)
    m_hat = m / (1.0 - ADAM_B1 ** ADAM_STEP)
    v_hat = v / (1.0 - ADAM_B2 ** ADAM_STEP)
    return -ADAM_LR * (m_hat / (jnp.sqrt(v_hat) + ADAM_EPS) + ADAM_WD * w), m, v


PARAM_ORDER = ("norm_g", "w_in", "b_f", "w_pool", "pool_scale", "fox_q_g", "fox_k_g", "mem_norm_g", "w_mem_kv", "mem_q_g", "mem_k_g", "w_out")


def _update(red_in, red_kv, red_out, sred, params):
    def body(rin_ref, rkv_ref, rout_ref, sred_ref, *refs):
        ins, outs = refs[:3 * len(PARAM_ORDER)], refs[3 * len(PARAM_ORDER):]
        wide = lambda row: jnp.concatenate([sred_ref[row + k:row + k + 1, :] for k in range(4)], axis=1)
        head = lambda row: sred_ref[row:row + 1, 0:HEAD_DIM]
        grads = {
            "norm_g": lambda: wide(SB_NORM_G), "w_in": lambda: rin_ref[...], "b_f": lambda: sred_ref[SB_B_F:SB_B_F + 1, 0:FOX_HEADS],
            "pool_scale": lambda: sred_ref[SB_POOL_SCALE:SB_POOL_SCALE + 1, :], "fox_q_g": lambda: head(SB_FOX_Q),
            "fox_k_g": lambda: head(SB_FOX_K), "mem_norm_g": lambda: wide(SB_MEM_NORM_G), "w_mem_kv": lambda: rkv_ref[...],
            "mem_q_g": lambda: head(SB_MEM_Q), "mem_k_g": lambda: head(SB_MEM_K), "w_out": lambda: rout_ref[...],
        }
        for p, name in enumerate(PARAM_ORDER):
            w_ref, m_ref, v_ref = ins[3 * p:3 * p + 3]
            g_out, d_out, m_out, v_out = outs[4 * p:4 * p + 4]
            if name == "w_pool":
                for grp in range(4):
                    g = sred_ref[SB_W_POOL:SB_W_POOL + 64, 64 * grp:64 * (grp + 1)]
                    delta, m_new, v_new = _adamw(w_ref[grp], g, m_ref[grp], v_ref[grp])
                    g_out[grp], d_out[grp], m_out[grp], v_out[grp] = g, delta, m_new, v_new
            elif name == "w_in":
                for j in range(8):
                    rows = pl.ds(j, IN_CHUNK, stride=8)
                    g = rin_ref[0:IN_CHUNK, 128 * j:128 * (j + 1)]
                    delta, m_new, v_new = _adamw(w_ref[rows, :], g, m_ref[rows, :], v_ref[rows, :])
                    g_out[rows, :], d_out[rows, :], m_out[rows, :], v_out[rows, :] = g, delta, m_new, v_new
            else:
                g = grads[name]()
                delta, m_new, v_new = _adamw(w_ref[...], g, m_ref[...], v_ref[...])
                g_out[...], d_out[...], m_out[...], v_out[...] = g, delta, m_new, v_new

    flat = [t for name in PARAM_ORDER for t in params[name]]
    shapes = [params[name][0].shape for name in PARAM_ORDER for _ in range(4)]
    outs = pl.pallas_call(
        body,
        name="adamw_update",
        out_shape=tuple(jax.ShapeDtypeStruct(s, F32) for s in shapes),
        in_specs=[pl.BlockSpec(memory_space=pltpu.VMEM)] * (4 + len(flat)),
        out_specs=(pl.BlockSpec(memory_space=pltpu.VMEM),) * len(shapes),
        compiler_params=pltpu.CompilerParams(vmem_limit_bytes=VMEM_LIMIT),
    )(red_in, red_kv, red_out, sred, *flat)
    return {name: outs[4 * p:4 * p + 4] for p, name in enumerate(PARAM_ORDER)}


def kernel(x, mem, norm_g, w_in, b_f, w_pool, pool_scale, fox_q_g, fox_k_g, mem_norm_g, w_mem_kv, mem_q_g, mem_k_g, w_out, loss_target, m_norm_g, m_w_in, m_b_f, m_w_pool, m_pool_scale, m_fox_q_g, m_fox_k_g, m_mem_norm_g, m_w_mem_kv, m_mem_q_g, m_mem_k_g, m_w_out, v_norm_g, v_w_in, v_b_f, v_w_pool, v_pool_scale, v_fox_q_g, v_fox_k_g, v_mem_norm_g, v_w_mem_kv, v_mem_q_g, v_mem_k_g, v_w_out):
    given = dict(norm_g=(norm_g, m_norm_g, v_norm_g), w_in=(w_in, m_w_in, v_w_in), b_f=(b_f, m_b_f, v_b_f),
                 w_pool=(w_pool, m_w_pool, v_w_pool), pool_scale=(pool_scale, m_pool_scale, v_pool_scale),
                 fox_q_g=(fox_q_g, m_fox_q_g, v_fox_q_g), fox_k_g=(fox_k_g, m_fox_k_g, v_fox_k_g),
                 mem_norm_g=(mem_norm_g, m_mem_norm_g, v_mem_norm_g), w_mem_kv=(w_mem_kv, m_w_mem_kv, v_w_mem_kv),
                 mem_q_g=(mem_q_g, m_mem_q_g, v_mem_q_g), mem_k_g=(mem_k_g, m_mem_k_g, v_mem_k_g), w_out=(w_out, m_w_out, v_w_out))
    params = {name: tuple(t[0] if t.ndim > 2 else t for t in wmv) for name, wmv in given.items()}
    params["w_in"] = tuple(t.T.reshape(IN_CHUNK * 8, 128) for t in params["w_in"])
    x2, mem2, target2 = x[0], mem[0], loss_target[0]

    seg = jnp.asarray(np.kron(np.eye(FOX_HEADS), np.full((HEAD_DIM, HEAD_DIM), 1.0 / HEAD_DIM)), BF16)
    w_my, w_kv16, w_out16, wp_bd, bf_pad, gq8, gk8, gqm4 = _gather_w_in(
        params["w_in"][0], params["w_mem_kv"][0], params["w_out"][0], params["w_pool"][0], b_f, fox_q_g, fox_k_g, mem_q_g)
    proj, h16 = _fwd_proj(x2, norm_g, w_my)
    q_aug, k_aug, v_h, kt_aug, vt_aug = _fox_prep(proj, bf_pad, gq8, gk8, seg)
    olse, lse_rows, w_kv_all, w_out_all = _fox_fwd(q_aug, k_aug, vt_aug, w_kv16, w_out16)
    km, vm = _mem_prep(mem2, mem_norm_g, w_kv_all, mem_k_g)
    mixed = _mix_fwd(proj, olse, km, vm, wp_bd, pool_scale, gqm4, seg)
    d_out, d_mixed, dw_out, loss_blk = _out_loss(mixed, x2, target2, w_out_all)

    d_ua, d_gb, d_qm, d_o, delta_rows, dwp_bd, d_scale, dkm, dvm, d_gqm = _mix_bwd(proj, olse, d_mixed, km, vm, wp_bd, pool_scale,
                                                                                    gqm4, seg)
    dw_kv, d_mem_g, d_gkm = _mem_bwd(mem2, mem_norm_g, w_kv_all, mem_k_g, dkm, dvm)
    dq_aug, dk_aug, dv_h, land_kv, land_out = _fox_bwd(q_aug, k_aug, kt_aug, v_h, d_o, lse_rows, delta_rows, dw_kv, dw_out)
    d_q, d_k, d_v, d_f, d_gq, d_gk, d_bf = _fox_post(proj, bf_pad, gq8, gk8, seg, dq_aug, dk_aug, dv_h)
    pieces = (d_ua, d_q, d_k, d_v, d_gb, d_qm)
    dw_in = _grad_w_in(h16, pieces, d_f)
    grad_x, red_in, red_kv, red_out, sred = _grad_x_exchange(pieces, d_f, w_my, x2, norm_g, d_out, dw_in, land_kv, land_out, d_mem_g,
                                                             d_scale, d_bf, d_gq, d_gk, d_gqm, d_gkm, dwp_bd, loss_blk)
    new = _update(red_in, red_kv, red_out, sred, params)
    result = [sred[SB_LOSS, 0], grad_x[None]]
    for kind in range(4):
        for name in PARAM_ORDER:
            out = new[name][kind]
            if name == "w_in":
                out = out.reshape(IN_CHUNK, D_MODEL).T
            result.append(out[None] if given[name][0].ndim > 2 else out)
    return tuple(result)
```

```python
import functools

import jax
import jax.numpy as jnp
import numpy as np
from jax import lax
from jax.experimental import pallas as pl
from jax.experimental.pallas import tpu as pltpu

F32 = jnp.float32
BF16 = jnp.bfloat16
MESH = pl.DeviceIdType.MESH

N_DEV = 8
D_MODEL = 1024
HEAD_DIM = 64
FOX_HEADS = 8
MEM_HEADS = 4
N_MEM = 256
POOL_WIDTH = 256
EPS = 1e-6
IN_WIDTH = 3080
IN_CHUNK = IN_WIDTH // N_DEV
CHUNK_ROWS = 400
F_OFF = 2048
MY_WIDTH = 3200
MY_F_OFF = 3072
PIECE = 512
TM = 256
TMM = 512
TA = 256
HB = 8
HB_FWD = 8
AUG_ROWS = 72
HALO = 16
VMEM_LIMIT = 56 * 1024 * 1024

ADAM_LR = 0.001
ADAM_B1 = 0.9
ADAM_B2 = 0.999
ADAM_EPS = 1e-08
ADAM_WD = 0.01
ADAM_STEP = 10


def _dot(a, b):
    return jnp.dot(a, b, preferred_element_type=F32)


def _dot_nt(a, b):
    return lax.dot_general(a, b, (((1,), (1,)), ((), ())), preferred_element_type=F32)


def _dot_tn(a, b):
    return lax.dot_general(a, b, (((0,), (0,)), ((), ())), preferred_element_type=F32)


def _sigmoid(g):
    return 0.5 + 0.5 * jnp.tanh(0.5 * g)


def _split3(v):
    hi = v.astype(BF16)
    r1 = v - hi.astype(F32)
    mid = r1.astype(BF16)
    lo = (r1 - mid.astype(F32)).astype(BF16)
    return hi, mid, lo


def _rms(v):
    return lax.rsqrt(jnp.mean(v * v, axis=-1, keepdims=True) + EPS)


def _rms_bwd(a, r, g, dy):
    da = dy * g
    return r * (da - a * jnp.mean(da * a, axis=-1, keepdims=True)), dy * a


def _head_mean(z, seg):
    hi = z.astype(BF16)
    lo = (z - hi.astype(F32)).astype(BF16)
    if z.shape[1] == 256:
        return _dot(hi, seg) + _dot(lo, seg)
    half = seg[0:256, 0:256]
    return jnp.concatenate([_dot(hi[:, 0:256], half) + _dot(lo[:, 0:256], half),
                            _dot(hi[:, 256:512], half) + _dot(lo[:, 256:512], half)], axis=1)


def _head_rms(v, seg):
    return lax.rsqrt(_head_mean(v * v, seg) + EPS)


def _head_rms_bwd(a, r, g, dy, seg):
    da = dy * g
    return r * (da - a * _head_mean(da * a, seg)), dy * a


def _fold_heads(row, n_heads):
    out = row[:, 0:HEAD_DIM]
    for h in range(1, n_heads):
        out = out + row[:, HEAD_DIM * h:HEAD_DIM * (h + 1)]
    return out


def _params(sem, limit=VMEM_LIMIT):
    return pltpu.CompilerParams(dimension_semantics=sem, vmem_limit_bytes=limit)


def _full(shape):
    return pl.BlockSpec(shape, lambda *_: (0,) * len(shape))


def _chunk_segments(d):
    runs = []
    for lo, hi, shift in ((0, F_OFF, 0), (F_OFF, F_OFF + FOX_HEADS, MY_F_OFF - F_OFF), (F_OFF + FOX_HEADS, IN_WIDTH, -FOX_HEADS)):
        a, b = max(lo, IN_CHUNK * d), min(hi, IN_CHUNK * (d + 1))
        if a < b:
            runs.append((a + shift, b - a))
    return runs


def _my_place():
    x, y, c = lax.axis_index("x"), lax.axis_index("y"), lax.axis_index("c")
    return x, y, c, 4 * x + 2 * y + c


def _shard_rows(dev):
    return pl.ds(pl.multiple_of(128 * dev, 128), 128)


def _gather_w_in(w_in, w_kv, w_out, w_pool, b_f, gq, gk, gqm):
    def body(win_ref, wkv_ref, wout_ref, wp_ref, bf_ref, gq_ref, gk_ref, gqm_ref, wmy_ref, kv16_ref, out16_ref, wpbd_ref, bfpad_ref,
             gq8_ref, gk8_ref, gqm4_ref, in_all, pay_in, win_rows, send_sems, recv_sems, load_sem):
        x, y, c, me = _my_place()
        here, sibling = (x, y, c), (x, y, 1 - c)
        x_chip, y_chip, far_chip = (1 - x, y), (x, 1 - y), (1 - x, 1 - y)
        relay_from = (x ^ (1 - c), y ^ c)
        relay_to = (x ^ c, y ^ (1 - c))

        load = pltpu.make_async_copy(win_ref, win_rows, load_sem)
        load.start()
        load.wait()
        pay_in[...] = jnp.zeros((CHUNK_ROWS, D_MODEL), BF16)
        for j in range(8):
            pay_in[0:IN_CHUNK, 128 * j:128 * (j + 1)] = win_rows[pl.ds(j, IN_CHUNK, stride=8), :].astype(BF16)

        def copy(k, block, to, own=False):
            px, py, pc = block
            dst = in_all.at[4 * px + 2 * py + pc]
            return pltpu.make_async_remote_copy(src_ref=pay_in if own else dst, dst_ref=dst, send_sem=send_sems.at[k],
                                                recv_sem=recv_sems.at[k], device_id=to, device_id_type=MESH)

        first = [copy(0, here, sibling, own=True), copy(1, here, (*x_chip, c), own=True), copy(2, here, (*y_chip, c), own=True)]
        for cp in first:
            cp.start()
        in_all[me] = pay_in[...]
        kv16_ref[...] = wkv_ref[...].astype(BF16)
        out16_ref[...] = wout_ref[...].astype(BF16)
        wpbd_ref[...] = jnp.zeros((POOL_WIDTH, POOL_WIDTH), BF16)
        for grp in range(4):
            sl = slice(64 * grp, 64 * (grp + 1))
            wpbd_ref[sl, sl] = wp_ref[grp].astype(BF16)
        bfpad_ref[...] = jnp.zeros((1, 128), F32)
        bfpad_ref[:, 0:FOX_HEADS] = bf_ref[...]
        gq8_ref[...] = jnp.concatenate([gq_ref[...]] * FOX_HEADS, axis=1)
        gk8_ref[...] = jnp.concatenate([gk_ref[...]] * FOX_HEADS, axis=1)
        gqm4_ref[...] = jnp.concatenate([gqm_ref[...]] * MEM_HEADS, axis=1)
        copy(1 + c, (*relay_from, c), here).wait_recv()
        passed = [copy(3, (*relay_from, c), (*relay_to, c)), copy(4, (*relay_from, c), sibling)]
        for cp in passed:
            cp.start()
        copy(2 - c, (*relay_to, c), here).wait_recv()
        passed.append(copy(5, (*relay_to, c), sibling))
        passed[-1].start()
        copy(3, (*far_chip, c), here).wait_recv()
        passed.append(copy(6, (*far_chip, c), sibling))
        passed[-1].start()
        copy(0, sibling, here).wait_recv()
        for k, chip in ((4, relay_to), (5, relay_from), (6, far_chip)):
            copy(k, (*chip, 1 - c), here).wait_recv()
        for cp in first + passed:
            cp.wait_send()

        row_pad = jnp.zeros((512 - CHUNK_ROWS, 256), F32)
        for r0 in range(0, D_MODEL, 256):
            ch = [jnp.concatenate([in_all[d, :, r0:r0 + 256].astype(F32), row_pad], axis=0).T[:, 0:IN_CHUNK] for d in range(N_DEV)]
            lo = F_OFF - 5 * IN_CHUNK
            full = jnp.concatenate(ch[:5] + [ch[5][:, :lo], ch[5][:, lo + FOX_HEADS:], ch[6], ch[7], ch[5][:, lo:lo + FOX_HEADS],
                                             jnp.zeros((256, MY_WIDTH - IN_WIDTH), F32)], axis=1)
            wmy_ref[r0:r0 + 256, :] = full.astype(BF16)

    return pl.pallas_call(
        body,
        name="gather_w_in",
        out_shape=(jax.ShapeDtypeStruct((D_MODEL, MY_WIDTH), BF16), jax.ShapeDtypeStruct((128, 512), BF16),
                   jax.ShapeDtypeStruct((128, D_MODEL), BF16), jax.ShapeDtypeStruct((POOL_WIDTH, POOL_WIDTH), BF16),
                   jax.ShapeDtypeStruct((1, 128), F32), jax.ShapeDtypeStruct((1, PIECE), F32), jax.ShapeDtypeStruct((1, PIECE), F32),
                   jax.ShapeDtypeStruct((1, 256), F32)),
        in_specs=[pl.BlockSpec(memory_space=pl.ANY)] + [pl.BlockSpec(memory_space=pltpu.VMEM)] * 7,
        out_specs=(pl.BlockSpec(memory_space=pltpu.VMEM),) * 8,
        scratch_shapes=[
            pltpu.VMEM((N_DEV, CHUNK_ROWS, D_MODEL), BF16),
            pltpu.VMEM((CHUNK_ROWS, D_MODEL), BF16),
            pltpu.VMEM((IN_CHUNK * 8, 128), F32),
            pltpu.SemaphoreType.DMA((7,)),
            pltpu.SemaphoreType.DMA((7,)),
            pltpu.SemaphoreType.DMA,
        ],
        compiler_params=pltpu.CompilerParams(vmem_limit_bytes=VMEM_LIMIT),
    )(w_in, w_kv, w_out, w_pool, b_f, gq, gk, gqm)


def _row_block_exchange(kv_ref, out_ref, kv_dst, out_dst, send_sems, recv_sems, local_sems, gather):
    x, y, c, me = _my_place()
    remote = []
    for k in range(1, N_DEV):
        px, py, pc = x ^ (k >> 2), y ^ ((k >> 1) & 1), c ^ (k & 1)
        peer = 4 * px + 2 * py + pc
        for fam, (src, dst) in enumerate(((kv_ref, kv_dst), (out_ref, out_dst))):
            remote.append(pltpu.make_async_remote_copy(
                src_ref=src if gather else src.at[_shard_rows(peer)], dst_ref=dst.at[_shard_rows(me)] if gather else dst.at[me],
                send_sem=send_sems.at[7 * fam + k - 1], recv_sem=recv_sems.at[7 * fam + k - 1],
                device_id=(px, py, pc), device_id_type=MESH))
    local = [pltpu.make_async_copy(src if gather else src.at[_shard_rows(me)], dst.at[_shard_rows(me)] if gather else dst.at[me],
                                   local_sems.at[fam]) for fam, (src, dst) in enumerate(((kv_ref, kv_dst), (out_ref, out_dst)))]
    return remote, local


SB_ROWS, SB_W = 80, 256
SB_NORM_G, SB_MEM_NORM_G, SB_POOL_SCALE, SB_B_F, SB_FOX_Q, SB_FOX_K, SB_MEM_Q, SB_MEM_K, SB_LOSS, SB_W_POOL = 0, 4, 8, 9, 10, 11, 12, 13, 14, 16


def _fwd_proj(x, norm_g, w_my):
    s_len = x.shape[0]

    def body(x_ref, g_ref, w_ref, proj_ref, h_ref):
        xv = x_ref[...]
        h = (xv * _rms(xv) * g_ref[...]).astype(BF16)
        h_ref[...] = h
        proj_ref[...] = _dot(h, w_ref[...])

    return pl.pallas_call(
        body,
        name="fwd_proj",
        grid=(s_len // TMM,),
        in_specs=[pl.BlockSpec((TMM, D_MODEL), lambda i: (i, 0)), _full((1, D_MODEL)), _full((D_MODEL, MY_WIDTH))],
        out_specs=[pl.BlockSpec((TMM, MY_WIDTH), lambda i: (i, 0)), pl.BlockSpec((TMM, D_MODEL), lambda i: (i, 0))],
        out_shape=[jax.ShapeDtypeStruct((s_len, MY_WIDTH), F32), jax.ShapeDtypeStruct((s_len, D_MODEL), BF16)],
        compiler_params=_params(("parallel",)),
    )(x, norm_g, w_my)


def _log_sigmoid(z):
    return jnp.minimum(z, 0.0) - jnp.log(1.0 + jnp.exp(-jnp.abs(z)))


def _forget_lane_maps():
    to_q = np.zeros((128, FOX_HEADS * 128), np.float32)
    to_k = np.zeros((128, FOX_HEADS * 128), np.float32)
    for h in range(FOX_HEADS):
        for term in range(3):
            to_q[8 * term + h, 128 * h + 64 + term] = 1.0
            to_q[24, 128 * h + 67 + term] = 1.0
            to_k[24, 128 * h + 64 + term] = 1.0
            to_k[8 * term + h, 128 * h + 67 + term] = -1.0
    return jnp.asarray(to_q, BF16), jnp.asarray(to_k, BF16)


def _fox_prep(proj, bf_pad, gq, gk, seg):
    s_len = proj.shape[0]
    to_q, to_k = _forget_lane_maps()

    def body(q_ref, k_ref, v_ref, f_ref, bf_ref, gq_ref, gk_ref, seg_ref, eq_ref, ek_ref,
             qa_ref, ka_ref, vh_ref, kt_ref, vt_ref, carry_ref):
        @pl.when(pl.program_id(0) == 0)
        def _():
            carry_ref[...] = jnp.zeros((1, 128), F32)

        lane = lax.broadcasted_iota(jnp.int32, (TM, 128), 1)
        logf = jnp.where(lane < FOX_HEADS, _log_sigmoid(f_ref[...] + bf_ref[...]), 0.0)
        row = lax.broadcasted_iota(jnp.int32, (TM, TM), 0)
        col = lax.broadcasted_iota(jnp.int32, (TM, TM), 1)
        tri = jnp.where(col <= row, 1.0, 0.0).astype(BF16)
        hi, mid, lo = _split3(logf)
        cum = _dot(tri, hi) + _dot(tri, mid) + _dot(tri, lo) + carry_ref[...]
        carry_ref[...] = cum[TM - 1:TM, :]
        f_hi, f_mid, f_lo = [t.astype(F32) for t in _split3(cum)]
        packed = (f_hi + pltpu.roll(f_mid, 8, 1) + pltpu.roll(f_lo, 16, 1) + jnp.where(lane == 24, 1.0, 0.0)).astype(BF16)
        extra_q = _dot(packed, eq_ref[...])
        extra_k = _dot(packed, ek_ref[...])
        one_at_64 = jnp.where(lane == 64, 1.0, 0.0)
        zeros = jnp.zeros((TM, HEAD_DIM), F32)
        q_all = q_ref[...]
        k_all = k_ref[...]
        qn_all = q_all * _head_rms(q_all, seg_ref[...]) * gq_ref[...] * 0.125
        kn_all = k_all * _head_rms(k_all, seg_ref[...]) * gk_ref[...]
        for h in range(FOX_HEADS):
            sl = slice(HEAD_DIM * h, HEAD_DIM * (h + 1))
            tile = slice(128 * h, 128 * (h + 1))
            qa = jnp.where(lane < 64, jnp.concatenate([qn_all[:, sl], zeros], axis=1), extra_q[:, tile])
            ka = jnp.where(lane < 64, jnp.concatenate([kn_all[:, sl], zeros], axis=1), extra_k[:, tile])
            vh = v_ref[:, sl]
            v_aug = jnp.where(lane < 64, jnp.concatenate([vh, zeros], axis=1), one_at_64)
            qa_ref[h] = qa.astype(BF16)
            ka_ref[h] = ka.astype(BF16)
            vh_ref[h] = vh.astype(BF16)
            kt_ref[h, 0] = ka.T.astype(BF16)
            vt_ref[h, 0] = v_aug.T.astype(BF16)

    col_blk = lambda j: pl.BlockSpec((TM, PIECE), lambda i: (i, j))
    head_blk = lambda w: pl.BlockSpec((FOX_HEADS, TM, w), lambda i: (0, i, 0))
    tile_blk = pl.BlockSpec((FOX_HEADS, 1, 128, TM), lambda i: (0, i, 0, 0))
    tiled = jax.ShapeDtypeStruct((FOX_HEADS, s_len // TM, 128, TM), BF16)
    return pl.pallas_call(
        body,
        name="fox_prep",
        grid=(s_len // TM,),
        in_specs=[col_blk(1), col_blk(2), col_blk(3), pl.BlockSpec((TM, 128), lambda i: (i, MY_F_OFF // 128)),
                  _full((1, 128)), _full((1, PIECE)), _full((1, PIECE)), _full((PIECE, PIECE)),
                  _full((128, FOX_HEADS * 128)), _full((128, FOX_HEADS * 128))],
        out_specs=[head_blk(128), head_blk(128), head_blk(HEAD_DIM), tile_blk, tile_blk],
        out_shape=[jax.ShapeDtypeStruct((FOX_HEADS, s_len, 128), BF16), jax.ShapeDtypeStruct((FOX_HEADS, s_len, 128), BF16),
                   jax.ShapeDtypeStruct((FOX_HEADS, s_len, HEAD_DIM), BF16), tiled, tiled],
        scratch_shapes=[pltpu.VMEM((1, 128), F32)],
        compiler_params=_params(("arbitrary",)),
    )(proj, proj, proj, proj, bf_pad, gq, gk, seg, to_q, to_k)


def _mem_prep(mem, g, w_kv, gk):
    def body(mem_ref, g_ref, w_ref, gk_ref, km_ref, vm_ref):
        m = mem_ref[...]
        kv = _dot((m * _rms(m) * g_ref[...]).astype(BF16), w_ref[...])
        for h in range(MEM_HEADS):
            sl = slice(HEAD_DIM * h, HEAD_DIM * (h + 1))
            kh = kv[:, sl]
            km_ref[:, sl] = (kh * _rms(kh) * gk_ref[...]).astype(BF16)
        vm_ref[...] = kv[:, 256:512].astype(BF16)

    return pl.pallas_call(
        body,
        name="mem_prep",
        out_shape=(jax.ShapeDtypeStruct((N_MEM, 256), BF16),) * 2,
        in_specs=[pl.BlockSpec(memory_space=pltpu.VMEM)] * 4,
        out_specs=(pl.BlockSpec(memory_space=pltpu.VMEM),) * 2,
        compiler_params=pltpu.CompilerParams(vmem_limit_bytes=VMEM_LIMIT),
    )(mem, g, w_kv, gk)


def _causal_mask():
    row = lax.broadcasted_iota(jnp.int32, (TA, TA), 0)
    col = lax.broadcasted_iota(jnp.int32, (TA, TA), 1)
    return col <= row


def _causal_mask_t():
    row = lax.broadcasted_iota(jnp.int32, (TA, TA), 0)
    col = lax.broadcasted_iota(jnp.int32, (TA, TA), 1)
    return row <= col


def _fox_fwd(q_aug, k_aug, vt_aug, w_kv16, w_out16):
    s_len = q_aug.shape[1]
    n_tiles = s_len // TA
    hb = HB_FWD
    n_groups = FOX_HEADS // hb

    def body(q_ref, k_new, vt_new, kv16_ref, out16_ref, o_ref, st_ref, kv_all, out_all, k_ref, vt_ref, send_sems, recv_sems, local_sems):
        qi = pl.program_id(1)
        for h in range(hb):
            k_ref[h, pl.ds(pl.multiple_of(qi * TA, TA), TA), :] = k_new[h]
            vt_ref[h, qi] = vt_new[h, 0]
        remote, local = _row_block_exchange(kv16_ref, out16_ref, kv_all, out_all, send_sems, recv_sems, local_sems, gather=True)

        @pl.when((pl.program_id(0) == 0) & (qi == 0))
        def _():
            for cp in remote + local:
                cp.start()

        qs = [q_ref[h] for h in range(hb)]

        def step(t0, carry, masked, width):
            rows = pl.ds(pl.multiple_of(t0 * TA, TA), width * TA)
            scores = [_dot_nt(k_ref[h, rows, :], qs[h]) for h in range(hb)]
            soft = []
            for h in range(hb):
                m, _ = carry[h]
                s = jnp.where(_causal_mask_t(), scores[h], -1e30) if masked else scores[h]
                m_new = jnp.maximum(m, jnp.max(s, axis=0, keepdims=True))
                soft.append((m_new, jnp.exp(m - m_new), jnp.exp(s - m_new).astype(BF16)))
            out = []
            for h, (m_new, alpha, p) in enumerate(soft):
                acc = alpha * carry[h][1]
                for t in range(width):
                    acc = acc + _dot(vt_ref[h, t0 + t, 0:AUG_ROWS, :], p[t * TA:(t + 1) * TA])
                out.append((m_new, acc))
            return tuple(out)

        init = tuple((jnp.full((1, TA), -1e30, F32), jnp.zeros((AUG_ROWS, TA), F32)) for _ in range(hb))
        carry = lax.fori_loop(0, qi // 2, lambda j, cr: step(2 * j, cr, False, 2), init)
        carry = lax.fori_loop(2 * (qi // 2), qi, lambda j, cr: step(j, cr, False, 1), carry)
        carry = step(qi, carry, True, 1)
        for h in range(hb):
            m, acc = carry[h]
            l = acc[HEAD_DIM:HEAD_DIM + 1, :]
            lse = m + jnp.log(l)
            o_ref[h] = jnp.concatenate([acc[0:HEAD_DIM] / l, jnp.broadcast_to(lse, (HEAD_DIM, TA))], axis=0).T
            st_ref[h, 0] = jnp.broadcast_to(lse, (8, TA))

        @pl.when((pl.program_id(0) == n_groups - 1) & (qi == n_tiles - 1))
        def _():
            for cp in remote:
                cp.wait_recv()
            for cp in remote:
                cp.wait_send()
            for cp in local:
                cp.wait()

    hbm = pl.BlockSpec(memory_space=pl.ANY)
    return pl.pallas_call(
        body,
        name="fox_fwd",
        grid=(n_groups, n_tiles),
        in_specs=[pl.BlockSpec((hb, TA, 128), lambda g, i: (g, i, 0)), pl.BlockSpec((hb, TA, 128), lambda g, i: (g, i, 0)),
                  pl.BlockSpec((hb, 1, 128, TA), lambda g, i: (g, i, 0, 0)), hbm, hbm],
        out_specs=[pl.BlockSpec((hb, TA, 128), lambda g, i: (g, i, 0)), pl.BlockSpec((hb, 1, 8, TA), lambda g, i: (g, i, 0, 0)),
                   hbm, hbm],
        out_shape=[jax.ShapeDtypeStruct((FOX_HEADS, s_len, 128), F32), jax.ShapeDtypeStruct((FOX_HEADS, n_tiles, 8, TA), F32),
                   jax.ShapeDtypeStruct((D_MODEL, 512), BF16), jax.ShapeDtypeStruct((D_MODEL, D_MODEL), BF16)],
        scratch_shapes=[pltpu.VMEM((hb, s_len, 128), BF16), pltpu.VMEM((hb, n_tiles, 128, TA), BF16),
                        pltpu.SemaphoreType.DMA((14,)), pltpu.SemaphoreType.DMA((14,)), pltpu.SemaphoreType.DMA((2,))],
        compiler_params=_params(("arbitrary", "arbitrary")),
    )(q_aug, k_aug, vt_aug, w_kv16, w_out16)


def _lane_group(lane, a, b, c, d):
    return jnp.where(lane < 64, a, jnp.where(lane < 128, b, jnp.where(lane < 192, c, d)))


def _pool_count(row0):
    lane = lax.broadcasted_iota(jnp.int32, (TM, POOL_WIDTH), 1)
    t1 = row0 + lax.broadcasted_iota(jnp.int32, (TM, POOL_WIDTH), 0) + 1
    return 1.0 / jnp.minimum(t1, _lane_group(lane, 2, 4, 8, 16)).astype(F32), lane


def _pool_delta(u, halo, cnt, lane):
    xe = jnp.concatenate([halo, u], axis=0)
    s2 = xe + pltpu.roll(xe, 1, 0)
    s4 = s2 + pltpu.roll(s2, 2, 0)
    s8 = s4 + pltpu.roll(s4, 4, 0)
    s16 = s8 + pltpu.roll(s8, 8, 0)
    win = _lane_group(lane, s2[HALO:], s4[HALO:], s8[HALO:], s16[HALO:])
    return win * cnt - u


def _pool_delta_bwd(dd, dd_next, cnt, cnt_next, lane):
    ee = jnp.concatenate([dd * cnt, dd_next * cnt_next], axis=0)
    n = TM + HALO
    r2 = ee + pltpu.roll(ee, n - 1, 0)
    r4 = r2 + pltpu.roll(r2, n - 2, 0)
    r8 = r4 + pltpu.roll(r4, n - 4, 0)
    r16 = r8 + pltpu.roll(r8, n - 8, 0)
    return _lane_group(lane, r2[:TM], r4[:TM], r8[:TM], r16[:TM]) - dd


def _mem_attn(qm, gq, seg, km_ref, vm_ref):
    r = _head_rms(qm, seg)
    a = qm * r
    q16 = (a * gq * 0.125).astype(BF16)
    heads = []
    for h in range(MEM_HEADS):
        sl = slice(HEAD_DIM * h, HEAD_DIM * (h + 1))
        s = _dot_nt(q16[:, sl], km_ref[:, sl])
        e = jnp.exp(s - jnp.max(s, axis=-1, keepdims=True))
        p = e * (1.0 / jnp.sum(e, axis=-1, keepdims=True))
        heads.append((p, _dot(p.astype(BF16), vm_ref[:, sl])))
    return a, r, q16, heads


def _row_specs(s_len):
    n_halo = s_len // HALO
    piece = lambda j: pl.BlockSpec((TM, PIECE), lambda i: (i, j))
    before = lambda j: pl.BlockSpec((HALO, 256), lambda i: (jnp.maximum(i * (TM // HALO) - 1, 0), j))
    after = lambda j: pl.BlockSpec((HALO, 256), lambda i: (jnp.minimum((i + 1) * (TM // HALO), n_halo - 1), j))
    return piece, before, after


def _mix_fwd(proj, olse, km, vm, wp_bd, pool_scale, gq_m, seg):
    s_len = proj.shape[0]

    def body(ua_ref, halo_ref, gb_ref, qm_ref, ol_ref, km_ref, vm_ref, wp_ref, sc_ref, gq_ref, seg_ref, out_ref):
        i = pl.program_id(0)
        u = ua_ref[:, 0:256]
        g_a = ua_ref[:, 256:512]
        halo = jnp.where(i > 0, halo_ref[...], 0.0)
        cnt, lane = _pool_count(i * TM)
        d = _pool_delta(u, halo, cnt, lane).astype(BF16)
        y_a = _dot(d, wp_ref[...]) * sc_ref[...]
        out_ref[:, 0:256] = (y_a * (g_a * _sigmoid(g_a))).astype(BF16)
        g_b = gb_ref[...]
        y_b = jnp.concatenate([ol_ref[h][:, 0:HEAD_DIM] for h in range(FOX_HEADS)], axis=1)
        out_ref[:, 256:768] = (y_b * (g_b * _sigmoid(g_b))).astype(BF16)
        _, _, _, heads = _mem_attn(qm_ref[:, 0:256], gq_ref[...], seg_ref[0:256, 0:256], km_ref, vm_ref)
        g_m = qm_ref[:, 256:512]
        y_m = jnp.concatenate([y for _, y in heads], axis=1)
        out_ref[:, 768:1024] = (y_m * (g_m * _sigmoid(g_m))).astype(BF16)

    piece, before, _ = _row_specs(s_len)
    return pl.pallas_call(
        body,
        name="mix_fwd",
        grid=(s_len // TM,),
        in_specs=[piece(0), before(0), piece(4), piece(5), pl.BlockSpec((FOX_HEADS, TM, 128), lambda i: (0, i, 0)),
                  _full((N_MEM, 256)), _full((N_MEM, 256)), _full((256, 256)), _full((1, 256)), _full((1, 256)),
                  _full((PIECE, PIECE))],
        out_specs=pl.BlockSpec((TM, D_MODEL), lambda i: (i, 0)),
        out_shape=jax.ShapeDtypeStruct((s_len, D_MODEL), BF16),
        compiler_params=_params(("parallel",)),
    )(proj, proj, proj, proj, olse, km, vm, wp_bd, pool_scale, gq_m, seg)


def _out_loss(mixed, x, target, w_out):
    s_len = x.shape[0]

    def body(mix_ref, x_ref, t_ref, w_ref, dout_ref, dmix_ref, dw16_ref, loss_ref, dw_ref):
        @pl.when(pl.program_id(0) == 0)
        def _():
            dw_ref[...] = jnp.zeros((D_MODEL, D_MODEL), F32)
            loss_ref[...] = jnp.zeros((8, 128), F32)

        mixed16 = mix_ref[...]
        err = x_ref[...] + _dot(mixed16, w_ref[...]) - t_ref[...]
        loss_ref[...] += 0.5 * jnp.sum(jnp.mean(err * err, axis=-1, keepdims=True))
        d_out = err / float(D_MODEL)
        dout_ref[...] = d_out
        d16 = d_out.astype(BF16)
        dmix_ref[...] = _dot_nt(d16, w_ref[...])
        dw_ref[...] += _dot_tn(mixed16, d16)

        @pl.when(pl.program_id(0) == pl.num_programs(0) - 1)
        def _():
            dw16_ref[...] = dw_ref[...].astype(BF16)

    row = pl.BlockSpec((TMM, D_MODEL), lambda i: (i, 0))
    return pl.pallas_call(
        body,
        name="out_loss",
        grid=(s_len // TMM,),
        in_specs=[row, row, row, _full((D_MODEL, D_MODEL))],
        out_specs=[row, row, _full((D_MODEL, D_MODEL)), _full((8, 128))],
        out_shape=[jax.ShapeDtypeStruct((s_len, D_MODEL), F32), jax.ShapeDtypeStruct((s_len, D_MODEL), F32),
                   jax.ShapeDtypeStruct((D_MODEL, D_MODEL), BF16), jax.ShapeDtypeStruct((8, 128), F32)],
        scratch_shapes=[pltpu.VMEM((D_MODEL, D_MODEL), F32)],
        compiler_params=_params(("arbitrary",)),
    )(mixed, x, target, w_out)


def _silu_pair(g):
    s = _sigmoid(g)
    return g * s, s * (1.0 + g * (1.0 - s))


def _mix_bwd(proj, olse, d_mixed, km, vm, wp_bd, pool_scale, gq_m, seg):
    s_len = proj.shape[0]
    n_tiles = s_len // TM

    def body(ua_ref, halo_ref, ga_next_ref, gb_ref, qm_ref, ol_ref, dm_ref, dm_next_ref, km_ref, vm_ref, wp_ref, sc_ref, gq_ref,
             seg_ref, dua_ref, dgb_ref, dqm_ref, do_ref, dl_ref, dwp_ref, dsc_ref, dkm_ref, dvm_ref, dgq_ref):
        i = pl.program_id(0)

        @pl.when(i == 0)
        def _():
            dwp_ref[...] = jnp.zeros((256, 256), F32)
            dsc_ref[...] = jnp.zeros((1, 256), F32)
            dkm_ref[...] = jnp.zeros((N_MEM, 256), F32)
            dvm_ref[...] = jnp.zeros((N_MEM, 256), F32)
            dgq_ref[...] = jnp.zeros((1, HEAD_DIM), F32)

        u = ua_ref[:, 0:256]
        g_a = ua_ref[:, 256:512]
        halo = jnp.where(i > 0, halo_ref[...], 0.0)
        cnt, lane = _pool_count(i * TM)
        d16 = _pool_delta(u, halo, cnt, lane).astype(BF16)
        t = _dot(d16, wp_ref[...])
        y_a = t * sc_ref[...]
        silu_a, dsilu_a = _silu_pair(g_a)
        dm_a = dm_ref[:, 0:256]
        dy_a = dm_a * silu_a
        dsc_ref[...] += jnp.sum(dy_a * t, axis=0, keepdims=True)
        dt16 = (dy_a * sc_ref[...]).astype(BF16)
        dwp_ref[...] += _dot_tn(d16, dt16)
        dd = _dot_nt(dt16, wp_ref[...])
        g_next = ga_next_ref[...]
        dt_next = (dm_next_ref[...] * (g_next * _sigmoid(g_next)) * sc_ref[...]).astype(BF16)
        dd_next = jnp.where(i < n_tiles - 1, _dot_nt(dt_next, wp_ref[...]), 0.0)
        cnt_next = _pool_count((i + 1) * TM)[0][0:HALO]
        dua_ref[:, 0:256] = _pool_delta_bwd(dd, dd_next, cnt, cnt_next, lane).astype(BF16)
        dua_ref[:, 256:512] = (dm_a * y_a * dsilu_a).astype(BF16)

        silu_b, dsilu_b = _silu_pair(gb_ref[...])
        dm_b = dm_ref[:, 256:768]
        o_all = jnp.concatenate([ol_ref[h][:, 0:HEAD_DIM] for h in range(FOX_HEADS)], axis=1)
        d_o = dm_b * silu_b
        dgb_ref[...] = (dm_b * o_all * dsilu_b).astype(BF16)
        for h in range(FOX_HEADS):
            do_ref[h] = d_o[:, HEAD_DIM * h:HEAD_DIM * (h + 1)].astype(BF16)
        prod = d_o * o_all
        hi = prod.astype(BF16)
        lo = (prod - hi.astype(F32)).astype(BF16)
        head_of_lane = lax.broadcasted_iota(jnp.int32, (FOX_HEADS, PIECE), 1) // HEAD_DIM
        pick = jnp.where(head_of_lane == lax.broadcasted_iota(jnp.int32, (FOX_HEADS, PIECE), 0), 1.0, 0.0).astype(BF16)
        delta = _dot_nt(pick, hi) + _dot_nt(pick, lo)
        for h in range(FOX_HEADS):
            dl_ref[h, 0] = jnp.broadcast_to(delta[h:h + 1, :], (8, TM))

        seg = seg_ref[0:256, 0:256]
        a, r, q16, heads = _mem_attn(qm_ref[:, 0:256], gq_ref[...], seg, km_ref, vm_ref)
        silu_m, dsilu_m = _silu_pair(qm_ref[:, 256:512])
        dm_m = dm_ref[:, 768:1024]
        y_m = jnp.concatenate([y for _, y in heads], axis=1)
        dqm_ref[:, 256:512] = (dm_m * y_m * dsilu_m).astype(BF16)
        dy16_all = (dm_m * silu_m).astype(BF16)
        d_scores = []
        for h in range(MEM_HEADS):
            sl = slice(HEAD_DIM * h, HEAD_DIM * (h + 1))
            p = heads[h][0]
            dy16 = dy16_all[:, sl]
            dp = _dot_nt(dy16, vm_ref[:, sl])
            dvm_ref[:, sl] += _dot_tn(p.astype(BF16), dy16)
            ds16 = (p * (dp - jnp.sum(dp * p, axis=-1, keepdims=True))).astype(BF16)
            dkm_ref[:, sl] += _dot_tn(ds16, q16[:, sl])
            d_scores.append(_dot(ds16, km_ref[:, sl]))
        dq, dg_rows = _head_rms_bwd(a, r, gq_ref[...], jnp.concatenate(d_scores, axis=1) * 0.125, seg)
        dqm_ref[:, 0:256] = dq.astype(BF16)
        dgq_ref[...] += _fold_heads(jnp.sum(dg_rows, axis=0, keepdims=True), MEM_HEADS)

    piece, before, after = _row_specs(s_len)
    n_halo = s_len // HALO
    row = pl.BlockSpec((TM, D_MODEL), lambda i: (i, 0))
    dm_after = pl.BlockSpec((HALO, 256), lambda i: (jnp.minimum((i + 1) * (TM // HALO), n_halo - 1), 0))
    out_piece = pl.BlockSpec((TM, PIECE), lambda i: (i, 0))
    return pl.pallas_call(
        body,
        name="mix_bwd",
        grid=(n_tiles,),
        in_specs=[piece(0), before(0), after(1), piece(4), piece(5), pl.BlockSpec((FOX_HEADS, TM, 128), lambda i: (0, i, 0)),
                  row, dm_after, _full((N_MEM, 256)), _full((N_MEM, 256)), _full((256, 256)), _full((1, 256)), _full((1, 256)),
                  _full((PIECE, PIECE))],
        out_specs=[out_piece, out_piece, out_piece, pl.BlockSpec((FOX_HEADS, TM, HEAD_DIM), lambda i: (0, i, 0)),
                   pl.BlockSpec((FOX_HEADS, 1, 8, TM), lambda i: (0, i, 0, 0)),
                   _full((256, 256)), _full((1, 256)), _full((N_MEM, 256)), _full((N_MEM, 256)), _full((1, HEAD_DIM))],
        out_shape=[jax.ShapeDtypeStruct((s_len, PIECE), BF16)] * 3 + [
            jax.ShapeDtypeStruct((FOX_HEADS, s_len, HEAD_DIM), BF16),
            jax.ShapeDtypeStruct((FOX_HEADS, n_tiles, 8, TM), F32),
            jax.ShapeDtypeStruct((256, 256), F32), jax.ShapeDtypeStruct((1, 256), F32),
            jax.ShapeDtypeStruct((N_MEM, 256), F32), jax.ShapeDtypeStruct((N_MEM, 256), F32),
            jax.ShapeDtypeStruct((1, HEAD_DIM), F32)],
        compiler_params=_params(("arbitrary",)),
    )(proj, proj, proj, proj, proj, olse, d_mixed, d_mixed, km, vm, wp_bd, pool_scale, gq_m, seg)


def _fox_bwd(q_aug, k_aug, kt_aug, v_h, d_o, lse_rows, delta_rows, dw_kv16, dw_out16):
    s_len = q_aug.shape[1]
    n_tiles = s_len // TA
    n_groups = FOX_HEADS // HB

    def body(q_ref, k_ref, kt_ref, v_ref, do_ref, st_ref, dl_ref, dkv16_ref, dout16_ref, dq_ref, dk_ref, dv_ref, land_kv, land_out,
             dqt_ref, send_sems, recv_sems, local_sems):
        j = pl.program_id(1)
        remote, local = _row_block_exchange(dkv16_ref, dout16_ref, land_kv, land_out, send_sems, recv_sems, local_sems, gather=False)

        @pl.when((pl.program_id(0) == 0) & (j == 0))
        def _():
            for cp in remote + local:
                cp.start()

        @pl.when(j == 0)
        def _():
            dqt_ref[...] = jnp.zeros((HB, n_tiles, AUG_ROWS, TA), F32)

        ks = [k_ref[h] for h in range(HB)]
        kts = [kt_ref[h, 0, 0:AUG_ROWS, :] for h in range(HB)]
        vs = [v_ref[h] for h in range(HB)]

        def pair(i0, acc, masked, width):
            rows = pl.ds(pl.multiple_of(i0 * TA, TA), width * TA)
            qs = [q_ref[h, rows, :] for h in range(HB)]
            d_os = [do_ref[h, rows, :] for h in range(HB)]
            row_stat = lambda ref, h: jnp.concatenate([ref[h, i0 + t, 0:1, :] for t in range(width)], axis=1)
            scores = [(_dot_nt(ks[h], qs[h]), _dot_nt(vs[h], d_os[h])) for h in range(HB)]
            probs = []
            for h in range(HB):
                s, dp = scores[h]
                if masked:
                    s = jnp.where(_causal_mask_t(), s, -1e30)
                p = jnp.exp(s - row_stat(st_ref, h))
                probs.append((p.astype(BF16), (p * (dp - row_stat(dl_ref, h))).astype(BF16)))
            out = []
            for h in range(HB):
                p16, ds16 = probs[h]
                dqt = _dot(kts[h], ds16)
                for t in range(width):
                    dqt_ref[h, i0 + t] += dqt[:, t * TA:(t + 1) * TA]
                out.append((acc[h][0] + _dot(ds16, qs[h]), acc[h][1] + _dot(p16, d_os[h])))
            return tuple(out)

        zero = tuple((jnp.zeros((TA, 128), F32), jnp.zeros((TA, HEAD_DIM), F32)) for _ in range(HB))
        acc = pair(j, zero, True, 1)
        odd = (n_tiles - 1 - j) % 2
        acc = lax.fori_loop(j + 1, j + 1 + odd, lambda i, a: pair(i, a, False, 1), acc)
        acc = lax.fori_loop(0, (n_tiles - 1 - j) // 2, lambda t, a: pair(j + 1 + odd + 2 * t, a, False, 2), acc)
        pad = jnp.zeros((128 - AUG_ROWS, TA), F32)
        for h in range(HB):
            dk_ref[h] = acc[h][0]
            dv_ref[h] = acc[h][1]
            dq_ref[h] = jnp.concatenate([dqt_ref[h, j], pad], axis=0).T

        @pl.when((pl.program_id(0) == n_groups - 1) & (j == n_tiles - 1))
        def _():
            for cp in remote:
                cp.wait_recv()
            for cp in remote:
                cp.wait_send()
            for cp in local:
                cp.wait()

    assert n_groups == 1
    resident = pl.BlockSpec(memory_space=pltpu.VMEM)
    whole = lambda w: resident
    rows_blk = lambda r: resident
    tile = lambda w: pl.BlockSpec((HB, TA, w), lambda g, j: (g, j, 0))
    hbm = pl.BlockSpec(memory_space=pl.ANY)
    return pl.pallas_call(
        body,
        name="fox_bwd",
        grid=(n_groups, n_tiles),
        in_specs=[whole(128), tile(128), pl.BlockSpec((HB, 1, 128, TA), lambda g, j: (g, j, 0, 0)), tile(HEAD_DIM),
                  whole(HEAD_DIM), rows_blk(8), rows_blk(8), hbm, hbm],
        out_specs=[tile(128), tile(128), tile(HEAD_DIM), hbm, hbm],
        out_shape=[jax.ShapeDtypeStruct((FOX_HEADS, s_len, 128), F32), jax.ShapeDtypeStruct((FOX_HEADS, s_len, 128), F32),
                   jax.ShapeDtypeStruct((FOX_HEADS, s_len, HEAD_DIM), F32),
                   jax.ShapeDtypeStruct((N_DEV, 128, 512), BF16), jax.ShapeDtypeStruct((N_DEV, 128, D_MODEL), BF16)],
        scratch_shapes=[pltpu.VMEM((HB, n_tiles, AUG_ROWS, TA), F32),
                        pltpu.SemaphoreType.DMA((14,)), pltpu.SemaphoreType.DMA((14,)), pltpu.SemaphoreType.DMA((2,))],
        compiler_params=_params(("arbitrary", "arbitrary")),
    )(q_aug, k_aug, kt_aug, v_h, d_o, lse_rows, delta_rows, dw_kv16, dw_out16)


def _fox_post(proj, bf_pad, gq, gk, seg, dq_aug, dk_aug, dv_h):
    s_len = proj.shape[0]
    n_tiles = s_len // TM

    def body(q_ref, k_ref, f_ref, bf_ref, gq_ref, gk_ref, seg_ref, dqa_ref, dka_ref, dvh_ref,
             dq_ref, dk_ref, dv_ref, df_ref, dgq_ref, dgk_ref, dbf_ref, carry_ref):
        @pl.when(pl.program_id(0) == 0)
        def _():
            carry_ref[...] = jnp.zeros((1, 128), F32)
            dgq_ref[...] = jnp.zeros((1, HEAD_DIM), F32)
            dgk_ref[...] = jnp.zeros((1, HEAD_DIM), F32)
            dbf_ref[...] = jnp.zeros((1, 128), F32)

        lane = lax.broadcasted_iota(jnp.int32, (TM, 128), 1)
        seg = seg_ref[...]
        dqas = [dqa_ref[h] for h in range(FOX_HEADS)]
        dkas = [dka_ref[h] for h in range(FOX_HEADS)]
        d_cum = jnp.zeros((TM, 128), F32)
        for h in range(FOX_HEADS):
            d_cum = jnp.where(lane == h, dqas[h][:, 64:65] - dkas[h][:, 67:68], d_cum)
        q_all = q_ref[...]
        k_all = k_ref[...]
        rq = _head_rms(q_all, seg)
        rk = _head_rms(k_all, seg)
        dy_q = jnp.concatenate([t[:, 0:HEAD_DIM] for t in dqas], axis=1) * 0.125
        dy_k = jnp.concatenate([t[:, 0:HEAD_DIM] for t in dkas], axis=1)
        dq, dgq_rows = _head_rms_bwd(q_all * rq, rq, gq_ref[...], dy_q, seg)
        dk, dgk_rows = _head_rms_bwd(k_all * rk, rk, gk_ref[...], dy_k, seg)
        dq_ref[...] = dq.astype(BF16)
        dk_ref[...] = dk.astype(BF16)
        dv_ref[...] = jnp.concatenate([dvh_ref[h] for h in range(FOX_HEADS)], axis=1).astype(BF16)
        dgq_ref[...] += _fold_heads(jnp.sum(dgq_rows, axis=0, keepdims=True), FOX_HEADS)
        dgk_ref[...] += _fold_heads(jnp.sum(dgk_rows, axis=0, keepdims=True), FOX_HEADS)

        row = lax.broadcasted_iota(jnp.int32, (TM, TM), 0)
        col = lax.broadcasted_iota(jnp.int32, (TM, TM), 1)
        tri = jnp.where(col >= row, 1.0, 0.0).astype(BF16)
        hi, mid, lo = _split3(d_cum)
        d_logf = _dot(tri, hi) + _dot(tri, mid) + _dot(tri, lo) + carry_ref[...]
        carry_ref[...] = d_logf[0:1, :]
        z = f_ref[...] + bf_ref[...]
        d_f = jnp.where(lane < FOX_HEADS, d_logf / (1.0 + jnp.exp(z)), 0.0)
        df_ref[...] = d_f.astype(BF16)
        dbf_ref[...] += jnp.sum(d_f, axis=0, keepdims=True)

    rev = lambda i: n_tiles - 1 - i
    col_blk = lambda j: pl.BlockSpec((TM, PIECE), lambda i: (rev(i), j))
    head_blk = lambda w: pl.BlockSpec((FOX_HEADS, TM, w), lambda i: (0, rev(i), 0))
    out_piece = pl.BlockSpec((TM, PIECE), lambda i: (rev(i), 0))
    return pl.pallas_call(
        body,
        name="fox_post",
        grid=(n_tiles,),
        in_specs=[col_blk(1), col_blk(2), pl.BlockSpec((TM, 128), lambda i: (rev(i), MY_F_OFF // 128)),
                  _full((1, 128)), _full((1, PIECE)), _full((1, PIECE)), _full((PIECE, PIECE)),
                  head_blk(128), head_blk(128), head_blk(HEAD_DIM)],
        out_specs=[out_piece, out_piece, out_piece, pl.BlockSpec((TM, 128), lambda i: (rev(i), 0)),
                   _full((1, HEAD_DIM)), _full((1, HEAD_DIM)), _full((1, 128))],
        out_shape=[jax.ShapeDtypeStruct((s_len, PIECE), BF16)] * 3 + [
            jax.ShapeDtypeStruct((s_len, 128), BF16), jax.ShapeDtypeStruct((1, HEAD_DIM), F32),
            jax.ShapeDtypeStruct((1, HEAD_DIM), F32), jax.ShapeDtypeStruct((1, 128), F32)],
        scratch_shapes=[pltpu.VMEM((1, 128), F32)],
        compiler_params=_params(("arbitrary",)),
    )(proj, proj, proj, bf_pad, gq, gk, seg, dq_aug, dk_aug, dv_h)


def _mem_bwd(mem, g, w_kv, gk, dkm, dvm):
    def body(mem_ref, g_ref, w_ref, gk_ref, dkm_ref, dvm_ref, dw_ref, dg_ref, dgk_ref, dkv_ref):
        m = mem_ref[...]
        r = _rms(m)
        a = m * r
        mn16 = (a * g_ref[...]).astype(BF16)
        kv = _dot(mn16, w_ref[...])
        dgk = jnp.zeros((N_MEM, HEAD_DIM), F32)
        for h in range(MEM_HEADS):
            sl = slice(HEAD_DIM * h, HEAD_DIM * (h + 1))
            kh = kv[:, sl]
            rk = _rms(kh)
            dk, dg_rows = _rms_bwd(kh * rk, rk, gk_ref[...], dkm_ref[:, sl])
            dkv_ref[:, sl] = dk.astype(BF16)
            dgk = dgk + dg_rows
        dkv_ref[:, 256:512] = dvm_ref[...].astype(BF16)
        dgk_ref[...] = jnp.sum(dgk, axis=0, keepdims=True)
        dkv16 = dkv_ref[...]
        dw_ref[...] = _dot_tn(mn16, dkv16).astype(BF16)
        dg_ref[...] = jnp.sum(_dot_nt(dkv16, w_ref[...]) * a, axis=0, keepdims=True)

    return pl.pallas_call(
        body,
        name="mem_bwd",
        out_shape=(jax.ShapeDtypeStruct((D_MODEL, 512), BF16), jax.ShapeDtypeStruct((1, D_MODEL), F32),
                   jax.ShapeDtypeStruct((1, HEAD_DIM), F32)),
        in_specs=[pl.BlockSpec(memory_space=pltpu.VMEM)] * 6,
        out_specs=(pl.BlockSpec(memory_space=pltpu.VMEM),) * 3,
        scratch_shapes=[pltpu.VMEM((N_MEM, 512), BF16)],
        compiler_params=pltpu.CompilerParams(vmem_limit_bytes=VMEM_LIMIT),
    )(mem, g, w_kv, gk, dkm, dvm)


def _grad_x_exchange(pieces, d_f, w_my, x, norm_g, d_out, dw_in, land_kv, land_out, d_mem_g, d_scale, d_bf, d_gq, d_gk, d_gqm,
                     d_gkm, dwp_bd, loss_blk):
    s_len = x.shape[0]
    n_steps = s_len // TM
    step2, step3 = n_steps // 4, (11 * n_steps) // 16
    half = D_MODEL // 2

    def body(p0, p1, p2, p3, p4, p5, df_ref, x_ref, dout_ref, w_ref, g_ref, in_ref, lkv_ref, lout_ref,
             dmg, dsc, dbf, dgq, dgk, dgqm, dgkm, dwp, loss_ref,
             gx_ref, rin_ref, rkv_ref, rout_ref, sred_ref,
             dng, land1, send2a, send2b, keep2a, keep2b, land2a, land2b, send3a, send3b, land3a, land3b, sb_ref, sland,
             own4, lkv_v, lout_v, send_sems, recv_sems, load_sems):
        i = pl.program_id(0)
        x_, y_, c_, me = _my_place()
        sibling, x_nbr, y_nbr = (x_, y_, 1 - c_), (1 - x_, y_, c_), (x_, 1 - y_, c_)
        loads = [pltpu.make_async_copy(in_ref.at[2 * k + c_], own4.at[k], load_sems.at[k]) for k in range(4)]
        loads += [pltpu.make_async_copy(lkv_ref, lkv_v, load_sems.at[4]), pltpu.make_async_copy(lout_ref, lout_v, load_sems.at[5])]

        def rcopy(src, dst, sem, to):
            return pltpu.make_async_remote_copy(src_ref=src, dst_ref=dst, send_sem=send_sems.at[sem], recv_sem=recv_sems.at[sem],
                                                device_id=to, device_id_type=MESH)

        early_rows, late_rows = pl.ds(8, SB_ROWS - 8), pl.ds(0, 8)
        small_early, small_late = [], []
        for k in range(1, N_DEV):
            peer = (x_ ^ (k >> 2), y_ ^ ((k >> 1) & 1), c_ ^ (k & 1))
            small_early.append(rcopy(sb_ref.at[early_rows], sland.at[me, early_rows], k - 1, peer))
            small_late.append(rcopy(sb_ref.at[late_rows], sland.at[me, late_rows], 16 + k, peer))
        small = small_early + small_late
        stage1 = [rcopy(in_ref.at[2 * k + 1 - c_], land1.at[k], 7 + k, sibling) for k in range(4)]
        stage2 = [rcopy(send2a.at[j], land2a.at[j], 11 + j, x_nbr) for j in range(2)]
        stage2 += [rcopy(send2b.at[j], land2b.at[j], 13 + j, y_nbr) for j in range(2)]
        stage3 = [rcopy(send3a, land3a, 15, y_nbr), rcopy(send3b, land3b, 16, x_nbr)]
        top, bot = slice(0, half), slice(half, D_MODEL)

        @pl.when(i == 0)
        def _():
            dng[...] = jnp.zeros((1, D_MODEL), F32)
            for cp in stage1 + loads:
                cp.start()
            sb_ref[...] = jnp.zeros((SB_ROWS, SB_W), F32)
            sb_ref[SB_POOL_SCALE:SB_POOL_SCALE + 1, :] = dsc[...]
            sb_ref[SB_B_F:SB_B_F + 1, 0:128] = dbf[...]
            for row, ref in ((SB_FOX_Q, dgq), (SB_FOX_K, dgk), (SB_MEM_Q, dgqm), (SB_MEM_K, dgkm)):
                sb_ref[row:row + 1, 0:HEAD_DIM] = ref[...]
            sb_ref[SB_LOSS:SB_LOSS + 1, 0:128] = loss_ref[0:1, :]
            for grp in range(4):
                sl = slice(64 * grp, 64 * (grp + 1))
                sb_ref[SB_W_POOL:SB_W_POOL + 64, sl] = dwp[sl, sl]
            for cp in small_early:
                cp.start()

        @pl.when(i == step2)
        def _():
            for cp in stage1:
                cp.wait_recv()
            for cp in loads[0:4]:
                cp.wait()

            def chip_sum(k, cols):
                return own4[k, :, cols].astype(F32) + land1[k, :, cols].astype(F32)

            for j in range(2):
                send2a[j] = chip_sum(2 * (1 - x_) + j, top).astype(BF16)
                keep2a[j] = chip_sum(2 * x_ + j, top)
                send2b[j] = chip_sum(2 * j + 1 - y_, bot).astype(BF16)
                keep2b[j] = chip_sum(2 * j + y_, bot)
            for cp in stage2:
                cp.start()

        @pl.when(i == step3)
        def _():
            for cp in stage2:
                cp.wait_recv()
            for j in range(2):
                keep2a[j] = keep2a[j] + land2a[j].astype(F32)
                keep2b[j] = keep2b[j] + land2b[j].astype(F32)
            send3a[...] = keep2a[1 - y_].astype(BF16)
            send3b[...] = keep2b[1 - x_].astype(BF16)
            for cp in stage3:
                cp.start()

        dh = _dot_nt(df_ref[...], w_ref[:, MY_F_OFF:MY_WIDTH])
        for j, p in enumerate((p0, p1, p2, p3, p4, p5)):
            dh = dh + _dot_nt(p[...], w_ref[:, PIECE * j:PIECE * (j + 1)])
        xv = x_ref[...]
        r = _rms(xv)
        dx, dg_rows = _rms_bwd(xv * r, r, g_ref[...], dh)
        gx_ref[...] = dout_ref[...] + dx
        dng[...] += jnp.sum(dg_rows, axis=0, keepdims=True)

        @pl.when(i == n_steps - 1)
        def _():
            for k in range(4):
                sb_ref[SB_NORM_G + k:SB_NORM_G + k + 1, :] = dng[:, SB_W * k:SB_W * (k + 1)]
                sb_ref[SB_MEM_NORM_G + k:SB_MEM_NORM_G + k + 1, :] = dmg[:, SB_W * k:SB_W * (k + 1)]
            for cp in small_late:
                cp.start()
            sland[me] = sb_ref[...]
            for cp in stage3:
                cp.wait_recv()
            rin_ref[:, top] = keep2a[y_] + land3a[...].astype(F32)
            rin_ref[:, bot] = keep2b[x_] + land3b[...].astype(F32)
            for cp in small:
                cp.wait_recv()
            for cp in small + stage1 + stage2 + stage3:
                cp.wait_send()
            for cp in loads[4:6]:
                cp.wait()
            for land, red in ((lkv_v, rkv_ref), (lout_v, rout_ref), (sland, sred_ref)):
                acc = land[0].astype(F32)
                for d in range(1, N_DEV):
                    acc = acc + land[d].astype(F32)
                red[...] = acc

    piece = pl.BlockSpec((TM, PIECE), lambda i: (i, 0))
    row = pl.BlockSpec((TM, D_MODEL), lambda i: (i, 0))
    vmem = pl.BlockSpec(memory_space=pltpu.VMEM)
    half_chunk = lambda n, dt: pltpu.VMEM((n, CHUNK_ROWS, half), dt)
    whole = (w_my, norm_g, dw_in, land_kv, land_out, d_mem_g, d_scale, d_bf, d_gq, d_gk, d_gqm, d_gkm, dwp_bd, loss_blk)
    return pl.pallas_call(
        body,
        name="grad_x_exchange",
        grid=(n_steps,),
        in_specs=[piece] * 6 + [pl.BlockSpec((TM, 128), lambda i: (i, 0)), row, row, vmem, vmem]
        + [pl.BlockSpec(memory_space=pl.ANY)] * 3 + [vmem] * (len(whole) - 5),
        out_specs=[row, vmem, vmem, vmem, vmem],
        out_shape=[jax.ShapeDtypeStruct((s_len, D_MODEL), F32), jax.ShapeDtypeStruct((CHUNK_ROWS, D_MODEL), F32),
                   jax.ShapeDtypeStruct((128, 512), F32), jax.ShapeDtypeStruct((128, D_MODEL), F32),
                   jax.ShapeDtypeStruct((SB_ROWS, SB_W), F32)],
        scratch_shapes=[
            pltpu.VMEM((1, D_MODEL), F32),
            pltpu.VMEM((4, CHUNK_ROWS, D_MODEL), BF16),
            half_chunk(2, BF16), half_chunk(2, BF16), half_chunk(2, F32), half_chunk(2, F32), half_chunk(2, BF16), half_chunk(2, BF16),
            pltpu.VMEM((CHUNK_ROWS, half), BF16), pltpu.VMEM((CHUNK_ROWS, half), BF16),
            pltpu.VMEM((CHUNK_ROWS, half), BF16), pltpu.VMEM((CHUNK_ROWS, half), BF16),
            pltpu.VMEM((SB_ROWS, SB_W), F32),
            pltpu.VMEM((N_DEV, SB_ROWS, SB_W), F32),
            pltpu.VMEM((4, CHUNK_ROWS, D_MODEL), BF16),
            pltpu.VMEM((N_DEV, 128, 512), BF16),
            pltpu.VMEM((N_DEV, 128, D_MODEL), BF16),
            pltpu.SemaphoreType.DMA((24,)),
            pltpu.SemaphoreType.DMA((24,)),
            pltpu.SemaphoreType.DMA((6,)),
        ],
        compiler_params=_params(("arbitrary",)),
    )(*pieces, d_f, x, d_out, *whole)


def _grad_w_in(h16, pieces, d_f):
    s_len = h16.shape[0]
    tk = 512

    def body(h_ref, p0, p1, p2, p3, p4, p5, df_ref, out_ref, dw_ref):
        @pl.when(pl.program_id(0) == 0)
        def _():
            dw_ref[...] = jnp.zeros((D_MODEL, MY_WIDTH), F32)

        h = h_ref[...]
        for j, p in enumerate((p0, p1, p2, p3, p4, p5)):
            dw_ref[:, PIECE * j:PIECE * (j + 1)] += _dot_tn(h, p[...])
        dw_ref[:, MY_F_OFF:MY_WIDTH] += _dot_tn(h, df_ref[...])

        @pl.when(pl.program_id(0) == pl.num_programs(0) - 1)
        def _():
            for d in range(N_DEV):
                parts = [dw_ref[:, start:start + width] for start, width in _chunk_segments(d)]
                parts.append(jnp.zeros((D_MODEL, 512 - IN_CHUNK), F32))
                out_ref[d] = jnp.concatenate(parts, axis=1).T[0:CHUNK_ROWS].astype(BF16)

    piece = pl.BlockSpec((tk, PIECE), lambda i: (i, 0))
    return pl.pallas_call(
        body,
        name="grad_w_in",
        grid=(s_len // tk,),
        in_specs=[pl.BlockSpec((tk, D_MODEL), lambda i: (i, 0))] + [piece] * 6 + [pl.BlockSpec((tk, 128), lambda i: (i, 0))],
        out_specs=_full((N_DEV, CHUNK_ROWS, D_MODEL)),
        out_shape=jax.ShapeDtypeStruct((N_DEV, CHUNK_ROWS, D_MODEL), BF16),
        scratch_shapes=[pltpu.VMEM((D_MODEL, MY_WIDTH), F32)],
        compiler_params=_params(("arbitrary",)),
    )(h16, *pieces, d_f)


def _adamw(w, g, m, v):
    m = ADAM_B1 * m + (1.0 - ADAM_B1) * g
    v = ADAM_B2 * v + (1.0 - ADAM_B2) * (g * g)
    m_hat = m / (1.0 - ADAM_B1 ** ADAM_STEP)
    v_hat = v / (1.0 - ADAM_B2 ** ADAM_STEP)
    return -ADAM_LR * (m_hat / (jnp.sqrt(v_hat) + ADAM_EPS) + ADAM_WD * w), m, v


PARAM_ORDER = ("norm_g", "w_in", "b_f", "w_pool", "pool_scale", "fox_q_g", "fox_k_g", "mem_norm_g", "w_mem_kv", "mem_q_g", "mem_k_g", "w_out")


def _update(red_in, red_kv, red_out, sred, params):
    def body(rin_ref, rkv_ref, rout_ref, sred_ref, *refs):
        ins, outs = refs[:3 * len(PARAM_ORDER)], refs[3 * len(PARAM_ORDER):]
        wide = lambda row: jnp.concatenate([sred_ref[row + k:row + k + 1, :] for k in range(4)], axis=1)
        head = lambda row: sred_ref[row:row + 1, 0:HEAD_DIM]
        grads = {
            "norm_g": lambda: wide(SB_NORM_G), "w_in": lambda: rin_ref[...], "b_f": lambda: sred_ref[SB_B_F:SB_B_F + 1, 0:FOX_HEADS],
            "pool_scale": lambda: sred_ref[SB_POOL_SCALE:SB_POOL_SCALE + 1, :], "fox_q_g": lambda: head(SB_FOX_Q),
            "fox_k_g": lambda: head(SB_FOX_K), "mem_norm_g": lambda: wide(SB_MEM_NORM_G), "w_mem_kv": lambda: rkv_ref[...],
            "mem_q_g": lambda: head(SB_MEM_Q), "mem_k_g": lambda: head(SB_MEM_K), "w_out": lambda: rout_ref[...],
        }
        for p, name in enumerate(PARAM_ORDER):
            w_ref, m_ref, v_ref = ins[3 * p:3 * p + 3]
            g_out, d_out, m_out, v_out = outs[4 * p:4 * p + 4]
            if name == "w_pool":
                for grp in range(4):
                    g = sred_ref[SB_W_POOL:SB_W_POOL + 64, 64 * grp:64 * (grp + 1)]
                    delta, m_new, v_new = _adamw(w_ref[grp], g, m_ref[grp], v_ref[grp])
                    g_out[grp], d_out[grp], m_out[grp], v_out[grp] = g, delta, m_new, v_new
            elif name == "w_in":
                for j in range(8):
                    rows = pl.ds(j, IN_CHUNK, stride=8)
                    g = rin_ref[0:IN_CHUNK, 128 * j:128 * (j + 1)]
                    delta, m_new, v_new = _adamw(w_ref[rows, :], g, m_ref[rows, :], v_ref[rows, :])
                    g_out[rows, :], d_out[rows, :], m_out[rows, :], v_out[rows, :] = g, delta, m_new, v_new
            else:
                g = grads[name]()
                delta, m_new, v_new = _adamw(w_ref[...], g, m_ref[...], v_ref[...])
                g_out[...], d_out[...], m_out[...], v_out[...] = g, delta, m_new, v_new

    flat = [t for name in PARAM_ORDER for t in params[name]]
    shapes = [params[name][0].shape for name in PARAM_ORDER for _ in range(4)]
    outs = pl.pallas_call(
        body,
        name="adamw_update",
        out_shape=tuple(jax.ShapeDtypeStruct(s, F32) for s in shapes),
        in_specs=[pl.BlockSpec(memory_space=pltpu.VMEM)] * (4 + len(flat)),
        out_specs=(pl.BlockSpec(memory_space=pltpu.VMEM),) * len(shapes),
        compiler_params=pltpu.CompilerParams(vmem_limit_bytes=VMEM_LIMIT),
    )(red_in, red_kv, red_out, sred, *flat)
    return {name: outs[4 * p:4 * p + 4] for p, name in enumerate(PARAM_ORDER)}


def kernel(x, mem, norm_g, w_in, b_f, w_pool, pool_scale, fox_q_g, fox_k_g, mem_norm_g, w_mem_kv, mem_q_g, mem_k_g, w_out, loss_target, m_norm_g, m_w_in, m_b_f, m_w_pool, m_pool_scale, m_fox_q_g, m_fox_k_g, m_mem_norm_g, m_w_mem_kv, m_mem_q_g, m_mem_k_g, m_w_out, v_norm_g, v_w_in, v_b_f, v_w_pool, v_pool_scale, v_fox_q_g, v_fox_k_g, v_mem_norm_g, v_w_mem_kv, v_mem_q_g, v_mem_k_g, v_w_out):
    given = dict(norm_g=(norm_g, m_norm_g, v_norm_g), w_in=(w_in, m_w_in, v_w_in), b_f=(b_f, m_b_f, v_b_f),
                 w_pool=(w_pool, m_w_pool, v_w_pool), pool_scale=(pool_scale, m_pool_scale, v_pool_scale),
                 fox_q_g=(fox_q_g, m_fox_q_g, v_fox_q_g), fox_k_g=(fox_k_g, m_fox_k_g, v_fox_k_g),
                 mem_norm_g=(mem_norm_g, m_mem_norm_g, v_mem_norm_g), w_mem_kv=(w_mem_kv, m_w_mem_kv, v_w_mem_kv),
                 mem_q_g=(mem_q_g, m_mem_q_g, v_mem_q_g), mem_k_g=(mem_k_g, m_mem_k_g, v_mem_k_g), w_out=(w_out, m_w_out, v_w_out))
    params = {name: tuple(t[0] if t.ndim > 2 else t for t in wmv) for name, wmv in given.items()}
    params["w_in"] = tuple(t.T.reshape(IN_CHUNK * 8, 128) for t in params["w_in"])
    x2, mem2, target2 = x[0], mem[0], loss_target[0]

    seg = jnp.asarray(np.kron(np.eye(FOX_HEADS), np.full((HEAD_DIM, HEAD_DIM), 1.0 / HEAD_DIM)), BF16)
    w_my, w_kv16, w_out16, wp_bd, bf_pad, gq8, gk8, gqm4 = _gather_w_in(
        params["w_in"][0], params["w_mem_kv"][0], params["w_out"][0], params["w_pool"][0], b_f, fox_q_g, fox_k_g, mem_q_g)
    proj, h16 = _fwd_proj(x2, norm_g, w_my)
    q_aug, k_aug, v_h, kt_aug, vt_aug = _fox_prep(proj, bf_pad, gq8, gk8, seg)
    olse, lse_rows, w_kv_all, w_out_all = _fox_fwd(q_aug, k_aug, vt_aug, w_kv16, w_out16)
    km, vm = _mem_prep(mem2, mem_norm_g, w_kv_all, mem_k_g)
    mixed = _mix_fwd(proj, olse, km, vm, wp_bd, pool_scale, gqm4, seg)
    d_out, d_mixed, dw_out, loss_blk = _out_loss(mixed, x2, target2, w_out_all)

    d_ua, d_gb, d_qm, d_o, delta_rows, dwp_bd, d_scale, dkm, dvm, d_gqm = _mix_bwd(proj, olse, d_mixed, km, vm, wp_bd, pool_scale,
                                                                                    gqm4, seg)
    dw_kv, d_mem_g, d_gkm = _mem_bwd(mem2, mem_norm_g, w_kv_all, mem_k_g, dkm, dvm)
    dq_aug, dk_aug, dv_h, land_kv, land_out = _fox_bwd(q_aug, k_aug, kt_aug, v_h, d_o, lse_rows, delta_rows, dw_kv, dw_out)
    d_q, d_k, d_v, d_f, d_gq, d_gk, d_bf = _fox_post(proj, bf_pad, gq8, gk8, seg, dq_aug, dk_aug, dv_h)
    pieces = (d_ua, d_q, d_k, d_v, d_gb, d_qm)
    dw_in = _grad_w_in(h16, pieces, d_f)
    grad_x, red_in, red_kv, red_out, sred = _grad_x_exchange(pieces, d_f, w_my, x2, norm_g, d_out, dw_in, land_kv, land_out, d_mem_g,
                                                             d_scale, d_bf, d_gq, d_gk, d_gqm, d_gkm, dwp_bd, loss_blk)
    new = _update(red_in, red_kv, red_out, sred, params)
    result = [sred[SB_LOSS, 0], grad_x[None]]
    for kind in range(4):
        for name in PARAM_ORDER:
            out = new[name][kind]
            if name == "w_in":
                out = out.reshape(IN_CHUNK, D_MODEL).T
            result.append(out[None] if given[name][0].ndim > 2 else out)
    return tuple(result)
```

```python
import functools

import jax
import jax.numpy as jnp
import numpy as np
from jax import lax
from jax.experimental import pallas as pl
from jax.experimental.pallas import tpu as pltpu

F32 = jnp.float32
BF16 = jnp.bfloat16
MESH = pl.DeviceIdType.MESH

N_DEV = 8
D_MODEL = 1024
HEAD_DIM = 64
FOX_HEADS = 8
MEM_HEADS = 4
N_MEM = 256
POOL_WIDTH = 256
EPS = 1e-6
IN_WIDTH = 3080
IN_CHUNK = IN_WIDTH // N_DEV
CHUNK_ROWS = 400
F_OFF = 2048
MY_WIDTH = 3200
MY_F_OFF = 3072
PIECE = 512
TM = 256
TMM = 512
TA = 256
HB = 8
HB_FWD = 8
AUG_ROWS = 72
HALO = 16
VMEM_LIMIT = 56 * 1024 * 1024

ADAM_LR = 0.001
ADAM_B1 = 0.9
ADAM_B2 = 0.999
ADAM_EPS = 1e-08
ADAM_WD = 0.01
ADAM_STEP = 10


def _dot(a, b):
    return jnp.dot(a, b, preferred_element_type=F32)


def _dot_nt(a, b):
    return lax.dot_general(a, b, (((1,), (1,)), ((), ())), preferred_element_type=F32)


def _dot_tn(a, b):
    return lax.dot_general(a, b, (((0,), (0,)), ((), ())), preferred_element_type=F32)


def _sigmoid(g):
    return 0.5 + 0.5 * jnp.tanh(0.5 * g)


def _split3(v):
    hi = v.astype(BF16)
    r1 = v - hi.astype(F32)
    mid = r1.astype(BF16)
    lo = (r1 - mid.astype(F32)).astype(BF16)
    return hi, mid, lo


def _rms(v):
    return lax.rsqrt(jnp.mean(v * v, axis=-1, keepdims=True) + EPS)


def _rms_bwd(a, r, g, dy):
    da = dy * g
    return r * (da - a * jnp.mean(da * a, axis=-1, keepdims=True)), dy * a


def _head_mean(z, seg):
    hi = z.astype(BF16)
    lo = (z - hi.astype(F32)).astype(BF16)
    if z.shape[1] == 256:
        return _dot(hi, seg) + _dot(lo, seg)
    half = seg[0:256, 0:256]
    return jnp.concatenate([_dot(hi[:, 0:256], half) + _dot(lo[:, 0:256], half),
                            _dot(hi[:, 256:512], half) + _dot(lo[:, 256:512], half)], axis=1)


def _head_rms(v, seg):
    return lax.rsqrt(_head_mean(v * v, seg) + EPS)


def _head_rms_bwd(a, r, g, dy, seg):
    da = dy * g
    return r * (da - a * _head_mean(da * a, seg)), dy * a


def _fold_heads(row, n_heads):
    out = row[:, 0:HEAD_DIM]
    for h in range(1, n_heads):
        out = out + row[:, HEAD_DIM * h:HEAD_DIM * (h + 1)]
    return out


def _params(sem, limit=VMEM_LIMIT):
    return pltpu.CompilerParams(dimension_semantics=sem, vmem_limit_bytes=limit)


def _full(shape):
    return pl.BlockSpec(shape, lambda *_: (0,) * len(shape))


def _chunk_segments(d):
    runs = []
    for lo, hi, shift in ((0, F_OFF, 0), (F_OFF, F_OFF + FOX_HEADS, MY_F_OFF - F_OFF), (F_OFF + FOX_HEADS, IN_WIDTH, -FOX_HEADS)):
        a, b = max(lo, IN_CHUNK * d), min(hi, IN_CHUNK * (d + 1))
        if a < b:
            runs.append((a + shift, b - a))
    return runs


def _my_place():
    x, y, c = lax.axis_index("x"), lax.axis_index("y"), lax.axis_index("c")
    return x, y, c, 4 * x + 2 * y + c


def _shard_rows(dev):
    return pl.ds(pl.multiple_of(128 * dev, 128), 128)


def _gather_w_in(w_in, w_kv, w_out, w_pool, b_f, gq, gk, gqm):
    def body(win_ref, wkv_ref, wout_ref, wp_ref, bf_ref, gq_ref, gk_ref, gqm_ref, wmy_ref, kv16_ref, out16_ref, wpbd_ref, bfpad_ref,
             gq8_ref, gk8_ref, gqm4_ref, in_all, pay_in, win_rows, send_sems, recv_sems, load_sem):
        x, y, c, me = _my_place()
        here, sibling = (x, y, c), (x, y, 1 - c)
        x_chip, y_chip, far_chip = (1 - x, y), (x, 1 - y), (1 - x, 1 - y)
        relay_from = (x ^ (1 - c), y ^ c)
        relay_to = (x ^ c, y ^ (1 - c))

        load = pltpu.make_async_copy(win_ref, win_rows, load_sem)
        load.start()
        load.wait()
        pay_in[...] = jnp.zeros((CHUNK_ROWS, D_MODEL), BF16)
        for j in range(8):
            pay_in[0:IN_CHUNK, 128 * j:128 * (j + 1)] = win_rows[pl.ds(j, IN_CHUNK, stride=8), :].astype(BF16)

        def copy(k, block, to, own=False):
            px, py, pc = block
            dst = in_all.at[4 * px + 2 * py + pc]
            return pltpu.make_async_remote_copy(src_ref=pay_in if own else dst, dst_ref=dst, send_sem=send_sems.at[k],
                                                recv_sem=recv_sems.at[k], device_id=to, device_id_type=MESH)

        first = [copy(0, here, sibling, own=True), copy(1, here, (*x_chip, c), own=True), copy(2, here, (*y_chip, c), own=True)]
        for cp in first:
            cp.start()
        in_all[me] = pay_in[...]
        kv16_ref[...] = wkv_ref[...].astype(BF16)
        out16_ref[...] = wout_ref[...].astype(BF16)
        wpbd_ref[...] = jnp.zeros((POOL_WIDTH, POOL_WIDTH), BF16)
        for grp in range(4):
            sl = slice(64 * grp, 64 * (grp + 1))
            wpbd_ref[sl, sl] = wp_ref[grp].astype(BF16)
        bfpad_ref[...] = jnp.zeros((1, 128), F32)
        bfpad_ref[:, 0:FOX_HEADS] = bf_ref[...]
        gq8_ref[...] = jnp.concatenate([gq_ref[...]] * FOX_HEADS, axis=1)
        gk8_ref[...] = jnp.concatenate([gk_ref[...]] * FOX_HEADS, axis=1)
        gqm4_ref[...] = jnp.concatenate([gqm_ref[...]] * MEM_HEADS, axis=1)
        copy(1 + c, (*relay_from, c), here).wait_recv()
        passed = [copy(3, (*relay_from, c), (*relay_to, c)), copy(4, (*relay_from, c), sibling)]
        for cp in passed:
            cp.start()
        copy(2 - c, (*relay_to, c), here).wait_recv()
        passed.append(copy(5, (*relay_to, c), sibling))
        passed[-1].start()
        copy(3, (*far_chip, c), here).wait_recv()
        passed.append(copy(6, (*far_chip, c), sibling))
        passed[-1].start()
        copy(0, sibling, here).wait_recv()
        for k, chip in ((4, relay_to), (5, relay_from), (6, far_chip)):
            copy(k, (*chip, 1 - c), here).wait_recv()
        for cp in first + passed:
            cp.wait_send()

        row_pad = jnp.zeros((512 - CHUNK_ROWS, 256), F32)
        for r0 in range(0, D_MODEL, 256):
            ch = [jnp.concatenate([in_all[d, :, r0:r0 + 256].astype(F32), row_pad], axis=0).T[:, 0:IN_CHUNK] for d in range(N_DEV)]
            lo = F_OFF - 5 * IN_CHUNK
            full = jnp.concatenate(ch[:5] + [ch[5][:, :lo], ch[5][:, lo + FOX_HEADS:], ch[6], ch[7], ch[5][:, lo:lo + FOX_HEADS],
                                             jnp.zeros((256, MY_WIDTH - IN_WIDTH), F32)], axis=1)
            wmy_ref[r0:r0 + 256, :] = full.astype(BF16)

    return pl.pallas_call(
        body,
        name="gather_w_in",
        out_shape=(jax.ShapeDtypeStruct((D_MODEL, MY_WIDTH), BF16), jax.ShapeDtypeStruct((128, 512), BF16),
                   jax.ShapeDtypeStruct((128, D_MODEL), BF16), jax.ShapeDtypeStruct((POOL_WIDTH, POOL_WIDTH), BF16),
                   jax.ShapeDtypeStruct((1, 128), F32), jax.ShapeDtypeStruct((1, PIECE), F32), jax.ShapeDtypeStruct((1, PIECE), F32),
                   jax.ShapeDtypeStruct((1, 256), F32)),
        in_specs=[pl.BlockSpec(memory_space=pl.ANY)] + [pl.BlockSpec(memory_space=pltpu.VMEM)] * 7,
        out_specs=(pl.BlockSpec(memory_space=pltpu.VMEM),) * 8,
        scratch_shapes=[
            pltpu.VMEM((N_DEV, CHUNK_ROWS, D_MODEL), BF16),
            pltpu.VMEM((CHUNK_ROWS, D_MODEL), BF16),
            pltpu.VMEM((IN_CHUNK * 8, 128), F32),
            pltpu.SemaphoreType.DMA((7,)),
            pltpu.SemaphoreType.DMA((7,)),
            pltpu.SemaphoreType.DMA,
        ],
        compiler_params=pltpu.CompilerParams(vmem_limit_bytes=VMEM_LIMIT),
    )(w_in, w_kv, w_out, w_pool, b_f, gq, gk, gqm)


def _row_block_exchange(kv_ref, out_ref, kv_dst, out_dst, send_sems, recv_sems, local_sems, gather):
    x, y, c, me = _my_place()
    remote = []
    for k in range(1, N_DEV):
        px, py, pc = x ^ (k >> 2), y ^ ((k >> 1) & 1), c ^ (k & 1)
        peer = 4 * px + 2 * py + pc
        for fam, (src, dst) in enumerate(((kv_ref, kv_dst), (out_ref, out_dst))):
            remote.append(pltpu.make_async_remote_copy(
                src_ref=src if gather else src.at[_shard_rows(peer)], dst_ref=dst.at[_shard_rows(me)] if gather else dst.at[me],
                send_sem=send_sems.at[7 * fam + k - 1], recv_sem=recv_sems.at[7 * fam + k - 1],
                device_id=(px, py, pc), device_id_type=MESH))
    local = [pltpu.make_async_copy(src if gather else src.at[_shard_rows(me)], dst.at[_shard_rows(me)] if gather else dst.at[me],
                                   local_sems.at[fam]) for fam, (src, dst) in enumerate(((kv_ref, kv_dst), (out_ref, out_dst)))]
    return remote, local


SB_ROWS, SB_W = 80, 256
SB_NORM_G, SB_MEM_NORM_G, SB_POOL_SCALE, SB_B_F, SB_FOX_Q, SB_FOX_K, SB_MEM_Q, SB_MEM_K, SB_LOSS, SB_W_POOL = 0, 4, 8, 9, 10, 11, 12, 13, 14, 16


def _fwd_proj(x, norm_g, w_my):
    s_len = x.shape[0]

    def body(x_ref, g_ref, w_ref, proj_ref, h_ref):
        xv = x_ref[...]
        h = (xv * _rms(xv) * g_ref[...]).astype(BF16)
        h_ref[...] = h
        proj_ref[...] = _dot(h, w_ref[...])

    return pl.pallas_call(
        body,
        name="fwd_proj",
        grid=(s_len // TMM,),
        in_specs=[pl.BlockSpec((TMM, D_MODEL), lambda i: (i, 0)), _full((1, D_MODEL)), _full((D_MODEL, MY_WIDTH))],
        out_specs=[pl.BlockSpec((TMM, MY_WIDTH), lambda i: (i, 0)), pl.BlockSpec((TMM, D_MODEL), lambda i: (i, 0))],
        out_shape=[jax.ShapeDtypeStruct((s_len, MY_WIDTH), F32), jax.ShapeDtypeStruct((s_len, D_MODEL), BF16)],
        compiler_params=_params(("parallel",)),
    )(x, norm_g, w_my)


def _log_sigmoid(z):
    return jnp.minimum(z, 0.0) - jnp.log(1.0 + jnp.exp(-jnp.abs(z)))


def _forget_lane_maps():
    to_q = np.zeros((128, FOX_HEADS * 128), np.float32)
    to_k = np.zeros((128, FOX_HEADS * 128), np.float32)
    for h in range(FOX_HEADS):
        for term in range(3):
            to_q[8 * term + h, 128 * h + 64 + term] = 1.0
            to_q[24, 128 * h + 67 + term] = 1.0
            to_k[24, 128 * h + 64 + term] = 1.0
            to_k[8 * term + h, 128 * h + 67 + term] = -1.0
    return jnp.asarray(to_q, BF16), jnp.asarray(to_k, BF16)


def _fox_prep(proj, bf_pad, gq, gk, seg):
    s_len = proj.shape[0]
    to_q, to_k = _forget_lane_maps()

    def body(q_ref, k_ref, v_ref, f_ref, bf_ref, gq_ref, gk_ref, seg_ref, eq_ref, ek_ref,
             qa_ref, ka_ref, vh_ref, kt_ref, vt_ref, carry_ref):
        @pl.when(pl.program_id(0) == 0)
        def _():
            carry_ref[...] = jnp.zeros((1, 128), F32)

        lane = lax.broadcasted_iota(jnp.int32, (TM, 128), 1)
        logf = jnp.where(lane < FOX_HEADS, _log_sigmoid(f_ref[...] + bf_ref[...]), 0.0)
        row = lax.broadcasted_iota(jnp.int32, (TM, TM), 0)
        col = lax.broadcasted_iota(jnp.int32, (TM, TM), 1)
        tri = jnp.where(col <= row, 1.0, 0.0).astype(BF16)
        hi, mid, lo = _split3(logf)
        cum = _dot(tri, hi) + _dot(tri, mid) + _dot(tri, lo) + carry_ref[...]
        carry_ref[...] = cum[TM - 1:TM, :]
        f_hi, f_mid, f_lo = [t.astype(F32) for t in _split3(cum)]
        packed = (f_hi + pltpu.roll(f_mid, 8, 1) + pltpu.roll(f_lo, 16, 1) + jnp.where(lane == 24, 1.0, 0.0)).astype(BF16)
        extra_q = _dot(packed, eq_ref[...])
        extra_k = _dot(packed, ek_ref[...])
        one_at_64 = jnp.where(lane == 64, 1.0, 0.0)
        zeros = jnp.zeros((TM, HEAD_DIM), F32)
        q_all = q_ref[...]
        k_all = k_ref[...]
        qn_all = q_all * _head_rms(q_all, seg_ref[...]) * gq_ref[...] * 0.125
        kn_all = k_all * _head_rms(k_all, seg_ref[...]) * gk_ref[...]
        for h in range(FOX_HEADS):
            sl = slice(HEAD_DIM * h, HEAD_DIM * (h + 1))
            tile = slice(128 * h, 128 * (h + 1))
            qa = jnp.where(lane < 64, jnp.concatenate([qn_all[:, sl], zeros], axis=1), extra_q[:, tile])
            ka = jnp.where(lane < 64, jnp.concatenate([kn_all[:, sl], zeros], axis=1), extra_k[:, tile])
            vh = v_ref[:, sl]
            v_aug = jnp.where(lane < 64, jnp.concatenate([vh, zeros], axis=1), one_at_64)
            qa_ref[h] = qa.astype(BF16)
            ka_ref[h] = ka.astype(BF16)
            vh_ref[h] = vh.astype(BF16)
            kt_ref[h, 0] = ka.T.astype(BF16)
            vt_ref[h, 0] = v_aug.T.astype(BF16)

    col_blk = lambda j: pl.BlockSpec((TM, PIECE), lambda i: (i, j))
    head_blk = lambda w: pl.BlockSpec((FOX_HEADS, TM, w), lambda i: (0, i, 0))
    tile_blk = pl.BlockSpec((FOX_HEADS, 1, 128, TM), lambda i: (0, i, 0, 0))
    tiled = jax.ShapeDtypeStruct((FOX_HEADS, s_len // TM, 128, TM), BF16)
    return pl.pallas_call(
        body,
        name="fox_prep",
        grid=(s_len // TM,),
        in_specs=[col_blk(1), col_blk(2), col_blk(3), pl.BlockSpec((TM, 128), lambda i: (i, MY_F_OFF // 128)),
                  _full((1, 128)), _full((1, PIECE)), _full((1, PIECE)), _full((PIECE, PIECE)),
                  _full((128, FOX_HEADS * 128)), _full((128, FOX_HEADS * 128))],
        out_specs=[head_blk(128), head_blk(128), head_blk(HEAD_DIM), tile_blk, tile_blk],
        out_shape=[jax.ShapeDtypeStruct((FOX_HEADS, s_len, 128), BF16), jax.ShapeDtypeStruct((FOX_HEADS, s_len, 128), BF16),
                   jax.ShapeDtypeStruct((FOX_HEADS, s_len, HEAD_DIM), BF16), tiled, tiled],
        scratch_shapes=[pltpu.VMEM((1, 128), F32)],
        compiler_params=_params(("arbitrary",)),
    )(proj, proj, proj, proj, bf_pad, gq, gk, seg, to_q, to_k)


def _mem_prep(mem, g, w_kv, gk):
    def body(mem_ref, g_ref, w_ref, gk_ref, km_ref, vm_ref):
        m = mem_ref[...]
        kv = _dot((m * _rms(m) * g_ref[...]).astype(BF16), w_ref[...])
        for h in range(MEM_HEADS):
            sl = slice(HEAD_DIM * h, HEAD_DIM * (h + 1))
            kh = kv[:, sl]
            km_ref[:, sl] = (kh * _rms(kh) * gk_ref[...]).astype(BF16)
        vm_ref[...] = kv[:, 256:512].astype(BF16)

    return pl.pallas_call(
        body,
        name="mem_prep",
        out_shape=(jax.ShapeDtypeStruct((N_MEM, 256), BF16),) * 2,
        in_specs=[pl.BlockSpec(memory_space=pltpu.VMEM)] * 4,
        out_specs=(pl.BlockSpec(memory_space=pltpu.VMEM),) * 2,
        compiler_params=pltpu.CompilerParams(vmem_limit_bytes=VMEM_LIMIT),
    )(mem, g, w_kv, gk)


def _causal_mask():
    row = lax.broadcasted_iota(jnp.int32, (TA, TA), 0)
    col = lax.broadcasted_iota(jnp.int32, (TA, TA), 1)
    return col <= row


def _causal_mask_t():
    row = lax.broadcasted_iota(jnp.int32, (TA, TA), 0)
    col = lax.broadcasted_iota(jnp.int32, (TA, TA), 1)
    return row <= col


def _fox_fwd(q_aug, k_aug, vt_aug, w_kv16, w_out16):
    s_len = q_aug.shape[1]
    n_tiles = s_len // TA
    hb = HB_FWD
    n_groups = FOX_HEADS // hb

    def body(q_ref, k_new, vt_new, kv16_ref, out16_ref, o_ref, st_ref, kv_all, out_all, k_ref, vt_ref, send_sems, recv_sems, local_sems):
        qi = pl.program_id(1)
        for h in range(hb):
            k_ref[h, pl.ds(pl.multiple_of(qi * TA, TA), TA), :] = k_new[h]
            vt_ref[h, qi] = vt_new[h, 0]
        remote, local = _row_block_exchange(kv16_ref, out16_ref, kv_all, out_all, send_sems, recv_sems, local_sems, gather=True)

        @pl.when((pl.program_id(0) == 0) & (qi == 0))
        def _():
            for cp in remote + local:
                cp.start()

        qs = [q_ref[h] for h in range(hb)]

        def step(t0, carry, masked, width):
            rows = pl.ds(pl.multiple_of(t0 * TA, TA), width * TA)
            scores = [_dot_nt(k_ref[h, rows, :], qs[h]) for h in range(hb)]
            soft = []
            for h in range(hb):
                m, _ = carry[h]
                s = jnp.where(_causal_mask_t(), scores[h], -1e30) if masked else scores[h]
                m_new = jnp.maximum(m, jnp.max(s, axis=0, keepdims=True))
                soft.append((m_new, jnp.exp(m - m_new), jnp.exp(s - m_new).astype(BF16)))
            out = []
            for h, (m_new, alpha, p) in enumerate(soft):
                acc = alpha * carry[h][1]
                for t in range(width):
                    acc = acc + _dot(vt_ref[h, t0 + t, 0:AUG_ROWS, :], p[t * TA:(t + 1) * TA])
                out.append((m_new, acc))
            return tuple(out)

        init = tuple((jnp.full((1, TA), -1e30, F32), jnp.zeros((AUG_ROWS, TA), F32)) for _ in range(hb))
        carry = lax.fori_loop(0, qi // 2, lambda j, cr: step(2 * j, cr, False, 2), init)
        carry = lax.fori_loop(2 * (qi // 2), qi, lambda j, cr: step(j, cr, False, 1), carry)
        carry = step(qi, carry, True, 1)
        for h in range(hb):
            m, acc = carry[h]
            l = acc[HEAD_DIM:HEAD_DIM + 1, :]
            lse = m + jnp.log(l)
            o_ref[h] = jnp.concatenate([acc[0:HEAD_DIM] / l, jnp.broadcast_to(lse, (HEAD_DIM, TA))], axis=0).T
            st_ref[h, 0] = jnp.broadcast_to(lse, (8, TA))

        @pl.when((pl.program_id(0) == n_groups - 1) & (qi == n_tiles - 1))
        def _():
            for cp in remote:
                cp.wait_recv()
            for cp in remote:
                cp.wait_send()
            for cp in local:
                cp.wait()

    hbm = pl.BlockSpec(memory_space=pl.ANY)
    return pl.pallas_call(
        body,
        name="fox_fwd",
        grid=(n_groups, n_tiles),
        in_specs=[pl.BlockSpec((hb, TA, 128), lambda g, i: (g, i, 0)), pl.BlockSpec((hb, TA, 128), lambda g, i: (g, i, 0)),
                  pl.BlockSpec((hb, 1, 128, TA), lambda g, i: (g, i, 0, 0)), hbm, hbm],
        out_specs=[pl.BlockSpec((hb, TA, 128), lambda g, i: (g, i, 0)), pl.BlockSpec((hb, 1, 8, TA), lambda g, i: (g, i, 0, 0)),
                   hbm, hbm],
        out_shape=[jax.ShapeDtypeStruct((FOX_HEADS, s_len, 128), F32), jax.ShapeDtypeStruct((FOX_HEADS, n_tiles, 8, TA), F32),
                   jax.ShapeDtypeStruct((D_MODEL, 512), BF16), jax.ShapeDtypeStruct((D_MODEL, D_MODEL), BF16)],
        scratch_shapes=[pltpu.VMEM((hb, s_len, 128), BF16), pltpu.VMEM((hb, n_tiles, 128, TA), BF16),
                        pltpu.SemaphoreType.DMA((14,)), pltpu.SemaphoreType.DMA((14,)), pltpu.SemaphoreType.DMA((2,))],
        compiler_params=_params(("arbitrary", "arbitrary")),
    )(q_aug, k_aug, vt_aug, w_kv16, w_out16)


def _lane_group(lane, a, b, c, d):
    return jnp.where(lane < 64, a, jnp.where(lane < 128, b, jnp.where(lane < 192, c, d)))


def _pool_count(row0):
    lane = lax.broadcasted_iota(jnp.int32, (TM, POOL_WIDTH), 1)
    t1 = row0 + lax.broadcasted_iota(jnp.int32, (TM, POOL_WIDTH), 0) + 1
    return 1.0 / jnp.minimum(t1, _lane_group(lane, 2, 4, 8, 16)).astype(F32), lane


def _pool_delta(u, halo, cnt, lane):
    xe = jnp.concatenate([halo, u], axis=0)
    s2 = xe + pltpu.roll(xe, 1, 0)
    s4 = s2 + pltpu.roll(s2, 2, 0)
    s8 = s4 + pltpu.roll(s4, 4, 0)
    s16 = s8 + pltpu.roll(s8, 8, 0)
    win = _lane_group(lane, s2[HALO:], s4[HALO:], s8[HALO:], s16[HALO:])
    return win * cnt - u


def _pool_delta_bwd(dd, dd_next, cnt, cnt_next, lane):
    ee = jnp.concatenate([dd * cnt, dd_next * cnt_next], axis=0)
    n = TM + HALO
    r2 = ee + pltpu.roll(ee, n - 1, 0)
    r4 = r2 + pltpu.roll(r2, n - 2, 0)
    r8 = r4 + pltpu.roll(r4, n - 4, 0)
    r16 = r8 + pltpu.roll(r8, n - 8, 0)
    return _lane_group(lane, r2[:TM], r4[:TM], r8[:TM], r16[:TM]) - dd


def _mem_attn(qm, gq, seg, km_ref, vm_ref):
    r = _head_rms(qm, seg)
    a = qm * r
    q16 = (a * gq * 0.125).astype(BF16)
    heads = []
    for h in range(MEM_HEADS):
        sl = slice(HEAD_DIM * h, HEAD_DIM * (h + 1))
        s = _dot_nt(q16[:, sl], km_ref[:, sl])
        e = jnp.exp(s - jnp.max(s, axis=-1, keepdims=True))
        p = e * (1.0 / jnp.sum(e, axis=-1, keepdims=True))
        heads.append((p, _dot(p.astype(BF16), vm_ref[:, sl])))
    return a, r, q16, heads


def _row_specs(s_len):
    n_halo = s_len // HALO
    piece = lambda j: pl.BlockSpec((TM, PIECE), lambda i: (i, j))
    before = lambda j: pl.BlockSpec((HALO, 256), lambda i: (jnp.maximum(i * (TM // HALO) - 1, 0), j))
    after = lambda j: pl.BlockSpec((HALO, 256), lambda i: (jnp.minimum((i + 1) * (TM // HALO), n_halo - 1), j))
    return piece, before, after


def _mix_fwd(proj, olse, km, vm, wp_bd, pool_scale, gq_m, seg):
    s_len = proj.shape[0]

    def body(ua_ref, halo_ref, gb_ref, qm_ref, ol_ref, km_ref, vm_ref, wp_ref, sc_ref, gq_ref, seg_ref, out_ref):
        i = pl.program_id(0)
        u = ua_ref[:, 0:256]
        g_a = ua_ref[:, 256:512]
        halo = jnp.where(i > 0, halo_ref[...], 0.0)
        cnt, lane = _pool_count(i * TM)
        d = _pool_delta(u, halo, cnt, lane).astype(BF16)
        y_a = _dot(d, wp_ref[...]) * sc_ref[...]
        out_ref[:, 0:256] = (y_a * (g_a * _sigmoid(g_a))).astype(BF16)
        g_b = gb_ref[...]
        y_b = jnp.concatenate([ol_ref[h][:, 0:HEAD_DIM] for h in range(FOX_HEADS)], axis=1)
        out_ref[:, 256:768] = (y_b * (g_b * _sigmoid(g_b))).astype(BF16)
        _, _, _, heads = _mem_attn(qm_ref[:, 0:256], gq_ref[...], seg_ref[0:256, 0:256], km_ref, vm_ref)
        g_m = qm_ref[:, 256:512]
        y_m = jnp.concatenate([y for _, y in heads], axis=1)
        out_ref[:, 768:1024] = (y_m * (g_m * _sigmoid(g_m))).astype(BF16)

    piece, before, _ = _row_specs(s_len)
    return pl.pallas_call(
        body,
        name="mix_fwd",
        grid=(s_len // TM,),
        in_specs=[piece(0), before(0), piece(4), piece(5), pl.BlockSpec((FOX_HEADS, TM, 128), lambda i: (0, i, 0)),
                  _full((N_MEM, 256)), _full((N_MEM, 256)), _full((256, 256)), _full((1, 256)), _full((1, 256)),
                  _full((PIECE, PIECE))],
        out_specs=pl.BlockSpec((TM, D_MODEL), lambda i: (i, 0)),
        out_shape=jax.ShapeDtypeStruct((s_len, D_MODEL), BF16),
        compiler_params=_params(("parallel",)),
    )(proj, proj, proj, proj, olse, km, vm, wp_bd, pool_scale, gq_m, seg)


def _out_loss(mixed, x, target, w_out):
    s_len = x.shape[0]

    def body(mix_ref, x_ref, t_ref, w_ref, dout_ref, dmix_ref, dw16_ref, loss_ref, dw_ref):
        @pl.when(pl.program_id(0) == 0)
        def _():
            dw_ref[...] = jnp.zeros((D_MODEL, D_MODEL), F32)
            loss_ref[...] = jnp.zeros((8, 128), F32)

        mixed16 = mix_ref[...]
        err = x_ref[...] + _dot(mixed16, w_ref[...]) - t_ref[...]
        loss_ref[...] += 0.5 * jnp.sum(jnp.mean(err * err, axis=-1, keepdims=True))
        d_out = err / float(D_MODEL)
        dout_ref[...] = d_out
        d16 = d_out.astype(BF16)
        dmix_ref[...] = _dot_nt(d16, w_ref[...])
        dw_ref[...] += _dot_tn(mixed16, d16)

        @pl.when(pl.program_id(0) == pl.num_programs(0) - 1)
        def _():
            dw16_ref[...] = dw_ref[...].astype(BF16)

    row = pl.BlockSpec((TMM, D_MODEL), lambda i: (i, 0))
    return pl.pallas_call(
        body,
        name="out_loss",
        grid=(s_len // TMM,),
        in_specs=[row, row, row, _full((D_MODEL, D_MODEL))],
        out_specs=[row, row, _full((D_MODEL, D_MODEL)), _full((8, 128))],
        out_shape=[jax.ShapeDtypeStruct((s_len, D_MODEL), F32), jax.ShapeDtypeStruct((s_len, D_MODEL), F32),
                   jax.ShapeDtypeStruct((D_MODEL, D_MODEL), BF16), jax.ShapeDtypeStruct((8, 128), F32)],
        scratch_shapes=[pltpu.VMEM((D_MODEL, D_MODEL), F32)],
        compiler_params=_params(("arbitrary",)),
    )(mixed, x, target, w_out)


def _silu_pair(g):
    s = _sigmoid(g)
    return g * s, s * (1.0 + g * (1.0 - s))


def _mix_bwd(proj, olse, d_mixed, km, vm, wp_bd, pool_scale, gq_m, seg):
    s_len = proj.shape[0]
    n_tiles = s_len // TM

    def body(ua_ref, halo_ref, ga_next_ref, gb_ref, qm_ref, ol_ref, dm_ref, dm_next_ref, km_ref, vm_ref, wp_ref, sc_ref, gq_ref,
             seg_ref, dua_ref, dgb_ref, dqm_ref, do_ref, dl_ref, dwp_ref, dsc_ref, dkm_ref, dvm_ref, dgq_ref):
        i = pl.program_id(0)

        @pl.when(i == 0)
        def _():
            dwp_ref[...] = jnp.zeros((256, 256), F32)
            dsc_ref[...] = jnp.zeros((1, 256), F32)
            dkm_ref[...] = jnp.zeros((N_MEM, 256), F32)
            dvm_ref[...] = jnp.zeros((N_MEM, 256), F32)
            dgq_ref[...] = jnp.zeros((1, HEAD_DIM), F32)

        u = ua_ref[:, 0:256]
        g_a = ua_ref[:, 256:512]
        halo = jnp.where(i > 0, halo_ref[...], 0.0)
        cnt, lane = _pool_count(i * TM)
        d16 = _pool_delta(u, halo, cnt, lane).astype(BF16)
        t = _dot(d16, wp_ref[...])
        y_a = t * sc_ref[...]
        silu_a, dsilu_a = _silu_pair(g_a)
        dm_a = dm_ref[:, 0:256]
        dy_a = dm_a * silu_a
        dsc_ref[...] += jnp.sum(dy_a * t, axis=0, keepdims=True)
        dt16 = (dy_a * sc_ref[...]).astype(BF16)
        dwp_ref[...] += _dot_tn(d16, dt16)
        dd = _dot_nt(dt16, wp_ref[...])
        g_next = ga_next_ref[...]
        dt_next = (dm_next_ref[...] * (g_next * _sigmoid(g_next)) * sc_ref[...]).astype(BF16)
        dd_next = jnp.where(i < n_tiles - 1, _dot_nt(dt_next, wp_ref[...]), 0.0)
        cnt_next = _pool_count((i + 1) * TM)[0][0:HALO]
        dua_ref[:, 0:256] = _pool_delta_bwd(dd, dd_next, cnt, cnt_next, lane).astype(BF16)
        dua_ref[:, 256:512] = (dm_a * y_a * dsilu_a).astype(BF16)

        silu_b, dsilu_b = _silu_pair(gb_ref[...])
        dm_b = dm_ref[:, 256:768]
        o_all = jnp.concatenate([ol_ref[h][:, 0:HEAD_DIM] for h in range(FOX_HEADS)], axis=1)
        d_o = dm_b * silu_b
        dgb_ref[...] = (dm_b * o_all * dsilu_b).astype(BF16)
        for h in range(FOX_HEADS):
            do_ref[h] = d_o[:, HEAD_DIM * h:HEAD_DIM * (h + 1)].astype(BF16)
        prod = d_o * o_all
        hi = prod.astype(BF16)
        lo = (prod - hi.astype(F32)).astype(BF16)
        head_of_lane = lax.broadcasted_iota(jnp.int32, (FOX_HEADS, PIECE), 1) // HEAD_DIM
        pick = jnp.where(head_of_lane == lax.broadcasted_iota(jnp.int32, (FOX_HEADS, PIECE), 0), 1.0, 0.0).astype(BF16)
        delta = _dot_nt(pick, hi) + _dot_nt(pick, lo)
        for h in range(FOX_HEADS):
            dl_ref[h, 0] = jnp.broadcast_to(delta[h:h + 1, :], (8, TM))

        seg = seg_ref[0:256, 0:256]
        a, r, q16, heads = _mem_attn(qm_ref[:, 0:256], gq_ref[...], seg, km_ref, vm_ref)
        silu_m, dsilu_m = _silu_pair(qm_ref[:, 256:512])
        dm_m = dm_ref[:, 768:1024]
        y_m = jnp.concatenate([y for _, y in heads], axis=1)
        dqm_ref[:, 256:512] = (dm_m * y_m * dsilu_m).astype(BF16)
        dy16_all = (dm_m * silu_m).astype(BF16)
        d_scores = []
        for h in range(MEM_HEADS):
            sl = slice(HEAD_DIM * h, HEAD_DIM * (h + 1))
            p = heads[h][0]
            dy16 = dy16_all[:, sl]
            dp = _dot_nt(dy16, vm_ref[:, sl])
            dvm_ref[:, sl] += _dot_tn(p.astype(BF16), dy16)
            ds16 = (p * (dp - jnp.sum(dp * p, axis=-1, keepdims=True))).astype(BF16)
            dkm_ref[:, sl] += _dot_tn(ds16, q16[:, sl])
            d_scores.append(_dot(ds16, km_ref[:, sl]))
        dq, dg_rows = _head_rms_bwd(a, r, gq_ref[...], jnp.concatenate(d_scores, axis=1) * 0.125, seg)
        dqm_ref[:, 0:256] = dq.astype(BF16)
        dgq_ref[...] += _fold_heads(jnp.sum(dg_rows, axis=0, keepdims=True), MEM_HEADS)

    piece, before, after = _row_specs(s_len)
    n_halo = s_len // HALO
    row = pl.BlockSpec((TM, D_MODEL), lambda i: (i, 0))
    dm_after = pl.BlockSpec((HALO, 256), lambda i: (jnp.minimum((i + 1) * (TM // HALO), n_halo - 1), 0))
    out_piece = pl.BlockSpec((TM, PIECE), lambda i: (i, 0))
    return pl.pallas_call(
        body,
        name="mix_bwd",
        grid=(n_tiles,),
        in_specs=[piece(0), before(0), after(1), piece(4), piece(5), pl.BlockSpec((FOX_HEADS, TM, 128), lambda i: (0, i, 0)),
                  row, dm_after, _full((N_MEM, 256)), _full((N_MEM, 256)), _full((256, 256)), _full((1, 256)), _full((1, 256)),
                  _full((PIECE, PIECE))],
        out_specs=[out_piece, out_piece, out_piece, pl.BlockSpec((FOX_HEADS, TM, HEAD_DIM), lambda i: (0, i, 0)),
                   pl.BlockSpec((FOX_HEADS, 1, 8, TM), lambda i: (0, i, 0, 0)),
                   _full((256, 256)), _full((1, 256)), _full((N_MEM, 256)), _full((N_MEM, 256)), _full((1, HEAD_DIM))],
        out_shape=[jax.ShapeDtypeStruct((s_len, PIECE), BF16)] * 3 + [
            jax.ShapeDtypeStruct((FOX_HEADS, s_len, HEAD_DIM), BF16),
            jax.ShapeDtypeStruct((FOX_HEADS, n_tiles, 8, TM), F32),
            jax.ShapeDtypeStruct((256, 256), F32), jax.ShapeDtypeStruct((1, 256), F32),
            jax.ShapeDtypeStruct((N_MEM, 256), F32), jax.ShapeDtypeStruct((N_MEM, 256), F32),
            jax.ShapeDtypeStruct((1, HEAD_DIM), F32)],
        compiler_params=_params(("arbitrary",)),
    )(proj, proj, proj, proj, proj, olse, d_mixed, d_mixed, km, vm, wp_bd, pool_scale, gq_m, seg)


def _fox_bwd(q_aug, k_aug, kt_aug, v_h, d_o, lse_rows, delta_rows, dw_kv16, dw_out16):
    s_len = q_aug.shape[1]
    n_tiles = s_len // TA
    n_groups = FOX_HEADS // HB

    def body(q_hbm, k_ref, kt_ref, v_ref, do_hbm, st_ref, dl_ref, dkv16_ref, dout16_ref, dq_ref, dk_ref, dv_ref, land_kv, land_out,
             dqt_ref, q_ref, do_ref, send_sems, recv_sems, local_sems, tile_sems):
        j = pl.program_id(1)
        remote, local = _row_block_exchange(dkv16_ref, dout16_ref, land_kv, land_out, send_sems, recv_sems, local_sems, gather=False)

        def tile_loads(i):
            rows = pl.ds(pl.multiple_of(i * TA, TA), TA)
            return [pltpu.make_async_copy(q_hbm.at[:, rows, :], q_ref.at[:, rows, :], tile_sems.at[2 * i]),
                    pltpu.make_async_copy(do_hbm.at[:, rows, :], do_ref.at[:, rows, :], tile_sems.at[2 * i + 1])]

        @pl.when((pl.program_id(0) == 0) & (j == 0))
        def _():
            for i in range(n_tiles):
                for cp in tile_loads(i):
                    cp.start()
            for cp in remote + local:
                cp.start()

        @pl.when(j == 0)
        def _():
            dqt_ref[...] = jnp.zeros((HB, n_tiles, AUG_ROWS, TA), F32)

        ks = [k_ref[h] for h in range(HB)]
        kts = [kt_ref[h, 0, 0:AUG_ROWS, :] for h in range(HB)]
        vs = [v_ref[h] for h in range(HB)]

        def pair(i0, acc, masked, width):
            @pl.when(j == 0)
            def _():
                for t in range(width):
                    for cp in tile_loads(i0 + t):
                        cp.wait()

            rows = pl.ds(pl.multiple_of(i0 * TA, TA), width * TA)
            qs = [q_ref[h, rows, :] for h in range(HB)]
            d_os = [do_ref[h, rows, :] for h in range(HB)]
            row_stat = lambda ref, h: jnp.concatenate([ref[h, i0 + t, 0:1, :] for t in range(width)], axis=1)
            scores = [(_dot_nt(ks[h], qs[h]), _dot_nt(vs[h], d_os[h])) for h in range(HB)]
            probs = []
            for h in range(HB):
                s, dp = scores[h]
                if masked:
                    s = jnp.where(_causal_mask_t(), s, -1e30)
                p = jnp.exp(s - row_stat(st_ref, h))
                probs.append((p.astype(BF16), (p * (dp - row_stat(dl_ref, h))).astype(BF16)))
            out = []
            for h in range(HB):
                p16, ds16 = probs[h]
                dqt = _dot(kts[h], ds16)
                for t in range(width):
                    dqt_ref[h, i0 + t] += dqt[:, t * TA:(t + 1) * TA]
                out.append((acc[h][0] + _dot(ds16, qs[h]), acc[h][1] + _dot(p16, d_os[h])))
            return tuple(out)

        zero = tuple((jnp.zeros((TA, 128), F32), jnp.zeros((TA, HEAD_DIM), F32)) for _ in range(HB))
        acc = pair(j, zero, True, 1)
        odd = (n_tiles - 1 - j) % 2
        acc = lax.fori_loop(j + 1, j + 1 + odd, lambda i, a: pair(i, a, False, 1), acc)
        acc = lax.fori_loop(0, (n_tiles - 1 - j) // 2, lambda t, a: pair(j + 1 + odd + 2 * t, a, False, 2), acc)
        pad = jnp.zeros((128 - AUG_ROWS, TA), F32)
        for h in range(HB):
            dk_ref[h] = acc[h][0]
            dv_ref[h] = acc[h][1]
            dq_ref[h] = jnp.concatenate([dqt_ref[h, j], pad], axis=0).T

        @pl.when((pl.program_id(0) == n_groups - 1) & (j == n_tiles - 1))
        def _():
            for cp in remote:
                cp.wait_recv()
            for cp in remote:
                cp.wait_send()
            for cp in local:
                cp.wait()

    assert n_groups == 1
    resident = pl.BlockSpec(memory_space=pltpu.VMEM)
    rows_blk = lambda r: resident
    tile = lambda w: pl.BlockSpec((HB, TA, w), lambda g, j: (g, j, 0))
    hbm = pl.BlockSpec(memory_space=pl.ANY)
    return pl.pallas_call(
        body,
        name="fox_bwd",
        grid=(n_groups, n_tiles),
        in_specs=[hbm, tile(128), pl.BlockSpec((HB, 1, 128, TA), lambda g, j: (g, j, 0, 0)), tile(HEAD_DIM),
                  hbm, rows_blk(8), rows_blk(8), hbm, hbm],
        out_specs=[tile(128), tile(128), tile(HEAD_DIM), hbm, hbm],
        out_shape=[jax.ShapeDtypeStruct((FOX_HEADS, s_len, 128), F32), jax.ShapeDtypeStruct((FOX_HEADS, s_len, 128), F32),
                   jax.ShapeDtypeStruct((FOX_HEADS, s_len, HEAD_DIM), F32),
                   jax.ShapeDtypeStruct((N_DEV, 128, 512), BF16), jax.ShapeDtypeStruct((N_DEV, 128, D_MODEL), BF16)],
        scratch_shapes=[pltpu.VMEM((HB, n_tiles, AUG_ROWS, TA), F32),
                        pltpu.VMEM((HB, s_len, 128), BF16), pltpu.VMEM((HB, s_len, HEAD_DIM), BF16),
                        pltpu.SemaphoreType.DMA((14,)), pltpu.SemaphoreType.DMA((14,)), pltpu.SemaphoreType.DMA((2,)),
                        pltpu.SemaphoreType.DMA((2 * n_tiles,))],
        compiler_params=_params(("arbitrary", "arbitrary")),
    )(q_aug, k_aug, kt_aug, v_h, d_o, lse_rows, delta_rows, dw_kv16, dw_out16)


def _fox_post(proj, bf_pad, gq, gk, seg, dq_aug, dk_aug, dv_h):
    s_len = proj.shape[0]
    n_tiles = s_len // TM

    def body(q_ref, k_ref, f_ref, bf_ref, gq_ref, gk_ref, seg_ref, dqa_ref, dka_ref, dvh_ref,
             dq_ref, dk_ref, dv_ref, df_ref, dgq_ref, dgk_ref, dbf_ref, carry_ref):
        @pl.when(pl.program_id(0) == 0)
        def _():
            carry_ref[...] = jnp.zeros((1, 128), F32)
            dgq_ref[...] = jnp.zeros((1, HEAD_DIM), F32)
            dgk_ref[...] = jnp.zeros((1, HEAD_DIM), F32)
            dbf_ref[...] = jnp.zeros((1, 128), F32)

        lane = lax.broadcasted_iota(jnp.int32, (TM, 128), 1)
        seg = seg_ref[...]
        dqas = [dqa_ref[h] for h in range(FOX_HEADS)]
        dkas = [dka_ref[h] for h in range(FOX_HEADS)]
        d_cum = jnp.zeros((TM, 128), F32)
        for h in range(FOX_HEADS):
            d_cum = jnp.where(lane == h, dqas[h][:, 64:65] - dkas[h][:, 67:68], d_cum)
        q_all = q_ref[...]
        k_all = k_ref[...]
        rq = _head_rms(q_all, seg)
        rk = _head_rms(k_all, seg)
        dy_q = jnp.concatenate([t[:, 0:HEAD_DIM] for t in dqas], axis=1) * 0.125
        dy_k = jnp.concatenate([t[:, 0:HEAD_DIM] for t in dkas], axis=1)
        dq, dgq_rows = _head_rms_bwd(q_all * rq, rq, gq_ref[...], dy_q, seg)
        dk, dgk_rows = _head_rms_bwd(k_all * rk, rk, gk_ref[...], dy_k, seg)
        dq_ref[...] = dq.astype(BF16)
        dk_ref[...] = dk.astype(BF16)
        dv_ref[...] = jnp.concatenate([dvh_ref[h] for h in range(FOX_HEADS)], axis=1).astype(BF16)
        dgq_ref[...] += _fold_heads(jnp.sum(dgq_rows, axis=0, keepdims=True), FOX_HEADS)
        dgk_ref[...] += _fold_heads(jnp.sum(dgk_rows, axis=0, keepdims=True), FOX_HEADS)

        row = lax.broadcasted_iota(jnp.int32, (TM, TM), 0)
        col = lax.broadcasted_iota(jnp.int32, (TM, TM), 1)
        tri = jnp.where(col >= row, 1.0, 0.0).astype(BF16)
        hi, mid, lo = _split3(d_cum)
        d_logf = _dot(tri, hi) + _dot(tri, mid) + _dot(tri, lo) + carry_ref[...]
        carry_ref[...] = d_logf[0:1, :]
        z = f_ref[...] + bf_ref[...]
        d_f = jnp.where(lane < FOX_HEADS, d_logf / (1.0 + jnp.exp(z)), 0.0)
        df_ref[...] = d_f.astype(BF16)
        dbf_ref[...] += jnp.sum(d_f, axis=0, keepdims=True)

    rev = lambda i: n_tiles - 1 - i
    col_blk = lambda j: pl.BlockSpec((TM, PIECE), lambda i: (rev(i), j))
    head_blk = lambda w: pl.BlockSpec((FOX_HEADS, TM, w), lambda i: (0, rev(i), 0))
    out_piece = pl.BlockSpec((TM, PIECE), lambda i: (rev(i), 0))
    return pl.pallas_call(
        body,
        name="fox_post",
        grid=(n_tiles,),
        in_specs=[col_blk(1), col_blk(2), pl.BlockSpec((TM, 128), lambda i: (rev(i), MY_F_OFF // 128)),
                  _full((1, 128)), _full((1, PIECE)), _full((1, PIECE)), _full((PIECE, PIECE)),
                  head_blk(128), head_blk(128), head_blk(HEAD_DIM)],
        out_specs=[out_piece, out_piece, out_piece, pl.BlockSpec((TM, 128), lambda i: (rev(i), 0)),
                   _full((1, HEAD_DIM)), _full((1, HEAD_DIM)), _full((1, 128))],
        out_shape=[jax.ShapeDtypeStruct((s_len, PIECE), BF16)] * 3 + [
            jax.ShapeDtypeStruct((s_len, 128), BF16), jax.ShapeDtypeStruct((1, HEAD_DIM), F32),
            jax.ShapeDtypeStruct((1, HEAD_DIM), F32), jax.ShapeDtypeStruct((1, 128), F32)],
        scratch_shapes=[pltpu.VMEM((1, 128), F32)],
        compiler_params=_params(("arbitrary",)),
    )(proj, proj, proj, bf_pad, gq, gk, seg, dq_aug, dk_aug, dv_h)


def _mem_bwd(mem, g, w_kv, gk, dkm, dvm):
    def body(mem_ref, g_ref, w_ref, gk_ref, dkm_ref, dvm_ref, dw_ref, dg_ref, dgk_ref, dkv_ref):
        m = mem_ref[...]
        r = _rms(m)
        a = m * r
        mn16 = (a * g_ref[...]).astype(BF16)
        kv = _dot(mn16, w_ref[...])
        dgk = jnp.zeros((N_MEM, HEAD_DIM), F32)
        for h in range(MEM_HEADS):
            sl = slice(HEAD_DIM * h, HEAD_DIM * (h + 1))
            kh = kv[:, sl]
            rk = _rms(kh)
            dk, dg_rows = _rms_bwd(kh * rk, rk, gk_ref[...], dkm_ref[:, sl])
            dkv_ref[:, sl] = dk.astype(BF16)
            dgk = dgk + dg_rows
        dkv_ref[:, 256:512] = dvm_ref[...].astype(BF16)
        dgk_ref[...] = jnp.sum(dgk, axis=0, keepdims=True)
        dkv16 = dkv_ref[...]
        dw_ref[...] = _dot_tn(mn16, dkv16).astype(BF16)
        dg_ref[...] = jnp.sum(_dot_nt(dkv16, w_ref[...]) * a, axis=0, keepdims=True)

    return pl.pallas_call(
        body,
        name="mem_bwd",
        out_shape=(jax.ShapeDtypeStruct((D_MODEL, 512), BF16), jax.ShapeDtypeStruct((1, D_MODEL), F32),
                   jax.ShapeDtypeStruct((1, HEAD_DIM), F32)),
        in_specs=[pl.BlockSpec(memory_space=pltpu.VMEM)] * 6,
        out_specs=(pl.BlockSpec(memory_space=pltpu.VMEM),) * 3,
        scratch_shapes=[pltpu.VMEM((N_MEM, 512), BF16)],
        compiler_params=pltpu.CompilerParams(vmem_limit_bytes=VMEM_LIMIT),
    )(mem, g, w_kv, gk, dkm, dvm)


def _grad_x_exchange(pieces, d_f, w_my, x, norm_g, d_out, dw_in, land_kv, land_out, d_mem_g, d_scale, d_bf, d_gq, d_gk, d_gqm,
                     d_gkm, dwp_bd, loss_blk):
    s_len = x.shape[0]
    n_steps = s_len // TM
    step2, step3 = n_steps // 4, (11 * n_steps) // 16
    half = D_MODEL // 2

    def body(p0, p1, p2, p3, p4, p5, df_ref, x_ref, dout_ref, w_ref, g_ref, in_ref, lkv_ref, lout_ref,
             dmg, dsc, dbf, dgq, dgk, dgqm, dgkm, dwp, loss_ref,
             gx_ref, rin_ref, rkv_ref, rout_ref, sred_ref,
             dng, land1, send2a, send2b, keep2a, keep2b, land2a, land2b, send3a, send3b, land3a, land3b, sb_ref, sland,
             own4, lkv_v, lout_v, send_sems, recv_sems, load_sems):
        i = pl.program_id(0)
        x_, y_, c_, me = _my_place()
        sibling, x_nbr, y_nbr = (x_, y_, 1 - c_), (1 - x_, y_, c_), (x_, 1 - y_, c_)
        loads = [pltpu.make_async_copy(in_ref.at[2 * k + c_], own4.at[k], load_sems.at[k]) for k in range(4)]
        loads += [pltpu.make_async_copy(lkv_ref, lkv_v, load_sems.at[4]), pltpu.make_async_copy(lout_ref, lout_v, load_sems.at[5])]

        def rcopy(src, dst, sem, to):
            return pltpu.make_async_remote_copy(src_ref=src, dst_ref=dst, send_sem=send_sems.at[sem], recv_sem=recv_sems.at[sem],
                                                device_id=to, device_id_type=MESH)

        early_rows, late_rows = pl.ds(8, SB_ROWS - 8), pl.ds(0, 8)
        small_early, small_late = [], []
        for k in range(1, N_DEV):
            peer = (x_ ^ (k >> 2), y_ ^ ((k >> 1) & 1), c_ ^ (k & 1))
            small_early.append(rcopy(sb_ref.at[early_rows], sland.at[me, early_rows], k - 1, peer))
            small_late.append(rcopy(sb_ref.at[late_rows], sland.at[me, late_rows], 16 + k, peer))
        small = small_early + small_late
        stage1 = [rcopy(in_ref.at[2 * k + 1 - c_], land1.at[k], 7 + k, sibling) for k in range(4)]
        stage2 = [rcopy(send2a.at[j], land2a.at[j], 11 + j, x_nbr) for j in range(2)]
        stage2 += [rcopy(send2b.at[j], land2b.at[j], 13 + j, y_nbr) for j in range(2)]
        stage3 = [rcopy(send3a, land3a, 15, y_nbr), rcopy(send3b, land3b, 16, x_nbr)]
        top, bot = slice(0, half), slice(half, D_MODEL)

        @pl.when(i == 0)
        def _():
            dng[...] = jnp.zeros((1, D_MODEL), F32)
            for cp in stage1 + loads:
                cp.start()
            sb_ref[...] = jnp.zeros((SB_ROWS, SB_W), F32)
            sb_ref[SB_POOL_SCALE:SB_POOL_SCALE + 1, :] = dsc[...]
            sb_ref[SB_B_F:SB_B_F + 1, 0:128] = dbf[...]
            for row, ref in ((SB_FOX_Q, dgq), (SB_FOX_K, dgk), (SB_MEM_Q, dgqm), (SB_MEM_K, dgkm)):
                sb_ref[row:row + 1, 0:HEAD_DIM] = ref[...]
            sb_ref[SB_LOSS:SB_LOSS + 1, 0:128] = loss_ref[0:1, :]
            for grp in range(4):
                sl = slice(64 * grp, 64 * (grp + 1))
                sb_ref[SB_W_POOL:SB_W_POOL + 64, sl] = dwp[sl, sl]
            for cp in small_early:
                cp.start()

        @pl.when(i == step2)
        def _():
            for cp in stage1:
                cp.wait_recv()
            for cp in loads[0:4]:
                cp.wait()

            def chip_sum(k, cols):
                return own4[k, :, cols].astype(F32) + land1[k, :, cols].astype(F32)

            for j in range(2):
                send2a[j] = chip_sum(2 * (1 - x_) + j, top).astype(BF16)
                keep2a[j] = chip_sum(2 * x_ + j, top)
                send2b[j] = chip_sum(2 * j + 1 - y_, bot).astype(BF16)
                keep2b[j] = chip_sum(2 * j + y_, bot)
            for cp in stage2:
                cp.start()

        @pl.when(i == step3)
        def _():
            for cp in stage2:
                cp.wait_recv()
            for j in range(2):
                keep2a[j] = keep2a[j] + land2a[j].astype(F32)
                keep2b[j] = keep2b[j] + land2b[j].astype(F32)
            send3a[...] = keep2a[1 - y_].astype(BF16)
            send3b[...] = keep2b[1 - x_].astype(BF16)
            for cp in stage3:
                cp.start()

        dh = _dot_nt(df_ref[...], w_ref[:, MY_F_OFF:MY_WIDTH])
        for j, p in enumerate((p0, p1, p2, p3, p4, p5)):
            dh = dh + _dot_nt(p[...], w_ref[:, PIECE * j:PIECE * (j + 1)])
        xv = x_ref[...]
        r = _rms(xv)
        dx, dg_rows = _rms_bwd(xv * r, r, g_ref[...], dh)
        gx_ref[...] = dout_ref[...] + dx
        dng[...] += jnp.sum(dg_rows, axis=0, keepdims=True)

        @pl.when(i == n_steps - 1)
        def _():
            for k in range(4):
                sb_ref[SB_NORM_G + k:SB_NORM_G + k + 1, :] = dng[:, SB_W * k:SB_W * (k + 1)]
                sb_ref[SB_MEM_NORM_G + k:SB_MEM_NORM_G + k + 1, :] = dmg[:, SB_W * k:SB_W * (k + 1)]
            for cp in small_late:
                cp.start()
            sland[me] = sb_ref[...]
            for cp in stage3:
                cp.wait_recv()
            rin_ref[:, top] = keep2a[y_] + land3a[...].astype(F32)
            rin_ref[:, bot] = keep2b[x_] + land3b[...].astype(F32)
            for cp in small:
                cp.wait_recv()
            for cp in small + stage1 + stage2 + stage3:
                cp.wait_send()
            for cp in loads[4:6]:
                cp.wait()
            for land, red in ((lkv_v, rkv_ref), (lout_v, rout_ref), (sland, sred_ref)):
                acc = land[0].astype(F32)
                for d in range(1, N_DEV):
                    acc = acc + land[d].astype(F32)
                red[...] = acc

    piece = pl.BlockSpec((TM, PIECE), lambda i: (i, 0))
    row = pl.BlockSpec((TM, D_MODEL), lambda i: (i, 0))
    vmem = pl.BlockSpec(memory_space=pltpu.VMEM)
    half_chunk = lambda n, dt: pltpu.VMEM((n, CHUNK_ROWS, half), dt)
    whole = (w_my, norm_g, dw_in, land_kv, land_out, d_mem_g, d_scale, d_bf, d_gq, d_gk, d_gqm, d_gkm, dwp_bd, loss_blk)
    return pl.pallas_call(
        body,
        name="grad_x_exchange",
        grid=(n_steps,),
        in_specs=[piece] * 6 + [pl.BlockSpec((TM, 128), lambda i: (i, 0)), row, row, vmem, vmem]
        + [pl.BlockSpec(memory_space=pl.ANY)] * 3 + [vmem] * (len(whole) - 5),
        out_specs=[row, vmem, vmem, vmem, vmem],
        out_shape=[jax.ShapeDtypeStruct((s_len, D_MODEL), F32), jax.ShapeDtypeStruct((CHUNK_ROWS, D_MODEL), F32),
                   jax.ShapeDtypeStruct((128, 512), F32), jax.ShapeDtypeStruct((128, D_MODEL), F32),
                   jax.ShapeDtypeStruct((SB_ROWS, SB_W), F32)],
        scratch_shapes=[
            pltpu.VMEM((1, D_MODEL), F32),
            pltpu.VMEM((4, CHUNK_ROWS, D_MODEL), BF16),
            half_chunk(2, BF16), half_chunk(2, BF16), half_chunk(2, F32), half_chunk(2, F32), half_chunk(2, BF16), half_chunk(2, BF16),
            pltpu.VMEM((CHUNK_ROWS, half), BF16), pltpu.VMEM((CHUNK_ROWS, half), BF16),
            pltpu.VMEM((CHUNK_ROWS, half), BF16), pltpu.VMEM((CHUNK_ROWS, half), BF16),
            pltpu.VMEM((SB_ROWS, SB_W), F32),
            pltpu.VMEM((N_DEV, SB_ROWS, SB_W), F32),
            pltpu.VMEM((4, CHUNK_ROWS, D_MODEL), BF16),
            pltpu.VMEM((N_DEV, 128, 512), BF16),
            pltpu.VMEM((N_DEV, 128, D_MODEL), BF16),
            pltpu.SemaphoreType.DMA((24,)),
            pltpu.SemaphoreType.DMA((24,)),
            pltpu.SemaphoreType.DMA((6,)),
        ],
        compiler_params=_params(("arbitrary",)),
    )(*pieces, d_f, x, d_out, *whole)


def _grad_w_in(h16, pieces, d_f):
    s_len = h16.shape[0]
    tk = 512

    def body(h_ref, p0, p1, p2, p3, p4, p5, df_ref, out_ref, dw_ref):
        @pl.when(pl.program_id(0) == 0)
        def _():
            dw_ref[...] = jnp.zeros((D_MODEL, MY_WIDTH), F32)

        h = h_ref[...]
        for j, p in enumerate((p0, p1, p2, p3, p4, p5)):
            dw_ref[:, PIECE * j:PIECE * (j + 1)] += _dot_tn(h, p[...])
        dw_ref[:, MY_F_OFF:MY_WIDTH] += _dot_tn(h, df_ref[...])

        @pl.when(pl.program_id(0) == pl.num_programs(0) - 1)
        def _():
            for d in range(N_DEV):
                parts = [dw_ref[:, start:start + width] for start, width in _chunk_segments(d)]
                parts.append(jnp.zeros((D_MODEL, 512 - IN_CHUNK), F32))
                out_ref[d] = jnp.concatenate(parts, axis=1).T[0:CHUNK_ROWS].astype(BF16)

    piece = pl.BlockSpec((tk, PIECE), lambda i: (i, 0))
    return pl.pallas_call(
        body,
        name="grad_w_in",
        grid=(s_len // tk,),
        in_specs=[pl.BlockSpec((tk, D_MODEL), lambda i: (i, 0))] + [piece] * 6 + [pl.BlockSpec((tk, 128), lambda i: (i, 0))],
        out_specs=_full((N_DEV, CHUNK_ROWS, D_MODEL)),
        out_shape=jax.ShapeDtypeStruct((N_DEV, CHUNK_ROWS, D_MODEL), BF16),
        scratch_shapes=[pltpu.VMEM((D_MODEL, MY_WIDTH), F32)],
        compiler_params=_params(("arbitrary",)),
    )(h16, *pieces, d_f)


def _adamw(w, g, m, v):
    m = ADAM_B1 * m + (1.0 - ADAM_B1) * g
    v = ADAM_B2 * v + (1.0 - ADAM_B2) * (g * g)
    m_hat = m / (1.0 - ADAM_B1 ** ADAM_STEP)
    v_hat = v / (1.0 - ADAM_B2 ** ADAM_STEP)
    return -ADAM_LR * (m_hat / (jnp.sqrt(v_hat) + ADAM_EPS) + ADAM_WD * w), m, v


PARAM_ORDER = ("norm_g", "w_in", "b_f", "w_pool", "pool_scale", "fox_q_g", "fox_k_g", "mem_norm_g", "w_mem_kv", "mem_q_g", "mem_k_g", "w_out")


def _update(red_in, red_kv, red_out, sred, params):
    def body(rin_ref, rkv_ref, rout_ref, sred_ref, *refs):
        ins, outs = refs[:3 * len(PARAM_ORDER)], refs[3 * len(PARAM_ORDER):]
        wide = lambda row: jnp.concatenate([sred_ref[row + k:row + k + 1, :] for k in range(4)], axis=1)
        head = lambda row: sred_ref[row:row + 1, 0:HEAD_DIM]
        grads = {
            "norm_g": lambda: wide(SB_NORM_G), "w_in": lambda: rin_ref[...], "b_f": lambda: sred_ref[SB_B_F:SB_B_F + 1, 0:FOX_HEADS],
            "pool_scale": lambda: sred_ref[SB_POOL_SCALE:SB_POOL_SCALE + 1, :], "fox_q_g": lambda: head(SB_FOX_Q),
            "fox_k_g": lambda: head(SB_FOX_K), "mem_norm_g": lambda: wide(SB_MEM_NORM_G), "w_mem_kv": lambda: rkv_ref[...],
            "mem_q_g": lambda: head(SB_MEM_Q), "mem_k_g": lambda: head(SB_MEM_K), "w_out": lambda: rout_ref[...],
        }
        for p, name in enumerate(PARAM_ORDER):
            w_ref, m_ref, v_ref = ins[3 * p:3 * p + 3]
            g_out, d_out, m_out, v_out = outs[4 * p:4 * p + 4]
            if name == "w_pool":
                for grp in range(4):
                    g = sred_ref[SB_W_POOL:SB_W_POOL + 64, 64 * grp:64 * (grp + 1)]
                    delta, m_new, v_new = _adamw(w_ref[grp], g, m_ref[grp], v_ref[grp])
                    g_out[grp], d_out[grp], m_out[grp], v_out[grp] = g, delta, m_new, v_new
            elif name == "w_in":
                for j in range(8):
                    rows = pl.ds(j, IN_CHUNK, stride=8)
                    g = rin_ref[0:IN_CHUNK, 128 * j:128 * (j + 1)]
                    delta, m_new, v_new = _adamw(w_ref[rows, :], g, m_ref[rows, :], v_ref[rows, :])
                    g_out[rows, :], d_out[rows, :], m_out[rows, :], v_out[rows, :] = g, delta, m_new, v_new
            else:
                g = grads[name]()
                delta, m_new, v_new = _adamw(w_ref[...], g, m_ref[...], v_ref[...])
                g_out[...], d_out[...], m_out[...], v_out[...] = g, delta, m_new, v_new

    flat = [t for name in PARAM_ORDER for t in params[name]]
    shapes = [params[name][0].shape for name in PARAM_ORDER for _ in range(4)]
    outs = pl.pallas_call(
        body,
        name="adamw_update",
        out_shape=tuple(jax.ShapeDtypeStruct(s, F32) for s in shapes),
        in_specs=[pl.BlockSpec(memory_space=pltpu.VMEM)] * (4 + len(flat)),
        out_specs=(pl.BlockSpec(memory_space=pltpu.VMEM),) * len(shapes),
        compiler_params=pltpu.CompilerParams(vmem_limit_bytes=VMEM_LIMIT),
    )(red_in, red_kv, red_out, sred, *flat)
    return {name: outs[4 * p:4 * p + 4] for p, name in enumerate(PARAM_ORDER)}


def kernel(x, mem, norm_g, w_in, b_f, w_pool, pool_scale, fox_q_g, fox_k_g, mem_norm_g, w_mem_kv, mem_q_g, mem_k_g, w_out, loss_target, m_norm_g, m_w_in, m_b_f, m_w_pool, m_pool_scale, m_fox_q_g, m_fox_k_g, m_mem_norm_g, m_w_mem_kv, m_mem_q_g, m_mem_k_g, m_w_out, v_norm_g, v_w_in, v_b_f, v_w_pool, v_pool_scale, v_fox_q_g, v_fox_k_g, v_mem_norm_g, v_w_mem_kv, v_mem_q_g, v_mem_k_g, v_w_out):
    given = dict(norm_g=(norm_g, m_norm_g, v_norm_g), w_in=(w_in, m_w_in, v_w_in), b_f=(b_f, m_b_f, v_b_f),
                 w_pool=(w_pool, m_w_pool, v_w_pool), pool_scale=(pool_scale, m_pool_scale, v_pool_scale),
                 fox_q_g=(fox_q_g, m_fox_q_g, v_fox_q_g), fox_k_g=(fox_k_g, m_fox_k_g, v_fox_k_g),
                 mem_norm_g=(mem_norm_g, m_mem_norm_g, v_mem_norm_g), w_mem_kv=(w_mem_kv, m_w_mem_kv, v_w_mem_kv),
                 mem_q_g=(mem_q_g, m_mem_q_g, v_mem_q_g), mem_k_g=(mem_k_g, m_mem_k_g, v_mem_k_g), w_out=(w_out, m_w_out, v_w_out))
    params = {name: tuple(t[0] if t.ndim > 2 else t for t in wmv) for name, wmv in given.items()}
    params["w_in"] = tuple(t.T.reshape(IN_CHUNK * 8, 128) for t in params["w_in"])
    x2, mem2, target2 = x[0], mem[0], loss_target[0]

    seg = jnp.asarray(np.kron(np.eye(FOX_HEADS), np.full((HEAD_DIM, HEAD_DIM), 1.0 / HEAD_DIM)), BF16)
    w_my, w_kv16, w_out16, wp_bd, bf_pad, gq8, gk8, gqm4 = _gather_w_in(
        params["w_in"][0], params["w_mem_kv"][0], params["w_out"][0], params["w_pool"][0], b_f, fox_q_g, fox_k_g, mem_q_g)
    proj, h16 = _fwd_proj(x2, norm_g, w_my)
    q_aug, k_aug, v_h, kt_aug, vt_aug = _fox_prep(proj, bf_pad, gq8, gk8, seg)
    olse, lse_rows, w_kv_all, w_out_all = _fox_fwd(q_aug, k_aug, vt_aug, w_kv16, w_out16)
    km, vm = _mem_prep(mem2, mem_norm_g, w_kv_all, mem_k_g)
    mixed = _mix_fwd(proj, olse, km, vm, wp_bd, pool_scale, gqm4, seg)
    d_out, d_mixed, dw_out, loss_blk = _out_loss(mixed, x2, target2, w_out_all)

    d_ua, d_gb, d_qm, d_o, delta_rows, dwp_bd, d_scale, dkm, dvm, d_gqm = _mix_bwd(proj, olse, d_mixed, km, vm, wp_bd, pool_scale,
                                                                                    gqm4, seg)
    dw_kv, d_mem_g, d_gkm = _mem_bwd(mem2, mem_norm_g, w_kv_all, mem_k_g, dkm, dvm)
    dq_aug, dk_aug, dv_h, land_kv, land_out = _fox_bwd(q_aug, k_aug, kt_aug, v_h, d_o, lse_rows, delta_rows, dw_kv, dw_out)
    d_q, d_k, d_v, d_f, d_gq, d_gk, d_bf = _fox_post(proj, bf_pad, gq8, gk8, seg, dq_aug, dk_aug, dv_h)
    pieces = (d_ua, d_q, d_k, d_v, d_gb, d_qm)
    dw_in = _grad_w_in(h16, pieces, d_f)
    grad_x, red_in, red_kv, red_out, sred = _grad_x_exchange(pieces, d_f, w_my, x2, norm_g, d_out, dw_in, land_kv, land_out, d_mem_g,
                                                             d_scale, d_bf, d_gq, d_gk, d_gqm, d_gkm, dwp_bd, loss_blk)
    new = _update(red_in, red_kv, red_out, sred, params)
    result = [sred[SB_LOSS, 0], grad_x[None]]
    for kind in range(4):
        for name in PARAM_ORDER:
            out = new[name][kind]
            if name == "w_in":
                out = out.reshape(IN_CHUNK, D_MODEL).T
            result.append(out[None] if given[name][0].ndim > 2 else out)
    return tuple(result)
```

```python
import functools

import jax
import jax.numpy as jnp
import numpy as np
from jax import lax
from jax.experimental import pallas as pl
from jax.experimental.pallas import tpu as pltpu

F32 = jnp.float32
BF16 = jnp.bfloat16
MESH = pl.DeviceIdType.MESH

N_DEV = 8
D_MODEL = 1024
HEAD_DIM = 64
FOX_HEADS = 8
MEM_HEADS = 4
N_MEM = 256
POOL_WIDTH = 256
EPS = 1e-6
IN_WIDTH = 3080
IN_CHUNK = IN_WIDTH // N_DEV
CHUNK_ROWS = 400
F_OFF = 2048
MY_WIDTH = 3200
MY_F_OFF = 3072
PIECE = 512
TM = 256
TMM = 512
TA = 256
HB = 8
HB_FWD = 8
AUG_ROWS = 72
HALO = 16
VMEM_LIMIT = 56 * 1024 * 1024

ADAM_LR = 0.001
ADAM_B1 = 0.9
ADAM_B2 = 0.999
ADAM_EPS = 1e-08
ADAM_WD = 0.01
ADAM_STEP = 10


def _dot(a, b):
    return jnp.dot(a, b, preferred_element_type=F32)


def _dot_nt(a, b):
    return lax.dot_general(a, b, (((1,), (1,)), ((), ())), preferred_element_type=F32)


def _dot_tn(a, b):
    return lax.dot_general(a, b, (((0,), (0,)), ((), ())), preferred_element_type=F32)


def _sigmoid(g):
    return 0.5 + 0.5 * jnp.tanh(0.5 * g)


def _split3(v):
    hi = v.astype(BF16)
    r1 = v - hi.astype(F32)
    mid = r1.astype(BF16)
    lo = (r1 - mid.astype(F32)).astype(BF16)
    return hi, mid, lo


def _rms(v):
    return lax.rsqrt(jnp.mean(v * v, axis=-1, keepdims=True) + EPS)


def _rms_bwd(a, r, g, dy):
    da = dy * g
    return r * (da - a * jnp.mean(da * a, axis=-1, keepdims=True)), dy * a


def _head_mean(z, seg):
    hi = z.astype(BF16)
    lo = (z - hi.astype(F32)).astype(BF16)
    if z.shape[1] == 256:
        return _dot(hi, seg) + _dot(lo, seg)
    half = seg[0:256, 0:256]
    return jnp.concatenate([_dot(hi[:, 0:256], half) + _dot(lo[:, 0:256], half),
                            _dot(hi[:, 256:512], half) + _dot(lo[:, 256:512], half)], axis=1)


def _head_rms(v, seg):
    return lax.rsqrt(_head_mean(v * v, seg) + EPS)


def _head_rms_bwd(a, r, g, dy, seg):
    da = dy * g
    return r * (da - a * _head_mean(da * a, seg)), dy * a


def _fold_heads(row, n_heads):
    out = row[:, 0:HEAD_DIM]
    for h in range(1, n_heads):
        out = out + row[:, HEAD_DIM * h:HEAD_DIM * (h + 1)]
    return out


def _params(sem, limit=VMEM_LIMIT):
    return pltpu.CompilerParams(dimension_semantics=sem, vmem_limit_bytes=limit)


def _full(shape):
    return pl.BlockSpec(shape, lambda *_: (0,) * len(shape))


def _chunk_segments(d):
    runs = []
    for lo, hi, shift in ((0, F_OFF, 0), (F_OFF, F_OFF + FOX_HEADS, MY_F_OFF - F_OFF), (F_OFF + FOX_HEADS, IN_WIDTH, -FOX_HEADS)):
        a, b = max(lo, IN_CHUNK * d), min(hi, IN_CHUNK * (d + 1))
        if a < b:
            runs.append((a + shift, b - a))
    return runs


def _my_place():
    x, y, c = lax.axis_index("x"), lax.axis_index("y"), lax.axis_index("c")
    return x, y, c, 4 * x + 2 * y + c


def _shard_rows(dev):
    return pl.ds(pl.multiple_of(128 * dev, 128), 128)


def _gather_w_in(w_in, w_kv, w_out, w_pool, b_f, gq, gk, gqm):
    def body(win_ref, wkv_ref, wout_ref, wp_ref, bf_ref, gq_ref, gk_ref, gqm_ref, wmy_ref, kv16_ref, out16_ref, wpbd_ref, bfpad_ref,
             gq8_ref, gk8_ref, gqm4_ref, in_all, pay_in, win_rows, send_sems, recv_sems, load_sem):
        x, y, c, me = _my_place()
        here, sibling = (x, y, c), (x, y, 1 - c)
        x_chip, y_chip, far_chip = (1 - x, y), (x, 1 - y), (1 - x, 1 - y)
        relay_from = (x ^ (1 - c), y ^ c)
        relay_to = (x ^ c, y ^ (1 - c))

        load = pltpu.make_async_copy(win_ref, win_rows, load_sem)
        load.start()
        load.wait()
        pay_in[...] = jnp.zeros((CHUNK_ROWS, D_MODEL), BF16)
        for j in range(8):
            pay_in[0:IN_CHUNK, 128 * j:128 * (j + 1)] = win_rows[pl.ds(j, IN_CHUNK, stride=8), :].astype(BF16)

        def copy(k, block, to, own=False):
            px, py, pc = block
            dst = in_all.at[4 * px + 2 * py + pc]
            return pltpu.make_async_remote_copy(src_ref=pay_in if own else dst, dst_ref=dst, send_sem=send_sems.at[k],
                                                recv_sem=recv_sems.at[k], device_id=to, device_id_type=MESH)

        first = [copy(0, here, sibling, own=True), copy(1, here, (*x_chip, c), own=True), copy(2, here, (*y_chip, c), own=True)]
        for cp in first:
            cp.start()
        in_all[me] = pay_in[...]
        kv16_ref[...] = wkv_ref[...].astype(BF16)
        out16_ref[...] = wout_ref[...].astype(BF16)
        wpbd_ref[...] = jnp.zeros((POOL_WIDTH, POOL_WIDTH), BF16)
        for grp in range(4):
            sl = slice(64 * grp, 64 * (grp + 1))
            wpbd_ref[sl, sl] = wp_ref[grp].astype(BF16)
        bfpad_ref[...] = jnp.zeros((1, 128), F32)
        bfpad_ref[:, 0:FOX_HEADS] = bf_ref[...]
        gq8_ref[...] = jnp.concatenate([gq_ref[...]] * FOX_HEADS, axis=1)
        gk8_ref[...] = jnp.concatenate([gk_ref[...]] * FOX_HEADS, axis=1)
        gqm4_ref[...] = jnp.concatenate([gqm_ref[...]] * MEM_HEADS, axis=1)
        copy(1 + c, (*relay_from, c), here).wait_recv()
        passed = [copy(3, (*relay_from, c), (*relay_to, c)), copy(4, (*relay_from, c), sibling)]
        for cp in passed:
            cp.start()
        copy(2 - c, (*relay_to, c), here).wait_recv()
        passed.append(copy(5, (*relay_to, c), sibling))
        passed[-1].start()
        copy(3, (*far_chip, c), here).wait_recv()
        passed.append(copy(6, (*far_chip, c), sibling))
        passed[-1].start()
        copy(0, sibling, here).wait_recv()
        for k, chip in ((4, relay_to), (5, relay_from), (6, far_chip)):
            copy(k, (*chip, 1 - c), here).wait_recv()
        for cp in first + passed:
            cp.wait_send()

        row_pad = jnp.zeros((512 - CHUNK_ROWS, 256), F32)
        for r0 in range(0, D_MODEL, 256):
            ch = [jnp.concatenate([in_all[d, :, r0:r0 + 256].astype(F32), row_pad], axis=0).T[:, 0:IN_CHUNK] for d in range(N_DEV)]
            lo = F_OFF - 5 * IN_CHUNK
            full = jnp.concatenate(ch[:5] + [ch[5][:, :lo], ch[5][:, lo + FOX_HEADS:], ch[6], ch[7], ch[5][:, lo:lo + FOX_HEADS],
                                             jnp.zeros((256, MY_WIDTH - IN_WIDTH), F32)], axis=1)
            wmy_ref[r0:r0 + 256, :] = full.astype(BF16)

    return pl.pallas_call(
        body,
        name="gather_w_in",
        out_shape=(jax.ShapeDtypeStruct((D_MODEL, MY_WIDTH), BF16), jax.ShapeDtypeStruct((128, 512), BF16),
                   jax.ShapeDtypeStruct((128, D_MODEL), BF16), jax.ShapeDtypeStruct((POOL_WIDTH, POOL_WIDTH), BF16),
                   jax.ShapeDtypeStruct((1, 128), F32), jax.ShapeDtypeStruct((1, PIECE), F32), jax.ShapeDtypeStruct((1, PIECE), F32),
                   jax.ShapeDtypeStruct((1, 256), F32)),
        in_specs=[pl.BlockSpec(memory_space=pl.ANY)] + [pl.BlockSpec(memory_space=pltpu.VMEM)] * 7,
        out_specs=(pl.BlockSpec(memory_space=pltpu.VMEM),) * 8,
        scratch_shapes=[
            pltpu.VMEM((N_DEV, CHUNK_ROWS, D_MODEL), BF16),
            pltpu.VMEM((CHUNK_ROWS, D_MODEL), BF16),
            pltpu.VMEM((IN_CHUNK * 8, 128), F32),
            pltpu.SemaphoreType.DMA((7,)),
            pltpu.SemaphoreType.DMA((7,)),
            pltpu.SemaphoreType.DMA,
        ],
        compiler_params=pltpu.CompilerParams(vmem_limit_bytes=VMEM_LIMIT),
    )(w_in, w_kv, w_out, w_pool, b_f, gq, gk, gqm)


def _row_block_exchange(kv_ref, out_ref, kv_dst, out_dst, send_sems, recv_sems, local_sems, gather):
    x, y, c, me = _my_place()
    remote = []
    for k in range(1, N_DEV):
        px, py, pc = x ^ (k >> 2), y ^ ((k >> 1) & 1), c ^ (k & 1)
        peer = 4 * px + 2 * py + pc
        for fam, (src, dst) in enumerate(((kv_ref, kv_dst), (out_ref, out_dst))):
            remote.append(pltpu.make_async_remote_copy(
                src_ref=src if gather else src.at[_shard_rows(peer)], dst_ref=dst.at[_shard_rows(me)] if gather else dst.at[me],
                send_sem=send_sems.at[7 * fam + k - 1], recv_sem=recv_sems.at[7 * fam + k - 1],
                device_id=(px, py, pc), device_id_type=MESH))
    local = [pltpu.make_async_copy(src if gather else src.at[_shard_rows(me)], dst.at[_shard_rows(me)] if gather else dst.at[me],
                                   local_sems.at[fam]) for fam, (src, dst) in enumerate(((kv_ref, kv_dst), (out_ref, out_dst)))]
    return remote, local


SB_ROWS, SB_W = 80, 256
SB_NORM_G, SB_MEM_NORM_G, SB_POOL_SCALE, SB_B_F, SB_FOX_Q, SB_FOX_K, SB_MEM_Q, SB_MEM_K, SB_LOSS, SB_W_POOL = 0, 4, 8, 9, 10, 11, 12, 13, 14, 16


def _fwd_proj(x, norm_g, w_my):
    s_len = x.shape[0]

    def body(x_ref, g_ref, w_ref, proj_ref, h_ref):
        xv = x_ref[...]
        h = (xv * _rms(xv) * g_ref[...]).astype(BF16)
        h_ref[...] = h
        proj_ref[...] = _dot(h, w_ref[...])

    return pl.pallas_call(
        body,
        name="fwd_proj",
        grid=(s_len // TMM,),
        in_specs=[pl.BlockSpec((TMM, D_MODEL), lambda i: (i, 0)), _full((1, D_MODEL)), _full((D_MODEL, MY_WIDTH))],
        out_specs=[pl.BlockSpec((TMM, MY_WIDTH), lambda i: (i, 0)), pl.BlockSpec((TMM, D_MODEL), lambda i: (i, 0))],
        out_shape=[jax.ShapeDtypeStruct((s_len, MY_WIDTH), F32), jax.ShapeDtypeStruct((s_len, D_MODEL), BF16)],
        compiler_params=_params(("parallel",)),
    )(x, norm_g, w_my)


def _log_sigmoid(z):
    return jnp.minimum(z, 0.0) - jnp.log(1.0 + jnp.exp(-jnp.abs(z)))


def _forget_lane_maps():
    to_q = np.zeros((128, FOX_HEADS * 128), np.float32)
    to_k = np.zeros((128, FOX_HEADS * 128), np.float32)
    for h in range(FOX_HEADS):
        for term in range(3):
            to_q[8 * term + h, 128 * h + 64 + term] = 1.0
            to_q[24, 128 * h + 67 + term] = 1.0
            to_k[24, 128 * h + 64 + term] = 1.0
            to_k[8 * term + h, 128 * h + 67 + term] = -1.0
    return jnp.asarray(to_q, BF16), jnp.asarray(to_k, BF16)


def _fox_prep(proj, bf_pad, gq, gk, seg):
    s_len = proj.shape[0]
    to_q, to_k = _forget_lane_maps()

    def body(q_ref, k_ref, v_ref, f_ref, bf_ref, gq_ref, gk_ref, seg_ref, eq_ref, ek_ref,
             qa_ref, ka_ref, vh_ref, kt_ref, vt_ref, carry_ref):
        @pl.when(pl.program_id(0) == 0)
        def _():
            carry_ref[...] = jnp.zeros((1, 128), F32)

        lane = lax.broadcasted_iota(jnp.int32, (TM, 128), 1)
        logf = jnp.where(lane < FOX_HEADS, _log_sigmoid(f_ref[...] + bf_ref[...]), 0.0)
        row = lax.broadcasted_iota(jnp.int32, (TM, TM), 0)
        col = lax.broadcasted_iota(jnp.int32, (TM, TM), 1)
        tri = jnp.where(col <= row, 1.0, 0.0).astype(BF16)
        hi, mid, lo = _split3(logf)
        cum = _dot(tri, hi) + _dot(tri, mid) + _dot(tri, lo) + carry_ref[...]
        carry_ref[...] = cum[TM - 1:TM, :]
        f_hi, f_mid, f_lo = [t.astype(F32) for t in _split3(cum)]
        packed = (f_hi + pltpu.roll(f_mid, 8, 1) + pltpu.roll(f_lo, 16, 1) + jnp.where(lane == 24, 1.0, 0.0)).astype(BF16)
        extra_q = _dot(packed, eq_ref[...])
        extra_k = _dot(packed, ek_ref[...])
        one_at_64 = jnp.where(lane == 64, 1.0, 0.0)
        zeros = jnp.zeros((TM, HEAD_DIM), F32)
        q_all = q_ref[...]
        k_all = k_ref[...]
        qn_all = q_all * _head_rms(q_all, seg_ref[...]) * gq_ref[...] * 0.125
        kn_all = k_all * _head_rms(k_all, seg_ref[...]) * gk_ref[...]
        for h in range(FOX_HEADS):
            sl = slice(HEAD_DIM * h, HEAD_DIM * (h + 1))
            tile = slice(128 * h, 128 * (h + 1))
            qa = jnp.where(lane < 64, jnp.concatenate([qn_all[:, sl], zeros], axis=1), extra_q[:, tile])
            ka = jnp.where(lane < 64, jnp.concatenate([kn_all[:, sl], zeros], axis=1), extra_k[:, tile])
            vh = v_ref[:, sl]
            v_aug = jnp.where(lane < 64, jnp.concatenate([vh, zeros], axis=1), one_at_64)
            qa_ref[h] = qa.astype(BF16)
            ka_ref[h] = ka.astype(BF16)
            vh_ref[h] = vh.astype(BF16)
            kt_ref[h, 0] = ka.T.astype(BF16)
            vt_ref[h, 0] = v_aug.T.astype(BF16)

    col_blk = lambda j: pl.BlockSpec((TM, PIECE), lambda i: (i, j))
    head_blk = lambda w: pl.BlockSpec((FOX_HEADS, TM, w), lambda i: (0, i, 0))
    tile_blk = pl.BlockSpec((FOX_HEADS, 1, 128, TM), lambda i: (0, i, 0, 0))
    tiled = jax.ShapeDtypeStruct((FOX_HEADS, s_len // TM, 128, TM), BF16)
    return pl.pallas_call(
        body,
        name="fox_prep",
        grid=(s_len // TM,),
        in_specs=[col_blk(1), col_blk(2), col_blk(3), pl.BlockSpec((TM, 128), lambda i: (i, MY_F_OFF // 128)),
                  _full((1, 128)), _full((1, PIECE)), _full((1, PIECE)), _full((PIECE, PIECE)),
                  _full((128, FOX_HEADS * 128)), _full((128, FOX_HEADS * 128))],
        out_specs=[head_blk(128), head_blk(128), head_blk(HEAD_DIM), tile_blk, tile_blk],
        out_shape=[jax.ShapeDtypeStruct((FOX_HEADS, s_len, 128), BF16), jax.ShapeDtypeStruct((FOX_HEADS, s_len, 128), BF16),
                   jax.ShapeDtypeStruct((FOX_HEADS, s_len, HEAD_DIM), BF16), tiled, tiled],
        scratch_shapes=[pltpu.VMEM((1, 128), F32)],
        compiler_params=_params(("arbitrary",)),
    )(proj, proj, proj, proj, bf_pad, gq, gk, seg, to_q, to_k)


def _mem_prep(mem, g, w_kv, gk):
    def body(mem_ref, g_ref, w_ref, gk_ref, km_ref, vm_ref):
        m = mem_ref[...]
        kv = _dot((m * _rms(m) * g_ref[...]).astype(BF16), w_ref[...])
        for h in range(MEM_HEADS):
            sl = slice(HEAD_DIM * h, HEAD_DIM * (h + 1))
            kh = kv[:, sl]
            km_ref[:, sl] = (kh * _rms(kh) * gk_ref[...]).astype(BF16)
        vm_ref[...] = kv[:, 256:512].astype(BF16)

    return pl.pallas_call(
        body,
        name="mem_prep",
        out_shape=(jax.ShapeDtypeStruct((N_MEM, 256), BF16),) * 2,
        in_specs=[pl.BlockSpec(memory_space=pltpu.VMEM)] * 4,
        out_specs=(pl.BlockSpec(memory_space=pltpu.VMEM),) * 2,
        compiler_params=pltpu.CompilerParams(vmem_limit_bytes=VMEM_LIMIT),
    )(mem, g, w_kv, gk)


def _causal_mask():
    row = lax.broadcasted_iota(jnp.int32, (TA, TA), 0)
    col = lax.broadcasted_iota(jnp.int32, (TA, TA), 1)
    return col <= row


def _causal_mask_t():
    row = lax.broadcasted_iota(jnp.int32, (TA, TA), 0)
    col = lax.broadcasted_iota(jnp.int32, (TA, TA), 1)
    return row <= col


def _fox_fwd(q_aug, k_aug, vt_aug, w_kv16, w_out16):
    s_len = q_aug.shape[1]
    n_tiles = s_len // TA
    hb = HB_FWD
    n_groups = FOX_HEADS // hb

    def body(q_ref, k_new, vt_new, kv16_ref, out16_ref, o_ref, st_ref, kv_all, out_all, k_ref, vt_ref, send_sems, recv_sems, local_sems):
        qi = pl.program_id(1)
        for h in range(hb):
            k_ref[h, pl.ds(pl.multiple_of(qi * TA, TA), TA), :] = k_new[h]
            vt_ref[h, qi] = vt_new[h, 0]
        remote, local = _row_block_exchange(kv16_ref, out16_ref, kv_all, out_all, send_sems, recv_sems, local_sems, gather=True)

        @pl.when((pl.program_id(0) == 0) & (qi == 0))
        def _():
            for cp in remote + local:
                cp.start()

        qs = [q_ref[h] for h in range(hb)]

        def step(t0, carry, masked, width):
            rows = pl.ds(pl.multiple_of(t0 * TA, TA), width * TA)
            scores = [_dot_nt(k_ref[h, rows, :], qs[h]) for h in range(hb)]
            soft = []
            for h in range(hb):
                m, _ = carry[h]
                s = jnp.where(_causal_mask_t(), scores[h], -1e30) if masked else scores[h]
                m_new = jnp.maximum(m, jnp.max(s, axis=0, keepdims=True))
                soft.append((m_new, jnp.exp(m - m_new), jnp.exp(s - m_new).astype(BF16)))
            out = []
            for h, (m_new, alpha, p) in enumerate(soft):
                acc = alpha * carry[h][1]
                for t in range(width):
                    acc = acc + _dot(vt_ref[h, t0 + t, 0:AUG_ROWS, :], p[t * TA:(t + 1) * TA])
                out.append((m_new, acc))
            return tuple(out)

        init = tuple((jnp.full((1, TA), -1e30, F32), jnp.zeros((AUG_ROWS, TA), F32)) for _ in range(hb))
        carry = lax.fori_loop(0, qi // 2, lambda j, cr: step(2 * j, cr, False, 2), init)
        carry = lax.fori_loop(2 * (qi // 2), qi, lambda j, cr: step(j, cr, False, 1), carry)
        carry = step(qi, carry, True, 1)
        for h in range(hb):
            m, acc = carry[h]
            l = acc[HEAD_DIM:HEAD_DIM + 1, :]
            lse = m + jnp.log(l)
            o_ref[h] = jnp.concatenate([acc[0:HEAD_DIM] / l, jnp.broadcast_to(lse, (HEAD_DIM, TA))], axis=0).T
            st_ref[h, 0] = jnp.broadcast_to(lse, (8, TA))

        @pl.when((pl.program_id(0) == n_groups - 1) & (qi == n_tiles - 1))
        def _():
            for cp in remote:
                cp.wait_recv()
            for cp in remote:
                cp.wait_send()
            for cp in local:
                cp.wait()

    hbm = pl.BlockSpec(memory_space=pl.ANY)
    return pl.pallas_call(
        body,
        name="fox_fwd",
        grid=(n_groups, n_tiles),
        in_specs=[pl.BlockSpec((hb, TA, 128), lambda g, i: (g, i, 0)), pl.BlockSpec((hb, TA, 128), lambda g, i: (g, i, 0)),
                  pl.BlockSpec((hb, 1, 128, TA), lambda g, i: (g, i, 0, 0)), hbm, hbm],
        out_specs=[pl.BlockSpec((hb, TA, 128), lambda g, i: (g, i, 0)), pl.BlockSpec((hb, 1, 8, TA), lambda g, i: (g, i, 0, 0)),
                   hbm, hbm],
        out_shape=[jax.ShapeDtypeStruct((FOX_HEADS, s_len, 128), F32), jax.ShapeDtypeStruct((FOX_HEADS, n_tiles, 8, TA), F32),
                   jax.ShapeDtypeStruct((D_MODEL, 512), BF16), jax.ShapeDtypeStruct((D_MODEL, D_MODEL), BF16)],
        scratch_shapes=[pltpu.VMEM((hb, s_len, 128), BF16), pltpu.VMEM((hb, n_tiles, 128, TA), BF16),
                        pltpu.SemaphoreType.DMA((14,)), pltpu.SemaphoreType.DMA((14,)), pltpu.SemaphoreType.DMA((2,))],
        compiler_params=_params(("arbitrary", "arbitrary")),
    )(q_aug, k_aug, vt_aug, w_kv16, w_out16)


def _lane_group(lane, a, b, c, d):
    return jnp.where(lane < 64, a, jnp.where(lane < 128, b, jnp.where(lane < 192, c, d)))


def _pool_count(row0):
    lane = lax.broadcasted_iota(jnp.int32, (TM, POOL_WIDTH), 1)
    t1 = row0 + lax.broadcasted_iota(jnp.int32, (TM, POOL_WIDTH), 0) + 1
    return 1.0 / jnp.minimum(t1, _lane_group(lane, 2, 4, 8, 16)).astype(F32), lane


def _pool_delta(u, halo, cnt, lane):
    xe = jnp.concatenate([halo, u], axis=0)
    s2 = xe + pltpu.roll(xe, 1, 0)
    s4 = s2 + pltpu.roll(s2, 2, 0)
    s8 = s4 + pltpu.roll(s4, 4, 0)
    s16 = s8 + pltpu.roll(s8, 8, 0)
    win = _lane_group(lane, s2[HALO:], s4[HALO:], s8[HALO:], s16[HALO:])
    return win * cnt - u


def _pool_delta_bwd(dd, dd_next, cnt, cnt_next, lane):
    ee = jnp.concatenate([dd * cnt, dd_next * cnt_next], axis=0)
    n = TM + HALO
    r2 = ee + pltpu.roll(ee, n - 1, 0)
    r4 = r2 + pltpu.roll(r2, n - 2, 0)
    r8 = r4 + pltpu.roll(r4, n - 4, 0)
    r16 = r8 + pltpu.roll(r8, n - 8, 0)
    return _lane_group(lane, r2[:TM], r4[:TM], r8[:TM], r16[:TM]) - dd


def _mem_attn(qm, gq, seg, km_ref, vm_ref):
    r = _head_rms(qm, seg)
    a = qm * r
    q16 = (a * gq * 0.125).astype(BF16)
    heads = []
    for h in range(MEM_HEADS):
        sl = slice(HEAD_DIM * h, HEAD_DIM * (h + 1))
        s = _dot_nt(q16[:, sl], km_ref[:, sl])
        e = jnp.exp(s - jnp.max(s, axis=-1, keepdims=True))
        p = e * (1.0 / jnp.sum(e, axis=-1, keepdims=True))
        heads.append((p, _dot(p.astype(BF16), vm_ref[:, sl])))
    return a, r, q16, heads


def _row_specs(s_len):
    n_halo = s_len // HALO
    piece = lambda j: pl.BlockSpec((TM, PIECE), lambda i: (i, j))
    before = lambda j: pl.BlockSpec((HALO, 256), lambda i: (jnp.maximum(i * (TM // HALO) - 1, 0), j))
    after = lambda j: pl.BlockSpec((HALO, 256), lambda i: (jnp.minimum((i + 1) * (TM // HALO), n_halo - 1), j))
    return piece, before, after


def _mix_fwd(proj, olse, km, vm, wp_bd, pool_scale, gq_m, seg):
    s_len = proj.shape[0]

    def body(ua_ref, halo_ref, gb_ref, qm_ref, ol_ref, km_ref, vm_ref, wp_ref, sc_ref, gq_ref, seg_ref, out_ref):
        i = pl.program_id(0)
        u = ua_ref[:, 0:256]
        g_a = ua_ref[:, 256:512]
        halo = jnp.where(i > 0, halo_ref[...], 0.0)
        cnt, lane = _pool_count(i * TM)
        d = _pool_delta(u, halo, cnt, lane).astype(BF16)
        y_a = _dot(d, wp_ref[...]) * sc_ref[...]
        out_ref[:, 0:256] = (y_a * (g_a * _sigmoid(g_a))).astype(BF16)
        g_b = gb_ref[...]
        y_b = jnp.concatenate([ol_ref[h][:, 0:HEAD_DIM] for h in range(FOX_HEADS)], axis=1)
        out_ref[:, 256:768] = (y_b * (g_b * _sigmoid(g_b))).astype(BF16)
        _, _, _, heads = _mem_attn(qm_ref[:, 0:256], gq_ref[...], seg_ref[0:256, 0:256], km_ref, vm_ref)
        g_m = qm_ref[:, 256:512]
        y_m = jnp.concatenate([y for _, y in heads], axis=1)
        out_ref[:, 768:1024] = (y_m * (g_m * _sigmoid(g_m))).astype(BF16)

    piece, before, _ = _row_specs(s_len)
    return pl.pallas_call(
        body,
        name="mix_fwd",
        grid=(s_len // TM,),
        in_specs=[piece(0), before(0), piece(4), piece(5), pl.BlockSpec((FOX_HEADS, TM, 128), lambda i: (0, i, 0)),
                  _full((N_MEM, 256)), _full((N_MEM, 256)), _full((256, 256)), _full((1, 256)), _full((1, 256)),
                  _full((PIECE, PIECE))],
        out_specs=pl.BlockSpec((TM, D_MODEL), lambda i: (i, 0)),
        out_shape=jax.ShapeDtypeStruct((s_len, D_MODEL), BF16),
        compiler_params=_params(("parallel",)),
    )(proj, proj, proj, proj, olse, km, vm, wp_bd, pool_scale, gq_m, seg)


def _out_loss(mixed, x, target, w_out):
    s_len = x.shape[0]

    def body(mix_ref, x_ref, t_ref, w_ref, dout_ref, dmix_ref, dw16_ref, loss_ref, dw_ref):
        @pl.when(pl.program_id(0) == 0)
        def _():
            dw_ref[...] = jnp.zeros((D_MODEL, D_MODEL), F32)
            loss_ref[...] = jnp.zeros((8, 128), F32)

        mixed16 = mix_ref[...]
        err = x_ref[...] + _dot(mixed16, w_ref[...]) - t_ref[...]
        loss_ref[...] += 0.5 * jnp.sum(jnp.mean(err * err, axis=-1, keepdims=True))
        d_out = err / float(D_MODEL)
        dout_ref[...] = d_out
        d16 = d_out.astype(BF16)
        dmix_ref[...] = _dot_nt(d16, w_ref[...])
        dw_ref[...] += _dot_tn(mixed16, d16)

        @pl.when(pl.program_id(0) == pl.num_programs(0) - 1)
        def _():
            dw16_ref[...] = dw_ref[...].astype(BF16)

    row = pl.BlockSpec((TMM, D_MODEL), lambda i: (i, 0))
    return pl.pallas_call(
        body,
        name="out_loss",
        grid=(s_len // TMM,),
        in_specs=[row, row, row, _full((D_MODEL, D_MODEL))],
        out_specs=[row, row, _full((D_MODEL, D_MODEL)), _full((8, 128))],
        out_shape=[jax.ShapeDtypeStruct((s_len, D_MODEL), F32), jax.ShapeDtypeStruct((s_len, D_MODEL), F32),
                   jax.ShapeDtypeStruct((D_MODEL, D_MODEL), BF16), jax.ShapeDtypeStruct((8, 128), F32)],
        scratch_shapes=[pltpu.VMEM((D_MODEL, D_MODEL), F32)],
        compiler_params=_params(("arbitrary",)),
    )(mixed, x, target, w_out)


def _silu_pair(g):
    s = _sigmoid(g)
    return g * s, s * (1.0 + g * (1.0 - s))


def _mix_bwd(proj, olse, d_mixed, km, vm, wp_bd, pool_scale, gq_m, seg):
    s_len = proj.shape[0]
    n_tiles = s_len // TM

    def body(ua_ref, halo_ref, ga_next_ref, gb_ref, qm_ref, ol_ref, dm_ref, dm_next_ref, km_ref, vm_ref, wp_ref, sc_ref, gq_ref,
             seg_ref, dua_ref, dgb_ref, dqm_ref, do_ref, dl_ref, dwp_ref, dsc_ref, dkm_ref, dvm_ref, dgq_ref):
        i = pl.program_id(0)

        @pl.when(i == 0)
        def _():
            dwp_ref[...] = jnp.zeros((256, 256), F32)
            dsc_ref[...] = jnp.zeros((1, 256), F32)
            dkm_ref[...] = jnp.zeros((N_MEM, 256), F32)
            dvm_ref[...] = jnp.zeros((N_MEM, 256), F32)
            dgq_ref[...] = jnp.zeros((1, HEAD_DIM), F32)

        u = ua_ref[:, 0:256]
        g_a = ua_ref[:, 256:512]
        halo = jnp.where(i > 0, halo_ref[...], 0.0)
        cnt, lane = _pool_count(i * TM)
        d16 = _pool_delta(u, halo, cnt, lane).astype(BF16)
        t = _dot(d16, wp_ref[...])
        y_a = t * sc_ref[...]
        silu_a, dsilu_a = _silu_pair(g_a)
        dm_a = dm_ref[:, 0:256]
        dy_a = dm_a * silu_a
        dsc_ref[...] += jnp.sum(dy_a * t, axis=0, keepdims=True)
        dt16 = (dy_a * sc_ref[...]).astype(BF16)
        dwp_ref[...] += _dot_tn(d16, dt16)
        dd = _dot_nt(dt16, wp_ref[...])
        g_next = ga_next_ref[...]
        dt_next = (dm_next_ref[...] * (g_next * _sigmoid(g_next)) * sc_ref[...]).astype(BF16)
        dd_next = jnp.where(i < n_tiles - 1, _dot_nt(dt_next, wp_ref[...]), 0.0)
        cnt_next = _pool_count((i + 1) * TM)[0][0:HALO]
        dua_ref[:, 0:256] = _pool_delta_bwd(dd, dd_next, cnt, cnt_next, lane).astype(BF16)
        dua_ref[:, 256:512] = (dm_a * y_a * dsilu_a).astype(BF16)

        silu_b, dsilu_b = _silu_pair(gb_ref[...])
        dm_b = dm_ref[:, 256:768]
        o_all = jnp.concatenate([ol_ref[h][:, 0:HEAD_DIM] for h in range(FOX_HEADS)], axis=1)
        d_o = dm_b * silu_b
        dgb_ref[...] = (dm_b * o_all * dsilu_b).astype(BF16)
        for h in range(FOX_HEADS):
            do_ref[h] = d_o[:, HEAD_DIM * h:HEAD_DIM * (h + 1)].astype(BF16)
        prod = d_o * o_all
        hi = prod.astype(BF16)
        lo = (prod - hi.astype(F32)).astype(BF16)
        head_of_lane = lax.broadcasted_iota(jnp.int32, (FOX_HEADS, PIECE), 1) // HEAD_DIM
        pick = jnp.where(head_of_lane == lax.broadcasted_iota(jnp.int32, (FOX_HEADS, PIECE), 0), 1.0, 0.0).astype(BF16)
        delta = _dot_nt(pick, hi) + _dot_nt(pick, lo)
        for h in range(FOX_HEADS):
            dl_ref[h, 0] = jnp.broadcast_to(delta[h:h + 1, :], (8, TM))

        seg = seg_ref[0:256, 0:256]
        a, r, q16, heads = _mem_attn(qm_ref[:, 0:256], gq_ref[...], seg, km_ref, vm_ref)
        silu_m, dsilu_m = _silu_pair(qm_ref[:, 256:512])
        dm_m = dm_ref[:, 768:1024]
        y_m = jnp.concatenate([y for _, y in heads], axis=1)
        dqm_ref[:, 256:512] = (dm_m * y_m * dsilu_m).astype(BF16)
        dy16_all = (dm_m * silu_m).astype(BF16)
        d_scores = []
        for h in range(MEM_HEADS):
            sl = slice(HEAD_DIM * h, HEAD_DIM * (h + 1))
            p = heads[h][0]
            dy16 = dy16_all[:, sl]
            dp = _dot_nt(dy16, vm_ref[:, sl])
            dvm_ref[:, sl] += _dot_tn(p.astype(BF16), dy16)
            ds16 = (p * (dp - jnp.sum(dp * p, axis=-1, keepdims=True))).astype(BF16)
            dkm_ref[:, sl] += _dot_tn(ds16, q16[:, sl])
            d_scores.append(_dot(ds16, km_ref[:, sl]))
        dq, dg_rows = _head_rms_bwd(a, r, gq_ref[...], jnp.concatenate(d_scores, axis=1) * 0.125, seg)
        dqm_ref[:, 0:256] = dq.astype(BF16)
        dgq_ref[...] += _fold_heads(jnp.sum(dg_rows, axis=0, keepdims=True), MEM_HEADS)

    piece, before, after = _row_specs(s_len)
    n_halo = s_len // HALO
    row = pl.BlockSpec((TM, D_MODEL), lambda i: (i, 0))
    dm_after = pl.BlockSpec((HALO, 256), lambda i: (jnp.minimum((i + 1) * (TM // HALO), n_halo - 1), 0))
    out_piece = pl.BlockSpec((TM, PIECE), lambda i: (i, 0))
    return pl.pallas_call(
        body,
        name="mix_bwd",
        grid=(n_tiles,),
        in_specs=[piece(0), before(0), after(1), piece(4), piece(5), pl.BlockSpec((FOX_HEADS, TM, 128), lambda i: (0, i, 0)),
                  row, dm_after, _full((N_MEM, 256)), _full((N_MEM, 256)), _full((256, 256)), _full((1, 256)), _full((1, 256)),
                  _full((PIECE, PIECE))],
        out_specs=[out_piece, out_piece, out_piece, pl.BlockSpec((FOX_HEADS, TM, HEAD_DIM), lambda i: (0, i, 0)),
                   pl.BlockSpec((FOX_HEADS, 1, 8, TM), lambda i: (0, i, 0, 0)),
                   _full((256, 256)), _full((1, 256)), _full((N_MEM, 256)), _full((N_MEM, 256)), _full((1, HEAD_DIM))],
        out_shape=[jax.ShapeDtypeStruct((s_len, PIECE), BF16)] * 3 + [
            jax.ShapeDtypeStruct((FOX_HEADS, s_len, HEAD_DIM), BF16),
            jax.ShapeDtypeStruct((FOX_HEADS, n_tiles, 8, TM), F32),
            jax.ShapeDtypeStruct((256, 256), F32), jax.ShapeDtypeStruct((1, 256), F32),
            jax.ShapeDtypeStruct((N_MEM, 256), F32), jax.ShapeDtypeStruct((N_MEM, 256), F32),
            jax.ShapeDtypeStruct((1, HEAD_DIM), F32)],
        compiler_params=_params(("arbitrary",)),
    )(proj, proj, proj, proj, proj, olse, d_mixed, d_mixed, km, vm, wp_bd, pool_scale, gq_m, seg)


def _fox_bwd(q_aug, k_aug, kt_aug, v_h, d_o, lse_rows, delta_rows, dw_kv16, dw_out16):
    s_len = q_aug.shape[1]
    n_tiles = s_len // TA
    n_groups = FOX_HEADS // HB

    def body(q_hbm, k_ref, kt_ref, v_ref, do_hbm, st_ref, dl_ref, dkv16_ref, dout16_ref, dq_ref, dk_ref, dv_ref, land_kv, land_out,
             dqt_ref, q_ref, do_ref, send_sems, recv_sems, local_sems, tile_sems):
        j = pl.program_id(1)
        remote, local = _row_block_exchange(dkv16_ref, dout16_ref, land_kv, land_out, send_sems, recv_sems, local_sems, gather=False)

        def tile_loads(i):
            rows = pl.ds(pl.multiple_of(i * TA, TA), TA)
            return [pltpu.make_async_copy(q_hbm.at[:, rows, :], q_ref.at[:, rows, :], tile_sems.at[2 * i]),
                    pltpu.make_async_copy(do_hbm.at[:, rows, :], do_ref.at[:, rows, :], tile_sems.at[2 * i + 1])]

        @pl.when((pl.program_id(0) == 0) & (j == 0))
        def _():
            for i in range(n_tiles):
                for cp in tile_loads(i):
                    cp.start()
            for cp in remote + local:
                cp.start()

        @pl.when(j == 0)
        def _():
            dqt_ref[...] = jnp.zeros((HB, n_tiles, AUG_ROWS, TA), F32)

        ks = [k_ref[h] for h in range(HB)]
        kts = [kt_ref[h, 0, 0:AUG_ROWS, :] for h in range(HB)]
        vs = [v_ref[h] for h in range(HB)]

        def pair(i0, acc, masked, width):
            @pl.when(j == 0)
            def _():
                for t in range(width):
                    for cp in tile_loads(i0 + t):
                        cp.wait()

            rows = pl.ds(pl.multiple_of(i0 * TA, TA), width * TA)
            qs = [q_ref[h, rows, :] for h in range(HB)]
            d_os = [do_ref[h, rows, :] for h in range(HB)]
            row_stat = lambda ref, h: jnp.concatenate([ref[h, i0 + t, 0:1, :] for t in range(width)], axis=1)
            scores = [(_dot_nt(ks[h], qs[h]), _dot_nt(vs[h], d_os[h])) for h in range(HB)]
            probs = []
            for h in range(HB):
                s, dp = scores[h]
                if masked:
                    s = jnp.where(_causal_mask_t(), s, -1e30)
                p = jnp.exp(s - row_stat(st_ref, h))
                probs.append((p.astype(BF16), (p * (dp - row_stat(dl_ref, h))).astype(BF16)))
            out = []
            for h in range(HB):
                p16, ds16 = probs[h]
                dqt = _dot(kts[h], ds16)
                for t in range(width):
                    dqt_ref[h, i0 + t] += dqt[:, t * TA:(t + 1) * TA]
                out.append((acc[h][0] + _dot(ds16, qs[h]), acc[h][1] + _dot(p16, d_os[h])))
            return tuple(out)

        zero = tuple((jnp.zeros((TA, 128), F32), jnp.zeros((TA, HEAD_DIM), F32)) for _ in range(HB))
        acc = pair(j, zero, True, 1)
        odd = (n_tiles - 1 - j) % 2
        acc = lax.fori_loop(j + 1, j + 1 + odd, lambda i, a: pair(i, a, False, 1), acc)
        acc = lax.fori_loop(0, (n_tiles - 1 - j) // 2, lambda t, a: pair(j + 1 + odd + 2 * t, a, False, 2), acc)
        pad = jnp.zeros((128 - AUG_ROWS, TA), F32)
        for h in range(HB):
            dk_ref[h] = acc[h][0]
            dv_ref[h] = acc[h][1].astype(BF16)
            dq_ref[h] = jnp.concatenate([dqt_ref[h, j], pad], axis=0).T

        @pl.when((pl.program_id(0) == n_groups - 1) & (j == n_tiles - 1))
        def _():
            for cp in remote:
                cp.wait_recv()
            for cp in remote:
                cp.wait_send()
            for cp in local:
                cp.wait()

    assert n_groups == 1
    resident = pl.BlockSpec(memory_space=pltpu.VMEM)
    rows_blk = lambda r: resident
    tile = lambda w: pl.BlockSpec((HB, TA, w), lambda g, j: (g, j, 0))
    hbm = pl.BlockSpec(memory_space=pl.ANY)
    return pl.pallas_call(
        body,
        name="fox_bwd",
        grid=(n_groups, n_tiles),
        in_specs=[hbm, tile(128), pl.BlockSpec((HB, 1, 128, TA), lambda g, j: (g, j, 0, 0)), tile(HEAD_DIM),
                  hbm, rows_blk(8), rows_blk(8), hbm, hbm],
        out_specs=[tile(128), tile(128), tile(HEAD_DIM), hbm, hbm],
        out_shape=[jax.ShapeDtypeStruct((FOX_HEADS, s_len, 128), F32), jax.ShapeDtypeStruct((FOX_HEADS, s_len, 128), F32),
                   jax.ShapeDtypeStruct((FOX_HEADS, s_len, HEAD_DIM), BF16),
                   jax.ShapeDtypeStruct((N_DEV, 128, 512), BF16), jax.ShapeDtypeStruct((N_DEV, 128, D_MODEL), BF16)],
        scratch_shapes=[pltpu.VMEM((HB, n_tiles, AUG_ROWS, TA), F32),
                        pltpu.VMEM((HB, s_len, 128), BF16), pltpu.VMEM((HB, s_len, HEAD_DIM), BF16),
                        pltpu.SemaphoreType.DMA((14,)), pltpu.SemaphoreType.DMA((14,)), pltpu.SemaphoreType.DMA((2,)),
                        pltpu.SemaphoreType.DMA((2 * n_tiles,))],
        compiler_params=_params(("arbitrary", "arbitrary")),
    )(q_aug, k_aug, kt_aug, v_h, d_o, lse_rows, delta_rows, dw_kv16, dw_out16)


def _fox_post(proj, bf_pad, gq, gk, seg, dq_aug, dk_aug, dv_h):
    s_len = proj.shape[0]
    n_tiles = s_len // TM

    def body(q_ref, k_ref, f_ref, bf_ref, gq_ref, gk_ref, seg_ref, dqa_ref, dka_ref, dvh_ref,
             dq_ref, dk_ref, dv_ref, df_ref, dgq_ref, dgk_ref, dbf_ref, carry_ref):
        @pl.when(pl.program_id(0) == 0)
        def _():
            carry_ref[...] = jnp.zeros((1, 128), F32)
            dgq_ref[...] = jnp.zeros((1, HEAD_DIM), F32)
            dgk_ref[...] = jnp.zeros((1, HEAD_DIM), F32)
            dbf_ref[...] = jnp.zeros((1, 128), F32)

        lane = lax.broadcasted_iota(jnp.int32, (TM, 128), 1)
        seg = seg_ref[...]
        dqas = [dqa_ref[h] for h in range(FOX_HEADS)]
        dkas = [dka_ref[h] for h in range(FOX_HEADS)]
        d_cum = jnp.zeros((TM, 128), F32)
        for h in range(FOX_HEADS):
            d_cum = jnp.where(lane == h, dqas[h][:, 64:65] - dkas[h][:, 67:68], d_cum)
        q_all = q_ref[...]
        k_all = k_ref[...]
        rq = _head_rms(q_all, seg)
        rk = _head_rms(k_all, seg)
        dy_q = jnp.concatenate([t[:, 0:HEAD_DIM] for t in dqas], axis=1) * 0.125
        dy_k = jnp.concatenate([t[:, 0:HEAD_DIM] for t in dkas], axis=1)
        dq, dgq_rows = _head_rms_bwd(q_all * rq, rq, gq_ref[...], dy_q, seg)
        dk, dgk_rows = _head_rms_bwd(k_all * rk, rk, gk_ref[...], dy_k, seg)
        dq_ref[...] = dq.astype(BF16)
        dk_ref[...] = dk.astype(BF16)
        dv_ref[...] = jnp.concatenate([dvh_ref[h].astype(F32) for h in range(FOX_HEADS)], axis=1).astype(BF16)
        dgq_ref[...] += _fold_heads(jnp.sum(dgq_rows, axis=0, keepdims=True), FOX_HEADS)
        dgk_ref[...] += _fold_heads(jnp.sum(dgk_rows, axis=0, keepdims=True), FOX_HEADS)

        row = lax.broadcasted_iota(jnp.int32, (TM, TM), 0)
        col = lax.broadcasted_iota(jnp.int32, (TM, TM), 1)
        tri = jnp.where(col >= row, 1.0, 0.0).astype(BF16)
        hi, mid, lo = _split3(d_cum)
        d_logf = _dot(tri, hi) + _dot(tri, mid) + _dot(tri, lo) + carry_ref[...]
        carry_ref[...] = d_logf[0:1, :]
        z = f_ref[...] + bf_ref[...]
        d_f = jnp.where(lane < FOX_HEADS, d_logf / (1.0 + jnp.exp(z)), 0.0)
        df_ref[...] = d_f.astype(BF16)
        dbf_ref[...] += jnp.sum(d_f, axis=0, keepdims=True)

    rev = lambda i: n_tiles - 1 - i
    col_blk = lambda j: pl.BlockSpec((TM, PIECE), lambda i: (rev(i), j))
    head_blk = lambda w: pl.BlockSpec((FOX_HEADS, TM, w), lambda i: (0, rev(i), 0))
    out_piece = pl.BlockSpec((TM, PIECE), lambda i: (rev(i), 0))
    return pl.pallas_call(
        body,
        name="fox_post",
        grid=(n_tiles,),
        in_specs=[col_blk(1), col_blk(2), pl.BlockSpec((TM, 128), lambda i: (rev(i), MY_F_OFF // 128)),
                  _full((1, 128)), _full((1, PIECE)), _full((1, PIECE)), _full((PIECE, PIECE)),
                  head_blk(128), head_blk(128), head_blk(HEAD_DIM)],
        out_specs=[out_piece, out_piece, out_piece, pl.BlockSpec((TM, 128), lambda i: (rev(i), 0)),
                   _full((1, HEAD_DIM)), _full((1, HEAD_DIM)), _full((1, 128))],
        out_shape=[jax.ShapeDtypeStruct((s_len, PIECE), BF16)] * 3 + [
            jax.ShapeDtypeStruct((s_len, 128), BF16), jax.ShapeDtypeStruct((1, HEAD_DIM), F32),
            jax.ShapeDtypeStruct((1, HEAD_DIM), F32), jax.ShapeDtypeStruct((1, 128), F32)],
        scratch_shapes=[pltpu.VMEM((1, 128), F32)],
        compiler_params=_params(("arbitrary",)),
    )(proj, proj, proj, bf_pad, gq, gk, seg, dq_aug, dk_aug, dv_h)


def _mem_bwd(mem, g, w_kv, gk, dkm, dvm):
    def body(mem_ref, g_ref, w_ref, gk_ref, dkm_ref, dvm_ref, dw_ref, dg_ref, dgk_ref, dkv_ref):
        m = mem_ref[...]
        r = _rms(m)
        a = m * r
        mn16 = (a * g_ref[...]).astype(BF16)
        kv = _dot(mn16, w_ref[...])
        dgk = jnp.zeros((N_MEM, HEAD_DIM), F32)
        for h in range(MEM_HEADS):
            sl = slice(HEAD_DIM * h, HEAD_DIM * (h + 1))
            kh = kv[:, sl]
            rk = _rms(kh)
            dk, dg_rows = _rms_bwd(kh * rk, rk, gk_ref[...], dkm_ref[:, sl])
            dkv_ref[:, sl] = dk.astype(BF16)
            dgk = dgk + dg_rows
        dkv_ref[:, 256:512] = dvm_ref[...].astype(BF16)
        dgk_ref[...] = jnp.sum(dgk, axis=0, keepdims=True)
        dkv16 = dkv_ref[...]
        dw_ref[...] = _dot_tn(mn16, dkv16).astype(BF16)
        dg_ref[...] = jnp.sum(_dot_nt(dkv16, w_ref[...]) * a, axis=0, keepdims=True)

    return pl.pallas_call(
        body,
        name="mem_bwd",
        out_shape=(jax.ShapeDtypeStruct((D_MODEL, 512), BF16), jax.ShapeDtypeStruct((1, D_MODEL), F32),
                   jax.ShapeDtypeStruct((1, HEAD_DIM), F32)),
        in_specs=[pl.BlockSpec(memory_space=pltpu.VMEM)] * 6,
        out_specs=(pl.BlockSpec(memory_space=pltpu.VMEM),) * 3,
        scratch_shapes=[pltpu.VMEM((N_MEM, 512), BF16)],
        compiler_params=pltpu.CompilerParams(vmem_limit_bytes=VMEM_LIMIT),
    )(mem, g, w_kv, gk, dkm, dvm)


def _grad_x_exchange(pieces, d_f, w_my, x, norm_g, d_out, dw_in, land_kv, land_out, d_mem_g, d_scale, d_bf, d_gq, d_gk, d_gqm,
                     d_gkm, dwp_bd, loss_blk):
    s_len = x.shape[0]
    n_steps = s_len // TM
    step2, step3 = n_steps // 4, (11 * n_steps) // 16
    half = D_MODEL // 2

    def body(p0, p1, p2, p3, p4, p5, df_ref, x_ref, dout_ref, w_ref, g_ref, in_ref, lkv_ref, lout_ref,
             dmg, dsc, dbf, dgq, dgk, dgqm, dgkm, dwp, loss_ref,
             gx_ref, rin_ref, rkv_ref, rout_ref, sred_ref,
             dng, land1, send2a, send2b, keep2a, keep2b, land2a, land2b, send3a, send3b, land3a, land3b, sb_ref, sland,
             own4, lkv_v, lout_v, send_sems, recv_sems, load_sems):
        i = pl.program_id(0)
        x_, y_, c_, me = _my_place()
        sibling, x_nbr, y_nbr = (x_, y_, 1 - c_), (1 - x_, y_, c_), (x_, 1 - y_, c_)
        loads = [pltpu.make_async_copy(in_ref.at[2 * k + c_], own4.at[k], load_sems.at[k]) for k in range(4)]
        loads += [pltpu.make_async_copy(lkv_ref, lkv_v, load_sems.at[4]), pltpu.make_async_copy(lout_ref, lout_v, load_sems.at[5])]

        def rcopy(src, dst, sem, to):
            return pltpu.make_async_remote_copy(src_ref=src, dst_ref=dst, send_sem=send_sems.at[sem], recv_sem=recv_sems.at[sem],
                                                device_id=to, device_id_type=MESH)

        early_rows, late_rows = pl.ds(8, SB_ROWS - 8), pl.ds(0, 8)
        small_early, small_late = [], []
        for k in range(1, N_DEV):
            peer = (x_ ^ (k >> 2), y_ ^ ((k >> 1) & 1), c_ ^ (k & 1))
            small_early.append(rcopy(sb_ref.at[early_rows], sland.at[me, early_rows], k - 1, peer))
            small_late.append(rcopy(sb_ref.at[late_rows], sland.at[me, late_rows], 16 + k, peer))
        small = small_early + small_late
        stage1 = [rcopy(in_ref.at[2 * k + 1 - c_], land1.at[k], 7 + k, sibling) for k in range(4)]
        stage2 = [rcopy(send2a.at[j], land2a.at[j], 11 + j, x_nbr) for j in range(2)]
        stage2 += [rcopy(send2b.at[j], land2b.at[j], 13 + j, y_nbr) for j in range(2)]
        stage3 = [rcopy(send3a, land3a, 15, y_nbr), rcopy(send3b, land3b, 16, x_nbr)]
        top, bot = slice(0, half), slice(half, D_MODEL)

        @pl.when(i == 0)
        def _():
            dng[...] = jnp.zeros((1, D_MODEL), F32)
            for cp in stage1 + loads:
                cp.start()
            sb_ref[...] = jnp.zeros((SB_ROWS, SB_W), F32)
            sb_ref[SB_POOL_SCALE:SB_POOL_SCALE + 1, :] = dsc[...]
            sb_ref[SB_B_F:SB_B_F + 1, 0:128] = dbf[...]
            for row, ref in ((SB_FOX_Q, dgq), (SB_FOX_K, dgk), (SB_MEM_Q, dgqm), (SB_MEM_K, dgkm)):
                sb_ref[row:row + 1, 0:HEAD_DIM] = ref[...]
            sb_ref[SB_LOSS:SB_LOSS + 1, 0:128] = loss_ref[0:1, :]
            for grp in range(4):
                sl = slice(64 * grp, 64 * (grp + 1))
                sb_ref[SB_W_POOL:SB_W_POOL + 64, sl] = dwp[sl, sl]
            for cp in small_early:
                cp.start()

        @pl.when(i == step2)
        def _():
            for cp in stage1:
                cp.wait_recv()
            for cp in loads[0:4]:
                cp.wait()

            def chip_sum(k, cols):
                return own4[k, :, cols].astype(F32) + land1[k, :, cols].astype(F32)

            for j in range(2):
                send2a[j] = chip_sum(2 * (1 - x_) + j, top).astype(BF16)
                keep2a[j] = chip_sum(2 * x_ + j, top)
                send2b[j] = chip_sum(2 * j + 1 - y_, bot).astype(BF16)
                keep2b[j] = chip_sum(2 * j + y_, bot)
            for cp in stage2:
                cp.start()

        @pl.when(i == step3)
        def _():
            for cp in stage2:
                cp.wait_recv()
            for j in range(2):
                keep2a[j] = keep2a[j] + land2a[j].astype(F32)
                keep2b[j] = keep2b[j] + land2b[j].astype(F32)
            send3a[...] = keep2a[1 - y_].astype(BF16)
            send3b[...] = keep2b[1 - x_].astype(BF16)
            for cp in stage3:
                cp.start()

        dh = _dot_nt(df_ref[...], w_ref[:, MY_F_OFF:MY_WIDTH])
        for j, p in enumerate((p0, p1, p2, p3, p4, p5)):
            dh = dh + _dot_nt(p[...], w_ref[:, PIECE * j:PIECE * (j + 1)])
        xv = x_ref[...]
        r = _rms(xv)
        dx, dg_rows = _rms_bwd(xv * r, r, g_ref[...], dh)
        gx_ref[...] = dout_ref[...] + dx
        dng[...] += jnp.sum(dg_rows, axis=0, keepdims=True)

        @pl.when(i == n_steps - 1)
        def _():
            for k in range(4):
                sb_ref[SB_NORM_G + k:SB_NORM_G + k + 1, :] = dng[:, SB_W * k:SB_W * (k + 1)]
                sb_ref[SB_MEM_NORM_G + k:SB_MEM_NORM_G + k + 1, :] = dmg[:, SB_W * k:SB_W * (k + 1)]
            for cp in small_late:
                cp.start()
            sland[me] = sb_ref[...]
            for cp in stage3:
                cp.wait_recv()
            rin_ref[:, top] = keep2a[y_] + land3a[...].astype(F32)
            rin_ref[:, bot] = keep2b[x_] + land3b[...].astype(F32)
            for cp in small:
                cp.wait_recv()
            for cp in small + stage1 + stage2 + stage3:
                cp.wait_send()
            for cp in loads[4:6]:
                cp.wait()
            for land, red in ((lkv_v, rkv_ref), (lout_v, rout_ref), (sland, sred_ref)):
                acc = land[0].astype(F32)
                for d in range(1, N_DEV):
                    acc = acc + land[d].astype(F32)
                red[...] = acc

    piece = pl.BlockSpec((TM, PIECE), lambda i: (i, 0))
    row = pl.BlockSpec((TM, D_MODEL), lambda i: (i, 0))
    vmem = pl.BlockSpec(memory_space=pltpu.VMEM)
    half_chunk = lambda n, dt: pltpu.VMEM((n, CHUNK_ROWS, half), dt)
    whole = (w_my, norm_g, dw_in, land_kv, land_out, d_mem_g, d_scale, d_bf, d_gq, d_gk, d_gqm, d_gkm, dwp_bd, loss_blk)
    return pl.pallas_call(
        body,
        name="grad_x_exchange",
        grid=(n_steps,),
        in_specs=[piece] * 6 + [pl.BlockSpec((TM, 128), lambda i: (i, 0)), row, row, vmem, vmem]
        + [pl.BlockSpec(memory_space=pl.ANY)] * 3 + [vmem] * (len(whole) - 5),
        out_specs=[row, vmem, vmem, vmem, vmem],
        out_shape=[jax.ShapeDtypeStruct((s_len, D_MODEL), F32), jax.ShapeDtypeStruct((CHUNK_ROWS, D_MODEL), F32),
                   jax.ShapeDtypeStruct((128, 512), F32), jax.ShapeDtypeStruct((128, D_MODEL), F32),
                   jax.ShapeDtypeStruct((SB_ROWS, SB_W), F32)],
        scratch_shapes=[
            pltpu.VMEM((1, D_MODEL), F32),
            pltpu.VMEM((4, CHUNK_ROWS, D_MODEL), BF16),
            half_chunk(2, BF16), half_chunk(2, BF16), half_chunk(2, F32), half_chunk(2, F32), half_chunk(2, BF16), half_chunk(2, BF16),
            pltpu.VMEM((CHUNK_ROWS, half), BF16), pltpu.VMEM((CHUNK_ROWS, half), BF16),
            pltpu.VMEM((CHUNK_ROWS, half), BF16), pltpu.VMEM((CHUNK_ROWS, half), BF16),
            pltpu.VMEM((SB_ROWS, SB_W), F32),
            pltpu.VMEM((N_DEV, SB_ROWS, SB_W), F32),
            pltpu.VMEM((4, CHUNK_ROWS, D_MODEL), BF16),
            pltpu.VMEM((N_DEV, 128, 512), BF16),
            pltpu.VMEM((N_DEV, 128, D_MODEL), BF16),
            pltpu.SemaphoreType.DMA((24,)),
            pltpu.SemaphoreType.DMA((24,)),
            pltpu.SemaphoreType.DMA((6,)),
        ],
        compiler_params=_params(("arbitrary",)),
    )(*pieces, d_f, x, d_out, *whole)


def _grad_w_in(h16, pieces, d_f):
    s_len = h16.shape[0]
    tk = 512

    def body(h_ref, p0, p1, p2, p3, p4, p5, df_ref, out_ref, dw_ref):
        @pl.when(pl.program_id(0) == 0)
        def _():
            dw_ref[...] = jnp.zeros((D_MODEL, MY_WIDTH), F32)

        h = h_ref[...]
        for j, p in enumerate((p0, p1, p2, p3, p4, p5)):
            dw_ref[:, PIECE * j:PIECE * (j + 1)] += _dot_tn(h, p[...])
        dw_ref[:, MY_F_OFF:MY_WIDTH] += _dot_tn(h, df_ref[...])

        @pl.when(pl.program_id(0) == pl.num_programs(0) - 1)
        def _():
            for d in range(N_DEV):
                parts = [dw_ref[:, start:start + width] for start, width in _chunk_segments(d)]
                parts.append(jnp.zeros((D_MODEL, 512 - IN_CHUNK), F32))
                out_ref[d] = jnp.concatenate(parts, axis=1).T[0:CHUNK_ROWS].astype(BF16)

    piece = pl.BlockSpec((tk, PIECE), lambda i: (i, 0))
    return pl.pallas_call(
        body,
        name="grad_w_in",
        grid=(s_len // tk,),
        in_specs=[pl.BlockSpec((tk, D_MODEL), lambda i: (i, 0))] + [piece] * 6 + [pl.BlockSpec((tk, 128), lambda i: (i, 0))],
        out_specs=_full((N_DEV, CHUNK_ROWS, D_MODEL)),
        out_shape=jax.ShapeDtypeStruct((N_DEV, CHUNK_ROWS, D_MODEL), BF16),
        scratch_shapes=[pltpu.VMEM((D_MODEL, MY_WIDTH), F32)],
        compiler_params=_params(("arbitrary",)),
    )(h16, *pieces, d_f)


def _adamw(w, g, m, v):
    m = ADAM_B1 * m + (1.0 - ADAM_B1) * g
    v = ADAM_B2 * v + (1.0 - ADAM_B2) * (g * g)
    m_hat = m / (1.0 - ADAM_B1 ** ADAM_STEP)
    v_hat = v / (1.0 - ADAM_B2 ** ADAM_STEP)
    return -ADAM_LR * (m_hat / (jnp.sqrt(v_hat) + ADAM_EPS) + ADAM_WD * w), m, v


PARAM_ORDER = ("norm_g", "w_in", "b_f", "w_pool", "pool_scale", "fox_q_g", "fox_k_g", "mem_norm_g", "w_mem_kv", "mem_q_g", "mem_k_g", "w_out")


def _update(red_in, red_kv, red_out, sred, params):
    def body(rin_ref, rkv_ref, rout_ref, sred_ref, *refs):
        ins, outs = refs[:3 * len(PARAM_ORDER)], refs[3 * len(PARAM_ORDER):]
        wide = lambda row: jnp.concatenate([sred_ref[row + k:row + k + 1, :] for k in range(4)], axis=1)
        head = lambda row: sred_ref[row:row + 1, 0:HEAD_DIM]
        grads = {
            "norm_g": lambda: wide(SB_NORM_G), "w_in": lambda: rin_ref[...], "b_f": lambda: sred_ref[SB_B_F:SB_B_F + 1, 0:FOX_HEADS],
            "pool_scale": lambda: sred_ref[SB_POOL_SCALE:SB_POOL_SCALE + 1, :], "fox_q_g": lambda: head(SB_FOX_Q),
            "fox_k_g": lambda: head(SB_FOX_K), "mem_norm_g": lambda: wide(SB_MEM_NORM_G), "w_mem_kv": lambda: rkv_ref[...],
            "mem_q_g": lambda: head(SB_MEM_Q), "mem_k_g": lambda: head(SB_MEM_K), "w_out": lambda: rout_ref[...],
        }
        for p, name in enumerate(PARAM_ORDER):
            w_ref, m_ref, v_ref = ins[3 * p:3 * p + 3]
            g_out, d_out, m_out, v_out = outs[4 * p:4 * p + 4]
            if name == "w_pool":
                for grp in range(4):
                    g = sred_ref[SB_W_POOL:SB_W_POOL + 64, 64 * grp:64 * (grp + 1)]
                    delta, m_new, v_new = _adamw(w_ref[grp], g, m_ref[grp], v_ref[grp])
                    g_out[grp], d_out[grp], m_out[grp], v_out[grp] = g, delta, m_new, v_new
            elif name == "w_in":
                for j in range(8):
                    rows = pl.ds(j, IN_CHUNK, stride=8)
                    g = rin_ref[0:IN_CHUNK, 128 * j:128 * (j + 1)]
                    delta, m_new, v_new = _adamw(w_ref[rows, :], g, m_ref[rows, :], v_ref[rows, :])
                    g_out[rows, :], d_out[rows, :], m_out[rows, :], v_out[rows, :] = g, delta, m_new, v_new
            else:
                g = grads[name]()
                delta, m_new, v_new = _adamw(w_ref[...], g, m_ref[...], v_ref[...])
                g_out[...], d_out[...], m_out[...], v_out[...] = g, delta, m_new, v_new

    flat = [t for name in PARAM_ORDER for t in params[name]]
    shapes = [params[name][0].shape for name in PARAM_ORDER for _ in range(4)]
    outs = pl.pallas_call(
        body,
        name="adamw_update",
        out_shape=tuple(jax.ShapeDtypeStruct(s, F32) for s in shapes),
        in_specs=[pl.BlockSpec(memory_space=pltpu.VMEM)] * (4 + len(flat)),
        out_specs=(pl.BlockSpec(memory_space=pltpu.VMEM),) * len(shapes),
        compiler_params=pltpu.CompilerParams(vmem_limit_bytes=VMEM_LIMIT),
    )(red_in, red_kv, red_out, sred, *flat)
    return {name: outs[4 * p:4 * p + 4] for p, name in enumerate(PARAM_ORDER)}


def kernel(x, mem, norm_g, w_in, b_f, w_pool, pool_scale, fox_q_g, fox_k_g, mem_norm_g, w_mem_kv, mem_q_g, mem_k_g, w_out, loss_target, m_norm_g, m_w_in, m_b_f, m_w_pool, m_pool_scale, m_fox_q_g, m_fox_k_g, m_mem_norm_g, m_w_mem_kv, m_mem_q_g, m_mem_k_g, m_w_out, v_norm_g, v_w_in, v_b_f, v_w_pool, v_pool_scale, v_fox_q_g, v_fox_k_g, v_mem_norm_g, v_w_mem_kv, v_mem_q_g, v_mem_k_g, v_w_out):
    given = dict(norm_g=(norm_g, m_norm_g, v_norm_g), w_in=(w_in, m_w_in, v_w_in), b_f=(b_f, m_b_f, v_b_f),
                 w_pool=(w_pool, m_w_pool, v_w_pool), pool_scale=(pool_scale, m_pool_scale, v_pool_scale),
                 fox_q_g=(fox_q_g, m_fox_q_g, v_fox_q_g), fox_k_g=(fox_k_g, m_fox_k_g, v_fox_k_g),
                 mem_norm_g=(mem_norm_g, m_mem_norm_g, v_mem_norm_g), w_mem_kv=(w_mem_kv, m_w_mem_kv, v_w_mem_kv),
                 mem_q_g=(mem_q_g, m_mem_q_g, v_mem_q_g), mem_k_g=(mem_k_g, m_mem_k_g, v_mem_k_g), w_out=(w_out, m_w_out, v_w_out))
    params = {name: tuple(t[0] if t.ndim > 2 else t for t in wmv) for name, wmv in given.items()}
    params["w_in"] = tuple(t.T.reshape(IN_CHUNK * 8, 128) for t in params["w_in"])
    x2, mem2, target2 = x[0], mem[0], loss_target[0]

    seg = jnp.asarray(np.kron(np.eye(FOX_HEADS), np.full((HEAD_DIM, HEAD_DIM), 1.0 / HEAD_DIM)), BF16)
    w_my, w_kv16, w_out16, wp_bd, bf_pad, gq8, gk8, gqm4 = _gather_w_in(
        params["w_in"][0], params["w_mem_kv"][0], params["w_out"][0], params["w_pool"][0], b_f, fox_q_g, fox_k_g, mem_q_g)
    proj, h16 = _fwd_proj(x2, norm_g, w_my)
    q_aug, k_aug, v_h, kt_aug, vt_aug = _fox_prep(proj, bf_pad, gq8, gk8, seg)
    olse, lse_rows, w_kv_all, w_out_all = _fox_fwd(q_aug, k_aug, vt_aug, w_kv16, w_out16)
    km, vm = _mem_prep(mem2, mem_norm_g, w_kv_all, mem_k_g)
    mixed = _mix_fwd(proj, olse, km, vm, wp_bd, pool_scale, gqm4, seg)
    d_out, d_mixed, dw_out, loss_blk = _out_loss(mixed, x2, target2, w_out_all)

    d_ua, d_gb, d_qm, d_o, delta_rows, dwp_bd, d_scale, dkm, dvm, d_gqm = _mix_bwd(proj, olse, d_mixed, km, vm, wp_bd, pool_scale,
                                                                                    gqm4, seg)
    dw_kv, d_mem_g, d_gkm = _mem_bwd(mem2, mem_norm_g, w_kv_all, mem_k_g, dkm, dvm)
    dq_aug, dk_aug, dv_h, land_kv, land_out = _fox_bwd(q_aug, k_aug, kt_aug, v_h, d_o, lse_rows, delta_rows, dw_kv, dw_out)
    d_q, d_k, d_v, d_f, d_gq, d_gk, d_bf = _fox_post(proj, bf_pad, gq8, gk8, seg, dq_aug, dk_aug, dv_h)
    pieces = (d_ua, d_q, d_k, d_v, d_gb, d_qm)
    dw_in = _grad_w_in(h16, pieces, d_f)
    grad_x, red_in, red_kv, red_out, sred = _grad_x_exchange(pieces, d_f, w_my, x2, norm_g, d_out, dw_in, land_kv, land_out, d_mem_g,
                                                             d_scale, d_bf, d_gq, d_gk, d_gqm, d_gkm, dwp_bd, loss_blk)
    new = _update(red_in, red_kv, red_out, sred, params)
    result = [sred[SB_LOSS, 0], grad_x[None]]
    for kind in range(4):
        for name in PARAM_ORDER:
            out = new[name][kind]
            if name == "w_in":
                out = out.reshape(IN_CHUNK, D_MODEL).T
            result.append(out[None] if given[name][0].ndim > 2 else out)
    return tuple(result)
```

```python
import functools

import jax
import jax.numpy as jnp
import numpy as np
from jax import lax
from jax.experimental import pallas as pl
from jax.experimental.pallas import tpu as pltpu

F32 = jnp.float32
BF16 = jnp.bfloat16
MESH = pl.DeviceIdType.MESH

N_DEV = 8
D_MODEL = 1024
HEAD_DIM = 64
FOX_HEADS = 8
MEM_HEADS = 4
N_MEM = 256
POOL_WIDTH = 256
EPS = 1e-6
IN_WIDTH = 3080
IN_CHUNK = IN_WIDTH // N_DEV
CHUNK_ROWS = 400
F_OFF = 2048
MY_WIDTH = 3200
MY_F_OFF = 3072
PIECE = 512
TM = 256
TMM = 512
TA = 256
HB = 8
HB_FWD = 8
AUG_ROWS = 72
HALO = 16
VMEM_LIMIT = 56 * 1024 * 1024

ADAM_LR = 0.001
ADAM_B1 = 0.9
ADAM_B2 = 0.999
ADAM_EPS = 1e-08
ADAM_WD = 0.01
ADAM_STEP = 10


def _dot(a, b):
    return jnp.dot(a, b, preferred_element_type=F32)


def _dot_nt(a, b):
    return lax.dot_general(a, b, (((1,), (1,)), ((), ())), preferred_element_type=F32)


def _dot_tn(a, b):
    return lax.dot_general(a, b, (((0,), (0,)), ((), ())), preferred_element_type=F32)


def _sigmoid(g):
    return 0.5 + 0.5 * jnp.tanh(0.5 * g)


def _split3(v):
    hi = v.astype(BF16)
    r1 = v - hi.astype(F32)
    mid = r1.astype(BF16)
    lo = (r1 - mid.astype(F32)).astype(BF16)
    return hi, mid, lo


def _rms(v):
    return lax.rsqrt(jnp.mean(v * v, axis=-1, keepdims=True) + EPS)


def _rms_bwd(a, r, g, dy):
    da = dy * g
    return r * (da - a * jnp.mean(da * a, axis=-1, keepdims=True)), dy * a


def _head_mean(z, seg):
    hi = z.astype(BF16)
    lo = (z - hi.astype(F32)).astype(BF16)
    if z.shape[1] == 256:
        return _dot(hi, seg) + _dot(lo, seg)
    half = seg[0:256, 0:256]
    return jnp.concatenate([_dot(hi[:, 0:256], half) + _dot(lo[:, 0:256], half),
                            _dot(hi[:, 256:512], half) + _dot(lo[:, 256:512], half)], axis=1)


def _head_rms(v, seg):
    return lax.rsqrt(_head_mean(v * v, seg) + EPS)


def _head_rms_bwd(a, r, g, dy, seg):
    da = dy * g
    return r * (da - a * _head_mean(da * a, seg)), dy * a


def _fold_heads(row, n_heads):
    out = row[:, 0:HEAD_DIM]
    for h in range(1, n_heads):
        out = out + row[:, HEAD_DIM * h:HEAD_DIM * (h + 1)]
    return out


def _params(sem, limit=VMEM_LIMIT):
    return pltpu.CompilerParams(dimension_semantics=sem, vmem_limit_bytes=limit)


def _full(shape):
    return pl.BlockSpec(shape, lambda *_: (0,) * len(shape))


def _chunk_segments(d):
    runs = []
    for lo, hi, shift in ((0, F_OFF, 0), (F_OFF, F_OFF + FOX_HEADS, MY_F_OFF - F_OFF), (F_OFF + FOX_HEADS, IN_WIDTH, -FOX_HEADS)):
        a, b = max(lo, IN_CHUNK * d), min(hi, IN_CHUNK * (d + 1))
        if a < b:
            runs.append((a + shift, b - a))
    return runs


def _my_place():
    x, y, c = lax.axis_index("x"), lax.axis_index("y"), lax.axis_index("c")
    return x, y, c, 4 * x + 2 * y + c


def _shard_rows(dev):
    return pl.ds(pl.multiple_of(128 * dev, 128), 128)


def _gather_w_in(w_in, w_kv, w_out, w_pool, b_f, gq, gk, gqm):
    def body(win_ref, wkv_ref, wout_ref, wp_ref, bf_ref, gq_ref, gk_ref, gqm_ref, wmy_ref, kv16_ref, out16_ref, wpbd_ref, bfpad_ref,
             gq8_ref, gk8_ref, gqm4_ref, in_all, pay_in, win_rows, send_sems, recv_sems, load_sem):
        x, y, c, me = _my_place()
        here, sibling = (x, y, c), (x, y, 1 - c)
        x_chip, y_chip, far_chip = (1 - x, y), (x, 1 - y), (1 - x, 1 - y)
        relay_from = (x ^ (1 - c), y ^ c)
        relay_to = (x ^ c, y ^ (1 - c))

        load = pltpu.make_async_copy(win_ref, win_rows, load_sem)
        load.start()
        load.wait()
        pay_in[...] = jnp.zeros((CHUNK_ROWS, D_MODEL), BF16)
        for j in range(8):
            pay_in[0:IN_CHUNK, 128 * j:128 * (j + 1)] = win_rows[pl.ds(j, IN_CHUNK, stride=8), :].astype(BF16)

        def copy(k, block, to, own=False):
            px, py, pc = block
            dst = in_all.at[4 * px + 2 * py + pc]
            return pltpu.make_async_remote_copy(src_ref=pay_in if own else dst, dst_ref=dst, send_sem=send_sems.at[k],
                                                recv_sem=recv_sems.at[k], device_id=to, device_id_type=MESH)

        first = [copy(0, here, sibling, own=True), copy(1, here, (*x_chip, c), own=True), copy(2, here, (*y_chip, c), own=True)]
        for cp in first:
            cp.start()
        in_all[me] = pay_in[...]
        kv16_ref[...] = wkv_ref[...].astype(BF16)
        out16_ref[...] = wout_ref[...].astype(BF16)
        wpbd_ref[...] = jnp.zeros((POOL_WIDTH, POOL_WIDTH), BF16)
        for grp in range(4):
            sl = slice(64 * grp, 64 * (grp + 1))
            wpbd_ref[sl, sl] = wp_ref[grp].astype(BF16)
        bfpad_ref[...] = jnp.zeros((1, 128), F32)
        bfpad_ref[:, 0:FOX_HEADS] = bf_ref[...]
        gq8_ref[...] = jnp.concatenate([gq_ref[...]] * FOX_HEADS, axis=1)
        gk8_ref[...] = jnp.concatenate([gk_ref[...]] * FOX_HEADS, axis=1)
        gqm4_ref[...] = jnp.concatenate([gqm_ref[...]] * MEM_HEADS, axis=1)
        copy(1 + c, (*relay_from, c), here).wait_recv()
        passed = [copy(3, (*relay_from, c), (*relay_to, c)), copy(4, (*relay_from, c), sibling)]
        for cp in passed:
            cp.start()
        copy(2 - c, (*relay_to, c), here).wait_recv()
        passed.append(copy(5, (*relay_to, c), sibling))
        passed[-1].start()
        copy(3, (*far_chip, c), here).wait_recv()
        passed.append(copy(6, (*far_chip, c), sibling))
        passed[-1].start()
        copy(0, sibling, here).wait_recv()
        for k, chip in ((4, relay_to), (5, relay_from), (6, far_chip)):
            copy(k, (*chip, 1 - c), here).wait_recv()
        for cp in first + passed:
            cp.wait_send()

        row_pad = jnp.zeros((512 - CHUNK_ROWS, 256), F32)
        for r0 in range(0, D_MODEL, 256):
            ch = [jnp.concatenate([in_all[d, :, r0:r0 + 256].astype(F32), row_pad], axis=0).T[:, 0:IN_CHUNK] for d in range(N_DEV)]
            lo = F_OFF - 5 * IN_CHUNK
            full = jnp.concatenate(ch[:5] + [ch[5][:, :lo], ch[5][:, lo + FOX_HEADS:], ch[6], ch[7], ch[5][:, lo:lo + FOX_HEADS],
                                             jnp.zeros((256, MY_WIDTH - IN_WIDTH), F32)], axis=1)
            wmy_ref[r0:r0 + 256, :] = full.astype(BF16)

    return pl.pallas_call(
        body,
        name="gather_w_in",
        out_shape=(jax.ShapeDtypeStruct((D_MODEL, MY_WIDTH), BF16), jax.ShapeDtypeStruct((128, 512), BF16),
                   jax.ShapeDtypeStruct((128, D_MODEL), BF16), jax.ShapeDtypeStruct((POOL_WIDTH, POOL_WIDTH), BF16),
                   jax.ShapeDtypeStruct((1, 128), F32), jax.ShapeDtypeStruct((1, PIECE), F32), jax.ShapeDtypeStruct((1, PIECE), F32),
                   jax.ShapeDtypeStruct((1, 256), F32)),
        in_specs=[pl.BlockSpec(memory_space=pl.ANY)] + [pl.BlockSpec(memory_space=pltpu.VMEM)] * 7,
        out_specs=(pl.BlockSpec(memory_space=pltpu.VMEM),) * 8,
        scratch_shapes=[
            pltpu.VMEM((N_DEV, CHUNK_ROWS, D_MODEL), BF16),
            pltpu.VMEM((CHUNK_ROWS, D_MODEL), BF16),
            pltpu.VMEM((IN_CHUNK * 8, 128), F32),
            pltpu.SemaphoreType.DMA((7,)),
            pltpu.SemaphoreType.DMA((7,)),
            pltpu.SemaphoreType.DMA,
        ],
        compiler_params=pltpu.CompilerParams(vmem_limit_bytes=VMEM_LIMIT),
    )(w_in, w_kv, w_out, w_pool, b_f, gq, gk, gqm)


def _row_block_exchange(kv_ref, out_ref, kv_dst, out_dst, send_sems, recv_sems, local_sems, gather):
    x, y, c, me = _my_place()
    remote = []
    for k in range(1, N_DEV):
        px, py, pc = x ^ (k >> 2), y ^ ((k >> 1) & 1), c ^ (k & 1)
        peer = 4 * px + 2 * py + pc
        for fam, (src, dst) in enumerate(((kv_ref, kv_dst), (out_ref, out_dst))):
            remote.append(pltpu.make_async_remote_copy(
                src_ref=src if gather else src.at[_shard_rows(peer)], dst_ref=dst.at[_shard_rows(me)] if gather else dst.at[me],
                send_sem=send_sems.at[7 * fam + k - 1], recv_sem=recv_sems.at[7 * fam + k - 1],
                device_id=(px, py, pc), device_id_type=MESH))
    local = [pltpu.make_async_copy(src if gather else src.at[_shard_rows(me)], dst.at[_shard_rows(me)] if gather else dst.at[me],
                                   local_sems.at[fam]) for fam, (src, dst) in enumerate(((kv_ref, kv_dst), (out_ref, out_dst)))]
    return remote, local


SB_ROWS, SB_W = 80, 256
SB_NORM_G, SB_MEM_NORM_G, SB_POOL_SCALE, SB_B_F, SB_FOX_Q, SB_FOX_K, SB_MEM_Q, SB_MEM_K, SB_LOSS, SB_W_POOL = 0, 4, 8, 9, 10, 11, 12, 13, 14, 16


def _fwd_proj(x, norm_g, w_my):
    s_len = x.shape[0]

    def body(x_ref, g_ref, w_ref, proj_ref, h_ref):
        xv = x_ref[...]
        h = (xv * _rms(xv) * g_ref[...]).astype(BF16)
        h_ref[...] = h
        proj_ref[...] = _dot(h, w_ref[...])

    return pl.pallas_call(
        body,
        name="fwd_proj",
        grid=(s_len // TMM,),
        in_specs=[pl.BlockSpec((TMM, D_MODEL), lambda i: (i, 0)), _full((1, D_MODEL)), _full((D_MODEL, MY_WIDTH))],
        out_specs=[pl.BlockSpec((TMM, MY_WIDTH), lambda i: (i, 0)), pl.BlockSpec((TMM, D_MODEL), lambda i: (i, 0))],
        out_shape=[jax.ShapeDtypeStruct((s_len, MY_WIDTH), F32), jax.ShapeDtypeStruct((s_len, D_MODEL), BF16)],
        compiler_params=_params(("parallel",)),
    )(x, norm_g, w_my)


def _log_sigmoid(z):
    return jnp.minimum(z, 0.0) - jnp.log(1.0 + jnp.exp(-jnp.abs(z)))


def _forget_lane_maps():
    to_q = np.zeros((128, FOX_HEADS * 128), np.float32)
    to_k = np.zeros((128, FOX_HEADS * 128), np.float32)
    for h in range(FOX_HEADS):
        for term in range(3):
            to_q[8 * term + h, 128 * h + 64 + term] = 1.0
            to_q[24, 128 * h + 67 + term] = 1.0
            to_k[24, 128 * h + 64 + term] = 1.0
            to_k[8 * term + h, 128 * h + 67 + term] = -1.0
    return jnp.asarray(to_q, BF16), jnp.asarray(to_k, BF16)


def _fox_prep(proj, bf_pad, gq, gk, seg):
    s_len = proj.shape[0]
    to_q, to_k = _forget_lane_maps()

    def body(q_ref, k_ref, v_ref, f_ref, bf_ref, gq_ref, gk_ref, seg_ref, eq_ref, ek_ref,
             qa_ref, ka_ref, vh_ref, kt_ref, vt_ref, carry_ref):
        @pl.when(pl.program_id(0) == 0)
        def _():
            carry_ref[...] = jnp.zeros((1, 128), F32)

        lane = lax.broadcasted_iota(jnp.int32, (TM, 128), 1)
        logf = jnp.where(lane < FOX_HEADS, _log_sigmoid(f_ref[...] + bf_ref[...]), 0.0)
        row = lax.broadcasted_iota(jnp.int32, (TM, TM), 0)
        col = lax.broadcasted_iota(jnp.int32, (TM, TM), 1)
        tri = jnp.where(col <= row, 1.0, 0.0).astype(BF16)
        hi, mid, lo = _split3(logf)
        cum = _dot(tri, hi) + _dot(tri, mid) + _dot(tri, lo) + carry_ref[...]
        carry_ref[...] = cum[TM - 1:TM, :]
        f_hi, f_mid, f_lo = [t.astype(F32) for t in _split3(cum)]
        packed = (f_hi + pltpu.roll(f_mid, 8, 1) + pltpu.roll(f_lo, 16, 1) + jnp.where(lane == 24, 1.0, 0.0)).astype(BF16)
        extra_q = _dot(packed, eq_ref[...])
        extra_k = _dot(packed, ek_ref[...])
        one_at_64 = jnp.where(lane == 64, 1.0, 0.0)
        zeros = jnp.zeros((TM, HEAD_DIM), F32)
        q_all = q_ref[...]
        k_all = k_ref[...]
        qn_all = q_all * _head_rms(q_all, seg_ref[...]) * gq_ref[...] * 0.125
        kn_all = k_all * _head_rms(k_all, seg_ref[...]) * gk_ref[...]
        for h in range(FOX_HEADS):
            sl = slice(HEAD_DIM * h, HEAD_DIM * (h + 1))
            tile = slice(128 * h, 128 * (h + 1))
            qa = jnp.where(lane < 64, jnp.concatenate([qn_all[:, sl], zeros], axis=1), extra_q[:, tile])
            ka = jnp.where(lane < 64, jnp.concatenate([kn_all[:, sl], zeros], axis=1), extra_k[:, tile])
            vh = v_ref[:, sl]
            v_aug = jnp.where(lane < 64, jnp.concatenate([vh, zeros], axis=1), one_at_64)
            qa_ref[h] = qa.astype(BF16)
            ka_ref[h] = ka.astype(BF16)
            vh_ref[h] = vh.astype(BF16)
            kt_ref[h, 0] = ka.T.astype(BF16)
            vt_ref[h, 0] = v_aug.T.astype(BF16)

    col_blk = lambda j: pl.BlockSpec((TM, PIECE), lambda i: (i, j))
    head_blk = lambda w: pl.BlockSpec((FOX_HEADS, TM, w), lambda i: (0, i, 0))
    tile_blk = pl.BlockSpec((FOX_HEADS, 1, 128, TM), lambda i: (0, i, 0, 0))
    tiled = jax.ShapeDtypeStruct((FOX_HEADS, s_len // TM, 128, TM), BF16)
    return pl.pallas_call(
        body,
        name="fox_prep",
        grid=(s_len // TM,),
        in_specs=[col_blk(1), col_blk(2), col_blk(3), pl.BlockSpec((TM, 128), lambda i: (i, MY_F_OFF // 128)),
                  _full((1, 128)), _full((1, PIECE)), _full((1, PIECE)), _full((PIECE, PIECE)),
                  _full((128, FOX_HEADS * 128)), _full((128, FOX_HEADS * 128))],
        out_specs=[head_blk(128), head_blk(128), head_blk(HEAD_DIM), tile_blk, tile_blk],
        out_shape=[jax.ShapeDtypeStruct((FOX_HEADS, s_len, 128), BF16), jax.ShapeDtypeStruct((FOX_HEADS, s_len, 128), BF16),
                   jax.ShapeDtypeStruct((FOX_HEADS, s_len, HEAD_DIM), BF16), tiled, tiled],
        scratch_shapes=[pltpu.VMEM((1, 128), F32)],
        compiler_params=_params(("arbitrary",)),
    )(proj, proj, proj, proj, bf_pad, gq, gk, seg, to_q, to_k)


def _mem_prep(mem, g, w_kv, gk):
    def body(mem_ref, g_ref, w_ref, gk_ref, km_ref, vm_ref):
        m = mem_ref[...]
        kv = _dot((m * _rms(m) * g_ref[...]).astype(BF16), w_ref[...])
        for h in range(MEM_HEADS):
            sl = slice(HEAD_DIM * h, HEAD_DIM * (h + 1))
            kh = kv[:, sl]
            km_ref[:, sl] = (kh * _rms(kh) * gk_ref[...]).astype(BF16)
        vm_ref[...] = kv[:, 256:512].astype(BF16)

    return pl.pallas_call(
        body,
        name="mem_prep",
        out_shape=(jax.ShapeDtypeStruct((N_MEM, 256), BF16),) * 2,
        in_specs=[pl.BlockSpec(memory_space=pltpu.VMEM)] * 4,
        out_specs=(pl.BlockSpec(memory_space=pltpu.VMEM),) * 2,
        compiler_params=pltpu.CompilerParams(vmem_limit_bytes=VMEM_LIMIT),
    )(mem, g, w_kv, gk)


def _causal_mask():
    row = lax.broadcasted_iota(jnp.int32, (TA, TA), 0)
    col = lax.broadcasted_iota(jnp.int32, (TA, TA), 1)
    return col <= row


def _causal_mask_t():
    row = lax.broadcasted_iota(jnp.int32, (TA, TA), 0)
    col = lax.broadcasted_iota(jnp.int32, (TA, TA), 1)
    return row <= col


def _fox_fwd(q_aug, k_aug, vt_aug, w_kv16, w_out16):
    s_len = q_aug.shape[1]
    n_tiles = s_len // TA
    hb = HB_FWD
    n_groups = FOX_HEADS // hb

    def body(q_ref, k_new, vt_new, kv16_ref, out16_ref, o_ref, st_ref, kv_all, out_all, k_ref, vt_ref, send_sems, recv_sems, local_sems):
        qi = pl.program_id(1)
        for h in range(hb):
            k_ref[h, pl.ds(pl.multiple_of(qi * TA, TA), TA), :] = k_new[h]
            vt_ref[h, qi] = vt_new[h, 0]
        remote, local = _row_block_exchange(kv16_ref, out16_ref, kv_all, out_all, send_sems, recv_sems, local_sems, gather=True)

        @pl.when((pl.program_id(0) == 0) & (qi == 0))
        def _():
            for cp in remote + local:
                cp.start()

        qs = [q_ref[h] for h in range(hb)]

        def step(t0, carry, masked, width):
            rows = pl.ds(pl.multiple_of(t0 * TA, TA), width * TA)
            scores = [_dot_nt(k_ref[h, rows, :], qs[h]) for h in range(hb)]
            soft = []
            for h in range(hb):
                m, _ = carry[h]
                s = jnp.where(_causal_mask_t(), scores[h], -1e30) if masked else scores[h]
                m_new = jnp.maximum(m, jnp.max(s, axis=0, keepdims=True))
                soft.append((m_new, jnp.exp(m - m_new), jnp.exp(s - m_new).astype(BF16)))
            out = []
            for h, (m_new, alpha, p) in enumerate(soft):
                acc = alpha * carry[h][1]
                for t in range(width):
                    acc = acc + _dot(vt_ref[h, t0 + t, 0:AUG_ROWS, :], p[t * TA:(t + 1) * TA])
                out.append((m_new, acc))
            return tuple(out)

        init = tuple((jnp.full((1, TA), -1e30, F32), jnp.zeros((AUG_ROWS, TA), F32)) for _ in range(hb))
        carry = lax.fori_loop(0, qi // 2, lambda j, cr: step(2 * j, cr, False, 2), init)
        carry = lax.fori_loop(2 * (qi // 2), qi, lambda j, cr: step(j, cr, False, 1), carry)
        carry = step(qi, carry, True, 1)
        for h in range(hb):
            m, acc = carry[h]
            l = acc[HEAD_DIM:HEAD_DIM + 1, :]
            lse = m + jnp.log(l)
            o_ref[h] = jnp.concatenate([acc[0:HEAD_DIM] / l, jnp.broadcast_to(lse, (HEAD_DIM, TA))], axis=0).T
            st_ref[h, 0] = jnp.broadcast_to(lse, (8, TA))

        @pl.when((pl.program_id(0) == n_groups - 1) & (qi == n_tiles - 1))
        def _():
            for cp in remote:
                cp.wait_recv()
            for cp in remote:
                cp.wait_send()
            for cp in local:
                cp.wait()

    hbm = pl.BlockSpec(memory_space=pl.ANY)
    return pl.pallas_call(
        body,
        name="fox_fwd",
        grid=(n_groups, n_tiles),
        in_specs=[pl.BlockSpec((hb, TA, 128), lambda g, i: (g, i, 0)), pl.BlockSpec((hb, TA, 128), lambda g, i: (g, i, 0)),
                  pl.BlockSpec((hb, 1, 128, TA), lambda g, i: (g, i, 0, 0)), hbm, hbm],
        out_specs=[pl.BlockSpec((hb, TA, 128), lambda g, i: (g, i, 0)), pl.BlockSpec((hb, 1, 8, TA), lambda g, i: (g, i, 0, 0)),
                   hbm, hbm],
        out_shape=[jax.ShapeDtypeStruct((FOX_HEADS, s_len, 128), F32), jax.ShapeDtypeStruct((FOX_HEADS, n_tiles, 8, TA), F32),
                   jax.ShapeDtypeStruct((D_MODEL, 512), BF16), jax.ShapeDtypeStruct((D_MODEL, D_MODEL), BF16)],
        scratch_shapes=[pltpu.VMEM((hb, s_len, 128), BF16), pltpu.VMEM((hb, n_tiles, 128, TA), BF16),
                        pltpu.SemaphoreType.DMA((14,)), pltpu.SemaphoreType.DMA((14,)), pltpu.SemaphoreType.DMA((2,))],
        compiler_params=_params(("arbitrary", "arbitrary")),
    )(q_aug, k_aug, vt_aug, w_kv16, w_out16)


def _lane_group(lane, a, b, c, d):
    return jnp.where(lane < 64, a, jnp.where(lane < 128, b, jnp.where(lane < 192, c, d)))


def _pool_count(row0):
    lane = lax.broadcasted_iota(jnp.int32, (TM, POOL_WIDTH), 1)
    t1 = row0 + lax.broadcasted_iota(jnp.int32, (TM, POOL_WIDTH), 0) + 1
    return 1.0 / jnp.minimum(t1, _lane_group(lane, 2, 4, 8, 16)).astype(F32), lane


def _pool_delta(u, halo, cnt, lane):
    xe = jnp.concatenate([halo, u], axis=0)
    s2 = xe + pltpu.roll(xe, 1, 0)
    s4 = s2 + pltpu.roll(s2, 2, 0)
    s8 = s4 + pltpu.roll(s4, 4, 0)
    s16 = s8 + pltpu.roll(s8, 8, 0)
    win = _lane_group(lane, s2[HALO:], s4[HALO:], s8[HALO:], s16[HALO:])
    return win * cnt - u


def _pool_delta_bwd(dd, dd_next, cnt, cnt_next, lane):
    ee = jnp.concatenate([dd * cnt, dd_next * cnt_next], axis=0)
    n = TM + HALO
    r2 = ee + pltpu.roll(ee, n - 1, 0)
    r4 = r2 + pltpu.roll(r2, n - 2, 0)
    r8 = r4 + pltpu.roll(r4, n - 4, 0)
    r16 = r8 + pltpu.roll(r8, n - 8, 0)
    return _lane_group(lane, r2[:TM], r4[:TM], r8[:TM], r16[:TM]) - dd


def _mem_attn(qm, gq, seg, km_ref, vm_ref):
    r = _head_rms(qm, seg)
    a = qm * r
    q16 = (a * gq * 0.125).astype(BF16)
    heads = []
    for h in range(MEM_HEADS):
        sl = slice(HEAD_DIM * h, HEAD_DIM * (h + 1))
        s = _dot_nt(q16[:, sl], km_ref[:, sl])
        e = jnp.exp(s - jnp.max(s, axis=-1, keepdims=True))
        p = e * (1.0 / jnp.sum(e, axis=-1, keepdims=True))
        heads.append((p, _dot(p.astype(BF16), vm_ref[:, sl])))
    return a, r, q16, heads


def _row_specs(s_len):
    n_halo = s_len // HALO
    piece = lambda j: pl.BlockSpec((TM, PIECE), lambda i: (i, j))
    before = lambda j: pl.BlockSpec((HALO, 256), lambda i: (jnp.maximum(i * (TM // HALO) - 1, 0), j))
    after = lambda j: pl.BlockSpec((HALO, 256), lambda i: (jnp.minimum((i + 1) * (TM // HALO), n_halo - 1), j))
    return piece, before, after


def _mix_fwd(proj, olse, km, vm, wp_bd, pool_scale, gq_m, seg):
    s_len = proj.shape[0]

    def body(ua_ref, halo_ref, gb_ref, qm_ref, ol_ref, km_ref, vm_ref, wp_ref, sc_ref, gq_ref, seg_ref, out_ref):
        i = pl.program_id(0)
        u = ua_ref[:, 0:256]
        g_a = ua_ref[:, 256:512]
        halo = jnp.where(i > 0, halo_ref[...], 0.0)
        cnt, lane = _pool_count(i * TM)
        d = _pool_delta(u, halo, cnt, lane).astype(BF16)
        y_a = _dot(d, wp_ref[...]) * sc_ref[...]
        out_ref[:, 0:256] = (y_a * (g_a * _sigmoid(g_a))).astype(BF16)
        g_b = gb_ref[...]
        y_b = jnp.concatenate([ol_ref[h][:, 0:HEAD_DIM] for h in range(FOX_HEADS)], axis=1)
        out_ref[:, 256:768] = (y_b * (g_b * _sigmoid(g_b))).astype(BF16)
        _, _, _, heads = _mem_attn(qm_ref[:, 0:256], gq_ref[...], seg_ref[0:256, 0:256], km_ref, vm_ref)
        g_m = qm_ref[:, 256:512]
        y_m = jnp.concatenate([y for _, y in heads], axis=1)
        out_ref[:, 768:1024] = (y_m * (g_m * _sigmoid(g_m))).astype(BF16)

    piece, before, _ = _row_specs(s_len)
    return pl.pallas_call(
        body,
        name="mix_fwd",
        grid=(s_len // TM,),
        in_specs=[piece(0), before(0), piece(4), piece(5), pl.BlockSpec((FOX_HEADS, TM, 128), lambda i: (0, i, 0)),
                  _full((N_MEM, 256)), _full((N_MEM, 256)), _full((256, 256)), _full((1, 256)), _full((1, 256)),
                  _full((PIECE, PIECE))],
        out_specs=pl.BlockSpec((TM, D_MODEL), lambda i: (i, 0)),
        out_shape=jax.ShapeDtypeStruct((s_len, D_MODEL), BF16),
        compiler_params=_params(("parallel",)),
    )(proj, proj, proj, proj, olse, km, vm, wp_bd, pool_scale, gq_m, seg)


def _out_loss(mixed, x, target, w_out):
    s_len = x.shape[0]

    def body(mix_ref, x_ref, t_ref, w_ref, dout_ref, dmix_ref, dw16_ref, loss_ref, dw_ref):
        @pl.when(pl.program_id(0) == 0)
        def _():
            dw_ref[...] = jnp.zeros((D_MODEL, D_MODEL), F32)
            loss_ref[...] = jnp.zeros((8, 128), F32)

        mixed16 = mix_ref[...]
        err = x_ref[...] + _dot(mixed16, w_ref[...]) - t_ref[...]
        loss_ref[...] += 0.5 * jnp.sum(jnp.mean(err * err, axis=-1, keepdims=True))
        d_out = err / float(D_MODEL)
        dout_ref[...] = d_out
        d16 = d_out.astype(BF16)
        dmix_ref[...] = _dot_nt(d16, w_ref[...])
        dw_ref[...] += _dot_tn(mixed16, d16)

        @pl.when(pl.program_id(0) == pl.num_programs(0) - 1)
        def _():
            dw16_ref[...] = dw_ref[...].astype(BF16)

    row = pl.BlockSpec((TMM, D_MODEL), lambda i: (i, 0))
    return pl.pallas_call(
        body,
        name="out_loss",
        grid=(s_len // TMM,),
        in_specs=[row, row, row, _full((D_MODEL, D_MODEL))],
        out_specs=[row, row, _full((D_MODEL, D_MODEL)), _full((8, 128))],
        out_shape=[jax.ShapeDtypeStruct((s_len, D_MODEL), F32), jax.ShapeDtypeStruct((s_len, D_MODEL), F32),
                   jax.ShapeDtypeStruct((D_MODEL, D_MODEL), BF16), jax.ShapeDtypeStruct((8, 128), F32)],
        scratch_shapes=[pltpu.VMEM((D_MODEL, D_MODEL), F32)],
        compiler_params=_params(("arbitrary",)),
    )(mixed, x, target, w_out)


def _mix_out_loss(proj, olse, km, vm, wp_bd, pool_scale, gq_m, seg, x, target, w_out):
    s_len = x.shape[0]
    n_blocks = s_len // TMM
    halves = TMM // TM

    def body(ua_ref, halo_ref, gb_ref, qm_ref, ol_ref, km_ref, vm_ref, wp_ref, sc_ref, gq_ref, seg_ref, x_ref, t_ref, w_ref,
             dout_ref, dmix_ref, dw16_ref, loss_ref, mixed_scr, dw_ref):
        s = pl.program_id(0)

        @pl.when(s == 0)
        def _():
            dw_ref[...] = jnp.zeros((D_MODEL, D_MODEL), F32)
            loss_ref[...] = jnp.zeros((8, 128), F32)
            mixed_scr[...] = jnp.zeros((2, TMM, D_MODEL), BF16)

        mixed16 = mixed_scr[(s + 1) % 2]
        err = x_ref[...] + _dot(mixed16, w_ref[...]) - t_ref[...]
        loss_ref[...] += jnp.where(s >= 1, 0.5 * jnp.sum(jnp.mean(err * err, axis=-1, keepdims=True)), 0.0)
        d_out = err / float(D_MODEL)
        dout_ref[...] = d_out
        d16 = d_out.astype(BF16)
        dmix_ref[...] = _dot_nt(d16, w_ref[...])
        dw_ref[...] += _dot_tn(mixed16, d16)

        slot = s % 2
        for half in range(halves):
            r0 = TM * half
            rows = slice(r0, r0 + TM)
            u = ua_ref[rows, 0:256]
            g_a = ua_ref[rows, 256:512]
            halo = jnp.where(s > 0, halo_ref[...], 0.0) if half == 0 else ua_ref[r0 - HALO:r0, 0:256]
            cnt, lane = _pool_count(s * TMM + r0)
            d = _pool_delta(u, halo, cnt, lane).astype(BF16)
            y_a = _dot(d, wp_ref[...]) * sc_ref[...]
            mixed_scr[slot, rows, 0:256] = (y_a * (g_a * _sigmoid(g_a))).astype(BF16)
            g_b = gb_ref[rows, :]
            y_b = jnp.concatenate([ol_ref[h, rows, 0:HEAD_DIM] for h in range(FOX_HEADS)], axis=1)
            mixed_scr[slot, rows, 256:768] = (y_b * (g_b * _sigmoid(g_b))).astype(BF16)
            _, _, _, heads = _mem_attn(qm_ref[rows, 0:256], gq_ref[...], seg_ref[0:256, 0:256], km_ref, vm_ref)
            g_m = qm_ref[rows, 256:512]
            y_m = jnp.concatenate([y for _, y in heads], axis=1)
            mixed_scr[slot, rows, 768:1024] = (y_m * (g_m * _sigmoid(g_m))).astype(BF16)

        @pl.when(s == n_blocks)
        def _():
            dw16_ref[...] = dw_ref[...].astype(BF16)

    build = lambda s: jnp.minimum(s, n_blocks - 1)
    use = lambda s: jnp.maximum(s - 1, 0)
    piece = lambda j: pl.BlockSpec((TMM, PIECE), lambda s: (build(s), j))
    halo_blk = pl.BlockSpec((HALO, 256), lambda s: (jnp.maximum(build(s) * (TMM // HALO) - 1, 0), 0))
    row = pl.BlockSpec((TMM, D_MODEL), lambda s: (use(s), 0))
    return pl.pallas_call(
        body,
        name="mix_out_loss",
        grid=(n_blocks + 1,),
        in_specs=[piece(0), halo_blk, piece(4), piece(5), pl.BlockSpec((FOX_HEADS, TMM, 128), lambda s: (0, build(s), 0)),
                  _full((N_MEM, 256)), _full((N_MEM, 256)), _full((256, 256)), _full((1, 256)), _full((1, 256)),
                  _full((PIECE, PIECE)), row, row, _full((D_MODEL, D_MODEL))],
        out_specs=[row, row, _full((D_MODEL, D_MODEL)), _full((8, 128))],
        out_shape=[jax.ShapeDtypeStruct((s_len, D_MODEL), F32), jax.ShapeDtypeStruct((s_len, D_MODEL), F32),
                   jax.ShapeDtypeStruct((D_MODEL, D_MODEL), BF16), jax.ShapeDtypeStruct((8, 128), F32)],
        scratch_shapes=[pltpu.VMEM((2, TMM, D_MODEL), BF16), pltpu.VMEM((D_MODEL, D_MODEL), F32)],
        compiler_params=_params(("arbitrary",)),
    )(proj, proj, proj, proj, olse, km, vm, wp_bd, pool_scale, gq_m, seg, x, target, w_out)


def _silu_pair(g):
    s = _sigmoid(g)
    return g * s, s * (1.0 + g * (1.0 - s))


def _mix_bwd(proj, olse, d_mixed, km, vm, wp_bd, pool_scale, gq_m, seg):
    s_len = proj.shape[0]
    n_tiles = s_len // TM

    def body(ua_ref, halo_ref, ga_next_ref, gb_ref, qm_ref, ol_ref, dm_ref, dm_next_ref, km_ref, vm_ref, wp_ref, sc_ref, gq_ref,
             seg_ref, dua_ref, dgb_ref, dqm_ref, do_ref, dl_ref, dwp_ref, dsc_ref, dkm_ref, dvm_ref, dgq_ref):
        i = pl.program_id(0)

        @pl.when(i == 0)
        def _():
            dwp_ref[...] = jnp.zeros((256, 256), F32)
            dsc_ref[...] = jnp.zeros((1, 256), F32)
            dkm_ref[...] = jnp.zeros((N_MEM, 256), F32)
            dvm_ref[...] = jnp.zeros((N_MEM, 256), F32)
            dgq_ref[...] = jnp.zeros((1, HEAD_DIM), F32)

        u = ua_ref[:, 0:256]
        g_a = ua_ref[:, 256:512]
        halo = jnp.where(i > 0, halo_ref[...], 0.0)
        cnt, lane = _pool_count(i * TM)
        d16 = _pool_delta(u, halo, cnt, lane).astype(BF16)
        t = _dot(d16, wp_ref[...])
        y_a = t * sc_ref[...]
        silu_a, dsilu_a = _silu_pair(g_a)
        dm_a = dm_ref[:, 0:256]
        dy_a = dm_a * silu_a
        dsc_ref[...] += jnp.sum(dy_a * t, axis=0, keepdims=True)
        dt16 = (dy_a * sc_ref[...]).astype(BF16)
        dwp_ref[...] += _dot_tn(d16, dt16)
        dd = _dot_nt(dt16, wp_ref[...])
        g_next = ga_next_ref[...]
        dt_next = (dm_next_ref[...] * (g_next * _sigmoid(g_next)) * sc_ref[...]).astype(BF16)
        dd_next = jnp.where(i < n_tiles - 1, _dot_nt(dt_next, wp_ref[...]), 0.0)
        cnt_next = _pool_count((i + 1) * TM)[0][0:HALO]
        dua_ref[:, 0:256] = _pool_delta_bwd(dd, dd_next, cnt, cnt_next, lane).astype(BF16)
        dua_ref[:, 256:512] = (dm_a * y_a * dsilu_a).astype(BF16)

        silu_b, dsilu_b = _silu_pair(gb_ref[...])
        dm_b = dm_ref[:, 256:768]
        o_all = jnp.concatenate([ol_ref[h][:, 0:HEAD_DIM] for h in range(FOX_HEADS)], axis=1)
        d_o = dm_b * silu_b
        dgb_ref[...] = (dm_b * o_all * dsilu_b).astype(BF16)
        for h in range(FOX_HEADS):
            do_ref[h] = d_o[:, HEAD_DIM * h:HEAD_DIM * (h + 1)].astype(BF16)
        prod = d_o * o_all
        hi = prod.astype(BF16)
        lo = (prod - hi.astype(F32)).astype(BF16)
        head_of_lane = lax.broadcasted_iota(jnp.int32, (FOX_HEADS, PIECE), 1) // HEAD_DIM
        pick = jnp.where(head_of_lane == lax.broadcasted_iota(jnp.int32, (FOX_HEADS, PIECE), 0), 1.0, 0.0).astype(BF16)
        delta = _dot_nt(pick, hi) + _dot_nt(pick, lo)
        for h in range(FOX_HEADS):
            dl_ref[h, 0] = jnp.broadcast_to(delta[h:h + 1, :], (8, TM))

        seg = seg_ref[0:256, 0:256]
        a, r, q16, heads = _mem_attn(qm_ref[:, 0:256], gq_ref[...], seg, km_ref, vm_ref)
        silu_m, dsilu_m = _silu_pair(qm_ref[:, 256:512])
        dm_m = dm_ref[:, 768:1024]
        y_m = jnp.concatenate([y for _, y in heads], axis=1)
        dqm_ref[:, 256:512] = (dm_m * y_m * dsilu_m).astype(BF16)
        dy16_all = (dm_m * silu_m).astype(BF16)
        d_scores = []
        for h in range(MEM_HEADS):
            sl = slice(HEAD_DIM * h, HEAD_DIM * (h + 1))
            p = heads[h][0]
            dy16 = dy16_all[:, sl]
            dp = _dot_nt(dy16, vm_ref[:, sl])
            dvm_ref[:, sl] += _dot_tn(p.astype(BF16), dy16)
            ds16 = (p * (dp - jnp.sum(dp * p, axis=-1, keepdims=True))).astype(BF16)
            dkm_ref[:, sl] += _dot_tn(ds16, q16[:, sl])
            d_scores.append(_dot(ds16, km_ref[:, sl]))
        dq, dg_rows = _head_rms_bwd(a, r, gq_ref[...], jnp.concatenate(d_scores, axis=1) * 0.125, seg)
        dqm_ref[:, 0:256] = dq.astype(BF16)
        dgq_ref[...] += _fold_heads(jnp.sum(dg_rows, axis=0, keepdims=True), MEM_HEADS)

    piece, before, after = _row_specs(s_len)
    n_halo = s_len // HALO
    row = pl.BlockSpec((TM, D_MODEL), lambda i: (i, 0))
    dm_after = pl.BlockSpec((HALO, 256), lambda i: (jnp.minimum((i + 1) * (TM // HALO), n_halo - 1), 0))
    out_piece = pl.BlockSpec((TM, PIECE), lambda i: (i, 0))
    return pl.pallas_call(
        body,
        name="mix_bwd",
        grid=(n_tiles,),
        in_specs=[piece(0), before(0), after(1), piece(4), piece(5), pl.BlockSpec((FOX_HEADS, TM, 128), lambda i: (0, i, 0)),
                  row, dm_after, _full((N_MEM, 256)), _full((N_MEM, 256)), _full((256, 256)), _full((1, 256)), _full((1, 256)),
                  _full((PIECE, PIECE))],
        out_specs=[out_piece, out_piece, out_piece, pl.BlockSpec((FOX_HEADS, TM, HEAD_DIM), lambda i: (0, i, 0)),
                   pl.BlockSpec((FOX_HEADS, 1, 8, TM), lambda i: (0, i, 0, 0)),
                   _full((256, 256)), _full((1, 256)), _full((N_MEM, 256)), _full((N_MEM, 256)), _full((1, HEAD_DIM))],
        out_shape=[jax.ShapeDtypeStruct((s_len, PIECE), BF16)] * 3 + [
            jax.ShapeDtypeStruct((FOX_HEADS, s_len, HEAD_DIM), BF16),
            jax.ShapeDtypeStruct((FOX_HEADS, n_tiles, 8, TM), F32),
            jax.ShapeDtypeStruct((256, 256), F32), jax.ShapeDtypeStruct((1, 256), F32),
            jax.ShapeDtypeStruct((N_MEM, 256), F32), jax.ShapeDtypeStruct((N_MEM, 256), F32),
            jax.ShapeDtypeStruct((1, HEAD_DIM), F32)],
        compiler_params=_params(("arbitrary",)),
    )(proj, proj, proj, proj, proj, olse, d_mixed, d_mixed, km, vm, wp_bd, pool_scale, gq_m, seg)


def _fox_bwd(q_aug, k_aug, kt_aug, v_h, d_o, lse_rows, delta_rows, dw_kv16, dw_out16):
    s_len = q_aug.shape[1]
    n_tiles = s_len // TA
    n_groups = FOX_HEADS // HB

    def body(q_hbm, k_ref, kt_ref, v_ref, do_hbm, st_ref, dl_ref, dkv16_ref, dout16_ref, dq_ref, dk_ref, dv_ref, land_kv, land_out,
             dqt_ref, q_ref, do_ref, send_sems, recv_sems, local_sems, tile_sems):
        j = pl.program_id(1)
        remote, local = _row_block_exchange(dkv16_ref, dout16_ref, land_kv, land_out, send_sems, recv_sems, local_sems, gather=False)

        def tile_loads(i):
            rows = pl.ds(pl.multiple_of(i * TA, TA), TA)
            return [pltpu.make_async_copy(q_hbm.at[:, rows, :], q_ref.at[:, rows, :], tile_sems.at[2 * i]),
                    pltpu.make_async_copy(do_hbm.at[:, rows, :], do_ref.at[:, rows, :], tile_sems.at[2 * i + 1])]

        @pl.when((pl.program_id(0) == 0) & (j == 0))
        def _():
            for i in range(n_tiles):
                for cp in tile_loads(i):
                    cp.start()
            for cp in remote + local:
                cp.start()

        @pl.when(j == 0)
        def _():
            dqt_ref[...] = jnp.zeros((HB, n_tiles, AUG_ROWS, TA), F32)

        ks = [k_ref[h] for h in range(HB)]
        kts = [kt_ref[h, 0, 0:AUG_ROWS, :] for h in range(HB)]
        vs = [v_ref[h] for h in range(HB)]

        def pair(i0, acc, masked, width):
            @pl.when(j == 0)
            def _():
                for t in range(width):
                    for cp in tile_loads(i0 + t):
                        cp.wait()

            rows = pl.ds(pl.multiple_of(i0 * TA, TA), width * TA)
            qs = [q_ref[h, rows, :] for h in range(HB)]
            d_os = [do_ref[h, rows, :] for h in range(HB)]
            row_stat = lambda ref, h: jnp.concatenate([ref[h, i0 + t, 0:1, :] for t in range(width)], axis=1)
            scores = [(_dot_nt(ks[h], qs[h]), _dot_nt(vs[h], d_os[h])) for h in range(HB)]
            probs = []
            for h in range(HB):
                s, dp = scores[h]
                if masked:
                    s = jnp.where(_causal_mask_t(), s, -1e30)
                p = jnp.exp(s - row_stat(st_ref, h))
                probs.append((p.astype(BF16), (p * (dp - row_stat(dl_ref, h))).astype(BF16)))
            out = []
            for h in range(HB):
                p16, ds16 = probs[h]
                dqt = _dot(kts[h], ds16)
                for t in range(width):
                    dqt_ref[h, i0 + t] += dqt[:, t * TA:(t + 1) * TA]
                out.append((acc[h][0] + _dot(ds16, qs[h]), acc[h][1] + _dot(p16, d_os[h])))
            return tuple(out)

        zero = tuple((jnp.zeros((TA, 128), F32), jnp.zeros((TA, HEAD_DIM), F32)) for _ in range(HB))
        acc = pair(j, zero, True, 1)
        odd = (n_tiles - 1 - j) % 2
        acc = lax.fori_loop(j + 1, j + 1 + odd, lambda i, a: pair(i, a, False, 1), acc)
        acc = lax.fori_loop(0, (n_tiles - 1 - j) // 2, lambda t, a: pair(j + 1 + odd + 2 * t, a, False, 2), acc)
        pad = jnp.zeros((128 - AUG_ROWS, TA), F32)
        for h in range(HB):
            dk_ref[h] = acc[h][0]
            dv_ref[h] = acc[h][1].astype(BF16)
            dq_ref[h] = jnp.concatenate([dqt_ref[h, j], pad], axis=0).T

        @pl.when((pl.program_id(0) == n_groups - 1) & (j == n_tiles - 1))
        def _():
            for cp in remote:
                cp.wait_recv()
            for cp in remote:
                cp.wait_send()
            for cp in local:
                cp.wait()

    assert n_groups == 1
    resident = pl.BlockSpec(memory_space=pltpu.VMEM)
    rows_blk = lambda r: resident
    tile = lambda w: pl.BlockSpec((HB, TA, w), lambda g, j: (g, j, 0))
    hbm = pl.BlockSpec(memory_space=pl.ANY)
    return pl.pallas_call(
        body,
        name="fox_bwd",
        grid=(n_groups, n_tiles),
        in_specs=[hbm, tile(128), pl.BlockSpec((HB, 1, 128, TA), lambda g, j: (g, j, 0, 0)), tile(HEAD_DIM),
                  hbm, rows_blk(8), rows_blk(8), hbm, hbm],
        out_specs=[tile(128), tile(128), tile(HEAD_DIM), hbm, hbm],
        out_shape=[jax.ShapeDtypeStruct((FOX_HEADS, s_len, 128), F32), jax.ShapeDtypeStruct((FOX_HEADS, s_len, 128), F32),
                   jax.ShapeDtypeStruct((FOX_HEADS, s_len, HEAD_DIM), BF16),
                   jax.ShapeDtypeStruct((N_DEV, 128, 512), BF16), jax.ShapeDtypeStruct((N_DEV, 128, D_MODEL), BF16)],
        scratch_shapes=[pltpu.VMEM((HB, n_tiles, AUG_ROWS, TA), F32),
                        pltpu.VMEM((HB, s_len, 128), BF16), pltpu.VMEM((HB, s_len, HEAD_DIM), BF16),
                        pltpu.SemaphoreType.DMA((14,)), pltpu.SemaphoreType.DMA((14,)), pltpu.SemaphoreType.DMA((2,)),
                        pltpu.SemaphoreType.DMA((2 * n_tiles,))],
        compiler_params=_params(("arbitrary", "arbitrary")),
    )(q_aug, k_aug, kt_aug, v_h, d_o, lse_rows, delta_rows, dw_kv16, dw_out16)


def _fox_post(proj, bf_pad, gq, gk, seg, dq_aug, dk_aug, dv_h):
    s_len = proj.shape[0]
    n_tiles = s_len // TM

    def body(q_ref, k_ref, f_ref, bf_ref, gq_ref, gk_ref, seg_ref, dqa_ref, dka_ref, dvh_ref,
             dq_ref, dk_ref, dv_ref, df_ref, dgq_ref, dgk_ref, dbf_ref, carry_ref):
        @pl.when(pl.program_id(0) == 0)
        def _():
            carry_ref[...] = jnp.zeros((1, 128), F32)
            dgq_ref[...] = jnp.zeros((1, HEAD_DIM), F32)
            dgk_ref[...] = jnp.zeros((1, HEAD_DIM), F32)
            dbf_ref[...] = jnp.zeros((1, 128), F32)

        lane = lax.broadcasted_iota(jnp.int32, (TM, 128), 1)
        seg = seg_ref[...]
        dqas = [dqa_ref[h] for h in range(FOX_HEADS)]
        dkas = [dka_ref[h] for h in range(FOX_HEADS)]
        d_cum = jnp.zeros((TM, 128), F32)
        for h in range(FOX_HEADS):
            d_cum = jnp.where(lane == h, dqas[h][:, 64:65] - dkas[h][:, 67:68], d_cum)
        q_all = q_ref[...]
        k_all = k_ref[...]
        rq = _head_rms(q_all, seg)
        rk = _head_rms(k_all, seg)
        dy_q = jnp.concatenate([t[:, 0:HEAD_DIM] for t in dqas], axis=1) * 0.125
        dy_k = jnp.concatenate([t[:, 0:HEAD_DIM] for t in dkas], axis=1)
        dq, dgq_rows = _head_rms_bwd(q_all * rq, rq, gq_ref[...], dy_q, seg)
        dk, dgk_rows = _head_rms_bwd(k_all * rk, rk, gk_ref[...], dy_k, seg)
        dq_ref[...] = dq.astype(BF16)
        dk_ref[...] = dk.astype(BF16)
        dv_ref[...] = jnp.concatenate([dvh_ref[h].astype(F32) for h in range(FOX_HEADS)], axis=1).astype(BF16)
        dgq_ref[...] += _fold_heads(jnp.sum(dgq_rows, axis=0, keepdims=True), FOX_HEADS)
        dgk_ref[...] += _fold_heads(jnp.sum(dgk_rows, axis=0, keepdims=True), FOX_HEADS)

        row = lax.broadcasted_iota(jnp.int32, (TM, TM), 0)
        col = lax.broadcasted_iota(jnp.int32, (TM, TM), 1)
        tri = jnp.where(col >= row, 1.0, 0.0).astype(BF16)
        hi, mid, lo = _split3(d_cum)
        d_logf = _dot(tri, hi) + _dot(tri, mid) + _dot(tri, lo) + carry_ref[...]
        carry_ref[...] = d_logf[0:1, :]
        z = f_ref[...] + bf_ref[...]
        d_f = jnp.where(lane < FOX_HEADS, d_logf / (1.0 + jnp.exp(z)), 0.0)
        df_ref[...] = d_f.astype(BF16)
        dbf_ref[...] += jnp.sum(d_f, axis=0, keepdims=True)

    rev = lambda i: n_tiles - 1 - i
    col_blk = lambda j: pl.BlockSpec((TM, PIECE), lambda i: (rev(i), j))
    head_blk = lambda w: pl.BlockSpec((FOX_HEADS, TM, w), lambda i: (0, rev(i), 0))
    out_piece = pl.BlockSpec((TM, PIECE), lambda i: (rev(i), 0))
    return pl.pallas_call(
        body,
        name="fox_post",
        grid=(n_tiles,),
        in_specs=[col_blk(1), col_blk(2), pl.BlockSpec((TM, 128), lambda i: (rev(i), MY_F_OFF // 128)),
                  _full((1, 128)), _full((1, PIECE)), _full((1, PIECE)), _full((PIECE, PIECE)),
                  head_blk(128), head_blk(128), head_blk(HEAD_DIM)],
        out_specs=[out_piece, out_piece, out_piece, pl.BlockSpec((TM, 128), lambda i: (rev(i), 0)),
                   _full((1, HEAD_DIM)), _full((1, HEAD_DIM)), _full((1, 128))],
        out_shape=[jax.ShapeDtypeStruct((s_len, PIECE), BF16)] * 3 + [
            jax.ShapeDtypeStruct((s_len, 128), BF16), jax.ShapeDtypeStruct((1, HEAD_DIM), F32),
            jax.ShapeDtypeStruct((1, HEAD_DIM), F32), jax.ShapeDtypeStruct((1, 128), F32)],
        scratch_shapes=[pltpu.VMEM((1, 128), F32)],
        compiler_params=_params(("arbitrary",)),
    )(proj, proj, proj, bf_pad, gq, gk, seg, dq_aug, dk_aug, dv_h)


def _mem_bwd(mem, g, w_kv, gk, dkm, dvm):
    def body(mem_ref, g_ref, w_ref, gk_ref, dkm_ref, dvm_ref, dw_ref, dg_ref, dgk_ref, dkv_ref):
        m = mem_ref[...]
        r = _rms(m)
        a = m * r
        mn16 = (a * g_ref[...]).astype(BF16)
        kv = _dot(mn16, w_ref[...])
        dgk = jnp.zeros((N_MEM, HEAD_DIM), F32)
        for h in range(MEM_HEADS):
            sl = slice(HEAD_DIM * h, HEAD_DIM * (h + 1))
            kh = kv[:, sl]
            rk = _rms(kh)
            dk, dg_rows = _rms_bwd(kh * rk, rk, gk_ref[...], dkm_ref[:, sl])
            dkv_ref[:, sl] = dk.astype(BF16)
            dgk = dgk + dg_rows
        dkv_ref[:, 256:512] = dvm_ref[...].astype(BF16)
        dgk_ref[...] = jnp.sum(dgk, axis=0, keepdims=True)
        dkv16 = dkv_ref[...]
        dw_ref[...] = _dot_tn(mn16, dkv16).astype(BF16)
        dg_ref[...] = jnp.sum(_dot_nt(dkv16, w_ref[...]) * a, axis=0, keepdims=True)

    return pl.pallas_call(
        body,
        name="mem_bwd",
        out_shape=(jax.ShapeDtypeStruct((D_MODEL, 512), BF16), jax.ShapeDtypeStruct((1, D_MODEL), F32),
                   jax.ShapeDtypeStruct((1, HEAD_DIM), F32)),
        in_specs=[pl.BlockSpec(memory_space=pltpu.VMEM)] * 6,
        out_specs=(pl.BlockSpec(memory_space=pltpu.VMEM),) * 3,
        scratch_shapes=[pltpu.VMEM((N_MEM, 512), BF16)],
        compiler_params=pltpu.CompilerParams(vmem_limit_bytes=VMEM_LIMIT),
    )(mem, g, w_kv, gk, dkm, dvm)


def _grad_x_exchange(pieces, d_f, w_my, x, norm_g, d_out, dw_in, land_kv, land_out, d_mem_g, d_scale, d_bf, d_gq, d_gk, d_gqm,
                     d_gkm, dwp_bd, loss_blk):
    s_len = x.shape[0]
    n_steps = s_len // TM
    step2, step3 = n_steps // 4, (11 * n_steps) // 16
    half = D_MODEL // 2

    def body(p0, p1, p2, p3, p4, p5, df_ref, x_ref, dout_ref, w_ref, g_ref, in_ref, lkv_ref, lout_ref,
             dmg, dsc, dbf, dgq, dgk, dgqm, dgkm, dwp, loss_ref,
             gx_ref, rin_ref, rkv_ref, rout_ref, sred_ref,
             dng, land1, send2a, send2b, keep2a, keep2b, land2a, land2b, send3a, send3b, land3a, land3b, sb_ref, sland,
             own4, lkv_v, lout_v, send_sems, recv_sems, load_sems):
        i = pl.program_id(0)
        x_, y_, c_, me = _my_place()
        sibling, x_nbr, y_nbr = (x_, y_, 1 - c_), (1 - x_, y_, c_), (x_, 1 - y_, c_)
        loads = [pltpu.make_async_copy(in_ref.at[2 * k + c_], own4.at[k], load_sems.at[k]) for k in range(4)]
        loads += [pltpu.make_async_copy(lkv_ref, lkv_v, load_sems.at[4]), pltpu.make_async_copy(lout_ref, lout_v, load_sems.at[5])]

        def rcopy(src, dst, sem, to):
            return pltpu.make_async_remote_copy(src_ref=src, dst_ref=dst, send_sem=send_sems.at[sem], recv_sem=recv_sems.at[sem],
                                                device_id=to, device_id_type=MESH)

        early_rows, late_rows = pl.ds(8, SB_ROWS - 8), pl.ds(0, 8)
        small_early, small_late = [], []
        for k in range(1, N_DEV):
            peer = (x_ ^ (k >> 2), y_ ^ ((k >> 1) & 1), c_ ^ (k & 1))
            small_early.append(rcopy(sb_ref.at[early_rows], sland.at[me, early_rows], k - 1, peer))
            small_late.append(rcopy(sb_ref.at[late_rows], sland.at[me, late_rows], 16 + k, peer))
        small = small_early + small_late
        stage1 = [rcopy(in_ref.at[2 * k + 1 - c_], land1.at[k], 7 + k, sibling) for k in range(4)]
        stage2 = [rcopy(send2a.at[j], land2a.at[j], 11 + j, x_nbr) for j in range(2)]
        stage2 += [rcopy(send2b.at[j], land2b.at[j], 13 + j, y_nbr) for j in range(2)]
        stage3 = [rcopy(send3a, land3a, 15, y_nbr), rcopy(send3b, land3b, 16, x_nbr)]
        top, bot = slice(0, half), slice(half, D_MODEL)

        @pl.when(i == 0)
        def _():
            dng[...] = jnp.zeros((1, D_MODEL), F32)
            for cp in stage1 + loads:
                cp.start()
            sb_ref[...] = jnp.zeros((SB_ROWS, SB_W), F32)
            sb_ref[SB_POOL_SCALE:SB_POOL_SCALE + 1, :] = dsc[...]
            sb_ref[SB_B_F:SB_B_F + 1, 0:128] = dbf[...]
            for row, ref in ((SB_FOX_Q, dgq), (SB_FOX_K, dgk), (SB_MEM_Q, dgqm), (SB_MEM_K, dgkm)):
                sb_ref[row:row + 1, 0:HEAD_DIM] = ref[...]
            sb_ref[SB_LOSS:SB_LOSS + 1, 0:128] = loss_ref[0:1, :]
            for grp in range(4):
                sl = slice(64 * grp, 64 * (grp + 1))
                sb_ref[SB_W_POOL:SB_W_POOL + 64, sl] = dwp[sl, sl]
            for cp in small_early:
                cp.start()

        @pl.when(i == step2)
        def _():
            for cp in stage1:
                cp.wait_recv()
            for cp in loads[0:4]:
                cp.wait()

            def chip_sum(k, cols):
                return own4[k, :, cols].astype(F32) + land1[k, :, cols].astype(F32)

            for j in range(2):
                send2a[j] = chip_sum(2 * (1 - x_) + j, top).astype(BF16)
                keep2a[j] = chip_sum(2 * x_ + j, top)
                send2b[j] = chip_sum(2 * j + 1 - y_, bot).astype(BF16)
                keep2b[j] = chip_sum(2 * j + y_, bot)
            for cp in stage2:
                cp.start()

        @pl.when(i == step3)
        def _():
            for cp in stage2:
                cp.wait_recv()
            for j in range(2):
                keep2a[j] = keep2a[j] + land2a[j].astype(F32)
                keep2b[j] = keep2b[j] + land2b[j].astype(F32)
            send3a[...] = keep2a[1 - y_].astype(BF16)
            send3b[...] = keep2b[1 - x_].astype(BF16)
            for cp in stage3:
                cp.start()

        dh = _dot_nt(df_ref[...], w_ref[:, MY_F_OFF:MY_WIDTH])
        for j, p in enumerate((p0, p1, p2, p3, p4, p5)):
            dh = dh + _dot_nt(p[...], w_ref[:, PIECE * j:PIECE * (j + 1)])
        xv = x_ref[...]
        r = _rms(xv)
        dx, dg_rows = _rms_bwd(xv * r, r, g_ref[...], dh)
        gx_ref[...] = dout_ref[...] + dx
        dng[...] += jnp.sum(dg_rows, axis=0, keepdims=True)

        @pl.when(i == n_steps - 1)
        def _():
            for k in range(4):
                sb_ref[SB_NORM_G + k:SB_NORM_G + k + 1, :] = dng[:, SB_W * k:SB_W * (k + 1)]
                sb_ref[SB_MEM_NORM_G + k:SB_MEM_NORM_G + k + 1, :] = dmg[:, SB_W * k:SB_W * (k + 1)]
            for cp in small_late:
                cp.start()
            sland[me] = sb_ref[...]
            for cp in stage3:
                cp.wait_recv()
            rin_ref[:, top] = keep2a[y_] + land3a[...].astype(F32)
            rin_ref[:, bot] = keep2b[x_] + land3b[...].astype(F32)
            for cp in small:
                cp.wait_recv()
            for cp in small + stage1 + stage2 + stage3:
                cp.wait_send()
            for cp in loads[4:6]:
                cp.wait()
            for land, red in ((lkv_v, rkv_ref), (lout_v, rout_ref), (sland, sred_ref)):
                acc = land[0].astype(F32)
                for d in range(1, N_DEV):
                    acc = acc + land[d].astype(F32)
                red[...] = acc

    piece = pl.BlockSpec((TM, PIECE), lambda i: (i, 0))
    row = pl.BlockSpec((TM, D_MODEL), lambda i: (i, 0))
    vmem = pl.BlockSpec(memory_space=pltpu.VMEM)
    half_chunk = lambda n, dt: pltpu.VMEM((n, CHUNK_ROWS, half), dt)
    whole = (w_my, norm_g, dw_in, land_kv, land_out, d_mem_g, d_scale, d_bf, d_gq, d_gk, d_gqm, d_gkm, dwp_bd, loss_blk)
    return pl.pallas_call(
        body,
        name="grad_x_exchange",
        grid=(n_steps,),
        in_specs=[piece] * 6 + [pl.BlockSpec((TM, 128), lambda i: (i, 0)), row, row, vmem, vmem]
        + [pl.BlockSpec(memory_space=pl.ANY)] * 3 + [vmem] * (len(whole) - 5),
        out_specs=[row, vmem, vmem, vmem, vmem],
        out_shape=[jax.ShapeDtypeStruct((s_len, D_MODEL), F32), jax.ShapeDtypeStruct((CHUNK_ROWS, D_MODEL), F32),
                   jax.ShapeDtypeStruct((128, 512), F32), jax.ShapeDtypeStruct((128, D_MODEL), F32),
                   jax.ShapeDtypeStruct((SB_ROWS, SB_W), F32)],
        scratch_shapes=[
            pltpu.VMEM((1, D_MODEL), F32),
            pltpu.VMEM((4, CHUNK_ROWS, D_MODEL), BF16),
            half_chunk(2, BF16), half_chunk(2, BF16), half_chunk(2, F32), half_chunk(2, F32), half_chunk(2, BF16), half_chunk(2, BF16),
            pltpu.VMEM((CHUNK_ROWS, half), BF16), pltpu.VMEM((CHUNK_ROWS, half), BF16),
            pltpu.VMEM((CHUNK_ROWS, half), BF16), pltpu.VMEM((CHUNK_ROWS, half), BF16),
            pltpu.VMEM((SB_ROWS, SB_W), F32),
            pltpu.VMEM((N_DEV, SB_ROWS, SB_W), F32),
            pltpu.VMEM((4, CHUNK_ROWS, D_MODEL), BF16),
            pltpu.VMEM((N_DEV, 128, 512), BF16),
            pltpu.VMEM((N_DEV, 128, D_MODEL), BF16),
            pltpu.SemaphoreType.DMA((24,)),
            pltpu.SemaphoreType.DMA((24,)),
            pltpu.SemaphoreType.DMA((6,)),
        ],
        compiler_params=_params(("arbitrary",)),
    )(*pieces, d_f, x, d_out, *whole)


def _grad_w_in(h16, pieces, d_f):
    s_len = h16.shape[0]
    tk = 512

    def body(h_ref, p0, p1, p2, p3, p4, p5, df_ref, out_ref, dw_ref):
        @pl.when(pl.program_id(0) == 0)
        def _():
            dw_ref[...] = jnp.zeros((D_MODEL, MY_WIDTH), F32)

        h = h_ref[...]
        for j, p in enumerate((p0, p1, p2, p3, p4, p5)):
            dw_ref[:, PIECE * j:PIECE * (j + 1)] += _dot_tn(h, p[...])
        dw_ref[:, MY_F_OFF:MY_WIDTH] += _dot_tn(h, df_ref[...])

        @pl.when(pl.program_id(0) == pl.num_programs(0) - 1)
        def _():
            for d in range(N_DEV):
                parts = [dw_ref[:, start:start + width] for start, width in _chunk_segments(d)]
                parts.append(jnp.zeros((D_MODEL, 512 - IN_CHUNK), F32))
                out_ref[d] = jnp.concatenate(parts, axis=1).T[0:CHUNK_ROWS].astype(BF16)

    piece = pl.BlockSpec((tk, PIECE), lambda i: (i, 0))
    return pl.pallas_call(
        body,
        name="grad_w_in",
        grid=(s_len // tk,),
        in_specs=[pl.BlockSpec((tk, D_MODEL), lambda i: (i, 0))] + [piece] * 6 + [pl.BlockSpec((tk, 128), lambda i: (i, 0))],
        out_specs=_full((N_DEV, CHUNK_ROWS, D_MODEL)),
        out_shape=jax.ShapeDtypeStruct((N_DEV, CHUNK_ROWS, D_MODEL), BF16),
        scratch_shapes=[pltpu.VMEM((D_MODEL, MY_WIDTH), F32)],
        compiler_params=_params(("arbitrary",)),
    )(h16, *pieces, d_f)


def _adamw(w, g, m, v):
    m = ADAM_B1 * m + (1.0 - ADAM_B1) * g
    v = ADAM_B2 * v + (1.0 - ADAM_B2) * (g * g)
    m_hat = m / (1.0 - ADAM_B1 ** ADAM_STEP)
    v_hat = v / (1.0 - ADAM_B2 ** ADAM_STEP)
    return -ADAM_LR * (m_hat / (jnp.sqrt(v_hat) + ADAM_EPS) + ADAM_WD * w), m, v


PARAM_ORDER = ("norm_g", "w_in", "b_f", "w_pool", "pool_scale", "fox_q_g", "fox_k_g", "mem_norm_g", "w_mem_kv", "mem_q_g", "mem_k_g", "w_out")


def _update(red_in, red_kv, red_out, sred, params):
    def body(rin_ref, rkv_ref, rout_ref, sred_ref, *refs):
        ins, outs = refs[:3 * len(PARAM_ORDER)], refs[3 * len(PARAM_ORDER):]
        wide = lambda row: jnp.concatenate([sred_ref[row + k:row + k + 1, :] for k in range(4)], axis=1)
        head = lambda row: sred_ref[row:row + 1, 0:HEAD_DIM]
        grads = {
            "norm_g": lambda: wide(SB_NORM_G), "w_in": lambda: rin_ref[...], "b_f": lambda: sred_ref[SB_B_F:SB_B_F + 1, 0:FOX_HEADS],
            "pool_scale": lambda: sred_ref[SB_POOL_SCALE:SB_POOL_SCALE + 1, :], "fox_q_g": lambda: head(SB_FOX_Q),
            "fox_k_g": lambda: head(SB_FOX_K), "mem_norm_g": lambda: wide(SB_MEM_NORM_G), "w_mem_kv": lambda: rkv_ref[...],
            "mem_q_g": lambda: head(SB_MEM_Q), "mem_k_g": lambda: head(SB_MEM_K), "w_out": lambda: rout_ref[...],
        }
        for p, name in enumerate(PARAM_ORDER):
            w_ref, m_ref, v_ref = ins[3 * p:3 * p + 3]
            g_out, d_out, m_out, v_out = outs[4 * p:4 * p + 4]
            if name == "w_pool":
                for grp in range(4):
                    g = sred_ref[SB_W_POOL:SB_W_POOL + 64, 64 * grp:64 * (grp + 1)]
                    delta, m_new, v_new = _adamw(w_ref[grp], g, m_ref[grp], v_ref[grp])
                    g_out[grp], d_out[grp], m_out[grp], v_out[grp] = g, delta, m_new, v_new
            elif name == "w_in":
                for j in range(8):
                    rows = pl.ds(j, IN_CHUNK, stride=8)
                    g = rin_ref[0:IN_CHUNK, 128 * j:128 * (j + 1)]
                    delta, m_new, v_new = _adamw(w_ref[rows, :], g, m_ref[rows, :], v_ref[rows, :])
                    g_out[rows, :], d_out[rows, :], m_out[rows, :], v_out[rows, :] = g, delta, m_new, v_new
            else:
                g = grads[name]()
                delta, m_new, v_new = _adamw(w_ref[...], g, m_ref[...], v_ref[...])
                g_out[...], d_out[...], m_out[...], v_out[...] = g, delta, m_new, v_new

    flat = [t for name in PARAM_ORDER for t in params[name]]
    shapes = [params[name][0].shape for name in PARAM_ORDER for _ in range(4)]
    outs = pl.pallas_call(
        body,
        name="adamw_update",
        out_shape=tuple(jax.ShapeDtypeStruct(s, F32) for s in shapes),
        in_specs=[pl.BlockSpec(memory_space=pltpu.VMEM)] * (4 + len(flat)),
        out_specs=(pl.BlockSpec(memory_space=pltpu.VMEM),) * len(shapes),
        compiler_params=pltpu.CompilerParams(vmem_limit_bytes=VMEM_LIMIT),
    )(red_in, red_kv, red_out, sred, *flat)
    return {name: outs[4 * p:4 * p + 4] for p, name in enumerate(PARAM_ORDER)}


def kernel(x, mem, norm_g, w_in, b_f, w_pool, pool_scale, fox_q_g, fox_k_g, mem_norm_g, w_mem_kv, mem_q_g, mem_k_g, w_out, loss_target, m_norm_g, m_w_in, m_b_f, m_w_pool, m_pool_scale, m_fox_q_g, m_fox_k_g, m_mem_norm_g, m_w_mem_kv, m_mem_q_g, m_mem_k_g, m_w_out, v_norm_g, v_w_in, v_b_f, v_w_pool, v_pool_scale, v_fox_q_g, v_fox_k_g, v_mem_norm_g, v_w_mem_kv, v_mem_q_g, v_mem_k_g, v_w_out):
    given = dict(norm_g=(norm_g, m_norm_g, v_norm_g), w_in=(w_in, m_w_in, v_w_in), b_f=(b_f, m_b_f, v_b_f),
                 w_pool=(w_pool, m_w_pool, v_w_pool), pool_scale=(pool_scale, m_pool_scale, v_pool_scale),
                 fox_q_g=(fox_q_g, m_fox_q_g, v_fox_q_g), fox_k_g=(fox_k_g, m_fox_k_g, v_fox_k_g),
                 mem_norm_g=(mem_norm_g, m_mem_norm_g, v_mem_norm_g), w_mem_kv=(w_mem_kv, m_w_mem_kv, v_w_mem_kv),
                 mem_q_g=(mem_q_g, m_mem_q_g, v_mem_q_g), mem_k_g=(mem_k_g, m_mem_k_g, v_mem_k_g), w_out=(w_out, m_w_out, v_w_out))
    params = {name: tuple(t[0] if t.ndim > 2 else t for t in wmv) for name, wmv in given.items()}
    params["w_in"] = tuple(t.T.reshape(IN_CHUNK * 8, 128) for t in params["w_in"])
    x2, mem2, target2 = x[0], mem[0], loss_target[0]

    seg = jnp.asarray(np.kron(np.eye(FOX_HEADS), np.full((HEAD_DIM, HEAD_DIM), 1.0 / HEAD_DIM)), BF16)
    w_my, w_kv16, w_out16, wp_bd, bf_pad, gq8, gk8, gqm4 = _gather_w_in(
        params["w_in"][0], params["w_mem_kv"][0], params["w_out"][0], params["w_pool"][0], b_f, fox_q_g, fox_k_g, mem_q_g)
    proj, h16 = _fwd_proj(x2, norm_g, w_my)
    q_aug, k_aug, v_h, kt_aug, vt_aug = _fox_prep(proj, bf_pad, gq8, gk8, seg)
    olse, lse_rows, w_kv_all, w_out_all = _fox_fwd(q_aug, k_aug, vt_aug, w_kv16, w_out16)
    km, vm = _mem_prep(mem2, mem_norm_g, w_kv_all, mem_k_g)
    d_out, d_mixed, dw_out, loss_blk = _mix_out_loss(proj, olse, km, vm, wp_bd, pool_scale, gqm4, seg, x2, target2, w_out_all)

    d_ua, d_gb, d_qm, d_o, delta_rows, dwp_bd, d_scale, dkm, dvm, d_gqm = _mix_bwd(proj, olse, d_mixed, km, vm, wp_bd, pool_scale,
                                                                                    gqm4, seg)
    dw_kv, d_mem_g, d_gkm = _mem_bwd(mem2, mem_norm_g, w_kv_all, mem_k_g, dkm, dvm)
    dq_aug, dk_aug, dv_h, land_kv, land_out = _fox_bwd(q_aug, k_aug, kt_aug, v_h, d_o, lse_rows, delta_rows, dw_kv, dw_out)
    d_q, d_k, d_v, d_f, d_gq, d_gk, d_bf = _fox_post(proj, bf_pad, gq8, gk8, seg, dq_aug, dk_aug, dv_h)
    pieces = (d_ua, d_q, d_k, d_v, d_gb, d_qm)
    dw_in = _grad_w_in(h16, pieces, d_f)
    grad_x, red_in, red_kv, red_out, sred = _grad_x_exchange(pieces, d_f, w_my, x2, norm_g, d_out, dw_in, land_kv, land_out, d_mem_g,
                                                             d_scale, d_bf, d_gq, d_gk, d_gqm, d_gkm, dwp_bd, loss_blk)
    new = _update(red_in, red_kv, red_out, sred, params)
    result = [sred[SB_LOSS, 0], grad_x[None]]
    for kind in range(4):
        for name in PARAM_ORDER:
            out = new[name][kind]
            if name == "w_in":
                out = out.reshape(IN_CHUNK, D_MODEL).T
            result.append(out[None] if given[name][0].ndim > 2 else out)
    return tuple(result)
```

```python
import functools

import jax
import jax.numpy as jnp
import numpy as np
from jax import lax
from jax.experimental import pallas as pl
from jax.experimental.pallas import tpu as pltpu

F32 = jnp.float32
BF16 = jnp.bfloat16
MESH = pl.DeviceIdType.MESH

N_DEV = 8
D_MODEL = 1024
HEAD_DIM = 64
FOX_HEADS = 8
MEM_HEADS = 4
N_MEM = 256
POOL_WIDTH = 256
EPS = 1e-6
IN_WIDTH = 3080
IN_CHUNK = IN_WIDTH // N_DEV
CHUNK_ROWS = 400
F_OFF = 2048
MY_WIDTH = 3200
MY_F_OFF = 3072
PIECE = 512
TM = 256
TMM = 512
TA = 256
HB = 8
HB_FWD = 8
AUG_ROWS = 72
HALO = 16
VMEM_LIMIT = 56 * 1024 * 1024

ADAM_LR = 0.001
ADAM_B1 = 0.9
ADAM_B2 = 0.999
ADAM_EPS = 1e-08
ADAM_WD = 0.01
ADAM_STEP = 10


def _dot(a, b):
    return jnp.dot(a, b, preferred_element_type=F32)


def _dot_nt(a, b):
    return lax.dot_general(a, b, (((1,), (1,)), ((), ())), preferred_element_type=F32)


def _dot_tn(a, b):
    return lax.dot_general(a, b, (((0,), (0,)), ((), ())), preferred_element_type=F32)


def _sigmoid(g):
    return 0.5 + 0.5 * jnp.tanh(0.5 * g)


def _split3(v):
    hi = v.astype(BF16)
    r1 = v - hi.astype(F32)
    mid = r1.astype(BF16)
    lo = (r1 - mid.astype(F32)).astype(BF16)
    return hi, mid, lo


def _rms(v):
    return lax.rsqrt(jnp.mean(v * v, axis=-1, keepdims=True) + EPS)


def _rms_bwd(a, r, g, dy):
    da = dy * g
    return r * (da - a * jnp.mean(da * a, axis=-1, keepdims=True)), dy * a


def _head_mean(z, seg):
    hi = z.astype(BF16)
    lo = (z - hi.astype(F32)).astype(BF16)
    if z.shape[1] == 256:
        return _dot(hi, seg) + _dot(lo, seg)
    half = seg[0:256, 0:256]
    return jnp.concatenate([_dot(hi[:, 0:256], half) + _dot(lo[:, 0:256], half),
                            _dot(hi[:, 256:512], half) + _dot(lo[:, 256:512], half)], axis=1)


def _head_rms(v, seg):
    return lax.rsqrt(_head_mean(v * v, seg) + EPS)


def _head_rms_bwd(a, r, g, dy, seg):
    da = dy * g
    return r * (da - a * _head_mean(da * a, seg)), dy * a


def _fold_heads(row, n_heads):
    out = row[:, 0:HEAD_DIM]
    for h in range(1, n_heads):
        out = out + row[:, HEAD_DIM * h:HEAD_DIM * (h + 1)]
    return out


def _params(sem, limit=VMEM_LIMIT):
    return pltpu.CompilerParams(dimension_semantics=sem, vmem_limit_bytes=limit)


def _full(shape):
    return pl.BlockSpec(shape, lambda *_: (0,) * len(shape))


def _chunk_segments(d):
    runs = []
    for lo, hi, shift in ((0, F_OFF, 0), (F_OFF, F_OFF + FOX_HEADS, MY_F_OFF - F_OFF), (F_OFF + FOX_HEADS, IN_WIDTH, -FOX_HEADS)):
        a, b = max(lo, IN_CHUNK * d), min(hi, IN_CHUNK * (d + 1))
        if a < b:
            runs.append((a + shift, b - a))
    return runs


def _my_place():
    x, y, c = lax.axis_index("x"), lax.axis_index("y"), lax.axis_index("c")
    return x, y, c, 4 * x + 2 * y + c


def _shard_rows(dev):
    return pl.ds(pl.multiple_of(128 * dev, 128), 128)


def _gather_w_in(w_in, w_kv, w_out, w_pool, b_f, gq, gk, gqm):
    def body(win_ref, wkv_ref, wout_ref, wp_ref, bf_ref, gq_ref, gk_ref, gqm_ref, wmy_ref, kv16_ref, out16_ref, wpbd_ref, bfpad_ref,
             gq8_ref, gk8_ref, gqm4_ref, in_all, pay_in, win_rows, send_sems, recv_sems, load_sem):
        x, y, c, me = _my_place()
        here, sibling = (x, y, c), (x, y, 1 - c)
        x_chip, y_chip, far_chip = (1 - x, y), (x, 1 - y), (1 - x, 1 - y)
        relay_from = (x ^ (1 - c), y ^ c)
        relay_to = (x ^ c, y ^ (1 - c))

        load = pltpu.make_async_copy(win_ref, win_rows, load_sem)
        load.start()
        load.wait()
        pay_in[...] = jnp.zeros((CHUNK_ROWS, D_MODEL), BF16)
        for j in range(8):
            pay_in[0:IN_CHUNK, 128 * j:128 * (j + 1)] = win_rows[pl.ds(j, IN_CHUNK, stride=8), :].astype(BF16)

        def copy(k, block, to, own=False):
            px, py, pc = block
            dst = in_all.at[4 * px + 2 * py + pc]
            return pltpu.make_async_remote_copy(src_ref=pay_in if own else dst, dst_ref=dst, send_sem=send_sems.at[k],
                                                recv_sem=recv_sems.at[k], device_id=to, device_id_type=MESH)

        first = [copy(0, here, sibling, own=True), copy(1, here, (*x_chip, c), own=True), copy(2, here, (*y_chip, c), own=True)]
        for cp in first:
            cp.start()
        in_all[me] = pay_in[...]
        kv16_ref[...] = wkv_ref[...].astype(BF16)
        out16_ref[...] = wout_ref[...].astype(BF16)
        wpbd_ref[...] = jnp.zeros((POOL_WIDTH, POOL_WIDTH), BF16)
        for grp in range(4):
            sl = slice(64 * grp, 64 * (grp + 1))
            wpbd_ref[sl, sl] = wp_ref[grp].astype(BF16)
        bfpad_ref[...] = jnp.zeros((1, 128), F32)
        bfpad_ref[:, 0:FOX_HEADS] = bf_ref[...]
        gq8_ref[...] = jnp.concatenate([gq_ref[...]] * FOX_HEADS, axis=1)
        gk8_ref[...] = jnp.concatenate([gk_ref[...]] * FOX_HEADS, axis=1)
        gqm4_ref[...] = jnp.concatenate([gqm_ref[...]] * MEM_HEADS, axis=1)
        copy(1 + c, (*relay_from, c), here).wait_recv()
        passed = [copy(3, (*relay_from, c), (*relay_to, c)), copy(4, (*relay_from, c), sibling)]
        for cp in passed:
            cp.start()
        copy(2 - c, (*relay_to, c), here).wait_recv()
        passed.append(copy(5, (*relay_to, c), sibling))
        passed[-1].start()
        copy(3, (*far_chip, c), here).wait_recv()
        passed.append(copy(6, (*far_chip, c), sibling))
        passed[-1].start()
        copy(0, sibling, here).wait_recv()
        for k, chip in ((4, relay_to), (5, relay_from), (6, far_chip)):
            copy(k, (*chip, 1 - c), here).wait_recv()
        for cp in first + passed:
            cp.wait_send()

        row_pad = jnp.zeros((512 - CHUNK_ROWS, 256), F32)
        for r0 in range(0, D_MODEL, 256):
            ch = [jnp.concatenate([in_all[d, :, r0:r0 + 256].astype(F32), row_pad], axis=0).T[:, 0:IN_CHUNK] for d in range(N_DEV)]
            lo = F_OFF - 5 * IN_CHUNK
            full = jnp.concatenate(ch[:5] + [ch[5][:, :lo], ch[5][:, lo + FOX_HEADS:], ch[6], ch[7], ch[5][:, lo:lo + FOX_HEADS],
                                             jnp.zeros((256, MY_WIDTH - IN_WIDTH), F32)], axis=1)
            wmy_ref[r0:r0 + 256, :] = full.astype(BF16)

    return pl.pallas_call(
        body,
        name="gather_w_in",
        out_shape=(jax.ShapeDtypeStruct((D_MODEL, MY_WIDTH), BF16), jax.ShapeDtypeStruct((128, 512), BF16),
                   jax.ShapeDtypeStruct((128, D_MODEL), BF16), jax.ShapeDtypeStruct((POOL_WIDTH, POOL_WIDTH), BF16),
                   jax.ShapeDtypeStruct((1, 128), F32), jax.ShapeDtypeStruct((1, PIECE), F32), jax.ShapeDtypeStruct((1, PIECE), F32),
                   jax.ShapeDtypeStruct((1, 256), F32)),
        in_specs=[pl.BlockSpec(memory_space=pl.ANY)] + [pl.BlockSpec(memory_space=pltpu.VMEM)] * 7,
        out_specs=(pl.BlockSpec(memory_space=pltpu.VMEM),) * 8,
        scratch_shapes=[
            pltpu.VMEM((N_DEV, CHUNK_ROWS, D_MODEL), BF16),
            pltpu.VMEM((CHUNK_ROWS, D_MODEL), BF16),
            pltpu.VMEM((IN_CHUNK * 8, 128), F32),
            pltpu.SemaphoreType.DMA((7,)),
            pltpu.SemaphoreType.DMA((7,)),
            pltpu.SemaphoreType.DMA,
        ],
        compiler_params=pltpu.CompilerParams(vmem_limit_bytes=VMEM_LIMIT),
    )(w_in, w_kv, w_out, w_pool, b_f, gq, gk, gqm)


def _row_block_exchange(kv_ref, out_ref, kv_dst, out_dst, send_sems, recv_sems, local_sems, gather):
    x, y, c, me = _my_place()
    remote = []
    for k in range(1, N_DEV):
        px, py, pc = x ^ (k >> 2), y ^ ((k >> 1) & 1), c ^ (k & 1)
        peer = 4 * px + 2 * py + pc
        for fam, (src, dst) in enumerate(((kv_ref, kv_dst), (out_ref, out_dst))):
            remote.append(pltpu.make_async_remote_copy(
                src_ref=src if gather else src.at[_shard_rows(peer)], dst_ref=dst.at[_shard_rows(me)] if gather else dst.at[me],
                send_sem=send_sems.at[7 * fam + k - 1], recv_sem=recv_sems.at[7 * fam + k - 1],
                device_id=(px, py, pc), device_id_type=MESH))
    local = [pltpu.make_async_copy(src if gather else src.at[_shard_rows(me)], dst.at[_shard_rows(me)] if gather else dst.at[me],
                                   local_sems.at[fam]) for fam, (src, dst) in enumerate(((kv_ref, kv_dst), (out_ref, out_dst)))]
    return remote, local


SB_ROWS, SB_W = 80, 256
SB_NORM_G, SB_MEM_NORM_G, SB_POOL_SCALE, SB_B_F, SB_FOX_Q, SB_FOX_K, SB_MEM_Q, SB_MEM_K, SB_LOSS, SB_W_POOL = 0, 4, 8, 9, 10, 11, 12, 13, 14, 16


def _fwd_proj(x, norm_g, w_my):
    s_len = x.shape[0]

    def body(x_ref, g_ref, w_ref, proj_ref, h_ref):
        xv = x_ref[...]
        h = (xv * _rms(xv) * g_ref[...]).astype(BF16)
        h_ref[...] = h
        proj_ref[...] = _dot(h, w_ref[...])

    return pl.pallas_call(
        body,
        name="fwd_proj",
        grid=(s_len // TMM,),
        in_specs=[pl.BlockSpec((TMM, D_MODEL), lambda i: (i, 0)), _full((1, D_MODEL)), _full((D_MODEL, MY_WIDTH))],
        out_specs=[pl.BlockSpec((TMM, MY_WIDTH), lambda i: (i, 0)), pl.BlockSpec((TMM, D_MODEL), lambda i: (i, 0))],
        out_shape=[jax.ShapeDtypeStruct((s_len, MY_WIDTH), F32), jax.ShapeDtypeStruct((s_len, D_MODEL), BF16)],
        compiler_params=_params(("parallel",)),
    )(x, norm_g, w_my)


def _log_sigmoid(z):
    return jnp.minimum(z, 0.0) - jnp.log(1.0 + jnp.exp(-jnp.abs(z)))


def _forget_lane_maps():
    to_q = np.zeros((128, FOX_HEADS * 128), np.float32)
    to_k = np.zeros((128, FOX_HEADS * 128), np.float32)
    for h in range(FOX_HEADS):
        for term in range(3):
            to_q[8 * term + h, 128 * h + 64 + term] = 1.0
            to_q[24, 128 * h + 67 + term] = 1.0
            to_k[24, 128 * h + 64 + term] = 1.0
            to_k[8 * term + h, 128 * h + 67 + term] = -1.0
    return jnp.asarray(to_q, BF16), jnp.asarray(to_k, BF16)


def _fox_prep(proj, bf_pad, gq, gk, seg):
    s_len = proj.shape[0]
    to_q, to_k = _forget_lane_maps()

    def body(q_ref, k_ref, v_ref, f_ref, bf_ref, gq_ref, gk_ref, seg_ref, eq_ref, ek_ref,
             qa_ref, ka_ref, vh_ref, kt_ref, vt_ref, carry_ref):
        @pl.when(pl.program_id(0) == 0)
        def _():
            carry_ref[...] = jnp.zeros((1, 128), F32)

        lane = lax.broadcasted_iota(jnp.int32, (TM, 128), 1)
        logf = jnp.where(lane < FOX_HEADS, _log_sigmoid(f_ref[...] + bf_ref[...]), 0.0)
        row = lax.broadcasted_iota(jnp.int32, (TM, TM), 0)
        col = lax.broadcasted_iota(jnp.int32, (TM, TM), 1)
        tri = jnp.where(col <= row, 1.0, 0.0).astype(BF16)
        hi, mid, lo = _split3(logf)
        cum = _dot(tri, hi) + _dot(tri, mid) + _dot(tri, lo) + carry_ref[...]
        carry_ref[...] = cum[TM - 1:TM, :]
        f_hi, f_mid, f_lo = [t.astype(F32) for t in _split3(cum)]
        packed = (f_hi + pltpu.roll(f_mid, 8, 1) + pltpu.roll(f_lo, 16, 1) + jnp.where(lane == 24, 1.0, 0.0)).astype(BF16)
        extra_q = _dot(packed, eq_ref[...])
        extra_k = _dot(packed, ek_ref[...])
        one_at_64 = jnp.where(lane == 64, 1.0, 0.0)
        zeros = jnp.zeros((TM, HEAD_DIM), F32)
        q_all = q_ref[...]
        k_all = k_ref[...]
        qn_all = q_all * _head_rms(q_all, seg_ref[...]) * gq_ref[...] * 0.125
        kn_all = k_all * _head_rms(k_all, seg_ref[...]) * gk_ref[...]
        for h in range(FOX_HEADS):
            sl = slice(HEAD_DIM * h, HEAD_DIM * (h + 1))
            tile = slice(128 * h, 128 * (h + 1))
            qa = jnp.where(lane < 64, jnp.concatenate([qn_all[:, sl], zeros], axis=1), extra_q[:, tile])
            ka = jnp.where(lane < 64, jnp.concatenate([kn_all[:, sl], zeros], axis=1), extra_k[:, tile])
            vh = v_ref[:, sl]
            v_aug = jnp.where(lane < 64, jnp.concatenate([vh, zeros], axis=1), one_at_64)
            qa_ref[h] = qa.astype(BF16)
            ka_ref[h] = ka.astype(BF16)
            vh_ref[h] = vh.astype(BF16)
            kt_ref[h, 0] = ka.T.astype(BF16)
            vt_ref[h, 0] = v_aug.T.astype(BF16)

    col_blk = lambda j: pl.BlockSpec((TM, PIECE), lambda i: (i, j))
    head_blk = lambda w: pl.BlockSpec((FOX_HEADS, TM, w), lambda i: (0, i, 0))
    tile_blk = pl.BlockSpec((FOX_HEADS, 1, 128, TM), lambda i: (0, i, 0, 0))
    tiled = jax.ShapeDtypeStruct((FOX_HEADS, s_len // TM, 128, TM), BF16)
    return pl.pallas_call(
        body,
        name="fox_prep",
        grid=(s_len // TM,),
        in_specs=[col_blk(1), col_blk(2), col_blk(3), pl.BlockSpec((TM, 128), lambda i: (i, MY_F_OFF // 128)),
                  _full((1, 128)), _full((1, PIECE)), _full((1, PIECE)), _full((PIECE, PIECE)),
                  _full((128, FOX_HEADS * 128)), _full((128, FOX_HEADS * 128))],
        out_specs=[head_blk(128), head_blk(128), head_blk(HEAD_DIM), tile_blk, tile_blk],
        out_shape=[jax.ShapeDtypeStruct((FOX_HEADS, s_len, 128), BF16), jax.ShapeDtypeStruct((FOX_HEADS, s_len, 128), BF16),
                   jax.ShapeDtypeStruct((FOX_HEADS, s_len, HEAD_DIM), BF16), tiled, tiled],
        scratch_shapes=[pltpu.VMEM((1, 128), F32)],
        compiler_params=_params(("arbitrary",)),
    )(proj, proj, proj, proj, bf_pad, gq, gk, seg, to_q, to_k)


def _mem_prep(mem, g, w_kv, gk):
    def body(mem_ref, g_ref, w_ref, gk_ref, km_ref, vm_ref):
        m = mem_ref[...]
        kv = _dot((m * _rms(m) * g_ref[...]).astype(BF16), w_ref[...])
        for h in range(MEM_HEADS):
            sl = slice(HEAD_DIM * h, HEAD_DIM * (h + 1))
            kh = kv[:, sl]
            km_ref[:, sl] = (kh * _rms(kh) * gk_ref[...]).astype(BF16)
        vm_ref[...] = kv[:, 256:512].astype(BF16)

    return pl.pallas_call(
        body,
        name="mem_prep",
        out_shape=(jax.ShapeDtypeStruct((N_MEM, 256), BF16),) * 2,
        in_specs=[pl.BlockSpec(memory_space=pltpu.VMEM)] * 4,
        out_specs=(pl.BlockSpec(memory_space=pltpu.VMEM),) * 2,
        compiler_params=pltpu.CompilerParams(vmem_limit_bytes=VMEM_LIMIT),
    )(mem, g, w_kv, gk)


def _causal_mask():
    row = lax.broadcasted_iota(jnp.int32, (TA, TA), 0)
    col = lax.broadcasted_iota(jnp.int32, (TA, TA), 1)
    return col <= row


def _causal_mask_t():
    row = lax.broadcasted_iota(jnp.int32, (TA, TA), 0)
    col = lax.broadcasted_iota(jnp.int32, (TA, TA), 1)
    return row <= col


def _fox_fwd(q_aug, k_aug, vt_aug, w_kv16, w_out16):
    s_len = q_aug.shape[1]
    n_tiles = s_len // TA
    hb = HB_FWD
    n_groups = FOX_HEADS // hb

    def body(q_ref, k_new, vt_new, kv16_ref, out16_ref, o_ref, st_ref, kv_all, out_all, k_ref, vt_ref, send_sems, recv_sems, local_sems):
        qi = pl.program_id(1)
        for h in range(hb):
            k_ref[h, pl.ds(pl.multiple_of(qi * TA, TA), TA), :] = k_new[h]
            vt_ref[h, qi] = vt_new[h, 0]
        remote, local = _row_block_exchange(kv16_ref, out16_ref, kv_all, out_all, send_sems, recv_sems, local_sems, gather=True)

        @pl.when((pl.program_id(0) == 0) & (qi == 0))
        def _():
            for cp in remote + local:
                cp.start()

        qs = [q_ref[h] for h in range(hb)]

        def step(t0, carry, masked, width):
            rows = pl.ds(pl.multiple_of(t0 * TA, TA), width * TA)
            scores = [_dot_nt(k_ref[h, rows, :], qs[h]) for h in range(hb)]
            soft = []
            for h in range(hb):
                m, _ = carry[h]
                s = jnp.where(_causal_mask_t(), scores[h], -1e30) if masked else scores[h]
                m_new = jnp.maximum(m, jnp.max(s, axis=0, keepdims=True))
                soft.append((m_new, jnp.exp(m - m_new), jnp.exp(s - m_new).astype(BF16)))
            out = []
            for h, (m_new, alpha, p) in enumerate(soft):
                acc = alpha * carry[h][1]
                for t in range(width):
                    acc = acc + _dot(vt_ref[h, t0 + t, 0:AUG_ROWS, :], p[t * TA:(t + 1) * TA])
                out.append((m_new, acc))
            return tuple(out)

        init = tuple((jnp.full((1, TA), -1e30, F32), jnp.zeros((AUG_ROWS, TA), F32)) for _ in range(hb))
        carry = lax.fori_loop(0, qi // 2, lambda j, cr: step(2 * j, cr, False, 2), init)
        carry = lax.fori_loop(2 * (qi // 2), qi, lambda j, cr: step(j, cr, False, 1), carry)
        carry = step(qi, carry, True, 1)
        for h in range(hb):
            m, acc = carry[h]
            l = acc[HEAD_DIM:HEAD_DIM + 1, :]
            lse = m + jnp.log(l)
            o_ref[h] = jnp.concatenate([acc[0:HEAD_DIM] / l, jnp.broadcast_to(lse, (HEAD_DIM, TA))], axis=0).T
            st_ref[h, 0] = jnp.broadcast_to(lse, (8, TA))

        @pl.when((pl.program_id(0) == n_groups - 1) & (qi == n_tiles - 1))
        def _():
            for cp in remote:
                cp.wait_recv()
            for cp in remote:
                cp.wait_send()
            for cp in local:
                cp.wait()

    hbm = pl.BlockSpec(memory_space=pl.ANY)
    return pl.pallas_call(
        body,
        name="fox_fwd",
        grid=(n_groups, n_tiles),
        in_specs=[pl.BlockSpec((hb, TA, 128), lambda g, i: (g, i, 0)), pl.BlockSpec((hb, TA, 128), lambda g, i: (g, i, 0)),
                  pl.BlockSpec((hb, 1, 128, TA), lambda g, i: (g, i, 0, 0)), hbm, hbm],
        out_specs=[pl.BlockSpec((hb, TA, 128), lambda g, i: (g, i, 0)), pl.BlockSpec((hb, 1, 8, TA), lambda g, i: (g, i, 0, 0)),
                   hbm, hbm],
        out_shape=[jax.ShapeDtypeStruct((FOX_HEADS, s_len, 128), F32), jax.ShapeDtypeStruct((FOX_HEADS, n_tiles, 8, TA), F32),
                   jax.ShapeDtypeStruct((D_MODEL, 512), BF16), jax.ShapeDtypeStruct((D_MODEL, D_MODEL), BF16)],
        scratch_shapes=[pltpu.VMEM((hb, s_len, 128), BF16), pltpu.VMEM((hb, n_tiles, 128, TA), BF16),
                        pltpu.SemaphoreType.DMA((14,)), pltpu.SemaphoreType.DMA((14,)), pltpu.SemaphoreType.DMA((2,))],
        compiler_params=_params(("arbitrary", "arbitrary")),
    )(q_aug, k_aug, vt_aug, w_kv16, w_out16)


def _lane_group(lane, a, b, c, d):
    return jnp.where(lane < 64, a, jnp.where(lane < 128, b, jnp.where(lane < 192, c, d)))


def _pool_count(row0):
    lane = lax.broadcasted_iota(jnp.int32, (TM, POOL_WIDTH), 1)
    t1 = row0 + lax.broadcasted_iota(jnp.int32, (TM, POOL_WIDTH), 0) + 1
    return 1.0 / jnp.minimum(t1, _lane_group(lane, 2, 4, 8, 16)).astype(F32), lane


def _pool_delta(u, halo, cnt, lane):
    xe = jnp.concatenate([halo, u], axis=0)
    s2 = xe + pltpu.roll(xe, 1, 0)
    s4 = s2 + pltpu.roll(s2, 2, 0)
    s8 = s4 + pltpu.roll(s4, 4, 0)
    s16 = s8 + pltpu.roll(s8, 8, 0)
    win = _lane_group(lane, s2[HALO:], s4[HALO:], s8[HALO:], s16[HALO:])
    return win * cnt - u


def _pool_delta_bwd(dd, dd_next, cnt, cnt_next, lane):
    ee = jnp.concatenate([dd * cnt, dd_next * cnt_next], axis=0)
    n = TM + HALO
    r2 = ee + pltpu.roll(ee, n - 1, 0)
    r4 = r2 + pltpu.roll(r2, n - 2, 0)
    r8 = r4 + pltpu.roll(r4, n - 4, 0)
    r16 = r8 + pltpu.roll(r8, n - 8, 0)
    return _lane_group(lane, r2[:TM], r4[:TM], r8[:TM], r16[:TM]) - dd


def _mem_attn(qm, gq, seg, km_ref, vm_ref):
    r = _head_rms(qm, seg)
    a = qm * r
    q16 = (a * gq * 0.125).astype(BF16)
    heads = []
    for h in range(MEM_HEADS):
        sl = slice(HEAD_DIM * h, HEAD_DIM * (h + 1))
        s = _dot_nt(q16[:, sl], km_ref[:, sl])
        e = jnp.exp(s - jnp.max(s, axis=-1, keepdims=True))
        p = e * (1.0 / jnp.sum(e, axis=-1, keepdims=True))
        heads.append((p, _dot(p.astype(BF16), vm_ref[:, sl])))
    return a, r, q16, heads


def _mix_out_loss(proj, olse, km, vm, wp_bd, pool_scale, gq_m, seg, x, target, w_out):
    s_len = x.shape[0]
    n_blocks = s_len // TMM
    halves = TMM // TM

    def body(ua_ref, halo_ref, gb_ref, qm_ref, ol_ref, km_ref, vm_ref, wp_ref, sc_ref, gq_ref, seg_ref, x_ref, t_ref, w_ref,
             dout_ref, dmix_ref, dw16_ref, loss_ref, mixed_scr, dw_ref):
        s = pl.program_id(0)

        @pl.when(s == 0)
        def _():
            dw_ref[...] = jnp.zeros((D_MODEL, D_MODEL), F32)
            loss_ref[...] = jnp.zeros((8, 128), F32)
            mixed_scr[...] = jnp.zeros((2, TMM, D_MODEL), BF16)

        mixed16 = mixed_scr[(s + 1) % 2]
        err = x_ref[...] + _dot(mixed16, w_ref[...]) - t_ref[...]
        loss_ref[...] += jnp.where(s >= 1, 0.5 * jnp.sum(jnp.mean(err * err, axis=-1, keepdims=True)), 0.0)
        d_out = err / float(D_MODEL)
        dout_ref[...] = d_out
        d16 = d_out.astype(BF16)
        dmix_ref[...] = _dot_nt(d16, w_ref[...])
        dw_ref[...] += _dot_tn(mixed16, d16)

        slot = s % 2
        for half in range(halves):
            r0 = TM * half
            rows = slice(r0, r0 + TM)
            u = ua_ref[rows, 0:256]
            g_a = ua_ref[rows, 256:512]
            halo = jnp.where(s > 0, halo_ref[...], 0.0) if half == 0 else ua_ref[r0 - HALO:r0, 0:256]
            cnt, lane = _pool_count(s * TMM + r0)
            d = _pool_delta(u, halo, cnt, lane).astype(BF16)
            y_a = _dot(d, wp_ref[...]) * sc_ref[...]
            mixed_scr[slot, rows, 0:256] = (y_a * (g_a * _sigmoid(g_a))).astype(BF16)
            g_b = gb_ref[rows, :]
            y_b = jnp.concatenate([ol_ref[h, rows, 0:HEAD_DIM] for h in range(FOX_HEADS)], axis=1)
            mixed_scr[slot, rows, 256:768] = (y_b * (g_b * _sigmoid(g_b))).astype(BF16)
            _, _, _, heads = _mem_attn(qm_ref[rows, 0:256], gq_ref[...], seg_ref[0:256, 0:256], km_ref, vm_ref)
            g_m = qm_ref[rows, 256:512]
            y_m = jnp.concatenate([y for _, y in heads], axis=1)
            mixed_scr[slot, rows, 768:1024] = (y_m * (g_m * _sigmoid(g_m))).astype(BF16)

        @pl.when(s == n_blocks)
        def _():
            dw16_ref[...] = dw_ref[...].astype(BF16)

    build = lambda s: jnp.minimum(s, n_blocks - 1)
    use = lambda s: jnp.maximum(s - 1, 0)
    piece = lambda j: pl.BlockSpec((TMM, PIECE), lambda s: (build(s), j))
    halo_blk = pl.BlockSpec((HALO, 256), lambda s: (jnp.maximum(build(s) * (TMM // HALO) - 1, 0), 0))
    row = pl.BlockSpec((TMM, D_MODEL), lambda s: (use(s), 0))
    return pl.pallas_call(
        body,
        name="mix_out_loss",
        grid=(n_blocks + 1,),
        in_specs=[piece(0), halo_blk, piece(4), piece(5), pl.BlockSpec((FOX_HEADS, TMM, 128), lambda s: (0, build(s), 0)),
                  _full((N_MEM, 256)), _full((N_MEM, 256)), _full((256, 256)), _full((1, 256)), _full((1, 256)),
                  _full((PIECE, PIECE)), row, row, _full((D_MODEL, D_MODEL))],
        out_specs=[row, row, _full((D_MODEL, D_MODEL)), _full((8, 128))],
        out_shape=[jax.ShapeDtypeStruct((s_len, D_MODEL), F32), jax.ShapeDtypeStruct((s_len, D_MODEL), F32),
                   jax.ShapeDtypeStruct((D_MODEL, D_MODEL), BF16), jax.ShapeDtypeStruct((8, 128), F32)],
        scratch_shapes=[pltpu.VMEM((2, TMM, D_MODEL), BF16), pltpu.VMEM((D_MODEL, D_MODEL), F32)],
        compiler_params=_params(("arbitrary",)),
    )(proj, proj, proj, proj, olse, km, vm, wp_bd, pool_scale, gq_m, seg, x, target, w_out)


def _silu_pair(g):
    s = _sigmoid(g)
    return g * s, s * (1.0 + g * (1.0 - s))


def _mix_bwd(proj, olse, d_mixed, km, vm, wp_bd, pool_scale, gq_m, seg, h16):
    s_len = proj.shape[0]
    n_tiles = s_len // TM

    def body(ua_ref, halo_ref, ga_next_ref, gb_ref, qm_ref, ol_ref, dm_ref, dm_next_ref, km_ref, vm_ref, wp_ref, sc_ref, gq_ref,
             seg_ref, h_ref, dua_ref, dgb_ref, dqm_ref, do_ref, dl_ref, dwp_ref, dsc_ref, dkm_ref, dvm_ref, dgq_ref, dwe_ref,
             kept):
        step = pl.program_id(0)
        i = jnp.minimum(step, n_tiles - 1)
        live = lambda v: jnp.where(step < n_tiles, v, 0.0)

        @pl.when(step == 0)
        def _():
            dwp_ref[...] = jnp.zeros((256, 256), F32)
            dsc_ref[...] = jnp.zeros((1, 256), F32)
            dkm_ref[...] = jnp.zeros((N_MEM, 256), F32)
            dvm_ref[...] = jnp.zeros((N_MEM, 256), F32)
            dgq_ref[...] = jnp.zeros((1, HEAD_DIM), F32)
            dwe_ref[...] = jnp.zeros((D_MODEL, 3 * PIECE), F32)
            kept[...] = jnp.zeros((2, 3, TM, PIECE), BF16)

        h_prev = h_ref[...]
        for k in range(3):
            dwe_ref[:, PIECE * k:PIECE * (k + 1)] += _dot_tn(h_prev, kept[(step + 1) % 2, k])

        u = ua_ref[:, 0:256]
        g_a = ua_ref[:, 256:512]
        halo = jnp.where(i > 0, halo_ref[...], 0.0)
        cnt, lane = _pool_count(i * TM)
        d16 = _pool_delta(u, halo, cnt, lane).astype(BF16)
        t = _dot(d16, wp_ref[...])
        y_a = t * sc_ref[...]
        silu_a, dsilu_a = _silu_pair(g_a)
        dm_a = dm_ref[:, 0:256]
        dy_a = dm_a * silu_a
        dsc_ref[...] += live(jnp.sum(dy_a * t, axis=0, keepdims=True))
        dt16 = (dy_a * sc_ref[...]).astype(BF16)
        dwp_ref[...] += live(_dot_tn(d16, dt16))
        dd = _dot_nt(dt16, wp_ref[...])
        g_next = ga_next_ref[...]
        dt_next = (dm_next_ref[...] * (g_next * _sigmoid(g_next)) * sc_ref[...]).astype(BF16)
        dd_next = jnp.where(i < n_tiles - 1, _dot_nt(dt_next, wp_ref[...]), 0.0)
        cnt_next = _pool_count((i + 1) * TM)[0][0:HALO]
        slot = step % 2
        d_u = _pool_delta_bwd(dd, dd_next, cnt, cnt_next, lane).astype(BF16)
        d_ga = (dm_a * y_a * dsilu_a).astype(BF16)
        dua_ref[:, 0:256] = d_u
        dua_ref[:, 256:512] = d_ga
        kept[slot, 0, :, 0:256] = d_u
        kept[slot, 0, :, 256:512] = d_ga

        silu_b, dsilu_b = _silu_pair(gb_ref[...])
        dm_b = dm_ref[:, 256:768]
        o_all = jnp.concatenate([ol_ref[h][:, 0:HEAD_DIM] for h in range(FOX_HEADS)], axis=1)
        d_o = dm_b * silu_b
        d_gb = (dm_b * o_all * dsilu_b).astype(BF16)
        dgb_ref[...] = d_gb
        kept[slot, 1] = d_gb
        for h in range(FOX_HEADS):
            do_ref[h] = d_o[:, HEAD_DIM * h:HEAD_DIM * (h + 1)].astype(BF16)
        prod = d_o * o_all
        hi = prod.astype(BF16)
        lo = (prod - hi.astype(F32)).astype(BF16)
        head_of_lane = lax.broadcasted_iota(jnp.int32, (FOX_HEADS, PIECE), 1) // HEAD_DIM
        pick = jnp.where(head_of_lane == lax.broadcasted_iota(jnp.int32, (FOX_HEADS, PIECE), 0), 1.0, 0.0).astype(BF16)
        delta = _dot_nt(pick, hi) + _dot_nt(pick, lo)
        for h in range(FOX_HEADS):
            dl_ref[h, 0] = jnp.broadcast_to(delta[h:h + 1, :], (8, TM))

        seg = seg_ref[0:256, 0:256]
        a, r, q16, heads = _mem_attn(qm_ref[:, 0:256], gq_ref[...], seg, km_ref, vm_ref)
        silu_m, dsilu_m = _silu_pair(qm_ref[:, 256:512])
        dm_m = dm_ref[:, 768:1024]
        y_m = jnp.concatenate([y for _, y in heads], axis=1)
        d_gm = (dm_m * y_m * dsilu_m).astype(BF16)
        dqm_ref[:, 256:512] = d_gm
        kept[slot, 2, :, 256:512] = d_gm
        dy16_all = (dm_m * silu_m).astype(BF16)
        d_scores = []
        for h in range(MEM_HEADS):
            sl = slice(HEAD_DIM * h, HEAD_DIM * (h + 1))
            p = heads[h][0]
            dy16 = dy16_all[:, sl]
            dp = _dot_nt(dy16, vm_ref[:, sl])
            dvm_ref[:, sl] += live(_dot_tn(p.astype(BF16), dy16))
            ds16 = (p * (dp - jnp.sum(dp * p, axis=-1, keepdims=True))).astype(BF16)
            dkm_ref[:, sl] += live(_dot_tn(ds16, q16[:, sl]))
            d_scores.append(_dot(ds16, km_ref[:, sl]))
        dq, dg_rows = _head_rms_bwd(a, r, gq_ref[...], jnp.concatenate(d_scores, axis=1) * 0.125, seg)
        d_qm = dq.astype(BF16)
        dqm_ref[:, 0:256] = d_qm
        kept[slot, 2, :, 0:256] = d_qm
        dgq_ref[...] += live(_fold_heads(jnp.sum(dg_rows, axis=0, keepdims=True), MEM_HEADS))

    n_halo = s_len // HALO
    tile = lambda s: jnp.minimum(s, n_tiles - 1)
    piece = lambda j: pl.BlockSpec((TM, PIECE), lambda s: (tile(s), j))
    before = pl.BlockSpec((HALO, 256), lambda s: (jnp.maximum(tile(s) * (TM // HALO) - 1, 0), 0))
    after = lambda j: pl.BlockSpec((HALO, 256), lambda s: (jnp.minimum((tile(s) + 1) * (TM // HALO), n_halo - 1), j))
    row = pl.BlockSpec((TM, D_MODEL), lambda s: (tile(s), 0))
    out_piece = pl.BlockSpec((TM, PIECE), lambda s: (tile(s), 0))
    h_prev = pl.BlockSpec((TM, D_MODEL), lambda s: (jnp.maximum(s - 1, 0), 0))
    return pl.pallas_call(
        body,
        name="mix_bwd",
        grid=(n_tiles + 1,),
        in_specs=[piece(0), before, after(1), piece(4), piece(5), pl.BlockSpec((FOX_HEADS, TM, 128), lambda s: (0, tile(s), 0)),
                  row, after(0), _full((N_MEM, 256)), _full((N_MEM, 256)), _full((256, 256)), _full((1, 256)), _full((1, 256)),
                  _full((PIECE, PIECE)), h_prev],
        out_specs=[out_piece, out_piece, out_piece, pl.BlockSpec((FOX_HEADS, TM, HEAD_DIM), lambda s: (0, tile(s), 0)),
                   pl.BlockSpec((FOX_HEADS, 1, 8, TM), lambda s: (0, tile(s), 0, 0)),
                   _full((256, 256)), _full((1, 256)), _full((N_MEM, 256)), _full((N_MEM, 256)), _full((1, HEAD_DIM)),
                   pl.BlockSpec(memory_space=pltpu.VMEM)],
        out_shape=[jax.ShapeDtypeStruct((s_len, PIECE), BF16)] * 3 + [
            jax.ShapeDtypeStruct((FOX_HEADS, s_len, HEAD_DIM), BF16),
            jax.ShapeDtypeStruct((FOX_HEADS, n_tiles, 8, TM), F32),
            jax.ShapeDtypeStruct((256, 256), F32), jax.ShapeDtypeStruct((1, 256), F32),
            jax.ShapeDtypeStruct((N_MEM, 256), F32), jax.ShapeDtypeStruct((N_MEM, 256), F32),
            jax.ShapeDtypeStruct((1, HEAD_DIM), F32), jax.ShapeDtypeStruct((D_MODEL, 3 * PIECE), F32)],
        scratch_shapes=[pltpu.VMEM((2, 3, TM, PIECE), BF16)],
        compiler_params=_params(("arbitrary",)),
    )(proj, proj, proj, proj, proj, olse, d_mixed, d_mixed, km, vm, wp_bd, pool_scale, gq_m, seg, h16)


def _fox_bwd(q_aug, k_aug, kt_aug, v_h, d_o, lse_rows, delta_rows, dw_kv16, dw_out16):
    s_len = q_aug.shape[1]
    n_tiles = s_len // TA
    n_groups = FOX_HEADS // HB

    def body(q_hbm, k_ref, kt_ref, v_ref, do_hbm, st_ref, dl_ref, dkv16_ref, dout16_ref, dq_ref, dk_ref, dv_ref, land_kv, land_out,
             dqt_ref, q_ref, do_ref, send_sems, recv_sems, local_sems, tile_sems):
        j = pl.program_id(1)
        remote, local = _row_block_exchange(dkv16_ref, dout16_ref, land_kv, land_out, send_sems, recv_sems, local_sems, gather=False)

        def tile_loads(i):
            rows = pl.ds(pl.multiple_of(i * TA, TA), TA)
            return [pltpu.make_async_copy(q_hbm.at[:, rows, :], q_ref.at[:, rows, :], tile_sems.at[2 * i]),
                    pltpu.make_async_copy(do_hbm.at[:, rows, :], do_ref.at[:, rows, :], tile_sems.at[2 * i + 1])]

        @pl.when((pl.program_id(0) == 0) & (j == 0))
        def _():
            for i in range(n_tiles):
                for cp in tile_loads(i):
                    cp.start()
            for cp in remote + local:
                cp.start()

        @pl.when(j == 0)
        def _():
            dqt_ref[...] = jnp.zeros((HB, n_tiles, AUG_ROWS, TA), F32)

        ks = [k_ref[h] for h in range(HB)]
        kts = [kt_ref[h, 0, 0:AUG_ROWS, :] for h in range(HB)]
        vs = [v_ref[h] for h in range(HB)]

        def pair(i0, acc, masked, width):
            @pl.when(j == 0)
            def _():
                for t in range(width):
                    for cp in tile_loads(i0 + t):
                        cp.wait()

            rows = pl.ds(pl.multiple_of(i0 * TA, TA), width * TA)
            qs = [q_ref[h, rows, :] for h in range(HB)]
            d_os = [do_ref[h, rows, :] for h in range(HB)]
            row_stat = lambda ref, h: jnp.concatenate([ref[h, i0 + t, 0:1, :] for t in range(width)], axis=1)
            scores = [(_dot_nt(ks[h], qs[h]), _dot_nt(vs[h], d_os[h])) for h in range(HB)]
            probs = []
            for h in range(HB):
                s, dp = scores[h]
                if masked:
                    s = jnp.where(_causal_mask_t(), s, -1e30)
                p = jnp.exp(s - row_stat(st_ref, h))
                probs.append((p.astype(BF16), (p * (dp - row_stat(dl_ref, h))).astype(BF16)))
            out = []
            for h in range(HB):
                p16, ds16 = probs[h]
                dqt = _dot(kts[h], ds16)
                for t in range(width):
                    dqt_ref[h, i0 + t] += dqt[:, t * TA:(t + 1) * TA]
                out.append((acc[h][0] + _dot(ds16, qs[h]), acc[h][1] + _dot(p16, d_os[h])))
            return tuple(out)

        zero = tuple((jnp.zeros((TA, 128), F32), jnp.zeros((TA, HEAD_DIM), F32)) for _ in range(HB))
        acc = pair(j, zero, True, 1)
        odd = (n_tiles - 1 - j) % 2
        acc = lax.fori_loop(j + 1, j + 1 + odd, lambda i, a: pair(i, a, False, 1), acc)
        acc = lax.fori_loop(0, (n_tiles - 1 - j) // 2, lambda t, a: pair(j + 1 + odd + 2 * t, a, False, 2), acc)
        pad = jnp.zeros((128 - AUG_ROWS, TA), F32)
        for h in range(HB):
            dk_ref[h] = acc[h][0]
            dv_ref[h] = acc[h][1].astype(BF16)
            dq_ref[h] = jnp.concatenate([dqt_ref[h, j], pad], axis=0).T

        @pl.when((pl.program_id(0) == n_groups - 1) & (j == n_tiles - 1))
        def _():
            for cp in remote:
                cp.wait_recv()
            for cp in remote:
                cp.wait_send()
            for cp in local:
                cp.wait()

    assert n_groups == 1
    resident = pl.BlockSpec(memory_space=pltpu.VMEM)
    rows_blk = lambda r: resident
    tile = lambda w: pl.BlockSpec((HB, TA, w), lambda g, j: (g, j, 0))
    hbm = pl.BlockSpec(memory_space=pl.ANY)
    return pl.pallas_call(
        body,
        name="fox_bwd",
        grid=(n_groups, n_tiles),
        in_specs=[hbm, tile(128), pl.BlockSpec((HB, 1, 128, TA), lambda g, j: (g, j, 0, 0)), tile(HEAD_DIM),
                  hbm, rows_blk(8), rows_blk(8), hbm, hbm],
        out_specs=[tile(128), tile(128), tile(HEAD_DIM), hbm, hbm],
        out_shape=[jax.ShapeDtypeStruct((FOX_HEADS, s_len, 128), F32), jax.ShapeDtypeStruct((FOX_HEADS, s_len, 128), F32),
                   jax.ShapeDtypeStruct((FOX_HEADS, s_len, HEAD_DIM), BF16),
                   jax.ShapeDtypeStruct((N_DEV, 128, 512), BF16), jax.ShapeDtypeStruct((N_DEV, 128, D_MODEL), BF16)],
        scratch_shapes=[pltpu.VMEM((HB, n_tiles, AUG_ROWS, TA), F32),
                        pltpu.VMEM((HB, s_len, 128), BF16), pltpu.VMEM((HB, s_len, HEAD_DIM), BF16),
                        pltpu.SemaphoreType.DMA((14,)), pltpu.SemaphoreType.DMA((14,)), pltpu.SemaphoreType.DMA((2,)),
                        pltpu.SemaphoreType.DMA((2 * n_tiles,))],
        compiler_params=_params(("arbitrary", "arbitrary")),
    )(q_aug, k_aug, kt_aug, v_h, d_o, lse_rows, delta_rows, dw_kv16, dw_out16)


def _fox_post(proj, bf_pad, gq, gk, seg, dq_aug, dk_aug, dv_h):
    s_len = proj.shape[0]
    n_tiles = s_len // TM

    def body(q_ref, k_ref, f_ref, bf_ref, gq_ref, gk_ref, seg_ref, dqa_ref, dka_ref, dvh_ref,
             dq_ref, dk_ref, dv_ref, df_ref, dgq_ref, dgk_ref, dbf_ref, carry_ref):
        @pl.when(pl.program_id(0) == 0)
        def _():
            carry_ref[...] = jnp.zeros((1, 128), F32)
            dgq_ref[...] = jnp.zeros((1, HEAD_DIM), F32)
            dgk_ref[...] = jnp.zeros((1, HEAD_DIM), F32)
            dbf_ref[...] = jnp.zeros((1, 128), F32)

        lane = lax.broadcasted_iota(jnp.int32, (TM, 128), 1)
        seg = seg_ref[...]
        dqas = [dqa_ref[h] for h in range(FOX_HEADS)]
        dkas = [dka_ref[h] for h in range(FOX_HEADS)]
        d_cum = jnp.zeros((TM, 128), F32)
        for h in range(FOX_HEADS):
            d_cum = jnp.where(lane == h, dqas[h][:, 64:65] - dkas[h][:, 67:68], d_cum)
        q_all = q_ref[...]
        k_all = k_ref[...]
        rq = _head_rms(q_all, seg)
        rk = _head_rms(k_all, seg)
        dy_q = jnp.concatenate([t[:, 0:HEAD_DIM] for t in dqas], axis=1) * 0.125
        dy_k = jnp.concatenate([t[:, 0:HEAD_DIM] for t in dkas], axis=1)
        dq, dgq_rows = _head_rms_bwd(q_all * rq, rq, gq_ref[...], dy_q, seg)
        dk, dgk_rows = _head_rms_bwd(k_all * rk, rk, gk_ref[...], dy_k, seg)
        dq_ref[...] = dq.astype(BF16)
        dk_ref[...] = dk.astype(BF16)
        dv_ref[...] = jnp.concatenate([dvh_ref[h].astype(F32) for h in range(FOX_HEADS)], axis=1).astype(BF16)
        dgq_ref[...] += _fold_heads(jnp.sum(dgq_rows, axis=0, keepdims=True), FOX_HEADS)
        dgk_ref[...] += _fold_heads(jnp.sum(dgk_rows, axis=0, keepdims=True), FOX_HEADS)

        row = lax.broadcasted_iota(jnp.int32, (TM, TM), 0)
        col = lax.broadcasted_iota(jnp.int32, (TM, TM), 1)
        tri = jnp.where(col >= row, 1.0, 0.0).astype(BF16)
        hi, mid, lo = _split3(d_cum)
        d_logf = _dot(tri, hi) + _dot(tri, mid) + _dot(tri, lo) + carry_ref[...]
        carry_ref[...] = d_logf[0:1, :]
        z = f_ref[...] + bf_ref[...]
        d_f = jnp.where(lane < FOX_HEADS, d_logf / (1.0 + jnp.exp(z)), 0.0)
        df_ref[...] = d_f.astype(BF16)
        dbf_ref[...] += jnp.sum(d_f, axis=0, keepdims=True)

    rev = lambda i: n_tiles - 1 - i
    col_blk = lambda j: pl.BlockSpec((TM, PIECE), lambda i: (rev(i), j))
    head_blk = lambda w: pl.BlockSpec((FOX_HEADS, TM, w), lambda i: (0, rev(i), 0))
    out_piece = pl.BlockSpec((TM, PIECE), lambda i: (rev(i), 0))
    return pl.pallas_call(
        body,
        name="fox_post",
        grid=(n_tiles,),
        in_specs=[col_blk(1), col_blk(2), pl.BlockSpec((TM, 128), lambda i: (rev(i), MY_F_OFF // 128)),
                  _full((1, 128)), _full((1, PIECE)), _full((1, PIECE)), _full((PIECE, PIECE)),
                  head_blk(128), head_blk(128), head_blk(HEAD_DIM)],
        out_specs=[out_piece, out_piece, out_piece, pl.BlockSpec((TM, 128), lambda i: (rev(i), 0)),
                   _full((1, HEAD_DIM)), _full((1, HEAD_DIM)), _full((1, 128))],
        out_shape=[jax.ShapeDtypeStruct((s_len, PIECE), BF16)] * 3 + [
            jax.ShapeDtypeStruct((s_len, 128), BF16), jax.ShapeDtypeStruct((1, HEAD_DIM), F32),
            jax.ShapeDtypeStruct((1, HEAD_DIM), F32), jax.ShapeDtypeStruct((1, 128), F32)],
        scratch_shapes=[pltpu.VMEM((1, 128), F32)],
        compiler_params=_params(("arbitrary",)),
    )(proj, proj, proj, bf_pad, gq, gk, seg, dq_aug, dk_aug, dv_h)


def _mem_bwd(mem, g, w_kv, gk, dkm, dvm):
    def body(mem_ref, g_ref, w_ref, gk_ref, dkm_ref, dvm_ref, dw_ref, dg_ref, dgk_ref, dkv_ref):
        m = mem_ref[...]
        r = _rms(m)
        a = m * r
        mn16 = (a * g_ref[...]).astype(BF16)
        kv = _dot(mn16, w_ref[...])
        dgk = jnp.zeros((N_MEM, HEAD_DIM), F32)
        for h in range(MEM_HEADS):
            sl = slice(HEAD_DIM * h, HEAD_DIM * (h + 1))
            kh = kv[:, sl]
            rk = _rms(kh)
            dk, dg_rows = _rms_bwd(kh * rk, rk, gk_ref[...], dkm_ref[:, sl])
            dkv_ref[:, sl] = dk.astype(BF16)
            dgk = dgk + dg_rows
        dkv_ref[:, 256:512] = dvm_ref[...].astype(BF16)
        dgk_ref[...] = jnp.sum(dgk, axis=0, keepdims=True)
        dkv16 = dkv_ref[...]
        dw_ref[...] = _dot_tn(mn16, dkv16).astype(BF16)
        dg_ref[...] = jnp.sum(_dot_nt(dkv16, w_ref[...]) * a, axis=0, keepdims=True)

    return pl.pallas_call(
        body,
        name="mem_bwd",
        out_shape=(jax.ShapeDtypeStruct((D_MODEL, 512), BF16), jax.ShapeDtypeStruct((1, D_MODEL), F32),
                   jax.ShapeDtypeStruct((1, HEAD_DIM), F32)),
        in_specs=[pl.BlockSpec(memory_space=pltpu.VMEM)] * 6,
        out_specs=(pl.BlockSpec(memory_space=pltpu.VMEM),) * 3,
        scratch_shapes=[pltpu.VMEM((N_MEM, 512), BF16)],
        compiler_params=pltpu.CompilerParams(vmem_limit_bytes=VMEM_LIMIT),
    )(mem, g, w_kv, gk, dkm, dvm)


def _grad_x_exchange(pieces, d_f, w_my, x, norm_g, d_out, dw_in, land_kv, land_out, d_mem_g, d_scale, d_bf, d_gq, d_gk, d_gqm,
                     d_gkm, dwp_bd, loss_blk):
    s_len = x.shape[0]
    n_steps = s_len // TM
    step2, step3 = n_steps // 4, (11 * n_steps) // 16
    half = D_MODEL // 2

    def body(p0, p1, p2, p3, p4, p5, df_ref, x_ref, dout_ref, w_ref, g_ref, in_ref, lkv_ref, lout_ref,
             dmg, dsc, dbf, dgq, dgk, dgqm, dgkm, dwp, loss_ref,
             gx_ref, rin_ref, rkv_ref, rout_ref, sred_ref,
             dng, land1, send2a, send2b, keep2a, keep2b, land2a, land2b, send3a, send3b, land3a, land3b, sb_ref, sland,
             own4, lkv_v, lout_v, send_sems, recv_sems, load_sems):
        i = pl.program_id(0)
        x_, y_, c_, me = _my_place()
        sibling, x_nbr, y_nbr = (x_, y_, 1 - c_), (1 - x_, y_, c_), (x_, 1 - y_, c_)
        loads = [pltpu.make_async_copy(in_ref.at[2 * k + c_], own4.at[k], load_sems.at[k]) for k in range(4)]
        loads += [pltpu.make_async_copy(lkv_ref, lkv_v, load_sems.at[4]), pltpu.make_async_copy(lout_ref, lout_v, load_sems.at[5])]

        def rcopy(src, dst, sem, to):
            return pltpu.make_async_remote_copy(src_ref=src, dst_ref=dst, send_sem=send_sems.at[sem], recv_sem=recv_sems.at[sem],
                                                device_id=to, device_id_type=MESH)

        early_rows, late_rows = pl.ds(8, SB_ROWS - 8), pl.ds(0, 8)
        small_early, small_late = [], []
        for k in range(1, N_DEV):
            peer = (x_ ^ (k >> 2), y_ ^ ((k >> 1) & 1), c_ ^ (k & 1))
            small_early.append(rcopy(sb_ref.at[early_rows], sland.at[me, early_rows], k - 1, peer))
            small_late.append(rcopy(sb_ref.at[late_rows], sland.at[me, late_rows], 16 + k, peer))
        small = small_early + small_late
        stage1 = [rcopy(in_ref.at[2 * k + 1 - c_], land1.at[k], 7 + k, sibling) for k in range(4)]
        stage2 = [rcopy(send2a.at[j], land2a.at[j], 11 + j, x_nbr) for j in range(2)]
        stage2 += [rcopy(send2b.at[j], land2b.at[j], 13 + j, y_nbr) for j in range(2)]
        stage3 = [rcopy(send3a, land3a, 15, y_nbr), rcopy(send3b, land3b, 16, x_nbr)]
        top, bot = slice(0, half), slice(half, D_MODEL)

        @pl.when(i == 0)
        def _():
            dng[...] = jnp.zeros((1, D_MODEL), F32)
            for cp in stage1 + loads:
                cp.start()
            sb_ref[...] = jnp.zeros((SB_ROWS, SB_W), F32)
            sb_ref[SB_POOL_SCALE:SB_POOL_SCALE + 1, :] = dsc[...]
            sb_ref[SB_B_F:SB_B_F + 1, 0:128] = dbf[...]
            for row, ref in ((SB_FOX_Q, dgq), (SB_FOX_K, dgk), (SB_MEM_Q, dgqm), (SB_MEM_K, dgkm)):
                sb_ref[row:row + 1, 0:HEAD_DIM] = ref[...]
            sb_ref[SB_LOSS:SB_LOSS + 1, 0:128] = loss_ref[0:1, :]
            for grp in range(4):
                sl = slice(64 * grp, 64 * (grp + 1))
                sb_ref[SB_W_POOL:SB_W_POOL + 64, sl] = dwp[sl, sl]
            for cp in small_early:
                cp.start()

        @pl.when(i == step2)
        def _():
            for cp in stage1:
                cp.wait_recv()
            for cp in loads[0:4]:
                cp.wait()

            def chip_sum(k, cols):
                return own4[k, :, cols].astype(F32) + land1[k, :, cols].astype(F32)

            for j in range(2):
                send2a[j] = chip_sum(2 * (1 - x_) + j, top).astype(BF16)
                keep2a[j] = chip_sum(2 * x_ + j, top)
                send2b[j] = chip_sum(2 * j + 1 - y_, bot).astype(BF16)
                keep2b[j] = chip_sum(2 * j + y_, bot)
            for cp in stage2:
                cp.start()

        @pl.when(i == step3)
        def _():
            for cp in stage2:
                cp.wait_recv()
            for j in range(2):
                keep2a[j] = keep2a[j] + land2a[j].astype(F32)
                keep2b[j] = keep2b[j] + land2b[j].astype(F32)
            send3a[...] = keep2a[1 - y_].astype(BF16)
            send3b[...] = keep2b[1 - x_].astype(BF16)
            for cp in stage3:
                cp.start()

        dh = _dot_nt(df_ref[...], w_ref[:, MY_F_OFF:MY_WIDTH])
        for j, p in enumerate((p0, p1, p2, p3, p4, p5)):
            dh = dh + _dot_nt(p[...], w_ref[:, PIECE * j:PIECE * (j + 1)])
        xv = x_ref[...]
        r = _rms(xv)
        dx, dg_rows = _rms_bwd(xv * r, r, g_ref[...], dh)
        gx_ref[...] = dout_ref[...] + dx
        dng[...] += jnp.sum(dg_rows, axis=0, keepdims=True)

        @pl.when(i == n_steps - 1)
        def _():
            for k in range(4):
                sb_ref[SB_NORM_G + k:SB_NORM_G + k + 1, :] = dng[:, SB_W * k:SB_W * (k + 1)]
                sb_ref[SB_MEM_NORM_G + k:SB_MEM_NORM_G + k + 1, :] = dmg[:, SB_W * k:SB_W * (k + 1)]
            for cp in small_late:
                cp.start()
            sland[me] = sb_ref[...]
            for cp in stage3:
                cp.wait_recv()
            rin_ref[:, top] = keep2a[y_] + land3a[...].astype(F32)
            rin_ref[:, bot] = keep2b[x_] + land3b[...].astype(F32)
            for cp in small:
                cp.wait_recv()
            for cp in small + stage1 + stage2 + stage3:
                cp.wait_send()
            for cp in loads[4:6]:
                cp.wait()
            for land, red in ((lkv_v, rkv_ref), (lout_v, rout_ref), (sland, sred_ref)):
                acc = land[0].astype(F32)
                for d in range(1, N_DEV):
                    acc = acc + land[d].astype(F32)
                red[...] = acc

    piece = pl.BlockSpec((TM, PIECE), lambda i: (i, 0))
    row = pl.BlockSpec((TM, D_MODEL), lambda i: (i, 0))
    vmem = pl.BlockSpec(memory_space=pltpu.VMEM)
    half_chunk = lambda n, dt: pltpu.VMEM((n, CHUNK_ROWS, half), dt)
    whole = (w_my, norm_g, dw_in, land_kv, land_out, d_mem_g, d_scale, d_bf, d_gq, d_gk, d_gqm, d_gkm, dwp_bd, loss_blk)
    return pl.pallas_call(
        body,
        name="grad_x_exchange",
        grid=(n_steps,),
        in_specs=[piece] * 6 + [pl.BlockSpec((TM, 128), lambda i: (i, 0)), row, row, vmem, vmem]
        + [pl.BlockSpec(memory_space=pl.ANY)] * 3 + [vmem] * (len(whole) - 5),
        out_specs=[row, vmem, vmem, vmem, vmem],
        out_shape=[jax.ShapeDtypeStruct((s_len, D_MODEL), F32), jax.ShapeDtypeStruct((CHUNK_ROWS, D_MODEL), F32),
                   jax.ShapeDtypeStruct((128, 512), F32), jax.ShapeDtypeStruct((128, D_MODEL), F32),
                   jax.ShapeDtypeStruct((SB_ROWS, SB_W), F32)],
        scratch_shapes=[
            pltpu.VMEM((1, D_MODEL), F32),
            pltpu.VMEM((4, CHUNK_ROWS, D_MODEL), BF16),
            half_chunk(2, BF16), half_chunk(2, BF16), half_chunk(2, F32), half_chunk(2, F32), half_chunk(2, BF16), half_chunk(2, BF16),
            pltpu.VMEM((CHUNK_ROWS, half), BF16), pltpu.VMEM((CHUNK_ROWS, half), BF16),
            pltpu.VMEM((CHUNK_ROWS, half), BF16), pltpu.VMEM((CHUNK_ROWS, half), BF16),
            pltpu.VMEM((SB_ROWS, SB_W), F32),
            pltpu.VMEM((N_DEV, SB_ROWS, SB_W), F32),
            pltpu.VMEM((4, CHUNK_ROWS, D_MODEL), BF16),
            pltpu.VMEM((N_DEV, 128, 512), BF16),
            pltpu.VMEM((N_DEV, 128, D_MODEL), BF16),
            pltpu.SemaphoreType.DMA((24,)),
            pltpu.SemaphoreType.DMA((24,)),
            pltpu.SemaphoreType.DMA((6,)),
        ],
        compiler_params=_params(("arbitrary",)),
    )(*pieces, d_f, x, d_out, *whole)


def _grad_w_in(h16, late_pieces, d_f, dw_early):
    s_len = h16.shape[0]
    tk = 512

    def body(h_ref, p1, p2, p3, df_ref, early_ref, out_ref, dw_ref):
        @pl.when(pl.program_id(0) == 0)
        def _():
            dw_ref[...] = jnp.zeros((D_MODEL, MY_WIDTH), F32)
            for k, j in enumerate((0, 4, 5)):
                dw_ref[:, PIECE * j:PIECE * (j + 1)] = early_ref[:, PIECE * k:PIECE * (k + 1)]

        h = h_ref[...]
        for j, p in ((1, p1), (2, p2), (3, p3)):
            dw_ref[:, PIECE * j:PIECE * (j + 1)] += _dot_tn(h, p[...])
        dw_ref[:, MY_F_OFF:MY_WIDTH] += _dot_tn(h, df_ref[...])

        @pl.when(pl.program_id(0) == pl.num_programs(0) - 1)
        def _():
            for d in range(N_DEV):
                parts = [dw_ref[:, start:start + width] for start, width in _chunk_segments(d)]
                parts.append(jnp.zeros((D_MODEL, 512 - IN_CHUNK), F32))
                out_ref[d] = jnp.concatenate(parts, axis=1).T[0:CHUNK_ROWS].astype(BF16)

    piece = pl.BlockSpec((tk, PIECE), lambda i: (i, 0))
    return pl.pallas_call(
        body,
        name="grad_w_in",
        grid=(s_len // tk,),
        in_specs=[pl.BlockSpec((tk, D_MODEL), lambda i: (i, 0))] + [piece] * 3 + [pl.BlockSpec((tk, 128), lambda i: (i, 0)),
                                                                                 pl.BlockSpec(memory_space=pltpu.VMEM)],
        out_specs=_full((N_DEV, CHUNK_ROWS, D_MODEL)),
        out_shape=jax.ShapeDtypeStruct((N_DEV, CHUNK_ROWS, D_MODEL), BF16),
        scratch_shapes=[pltpu.VMEM((D_MODEL, MY_WIDTH), F32)],
        compiler_params=_params(("arbitrary",)),
    )(h16, *late_pieces, d_f, dw_early)


def _adamw(w, g, m, v):
    m = ADAM_B1 * m + (1.0 - ADAM_B1) * g
    v = ADAM_B2 * v + (1.0 - ADAM_B2) * (g * g)
    m_hat = m / (1.0 - ADAM_B1 ** ADAM_STEP)
    v_hat = v / (1.0 - ADAM_B2 ** ADAM_STEP)
    return -ADAM_LR * (m_hat / (jnp.sqrt(v_hat) + ADAM_EPS) + ADAM_WD * w), m, v


PARAM_ORDER = ("norm_g", "w_in", "b_f", "w_pool", "pool_scale", "fox_q_g", "fox_k_g", "mem_norm_g", "w_mem_kv", "mem_q_g", "mem_k_g", "w_out")


def _update(red_in, red_kv, red_out, sred, params):
    def body(rin_ref, rkv_ref, rout_ref, sred_ref, *refs):
        ins, outs = refs[:3 * len(PARAM_ORDER)], refs[3 * len(PARAM_ORDER):]
        wide = lambda row: jnp.concatenate([sred_ref[row + k:row + k + 1, :] for k in range(4)], axis=1)
        head = lambda row: sred_ref[row:row + 1, 0:HEAD_DIM]
        grads = {
            "norm_g": lambda: wide(SB_NORM_G), "w_in": lambda: rin_ref[...], "b_f": lambda: sred_ref[SB_B_F:SB_B_F + 1, 0:FOX_HEADS],
            "pool_scale": lambda: sred_ref[SB_POOL_SCALE:SB_POOL_SCALE + 1, :], "fox_q_g": lambda: head(SB_FOX_Q),
            "fox_k_g": lambda: head(SB_FOX_K), "mem_norm_g": lambda: wide(SB_MEM_NORM_G), "w_mem_kv": lambda: rkv_ref[...],
            "mem_q_g": lambda: head(SB_MEM_Q), "mem_k_g": lambda: head(SB_MEM_K), "w_out": lambda: rout_ref[...],
        }
        for p, name in enumerate(PARAM_ORDER):
            w_ref, m_ref, v_ref = ins[3 * p:3 * p + 3]
            g_out, d_out, m_out, v_out = outs[4 * p:4 * p + 4]
            if name == "w_pool":
                for grp in range(4):
                    g = sred_ref[SB_W_POOL:SB_W_POOL + 64, 64 * grp:64 * (grp + 1)]
                    delta, m_new, v_new = _adamw(w_ref[grp], g, m_ref[grp], v_ref[grp])
                    g_out[grp], d_out[grp], m_out[grp], v_out[grp] = g, delta, m_new, v_new
            elif name == "w_in":
                for j in range(8):
                    rows = pl.ds(j, IN_CHUNK, stride=8)
                    g = rin_ref[0:IN_CHUNK, 128 * j:128 * (j + 1)]
                    delta, m_new, v_new = _adamw(w_ref[rows, :], g, m_ref[rows, :], v_ref[rows, :])
                    g_out[rows, :], d_out[rows, :], m_out[rows, :], v_out[rows, :] = g, delta, m_new, v_new
            else:
                g = grads[name]()
                delta, m_new, v_new = _adamw(w_ref[...], g, m_ref[...], v_ref[...])
                g_out[...], d_out[...], m_out[...], v_out[...] = g, delta, m_new, v_new

    flat = [t for name in PARAM_ORDER for t in params[name]]
    shapes = [params[name][0].shape for name in PARAM_ORDER for _ in range(4)]
    outs = pl.pallas_call(
        body,
        name="adamw_update",
        out_shape=tuple(jax.ShapeDtypeStruct(s, F32) for s in shapes),
        in_specs=[pl.BlockSpec(memory_space=pltpu.VMEM)] * (4 + len(flat)),
        out_specs=(pl.BlockSpec(memory_space=pltpu.VMEM),) * len(shapes),
        compiler_params=pltpu.CompilerParams(vmem_limit_bytes=VMEM_LIMIT),
    )(red_in, red_kv, red_out, sred, *flat)
    return {name: outs[4 * p:4 * p + 4] for p, name in enumerate(PARAM_ORDER)}


def kernel(x, mem, norm_g, w_in, b_f, w_pool, pool_scale, fox_q_g, fox_k_g, mem_norm_g, w_mem_kv, mem_q_g, mem_k_g, w_out, loss_target, m_norm_g, m_w_in, m_b_f, m_w_pool, m_pool_scale, m_fox_q_g, m_fox_k_g, m_mem_norm_g, m_w_mem_kv, m_mem_q_g, m_mem_k_g, m_w_out, v_norm_g, v_w_in, v_b_f, v_w_pool, v_pool_scale, v_fox_q_g, v_fox_k_g, v_mem_norm_g, v_w_mem_kv, v_mem_q_g, v_mem_k_g, v_w_out):
    given = dict(norm_g=(norm_g, m_norm_g, v_norm_g), w_in=(w_in, m_w_in, v_w_in), b_f=(b_f, m_b_f, v_b_f),
                 w_pool=(w_pool, m_w_pool, v_w_pool), pool_scale=(pool_scale, m_pool_scale, v_pool_scale),
                 fox_q_g=(fox_q_g, m_fox_q_g, v_fox_q_g), fox_k_g=(fox_k_g, m_fox_k_g, v_fox_k_g),
                 mem_norm_g=(mem_norm_g, m_mem_norm_g, v_mem_norm_g), w_mem_kv=(w_mem_kv, m_w_mem_kv, v_w_mem_kv),
                 mem_q_g=(mem_q_g, m_mem_q_g, v_mem_q_g), mem_k_g=(mem_k_g, m_mem_k_g, v_mem_k_g), w_out=(w_out, m_w_out, v_w_out))
    params = {name: tuple(t[0] if t.ndim > 2 else t for t in wmv) for name, wmv in given.items()}
    params["w_in"] = tuple(t.T.reshape(IN_CHUNK * 8, 128) for t in params["w_in"])
    x2, mem2, target2 = x[0], mem[0], loss_target[0]

    seg = jnp.asarray(np.kron(np.eye(FOX_HEADS), np.full((HEAD_DIM, HEAD_DIM), 1.0 / HEAD_DIM)), BF16)
    w_my, w_kv16, w_out16, wp_bd, bf_pad, gq8, gk8, gqm4 = _gather_w_in(
        params["w_in"][0], params["w_mem_kv"][0], params["w_out"][0], params["w_pool"][0], b_f, fox_q_g, fox_k_g, mem_q_g)
    proj, h16 = _fwd_proj(x2, norm_g, w_my)
    q_aug, k_aug, v_h, kt_aug, vt_aug = _fox_prep(proj, bf_pad, gq8, gk8, seg)
    olse, lse_rows, w_kv_all, w_out_all = _fox_fwd(q_aug, k_aug, vt_aug, w_kv16, w_out16)
    km, vm = _mem_prep(mem2, mem_norm_g, w_kv_all, mem_k_g)
    d_out, d_mixed, dw_out, loss_blk = _mix_out_loss(proj, olse, km, vm, wp_bd, pool_scale, gqm4, seg, x2, target2, w_out_all)

    d_ua, d_gb, d_qm, d_o, delta_rows, dwp_bd, d_scale, dkm, dvm, d_gqm, dw_early = _mix_bwd(
        proj, olse, d_mixed, km, vm, wp_bd, pool_scale, gqm4, seg, h16)
    dw_kv, d_mem_g, d_gkm = _mem_bwd(mem2, mem_norm_g, w_kv_all, mem_k_g, dkm, dvm)
    dq_aug, dk_aug, dv_h, land_kv, land_out = _fox_bwd(q_aug, k_aug, kt_aug, v_h, d_o, lse_rows, delta_rows, dw_kv, dw_out)
    d_q, d_k, d_v, d_f, d_gq, d_gk, d_bf = _fox_post(proj, bf_pad, gq8, gk8, seg, dq_aug, dk_aug, dv_h)
    pieces = (d_ua, d_q, d_k, d_v, d_gb, d_qm)
    dw_in = _grad_w_in(h16, (d_q, d_k, d_v), d_f, dw_early)
    grad_x, red_in, red_kv, red_out, sred = _grad_x_exchange(pieces, d_f, w_my, x2, norm_g, d_out, dw_in, land_kv, land_out, d_mem_g,
                                                             d_scale, d_bf, d_gq, d_gk, d_gqm, d_gkm, dwp_bd, loss_blk)
    new = _update(red_in, red_kv, red_out, sred, params)
    result = [sred[SB_LOSS, 0], grad_x[None]]
    for kind in range(4):
        for name in PARAM_ORDER:
            out = new[name][kind]
            if name == "w_in":
                out = out.reshape(IN_CHUNK, D_MODEL).T
            result.append(out[None] if given[name][0].ndim > 2 else out)
    return tuple(result)
```

```python
import functools

import jax
import jax.numpy as jnp
import numpy as np
from jax import lax
from jax.experimental import pallas as pl
from jax.experimental.pallas import tpu as pltpu

F32 = jnp.float32
BF16 = jnp.bfloat16
MESH = pl.DeviceIdType.MESH

N_DEV = 8
D_MODEL = 1024
HEAD_DIM = 64
FOX_HEADS = 8
MEM_HEADS = 4
N_MEM = 256
POOL_WIDTH = 256
EPS = 1e-6
IN_WIDTH = 3080
IN_CHUNK = IN_WIDTH // N_DEV
CHUNK_ROWS = 400
F_OFF = 2048
MY_WIDTH = 3200
MY_F_OFF = 3072
PIECE = 512
TM = 256
TMM = 512
TA = 256
HB = 8
HB_FWD = 8
AUG_ROWS = 72
HALO = 16
VMEM_LIMIT = 56 * 1024 * 1024

ADAM_LR = 0.001
ADAM_B1 = 0.9
ADAM_B2 = 0.999
ADAM_EPS = 1e-08
ADAM_WD = 0.01
ADAM_STEP = 10


def _dot(a, b):
    return jnp.dot(a, b, preferred_element_type=F32)


def _dot_nt(a, b):
    return lax.dot_general(a, b, (((1,), (1,)), ((), ())), preferred_element_type=F32)


def _dot_tn(a, b):
    return lax.dot_general(a, b, (((0,), (0,)), ((), ())), preferred_element_type=F32)


def _sigmoid(g):
    return 0.5 + 0.5 * jnp.tanh(0.5 * g)


def _split3(v):
    hi = v.astype(BF16)
    r1 = v - hi.astype(F32)
    mid = r1.astype(BF16)
    lo = (r1 - mid.astype(F32)).astype(BF16)
    return hi, mid, lo


def _rms(v):
    return lax.rsqrt(jnp.mean(v * v, axis=-1, keepdims=True) + EPS)


def _rms_bwd(a, r, g, dy):
    da = dy * g
    return r * (da - a * jnp.mean(da * a, axis=-1, keepdims=True)), dy * a


def _head_mean(z, seg):
    hi = z.astype(BF16)
    lo = (z - hi.astype(F32)).astype(BF16)
    if z.shape[1] == 256:
        return _dot(hi, seg) + _dot(lo, seg)
    half = seg[0:256, 0:256]
    return jnp.concatenate([_dot(hi[:, 0:256], half) + _dot(lo[:, 0:256], half),
                            _dot(hi[:, 256:512], half) + _dot(lo[:, 256:512], half)], axis=1)


def _head_rms(v, seg):
    return lax.rsqrt(_head_mean(v * v, seg) + EPS)


def _head_rms_bwd(a, r, g, dy, seg):
    da = dy * g
    return r * (da - a * _head_mean(da * a, seg)), dy * a


def _fold_heads(row, n_heads):
    out = row[:, 0:HEAD_DIM]
    for h in range(1, n_heads):
        out = out + row[:, HEAD_DIM * h:HEAD_DIM * (h + 1)]
    return out


def _params(sem, limit=VMEM_LIMIT):
    return pltpu.CompilerParams(dimension_semantics=sem, vmem_limit_bytes=limit)


def _full(shape):
    return pl.BlockSpec(shape, lambda *_: (0,) * len(shape))


def _chunk_segments(d):
    runs = []
    for lo, hi, shift in ((0, F_OFF, 0), (F_OFF, F_OFF + FOX_HEADS, MY_F_OFF - F_OFF), (F_OFF + FOX_HEADS, IN_WIDTH, -FOX_HEADS)):
        a, b = max(lo, IN_CHUNK * d), min(hi, IN_CHUNK * (d + 1))
        if a < b:
            runs.append((a + shift, b - a))
    return runs


def _my_place():
    x, y, c = lax.axis_index("x"), lax.axis_index("y"), lax.axis_index("c")
    return x, y, c, 4 * x + 2 * y + c


def _shard_rows(dev):
    return pl.ds(pl.multiple_of(128 * dev, 128), 128)


def _gather_w_in(w_in, w_kv, w_out, w_pool, b_f, gq, gk, gqm):
    def body(win_ref, wkv_ref, wout_ref, wp_ref, bf_ref, gq_ref, gk_ref, gqm_ref, wmy_ref, kv16_ref, out16_ref, wpbd_ref, bfpad_ref,
             gq8_ref, gk8_ref, gqm4_ref, in_all, pay_in, win_rows, send_sems, recv_sems, load_sem):
        x, y, c, me = _my_place()
        here, sibling = (x, y, c), (x, y, 1 - c)
        x_chip, y_chip, far_chip = (1 - x, y), (x, 1 - y), (1 - x, 1 - y)
        relay_from = (x ^ (1 - c), y ^ c)
        relay_to = (x ^ c, y ^ (1 - c))

        load = pltpu.make_async_copy(win_ref, win_rows, load_sem)
        load.start()
        load.wait()
        pay_in[...] = jnp.zeros((CHUNK_ROWS, D_MODEL), BF16)
        for j in range(8):
            pay_in[0:IN_CHUNK, 128 * j:128 * (j + 1)] = win_rows[pl.ds(j, IN_CHUNK, stride=8), :].astype(BF16)

        def copy(k, block, to, own=False):
            px, py, pc = block
            dst = in_all.at[4 * px + 2 * py + pc]
            return pltpu.make_async_remote_copy(src_ref=pay_in if own else dst, dst_ref=dst, send_sem=send_sems.at[k],
                                                recv_sem=recv_sems.at[k], device_id=to, device_id_type=MESH)

        first = [copy(0, here, sibling, own=True), copy(1, here, (*x_chip, c), own=True), copy(2, here, (*y_chip, c), own=True)]
        for cp in first:
            cp.start()
        in_all[me] = pay_in[...]
        kv16_ref[...] = wkv_ref[...].astype(BF16)
        out16_ref[...] = wout_ref[...].astype(BF16)
        wpbd_ref[...] = jnp.zeros((POOL_WIDTH, POOL_WIDTH), BF16)
        for grp in range(4):
            sl = slice(64 * grp, 64 * (grp + 1))
            wpbd_ref[sl, sl] = wp_ref[grp].astype(BF16)
        bfpad_ref[...] = jnp.zeros((1, 128), F32)
        bfpad_ref[:, 0:FOX_HEADS] = bf_ref[...]
        gq8_ref[...] = jnp.concatenate([gq_ref[...]] * FOX_HEADS, axis=1)
        gk8_ref[...] = jnp.concatenate([gk_ref[...]] * FOX_HEADS, axis=1)
        gqm4_ref[...] = jnp.concatenate([gqm_ref[...]] * MEM_HEADS, axis=1)
        copy(1 + c, (*relay_from, c), here).wait_recv()
        passed = [copy(3, (*relay_from, c), (*relay_to, c)), copy(4, (*relay_from, c), sibling)]
        for cp in passed:
            cp.start()
        copy(2 - c, (*relay_to, c), here).wait_recv()
        passed.append(copy(5, (*relay_to, c), sibling))
        passed[-1].start()
        copy(3, (*far_chip, c), here).wait_recv()
        passed.append(copy(6, (*far_chip, c), sibling))
        passed[-1].start()
        copy(0, sibling, here).wait_recv()
        for k, chip in ((4, relay_to), (5, relay_from), (6, far_chip)):
            copy(k, (*chip, 1 - c), here).wait_recv()
        for cp in first + passed:
            cp.wait_send()

        row_pad = jnp.zeros((512 - CHUNK_ROWS, 256), F32)
        for r0 in range(0, D_MODEL, 256):
            ch = [jnp.concatenate([in_all[d, :, r0:r0 + 256].astype(F32), row_pad], axis=0).T[:, 0:IN_CHUNK] for d in range(N_DEV)]
            lo = F_OFF - 5 * IN_CHUNK
            full = jnp.concatenate(ch[:5] + [ch[5][:, :lo], ch[5][:, lo + FOX_HEADS:], ch[6], ch[7], ch[5][:, lo:lo + FOX_HEADS],
                                             jnp.zeros((256, MY_WIDTH - IN_WIDTH), F32)], axis=1)
            wmy_ref[r0:r0 + 256, :] = full.astype(BF16)

    return pl.pallas_call(
        body,
        name="gather_w_in",
        out_shape=(jax.ShapeDtypeStruct((D_MODEL, MY_WIDTH), BF16), jax.ShapeDtypeStruct((128, 512), BF16),
                   jax.ShapeDtypeStruct((128, D_MODEL), BF16), jax.ShapeDtypeStruct((POOL_WIDTH, POOL_WIDTH), BF16),
                   jax.ShapeDtypeStruct((1, 128), F32), jax.ShapeDtypeStruct((1, PIECE), F32), jax.ShapeDtypeStruct((1, PIECE), F32),
                   jax.ShapeDtypeStruct((1, 256), F32)),
        in_specs=[pl.BlockSpec(memory_space=pl.ANY)] + [pl.BlockSpec(memory_space=pltpu.VMEM)] * 7,
        out_specs=(pl.BlockSpec(memory_space=pltpu.VMEM),) * 8,
        scratch_shapes=[
            pltpu.VMEM((N_DEV, CHUNK_ROWS, D_MODEL), BF16),
            pltpu.VMEM((CHUNK_ROWS, D_MODEL), BF16),
            pltpu.VMEM((IN_CHUNK * 8, 128), F32),
            pltpu.SemaphoreType.DMA((7,)),
            pltpu.SemaphoreType.DMA((7,)),
            pltpu.SemaphoreType.DMA,
        ],
        compiler_params=pltpu.CompilerParams(vmem_limit_bytes=VMEM_LIMIT),
    )(w_in, w_kv, w_out, w_pool, b_f, gq, gk, gqm)


def _row_block_exchange(kv_ref, out_ref, kv_dst, out_dst, send_sems, recv_sems, local_sems, gather):
    x, y, c, me = _my_place()
    remote = []
    for k in range(1, N_DEV):
        px, py, pc = x ^ (k >> 2), y ^ ((k >> 1) & 1), c ^ (k & 1)
        peer = 4 * px + 2 * py + pc
        for fam, (src, dst) in enumerate(((kv_ref, kv_dst), (out_ref, out_dst))):
            remote.append(pltpu.make_async_remote_copy(
                src_ref=src if gather else src.at[_shard_rows(peer)], dst_ref=dst.at[_shard_rows(me)] if gather else dst.at[me],
                send_sem=send_sems.at[7 * fam + k - 1], recv_sem=recv_sems.at[7 * fam + k - 1],
                device_id=(px, py, pc), device_id_type=MESH))
    local = [pltpu.make_async_copy(src if gather else src.at[_shard_rows(me)], dst.at[_shard_rows(me)] if gather else dst.at[me],
                                   local_sems.at[fam]) for fam, (src, dst) in enumerate(((kv_ref, kv_dst), (out_ref, out_dst)))]
    return remote, local


SB_ROWS, SB_W = 80, 256
SB_NORM_G, SB_MEM_NORM_G, SB_POOL_SCALE, SB_B_F, SB_FOX_Q, SB_FOX_K, SB_MEM_Q, SB_MEM_K, SB_LOSS, SB_W_POOL = 0, 4, 8, 9, 10, 11, 12, 13, 14, 16


def _log_sigmoid(z):
    return jnp.minimum(z, 0.0) - jnp.log(1.0 + jnp.exp(-jnp.abs(z)))


def _forget_lane_maps():
    to_q = np.zeros((128, FOX_HEADS * 128), np.float32)
    to_k = np.zeros((128, FOX_HEADS * 128), np.float32)
    for h in range(FOX_HEADS):
        for term in range(3):
            to_q[8 * term + h, 128 * h + 64 + term] = 1.0
            to_q[24, 128 * h + 67 + term] = 1.0
            to_k[24, 128 * h + 64 + term] = 1.0
            to_k[8 * term + h, 128 * h + 67 + term] = -1.0
    return jnp.asarray(to_q, BF16), jnp.asarray(to_k, BF16)


def _proj_prep(x, norm_g, w_my, bf_pad, gq, gk, seg):
    s_len = x.shape[0]
    n_blocks = s_len // TMM
    halves = TMM // TM
    to_q, to_k = _forget_lane_maps()
    q_off, f_off = PIECE, MY_F_OFF
    kept = 3 * PIECE + 128

    def body(x_ref, g_ref, w_ref, bf_ref, gq_ref, gk_ref, seg_ref, eq_ref, ek_ref,
             proj_ref, h_ref, qa_ref, ka_ref, vh_ref, kt_ref, vt_ref, kept_even, kept_odd, carry_ref):
        s = pl.program_id(0)

        def project(kept_ref):
            xv = x_ref[...]
            h_ref[...] = (xv * _rms(xv) * g_ref[...]).astype(BF16)
            yield
            for c0 in range(0, MY_WIDTH, 256):
                c1 = min(c0 + 256, MY_WIDTH)
                res = _dot(h_ref[...], w_ref[:, c0:c1])
                proj_ref[:, c0:c1] = res
                if q_off <= c0 < q_off + 3 * PIECE:
                    kept_ref[:, c0 - q_off:c1 - q_off] = res
                if c0 == f_off:
                    kept_ref[:, 3 * PIECE:kept] = res
                yield

        def operands(kept_ref):
            lane = lax.broadcasted_iota(jnp.int32, (TM, 128), 1)
            row = lax.broadcasted_iota(jnp.int32, (TM, TM), 0)
            col = lax.broadcasted_iota(jnp.int32, (TM, TM), 1)
            tri = jnp.where(col <= row, 1.0, 0.0).astype(BF16)
            one_at_64 = jnp.where(lane == 64, 1.0, 0.0)
            zeros = jnp.zeros((TM, HEAD_DIM), F32)
            for half in range(halves):
                rows = slice(TM * half, TM * (half + 1))
                logf = jnp.where(lane < FOX_HEADS, _log_sigmoid(kept_ref[rows, 3 * PIECE:kept] + bf_ref[...]), 0.0)
                hi, mid, lo = _split3(logf)
                cum = _dot(tri, hi) + _dot(tri, mid) + _dot(tri, lo) + carry_ref[...]
                carry_ref[...] = cum[TM - 1:TM, :]
                f_hi, f_mid, f_lo = [t.astype(F32) for t in _split3(cum)]
                packed = (f_hi + pltpu.roll(f_mid, 8, 1) + pltpu.roll(f_lo, 16, 1)
                          + jnp.where(lane == 24, 1.0, 0.0)).astype(BF16)
                extra_q = _dot(packed, eq_ref[...])
                extra_k = _dot(packed, ek_ref[...])
                q_all = kept_ref[rows, 0:PIECE]
                k_all = kept_ref[rows, PIECE:2 * PIECE]
                qn_all = q_all * _head_rms(q_all, seg_ref[...]) * gq_ref[...] * 0.125
                kn_all = k_all * _head_rms(k_all, seg_ref[...]) * gk_ref[...]
                yield
                for h in range(FOX_HEADS):
                    sl = slice(HEAD_DIM * h, HEAD_DIM * (h + 1))
                    tile = slice(128 * h, 128 * (h + 1))
                    qa = jnp.where(lane < 64, jnp.concatenate([qn_all[:, sl], zeros], axis=1), extra_q[:, tile])
                    ka = jnp.where(lane < 64, jnp.concatenate([kn_all[:, sl], zeros], axis=1), extra_k[:, tile])
                    vh = kept_ref[rows, 2 * PIECE + HEAD_DIM * h:2 * PIECE + HEAD_DIM * (h + 1)]
                    v_aug = jnp.where(lane < 64, jnp.concatenate([vh, zeros], axis=1), one_at_64)
                    qa_ref[h, rows, :] = qa.astype(BF16)
                    ka_ref[h, rows, :] = ka.astype(BF16)
                    vh_ref[h, rows, :] = vh.astype(BF16)
                    kt_ref[h, half] = ka.T.astype(BF16)
                    vt_ref[h, half] = v_aug.T.astype(BF16)
                    if h % 2 == 1:
                        yield

        def alternate(*parts):
            state = [[part, 0, stages] for part, stages in parts]
            while state:
                least = min(state, key=lambda entry: entry[1] / entry[2])
                least[1] += 1
                if next(least[0], "done") == "done":
                    state.remove(least)

        projecting = lambda kept_ref: (project(kept_ref), 1 + pl.cdiv(MY_WIDTH, 256))
        making = lambda kept_ref: (operands(kept_ref), halves * (1 + FOX_HEADS // 2))

        @pl.when(s == 0)
        def _():
            carry_ref[...] = jnp.zeros((1, 128), F32)
            alternate(projecting(kept_even))

        @pl.when((s > 0) & (s < n_blocks) & (s % 2 == 1))
        def _():
            alternate(projecting(kept_odd), making(kept_even))

        @pl.when((s > 0) & (s < n_blocks) & (s % 2 == 0))
        def _():
            alternate(projecting(kept_even), making(kept_odd))

        @pl.when(s == n_blocks)
        def _():
            alternate(making(kept_odd if n_blocks % 2 == 0 else kept_even))

    made = lambda s: jnp.minimum(s, n_blocks - 1)
    used = lambda s: jnp.maximum(s - 1, 0)
    head_blk = lambda w: pl.BlockSpec((FOX_HEADS, TMM, w), lambda s: (0, used(s), 0))
    tile_blk = pl.BlockSpec((FOX_HEADS, halves, 128, TM), lambda s: (0, used(s), 0, 0))
    tiled = jax.ShapeDtypeStruct((FOX_HEADS, s_len // TM, 128, TM), BF16)
    return pl.pallas_call(
        body,
        name="proj_prep",
        grid=(n_blocks + 1,),
        in_specs=[pl.BlockSpec((TMM, D_MODEL), lambda s: (made(s), 0)), _full((1, D_MODEL)), _full((D_MODEL, MY_WIDTH)),
                  _full((1, 128)), _full((1, PIECE)), _full((1, PIECE)), _full((PIECE, PIECE)),
                  _full((128, FOX_HEADS * 128)), _full((128, FOX_HEADS * 128))],
        out_specs=[pl.BlockSpec((TMM, MY_WIDTH), lambda s: (made(s), 0)), pl.BlockSpec((TMM, D_MODEL), lambda s: (made(s), 0)),
                   head_blk(128), head_blk(128), head_blk(HEAD_DIM), tile_blk, tile_blk],
        out_shape=[jax.ShapeDtypeStruct((s_len, MY_WIDTH), F32), jax.ShapeDtypeStruct((s_len, D_MODEL), BF16),
                   jax.ShapeDtypeStruct((FOX_HEADS, s_len, 128), BF16), jax.ShapeDtypeStruct((FOX_HEADS, s_len, 128), BF16),
                   jax.ShapeDtypeStruct((FOX_HEADS, s_len, HEAD_DIM), BF16), tiled, tiled],
        scratch_shapes=[pltpu.VMEM((TMM, kept), F32), pltpu.VMEM((TMM, kept), F32), pltpu.VMEM((1, 128), F32)],
        compiler_params=_params(("arbitrary",)),
    )(x, norm_g, w_my, bf_pad, gq, gk, seg, to_q, to_k)


def _mem_prep(mem, g, w_kv, gk):
    def body(mem_ref, g_ref, w_ref, gk_ref, km_ref, vm_ref):
        m = mem_ref[...]
        kv = _dot((m * _rms(m) * g_ref[...]).astype(BF16), w_ref[...])
        for h in range(MEM_HEADS):
            sl = slice(HEAD_DIM * h, HEAD_DIM * (h + 1))
            kh = kv[:, sl]
            km_ref[:, sl] = (kh * _rms(kh) * gk_ref[...]).astype(BF16)
        vm_ref[...] = kv[:, 256:512].astype(BF16)

    return pl.pallas_call(
        body,
        name="mem_prep",
        out_shape=(jax.ShapeDtypeStruct((N_MEM, 256), BF16),) * 2,
        in_specs=[pl.BlockSpec(memory_space=pltpu.VMEM)] * 4,
        out_specs=(pl.BlockSpec(memory_space=pltpu.VMEM),) * 2,
        compiler_params=pltpu.CompilerParams(vmem_limit_bytes=VMEM_LIMIT),
    )(mem, g, w_kv, gk)


def _causal_mask_t():
    row = lax.broadcasted_iota(jnp.int32, (TA, TA), 0)
    col = lax.broadcasted_iota(jnp.int32, (TA, TA), 1)
    return row <= col


def _fox_fwd(q_aug, k_aug, vt_aug, w_kv16, w_out16):
    s_len = q_aug.shape[1]
    n_tiles = s_len // TA
    hb = HB_FWD
    n_groups = FOX_HEADS // hb

    def body(q_ref, k_new, vt_new, kv16_ref, out16_ref, o_ref, st_ref, kv_all, out_all, k_ref, vt_ref, send_sems, recv_sems, local_sems):
        qi = pl.program_id(1)
        for h in range(hb):
            k_ref[h, pl.ds(pl.multiple_of(qi * TA, TA), TA), :] = k_new[h]
            vt_ref[h, qi] = vt_new[h, 0]
        remote, local = _row_block_exchange(kv16_ref, out16_ref, kv_all, out_all, send_sems, recv_sems, local_sems, gather=True)

        @pl.when((pl.program_id(0) == 0) & (qi == 0))
        def _():
            for cp in remote + local:
                cp.start()

        qs = [q_ref[h] for h in range(hb)]

        def step(t0, carry, masked, width):
            rows = pl.ds(pl.multiple_of(t0 * TA, TA), width * TA)
            scores = [_dot_nt(k_ref[h, rows, :], qs[h]) for h in range(hb)]
            soft = []
            for h in range(hb):
                m, _ = carry[h]
                s = jnp.where(_causal_mask_t(), scores[h], -1e30) if masked else scores[h]
                m_new = jnp.maximum(m, jnp.max(s, axis=0, keepdims=True))
                soft.append((m_new, jnp.exp(m - m_new), jnp.exp(s - m_new).astype(BF16)))
            out = []
            for h, (m_new, alpha, p) in enumerate(soft):
                acc = alpha * carry[h][1]
                for t in range(width):
                    acc = acc + _dot(vt_ref[h, t0 + t, 0:AUG_ROWS, :], p[t * TA:(t + 1) * TA])
                out.append((m_new, acc))
            return tuple(out)

        init = tuple((jnp.full((1, TA), -1e30, F32), jnp.zeros((AUG_ROWS, TA), F32)) for _ in range(hb))
        carry = lax.fori_loop(0, qi // 2, lambda j, cr: step(2 * j, cr, False, 2), init)
        carry = lax.fori_loop(2 * (qi // 2), qi, lambda j, cr: step(j, cr, False, 1), carry)
        carry = step(qi, carry, True, 1)
        for h in range(hb):
            m, acc = carry[h]
            l = acc[HEAD_DIM:HEAD_DIM + 1, :]
            lse = m + jnp.log(l)
            o_ref[h] = jnp.concatenate([acc[0:HEAD_DIM] / l, jnp.broadcast_to(lse, (HEAD_DIM, TA))], axis=0).T
            st_ref[h, 0] = jnp.broadcast_to(lse, (8, TA))

        @pl.when((pl.program_id(0) == n_groups - 1) & (qi == n_tiles - 1))
        def _():
            for cp in remote:
                cp.wait_recv()
            for cp in remote:
                cp.wait_send()
            for cp in local:
                cp.wait()

    hbm = pl.BlockSpec(memory_space=pl.ANY)
    return pl.pallas_call(
        body,
        name="fox_fwd",
        grid=(n_groups, n_tiles),
        in_specs=[pl.BlockSpec((hb, TA, 128), lambda g, i: (g, i, 0)), pl.BlockSpec((hb, TA, 128), lambda g, i: (g, i, 0)),
                  pl.BlockSpec((hb, 1, 128, TA), lambda g, i: (g, i, 0, 0)), hbm, hbm],
        out_specs=[pl.BlockSpec((hb, TA, 128), lambda g, i: (g, i, 0)), pl.BlockSpec((hb, 1, 8, TA), lambda g, i: (g, i, 0, 0)),
                   hbm, hbm],
        out_shape=[jax.ShapeDtypeStruct((FOX_HEADS, s_len, 128), F32), jax.ShapeDtypeStruct((FOX_HEADS, n_tiles, 8, TA), F32),
                   jax.ShapeDtypeStruct((D_MODEL, 512), BF16), jax.ShapeDtypeStruct((D_MODEL, D_MODEL), BF16)],
        scratch_shapes=[pltpu.VMEM((hb, s_len, 128), BF16), pltpu.VMEM((hb, n_tiles, 128, TA), BF16),
                        pltpu.SemaphoreType.DMA((14,)), pltpu.SemaphoreType.DMA((14,)), pltpu.SemaphoreType.DMA((2,))],
        compiler_params=_params(("arbitrary", "arbitrary")),
    )(q_aug, k_aug, vt_aug, w_kv16, w_out16)


def _lane_group(lane, a, b, c, d):
    return jnp.where(lane < 64, a, jnp.where(lane < 128, b, jnp.where(lane < 192, c, d)))


def _pool_count(row0):
    lane = lax.broadcasted_iota(jnp.int32, (TM, POOL_WIDTH), 1)
    t1 = row0 + lax.broadcasted_iota(jnp.int32, (TM, POOL_WIDTH), 0) + 1
    return 1.0 / jnp.minimum(t1, _lane_group(lane, 2, 4, 8, 16)).astype(F32), lane


def _pool_delta(u, halo, cnt, lane):
    xe = jnp.concatenate([halo, u], axis=0)
    s2 = xe + pltpu.roll(xe, 1, 0)
    s4 = s2 + pltpu.roll(s2, 2, 0)
    s8 = s4 + pltpu.roll(s4, 4, 0)
    s16 = s8 + pltpu.roll(s8, 8, 0)
    win = _lane_group(lane, s2[HALO:], s4[HALO:], s8[HALO:], s16[HALO:])
    return win * cnt - u


def _pool_delta_bwd(dd, dd_next, cnt, cnt_next, lane):
    ee = jnp.concatenate([dd * cnt, dd_next * cnt_next], axis=0)
    n = TM + HALO
    r2 = ee + pltpu.roll(ee, n - 1, 0)
    r4 = r2 + pltpu.roll(r2, n - 2, 0)
    r8 = r4 + pltpu.roll(r4, n - 4, 0)
    r16 = r8 + pltpu.roll(r8, n - 8, 0)
    return _lane_group(lane, r2[:TM], r4[:TM], r8[:TM], r16[:TM]) - dd


def _mem_attn(qm, gq, seg, km_ref, vm_ref):
    r = _head_rms(qm, seg)
    a = qm * r
    q16 = (a * gq * 0.125).astype(BF16)
    heads = []
    for h in range(MEM_HEADS):
        sl = slice(HEAD_DIM * h, HEAD_DIM * (h + 1))
        s = _dot_nt(q16[:, sl], km_ref[:, sl])
        e = jnp.exp(s - jnp.max(s, axis=-1, keepdims=True))
        p = e * (1.0 / jnp.sum(e, axis=-1, keepdims=True))
        heads.append((p, _dot(p.astype(BF16), vm_ref[:, sl])))
    return a, r, q16, heads


def _mix_out_loss(proj, olse, km, vm, wp_bd, pool_scale, gq_m, seg, x, target, w_out):
    s_len = x.shape[0]
    n_blocks = s_len // TMM
    halves = TMM // TM

    def body(ua_ref, halo_ref, gb_ref, qm_ref, ol_ref, km_ref, vm_ref, wp_ref, sc_ref, gq_ref, seg_ref, x_ref, t_ref, w_ref,
             dout_ref, dmix_ref, dw16_ref, loss_ref, mixed_scr, dw_ref):
        s = pl.program_id(0)

        @pl.when(s == 0)
        def _():
            dw_ref[...] = jnp.zeros((D_MODEL, D_MODEL), F32)
            loss_ref[...] = jnp.zeros((8, 128), F32)
            mixed_scr[...] = jnp.zeros((2, TMM, D_MODEL), BF16)

        mixed16 = mixed_scr[(s + 1) % 2]
        err = x_ref[...] + _dot(mixed16, w_ref[...]) - t_ref[...]
        loss_ref[...] += jnp.where(s >= 1, 0.5 * jnp.sum(jnp.mean(err * err, axis=-1, keepdims=True)), 0.0)
        d_out = err / float(D_MODEL)
        dout_ref[...] = d_out
        d16 = d_out.astype(BF16)
        dmix_ref[...] = _dot_nt(d16, w_ref[...])
        dw_ref[...] += _dot_tn(mixed16, d16)

        slot = s % 2
        for half in range(halves):
            r0 = TM * half
            rows = slice(r0, r0 + TM)
            u = ua_ref[rows, 0:256]
            g_a = ua_ref[rows, 256:512]
            halo = jnp.where(s > 0, halo_ref[...], 0.0) if half == 0 else ua_ref[r0 - HALO:r0, 0:256]
            cnt, lane = _pool_count(s * TMM + r0)
            d = _pool_delta(u, halo, cnt, lane).astype(BF16)
            y_a = _dot(d, wp_ref[...]) * sc_ref[...]
            mixed_scr[slot, rows, 0:256] = (y_a * (g_a * _sigmoid(g_a))).astype(BF16)
            g_b = gb_ref[rows, :]
            y_b = jnp.concatenate([ol_ref[h, rows, 0:HEAD_DIM] for h in range(FOX_HEADS)], axis=1)
            mixed_scr[slot, rows, 256:768] = (y_b * (g_b * _sigmoid(g_b))).astype(BF16)
            _, _, _, heads = _mem_attn(qm_ref[rows, 0:256], gq_ref[...], seg_ref[0:256, 0:256], km_ref, vm_ref)
            g_m = qm_ref[rows, 256:512]
            y_m = jnp.concatenate([y for _, y in heads], axis=1)
            mixed_scr[slot, rows, 768:1024] = (y_m * (g_m * _sigmoid(g_m))).astype(BF16)

        @pl.when(s == n_blocks)
        def _():
            dw16_ref[...] = dw_ref[...].astype(BF16)

    build = lambda s: jnp.minimum(s, n_blocks - 1)
    use = lambda s: jnp.maximum(s - 1, 0)
    piece = lambda j: pl.BlockSpec((TMM, PIECE), lambda s: (build(s), j))
    halo_blk = pl.BlockSpec((HALO, 256), lambda s: (jnp.maximum(build(s) * (TMM // HALO) - 1, 0), 0))
    row = pl.BlockSpec((TMM, D_MODEL), lambda s: (use(s), 0))
    return pl.pallas_call(
        body,
        name="mix_out_loss",
        grid=(n_blocks + 1,),
        in_specs=[piece(0), halo_blk, piece(4), piece(5), pl.BlockSpec((FOX_HEADS, TMM, 128), lambda s: (0, build(s), 0)),
                  _full((N_MEM, 256)), _full((N_MEM, 256)), _full((256, 256)), _full((1, 256)), _full((1, 256)),
                  _full((PIECE, PIECE)), row, row, _full((D_MODEL, D_MODEL))],
        out_specs=[row, row, _full((D_MODEL, D_MODEL)), _full((8, 128))],
        out_shape=[jax.ShapeDtypeStruct((s_len, D_MODEL), F32), jax.ShapeDtypeStruct((s_len, D_MODEL), F32),
                   jax.ShapeDtypeStruct((D_MODEL, D_MODEL), BF16), jax.ShapeDtypeStruct((8, 128), F32)],
        scratch_shapes=[pltpu.VMEM((2, TMM, D_MODEL), BF16), pltpu.VMEM((D_MODEL, D_MODEL), F32)],
        compiler_params=_params(("arbitrary",)),
    )(proj, proj, proj, proj, olse, km, vm, wp_bd, pool_scale, gq_m, seg, x, target, w_out)


def _silu_pair(g):
    s = _sigmoid(g)
    return g * s, s * (1.0 + g * (1.0 - s))


def _mix_bwd(proj, olse, d_mixed, km, vm, wp_bd, pool_scale, gq_m, seg):
    s_len = proj.shape[0]
    n_tiles = s_len // TM

    def body(ua_ref, halo_ref, ga_next_ref, gb_ref, qm_ref, ol_ref, dm_ref, dm_next_ref, km_ref, vm_ref, wp_ref, sc_ref, gq_ref,
             seg_ref, dua_ref, dgb_ref, dqm_ref, do_ref, dl_ref, dwp_ref, dsc_ref, dkm_ref, dvm_ref, dgq_ref):
        i = pl.program_id(0)

        @pl.when(i == 0)
        def _():
            dwp_ref[...] = jnp.zeros((256, 256), F32)
            dsc_ref[...] = jnp.zeros((1, 256), F32)
            dkm_ref[...] = jnp.zeros((N_MEM, 256), F32)
            dvm_ref[...] = jnp.zeros((N_MEM, 256), F32)
            dgq_ref[...] = jnp.zeros((1, HEAD_DIM), F32)

        u = ua_ref[:, 0:256]
        g_a = ua_ref[:, 256:512]
        halo = jnp.where(i > 0, halo_ref[...], 0.0)
        cnt, lane = _pool_count(i * TM)
        d16 = _pool_delta(u, halo, cnt, lane).astype(BF16)
        t = _dot(d16, wp_ref[...])
        y_a = t * sc_ref[...]
        silu_a, dsilu_a = _silu_pair(g_a)
        dm_a = dm_ref[:, 0:256]
        dy_a = dm_a * silu_a
        dsc_ref[...] += jnp.sum(dy_a * t, axis=0, keepdims=True)
        dt16 = (dy_a * sc_ref[...]).astype(BF16)
        dwp_ref[...] += _dot_tn(d16, dt16)
        dd = _dot_nt(dt16, wp_ref[...])
        g_next = ga_next_ref[...]
        dt_next = (dm_next_ref[...] * (g_next * _sigmoid(g_next)) * sc_ref[...]).astype(BF16)
        dd_next = jnp.where(i < n_tiles - 1, _dot_nt(dt_next, wp_ref[...]), 0.0)
        cnt_next = _pool_count((i + 1) * TM)[0][0:HALO]
        dua_ref[:, 0:256] = _pool_delta_bwd(dd, dd_next, cnt, cnt_next, lane).astype(BF16)
        dua_ref[:, 256:512] = (dm_a * y_a * dsilu_a).astype(BF16)

        silu_b, dsilu_b = _silu_pair(gb_ref[...])
        dm_b = dm_ref[:, 256:768]
        o_all = jnp.concatenate([ol_ref[h][:, 0:HEAD_DIM] for h in range(FOX_HEADS)], axis=1)
        d_o = dm_b * silu_b
        dgb_ref[...] = (dm_b * o_all * dsilu_b).astype(BF16)
        for h in range(FOX_HEADS):
            do_ref[h] = d_o[:, HEAD_DIM * h:HEAD_DIM * (h + 1)].astype(BF16)
        prod = d_o * o_all
        hi = prod.astype(BF16)
        lo = (prod - hi.astype(F32)).astype(BF16)
        head_of_lane = lax.broadcasted_iota(jnp.int32, (FOX_HEADS, PIECE), 1) // HEAD_DIM
        pick = jnp.where(head_of_lane == lax.broadcasted_iota(jnp.int32, (FOX_HEADS, PIECE), 0), 1.0, 0.0).astype(BF16)
        delta = _dot_nt(pick, hi) + _dot_nt(pick, lo)
        for h in range(FOX_HEADS):
            dl_ref[h, 0] = jnp.broadcast_to(delta[h:h + 1, :], (8, TM))

        seg = seg_ref[0:256, 0:256]
        a, r, q16, heads = _mem_attn(qm_ref[:, 0:256], gq_ref[...], seg, km_ref, vm_ref)
        silu_m, dsilu_m = _silu_pair(qm_ref[:, 256:512])
        dm_m = dm_ref[:, 768:1024]
        y_m = jnp.concatenate([y for _, y in heads], axis=1)
        dqm_ref[:, 256:512] = (dm_m * y_m * dsilu_m).astype(BF16)
        dy16_all = (dm_m * silu_m).astype(BF16)
        d_scores = []
        for h in range(MEM_HEADS):
            sl = slice(HEAD_DIM * h, HEAD_DIM * (h + 1))
            p = heads[h][0]
            dy16 = dy16_all[:, sl]
            dp = _dot_nt(dy16, vm_ref[:, sl])
            dvm_ref[:, sl] += _dot_tn(p.astype(BF16), dy16)
            ds16 = (p * (dp - jnp.sum(dp * p, axis=-1, keepdims=True))).astype(BF16)
            dkm_ref[:, sl] += _dot_tn(ds16, q16[:, sl])
            d_scores.append(_dot(ds16, km_ref[:, sl]))
        dq, dg_rows = _head_rms_bwd(a, r, gq_ref[...], jnp.concatenate(d_scores, axis=1) * 0.125, seg)
        dqm_ref[:, 0:256] = dq.astype(BF16)
        dgq_ref[...] += _fold_heads(jnp.sum(dg_rows, axis=0, keepdims=True), MEM_HEADS)

    n_halo = s_len // HALO
    piece = lambda j: pl.BlockSpec((TM, PIECE), lambda i: (i, j))
    before = pl.BlockSpec((HALO, 256), lambda i: (jnp.maximum(i * (TM // HALO) - 1, 0), 0))
    after = lambda j: pl.BlockSpec((HALO, 256), lambda i: (jnp.minimum((i + 1) * (TM // HALO), n_halo - 1), j))
    row = pl.BlockSpec((TM, D_MODEL), lambda i: (i, 0))
    out_piece = pl.BlockSpec((TM, PIECE), lambda i: (i, 0))
    return pl.pallas_call(
        body,
        name="mix_bwd",
        grid=(n_tiles,),
        in_specs=[piece(0), before, after(1), piece(4), piece(5), pl.BlockSpec((FOX_HEADS, TM, 128), lambda i: (0, i, 0)),
                  row, after(0), _full((N_MEM, 256)), _full((N_MEM, 256)), _full((256, 256)), _full((1, 256)), _full((1, 256)),
                  _full((PIECE, PIECE))],
        out_specs=[out_piece, out_piece, out_piece, pl.BlockSpec((FOX_HEADS, TM, HEAD_DIM), lambda i: (0, i, 0)),
                   pl.BlockSpec((FOX_HEADS, 1, 8, TM), lambda i: (0, i, 0, 0)),
                   _full((256, 256)), _full((1, 256)), _full((N_MEM, 256)), _full((N_MEM, 256)), _full((1, HEAD_DIM))],
        out_shape=[jax.ShapeDtypeStruct((s_len, PIECE), BF16)] * 3 + [
            jax.ShapeDtypeStruct((FOX_HEADS, s_len, HEAD_DIM), BF16),
            jax.ShapeDtypeStruct((FOX_HEADS, n_tiles, 8, TM), F32),
            jax.ShapeDtypeStruct((256, 256), F32), jax.ShapeDtypeStruct((1, 256), F32),
            jax.ShapeDtypeStruct((N_MEM, 256), F32), jax.ShapeDtypeStruct((N_MEM, 256), F32),
            jax.ShapeDtypeStruct((1, HEAD_DIM), F32)],
        compiler_params=_params(("arbitrary",)),
    )(proj, proj, proj, proj, proj, olse, d_mixed, d_mixed, km, vm, wp_bd, pool_scale, gq_m, seg)


def _fox_bwd(q_aug, k_aug, kt_aug, v_h, d_o, lse_rows, delta_rows, dw_kv16, dw_out16):
    s_len = q_aug.shape[1]
    n_tiles = s_len // TA
    n_groups = FOX_HEADS // HB

    def body(q_hbm, k_ref, kt_ref, v_ref, do_hbm, st_ref, dl_ref, dkv16_ref, dout16_ref, dq_ref, dk_ref, dv_ref, land_kv, land_out,
             dqt_ref, q_ref, do_ref, send_sems, recv_sems, local_sems, tile_sems):
        j = pl.program_id(1)
        remote, local = _row_block_exchange(dkv16_ref, dout16_ref, land_kv, land_out, send_sems, recv_sems, local_sems, gather=False)

        def tile_loads(i):
            rows = pl.ds(pl.multiple_of(i * TA, TA), TA)
            return [pltpu.make_async_copy(q_hbm.at[:, rows, :], q_ref.at[:, rows, :], tile_sems.at[2 * i]),
                    pltpu.make_async_copy(do_hbm.at[:, rows, :], do_ref.at[:, rows, :], tile_sems.at[2 * i + 1])]

        @pl.when((pl.program_id(0) == 0) & (j == 0))
        def _():
            for i in range(n_tiles):
                for cp in tile_loads(i):
                    cp.start()
            for cp in remote + local:
                cp.start()

        @pl.when(j == 0)
        def _():
            dqt_ref[...] = jnp.zeros((HB, n_tiles, AUG_ROWS, TA), F32)

        ks = [k_ref[h] for h in range(HB)]
        kts = [kt_ref[h, 0, 0:AUG_ROWS, :] for h in range(HB)]
        vs = [v_ref[h] for h in range(HB)]

        def pair(i0, acc, masked, width):
            @pl.when(j == 0)
            def _():
                for t in range(width):
                    for cp in tile_loads(i0 + t):
                        cp.wait()

            rows = pl.ds(pl.multiple_of(i0 * TA, TA), width * TA)
            qs = [q_ref[h, rows, :] for h in range(HB)]
            d_os = [do_ref[h, rows, :] for h in range(HB)]
            row_stat = lambda ref, h: jnp.concatenate([ref[h, i0 + t, 0:1, :] for t in range(width)], axis=1)
            scores = [(_dot_nt(ks[h], qs[h]), _dot_nt(vs[h], d_os[h])) for h in range(HB)]
            probs = []
            for h in range(HB):
                s, dp = scores[h]
                if masked:
                    s = jnp.where(_causal_mask_t(), s, -1e30)
                p = jnp.exp(s - row_stat(st_ref, h))
                probs.append((p.astype(BF16), (p * (dp - row_stat(dl_ref, h))).astype(BF16)))
            out = []
            for h in range(HB):
                p16, ds16 = probs[h]
                dqt = _dot(kts[h], ds16)
                for t in range(width):
                    dqt_ref[h, i0 + t] += dqt[:, t * TA:(t + 1) * TA]
                out.append((acc[h][0] + _dot(ds16, qs[h]), acc[h][1] + _dot(p16, d_os[h])))
            return tuple(out)

        zero = tuple((jnp.zeros((TA, 128), F32), jnp.zeros((TA, HEAD_DIM), F32)) for _ in range(HB))
        acc = pair(j, zero, True, 1)
        odd = (n_tiles - 1 - j) % 2
        acc = lax.fori_loop(j + 1, j + 1 + odd, lambda i, a: pair(i, a, False, 1), acc)
        acc = lax.fori_loop(0, (n_tiles - 1 - j) // 2, lambda t, a: pair(j + 1 + odd + 2 * t, a, False, 2), acc)
        pad = jnp.zeros((128 - AUG_ROWS, TA), F32)
        for h in range(HB):
            dk_ref[h] = acc[h][0]
            dv_ref[h] = acc[h][1].astype(BF16)
            dq_ref[h] = jnp.concatenate([dqt_ref[h, j], pad], axis=0).T

        @pl.when((pl.program_id(0) == n_groups - 1) & (j == n_tiles - 1))
        def _():
            for cp in remote:
                cp.wait_recv()
            for cp in remote:
                cp.wait_send()
            for cp in local:
                cp.wait()

    assert n_groups == 1
    resident = pl.BlockSpec(memory_space=pltpu.VMEM)
    rows_blk = lambda r: resident
    tile = lambda w: pl.BlockSpec((HB, TA, w), lambda g, j: (g, j, 0))
    hbm = pl.BlockSpec(memory_space=pl.ANY)
    return pl.pallas_call(
        body,
        name="fox_bwd",
        grid=(n_groups, n_tiles),
        in_specs=[hbm, tile(128), pl.BlockSpec((HB, 1, 128, TA), lambda g, j: (g, j, 0, 0)), tile(HEAD_DIM),
                  hbm, rows_blk(8), rows_blk(8), hbm, hbm],
        out_specs=[tile(128), tile(128), tile(HEAD_DIM), hbm, hbm],
        out_shape=[jax.ShapeDtypeStruct((FOX_HEADS, s_len, 128), F32), jax.ShapeDtypeStruct((FOX_HEADS, s_len, 128), F32),
                   jax.ShapeDtypeStruct((FOX_HEADS, s_len, HEAD_DIM), BF16),
                   jax.ShapeDtypeStruct((N_DEV, 128, 512), BF16), jax.ShapeDtypeStruct((N_DEV, 128, D_MODEL), BF16)],
        scratch_shapes=[pltpu.VMEM((HB, n_tiles, AUG_ROWS, TA), F32),
                        pltpu.VMEM((HB, s_len, 128), BF16), pltpu.VMEM((HB, s_len, HEAD_DIM), BF16),
                        pltpu.SemaphoreType.DMA((14,)), pltpu.SemaphoreType.DMA((14,)), pltpu.SemaphoreType.DMA((2,)),
                        pltpu.SemaphoreType.DMA((2 * n_tiles,))],
        compiler_params=_params(("arbitrary", "arbitrary")),
    )(q_aug, k_aug, kt_aug, v_h, d_o, lse_rows, delta_rows, dw_kv16, dw_out16)


def _fox_post(proj, bf_pad, gq, gk, seg, dq_aug, dk_aug, dv_h):
    s_len = proj.shape[0]
    n_tiles = s_len // TM

    def body(q_ref, k_ref, f_ref, bf_ref, gq_ref, gk_ref, seg_ref, dqa_ref, dka_ref, dvh_ref,
             dq_ref, dk_ref, dv_ref, df_ref, dgq_ref, dgk_ref, dbf_ref, carry_ref):
        @pl.when(pl.program_id(0) == 0)
        def _():
            carry_ref[...] = jnp.zeros((1, 128), F32)
            dgq_ref[...] = jnp.zeros((1, HEAD_DIM), F32)
            dgk_ref[...] = jnp.zeros((1, HEAD_DIM), F32)
            dbf_ref[...] = jnp.zeros((1, 128), F32)

        lane = lax.broadcasted_iota(jnp.int32, (TM, 128), 1)
        seg = seg_ref[...]
        dqas = [dqa_ref[h] for h in range(FOX_HEADS)]
        dkas = [dka_ref[h] for h in range(FOX_HEADS)]
        d_cum = jnp.zeros((TM, 128), F32)
        for h in range(FOX_HEADS):
            d_cum = jnp.where(lane == h, dqas[h][:, 64:65] - dkas[h][:, 67:68], d_cum)
        q_all = q_ref[...]
        k_all = k_ref[...]
        rq = _head_rms(q_all, seg)
        rk = _head_rms(k_all, seg)
        dy_q = jnp.concatenate([t[:, 0:HEAD_DIM] for t in dqas], axis=1) * 0.125
        dy_k = jnp.concatenate([t[:, 0:HEAD_DIM] for t in dkas], axis=1)
        dq, dgq_rows = _head_rms_bwd(q_all * rq, rq, gq_ref[...], dy_q, seg)
        dk, dgk_rows = _head_rms_bwd(k_all * rk, rk, gk_ref[...], dy_k, seg)
        dq_ref[...] = dq.astype(BF16)
        dk_ref[...] = dk.astype(BF16)
        dv_ref[...] = jnp.concatenate([dvh_ref[h].astype(F32) for h in range(FOX_HEADS)], axis=1).astype(BF16)
        dgq_ref[...] += _fold_heads(jnp.sum(dgq_rows, axis=0, keepdims=True), FOX_HEADS)
        dgk_ref[...] += _fold_heads(jnp.sum(dgk_rows, axis=0, keepdims=True), FOX_HEADS)

        row = lax.broadcasted_iota(jnp.int32, (TM, TM), 0)
        col = lax.broadcasted_iota(jnp.int32, (TM, TM), 1)
        tri = jnp.where(col >= row, 1.0, 0.0).astype(BF16)
        hi, mid, lo = _split3(d_cum)
        d_logf = _dot(tri, hi) + _dot(tri, mid) + _dot(tri, lo) + carry_ref[...]
        carry_ref[...] = d_logf[0:1, :]
        z = f_ref[...] + bf_ref[...]
        d_f = jnp.where(lane < FOX_HEADS, d_logf / (1.0 + jnp.exp(z)), 0.0)
        df_ref[...] = d_f.astype(BF16)
        dbf_ref[...] += jnp.sum(d_f, axis=0, keepdims=True)

    rev = lambda i: n_tiles - 1 - i
    col_blk = lambda j: pl.BlockSpec((TM, PIECE), lambda i: (rev(i), j))
    head_blk = lambda w: pl.BlockSpec((FOX_HEADS, TM, w), lambda i: (0, rev(i), 0))
    out_piece = pl.BlockSpec((TM, PIECE), lambda i: (rev(i), 0))
    return pl.pallas_call(
        body,
        name="fox_post",
        grid=(n_tiles,),
        in_specs=[col_blk(1), col_blk(2), pl.BlockSpec((TM, 128), lambda i: (rev(i), MY_F_OFF // 128)),
                  _full((1, 128)), _full((1, PIECE)), _full((1, PIECE)), _full((PIECE, PIECE)),
                  head_blk(128), head_blk(128), head_blk(HEAD_DIM)],
        out_specs=[out_piece, out_piece, out_piece, pl.BlockSpec((TM, 128), lambda i: (rev(i), 0)),
                   _full((1, HEAD_DIM)), _full((1, HEAD_DIM)), _full((1, 128))],
        out_shape=[jax.ShapeDtypeStruct((s_len, PIECE), BF16)] * 3 + [
            jax.ShapeDtypeStruct((s_len, 128), BF16), jax.ShapeDtypeStruct((1, HEAD_DIM), F32),
            jax.ShapeDtypeStruct((1, HEAD_DIM), F32), jax.ShapeDtypeStruct((1, 128), F32)],
        scratch_shapes=[pltpu.VMEM((1, 128), F32)],
        compiler_params=_params(("arbitrary",)),
    )(proj, proj, proj, bf_pad, gq, gk, seg, dq_aug, dk_aug, dv_h)


def _mem_bwd(mem, g, w_kv, gk, dkm, dvm):
    def body(mem_ref, g_ref, w_ref, gk_ref, dkm_ref, dvm_ref, dw_ref, dg_ref, dgk_ref, dkv_ref):
        m = mem_ref[...]
        r = _rms(m)
        a = m * r
        mn16 = (a * g_ref[...]).astype(BF16)
        kv = _dot(mn16, w_ref[...])
        dgk = jnp.zeros((N_MEM, HEAD_DIM), F32)
        for h in range(MEM_HEADS):
            sl = slice(HEAD_DIM * h, HEAD_DIM * (h + 1))
            kh = kv[:, sl]
            rk = _rms(kh)
            dk, dg_rows = _rms_bwd(kh * rk, rk, gk_ref[...], dkm_ref[:, sl])
            dkv_ref[:, sl] = dk.astype(BF16)
            dgk = dgk + dg_rows
        dkv_ref[:, 256:512] = dvm_ref[...].astype(BF16)
        dgk_ref[...] = jnp.sum(dgk, axis=0, keepdims=True)
        dkv16 = dkv_ref[...]
        dw_ref[...] = _dot_tn(mn16, dkv16).astype(BF16)
        dg_ref[...] = jnp.sum(_dot_nt(dkv16, w_ref[...]) * a, axis=0, keepdims=True)

    return pl.pallas_call(
        body,
        name="mem_bwd",
        out_shape=(jax.ShapeDtypeStruct((D_MODEL, 512), BF16), jax.ShapeDtypeStruct((1, D_MODEL), F32),
                   jax.ShapeDtypeStruct((1, HEAD_DIM), F32)),
        in_specs=[pl.BlockSpec(memory_space=pltpu.VMEM)] * 6,
        out_specs=(pl.BlockSpec(memory_space=pltpu.VMEM),) * 3,
        scratch_shapes=[pltpu.VMEM((N_MEM, 512), BF16)],
        compiler_params=pltpu.CompilerParams(vmem_limit_bytes=VMEM_LIMIT),
    )(mem, g, w_kv, gk, dkm, dvm)


def _grad_x_exchange(pieces, d_f, w_my, x, norm_g, d_out, dw_in, land_kv, land_out, d_mem_g, d_scale, d_bf, d_gq, d_gk, d_gqm,
                     d_gkm, dwp_bd, loss_blk):
    s_len = x.shape[0]
    n_steps = s_len // TM
    step2, step3 = n_steps // 4, (11 * n_steps) // 16
    half = D_MODEL // 2

    def body(p0, p1, p2, p3, p4, p5, df_ref, x_ref, dout_ref, w_ref, g_ref, in_ref, lkv_ref, lout_ref,
             dmg, dsc, dbf, dgq, dgk, dgqm, dgkm, dwp, loss_ref,
             gx_ref, rin_ref, rkv_ref, rout_ref, sred_ref,
             dng, land1, send2a, send2b, keep2a, keep2b, land2a, land2b, send3a, send3b, land3a, land3b, sb_ref, sland,
             own4, lkv_v, lout_v, send_sems, recv_sems, load_sems):
        i = pl.program_id(0)
        x_, y_, c_, me = _my_place()
        sibling, x_nbr, y_nbr = (x_, y_, 1 - c_), (1 - x_, y_, c_), (x_, 1 - y_, c_)
        loads = [pltpu.make_async_copy(in_ref.at[2 * k + c_], own4.at[k], load_sems.at[k]) for k in range(4)]
        loads += [pltpu.make_async_copy(lkv_ref, lkv_v, load_sems.at[4]), pltpu.make_async_copy(lout_ref, lout_v, load_sems.at[5])]

        def rcopy(src, dst, sem, to):
            return pltpu.make_async_remote_copy(src_ref=src, dst_ref=dst, send_sem=send_sems.at[sem], recv_sem=recv_sems.at[sem],
                                                device_id=to, device_id_type=MESH)

        early_rows, late_rows = pl.ds(8, SB_ROWS - 8), pl.ds(0, 8)
        small_early, small_late = [], []
        for k in range(1, N_DEV):
            peer = (x_ ^ (k >> 2), y_ ^ ((k >> 1) & 1), c_ ^ (k & 1))
            small_early.append(rcopy(sb_ref.at[early_rows], sland.at[me, early_rows], k - 1, peer))
            small_late.append(rcopy(sb_ref.at[late_rows], sland.at[me, late_rows], 16 + k, peer))
        small = small_early + small_late
        stage1 = [rcopy(in_ref.at[2 * k + 1 - c_], land1.at[k], 7 + k, sibling) for k in range(4)]
        stage2 = [rcopy(send2a.at[j], land2a.at[j], 11 + j, x_nbr) for j in range(2)]
        stage2 += [rcopy(send2b.at[j], land2b.at[j], 13 + j, y_nbr) for j in range(2)]
        stage3 = [rcopy(send3a, land3a, 15, y_nbr), rcopy(send3b, land3b, 16, x_nbr)]
        top, bot = slice(0, half), slice(half, D_MODEL)

        @pl.when(i == 0)
        def _():
            dng[...] = jnp.zeros((1, D_MODEL), F32)
            for cp in stage1 + loads:
                cp.start()
            sb_ref[...] = jnp.zeros((SB_ROWS, SB_W), F32)
            sb_ref[SB_POOL_SCALE:SB_POOL_SCALE + 1, :] = dsc[...]
            sb_ref[SB_B_F:SB_B_F + 1, 0:128] = dbf[...]
            for row, ref in ((SB_FOX_Q, dgq), (SB_FOX_K, dgk), (SB_MEM_Q, dgqm), (SB_MEM_K, dgkm)):
                sb_ref[row:row + 1, 0:HEAD_DIM] = ref[...]
            sb_ref[SB_LOSS:SB_LOSS + 1, 0:128] = loss_ref[0:1, :]
            for grp in range(4):
                sl = slice(64 * grp, 64 * (grp + 1))
                sb_ref[SB_W_POOL:SB_W_POOL + 64, sl] = dwp[sl, sl]
            for cp in small_early:
                cp.start()

        @pl.when(i == step2)
        def _():
            for cp in stage1:
                cp.wait_recv()
            for cp in loads[0:4]:
                cp.wait()

            def chip_sum(k, cols):
                return own4[k, :, cols].astype(F32) + land1[k, :, cols].astype(F32)

            for j in range(2):
                send2a[j] = chip_sum(2 * (1 - x_) + j, top).astype(BF16)
                keep2a[j] = chip_sum(2 * x_ + j, top)
                send2b[j] = chip_sum(2 * j + 1 - y_, bot).astype(BF16)
                keep2b[j] = chip_sum(2 * j + y_, bot)
            for cp in stage2:
                cp.start()

        @pl.when(i == step3)
        def _():
            for cp in stage2:
                cp.wait_recv()
            for j in range(2):
                keep2a[j] = keep2a[j] + land2a[j].astype(F32)
                keep2b[j] = keep2b[j] + land2b[j].astype(F32)
            send3a[...] = keep2a[1 - y_].astype(BF16)
            send3b[...] = keep2b[1 - x_].astype(BF16)
            for cp in stage3:
                cp.start()

        dh = _dot_nt(df_ref[...], w_ref[:, MY_F_OFF:MY_WIDTH])
        for j, p in enumerate((p0, p1, p2, p3, p4, p5)):
            dh = dh + _dot_nt(p[...], w_ref[:, PIECE * j:PIECE * (j + 1)])
        xv = x_ref[...]
        r = _rms(xv)
        dx, dg_rows = _rms_bwd(xv * r, r, g_ref[...], dh)
        gx_ref[...] = dout_ref[...] + dx
        dng[...] += jnp.sum(dg_rows, axis=0, keepdims=True)

        @pl.when(i == n_steps - 1)
        def _():
            for k in range(4):
                sb_ref[SB_NORM_G + k:SB_NORM_G + k + 1, :] = dng[:, SB_W * k:SB_W * (k + 1)]
                sb_ref[SB_MEM_NORM_G + k:SB_MEM_NORM_G + k + 1, :] = dmg[:, SB_W * k:SB_W * (k + 1)]
            for cp in small_late:
                cp.start()
            sland[me] = sb_ref[...]
            for cp in stage3:
                cp.wait_recv()
            rin_ref[:, top] = keep2a[y_] + land3a[...].astype(F32)
            rin_ref[:, bot] = keep2b[x_] + land3b[...].astype(F32)
            for cp in small:
                cp.wait_recv()
            for cp in small + stage1 + stage2 + stage3:
                cp.wait_send()
            for cp in loads[4:6]:
                cp.wait()
            for land, red in ((lkv_v, rkv_ref), (lout_v, rout_ref), (sland, sred_ref)):
                acc = land[0].astype(F32)
                for d in range(1, N_DEV):
                    acc = acc + land[d].astype(F32)
                red[...] = acc

    piece = pl.BlockSpec((TM, PIECE), lambda i: (i, 0))
    row = pl.BlockSpec((TM, D_MODEL), lambda i: (i, 0))
    vmem = pl.BlockSpec(memory_space=pltpu.VMEM)
    half_chunk = lambda n, dt: pltpu.VMEM((n, CHUNK_ROWS, half), dt)
    whole = (w_my, norm_g, dw_in, land_kv, land_out, d_mem_g, d_scale, d_bf, d_gq, d_gk, d_gqm, d_gkm, dwp_bd, loss_blk)
    return pl.pallas_call(
        body,
        name="grad_x_exchange",
        grid=(n_steps,),
        in_specs=[piece] * 6 + [pl.BlockSpec((TM, 128), lambda i: (i, 0)), row, row, vmem, vmem]
        + [pl.BlockSpec(memory_space=pl.ANY)] * 3 + [vmem] * (len(whole) - 5),
        out_specs=[row, vmem, vmem, vmem, vmem],
        out_shape=[jax.ShapeDtypeStruct((s_len, D_MODEL), F32), jax.ShapeDtypeStruct((CHUNK_ROWS, D_MODEL), F32),
                   jax.ShapeDtypeStruct((128, 512), F32), jax.ShapeDtypeStruct((128, D_MODEL), F32),
                   jax.ShapeDtypeStruct((SB_ROWS, SB_W), F32)],
        scratch_shapes=[
            pltpu.VMEM((1, D_MODEL), F32),
            pltpu.VMEM((4, CHUNK_ROWS, D_MODEL), BF16),
            half_chunk(2, BF16), half_chunk(2, BF16), half_chunk(2, F32), half_chunk(2, F32), half_chunk(2, BF16), half_chunk(2, BF16),
            pltpu.VMEM((CHUNK_ROWS, half), BF16), pltpu.VMEM((CHUNK_ROWS, half), BF16),
            pltpu.VMEM((CHUNK_ROWS, half), BF16), pltpu.VMEM((CHUNK_ROWS, half), BF16),
            pltpu.VMEM((SB_ROWS, SB_W), F32),
            pltpu.VMEM((N_DEV, SB_ROWS, SB_W), F32),
            pltpu.VMEM((4, CHUNK_ROWS, D_MODEL), BF16),
            pltpu.VMEM((N_DEV, 128, 512), BF16),
            pltpu.VMEM((N_DEV, 128, D_MODEL), BF16),
            pltpu.SemaphoreType.DMA((24,)),
            pltpu.SemaphoreType.DMA((24,)),
            pltpu.SemaphoreType.DMA((6,)),
        ],
        compiler_params=_params(("arbitrary",)),
    )(*pieces, d_f, x, d_out, *whole)


def _grad_w_in(h16, pieces, d_f):
    s_len = h16.shape[0]
    tk = 512

    def body(h_ref, p0, p1, p2, p3, p4, p5, df_ref, out_ref, dw_ref):
        @pl.when(pl.program_id(0) == 0)
        def _():
            dw_ref[...] = jnp.zeros((D_MODEL, MY_WIDTH), F32)

        h = h_ref[...]
        for j, p in enumerate((p0, p1, p2, p3, p4, p5)):
            dw_ref[:, PIECE * j:PIECE * (j + 1)] += _dot_tn(h, p[...])
        dw_ref[:, MY_F_OFF:MY_WIDTH] += _dot_tn(h, df_ref[...])

        @pl.when(pl.program_id(0) == pl.num_programs(0) - 1)
        def _():
            for d in range(N_DEV):
                parts = [dw_ref[:, start:start + width] for start, width in _chunk_segments(d)]
                parts.append(jnp.zeros((D_MODEL, 512 - IN_CHUNK), F32))
                out_ref[d] = jnp.concatenate(parts, axis=1).T[0:CHUNK_ROWS].astype(BF16)

    piece = pl.BlockSpec((tk, PIECE), lambda i: (i, 0))
    return pl.pallas_call(
        body,
        name="grad_w_in",
        grid=(s_len // tk,),
        in_specs=[pl.BlockSpec((tk, D_MODEL), lambda i: (i, 0))] + [piece] * 6 + [pl.BlockSpec((tk, 128), lambda i: (i, 0))],
        out_specs=_full((N_DEV, CHUNK_ROWS, D_MODEL)),
        out_shape=jax.ShapeDtypeStruct((N_DEV, CHUNK_ROWS, D_MODEL), BF16),
        scratch_shapes=[pltpu.VMEM((D_MODEL, MY_WIDTH), F32)],
        compiler_params=_params(("arbitrary",)),
    )(h16, *pieces, d_f)


def _adamw(w, g, m, v):
    m = ADAM_B1 * m + (1.0 - ADAM_B1) * g
    v = ADAM_B2 * v + (1.0 - ADAM_B2) * (g * g)
    m_hat = m / (1.0 - ADAM_B1 ** ADAM_STEP)
    v_hat = v / (1.0 - ADAM_B2 ** ADAM_STEP)
    return -ADAM_LR * (m_hat / (jnp.sqrt(v_hat) + ADAM_EPS) + ADAM_WD * w), m, v


PARAM_ORDER = ("norm_g", "w_in", "b_f", "w_pool", "pool_scale", "fox_q_g", "fox_k_g", "mem_norm_g", "w_mem_kv", "mem_q_g", "mem_k_g", "w_out")


def _update(red_in, red_kv, red_out, sred, params):
    def body(rin_ref, rkv_ref, rout_ref, sred_ref, *refs):
        ins, outs = refs[:3 * len(PARAM_ORDER)], refs[3 * len(PARAM_ORDER):]
        wide = lambda row: jnp.concatenate([sred_ref[row + k:row + k + 1, :] for k in range(4)], axis=1)
        head = lambda row: sred_ref[row:row + 1, 0:HEAD_DIM]
        grads = {
            "norm_g": lambda: wide(SB_NORM_G), "w_in": lambda: rin_ref[...], "b_f": lambda: sred_ref[SB_B_F:SB_B_F + 1, 0:FOX_HEADS],
            "pool_scale": lambda: sred_ref[SB_POOL_SCALE:SB_POOL_SCALE + 1, :], "fox_q_g": lambda: head(SB_FOX_Q),
            "fox_k_g": lambda: head(SB_FOX_K), "mem_norm_g": lambda: wide(SB_MEM_NORM_G), "w_mem_kv": lambda: rkv_ref[...],
            "mem_q_g": lambda: head(SB_MEM_Q), "mem_k_g": lambda: head(SB_MEM_K), "w_out": lambda: rout_ref[...],
        }
        for p, name in enumerate(PARAM_ORDER):
            w_ref, m_ref, v_ref = ins[3 * p:3 * p + 3]
            g_out, d_out, m_out, v_out = outs[4 * p:4 * p + 4]
            if name == "w_pool":
                for grp in range(4):
                    g = sred_ref[SB_W_POOL:SB_W_POOL + 64, 64 * grp:64 * (grp + 1)]
                    delta, m_new, v_new = _adamw(w_ref[grp], g, m_ref[grp], v_ref[grp])
                    g_out[grp], d_out[grp], m_out[grp], v_out[grp] = g, delta, m_new, v_new
            elif name == "w_in":
                for j in range(8):
                    rows = pl.ds(j, IN_CHUNK, stride=8)
                    g = rin_ref[0:IN_CHUNK, 128 * j:128 * (j + 1)]
                    delta, m_new, v_new = _adamw(w_ref[rows, :], g, m_ref[rows, :], v_ref[rows, :])
                    g_out[rows, :], d_out[rows, :], m_out[rows, :], v_out[rows, :] = g, delta, m_new, v_new
            else:
                g = grads[name]()
                delta, m_new, v_new = _adamw(w_ref[...], g, m_ref[...], v_ref[...])
                g_out[...], d_out[...], m_out[...], v_out[...] = g, delta, m_new, v_new

    flat = [t for name in PARAM_ORDER for t in params[name]]
    shapes = [params[name][0].shape for name in PARAM_ORDER for _ in range(4)]
    outs = pl.pallas_call(
        body,
        name="adamw_update",
        out_shape=tuple(jax.ShapeDtypeStruct(s, F32) for s in shapes),
        in_specs=[pl.BlockSpec(memory_space=pltpu.VMEM)] * (4 + len(flat)),
        out_specs=(pl.BlockSpec(memory_space=pltpu.VMEM),) * len(shapes),
        compiler_params=pltpu.CompilerParams(vmem_limit_bytes=VMEM_LIMIT),
    )(red_in, red_kv, red_out, sred, *flat)
    return {name: outs[4 * p:4 * p + 4] for p, name in enumerate(PARAM_ORDER)}


def kernel(x, mem, norm_g, w_in, b_f, w_pool, pool_scale, fox_q_g, fox_k_g, mem_norm_g, w_mem_kv, mem_q_g, mem_k_g, w_out, loss_target, m_norm_g, m_w_in, m_b_f, m_w_pool, m_pool_scale, m_fox_q_g, m_fox_k_g, m_mem_norm_g, m_w_mem_kv, m_mem_q_g, m_mem_k_g, m_w_out, v_norm_g, v_w_in, v_b_f, v_w_pool, v_pool_scale, v_fox_q_g, v_fox_k_g, v_mem_norm_g, v_w_mem_kv, v_mem_q_g, v_mem_k_g, v_w_out):
    given = dict(norm_g=(norm_g, m_norm_g, v_norm_g), w_in=(w_in, m_w_in, v_w_in), b_f=(b_f, m_b_f, v_b_f),
                 w_pool=(w_pool, m_w_pool, v_w_pool), pool_scale=(pool_scale, m_pool_scale, v_pool_scale),
                 fox_q_g=(fox_q_g, m_fox_q_g, v_fox_q_g), fox_k_g=(fox_k_g, m_fox_k_g, v_fox_k_g),
                 mem_norm_g=(mem_norm_g, m_mem_norm_g, v_mem_norm_g), w_mem_kv=(w_mem_kv, m_w_mem_kv, v_w_mem_kv),
                 mem_q_g=(mem_q_g, m_mem_q_g, v_mem_q_g), mem_k_g=(mem_k_g, m_mem_k_g, v_mem_k_g), w_out=(w_out, m_w_out, v_w_out))
    params = {name: tuple(t[0] if t.ndim > 2 else t for t in wmv) for name, wmv in given.items()}
    params["w_in"] = tuple(t.T.reshape(IN_CHUNK * 8, 128) for t in params["w_in"])
    x2, mem2, target2 = x[0], mem[0], loss_target[0]

    seg = jnp.asarray(np.kron(np.eye(FOX_HEADS), np.full((HEAD_DIM, HEAD_DIM), 1.0 / HEAD_DIM)), BF16)
    w_my, w_kv16, w_out16, wp_bd, bf_pad, gq8, gk8, gqm4 = _gather_w_in(
        params["w_in"][0], params["w_mem_kv"][0], params["w_out"][0], params["w_pool"][0], b_f, fox_q_g, fox_k_g, mem_q_g)
    proj, h16, q_aug, k_aug, v_h, kt_aug, vt_aug = _proj_prep(x2, norm_g, w_my, bf_pad, gq8, gk8, seg)
    olse, lse_rows, w_kv_all, w_out_all = _fox_fwd(q_aug, k_aug, vt_aug, w_kv16, w_out16)
    km, vm = _mem_prep(mem2, mem_norm_g, w_kv_all, mem_k_g)
    d_out, d_mixed, dw_out, loss_blk = _mix_out_loss(proj, olse, km, vm, wp_bd, pool_scale, gqm4, seg, x2, target2, w_out_all)

    d_ua, d_gb, d_qm, d_o, delta_rows, dwp_bd, d_scale, dkm, dvm, d_gqm = _mix_bwd(proj, olse, d_mixed, km, vm, wp_bd, pool_scale,
                                                                                    gqm4, seg)
    dw_kv, d_mem_g, d_gkm = _mem_bwd(mem2, mem_norm_g, w_kv_all, mem_k_g, dkm, dvm)
    dq_aug, dk_aug, dv_h, land_kv, land_out = _fox_bwd(q_aug, k_aug, kt_aug, v_h, d_o, lse_rows, delta_rows, dw_kv, dw_out)
    d_q, d_k, d_v, d_f, d_gq, d_gk, d_bf = _fox_post(proj, bf_pad, gq8, gk8, seg, dq_aug, dk_aug, dv_h)
    pieces = (d_ua, d_q, d_k, d_v, d_gb, d_qm)
    dw_in = _grad_w_in(h16, pieces, d_f)
    grad_x, red_in, red_kv, red_out, sred = _grad_x_exchange(pieces, d_f, w_my, x2, norm_g, d_out, dw_in, land_kv, land_out, d_mem_g,
                                                             d_scale, d_bf, d_gq, d_gk, d_gqm, d_gkm, dwp_bd, loss_blk)
    new = _update(red_in, red_kv, red_out, sred, params)
    result = [sred[SB_LOSS, 0], grad_x[None]]
    for kind in range(4):
        for name in PARAM_ORDER:
            out = new[name][kind]
            if name == "w_in":
                out = out.reshape(IN_CHUNK, D_MODEL).T
            result.append(out[None] if given[name][0].ndim > 2 else out)
    return tuple(result)
```

```python
import functools

import jax
import jax.numpy as jnp
import numpy as np
from jax import lax
from jax.experimental import pallas as pl
from jax.experimental.pallas import tpu as pltpu

F32 = jnp.float32
BF16 = jnp.bfloat16
MESH = pl.DeviceIdType.MESH

N_DEV = 8
D_MODEL = 1024
HEAD_DIM = 64
FOX_HEADS = 8
MEM_HEADS = 4
N_MEM = 256
POOL_WIDTH = 256
EPS = 1e-6
IN_WIDTH = 3080
IN_CHUNK = IN_WIDTH // N_DEV
CHUNK_ROWS = 400
F_OFF = 2048
MY_WIDTH = 3200
MY_F_OFF = 3072
PIECE = 512
TM = 256
TMM = 512
TA = 256
HB = 8
HB_FWD = 8
AUG_ROWS = 72
HALO = 16
VMEM_LIMIT = 56 * 1024 * 1024

ADAM_LR = 0.001
ADAM_B1 = 0.9
ADAM_B2 = 0.999
ADAM_EPS = 1e-08
ADAM_WD = 0.01
ADAM_STEP = 10


def _dot(a, b):
    return jnp.dot(a, b, preferred_element_type=F32)


def _dot_nt(a, b):
    return lax.dot_general(a, b, (((1,), (1,)), ((), ())), preferred_element_type=F32)


def _dot_tn(a, b):
    return lax.dot_general(a, b, (((0,), (0,)), ((), ())), preferred_element_type=F32)


def _sigmoid(g):
    return 0.5 + 0.5 * jnp.tanh(0.5 * g)


def _split3(v):
    hi = v.astype(BF16)
    r1 = v - hi.astype(F32)
    mid = r1.astype(BF16)
    lo = (r1 - mid.astype(F32)).astype(BF16)
    return hi, mid, lo


def _rms(v):
    return lax.rsqrt(jnp.mean(v * v, axis=-1, keepdims=True) + EPS)


def _rms_bwd(a, r, g, dy):
    da = dy * g
    return r * (da - a * jnp.mean(da * a, axis=-1, keepdims=True)), dy * a


def _head_mean(z, seg):
    hi = z.astype(BF16)
    lo = (z - hi.astype(F32)).astype(BF16)
    if z.shape[1] == 256:
        return _dot(hi, seg) + _dot(lo, seg)
    half = seg[0:256, 0:256]
    return jnp.concatenate([_dot(hi[:, 0:256], half) + _dot(lo[:, 0:256], half),
                            _dot(hi[:, 256:512], half) + _dot(lo[:, 256:512], half)], axis=1)


def _head_rms(v, seg):
    return lax.rsqrt(_head_mean(v * v, seg) + EPS)


def _head_rms_bwd(a, r, g, dy, seg):
    da = dy * g
    return r * (da - a * _head_mean(da * a, seg)), dy * a


def _fold_heads(row, n_heads):
    out = row[:, 0:HEAD_DIM]
    for h in range(1, n_heads):
        out = out + row[:, HEAD_DIM * h:HEAD_DIM * (h + 1)]
    return out


def _params(sem, limit=VMEM_LIMIT):
    return pltpu.CompilerParams(dimension_semantics=sem, vmem_limit_bytes=limit)


def _full(shape):
    return pl.BlockSpec(shape, lambda *_: (0,) * len(shape))


def _chunk_segments(d):
    runs = []
    for lo, hi, shift in ((0, F_OFF, 0), (F_OFF, F_OFF + FOX_HEADS, MY_F_OFF - F_OFF), (F_OFF + FOX_HEADS, IN_WIDTH, -FOX_HEADS)):
        a, b = max(lo, IN_CHUNK * d), min(hi, IN_CHUNK * (d + 1))
        if a < b:
            runs.append((a + shift, b - a))
    return runs


def _my_place():
    x, y, c = lax.axis_index("x"), lax.axis_index("y"), lax.axis_index("c")
    return x, y, c, 4 * x + 2 * y + c


def _shard_rows(dev):
    return pl.ds(pl.multiple_of(128 * dev, 128), 128)


def _gather_w_in(w_in, w_kv, w_out, w_pool, b_f, gq, gk, gqm):
    def body(win_ref, wkv_ref, wout_ref, wp_ref, bf_ref, gq_ref, gk_ref, gqm_ref, wmy_ref, kv16_ref, out16_ref, wpbd_ref, bfpad_ref,
             gq8_ref, gk8_ref, gqm4_ref, in_all, pay_in, win_rows, send_sems, recv_sems, load_sem):
        x, y, c, me = _my_place()
        here, sibling = (x, y, c), (x, y, 1 - c)
        x_chip, y_chip, far_chip = (1 - x, y), (x, 1 - y), (1 - x, 1 - y)
        relay_from = (x ^ (1 - c), y ^ c)
        relay_to = (x ^ c, y ^ (1 - c))

        load = pltpu.make_async_copy(win_ref, win_rows, load_sem)
        load.start()
        load.wait()
        pay_in[...] = jnp.zeros((CHUNK_ROWS, D_MODEL), BF16)
        for j in range(8):
            pay_in[0:IN_CHUNK, 128 * j:128 * (j + 1)] = win_rows[pl.ds(j, IN_CHUNK, stride=8), :].astype(BF16)

        def copy(k, block, to, own=False):
            px, py, pc = block
            dst = in_all.at[4 * px + 2 * py + pc]
            return pltpu.make_async_remote_copy(src_ref=pay_in if own else dst, dst_ref=dst, send_sem=send_sems.at[k],
                                                recv_sem=recv_sems.at[k], device_id=to, device_id_type=MESH)

        first = [copy(0, here, sibling, own=True), copy(1, here, (*x_chip, c), own=True), copy(2, here, (*y_chip, c), own=True)]
        for cp in first:
            cp.start()
        in_all[me] = pay_in[...]
        kv16_ref[...] = wkv_ref[...].astype(BF16)
        out16_ref[...] = wout_ref[...].astype(BF16)
        wpbd_ref[...] = jnp.zeros((POOL_WIDTH, POOL_WIDTH), BF16)
        for grp in range(4):
            sl = slice(64 * grp, 64 * (grp + 1))
            wpbd_ref[sl, sl] = wp_ref[grp].astype(BF16)
        bfpad_ref[...] = jnp.zeros((1, 128), F32)
        bfpad_ref[:, 0:FOX_HEADS] = bf_ref[...]
        gq8_ref[...] = jnp.concatenate([gq_ref[...]] * FOX_HEADS, axis=1)
        gk8_ref[...] = jnp.concatenate([gk_ref[...]] * FOX_HEADS, axis=1)
        gqm4_ref[...] = jnp.concatenate([gqm_ref[...]] * MEM_HEADS, axis=1)
        copy(1 + c, (*relay_from, c), here).wait_recv()
        passed = [copy(3, (*relay_from, c), (*relay_to, c)), copy(4, (*relay_from, c), sibling)]
        for cp in passed:
            cp.start()
        copy(2 - c, (*relay_to, c), here).wait_recv()
        passed.append(copy(5, (*relay_to, c), sibling))
        passed[-1].start()
        copy(3, (*far_chip, c), here).wait_recv()
        passed.append(copy(6, (*far_chip, c), sibling))
        passed[-1].start()
        copy(0, sibling, here).wait_recv()
        for k, chip in ((4, relay_to), (5, relay_from), (6, far_chip)):
            copy(k, (*chip, 1 - c), here).wait_recv()
        for cp in first + passed:
            cp.wait_send()

        row_pad = jnp.zeros((512 - CHUNK_ROWS, 256), F32)
        for r0 in range(0, D_MODEL, 256):
            ch = [jnp.concatenate([in_all[d, :, r0:r0 + 256].astype(F32), row_pad], axis=0).T[:, 0:IN_CHUNK] for d in range(N_DEV)]
            lo = F_OFF - 5 * IN_CHUNK
            full = jnp.concatenate(ch[:5] + [ch[5][:, :lo], ch[5][:, lo + FOX_HEADS:], ch[6], ch[7], ch[5][:, lo:lo + FOX_HEADS],
                                             jnp.zeros((256, MY_WIDTH - IN_WIDTH), F32)], axis=1)
            wmy_ref[r0:r0 + 256, :] = full.astype(BF16)

    return pl.pallas_call(
        body,
        name="gather_w_in",
        out_shape=(jax.ShapeDtypeStruct((D_MODEL, MY_WIDTH), BF16), jax.ShapeDtypeStruct((128, 512), BF16),
                   jax.ShapeDtypeStruct((128, D_MODEL), BF16), jax.ShapeDtypeStruct((POOL_WIDTH, POOL_WIDTH), BF16),
                   jax.ShapeDtypeStruct((1, 128), F32), jax.ShapeDtypeStruct((1, PIECE), F32), jax.ShapeDtypeStruct((1, PIECE), F32),
                   jax.ShapeDtypeStruct((1, 256), F32)),
        in_specs=[pl.BlockSpec(memory_space=pl.ANY)] + [pl.BlockSpec(memory_space=pltpu.VMEM)] * 7,
        out_specs=(pl.BlockSpec(memory_space=pltpu.VMEM),) * 8,
        scratch_shapes=[
            pltpu.VMEM((N_DEV, CHUNK_ROWS, D_MODEL), BF16),
            pltpu.VMEM((CHUNK_ROWS, D_MODEL), BF16),
            pltpu.VMEM((IN_CHUNK * 8, 128), F32),
            pltpu.SemaphoreType.DMA((7,)),
            pltpu.SemaphoreType.DMA((7,)),
            pltpu.SemaphoreType.DMA,
        ],
        compiler_params=pltpu.CompilerParams(vmem_limit_bytes=VMEM_LIMIT),
    )(w_in, w_kv, w_out, w_pool, b_f, gq, gk, gqm)


def _row_block_exchange(kv_ref, out_ref, kv_dst, out_dst, send_sems, recv_sems, local_sems, gather):
    x, y, c, me = _my_place()
    remote = []
    for k in range(1, N_DEV):
        px, py, pc = x ^ (k >> 2), y ^ ((k >> 1) & 1), c ^ (k & 1)
        peer = 4 * px + 2 * py + pc
        for fam, (src, dst) in enumerate(((kv_ref, kv_dst), (out_ref, out_dst))):
            remote.append(pltpu.make_async_remote_copy(
                src_ref=src if gather else src.at[_shard_rows(peer)], dst_ref=dst.at[_shard_rows(me)] if gather else dst.at[me],
                send_sem=send_sems.at[7 * fam + k - 1], recv_sem=recv_sems.at[7 * fam + k - 1],
                device_id=(px, py, pc), device_id_type=MESH))
    local = [pltpu.make_async_copy(src if gather else src.at[_shard_rows(me)], dst.at[_shard_rows(me)] if gather else dst.at[me],
                                   local_sems.at[fam]) for fam, (src, dst) in enumerate(((kv_ref, kv_dst), (out_ref, out_dst)))]
    return remote, local


SB_ROWS, SB_W = 80, 256
SB_NORM_G, SB_MEM_NORM_G, SB_POOL_SCALE, SB_B_F, SB_FOX_Q, SB_FOX_K, SB_MEM_Q, SB_MEM_K, SB_LOSS, SB_W_POOL = 0, 4, 8, 9, 10, 11, 12, 13, 14, 16


def _alternate(*parts):
    state = [[part, 0, stages] for part, stages in parts]
    while state:
        least = min(state, key=lambda entry: entry[1] / entry[2])
        least[1] += 1
        if next(least[0], "done") == "done":
            state.remove(least)


def _log_sigmoid(z):
    return jnp.minimum(z, 0.0) - jnp.log(1.0 + jnp.exp(-jnp.abs(z)))


def _forget_lane_maps():
    to_q = np.zeros((128, FOX_HEADS * 128), np.float32)
    to_k = np.zeros((128, FOX_HEADS * 128), np.float32)
    for h in range(FOX_HEADS):
        for term in range(3):
            to_q[8 * term + h, 128 * h + 64 + term] = 1.0
            to_q[24, 128 * h + 67 + term] = 1.0
            to_k[24, 128 * h + 64 + term] = 1.0
            to_k[8 * term + h, 128 * h + 67 + term] = -1.0
    return jnp.asarray(to_q, BF16), jnp.asarray(to_k, BF16)


def _proj_prep(x, norm_g, w_my, bf_pad, gq, gk, seg):
    s_len = x.shape[0]
    n_blocks = s_len // TMM
    halves = TMM // TM
    to_q, to_k = _forget_lane_maps()
    q_off, f_off = PIECE, MY_F_OFF
    kept = 3 * PIECE + 128

    def body(x_ref, g_ref, w_ref, bf_ref, gq_ref, gk_ref, seg_ref, eq_ref, ek_ref,
             proj_ref, h_ref, qa_ref, ka_ref, vh_ref, kt_ref, vt_ref, kept_even, kept_odd, carry_ref):
        s = pl.program_id(0)

        def project(kept_ref):
            xv = x_ref[...]
            h_ref[...] = (xv * _rms(xv) * g_ref[...]).astype(BF16)
            yield
            for c0 in range(0, MY_WIDTH, 256):
                c1 = min(c0 + 256, MY_WIDTH)
                res = _dot(h_ref[...], w_ref[:, c0:c1])
                proj_ref[:, c0:c1] = res
                if q_off <= c0 < q_off + 3 * PIECE:
                    kept_ref[:, c0 - q_off:c1 - q_off] = res
                if c0 == f_off:
                    kept_ref[:, 3 * PIECE:kept] = res
                yield

        def operands(kept_ref):
            lane = lax.broadcasted_iota(jnp.int32, (TM, 128), 1)
            row = lax.broadcasted_iota(jnp.int32, (TM, TM), 0)
            col = lax.broadcasted_iota(jnp.int32, (TM, TM), 1)
            tri = jnp.where(col <= row, 1.0, 0.0).astype(BF16)
            one_at_64 = jnp.where(lane == 64, 1.0, 0.0)
            zeros = jnp.zeros((TM, HEAD_DIM), F32)
            for half in range(halves):
                rows = slice(TM * half, TM * (half + 1))
                logf = jnp.where(lane < FOX_HEADS, _log_sigmoid(kept_ref[rows, 3 * PIECE:kept] + bf_ref[...]), 0.0)
                hi, mid, lo = _split3(logf)
                cum = _dot(tri, hi) + _dot(tri, mid) + _dot(tri, lo) + carry_ref[...]
                carry_ref[...] = cum[TM - 1:TM, :]
                f_hi, f_mid, f_lo = [t.astype(F32) for t in _split3(cum)]
                packed = (f_hi + pltpu.roll(f_mid, 8, 1) + pltpu.roll(f_lo, 16, 1)
                          + jnp.where(lane == 24, 1.0, 0.0)).astype(BF16)
                extra_q = _dot(packed, eq_ref[...])
                extra_k = _dot(packed, ek_ref[...])
                q_all = kept_ref[rows, 0:PIECE]
                k_all = kept_ref[rows, PIECE:2 * PIECE]
                qn_all = q_all * _head_rms(q_all, seg_ref[...]) * gq_ref[...] * 0.125
                kn_all = k_all * _head_rms(k_all, seg_ref[...]) * gk_ref[...]
                yield
                for h in range(FOX_HEADS):
                    sl = slice(HEAD_DIM * h, HEAD_DIM * (h + 1))
                    tile = slice(128 * h, 128 * (h + 1))
                    qa = jnp.where(lane < 64, jnp.concatenate([qn_all[:, sl], zeros], axis=1), extra_q[:, tile])
                    ka = jnp.where(lane < 64, jnp.concatenate([kn_all[:, sl], zeros], axis=1), extra_k[:, tile])
                    vh = kept_ref[rows, 2 * PIECE + HEAD_DIM * h:2 * PIECE + HEAD_DIM * (h + 1)]
                    v_aug = jnp.where(lane < 64, jnp.concatenate([vh, zeros], axis=1), one_at_64)
                    qa_ref[h, rows, :] = qa.astype(BF16)
                    ka_ref[h, rows, :] = ka.astype(BF16)
                    vh_ref[h, rows, :] = vh.astype(BF16)
                    kt_ref[h, half] = ka.T.astype(BF16)
                    vt_ref[h, half] = v_aug.T.astype(BF16)
                    if h % 2 == 1:
                        yield

        projecting = lambda kept_ref: (project(kept_ref), 1 + pl.cdiv(MY_WIDTH, 256))
        making = lambda kept_ref: (operands(kept_ref), halves * (1 + FOX_HEADS // 2))

        @pl.when(s == 0)
        def _():
            carry_ref[...] = jnp.zeros((1, 128), F32)
            _alternate(projecting(kept_even))

        @pl.when((s > 0) & (s < n_blocks) & (s % 2 == 1))
        def _():
            _alternate(projecting(kept_odd), making(kept_even))

        @pl.when((s > 0) & (s < n_blocks) & (s % 2 == 0))
        def _():
            _alternate(projecting(kept_even), making(kept_odd))

        @pl.when(s == n_blocks)
        def _():
            _alternate(making(kept_odd if n_blocks % 2 == 0 else kept_even))

    made = lambda s: jnp.minimum(s, n_blocks - 1)
    used = lambda s: jnp.maximum(s - 1, 0)
    head_blk = lambda w: pl.BlockSpec((FOX_HEADS, TMM, w), lambda s: (0, used(s), 0))
    tile_blk = pl.BlockSpec((FOX_HEADS, halves, 128, TM), lambda s: (0, used(s), 0, 0))
    tiled = jax.ShapeDtypeStruct((FOX_HEADS, s_len // TM, 128, TM), BF16)
    return pl.pallas_call(
        body,
        name="proj_prep",
        grid=(n_blocks + 1,),
        in_specs=[pl.BlockSpec((TMM, D_MODEL), lambda s: (made(s), 0)), _full((1, D_MODEL)), _full((D_MODEL, MY_WIDTH)),
                  _full((1, 128)), _full((1, PIECE)), _full((1, PIECE)), _full((PIECE, PIECE)),
                  _full((128, FOX_HEADS * 128)), _full((128, FOX_HEADS * 128))],
        out_specs=[pl.BlockSpec((TMM, MY_WIDTH), lambda s: (made(s), 0)), pl.BlockSpec((TMM, D_MODEL), lambda s: (made(s), 0)),
                   head_blk(128), head_blk(128), head_blk(HEAD_DIM), tile_blk, tile_blk],
        out_shape=[jax.ShapeDtypeStruct((s_len, MY_WIDTH), F32), jax.ShapeDtypeStruct((s_len, D_MODEL), BF16),
                   jax.ShapeDtypeStruct((FOX_HEADS, s_len, 128), BF16), jax.ShapeDtypeStruct((FOX_HEADS, s_len, 128), BF16),
                   jax.ShapeDtypeStruct((FOX_HEADS, s_len, HEAD_DIM), BF16), tiled, tiled],
        scratch_shapes=[pltpu.VMEM((TMM, kept), F32), pltpu.VMEM((TMM, kept), F32), pltpu.VMEM((1, 128), F32)],
        compiler_params=_params(("arbitrary",)),
    )(x, norm_g, w_my, bf_pad, gq, gk, seg, to_q, to_k)


def _mem_prep(mem, g, w_kv, gk):
    def body(mem_ref, g_ref, w_ref, gk_ref, km_ref, vm_ref):
        m = mem_ref[...]
        kv = _dot((m * _rms(m) * g_ref[...]).astype(BF16), w_ref[...])
        for h in range(MEM_HEADS):
            sl = slice(HEAD_DIM * h, HEAD_DIM * (h + 1))
            kh = kv[:, sl]
            km_ref[:, sl] = (kh * _rms(kh) * gk_ref[...]).astype(BF16)
        vm_ref[...] = kv[:, 256:512].astype(BF16)

    return pl.pallas_call(
        body,
        name="mem_prep",
        out_shape=(jax.ShapeDtypeStruct((N_MEM, 256), BF16),) * 2,
        in_specs=[pl.BlockSpec(memory_space=pltpu.VMEM)] * 4,
        out_specs=(pl.BlockSpec(memory_space=pltpu.VMEM),) * 2,
        compiler_params=pltpu.CompilerParams(vmem_limit_bytes=VMEM_LIMIT),
    )(mem, g, w_kv, gk)


def _causal_mask_t():
    row = lax.broadcasted_iota(jnp.int32, (TA, TA), 0)
    col = lax.broadcasted_iota(jnp.int32, (TA, TA), 1)
    return row <= col


def _fox_fwd(q_aug, k_aug, vt_aug, w_kv16, w_out16):
    s_len = q_aug.shape[1]
    n_tiles = s_len // TA
    hb = HB_FWD
    n_groups = FOX_HEADS // hb

    def body(q_ref, k_new, vt_new, kv16_ref, out16_ref, o_ref, st_ref, kv_all, out_all, k_ref, vt_ref, send_sems, recv_sems, local_sems):
        qi = pl.program_id(1)
        for h in range(hb):
            k_ref[h, pl.ds(pl.multiple_of(qi * TA, TA), TA), :] = k_new[h]
            vt_ref[h, qi] = vt_new[h, 0]
        remote, local = _row_block_exchange(kv16_ref, out16_ref, kv_all, out_all, send_sems, recv_sems, local_sems, gather=True)

        @pl.when((pl.program_id(0) == 0) & (qi == 0))
        def _():
            for cp in remote + local:
                cp.start()

        qs = [q_ref[h] for h in range(hb)]

        def step(t0, carry, masked, width):
            rows = pl.ds(pl.multiple_of(t0 * TA, TA), width * TA)
            scores = [_dot_nt(k_ref[h, rows, :], qs[h]) for h in range(hb)]
            soft = []
            for h in range(hb):
                m, _ = carry[h]
                s = jnp.where(_causal_mask_t(), scores[h], -1e30) if masked else scores[h]
                m_new = jnp.maximum(m, jnp.max(s, axis=0, keepdims=True))
                soft.append((m_new, jnp.exp(m - m_new), jnp.exp(s - m_new).astype(BF16)))
            out = []
            for h, (m_new, alpha, p) in enumerate(soft):
                acc = alpha * carry[h][1]
                for t in range(width):
                    acc = acc + _dot(vt_ref[h, t0 + t, 0:AUG_ROWS, :], p[t * TA:(t + 1) * TA])
                out.append((m_new, acc))
            return tuple(out)

        init = tuple((jnp.full((1, TA), -1e30, F32), jnp.zeros((AUG_ROWS, TA), F32)) for _ in range(hb))
        carry = lax.fori_loop(0, qi // 2, lambda j, cr: step(2 * j, cr, False, 2), init)
        carry = lax.fori_loop(2 * (qi // 2), qi, lambda j, cr: step(j, cr, False, 1), carry)
        carry = step(qi, carry, True, 1)
        for h in range(hb):
            m, acc = carry[h]
            l = acc[HEAD_DIM:HEAD_DIM + 1, :]
            lse = m + jnp.log(l)
            o_ref[h] = jnp.concatenate([acc[0:HEAD_DIM] / l, jnp.broadcast_to(lse, (HEAD_DIM, TA))], axis=0).T
            st_ref[h, 0] = jnp.broadcast_to(lse, (8, TA))

        @pl.when((pl.program_id(0) == n_groups - 1) & (qi == n_tiles - 1))
        def _():
            for cp in remote:
                cp.wait_recv()
            for cp in remote:
                cp.wait_send()
            for cp in local:
                cp.wait()

    hbm = pl.BlockSpec(memory_space=pl.ANY)
    return pl.pallas_call(
        body,
        name="fox_fwd",
        grid=(n_groups, n_tiles),
        in_specs=[pl.BlockSpec((hb, TA, 128), lambda g, i: (g, i, 0)), pl.BlockSpec((hb, TA, 128), lambda g, i: (g, i, 0)),
                  pl.BlockSpec((hb, 1, 128, TA), lambda g, i: (g, i, 0, 0)), hbm, hbm],
        out_specs=[pl.BlockSpec((hb, TA, 128), lambda g, i: (g, i, 0)), pl.BlockSpec((hb, 1, 8, TA), lambda g, i: (g, i, 0, 0)),
                   hbm, hbm],
        out_shape=[jax.ShapeDtypeStruct((FOX_HEADS, s_len, 128), F32), jax.ShapeDtypeStruct((FOX_HEADS, n_tiles, 8, TA), F32),
                   jax.ShapeDtypeStruct((D_MODEL, 512), BF16), jax.ShapeDtypeStruct((D_MODEL, D_MODEL), BF16)],
        scratch_shapes=[pltpu.VMEM((hb, s_len, 128), BF16), pltpu.VMEM((hb, n_tiles, 128, TA), BF16),
                        pltpu.SemaphoreType.DMA((14,)), pltpu.SemaphoreType.DMA((14,)), pltpu.SemaphoreType.DMA((2,))],
        compiler_params=_params(("arbitrary", "arbitrary")),
    )(q_aug, k_aug, vt_aug, w_kv16, w_out16)


def _lane_group(lane, a, b, c, d):
    return jnp.where(lane < 64, a, jnp.where(lane < 128, b, jnp.where(lane < 192, c, d)))


def _pool_count(row0):
    lane = lax.broadcasted_iota(jnp.int32, (TM, POOL_WIDTH), 1)
    t1 = row0 + lax.broadcasted_iota(jnp.int32, (TM, POOL_WIDTH), 0) + 1
    return 1.0 / jnp.minimum(t1, _lane_group(lane, 2, 4, 8, 16)).astype(F32), lane


def _pool_delta(u, halo, cnt, lane):
    xe = jnp.concatenate([halo, u], axis=0)
    s2 = xe + pltpu.roll(xe, 1, 0)
    s4 = s2 + pltpu.roll(s2, 2, 0)
    s8 = s4 + pltpu.roll(s4, 4, 0)
    s16 = s8 + pltpu.roll(s8, 8, 0)
    win = _lane_group(lane, s2[HALO:], s4[HALO:], s8[HALO:], s16[HALO:])
    return win * cnt - u


def _pool_delta_bwd(dd, dd_next, cnt, cnt_next, lane):
    ee = jnp.concatenate([dd * cnt, dd_next * cnt_next], axis=0)
    n = TM + HALO
    r2 = ee + pltpu.roll(ee, n - 1, 0)
    r4 = r2 + pltpu.roll(r2, n - 2, 0)
    r8 = r4 + pltpu.roll(r4, n - 4, 0)
    r16 = r8 + pltpu.roll(r8, n - 8, 0)
    return _lane_group(lane, r2[:TM], r4[:TM], r8[:TM], r16[:TM]) - dd


def _mem_attn(qm, gq, seg, km_ref, vm_ref):
    r = _head_rms(qm, seg)
    a = qm * r
    q16 = (a * gq * 0.125).astype(BF16)
    heads = []
    for h in range(MEM_HEADS):
        sl = slice(HEAD_DIM * h, HEAD_DIM * (h + 1))
        s = _dot_nt(q16[:, sl], km_ref[:, sl])
        e = jnp.exp(s - jnp.max(s, axis=-1, keepdims=True))
        p = e * (1.0 / jnp.sum(e, axis=-1, keepdims=True))
        heads.append((p, _dot(p.astype(BF16), vm_ref[:, sl])))
    return a, r, q16, heads


def _mix_out_loss(proj, olse, km, vm, wp_bd, pool_scale, gq_m, seg, x, target, w_out):
    s_len = x.shape[0]
    n_blocks = s_len // TMM
    halves = TMM // TM

    def body(ua_ref, halo_ref, gb_ref, qm_ref, ol_ref, km_ref, vm_ref, wp_ref, sc_ref, gq_ref, seg_ref, x_ref, t_ref, w_ref,
             dout_ref, dmix_ref, dw16_ref, loss_ref, mixed_even, mixed_odd, d16_ref, dw_ref):
        s = pl.program_id(0)
        chunks = [slice(c0, c0 + 256) for c0 in range(0, D_MODEL, 256)]

        def matmuls(mixed_ref):
            for cols in chunks:
                err = x_ref[:, cols] + _dot(mixed_ref[...], w_ref[:, cols]) - t_ref[:, cols]
                loss_ref[...] += (0.5 / D_MODEL) * jnp.sum(err * err)
                d_out = err / float(D_MODEL)
                dout_ref[:, cols] = d_out
                d16_ref[:, cols] = d_out.astype(BF16)
                yield
            for cols in chunks:
                dmix_ref[:, cols] = _dot_nt(d16_ref[...], w_ref[cols, :])
                yield
            for cols in chunks:
                dw_ref[:, cols] += _dot_tn(mixed_ref[...], d16_ref[:, cols])
                yield

        def concatenation(mixed_ref):
            for half in range(halves):
                r0 = TM * half
                rows = slice(r0, r0 + TM)
                u = ua_ref[rows, 0:256]
                g_a = ua_ref[rows, 256:512]
                halo = jnp.where(s > 0, halo_ref[...], 0.0) if half == 0 else ua_ref[r0 - HALO:r0, 0:256]
                cnt, lane = _pool_count(s * TMM + r0)
                d = _pool_delta(u, halo, cnt, lane).astype(BF16)
                y_a = _dot(d, wp_ref[...]) * sc_ref[...]
                mixed_ref[rows, 0:256] = (y_a * (g_a * _sigmoid(g_a))).astype(BF16)
                yield
                g_b = gb_ref[rows, :]
                y_b = jnp.concatenate([ol_ref[h, rows, 0:HEAD_DIM] for h in range(FOX_HEADS)], axis=1)
                mixed_ref[rows, 256:768] = (y_b * (g_b * _sigmoid(g_b))).astype(BF16)
                yield
                _, _, _, heads = _mem_attn(qm_ref[rows, 0:256], gq_ref[...], seg_ref[0:256, 0:256], km_ref, vm_ref)
                g_m = qm_ref[rows, 256:512]
                y_m = jnp.concatenate([y for _, y in heads], axis=1)
                mixed_ref[rows, 768:1024] = (y_m * (g_m * _sigmoid(g_m))).astype(BF16)
                yield

        multiplying = lambda mixed_ref: (matmuls(mixed_ref), 3 * len(chunks))
        building = lambda mixed_ref: (concatenation(mixed_ref), 3 * halves)

        @pl.when(s == 0)
        def _():
            dw_ref[...] = jnp.zeros((D_MODEL, D_MODEL), F32)
            loss_ref[...] = jnp.zeros((8, 128), F32)
            _alternate(building(mixed_even))

        @pl.when((s > 0) & (s < n_blocks) & (s % 2 == 1))
        def _():
            _alternate(multiplying(mixed_even), building(mixed_odd))

        @pl.when((s > 0) & (s < n_blocks) & (s % 2 == 0))
        def _():
            _alternate(multiplying(mixed_odd), building(mixed_even))

        @pl.when(s == n_blocks)
        def _():
            _alternate(multiplying(mixed_odd if n_blocks % 2 == 0 else mixed_even))
            dw16_ref[...] = dw_ref[...].astype(BF16)

    build = lambda s: jnp.minimum(s, n_blocks - 1)
    use = lambda s: jnp.maximum(s - 1, 0)
    piece = lambda j: pl.BlockSpec((TMM, PIECE), lambda s: (build(s), j))
    halo_blk = pl.BlockSpec((HALO, 256), lambda s: (jnp.maximum(build(s) * (TMM // HALO) - 1, 0), 0))
    row = pl.BlockSpec((TMM, D_MODEL), lambda s: (use(s), 0))
    return pl.pallas_call(
        body,
        name="mix_out_loss",
        grid=(n_blocks + 1,),
        in_specs=[piece(0), halo_blk, piece(4), piece(5), pl.BlockSpec((FOX_HEADS, TMM, 128), lambda s: (0, build(s), 0)),
                  _full((N_MEM, 256)), _full((N_MEM, 256)), _full((256, 256)), _full((1, 256)), _full((1, 256)),
                  _full((PIECE, PIECE)), row, row, _full((D_MODEL, D_MODEL))],
        out_specs=[row, row, _full((D_MODEL, D_MODEL)), _full((8, 128))],
        out_shape=[jax.ShapeDtypeStruct((s_len, D_MODEL), F32), jax.ShapeDtypeStruct((s_len, D_MODEL), F32),
                   jax.ShapeDtypeStruct((D_MODEL, D_MODEL), BF16), jax.ShapeDtypeStruct((8, 128), F32)],
        scratch_shapes=[pltpu.VMEM((TMM, D_MODEL), BF16), pltpu.VMEM((TMM, D_MODEL), BF16), pltpu.VMEM((TMM, D_MODEL), BF16),
                        pltpu.VMEM((D_MODEL, D_MODEL), F32)],
        compiler_params=_params(("arbitrary",)),
    )(proj, proj, proj, proj, olse, km, vm, wp_bd, pool_scale, gq_m, seg, x, target, w_out)


def _silu_pair(g):
    s = _sigmoid(g)
    return g * s, s * (1.0 + g * (1.0 - s))


def _mix_bwd(proj, olse, d_mixed, km, vm, wp_bd, pool_scale, gq_m, seg):
    s_len = proj.shape[0]
    n_tiles = s_len // TM

    def body(ua_ref, halo_ref, ga_next_ref, gb_ref, qm_ref, ol_ref, dm_ref, dm_next_ref, km_ref, vm_ref, wp_ref, sc_ref, gq_ref,
             seg_ref, dua_ref, dgb_ref, dqm_ref, do_ref, dl_ref, dwp_ref, dsc_ref, dkm_ref, dvm_ref, dgq_ref):
        i = pl.program_id(0)

        @pl.when(i == 0)
        def _():
            dwp_ref[...] = jnp.zeros((256, 256), F32)
            dsc_ref[...] = jnp.zeros((1, 256), F32)
            dkm_ref[...] = jnp.zeros((N_MEM, 256), F32)
            dvm_ref[...] = jnp.zeros((N_MEM, 256), F32)
            dgq_ref[...] = jnp.zeros((1, HEAD_DIM), F32)

        u = ua_ref[:, 0:256]
        g_a = ua_ref[:, 256:512]
        halo = jnp.where(i > 0, halo_ref[...], 0.0)
        cnt, lane = _pool_count(i * TM)
        d16 = _pool_delta(u, halo, cnt, lane).astype(BF16)
        t = _dot(d16, wp_ref[...])
        y_a = t * sc_ref[...]
        silu_a, dsilu_a = _silu_pair(g_a)
        dm_a = dm_ref[:, 0:256]
        dy_a = dm_a * silu_a
        dsc_ref[...] += jnp.sum(dy_a * t, axis=0, keepdims=True)
        dt16 = (dy_a * sc_ref[...]).astype(BF16)
        dwp_ref[...] += _dot_tn(d16, dt16)
        dd = _dot_nt(dt16, wp_ref[...])
        g_next = ga_next_ref[...]
        dt_next = (dm_next_ref[...] * (g_next * _sigmoid(g_next)) * sc_ref[...]).astype(BF16)
        dd_next = jnp.where(i < n_tiles - 1, _dot_nt(dt_next, wp_ref[...]), 0.0)
        cnt_next = _pool_count((i + 1) * TM)[0][0:HALO]
        dua_ref[:, 0:256] = _pool_delta_bwd(dd, dd_next, cnt, cnt_next, lane).astype(BF16)
        dua_ref[:, 256:512] = (dm_a * y_a * dsilu_a).astype(BF16)

        silu_b, dsilu_b = _silu_pair(gb_ref[...])
        dm_b = dm_ref[:, 256:768]
        o_all = jnp.concatenate([ol_ref[h][:, 0:HEAD_DIM] for h in range(FOX_HEADS)], axis=1)
        d_o = dm_b * silu_b
        dgb_ref[...] = (dm_b * o_all * dsilu_b).astype(BF16)
        for h in range(FOX_HEADS):
            do_ref[h] = d_o[:, HEAD_DIM * h:HEAD_DIM * (h + 1)].astype(BF16)
        prod = d_o * o_all
        hi = prod.astype(BF16)
        lo = (prod - hi.astype(F32)).astype(BF16)
        head_of_lane = lax.broadcasted_iota(jnp.int32, (FOX_HEADS, PIECE), 1) // HEAD_DIM
        pick = jnp.where(head_of_lane == lax.broadcasted_iota(jnp.int32, (FOX_HEADS, PIECE), 0), 1.0, 0.0).astype(BF16)
        delta = _dot_nt(pick, hi) + _dot_nt(pick, lo)
        for h in range(FOX_HEADS):
            dl_ref[h, 0] = jnp.broadcast_to(delta[h:h + 1, :], (8, TM))

        seg = seg_ref[0:256, 0:256]
        a, r, q16, heads = _mem_attn(qm_ref[:, 0:256], gq_ref[...], seg, km_ref, vm_ref)
        silu_m, dsilu_m = _silu_pair(qm_ref[:, 256:512])
        dm_m = dm_ref[:, 768:1024]
        y_m = jnp.concatenate([y for _, y in heads], axis=1)
        dqm_ref[:, 256:512] = (dm_m * y_m * dsilu_m).astype(BF16)
        dy16_all = (dm_m * silu_m).astype(BF16)
        d_scores = []
        for h in range(MEM_HEADS):
            sl = slice(HEAD_DIM * h, HEAD_DIM * (h + 1))
            p = heads[h][0]
            dy16 = dy16_all[:, sl]
            dp = _dot_nt(dy16, vm_ref[:, sl])
            dvm_ref[:, sl] += _dot_tn(p.astype(BF16), dy16)
            ds16 = (p * (dp - jnp.sum(dp * p, axis=-1, keepdims=True))).astype(BF16)
            dkm_ref[:, sl] += _dot_tn(ds16, q16[:, sl])
            d_scores.append(_dot(ds16, km_ref[:, sl]))
        dq, dg_rows = _head_rms_bwd(a, r, gq_ref[...], jnp.concatenate(d_scores, axis=1) * 0.125, seg)
        dqm_ref[:, 0:256] = dq.astype(BF16)
        dgq_ref[...] += _fold_heads(jnp.sum(dg_rows, axis=0, keepdims=True), MEM_HEADS)

    n_halo = s_len // HALO
    piece = lambda j: pl.BlockSpec((TM, PIECE), lambda i: (i, j))
    before = pl.BlockSpec((HALO, 256), lambda i: (jnp.maximum(i * (TM // HALO) - 1, 0), 0))
    after = lambda j: pl.BlockSpec((HALO, 256), lambda i: (jnp.minimum((i + 1) * (TM // HALO), n_halo - 1), j))
    row = pl.BlockSpec((TM, D_MODEL), lambda i: (i, 0))
    out_piece = pl.BlockSpec((TM, PIECE), lambda i: (i, 0))
    return pl.pallas_call(
        body,
        name="mix_bwd",
        grid=(n_tiles,),
        in_specs=[piece(0), before, after(1), piece(4), piece(5), pl.BlockSpec((FOX_HEADS, TM, 128), lambda i: (0, i, 0)),
                  row, after(0), _full((N_MEM, 256)), _full((N_MEM, 256)), _full((256, 256)), _full((1, 256)), _full((1, 256)),
                  _full((PIECE, PIECE))],
        out_specs=[out_piece, out_piece, out_piece, pl.BlockSpec((FOX_HEADS, TM, HEAD_DIM), lambda i: (0, i, 0)),
                   pl.BlockSpec((FOX_HEADS, 1, 8, TM), lambda i: (0, i, 0, 0)),
                   _full((256, 256)), _full((1, 256)), _full((N_MEM, 256)), _full((N_MEM, 256)), _full((1, HEAD_DIM))],
        out_shape=[jax.ShapeDtypeStruct((s_len, PIECE), BF16)] * 3 + [
            jax.ShapeDtypeStruct((FOX_HEADS, s_len, HEAD_DIM), BF16),
            jax.ShapeDtypeStruct((FOX_HEADS, n_tiles, 8, TM), F32),
            jax.ShapeDtypeStruct((256, 256), F32), jax.ShapeDtypeStruct((1, 256), F32),
            jax.ShapeDtypeStruct((N_MEM, 256), F32), jax.ShapeDtypeStruct((N_MEM, 256), F32),
            jax.ShapeDtypeStruct((1, HEAD_DIM), F32)],
        compiler_params=_params(("arbitrary",)),
    )(proj, proj, proj, proj, proj, olse, d_mixed, d_mixed, km, vm, wp_bd, pool_scale, gq_m, seg)


def _fox_bwd(q_aug, k_aug, kt_aug, v_h, d_o, lse_rows, delta_rows, dw_kv16, dw_out16):
    s_len = q_aug.shape[1]
    n_tiles = s_len // TA
    n_groups = FOX_HEADS // HB

    def body(q_hbm, k_ref, kt_ref, v_ref, do_hbm, st_ref, dl_ref, dkv16_ref, dout16_ref, dq_ref, dk_ref, dv_ref, land_kv, land_out,
             dqt_ref, q_ref, do_ref, send_sems, recv_sems, local_sems, tile_sems):
        j = pl.program_id(1)
        remote, local = _row_block_exchange(dkv16_ref, dout16_ref, land_kv, land_out, send_sems, recv_sems, local_sems, gather=False)

        def tile_loads(i):
            rows = pl.ds(pl.multiple_of(i * TA, TA), TA)
            return [pltpu.make_async_copy(q_hbm.at[:, rows, :], q_ref.at[:, rows, :], tile_sems.at[2 * i]),
                    pltpu.make_async_copy(do_hbm.at[:, rows, :], do_ref.at[:, rows, :], tile_sems.at[2 * i + 1])]

        @pl.when((pl.program_id(0) == 0) & (j == 0))
        def _():
            for i in range(n_tiles):
                for cp in tile_loads(i):
                    cp.start()
            for cp in remote + local:
                cp.start()

        @pl.when(j == 0)
        def _():
            dqt_ref[...] = jnp.zeros((HB, n_tiles, AUG_ROWS, TA), F32)

        ks = [k_ref[h] for h in range(HB)]
        kts = [kt_ref[h, 0, 0:AUG_ROWS, :] for h in range(HB)]
        vs = [v_ref[h] for h in range(HB)]

        def pair(i0, acc, masked, width):
            @pl.when(j == 0)
            def _():
                for t in range(width):
                    for cp in tile_loads(i0 + t):
                        cp.wait()

            rows = pl.ds(pl.multiple_of(i0 * TA, TA), width * TA)
            qs = [q_ref[h, rows, :] for h in range(HB)]
            d_os = [do_ref[h, rows, :] for h in range(HB)]
            row_stat = lambda ref, h: jnp.concatenate([ref[h, i0 + t, 0:1, :] for t in range(width)], axis=1)
            scores = [(_dot_nt(ks[h], qs[h]), _dot_nt(vs[h], d_os[h])) for h in range(HB)]
            probs = []
            for h in range(HB):
                s, dp = scores[h]
                if masked:
                    s = jnp.where(_causal_mask_t(), s, -1e30)
                p = jnp.exp(s - row_stat(st_ref, h))
                probs.append((p.astype(BF16), (p * (dp - row_stat(dl_ref, h))).astype(BF16)))
            out = []
            for h in range(HB):
                p16, ds16 = probs[h]
                dqt = _dot(kts[h], ds16)
                for t in range(width):
                    dqt_ref[h, i0 + t] += dqt[:, t * TA:(t + 1) * TA]
                out.append((acc[h][0] + _dot(ds16, qs[h]), acc[h][1] + _dot(p16, d_os[h])))
            return tuple(out)

        zero = tuple((jnp.zeros((TA, 128), F32), jnp.zeros((TA, HEAD_DIM), F32)) for _ in range(HB))
        acc = pair(j, zero, True, 1)
        odd = (n_tiles - 1 - j) % 2
        acc = lax.fori_loop(j + 1, j + 1 + odd, lambda i, a: pair(i, a, False, 1), acc)
        acc = lax.fori_loop(0, (n_tiles - 1 - j) // 2, lambda t, a: pair(j + 1 + odd + 2 * t, a, False, 2), acc)
        pad = jnp.zeros((128 - AUG_ROWS, TA), F32)
        for h in range(HB):
            dk_ref[h] = acc[h][0]
            dv_ref[h] = acc[h][1].astype(BF16)
            dq_ref[h] = jnp.concatenate([dqt_ref[h, j], pad], axis=0).T

        @pl.when((pl.program_id(0) == n_groups - 1) & (j == n_tiles - 1))
        def _():
            for cp in remote:
                cp.wait_recv()
            for cp in remote:
                cp.wait_send()
            for cp in local:
                cp.wait()

    assert n_groups == 1
    resident = pl.BlockSpec(memory_space=pltpu.VMEM)
    rows_blk = lambda r: resident
    tile = lambda w: pl.BlockSpec((HB, TA, w), lambda g, j: (g, j, 0))
    hbm = pl.BlockSpec(memory_space=pl.ANY)
    return pl.pallas_call(
        body,
        name="fox_bwd",
        grid=(n_groups, n_tiles),
        in_specs=[hbm, tile(128), pl.BlockSpec((HB, 1, 128, TA), lambda g, j: (g, j, 0, 0)), tile(HEAD_DIM),
                  hbm, rows_blk(8), rows_blk(8), hbm, hbm],
        out_specs=[tile(128), tile(128), tile(HEAD_DIM), hbm, hbm],
        out_shape=[jax.ShapeDtypeStruct((FOX_HEADS, s_len, 128), F32), jax.ShapeDtypeStruct((FOX_HEADS, s_len, 128), F32),
                   jax.ShapeDtypeStruct((FOX_HEADS, s_len, HEAD_DIM), BF16),
                   jax.ShapeDtypeStruct((N_DEV, 128, 512), BF16), jax.ShapeDtypeStruct((N_DEV, 128, D_MODEL), BF16)],
        scratch_shapes=[pltpu.VMEM((HB, n_tiles, AUG_ROWS, TA), F32),
                        pltpu.VMEM((HB, s_len, 128), BF16), pltpu.VMEM((HB, s_len, HEAD_DIM), BF16),
                        pltpu.SemaphoreType.DMA((14,)), pltpu.SemaphoreType.DMA((14,)), pltpu.SemaphoreType.DMA((2,)),
                        pltpu.SemaphoreType.DMA((2 * n_tiles,))],
        compiler_params=_params(("arbitrary", "arbitrary")),
    )(q_aug, k_aug, kt_aug, v_h, d_o, lse_rows, delta_rows, dw_kv16, dw_out16)


def _fox_post(proj, bf_pad, gq, gk, seg, dq_aug, dk_aug, dv_h):
    s_len = proj.shape[0]
    n_tiles = s_len // TM

    def body(q_ref, k_ref, f_ref, bf_ref, gq_ref, gk_ref, seg_ref, dqa_ref, dka_ref, dvh_ref,
             dq_ref, dk_ref, dv_ref, df_ref, dgq_ref, dgk_ref, dbf_ref, carry_ref):
        @pl.when(pl.program_id(0) == 0)
        def _():
            carry_ref[...] = jnp.zeros((1, 128), F32)
            dgq_ref[...] = jnp.zeros((1, HEAD_DIM), F32)
            dgk_ref[...] = jnp.zeros((1, HEAD_DIM), F32)
            dbf_ref[...] = jnp.zeros((1, 128), F32)

        lane = lax.broadcasted_iota(jnp.int32, (TM, 128), 1)
        seg = seg_ref[...]
        dqas = [dqa_ref[h] for h in range(FOX_HEADS)]
        dkas = [dka_ref[h] for h in range(FOX_HEADS)]
        d_cum = jnp.zeros((TM, 128), F32)
        for h in range(FOX_HEADS):
            d_cum = jnp.where(lane == h, dqas[h][:, 64:65] - dkas[h][:, 67:68], d_cum)
        q_all = q_ref[...]
        k_all = k_ref[...]
        rq = _head_rms(q_all, seg)
        rk = _head_rms(k_all, seg)
        dy_q = jnp.concatenate([t[:, 0:HEAD_DIM] for t in dqas], axis=1) * 0.125
        dy_k = jnp.concatenate([t[:, 0:HEAD_DIM] for t in dkas], axis=1)
        dq, dgq_rows = _head_rms_bwd(q_all * rq, rq, gq_ref[...], dy_q, seg)
        dk, dgk_rows = _head_rms_bwd(k_all * rk, rk, gk_ref[...], dy_k, seg)
        dq_ref[...] = dq.astype(BF16)
        dk_ref[...] = dk.astype(BF16)
        dv_ref[...] = jnp.concatenate([dvh_ref[h].astype(F32) for h in range(FOX_HEADS)], axis=1).astype(BF16)
        dgq_ref[...] += _fold_heads(jnp.sum(dgq_rows, axis=0, keepdims=True), FOX_HEADS)
        dgk_ref[...] += _fold_heads(jnp.sum(dgk_rows, axis=0, keepdims=True), FOX_HEADS)

        row = lax.broadcasted_iota(jnp.int32, (TM, TM), 0)
        col = lax.broadcasted_iota(jnp.int32, (TM, TM), 1)
        tri = jnp.where(col >= row, 1.0, 0.0).astype(BF16)
        hi, mid, lo = _split3(d_cum)
        d_logf = _dot(tri, hi) + _dot(tri, mid) + _dot(tri, lo) + carry_ref[...]
        carry_ref[...] = d_logf[0:1, :]
        z = f_ref[...] + bf_ref[...]
        d_f = jnp.where(lane < FOX_HEADS, d_logf / (1.0 + jnp.exp(z)), 0.0)
        df_ref[...] = d_f.astype(BF16)
        dbf_ref[...] += jnp.sum(d_f, axis=0, keepdims=True)

    rev = lambda i: n_tiles - 1 - i
    col_blk = lambda j: pl.BlockSpec((TM, PIECE), lambda i: (rev(i), j))
    head_blk = lambda w: pl.BlockSpec((FOX_HEADS, TM, w), lambda i: (0, rev(i), 0))
    out_piece = pl.BlockSpec((TM, PIECE), lambda i: (rev(i), 0))
    return pl.pallas_call(
        body,
        name="fox_post",
        grid=(n_tiles,),
        in_specs=[col_blk(1), col_blk(2), pl.BlockSpec((TM, 128), lambda i: (rev(i), MY_F_OFF // 128)),
                  _full((1, 128)), _full((1, PIECE)), _full((1, PIECE)), _full((PIECE, PIECE)),
                  head_blk(128), head_blk(128), head_blk(HEAD_DIM)],
        out_specs=[out_piece, out_piece, out_piece, pl.BlockSpec((TM, 128), lambda i: (rev(i), 0)),
                   _full((1, HEAD_DIM)), _full((1, HEAD_DIM)), _full((1, 128))],
        out_shape=[jax.ShapeDtypeStruct((s_len, PIECE), BF16)] * 3 + [
            jax.ShapeDtypeStruct((s_len, 128), BF16), jax.ShapeDtypeStruct((1, HEAD_DIM), F32),
            jax.ShapeDtypeStruct((1, HEAD_DIM), F32), jax.ShapeDtypeStruct((1, 128), F32)],
        scratch_shapes=[pltpu.VMEM((1, 128), F32)],
        compiler_params=_params(("arbitrary",)),
    )(proj, proj, proj, bf_pad, gq, gk, seg, dq_aug, dk_aug, dv_h)


def _mem_bwd(mem, g, w_kv, gk, dkm, dvm):
    def body(mem_ref, g_ref, w_ref, gk_ref, dkm_ref, dvm_ref, dw_ref, dg_ref, dgk_ref, dkv_ref):
        m = mem_ref[...]
        r = _rms(m)
        a = m * r
        mn16 = (a * g_ref[...]).astype(BF16)
        kv = _dot(mn16, w_ref[...])
        dgk = jnp.zeros((N_MEM, HEAD_DIM), F32)
        for h in range(MEM_HEADS):
            sl = slice(HEAD_DIM * h, HEAD_DIM * (h + 1))
            kh = kv[:, sl]
            rk = _rms(kh)
            dk, dg_rows = _rms_bwd(kh * rk, rk, gk_ref[...], dkm_ref[:, sl])
            dkv_ref[:, sl] = dk.astype(BF16)
            dgk = dgk + dg_rows
        dkv_ref[:, 256:512] = dvm_ref[...].astype(BF16)
        dgk_ref[...] = jnp.sum(dgk, axis=0, keepdims=True)
        dkv16 = dkv_ref[...]
        dw_ref[...] = _dot_tn(mn16, dkv16).astype(BF16)
        dg_ref[...] = jnp.sum(_dot_nt(dkv16, w_ref[...]) * a, axis=0, keepdims=True)

    return pl.pallas_call(
        body,
        name="mem_bwd",
        out_shape=(jax.ShapeDtypeStruct((D_MODEL, 512), BF16), jax.ShapeDtypeStruct((1, D_MODEL), F32),
                   jax.ShapeDtypeStruct((1, HEAD_DIM), F32)),
        in_specs=[pl.BlockSpec(memory_space=pltpu.VMEM)] * 6,
        out_specs=(pl.BlockSpec(memory_space=pltpu.VMEM),) * 3,
        scratch_shapes=[pltpu.VMEM((N_MEM, 512), BF16)],
        compiler_params=pltpu.CompilerParams(vmem_limit_bytes=VMEM_LIMIT),
    )(mem, g, w_kv, gk, dkm, dvm)


def _grad_x_exchange(pieces, d_f, w_my, x, norm_g, d_out, dw_in, land_kv, land_out, d_mem_g, d_scale, d_bf, d_gq, d_gk, d_gqm,
                     d_gkm, dwp_bd, loss_blk):
    s_len = x.shape[0]
    n_steps = s_len // TM
    step2, step3 = n_steps // 4, (11 * n_steps) // 16
    half = D_MODEL // 2

    def body(p0, p1, p2, p3, p4, p5, df_ref, x_ref, dout_ref, w_ref, g_ref, in_ref, lkv_ref, lout_ref,
             dmg, dsc, dbf, dgq, dgk, dgqm, dgkm, dwp, loss_ref,
             gx_ref, rin_ref, rkv_ref, rout_ref, sred_ref,
             dng, land1, send2a, send2b, keep2a, keep2b, land2a, land2b, send3a, send3b, land3a, land3b, sb_ref, sland,
             own4, lkv_v, lout_v, send_sems, recv_sems, load_sems):
        i = pl.program_id(0)
        x_, y_, c_, me = _my_place()
        sibling, x_nbr, y_nbr = (x_, y_, 1 - c_), (1 - x_, y_, c_), (x_, 1 - y_, c_)
        loads = [pltpu.make_async_copy(in_ref.at[2 * k + c_], own4.at[k], load_sems.at[k]) for k in range(4)]
        loads += [pltpu.make_async_copy(lkv_ref, lkv_v, load_sems.at[4]), pltpu.make_async_copy(lout_ref, lout_v, load_sems.at[5])]

        def rcopy(src, dst, sem, to):
            return pltpu.make_async_remote_copy(src_ref=src, dst_ref=dst, send_sem=send_sems.at[sem], recv_sem=recv_sems.at[sem],
                                                device_id=to, device_id_type=MESH)

        early_rows, late_rows = pl.ds(8, SB_ROWS - 8), pl.ds(0, 8)
        small_early, small_late = [], []
        for k in range(1, N_DEV):
            peer = (x_ ^ (k >> 2), y_ ^ ((k >> 1) & 1), c_ ^ (k & 1))
            small_early.append(rcopy(sb_ref.at[early_rows], sland.at[me, early_rows], k - 1, peer))
            small_late.append(rcopy(sb_ref.at[late_rows], sland.at[me, late_rows], 16 + k, peer))
        small = small_early + small_late
        stage1 = [rcopy(in_ref.at[2 * k + 1 - c_], land1.at[k], 7 + k, sibling) for k in range(4)]
        stage2 = [rcopy(send2a.at[j], land2a.at[j], 11 + j, x_nbr) for j in range(2)]
        stage2 += [rcopy(send2b.at[j], land2b.at[j], 13 + j, y_nbr) for j in range(2)]
        stage3 = [rcopy(send3a, land3a, 15, y_nbr), rcopy(send3b, land3b, 16, x_nbr)]
        top, bot = slice(0, half), slice(half, D_MODEL)

        @pl.when(i == 0)
        def _():
            dng[...] = jnp.zeros((1, D_MODEL), F32)
            for cp in stage1 + loads:
                cp.start()
            sb_ref[...] = jnp.zeros((SB_ROWS, SB_W), F32)
            sb_ref[SB_POOL_SCALE:SB_POOL_SCALE + 1, :] = dsc[...]
            sb_ref[SB_B_F:SB_B_F + 1, 0:128] = dbf[...]
            for row, ref in ((SB_FOX_Q, dgq), (SB_FOX_K, dgk), (SB_MEM_Q, dgqm), (SB_MEM_K, dgkm)):
                sb_ref[row:row + 1, 0:HEAD_DIM] = ref[...]
            sb_ref[SB_LOSS:SB_LOSS + 1, 0:128] = loss_ref[0:1, :]
            for grp in range(4):
                sl = slice(64 * grp, 64 * (grp + 1))
                sb_ref[SB_W_POOL:SB_W_POOL + 64, sl] = dwp[sl, sl]
            for cp in small_early:
                cp.start()

        @pl.when(i == step2)
        def _():
            for cp in stage1:
                cp.wait_recv()
            for cp in loads[0:4]:
                cp.wait()

            def chip_sum(k, cols):
                return own4[k, :, cols].astype(F32) + land1[k, :, cols].astype(F32)

            for j in range(2):
                send2a[j] = chip_sum(2 * (1 - x_) + j, top).astype(BF16)
                keep2a[j] = chip_sum(2 * x_ + j, top)
                send2b[j] = chip_sum(2 * j + 1 - y_, bot).astype(BF16)
                keep2b[j] = chip_sum(2 * j + y_, bot)
            for cp in stage2:
                cp.start()

        @pl.when(i == step3)
        def _():
            for cp in stage2:
                cp.wait_recv()
            for j in range(2):
                keep2a[j] = keep2a[j] + land2a[j].astype(F32)
                keep2b[j] = keep2b[j] + land2b[j].astype(F32)
            send3a[...] = keep2a[1 - y_].astype(BF16)
            send3b[...] = keep2b[1 - x_].astype(BF16)
            for cp in stage3:
                cp.start()

        dh = _dot_nt(df_ref[...], w_ref[:, MY_F_OFF:MY_WIDTH])
        for j, p in enumerate((p0, p1, p2, p3, p4, p5)):
            dh = dh + _dot_nt(p[...], w_ref[:, PIECE * j:PIECE * (j + 1)])
        xv = x_ref[...]
        r = _rms(xv)
        dx, dg_rows = _rms_bwd(xv * r, r, g_ref[...], dh)
        gx_ref[...] = dout_ref[...] + dx
        dng[...] += jnp.sum(dg_rows, axis=0, keepdims=True)

        @pl.when(i == n_steps - 1)
        def _():
            for k in range(4):
                sb_ref[SB_NORM_G + k:SB_NORM_G + k + 1, :] = dng[:, SB_W * k:SB_W * (k + 1)]
                sb_ref[SB_MEM_NORM_G + k:SB_MEM_NORM_G + k + 1, :] = dmg[:, SB_W * k:SB_W * (k + 1)]
            for cp in small_late:
                cp.start()
            sland[me] = sb_ref[...]
            for cp in stage3:
                cp.wait_recv()
            rin_ref[:, top] = keep2a[y_] + land3a[...].astype(F32)
            rin_ref[:, bot] = keep2b[x_] + land3b[...].astype(F32)
            for cp in small:
                cp.wait_recv()
            for cp in small + stage1 + stage2 + stage3:
                cp.wait_send()
            for cp in loads[4:6]:
                cp.wait()
            for land, red in ((lkv_v, rkv_ref), (lout_v, rout_ref), (sland, sred_ref)):
                acc = land[0].astype(F32)
                for d in range(1, N_DEV):
                    acc = acc + land[d].astype(F32)
                red[...] = acc

    piece = pl.BlockSpec((TM, PIECE), lambda i: (i, 0))
    row = pl.BlockSpec((TM, D_MODEL), lambda i: (i, 0))
    vmem = pl.BlockSpec(memory_space=pltpu.VMEM)
    half_chunk = lambda n, dt: pltpu.VMEM((n, CHUNK_ROWS, half), dt)
    whole = (w_my, norm_g, dw_in, land_kv, land_out, d_mem_g, d_scale, d_bf, d_gq, d_gk, d_gqm, d_gkm, dwp_bd, loss_blk)
    return pl.pallas_call(
        body,
        name="grad_x_exchange",
        grid=(n_steps,),
        in_specs=[piece] * 6 + [pl.BlockSpec((TM, 128), lambda i: (i, 0)), row, row, vmem, vmem]
        + [pl.BlockSpec(memory_space=pl.ANY)] * 3 + [vmem] * (len(whole) - 5),
        out_specs=[row, vmem, vmem, vmem, vmem],
        out_shape=[jax.ShapeDtypeStruct((s_len, D_MODEL), F32), jax.ShapeDtypeStruct((CHUNK_ROWS, D_MODEL), F32),
                   jax.ShapeDtypeStruct((128, 512), F32), jax.ShapeDtypeStruct((128, D_MODEL), F32),
                   jax.ShapeDtypeStruct((SB_ROWS, SB_W), F32)],
        scratch_shapes=[
            pltpu.VMEM((1, D_MODEL), F32),
            pltpu.VMEM((4, CHUNK_ROWS, D_MODEL), BF16),
            half_chunk(2, BF16), half_chunk(2, BF16), half_chunk(2, F32), half_chunk(2, F32), half_chunk(2, BF16), half_chunk(2, BF16),
            pltpu.VMEM((CHUNK_ROWS, half), BF16), pltpu.VMEM((CHUNK_ROWS, half), BF16),
            pltpu.VMEM((CHUNK_ROWS, half), BF16), pltpu.VMEM((CHUNK_ROWS, half), BF16),
            pltpu.VMEM((SB_ROWS, SB_W), F32),
            pltpu.VMEM((N_DEV, SB_ROWS, SB_W), F32),
            pltpu.VMEM((4, CHUNK_ROWS, D_MODEL), BF16),
            pltpu.VMEM((N_DEV, 128, 512), BF16),
            pltpu.VMEM((N_DEV, 128, D_MODEL), BF16),
            pltpu.SemaphoreType.DMA((24,)),
            pltpu.SemaphoreType.DMA((24,)),
            pltpu.SemaphoreType.DMA((6,)),
        ],
        compiler_params=_params(("arbitrary",)),
    )(*pieces, d_f, x, d_out, *whole)


def _grad_w_in(h16, pieces, d_f):
    s_len = h16.shape[0]
    tk = 512

    def body(h_ref, p0, p1, p2, p3, p4, p5, df_ref, out_ref, dw_ref):
        @pl.when(pl.program_id(0) == 0)
        def _():
            dw_ref[...] = jnp.zeros((D_MODEL, MY_WIDTH), F32)

        h = h_ref[...]
        for j, p in enumerate((p0, p1, p2, p3, p4, p5)):
            dw_ref[:, PIECE * j:PIECE * (j + 1)] += _dot_tn(h, p[...])
        dw_ref[:, MY_F_OFF:MY_WIDTH] += _dot_tn(h, df_ref[...])

        @pl.when(pl.program_id(0) == pl.num_programs(0) - 1)
        def _():
            for d in range(N_DEV):
                parts = [dw_ref[:, start:start + width] for start, width in _chunk_segments(d)]
                parts.append(jnp.zeros((D_MODEL, 512 - IN_CHUNK), F32))
                out_ref[d] = jnp.concatenate(parts, axis=1).T[0:CHUNK_ROWS].astype(BF16)

    piece = pl.BlockSpec((tk, PIECE), lambda i: (i, 0))
    return pl.pallas_call(
        body,
        name="grad_w_in",
        grid=(s_len // tk,),
        in_specs=[pl.BlockSpec((tk, D_MODEL), lambda i: (i, 0))] + [piece] * 6 + [pl.BlockSpec((tk, 128), lambda i: (i, 0))],
        out_specs=_full((N_DEV, CHUNK_ROWS, D_MODEL)),
        out_shape=jax.ShapeDtypeStruct((N_DEV, CHUNK_ROWS, D_MODEL), BF16),
        scratch_shapes=[pltpu.VMEM((D_MODEL, MY_WIDTH), F32)],
        compiler_params=_params(("arbitrary",)),
    )(h16, *pieces, d_f)


def _adamw(w, g, m, v):
    m = ADAM_B1 * m + (1.0 - ADAM_B1) * g
    v = ADAM_B2 * v + (1.0 - ADAM_B2) * (g * g)
    m_hat = m / (1.0 - ADAM_B1 ** ADAM_STEP)
    v_hat = v / (1.0 - ADAM_B2 ** ADAM_STEP)
    return -ADAM_LR * (m_hat / (jnp.sqrt(v_hat) + ADAM_EPS) + ADAM_WD * w), m, v


PARAM_ORDER = ("norm_g", "w_in", "b_f", "w_pool", "pool_scale", "fox_q_g", "fox_k_g", "mem_norm_g", "w_mem_kv", "mem_q_g", "mem_k_g", "w_out")


def _update(red_in, red_kv, red_out, sred, params):
    def body(rin_ref, rkv_ref, rout_ref, sred_ref, *refs):
        ins, outs = refs[:3 * len(PARAM_ORDER)], refs[3 * len(PARAM_ORDER):]
        wide = lambda row: jnp.concatenate([sred_ref[row + k:row + k + 1, :] for k in range(4)], axis=1)
        head = lambda row: sred_ref[row:row + 1, 0:HEAD_DIM]
        grads = {
            "norm_g": lambda: wide(SB_NORM_G), "w_in": lambda: rin_ref[...], "b_f": lambda: sred_ref[SB_B_F:SB_B_F + 1, 0:FOX_HEADS],
            "pool_scale": lambda: sred_ref[SB_POOL_SCALE:SB_POOL_SCALE + 1, :], "fox_q_g": lambda: head(SB_FOX_Q),
            "fox_k_g": lambda: head(SB_FOX_K), "mem_norm_g": lambda: wide(SB_MEM_NORM_G), "w_mem_kv": lambda: rkv_ref[...],
            "mem_q_g": lambda: head(SB_MEM_Q), "mem_k_g": lambda: head(SB_MEM_K), "w_out": lambda: rout_ref[...],
        }
        for p, name in enumerate(PARAM_ORDER):
            w_ref, m_ref, v_ref = ins[3 * p:3 * p + 3]
            g_out, d_out, m_out, v_out = outs[4 * p:4 * p + 4]
            if name == "w_pool":
                for grp in range(4):
                    g = sred_ref[SB_W_POOL:SB_W_POOL + 64, 64 * grp:64 * (grp + 1)]
                    delta, m_new, v_new = _adamw(w_ref[grp], g, m_ref[grp], v_ref[grp])
                    g_out[grp], d_out[grp], m_out[grp], v_out[grp] = g, delta, m_new, v_new
            elif name == "w_in":
                for j in range(8):
                    rows = pl.ds(j, IN_CHUNK, stride=8)
                    g = rin_ref[0:IN_CHUNK, 128 * j:128 * (j + 1)]
                    delta, m_new, v_new = _adamw(w_ref[rows, :], g, m_ref[rows, :], v_ref[rows, :])
                    g_out[rows, :], d_out[rows, :], m_out[rows, :], v_out[rows, :] = g, delta, m_new, v_new
            else:
                g = grads[name]()
                delta, m_new, v_new = _adamw(w_ref[...], g, m_ref[...], v_ref[...])
                g_out[...], d_out[...], m_out[...], v_out[...] = g, delta, m_new, v_new

    flat = [t for name in PARAM_ORDER for t in params[name]]
    shapes = [params[name][0].shape for name in PARAM_ORDER for _ in range(4)]
    outs = pl.pallas_call(
        body,
        name="adamw_update",
        out_shape=tuple(jax.ShapeDtypeStruct(s, F32) for s in shapes),
        in_specs=[pl.BlockSpec(memory_space=pltpu.VMEM)] * (4 + len(flat)),
        out_specs=(pl.BlockSpec(memory_space=pltpu.VMEM),) * len(shapes),
        compiler_params=pltpu.CompilerParams(vmem_limit_bytes=VMEM_LIMIT),
    )(red_in, red_kv, red_out, sred, *flat)
    return {name: outs[4 * p:4 * p + 4] for p, name in enumerate(PARAM_ORDER)}


def kernel(x, mem, norm_g, w_in, b_f, w_pool, pool_scale, fox_q_g, fox_k_g, mem_norm_g, w_mem_kv, mem_q_g, mem_k_g, w_out, loss_target, m_norm_g, m_w_in, m_b_f, m_w_pool, m_pool_scale, m_fox_q_g, m_fox_k_g, m_mem_norm_g, m_w_mem_kv, m_mem_q_g, m_mem_k_g, m_w_out, v_norm_g, v_w_in, v_b_f, v_w_pool, v_pool_scale, v_fox_q_g, v_fox_k_g, v_mem_norm_g, v_w_mem_kv, v_mem_q_g, v_mem_k_g, v_w_out):
    given = dict(norm_g=(norm_g, m_norm_g, v_norm_g), w_in=(w_in, m_w_in, v_w_in), b_f=(b_f, m_b_f, v_b_f),
                 w_pool=(w_pool, m_w_pool, v_w_pool), pool_scale=(pool_scale, m_pool_scale, v_pool_scale),
                 fox_q_g=(fox_q_g, m_fox_q_g, v_fox_q_g), fox_k_g=(fox_k_g, m_fox_k_g, v_fox_k_g),
                 mem_norm_g=(mem_norm_g, m_mem_norm_g, v_mem_norm_g), w_mem_kv=(w_mem_kv, m_w_mem_kv, v_w_mem_kv),
                 mem_q_g=(mem_q_g, m_mem_q_g, v_mem_q_g), mem_k_g=(mem_k_g, m_mem_k_g, v_mem_k_g), w_out=(w_out, m_w_out, v_w_out))
    params = {name: tuple(t[0] if t.ndim > 2 else t for t in wmv) for name, wmv in given.items()}
    params["w_in"] = tuple(t.T.reshape(IN_CHUNK * 8, 128) for t in params["w_in"])
    x2, mem2, target2 = x[0], mem[0], loss_target[0]

    seg = jnp.asarray(np.kron(np.eye(FOX_HEADS), np.full((HEAD_DIM, HEAD_DIM), 1.0 / HEAD_DIM)), BF16)
    w_my, w_kv16, w_out16, wp_bd, bf_pad, gq8, gk8, gqm4 = _gather_w_in(
        params["w_in"][0], params["w_mem_kv"][0], params["w_out"][0], params["w_pool"][0], b_f, fox_q_g, fox_k_g, mem_q_g)
    proj, h16, q_aug, k_aug, v_h, kt_aug, vt_aug = _proj_prep(x2, norm_g, w_my, bf_pad, gq8, gk8, seg)
    olse, lse_rows, w_kv_all, w_out_all = _fox_fwd(q_aug, k_aug, vt_aug, w_kv16, w_out16)
    km, vm = _mem_prep(mem2, mem_norm_g, w_kv_all, mem_k_g)
    d_out, d_mixed, dw_out, loss_blk = _mix_out_loss(proj, olse, km, vm, wp_bd, pool_scale, gqm4, seg, x2, target2, w_out_all)

    d_ua, d_gb, d_qm, d_o, delta_rows, dwp_bd, d_scale, dkm, dvm, d_gqm = _mix_bwd(proj, olse, d_mixed, km, vm, wp_bd, pool_scale,
                                                                                    gqm4, seg)
    dw_kv, d_mem_g, d_gkm = _mem_bwd(mem2, mem_norm_g, w_kv_all, mem_k_g, dkm, dvm)
    dq_aug, dk_aug, dv_h, land_kv, land_out = _fox_bwd(q_aug, k_aug, kt_aug, v_h, d_o, lse_rows, delta_rows, dw_kv, dw_out)
    d_q, d_k, d_v, d_f, d_gq, d_gk, d_bf = _fox_post(proj, bf_pad, gq8, gk8, seg, dq_aug, dk_aug, dv_h)
    pieces = (d_ua, d_q, d_k, d_v, d_gb, d_qm)
    dw_in = _grad_w_in(h16, pieces, d_f)
    grad_x, red_in, red_kv, red_out, sred = _grad_x_exchange(pieces, d_f, w_my, x2, norm_g, d_out, dw_in, land_kv, land_out, d_mem_g,
                                                             d_scale, d_bf, d_gq, d_gk, d_gqm, d_gkm, dwp_bd, loss_blk)
    new = _update(red_in, red_kv, red_out, sred, params)
    result = [sred[SB_LOSS, 0], grad_x[None]]
    for kind in range(4):
        for name in PARAM_ORDER:
            out = new[name][kind]
            if name == "w_in":
                out = out.reshape(IN_CHUNK, D_MODEL).T
            result.append(out[None] if given[name][0].ndim > 2 else out)
    return tuple(result)
```

```python
import functools

import jax
import jax.numpy as jnp
import numpy as np
from jax import lax
from jax.experimental import pallas as pl
from jax.experimental.pallas import tpu as pltpu

F32 = jnp.float32
BF16 = jnp.bfloat16
MESH = pl.DeviceIdType.MESH

N_DEV = 8
D_MODEL = 1024
HEAD_DIM = 64
FOX_HEADS = 8
MEM_HEADS = 4
N_MEM = 256
POOL_WIDTH = 256
EPS = 1e-6
IN_WIDTH = 3080
IN_CHUNK = IN_WIDTH // N_DEV
CHUNK_ROWS = 400
F_OFF = 2048
MY_WIDTH = 3200
MY_F_OFF = 3072
PIECE = 512
TM = 256
TMM = 512
TA = 256
HB = 8
HB_FWD = 8
AUG_ROWS = 72
HALO = 16
VMEM_LIMIT = 56 * 1024 * 1024

ADAM_LR = 0.001
ADAM_B1 = 0.9
ADAM_B2 = 0.999
ADAM_EPS = 1e-08
ADAM_WD = 0.01
ADAM_STEP = 10


def _dot(a, b):
    return jnp.dot(a, b, preferred_element_type=F32)


def _dot_nt(a, b):
    return lax.dot_general(a, b, (((1,), (1,)), ((), ())), preferred_element_type=F32)


def _dot_tn(a, b):
    return lax.dot_general(a, b, (((0,), (0,)), ((), ())), preferred_element_type=F32)


def _sigmoid(g):
    return 0.5 + 0.5 * jnp.tanh(0.5 * g)


def _split3(v):
    hi = v.astype(BF16)
    r1 = v - hi.astype(F32)
    mid = r1.astype(BF16)
    lo = (r1 - mid.astype(F32)).astype(BF16)
    return hi, mid, lo


def _rms(v):
    return lax.rsqrt(jnp.mean(v * v, axis=-1, keepdims=True) + EPS)


def _rms_bwd(a, r, g, dy):
    da = dy * g
    return r * (da - a * jnp.mean(da * a, axis=-1, keepdims=True)), dy * a


def _head_mean(z, seg):
    hi = z.astype(BF16)
    lo = (z - hi.astype(F32)).astype(BF16)
    if z.shape[1] == 256:
        return _dot(hi, seg) + _dot(lo, seg)
    half = seg[0:256, 0:256]
    return jnp.concatenate([_dot(hi[:, 0:256], half) + _dot(lo[:, 0:256], half),
                            _dot(hi[:, 256:512], half) + _dot(lo[:, 256:512], half)], axis=1)


def _head_rms(v, seg):
    return lax.rsqrt(_head_mean(v * v, seg) + EPS)


def _head_rms_bwd(a, r, g, dy, seg):
    da = dy * g
    return r * (da - a * _head_mean(da * a, seg)), dy * a


def _fold_heads(row, n_heads):
    out = row[:, 0:HEAD_DIM]
    for h in range(1, n_heads):
        out = out + row[:, HEAD_DIM * h:HEAD_DIM * (h + 1)]
    return out


def _params(sem, limit=VMEM_LIMIT):
    return pltpu.CompilerParams(dimension_semantics=sem, vmem_limit_bytes=limit)


def _full(shape):
    return pl.BlockSpec(shape, lambda *_: (0,) * len(shape))


def _chunk_segments(d):
    runs = []
    for lo, hi, shift in ((0, F_OFF, 0), (F_OFF, F_OFF + FOX_HEADS, MY_F_OFF - F_OFF), (F_OFF + FOX_HEADS, IN_WIDTH, -FOX_HEADS)):
        a, b = max(lo, IN_CHUNK * d), min(hi, IN_CHUNK * (d + 1))
        if a < b:
            runs.append((a + shift, b - a))
    return runs


def _my_place():
    x, y, c = lax.axis_index("x"), lax.axis_index("y"), lax.axis_index("c")
    return x, y, c, 4 * x + 2 * y + c


def _shard_rows(dev):
    return pl.ds(pl.multiple_of(128 * dev, 128), 128)


def _gather_w_in(w_in, w_kv, w_out, w_pool, b_f, gq, gk, gqm):
    def body(win_ref, wkv_ref, wout_ref, wp_ref, bf_ref, gq_ref, gk_ref, gqm_ref, wmy_ref, kv16_ref, out16_ref, wpbd_ref, bfpad_ref,
             gq8_ref, gk8_ref, gqm4_ref, in_all, pay_in, win_rows, send_sems, recv_sems, load_sem):
        x, y, c, me = _my_place()
        here, sibling = (x, y, c), (x, y, 1 - c)
        x_chip, y_chip, far_chip = (1 - x, y), (x, 1 - y), (1 - x, 1 - y)
        relay_from = (x ^ (1 - c), y ^ c)
        relay_to = (x ^ c, y ^ (1 - c))

        load = pltpu.make_async_copy(win_ref, win_rows, load_sem)
        load.start()
        load.wait()
        pay_in[...] = jnp.zeros((CHUNK_ROWS, D_MODEL), BF16)
        for j in range(8):
            pay_in[0:IN_CHUNK, 128 * j:128 * (j + 1)] = win_rows[pl.ds(j, IN_CHUNK, stride=8), :].astype(BF16)

        def copy(k, block, to, own=False):
            px, py, pc = block
            dst = in_all.at[4 * px + 2 * py + pc]
            return pltpu.make_async_remote_copy(src_ref=pay_in if own else dst, dst_ref=dst, send_sem=send_sems.at[k],
                                                recv_sem=recv_sems.at[k], device_id=to, device_id_type=MESH)

        first = [copy(0, here, sibling, own=True), copy(1, here, (*x_chip, c), own=True), copy(2, here, (*y_chip, c), own=True)]
        for cp in first:
            cp.start()
        in_all[me] = pay_in[...]
        kv16_ref[...] = wkv_ref[...].astype(BF16)
        out16_ref[...] = wout_ref[...].astype(BF16)
        wpbd_ref[...] = jnp.zeros((POOL_WIDTH, POOL_WIDTH), BF16)
        for grp in range(4):
            sl = slice(64 * grp, 64 * (grp + 1))
            wpbd_ref[sl, sl] = wp_ref[grp].astype(BF16)
        bfpad_ref[...] = jnp.zeros((1, 128), F32)
        bfpad_ref[:, 0:FOX_HEADS] = bf_ref[...]
        gq8_ref[...] = jnp.concatenate([gq_ref[...]] * FOX_HEADS, axis=1)
        gk8_ref[...] = jnp.concatenate([gk_ref[...]] * FOX_HEADS, axis=1)
        gqm4_ref[...] = jnp.concatenate([gqm_ref[...]] * MEM_HEADS, axis=1)
        copy(1 + c, (*relay_from, c), here).wait_recv()
        passed = [copy(3, (*relay_from, c), (*relay_to, c)), copy(4, (*relay_from, c), sibling)]
        for cp in passed:
            cp.start()
        copy(2 - c, (*relay_to, c), here).wait_recv()
        passed.append(copy(5, (*relay_to, c), sibling))
        passed[-1].start()
        copy(3, (*far_chip, c), here).wait_recv()
        passed.append(copy(6, (*far_chip, c), sibling))
        passed[-1].start()
        copy(0, sibling, here).wait_recv()
        for k, chip in ((4, relay_to), (5, relay_from), (6, far_chip)):
            copy(k, (*chip, 1 - c), here).wait_recv()
        for cp in first + passed:
            cp.wait_send()

        row_pad = jnp.zeros((512 - CHUNK_ROWS, 256), F32)
        for r0 in range(0, D_MODEL, 256):
            ch = [jnp.concatenate([in_all[d, :, r0:r0 + 256].astype(F32), row_pad], axis=0).T[:, 0:IN_CHUNK] for d in range(N_DEV)]
            lo = F_OFF - 5 * IN_CHUNK
            full = jnp.concatenate(ch[:5] + [ch[5][:, :lo], ch[5][:, lo + FOX_HEADS:], ch[6], ch[7], ch[5][:, lo:lo + FOX_HEADS],
                                             jnp.zeros((256, MY_WIDTH - IN_WIDTH), F32)], axis=1)
            wmy_ref[r0:r0 + 256, :] = full.astype(BF16)

    return pl.pallas_call(
        body,
        name="gather_w_in",
        out_shape=(jax.ShapeDtypeStruct((D_MODEL, MY_WIDTH), BF16), jax.ShapeDtypeStruct((128, 512), BF16),
                   jax.ShapeDtypeStruct((128, D_MODEL), BF16), jax.ShapeDtypeStruct((POOL_WIDTH, POOL_WIDTH), BF16),
                   jax.ShapeDtypeStruct((1, 128), F32), jax.ShapeDtypeStruct((1, PIECE), F32), jax.ShapeDtypeStruct((1, PIECE), F32),
                   jax.ShapeDtypeStruct((1, 256), F32)),
        in_specs=[pl.BlockSpec(memory_space=pl.ANY)] + [pl.BlockSpec(memory_space=pltpu.VMEM)] * 7,
        out_specs=(pl.BlockSpec(memory_space=pltpu.VMEM),) * 8,
        scratch_shapes=[
            pltpu.VMEM((N_DEV, CHUNK_ROWS, D_MODEL), BF16),
            pltpu.VMEM((CHUNK_ROWS, D_MODEL), BF16),
            pltpu.VMEM((IN_CHUNK * 8, 128), F32),
            pltpu.SemaphoreType.DMA((7,)),
            pltpu.SemaphoreType.DMA((7,)),
            pltpu.SemaphoreType.DMA,
        ],
        compiler_params=pltpu.CompilerParams(vmem_limit_bytes=VMEM_LIMIT),
    )(w_in, w_kv, w_out, w_pool, b_f, gq, gk, gqm)


def _row_block_exchange(kv_ref, out_ref, kv_dst, out_dst, send_sems, recv_sems, local_sems, gather):
    x, y, c, me = _my_place()
    remote = []
    for k in range(1, N_DEV):
        px, py, pc = x ^ (k >> 2), y ^ ((k >> 1) & 1), c ^ (k & 1)
        peer = 4 * px + 2 * py + pc
        for fam, (src, dst) in enumerate(((kv_ref, kv_dst), (out_ref, out_dst))):
            remote.append(pltpu.make_async_remote_copy(
                src_ref=src if gather else src.at[_shard_rows(peer)], dst_ref=dst.at[_shard_rows(me)] if gather else dst.at[me],
                send_sem=send_sems.at[7 * fam + k - 1], recv_sem=recv_sems.at[7 * fam + k - 1],
                device_id=(px, py, pc), device_id_type=MESH))
    local = [pltpu.make_async_copy(src if gather else src.at[_shard_rows(me)], dst.at[_shard_rows(me)] if gather else dst.at[me],
                                   local_sems.at[fam]) for fam, (src, dst) in enumerate(((kv_ref, kv_dst), (out_ref, out_dst)))]
    return remote, local


SB_ROWS, SB_W = 80, 256
SB_NORM_G, SB_MEM_NORM_G, SB_POOL_SCALE, SB_B_F, SB_FOX_Q, SB_FOX_K, SB_MEM_Q, SB_MEM_K, SB_LOSS, SB_W_POOL = 0, 4, 8, 9, 10, 11, 12, 13, 14, 16


def _alternate(*parts):
    state = [[part, 0, stages] for part, stages in parts]
    while state:
        least = min(state, key=lambda entry: entry[1] / entry[2])
        least[1] += 1
        if next(least[0], "done") == "done":
            state.remove(least)


def _log_sigmoid(z):
    return jnp.minimum(z, 0.0) - jnp.log(1.0 + jnp.exp(-jnp.abs(z)))


def _forget_lane_maps():
    to_q = np.zeros((128, FOX_HEADS * 128), np.float32)
    to_k = np.zeros((128, FOX_HEADS * 128), np.float32)
    for h in range(FOX_HEADS):
        for term in range(3):
            to_q[8 * term + h, 128 * h + 64 + term] = 1.0
            to_q[24, 128 * h + 67 + term] = 1.0
            to_k[24, 128 * h + 64 + term] = 1.0
            to_k[8 * term + h, 128 * h + 67 + term] = -1.0
    return jnp.asarray(to_q, BF16), jnp.asarray(to_k, BF16)


def _proj_prep(x, norm_g, w_my, bf_pad, gq, gk, seg):
    s_len = x.shape[0]
    n_blocks = s_len // TMM
    halves = TMM // TM
    to_q, to_k = _forget_lane_maps()
    q_off, f_off = PIECE, MY_F_OFF
    kept = 3 * PIECE + 128

    def body(x_ref, g_ref, w_ref, bf_ref, gq_ref, gk_ref, seg_ref, eq_ref, ek_ref,
             proj_ref, h_ref, qa_ref, ka_ref, vh_ref, kt_ref, vt_ref, kept_even, kept_odd, carry_ref):
        s = pl.program_id(0)

        def project(kept_ref):
            xv = x_ref[...]
            h_ref[...] = (xv * _rms(xv) * g_ref[...]).astype(BF16)
            yield
            for c0 in range(0, MY_WIDTH, 256):
                c1 = min(c0 + 256, MY_WIDTH)
                res = _dot(h_ref[...], w_ref[:, c0:c1])
                proj_ref[:, c0:c1] = res
                if q_off <= c0 < q_off + 3 * PIECE:
                    kept_ref[:, c0 - q_off:c1 - q_off] = res
                if c0 == f_off:
                    kept_ref[:, 3 * PIECE:kept] = res
                yield

        def operands(kept_ref):
            lane = lax.broadcasted_iota(jnp.int32, (TM, 128), 1)
            row = lax.broadcasted_iota(jnp.int32, (TM, TM), 0)
            col = lax.broadcasted_iota(jnp.int32, (TM, TM), 1)
            tri = jnp.where(col <= row, 1.0, 0.0).astype(BF16)
            one_at_64 = jnp.where(lane == 64, 1.0, 0.0)
            zeros = jnp.zeros((TM, HEAD_DIM), F32)
            for half in range(halves):
                rows = slice(TM * half, TM * (half + 1))
                logf = jnp.where(lane < FOX_HEADS, _log_sigmoid(kept_ref[rows, 3 * PIECE:kept] + bf_ref[...]), 0.0)
                hi, mid, lo = _split3(logf)
                cum = _dot(tri, hi) + _dot(tri, mid) + _dot(tri, lo) + carry_ref[...]
                q_all = kept_ref[rows, 0:PIECE]
                k_all = kept_ref[rows, PIECE:2 * PIECE]
                rq = _head_rms(q_all, seg_ref[...])
                rk = _head_rms(k_all, seg_ref[...])
                yield
                carry_ref[...] = cum[TM - 1:TM, :]
                f_hi, f_mid, f_lo = [t.astype(F32) for t in _split3(cum)]
                packed = (f_hi + pltpu.roll(f_mid, 8, 1) + pltpu.roll(f_lo, 16, 1)
                          + jnp.where(lane == 24, 1.0, 0.0)).astype(BF16)
                extra_q = _dot(packed, eq_ref[...])
                extra_k = _dot(packed, ek_ref[...])
                qn_all = q_all * rq * gq_ref[...] * 0.125
                kn_all = k_all * rk * gk_ref[...]
                yield
                for h in range(FOX_HEADS):
                    sl = slice(HEAD_DIM * h, HEAD_DIM * (h + 1))
                    tile = slice(128 * h, 128 * (h + 1))
                    qa = jnp.where(lane < 64, jnp.concatenate([qn_all[:, sl], zeros], axis=1), extra_q[:, tile])
                    ka = jnp.where(lane < 64, jnp.concatenate([kn_all[:, sl], zeros], axis=1), extra_k[:, tile])
                    vh = kept_ref[rows, 2 * PIECE + HEAD_DIM * h:2 * PIECE + HEAD_DIM * (h + 1)]
                    v_aug = jnp.where(lane < 64, jnp.concatenate([vh, zeros], axis=1), one_at_64)
                    qa_ref[h, rows, :] = qa.astype(BF16)
                    ka_ref[h, rows, :] = ka.astype(BF16)
                    vh_ref[h, rows, :] = vh.astype(BF16)
                    kt_ref[h, half] = ka.T.astype(BF16)
                    vt_ref[h, half] = v_aug.T.astype(BF16)
                    if h % 2 == 1:
                        yield

        projecting = lambda kept_ref: (project(kept_ref), 1 + pl.cdiv(MY_WIDTH, 256))
        making = lambda kept_ref: (operands(kept_ref), halves * (2 + FOX_HEADS // 2))

        @pl.when(s == 0)
        def _():
            carry_ref[...] = jnp.zeros((1, 128), F32)
            _alternate(projecting(kept_even))

        @pl.when((s > 0) & (s < n_blocks) & (s % 2 == 1))
        def _():
            _alternate(projecting(kept_odd), making(kept_even))

        @pl.when((s > 0) & (s < n_blocks) & (s % 2 == 0))
        def _():
            _alternate(projecting(kept_even), making(kept_odd))

        @pl.when(s == n_blocks)
        def _():
            _alternate(making(kept_odd if n_blocks % 2 == 0 else kept_even))

    made = lambda s: jnp.minimum(s, n_blocks - 1)
    used = lambda s: jnp.maximum(s - 1, 0)
    head_blk = lambda w: pl.BlockSpec((FOX_HEADS, TMM, w), lambda s: (0, used(s), 0))
    tile_blk = pl.BlockSpec((FOX_HEADS, halves, 128, TM), lambda s: (0, used(s), 0, 0))
    tiled = jax.ShapeDtypeStruct((FOX_HEADS, s_len // TM, 128, TM), BF16)
    return pl.pallas_call(
        body,
        name="proj_prep",
        grid=(n_blocks + 1,),
        in_specs=[pl.BlockSpec((TMM, D_MODEL), lambda s: (made(s), 0)), _full((1, D_MODEL)), _full((D_MODEL, MY_WIDTH)),
                  _full((1, 128)), _full((1, PIECE)), _full((1, PIECE)), _full((PIECE, PIECE)),
                  _full((128, FOX_HEADS * 128)), _full((128, FOX_HEADS * 128))],
        out_specs=[pl.BlockSpec((TMM, MY_WIDTH), lambda s: (made(s), 0)), pl.BlockSpec((TMM, D_MODEL), lambda s: (made(s), 0)),
                   head_blk(128), head_blk(128), head_blk(HEAD_DIM), tile_blk, tile_blk],
        out_shape=[jax.ShapeDtypeStruct((s_len, MY_WIDTH), F32), jax.ShapeDtypeStruct((s_len, D_MODEL), BF16),
                   jax.ShapeDtypeStruct((FOX_HEADS, s_len, 128), BF16), jax.ShapeDtypeStruct((FOX_HEADS, s_len, 128), BF16),
                   jax.ShapeDtypeStruct((FOX_HEADS, s_len, HEAD_DIM), BF16), tiled, tiled],
        scratch_shapes=[pltpu.VMEM((TMM, kept), F32), pltpu.VMEM((TMM, kept), F32), pltpu.VMEM((1, 128), F32)],
        compiler_params=_params(("arbitrary",)),
    )(x, norm_g, w_my, bf_pad, gq, gk, seg, to_q, to_k)


def _mem_prep(mem, g, w_kv, gk):
    def body(mem_ref, g_ref, w_ref, gk_ref, km_ref, vm_ref):
        m = mem_ref[...]
        kv = _dot((m * _rms(m) * g_ref[...]).astype(BF16), w_ref[...])
        for h in range(MEM_HEADS):
            sl = slice(HEAD_DIM * h, HEAD_DIM * (h + 1))
            kh = kv[:, sl]
            km_ref[:, sl] = (kh * _rms(kh) * gk_ref[...]).astype(BF16)
        vm_ref[...] = kv[:, 256:512].astype(BF16)

    return pl.pallas_call(
        body,
        name="mem_prep",
        out_shape=(jax.ShapeDtypeStruct((N_MEM, 256), BF16),) * 2,
        in_specs=[pl.BlockSpec(memory_space=pltpu.VMEM)] * 4,
        out_specs=(pl.BlockSpec(memory_space=pltpu.VMEM),) * 2,
        compiler_params=pltpu.CompilerParams(vmem_limit_bytes=VMEM_LIMIT),
    )(mem, g, w_kv, gk)


def _causal_mask_t():
    row = lax.broadcasted_iota(jnp.int32, (TA, TA), 0)
    col = lax.broadcasted_iota(jnp.int32, (TA, TA), 1)
    return row <= col


def _fox_fwd(q_aug, k_aug, vt_aug, w_kv16, w_out16):
    s_len = q_aug.shape[1]
    n_tiles = s_len // TA
    hb = HB_FWD
    n_groups = FOX_HEADS // hb

    def body(q_ref, k_new, vt_new, kv16_ref, out16_ref, o_ref, st_ref, kv_all, out_all, k_ref, vt_ref, send_sems, recv_sems, local_sems):
        qi = pl.program_id(1)
        for h in range(hb):
            k_ref[h, pl.ds(pl.multiple_of(qi * TA, TA), TA), :] = k_new[h]
            vt_ref[h, qi] = vt_new[h, 0]
        remote, local = _row_block_exchange(kv16_ref, out16_ref, kv_all, out_all, send_sems, recv_sems, local_sems, gather=True)

        @pl.when((pl.program_id(0) == 0) & (qi == 0))
        def _():
            for cp in remote + local:
                cp.start()

        qs = [q_ref[h] for h in range(hb)]

        def step(t0, carry, masked, width):
            rows = pl.ds(pl.multiple_of(t0 * TA, TA), width * TA)
            scores = [_dot_nt(k_ref[h, rows, :], qs[h]) for h in range(hb)]
            soft = []
            for h in range(hb):
                m, _ = carry[h]
                s = jnp.where(_causal_mask_t(), scores[h], -1e30) if masked else scores[h]
                m_new = jnp.maximum(m, jnp.max(s, axis=0, keepdims=True))
                soft.append((m_new, jnp.exp(m - m_new), jnp.exp(s - m_new).astype(BF16)))
            out = []
            for h, (m_new, alpha, p) in enumerate(soft):
                acc = alpha * carry[h][1]
                for t in range(width):
                    acc = acc + _dot(vt_ref[h, t0 + t, 0:AUG_ROWS, :], p[t * TA:(t + 1) * TA])
                out.append((m_new, acc))
            return tuple(out)

        init = tuple((jnp.full((1, TA), -1e30, F32), jnp.zeros((AUG_ROWS, TA), F32)) for _ in range(hb))
        carry = lax.fori_loop(0, qi // 2, lambda j, cr: step(2 * j, cr, False, 2), init)
        carry = lax.fori_loop(2 * (qi // 2), qi, lambda j, cr: step(j, cr, False, 1), carry)
        carry = step(qi, carry, True, 1)
        for h in range(hb):
            m, acc = carry[h]
            l = acc[HEAD_DIM:HEAD_DIM + 1, :]
            lse = m + jnp.log(l)
            o_ref[h] = jnp.concatenate([acc[0:HEAD_DIM] / l, jnp.broadcast_to(lse, (HEAD_DIM, TA))], axis=0).T
            st_ref[h, 0] = jnp.broadcast_to(lse, (8, TA))

        @pl.when((pl.program_id(0) == n_groups - 1) & (qi == n_tiles - 1))
        def _():
            for cp in remote:
                cp.wait_recv()
            for cp in remote:
                cp.wait_send()
            for cp in local:
                cp.wait()

    hbm = pl.BlockSpec(memory_space=pl.ANY)
    return pl.pallas_call(
        body,
        name="fox_fwd",
        grid=(n_groups, n_tiles),
        in_specs=[pl.BlockSpec((hb, TA, 128), lambda g, i: (g, i, 0)), pl.BlockSpec((hb, TA, 128), lambda g, i: (g, i, 0)),
                  pl.BlockSpec((hb, 1, 128, TA), lambda g, i: (g, i, 0, 0)), hbm, hbm],
        out_specs=[pl.BlockSpec((hb, TA, 128), lambda g, i: (g, i, 0)), pl.BlockSpec((hb, 1, 8, TA), lambda g, i: (g, i, 0, 0)),
                   hbm, hbm],
        out_shape=[jax.ShapeDtypeStruct((FOX_HEADS, s_len, 128), F32), jax.ShapeDtypeStruct((FOX_HEADS, n_tiles, 8, TA), F32),
                   jax.ShapeDtypeStruct((D_MODEL, 512), BF16), jax.ShapeDtypeStruct((D_MODEL, D_MODEL), BF16)],
        scratch_shapes=[pltpu.VMEM((hb, s_len, 128), BF16), pltpu.VMEM((hb, n_tiles, 128, TA), BF16),
                        pltpu.SemaphoreType.DMA((14,)), pltpu.SemaphoreType.DMA((14,)), pltpu.SemaphoreType.DMA((2,))],
        compiler_params=_params(("arbitrary", "arbitrary")),
    )(q_aug, k_aug, vt_aug, w_kv16, w_out16)


def _lane_group(lane, a, b, c, d):
    return jnp.where(lane < 64, a, jnp.where(lane < 128, b, jnp.where(lane < 192, c, d)))


def _pool_count(row0):
    lane = lax.broadcasted_iota(jnp.int32, (TM, POOL_WIDTH), 1)
    t1 = row0 + lax.broadcasted_iota(jnp.int32, (TM, POOL_WIDTH), 0) + 1
    return 1.0 / jnp.minimum(t1, _lane_group(lane, 2, 4, 8, 16)).astype(F32), lane


def _pool_delta(u, halo, cnt, lane):
    xe = jnp.concatenate([halo, u], axis=0)
    s2 = xe + pltpu.roll(xe, 1, 0)
    s4 = s2 + pltpu.roll(s2, 2, 0)
    s8 = s4 + pltpu.roll(s4, 4, 0)
    s16 = s8 + pltpu.roll(s8, 8, 0)
    win = _lane_group(lane, s2[HALO:], s4[HALO:], s8[HALO:], s16[HALO:])
    return win * cnt - u


def _pool_delta_bwd(dd, dd_next, cnt, cnt_next, lane):
    ee = jnp.concatenate([dd * cnt, dd_next * cnt_next], axis=0)
    n = TM + HALO
    r2 = ee + pltpu.roll(ee, n - 1, 0)
    r4 = r2 + pltpu.roll(r2, n - 2, 0)
    r8 = r4 + pltpu.roll(r4, n - 4, 0)
    r16 = r8 + pltpu.roll(r8, n - 8, 0)
    return _lane_group(lane, r2[:TM], r4[:TM], r8[:TM], r16[:TM]) - dd


def _mem_attn(qm, gq, seg, km_ref, vm_ref):
    r = _head_rms(qm, seg)
    a = qm * r
    q16 = (a * gq * 0.125).astype(BF16)
    heads = []
    for h in range(MEM_HEADS):
        sl = slice(HEAD_DIM * h, HEAD_DIM * (h + 1))
        s = _dot_nt(q16[:, sl], km_ref[:, sl])
        e = jnp.exp(s - jnp.max(s, axis=-1, keepdims=True))
        p = e * (1.0 / jnp.sum(e, axis=-1, keepdims=True))
        heads.append((p, _dot(p.astype(BF16), vm_ref[:, sl])))
    return a, r, q16, heads


def _mem_attn_stages(qm, gq, seg, km_ref, vm_ref, result):
    r = _head_rms(qm, seg)
    a = qm * r
    q16 = (a * gq * 0.125).astype(BF16)
    yield
    head_slices = [slice(HEAD_DIM * h, HEAD_DIM * (h + 1)) for h in range(MEM_HEADS)]
    scores = [_dot_nt(q16[:, sl], km_ref[:, sl]) for sl in head_slices]
    yield
    heads = []
    for s, sl in zip(scores, head_slices):
        e = jnp.exp(s - jnp.max(s, axis=-1, keepdims=True))
        p = e * (1.0 / jnp.sum(e, axis=-1, keepdims=True))
        heads.append((p, _dot(p.astype(BF16), vm_ref[:, sl])))
    result.extend([a, r, q16, heads])
    yield


def _mix_out_loss(proj, olse, km, vm, wp_bd, pool_scale, gq_m, seg, x, target, w_out):
    s_len = x.shape[0]
    n_blocks = s_len // TMM
    halves = TMM // TM

    def body(ua_ref, halo_ref, gb_ref, qm_ref, ol_ref, km_ref, vm_ref, wp_ref, sc_ref, gq_ref, seg_ref, x_ref, t_ref, w_ref,
             dout_ref, dmix_ref, dw16_ref, loss_ref, mixed_even, mixed_odd, d16_ref, dw_ref):
        s = pl.program_id(0)
        chunks = [slice(c0, c0 + 256) for c0 in range(0, D_MODEL, 256)]

        def matmuls(mixed_ref):
            for cols in chunks:
                err = x_ref[:, cols] + _dot(mixed_ref[...], w_ref[:, cols]) - t_ref[:, cols]
                loss_ref[...] += (0.5 / D_MODEL) * jnp.sum(err * err)
                d_out = err / float(D_MODEL)
                dout_ref[:, cols] = d_out
                d16_ref[:, cols] = d_out.astype(BF16)
                yield
            for cols in chunks:
                dmix_ref[:, cols] = _dot_nt(d16_ref[...], w_ref[cols, :])
                yield
            for cols in chunks:
                dw_ref[:, cols] += _dot_tn(mixed_ref[...], d16_ref[:, cols])
                yield

        def concatenation(mixed_ref):
            for half in range(halves):
                r0 = TM * half
                rows = slice(r0, r0 + TM)
                u = ua_ref[rows, 0:256]
                g_a = ua_ref[rows, 256:512]
                halo = jnp.where(s > 0, halo_ref[...], 0.0) if half == 0 else ua_ref[r0 - HALO:r0, 0:256]
                cnt, lane = _pool_count(s * TMM + r0)
                d = _pool_delta(u, halo, cnt, lane).astype(BF16)
                y_a = _dot(d, wp_ref[...]) * sc_ref[...]
                mixed_ref[rows, 0:256] = (y_a * (g_a * _sigmoid(g_a))).astype(BF16)
                yield
                g_b = gb_ref[rows, :]
                y_b = jnp.concatenate([ol_ref[h, rows, 0:HEAD_DIM] for h in range(FOX_HEADS)], axis=1)
                mixed_ref[rows, 256:768] = (y_b * (g_b * _sigmoid(g_b))).astype(BF16)
                yield
                attn = []
                stages = _mem_attn_stages(qm_ref[rows, 0:256], gq_ref[...], seg_ref[0:256, 0:256], km_ref, vm_ref, attn)
                next(stages)
                yield
                next(stages)
                yield
                next(stages)
                g_m = qm_ref[rows, 256:512]
                y_m = jnp.concatenate([y for _, y in attn[3]], axis=1)
                mixed_ref[rows, 768:1024] = (y_m * (g_m * _sigmoid(g_m))).astype(BF16)
                yield

        multiplying = lambda mixed_ref: (matmuls(mixed_ref), 3 * len(chunks))
        building = lambda mixed_ref: (concatenation(mixed_ref), 5 * halves)

        @pl.when(s == 0)
        def _():
            dw_ref[...] = jnp.zeros((D_MODEL, D_MODEL), F32)
            loss_ref[...] = jnp.zeros((8, 128), F32)
            _alternate(building(mixed_even))

        @pl.when((s > 0) & (s < n_blocks) & (s % 2 == 1))
        def _():
            _alternate(multiplying(mixed_even), building(mixed_odd))

        @pl.when((s > 0) & (s < n_blocks) & (s % 2 == 0))
        def _():
            _alternate(multiplying(mixed_odd), building(mixed_even))

        @pl.when(s == n_blocks)
        def _():
            _alternate(multiplying(mixed_odd if n_blocks % 2 == 0 else mixed_even))
            dw16_ref[...] = dw_ref[...].astype(BF16)

    build = lambda s: jnp.minimum(s, n_blocks - 1)
    use = lambda s: jnp.maximum(s - 1, 0)
    piece = lambda j: pl.BlockSpec((TMM, PIECE), lambda s: (build(s), j))
    halo_blk = pl.BlockSpec((HALO, 256), lambda s: (jnp.maximum(build(s) * (TMM // HALO) - 1, 0), 0))
    row = pl.BlockSpec((TMM, D_MODEL), lambda s: (use(s), 0))
    return pl.pallas_call(
        body,
        name="mix_out_loss",
        grid=(n_blocks + 1,),
        in_specs=[piece(0), halo_blk, piece(4), piece(5), pl.BlockSpec((FOX_HEADS, TMM, 128), lambda s: (0, build(s), 0)),
                  _full((N_MEM, 256)), _full((N_MEM, 256)), _full((256, 256)), _full((1, 256)), _full((1, 256)),
                  _full((PIECE, PIECE)), row, row, _full((D_MODEL, D_MODEL))],
        out_specs=[row, row, _full((D_MODEL, D_MODEL)), _full((8, 128))],
        out_shape=[jax.ShapeDtypeStruct((s_len, D_MODEL), F32), jax.ShapeDtypeStruct((s_len, D_MODEL), F32),
                   jax.ShapeDtypeStruct((D_MODEL, D_MODEL), BF16), jax.ShapeDtypeStruct((8, 128), F32)],
        scratch_shapes=[pltpu.VMEM((TMM, D_MODEL), BF16), pltpu.VMEM((TMM, D_MODEL), BF16), pltpu.VMEM((TMM, D_MODEL), BF16),
                        pltpu.VMEM((D_MODEL, D_MODEL), F32)],
        compiler_params=_params(("arbitrary",)),
    )(proj, proj, proj, proj, olse, km, vm, wp_bd, pool_scale, gq_m, seg, x, target, w_out)


def _silu_pair(g):
    s = _sigmoid(g)
    return g * s, s * (1.0 + g * (1.0 - s))


def _mix_bwd(proj, olse, d_mixed, km, vm, wp_bd, pool_scale, gq_m, seg):
    s_len = proj.shape[0]
    n_tiles = s_len // TM

    def body(ua_ref, halo_ref, ga_next_ref, gb_ref, qm_ref, ol_ref, dm_ref, dm_next_ref, km_ref, vm_ref, wp_ref, sc_ref, gq_ref,
             seg_ref, dua_ref, dgb_ref, dqm_ref, do_ref, dl_ref, dwp_ref, dsc_ref, dkm_ref, dvm_ref, dgq_ref):
        i = pl.program_id(0)

        @pl.when(i == 0)
        def _():
            dwp_ref[...] = jnp.zeros((256, 256), F32)
            dsc_ref[...] = jnp.zeros((1, 256), F32)
            dkm_ref[...] = jnp.zeros((N_MEM, 256), F32)
            dvm_ref[...] = jnp.zeros((N_MEM, 256), F32)
            dgq_ref[...] = jnp.zeros((1, HEAD_DIM), F32)

        def pooling_mixer():
            u = ua_ref[:, 0:256]
            g_a = ua_ref[:, 256:512]
            halo = jnp.where(i > 0, halo_ref[...], 0.0)
            cnt, lane = _pool_count(i * TM)
            d16 = _pool_delta(u, halo, cnt, lane).astype(BF16)
            t = _dot(d16, wp_ref[...])
            yield
            y_a = t * sc_ref[...]
            silu_a, dsilu_a = _silu_pair(g_a)
            dm_a = dm_ref[:, 0:256]
            dy_a = dm_a * silu_a
            dsc_ref[...] += jnp.sum(dy_a * t, axis=0, keepdims=True)
            dt16 = (dy_a * sc_ref[...]).astype(BF16)
            dwp_ref[...] += _dot_tn(d16, dt16)
            dd = _dot_nt(dt16, wp_ref[...])
            dua_ref[:, 256:512] = (dm_a * y_a * dsilu_a).astype(BF16)
            yield
            g_next = ga_next_ref[...]
            dt_next = (dm_next_ref[...] * (g_next * _sigmoid(g_next)) * sc_ref[...]).astype(BF16)
            dd_next = jnp.where(i < n_tiles - 1, _dot_nt(dt_next, wp_ref[...]), 0.0)
            cnt_next = _pool_count((i + 1) * TM)[0][0:HALO]
            dua_ref[:, 0:256] = _pool_delta_bwd(dd, dd_next, cnt, cnt_next, lane).astype(BF16)
            yield

        def output_gate():
            silu_b, dsilu_b = _silu_pair(gb_ref[...])
            dm_b = dm_ref[:, 256:768]
            o_all = jnp.concatenate([ol_ref[h][:, 0:HEAD_DIM] for h in range(FOX_HEADS)], axis=1)
            d_o = dm_b * silu_b
            dgb_ref[...] = (dm_b * o_all * dsilu_b).astype(BF16)
            for h in range(FOX_HEADS):
                do_ref[h] = d_o[:, HEAD_DIM * h:HEAD_DIM * (h + 1)].astype(BF16)
            yield
            prod = d_o * o_all
            hi = prod.astype(BF16)
            lo = (prod - hi.astype(F32)).astype(BF16)
            head_of_lane = lax.broadcasted_iota(jnp.int32, (FOX_HEADS, PIECE), 1) // HEAD_DIM
            pick = jnp.where(head_of_lane == lax.broadcasted_iota(jnp.int32, (FOX_HEADS, PIECE), 0), 1.0, 0.0).astype(BF16)
            delta = _dot_nt(pick, hi) + _dot_nt(pick, lo)
            for h in range(FOX_HEADS):
                dl_ref[h, 0] = jnp.broadcast_to(delta[h:h + 1, :], (8, TM))
            yield

        def memory_attention():
            seg = seg_ref[0:256, 0:256]
            a, r, q16, heads = _mem_attn(qm_ref[:, 0:256], gq_ref[...], seg, km_ref, vm_ref)
            yield
            silu_m, dsilu_m = _silu_pair(qm_ref[:, 256:512])
            dm_m = dm_ref[:, 768:1024]
            y_m = jnp.concatenate([y for _, y in heads], axis=1)
            dqm_ref[:, 256:512] = (dm_m * y_m * dsilu_m).astype(BF16)
            dy16_all = (dm_m * silu_m).astype(BF16)
            d_scores = []
            for h in range(MEM_HEADS):
                sl = slice(HEAD_DIM * h, HEAD_DIM * (h + 1))
                p = heads[h][0]
                dy16 = dy16_all[:, sl]
                dp = _dot_nt(dy16, vm_ref[:, sl])
                dvm_ref[:, sl] += _dot_tn(p.astype(BF16), dy16)
                ds16 = (p * (dp - jnp.sum(dp * p, axis=-1, keepdims=True))).astype(BF16)
                dkm_ref[:, sl] += _dot_tn(ds16, q16[:, sl])
                d_scores.append(_dot(ds16, km_ref[:, sl]))
                yield
            dq, dg_rows = _head_rms_bwd(a, r, gq_ref[...], jnp.concatenate(d_scores, axis=1) * 0.125, seg)
            dqm_ref[:, 0:256] = dq.astype(BF16)
            dgq_ref[...] += _fold_heads(jnp.sum(dg_rows, axis=0, keepdims=True), MEM_HEADS)
            yield

        _alternate((pooling_mixer(), 3))
        _alternate((memory_attention(), 2 + MEM_HEADS), (output_gate(), 2))

    n_halo = s_len // HALO
    piece = lambda j: pl.BlockSpec((TM, PIECE), lambda i: (i, j))
    before = pl.BlockSpec((HALO, 256), lambda i: (jnp.maximum(i * (TM // HALO) - 1, 0), 0))
    after = lambda j: pl.BlockSpec((HALO, 256), lambda i: (jnp.minimum((i + 1) * (TM // HALO), n_halo - 1), j))
    row = pl.BlockSpec((TM, D_MODEL), lambda i: (i, 0))
    out_piece = pl.BlockSpec((TM, PIECE), lambda i: (i, 0))
    return pl.pallas_call(
        body,
        name="mix_bwd",
        grid=(n_tiles,),
        in_specs=[piece(0), before, after(1), piece(4), piece(5), pl.BlockSpec((FOX_HEADS, TM, 128), lambda i: (0, i, 0)),
                  row, after(0), _full((N_MEM, 256)), _full((N_MEM, 256)), _full((256, 256)), _full((1, 256)), _full((1, 256)),
                  _full((PIECE, PIECE))],
        out_specs=[out_piece, out_piece, out_piece, pl.BlockSpec((FOX_HEADS, TM, HEAD_DIM), lambda i: (0, i, 0)),
                   pl.BlockSpec((FOX_HEADS, 1, 8, TM), lambda i: (0, i, 0, 0)),
                   _full((256, 256)), _full((1, 256)), _full((N_MEM, 256)), _full((N_MEM, 256)), _full((1, HEAD_DIM))],
        out_shape=[jax.ShapeDtypeStruct((s_len, PIECE), BF16)] * 3 + [
            jax.ShapeDtypeStruct((FOX_HEADS, s_len, HEAD_DIM), BF16),
            jax.ShapeDtypeStruct((FOX_HEADS, n_tiles, 8, TM), F32),
            jax.ShapeDtypeStruct((256, 256), F32), jax.ShapeDtypeStruct((1, 256), F32),
            jax.ShapeDtypeStruct((N_MEM, 256), F32), jax.ShapeDtypeStruct((N_MEM, 256), F32),
            jax.ShapeDtypeStruct((1, HEAD_DIM), F32)],
        compiler_params=_params(("arbitrary",)),
    )(proj, proj, proj, proj, proj, olse, d_mixed, d_mixed, km, vm, wp_bd, pool_scale, gq_m, seg)


def _fox_bwd(q_aug, k_aug, kt_aug, v_h, d_o, lse_rows, delta_rows, dw_kv16, dw_out16):
    s_len = q_aug.shape[1]
    n_tiles = s_len // TA
    n_groups = FOX_HEADS // HB

    def body(q_hbm, k_ref, kt_ref, v_ref, do_hbm, st_ref, dl_ref, dkv16_ref, dout16_ref, dq_ref, dk_ref, dv_ref, land_kv, land_out,
             dqt_ref, q_ref, do_ref, send_sems, recv_sems, local_sems, tile_sems):
        j = pl.program_id(1)
        remote, local = _row_block_exchange(dkv16_ref, dout16_ref, land_kv, land_out, send_sems, recv_sems, local_sems, gather=False)

        def tile_loads(i):
            rows = pl.ds(pl.multiple_of(i * TA, TA), TA)
            return [pltpu.make_async_copy(q_hbm.at[:, rows, :], q_ref.at[:, rows, :], tile_sems.at[2 * i]),
                    pltpu.make_async_copy(do_hbm.at[:, rows, :], do_ref.at[:, rows, :], tile_sems.at[2 * i + 1])]

        @pl.when((pl.program_id(0) == 0) & (j == 0))
        def _():
            for i in range(n_tiles):
                for cp in tile_loads(i):
                    cp.start()
            for cp in remote + local:
                cp.start()

        @pl.when(j == 0)
        def _():
            dqt_ref[...] = jnp.zeros((HB, n_tiles, AUG_ROWS, TA), F32)

        ks = [k_ref[h] for h in range(HB)]
        kts = [kt_ref[h, 0, 0:AUG_ROWS, :] for h in range(HB)]
        vs = [v_ref[h] for h in range(HB)]

        def pair(i0, acc, masked, width):
            @pl.when(j == 0)
            def _():
                for t in range(width):
                    for cp in tile_loads(i0 + t):
                        cp.wait()

            rows = pl.ds(pl.multiple_of(i0 * TA, TA), width * TA)
            qs = [q_ref[h, rows, :] for h in range(HB)]
            d_os = [do_ref[h, rows, :] for h in range(HB)]
            row_stat = lambda ref, h: jnp.concatenate([ref[h, i0 + t, 0:1, :] for t in range(width)], axis=1)
            scores = [(_dot_nt(ks[h], qs[h]), _dot_nt(vs[h], d_os[h])) for h in range(HB)]
            probs = []
            for h in range(HB):
                s, dp = scores[h]
                if masked:
                    s = jnp.where(_causal_mask_t(), s, -1e30)
                p = jnp.exp(s - row_stat(st_ref, h))
                probs.append((p.astype(BF16), (p * (dp - row_stat(dl_ref, h))).astype(BF16)))
            out = []
            for h in range(HB):
                p16, ds16 = probs[h]
                dqt = _dot(kts[h], ds16)
                for t in range(width):
                    dqt_ref[h, i0 + t] += dqt[:, t * TA:(t + 1) * TA]
                out.append((acc[h][0] + _dot(ds16, qs[h]), acc[h][1] + _dot(p16, d_os[h])))
            return tuple(out)

        zero = tuple((jnp.zeros((TA, 128), F32), jnp.zeros((TA, HEAD_DIM), F32)) for _ in range(HB))
        acc = pair(j, zero, True, 1)
        odd = (n_tiles - 1 - j) % 2
        acc = lax.fori_loop(j + 1, j + 1 + odd, lambda i, a: pair(i, a, False, 1), acc)
        acc = lax.fori_loop(0, (n_tiles - 1 - j) // 2, lambda t, a: pair(j + 1 + odd + 2 * t, a, False, 2), acc)
        pad = jnp.zeros((128 - AUG_ROWS, TA), F32)
        for h in range(HB):
            dk_ref[h] = acc[h][0]
            dv_ref[h] = acc[h][1].astype(BF16)
            dq_ref[h] = jnp.concatenate([dqt_ref[h, j], pad], axis=0).T

        @pl.when((pl.program_id(0) == n_groups - 1) & (j == n_tiles - 1))
        def _():
            for cp in remote:
                cp.wait_recv()
            for cp in remote:
                cp.wait_send()
            for cp in local:
                cp.wait()

    assert n_groups == 1
    resident = pl.BlockSpec(memory_space=pltpu.VMEM)
    rows_blk = lambda r: resident
    tile = lambda w: pl.BlockSpec((HB, TA, w), lambda g, j: (g, j, 0))
    hbm = pl.BlockSpec(memory_space=pl.ANY)
    return pl.pallas_call(
        body,
        name="fox_bwd",
        grid=(n_groups, n_tiles),
        in_specs=[hbm, tile(128), pl.BlockSpec((HB, 1, 128, TA), lambda g, j: (g, j, 0, 0)), tile(HEAD_DIM),
                  hbm, rows_blk(8), rows_blk(8), hbm, hbm],
        out_specs=[tile(128), tile(128), tile(HEAD_DIM), hbm, hbm],
        out_shape=[jax.ShapeDtypeStruct((FOX_HEADS, s_len, 128), F32), jax.ShapeDtypeStruct((FOX_HEADS, s_len, 128), F32),
                   jax.ShapeDtypeStruct((FOX_HEADS, s_len, HEAD_DIM), BF16),
                   jax.ShapeDtypeStruct((N_DEV, 128, 512), BF16), jax.ShapeDtypeStruct((N_DEV, 128, D_MODEL), BF16)],
        scratch_shapes=[pltpu.VMEM((HB, n_tiles, AUG_ROWS, TA), F32),
                        pltpu.VMEM((HB, s_len, 128), BF16), pltpu.VMEM((HB, s_len, HEAD_DIM), BF16),
                        pltpu.SemaphoreType.DMA((14,)), pltpu.SemaphoreType.DMA((14,)), pltpu.SemaphoreType.DMA((2,)),
                        pltpu.SemaphoreType.DMA((2 * n_tiles,))],
        compiler_params=_params(("arbitrary", "arbitrary")),
    )(q_aug, k_aug, kt_aug, v_h, d_o, lse_rows, delta_rows, dw_kv16, dw_out16)


def _fox_post(proj, bf_pad, gq, gk, seg, dq_aug, dk_aug, dv_h):
    s_len = proj.shape[0]
    n_tiles = s_len // TM

    def body(q_ref, k_ref, f_ref, bf_ref, gq_ref, gk_ref, seg_ref, dqa_ref, dka_ref, dvh_ref,
             dq_ref, dk_ref, dv_ref, df_ref, dgq_ref, dgk_ref, dbf_ref, carry_ref):
        @pl.when(pl.program_id(0) == 0)
        def _():
            carry_ref[...] = jnp.zeros((1, 128), F32)
            dgq_ref[...] = jnp.zeros((1, HEAD_DIM), F32)
            dgk_ref[...] = jnp.zeros((1, HEAD_DIM), F32)
            dbf_ref[...] = jnp.zeros((1, 128), F32)

        lane = lax.broadcasted_iota(jnp.int32, (TM, 128), 1)
        seg = seg_ref[...]
        dqas = [dqa_ref[h] for h in range(FOX_HEADS)]
        dkas = [dka_ref[h] for h in range(FOX_HEADS)]
        def head_norms():
            q_all = q_ref[...]
            k_all = k_ref[...]
            rq = _head_rms(q_all, seg)
            rk = _head_rms(k_all, seg)
            yield
            dy_q = jnp.concatenate([t[:, 0:HEAD_DIM] for t in dqas], axis=1) * 0.125
            dq, dgq_rows = _head_rms_bwd(q_all * rq, rq, gq_ref[...], dy_q, seg)
            dq_ref[...] = dq.astype(BF16)
            dgq_ref[...] += _fold_heads(jnp.sum(dgq_rows, axis=0, keepdims=True), FOX_HEADS)
            yield
            dy_k = jnp.concatenate([t[:, 0:HEAD_DIM] for t in dkas], axis=1)
            dk, dgk_rows = _head_rms_bwd(k_all * rk, rk, gk_ref[...], dy_k, seg)
            dk_ref[...] = dk.astype(BF16)
            dgk_ref[...] += _fold_heads(jnp.sum(dgk_rows, axis=0, keepdims=True), FOX_HEADS)
            dv_ref[...] = jnp.concatenate([dvh_ref[h].astype(F32) for h in range(FOX_HEADS)], axis=1).astype(BF16)
            yield

        def forget_gates():
            d_cum = jnp.zeros((TM, 128), F32)
            for h in range(FOX_HEADS):
                d_cum = jnp.where(lane == h, dqas[h][:, 64:65] - dkas[h][:, 67:68], d_cum)
            row = lax.broadcasted_iota(jnp.int32, (TM, TM), 0)
            col = lax.broadcasted_iota(jnp.int32, (TM, TM), 1)
            tri = jnp.where(col >= row, 1.0, 0.0).astype(BF16)
            hi, mid, lo = _split3(d_cum)
            d_logf = _dot(tri, hi) + _dot(tri, mid) + _dot(tri, lo) + carry_ref[...]
            yield
            carry_ref[...] = d_logf[0:1, :]
            z = f_ref[...] + bf_ref[...]
            d_f = jnp.where(lane < FOX_HEADS, d_logf / (1.0 + jnp.exp(z)), 0.0)
            df_ref[...] = d_f.astype(BF16)
            dbf_ref[...] += jnp.sum(d_f, axis=0, keepdims=True)
            yield

        _alternate((head_norms(), 3), (forget_gates(), 2))

    rev = lambda i: n_tiles - 1 - i
    col_blk = lambda j: pl.BlockSpec((TM, PIECE), lambda i: (rev(i), j))
    head_blk = lambda w: pl.BlockSpec((FOX_HEADS, TM, w), lambda i: (0, rev(i), 0))
    out_piece = pl.BlockSpec((TM, PIECE), lambda i: (rev(i), 0))
    return pl.pallas_call(
        body,
        name="fox_post",
        grid=(n_tiles,),
        in_specs=[col_blk(1), col_blk(2), pl.BlockSpec((TM, 128), lambda i: (rev(i), MY_F_OFF // 128)),
                  _full((1, 128)), _full((1, PIECE)), _full((1, PIECE)), _full((PIECE, PIECE)),
                  head_blk(128), head_blk(128), head_blk(HEAD_DIM)],
        out_specs=[out_piece, out_piece, out_piece, pl.BlockSpec((TM, 128), lambda i: (rev(i), 0)),
                   _full((1, HEAD_DIM)), _full((1, HEAD_DIM)), _full((1, 128))],
        out_shape=[jax.ShapeDtypeStruct((s_len, PIECE), BF16)] * 3 + [
            jax.ShapeDtypeStruct((s_len, 128), BF16), jax.ShapeDtypeStruct((1, HEAD_DIM), F32),
            jax.ShapeDtypeStruct((1, HEAD_DIM), F32), jax.ShapeDtypeStruct((1, 128), F32)],
        scratch_shapes=[pltpu.VMEM((1, 128), F32)],
        compiler_params=_params(("arbitrary",)),
    )(proj, proj, proj, bf_pad, gq, gk, seg, dq_aug, dk_aug, dv_h)


def _mem_bwd(mem, g, w_kv, gk, dkm, dvm):
    def body(mem_ref, g_ref, w_ref, gk_ref, dkm_ref, dvm_ref, dw_ref, dg_ref, dgk_ref, dkv_ref):
        m = mem_ref[...]
        r = _rms(m)
        a = m * r
        mn16 = (a * g_ref[...]).astype(BF16)
        kv = _dot(mn16, w_ref[...])
        dgk = jnp.zeros((N_MEM, HEAD_DIM), F32)
        for h in range(MEM_HEADS):
            sl = slice(HEAD_DIM * h, HEAD_DIM * (h + 1))
            kh = kv[:, sl]
            rk = _rms(kh)
            dk, dg_rows = _rms_bwd(kh * rk, rk, gk_ref[...], dkm_ref[:, sl])
            dkv_ref[:, sl] = dk.astype(BF16)
            dgk = dgk + dg_rows
        dkv_ref[:, 256:512] = dvm_ref[...].astype(BF16)
        dgk_ref[...] = jnp.sum(dgk, axis=0, keepdims=True)
        dkv16 = dkv_ref[...]
        dw_ref[...] = _dot_tn(mn16, dkv16).astype(BF16)
        dg_ref[...] = jnp.sum(_dot_nt(dkv16, w_ref[...]) * a, axis=0, keepdims=True)

    return pl.pallas_call(
        body,
        name="mem_bwd",
        out_shape=(jax.ShapeDtypeStruct((D_MODEL, 512), BF16), jax.ShapeDtypeStruct((1, D_MODEL), F32),
                   jax.ShapeDtypeStruct((1, HEAD_DIM), F32)),
        in_specs=[pl.BlockSpec(memory_space=pltpu.VMEM)] * 6,
        out_specs=(pl.BlockSpec(memory_space=pltpu.VMEM),) * 3,
        scratch_shapes=[pltpu.VMEM((N_MEM, 512), BF16)],
        compiler_params=pltpu.CompilerParams(vmem_limit_bytes=VMEM_LIMIT),
    )(mem, g, w_kv, gk, dkm, dvm)


def _grad_x_exchange(pieces, d_f, w_my, x, norm_g, d_out, dw_in, land_kv, land_out, d_mem_g, d_scale, d_bf, d_gq, d_gk, d_gqm,
                     d_gkm, dwp_bd, loss_blk):
    s_len = x.shape[0]
    n_steps = s_len // TM
    step2, step3 = n_steps // 4, (11 * n_steps) // 16
    half = D_MODEL // 2

    def body(p0, p1, p2, p3, p4, p5, df_ref, x_ref, dout_ref, w_ref, g_ref, in_ref, lkv_ref, lout_ref,
             dmg, dsc, dbf, dgq, dgk, dgqm, dgkm, dwp, loss_ref,
             gx_ref, rin_ref, rkv_ref, rout_ref, sred_ref,
             dng, land1, send2a, send2b, keep2a, keep2b, land2a, land2b, send3a, send3b, land3a, land3b, sb_ref, sland,
             own4, lkv_v, lout_v, send_sems, recv_sems, load_sems):
        i = pl.program_id(0)
        x_, y_, c_, me = _my_place()
        sibling, x_nbr, y_nbr = (x_, y_, 1 - c_), (1 - x_, y_, c_), (x_, 1 - y_, c_)
        loads = [pltpu.make_async_copy(in_ref.at[2 * k + c_], own4.at[k], load_sems.at[k]) for k in range(4)]
        loads += [pltpu.make_async_copy(lkv_ref, lkv_v, load_sems.at[4]), pltpu.make_async_copy(lout_ref, lout_v, load_sems.at[5])]

        def rcopy(src, dst, sem, to):
            return pltpu.make_async_remote_copy(src_ref=src, dst_ref=dst, send_sem=send_sems.at[sem], recv_sem=recv_sems.at[sem],
                                                device_id=to, device_id_type=MESH)

        early_rows, late_rows = pl.ds(8, SB_ROWS - 8), pl.ds(0, 8)
        small_early, small_late = [], []
        for k in range(1, N_DEV):
            peer = (x_ ^ (k >> 2), y_ ^ ((k >> 1) & 1), c_ ^ (k & 1))
            small_early.append(rcopy(sb_ref.at[early_rows], sland.at[me, early_rows], k - 1, peer))
            small_late.append(rcopy(sb_ref.at[late_rows], sland.at[me, late_rows], 16 + k, peer))
        small = small_early + small_late
        stage1 = [rcopy(in_ref.at[2 * k + 1 - c_], land1.at[k], 7 + k, sibling) for k in range(4)]
        stage2 = [rcopy(send2a.at[j], land2a.at[j], 11 + j, x_nbr) for j in range(2)]
        stage2 += [rcopy(send2b.at[j], land2b.at[j], 13 + j, y_nbr) for j in range(2)]
        stage3 = [rcopy(send3a, land3a, 15, y_nbr), rcopy(send3b, land3b, 16, x_nbr)]
        top, bot = slice(0, half), slice(half, D_MODEL)

        @pl.when(i == 0)
        def _():
            dng[...] = jnp.zeros((1, D_MODEL), F32)
            for cp in stage1 + loads:
                cp.start()
            sb_ref[...] = jnp.zeros((SB_ROWS, SB_W), F32)
            sb_ref[SB_POOL_SCALE:SB_POOL_SCALE + 1, :] = dsc[...]
            sb_ref[SB_B_F:SB_B_F + 1, 0:128] = dbf[...]
            for row, ref in ((SB_FOX_Q, dgq), (SB_FOX_K, dgk), (SB_MEM_Q, dgqm), (SB_MEM_K, dgkm)):
                sb_ref[row:row + 1, 0:HEAD_DIM] = ref[...]
            sb_ref[SB_LOSS:SB_LOSS + 1, 0:128] = loss_ref[0:1, :]
            for grp in range(4):
                sl = slice(64 * grp, 64 * (grp + 1))
                sb_ref[SB_W_POOL:SB_W_POOL + 64, sl] = dwp[sl, sl]
            for cp in small_early:
                cp.start()

        @pl.when(i == step2)
        def _():
            for cp in stage1:
                cp.wait_recv()
            for cp in loads[0:4]:
                cp.wait()

            def chip_sum(k, cols):
                return own4[k, :, cols].astype(F32) + land1[k, :, cols].astype(F32)

            for j in range(2):
                send2a[j] = chip_sum(2 * (1 - x_) + j, top).astype(BF16)
                keep2a[j] = chip_sum(2 * x_ + j, top)
                send2b[j] = chip_sum(2 * j + 1 - y_, bot).astype(BF16)
                keep2b[j] = chip_sum(2 * j + y_, bot)
            for cp in stage2:
                cp.start()

        @pl.when(i == step3)
        def _():
            for cp in stage2:
                cp.wait_recv()
            for j in range(2):
                keep2a[j] = keep2a[j] + land2a[j].astype(F32)
                keep2b[j] = keep2b[j] + land2b[j].astype(F32)
            send3a[...] = keep2a[1 - y_].astype(BF16)
            send3b[...] = keep2b[1 - x_].astype(BF16)
            for cp in stage3:
                cp.start()

        dh = _dot_nt(df_ref[...], w_ref[:, MY_F_OFF:MY_WIDTH])
        for j, p in enumerate((p0, p1, p2, p3, p4, p5)):
            dh = dh + _dot_nt(p[...], w_ref[:, PIECE * j:PIECE * (j + 1)])
        xv = x_ref[...]
        r = _rms(xv)
        dx, dg_rows = _rms_bwd(xv * r, r, g_ref[...], dh)
        gx_ref[...] = dout_ref[...] + dx
        dng[...] += jnp.sum(dg_rows, axis=0, keepdims=True)

        @pl.when(i == n_steps - 1)
        def _():
            for k in range(4):
                sb_ref[SB_NORM_G + k:SB_NORM_G + k + 1, :] = dng[:, SB_W * k:SB_W * (k + 1)]
                sb_ref[SB_MEM_NORM_G + k:SB_MEM_NORM_G + k + 1, :] = dmg[:, SB_W * k:SB_W * (k + 1)]
            for cp in small_late:
                cp.start()
            sland[me] = sb_ref[...]
            for cp in stage3:
                cp.wait_recv()
            rin_ref[:, top] = keep2a[y_] + land3a[...].astype(F32)
            rin_ref[:, bot] = keep2b[x_] + land3b[...].astype(F32)
            for cp in small:
                cp.wait_recv()
            for cp in small + stage1 + stage2 + stage3:
                cp.wait_send()
            for cp in loads[4:6]:
                cp.wait()
            for land, red in ((lkv_v, rkv_ref), (lout_v, rout_ref), (sland, sred_ref)):
                acc = land[0].astype(F32)
                for d in range(1, N_DEV):
                    acc = acc + land[d].astype(F32)
                red[...] = acc

    piece = pl.BlockSpec((TM, PIECE), lambda i: (i, 0))
    row = pl.BlockSpec((TM, D_MODEL), lambda i: (i, 0))
    vmem = pl.BlockSpec(memory_space=pltpu.VMEM)
    half_chunk = lambda n, dt: pltpu.VMEM((n, CHUNK_ROWS, half), dt)
    whole = (w_my, norm_g, dw_in, land_kv, land_out, d_mem_g, d_scale, d_bf, d_gq, d_gk, d_gqm, d_gkm, dwp_bd, loss_blk)
    return pl.pallas_call(
        body,
        name="grad_x_exchange",
        grid=(n_steps,),
        in_specs=[piece] * 6 + [pl.BlockSpec((TM, 128), lambda i: (i, 0)), row, row, vmem, vmem]
        + [pl.BlockSpec(memory_space=pl.ANY)] * 3 + [vmem] * (len(whole) - 5),
        out_specs=[row, vmem, vmem, vmem, vmem],
        out_shape=[jax.ShapeDtypeStruct((s_len, D_MODEL), F32), jax.ShapeDtypeStruct((CHUNK_ROWS, D_MODEL), F32),
                   jax.ShapeDtypeStruct((128, 512), F32), jax.ShapeDtypeStruct((128, D_MODEL), F32),
                   jax.ShapeDtypeStruct((SB_ROWS, SB_W), F32)],
        scratch_shapes=[
            pltpu.VMEM((1, D_MODEL), F32),
            pltpu.VMEM((4, CHUNK_ROWS, D_MODEL), BF16),
            half_chunk(2, BF16), half_chunk(2, BF16), half_chunk(2, F32), half_chunk(2, F32), half_chunk(2, BF16), half_chunk(2, BF16),
            pltpu.VMEM((CHUNK_ROWS, half), BF16), pltpu.VMEM((CHUNK_ROWS, half), BF16),
            pltpu.VMEM((CHUNK_ROWS, half), BF16), pltpu.VMEM((CHUNK_ROWS, half), BF16),
            pltpu.VMEM((SB_ROWS, SB_W), F32),
            pltpu.VMEM((N_DEV, SB_ROWS, SB_W), F32),
            pltpu.VMEM((4, CHUNK_ROWS, D_MODEL), BF16),
            pltpu.VMEM((N_DEV, 128, 512), BF16),
            pltpu.VMEM((N_DEV, 128, D_MODEL), BF16),
            pltpu.SemaphoreType.DMA((24,)),
            pltpu.SemaphoreType.DMA((24,)),
            pltpu.SemaphoreType.DMA((6,)),
        ],
        compiler_params=_params(("arbitrary",)),
    )(*pieces, d_f, x, d_out, *whole)


def _grad_w_in(h16, pieces, d_f):
    s_len = h16.shape[0]
    tk = 512

    def body(h_ref, p0, p1, p2, p3, p4, p5, df_ref, out_ref, dw_ref):
        @pl.when(pl.program_id(0) == 0)
        def _():
            dw_ref[...] = jnp.zeros((D_MODEL, MY_WIDTH), F32)

        h = h_ref[...]
        for j, p in enumerate((p0, p1, p2, p3, p4, p5)):
            dw_ref[:, PIECE * j:PIECE * (j + 1)] += _dot_tn(h, p[...])
        dw_ref[:, MY_F_OFF:MY_WIDTH] += _dot_tn(h, df_ref[...])

        @pl.when(pl.program_id(0) == pl.num_programs(0) - 1)
        def _():
            for d in range(N_DEV):
                parts = [dw_ref[:, start:start + width] for start, width in _chunk_segments(d)]
                parts.append(jnp.zeros((D_MODEL, 512 - IN_CHUNK), F32))
                out_ref[d] = jnp.concatenate(parts, axis=1).T[0:CHUNK_ROWS].astype(BF16)

    piece = pl.BlockSpec((tk, PIECE), lambda i: (i, 0))
    return pl.pallas_call(
        body,
        name="grad_w_in",
        grid=(s_len // tk,),
        in_specs=[pl.BlockSpec((tk, D_MODEL), lambda i: (i, 0))] + [piece] * 6 + [pl.BlockSpec((tk, 128), lambda i: (i, 0))],
        out_specs=_full((N_DEV, CHUNK_ROWS, D_MODEL)),
        out_shape=jax.ShapeDtypeStruct((N_DEV, CHUNK_ROWS, D_MODEL), BF16),
        scratch_shapes=[pltpu.VMEM((D_MODEL, MY_WIDTH), F32)],
        compiler_params=_params(("arbitrary",)),
    )(h16, *pieces, d_f)


def _adamw(w, g, m, v):
    m = ADAM_B1 * m + (1.0 - ADAM_B1) * g
    v = ADAM_B2 * v + (1.0 - ADAM_B2) * (g * g)
    m_hat = m / (1.0 - ADAM_B1 ** ADAM_STEP)
    v_hat = v / (1.0 - ADAM_B2 ** ADAM_STEP)
    return -ADAM_LR * (m_hat / (jnp.sqrt(v_hat) + ADAM_EPS) + ADAM_WD * w), m, v


PARAM_ORDER = ("norm_g", "w_in", "b_f", "w_pool", "pool_scale", "fox_q_g", "fox_k_g", "mem_norm_g", "w_mem_kv", "mem_q_g", "mem_k_g", "w_out")


def _update(red_in, red_kv, red_out, sred, params):
    def body(rin_ref, rkv_ref, rout_ref, sred_ref, *refs):
        ins, outs = refs[:3 * len(PARAM_ORDER)], refs[3 * len(PARAM_ORDER):]
        wide = lambda row: jnp.concatenate([sred_ref[row + k:row + k + 1, :] for k in range(4)], axis=1)
        head = lambda row: sred_ref[row:row + 1, 0:HEAD_DIM]
        grads = {
            "norm_g": lambda: wide(SB_NORM_G), "w_in": lambda: rin_ref[...], "b_f": lambda: sred_ref[SB_B_F:SB_B_F + 1, 0:FOX_HEADS],
            "pool_scale": lambda: sred_ref[SB_POOL_SCALE:SB_POOL_SCALE + 1, :], "fox_q_g": lambda: head(SB_FOX_Q),
            "fox_k_g": lambda: head(SB_FOX_K), "mem_norm_g": lambda: wide(SB_MEM_NORM_G), "w_mem_kv": lambda: rkv_ref[...],
            "mem_q_g": lambda: head(SB_MEM_Q), "mem_k_g": lambda: head(SB_MEM_K), "w_out": lambda: rout_ref[...],
        }
        for p, name in enumerate(PARAM_ORDER):
            w_ref, m_ref, v_ref = ins[3 * p:3 * p + 3]
            g_out, d_out, m_out, v_out = outs[4 * p:4 * p + 4]
            if name == "w_pool":
                for grp in range(4):
                    g = sred_ref[SB_W_POOL:SB_W_POOL + 64, 64 * grp:64 * (grp + 1)]
                    delta, m_new, v_new = _adamw(w_ref[grp], g, m_ref[grp], v_ref[grp])
                    g_out[grp], d_out[grp], m_out[grp], v_out[grp] = g, delta, m_new, v_new
            elif name == "w_in":
                for j in range(8):
                    rows = pl.ds(j, IN_CHUNK, stride=8)
                    g = rin_ref[0:IN_CHUNK, 128 * j:128 * (j + 1)]
                    delta, m_new, v_new = _adamw(w_ref[rows, :], g, m_ref[rows, :], v_ref[rows, :])
                    g_out[rows, :], d_out[rows, :], m_out[rows, :], v_out[rows, :] = g, delta, m_new, v_new
            else:
                g = grads[name]()
                delta, m_new, v_new = _adamw(w_ref[...], g, m_ref[...], v_ref[...])
                g_out[...], d_out[...], m_out[...], v_out[...] = g, delta, m_new, v_new

    flat = [t for name in PARAM_ORDER for t in params[name]]
    shapes = [params[name][0].shape for name in PARAM_ORDER for _ in range(4)]
    outs = pl.pallas_call(
        body,
        name="adamw_update",
        out_shape=tuple(jax.ShapeDtypeStruct(s, F32) for s in shapes),
        in_specs=[pl.BlockSpec(memory_space=pltpu.VMEM)] * (4 + len(flat)),
        out_specs=(pl.BlockSpec(memory_space=pltpu.VMEM),) * len(shapes),
        compiler_params=pltpu.CompilerParams(vmem_limit_bytes=VMEM_LIMIT),
    )(red_in, red_kv, red_out, sred, *flat)
    return {name: outs[4 * p:4 * p + 4] for p, name in enumerate(PARAM_ORDER)}


def kernel(x, mem, norm_g, w_in, b_f, w_pool, pool_scale, fox_q_g, fox_k_g, mem_norm_g, w_mem_kv, mem_q_g, mem_k_g, w_out, loss_target, m_norm_g, m_w_in, m_b_f, m_w_pool, m_pool_scale, m_fox_q_g, m_fox_k_g, m_mem_norm_g, m_w_mem_kv, m_mem_q_g, m_mem_k_g, m_w_out, v_norm_g, v_w_in, v_b_f, v_w_pool, v_pool_scale, v_fox_q_g, v_fox_k_g, v_mem_norm_g, v_w_mem_kv, v_mem_q_g, v_mem_k_g, v_w_out):
    given = dict(norm_g=(norm_g, m_norm_g, v_norm_g), w_in=(w_in, m_w_in, v_w_in), b_f=(b_f, m_b_f, v_b_f),
                 w_pool=(w_pool, m_w_pool, v_w_pool), pool_scale=(pool_scale, m_pool_scale, v_pool_scale),
                 fox_q_g=(fox_q_g, m_fox_q_g, v_fox_q_g), fox_k_g=(fox_k_g, m_fox_k_g, v_fox_k_g),
                 mem_norm_g=(mem_norm_g, m_mem_norm_g, v_mem_norm_g), w_mem_kv=(w_mem_kv, m_w_mem_kv, v_w_mem_kv),
                 mem_q_g=(mem_q_g, m_mem_q_g, v_mem_q_g), mem_k_g=(mem_k_g, m_mem_k_g, v_mem_k_g), w_out=(w_out, m_w_out, v_w_out))
    params = {name: tuple(t[0] if t.ndim > 2 else t for t in wmv) for name, wmv in given.items()}
    params["w_in"] = tuple(t.T.reshape(IN_CHUNK * 8, 128) for t in params["w_in"])
    x2, mem2, target2 = x[0], mem[0], loss_target[0]

    seg = jnp.asarray(np.kron(np.eye(FOX_HEADS), np.full((HEAD_DIM, HEAD_DIM), 1.0 / HEAD_DIM)), BF16)
    w_my, w_kv16, w_out16, wp_bd, bf_pad, gq8, gk8, gqm4 = _gather_w_in(
        params["w_in"][0], params["w_mem_kv"][0], params["w_out"][0], params["w_pool"][0], b_f, fox_q_g, fox_k_g, mem_q_g)
    proj, h16, q_aug, k_aug, v_h, kt_aug, vt_aug = _proj_prep(x2, norm_g, w_my, bf_pad, gq8, gk8, seg)
    olse, lse_rows, w_kv_all, w_out_all = _fox_fwd(q_aug, k_aug, vt_aug, w_kv16, w_out16)
    km, vm = _mem_prep(mem2, mem_norm_g, w_kv_all, mem_k_g)
    d_out, d_mixed, dw_out, loss_blk = _mix_out_loss(proj, olse, km, vm, wp_bd, pool_scale, gqm4, seg, x2, target2, w_out_all)

    d_ua, d_gb, d_qm, d_o, delta_rows, dwp_bd, d_scale, dkm, dvm, d_gqm = _mix_bwd(proj, olse, d_mixed, km, vm, wp_bd, pool_scale,
                                                                                    gqm4, seg)
    dw_kv, d_mem_g, d_gkm = _mem_bwd(mem2, mem_norm_g, w_kv_all, mem_k_g, dkm, dvm)
    dq_aug, dk_aug, dv_h, land_kv, land_out = _fox_bwd(q_aug, k_aug, kt_aug, v_h, d_o, lse_rows, delta_rows, dw_kv, dw_out)
    d_q, d_k, d_v, d_f, d_gq, d_gk, d_bf = _fox_post(proj, bf_pad, gq8, gk8, seg, dq_aug, dk_aug, dv_h)
    pieces = (d_ua, d_q, d_k, d_v, d_gb, d_qm)
    dw_in = _grad_w_in(h16, pieces, d_f)
    grad_x, red_in, red_kv, red_out, sred = _grad_x_exchange(pieces, d_f, w_my, x2, norm_g, d_out, dw_in, land_kv, land_out, d_mem_g,
                                                             d_scale, d_bf, d_gq, d_gk, d_gqm, d_gkm, dwp_bd, loss_blk)
    new = _update(red_in, red_kv, red_out, sred, params)
    result = [sred[SB_LOSS, 0], grad_x[None]]
    for kind in range(4):
        for name in PARAM_ORDER:
            out = new[name][kind]
            if name == "w_in":
                out = out.reshape(IN_CHUNK, D_MODEL).T
            result.append(out[None] if given[name][0].ndim > 2 else out)
    return tuple(result)
```

```python
import functools

import jax
import jax.numpy as jnp
import numpy as np
from jax import lax
from jax.experimental import pallas as pl
from jax.experimental.pallas import tpu as pltpu

F32 = jnp.float32
BF16 = jnp.bfloat16
MESH = pl.DeviceIdType.MESH

N_DEV = 8
D_MODEL = 1024
HEAD_DIM = 64
FOX_HEADS = 8
MEM_HEADS = 4
N_MEM = 256
POOL_WIDTH = 256
EPS = 1e-6
IN_WIDTH = 3080
IN_CHUNK = IN_WIDTH // N_DEV
CHUNK_ROWS = 400
F_OFF = 2048
MY_WIDTH = 3200
MY_F_OFF = 3072
PIECE = 512
TM = 256
TMM = 512
TA = 256
HB = 8
HB_FWD = 8
AUG_ROWS = 72
HALO = 16
VMEM_LIMIT = 56 * 1024 * 1024

ADAM_LR = 0.001
ADAM_B1 = 0.9
ADAM_B2 = 0.999
ADAM_EPS = 1e-08
ADAM_WD = 0.01
ADAM_STEP = 10


def _dot(a, b):
    return jnp.dot(a, b, preferred_element_type=F32)


def _dot_nt(a, b):
    return lax.dot_general(a, b, (((1,), (1,)), ((), ())), preferred_element_type=F32)


def _dot_tn(a, b):
    return lax.dot_general(a, b, (((0,), (0,)), ((), ())), preferred_element_type=F32)


def _sigmoid(g):
    return 0.5 + 0.5 * jnp.tanh(0.5 * g)


def _split3(v):
    hi = v.astype(BF16)
    r1 = v - hi.astype(F32)
    mid = r1.astype(BF16)
    lo = (r1 - mid.astype(F32)).astype(BF16)
    return hi, mid, lo


def _rms(v):
    return lax.rsqrt(jnp.mean(v * v, axis=-1, keepdims=True) + EPS)


def _rms_bwd(a, r, g, dy):
    da = dy * g
    return r * (da - a * jnp.mean(da * a, axis=-1, keepdims=True)), dy * a


def _head_mean(z, seg):
    hi = z.astype(BF16)
    lo = (z - hi.astype(F32)).astype(BF16)
    if z.shape[1] == 256:
        return _dot(hi, seg) + _dot(lo, seg)
    half = seg[0:256, 0:256]
    return jnp.concatenate([_dot(hi[:, 0:256], half) + _dot(lo[:, 0:256], half),
                            _dot(hi[:, 256:512], half) + _dot(lo[:, 256:512], half)], axis=1)


def _head_rms(v, seg):
    return lax.rsqrt(_head_mean(v * v, seg) + EPS)


def _head_rms_bwd(a, r, g, dy, seg):
    da = dy * g
    return r * (da - a * _head_mean(da * a, seg)), dy * a


def _fold_heads(row, n_heads):
    out = row[:, 0:HEAD_DIM]
    for h in range(1, n_heads):
        out = out + row[:, HEAD_DIM * h:HEAD_DIM * (h + 1)]
    return out


def _params(sem, limit=VMEM_LIMIT):
    return pltpu.CompilerParams(dimension_semantics=sem, vmem_limit_bytes=limit)


def _full(shape):
    return pl.BlockSpec(shape, lambda *_: (0,) * len(shape))


def _chunk_segments(d):
    runs = []
    for lo, hi, shift in ((0, F_OFF, 0), (F_OFF, F_OFF + FOX_HEADS, MY_F_OFF - F_OFF), (F_OFF + FOX_HEADS, IN_WIDTH, -FOX_HEADS)):
        a, b = max(lo, IN_CHUNK * d), min(hi, IN_CHUNK * (d + 1))
        if a < b:
            runs.append((a + shift, b - a))
    return runs


def _my_place():
    x, y, c = lax.axis_index("x"), lax.axis_index("y"), lax.axis_index("c")
    return x, y, c, 4 * x + 2 * y + c


def _shard_rows(dev):
    return pl.ds(pl.multiple_of(128 * dev, 128), 128)


def _gather_w_in(w_in, w_kv, w_out, w_pool, b_f, gq, gk, gqm):
    def body(win_ref, wkv_ref, wout_ref, wp_ref, bf_ref, gq_ref, gk_ref, gqm_ref, wmy_ref, kv16_ref, out16_ref, wpbd_ref, bfpad_ref,
             gq8_ref, gk8_ref, gqm4_ref, in_all, pay_in, win_rows, send_sems, recv_sems, load_sem):
        x, y, c, me = _my_place()
        here, sibling = (x, y, c), (x, y, 1 - c)
        x_chip, y_chip, far_chip = (1 - x, y), (x, 1 - y), (1 - x, 1 - y)
        relay_from = (x ^ (1 - c), y ^ c)
        relay_to = (x ^ c, y ^ (1 - c))

        load = pltpu.make_async_copy(win_ref, win_rows, load_sem)
        load.start()
        load.wait()
        pay_in[...] = jnp.zeros((CHUNK_ROWS, D_MODEL), BF16)
        for j in range(8):
            pay_in[0:IN_CHUNK, 128 * j:128 * (j + 1)] = win_rows[pl.ds(j, IN_CHUNK, stride=8), :].astype(BF16)

        def copy(k, block, to, own=False):
            px, py, pc = block
            dst = in_all.at[4 * px + 2 * py + pc]
            return pltpu.make_async_remote_copy(src_ref=pay_in if own else dst, dst_ref=dst, send_sem=send_sems.at[k],
                                                recv_sem=recv_sems.at[k], device_id=to, device_id_type=MESH)

        first = [copy(0, here, sibling, own=True), copy(1, here, (*x_chip, c), own=True), copy(2, here, (*y_chip, c), own=True)]
        for cp in first:
            cp.start()
        in_all[me] = pay_in[...]
        kv16_ref[...] = wkv_ref[...].astype(BF16)
        out16_ref[...] = wout_ref[...].astype(BF16)
        wpbd_ref[...] = jnp.zeros((POOL_WIDTH, POOL_WIDTH), BF16)
        for grp in range(4):
            sl = slice(64 * grp, 64 * (grp + 1))
            wpbd_ref[sl, sl] = wp_ref[grp].astype(BF16)
        bfpad_ref[...] = jnp.zeros((1, 128), F32)
        bfpad_ref[:, 0:FOX_HEADS] = bf_ref[...]
        gq8_ref[...] = jnp.concatenate([gq_ref[...]] * FOX_HEADS, axis=1)
        gk8_ref[...] = jnp.concatenate([gk_ref[...]] * FOX_HEADS, axis=1)
        gqm4_ref[...] = jnp.concatenate([gqm_ref[...]] * MEM_HEADS, axis=1)
        copy(1 + c, (*relay_from, c), here).wait_recv()
        passed = [copy(3, (*relay_from, c), (*relay_to, c)), copy(4, (*relay_from, c), sibling)]
        for cp in passed:
            cp.start()
        copy(2 - c, (*relay_to, c), here).wait_recv()
        passed.append(copy(5, (*relay_to, c), sibling))
        passed[-1].start()
        copy(3, (*far_chip, c), here).wait_recv()
        passed.append(copy(6, (*far_chip, c), sibling))
        passed[-1].start()
        copy(0, sibling, here).wait_recv()
        for k, chip in ((4, relay_to), (5, relay_from), (6, far_chip)):
            copy(k, (*chip, 1 - c), here).wait_recv()
        for cp in first + passed:
            cp.wait_send()

        row_pad = jnp.zeros((512 - CHUNK_ROWS, 256), F32)
        for r0 in range(0, D_MODEL, 256):
            ch = [jnp.concatenate([in_all[d, :, r0:r0 + 256].astype(F32), row_pad], axis=0).T[:, 0:IN_CHUNK] for d in range(N_DEV)]
            lo = F_OFF - 5 * IN_CHUNK
            full = jnp.concatenate(ch[:5] + [ch[5][:, :lo], ch[5][:, lo + FOX_HEADS:], ch[6], ch[7], ch[5][:, lo:lo + FOX_HEADS],
                                             jnp.zeros((256, MY_WIDTH - IN_WIDTH), F32)], axis=1)
            wmy_ref[r0:r0 + 256, :] = full.astype(BF16)

    return pl.pallas_call(
        body,
        name="gather_w_in",
        out_shape=(jax.ShapeDtypeStruct((D_MODEL, MY_WIDTH), BF16), jax.ShapeDtypeStruct((128, 512), BF16),
                   jax.ShapeDtypeStruct((128, D_MODEL), BF16), jax.ShapeDtypeStruct((POOL_WIDTH, POOL_WIDTH), BF16),
                   jax.ShapeDtypeStruct((1, 128), F32), jax.ShapeDtypeStruct((1, PIECE), F32), jax.ShapeDtypeStruct((1, PIECE), F32),
                   jax.ShapeDtypeStruct((1, 256), F32)),
        in_specs=[pl.BlockSpec(memory_space=pl.ANY)] + [pl.BlockSpec(memory_space=pltpu.VMEM)] * 7,
        out_specs=(pl.BlockSpec(memory_space=pltpu.VMEM),) * 8,
        scratch_shapes=[
            pltpu.VMEM((N_DEV, CHUNK_ROWS, D_MODEL), BF16),
            pltpu.VMEM((CHUNK_ROWS, D_MODEL), BF16),
            pltpu.VMEM((IN_CHUNK * 8, 128), F32),
            pltpu.SemaphoreType.DMA((7,)),
            pltpu.SemaphoreType.DMA((7,)),
            pltpu.SemaphoreType.DMA,
        ],
        compiler_params=pltpu.CompilerParams(vmem_limit_bytes=VMEM_LIMIT),
    )(w_in, w_kv, w_out, w_pool, b_f, gq, gk, gqm)


def _row_block_exchange(kv_ref, out_ref, kv_dst, out_dst, send_sems, recv_sems, local_sems, gather):
    x, y, c, me = _my_place()
    remote = []
    for k in range(1, N_DEV):
        px, py, pc = x ^ (k >> 2), y ^ ((k >> 1) & 1), c ^ (k & 1)
        peer = 4 * px + 2 * py + pc
        for fam, (src, dst) in enumerate(((kv_ref, kv_dst), (out_ref, out_dst))):
            remote.append(pltpu.make_async_remote_copy(
                src_ref=src if gather else src.at[_shard_rows(peer)], dst_ref=dst.at[_shard_rows(me)] if gather else dst.at[me],
                send_sem=send_sems.at[7 * fam + k - 1], recv_sem=recv_sems.at[7 * fam + k - 1],
                device_id=(px, py, pc), device_id_type=MESH))
    local = [pltpu.make_async_copy(src if gather else src.at[_shard_rows(me)], dst.at[_shard_rows(me)] if gather else dst.at[me],
                                   local_sems.at[fam]) for fam, (src, dst) in enumerate(((kv_ref, kv_dst), (out_ref, out_dst)))]
    return remote, local


SB_ROWS, SB_W = 80, 256
SB_NORM_G, SB_MEM_NORM_G, SB_POOL_SCALE, SB_B_F, SB_FOX_Q, SB_FOX_K, SB_MEM_Q, SB_MEM_K, SB_LOSS, SB_W_POOL = 0, 4, 8, 9, 10, 11, 12, 13, 14, 16


def _alternate(*parts):
    state = [[part, 0, stages] for part, stages in parts]
    while state:
        least = min(state, key=lambda entry: entry[1] / entry[2])
        least[1] += 1
        if next(least[0], "done") == "done":
            state.remove(least)


def _log_sigmoid(z):
    return jnp.minimum(z, 0.0) - jnp.log(1.0 + jnp.exp(-jnp.abs(z)))


def _forget_lane_maps():
    to_q = np.zeros((128, FOX_HEADS * 128), np.float32)
    to_k = np.zeros((128, FOX_HEADS * 128), np.float32)
    for h in range(FOX_HEADS):
        for term in range(3):
            to_q[8 * term + h, 128 * h + 64 + term] = 1.0
            to_q[24, 128 * h + 67 + term] = 1.0
            to_k[24, 128 * h + 64 + term] = 1.0
            to_k[8 * term + h, 128 * h + 67 + term] = -1.0
    return jnp.asarray(to_q, BF16), jnp.asarray(to_k, BF16)


def _proj_prep(x, norm_g, w_my, bf_pad, gq, gk, seg):
    s_len = x.shape[0]
    n_blocks = s_len // TMM
    halves = TMM // TM
    to_q, to_k = _forget_lane_maps()
    q_off, f_off = PIECE, MY_F_OFF
    kept = 3 * PIECE + 128

    def body(x_ref, g_ref, w_ref, bf_ref, gq_ref, gk_ref, seg_ref, eq_ref, ek_ref,
             proj_ref, h_ref, qa_ref, ka_ref, vh_ref, kt_ref, vt_ref, kept_even, kept_odd, carry_ref):
        s = pl.program_id(0)

        def project(kept_ref):
            xv = x_ref[...]
            h_ref[...] = (xv * _rms(xv) * g_ref[...]).astype(BF16)
            yield
            for c0 in range(0, MY_WIDTH, 256):
                c1 = min(c0 + 256, MY_WIDTH)
                res = _dot(h_ref[...], w_ref[:, c0:c1])
                proj_ref[:, c0:c1] = res
                if q_off <= c0 < q_off + 3 * PIECE:
                    kept_ref[:, c0 - q_off:c1 - q_off] = res
                if c0 == f_off:
                    kept_ref[:, 3 * PIECE:kept] = res
                yield

        def operands(kept_ref):
            lane = lax.broadcasted_iota(jnp.int32, (TM, 128), 1)
            row = lax.broadcasted_iota(jnp.int32, (TM, TM), 0)
            col = lax.broadcasted_iota(jnp.int32, (TM, TM), 1)
            tri = jnp.where(col <= row, 1.0, 0.0).astype(BF16)
            one_at_64 = jnp.where(lane == 64, 1.0, 0.0).astype(BF16)
            zeros = jnp.zeros((TM, HEAD_DIM), BF16)
            for half in range(halves):
                rows = slice(TM * half, TM * (half + 1))
                logf = jnp.where(lane < FOX_HEADS, _log_sigmoid(kept_ref[rows, 3 * PIECE:kept] + bf_ref[...]), 0.0)
                hi, mid, lo = _split3(logf)
                cum = _dot(tri, hi) + _dot(tri, mid) + _dot(tri, lo) + carry_ref[...]
                q_all = kept_ref[rows, 0:PIECE]
                k_all = kept_ref[rows, PIECE:2 * PIECE]
                rq = _head_rms(q_all, seg_ref[...])
                rk = _head_rms(k_all, seg_ref[...])
                yield
                carry_ref[...] = cum[TM - 1:TM, :]
                f_hi, f_mid, f_lo = [t.astype(F32) for t in _split3(cum)]
                packed = (f_hi + pltpu.roll(f_mid, 8, 1) + pltpu.roll(f_lo, 16, 1)
                          + jnp.where(lane == 24, 1.0, 0.0)).astype(BF16)
                extra_q = _dot(packed, eq_ref[...]).astype(BF16)
                extra_k = _dot(packed, ek_ref[...]).astype(BF16)
                qn_all = (q_all * rq * gq_ref[...] * 0.125).astype(BF16)
                kn_all = (k_all * rk * gk_ref[...]).astype(BF16)
                v_all = kept_ref[rows, 2 * PIECE:3 * PIECE].astype(BF16)
                yield
                for h in range(FOX_HEADS):
                    sl = slice(HEAD_DIM * h, HEAD_DIM * (h + 1))
                    tile = slice(128 * h, 128 * (h + 1))
                    qa = jnp.where(lane < 64, jnp.concatenate([qn_all[:, sl], zeros], axis=1), extra_q[:, tile])
                    ka = jnp.where(lane < 64, jnp.concatenate([kn_all[:, sl], zeros], axis=1), extra_k[:, tile])
                    vh = v_all[:, sl]
                    v_aug = jnp.where(lane < 64, jnp.concatenate([vh, zeros], axis=1), one_at_64)
                    qa_ref[h, rows, :] = qa
                    ka_ref[h, rows, :] = ka
                    vh_ref[h, rows, :] = vh
                    kt_ref[h, half] = ka.T
                    vt_ref[h, half] = v_aug.T
                    if h % 2 == 1:
                        yield

        projecting = lambda kept_ref: (project(kept_ref), 1 + pl.cdiv(MY_WIDTH, 256))
        making = lambda kept_ref: (operands(kept_ref), halves * (2 + FOX_HEADS // 2))

        @pl.when(s == 0)
        def _():
            carry_ref[...] = jnp.zeros((1, 128), F32)
            _alternate(projecting(kept_even))

        @pl.when((s > 0) & (s < n_blocks) & (s % 2 == 1))
        def _():
            _alternate(projecting(kept_odd), making(kept_even))

        @pl.when((s > 0) & (s < n_blocks) & (s % 2 == 0))
        def _():
            _alternate(projecting(kept_even), making(kept_odd))

        @pl.when(s == n_blocks)
        def _():
            _alternate(making(kept_odd if n_blocks % 2 == 0 else kept_even))

    made = lambda s: jnp.minimum(s, n_blocks - 1)
    used = lambda s: jnp.maximum(s - 1, 0)
    head_blk = lambda w: pl.BlockSpec((FOX_HEADS, TMM, w), lambda s: (0, used(s), 0))
    tile_blk = pl.BlockSpec((FOX_HEADS, halves, 128, TM), lambda s: (0, used(s), 0, 0))
    tiled = jax.ShapeDtypeStruct((FOX_HEADS, s_len // TM, 128, TM), BF16)
    return pl.pallas_call(
        body,
        name="proj_prep",
        grid=(n_blocks + 1,),
        in_specs=[pl.BlockSpec((TMM, D_MODEL), lambda s: (made(s), 0)), _full((1, D_MODEL)), _full((D_MODEL, MY_WIDTH)),
                  _full((1, 128)), _full((1, PIECE)), _full((1, PIECE)), _full((PIECE, PIECE)),
                  _full((128, FOX_HEADS * 128)), _full((128, FOX_HEADS * 128))],
        out_specs=[pl.BlockSpec((TMM, MY_WIDTH), lambda s: (made(s), 0)), pl.BlockSpec((TMM, D_MODEL), lambda s: (made(s), 0)),
                   head_blk(128), head_blk(128), head_blk(HEAD_DIM), tile_blk, tile_blk],
        out_shape=[jax.ShapeDtypeStruct((s_len, MY_WIDTH), F32), jax.ShapeDtypeStruct((s_len, D_MODEL), BF16),
                   jax.ShapeDtypeStruct((FOX_HEADS, s_len, 128), BF16), jax.ShapeDtypeStruct((FOX_HEADS, s_len, 128), BF16),
                   jax.ShapeDtypeStruct((FOX_HEADS, s_len, HEAD_DIM), BF16), tiled, tiled],
        scratch_shapes=[pltpu.VMEM((TMM, kept), F32), pltpu.VMEM((TMM, kept), F32), pltpu.VMEM((1, 128), F32)],
        compiler_params=_params(("arbitrary",)),
    )(x, norm_g, w_my, bf_pad, gq, gk, seg, to_q, to_k)


def _mem_prep(mem, g, w_kv, gk):
    def body(mem_ref, g_ref, w_ref, gk_ref, km_ref, vm_ref):
        m = mem_ref[...]
        kv = _dot((m * _rms(m) * g_ref[...]).astype(BF16), w_ref[...])
        for h in range(MEM_HEADS):
            sl = slice(HEAD_DIM * h, HEAD_DIM * (h + 1))
            kh = kv[:, sl]
            km_ref[:, sl] = (kh * _rms(kh) * gk_ref[...]).astype(BF16)
        vm_ref[...] = kv[:, 256:512].astype(BF16)

    return pl.pallas_call(
        body,
        name="mem_prep",
        out_shape=(jax.ShapeDtypeStruct((N_MEM, 256), BF16),) * 2,
        in_specs=[pl.BlockSpec(memory_space=pltpu.VMEM)] * 4,
        out_specs=(pl.BlockSpec(memory_space=pltpu.VMEM),) * 2,
        compiler_params=pltpu.CompilerParams(vmem_limit_bytes=VMEM_LIMIT),
    )(mem, g, w_kv, gk)


def _causal_mask_t():
    row = lax.broadcasted_iota(jnp.int32, (TA, TA), 0)
    col = lax.broadcasted_iota(jnp.int32, (TA, TA), 1)
    return row <= col


def _fox_fwd(q_aug, k_aug, vt_aug, w_kv16, w_out16):
    s_len = q_aug.shape[1]
    n_tiles = s_len // TA
    hb = HB_FWD
    n_groups = FOX_HEADS // hb

    def body(q_ref, k_new, vt_new, kv16_ref, out16_ref, o_ref, st_ref, kv_all, out_all, k_ref, vt_ref, send_sems, recv_sems, local_sems):
        qi = pl.program_id(1)
        for h in range(hb):
            k_ref[h, pl.ds(pl.multiple_of(qi * TA, TA), TA), :] = k_new[h]
            vt_ref[h, qi] = vt_new[h, 0]
        remote, local = _row_block_exchange(kv16_ref, out16_ref, kv_all, out_all, send_sems, recv_sems, local_sems, gather=True)

        @pl.when((pl.program_id(0) == 0) & (qi == 0))
        def _():
            for cp in remote + local:
                cp.start()

        qs = [q_ref[h] for h in range(hb)]

        def step(t0, carry, masked, width):
            rows = pl.ds(pl.multiple_of(t0 * TA, TA), width * TA)
            scores = [_dot_nt(k_ref[h, rows, :], qs[h]) for h in range(hb)]
            soft = []
            for h in range(hb):
                m, _ = carry[h]
                s = jnp.where(_causal_mask_t(), scores[h], -1e30) if masked else scores[h]
                m_new = jnp.maximum(m, jnp.max(s, axis=0, keepdims=True))
                soft.append((m_new, jnp.exp(m - m_new), jnp.exp(s - m_new).astype(BF16)))
            out = []
            for h, (m_new, alpha, p) in enumerate(soft):
                acc = alpha * carry[h][1]
                for t in range(width):
                    acc = acc + _dot(vt_ref[h, t0 + t, 0:AUG_ROWS, :], p[t * TA:(t + 1) * TA])
                out.append((m_new, acc))
            return tuple(out)

        init = tuple((jnp.full((1, TA), -1e30, F32), jnp.zeros((AUG_ROWS, TA), F32)) for _ in range(hb))
        carry = lax.fori_loop(0, qi // 2, lambda j, cr: step(2 * j, cr, False, 2), init)
        carry = lax.fori_loop(2 * (qi // 2), qi, lambda j, cr: step(j, cr, False, 1), carry)
        carry = step(qi, carry, True, 1)
        for h in range(hb):
            m, acc = carry[h]
            l = acc[HEAD_DIM:HEAD_DIM + 1, :]
            lse = m + jnp.log(l)
            o_ref[h] = jnp.concatenate([acc[0:HEAD_DIM] / l, jnp.broadcast_to(lse, (HEAD_DIM, TA))], axis=0).T
            st_ref[h, 0] = jnp.broadcast_to(lse, (8, TA))

        @pl.when((pl.program_id(0) == n_groups - 1) & (qi == n_tiles - 1))
        def _():
            for cp in remote:
                cp.wait_recv()
            for cp in remote:
                cp.wait_send()
            for cp in local:
                cp.wait()

    hbm = pl.BlockSpec(memory_space=pl.ANY)
    return pl.pallas_call(
        body,
        name="fox_fwd",
        grid=(n_groups, n_tiles),
        in_specs=[pl.BlockSpec((hb, TA, 128), lambda g, i: (g, i, 0)), pl.BlockSpec((hb, TA, 128), lambda g, i: (g, i, 0)),
                  pl.BlockSpec((hb, 1, 128, TA), lambda g, i: (g, i, 0, 0)), hbm, hbm],
        out_specs=[pl.BlockSpec((hb, TA, 128), lambda g, i: (g, i, 0)), pl.BlockSpec((hb, 1, 8, TA), lambda g, i: (g, i, 0, 0)),
                   hbm, hbm],
        out_shape=[jax.ShapeDtypeStruct((FOX_HEADS, s_len, 128), F32), jax.ShapeDtypeStruct((FOX_HEADS, n_tiles, 8, TA), F32),
                   jax.ShapeDtypeStruct((D_MODEL, 512), BF16), jax.ShapeDtypeStruct((D_MODEL, D_MODEL), BF16)],
        scratch_shapes=[pltpu.VMEM((hb, s_len, 128), BF16), pltpu.VMEM((hb, n_tiles, 128, TA), BF16),
                        pltpu.SemaphoreType.DMA((14,)), pltpu.SemaphoreType.DMA((14,)), pltpu.SemaphoreType.DMA((2,))],
        compiler_params=_params(("arbitrary", "arbitrary")),
    )(q_aug, k_aug, vt_aug, w_kv16, w_out16)


def _lane_group(lane, a, b, c, d):
    return jnp.where(lane < 64, a, jnp.where(lane < 128, b, jnp.where(lane < 192, c, d)))


def _pool_count(row0):
    lane = lax.broadcasted_iota(jnp.int32, (TM, POOL_WIDTH), 1)
    t1 = row0 + lax.broadcasted_iota(jnp.int32, (TM, POOL_WIDTH), 0) + 1
    return 1.0 / jnp.minimum(t1, _lane_group(lane, 2, 4, 8, 16)).astype(F32), lane


def _pool_delta(u, halo, cnt, lane):
    xe = jnp.concatenate([halo, u], axis=0)
    s2 = xe + pltpu.roll(xe, 1, 0)
    s4 = s2 + pltpu.roll(s2, 2, 0)
    s8 = s4 + pltpu.roll(s4, 4, 0)
    s16 = s8 + pltpu.roll(s8, 8, 0)
    win = _lane_group(lane, s2[HALO:], s4[HALO:], s8[HALO:], s16[HALO:])
    return win * cnt - u


def _pool_delta_bwd(dd, dd_next, cnt, cnt_next, lane):
    ee = jnp.concatenate([dd * cnt, dd_next * cnt_next], axis=0)
    n = TM + HALO
    r2 = ee + pltpu.roll(ee, n - 1, 0)
    r4 = r2 + pltpu.roll(r2, n - 2, 0)
    r8 = r4 + pltpu.roll(r4, n - 4, 0)
    r16 = r8 + pltpu.roll(r8, n - 8, 0)
    return _lane_group(lane, r2[:TM], r4[:TM], r8[:TM], r16[:TM]) - dd


def _mem_attn(qm, gq, seg, km_ref, vm_ref):
    r = _head_rms(qm, seg)
    a = qm * r
    q16 = (a * gq * 0.125).astype(BF16)
    heads = []
    for h in range(MEM_HEADS):
        sl = slice(HEAD_DIM * h, HEAD_DIM * (h + 1))
        s = _dot_nt(q16[:, sl], km_ref[:, sl])
        e = jnp.exp(s - jnp.max(s, axis=-1, keepdims=True))
        p = e * (1.0 / jnp.sum(e, axis=-1, keepdims=True))
        heads.append((p, _dot(p.astype(BF16), vm_ref[:, sl])))
    return a, r, q16, heads


def _mem_attn_stages(qm, gq, seg, km_ref, vm_ref, result):
    r = _head_rms(qm, seg)
    a = qm * r
    q16 = (a * gq * 0.125).astype(BF16)
    yield
    head_slices = [slice(HEAD_DIM * h, HEAD_DIM * (h + 1)) for h in range(MEM_HEADS)]
    scores = [_dot_nt(q16[:, sl], km_ref[:, sl]) for sl in head_slices]
    yield
    heads = []
    for s, sl in zip(scores, head_slices):
        e = jnp.exp(s - jnp.max(s, axis=-1, keepdims=True))
        p = e * (1.0 / jnp.sum(e, axis=-1, keepdims=True))
        heads.append((p, _dot(p.astype(BF16), vm_ref[:, sl])))
    result.extend([a, r, q16, heads])
    yield


def _mix_out_loss(proj, olse, km, vm, wp_bd, pool_scale, gq_m, seg, x, target, w_out):
    s_len = x.shape[0]
    n_blocks = s_len // TMM
    halves = TMM // TM

    def body(ua_ref, halo_ref, gb_ref, qm_ref, ol_ref, km_ref, vm_ref, wp_ref, sc_ref, gq_ref, seg_ref, x_ref, t_ref, w_ref,
             dout_ref, dmix_ref, dw16_ref, loss_ref, mixed_even, mixed_odd, d16_ref, dw_ref):
        s = pl.program_id(0)
        chunks = [slice(c0, c0 + 256) for c0 in range(0, D_MODEL, 256)]

        def matmuls(mixed_ref):
            for cols in chunks:
                err = x_ref[:, cols] + _dot(mixed_ref[...], w_ref[:, cols]) - t_ref[:, cols]
                loss_ref[...] += (0.5 / D_MODEL) * jnp.sum(err * err)
                d_out = err / float(D_MODEL)
                dout_ref[:, cols] = d_out
                d16_ref[:, cols] = d_out.astype(BF16)
                yield
            for cols in chunks:
                dmix_ref[:, cols] = _dot_nt(d16_ref[...], w_ref[cols, :])
                yield
            for cols in chunks:
                dw_ref[:, cols] += _dot_tn(mixed_ref[...], d16_ref[:, cols])
                yield

        def concatenation(mixed_ref):
            for half in range(halves):
                r0 = TM * half
                rows = slice(r0, r0 + TM)
                u = ua_ref[rows, 0:256]
                g_a = ua_ref[rows, 256:512]
                halo = jnp.where(s > 0, halo_ref[...], 0.0) if half == 0 else ua_ref[r0 - HALO:r0, 0:256]
                cnt, lane = _pool_count(s * TMM + r0)
                d = _pool_delta(u, halo, cnt, lane).astype(BF16)
                y_a = _dot(d, wp_ref[...]) * sc_ref[...]
                mixed_ref[rows, 0:256] = (y_a * (g_a * _sigmoid(g_a))).astype(BF16)
                yield
                g_b = gb_ref[rows, :]
                y_b = jnp.concatenate([ol_ref[h, rows, 0:HEAD_DIM] for h in range(FOX_HEADS)], axis=1)
                mixed_ref[rows, 256:768] = (y_b * (g_b * _sigmoid(g_b))).astype(BF16)
                yield
                attn = []
                stages = _mem_attn_stages(qm_ref[rows, 0:256], gq_ref[...], seg_ref[0:256, 0:256], km_ref, vm_ref, attn)
                next(stages)
                yield
                next(stages)
                yield
                next(stages)
                g_m = qm_ref[rows, 256:512]
                y_m = jnp.concatenate([y for _, y in attn[3]], axis=1)
                mixed_ref[rows, 768:1024] = (y_m * (g_m * _sigmoid(g_m))).astype(BF16)
                yield

        multiplying = lambda mixed_ref: (matmuls(mixed_ref), 3 * len(chunks))
        building = lambda mixed_ref: (concatenation(mixed_ref), 5 * halves)

        @pl.when(s == 0)
        def _():
            dw_ref[...] = jnp.zeros((D_MODEL, D_MODEL), F32)
            loss_ref[...] = jnp.zeros((8, 128), F32)
            _alternate(building(mixed_even))

        @pl.when((s > 0) & (s < n_blocks) & (s % 2 == 1))
        def _():
            _alternate(multiplying(mixed_even), building(mixed_odd))

        @pl.when((s > 0) & (s < n_blocks) & (s % 2 == 0))
        def _():
            _alternate(multiplying(mixed_odd), building(mixed_even))

        @pl.when(s == n_blocks)
        def _():
            _alternate(multiplying(mixed_odd if n_blocks % 2 == 0 else mixed_even))
            dw16_ref[...] = dw_ref[...].astype(BF16)

    build = lambda s: jnp.minimum(s, n_blocks - 1)
    use = lambda s: jnp.maximum(s - 1, 0)
    piece = lambda j: pl.BlockSpec((TMM, PIECE), lambda s: (build(s), j))
    halo_blk = pl.BlockSpec((HALO, 256), lambda s: (jnp.maximum(build(s) * (TMM // HALO) - 1, 0), 0))
    row = pl.BlockSpec((TMM, D_MODEL), lambda s: (use(s), 0))
    return pl.pallas_call(
        body,
        name="mix_out_loss",
        grid=(n_blocks + 1,),
        in_specs=[piece(0), halo_blk, piece(4), piece(5), pl.BlockSpec((FOX_HEADS, TMM, 128), lambda s: (0, build(s), 0)),
                  _full((N_MEM, 256)), _full((N_MEM, 256)), _full((256, 256)), _full((1, 256)), _full((1, 256)),
                  _full((PIECE, PIECE)), row, row, _full((D_MODEL, D_MODEL))],
        out_specs=[row, row, _full((D_MODEL, D_MODEL)), _full((8, 128))],
        out_shape=[jax.ShapeDtypeStruct((s_len, D_MODEL), F32), jax.ShapeDtypeStruct((s_len, D_MODEL), F32),
                   jax.ShapeDtypeStruct((D_MODEL, D_MODEL), BF16), jax.ShapeDtypeStruct((8, 128), F32)],
        scratch_shapes=[pltpu.VMEM((TMM, D_MODEL), BF16), pltpu.VMEM((TMM, D_MODEL), BF16), pltpu.VMEM((TMM, D_MODEL), BF16),
                        pltpu.VMEM((D_MODEL, D_MODEL), F32)],
        compiler_params=_params(("arbitrary",)),
    )(proj, proj, proj, proj, olse, km, vm, wp_bd, pool_scale, gq_m, seg, x, target, w_out)


def _silu_pair(g):
    s = _sigmoid(g)
    return g * s, s * (1.0 + g * (1.0 - s))


def _mix_bwd(proj, olse, d_mixed, km, vm, wp_bd, pool_scale, gq_m, seg):
    s_len = proj.shape[0]
    n_tiles = s_len // TM

    def body(ua_ref, halo_ref, ga_next_ref, gb_ref, qm_ref, ol_ref, dm_ref, dm_next_ref, km_ref, vm_ref, wp_ref, sc_ref, gq_ref,
             seg_ref, dua_ref, dgb_ref, dqm_ref, do_ref, dl_ref, dwp_ref, dsc_ref, dkm_ref, dvm_ref, dgq_ref):
        i = pl.program_id(0)

        @pl.when(i == 0)
        def _():
            dwp_ref[...] = jnp.zeros((256, 256), F32)
            dsc_ref[...] = jnp.zeros((1, 256), F32)
            dkm_ref[...] = jnp.zeros((N_MEM, 256), F32)
            dvm_ref[...] = jnp.zeros((N_MEM, 256), F32)
            dgq_ref[...] = jnp.zeros((1, HEAD_DIM), F32)

        def pooling_mixer():
            u = ua_ref[:, 0:256]
            g_a = ua_ref[:, 256:512]
            halo = jnp.where(i > 0, halo_ref[...], 0.0)
            cnt, lane = _pool_count(i * TM)
            d16 = _pool_delta(u, halo, cnt, lane).astype(BF16)
            t = _dot(d16, wp_ref[...])
            yield
            y_a = t * sc_ref[...]
            silu_a, dsilu_a = _silu_pair(g_a)
            dm_a = dm_ref[:, 0:256]
            dy_a = dm_a * silu_a
            dsc_ref[...] += jnp.sum(dy_a * t, axis=0, keepdims=True)
            dt16 = (dy_a * sc_ref[...]).astype(BF16)
            dwp_ref[...] += _dot_tn(d16, dt16)
            dd = _dot_nt(dt16, wp_ref[...])
            dua_ref[:, 256:512] = (dm_a * y_a * dsilu_a).astype(BF16)
            yield
            g_next = ga_next_ref[...]
            dt_next = (dm_next_ref[...] * (g_next * _sigmoid(g_next)) * sc_ref[...]).astype(BF16)
            dd_next = jnp.where(i < n_tiles - 1, _dot_nt(dt_next, wp_ref[...]), 0.0)
            cnt_next = _pool_count((i + 1) * TM)[0][0:HALO]
            dua_ref[:, 0:256] = _pool_delta_bwd(dd, dd_next, cnt, cnt_next, lane).astype(BF16)
            yield

        def output_gate():
            silu_b, dsilu_b = _silu_pair(gb_ref[...])
            dm_b = dm_ref[:, 256:768]
            o_all = jnp.concatenate([ol_ref[h][:, 0:HEAD_DIM] for h in range(FOX_HEADS)], axis=1)
            d_o = dm_b * silu_b
            dgb_ref[...] = (dm_b * o_all * dsilu_b).astype(BF16)
            d_o16 = d_o.astype(BF16)
            for h in range(FOX_HEADS):
                do_ref[h] = d_o16[:, HEAD_DIM * h:HEAD_DIM * (h + 1)]
            yield
            prod = d_o * o_all
            hi = prod.astype(BF16)
            lo = (prod - hi.astype(F32)).astype(BF16)
            head_of_lane = lax.broadcasted_iota(jnp.int32, (FOX_HEADS, PIECE), 1) // HEAD_DIM
            pick = jnp.where(head_of_lane == lax.broadcasted_iota(jnp.int32, (FOX_HEADS, PIECE), 0), 1.0, 0.0).astype(BF16)
            delta = _dot_nt(pick, hi) + _dot_nt(pick, lo)
            for h in range(FOX_HEADS):
                dl_ref[h, 0] = jnp.broadcast_to(delta[h:h + 1, :], (8, TM))
            yield

        def memory_attention():
            seg = seg_ref[0:256, 0:256]
            a, r, q16, heads = _mem_attn(qm_ref[:, 0:256], gq_ref[...], seg, km_ref, vm_ref)
            yield
            silu_m, dsilu_m = _silu_pair(qm_ref[:, 256:512])
            dm_m = dm_ref[:, 768:1024]
            y_m = jnp.concatenate([y for _, y in heads], axis=1)
            dqm_ref[:, 256:512] = (dm_m * y_m * dsilu_m).astype(BF16)
            dy16_all = (dm_m * silu_m).astype(BF16)
            d_scores = []
            for h in range(MEM_HEADS):
                sl = slice(HEAD_DIM * h, HEAD_DIM * (h + 1))
                p = heads[h][0]
                dy16 = dy16_all[:, sl]
                dp = _dot_nt(dy16, vm_ref[:, sl])
                dvm_ref[:, sl] += _dot_tn(p.astype(BF16), dy16)
                ds16 = (p * (dp - jnp.sum(dp * p, axis=-1, keepdims=True))).astype(BF16)
                dkm_ref[:, sl] += _dot_tn(ds16, q16[:, sl])
                d_scores.append(_dot(ds16, km_ref[:, sl]))
                yield
            dq, dg_rows = _head_rms_bwd(a, r, gq_ref[...], jnp.concatenate(d_scores, axis=1) * 0.125, seg)
            dqm_ref[:, 0:256] = dq.astype(BF16)
            dgq_ref[...] += _fold_heads(jnp.sum(dg_rows, axis=0, keepdims=True), MEM_HEADS)
            yield

        _alternate((pooling_mixer(), 3))
        _alternate((memory_attention(), 2 + MEM_HEADS), (output_gate(), 2))

    n_halo = s_len // HALO
    piece = lambda j: pl.BlockSpec((TM, PIECE), lambda i: (i, j))
    before = pl.BlockSpec((HALO, 256), lambda i: (jnp.maximum(i * (TM // HALO) - 1, 0), 0))
    after = lambda j: pl.BlockSpec((HALO, 256), lambda i: (jnp.minimum((i + 1) * (TM // HALO), n_halo - 1), j))
    row = pl.BlockSpec((TM, D_MODEL), lambda i: (i, 0))
    out_piece = pl.BlockSpec((TM, PIECE), lambda i: (i, 0))
    return pl.pallas_call(
        body,
        name="mix_bwd",
        grid=(n_tiles,),
        in_specs=[piece(0), before, after(1), piece(4), piece(5), pl.BlockSpec((FOX_HEADS, TM, 128), lambda i: (0, i, 0)),
                  row, after(0), _full((N_MEM, 256)), _full((N_MEM, 256)), _full((256, 256)), _full((1, 256)), _full((1, 256)),
                  _full((PIECE, PIECE))],
        out_specs=[out_piece, out_piece, out_piece, pl.BlockSpec((FOX_HEADS, TM, HEAD_DIM), lambda i: (0, i, 0)),
                   pl.BlockSpec((FOX_HEADS, 1, 8, TM), lambda i: (0, i, 0, 0)),
                   _full((256, 256)), _full((1, 256)), _full((N_MEM, 256)), _full((N_MEM, 256)), _full((1, HEAD_DIM))],
        out_shape=[jax.ShapeDtypeStruct((s_len, PIECE), BF16)] * 3 + [
            jax.ShapeDtypeStruct((FOX_HEADS, s_len, HEAD_DIM), BF16),
            jax.ShapeDtypeStruct((FOX_HEADS, n_tiles, 8, TM), F32),
            jax.ShapeDtypeStruct((256, 256), F32), jax.ShapeDtypeStruct((1, 256), F32),
            jax.ShapeDtypeStruct((N_MEM, 256), F32), jax.ShapeDtypeStruct((N_MEM, 256), F32),
            jax.ShapeDtypeStruct((1, HEAD_DIM), F32)],
        compiler_params=_params(("arbitrary",)),
    )(proj, proj, proj, proj, proj, olse, d_mixed, d_mixed, km, vm, wp_bd, pool_scale, gq_m, seg)


def _fox_bwd(q_aug, k_aug, kt_aug, v_h, d_o, lse_rows, delta_rows, dw_kv16, dw_out16):
    s_len = q_aug.shape[1]
    n_tiles = s_len // TA
    n_groups = FOX_HEADS // HB

    def body(q_hbm, k_ref, kt_ref, v_ref, do_hbm, st_ref, dl_ref, dkv16_ref, dout16_ref, dq_ref, dk_ref, dv_ref, land_kv, land_out,
             dqt_ref, q_ref, do_ref, send_sems, recv_sems, local_sems, tile_sems):
        j = pl.program_id(1)
        remote, local = _row_block_exchange(dkv16_ref, dout16_ref, land_kv, land_out, send_sems, recv_sems, local_sems, gather=False)

        def tile_loads(i):
            rows = pl.ds(pl.multiple_of(i * TA, TA), TA)
            return [pltpu.make_async_copy(q_hbm.at[:, rows, :], q_ref.at[:, rows, :], tile_sems.at[2 * i]),
                    pltpu.make_async_copy(do_hbm.at[:, rows, :], do_ref.at[:, rows, :], tile_sems.at[2 * i + 1])]

        @pl.when((pl.program_id(0) == 0) & (j == 0))
        def _():
            for i in range(n_tiles):
                for cp in tile_loads(i):
                    cp.start()
            for cp in remote + local:
                cp.start()

        @pl.when(j == 0)
        def _():
            dqt_ref[...] = jnp.zeros((HB, n_tiles, AUG_ROWS, TA), F32)

        ks = [k_ref[h] for h in range(HB)]
        kts = [kt_ref[h, 0, 0:AUG_ROWS, :] for h in range(HB)]
        vs = [v_ref[h] for h in range(HB)]

        def pair(i0, acc, masked, width):
            @pl.when(j == 0)
            def _():
                for t in range(width):
                    for cp in tile_loads(i0 + t):
                        cp.wait()

            rows = pl.ds(pl.multiple_of(i0 * TA, TA), width * TA)
            qs = [q_ref[h, rows, :] for h in range(HB)]
            d_os = [do_ref[h, rows, :] for h in range(HB)]
            row_stat = lambda ref, h: jnp.concatenate([ref[h, i0 + t, 0:1, :] for t in range(width)], axis=1)
            scores = [(_dot_nt(ks[h], qs[h]), _dot_nt(vs[h], d_os[h])) for h in range(HB)]
            probs = []
            for h in range(HB):
                s, dp = scores[h]
                if masked:
                    s = jnp.where(_causal_mask_t(), s, -1e30)
                p = jnp.exp(s - row_stat(st_ref, h))
                probs.append((p.astype(BF16), (p * (dp - row_stat(dl_ref, h))).astype(BF16)))
            out = []
            for h in range(HB):
                p16, ds16 = probs[h]
                dqt = _dot(kts[h], ds16)
                for t in range(width):
                    dqt_ref[h, i0 + t] += dqt[:, t * TA:(t + 1) * TA]
                out.append((acc[h][0] + _dot(ds16, qs[h]), acc[h][1] + _dot(p16, d_os[h])))
            return tuple(out)

        zero = tuple((jnp.zeros((TA, 128), F32), jnp.zeros((TA, HEAD_DIM), F32)) for _ in range(HB))
        acc = pair(j, zero, True, 1)
        odd = (n_tiles - 1 - j) % 2
        acc = lax.fori_loop(j + 1, j + 1 + odd, lambda i, a: pair(i, a, False, 1), acc)
        acc = lax.fori_loop(0, (n_tiles - 1 - j) // 2, lambda t, a: pair(j + 1 + odd + 2 * t, a, False, 2), acc)
        pad = jnp.zeros((128 - AUG_ROWS, TA), F32)
        for h in range(HB):
            dk_ref[h] = acc[h][0]
            dv_ref[h] = acc[h][1].astype(BF16)
            dq_ref[h] = jnp.concatenate([dqt_ref[h, j], pad], axis=0).T

        @pl.when((pl.program_id(0) == n_groups - 1) & (j == n_tiles - 1))
        def _():
            for cp in remote:
                cp.wait_recv()
            for cp in remote:
                cp.wait_send()
            for cp in local:
                cp.wait()

    assert n_groups == 1
    resident = pl.BlockSpec(memory_space=pltpu.VMEM)
    rows_blk = lambda r: resident
    tile = lambda w: pl.BlockSpec((HB, TA, w), lambda g, j: (g, j, 0))
    hbm = pl.BlockSpec(memory_space=pl.ANY)
    return pl.pallas_call(
        body,
        name="fox_bwd",
        grid=(n_groups, n_tiles),
        in_specs=[hbm, tile(128), pl.BlockSpec((HB, 1, 128, TA), lambda g, j: (g, j, 0, 0)), tile(HEAD_DIM),
                  hbm, rows_blk(8), rows_blk(8), hbm, hbm],
        out_specs=[tile(128), tile(128), tile(HEAD_DIM), hbm, hbm],
        out_shape=[jax.ShapeDtypeStruct((FOX_HEADS, s_len, 128), F32), jax.ShapeDtypeStruct((FOX_HEADS, s_len, 128), F32),
                   jax.ShapeDtypeStruct((FOX_HEADS, s_len, HEAD_DIM), BF16),
                   jax.ShapeDtypeStruct((N_DEV, 128, 512), BF16), jax.ShapeDtypeStruct((N_DEV, 128, D_MODEL), BF16)],
        scratch_shapes=[pltpu.VMEM((HB, n_tiles, AUG_ROWS, TA), F32),
                        pltpu.VMEM((HB, s_len, 128), BF16), pltpu.VMEM((HB, s_len, HEAD_DIM), BF16),
                        pltpu.SemaphoreType.DMA((14,)), pltpu.SemaphoreType.DMA((14,)), pltpu.SemaphoreType.DMA((2,)),
                        pltpu.SemaphoreType.DMA((2 * n_tiles,))],
        compiler_params=_params(("arbitrary", "arbitrary")),
    )(q_aug, k_aug, kt_aug, v_h, d_o, lse_rows, delta_rows, dw_kv16, dw_out16)


def _fox_post(proj, bf_pad, gq, gk, seg, dq_aug, dk_aug, dv_h):
    s_len = proj.shape[0]
    n_tiles = s_len // TM

    def body(q_ref, k_ref, f_ref, bf_ref, gq_ref, gk_ref, seg_ref, dqa_ref, dka_ref, dvh_ref,
             dq_ref, dk_ref, dv_ref, df_ref, dgq_ref, dgk_ref, dbf_ref, carry_ref):
        @pl.when(pl.program_id(0) == 0)
        def _():
            carry_ref[...] = jnp.zeros((1, 128), F32)
            dgq_ref[...] = jnp.zeros((1, HEAD_DIM), F32)
            dgk_ref[...] = jnp.zeros((1, HEAD_DIM), F32)
            dbf_ref[...] = jnp.zeros((1, 128), F32)

        lane = lax.broadcasted_iota(jnp.int32, (TM, 128), 1)
        seg = seg_ref[...]
        dqas = [dqa_ref[h] for h in range(FOX_HEADS)]
        dkas = [dka_ref[h] for h in range(FOX_HEADS)]
        def head_norms():
            q_all = q_ref[...]
            k_all = k_ref[...]
            rq = _head_rms(q_all, seg)
            rk = _head_rms(k_all, seg)
            yield
            dy_q = jnp.concatenate([t[:, 0:HEAD_DIM] for t in dqas], axis=1) * 0.125
            dq, dgq_rows = _head_rms_bwd(q_all * rq, rq, gq_ref[...], dy_q, seg)
            dq_ref[...] = dq.astype(BF16)
            dgq_ref[...] += _fold_heads(jnp.sum(dgq_rows, axis=0, keepdims=True), FOX_HEADS)
            yield
            dy_k = jnp.concatenate([t[:, 0:HEAD_DIM] for t in dkas], axis=1)
            dk, dgk_rows = _head_rms_bwd(k_all * rk, rk, gk_ref[...], dy_k, seg)
            dk_ref[...] = dk.astype(BF16)
            dgk_ref[...] += _fold_heads(jnp.sum(dgk_rows, axis=0, keepdims=True), FOX_HEADS)
            dv_ref[...] = jnp.concatenate([dvh_ref[h] for h in range(FOX_HEADS)], axis=1)
            yield

        def forget_gates():
            d_cum = jnp.zeros((TM, 128), F32)
            for h in range(FOX_HEADS):
                d_cum = jnp.where(lane == h, dqas[h][:, 64:65] - dkas[h][:, 67:68], d_cum)
            row = lax.broadcasted_iota(jnp.int32, (TM, TM), 0)
            col = lax.broadcasted_iota(jnp.int32, (TM, TM), 1)
            tri = jnp.where(col >= row, 1.0, 0.0).astype(BF16)
            hi, mid, lo = _split3(d_cum)
            d_logf = _dot(tri, hi) + _dot(tri, mid) + _dot(tri, lo) + carry_ref[...]
            yield
            carry_ref[...] = d_logf[0:1, :]
            z = f_ref[...] + bf_ref[...]
            d_f = jnp.where(lane < FOX_HEADS, d_logf / (1.0 + jnp.exp(z)), 0.0)
            df_ref[...] = d_f.astype(BF16)
            dbf_ref[...] += jnp.sum(d_f, axis=0, keepdims=True)
            yield

        _alternate((head_norms(), 3), (forget_gates(), 2))

    rev = lambda i: n_tiles - 1 - i
    col_blk = lambda j: pl.BlockSpec((TM, PIECE), lambda i: (rev(i), j))
    head_blk = lambda w: pl.BlockSpec((FOX_HEADS, TM, w), lambda i: (0, rev(i), 0))
    out_piece = pl.BlockSpec((TM, PIECE), lambda i: (rev(i), 0))
    return pl.pallas_call(
        body,
        name="fox_post",
        grid=(n_tiles,),
        in_specs=[col_blk(1), col_blk(2), pl.BlockSpec((TM, 128), lambda i: (rev(i), MY_F_OFF // 128)),
                  _full((1, 128)), _full((1, PIECE)), _full((1, PIECE)), _full((PIECE, PIECE)),
                  head_blk(128), head_blk(128), head_blk(HEAD_DIM)],
        out_specs=[out_piece, out_piece, out_piece, pl.BlockSpec((TM, 128), lambda i: (rev(i), 0)),
                   _full((1, HEAD_DIM)), _full((1, HEAD_DIM)), _full((1, 128))],
        out_shape=[jax.ShapeDtypeStruct((s_len, PIECE), BF16)] * 3 + [
            jax.ShapeDtypeStruct((s_len, 128), BF16), jax.ShapeDtypeStruct((1, HEAD_DIM), F32),
            jax.ShapeDtypeStruct((1, HEAD_DIM), F32), jax.ShapeDtypeStruct((1, 128), F32)],
        scratch_shapes=[pltpu.VMEM((1, 128), F32)],
        compiler_params=_params(("arbitrary",)),
    )(proj, proj, proj, bf_pad, gq, gk, seg, dq_aug, dk_aug, dv_h)


def _mem_bwd(mem, g, w_kv, gk, dkm, dvm):
    def body(mem_ref, g_ref, w_ref, gk_ref, dkm_ref, dvm_ref, dw_ref, dg_ref, dgk_ref, dkv_ref):
        m = mem_ref[...]
        r = _rms(m)
        a = m * r
        mn16 = (a * g_ref[...]).astype(BF16)
        kv = _dot(mn16, w_ref[...])
        dgk = jnp.zeros((N_MEM, HEAD_DIM), F32)
        for h in range(MEM_HEADS):
            sl = slice(HEAD_DIM * h, HEAD_DIM * (h + 1))
            kh = kv[:, sl]
            rk = _rms(kh)
            dk, dg_rows = _rms_bwd(kh * rk, rk, gk_ref[...], dkm_ref[:, sl])
            dkv_ref[:, sl] = dk.astype(BF16)
            dgk = dgk + dg_rows
        dkv_ref[:, 256:512] = dvm_ref[...].astype(BF16)
        dgk_ref[...] = jnp.sum(dgk, axis=0, keepdims=True)
        dkv16 = dkv_ref[...]
        dw_ref[...] = _dot_tn(mn16, dkv16).astype(BF16)
        dg_ref[...] = jnp.sum(_dot_nt(dkv16, w_ref[...]) * a, axis=0, keepdims=True)

    return pl.pallas_call(
        body,
        name="mem_bwd",
        out_shape=(jax.ShapeDtypeStruct((D_MODEL, 512), BF16), jax.ShapeDtypeStruct((1, D_MODEL), F32),
                   jax.ShapeDtypeStruct((1, HEAD_DIM), F32)),
        in_specs=[pl.BlockSpec(memory_space=pltpu.VMEM)] * 6,
        out_specs=(pl.BlockSpec(memory_space=pltpu.VMEM),) * 3,
        scratch_shapes=[pltpu.VMEM((N_MEM, 512), BF16)],
        compiler_params=pltpu.CompilerParams(vmem_limit_bytes=VMEM_LIMIT),
    )(mem, g, w_kv, gk, dkm, dvm)


def _grad_x_exchange(pieces, d_f, w_my, x, norm_g, d_out, dw_in, land_kv, land_out, d_mem_g, d_scale, d_bf, d_gq, d_gk, d_gqm,
                     d_gkm, dwp_bd, loss_blk):
    s_len = x.shape[0]
    n_steps = s_len // TM
    step2, step3 = n_steps // 4, (11 * n_steps) // 16
    half = D_MODEL // 2

    def body(p0, p1, p2, p3, p4, p5, df_ref, x_ref, dout_ref, w_ref, g_ref, in_ref, lkv_ref, lout_ref,
             dmg, dsc, dbf, dgq, dgk, dgqm, dgkm, dwp, loss_ref,
             gx_ref, rin_ref, rkv_ref, rout_ref, sred_ref,
             dng, land1, send2a, send2b, keep2a, keep2b, land2a, land2b, send3a, send3b, land3a, land3b, sb_ref, sland,
             own4, lkv_v, lout_v, send_sems, recv_sems, load_sems):
        i = pl.program_id(0)
        x_, y_, c_, me = _my_place()
        sibling, x_nbr, y_nbr = (x_, y_, 1 - c_), (1 - x_, y_, c_), (x_, 1 - y_, c_)
        loads = [pltpu.make_async_copy(in_ref.at[2 * k + c_], own4.at[k], load_sems.at[k]) for k in range(4)]
        loads += [pltpu.make_async_copy(lkv_ref, lkv_v, load_sems.at[4]), pltpu.make_async_copy(lout_ref, lout_v, load_sems.at[5])]

        def rcopy(src, dst, sem, to):
            return pltpu.make_async_remote_copy(src_ref=src, dst_ref=dst, send_sem=send_sems.at[sem], recv_sem=recv_sems.at[sem],
                                                device_id=to, device_id_type=MESH)

        early_rows, late_rows = pl.ds(8, SB_ROWS - 8), pl.ds(0, 8)
        small_early, small_late = [], []
        for k in range(1, N_DEV):
            peer = (x_ ^ (k >> 2), y_ ^ ((k >> 1) & 1), c_ ^ (k & 1))
            small_early.append(rcopy(sb_ref.at[early_rows], sland.at[me, early_rows], k - 1, peer))
            small_late.append(rcopy(sb_ref.at[late_rows], sland.at[me, late_rows], 16 + k, peer))
        small = small_early + small_late
        stage1 = [rcopy(in_ref.at[2 * k + 1 - c_], land1.at[k], 7 + k, sibling) for k in range(4)]
        stage2 = [rcopy(send2a.at[j], land2a.at[j], 11 + j, x_nbr) for j in range(2)]
        stage2 += [rcopy(send2b.at[j], land2b.at[j], 13 + j, y_nbr) for j in range(2)]
        stage3 = [rcopy(send3a, land3a, 15, y_nbr), rcopy(send3b, land3b, 16, x_nbr)]
        top, bot = slice(0, half), slice(half, D_MODEL)

        @pl.when(i == 0)
        def _():
            dng[...] = jnp.zeros((1, D_MODEL), F32)
            for cp in stage1 + loads:
                cp.start()
            sb_ref[...] = jnp.zeros((SB_ROWS, SB_W), F32)
            sb_ref[SB_POOL_SCALE:SB_POOL_SCALE + 1, :] = dsc[...]
            sb_ref[SB_B_F:SB_B_F + 1, 0:128] = dbf[...]
            for row, ref in ((SB_FOX_Q, dgq), (SB_FOX_K, dgk), (SB_MEM_Q, dgqm), (SB_MEM_K, dgkm)):
                sb_ref[row:row + 1, 0:HEAD_DIM] = ref[...]
            sb_ref[SB_LOSS:SB_LOSS + 1, 0:128] = loss_ref[0:1, :]
            for grp in range(4):
                sl = slice(64 * grp, 64 * (grp + 1))
                sb_ref[SB_W_POOL:SB_W_POOL + 64, sl] = dwp[sl, sl]
            for cp in small_early:
                cp.start()

        @pl.when(i == step2)
        def _():
            for cp in stage1:
                cp.wait_recv()
            for cp in loads[0:4]:
                cp.wait()

            def chip_sum(k, cols):
                return own4[k, :, cols].astype(F32) + land1[k, :, cols].astype(F32)

            for j in range(2):
                send2a[j] = chip_sum(2 * (1 - x_) + j, top).astype(BF16)
                keep2a[j] = chip_sum(2 * x_ + j, top)
                send2b[j] = chip_sum(2 * j + 1 - y_, bot).astype(BF16)
                keep2b[j] = chip_sum(2 * j + y_, bot)
            for cp in stage2:
                cp.start()

        @pl.when(i == step3)
        def _():
            for cp in stage2:
                cp.wait_recv()
            for j in range(2):
                keep2a[j] = keep2a[j] + land2a[j].astype(F32)
                keep2b[j] = keep2b[j] + land2b[j].astype(F32)
            send3a[...] = keep2a[1 - y_].astype(BF16)
            send3b[...] = keep2b[1 - x_].astype(BF16)
            for cp in stage3:
                cp.start()

        dh = _dot_nt(df_ref[...], w_ref[:, MY_F_OFF:MY_WIDTH])
        for j, p in enumerate((p0, p1, p2, p3, p4, p5)):
            dh = dh + _dot_nt(p[...], w_ref[:, PIECE * j:PIECE * (j + 1)])
        xv = x_ref[...]
        r = _rms(xv)
        dx, dg_rows = _rms_bwd(xv * r, r, g_ref[...], dh)
        gx_ref[...] = dout_ref[...] + dx
        dng[...] += jnp.sum(dg_rows, axis=0, keepdims=True)

        @pl.when(i == n_steps - 1)
        def _():
            for k in range(4):
                sb_ref[SB_NORM_G + k:SB_NORM_G + k + 1, :] = dng[:, SB_W * k:SB_W * (k + 1)]
                sb_ref[SB_MEM_NORM_G + k:SB_MEM_NORM_G + k + 1, :] = dmg[:, SB_W * k:SB_W * (k + 1)]
            for cp in small_late:
                cp.start()
            sland[me] = sb_ref[...]
            for cp in stage3:
                cp.wait_recv()
            rin_ref[:, top] = keep2a[y_] + land3a[...].astype(F32)
            rin_ref[:, bot] = keep2b[x_] + land3b[...].astype(F32)
            for cp in small:
                cp.wait_recv()
            for cp in small + stage1 + stage2 + stage3:
                cp.wait_send()
            for cp in loads[4:6]:
                cp.wait()
            for land, red in ((lkv_v, rkv_ref), (lout_v, rout_ref), (sland, sred_ref)):
                acc = land[0].astype(F32)
                for d in range(1, N_DEV):
                    acc = acc + land[d].astype(F32)
                red[...] = acc

    piece = pl.BlockSpec((TM, PIECE), lambda i: (i, 0))
    row = pl.BlockSpec((TM, D_MODEL), lambda i: (i, 0))
    vmem = pl.BlockSpec(memory_space=pltpu.VMEM)
    half_chunk = lambda n, dt: pltpu.VMEM((n, CHUNK_ROWS, half), dt)
    whole = (w_my, norm_g, dw_in, land_kv, land_out, d_mem_g, d_scale, d_bf, d_gq, d_gk, d_gqm, d_gkm, dwp_bd, loss_blk)
    return pl.pallas_call(
        body,
        name="grad_x_exchange",
        grid=(n_steps,),
        in_specs=[piece] * 6 + [pl.BlockSpec((TM, 128), lambda i: (i, 0)), row, row, vmem, vmem]
        + [pl.BlockSpec(memory_space=pl.ANY)] * 3 + [vmem] * (len(whole) - 5),
        out_specs=[row, vmem, vmem, vmem, vmem],
        out_shape=[jax.ShapeDtypeStruct((s_len, D_MODEL), F32), jax.ShapeDtypeStruct((CHUNK_ROWS, D_MODEL), F32),
                   jax.ShapeDtypeStruct((128, 512), F32), jax.ShapeDtypeStruct((128, D_MODEL), F32),
                   jax.ShapeDtypeStruct((SB_ROWS, SB_W), F32)],
        scratch_shapes=[
            pltpu.VMEM((1, D_MODEL), F32),
            pltpu.VMEM((4, CHUNK_ROWS, D_MODEL), BF16),
            half_chunk(2, BF16), half_chunk(2, BF16), half_chunk(2, F32), half_chunk(2, F32), half_chunk(2, BF16), half_chunk(2, BF16),
            pltpu.VMEM((CHUNK_ROWS, half), BF16), pltpu.VMEM((CHUNK_ROWS, half), BF16),
            pltpu.VMEM((CHUNK_ROWS, half), BF16), pltpu.VMEM((CHUNK_ROWS, half), BF16),
            pltpu.VMEM((SB_ROWS, SB_W), F32),
            pltpu.VMEM((N_DEV, SB_ROWS, SB_W), F32),
            pltpu.VMEM((4, CHUNK_ROWS, D_MODEL), BF16),
            pltpu.VMEM((N_DEV, 128, 512), BF16),
            pltpu.VMEM((N_DEV, 128, D_MODEL), BF16),
            pltpu.SemaphoreType.DMA((24,)),
            pltpu.SemaphoreType.DMA((24,)),
            pltpu.SemaphoreType.DMA((6,)),
        ],
        compiler_params=_params(("arbitrary",)),
    )(*pieces, d_f, x, d_out, *whole)


def _grad_w_in(h16, pieces, d_f):
    s_len = h16.shape[0]
    tk = 512

    def body(h_ref, p0, p1, p2, p3, p4, p5, df_ref, out_ref, dw_ref):
        @pl.when(pl.program_id(0) == 0)
        def _():
            dw_ref[...] = jnp.zeros((D_MODEL, MY_WIDTH), F32)

        h = h_ref[...]
        for j, p in enumerate((p0, p1, p2, p3, p4, p5)):
            dw_ref[:, PIECE * j:PIECE * (j + 1)] += _dot_tn(h, p[...])
        dw_ref[:, MY_F_OFF:MY_WIDTH] += _dot_tn(h, df_ref[...])

        @pl.when(pl.program_id(0) == pl.num_programs(0) - 1)
        def _():
            for d in range(N_DEV):
                parts = [dw_ref[:, start:start + width] for start, width in _chunk_segments(d)]
                parts.append(jnp.zeros((D_MODEL, 512 - IN_CHUNK), F32))
                out_ref[d] = jnp.concatenate(parts, axis=1).T[0:CHUNK_ROWS].astype(BF16)

    piece = pl.BlockSpec((tk, PIECE), lambda i: (i, 0))
    return pl.pallas_call(
        body,
        name="grad_w_in",
        grid=(s_len // tk,),
        in_specs=[pl.BlockSpec((tk, D_MODEL), lambda i: (i, 0))] + [piece] * 6 + [pl.BlockSpec((tk, 128), lambda i: (i, 0))],
        out_specs=_full((N_DEV, CHUNK_ROWS, D_MODEL)),
        out_shape=jax.ShapeDtypeStruct((N_DEV, CHUNK_ROWS, D_MODEL), BF16),
        scratch_shapes=[pltpu.VMEM((D_MODEL, MY_WIDTH), F32)],
        compiler_params=_params(("arbitrary",)),
    )(h16, *pieces, d_f)


def _adamw(w, g, m, v):
    m = ADAM_B1 * m + (1.0 - ADAM_B1) * g
    v = ADAM_B2 * v + (1.0 - ADAM_B2) * (g * g)
    m_hat = m / (1.0 - ADAM_B1 ** ADAM_STEP)
    v_hat = v / (1.0 - ADAM_B2 ** ADAM_STEP)
    return -ADAM_LR * (m_hat / (jnp.sqrt(v_hat) + ADAM_EPS) + ADAM_WD * w), m, v


PARAM_ORDER = ("norm_g", "w_in", "b_f", "w_pool", "pool_scale", "fox_q_g", "fox_k_g", "mem_norm_g", "w_mem_kv", "mem_q_g", "mem_k_g", "w_out")


def _update(red_in, red_kv, red_out, sred, params):
    def body(rin_ref, rkv_ref, rout_ref, sred_ref, *refs):
        ins, outs = refs[:3 * len(PARAM_ORDER)], refs[3 * len(PARAM_ORDER):]
        wide = lambda row: jnp.concatenate([sred_ref[row + k:row + k + 1, :] for k in range(4)], axis=1)
        head = lambda row: sred_ref[row:row + 1, 0:HEAD_DIM]
        grads = {
            "norm_g": lambda: wide(SB_NORM_G), "w_in": lambda: rin_ref[...], "b_f": lambda: sred_ref[SB_B_F:SB_B_F + 1, 0:FOX_HEADS],
            "pool_scale": lambda: sred_ref[SB_POOL_SCALE:SB_POOL_SCALE + 1, :], "fox_q_g": lambda: head(SB_FOX_Q),
            "fox_k_g": lambda: head(SB_FOX_K), "mem_norm_g": lambda: wide(SB_MEM_NORM_G), "w_mem_kv": lambda: rkv_ref[...],
            "mem_q_g": lambda: head(SB_MEM_Q), "mem_k_g": lambda: head(SB_MEM_K), "w_out": lambda: rout_ref[...],
        }
        for p, name in enumerate(PARAM_ORDER):
            w_ref, m_ref, v_ref = ins[3 * p:3 * p + 3]
            g_out, d_out, m_out, v_out = outs[4 * p:4 * p + 4]
            if name == "w_pool":
                for grp in range(4):
                    g = sred_ref[SB_W_POOL:SB_W_POOL + 64, 64 * grp:64 * (grp + 1)]
                    delta, m_new, v_new = _adamw(w_ref[grp], g, m_ref[grp], v_ref[grp])
                    g_out[grp], d_out[grp], m_out[grp], v_out[grp] = g, delta, m_new, v_new
            elif name == "w_in":
                for j in range(8):
                    rows = pl.ds(j, IN_CHUNK, stride=8)
                    g = rin_ref[0:IN_CHUNK, 128 * j:128 * (j + 1)]
                    delta, m_new, v_new = _adamw(w_ref[rows, :], g, m_ref[rows, :], v_ref[rows, :])
                    g_out[rows, :], d_out[rows, :], m_out[rows, :], v_out[rows, :] = g, delta, m_new, v_new
            else:
                g = grads[name]()
                delta, m_new, v_new = _adamw(w_ref[...], g, m_ref[...], v_ref[...])
                g_out[...], d_out[...], m_out[...], v_out[...] = g, delta, m_new, v_new

    flat = [t for name in PARAM_ORDER for t in params[name]]
    shapes = [params[name][0].shape for name in PARAM_ORDER for _ in range(4)]
    outs = pl.pallas_call(
        body,
        name="adamw_update",
        out_shape=tuple(jax.ShapeDtypeStruct(s, F32) for s in shapes),
        in_specs=[pl.BlockSpec(memory_space=pltpu.VMEM)] * (4 + len(flat)),
        out_specs=(pl.BlockSpec(memory_space=pltpu.VMEM),) * len(shapes),
        compiler_params=pltpu.CompilerParams(vmem_limit_bytes=VMEM_LIMIT),
    )(red_in, red_kv, red_out, sred, *flat)
    return {name: outs[4 * p:4 * p + 4] for p, name in enumerate(PARAM_ORDER)}


def kernel(x, mem, norm_g, w_in, b_f, w_pool, pool_scale, fox_q_g, fox_k_g, mem_norm_g, w_mem_kv, mem_q_g, mem_k_g, w_out, loss_target, m_norm_g, m_w_in, m_b_f, m_w_pool, m_pool_scale, m_fox_q_g, m_fox_k_g, m_mem_norm_g, m_w_mem_kv, m_mem_q_g, m_mem_k_g, m_w_out, v_norm_g, v_w_in, v_b_f, v_w_pool, v_pool_scale, v_fox_q_g, v_fox_k_g, v_mem_norm_g, v_w_mem_kv, v_mem_q_g, v_mem_k_g, v_w_out):
    given = dict(norm_g=(norm_g, m_norm_g, v_norm_g), w_in=(w_in, m_w_in, v_w_in), b_f=(b_f, m_b_f, v_b_f),
                 w_pool=(w_pool, m_w_pool, v_w_pool), pool_scale=(pool_scale, m_pool_scale, v_pool_scale),
                 fox_q_g=(fox_q_g, m_fox_q_g, v_fox_q_g), fox_k_g=(fox_k_g, m_fox_k_g, v_fox_k_g),
                 mem_norm_g=(mem_norm_g, m_mem_norm_g, v_mem_norm_g), w_mem_kv=(w_mem_kv, m_w_mem_kv, v_w_mem_kv),
                 mem_q_g=(mem_q_g, m_mem_q_g, v_mem_q_g), mem_k_g=(mem_k_g, m_mem_k_g, v_mem_k_g), w_out=(w_out, m_w_out, v_w_out))
    params = {name: tuple(t[0] if t.ndim > 2 else t for t in wmv) for name, wmv in given.items()}
    params["w_in"] = tuple(t.T.reshape(IN_CHUNK * 8, 128) for t in params["w_in"])
    x2, mem2, target2 = x[0], mem[0], loss_target[0]

    seg = jnp.asarray(np.kron(np.eye(FOX_HEADS), np.full((HEAD_DIM, HEAD_DIM), 1.0 / HEAD_DIM)), BF16)
    w_my, w_kv16, w_out16, wp_bd, bf_pad, gq8, gk8, gqm4 = _gather_w_in(
        params["w_in"][0], params["w_mem_kv"][0], params["w_out"][0], params["w_pool"][0], b_f, fox_q_g, fox_k_g, mem_q_g)
    proj, h16, q_aug, k_aug, v_h, kt_aug, vt_aug = _proj_prep(x2, norm_g, w_my, bf_pad, gq8, gk8, seg)
    olse, lse_rows, w_kv_all, w_out_all = _fox_fwd(q_aug, k_aug, vt_aug, w_kv16, w_out16)
    km, vm = _mem_prep(mem2, mem_norm_g, w_kv_all, mem_k_g)
    d_out, d_mixed, dw_out, loss_blk = _mix_out_loss(proj, olse, km, vm, wp_bd, pool_scale, gqm4, seg, x2, target2, w_out_all)

    d_ua, d_gb, d_qm, d_o, delta_rows, dwp_bd, d_scale, dkm, dvm, d_gqm = _mix_bwd(proj, olse, d_mixed, km, vm, wp_bd, pool_scale,
                                                                                    gqm4, seg)
    dw_kv, d_mem_g, d_gkm = _mem_bwd(mem2, mem_norm_g, w_kv_all, mem_k_g, dkm, dvm)
    dq_aug, dk_aug, dv_h, land_kv, land_out = _fox_bwd(q_aug, k_aug, kt_aug, v_h, d_o, lse_rows, delta_rows, dw_kv, dw_out)
    d_q, d_k, d_v, d_f, d_gq, d_gk, d_bf = _fox_post(proj, bf_pad, gq8, gk8, seg, dq_aug, dk_aug, dv_h)
    pieces = (d_ua, d_q, d_k, d_v, d_gb, d_qm)
    dw_in = _grad_w_in(h16, pieces, d_f)
    grad_x, red_in, red_kv, red_out, sred = _grad_x_exchange(pieces, d_f, w_my, x2, norm_g, d_out, dw_in, land_kv, land_out, d_mem_g,
                                                             d_scale, d_bf, d_gq, d_gk, d_gqm, d_gkm, dwp_bd, loss_blk)
    new = _update(red_in, red_kv, red_out, sred, params)
    result = [sred[SB_LOSS, 0], grad_x[None]]
    for kind in range(4):
        for name in PARAM_ORDER:
            out = new[name][kind]
            if name == "w_in":
                out = out.reshape(IN_CHUNK, D_MODEL).T
            result.append(out[None] if given[name][0].ndim > 2 else out)
    return tuple(result)
```

```python
import functools

import jax
import jax.numpy as jnp
import numpy as np
from jax import lax
from jax.experimental import pallas as pl
from jax.experimental.pallas import tpu as pltpu

F32 = jnp.float32
BF16 = jnp.bfloat16
MESH = pl.DeviceIdType.MESH

N_DEV = 8
D_MODEL = 1024
HEAD_DIM = 64
FOX_HEADS = 8
MEM_HEADS = 4
N_MEM = 256
POOL_WIDTH = 256
EPS = 1e-6
IN_WIDTH = 3080
IN_CHUNK = IN_WIDTH // N_DEV
CHUNK_ROWS = 400
F_OFF = 2048
MY_WIDTH = 3200
MY_F_OFF = 3072
PIECE = 512
GATES_OFF = 4 * PIECE
AUG_KEPT = 80
TM = 256
TMM = 512
TA = 256
HB = 8
HB_FWD = 8
AUG_ROWS = 72
HALO = 16
VMEM_LIMIT = 56 * 1024 * 1024

ADAM_LR = 0.001
ADAM_B1 = 0.9
ADAM_B2 = 0.999
ADAM_EPS = 1e-08
ADAM_WD = 0.01
ADAM_STEP = 10


def _dot(a, b):
    return jnp.dot(a, b, preferred_element_type=F32)


def _dot_nt(a, b):
    return lax.dot_general(a, b, (((1,), (1,)), ((), ())), preferred_element_type=F32)


def _dot_tn(a, b):
    return lax.dot_general(a, b, (((0,), (0,)), ((), ())), preferred_element_type=F32)


def _sigmoid(g):
    return 0.5 + 0.5 * jnp.tanh(0.5 * g)


def _split3(v):
    hi = v.astype(BF16)
    r1 = v - hi.astype(F32)
    mid = r1.astype(BF16)
    lo = (r1 - mid.astype(F32)).astype(BF16)
    return hi, mid, lo


def _rms(v):
    return lax.rsqrt(jnp.mean(v * v, axis=-1, keepdims=True) + EPS)


def _rms_bwd(a, r, g, dy):
    da = dy * g
    return r * (da - a * jnp.mean(da * a, axis=-1, keepdims=True)), dy * a


def _head_mean(z, seg):
    hi = z.astype(BF16)
    lo = (z - hi.astype(F32)).astype(BF16)
    if z.shape[1] == 256:
        return _dot(hi, seg) + _dot(lo, seg)
    half = seg[0:256, 0:256]
    return jnp.concatenate([_dot(hi[:, 0:256], half) + _dot(lo[:, 0:256], half),
                            _dot(hi[:, 256:512], half) + _dot(lo[:, 256:512], half)], axis=1)


def _head_rms(v, seg):
    return lax.rsqrt(_head_mean(v * v, seg) + EPS)


def _head_rms_bwd(a, r, g, dy, seg):
    da = dy * g
    return r * (da - a * _head_mean(da * a, seg)), dy * a


def _fold_heads(row, n_heads):
    out = row[:, 0:HEAD_DIM]
    for h in range(1, n_heads):
        out = out + row[:, HEAD_DIM * h:HEAD_DIM * (h + 1)]
    return out


def _params(sem, limit=VMEM_LIMIT):
    return pltpu.CompilerParams(dimension_semantics=sem, vmem_limit_bytes=limit)


def _full(shape):
    return pl.BlockSpec(shape, lambda *_: (0,) * len(shape))


def _chunk_segments(d):
    runs = []
    for lo, hi, shift in ((0, F_OFF, 0), (F_OFF, F_OFF + FOX_HEADS, MY_F_OFF - F_OFF), (F_OFF + FOX_HEADS, IN_WIDTH, -FOX_HEADS)):
        a, b = max(lo, IN_CHUNK * d), min(hi, IN_CHUNK * (d + 1))
        if a < b:
            runs.append((a + shift, b - a))
    return runs


def _my_place():
    x, y, c = lax.axis_index("x"), lax.axis_index("y"), lax.axis_index("c")
    return x, y, c, 4 * x + 2 * y + c


def _shard_rows(dev):
    return pl.ds(pl.multiple_of(128 * dev, 128), 128)


def _gather_w_in(w_in, w_kv, w_out, w_pool, b_f, gq, gk, gqm):
    def body(win_ref, wkv_ref, wout_ref, wp_ref, bf_ref, gq_ref, gk_ref, gqm_ref, wmy_ref, kv16_ref, out16_ref, wpbd_ref, bfpad_ref,
             gq8_ref, gk8_ref, gqm4_ref, in_all, pay_in, win_rows, send_sems, recv_sems, load_sem):
        x, y, c, me = _my_place()
        here, sibling = (x, y, c), (x, y, 1 - c)
        x_chip, y_chip, far_chip = (1 - x, y), (x, 1 - y), (1 - x, 1 - y)
        relay_from = (x ^ (1 - c), y ^ c)
        relay_to = (x ^ c, y ^ (1 - c))

        load = pltpu.make_async_copy(win_ref, win_rows, load_sem)
        load.start()
        load.wait()
        pay_in[...] = jnp.zeros((CHUNK_ROWS, D_MODEL), BF16)
        for j in range(8):
            pay_in[0:IN_CHUNK, 128 * j:128 * (j + 1)] = win_rows[pl.ds(j, IN_CHUNK, stride=8), :].astype(BF16)

        def copy(k, block, to, own=False):
            px, py, pc = block
            dst = in_all.at[4 * px + 2 * py + pc]
            return pltpu.make_async_remote_copy(src_ref=pay_in if own else dst, dst_ref=dst, send_sem=send_sems.at[k],
                                                recv_sem=recv_sems.at[k], device_id=to, device_id_type=MESH)

        first = [copy(0, here, sibling, own=True), copy(1, here, (*x_chip, c), own=True), copy(2, here, (*y_chip, c), own=True)]
        for cp in first:
            cp.start()
        in_all[me] = pay_in[...]
        kv16_ref[...] = wkv_ref[...].astype(BF16)
        out16_ref[...] = wout_ref[...].astype(BF16)
        wpbd_ref[...] = jnp.zeros((POOL_WIDTH, POOL_WIDTH), BF16)
        for grp in range(4):
            sl = slice(64 * grp, 64 * (grp + 1))
            wpbd_ref[sl, sl] = wp_ref[grp].astype(BF16)
        bfpad_ref[...] = jnp.zeros((1, 128), F32)
        bfpad_ref[:, 0:FOX_HEADS] = bf_ref[...]
        gq8_ref[...] = jnp.concatenate([gq_ref[...]] * FOX_HEADS, axis=1)
        gk8_ref[...] = jnp.concatenate([gk_ref[...]] * FOX_HEADS, axis=1)
        gqm4_ref[...] = jnp.concatenate([gqm_ref[...]] * MEM_HEADS, axis=1)
        copy(1 + c, (*relay_from, c), here).wait_recv()
        passed = [copy(3, (*relay_from, c), (*relay_to, c)), copy(4, (*relay_from, c), sibling)]
        for cp in passed:
            cp.start()
        copy(2 - c, (*relay_to, c), here).wait_recv()
        passed.append(copy(5, (*relay_to, c), sibling))
        passed[-1].start()
        copy(3, (*far_chip, c), here).wait_recv()
        passed.append(copy(6, (*far_chip, c), sibling))
        passed[-1].start()
        copy(0, sibling, here).wait_recv()
        for k, chip in ((4, relay_to), (5, relay_from), (6, far_chip)):
            copy(k, (*chip, 1 - c), here).wait_recv()
        for cp in first + passed:
            cp.wait_send()

        row_pad = jnp.zeros((512 - CHUNK_ROWS, 256), F32)
        for r0 in range(0, D_MODEL, 256):
            ch = [jnp.concatenate([in_all[d, :, r0:r0 + 256].astype(F32), row_pad], axis=0).T[:, 0:IN_CHUNK] for d in range(N_DEV)]
            lo = F_OFF - 5 * IN_CHUNK
            full = jnp.concatenate(ch[:5] + [ch[5][:, :lo], ch[5][:, lo + FOX_HEADS:], ch[6], ch[7], ch[5][:, lo:lo + FOX_HEADS],
                                             jnp.zeros((256, MY_WIDTH - IN_WIDTH), F32)], axis=1)
            wmy_ref[r0:r0 + 256, :] = full.astype(BF16)

    return pl.pallas_call(
        body,
        name="gather_w_in",
        out_shape=(jax.ShapeDtypeStruct((D_MODEL, MY_WIDTH), BF16), jax.ShapeDtypeStruct((128, 512), BF16),
                   jax.ShapeDtypeStruct((128, D_MODEL), BF16), jax.ShapeDtypeStruct((POOL_WIDTH, POOL_WIDTH), BF16),
                   jax.ShapeDtypeStruct((1, 128), F32), jax.ShapeDtypeStruct((1, PIECE), F32), jax.ShapeDtypeStruct((1, PIECE), F32),
                   jax.ShapeDtypeStruct((1, 256), F32)),
        in_specs=[pl.BlockSpec(memory_space=pl.ANY)] + [pl.BlockSpec(memory_space=pltpu.VMEM)] * 7,
        out_specs=(pl.BlockSpec(memory_space=pltpu.VMEM),) * 8,
        scratch_shapes=[
            pltpu.VMEM((N_DEV, CHUNK_ROWS, D_MODEL), BF16),
            pltpu.VMEM((CHUNK_ROWS, D_MODEL), BF16),
            pltpu.VMEM((IN_CHUNK * 8, 128), F32),
            pltpu.SemaphoreType.DMA((7,)),
            pltpu.SemaphoreType.DMA((7,)),
            pltpu.SemaphoreType.DMA,
        ],
        compiler_params=pltpu.CompilerParams(vmem_limit_bytes=VMEM_LIMIT),
    )(w_in, w_kv, w_out, w_pool, b_f, gq, gk, gqm)


def _row_block_exchange(kv_ref, out_ref, kv_dst, out_dst, send_sems, recv_sems, local_sems, gather):
    x, y, c, me = _my_place()
    remote = []
    for k in range(1, N_DEV):
        px, py, pc = x ^ (k >> 2), y ^ ((k >> 1) & 1), c ^ (k & 1)
        peer = 4 * px + 2 * py + pc
        for fam, (src, dst) in enumerate(((kv_ref, kv_dst), (out_ref, out_dst))):
            remote.append(pltpu.make_async_remote_copy(
                src_ref=src if gather else src.at[_shard_rows(peer)], dst_ref=dst.at[_shard_rows(me)] if gather else dst.at[me],
                send_sem=send_sems.at[7 * fam + k - 1], recv_sem=recv_sems.at[7 * fam + k - 1],
                device_id=(px, py, pc), device_id_type=MESH))
    local = [pltpu.make_async_copy(src if gather else src.at[_shard_rows(me)], dst.at[_shard_rows(me)] if gather else dst.at[me],
                                   local_sems.at[fam]) for fam, (src, dst) in enumerate(((kv_ref, kv_dst), (out_ref, out_dst)))]
    return remote, local


SB_ROWS, SB_W = 80, 256
SB_NORM_G, SB_MEM_NORM_G, SB_POOL_SCALE, SB_B_F, SB_FOX_Q, SB_FOX_K, SB_MEM_Q, SB_MEM_K, SB_LOSS, SB_W_POOL = 0, 4, 8, 9, 10, 11, 12, 13, 14, 16


def _alternate(*parts):
    state = [[part, 0, stages] for part, stages in parts]
    while state:
        least = min(state, key=lambda entry: entry[1] / entry[2])
        least[1] += 1
        if next(least[0], "done") == "done":
            state.remove(least)


def _log_sigmoid(z):
    return jnp.minimum(z, 0.0) - jnp.log(1.0 + jnp.exp(-jnp.abs(z)))


def _forget_lane_maps():
    to_q = np.zeros((128, FOX_HEADS * 128), np.float32)
    to_k = np.zeros((128, FOX_HEADS * 128), np.float32)
    for h in range(FOX_HEADS):
        for term in range(3):
            to_q[8 * term + h, 128 * h + 64 + term] = 1.0
            to_q[24, 128 * h + 67 + term] = 1.0
            to_k[24, 128 * h + 64 + term] = 1.0
            to_k[8 * term + h, 128 * h + 67 + term] = -1.0
    return jnp.asarray(to_q, BF16), jnp.asarray(to_k, BF16)


def _proj_prep(x, norm_g, w_my, bf_pad, gq, gk, seg):
    s_len = x.shape[0]
    n_blocks = s_len // TMM
    halves = TMM // TM
    to_q, to_k = _forget_lane_maps()
    q_off, f_off = PIECE, MY_F_OFF
    kept = 3 * PIECE + 128

    def body(x_ref, g_ref, w_ref, bf_ref, gq_ref, gk_ref, seg_ref, eq_ref, ek_ref,
             proj_a_ref, proj_b_ref, h_ref, qa_ref, ka_ref, vh_ref, kt_ref, vt_ref, kept_even, kept_odd, carry_ref):
        s = pl.program_id(0)

        def project(kept_ref):
            xv = x_ref[...]
            h_ref[...] = (xv * _rms(xv) * g_ref[...]).astype(BF16)
            yield
            for c0 in range(0, MY_WIDTH, 256):
                c1 = min(c0 + 256, MY_WIDTH)
                res = _dot(h_ref[...], w_ref[:, c0:c1])
                if c0 < 3 * PIECE:
                    proj_a_ref[:, c0:c1] = res
                if c0 >= GATES_OFF:
                    proj_b_ref[:, c0 - GATES_OFF:c1 - GATES_OFF] = res
                if q_off <= c0 < q_off + 3 * PIECE:
                    kept_ref[:, c0 - q_off:c1 - q_off] = res
                if c0 == f_off:
                    kept_ref[:, 3 * PIECE:kept] = res
                yield

        def operands(kept_ref):
            lane = lax.broadcasted_iota(jnp.int32, (TM, 128), 1)
            row = lax.broadcasted_iota(jnp.int32, (TM, TM), 0)
            col = lax.broadcasted_iota(jnp.int32, (TM, TM), 1)
            tri = jnp.where(col <= row, 1.0, 0.0).astype(BF16)
            one_at_64 = jnp.where(lane == 64, 1.0, 0.0).astype(BF16)
            zeros = jnp.zeros((TM, HEAD_DIM), BF16)
            for half in range(halves):
                rows = slice(TM * half, TM * (half + 1))
                logf = jnp.where(lane < FOX_HEADS, _log_sigmoid(kept_ref[rows, 3 * PIECE:kept] + bf_ref[...]), 0.0)
                hi, mid, lo = _split3(logf)
                cum = _dot(tri, hi) + _dot(tri, mid) + _dot(tri, lo) + carry_ref[...]
                q_all = kept_ref[rows, 0:PIECE]
                k_all = kept_ref[rows, PIECE:2 * PIECE]
                rq = _head_rms(q_all, seg_ref[...])
                rk = _head_rms(k_all, seg_ref[...])
                yield
                carry_ref[...] = cum[TM - 1:TM, :]
                f_hi, f_mid, f_lo = [t.astype(F32) for t in _split3(cum)]
                packed = (f_hi + pltpu.roll(f_mid, 8, 1) + pltpu.roll(f_lo, 16, 1)
                          + jnp.where(lane == 24, 1.0, 0.0)).astype(BF16)
                extra_q = _dot(packed, eq_ref[...]).astype(BF16)
                extra_k = _dot(packed, ek_ref[...]).astype(BF16)
                qn_all = (q_all * rq * gq_ref[...] * 0.125).astype(BF16)
                kn_all = (k_all * rk * gk_ref[...]).astype(BF16)
                v_all = kept_ref[rows, 2 * PIECE:3 * PIECE].astype(BF16)
                yield
                for h in range(FOX_HEADS):
                    sl = slice(HEAD_DIM * h, HEAD_DIM * (h + 1))
                    tile = slice(128 * h, 128 * (h + 1))
                    qa = jnp.where(lane < 64, jnp.concatenate([qn_all[:, sl], zeros], axis=1), extra_q[:, tile])
                    ka = jnp.where(lane < 64, jnp.concatenate([kn_all[:, sl], zeros], axis=1), extra_k[:, tile])
                    vh = v_all[:, sl]
                    v_aug = jnp.where(lane < 64, jnp.concatenate([vh, zeros], axis=1), one_at_64)
                    qa_ref[h, rows, :] = qa
                    ka_ref[h, rows, :] = ka
                    vh_ref[h, rows, :] = vh
                    kt_ref[h, half] = ka.T[0:AUG_KEPT]
                    vt_ref[h, half] = v_aug.T[0:AUG_KEPT]
                    if h % 2 == 1:
                        yield

        projecting = lambda kept_ref: (project(kept_ref), 1 + pl.cdiv(MY_WIDTH, 256))
        making = lambda kept_ref: (operands(kept_ref), halves * (2 + FOX_HEADS // 2))

        @pl.when(s == 0)
        def _():
            carry_ref[...] = jnp.zeros((1, 128), F32)
            _alternate(projecting(kept_even))

        @pl.when((s > 0) & (s < n_blocks) & (s % 2 == 1))
        def _():
            _alternate(projecting(kept_odd), making(kept_even))

        @pl.when((s > 0) & (s < n_blocks) & (s % 2 == 0))
        def _():
            _alternate(projecting(kept_even), making(kept_odd))

        @pl.when(s == n_blocks)
        def _():
            _alternate(making(kept_odd if n_blocks % 2 == 0 else kept_even))

    made = lambda s: jnp.minimum(s, n_blocks - 1)
    used = lambda s: jnp.maximum(s - 1, 0)
    head_blk = lambda w: pl.BlockSpec((FOX_HEADS, TMM, w), lambda s: (0, used(s), 0))
    tile_blk = pl.BlockSpec((FOX_HEADS, halves, AUG_KEPT, TM), lambda s: (0, used(s), 0, 0))
    tiled = jax.ShapeDtypeStruct((FOX_HEADS, s_len // TM, AUG_KEPT, TM), BF16)
    made_blk = lambda w: pl.BlockSpec((TMM, w), lambda s: (made(s), 0))
    return pl.pallas_call(
        body,
        name="proj_prep",
        grid=(n_blocks + 1,),
        in_specs=[pl.BlockSpec((TMM, D_MODEL), lambda s: (made(s), 0)), _full((1, D_MODEL)), _full((D_MODEL, MY_WIDTH)),
                  _full((1, 128)), _full((1, PIECE)), _full((1, PIECE)), _full((PIECE, PIECE)),
                  _full((128, FOX_HEADS * 128)), _full((128, FOX_HEADS * 128))],
        out_specs=[made_blk(3 * PIECE), made_blk(MY_WIDTH - GATES_OFF), made_blk(D_MODEL),
                   head_blk(128), head_blk(128), head_blk(HEAD_DIM), tile_blk, tile_blk],
        out_shape=[jax.ShapeDtypeStruct((s_len, 3 * PIECE), F32), jax.ShapeDtypeStruct((s_len, MY_WIDTH - GATES_OFF), F32),
                   jax.ShapeDtypeStruct((s_len, D_MODEL), BF16),
                   jax.ShapeDtypeStruct((FOX_HEADS, s_len, 128), BF16), jax.ShapeDtypeStruct((FOX_HEADS, s_len, 128), BF16),
                   jax.ShapeDtypeStruct((FOX_HEADS, s_len, HEAD_DIM), BF16), tiled, tiled],
        scratch_shapes=[pltpu.VMEM((TMM, kept), F32), pltpu.VMEM((TMM, kept), F32), pltpu.VMEM((1, 128), F32)],
        compiler_params=_params(("arbitrary",)),
    )(x, norm_g, w_my, bf_pad, gq, gk, seg, to_q, to_k)


def _mem_prep(mem, g, w_kv, gk):
    def body(mem_ref, g_ref, w_ref, gk_ref, km_ref, vm_ref):
        m = mem_ref[...]
        kv = _dot((m * _rms(m) * g_ref[...]).astype(BF16), w_ref[...])
        for h in range(MEM_HEADS):
            sl = slice(HEAD_DIM * h, HEAD_DIM * (h + 1))
            kh = kv[:, sl]
            km_ref[:, sl] = (kh * _rms(kh) * gk_ref[...]).astype(BF16)
        vm_ref[...] = kv[:, 256:512].astype(BF16)

    return pl.pallas_call(
        body,
        name="mem_prep",
        out_shape=(jax.ShapeDtypeStruct((N_MEM, 256), BF16),) * 2,
        in_specs=[pl.BlockSpec(memory_space=pltpu.VMEM)] * 4,
        out_specs=(pl.BlockSpec(memory_space=pltpu.VMEM),) * 2,
        compiler_params=pltpu.CompilerParams(vmem_limit_bytes=VMEM_LIMIT),
    )(mem, g, w_kv, gk)


def _causal_mask_t():
    row = lax.broadcasted_iota(jnp.int32, (TA, TA), 0)
    col = lax.broadcasted_iota(jnp.int32, (TA, TA), 1)
    return row <= col


def _fox_fwd(q_aug, k_aug, vt_aug, w_kv16, w_out16):
    s_len = q_aug.shape[1]
    n_tiles = s_len // TA
    hb = HB_FWD
    n_groups = FOX_HEADS // hb

    def body(q_ref, k_new, vt_new, kv16_ref, out16_ref, o_ref, st_ref, kv_all, out_all, k_ref, vt_ref, send_sems, recv_sems, local_sems):
        qi = pl.program_id(1)
        for h in range(hb):
            k_ref[h, pl.ds(pl.multiple_of(qi * TA, TA), TA), :] = k_new[h]
            vt_ref[h, qi] = vt_new[h, 0]
        remote, local = _row_block_exchange(kv16_ref, out16_ref, kv_all, out_all, send_sems, recv_sems, local_sems, gather=True)

        @pl.when((pl.program_id(0) == 0) & (qi == 0))
        def _():
            for cp in remote + local:
                cp.start()

        qs = [q_ref[h] for h in range(hb)]

        def step(t0, carry, masked, width):
            rows = pl.ds(pl.multiple_of(t0 * TA, TA), width * TA)
            scores = [_dot_nt(k_ref[h, rows, :], qs[h]) for h in range(hb)]
            soft = []
            for h in range(hb):
                m, _ = carry[h]
                s = jnp.where(_causal_mask_t(), scores[h], -1e30) if masked else scores[h]
                m_new = jnp.maximum(m, jnp.max(s, axis=0, keepdims=True))
                soft.append((m_new, jnp.exp(m - m_new), jnp.exp(s - m_new).astype(BF16)))
            out = []
            for h, (m_new, alpha, p) in enumerate(soft):
                acc = alpha * carry[h][1]
                for t in range(width):
                    acc = acc + _dot(vt_ref[h, t0 + t, 0:AUG_ROWS, :], p[t * TA:(t + 1) * TA])
                out.append((m_new, acc))
            return tuple(out)

        init = tuple((jnp.full((1, TA), -1e30, F32), jnp.zeros((AUG_ROWS, TA), F32)) for _ in range(hb))
        carry = lax.fori_loop(0, qi // 2, lambda j, cr: step(2 * j, cr, False, 2), init)
        carry = lax.fori_loop(2 * (qi // 2), qi, lambda j, cr: step(j, cr, False, 1), carry)
        carry = step(qi, carry, True, 1)
        for h in range(hb):
            m, acc = carry[h]
            l = acc[HEAD_DIM:HEAD_DIM + 1, :]
            lse = m + jnp.log(l)
            o_ref[h] = jnp.concatenate([acc[0:HEAD_DIM] / l, jnp.broadcast_to(lse, (HEAD_DIM, TA))], axis=0).T
            st_ref[h, 0] = jnp.broadcast_to(lse, (8, TA))

        @pl.when((pl.program_id(0) == n_groups - 1) & (qi == n_tiles - 1))
        def _():
            for cp in remote:
                cp.wait_recv()
            for cp in remote:
                cp.wait_send()
            for cp in local:
                cp.wait()

    hbm = pl.BlockSpec(memory_space=pl.ANY)
    return pl.pallas_call(
        body,
        name="fox_fwd",
        grid=(n_groups, n_tiles),
        in_specs=[pl.BlockSpec((hb, TA, 128), lambda g, i: (g, i, 0)), pl.BlockSpec((hb, TA, 128), lambda g, i: (g, i, 0)),
                  pl.BlockSpec((hb, 1, AUG_KEPT, TA), lambda g, i: (g, i, 0, 0)), hbm, hbm],
        out_specs=[pl.BlockSpec((hb, TA, 128), lambda g, i: (g, i, 0)), pl.BlockSpec((hb, 1, 8, TA), lambda g, i: (g, i, 0, 0)),
                   hbm, hbm],
        out_shape=[jax.ShapeDtypeStruct((FOX_HEADS, s_len, 128), F32), jax.ShapeDtypeStruct((FOX_HEADS, n_tiles, 8, TA), F32),
                   jax.ShapeDtypeStruct((D_MODEL, 512), BF16), jax.ShapeDtypeStruct((D_MODEL, D_MODEL), BF16)],
        scratch_shapes=[pltpu.VMEM((hb, s_len, 128), BF16), pltpu.VMEM((hb, n_tiles, AUG_KEPT, TA), BF16),
                        pltpu.SemaphoreType.DMA((14,)), pltpu.SemaphoreType.DMA((14,)), pltpu.SemaphoreType.DMA((2,))],
        compiler_params=_params(("arbitrary", "arbitrary")),
    )(q_aug, k_aug, vt_aug, w_kv16, w_out16)


def _lane_group(lane, a, b, c, d):
    return jnp.where(lane < 64, a, jnp.where(lane < 128, b, jnp.where(lane < 192, c, d)))


def _pool_count(row0):
    lane = lax.broadcasted_iota(jnp.int32, (TM, POOL_WIDTH), 1)
    t1 = row0 + lax.broadcasted_iota(jnp.int32, (TM, POOL_WIDTH), 0) + 1
    return 1.0 / jnp.minimum(t1, _lane_group(lane, 2, 4, 8, 16)).astype(F32), lane


def _pool_delta(u, halo, cnt, lane):
    xe = jnp.concatenate([halo, u], axis=0)
    s2 = xe + pltpu.roll(xe, 1, 0)
    s4 = s2 + pltpu.roll(s2, 2, 0)
    s8 = s4 + pltpu.roll(s4, 4, 0)
    s16 = s8 + pltpu.roll(s8, 8, 0)
    win = _lane_group(lane, s2[HALO:], s4[HALO:], s8[HALO:], s16[HALO:])
    return win * cnt - u


def _pool_delta_bwd(dd, dd_next, cnt, cnt_next, lane):
    ee = jnp.concatenate([dd * cnt, dd_next * cnt_next], axis=0)
    n = TM + HALO
    r2 = ee + pltpu.roll(ee, n - 1, 0)
    r4 = r2 + pltpu.roll(r2, n - 2, 0)
    r8 = r4 + pltpu.roll(r4, n - 4, 0)
    r16 = r8 + pltpu.roll(r8, n - 8, 0)
    return _lane_group(lane, r2[:TM], r4[:TM], r8[:TM], r16[:TM]) - dd


def _mem_attn(qm, gq, seg, km_ref, vm_ref):
    r = _head_rms(qm, seg)
    a = qm * r
    q16 = (a * gq * 0.125).astype(BF16)
    heads = []
    for h in range(MEM_HEADS):
        sl = slice(HEAD_DIM * h, HEAD_DIM * (h + 1))
        s = _dot_nt(q16[:, sl], km_ref[:, sl])
        e = jnp.exp(s - jnp.max(s, axis=-1, keepdims=True))
        p = e * (1.0 / jnp.sum(e, axis=-1, keepdims=True))
        heads.append((p, _dot(p.astype(BF16), vm_ref[:, sl])))
    return a, r, q16, heads


def _mem_attn_stages(qm, gq, seg, km_ref, vm_ref, result):
    r = _head_rms(qm, seg)
    a = qm * r
    q16 = (a * gq * 0.125).astype(BF16)
    yield
    head_slices = [slice(HEAD_DIM * h, HEAD_DIM * (h + 1)) for h in range(MEM_HEADS)]
    scores = [_dot_nt(q16[:, sl], km_ref[:, sl]) for sl in head_slices]
    yield
    heads = []
    for s, sl in zip(scores, head_slices):
        e = jnp.exp(s - jnp.max(s, axis=-1, keepdims=True))
        p = e * (1.0 / jnp.sum(e, axis=-1, keepdims=True))
        heads.append((p, _dot(p.astype(BF16), vm_ref[:, sl])))
    result.extend([a, r, q16, heads])
    yield


def _mix_out_loss(proj_a, proj_b, olse, km, vm, wp_bd, pool_scale, gq_m, seg, x, target, w_out):
    s_len = x.shape[0]
    n_blocks = s_len // TMM
    halves = TMM // TM

    def body(ua_ref, halo_ref, gb_ref, qm_ref, ol_ref, km_ref, vm_ref, wp_ref, sc_ref, gq_ref, seg_ref, x_ref, t_ref, w_ref,
             dout_ref, dmix_ref, dw16_ref, loss_ref, mixed_even, mixed_odd, d16_ref, dw_ref):
        s = pl.program_id(0)
        chunks = [slice(c0, c0 + 256) for c0 in range(0, D_MODEL, 256)]

        def matmuls(mixed_ref):
            for cols in chunks:
                err = x_ref[:, cols] + _dot(mixed_ref[...], w_ref[:, cols]) - t_ref[:, cols]
                loss_ref[...] += (0.5 / D_MODEL) * jnp.sum(err * err)
                d_out = err / float(D_MODEL)
                dout_ref[:, cols] = d_out
                d16_ref[:, cols] = d_out.astype(BF16)
                yield
            for cols in chunks:
                dmix_ref[:, cols] = _dot_nt(d16_ref[...], w_ref[cols, :])
                yield
            for cols in chunks:
                dw_ref[:, cols] += _dot_tn(mixed_ref[...], d16_ref[:, cols])
                yield

        def concatenation(mixed_ref):
            for half in range(halves):
                r0 = TM * half
                rows = slice(r0, r0 + TM)
                u = ua_ref[rows, 0:256]
                g_a = ua_ref[rows, 256:512]
                halo = jnp.where(s > 0, halo_ref[...], 0.0) if half == 0 else ua_ref[r0 - HALO:r0, 0:256]
                cnt, lane = _pool_count(s * TMM + r0)
                d = _pool_delta(u, halo, cnt, lane).astype(BF16)
                y_a = _dot(d, wp_ref[...]) * sc_ref[...]
                mixed_ref[rows, 0:256] = (y_a * (g_a * _sigmoid(g_a))).astype(BF16)
                yield
                g_b = gb_ref[rows, :]
                y_b = jnp.concatenate([ol_ref[h, rows, 0:HEAD_DIM] for h in range(FOX_HEADS)], axis=1)
                mixed_ref[rows, 256:768] = (y_b * (g_b * _sigmoid(g_b))).astype(BF16)
                yield
                attn = []
                stages = _mem_attn_stages(qm_ref[rows, 0:256], gq_ref[...], seg_ref[0:256, 0:256], km_ref, vm_ref, attn)
                next(stages)
                yield
                next(stages)
                yield
                next(stages)
                g_m = qm_ref[rows, 256:512]
                y_m = jnp.concatenate([y for _, y in attn[3]], axis=1)
                mixed_ref[rows, 768:1024] = (y_m * (g_m * _sigmoid(g_m))).astype(BF16)
                yield

        multiplying = lambda mixed_ref: (matmuls(mixed_ref), 3 * len(chunks))
        building = lambda mixed_ref: (concatenation(mixed_ref), 5 * halves)

        @pl.when(s == 0)
        def _():
            dw_ref[...] = jnp.zeros((D_MODEL, D_MODEL), F32)
            loss_ref[...] = jnp.zeros((8, 128), F32)
            _alternate(building(mixed_even))

        @pl.when((s > 0) & (s < n_blocks) & (s % 2 == 1))
        def _():
            _alternate(multiplying(mixed_even), building(mixed_odd))

        @pl.when((s > 0) & (s < n_blocks) & (s % 2 == 0))
        def _():
            _alternate(multiplying(mixed_odd), building(mixed_even))

        @pl.when(s == n_blocks)
        def _():
            _alternate(multiplying(mixed_odd if n_blocks % 2 == 0 else mixed_even))
            dw16_ref[...] = dw_ref[...].astype(BF16)

    build = lambda s: jnp.minimum(s, n_blocks - 1)
    use = lambda s: jnp.maximum(s - 1, 0)
    piece = lambda j: pl.BlockSpec((TMM, PIECE), lambda s: (build(s), j))
    halo_blk = pl.BlockSpec((HALO, 256), lambda s: (jnp.maximum(build(s) * (TMM // HALO) - 1, 0), 0))
    row = pl.BlockSpec((TMM, D_MODEL), lambda s: (use(s), 0))
    return pl.pallas_call(
        body,
        name="mix_out_loss",
        grid=(n_blocks + 1,),
        in_specs=[piece(0), halo_blk, piece(0), piece(1), pl.BlockSpec((FOX_HEADS, TMM, 128), lambda s: (0, build(s), 0)),
                  _full((N_MEM, 256)), _full((N_MEM, 256)), _full((256, 256)), _full((1, 256)), _full((1, 256)),
                  _full((PIECE, PIECE)), row, row, _full((D_MODEL, D_MODEL))],
        out_specs=[row, row, _full((D_MODEL, D_MODEL)), _full((8, 128))],
        out_shape=[jax.ShapeDtypeStruct((s_len, D_MODEL), F32), jax.ShapeDtypeStruct((s_len, D_MODEL), F32),
                   jax.ShapeDtypeStruct((D_MODEL, D_MODEL), BF16), jax.ShapeDtypeStruct((8, 128), F32)],
        scratch_shapes=[pltpu.VMEM((TMM, D_MODEL), BF16), pltpu.VMEM((TMM, D_MODEL), BF16), pltpu.VMEM((TMM, D_MODEL), BF16),
                        pltpu.VMEM((D_MODEL, D_MODEL), F32)],
        compiler_params=_params(("arbitrary",)),
    )(proj_a, proj_a, proj_b, proj_b, olse, km, vm, wp_bd, pool_scale, gq_m, seg, x, target, w_out)


def _silu_pair(g):
    s = _sigmoid(g)
    return g * s, s * (1.0 + g * (1.0 - s))


def _mix_bwd(proj_a, proj_b, olse, d_mixed, km, vm, wp_bd, pool_scale, gq_m, seg):
    s_len = proj_a.shape[0]
    n_tiles = s_len // TM

    def body(ua_ref, halo_ref, ga_next_ref, gb_ref, qm_ref, ol_ref, dm_ref, dm_next_ref, km_ref, vm_ref, wp_ref, sc_ref, gq_ref,
             seg_ref, dua_ref, dgb_ref, dqm_ref, do_ref, dl_ref, dwp_ref, dsc_ref, dkm_ref, dvm_ref, dgq_ref):
        i = pl.program_id(0)

        @pl.when(i == 0)
        def _():
            dwp_ref[...] = jnp.zeros((256, 256), F32)
            dsc_ref[...] = jnp.zeros((1, 256), F32)
            dkm_ref[...] = jnp.zeros((N_MEM, 256), F32)
            dvm_ref[...] = jnp.zeros((N_MEM, 256), F32)
            dgq_ref[...] = jnp.zeros((1, HEAD_DIM), F32)

        def pooling_mixer():
            u = ua_ref[:, 0:256]
            g_a = ua_ref[:, 256:512]
            halo = jnp.where(i > 0, halo_ref[...], 0.0)
            cnt, lane = _pool_count(i * TM)
            d16 = _pool_delta(u, halo, cnt, lane).astype(BF16)
            t = _dot(d16, wp_ref[...])
            yield
            y_a = t * sc_ref[...]
            silu_a, dsilu_a = _silu_pair(g_a)
            dm_a = dm_ref[:, 0:256]
            dy_a = dm_a * silu_a
            dsc_ref[...] += jnp.sum(dy_a * t, axis=0, keepdims=True)
            dt16 = (dy_a * sc_ref[...]).astype(BF16)
            dwp_ref[...] += _dot_tn(d16, dt16)
            dd = _dot_nt(dt16, wp_ref[...])
            dua_ref[:, 256:512] = (dm_a * y_a * dsilu_a).astype(BF16)
            yield
            g_next = ga_next_ref[...]
            dt_next = (dm_next_ref[...] * (g_next * _sigmoid(g_next)) * sc_ref[...]).astype(BF16)
            dd_next = jnp.where(i < n_tiles - 1, _dot_nt(dt_next, wp_ref[...]), 0.0)
            cnt_next = _pool_count((i + 1) * TM)[0][0:HALO]
            dua_ref[:, 0:256] = _pool_delta_bwd(dd, dd_next, cnt, cnt_next, lane).astype(BF16)
            yield

        def output_gate():
            silu_b, dsilu_b = _silu_pair(gb_ref[...])
            dm_b = dm_ref[:, 256:768]
            o_all = jnp.concatenate([ol_ref[h][:, 0:HEAD_DIM] for h in range(FOX_HEADS)], axis=1)
            d_o = dm_b * silu_b
            dgb_ref[...] = (dm_b * o_all * dsilu_b).astype(BF16)
            d_o16 = d_o.astype(BF16)
            for h in range(FOX_HEADS):
                do_ref[h] = d_o16[:, HEAD_DIM * h:HEAD_DIM * (h + 1)]
            yield
            prod = d_o * o_all
            hi = prod.astype(BF16)
            lo = (prod - hi.astype(F32)).astype(BF16)
            head_of_lane = lax.broadcasted_iota(jnp.int32, (FOX_HEADS, PIECE), 1) // HEAD_DIM
            pick = jnp.where(head_of_lane == lax.broadcasted_iota(jnp.int32, (FOX_HEADS, PIECE), 0), 1.0, 0.0).astype(BF16)
            delta = _dot_nt(pick, hi) + _dot_nt(pick, lo)
            for h in range(FOX_HEADS):
                dl_ref[h, 0] = jnp.broadcast_to(delta[h:h + 1, :], (8, TM))
            yield

        def memory_attention():
            seg = seg_ref[0:256, 0:256]
            a, r, q16, heads = _mem_attn(qm_ref[:, 0:256], gq_ref[...], seg, km_ref, vm_ref)
            yield
            silu_m, dsilu_m = _silu_pair(qm_ref[:, 256:512])
            dm_m = dm_ref[:, 768:1024]
            y_m = jnp.concatenate([y for _, y in heads], axis=1)
            dqm_ref[:, 256:512] = (dm_m * y_m * dsilu_m).astype(BF16)
            dy16_all = (dm_m * silu_m).astype(BF16)
            d_scores = []
            for h in range(MEM_HEADS):
                sl = slice(HEAD_DIM * h, HEAD_DIM * (h + 1))
                p = heads[h][0]
                dy16 = dy16_all[:, sl]
                dp = _dot_nt(dy16, vm_ref[:, sl])
                dvm_ref[:, sl] += _dot_tn(p.astype(BF16), dy16)
                ds16 = (p * (dp - jnp.sum(dp * p, axis=-1, keepdims=True))).astype(BF16)
                dkm_ref[:, sl] += _dot_tn(ds16, q16[:, sl])
                d_scores.append(_dot(ds16, km_ref[:, sl]))
                yield
            dq, dg_rows = _head_rms_bwd(a, r, gq_ref[...], jnp.concatenate(d_scores, axis=1) * 0.125, seg)
            dqm_ref[:, 0:256] = dq.astype(BF16)
            dgq_ref[...] += _fold_heads(jnp.sum(dg_rows, axis=0, keepdims=True), MEM_HEADS)
            yield

        _alternate((pooling_mixer(), 3))
        _alternate((memory_attention(), 2 + MEM_HEADS), (output_gate(), 2))

    n_halo = s_len // HALO
    piece = lambda j: pl.BlockSpec((TM, PIECE), lambda i: (i, j))
    before = pl.BlockSpec((HALO, 256), lambda i: (jnp.maximum(i * (TM // HALO) - 1, 0), 0))
    after = lambda j: pl.BlockSpec((HALO, 256), lambda i: (jnp.minimum((i + 1) * (TM // HALO), n_halo - 1), j))
    row = pl.BlockSpec((TM, D_MODEL), lambda i: (i, 0))
    out_piece = pl.BlockSpec((TM, PIECE), lambda i: (i, 0))
    return pl.pallas_call(
        body,
        name="mix_bwd",
        grid=(n_tiles,),
        in_specs=[piece(0), before, after(1), piece(0), piece(1), pl.BlockSpec((FOX_HEADS, TM, 128), lambda i: (0, i, 0)),
                  row, after(0), _full((N_MEM, 256)), _full((N_MEM, 256)), _full((256, 256)), _full((1, 256)), _full((1, 256)),
                  _full((PIECE, PIECE))],
        out_specs=[out_piece, out_piece, out_piece, pl.BlockSpec((FOX_HEADS, TM, HEAD_DIM), lambda i: (0, i, 0)),
                   pl.BlockSpec((FOX_HEADS, 1, 8, TM), lambda i: (0, i, 0, 0)),
                   _full((256, 256)), _full((1, 256)), _full((N_MEM, 256)), _full((N_MEM, 256)), _full((1, HEAD_DIM))],
        out_shape=[jax.ShapeDtypeStruct((s_len, PIECE), BF16)] * 3 + [
            jax.ShapeDtypeStruct((FOX_HEADS, s_len, HEAD_DIM), BF16),
            jax.ShapeDtypeStruct((FOX_HEADS, n_tiles, 8, TM), F32),
            jax.ShapeDtypeStruct((256, 256), F32), jax.ShapeDtypeStruct((1, 256), F32),
            jax.ShapeDtypeStruct((N_MEM, 256), F32), jax.ShapeDtypeStruct((N_MEM, 256), F32),
            jax.ShapeDtypeStruct((1, HEAD_DIM), F32)],
        compiler_params=_params(("arbitrary",)),
    )(proj_a, proj_a, proj_a, proj_b, proj_b, olse, d_mixed, d_mixed, km, vm, wp_bd, pool_scale, gq_m, seg)


def _fox_bwd(q_aug, k_aug, kt_aug, v_h, d_o, lse_rows, delta_rows, dw_kv16, dw_out16):
    s_len = q_aug.shape[1]
    n_tiles = s_len // TA
    n_groups = FOX_HEADS // HB

    def body(q_hbm, k_ref, kt_ref, v_ref, do_hbm, st_ref, dl_ref, dkv16_ref, dout16_ref, dq_ref, dk_ref, dv_ref, land_kv, land_out,
             dqt_ref, q_ref, do_ref, send_sems, recv_sems, local_sems, tile_sems):
        j = pl.program_id(1)
        remote, local = _row_block_exchange(dkv16_ref, dout16_ref, land_kv, land_out, send_sems, recv_sems, local_sems, gather=False)

        def tile_loads(i):
            rows = pl.ds(pl.multiple_of(i * TA, TA), TA)
            return [pltpu.make_async_copy(q_hbm.at[:, rows, :], q_ref.at[:, rows, :], tile_sems.at[2 * i]),
                    pltpu.make_async_copy(do_hbm.at[:, rows, :], do_ref.at[:, rows, :], tile_sems.at[2 * i + 1])]

        @pl.when((pl.program_id(0) == 0) & (j == 0))
        def _():
            for i in range(n_tiles):
                for cp in tile_loads(i):
                    cp.start()
            for cp in remote + local:
                cp.start()

        @pl.when(j == 0)
        def _():
            dqt_ref[...] = jnp.zeros((HB, n_tiles, AUG_ROWS, TA), F32)

        ks = [k_ref[h] for h in range(HB)]
        kts = [kt_ref[h, 0, 0:AUG_ROWS, :] for h in range(HB)]
        vs = [v_ref[h] for h in range(HB)]

        def pair(i0, acc, masked, width):
            @pl.when(j == 0)
            def _():
                for t in range(width):
                    for cp in tile_loads(i0 + t):
                        cp.wait()

            rows = pl.ds(pl.multiple_of(i0 * TA, TA), width * TA)
            qs = [q_ref[h, rows, :] for h in range(HB)]
            d_os = [do_ref[h, rows, :] for h in range(HB)]
            row_stat = lambda ref, h: jnp.concatenate([ref[h, i0 + t, 0:1, :] for t in range(width)], axis=1)
            scores = [(_dot_nt(ks[h], qs[h]), _dot_nt(vs[h], d_os[h])) for h in range(HB)]
            probs = []
            for h in range(HB):
                s, dp = scores[h]
                if masked:
                    s = jnp.where(_causal_mask_t(), s, -1e30)
                p = jnp.exp(s - row_stat(st_ref, h))
                probs.append((p.astype(BF16), (p * (dp - row_stat(dl_ref, h))).astype(BF16)))
            out = []
            for h in range(HB):
                p16, ds16 = probs[h]
                dqt = _dot(kts[h], ds16)
                for t in range(width):
                    dqt_ref[h, i0 + t] += dqt[:, t * TA:(t + 1) * TA]
                out.append((acc[h][0] + _dot(ds16, qs[h]), acc[h][1] + _dot(p16, d_os[h])))
            return tuple(out)

        zero = tuple((jnp.zeros((TA, 128), F32), jnp.zeros((TA, HEAD_DIM), F32)) for _ in range(HB))
        acc = pair(j, zero, True, 1)
        odd = (n_tiles - 1 - j) % 2
        acc = lax.fori_loop(j + 1, j + 1 + odd, lambda i, a: pair(i, a, False, 1), acc)
        acc = lax.fori_loop(0, (n_tiles - 1 - j) // 2, lambda t, a: pair(j + 1 + odd + 2 * t, a, False, 2), acc)
        pad = jnp.zeros((128 - AUG_ROWS, TA), F32)
        for h in range(HB):
            dk_ref[h] = acc[h][0]
            dv_ref[h] = acc[h][1].astype(BF16)
            dq_ref[h] = jnp.concatenate([dqt_ref[h, j], pad], axis=0).T

        @pl.when((pl.program_id(0) == n_groups - 1) & (j == n_tiles - 1))
        def _():
            for cp in remote:
                cp.wait_recv()
            for cp in remote:
                cp.wait_send()
            for cp in local:
                cp.wait()

    assert n_groups == 1
    resident = pl.BlockSpec(memory_space=pltpu.VMEM)
    rows_blk = lambda r: resident
    tile = lambda w: pl.BlockSpec((HB, TA, w), lambda g, j: (g, j, 0))
    hbm = pl.BlockSpec(memory_space=pl.ANY)
    return pl.pallas_call(
        body,
        name="fox_bwd",
        grid=(n_groups, n_tiles),
        in_specs=[hbm, tile(128), pl.BlockSpec((HB, 1, AUG_KEPT, TA), lambda g, j: (g, j, 0, 0)), tile(HEAD_DIM),
                  hbm, rows_blk(8), rows_blk(8), hbm, hbm],
        out_specs=[tile(128), tile(128), tile(HEAD_DIM), hbm, hbm],
        out_shape=[jax.ShapeDtypeStruct((FOX_HEADS, s_len, 128), F32), jax.ShapeDtypeStruct((FOX_HEADS, s_len, 128), F32),
                   jax.ShapeDtypeStruct((FOX_HEADS, s_len, HEAD_DIM), BF16),
                   jax.ShapeDtypeStruct((N_DEV, 128, 512), BF16), jax.ShapeDtypeStruct((N_DEV, 128, D_MODEL), BF16)],
        scratch_shapes=[pltpu.VMEM((HB, n_tiles, AUG_ROWS, TA), F32),
                        pltpu.VMEM((HB, s_len, 128), BF16), pltpu.VMEM((HB, s_len, HEAD_DIM), BF16),
                        pltpu.SemaphoreType.DMA((14,)), pltpu.SemaphoreType.DMA((14,)), pltpu.SemaphoreType.DMA((2,)),
                        pltpu.SemaphoreType.DMA((2 * n_tiles,))],
        compiler_params=_params(("arbitrary", "arbitrary")),
    )(q_aug, k_aug, kt_aug, v_h, d_o, lse_rows, delta_rows, dw_kv16, dw_out16)


def _fox_post(proj_a, proj_b, bf_pad, gq, gk, seg, dq_aug, dk_aug, dv_h):
    s_len = proj_a.shape[0]
    n_tiles = s_len // TM

    def body(q_ref, k_ref, f_ref, bf_ref, gq_ref, gk_ref, seg_ref, dqa_ref, dka_ref, dvh_ref,
             dq_ref, dk_ref, dv_ref, df_ref, dgq_ref, dgk_ref, dbf_ref, carry_ref):
        @pl.when(pl.program_id(0) == 0)
        def _():
            carry_ref[...] = jnp.zeros((1, 128), F32)
            dgq_ref[...] = jnp.zeros((1, HEAD_DIM), F32)
            dgk_ref[...] = jnp.zeros((1, HEAD_DIM), F32)
            dbf_ref[...] = jnp.zeros((1, 128), F32)

        lane = lax.broadcasted_iota(jnp.int32, (TM, 128), 1)
        seg = seg_ref[...]
        dqas = [dqa_ref[h] for h in range(FOX_HEADS)]
        dkas = [dka_ref[h] for h in range(FOX_HEADS)]
        def head_norms():
            q_all = q_ref[...]
            k_all = k_ref[...]
            rq = _head_rms(q_all, seg)
            rk = _head_rms(k_all, seg)
            yield
            dy_q = jnp.concatenate([t[:, 0:HEAD_DIM] for t in dqas], axis=1) * 0.125
            dq, dgq_rows = _head_rms_bwd(q_all * rq, rq, gq_ref[...], dy_q, seg)
            dq_ref[...] = dq.astype(BF16)
            dgq_ref[...] += _fold_heads(jnp.sum(dgq_rows, axis=0, keepdims=True), FOX_HEADS)
            yield
            dy_k = jnp.concatenate([t[:, 0:HEAD_DIM] for t in dkas], axis=1)
            dk, dgk_rows = _head_rms_bwd(k_all * rk, rk, gk_ref[...], dy_k, seg)
            dk_ref[...] = dk.astype(BF16)
            dgk_ref[...] += _fold_heads(jnp.sum(dgk_rows, axis=0, keepdims=True), FOX_HEADS)
            dv_ref[...] = jnp.concatenate([dvh_ref[h] for h in range(FOX_HEADS)], axis=1)
            yield

        def forget_gates():
            d_cum = jnp.zeros((TM, 128), F32)
            for h in range(FOX_HEADS):
                d_cum = jnp.where(lane == h, dqas[h][:, 64:65] - dkas[h][:, 67:68], d_cum)
            row = lax.broadcasted_iota(jnp.int32, (TM, TM), 0)
            col = lax.broadcasted_iota(jnp.int32, (TM, TM), 1)
            tri = jnp.where(col >= row, 1.0, 0.0).astype(BF16)
            hi, mid, lo = _split3(d_cum)
            d_logf = _dot(tri, hi) + _dot(tri, mid) + _dot(tri, lo) + carry_ref[...]
            yield
            carry_ref[...] = d_logf[0:1, :]
            z = f_ref[...] + bf_ref[...]
            d_f = jnp.where(lane < FOX_HEADS, d_logf / (1.0 + jnp.exp(z)), 0.0)
            df_ref[...] = d_f.astype(BF16)
            dbf_ref[...] += jnp.sum(d_f, axis=0, keepdims=True)
            yield

        _alternate((head_norms(), 3), (forget_gates(), 2))

    rev = lambda i: n_tiles - 1 - i
    col_blk = lambda j: pl.BlockSpec((TM, PIECE), lambda i: (rev(i), j))
    head_blk = lambda w: pl.BlockSpec((FOX_HEADS, TM, w), lambda i: (0, rev(i), 0))
    out_piece = pl.BlockSpec((TM, PIECE), lambda i: (rev(i), 0))
    return pl.pallas_call(
        body,
        name="fox_post",
        grid=(n_tiles,),
        in_specs=[col_blk(1), col_blk(2), pl.BlockSpec((TM, 128), lambda i: (rev(i), (MY_F_OFF - GATES_OFF) // 128)),
                  _full((1, 128)), _full((1, PIECE)), _full((1, PIECE)), _full((PIECE, PIECE)),
                  head_blk(128), head_blk(128), head_blk(HEAD_DIM)],
        out_specs=[out_piece, out_piece, out_piece, pl.BlockSpec((TM, 128), lambda i: (rev(i), 0)),
                   _full((1, HEAD_DIM)), _full((1, HEAD_DIM)), _full((1, 128))],
        out_shape=[jax.ShapeDtypeStruct((s_len, PIECE), BF16)] * 3 + [
            jax.ShapeDtypeStruct((s_len, 128), BF16), jax.ShapeDtypeStruct((1, HEAD_DIM), F32),
            jax.ShapeDtypeStruct((1, HEAD_DIM), F32), jax.ShapeDtypeStruct((1, 128), F32)],
        scratch_shapes=[pltpu.VMEM((1, 128), F32)],
        compiler_params=_params(("arbitrary",)),
    )(proj_a, proj_a, proj_b, bf_pad, gq, gk, seg, dq_aug, dk_aug, dv_h)


def _mem_bwd(mem, g, w_kv, gk, dkm, dvm):
    def body(mem_ref, g_ref, w_ref, gk_ref, dkm_ref, dvm_ref, dw_ref, dg_ref, dgk_ref, dkv_ref):
        m = mem_ref[...]
        r = _rms(m)
        a = m * r
        mn16 = (a * g_ref[...]).astype(BF16)
        kv = _dot(mn16, w_ref[...])
        dgk = jnp.zeros((N_MEM, HEAD_DIM), F32)
        for h in range(MEM_HEADS):
            sl = slice(HEAD_DIM * h, HEAD_DIM * (h + 1))
            kh = kv[:, sl]
            rk = _rms(kh)
            dk, dg_rows = _rms_bwd(kh * rk, rk, gk_ref[...], dkm_ref[:, sl])
            dkv_ref[:, sl] = dk.astype(BF16)
            dgk = dgk + dg_rows
        dkv_ref[:, 256:512] = dvm_ref[...].astype(BF16)
        dgk_ref[...] = jnp.sum(dgk, axis=0, keepdims=True)
        dkv16 = dkv_ref[...]
        dw_ref[...] = _dot_tn(mn16, dkv16).astype(BF16)
        dg_ref[...] = jnp.sum(_dot_nt(dkv16, w_ref[...]) * a, axis=0, keepdims=True)

    return pl.pallas_call(
        body,
        name="mem_bwd",
        out_shape=(jax.ShapeDtypeStruct((D_MODEL, 512), BF16), jax.ShapeDtypeStruct((1, D_MODEL), F32),
                   jax.ShapeDtypeStruct((1, HEAD_DIM), F32)),
        in_specs=[pl.BlockSpec(memory_space=pltpu.VMEM)] * 6,
        out_specs=(pl.BlockSpec(memory_space=pltpu.VMEM),) * 3,
        scratch_shapes=[pltpu.VMEM((N_MEM, 512), BF16)],
        compiler_params=pltpu.CompilerParams(vmem_limit_bytes=VMEM_LIMIT),
    )(mem, g, w_kv, gk, dkm, dvm)


def _grad_x_exchange(pieces, d_f, w_my, x, norm_g, d_out, dw_in, land_kv, land_out, d_mem_g, d_scale, d_bf, d_gq, d_gk, d_gqm,
                     d_gkm, dwp_bd, loss_blk):
    s_len = x.shape[0]
    n_steps = s_len // TM
    step2, step3 = n_steps // 4, (11 * n_steps) // 16
    half = D_MODEL // 2

    def body(p0, p1, p2, p3, p4, p5, df_ref, x_ref, dout_ref, w_ref, g_ref, in_ref, lkv_ref, lout_ref,
             dmg, dsc, dbf, dgq, dgk, dgqm, dgkm, dwp, loss_ref,
             gx_ref, rin_ref, rkv_ref, rout_ref, sred_ref,
             dng, land1, send2a, send2b, keep2a, keep2b, land2a, land2b, send3a, send3b, land3a, land3b, sb_ref, sland,
             own4, lkv_v, lout_v, send_sems, recv_sems, load_sems):
        i = pl.program_id(0)
        x_, y_, c_, me = _my_place()
        sibling, x_nbr, y_nbr = (x_, y_, 1 - c_), (1 - x_, y_, c_), (x_, 1 - y_, c_)
        loads = [pltpu.make_async_copy(in_ref.at[2 * k + c_], own4.at[k], load_sems.at[k]) for k in range(4)]
        loads += [pltpu.make_async_copy(lkv_ref, lkv_v, load_sems.at[4]), pltpu.make_async_copy(lout_ref, lout_v, load_sems.at[5])]

        def rcopy(src, dst, sem, to):
            return pltpu.make_async_remote_copy(src_ref=src, dst_ref=dst, send_sem=send_sems.at[sem], recv_sem=recv_sems.at[sem],
                                                device_id=to, device_id_type=MESH)

        early_rows, late_rows = pl.ds(8, SB_ROWS - 8), pl.ds(0, 8)
        small_early, small_late = [], []
        for k in range(1, N_DEV):
            peer = (x_ ^ (k >> 2), y_ ^ ((k >> 1) & 1), c_ ^ (k & 1))
            small_early.append(rcopy(sb_ref.at[early_rows], sland.at[me, early_rows], k - 1, peer))
            small_late.append(rcopy(sb_ref.at[late_rows], sland.at[me, late_rows], 16 + k, peer))
        small = small_early + small_late
        stage1 = [rcopy(in_ref.at[2 * k + 1 - c_], land1.at[k], 7 + k, sibling) for k in range(4)]
        stage2 = [rcopy(send2a.at[j], land2a.at[j], 11 + j, x_nbr) for j in range(2)]
        stage2 += [rcopy(send2b.at[j], land2b.at[j], 13 + j, y_nbr) for j in range(2)]
        stage3 = [rcopy(send3a, land3a, 15, y_nbr), rcopy(send3b, land3b, 16, x_nbr)]
        top, bot = slice(0, half), slice(half, D_MODEL)

        @pl.when(i == 0)
        def _():
            dng[...] = jnp.zeros((1, D_MODEL), F32)
            for cp in stage1 + loads:
                cp.start()
            sb_ref[...] = jnp.zeros((SB_ROWS, SB_W), F32)
            sb_ref[SB_POOL_SCALE:SB_POOL_SCALE + 1, :] = dsc[...]
            sb_ref[SB_B_F:SB_B_F + 1, 0:128] = dbf[...]
            for row, ref in ((SB_FOX_Q, dgq), (SB_FOX_K, dgk), (SB_MEM_Q, dgqm), (SB_MEM_K, dgkm)):
                sb_ref[row:row + 1, 0:HEAD_DIM] = ref[...]
            sb_ref[SB_LOSS:SB_LOSS + 1, 0:128] = loss_ref[0:1, :]
            for grp in range(4):
                sl = slice(64 * grp, 64 * (grp + 1))
                sb_ref[SB_W_POOL:SB_W_POOL + 64, sl] = dwp[sl, sl]
            for cp in small_early:
                cp.start()

        @pl.when(i == step2)
        def _():
            for cp in stage1:
                cp.wait_recv()
            for cp in loads[0:4]:
                cp.wait()

            def chip_sum(k, cols):
                return own4[k, :, cols].astype(F32) + land1[k, :, cols].astype(F32)

            for j in range(2):
                send2a[j] = chip_sum(2 * (1 - x_) + j, top).astype(BF16)
                keep2a[j] = chip_sum(2 * x_ + j, top)
                send2b[j] = chip_sum(2 * j + 1 - y_, bot).astype(BF16)
                keep2b[j] = chip_sum(2 * j + y_, bot)
            for cp in stage2:
                cp.start()

        @pl.when(i == step3)
        def _():
            for cp in stage2:
                cp.wait_recv()
            for j in range(2):
                keep2a[j] = keep2a[j] + land2a[j].astype(F32)
                keep2b[j] = keep2b[j] + land2b[j].astype(F32)
            send3a[...] = keep2a[1 - y_].astype(BF16)
            send3b[...] = keep2b[1 - x_].astype(BF16)
            for cp in stage3:
                cp.start()

        dh = _dot_nt(df_ref[...], w_ref[:, MY_F_OFF:MY_WIDTH])
        for j, p in enumerate((p0, p1, p2, p3, p4, p5)):
            dh = dh + _dot_nt(p[...], w_ref[:, PIECE * j:PIECE * (j + 1)])
        xv = x_ref[...]
        r = _rms(xv)
        dx, dg_rows = _rms_bwd(xv * r, r, g_ref[...], dh)
        gx_ref[...] = dout_ref[...] + dx
        dng[...] += jnp.sum(dg_rows, axis=0, keepdims=True)

        @pl.when(i == n_steps - 1)
        def _():
            for k in range(4):
                sb_ref[SB_NORM_G + k:SB_NORM_G + k + 1, :] = dng[:, SB_W * k:SB_W * (k + 1)]
                sb_ref[SB_MEM_NORM_G + k:SB_MEM_NORM_G + k + 1, :] = dmg[:, SB_W * k:SB_W * (k + 1)]
            for cp in small_late:
                cp.start()
            sland[me] = sb_ref[...]
            for cp in stage3:
                cp.wait_recv()
            rin_ref[:, top] = keep2a[y_] + land3a[...].astype(F32)
            rin_ref[:, bot] = keep2b[x_] + land3b[...].astype(F32)
            for cp in small:
                cp.wait_recv()
            for cp in small + stage1 + stage2 + stage3:
                cp.wait_send()
            for cp in loads[4:6]:
                cp.wait()
            for land, red in ((lkv_v, rkv_ref), (lout_v, rout_ref), (sland, sred_ref)):
                acc = land[0].astype(F32)
                for d in range(1, N_DEV):
                    acc = acc + land[d].astype(F32)
                red[...] = acc

    piece = pl.BlockSpec((TM, PIECE), lambda i: (i, 0))
    row = pl.BlockSpec((TM, D_MODEL), lambda i: (i, 0))
    vmem = pl.BlockSpec(memory_space=pltpu.VMEM)
    half_chunk = lambda n, dt: pltpu.VMEM((n, CHUNK_ROWS, half), dt)
    whole = (w_my, norm_g, dw_in, land_kv, land_out, d_mem_g, d_scale, d_bf, d_gq, d_gk, d_gqm, d_gkm, dwp_bd, loss_blk)
    return pl.pallas_call(
        body,
        name="grad_x_exchange",
        grid=(n_steps,),
        in_specs=[piece] * 6 + [pl.BlockSpec((TM, 128), lambda i: (i, 0)), row, row, vmem, vmem]
        + [pl.BlockSpec(memory_space=pl.ANY)] * 3 + [vmem] * (len(whole) - 5),
        out_specs=[row, vmem, vmem, vmem, vmem],
        out_shape=[jax.ShapeDtypeStruct((s_len, D_MODEL), F32), jax.ShapeDtypeStruct((CHUNK_ROWS, D_MODEL), F32),
                   jax.ShapeDtypeStruct((128, 512), F32), jax.ShapeDtypeStruct((128, D_MODEL), F32),
                   jax.ShapeDtypeStruct((SB_ROWS, SB_W), F32)],
        scratch_shapes=[
            pltpu.VMEM((1, D_MODEL), F32),
            pltpu.VMEM((4, CHUNK_ROWS, D_MODEL), BF16),
            half_chunk(2, BF16), half_chunk(2, BF16), half_chunk(2, F32), half_chunk(2, F32), half_chunk(2, BF16), half_chunk(2, BF16),
            pltpu.VMEM((CHUNK_ROWS, half), BF16), pltpu.VMEM((CHUNK_ROWS, half), BF16),
            pltpu.VMEM((CHUNK_ROWS, half), BF16), pltpu.VMEM((CHUNK_ROWS, half), BF16),
            pltpu.VMEM((SB_ROWS, SB_W), F32),
            pltpu.VMEM((N_DEV, SB_ROWS, SB_W), F32),
            pltpu.VMEM((4, CHUNK_ROWS, D_MODEL), BF16),
            pltpu.VMEM((N_DEV, 128, 512), BF16),
            pltpu.VMEM((N_DEV, 128, D_MODEL), BF16),
            pltpu.SemaphoreType.DMA((24,)),
            pltpu.SemaphoreType.DMA((24,)),
            pltpu.SemaphoreType.DMA((6,)),
        ],
        compiler_params=_params(("arbitrary",)),
    )(*pieces, d_f, x, d_out, *whole)


def _grad_w_in(h16, pieces, d_f):
    s_len = h16.shape[0]
    tk = 512

    def body(h_ref, p0, p1, p2, p3, p4, p5, df_ref, out_ref, dw_ref):
        @pl.when(pl.program_id(0) == 0)
        def _():
            dw_ref[...] = jnp.zeros((D_MODEL, MY_WIDTH), F32)

        h = h_ref[...]
        for j, p in enumerate((p0, p1, p2, p3, p4, p5)):
            dw_ref[:, PIECE * j:PIECE * (j + 1)] += _dot_tn(h, p[...])
        dw_ref[:, MY_F_OFF:MY_WIDTH] += _dot_tn(h, df_ref[...])

        @pl.when(pl.program_id(0) == pl.num_programs(0) - 1)
        def _():
            for d in range(N_DEV):
                parts = [dw_ref[:, start:start + width] for start, width in _chunk_segments(d)]
                parts.append(jnp.zeros((D_MODEL, 512 - IN_CHUNK), F32))
                out_ref[d] = jnp.concatenate(parts, axis=1).T[0:CHUNK_ROWS].astype(BF16)

    piece = pl.BlockSpec((tk, PIECE), lambda i: (i, 0))
    return pl.pallas_call(
        body,
        name="grad_w_in",
        grid=(s_len // tk,),
        in_specs=[pl.BlockSpec((tk, D_MODEL), lambda i: (i, 0))] + [piece] * 6 + [pl.BlockSpec((tk, 128), lambda i: (i, 0))],
        out_specs=_full((N_DEV, CHUNK_ROWS, D_MODEL)),
        out_shape=jax.ShapeDtypeStruct((N_DEV, CHUNK_ROWS, D_MODEL), BF16),
        scratch_shapes=[pltpu.VMEM((D_MODEL, MY_WIDTH), F32)],
        compiler_params=_params(("arbitrary",)),
    )(h16, *pieces, d_f)


def _adamw(w, g, m, v):
    m = ADAM_B1 * m + (1.0 - ADAM_B1) * g
    v = ADAM_B2 * v + (1.0 - ADAM_B2) * (g * g)
    m_hat = m / (1.0 - ADAM_B1 ** ADAM_STEP)
    v_hat = v / (1.0 - ADAM_B2 ** ADAM_STEP)
    return -ADAM_LR * (m_hat / (jnp.sqrt(v_hat) + ADAM_EPS) + ADAM_WD * w), m, v


PARAM_ORDER = ("norm_g", "w_in", "b_f", "w_pool", "pool_scale", "fox_q_g", "fox_k_g", "mem_norm_g", "w_mem_kv", "mem_q_g", "mem_k_g", "w_out")


def _update(red_in, red_kv, red_out, sred, params):
    def body(rin_ref, rkv_ref, rout_ref, sred_ref, *refs):
        ins, outs = refs[:3 * len(PARAM_ORDER)], refs[3 * len(PARAM_ORDER):]
        wide = lambda row: jnp.concatenate([sred_ref[row + k:row + k + 1, :] for k in range(4)], axis=1)
        head = lambda row: sred_ref[row:row + 1, 0:HEAD_DIM]
        grads = {
            "norm_g": lambda: wide(SB_NORM_G), "w_in": lambda: rin_ref[...], "b_f": lambda: sred_ref[SB_B_F:SB_B_F + 1, 0:FOX_HEADS],
            "pool_scale": lambda: sred_ref[SB_POOL_SCALE:SB_POOL_SCALE + 1, :], "fox_q_g": lambda: head(SB_FOX_Q),
            "fox_k_g": lambda: head(SB_FOX_K), "mem_norm_g": lambda: wide(SB_MEM_NORM_G), "w_mem_kv": lambda: rkv_ref[...],
            "mem_q_g": lambda: head(SB_MEM_Q), "mem_k_g": lambda: head(SB_MEM_K), "w_out": lambda: rout_ref[...],
        }
        for p, name in enumerate(PARAM_ORDER):
            w_ref, m_ref, v_ref = ins[3 * p:3 * p + 3]
            g_out, d_out, m_out, v_out = outs[4 * p:4 * p + 4]
            if name == "w_pool":
                for grp in range(4):
                    g = sred_ref[SB_W_POOL:SB_W_POOL + 64, 64 * grp:64 * (grp + 1)]
                    delta, m_new, v_new = _adamw(w_ref[grp], g, m_ref[grp], v_ref[grp])
                    g_out[grp], d_out[grp], m_out[grp], v_out[grp] = g, delta, m_new, v_new
            elif name == "w_in":
                for j in range(8):
                    rows = pl.ds(j, IN_CHUNK, stride=8)
                    g = rin_ref[0:IN_CHUNK, 128 * j:128 * (j + 1)]
                    delta, m_new, v_new = _adamw(w_ref[rows, :], g, m_ref[rows, :], v_ref[rows, :])
                    g_out[rows, :], d_out[rows, :], m_out[rows, :], v_out[rows, :] = g, delta, m_new, v_new
            else:
                g = grads[name]()
                delta, m_new, v_new = _adamw(w_ref[...], g, m_ref[...], v_ref[...])
                g_out[...], d_out[...], m_out[...], v_out[...] = g, delta, m_new, v_new

    flat = [t for name in PARAM_ORDER for t in params[name]]
    shapes = [params[name][0].shape for name in PARAM_ORDER for _ in range(4)]
    outs = pl.pallas_call(
        body,
        name="adamw_update",
        out_shape=tuple(jax.ShapeDtypeStruct(s, F32) for s in shapes),
        in_specs=[pl.BlockSpec(memory_space=pltpu.VMEM)] * (4 + len(flat)),
        out_specs=(pl.BlockSpec(memory_space=pltpu.VMEM),) * len(shapes),
        compiler_params=pltpu.CompilerParams(vmem_limit_bytes=VMEM_LIMIT),
    )(red_in, red_kv, red_out, sred, *flat)
    return {name: outs[4 * p:4 * p + 4] for p, name in enumerate(PARAM_ORDER)}


def kernel(x, mem, norm_g, w_in, b_f, w_pool, pool_scale, fox_q_g, fox_k_g, mem_norm_g, w_mem_kv, mem_q_g, mem_k_g, w_out, loss_target, m_norm_g, m_w_in, m_b_f, m_w_pool, m_pool_scale, m_fox_q_g, m_fox_k_g, m_mem_norm_g, m_w_mem_kv, m_mem_q_g, m_mem_k_g, m_w_out, v_norm_g, v_w_in, v_b_f, v_w_pool, v_pool_scale, v_fox_q_g, v_fox_k_g, v_mem_norm_g, v_w_mem_kv, v_mem_q_g, v_mem_k_g, v_w_out):
    given = dict(norm_g=(norm_g, m_norm_g, v_norm_g), w_in=(w_in, m_w_in, v_w_in), b_f=(b_f, m_b_f, v_b_f),
                 w_pool=(w_pool, m_w_pool, v_w_pool), pool_scale=(pool_scale, m_pool_scale, v_pool_scale),
                 fox_q_g=(fox_q_g, m_fox_q_g, v_fox_q_g), fox_k_g=(fox_k_g, m_fox_k_g, v_fox_k_g),
                 mem_norm_g=(mem_norm_g, m_mem_norm_g, v_mem_norm_g), w_mem_kv=(w_mem_kv, m_w_mem_kv, v_w_mem_kv),
                 mem_q_g=(mem_q_g, m_mem_q_g, v_mem_q_g), mem_k_g=(mem_k_g, m_mem_k_g, v_mem_k_g), w_out=(w_out, m_w_out, v_w_out))
    params = {name: tuple(t[0] if t.ndim > 2 else t for t in wmv) for name, wmv in given.items()}
    params["w_in"] = tuple(t.T.reshape(IN_CHUNK * 8, 128) for t in params["w_in"])
    x2, mem2, target2 = x[0], mem[0], loss_target[0]

    seg = jnp.asarray(np.kron(np.eye(FOX_HEADS), np.full((HEAD_DIM, HEAD_DIM), 1.0 / HEAD_DIM)), BF16)
    w_my, w_kv16, w_out16, wp_bd, bf_pad, gq8, gk8, gqm4 = _gather_w_in(
        params["w_in"][0], params["w_mem_kv"][0], params["w_out"][0], params["w_pool"][0], b_f, fox_q_g, fox_k_g, mem_q_g)
    proj_a, proj_b, h16, q_aug, k_aug, v_h, kt_aug, vt_aug = _proj_prep(x2, norm_g, w_my, bf_pad, gq8, gk8, seg)
    olse, lse_rows, w_kv_all, w_out_all = _fox_fwd(q_aug, k_aug, vt_aug, w_kv16, w_out16)
    km, vm = _mem_prep(mem2, mem_norm_g, w_kv_all, mem_k_g)
    d_out, d_mixed, dw_out, loss_blk = _mix_out_loss(proj_a, proj_b, olse, km, vm, wp_bd, pool_scale, gqm4, seg, x2, target2, w_out_all)

    d_ua, d_gb, d_qm, d_o, delta_rows, dwp_bd, d_scale, dkm, dvm, d_gqm = _mix_bwd(proj_a, proj_b, olse, d_mixed, km, vm, wp_bd,
                                                                                    pool_scale, gqm4, seg)
    dw_kv, d_mem_g, d_gkm = _mem_bwd(mem2, mem_norm_g, w_kv_all, mem_k_g, dkm, dvm)
    dq_aug, dk_aug, dv_h, land_kv, land_out = _fox_bwd(q_aug, k_aug, kt_aug, v_h, d_o, lse_rows, delta_rows, dw_kv, dw_out)
    d_q, d_k, d_v, d_f, d_gq, d_gk, d_bf = _fox_post(proj_a, proj_b, bf_pad, gq8, gk8, seg, dq_aug, dk_aug, dv_h)
    pieces = (d_ua, d_q, d_k, d_v, d_gb, d_qm)
    dw_in = _grad_w_in(h16, pieces, d_f)
    grad_x, red_in, red_kv, red_out, sred = _grad_x_exchange(pieces, d_f, w_my, x2, norm_g, d_out, dw_in, land_kv, land_out, d_mem_g,
                                                             d_scale, d_bf, d_gq, d_gk, d_gqm, d_gkm, dwp_bd, loss_blk)
    new = _update(red_in, red_kv, red_out, sred, params)
    result = [sred[SB_LOSS, 0], grad_x[None]]
    for kind in range(4):
        for name in PARAM_ORDER:
            out = new[name][kind]
            if name == "w_in":
                out = out.reshape(IN_CHUNK, D_MODEL).T
            result.append(out[None] if given[name][0].ndim > 2 else out)
    return tuple(result)
```

```python
import functools

import jax
import jax.numpy as jnp
import numpy as np
from jax import lax
from jax.experimental import pallas as pl
from jax.experimental.pallas import tpu as pltpu

F32 = jnp.float32
BF16 = jnp.bfloat16
MESH = pl.DeviceIdType.MESH

N_DEV = 8
D_MODEL = 1024
HEAD_DIM = 64
FOX_HEADS = 8
MEM_HEADS = 4
N_MEM = 256
POOL_WIDTH = 256
EPS = 1e-6
IN_WIDTH = 3080
IN_CHUNK = IN_WIDTH // N_DEV
CHUNK_ROWS = 400
F_OFF = 2048
MY_WIDTH = 3200
MY_F_OFF = 3072
PIECE = 512
GATES_OFF = 4 * PIECE
AUG_KEPT = 80
TM = 256
TMM = 512
TA = 256
HB = 8
HB_FWD = 8
AUG_ROWS = 72
HALO = 16
VMEM_LIMIT = 56 * 1024 * 1024

ADAM_LR = 0.001
ADAM_B1 = 0.9
ADAM_B2 = 0.999
ADAM_EPS = 1e-08
ADAM_WD = 0.01
ADAM_STEP = 10


def _dot(a, b):
    return jnp.dot(a, b, preferred_element_type=F32)


def _dot_nt(a, b):
    return lax.dot_general(a, b, (((1,), (1,)), ((), ())), preferred_element_type=F32)


def _dot_tn(a, b):
    return lax.dot_general(a, b, (((0,), (0,)), ((), ())), preferred_element_type=F32)


def _sigmoid(g):
    return 0.5 + 0.5 * jnp.tanh(0.5 * g)


def _split3(v):
    hi = v.astype(BF16)
    r1 = v - hi.astype(F32)
    mid = r1.astype(BF16)
    lo = (r1 - mid.astype(F32)).astype(BF16)
    return hi, mid, lo


def _rms(v):
    return lax.rsqrt(jnp.mean(v * v, axis=-1, keepdims=True) + EPS)


def _rms_bwd(a, r, g, dy):
    da = dy * g
    return r * (da - a * jnp.mean(da * a, axis=-1, keepdims=True)), dy * a


def _head_mean(z, seg):
    hi = z.astype(BF16)
    lo = (z - hi.astype(F32)).astype(BF16)
    if z.shape[1] == 256:
        return _dot(hi, seg) + _dot(lo, seg)
    half = seg[0:256, 0:256]
    return jnp.concatenate([_dot(hi[:, 0:256], half) + _dot(lo[:, 0:256], half),
                            _dot(hi[:, 256:512], half) + _dot(lo[:, 256:512], half)], axis=1)


def _head_rms(v, seg):
    return lax.rsqrt(_head_mean(v * v, seg) + EPS)


def _head_rms_bwd(a, r, g, dy, seg):
    da = dy * g
    return r * (da - a * _head_mean(da * a, seg)), dy * a


def _fold_heads(row, n_heads):
    out = row[:, 0:HEAD_DIM]
    for h in range(1, n_heads):
        out = out + row[:, HEAD_DIM * h:HEAD_DIM * (h + 1)]
    return out


def _params(sem, limit=VMEM_LIMIT):
    return pltpu.CompilerParams(dimension_semantics=sem, vmem_limit_bytes=limit)


def _full(shape):
    return pl.BlockSpec(shape, lambda *_: (0,) * len(shape))


def _chunk_segments(d):
    runs = []
    for lo, hi, shift in ((0, F_OFF, 0), (F_OFF, F_OFF + FOX_HEADS, MY_F_OFF - F_OFF), (F_OFF + FOX_HEADS, IN_WIDTH, -FOX_HEADS)):
        a, b = max(lo, IN_CHUNK * d), min(hi, IN_CHUNK * (d + 1))
        if a < b:
            runs.append((a + shift, b - a))
    return runs


def _my_place():
    x, y, c = lax.axis_index("x"), lax.axis_index("y"), lax.axis_index("c")
    return x, y, c, 4 * x + 2 * y + c


def _shard_rows(dev):
    return pl.ds(pl.multiple_of(128 * dev, 128), 128)


def _gather_w_in(w_in, w_kv, w_out, w_pool, b_f, gq, gk, gqm):
    def body(win_ref, wkv_ref, wout_ref, wp_ref, bf_ref, gq_ref, gk_ref, gqm_ref, wmy_ref, kv16_ref, out16_ref, wpbd_ref, bfpad_ref,
             gq8_ref, gk8_ref, gqm4_ref, in_all, pay_in, win_rows, send_sems, recv_sems, load_sem):
        x, y, c, me = _my_place()
        here, sibling = (x, y, c), (x, y, 1 - c)
        x_chip, y_chip, far_chip = (1 - x, y), (x, 1 - y), (1 - x, 1 - y)
        relay_from = (x ^ (1 - c), y ^ c)
        relay_to = (x ^ c, y ^ (1 - c))

        load = pltpu.make_async_copy(win_ref, win_rows, load_sem)
        load.start()
        load.wait()
        pay_in[...] = jnp.zeros((CHUNK_ROWS, D_MODEL), BF16)
        for j in range(8):
            pay_in[0:IN_CHUNK, 128 * j:128 * (j + 1)] = win_rows[pl.ds(j, IN_CHUNK, stride=8), :].astype(BF16)

        def copy(k, block, to, own=False):
            px, py, pc = block
            dst = in_all.at[4 * px + 2 * py + pc]
            return pltpu.make_async_remote_copy(src_ref=pay_in if own else dst, dst_ref=dst, send_sem=send_sems.at[k],
                                                recv_sem=recv_sems.at[k], device_id=to, device_id_type=MESH)

        first = [copy(0, here, sibling, own=True), copy(1, here, (*x_chip, c), own=True), copy(2, here, (*y_chip, c), own=True)]
        for cp in first:
            cp.start()
        in_all[me] = pay_in[...]
        kv16_ref[...] = wkv_ref[...].astype(BF16)
        out16_ref[...] = wout_ref[...].astype(BF16)
        wpbd_ref[...] = jnp.zeros((POOL_WIDTH, POOL_WIDTH), BF16)
        for grp in range(4):
            sl = slice(64 * grp, 64 * (grp + 1))
            wpbd_ref[sl, sl] = wp_ref[grp].astype(BF16)
        bfpad_ref[...] = jnp.zeros((1, 128), F32)
        bfpad_ref[:, 0:FOX_HEADS] = bf_ref[...]
        gq8_ref[...] = jnp.concatenate([gq_ref[...]] * FOX_HEADS, axis=1)
        gk8_ref[...] = jnp.concatenate([gk_ref[...]] * FOX_HEADS, axis=1)
        gqm4_ref[...] = jnp.concatenate([gqm_ref[...]] * MEM_HEADS, axis=1)
        copy(1 + c, (*relay_from, c), here).wait_recv()
        passed = [copy(3, (*relay_from, c), (*relay_to, c)), copy(4, (*relay_from, c), sibling)]
        for cp in passed:
            cp.start()
        copy(2 - c, (*relay_to, c), here).wait_recv()
        passed.append(copy(5, (*relay_to, c), sibling))
        passed[-1].start()
        copy(3, (*far_chip, c), here).wait_recv()
        passed.append(copy(6, (*far_chip, c), sibling))
        passed[-1].start()
        copy(0, sibling, here).wait_recv()
        for k, chip in ((4, relay_to), (5, relay_from), (6, far_chip)):
            copy(k, (*chip, 1 - c), here).wait_recv()
        for cp in first + passed:
            cp.wait_send()

        row_pad = jnp.zeros((512 - CHUNK_ROWS, 256), F32)
        for r0 in range(0, D_MODEL, 256):
            ch = [jnp.concatenate([in_all[d, :, r0:r0 + 256].astype(F32), row_pad], axis=0).T[:, 0:IN_CHUNK] for d in range(N_DEV)]
            lo = F_OFF - 5 * IN_CHUNK
            full = jnp.concatenate(ch[:5] + [ch[5][:, :lo], ch[5][:, lo + FOX_HEADS:], ch[6], ch[7], ch[5][:, lo:lo + FOX_HEADS],
                                             jnp.zeros((256, MY_WIDTH - IN_WIDTH), F32)], axis=1)
            wmy_ref[r0:r0 + 256, :] = full.astype(BF16)

    return pl.pallas_call(
        body,
        name="gather_w_in",
        out_shape=(jax.ShapeDtypeStruct((D_MODEL, MY_WIDTH), BF16), jax.ShapeDtypeStruct((128, 512), BF16),
                   jax.ShapeDtypeStruct((128, D_MODEL), BF16), jax.ShapeDtypeStruct((POOL_WIDTH, POOL_WIDTH), BF16),
                   jax.ShapeDtypeStruct((1, 128), F32), jax.ShapeDtypeStruct((1, PIECE), F32), jax.ShapeDtypeStruct((1, PIECE), F32),
                   jax.ShapeDtypeStruct((1, 256), F32)),
        in_specs=[pl.BlockSpec(memory_space=pl.ANY)] + [pl.BlockSpec(memory_space=pltpu.VMEM)] * 7,
        out_specs=(pl.BlockSpec(memory_space=pltpu.VMEM),) * 8,
        scratch_shapes=[
            pltpu.VMEM((N_DEV, CHUNK_ROWS, D_MODEL), BF16),
            pltpu.VMEM((CHUNK_ROWS, D_MODEL), BF16),
            pltpu.VMEM((IN_CHUNK * 8, 128), F32),
            pltpu.SemaphoreType.DMA((7,)),
            pltpu.SemaphoreType.DMA((7,)),
            pltpu.SemaphoreType.DMA,
        ],
        compiler_params=pltpu.CompilerParams(vmem_limit_bytes=VMEM_LIMIT),
    )(w_in, w_kv, w_out, w_pool, b_f, gq, gk, gqm)


def _row_block_exchange(kv_ref, out_ref, kv_dst, out_dst, send_sems, recv_sems, local_sems, gather):
    x, y, c, me = _my_place()
    remote = []
    for k in range(1, N_DEV):
        px, py, pc = x ^ (k >> 2), y ^ ((k >> 1) & 1), c ^ (k & 1)
        peer = 4 * px + 2 * py + pc
        for fam, (src, dst) in enumerate(((kv_ref, kv_dst), (out_ref, out_dst))):
            remote.append(pltpu.make_async_remote_copy(
                src_ref=src if gather else src.at[_shard_rows(peer)], dst_ref=dst.at[_shard_rows(me)] if gather else dst.at[me],
                send_sem=send_sems.at[7 * fam + k - 1], recv_sem=recv_sems.at[7 * fam + k - 1],
                device_id=(px, py, pc), device_id_type=MESH))
    local = [pltpu.make_async_copy(src if gather else src.at[_shard_rows(me)], dst.at[_shard_rows(me)] if gather else dst.at[me],
                                   local_sems.at[fam]) for fam, (src, dst) in enumerate(((kv_ref, kv_dst), (out_ref, out_dst)))]
    return remote, local


SB_ROWS, SB_W = 80, 256
SB_NORM_G, SB_MEM_NORM_G, SB_POOL_SCALE, SB_B_F, SB_FOX_Q, SB_FOX_K, SB_MEM_Q, SB_MEM_K, SB_LOSS, SB_W_POOL = 0, 4, 8, 9, 10, 11, 12, 13, 14, 16


def _alternate(*parts):
    state = [[part, 0, stages] for part, stages in parts]
    while state:
        least = min(state, key=lambda entry: entry[1] / entry[2])
        least[1] += 1
        if next(least[0], "done") == "done":
            state.remove(least)


def _log_sigmoid(z):
    return jnp.minimum(z, 0.0) - jnp.log(1.0 + jnp.exp(-jnp.abs(z)))


def _forget_lane_maps():
    to_q = np.zeros((128, FOX_HEADS * 128), np.float32)
    to_k = np.zeros((128, FOX_HEADS * 128), np.float32)
    for h in range(FOX_HEADS):
        for term in range(3):
            to_q[8 * term + h, 128 * h + 64 + term] = 1.0
            to_q[24, 128 * h + 67 + term] = 1.0
            to_k[24, 128 * h + 64 + term] = 1.0
            to_k[8 * term + h, 128 * h + 67 + term] = -1.0
    return jnp.asarray(to_q, BF16), jnp.asarray(to_k, BF16)


def _proj_prep(x, norm_g, w_my, bf_pad, gq, gk, seg):
    s_len = x.shape[0]
    n_blocks = s_len // TMM
    halves = TMM // TM
    to_q, to_k = _forget_lane_maps()
    q_off, f_off = PIECE, MY_F_OFF
    kept = 3 * PIECE + 128

    def body(x_ref, g_ref, w_ref, bf_ref, gq_ref, gk_ref, seg_ref, eq_ref, ek_ref,
             proj_a_ref, proj_b_ref, h_ref, qa_ref, ka_ref, vh_ref, kt_ref, vt_ref, kept_even, kept_odd, carry_ref):
        s = pl.program_id(0)

        def project(kept_ref):
            xv = x_ref[...]
            h_ref[...] = (xv * _rms(xv) * g_ref[...]).astype(BF16)
            yield
            for c0 in range(0, MY_WIDTH, 256):
                c1 = min(c0 + 256, MY_WIDTH)
                res = _dot(h_ref[...], w_ref[:, c0:c1])
                if c0 < 3 * PIECE:
                    proj_a_ref[:, c0:c1] = res
                if c0 >= GATES_OFF:
                    proj_b_ref[:, c0 - GATES_OFF:c1 - GATES_OFF] = res
                if q_off <= c0 < q_off + 3 * PIECE:
                    kept_ref[:, c0 - q_off:c1 - q_off] = res
                if c0 == f_off:
                    kept_ref[:, 3 * PIECE:kept] = res
                yield

        def operands(kept_ref):
            lane = lax.broadcasted_iota(jnp.int32, (TM, 128), 1)
            row = lax.broadcasted_iota(jnp.int32, (TM, TM), 0)
            col = lax.broadcasted_iota(jnp.int32, (TM, TM), 1)
            tri = jnp.where(col <= row, 1.0, 0.0).astype(BF16)
            one_at_64 = jnp.where(lane == 64, 1.0, 0.0).astype(BF16)
            zeros = jnp.zeros((TM, HEAD_DIM), BF16)
            for half in range(halves):
                rows = slice(TM * half, TM * (half + 1))
                logf = jnp.where(lane < FOX_HEADS, _log_sigmoid(kept_ref[rows, 3 * PIECE:kept] + bf_ref[...]), 0.0)
                hi, mid, lo = _split3(logf)
                cum = _dot(tri, hi) + _dot(tri, mid) + _dot(tri, lo) + carry_ref[...]
                q_all = kept_ref[rows, 0:PIECE]
                k_all = kept_ref[rows, PIECE:2 * PIECE]
                rq = _head_rms(q_all, seg_ref[...])
                rk = _head_rms(k_all, seg_ref[...])
                yield
                carry_ref[...] = cum[TM - 1:TM, :]
                f_hi, f_mid, f_lo = [t.astype(F32) for t in _split3(cum)]
                packed = (f_hi + pltpu.roll(f_mid, 8, 1) + pltpu.roll(f_lo, 16, 1)
                          + jnp.where(lane == 24, 1.0, 0.0)).astype(BF16)
                extra_q = _dot(packed, eq_ref[...]).astype(BF16)
                extra_k = _dot(packed, ek_ref[...]).astype(BF16)
                qn_all = (q_all * rq * gq_ref[...] * 0.125).astype(BF16)
                kn_all = (k_all * rk * gk_ref[...]).astype(BF16)
                v_all = kept_ref[rows, 2 * PIECE:3 * PIECE].astype(BF16)
                yield
                for h in range(FOX_HEADS):
                    sl = slice(HEAD_DIM * h, HEAD_DIM * (h + 1))
                    tile = slice(128 * h, 128 * (h + 1))
                    qa = jnp.where(lane < 64, jnp.concatenate([qn_all[:, sl], zeros], axis=1), extra_q[:, tile])
                    ka = jnp.where(lane < 64, jnp.concatenate([kn_all[:, sl], zeros], axis=1), extra_k[:, tile])
                    vh = v_all[:, sl]
                    v_aug = jnp.where(lane < 64, jnp.concatenate([vh, zeros], axis=1), one_at_64)
                    qa_ref[h, rows, :] = qa
                    ka_ref[h, rows, :] = ka
                    vh_ref[h, rows, :] = vh
                    kt_ref[h, half] = ka.T[0:AUG_KEPT]
                    vt_ref[h, half] = v_aug.T[0:AUG_KEPT]
                    if h % 2 == 1:
                        yield

        projecting = lambda kept_ref: (project(kept_ref), 1 + pl.cdiv(MY_WIDTH, 256))
        making = lambda kept_ref: (operands(kept_ref), halves * (2 + FOX_HEADS // 2))

        @pl.when(s == 0)
        def _():
            carry_ref[...] = jnp.zeros((1, 128), F32)
            _alternate(projecting(kept_even))

        @pl.when((s > 0) & (s < n_blocks) & (s % 2 == 1))
        def _():
            _alternate(projecting(kept_odd), making(kept_even))

        @pl.when((s > 0) & (s < n_blocks) & (s % 2 == 0))
        def _():
            _alternate(projecting(kept_even), making(kept_odd))

        @pl.when(s == n_blocks)
        def _():
            _alternate(making(kept_odd if n_blocks % 2 == 0 else kept_even))

    made = lambda s: jnp.minimum(s, n_blocks - 1)
    used = lambda s: jnp.maximum(s - 1, 0)
    head_blk = lambda w: pl.BlockSpec((FOX_HEADS, TMM, w), lambda s: (0, used(s), 0))
    tile_blk = pl.BlockSpec((FOX_HEADS, halves, AUG_KEPT, TM), lambda s: (0, used(s), 0, 0))
    tiled = jax.ShapeDtypeStruct((FOX_HEADS, s_len // TM, AUG_KEPT, TM), BF16)
    made_blk = lambda w: pl.BlockSpec((TMM, w), lambda s: (made(s), 0))
    return pl.pallas_call(
        body,
        name="proj_prep",
        grid=(n_blocks + 1,),
        in_specs=[pl.BlockSpec((TMM, D_MODEL), lambda s: (made(s), 0)), _full((1, D_MODEL)), _full((D_MODEL, MY_WIDTH)),
                  _full((1, 128)), _full((1, PIECE)), _full((1, PIECE)), _full((PIECE, PIECE)),
                  _full((128, FOX_HEADS * 128)), _full((128, FOX_HEADS * 128))],
        out_specs=[made_blk(3 * PIECE), made_blk(MY_WIDTH - GATES_OFF), made_blk(D_MODEL),
                   head_blk(128), head_blk(128), head_blk(HEAD_DIM), tile_blk, tile_blk],
        out_shape=[jax.ShapeDtypeStruct((s_len, 3 * PIECE), F32), jax.ShapeDtypeStruct((s_len, MY_WIDTH - GATES_OFF), F32),
                   jax.ShapeDtypeStruct((s_len, D_MODEL), BF16),
                   jax.ShapeDtypeStruct((FOX_HEADS, s_len, 128), BF16), jax.ShapeDtypeStruct((FOX_HEADS, s_len, 128), BF16),
                   jax.ShapeDtypeStruct((FOX_HEADS, s_len, HEAD_DIM), BF16), tiled, tiled],
        scratch_shapes=[pltpu.VMEM((TMM, kept), F32), pltpu.VMEM((TMM, kept), F32), pltpu.VMEM((1, 128), F32)],
        compiler_params=_params(("arbitrary",)),
    )(x, norm_g, w_my, bf_pad, gq, gk, seg, to_q, to_k)


def _mem_prep(mem, g, w_kv, gk):
    def body(mem_ref, g_ref, w_ref, gk_ref, km_ref, vm_ref):
        m = mem_ref[...]
        kv = _dot((m * _rms(m) * g_ref[...]).astype(BF16), w_ref[...])
        for h in range(MEM_HEADS):
            sl = slice(HEAD_DIM * h, HEAD_DIM * (h + 1))
            kh = kv[:, sl]
            km_ref[:, sl] = (kh * _rms(kh) * gk_ref[...]).astype(BF16)
        vm_ref[...] = kv[:, 256:512].astype(BF16)

    return pl.pallas_call(
        body,
        name="mem_prep",
        out_shape=(jax.ShapeDtypeStruct((N_MEM, 256), BF16),) * 2,
        in_specs=[pl.BlockSpec(memory_space=pltpu.VMEM)] * 4,
        out_specs=(pl.BlockSpec(memory_space=pltpu.VMEM),) * 2,
        compiler_params=pltpu.CompilerParams(vmem_limit_bytes=VMEM_LIMIT),
    )(mem, g, w_kv, gk)


def _causal_mask_t():
    row = lax.broadcasted_iota(jnp.int32, (TA, TA), 0)
    col = lax.broadcasted_iota(jnp.int32, (TA, TA), 1)
    return row <= col


def _fox_fwd(q_aug, k_aug, vt_aug, w_kv16, w_out16):
    s_len = q_aug.shape[1]
    n_tiles = s_len // TA
    hb = HB_FWD
    n_groups = FOX_HEADS // hb

    def body(q_ref, k_new, vt_new, kv16_ref, out16_ref, o_ref, st_ref, kv_all, out_all, k_ref, vt_ref, send_sems, recv_sems, local_sems):
        qi = pl.program_id(1)
        for h in range(hb):
            k_ref[h, pl.ds(pl.multiple_of(qi * TA, TA), TA), :] = k_new[h]
            vt_ref[h, qi] = vt_new[h, 0]
        remote, local = _row_block_exchange(kv16_ref, out16_ref, kv_all, out_all, send_sems, recv_sems, local_sems, gather=True)

        @pl.when((pl.program_id(0) == 0) & (qi == 0))
        def _():
            for cp in remote + local:
                cp.start()

        qs = [q_ref[h] for h in range(hb)]

        def step(t0, carry, masked, width):
            rows = pl.ds(pl.multiple_of(t0 * TA, TA), width * TA)
            scores = [_dot_nt(k_ref[h, rows, :], qs[h]) for h in range(hb)]
            soft = []
            for h in range(hb):
                m, _ = carry[h]
                s = jnp.where(_causal_mask_t(), scores[h], -1e30) if masked else scores[h]
                m_new = jnp.maximum(m, jnp.max(s, axis=0, keepdims=True))
                soft.append((m_new, jnp.exp(m - m_new), jnp.exp(s - m_new).astype(BF16)))
            out = []
            for h, (m_new, alpha, p) in enumerate(soft):
                acc = alpha * carry[h][1]
                for t in range(width):
                    acc = acc + _dot(vt_ref[h, t0 + t, 0:AUG_ROWS, :], p[t * TA:(t + 1) * TA])
                out.append((m_new, acc))
            return tuple(out)

        init = tuple((jnp.full((1, TA), -1e30, F32), jnp.zeros((AUG_ROWS, TA), F32)) for _ in range(hb))
        carry = lax.fori_loop(0, qi // 2, lambda j, cr: step(2 * j, cr, False, 2), init)
        carry = lax.fori_loop(2 * (qi // 2), qi, lambda j, cr: step(j, cr, False, 1), carry)
        carry = step(qi, carry, True, 1)
        for h in range(hb):
            m, acc = carry[h]
            l = acc[HEAD_DIM:HEAD_DIM + 1, :]
            lse = m + jnp.log(l)
            o_ref[h] = jnp.concatenate([acc[0:HEAD_DIM] / l, jnp.broadcast_to(lse, (HEAD_DIM, TA))], axis=0).T
            st_ref[h, 0] = jnp.broadcast_to(lse, (8, TA))

        @pl.when((pl.program_id(0) == n_groups - 1) & (qi == n_tiles - 1))
        def _():
            for cp in remote:
                cp.wait_recv()
            for cp in remote:
                cp.wait_send()
            for cp in local:
                cp.wait()

    hbm = pl.BlockSpec(memory_space=pl.ANY)
    return pl.pallas_call(
        body,
        name="fox_fwd",
        grid=(n_groups, n_tiles),
        in_specs=[pl.BlockSpec((hb, TA, 128), lambda g, i: (g, i, 0)), pl.BlockSpec((hb, TA, 128), lambda g, i: (g, i, 0)),
                  pl.BlockSpec((hb, 1, AUG_KEPT, TA), lambda g, i: (g, i, 0, 0)), hbm, hbm],
        out_specs=[pl.BlockSpec((hb, TA, 128), lambda g, i: (g, i, 0)), pl.BlockSpec((hb, 1, 8, TA), lambda g, i: (g, i, 0, 0)),
                   hbm, hbm],
        out_shape=[jax.ShapeDtypeStruct((FOX_HEADS, s_len, 128), F32), jax.ShapeDtypeStruct((FOX_HEADS, n_tiles, 8, TA), F32),
                   jax.ShapeDtypeStruct((D_MODEL, 512), BF16), jax.ShapeDtypeStruct((D_MODEL, D_MODEL), BF16)],
        scratch_shapes=[pltpu.VMEM((hb, s_len, 128), BF16), pltpu.VMEM((hb, n_tiles, AUG_KEPT, TA), BF16),
                        pltpu.SemaphoreType.DMA((14,)), pltpu.SemaphoreType.DMA((14,)), pltpu.SemaphoreType.DMA((2,))],
        compiler_params=_params(("arbitrary", "arbitrary")),
    )(q_aug, k_aug, vt_aug, w_kv16, w_out16)


def _lane_group(lane, a, b, c, d):
    return jnp.where(lane < 64, a, jnp.where(lane < 128, b, jnp.where(lane < 192, c, d)))


def _pool_count(row0):
    lane = lax.broadcasted_iota(jnp.int32, (TM, POOL_WIDTH), 1)
    t1 = row0 + lax.broadcasted_iota(jnp.int32, (TM, POOL_WIDTH), 0) + 1
    return 1.0 / jnp.minimum(t1, _lane_group(lane, 2, 4, 8, 16)).astype(F32), lane


def _pool_delta(u, halo, cnt, lane):
    xe = jnp.concatenate([halo, u], axis=0)
    s2 = xe + pltpu.roll(xe, 1, 0)
    s4 = s2 + pltpu.roll(s2, 2, 0)
    s8 = s4 + pltpu.roll(s4, 4, 0)
    s16 = s8 + pltpu.roll(s8, 8, 0)
    win = _lane_group(lane, s2[HALO:], s4[HALO:], s8[HALO:], s16[HALO:])
    return win * cnt - u


def _pool_delta_bwd(dd, dd_next, cnt, cnt_next, lane):
    ee = jnp.concatenate([dd * cnt, dd_next * cnt_next], axis=0)
    n = TM + HALO
    r2 = ee + pltpu.roll(ee, n - 1, 0)
    r4 = r2 + pltpu.roll(r2, n - 2, 0)
    r8 = r4 + pltpu.roll(r4, n - 4, 0)
    r16 = r8 + pltpu.roll(r8, n - 8, 0)
    return _lane_group(lane, r2[:TM], r4[:TM], r8[:TM], r16[:TM]) - dd


def _mem_attn(qm, gq, seg, km_ref, vm_ref):
    r = _head_rms(qm, seg)
    a = qm * r
    q16 = (a * gq * 0.125).astype(BF16)
    heads = []
    for h in range(MEM_HEADS):
        sl = slice(HEAD_DIM * h, HEAD_DIM * (h + 1))
        s = _dot_nt(q16[:, sl], km_ref[:, sl])
        e = jnp.exp(s - jnp.max(s, axis=-1, keepdims=True))
        p = e * (1.0 / jnp.sum(e, axis=-1, keepdims=True))
        heads.append((p, _dot(p.astype(BF16), vm_ref[:, sl])))
    return a, r, q16, heads


def _mem_attn_stages(qm, gq, seg, km_ref, vm_ref, result):
    r = _head_rms(qm, seg)
    a = qm * r
    q16 = (a * gq * 0.125).astype(BF16)
    yield
    head_slices = [slice(HEAD_DIM * h, HEAD_DIM * (h + 1)) for h in range(MEM_HEADS)]
    scores = [_dot_nt(q16[:, sl], km_ref[:, sl]) for sl in head_slices]
    yield
    heads = []
    for s, sl in zip(scores, head_slices):
        e = jnp.exp(s - jnp.max(s, axis=-1, keepdims=True))
        p = e * (1.0 / jnp.sum(e, axis=-1, keepdims=True))
        heads.append((p, _dot(p.astype(BF16), vm_ref[:, sl])))
    result.extend([a, r, q16, heads])
    yield


def _mix_out_loss(proj_a, proj_b, olse, km, vm, wp_bd, pool_scale, gq_m, seg, x, target, w_out):
    s_len = x.shape[0]
    n_blocks = s_len // TMM
    halves = TMM // TM

    def body(ua_ref, halo_ref, gb_ref, qm_ref, ol_ref, km_ref, vm_ref, wp_ref, sc_ref, gq_ref, seg_ref, x_ref, t_ref, w_ref,
             dout_ref, dmix_ref, dw16_ref, loss_ref, mixed_even, mixed_odd, d16_ref, dw_ref):
        s = pl.program_id(0)
        chunks = [slice(c0, c0 + 256) for c0 in range(0, D_MODEL, 256)]

        def matmuls(mixed_ref):
            for cols in chunks:
                err = x_ref[:, cols] + _dot(mixed_ref[...], w_ref[:, cols]) - t_ref[:, cols]
                loss_ref[...] += (0.5 / D_MODEL) * jnp.sum(err * err)
                d_out = err / float(D_MODEL)
                dout_ref[:, cols] = d_out
                d16_ref[:, cols] = d_out.astype(BF16)
                yield
            for cols in chunks:
                dmix_ref[:, cols] = _dot_nt(d16_ref[...], w_ref[cols, :])
                yield
            for cols in chunks:
                dw_ref[:, cols] += _dot_tn(mixed_ref[...], d16_ref[:, cols])
                yield

        def concatenation(mixed_ref):
            for half in range(halves):
                r0 = TM * half
                rows = slice(r0, r0 + TM)
                u = ua_ref[rows, 0:256]
                g_a = ua_ref[rows, 256:512]
                halo = jnp.where(s > 0, halo_ref[...], 0.0) if half == 0 else ua_ref[r0 - HALO:r0, 0:256]
                cnt, lane = _pool_count(s * TMM + r0)
                d = _pool_delta(u, halo, cnt, lane).astype(BF16)
                y_a = _dot(d, wp_ref[...]) * sc_ref[...]
                mixed_ref[rows, 0:256] = (y_a * (g_a * _sigmoid(g_a))).astype(BF16)
                yield
                g_b = gb_ref[rows, :]
                y_b = jnp.concatenate([ol_ref[h, rows, 0:HEAD_DIM] for h in range(FOX_HEADS)], axis=1)
                mixed_ref[rows, 256:768] = (y_b * (g_b * _sigmoid(g_b))).astype(BF16)
                yield
                attn = []
                stages = _mem_attn_stages(qm_ref[rows, 0:256], gq_ref[...], seg_ref[0:256, 0:256], km_ref, vm_ref, attn)
                next(stages)
                yield
                next(stages)
                yield
                next(stages)
                g_m = qm_ref[rows, 256:512]
                y_m = jnp.concatenate([y for _, y in attn[3]], axis=1)
                mixed_ref[rows, 768:1024] = (y_m * (g_m * _sigmoid(g_m))).astype(BF16)
                yield

        multiplying = lambda mixed_ref: (matmuls(mixed_ref), 3 * len(chunks))
        building = lambda mixed_ref: (concatenation(mixed_ref), 5 * halves)

        @pl.when(s == 0)
        def _():
            dw_ref[...] = jnp.zeros((D_MODEL, D_MODEL), F32)
            loss_ref[...] = jnp.zeros((8, 128), F32)
            _alternate(building(mixed_even))

        @pl.when((s > 0) & (s < n_blocks) & (s % 2 == 1))
        def _():
            _alternate(multiplying(mixed_even), building(mixed_odd))

        @pl.when((s > 0) & (s < n_blocks) & (s % 2 == 0))
        def _():
            _alternate(multiplying(mixed_odd), building(mixed_even))

        @pl.when(s == n_blocks)
        def _():
            _alternate(multiplying(mixed_odd if n_blocks % 2 == 0 else mixed_even))
            dw16_ref[...] = dw_ref[...].astype(BF16)

    build = lambda s: jnp.minimum(s, n_blocks - 1)
    use = lambda s: jnp.maximum(s - 1, 0)
    piece = lambda j: pl.BlockSpec((TMM, PIECE), lambda s: (build(s), j))
    halo_blk = pl.BlockSpec((HALO, 256), lambda s: (jnp.maximum(build(s) * (TMM // HALO) - 1, 0), 0))
    row = pl.BlockSpec((TMM, D_MODEL), lambda s: (use(s), 0))
    return pl.pallas_call(
        body,
        name="mix_out_loss",
        grid=(n_blocks + 1,),
        in_specs=[piece(0), halo_blk, piece(0), piece(1), pl.BlockSpec((FOX_HEADS, TMM, 128), lambda s: (0, build(s), 0)),
                  _full((N_MEM, 256)), _full((N_MEM, 256)), _full((256, 256)), _full((1, 256)), _full((1, 256)),
                  _full((PIECE, PIECE)), row, row, _full((D_MODEL, D_MODEL))],
        out_specs=[row, row, _full((D_MODEL, D_MODEL)), _full((8, 128))],
        out_shape=[jax.ShapeDtypeStruct((s_len, D_MODEL), F32), jax.ShapeDtypeStruct((s_len, D_MODEL), F32),
                   jax.ShapeDtypeStruct((D_MODEL, D_MODEL), BF16), jax.ShapeDtypeStruct((8, 128), F32)],
        scratch_shapes=[pltpu.VMEM((TMM, D_MODEL), BF16), pltpu.VMEM((TMM, D_MODEL), BF16), pltpu.VMEM((TMM, D_MODEL), BF16),
                        pltpu.VMEM((D_MODEL, D_MODEL), F32)],
        compiler_params=_params(("arbitrary",)),
    )(proj_a, proj_a, proj_b, proj_b, olse, km, vm, wp_bd, pool_scale, gq_m, seg, x, target, w_out)


def _silu_pair(g):
    s = _sigmoid(g)
    return g * s, s * (1.0 + g * (1.0 - s))


def _mix_bwd(proj_a, proj_b, olse, d_mixed, km, vm, wp_bd, pool_scale, gq_m, seg):
    s_len = proj_a.shape[0]
    n_tiles = s_len // TM

    def body(ua_ref, halo_ref, ga_next_ref, gb_ref, qm_ref, ol_ref, dm_ref, dm_next_ref, km_ref, vm_ref, wp_ref, sc_ref, gq_ref,
             seg_ref, dua_ref, dgb_ref, dqm_ref, do_ref, dl_ref, dwp_ref, dsc_ref, dkm_ref, dvm_ref, dgq_ref):
        i = pl.program_id(0)

        @pl.when(i == 0)
        def _():
            dwp_ref[...] = jnp.zeros((256, 256), F32)
            dsc_ref[...] = jnp.zeros((1, 256), F32)
            dkm_ref[...] = jnp.zeros((N_MEM, 256), F32)
            dvm_ref[...] = jnp.zeros((N_MEM, 256), F32)
            dgq_ref[...] = jnp.zeros((1, HEAD_DIM), F32)

        def pooling_mixer():
            u = ua_ref[:, 0:256]
            g_a = ua_ref[:, 256:512]
            halo = jnp.where(i > 0, halo_ref[...], 0.0)
            cnt, lane = _pool_count(i * TM)
            d16 = _pool_delta(u, halo, cnt, lane).astype(BF16)
            t = _dot(d16, wp_ref[...])
            yield
            y_a = t * sc_ref[...]
            silu_a, dsilu_a = _silu_pair(g_a)
            dm_a = dm_ref[:, 0:256]
            dy_a = dm_a * silu_a
            dsc_ref[...] += jnp.sum(dy_a * t, axis=0, keepdims=True)
            dt16 = (dy_a * sc_ref[...]).astype(BF16)
            dwp_ref[...] += _dot_tn(d16, dt16)
            dd = _dot_nt(dt16, wp_ref[...])
            dua_ref[:, 256:512] = (dm_a * y_a * dsilu_a).astype(BF16)
            yield
            g_next = ga_next_ref[...]
            dt_next = (dm_next_ref[...] * (g_next * _sigmoid(g_next)) * sc_ref[...]).astype(BF16)
            dd_next = jnp.where(i < n_tiles - 1, _dot_nt(dt_next, wp_ref[...]), 0.0)
            cnt_next = _pool_count((i + 1) * TM)[0][0:HALO]
            dua_ref[:, 0:256] = _pool_delta_bwd(dd, dd_next, cnt, cnt_next, lane).astype(BF16)
            yield

        def output_gate():
            silu_b, dsilu_b = _silu_pair(gb_ref[...])
            dm_b = dm_ref[:, 256:768]
            o_all = jnp.concatenate([ol_ref[h][:, 0:HEAD_DIM] for h in range(FOX_HEADS)], axis=1)
            d_o = dm_b * silu_b
            dgb_ref[...] = (dm_b * o_all * dsilu_b).astype(BF16)
            d_o16 = d_o.astype(BF16)
            for h in range(FOX_HEADS):
                do_ref[h] = d_o16[:, HEAD_DIM * h:HEAD_DIM * (h + 1)]
            yield
            prod = d_o * o_all
            hi = prod.astype(BF16)
            lo = (prod - hi.astype(F32)).astype(BF16)
            head_of_lane = lax.broadcasted_iota(jnp.int32, (FOX_HEADS, PIECE), 1) // HEAD_DIM
            pick = jnp.where(head_of_lane == lax.broadcasted_iota(jnp.int32, (FOX_HEADS, PIECE), 0), 1.0, 0.0).astype(BF16)
            delta = _dot_nt(pick, hi) + _dot_nt(pick, lo)
            for h in range(FOX_HEADS):
                dl_ref[h, 0] = jnp.broadcast_to(delta[h:h + 1, :], (8, TM))
            yield

        def memory_attention():
            seg = seg_ref[0:256, 0:256]
            a, r, q16, heads = _mem_attn(qm_ref[:, 0:256], gq_ref[...], seg, km_ref, vm_ref)
            yield
            silu_m, dsilu_m = _silu_pair(qm_ref[:, 256:512])
            dm_m = dm_ref[:, 768:1024]
            y_m = jnp.concatenate([y for _, y in heads], axis=1)
            dqm_ref[:, 256:512] = (dm_m * y_m * dsilu_m).astype(BF16)
            dy16_all = (dm_m * silu_m).astype(BF16)
            d_scores = []
            head_slices = [slice(HEAD_DIM * h, HEAD_DIM * (h + 1)) for h in range(MEM_HEADS)]
            dps = [_dot_nt(dy16_all[:, sl], vm_ref[:, sl]) for sl in head_slices]
            for h, sl in enumerate(head_slices):
                dvm_ref[:, sl] += _dot_tn(heads[h][0].astype(BF16), dy16_all[:, sl])
            yield
            for h, sl in enumerate(head_slices):
                p = heads[h][0]
                dp = dps[h]
                ds16 = (p * (dp - jnp.sum(dp * p, axis=-1, keepdims=True))).astype(BF16)
                dkm_ref[:, sl] += _dot_tn(ds16, q16[:, sl])
                d_scores.append(_dot(ds16, km_ref[:, sl]))
                yield
            dq, dg_rows = _head_rms_bwd(a, r, gq_ref[...], jnp.concatenate(d_scores, axis=1) * 0.125, seg)
            dqm_ref[:, 0:256] = dq.astype(BF16)
            dgq_ref[...] += _fold_heads(jnp.sum(dg_rows, axis=0, keepdims=True), MEM_HEADS)
            yield

        _alternate((pooling_mixer(), 3))
        _alternate((memory_attention(), 3 + MEM_HEADS), (output_gate(), 2))

    n_halo = s_len // HALO
    piece = lambda j: pl.BlockSpec((TM, PIECE), lambda i: (i, j))
    before = pl.BlockSpec((HALO, 256), lambda i: (jnp.maximum(i * (TM // HALO) - 1, 0), 0))
    after = lambda j: pl.BlockSpec((HALO, 256), lambda i: (jnp.minimum((i + 1) * (TM // HALO), n_halo - 1), j))
    row = pl.BlockSpec((TM, D_MODEL), lambda i: (i, 0))
    out_piece = pl.BlockSpec((TM, PIECE), lambda i: (i, 0))
    return pl.pallas_call(
        body,
        name="mix_bwd",
        grid=(n_tiles,),
        in_specs=[piece(0), before, after(1), piece(0), piece(1), pl.BlockSpec((FOX_HEADS, TM, 128), lambda i: (0, i, 0)),
                  row, after(0), _full((N_MEM, 256)), _full((N_MEM, 256)), _full((256, 256)), _full((1, 256)), _full((1, 256)),
                  _full((PIECE, PIECE))],
        out_specs=[out_piece, out_piece, out_piece, pl.BlockSpec((FOX_HEADS, TM, HEAD_DIM), lambda i: (0, i, 0)),
                   pl.BlockSpec((FOX_HEADS, 1, 8, TM), lambda i: (0, i, 0, 0)),
                   _full((256, 256)), _full((1, 256)), _full((N_MEM, 256)), _full((N_MEM, 256)), _full((1, HEAD_DIM))],
        out_shape=[jax.ShapeDtypeStruct((s_len, PIECE), BF16)] * 3 + [
            jax.ShapeDtypeStruct((FOX_HEADS, s_len, HEAD_DIM), BF16),
            jax.ShapeDtypeStruct((FOX_HEADS, n_tiles, 8, TM), F32),
            jax.ShapeDtypeStruct((256, 256), F32), jax.ShapeDtypeStruct((1, 256), F32),
            jax.ShapeDtypeStruct((N_MEM, 256), F32), jax.ShapeDtypeStruct((N_MEM, 256), F32),
            jax.ShapeDtypeStruct((1, HEAD_DIM), F32)],
        compiler_params=_params(("arbitrary",)),
    )(proj_a, proj_a, proj_a, proj_b, proj_b, olse, d_mixed, d_mixed, km, vm, wp_bd, pool_scale, gq_m, seg)


def _fox_bwd(q_aug, k_aug, kt_aug, v_h, d_o, lse_rows, delta_rows, dw_kv16, dw_out16):
    s_len = q_aug.shape[1]
    n_tiles = s_len // TA
    n_groups = FOX_HEADS // HB

    def body(q_hbm, k_ref, kt_ref, v_ref, do_hbm, st_ref, dl_ref, dkv16_ref, dout16_ref, dq_ref, dk_ref, dv_ref, land_kv, land_out,
             dqt_ref, q_ref, do_ref, send_sems, recv_sems, local_sems, tile_sems):
        j = pl.program_id(1)
        remote, local = _row_block_exchange(dkv16_ref, dout16_ref, land_kv, land_out, send_sems, recv_sems, local_sems, gather=False)

        def tile_loads(i):
            rows = pl.ds(pl.multiple_of(i * TA, TA), TA)
            return [pltpu.make_async_copy(q_hbm.at[:, rows, :], q_ref.at[:, rows, :], tile_sems.at[2 * i]),
                    pltpu.make_async_copy(do_hbm.at[:, rows, :], do_ref.at[:, rows, :], tile_sems.at[2 * i + 1])]

        @pl.when((pl.program_id(0) == 0) & (j == 0))
        def _():
            for i in range(n_tiles):
                for cp in tile_loads(i):
                    cp.start()
            for cp in remote + local:
                cp.start()

        @pl.when(j == 0)
        def _():
            dqt_ref[...] = jnp.zeros((HB, n_tiles, AUG_ROWS, TA), F32)

        ks = [k_ref[h] for h in range(HB)]
        kts = [kt_ref[h, 0, 0:AUG_ROWS, :] for h in range(HB)]
        vs = [v_ref[h] for h in range(HB)]

        def pair(i0, acc, masked, width):
            @pl.when(j == 0)
            def _():
                for t in range(width):
                    for cp in tile_loads(i0 + t):
                        cp.wait()

            rows = pl.ds(pl.multiple_of(i0 * TA, TA), width * TA)
            qs = [q_ref[h, rows, :] for h in range(HB)]
            d_os = [do_ref[h, rows, :] for h in range(HB)]
            row_stat = lambda ref, h: jnp.concatenate([ref[h, i0 + t, 0:1, :] for t in range(width)], axis=1)
            scores = [(_dot_nt(ks[h], qs[h]), _dot_nt(vs[h], d_os[h])) for h in range(HB)]
            probs = []
            for h in range(HB):
                s, dp = scores[h]
                if masked:
                    s = jnp.where(_causal_mask_t(), s, -1e30)
                p = jnp.exp(s - row_stat(st_ref, h))
                probs.append((p.astype(BF16), (p * (dp - row_stat(dl_ref, h))).astype(BF16)))
            out = []
            for h in range(HB):
                p16, ds16 = probs[h]
                dqt = _dot(kts[h], ds16)
                for t in range(width):
                    dqt_ref[h, i0 + t] += dqt[:, t * TA:(t + 1) * TA]
                out.append((acc[h][0] + _dot(ds16, qs[h]), acc[h][1] + _dot(p16, d_os[h])))
            return tuple(out)

        zero = tuple((jnp.zeros((TA, 128), F32), jnp.zeros((TA, HEAD_DIM), F32)) for _ in range(HB))
        acc = pair(j, zero, True, 1)
        odd = (n_tiles - 1 - j) % 2
        acc = lax.fori_loop(j + 1, j + 1 + odd, lambda i, a: pair(i, a, False, 1), acc)
        acc = lax.fori_loop(0, (n_tiles - 1 - j) // 2, lambda t, a: pair(j + 1 + odd + 2 * t, a, False, 2), acc)
        pad = jnp.zeros((128 - AUG_ROWS, TA), F32)
        for h in range(HB):
            dk_ref[h] = acc[h][0]
            dv_ref[h] = acc[h][1].astype(BF16)
            dq_ref[h] = jnp.concatenate([dqt_ref[h, j], pad], axis=0).T

        @pl.when((pl.program_id(0) == n_groups - 1) & (j == n_tiles - 1))
        def _():
            for cp in remote:
                cp.wait_recv()
            for cp in remote:
                cp.wait_send()
            for cp in local:
                cp.wait()

    assert n_groups == 1
    resident = pl.BlockSpec(memory_space=pltpu.VMEM)
    rows_blk = lambda r: resident
    tile = lambda w: pl.BlockSpec((HB, TA, w), lambda g, j: (g, j, 0))
    hbm = pl.BlockSpec(memory_space=pl.ANY)
    return pl.pallas_call(
        body,
        name="fox_bwd",
        grid=(n_groups, n_tiles),
        in_specs=[hbm, tile(128), pl.BlockSpec((HB, 1, AUG_KEPT, TA), lambda g, j: (g, j, 0, 0)), tile(HEAD_DIM),
                  hbm, rows_blk(8), rows_blk(8), hbm, hbm],
        out_specs=[tile(128), tile(128), tile(HEAD_DIM), hbm, hbm],
        out_shape=[jax.ShapeDtypeStruct((FOX_HEADS, s_len, 128), F32), jax.ShapeDtypeStruct((FOX_HEADS, s_len, 128), F32),
                   jax.ShapeDtypeStruct((FOX_HEADS, s_len, HEAD_DIM), BF16),
                   jax.ShapeDtypeStruct((N_DEV, 128, 512), BF16), jax.ShapeDtypeStruct((N_DEV, 128, D_MODEL), BF16)],
        scratch_shapes=[pltpu.VMEM((HB, n_tiles, AUG_ROWS, TA), F32),
                        pltpu.VMEM((HB, s_len, 128), BF16), pltpu.VMEM((HB, s_len, HEAD_DIM), BF16),
                        pltpu.SemaphoreType.DMA((14,)), pltpu.SemaphoreType.DMA((14,)), pltpu.SemaphoreType.DMA((2,)),
                        pltpu.SemaphoreType.DMA((2 * n_tiles,))],
        compiler_params=_params(("arbitrary", "arbitrary")),
    )(q_aug, k_aug, kt_aug, v_h, d_o, lse_rows, delta_rows, dw_kv16, dw_out16)


def _fox_post(proj_a, proj_b, bf_pad, gq, gk, seg, dq_aug, dk_aug, dv_h):
    s_len = proj_a.shape[0]
    n_tiles = s_len // TM

    def body(q_ref, k_ref, f_ref, bf_ref, gq_ref, gk_ref, seg_ref, dqa_ref, dka_ref, dvh_ref,
             dq_ref, dk_ref, dv_ref, df_ref, dgq_ref, dgk_ref, dbf_ref, carry_ref):
        @pl.when(pl.program_id(0) == 0)
        def _():
            carry_ref[...] = jnp.zeros((1, 128), F32)
            dgq_ref[...] = jnp.zeros((1, HEAD_DIM), F32)
            dgk_ref[...] = jnp.zeros((1, HEAD_DIM), F32)
            dbf_ref[...] = jnp.zeros((1, 128), F32)

        lane = lax.broadcasted_iota(jnp.int32, (TM, 128), 1)
        seg = seg_ref[...]
        dqas = [dqa_ref[h] for h in range(FOX_HEADS)]
        dkas = [dka_ref[h] for h in range(FOX_HEADS)]
        def head_norms():
            q_all = q_ref[...]
            k_all = k_ref[...]
            rq = _head_rms(q_all, seg)
            rk = _head_rms(k_all, seg)
            yield
            dy_q = jnp.concatenate([t[:, 0:HEAD_DIM] for t in dqas], axis=1) * 0.125
            dq, dgq_rows = _head_rms_bwd(q_all * rq, rq, gq_ref[...], dy_q, seg)
            dq_ref[...] = dq.astype(BF16)
            dgq_ref[...] += _fold_heads(jnp.sum(dgq_rows, axis=0, keepdims=True), FOX_HEADS)
            yield
            dy_k = jnp.concatenate([t[:, 0:HEAD_DIM] for t in dkas], axis=1)
            dk, dgk_rows = _head_rms_bwd(k_all * rk, rk, gk_ref[...], dy_k, seg)
            dk_ref[...] = dk.astype(BF16)
            dgk_ref[...] += _fold_heads(jnp.sum(dgk_rows, axis=0, keepdims=True), FOX_HEADS)
            dv_ref[...] = jnp.concatenate([dvh_ref[h] for h in range(FOX_HEADS)], axis=1)
            yield

        def forget_gates():
            d_cum = jnp.zeros((TM, 128), F32)
            for h in range(FOX_HEADS):
                d_cum = jnp.where(lane == h, dqas[h][:, 64:65] - dkas[h][:, 67:68], d_cum)
            row = lax.broadcasted_iota(jnp.int32, (TM, TM), 0)
            col = lax.broadcasted_iota(jnp.int32, (TM, TM), 1)
            tri = jnp.where(col >= row, 1.0, 0.0).astype(BF16)
            hi, mid, lo = _split3(d_cum)
            d_logf = _dot(tri, hi) + _dot(tri, mid) + _dot(tri, lo) + carry_ref[...]
            yield
            carry_ref[...] = d_logf[0:1, :]
            z = f_ref[...] + bf_ref[...]
            d_f = jnp.where(lane < FOX_HEADS, d_logf / (1.0 + jnp.exp(z)), 0.0)
            df_ref[...] = d_f.astype(BF16)
            dbf_ref[...] += jnp.sum(d_f, axis=0, keepdims=True)
            yield

        _alternate((head_norms(), 3), (forget_gates(), 2))

    rev = lambda i: n_tiles - 1 - i
    col_blk = lambda j: pl.BlockSpec((TM, PIECE), lambda i: (rev(i), j))
    head_blk = lambda w: pl.BlockSpec((FOX_HEADS, TM, w), lambda i: (0, rev(i), 0))
    out_piece = pl.BlockSpec((TM, PIECE), lambda i: (rev(i), 0))
    return pl.pallas_call(
        body,
        name="fox_post",
        grid=(n_tiles,),
        in_specs=[col_blk(1), col_blk(2), pl.BlockSpec((TM, 128), lambda i: (rev(i), (MY_F_OFF - GATES_OFF) // 128)),
                  _full((1, 128)), _full((1, PIECE)), _full((1, PIECE)), _full((PIECE, PIECE)),
                  head_blk(128), head_blk(128), head_blk(HEAD_DIM)],
        out_specs=[out_piece, out_piece, out_piece, pl.BlockSpec((TM, 128), lambda i: (rev(i), 0)),
                   _full((1, HEAD_DIM)), _full((1, HEAD_DIM)), _full((1, 128))],
        out_shape=[jax.ShapeDtypeStruct((s_len, PIECE), BF16)] * 3 + [
            jax.ShapeDtypeStruct((s_len, 128), BF16), jax.ShapeDtypeStruct((1, HEAD_DIM), F32),
            jax.ShapeDtypeStruct((1, HEAD_DIM), F32), jax.ShapeDtypeStruct((1, 128), F32)],
        scratch_shapes=[pltpu.VMEM((1, 128), F32)],
        compiler_params=_params(("arbitrary",)),
    )(proj_a, proj_a, proj_b, bf_pad, gq, gk, seg, dq_aug, dk_aug, dv_h)


def _mem_bwd(mem, g, w_kv, gk, dkm, dvm):
    def body(mem_ref, g_ref, w_ref, gk_ref, dkm_ref, dvm_ref, dw_ref, dg_ref, dgk_ref, dkv_ref):
        m = mem_ref[...]
        r = _rms(m)
        a = m * r
        mn16 = (a * g_ref[...]).astype(BF16)
        kv = _dot(mn16, w_ref[...])
        dgk = jnp.zeros((N_MEM, HEAD_DIM), F32)
        for h in range(MEM_HEADS):
            sl = slice(HEAD_DIM * h, HEAD_DIM * (h + 1))
            kh = kv[:, sl]
            rk = _rms(kh)
            dk, dg_rows = _rms_bwd(kh * rk, rk, gk_ref[...], dkm_ref[:, sl])
            dkv_ref[:, sl] = dk.astype(BF16)
            dgk = dgk + dg_rows
        dkv_ref[:, 256:512] = dvm_ref[...].astype(BF16)
        dgk_ref[...] = jnp.sum(dgk, axis=0, keepdims=True)
        dkv16 = dkv_ref[...]
        dw_ref[...] = _dot_tn(mn16, dkv16).astype(BF16)
        dg_ref[...] = jnp.sum(_dot_nt(dkv16, w_ref[...]) * a, axis=0, keepdims=True)

    return pl.pallas_call(
        body,
        name="mem_bwd",
        out_shape=(jax.ShapeDtypeStruct((D_MODEL, 512), BF16), jax.ShapeDtypeStruct((1, D_MODEL), F32),
                   jax.ShapeDtypeStruct((1, HEAD_DIM), F32)),
        in_specs=[pl.BlockSpec(memory_space=pltpu.VMEM)] * 6,
        out_specs=(pl.BlockSpec(memory_space=pltpu.VMEM),) * 3,
        scratch_shapes=[pltpu.VMEM((N_MEM, 512), BF16)],
        compiler_params=pltpu.CompilerParams(vmem_limit_bytes=VMEM_LIMIT),
    )(mem, g, w_kv, gk, dkm, dvm)


def _grad_x_exchange(pieces, d_f, w_my, x, norm_g, d_out, dw_in, land_kv, land_out, d_mem_g, d_scale, d_bf, d_gq, d_gk, d_gqm,
                     d_gkm, dwp_bd, loss_blk):
    s_len = x.shape[0]
    n_steps = s_len // TM
    step2, step3 = n_steps // 4, (11 * n_steps) // 16
    half = D_MODEL // 2

    def body(p0, p1, p2, p3, p4, p5, df_ref, x_ref, dout_ref, w_ref, g_ref, in_ref, lkv_ref, lout_ref,
             dmg, dsc, dbf, dgq, dgk, dgqm, dgkm, dwp, loss_ref,
             gx_ref, rin_ref, rkv_ref, rout_ref, sred_ref,
             dng, land1, send2a, send2b, keep2a, keep2b, land2a, land2b, send3a, send3b, land3a, land3b, sb_ref, sland,
             own4, lkv_v, lout_v, send_sems, recv_sems, load_sems):
        i = pl.program_id(0)
        x_, y_, c_, me = _my_place()
        sibling, x_nbr, y_nbr = (x_, y_, 1 - c_), (1 - x_, y_, c_), (x_, 1 - y_, c_)
        loads = [pltpu.make_async_copy(in_ref.at[2 * k + c_], own4.at[k], load_sems.at[k]) for k in range(4)]
        loads += [pltpu.make_async_copy(lkv_ref, lkv_v, load_sems.at[4]), pltpu.make_async_copy(lout_ref, lout_v, load_sems.at[5])]

        def rcopy(src, dst, sem, to):
            return pltpu.make_async_remote_copy(src_ref=src, dst_ref=dst, send_sem=send_sems.at[sem], recv_sem=recv_sems.at[sem],
                                                device_id=to, device_id_type=MESH)

        early_rows, late_rows = pl.ds(8, SB_ROWS - 8), pl.ds(0, 8)
        small_early, small_late = [], []
        for k in range(1, N_DEV):
            peer = (x_ ^ (k >> 2), y_ ^ ((k >> 1) & 1), c_ ^ (k & 1))
            small_early.append(rcopy(sb_ref.at[early_rows], sland.at[me, early_rows], k - 1, peer))
            small_late.append(rcopy(sb_ref.at[late_rows], sland.at[me, late_rows], 16 + k, peer))
        small = small_early + small_late
        stage1 = [rcopy(in_ref.at[2 * k + 1 - c_], land1.at[k], 7 + k, sibling) for k in range(4)]
        stage2 = [rcopy(send2a.at[j], land2a.at[j], 11 + j, x_nbr) for j in range(2)]
        stage2 += [rcopy(send2b.at[j], land2b.at[j], 13 + j, y_nbr) for j in range(2)]
        stage3 = [rcopy(send3a, land3a, 15, y_nbr), rcopy(send3b, land3b, 16, x_nbr)]
        top, bot = slice(0, half), slice(half, D_MODEL)

        @pl.when(i == 0)
        def _():
            dng[...] = jnp.zeros((1, D_MODEL), F32)
            for cp in stage1 + loads:
                cp.start()
            sb_ref[...] = jnp.zeros((SB_ROWS, SB_W), F32)
            sb_ref[SB_POOL_SCALE:SB_POOL_SCALE + 1, :] = dsc[...]
            sb_ref[SB_B_F:SB_B_F + 1, 0:128] = dbf[...]
            for row, ref in ((SB_FOX_Q, dgq), (SB_FOX_K, dgk), (SB_MEM_Q, dgqm), (SB_MEM_K, dgkm)):
                sb_ref[row:row + 1, 0:HEAD_DIM] = ref[...]
            sb_ref[SB_LOSS:SB_LOSS + 1, 0:128] = loss_ref[0:1, :]
            for grp in range(4):
                sl = slice(64 * grp, 64 * (grp + 1))
                sb_ref[SB_W_POOL:SB_W_POOL + 64, sl] = dwp[sl, sl]
            for cp in small_early:
                cp.start()

        @pl.when(i == step2)
        def _():
            for cp in stage1:
                cp.wait_recv()
            for cp in loads[0:4]:
                cp.wait()

            def chip_sum(k, cols):
                return own4[k, :, cols].astype(F32) + land1[k, :, cols].astype(F32)

            for j in range(2):
                send2a[j] = chip_sum(2 * (1 - x_) + j, top).astype(BF16)
                keep2a[j] = chip_sum(2 * x_ + j, top)
                send2b[j] = chip_sum(2 * j + 1 - y_, bot).astype(BF16)
                keep2b[j] = chip_sum(2 * j + y_, bot)
            for cp in stage2:
                cp.start()

        @pl.when(i == step3)
        def _():
            for cp in stage2:
                cp.wait_recv()
            for j in range(2):
                keep2a[j] = keep2a[j] + land2a[j].astype(F32)
                keep2b[j] = keep2b[j] + land2b[j].astype(F32)
            send3a[...] = keep2a[1 - y_].astype(BF16)
            send3b[...] = keep2b[1 - x_].astype(BF16)
            for cp in stage3:
                cp.start()

        dh = _dot_nt(df_ref[...], w_ref[:, MY_F_OFF:MY_WIDTH])
        for j, p in enumerate((p0, p1, p2, p3, p4, p5)):
            dh = dh + _dot_nt(p[...], w_ref[:, PIECE * j:PIECE * (j + 1)])
        xv = x_ref[...]
        r = _rms(xv)
        dx, dg_rows = _rms_bwd(xv * r, r, g_ref[...], dh)
        gx_ref[...] = dout_ref[...] + dx
        dng[...] += jnp.sum(dg_rows, axis=0, keepdims=True)

        @pl.when(i == n_steps - 1)
        def _():
            for k in range(4):
                sb_ref[SB_NORM_G + k:SB_NORM_G + k + 1, :] = dng[:, SB_W * k:SB_W * (k + 1)]
                sb_ref[SB_MEM_NORM_G + k:SB_MEM_NORM_G + k + 1, :] = dmg[:, SB_W * k:SB_W * (k + 1)]
            for cp in small_late:
                cp.start()
            sland[me] = sb_ref[...]
            for cp in stage3:
                cp.wait_recv()
            rin_ref[:, top] = keep2a[y_] + land3a[...].astype(F32)
            rin_ref[:, bot] = keep2b[x_] + land3b[...].astype(F32)
            for cp in small:
                cp.wait_recv()
            for cp in small + stage1 + stage2 + stage3:
                cp.wait_send()
            for cp in loads[4:6]:
                cp.wait()
            for land, red in ((lkv_v, rkv_ref), (lout_v, rout_ref), (sland, sred_ref)):
                acc = land[0].astype(F32)
                for d in range(1, N_DEV):
                    acc = acc + land[d].astype(F32)
                red[...] = acc

    piece = pl.BlockSpec((TM, PIECE), lambda i: (i, 0))
    row = pl.BlockSpec((TM, D_MODEL), lambda i: (i, 0))
    vmem = pl.BlockSpec(memory_space=pltpu.VMEM)
    half_chunk = lambda n, dt: pltpu.VMEM((n, CHUNK_ROWS, half), dt)
    whole = (w_my, norm_g, dw_in, land_kv, land_out, d_mem_g, d_scale, d_bf, d_gq, d_gk, d_gqm, d_gkm, dwp_bd, loss_blk)
    return pl.pallas_call(
        body,
        name="grad_x_exchange",
        grid=(n_steps,),
        in_specs=[piece] * 6 + [pl.BlockSpec((TM, 128), lambda i: (i, 0)), row, row, vmem, vmem]
        + [pl.BlockSpec(memory_space=pl.ANY)] * 3 + [vmem] * (len(whole) - 5),
        out_specs=[row, vmem, vmem, vmem, vmem],
        out_shape=[jax.ShapeDtypeStruct((s_len, D_MODEL), F32), jax.ShapeDtypeStruct((CHUNK_ROWS, D_MODEL), F32),
                   jax.ShapeDtypeStruct((128, 512), F32), jax.ShapeDtypeStruct((128, D_MODEL), F32),
                   jax.ShapeDtypeStruct((SB_ROWS, SB_W), F32)],
        scratch_shapes=[
            pltpu.VMEM((1, D_MODEL), F32),
            pltpu.VMEM((4, CHUNK_ROWS, D_MODEL), BF16),
            half_chunk(2, BF16), half_chunk(2, BF16), half_chunk(2, F32), half_chunk(2, F32), half_chunk(2, BF16), half_chunk(2, BF16),
            pltpu.VMEM((CHUNK_ROWS, half), BF16), pltpu.VMEM((CHUNK_ROWS, half), BF16),
            pltpu.VMEM((CHUNK_ROWS, half), BF16), pltpu.VMEM((CHUNK_ROWS, half), BF16),
            pltpu.VMEM((SB_ROWS, SB_W), F32),
            pltpu.VMEM((N_DEV, SB_ROWS, SB_W), F32),
            pltpu.VMEM((4, CHUNK_ROWS, D_MODEL), BF16),
            pltpu.VMEM((N_DEV, 128, 512), BF16),
            pltpu.VMEM((N_DEV, 128, D_MODEL), BF16),
            pltpu.SemaphoreType.DMA((24,)),
            pltpu.SemaphoreType.DMA((24,)),
            pltpu.SemaphoreType.DMA((6,)),
        ],
        compiler_params=_params(("arbitrary",)),
    )(*pieces, d_f, x, d_out, *whole)


def _grad_w_in(h16, pieces, d_f):
    s_len = h16.shape[0]
    tk = 512

    def body(h_ref, p0, p1, p2, p3, p4, p5, df_ref, out_ref, dw_ref):
        @pl.when(pl.program_id(0) == 0)
        def _():
            dw_ref[...] = jnp.zeros((D_MODEL, MY_WIDTH), F32)

        h = h_ref[...]
        for j, p in enumerate((p0, p1, p2, p3, p4, p5)):
            dw_ref[:, PIECE * j:PIECE * (j + 1)] += _dot_tn(h, p[...])
        dw_ref[:, MY_F_OFF:MY_WIDTH] += _dot_tn(h, df_ref[...])

        @pl.when(pl.program_id(0) == pl.num_programs(0) - 1)
        def _():
            for d in range(N_DEV):
                parts = [dw_ref[:, start:start + width] for start, width in _chunk_segments(d)]
                parts.append(jnp.zeros((D_MODEL, 512 - IN_CHUNK), F32))
                out_ref[d] = jnp.concatenate(parts, axis=1).T[0:CHUNK_ROWS].astype(BF16)

    piece = pl.BlockSpec((tk, PIECE), lambda i: (i, 0))
    return pl.pallas_call(
        body,
        name="grad_w_in",
        grid=(s_len // tk,),
        in_specs=[pl.BlockSpec((tk, D_MODEL), lambda i: (i, 0))] + [piece] * 6 + [pl.BlockSpec((tk, 128), lambda i: (i, 0))],
        out_specs=_full((N_DEV, CHUNK_ROWS, D_MODEL)),
        out_shape=jax.ShapeDtypeStruct((N_DEV, CHUNK_ROWS, D_MODEL), BF16),
        scratch_shapes=[pltpu.VMEM((D_MODEL, MY_WIDTH), F32)],
        compiler_params=_params(("arbitrary",)),
    )(h16, *pieces, d_f)


def _adamw(w, g, m, v):
    m = ADAM_B1 * m + (1.0 - ADAM_B1) * g
    v = ADAM_B2 * v + (1.0 - ADAM_B2) * (g * g)
    m_hat = m / (1.0 - ADAM_B1 ** ADAM_STEP)
    v_hat = v / (1.0 - ADAM_B2 ** ADAM_STEP)
    return -ADAM_LR * (m_hat / (jnp.sqrt(v_hat) + ADAM_EPS) + ADAM_WD * w), m, v


PARAM_ORDER = ("norm_g", "w_in", "b_f", "w_pool", "pool_scale", "fox_q_g", "fox_k_g", "mem_norm_g", "w_mem_kv", "mem_q_g", "mem_k_g", "w_out")


def _update(red_in, red_kv, red_out, sred, params):
    def body(rin_ref, rkv_ref, rout_ref, sred_ref, *refs):
        ins, outs = refs[:3 * len(PARAM_ORDER)], refs[3 * len(PARAM_ORDER):]
        wide = lambda row: jnp.concatenate([sred_ref[row + k:row + k + 1, :] for k in range(4)], axis=1)
        head = lambda row: sred_ref[row:row + 1, 0:HEAD_DIM]
        grads = {
            "norm_g": lambda: wide(SB_NORM_G), "w_in": lambda: rin_ref[...], "b_f": lambda: sred_ref[SB_B_F:SB_B_F + 1, 0:FOX_HEADS],
            "pool_scale": lambda: sred_ref[SB_POOL_SCALE:SB_POOL_SCALE + 1, :], "fox_q_g": lambda: head(SB_FOX_Q),
            "fox_k_g": lambda: head(SB_FOX_K), "mem_norm_g": lambda: wide(SB_MEM_NORM_G), "w_mem_kv": lambda: rkv_ref[...],
            "mem_q_g": lambda: head(SB_MEM_Q), "mem_k_g": lambda: head(SB_MEM_K), "w_out": lambda: rout_ref[...],
        }
        for p, name in enumerate(PARAM_ORDER):
            w_ref, m_ref, v_ref = ins[3 * p:3 * p + 3]
            g_out, d_out, m_out, v_out = outs[4 * p:4 * p + 4]
            if name == "w_pool":
                for grp in range(4):
                    g = sred_ref[SB_W_POOL:SB_W_POOL + 64, 64 * grp:64 * (grp + 1)]
                    delta, m_new, v_new = _adamw(w_ref[grp], g, m_ref[grp], v_ref[grp])
                    g_out[grp], d_out[grp], m_out[grp], v_out[grp] = g, delta, m_new, v_new
            elif name == "w_in":
                for j in range(8):
                    rows = pl.ds(j, IN_CHUNK, stride=8)
                    g = rin_ref[0:IN_CHUNK, 128 * j:128 * (j + 1)]
                    delta, m_new, v_new = _adamw(w_ref[rows, :], g, m_ref[rows, :], v_ref[rows, :])
                    g_out[rows, :], d_out[rows, :], m_out[rows, :], v_out[rows, :] = g, delta, m_new, v_new
            else:
                g = grads[name]()
                delta, m_new, v_new = _adamw(w_ref[...], g, m_ref[...], v_ref[...])
                g_out[...], d_out[...], m_out[...], v_out[...] = g, delta, m_new, v_new

    flat = [t for name in PARAM_ORDER for t in params[name]]
    shapes = [params[name][0].shape for name in PARAM_ORDER for _ in range(4)]
    outs = pl.pallas_call(
        body,
        name="adamw_update",
        out_shape=tuple(jax.ShapeDtypeStruct(s, F32) for s in shapes),
        in_specs=[pl.BlockSpec(memory_space=pltpu.VMEM)] * (4 + len(flat)),
        out_specs=(pl.BlockSpec(memory_space=pltpu.VMEM),) * len(shapes),
        compiler_params=pltpu.CompilerParams(vmem_limit_bytes=VMEM_LIMIT),
    )(red_in, red_kv, red_out, sred, *flat)
    return {name: outs[4 * p:4 * p + 4] for p, name in enumerate(PARAM_ORDER)}


def kernel(x, mem, norm_g, w_in, b_f, w_pool, pool_scale, fox_q_g, fox_k_g, mem_norm_g, w_mem_kv, mem_q_g, mem_k_g, w_out, loss_target, m_norm_g, m_w_in, m_b_f, m_w_pool, m_pool_scale, m_fox_q_g, m_fox_k_g, m_mem_norm_g, m_w_mem_kv, m_mem_q_g, m_mem_k_g, m_w_out, v_norm_g, v_w_in, v_b_f, v_w_pool, v_pool_scale, v_fox_q_g, v_fox_k_g, v_mem_norm_g, v_w_mem_kv, v_mem_q_g, v_mem_k_g, v_w_out):
    given = dict(norm_g=(norm_g, m_norm_g, v_norm_g), w_in=(w_in, m_w_in, v_w_in), b_f=(b_f, m_b_f, v_b_f),
                 w_pool=(w_pool, m_w_pool, v_w_pool), pool_scale=(pool_scale, m_pool_scale, v_pool_scale),
                 fox_q_g=(fox_q_g, m_fox_q_g, v_fox_q_g), fox_k_g=(fox_k_g, m_fox_k_g, v_fox_k_g),
                 mem_norm_g=(mem_norm_g, m_mem_norm_g, v_mem_norm_g), w_mem_kv=(w_mem_kv, m_w_mem_kv, v_w_mem_kv),
                 mem_q_g=(mem_q_g, m_mem_q_g, v_mem_q_g), mem_k_g=(mem_k_g, m_mem_k_g, v_mem_k_g), w_out=(w_out, m_w_out, v_w_out))
    params = {name: tuple(t[0] if t.ndim > 2 else t for t in wmv) for name, wmv in given.items()}
    params["w_in"] = tuple(t.T.reshape(IN_CHUNK * 8, 128) for t in params["w_in"])
    x2, mem2, target2 = x[0], mem[0], loss_target[0]

    seg = jnp.asarray(np.kron(np.eye(FOX_HEADS), np.full((HEAD_DIM, HEAD_DIM), 1.0 / HEAD_DIM)), BF16)
    w_my, w_kv16, w_out16, wp_bd, bf_pad, gq8, gk8, gqm4 = _gather_w_in(
        params["w_in"][0], params["w_mem_kv"][0], params["w_out"][0], params["w_pool"][0], b_f, fox_q_g, fox_k_g, mem_q_g)
    proj_a, proj_b, h16, q_aug, k_aug, v_h, kt_aug, vt_aug = _proj_prep(x2, norm_g, w_my, bf_pad, gq8, gk8, seg)
    olse, lse_rows, w_kv_all, w_out_all = _fox_fwd(q_aug, k_aug, vt_aug, w_kv16, w_out16)
    km, vm = _mem_prep(mem2, mem_norm_g, w_kv_all, mem_k_g)
    d_out, d_mixed, dw_out, loss_blk = _mix_out_loss(proj_a, proj_b, olse, km, vm, wp_bd, pool_scale, gqm4, seg, x2, target2, w_out_all)

    d_ua, d_gb, d_qm, d_o, delta_rows, dwp_bd, d_scale, dkm, dvm, d_gqm = _mix_bwd(proj_a, proj_b, olse, d_mixed, km, vm, wp_bd,
                                                                                    pool_scale, gqm4, seg)
    dw_kv, d_mem_g, d_gkm = _mem_bwd(mem2, mem_norm_g, w_kv_all, mem_k_g, dkm, dvm)
    dq_aug, dk_aug, dv_h, land_kv, land_out = _fox_bwd(q_aug, k_aug, kt_aug, v_h, d_o, lse_rows, delta_rows, dw_kv, dw_out)
    d_q, d_k, d_v, d_f, d_gq, d_gk, d_bf = _fox_post(proj_a, proj_b, bf_pad, gq8, gk8, seg, dq_aug, dk_aug, dv_h)
    pieces = (d_ua, d_q, d_k, d_v, d_gb, d_qm)
    dw_in = _grad_w_in(h16, pieces, d_f)
    grad_x, red_in, red_kv, red_out, sred = _grad_x_exchange(pieces, d_f, w_my, x2, norm_g, d_out, dw_in, land_kv, land_out, d_mem_g,
                                                             d_scale, d_bf, d_gq, d_gk, d_gqm, d_gkm, dwp_bd, loss_blk)
    new = _update(red_in, red_kv, red_out, sred, params)
    result = [sred[SB_LOSS, 0], grad_x[None]]
    for kind in range(4):
        for name in PARAM_ORDER:
            out = new[name][kind]
            if name == "w_in":
                out = out.reshape(IN_CHUNK, D_MODEL).T
            result.append(out[None] if given[name][0].ndim > 2 else out)
    return tuple(result)
```

```python
import functools

import jax
import jax.numpy as jnp
import numpy as np
from jax import lax
from jax.experimental import pallas as pl
from jax.experimental.pallas import tpu as pltpu

F32 = jnp.float32
BF16 = jnp.bfloat16
MESH = pl.DeviceIdType.MESH

N_DEV = 8
D_MODEL = 1024
HEAD_DIM = 64
FOX_HEADS = 8
MEM_HEADS = 4
N_MEM = 256
POOL_WIDTH = 256
EPS = 1e-6
IN_WIDTH = 3080
IN_CHUNK = IN_WIDTH // N_DEV
CHUNK_ROWS = 400
F_OFF = 2048
MY_WIDTH = 3200
MY_F_OFF = 3072
PIECE = 512
GATES_OFF = 4 * PIECE
AUG_KEPT = 80
TM = 256
TMM = 512
TA = 256
HB = 8
HB_FWD = 8
AUG_ROWS = 72
HALO = 16
VMEM_LIMIT = 56 * 1024 * 1024

ADAM_LR = 0.001
ADAM_B1 = 0.9
ADAM_B2 = 0.999
ADAM_EPS = 1e-08
ADAM_WD = 0.01
ADAM_STEP = 10


def _dot(a, b):
    return jnp.dot(a, b, preferred_element_type=F32)


def _dot_nt(a, b):
    return lax.dot_general(a, b, (((1,), (1,)), ((), ())), preferred_element_type=F32)


def _dot_tn(a, b):
    return lax.dot_general(a, b, (((0,), (0,)), ((), ())), preferred_element_type=F32)


def _sigmoid(g):
    return 0.5 + 0.5 * jnp.tanh(0.5 * g)


def _split3(v):
    hi = v.astype(BF16)
    r1 = v - hi.astype(F32)
    mid = r1.astype(BF16)
    lo = (r1 - mid.astype(F32)).astype(BF16)
    return hi, mid, lo


def _rms(v):
    return lax.rsqrt(jnp.mean(v * v, axis=-1, keepdims=True) + EPS)


def _rms_bwd(a, r, g, dy):
    da = dy * g
    return r * (da - a * jnp.mean(da * a, axis=-1, keepdims=True)), dy * a


def _head_mean(z, seg):
    hi = z.astype(BF16)
    lo = (z - hi.astype(F32)).astype(BF16)
    if z.shape[1] == 256:
        return _dot(hi, seg) + _dot(lo, seg)
    half = seg[0:256, 0:256]
    return jnp.concatenate([_dot(hi[:, 0:256], half) + _dot(lo[:, 0:256], half),
                            _dot(hi[:, 256:512], half) + _dot(lo[:, 256:512], half)], axis=1)


def _head_rms(v, seg):
    return lax.rsqrt(_head_mean(v * v, seg) + EPS)


def _head_rms_bwd(a, r, g, dy, seg):
    da = dy * g
    return r * (da - a * _head_mean(da * a, seg)), dy * a


def _fold_heads(row, n_heads):
    out = row[:, 0:HEAD_DIM]
    for h in range(1, n_heads):
        out = out + row[:, HEAD_DIM * h:HEAD_DIM * (h + 1)]
    return out


def _params(sem, limit=VMEM_LIMIT):
    return pltpu.CompilerParams(dimension_semantics=sem, vmem_limit_bytes=limit)


def _full(shape):
    return pl.BlockSpec(shape, lambda *_: (0,) * len(shape))


def _chunk_segments(d):
    runs = []
    for lo, hi, shift in ((0, F_OFF, 0), (F_OFF, F_OFF + FOX_HEADS, MY_F_OFF - F_OFF), (F_OFF + FOX_HEADS, IN_WIDTH, -FOX_HEADS)):
        a, b = max(lo, IN_CHUNK * d), min(hi, IN_CHUNK * (d + 1))
        if a < b:
            runs.append((a + shift, b - a))
    return runs


def _my_place():
    x, y, c = lax.axis_index("x"), lax.axis_index("y"), lax.axis_index("c")
    return x, y, c, 4 * x + 2 * y + c


def _shard_rows(dev):
    return pl.ds(pl.multiple_of(128 * dev, 128), 128)


def _gather_w_in(w_in, w_kv, w_out, w_pool, b_f, gq, gk, gqm):
    def body(win_ref, wkv_ref, wout_ref, wp_ref, bf_ref, gq_ref, gk_ref, gqm_ref, wmy_ref, kv16_ref, out16_ref, wpbd_ref, bfpad_ref,
             gq8_ref, gk8_ref, gqm4_ref, in_all, pay_in, win_rows, send_sems, recv_sems, load_sem):
        x, y, c, me = _my_place()
        here, sibling = (x, y, c), (x, y, 1 - c)
        x_chip, y_chip, far_chip = (1 - x, y), (x, 1 - y), (1 - x, 1 - y)
        relay_from = (x ^ (1 - c), y ^ c)
        relay_to = (x ^ c, y ^ (1 - c))

        load = pltpu.make_async_copy(win_ref, win_rows, load_sem)
        load.start()
        load.wait()
        pay_in[...] = jnp.zeros((CHUNK_ROWS, D_MODEL), BF16)
        for j in range(8):
            pay_in[0:IN_CHUNK, 128 * j:128 * (j + 1)] = win_rows[pl.ds(j, IN_CHUNK, stride=8), :].astype(BF16)

        def copy(k, block, to, own=False):
            px, py, pc = block
            dst = in_all.at[4 * px + 2 * py + pc]
            return pltpu.make_async_remote_copy(src_ref=pay_in if own else dst, dst_ref=dst, send_sem=send_sems.at[k],
                                                recv_sem=recv_sems.at[k], device_id=to, device_id_type=MESH)

        first = [copy(0, here, sibling, own=True), copy(1, here, (*x_chip, c), own=True), copy(2, here, (*y_chip, c), own=True)]
        for cp in first:
            cp.start()
        in_all[me] = pay_in[...]
        kv16_ref[...] = wkv_ref[...].astype(BF16)
        out16_ref[...] = wout_ref[...].astype(BF16)
        wpbd_ref[...] = jnp.zeros((POOL_WIDTH, POOL_WIDTH), BF16)
        for grp in range(4):
            sl = slice(64 * grp, 64 * (grp + 1))
            wpbd_ref[sl, sl] = wp_ref[grp].astype(BF16)
        bfpad_ref[...] = jnp.zeros((1, 128), F32)
        bfpad_ref[:, 0:FOX_HEADS] = bf_ref[...]
        gq8_ref[...] = jnp.concatenate([gq_ref[...]] * FOX_HEADS, axis=1)
        gk8_ref[...] = jnp.concatenate([gk_ref[...]] * FOX_HEADS, axis=1)
        gqm4_ref[...] = jnp.concatenate([gqm_ref[...]] * MEM_HEADS, axis=1)
        copy(1 + c, (*relay_from, c), here).wait_recv()
        passed = [copy(3, (*relay_from, c), (*relay_to, c)), copy(4, (*relay_from, c), sibling)]
        for cp in passed:
            cp.start()
        copy(2 - c, (*relay_to, c), here).wait_recv()
        passed.append(copy(5, (*relay_to, c), sibling))
        passed[-1].start()
        copy(3, (*far_chip, c), here).wait_recv()
        passed.append(copy(6, (*far_chip, c), sibling))
        passed[-1].start()
        copy(0, sibling, here).wait_recv()
        for k, chip in ((4, relay_to), (5, relay_from), (6, far_chip)):
            copy(k, (*chip, 1 - c), here).wait_recv()
        for cp in first + passed:
            cp.wait_send()

        row_pad = jnp.zeros((512 - CHUNK_ROWS, 256), F32)
        for r0 in range(0, D_MODEL, 256):
            ch = [jnp.concatenate([in_all[d, :, r0:r0 + 256].astype(F32), row_pad], axis=0).T[:, 0:IN_CHUNK] for d in range(N_DEV)]
            lo = F_OFF - 5 * IN_CHUNK
            full = jnp.concatenate(ch[:5] + [ch[5][:, :lo], ch[5][:, lo + FOX_HEADS:], ch[6], ch[7], ch[5][:, lo:lo + FOX_HEADS],
                                             jnp.zeros((256, MY_WIDTH - IN_WIDTH), F32)], axis=1)
            wmy_ref[r0:r0 + 256, :] = full.astype(BF16)

    return pl.pallas_call(
        body,
        name="gather_w_in",
        out_shape=(jax.ShapeDtypeStruct((D_MODEL, MY_WIDTH), BF16), jax.ShapeDtypeStruct((128, 512), BF16),
                   jax.ShapeDtypeStruct((128, D_MODEL), BF16), jax.ShapeDtypeStruct((POOL_WIDTH, POOL_WIDTH), BF16),
                   jax.ShapeDtypeStruct((1, 128), F32), jax.ShapeDtypeStruct((1, PIECE), F32), jax.ShapeDtypeStruct((1, PIECE), F32),
                   jax.ShapeDtypeStruct((1, 256), F32)),
        in_specs=[pl.BlockSpec(memory_space=pl.ANY)] + [pl.BlockSpec(memory_space=pltpu.VMEM)] * 7,
        out_specs=(pl.BlockSpec(memory_space=pltpu.VMEM),) * 8,
        scratch_shapes=[
            pltpu.VMEM((N_DEV, CHUNK_ROWS, D_MODEL), BF16),
            pltpu.VMEM((CHUNK_ROWS, D_MODEL), BF16),
            pltpu.VMEM((IN_CHUNK * 8, 128), F32),
            pltpu.SemaphoreType.DMA((7,)),
            pltpu.SemaphoreType.DMA((7,)),
            pltpu.SemaphoreType.DMA,
        ],
        compiler_params=pltpu.CompilerParams(vmem_limit_bytes=VMEM_LIMIT),
    )(w_in, w_kv, w_out, w_pool, b_f, gq, gk, gqm)


def _row_block_exchange(kv_ref, out_ref, kv_dst, out_dst, send_sems, recv_sems, local_sems, gather):
    x, y, c, me = _my_place()
    remote = []
    for k in range(1, N_DEV):
        px, py, pc = x ^ (k >> 2), y ^ ((k >> 1) & 1), c ^ (k & 1)
        peer = 4 * px + 2 * py + pc
        for fam, (src, dst) in enumerate(((kv_ref, kv_dst), (out_ref, out_dst))):
            remote.append(pltpu.make_async_remote_copy(
                src_ref=src if gather else src.at[_shard_rows(peer)], dst_ref=dst.at[_shard_rows(me)] if gather else dst.at[me],
                send_sem=send_sems.at[7 * fam + k - 1], recv_sem=recv_sems.at[7 * fam + k - 1],
                device_id=(px, py, pc), device_id_type=MESH))
    local = [pltpu.make_async_copy(src if gather else src.at[_shard_rows(me)], dst.at[_shard_rows(me)] if gather else dst.at[me],
                                   local_sems.at[fam]) for fam, (src, dst) in enumerate(((kv_ref, kv_dst), (out_ref, out_dst)))]
    return remote, local


SB_ROWS, SB_W = 80, 256
SB_NORM_G, SB_MEM_NORM_G, SB_POOL_SCALE, SB_B_F, SB_FOX_Q, SB_FOX_K, SB_MEM_Q, SB_MEM_K, SB_LOSS, SB_W_POOL = 0, 4, 8, 9, 10, 11, 12, 13, 14, 16


def _alternate(*parts):
    state = [[part, 0, stages] for part, stages in parts]
    while state:
        least = min(state, key=lambda entry: entry[1] / entry[2])
        least[1] += 1
        if next(least[0], "done") == "done":
            state.remove(least)


def _log_sigmoid(z):
    return jnp.minimum(z, 0.0) - jnp.log(1.0 + jnp.exp(-jnp.abs(z)))


def _forget_lane_maps():
    to_q = np.zeros((128, FOX_HEADS * 128), np.float32)
    to_k = np.zeros((128, FOX_HEADS * 128), np.float32)
    for h in range(FOX_HEADS):
        for term in range(3):
            to_q[8 * term + h, 128 * h + 64 + term] = 1.0
            to_q[24, 128 * h + 67 + term] = 1.0
            to_k[24, 128 * h + 64 + term] = 1.0
            to_k[8 * term + h, 128 * h + 67 + term] = -1.0
    return jnp.asarray(to_q, BF16), jnp.asarray(to_k, BF16)


def _proj_prep(x, norm_g, w_my, bf_pad, gq, gk, seg):
    s_len = x.shape[0]
    n_blocks = s_len // TMM
    halves = TMM // TM
    to_q, to_k = _forget_lane_maps()
    q_off, f_off = PIECE, MY_F_OFF
    kept = 3 * PIECE + 128

    def body(x_ref, g_ref, w_ref, bf_ref, gq_ref, gk_ref, seg_ref, eq_ref, ek_ref,
             proj_a_ref, proj_b_ref, h_ref, qa_ref, ka_ref, vh_ref, kt_ref, vt_ref, kept_even, kept_odd, carry_ref):
        s = pl.program_id(0)

        def project(kept_ref):
            xv = x_ref[...]
            h_ref[...] = (xv * _rms(xv) * g_ref[...]).astype(BF16)
            yield
            for c0 in range(0, MY_WIDTH, 256):
                c1 = min(c0 + 256, MY_WIDTH)
                res = _dot(h_ref[...], w_ref[:, c0:c1])
                if c0 < 3 * PIECE:
                    proj_a_ref[:, c0:c1] = res
                if c0 >= GATES_OFF:
                    proj_b_ref[:, c0 - GATES_OFF:c1 - GATES_OFF] = res
                if q_off <= c0 < q_off + 3 * PIECE:
                    kept_ref[:, c0 - q_off:c1 - q_off] = res
                if c0 == f_off:
                    kept_ref[:, 3 * PIECE:kept] = res
                yield

        def operands(kept_ref):
            lane = lax.broadcasted_iota(jnp.int32, (TM, 128), 1)
            row = lax.broadcasted_iota(jnp.int32, (TM, TM), 0)
            col = lax.broadcasted_iota(jnp.int32, (TM, TM), 1)
            tri = jnp.where(col <= row, 1.0, 0.0).astype(BF16)
            one_at_64 = jnp.where(lane == 64, 1.0, 0.0).astype(BF16)
            zeros = jnp.zeros((TM, HEAD_DIM), BF16)
            for half in range(halves):
                rows = slice(TM * half, TM * (half + 1))
                logf = jnp.where(lane < FOX_HEADS, _log_sigmoid(kept_ref[rows, 3 * PIECE:kept] + bf_ref[...]), 0.0)
                hi, mid, lo = _split3(logf)
                cum = _dot(tri, hi) + _dot(tri, mid) + _dot(tri, lo) + carry_ref[...]
                q_all = kept_ref[rows, 0:PIECE]
                k_all = kept_ref[rows, PIECE:2 * PIECE]
                rq = _head_rms(q_all, seg_ref[...])
                rk = _head_rms(k_all, seg_ref[...])
                yield
                carry_ref[...] = cum[TM - 1:TM, :]
                f_hi, f_mid, f_lo = [t.astype(F32) for t in _split3(cum)]
                packed = (f_hi + pltpu.roll(f_mid, 8, 1) + pltpu.roll(f_lo, 16, 1)
                          + jnp.where(lane == 24, 1.0, 0.0)).astype(BF16)
                extra_q = _dot(packed, eq_ref[...]).astype(BF16)
                extra_k = _dot(packed, ek_ref[...]).astype(BF16)
                qn_all = (q_all * rq * gq_ref[...] * 0.125).astype(BF16)
                kn_all = (k_all * rk * gk_ref[...]).astype(BF16)
                v_all = kept_ref[rows, 2 * PIECE:3 * PIECE].astype(BF16)
                yield
                for h in range(FOX_HEADS):
                    sl = slice(HEAD_DIM * h, HEAD_DIM * (h + 1))
                    tile = slice(128 * h, 128 * (h + 1))
                    qa = jnp.where(lane < 64, jnp.concatenate([qn_all[:, sl], zeros], axis=1), extra_q[:, tile])
                    ka = jnp.where(lane < 64, jnp.concatenate([kn_all[:, sl], zeros], axis=1), extra_k[:, tile])
                    vh = v_all[:, sl]
                    v_aug = jnp.where(lane < 64, jnp.concatenate([vh, zeros], axis=1), one_at_64)
                    qa_ref[h, rows, :] = qa
                    ka_ref[h, rows, :] = ka
                    vh_ref[h, rows, :] = vh
                    kt_ref[h, half] = ka.T[0:AUG_KEPT]
                    vt_ref[h, half] = v_aug.T[0:AUG_KEPT]
                    if h % 2 == 1:
                        yield

        projecting = lambda kept_ref: (project(kept_ref), 1 + pl.cdiv(MY_WIDTH, 256))
        making = lambda kept_ref: (operands(kept_ref), halves * (2 + FOX_HEADS // 2))

        @pl.when(s == 0)
        def _():
            carry_ref[...] = jnp.zeros((1, 128), F32)
            _alternate(projecting(kept_even))

        @pl.when((s > 0) & (s < n_blocks) & (s % 2 == 1))
        def _():
            _alternate(projecting(kept_odd), making(kept_even))

        @pl.when((s > 0) & (s < n_blocks) & (s % 2 == 0))
        def _():
            _alternate(projecting(kept_even), making(kept_odd))

        @pl.when(s == n_blocks)
        def _():
            _alternate(making(kept_odd if n_blocks % 2 == 0 else kept_even))

    made = lambda s: jnp.minimum(s, n_blocks - 1)
    used = lambda s: jnp.maximum(s - 1, 0)
    head_blk = lambda w: pl.BlockSpec((FOX_HEADS, TMM, w), lambda s: (0, used(s), 0))
    tile_blk = pl.BlockSpec((FOX_HEADS, halves, AUG_KEPT, TM), lambda s: (0, used(s), 0, 0))
    tiled = jax.ShapeDtypeStruct((FOX_HEADS, s_len // TM, AUG_KEPT, TM), BF16)
    made_blk = lambda w: pl.BlockSpec((TMM, w), lambda s: (made(s), 0))
    return pl.pallas_call(
        body,
        name="proj_prep",
        grid=(n_blocks + 1,),
        in_specs=[pl.BlockSpec((TMM, D_MODEL), lambda s: (made(s), 0)), _full((1, D_MODEL)), _full((D_MODEL, MY_WIDTH)),
                  _full((1, 128)), _full((1, PIECE)), _full((1, PIECE)), _full((PIECE, PIECE)),
                  _full((128, FOX_HEADS * 128)), _full((128, FOX_HEADS * 128))],
        out_specs=[made_blk(3 * PIECE), made_blk(MY_WIDTH - GATES_OFF), made_blk(D_MODEL),
                   head_blk(128), head_blk(128), head_blk(HEAD_DIM), tile_blk, tile_blk],
        out_shape=[jax.ShapeDtypeStruct((s_len, 3 * PIECE), F32), jax.ShapeDtypeStruct((s_len, MY_WIDTH - GATES_OFF), F32),
                   jax.ShapeDtypeStruct((s_len, D_MODEL), BF16),
                   jax.ShapeDtypeStruct((FOX_HEADS, s_len, 128), BF16), jax.ShapeDtypeStruct((FOX_HEADS, s_len, 128), BF16),
                   jax.ShapeDtypeStruct((FOX_HEADS, s_len, HEAD_DIM), BF16), tiled, tiled],
        scratch_shapes=[pltpu.VMEM((TMM, kept), F32), pltpu.VMEM((TMM, kept), F32), pltpu.VMEM((1, 128), F32)],
        compiler_params=_params(("arbitrary",)),
    )(x, norm_g, w_my, bf_pad, gq, gk, seg, to_q, to_k)


def _mem_prep(mem, g, w_kv, gk):
    def body(mem_ref, g_ref, w_ref, gk_ref, km_ref, vm_ref):
        m = mem_ref[...]
        kv = _dot((m * _rms(m) * g_ref[...]).astype(BF16), w_ref[...])
        for h in range(MEM_HEADS):
            sl = slice(HEAD_DIM * h, HEAD_DIM * (h + 1))
            kh = kv[:, sl]
            km_ref[:, sl] = (kh * _rms(kh) * gk_ref[...]).astype(BF16)
        vm_ref[...] = kv[:, 256:512].astype(BF16)

    return pl.pallas_call(
        body,
        name="mem_prep",
        out_shape=(jax.ShapeDtypeStruct((N_MEM, 256), BF16),) * 2,
        in_specs=[pl.BlockSpec(memory_space=pltpu.VMEM)] * 4,
        out_specs=(pl.BlockSpec(memory_space=pltpu.VMEM),) * 2,
        compiler_params=pltpu.CompilerParams(vmem_limit_bytes=VMEM_LIMIT),
    )(mem, g, w_kv, gk)


def _causal_mask_t():
    row = lax.broadcasted_iota(jnp.int32, (TA, TA), 0)
    col = lax.broadcasted_iota(jnp.int32, (TA, TA), 1)
    return row <= col


def _fox_fwd(q_aug, k_aug, vt_aug, w_kv16, w_out16):
    s_len = q_aug.shape[1]
    n_tiles = s_len // TA
    hb = HB_FWD
    n_groups = FOX_HEADS // hb

    def body(q_ref, k_new, vt_new, kv16_ref, out16_ref, o_ref, st_ref, kv_all, out_all, k_ref, vt_ref, send_sems, recv_sems, local_sems):
        qi = pl.program_id(1)
        for h in range(hb):
            k_ref[h, pl.ds(pl.multiple_of(qi * TA, TA), TA), :] = k_new[h]
            vt_ref[h, qi] = vt_new[h, 0]
        remote, local = _row_block_exchange(kv16_ref, out16_ref, kv_all, out_all, send_sems, recv_sems, local_sems, gather=True)

        @pl.when((pl.program_id(0) == 0) & (qi == 0))
        def _():
            for cp in remote + local:
                cp.start()

        qs = [q_ref[h] for h in range(hb)]

        def step(t0, carry, masked, width):
            rows = pl.ds(pl.multiple_of(t0 * TA, TA), width * TA)
            scores = [_dot_nt(k_ref[h, rows, :], qs[h]) for h in range(hb)]
            soft = []
            for h in range(hb):
                m, _ = carry[h]
                s = jnp.where(_causal_mask_t(), scores[h], -1e30) if masked else scores[h]
                m_new = jnp.maximum(m, jnp.max(s, axis=0, keepdims=True))
                soft.append((m_new, jnp.exp(m - m_new), jnp.exp(s - m_new).astype(BF16)))
            out = []
            for h, (m_new, alpha, p) in enumerate(soft):
                acc = alpha * carry[h][1]
                for t in range(width):
                    acc = acc + _dot(vt_ref[h, t0 + t, 0:AUG_ROWS, :], p[t * TA:(t + 1) * TA])
                out.append((m_new, acc))
            return tuple(out)

        init = tuple((jnp.full((1, TA), -1e30, F32), jnp.zeros((AUG_ROWS, TA), F32)) for _ in range(hb))
        carry = lax.fori_loop(0, qi // 2, lambda j, cr: step(2 * j, cr, False, 2), init)
        carry = lax.fori_loop(2 * (qi // 2), qi, lambda j, cr: step(j, cr, False, 1), carry)
        carry = step(qi, carry, True, 1)
        for h in range(hb):
            m, acc = carry[h]
            l = acc[HEAD_DIM:HEAD_DIM + 1, :]
            lse = m + jnp.log(l)
            o_ref[h] = jnp.concatenate([acc[0:HEAD_DIM] / l, jnp.broadcast_to(lse, (HEAD_DIM, TA))], axis=0).T
            st_ref[h, 0] = jnp.broadcast_to(lse, (8, TA))

        @pl.when((pl.program_id(0) == n_groups - 1) & (qi == n_tiles - 1))
        def _():
            for cp in remote:
                cp.wait_recv()
            for cp in remote:
                cp.wait_send()
            for cp in local:
                cp.wait()

    hbm = pl.BlockSpec(memory_space=pl.ANY)
    return pl.pallas_call(
        body,
        name="fox_fwd",
        grid=(n_groups, n_tiles),
        in_specs=[pl.BlockSpec((hb, TA, 128), lambda g, i: (g, i, 0)), pl.BlockSpec((hb, TA, 128), lambda g, i: (g, i, 0)),
                  pl.BlockSpec((hb, 1, AUG_KEPT, TA), lambda g, i: (g, i, 0, 0)), hbm, hbm],
        out_specs=[pl.BlockSpec((hb, TA, 128), lambda g, i: (g, i, 0)), pl.BlockSpec((hb, 1, 8, TA), lambda g, i: (g, i, 0, 0)),
                   hbm, hbm],
        out_shape=[jax.ShapeDtypeStruct((FOX_HEADS, s_len, 128), F32), jax.ShapeDtypeStruct((FOX_HEADS, n_tiles, 8, TA), F32),
                   jax.ShapeDtypeStruct((D_MODEL, 512), BF16), jax.ShapeDtypeStruct((D_MODEL, D_MODEL), BF16)],
        scratch_shapes=[pltpu.VMEM((hb, s_len, 128), BF16), pltpu.VMEM((hb, n_tiles, AUG_KEPT, TA), BF16),
                        pltpu.SemaphoreType.DMA((14,)), pltpu.SemaphoreType.DMA((14,)), pltpu.SemaphoreType.DMA((2,))],
        compiler_params=_params(("arbitrary", "arbitrary")),
    )(q_aug, k_aug, vt_aug, w_kv16, w_out16)


def _lane_group(lane, a, b, c, d):
    return jnp.where(lane < 64, a, jnp.where(lane < 128, b, jnp.where(lane < 192, c, d)))


def _pool_count(row0):
    lane = lax.broadcasted_iota(jnp.int32, (TM, POOL_WIDTH), 1)
    t1 = row0 + lax.broadcasted_iota(jnp.int32, (TM, POOL_WIDTH), 0) + 1
    return 1.0 / jnp.minimum(t1, _lane_group(lane, 2, 4, 8, 16)).astype(F32), lane


def _pool_delta(u, halo, cnt, lane):
    xe = jnp.concatenate([halo, u], axis=0)
    s2 = xe + pltpu.roll(xe, 1, 0)
    s4 = s2 + pltpu.roll(s2, 2, 0)
    s8 = s4 + pltpu.roll(s4, 4, 0)
    s16 = s8 + pltpu.roll(s8, 8, 0)
    win = _lane_group(lane, s2[HALO:], s4[HALO:], s8[HALO:], s16[HALO:])
    return win * cnt - u


def _pool_delta_bwd(dd, dd_next, cnt, cnt_next, lane):
    ee = jnp.concatenate([dd * cnt, dd_next * cnt_next], axis=0)
    n = TM + HALO
    r2 = ee + pltpu.roll(ee, n - 1, 0)
    r4 = r2 + pltpu.roll(r2, n - 2, 0)
    r8 = r4 + pltpu.roll(r4, n - 4, 0)
    r16 = r8 + pltpu.roll(r8, n - 8, 0)
    return _lane_group(lane, r2[:TM], r4[:TM], r8[:TM], r16[:TM]) - dd


def _mem_attn(qm, gq, seg, km_ref, vm_ref):
    r = _head_rms(qm, seg)
    a = qm * r
    q16 = (a * gq * 0.125).astype(BF16)
    heads = []
    for h in range(MEM_HEADS):
        sl = slice(HEAD_DIM * h, HEAD_DIM * (h + 1))
        s = _dot_nt(q16[:, sl], km_ref[:, sl])
        e = jnp.exp(s - jnp.max(s, axis=-1, keepdims=True))
        p = e * (1.0 / jnp.sum(e, axis=-1, keepdims=True))
        heads.append((p, _dot(p.astype(BF16), vm_ref[:, sl])))
    return a, r, q16, heads


def _mem_attn_stages(qm, gq, seg, km_ref, vm_ref, result):
    r = _head_rms(qm, seg)
    a = qm * r
    q16 = (a * gq * 0.125).astype(BF16)
    yield
    head_slices = [slice(HEAD_DIM * h, HEAD_DIM * (h + 1)) for h in range(MEM_HEADS)]
    scores = [_dot_nt(q16[:, sl], km_ref[:, sl]) for sl in head_slices]
    yield
    heads = []
    for s, sl in zip(scores, head_slices):
        e = jnp.exp(s - jnp.max(s, axis=-1, keepdims=True))
        p = e * (1.0 / jnp.sum(e, axis=-1, keepdims=True))
        heads.append((p, _dot(p.astype(BF16), vm_ref[:, sl])))
    result.extend([a, r, q16, heads])
    yield


def _mix_out_loss(proj_a, proj_b, olse, km, vm, wp_bd, pool_scale, gq_m, seg, x, target, w_out):
    s_len = x.shape[0]
    n_blocks = s_len // TMM
    halves = TMM // TM

    def body(ua_ref, halo_ref, gb_ref, qm_ref, ol_ref, km_ref, vm_ref, wp_ref, sc_ref, gq_ref, seg_ref, x_ref, t_ref, w_ref,
             dout_ref, dmix_ref, dw16_ref, loss_ref, mixed_even, mixed_odd, d16_ref, dw_ref):
        s = pl.program_id(0)
        chunks = [slice(c0, c0 + 256) for c0 in range(0, D_MODEL, 256)]

        def matmuls(mixed_ref):
            for cols in chunks:
                err = x_ref[:, cols] + _dot(mixed_ref[...], w_ref[:, cols]) - t_ref[:, cols]
                loss_ref[...] += (0.5 / D_MODEL) * jnp.sum(err * err)
                d_out = err / float(D_MODEL)
                dout_ref[:, cols] = d_out
                d16_ref[:, cols] = d_out.astype(BF16)
                yield
            for cols in chunks:
                dmix_ref[:, cols] = _dot_nt(d16_ref[...], w_ref[cols, :])
                yield
            for cols in chunks:
                dw_ref[:, cols] += _dot_tn(mixed_ref[...], d16_ref[:, cols])
                yield

        def concatenation(mixed_ref):
            for half in range(halves):
                r0 = TM * half
                rows = slice(r0, r0 + TM)
                u = ua_ref[rows, 0:256]
                g_a = ua_ref[rows, 256:512]
                halo = jnp.where(s > 0, halo_ref[...], 0.0) if half == 0 else ua_ref[r0 - HALO:r0, 0:256]
                cnt, lane = _pool_count(s * TMM + r0)
                d = _pool_delta(u, halo, cnt, lane).astype(BF16)
                y_a = _dot(d, wp_ref[...]) * sc_ref[...]
                mixed_ref[rows, 0:256] = (y_a * (g_a * _sigmoid(g_a))).astype(BF16)
                yield
                g_b = gb_ref[rows, :]
                y_b = jnp.concatenate([ol_ref[h, rows, 0:HEAD_DIM] for h in range(FOX_HEADS)], axis=1)
                mixed_ref[rows, 256:768] = (y_b * (g_b * _sigmoid(g_b))).astype(BF16)
                yield
                attn = []
                stages = _mem_attn_stages(qm_ref[rows, 0:256], gq_ref[...], seg_ref[0:256, 0:256], km_ref, vm_ref, attn)
                next(stages)
                yield
                next(stages)
                yield
                next(stages)
                g_m = qm_ref[rows, 256:512]
                y_m = jnp.concatenate([y for _, y in attn[3]], axis=1)
                mixed_ref[rows, 768:1024] = (y_m * (g_m * _sigmoid(g_m))).astype(BF16)
                yield

        multiplying = lambda mixed_ref: (matmuls(mixed_ref), 3 * len(chunks))
        building = lambda mixed_ref: (concatenation(mixed_ref), 5 * halves)

        @pl.when(s == 0)
        def _():
            dw_ref[...] = jnp.zeros((D_MODEL, D_MODEL), F32)
            loss_ref[...] = jnp.zeros((8, 128), F32)
            _alternate(building(mixed_even))

        @pl.when((s > 0) & (s < n_blocks) & (s % 2 == 1))
        def _():
            _alternate(multiplying(mixed_even), building(mixed_odd))

        @pl.when((s > 0) & (s < n_blocks) & (s % 2 == 0))
        def _():
            _alternate(multiplying(mixed_odd), building(mixed_even))

        @pl.when(s == n_blocks)
        def _():
            _alternate(multiplying(mixed_odd if n_blocks % 2 == 0 else mixed_even))
            dw16_ref[...] = dw_ref[...].astype(BF16)

    build = lambda s: jnp.minimum(s, n_blocks - 1)
    use = lambda s: jnp.maximum(s - 1, 0)
    piece = lambda j: pl.BlockSpec((TMM, PIECE), lambda s: (build(s), j))
    halo_blk = pl.BlockSpec((HALO, 256), lambda s: (jnp.maximum(build(s) * (TMM // HALO) - 1, 0), 0))
    row = pl.BlockSpec((TMM, D_MODEL), lambda s: (use(s), 0))
    return pl.pallas_call(
        body,
        name="mix_out_loss",
        grid=(n_blocks + 1,),
        in_specs=[piece(0), halo_blk, piece(0), piece(1), pl.BlockSpec((FOX_HEADS, TMM, 128), lambda s: (0, build(s), 0)),
                  _full((N_MEM, 256)), _full((N_MEM, 256)), _full((256, 256)), _full((1, 256)), _full((1, 256)),
                  _full((PIECE, PIECE)), row, row, _full((D_MODEL, D_MODEL))],
        out_specs=[row, row, _full((D_MODEL, D_MODEL)), _full((8, 128))],
        out_shape=[jax.ShapeDtypeStruct((s_len, D_MODEL), F32), jax.ShapeDtypeStruct((s_len, D_MODEL), F32),
                   jax.ShapeDtypeStruct((D_MODEL, D_MODEL), BF16), jax.ShapeDtypeStruct((8, 128), F32)],
        scratch_shapes=[pltpu.VMEM((TMM, D_MODEL), BF16), pltpu.VMEM((TMM, D_MODEL), BF16), pltpu.VMEM((TMM, D_MODEL), BF16),
                        pltpu.VMEM((D_MODEL, D_MODEL), F32)],
        compiler_params=_params(("arbitrary",)),
    )(proj_a, proj_a, proj_b, proj_b, olse, km, vm, wp_bd, pool_scale, gq_m, seg, x, target, w_out)


def _silu_pair(g):
    s = _sigmoid(g)
    return g * s, s * (1.0 + g * (1.0 - s))


def _mix_bwd(proj_a, proj_b, olse, d_mixed, km, vm, wp_bd, pool_scale, gq_m, seg):
    s_len = proj_a.shape[0]
    n_tiles = s_len // TM

    def body(ua_ref, halo_ref, ga_next_ref, gb_ref, qm_ref, ol_ref, dm_ref, dm_next_ref, km_ref, vm_ref, wp_ref, sc_ref, gq_ref,
             seg_ref, dua_ref, dgb_ref, dqm_ref, do_ref, dl_ref, dwp_ref, dsc_ref, dkm_ref, dvm_ref, dgq_ref):
        i = pl.program_id(0)

        @pl.when(i == 0)
        def _():
            dwp_ref[...] = jnp.zeros((256, 256), F32)
            dsc_ref[...] = jnp.zeros((1, 256), F32)
            dkm_ref[...] = jnp.zeros((N_MEM, 256), F32)
            dvm_ref[...] = jnp.zeros((N_MEM, 256), F32)
            dgq_ref[...] = jnp.zeros((1, HEAD_DIM), F32)

        def pooling_mixer():
            u = ua_ref[:, 0:256]
            g_a = ua_ref[:, 256:512]
            halo = jnp.where(i > 0, halo_ref[...], 0.0)
            cnt, lane = _pool_count(i * TM)
            d16 = _pool_delta(u, halo, cnt, lane).astype(BF16)
            t = _dot(d16, wp_ref[...])
            yield
            y_a = t * sc_ref[...]
            silu_a, dsilu_a = _silu_pair(g_a)
            dm_a = dm_ref[:, 0:256]
            dy_a = dm_a * silu_a
            dsc_ref[...] += jnp.sum(dy_a * t, axis=0, keepdims=True)
            dt16 = (dy_a * sc_ref[...]).astype(BF16)
            dwp_ref[...] += _dot_tn(d16, dt16)
            dd = _dot_nt(dt16, wp_ref[...])
            dua_ref[:, 256:512] = (dm_a * y_a * dsilu_a).astype(BF16)
            yield
            g_next = ga_next_ref[...]
            dt_next = (dm_next_ref[...] * (g_next * _sigmoid(g_next)) * sc_ref[...]).astype(BF16)
            dd_next = jnp.where(i < n_tiles - 1, _dot_nt(dt_next, wp_ref[...]), 0.0)
            cnt_next = _pool_count((i + 1) * TM)[0][0:HALO]
            dua_ref[:, 0:256] = _pool_delta_bwd(dd, dd_next, cnt, cnt_next, lane).astype(BF16)
            yield

        def output_gate():
            silu_b, dsilu_b = _silu_pair(gb_ref[...])
            dm_b = dm_ref[:, 256:768]
            o_all = jnp.concatenate([ol_ref[h][:, 0:HEAD_DIM] for h in range(FOX_HEADS)], axis=1)
            d_o = dm_b * silu_b
            dgb_ref[...] = (dm_b * o_all * dsilu_b).astype(BF16)
            d_o16 = d_o.astype(BF16)
            for h in range(FOX_HEADS):
                do_ref[h] = d_o16[:, HEAD_DIM * h:HEAD_DIM * (h + 1)]
            yield
            prod = d_o * o_all
            hi = prod.astype(BF16)
            lo = (prod - hi.astype(F32)).astype(BF16)
            head_of_lane = lax.broadcasted_iota(jnp.int32, (FOX_HEADS, PIECE), 1) // HEAD_DIM
            pick = jnp.where(head_of_lane == lax.broadcasted_iota(jnp.int32, (FOX_HEADS, PIECE), 0), 1.0, 0.0).astype(BF16)
            delta = _dot_nt(pick, hi) + _dot_nt(pick, lo)
            for h in range(FOX_HEADS):
                dl_ref[h, 0] = jnp.broadcast_to(delta[h:h + 1, :], (8, TM))
            yield

        def memory_attention():
            seg = seg_ref[0:256, 0:256]
            a, r, q16, heads = _mem_attn(qm_ref[:, 0:256], gq_ref[...], seg, km_ref, vm_ref)
            yield
            silu_m, dsilu_m = _silu_pair(qm_ref[:, 256:512])
            dm_m = dm_ref[:, 768:1024]
            y_m = jnp.concatenate([y for _, y in heads], axis=1)
            dqm_ref[:, 256:512] = (dm_m * y_m * dsilu_m).astype(BF16)
            dy16_all = (dm_m * silu_m).astype(BF16)
            d_scores = []
            head_slices = [slice(HEAD_DIM * h, HEAD_DIM * (h + 1)) for h in range(MEM_HEADS)]
            dps = [_dot_nt(dy16_all[:, sl], vm_ref[:, sl]) for sl in head_slices]
            for h, sl in enumerate(head_slices):
                dvm_ref[:, sl] += _dot_tn(heads[h][0].astype(BF16), dy16_all[:, sl])
            yield
            for h, sl in enumerate(head_slices):
                p = heads[h][0]
                dp = dps[h]
                ds16 = (p * (dp - jnp.sum(dp * p, axis=-1, keepdims=True))).astype(BF16)
                dkm_ref[:, sl] += _dot_tn(ds16, q16[:, sl])
                d_scores.append(_dot(ds16, km_ref[:, sl]))
                yield
            dq, dg_rows = _head_rms_bwd(a, r, gq_ref[...], jnp.concatenate(d_scores, axis=1) * 0.125, seg)
            dqm_ref[:, 0:256] = dq.astype(BF16)
            dgq_ref[...] += _fold_heads(jnp.sum(dg_rows, axis=0, keepdims=True), MEM_HEADS)
            yield

        _alternate((pooling_mixer(), 3))
        _alternate((memory_attention(), 3 + MEM_HEADS), (output_gate(), 2))

    n_halo = s_len // HALO
    piece = lambda j: pl.BlockSpec((TM, PIECE), lambda i: (i, j))
    before = pl.BlockSpec((HALO, 256), lambda i: (jnp.maximum(i * (TM // HALO) - 1, 0), 0))
    after = lambda j: pl.BlockSpec((HALO, 256), lambda i: (jnp.minimum((i + 1) * (TM // HALO), n_halo - 1), j))
    row = pl.BlockSpec((TM, D_MODEL), lambda i: (i, 0))
    out_piece = pl.BlockSpec((TM, PIECE), lambda i: (i, 0))
    return pl.pallas_call(
        body,
        name="mix_bwd",
        grid=(n_tiles,),
        in_specs=[piece(0), before, after(1), piece(0), piece(1), pl.BlockSpec((FOX_HEADS, TM, 128), lambda i: (0, i, 0)),
                  row, after(0), _full((N_MEM, 256)), _full((N_MEM, 256)), _full((256, 256)), _full((1, 256)), _full((1, 256)),
                  _full((PIECE, PIECE))],
        out_specs=[out_piece, out_piece, out_piece, pl.BlockSpec((FOX_HEADS, TM, HEAD_DIM), lambda i: (0, i, 0)),
                   pl.BlockSpec((FOX_HEADS, 1, 8, TM), lambda i: (0, i, 0, 0)),
                   _full((256, 256)), _full((1, 256)), _full((N_MEM, 256)), _full((N_MEM, 256)), _full((1, HEAD_DIM))],
        out_shape=[jax.ShapeDtypeStruct((s_len, PIECE), BF16)] * 3 + [
            jax.ShapeDtypeStruct((FOX_HEADS, s_len, HEAD_DIM), BF16),
            jax.ShapeDtypeStruct((FOX_HEADS, n_tiles, 8, TM), F32),
            jax.ShapeDtypeStruct((256, 256), F32), jax.ShapeDtypeStruct((1, 256), F32),
            jax.ShapeDtypeStruct((N_MEM, 256), F32), jax.ShapeDtypeStruct((N_MEM, 256), F32),
            jax.ShapeDtypeStruct((1, HEAD_DIM), F32)],
        compiler_params=_params(("arbitrary",)),
    )(proj_a, proj_a, proj_a, proj_b, proj_b, olse, d_mixed, d_mixed, km, vm, wp_bd, pool_scale, gq_m, seg)


def _fox_bwd(q_aug, k_aug, kt_aug, v_h, d_o, lse_rows, delta_rows, dw_kv16, dw_out16):
    s_len = q_aug.shape[1]
    n_tiles = s_len // TA
    n_groups = FOX_HEADS // HB

    def body(q_hbm, k_ref, kt_ref, v_ref, do_hbm, st_ref, dl_ref, dkv16_ref, dout16_ref, dq_ref, dk_ref, dv_ref, land_kv, land_out,
             dqt_ref, q_ref, do_ref, send_sems, recv_sems, local_sems, tile_sems):
        j = pl.program_id(1)
        remote, local = _row_block_exchange(dkv16_ref, dout16_ref, land_kv, land_out, send_sems, recv_sems, local_sems, gather=False)

        def tile_loads(i):
            rows = pl.ds(pl.multiple_of(i * TA, TA), TA)
            return [pltpu.make_async_copy(q_hbm.at[:, rows, :], q_ref.at[:, rows, :], tile_sems.at[2 * i]),
                    pltpu.make_async_copy(do_hbm.at[:, rows, :], do_ref.at[:, rows, :], tile_sems.at[2 * i + 1])]

        @pl.when((pl.program_id(0) == 0) & (j == 0))
        def _():
            for i in range(n_tiles):
                for cp in tile_loads(i):
                    cp.start()
            for cp in remote + local:
                cp.start()

        @pl.when(j == 0)
        def _():
            dqt_ref[...] = jnp.zeros((HB, n_tiles, AUG_ROWS, TA), F32)

        ks = [k_ref[h] for h in range(HB)]
        kts = [kt_ref[h, 0, 0:AUG_ROWS, :] for h in range(HB)]
        vs = [v_ref[h] for h in range(HB)]

        def pair(i0, acc, masked, width):
            @pl.when(j == 0)
            def _():
                for t in range(width):
                    for cp in tile_loads(i0 + t):
                        cp.wait()

            rows = pl.ds(pl.multiple_of(i0 * TA, TA), width * TA)
            qs = [q_ref[h, rows, :] for h in range(HB)]
            d_os = [do_ref[h, rows, :] for h in range(HB)]
            row_stat = lambda ref, h: jnp.concatenate([ref[h, i0 + t, 0:1, :] for t in range(width)], axis=1)
            scores = [(_dot_nt(ks[h], qs[h]), _dot_nt(vs[h], d_os[h])) for h in range(HB)]
            probs = []
            for h in range(HB):
                s, dp = scores[h]
                if masked:
                    s = jnp.where(_causal_mask_t(), s, -1e30)
                p = jnp.exp(s - row_stat(st_ref, h))
                probs.append((p.astype(BF16), (p * (dp - row_stat(dl_ref, h))).astype(BF16)))
            out = []
            for h in range(HB):
                p16, ds16 = probs[h]
                dqt = _dot(kts[h], ds16)
                for t in range(width):
                    dqt_ref[h, i0 + t] += dqt[:, t * TA:(t + 1) * TA]
                out.append((acc[h][0] + _dot(ds16, qs[h]), acc[h][1] + _dot(p16, d_os[h])))
            return tuple(out)

        zero = tuple((jnp.zeros((TA, 128), F32), jnp.zeros((TA, HEAD_DIM), F32)) for _ in range(HB))
        acc = pair(j, zero, True, 1)
        odd = (n_tiles - 1 - j) % 2
        acc = lax.fori_loop(j + 1, j + 1 + odd, lambda i, a: pair(i, a, False, 1), acc)
        acc = lax.fori_loop(0, (n_tiles - 1 - j) // 2, lambda t, a: pair(j + 1 + odd + 2 * t, a, False, 2), acc)
        pad = jnp.zeros((128 - AUG_ROWS, TA), F32)
        for h in range(HB):
            dk_ref[h] = acc[h][0]
            dv_ref[h] = acc[h][1].astype(BF16)
            dq_ref[h] = jnp.concatenate([dqt_ref[h, j], pad], axis=0).T

        @pl.when((pl.program_id(0) == n_groups - 1) & (j == n_tiles - 1))
        def _():
            for cp in remote:
                cp.wait_recv()
            for cp in remote:
                cp.wait_send()
            for cp in local:
                cp.wait()

    assert n_groups == 1
    resident = pl.BlockSpec(memory_space=pltpu.VMEM)
    rows_blk = lambda r: resident
    tile = lambda w: pl.BlockSpec((HB, TA, w), lambda g, j: (g, j, 0))
    hbm = pl.BlockSpec(memory_space=pl.ANY)
    return pl.pallas_call(
        body,
        name="fox_bwd",
        grid=(n_groups, n_tiles),
        in_specs=[hbm, tile(128), pl.BlockSpec((HB, 1, AUG_KEPT, TA), lambda g, j: (g, j, 0, 0)), tile(HEAD_DIM),
                  hbm, rows_blk(8), rows_blk(8), hbm, hbm],
        out_specs=[tile(128), tile(128), tile(HEAD_DIM), hbm, hbm],
        out_shape=[jax.ShapeDtypeStruct((FOX_HEADS, s_len, 128), F32), jax.ShapeDtypeStruct((FOX_HEADS, s_len, 128), F32),
                   jax.ShapeDtypeStruct((FOX_HEADS, s_len, HEAD_DIM), BF16),
                   jax.ShapeDtypeStruct((N_DEV, 128, 512), BF16), jax.ShapeDtypeStruct((N_DEV, 128, D_MODEL), BF16)],
        scratch_shapes=[pltpu.VMEM((HB, n_tiles, AUG_ROWS, TA), F32),
                        pltpu.VMEM((HB, s_len, 128), BF16), pltpu.VMEM((HB, s_len, HEAD_DIM), BF16),
                        pltpu.SemaphoreType.DMA((14,)), pltpu.SemaphoreType.DMA((14,)), pltpu.SemaphoreType.DMA((2,)),
                        pltpu.SemaphoreType.DMA((2 * n_tiles,))],
        compiler_params=_params(("arbitrary", "arbitrary")),
    )(q_aug, k_aug, kt_aug, v_h, d_o, lse_rows, delta_rows, dw_kv16, dw_out16)


def _fox_post(proj_a, proj_b, bf_pad, gq, gk, seg, dq_aug, dk_aug, dv_h):
    s_len = proj_a.shape[0]
    n_tiles = s_len // TM

    def body(q_ref, k_ref, f_ref, bf_ref, gq_ref, gk_ref, seg_ref, dqa_ref, dka_ref, dvh_ref,
             dq_ref, dk_ref, dv_ref, df_ref, dgq_ref, dgk_ref, dbf_ref, carry_ref):
        @pl.when(pl.program_id(0) == 0)
        def _():
            carry_ref[...] = jnp.zeros((1, 128), F32)
            dgq_ref[...] = jnp.zeros((1, HEAD_DIM), F32)
            dgk_ref[...] = jnp.zeros((1, HEAD_DIM), F32)
            dbf_ref[...] = jnp.zeros((1, 128), F32)

        lane = lax.broadcasted_iota(jnp.int32, (TM, 128), 1)
        seg = seg_ref[...]
        dqas = [dqa_ref[h] for h in range(FOX_HEADS)]
        dkas = [dka_ref[h] for h in range(FOX_HEADS)]
        def head_norms():
            q_all = q_ref[...]
            k_all = k_ref[...]
            rq = _head_rms(q_all, seg)
            rk = _head_rms(k_all, seg)
            yield
            dy_q = jnp.concatenate([t[:, 0:HEAD_DIM] for t in dqas], axis=1) * 0.125
            dq, dgq_rows = _head_rms_bwd(q_all * rq, rq, gq_ref[...], dy_q, seg)
            dq_ref[...] = dq.astype(BF16)
            dgq_ref[...] += _fold_heads(jnp.sum(dgq_rows, axis=0, keepdims=True), FOX_HEADS)
            yield
            dy_k = jnp.concatenate([t[:, 0:HEAD_DIM] for t in dkas], axis=1)
            dk, dgk_rows = _head_rms_bwd(k_all * rk, rk, gk_ref[...], dy_k, seg)
            dk_ref[...] = dk.astype(BF16)
            dgk_ref[...] += _fold_heads(jnp.sum(dgk_rows, axis=0, keepdims=True), FOX_HEADS)
            dv_ref[...] = jnp.concatenate([dvh_ref[h] for h in range(FOX_HEADS)], axis=1)
            yield

        def forget_gates():
            d_cum = jnp.zeros((TM, 128), F32)
            for h in range(FOX_HEADS):
                d_cum = jnp.where(lane == h, dqas[h][:, 64:65] - dkas[h][:, 67:68], d_cum)
            row = lax.broadcasted_iota(jnp.int32, (TM, TM), 0)
            col = lax.broadcasted_iota(jnp.int32, (TM, TM), 1)
            tri = jnp.where(col >= row, 1.0, 0.0).astype(BF16)
            hi, mid, lo = _split3(d_cum)
            d_logf = _dot(tri, hi) + _dot(tri, mid) + _dot(tri, lo) + carry_ref[...]
            yield
            carry_ref[...] = d_logf[0:1, :]
            z = f_ref[...] + bf_ref[...]
            d_f = jnp.where(lane < FOX_HEADS, d_logf / (1.0 + jnp.exp(z)), 0.0)
            df_ref[...] = d_f.astype(BF16)
            dbf_ref[...] += jnp.sum(d_f, axis=0, keepdims=True)
            yield

        _alternate((head_norms(), 3), (forget_gates(), 2))

    rev = lambda i: n_tiles - 1 - i
    col_blk = lambda j: pl.BlockSpec((TM, PIECE), lambda i: (rev(i), j))
    head_blk = lambda w: pl.BlockSpec((FOX_HEADS, TM, w), lambda i: (0, rev(i), 0))
    out_piece = pl.BlockSpec((TM, PIECE), lambda i: (rev(i), 0))
    return pl.pallas_call(
        body,
        name="fox_post",
        grid=(n_tiles,),
        in_specs=[col_blk(1), col_blk(2), pl.BlockSpec((TM, 128), lambda i: (rev(i), (MY_F_OFF - GATES_OFF) // 128)),
                  _full((1, 128)), _full((1, PIECE)), _full((1, PIECE)), _full((PIECE, PIECE)),
                  head_blk(128), head_blk(128), head_blk(HEAD_DIM)],
        out_specs=[out_piece, out_piece, out_piece, pl.BlockSpec((TM, 128), lambda i: (rev(i), 0)),
                   _full((1, HEAD_DIM)), _full((1, HEAD_DIM)), _full((1, 128))],
        out_shape=[jax.ShapeDtypeStruct((s_len, PIECE), BF16)] * 3 + [
            jax.ShapeDtypeStruct((s_len, 128), BF16), jax.ShapeDtypeStruct((1, HEAD_DIM), F32),
            jax.ShapeDtypeStruct((1, HEAD_DIM), F32), jax.ShapeDtypeStruct((1, 128), F32)],
        scratch_shapes=[pltpu.VMEM((1, 128), F32)],
        compiler_params=_params(("arbitrary",)),
    )(proj_a, proj_a, proj_b, bf_pad, gq, gk, seg, dq_aug, dk_aug, dv_h)


def _mem_bwd(mem, g, w_kv, gk, dkm, dvm):
    def body(mem_ref, g_ref, w_ref, gk_ref, dkm_ref, dvm_ref, dw_ref, dg_ref, dgk_ref, dkv_ref):
        m = mem_ref[...]
        r = _rms(m)
        a = m * r
        mn16 = (a * g_ref[...]).astype(BF16)
        kv = _dot(mn16, w_ref[...])
        dgk = jnp.zeros((N_MEM, HEAD_DIM), F32)
        for h in range(MEM_HEADS):
            sl = slice(HEAD_DIM * h, HEAD_DIM * (h + 1))
            kh = kv[:, sl]
            rk = _rms(kh)
            dk, dg_rows = _rms_bwd(kh * rk, rk, gk_ref[...], dkm_ref[:, sl])
            dkv_ref[:, sl] = dk.astype(BF16)
            dgk = dgk + dg_rows
        dkv_ref[:, 256:512] = dvm_ref[...].astype(BF16)
        dgk_ref[...] = jnp.sum(dgk, axis=0, keepdims=True)
        dkv16 = dkv_ref[...]
        dw_ref[...] = _dot_tn(mn16, dkv16).astype(BF16)
        dg_ref[...] = jnp.sum(_dot_nt(dkv16, w_ref[...]) * a, axis=0, keepdims=True)

    return pl.pallas_call(
        body,
        name="mem_bwd",
        out_shape=(jax.ShapeDtypeStruct((D_MODEL, 512), BF16), jax.ShapeDtypeStruct((1, D_MODEL), F32),
                   jax.ShapeDtypeStruct((1, HEAD_DIM), F32)),
        in_specs=[pl.BlockSpec(memory_space=pltpu.VMEM)] * 6,
        out_specs=(pl.BlockSpec(memory_space=pltpu.VMEM),) * 3,
        scratch_shapes=[pltpu.VMEM((N_MEM, 512), BF16)],
        compiler_params=pltpu.CompilerParams(vmem_limit_bytes=VMEM_LIMIT),
    )(mem, g, w_kv, gk, dkm, dvm)


def _grad_x_exchange(pieces, d_f, w_my, x, norm_g, d_out, dw_in, land_kv, land_out, d_mem_g, d_scale, d_bf, d_gq, d_gk, d_gqm,
                     d_gkm, dwp_bd, loss_blk):
    s_len = x.shape[0]
    n_steps = s_len // TM
    step2, step3 = n_steps // 8, (11 * n_steps) // 16
    half = D_MODEL // 2

    def body(p0, p1, p2, p3, p4, p5, df_ref, x_ref, dout_ref, w_ref, g_ref, in_ref, lkv_ref, lout_ref,
             dmg, dsc, dbf, dgq, dgk, dgqm, dgkm, dwp, loss_ref,
             gx_ref, rin_ref, rkv_ref, rout_ref, sred_ref,
             dng, land1, send2a, send2b, keep2a, keep2b, land2a, land2b, send3a, send3b, land3a, land3b, sb_ref, sland,
             own4, lkv_v, lout_v, send_sems, recv_sems, load_sems):
        i = pl.program_id(0)
        x_, y_, c_, me = _my_place()
        sibling, x_nbr, y_nbr = (x_, y_, 1 - c_), (1 - x_, y_, c_), (x_, 1 - y_, c_)
        loads = [pltpu.make_async_copy(in_ref.at[2 * k + c_], own4.at[k], load_sems.at[k]) for k in range(4)]
        loads += [pltpu.make_async_copy(lkv_ref, lkv_v, load_sems.at[4]), pltpu.make_async_copy(lout_ref, lout_v, load_sems.at[5])]

        def rcopy(src, dst, sem, to):
            return pltpu.make_async_remote_copy(src_ref=src, dst_ref=dst, send_sem=send_sems.at[sem], recv_sem=recv_sems.at[sem],
                                                device_id=to, device_id_type=MESH)

        early_rows, late_rows = pl.ds(8, SB_ROWS - 8), pl.ds(0, 8)
        small_early, small_late = [], []
        for k in range(1, N_DEV):
            peer = (x_ ^ (k >> 2), y_ ^ ((k >> 1) & 1), c_ ^ (k & 1))
            small_early.append(rcopy(sb_ref.at[early_rows], sland.at[me, early_rows], k - 1, peer))
            small_late.append(rcopy(sb_ref.at[late_rows], sland.at[me, late_rows], 16 + k, peer))
        small = small_early + small_late
        stage1 = [rcopy(in_ref.at[2 * k + 1 - c_], land1.at[k], 7 + k, sibling) for k in range(4)]
        stage2 = [rcopy(send2a.at[j], land2a.at[j], 11 + j, x_nbr) for j in range(2)]
        stage2 += [rcopy(send2b.at[j], land2b.at[j], 13 + j, y_nbr) for j in range(2)]
        stage3 = [rcopy(send3a, land3a, 15, y_nbr), rcopy(send3b, land3b, 16, x_nbr)]
        top, bot = slice(0, half), slice(half, D_MODEL)

        @pl.when(i == 0)
        def _():
            dng[...] = jnp.zeros((1, D_MODEL), F32)
            for cp in stage1 + loads:
                cp.start()
            sb_ref[...] = jnp.zeros((SB_ROWS, SB_W), F32)
            sb_ref[SB_POOL_SCALE:SB_POOL_SCALE + 1, :] = dsc[...]
            sb_ref[SB_B_F:SB_B_F + 1, 0:128] = dbf[...]
            for row, ref in ((SB_FOX_Q, dgq), (SB_FOX_K, dgk), (SB_MEM_Q, dgqm), (SB_MEM_K, dgkm)):
                sb_ref[row:row + 1, 0:HEAD_DIM] = ref[...]
            sb_ref[SB_LOSS:SB_LOSS + 1, 0:128] = loss_ref[0:1, :]
            for grp in range(4):
                sl = slice(64 * grp, 64 * (grp + 1))
                sb_ref[SB_W_POOL:SB_W_POOL + 64, sl] = dwp[sl, sl]
            for cp in small_early:
                cp.start()

        @pl.when(i == step2)
        def _():
            for cp in stage1:
                cp.wait_recv()
            for cp in loads[0:4]:
                cp.wait()

            def chip_sum(k, cols):
                return own4[k, :, cols].astype(F32) + land1[k, :, cols].astype(F32)

            for j in range(2):
                send2a[j] = chip_sum(2 * (1 - x_) + j, top).astype(BF16)
                keep2a[j] = chip_sum(2 * x_ + j, top)
                send2b[j] = chip_sum(2 * j + 1 - y_, bot).astype(BF16)
                keep2b[j] = chip_sum(2 * j + y_, bot)
            for cp in stage2:
                cp.start()

        @pl.when(i == step3)
        def _():
            for cp in stage2:
                cp.wait_recv()
            for j in range(2):
                keep2a[j] = keep2a[j] + land2a[j].astype(F32)
                keep2b[j] = keep2b[j] + land2b[j].astype(F32)
            send3a[...] = keep2a[1 - y_].astype(BF16)
            send3b[...] = keep2b[1 - x_].astype(BF16)
            for cp in stage3:
                cp.start()

        dh = _dot_nt(df_ref[...], w_ref[:, MY_F_OFF:MY_WIDTH])
        for j, p in enumerate((p0, p1, p2, p3, p4, p5)):
            dh = dh + _dot_nt(p[...], w_ref[:, PIECE * j:PIECE * (j + 1)])
        xv = x_ref[...]
        r = _rms(xv)
        dx, dg_rows = _rms_bwd(xv * r, r, g_ref[...], dh)
        gx_ref[...] = dout_ref[...] + dx
        dng[...] += jnp.sum(dg_rows, axis=0, keepdims=True)

        @pl.when(i == n_steps - 1)
        def _():
            for k in range(4):
                sb_ref[SB_NORM_G + k:SB_NORM_G + k + 1, :] = dng[:, SB_W * k:SB_W * (k + 1)]
                sb_ref[SB_MEM_NORM_G + k:SB_MEM_NORM_G + k + 1, :] = dmg[:, SB_W * k:SB_W * (k + 1)]
            for cp in small_late:
                cp.start()
            sland[me] = sb_ref[...]
            for cp in stage3:
                cp.wait_recv()
            rin_ref[:, top] = keep2a[y_] + land3a[...].astype(F32)
            rin_ref[:, bot] = keep2b[x_] + land3b[...].astype(F32)
            for cp in small:
                cp.wait_recv()
            for cp in small + stage1 + stage2 + stage3:
                cp.wait_send()
            for cp in loads[4:6]:
                cp.wait()
            for land, red in ((lkv_v, rkv_ref), (lout_v, rout_ref), (sland, sred_ref)):
                acc = land[0].astype(F32)
                for d in range(1, N_DEV):
                    acc = acc + land[d].astype(F32)
                red[...] = acc

    piece = pl.BlockSpec((TM, PIECE), lambda i: (i, 0))
    row = pl.BlockSpec((TM, D_MODEL), lambda i: (i, 0))
    vmem = pl.BlockSpec(memory_space=pltpu.VMEM)
    half_chunk = lambda n, dt: pltpu.VMEM((n, CHUNK_ROWS, half), dt)
    whole = (w_my, norm_g, dw_in, land_kv, land_out, d_mem_g, d_scale, d_bf, d_gq, d_gk, d_gqm, d_gkm, dwp_bd, loss_blk)
    return pl.pallas_call(
        body,
        name="grad_x_exchange",
        grid=(n_steps,),
        in_specs=[piece] * 6 + [pl.BlockSpec((TM, 128), lambda i: (i, 0)), row, row, vmem, vmem]
        + [pl.BlockSpec(memory_space=pl.ANY)] * 3 + [vmem] * (len(whole) - 5),
        out_specs=[row, vmem, vmem, vmem, vmem],
        out_shape=[jax.ShapeDtypeStruct((s_len, D_MODEL), F32), jax.ShapeDtypeStruct((CHUNK_ROWS, D_MODEL), F32),
                   jax.ShapeDtypeStruct((128, 512), F32), jax.ShapeDtypeStruct((128, D_MODEL), F32),
                   jax.ShapeDtypeStruct((SB_ROWS, SB_W), F32)],
        scratch_shapes=[
            pltpu.VMEM((1, D_MODEL), F32),
            pltpu.VMEM((4, CHUNK_ROWS, D_MODEL), BF16),
            half_chunk(2, BF16), half_chunk(2, BF16), half_chunk(2, F32), half_chunk(2, F32), half_chunk(2, BF16), half_chunk(2, BF16),
            pltpu.VMEM((CHUNK_ROWS, half), BF16), pltpu.VMEM((CHUNK_ROWS, half), BF16),
            pltpu.VMEM((CHUNK_ROWS, half), BF16), pltpu.VMEM((CHUNK_ROWS, half), BF16),
            pltpu.VMEM((SB_ROWS, SB_W), F32),
            pltpu.VMEM((N_DEV, SB_ROWS, SB_W), F32),
            pltpu.VMEM((4, CHUNK_ROWS, D_MODEL), BF16),
            pltpu.VMEM((N_DEV, 128, 512), BF16),
            pltpu.VMEM((N_DEV, 128, D_MODEL), BF16),
            pltpu.SemaphoreType.DMA((24,)),
            pltpu.SemaphoreType.DMA((24,)),
            pltpu.SemaphoreType.DMA((6,)),
        ],
        compiler_params=_params(("arbitrary",)),
    )(*pieces, d_f, x, d_out, *whole)


def _grad_w_in(h16, pieces, d_f):
    s_len = h16.shape[0]
    tk = 512

    def body(h_ref, p0, p1, p2, p3, p4, p5, df_ref, out_ref, dw_ref):
        @pl.when(pl.program_id(0) == 0)
        def _():
            dw_ref[...] = jnp.zeros((D_MODEL, MY_WIDTH), F32)

        h = h_ref[...]
        for j, p in enumerate((p0, p1, p2, p3, p4, p5)):
            dw_ref[:, PIECE * j:PIECE * (j + 1)] += _dot_tn(h, p[...])
        dw_ref[:, MY_F_OFF:MY_WIDTH] += _dot_tn(h, df_ref[...])

        @pl.when(pl.program_id(0) == pl.num_programs(0) - 1)
        def _():
            for d in range(N_DEV):
                parts = [dw_ref[:, start:start + width] for start, width in _chunk_segments(d)]
                parts.append(jnp.zeros((D_MODEL, 512 - IN_CHUNK), F32))
                out_ref[d] = jnp.concatenate(parts, axis=1).T[0:CHUNK_ROWS].astype(BF16)

    piece = pl.BlockSpec((tk, PIECE), lambda i: (i, 0))
    return pl.pallas_call(
        body,
        name="grad_w_in",
        grid=(s_len // tk,),
        in_specs=[pl.BlockSpec((tk, D_MODEL), lambda i: (i, 0))] + [piece] * 6 + [pl.BlockSpec((tk, 128), lambda i: (i, 0))],
        out_specs=_full((N_DEV, CHUNK_ROWS, D_MODEL)),
        out_shape=jax.ShapeDtypeStruct((N_DEV, CHUNK_ROWS, D_MODEL), BF16),
        scratch_shapes=[pltpu.VMEM((D_MODEL, MY_WIDTH), F32)],
        compiler_params=_params(("arbitrary",)),
    )(h16, *pieces, d_f)


def _adamw(w, g, m, v):
    m = ADAM_B1 * m + (1.0 - ADAM_B1) * g
    v = ADAM_B2 * v + (1.0 - ADAM_B2) * (g * g)
    m_hat = m / (1.0 - ADAM_B1 ** ADAM_STEP)
    v_hat = v / (1.0 - ADAM_B2 ** ADAM_STEP)
    return -ADAM_LR * (m_hat / (jnp.sqrt(v_hat) + ADAM_EPS) + ADAM_WD * w), m, v


PARAM_ORDER = ("norm_g", "w_in", "b_f", "w_pool", "pool_scale", "fox_q_g", "fox_k_g", "mem_norm_g", "w_mem_kv", "mem_q_g", "mem_k_g", "w_out")


def _update(red_in, red_kv, red_out, sred, params):
    def body(rin_ref, rkv_ref, rout_ref, sred_ref, *refs):
        ins, outs = refs[:3 * len(PARAM_ORDER)], refs[3 * len(PARAM_ORDER):]
        wide = lambda row: jnp.concatenate([sred_ref[row + k:row + k + 1, :] for k in range(4)], axis=1)
        head = lambda row: sred_ref[row:row + 1, 0:HEAD_DIM]
        grads = {
            "norm_g": lambda: wide(SB_NORM_G), "w_in": lambda: rin_ref[...], "b_f": lambda: sred_ref[SB_B_F:SB_B_F + 1, 0:FOX_HEADS],
            "pool_scale": lambda: sred_ref[SB_POOL_SCALE:SB_POOL_SCALE + 1, :], "fox_q_g": lambda: head(SB_FOX_Q),
            "fox_k_g": lambda: head(SB_FOX_K), "mem_norm_g": lambda: wide(SB_MEM_NORM_G), "w_mem_kv": lambda: rkv_ref[...],
            "mem_q_g": lambda: head(SB_MEM_Q), "mem_k_g": lambda: head(SB_MEM_K), "w_out": lambda: rout_ref[...],
        }
        for p, name in enumerate(PARAM_ORDER):
            w_ref, m_ref, v_ref = ins[3 * p:3 * p + 3]
            g_out, d_out, m_out, v_out = outs[4 * p:4 * p + 4]
            if name == "w_pool":
                for grp in range(4):
                    g = sred_ref[SB_W_POOL:SB_W_POOL + 64, 64 * grp:64 * (grp + 1)]
                    delta, m_new, v_new = _adamw(w_ref[grp], g, m_ref[grp], v_ref[grp])
                    g_out[grp], d_out[grp], m_out[grp], v_out[grp] = g, delta, m_new, v_new
            elif name == "w_in":
                for j in range(8):
                    rows = pl.ds(j, IN_CHUNK, stride=8)
                    g = rin_ref[0:IN_CHUNK, 128 * j:128 * (j + 1)]
                    delta, m_new, v_new = _adamw(w_ref[rows, :], g, m_ref[rows, :], v_ref[rows, :])
                    g_out[rows, :], d_out[rows, :], m_out[rows, :], v_out[rows, :] = g, delta, m_new, v_new
            else:
                g = grads[name]()
                delta, m_new, v_new = _adamw(w_ref[...], g, m_ref[...], v_ref[...])
                g_out[...], d_out[...], m_out[...], v_out[...] = g, delta, m_new, v_new

    flat = [t for name in PARAM_ORDER for t in params[name]]
    shapes = [params[name][0].shape for name in PARAM_ORDER for _ in range(4)]
    outs = pl.pallas_call(
        body,
        name="adamw_update",
        out_shape=tuple(jax.ShapeDtypeStruct(s, F32) for s in shapes),
        in_specs=[pl.BlockSpec(memory_space=pltpu.VMEM)] * (4 + len(flat)),
        out_specs=(pl.BlockSpec(memory_space=pltpu.VMEM),) * len(shapes),
        compiler_params=pltpu.CompilerParams(vmem_limit_bytes=VMEM_LIMIT),
    )(red_in, red_kv, red_out, sred, *flat)
    return {name: outs[4 * p:4 * p + 4] for p, name in enumerate(PARAM_ORDER)}


def kernel(x, mem, norm_g, w_in, b_f, w_pool, pool_scale, fox_q_g, fox_k_g, mem_norm_g, w_mem_kv, mem_q_g, mem_k_g, w_out, loss_target, m_norm_g, m_w_in, m_b_f, m_w_pool, m_pool_scale, m_fox_q_g, m_fox_k_g, m_mem_norm_g, m_w_mem_kv, m_mem_q_g, m_mem_k_g, m_w_out, v_norm_g, v_w_in, v_b_f, v_w_pool, v_pool_scale, v_fox_q_g, v_fox_k_g, v_mem_norm_g, v_w_mem_kv, v_mem_q_g, v_mem_k_g, v_w_out):
    given = dict(norm_g=(norm_g, m_norm_g, v_norm_g), w_in=(w_in, m_w_in, v_w_in), b_f=(b_f, m_b_f, v_b_f),
                 w_pool=(w_pool, m_w_pool, v_w_pool), pool_scale=(pool_scale, m_pool_scale, v_pool_scale),
                 fox_q_g=(fox_q_g, m_fox_q_g, v_fox_q_g), fox_k_g=(fox_k_g, m_fox_k_g, v_fox_k_g),
                 mem_norm_g=(mem_norm_g, m_mem_norm_g, v_mem_norm_g), w_mem_kv=(w_mem_kv, m_w_mem_kv, v_w_mem_kv),
                 mem_q_g=(mem_q_g, m_mem_q_g, v_mem_q_g), mem_k_g=(mem_k_g, m_mem_k_g, v_mem_k_g), w_out=(w_out, m_w_out, v_w_out))
    params = {name: tuple(t[0] if t.ndim > 2 else t for t in wmv) for name, wmv in given.items()}
    params["w_in"] = tuple(t.T.reshape(IN_CHUNK * 8, 128) for t in params["w_in"])
    x2, mem2, target2 = x[0], mem[0], loss_target[0]

    seg = jnp.asarray(np.kron(np.eye(FOX_HEADS), np.full((HEAD_DIM, HEAD_DIM), 1.0 / HEAD_DIM)), BF16)
    w_my, w_kv16, w_out16, wp_bd, bf_pad, gq8, gk8, gqm4 = _gather_w_in(
        params["w_in"][0], params["w_mem_kv"][0], params["w_out"][0], params["w_pool"][0], b_f, fox_q_g, fox_k_g, mem_q_g)
    proj_a, proj_b, h16, q_aug, k_aug, v_h, kt_aug, vt_aug = _proj_prep(x2, norm_g, w_my, bf_pad, gq8, gk8, seg)
    olse, lse_rows, w_kv_all, w_out_all = _fox_fwd(q_aug, k_aug, vt_aug, w_kv16, w_out16)
    km, vm = _mem_prep(mem2, mem_norm_g, w_kv_all, mem_k_g)
    d_out, d_mixed, dw_out, loss_blk = _mix_out_loss(proj_a, proj_b, olse, km, vm, wp_bd, pool_scale, gqm4, seg, x2, target2, w_out_all)

    d_ua, d_gb, d_qm, d_o, delta_rows, dwp_bd, d_scale, dkm, dvm, d_gqm = _mix_bwd(proj_a, proj_b, olse, d_mixed, km, vm, wp_bd,
                                                                                    pool_scale, gqm4, seg)
    dw_kv, d_mem_g, d_gkm = _mem_bwd(mem2, mem_norm_g, w_kv_all, mem_k_g, dkm, dvm)
    dq_aug, dk_aug, dv_h, land_kv, land_out = _fox_bwd(q_aug, k_aug, kt_aug, v_h, d_o, lse_rows, delta_rows, dw_kv, dw_out)
    d_q, d_k, d_v, d_f, d_gq, d_gk, d_bf = _fox_post(proj_a, proj_b, bf_pad, gq8, gk8, seg, dq_aug, dk_aug, dv_h)
    pieces = (d_ua, d_q, d_k, d_v, d_gb, d_qm)
    dw_in = _grad_w_in(h16, pieces, d_f)
    grad_x, red_in, red_kv, red_out, sred = _grad_x_exchange(pieces, d_f, w_my, x2, norm_g, d_out, dw_in, land_kv, land_out, d_mem_g,
                                                             d_scale, d_bf, d_gq, d_gk, d_gqm, d_gkm, dwp_bd, loss_blk)
    new = _update(red_in, red_kv, red_out, sred, params)
    result = [sred[SB_LOSS, 0], grad_x[None]]
    for kind in range(4):
        for name in PARAM_ORDER:
            out = new[name][kind]
            if name == "w_in":
                out = out.reshape(IN_CHUNK, D_MODEL).T
            result.append(out[None] if given[name][0].ndim > 2 else out)
    return tuple(result)
```

```python
import functools

import jax
import jax.numpy as jnp
import numpy as np
from jax import lax
from jax.experimental import pallas as pl
from jax.experimental.pallas import tpu as pltpu

F32 = jnp.float32
BF16 = jnp.bfloat16
MESH = pl.DeviceIdType.MESH

N_DEV = 8
D_MODEL = 1024
HEAD_DIM = 64
FOX_HEADS = 8
MEM_HEADS = 4
N_MEM = 256
POOL_WIDTH = 256
EPS = 1e-6
IN_WIDTH = 3080
IN_CHUNK = IN_WIDTH // N_DEV
CHUNK_ROWS = 400
F_OFF = 2048
MY_WIDTH = 3200
MY_F_OFF = 3072
PIECE = 512
GATES_OFF = 4 * PIECE
AUG_KEPT = 80
TM = 256
TMM = 512
TA = 256
HB = 8
HB_FWD = 8
AUG_ROWS = 72
HALO = 16
VMEM_LIMIT = 56 * 1024 * 1024

ADAM_LR = 0.001
ADAM_B1 = 0.9
ADAM_B2 = 0.999
ADAM_EPS = 1e-08
ADAM_WD = 0.01
ADAM_STEP = 10


def _dot(a, b):
    return jnp.dot(a, b, preferred_element_type=F32)


def _dot_nt(a, b):
    return lax.dot_general(a, b, (((1,), (1,)), ((), ())), preferred_element_type=F32)


def _dot_tn(a, b):
    return lax.dot_general(a, b, (((0,), (0,)), ((), ())), preferred_element_type=F32)


def _sigmoid(g):
    return 0.5 + 0.5 * jnp.tanh(0.5 * g)


def _split3(v):
    hi = v.astype(BF16)
    r1 = v - hi.astype(F32)
    mid = r1.astype(BF16)
    lo = (r1 - mid.astype(F32)).astype(BF16)
    return hi, mid, lo


def _rms(v):
    return lax.rsqrt(jnp.mean(v * v, axis=-1, keepdims=True) + EPS)


def _rms_bwd(a, r, g, dy):
    da = dy * g
    return r * (da - a * jnp.mean(da * a, axis=-1, keepdims=True)), dy * a


def _head_mean(z, seg):
    hi = z.astype(BF16)
    lo = (z - hi.astype(F32)).astype(BF16)
    if z.shape[1] == 256:
        return _dot(hi, seg) + _dot(lo, seg)
    half = seg[0:256, 0:256]
    return jnp.concatenate([_dot(hi[:, 0:256], half) + _dot(lo[:, 0:256], half),
                            _dot(hi[:, 256:512], half) + _dot(lo[:, 256:512], half)], axis=1)


def _head_rms(v, seg):
    return lax.rsqrt(_head_mean(v * v, seg) + EPS)


def _head_rms_bwd(a, r, g, dy, seg):
    da = dy * g
    return r * (da - a * _head_mean(da * a, seg)), dy * a


def _fold_heads(row, n_heads):
    out = row[:, 0:HEAD_DIM]
    for h in range(1, n_heads):
        out = out + row[:, HEAD_DIM * h:HEAD_DIM * (h + 1)]
    return out


def _params(sem, limit=VMEM_LIMIT):
    return pltpu.CompilerParams(dimension_semantics=sem, vmem_limit_bytes=limit)


def _full(shape):
    return pl.BlockSpec(shape, lambda *_: (0,) * len(shape))


def _chunk_segments(d):
    runs = []
    for lo, hi, shift in ((0, F_OFF, 0), (F_OFF, F_OFF + FOX_HEADS, MY_F_OFF - F_OFF), (F_OFF + FOX_HEADS, IN_WIDTH, -FOX_HEADS)):
        a, b = max(lo, IN_CHUNK * d), min(hi, IN_CHUNK * (d + 1))
        if a < b:
            runs.append((a + shift, b - a))
    return runs


def _my_place():
    x, y, c = lax.axis_index("x"), lax.axis_index("y"), lax.axis_index("c")
    return x, y, c, 4 * x + 2 * y + c


def _shard_rows(dev):
    return pl.ds(pl.multiple_of(128 * dev, 128), 128)


def _gather_w_in(w_in, w_kv, w_out, w_pool, b_f, gq, gk, gqm):
    def body(win_ref, wkv_ref, wout_ref, wp_ref, bf_ref, gq_ref, gk_ref, gqm_ref, wmy_ref, kv16_ref, out16_ref, wpbd_ref, bfpad_ref,
             gq8_ref, gk8_ref, gqm4_ref, in_all, pay_in, win_rows, send_sems, recv_sems, load_sem):
        x, y, c, me = _my_place()
        here, sibling = (x, y, c), (x, y, 1 - c)
        x_chip, y_chip, far_chip = (1 - x, y), (x, 1 - y), (1 - x, 1 - y)
        relay_from = (x ^ (1 - c), y ^ c)
        relay_to = (x ^ c, y ^ (1 - c))

        load = pltpu.make_async_copy(win_ref, win_rows, load_sem)
        load.start()
        load.wait()
        pay_in[...] = jnp.zeros((CHUNK_ROWS, D_MODEL), BF16)
        for j in range(8):
            pay_in[0:IN_CHUNK, 128 * j:128 * (j + 1)] = win_rows[pl.ds(j, IN_CHUNK, stride=8), :].astype(BF16)

        def copy(k, block, to, own=False):
            px, py, pc = block
            dst = in_all.at[4 * px + 2 * py + pc]
            return pltpu.make_async_remote_copy(src_ref=pay_in if own else dst, dst_ref=dst, send_sem=send_sems.at[k],
                                                recv_sem=recv_sems.at[k], device_id=to, device_id_type=MESH)

        first = [copy(0, here, sibling, own=True), copy(1, here, (*x_chip, c), own=True), copy(2, here, (*y_chip, c), own=True)]
        for cp in first:
            cp.start()
        in_all[me] = pay_in[...]
        kv16_ref[...] = wkv_ref[...].astype(BF16)
        out16_ref[...] = wout_ref[...].astype(BF16)
        wpbd_ref[...] = jnp.zeros((POOL_WIDTH, POOL_WIDTH), BF16)
        for grp in range(4):
            sl = slice(64 * grp, 64 * (grp + 1))
            wpbd_ref[sl, sl] = wp_ref[grp].astype(BF16)
        bfpad_ref[...] = jnp.zeros((1, 128), F32)
        bfpad_ref[:, 0:FOX_HEADS] = bf_ref[...]
        gq8_ref[...] = jnp.concatenate([gq_ref[...]] * FOX_HEADS, axis=1)
        gk8_ref[...] = jnp.concatenate([gk_ref[...]] * FOX_HEADS, axis=1)
        gqm4_ref[...] = jnp.concatenate([gqm_ref[...]] * MEM_HEADS, axis=1)
        copy(1 + c, (*relay_from, c), here).wait_recv()
        passed = [copy(3, (*relay_from, c), (*relay_to, c)), copy(4, (*relay_from, c), sibling)]
        for cp in passed:
            cp.start()
        copy(2 - c, (*relay_to, c), here).wait_recv()
        passed.append(copy(5, (*relay_to, c), sibling))
        passed[-1].start()
        copy(3, (*far_chip, c), here).wait_recv()
        passed.append(copy(6, (*far_chip, c), sibling))
        passed[-1].start()
        copy(0, sibling, here).wait_recv()
        for k, chip in ((4, relay_to), (5, relay_from), (6, far_chip)):
            copy(k, (*chip, 1 - c), here).wait_recv()
        for cp in first + passed:
            cp.wait_send()

        row_pad = jnp.zeros((512 - CHUNK_ROWS, 256), F32)
        for r0 in range(0, D_MODEL, 256):
            ch = [jnp.concatenate([in_all[d, :, r0:r0 + 256].astype(F32), row_pad], axis=0).T[:, 0:IN_CHUNK] for d in range(N_DEV)]
            lo = F_OFF - 5 * IN_CHUNK
            full = jnp.concatenate(ch[:5] + [ch[5][:, :lo], ch[5][:, lo + FOX_HEADS:], ch[6], ch[7], ch[5][:, lo:lo + FOX_HEADS],
                                             jnp.zeros((256, MY_WIDTH - IN_WIDTH), F32)], axis=1)
            wmy_ref[r0:r0 + 256, :] = full.astype(BF16)

    return pl.pallas_call(
        body,
        name="gather_w_in",
        out_shape=(jax.ShapeDtypeStruct((D_MODEL, MY_WIDTH), BF16), jax.ShapeDtypeStruct((128, 512), BF16),
                   jax.ShapeDtypeStruct((128, D_MODEL), BF16), jax.ShapeDtypeStruct((POOL_WIDTH, POOL_WIDTH), BF16),
                   jax.ShapeDtypeStruct((1, 128), F32), jax.ShapeDtypeStruct((1, PIECE), F32), jax.ShapeDtypeStruct((1, PIECE), F32),
                   jax.ShapeDtypeStruct((1, 256), F32)),
        in_specs=[pl.BlockSpec(memory_space=pl.ANY)] + [pl.BlockSpec(memory_space=pltpu.VMEM)] * 7,
        out_specs=(pl.BlockSpec(memory_space=pltpu.VMEM),) * 8,
        scratch_shapes=[
            pltpu.VMEM((N_DEV, CHUNK_ROWS, D_MODEL), BF16),
            pltpu.VMEM((CHUNK_ROWS, D_MODEL), BF16),
            pltpu.VMEM((IN_CHUNK * 8, 128), F32),
            pltpu.SemaphoreType.DMA((7,)),
            pltpu.SemaphoreType.DMA((7,)),
            pltpu.SemaphoreType.DMA,
        ],
        compiler_params=pltpu.CompilerParams(vmem_limit_bytes=VMEM_LIMIT),
    )(w_in, w_kv, w_out, w_pool, b_f, gq, gk, gqm)


def _row_block_exchange(kv_ref, out_ref, kv_dst, out_dst, send_sems, recv_sems, local_sems, gather):
    x, y, c, me = _my_place()
    remote = []
    for k in range(1, N_DEV):
        px, py, pc = x ^ (k >> 2), y ^ ((k >> 1) & 1), c ^ (k & 1)
        peer = 4 * px + 2 * py + pc
        for fam, (src, dst) in enumerate(((kv_ref, kv_dst), (out_ref, out_dst))):
            remote.append(pltpu.make_async_remote_copy(
                src_ref=src if gather else src.at[_shard_rows(peer)], dst_ref=dst.at[_shard_rows(me)] if gather else dst.at[me],
                send_sem=send_sems.at[7 * fam + k - 1], recv_sem=recv_sems.at[7 * fam + k - 1],
                device_id=(px, py, pc), device_id_type=MESH))
    local = [pltpu.make_async_copy(src if gather else src.at[_shard_rows(me)], dst.at[_shard_rows(me)] if gather else dst.at[me],
                                   local_sems.at[fam]) for fam, (src, dst) in enumerate(((kv_ref, kv_dst), (out_ref, out_dst)))]
    return remote, local


SB_ROWS, SB_W = 80, 256
SB_NORM_G, SB_MEM_NORM_G, SB_POOL_SCALE, SB_B_F, SB_FOX_Q, SB_FOX_K, SB_MEM_Q, SB_MEM_K, SB_LOSS, SB_W_POOL = 0, 4, 8, 9, 10, 11, 12, 13, 14, 16


def _alternate(*parts):
    state = [[part, 0, stages] for part, stages in parts]
    while state:
        least = min(state, key=lambda entry: entry[1] / entry[2])
        least[1] += 1
        if next(least[0], "done") == "done":
            state.remove(least)


def _log_sigmoid(z):
    return jnp.minimum(z, 0.0) - jnp.log(1.0 + jnp.exp(-jnp.abs(z)))


def _forget_lane_maps():
    to_q = np.zeros((128, FOX_HEADS * 128), np.float32)
    to_k = np.zeros((128, FOX_HEADS * 128), np.float32)
    for h in range(FOX_HEADS):
        for term in range(3):
            to_q[8 * term + h, 128 * h + 64 + term] = 1.0
            to_q[24, 128 * h + 67 + term] = 1.0
            to_k[24, 128 * h + 64 + term] = 1.0
            to_k[8 * term + h, 128 * h + 67 + term] = -1.0
    return jnp.asarray(to_q, BF16), jnp.asarray(to_k, BF16)


def _proj_prep(x, norm_g, w_my, bf_pad, gq, gk, seg):
    s_len = x.shape[0]
    n_blocks = s_len // TMM
    halves = TMM // TM
    to_q, to_k = _forget_lane_maps()
    q_off, f_off = PIECE, MY_F_OFF
    kept = 3 * PIECE + 128

    def body(x_hbm, g_ref, w_ref, bf_ref, gq_ref, gk_ref, seg_ref, eq_ref, ek_ref,
             proj_a_ref, proj_b_ref, h_ref, qa_ref, ka_ref, vh_ref, kt_ref, vt_ref, kept_even, kept_odd, carry_ref,
             x_ring, x_sems):
        s = pl.program_id(0)

        def fetch(block):
            return pltpu.make_async_copy(x_hbm.at[pl.ds(block * TMM, TMM), :], x_ring.at[block % 3], x_sems.at[block % 3])

        @pl.when(s == 0)
        def _():
            fetch(0).start()
            fetch(1).start()

        @pl.when(s + 2 < n_blocks)
        def _():
            fetch(s + 2).start()

        @pl.when(s < n_blocks)
        def _():
            fetch(s).wait()

        def project(kept_ref):
            xv = x_ring[s % 3]
            h_ref[...] = (xv * _rms(xv) * g_ref[...]).astype(BF16)
            yield
            for c0 in range(0, MY_WIDTH, 256):
                c1 = min(c0 + 256, MY_WIDTH)
                res = _dot(h_ref[...], w_ref[:, c0:c1])
                if c0 < 3 * PIECE:
                    proj_a_ref[:, c0:c1] = res
                if c0 >= GATES_OFF:
                    proj_b_ref[:, c0 - GATES_OFF:c1 - GATES_OFF] = res
                if q_off <= c0 < q_off + 3 * PIECE:
                    kept_ref[:, c0 - q_off:c1 - q_off] = res
                if c0 == f_off:
                    kept_ref[:, 3 * PIECE:kept] = res
                yield

        def operands(kept_ref):
            lane = lax.broadcasted_iota(jnp.int32, (TM, 128), 1)
            row = lax.broadcasted_iota(jnp.int32, (TM, TM), 0)
            col = lax.broadcasted_iota(jnp.int32, (TM, TM), 1)
            tri = jnp.where(col <= row, 1.0, 0.0).astype(BF16)
            one_at_64 = jnp.where(lane == 64, 1.0, 0.0).astype(BF16)
            zeros = jnp.zeros((TM, HEAD_DIM), BF16)
            for half in range(halves):
                rows = slice(TM * half, TM * (half + 1))
                logf = jnp.where(lane < FOX_HEADS, _log_sigmoid(kept_ref[rows, 3 * PIECE:kept] + bf_ref[...]), 0.0)
                hi, mid, lo = _split3(logf)
                cum = _dot(tri, hi) + _dot(tri, mid) + _dot(tri, lo) + carry_ref[...]
                q_all = kept_ref[rows, 0:PIECE]
                k_all = kept_ref[rows, PIECE:2 * PIECE]
                rq = _head_rms(q_all, seg_ref[...])
                rk = _head_rms(k_all, seg_ref[...])
                yield
                carry_ref[...] = cum[TM - 1:TM, :]
                f_hi, f_mid, f_lo = [t.astype(F32) for t in _split3(cum)]
                packed = (f_hi + pltpu.roll(f_mid, 8, 1) + pltpu.roll(f_lo, 16, 1)
                          + jnp.where(lane == 24, 1.0, 0.0)).astype(BF16)
                extra_q = _dot(packed, eq_ref[...]).astype(BF16)
                extra_k = _dot(packed, ek_ref[...]).astype(BF16)
                qn_all = (q_all * rq * gq_ref[...] * 0.125).astype(BF16)
                kn_all = (k_all * rk * gk_ref[...]).astype(BF16)
                v_all = kept_ref[rows, 2 * PIECE:3 * PIECE].astype(BF16)
                yield
                for h in range(FOX_HEADS):
                    sl = slice(HEAD_DIM * h, HEAD_DIM * (h + 1))
                    tile = slice(128 * h, 128 * (h + 1))
                    qa = jnp.where(lane < 64, jnp.concatenate([qn_all[:, sl], zeros], axis=1), extra_q[:, tile])
                    ka = jnp.where(lane < 64, jnp.concatenate([kn_all[:, sl], zeros], axis=1), extra_k[:, tile])
                    vh = v_all[:, sl]
                    v_aug = jnp.where(lane < 64, jnp.concatenate([vh, zeros], axis=1), one_at_64)
                    qa_ref[h, rows, :] = qa
                    ka_ref[h, rows, :] = ka
                    vh_ref[h, rows, :] = vh
                    kt_ref[h, half] = ka.T[0:AUG_KEPT]
                    vt_ref[h, half] = v_aug.T[0:AUG_KEPT]
                    if h % 2 == 1:
                        yield

        projecting = lambda kept_ref: (project(kept_ref), 1 + pl.cdiv(MY_WIDTH, 256))
        making = lambda kept_ref: (operands(kept_ref), halves * (2 + FOX_HEADS // 2))

        @pl.when(s == 0)
        def _():
            carry_ref[...] = jnp.zeros((1, 128), F32)
            _alternate(projecting(kept_even))

        @pl.when((s > 0) & (s < n_blocks) & (s % 2 == 1))
        def _():
            _alternate(projecting(kept_odd), making(kept_even))

        @pl.when((s > 0) & (s < n_blocks) & (s % 2 == 0))
        def _():
            _alternate(projecting(kept_even), making(kept_odd))

        @pl.when(s == n_blocks)
        def _():
            _alternate(making(kept_odd if n_blocks % 2 == 0 else kept_even))

    made = lambda s: jnp.minimum(s, n_blocks - 1)
    used = lambda s: jnp.maximum(s - 1, 0)
    head_blk = lambda w: pl.BlockSpec((FOX_HEADS, TMM, w), lambda s: (0, used(s), 0))
    tile_blk = pl.BlockSpec((FOX_HEADS, halves, AUG_KEPT, TM), lambda s: (0, used(s), 0, 0))
    tiled = jax.ShapeDtypeStruct((FOX_HEADS, s_len // TM, AUG_KEPT, TM), BF16)
    made_blk = lambda w: pl.BlockSpec((TMM, w), lambda s: (made(s), 0))
    return pl.pallas_call(
        body,
        name="proj_prep",
        grid=(n_blocks + 1,),
        in_specs=[pl.BlockSpec(memory_space=pl.ANY), _full((1, D_MODEL)), _full((D_MODEL, MY_WIDTH)),
                  _full((1, 128)), _full((1, PIECE)), _full((1, PIECE)), _full((PIECE, PIECE)),
                  _full((128, FOX_HEADS * 128)), _full((128, FOX_HEADS * 128))],
        out_specs=[made_blk(3 * PIECE), made_blk(MY_WIDTH - GATES_OFF), made_blk(D_MODEL),
                   head_blk(128), head_blk(128), head_blk(HEAD_DIM), tile_blk, tile_blk],
        out_shape=[jax.ShapeDtypeStruct((s_len, 3 * PIECE), F32), jax.ShapeDtypeStruct((s_len, MY_WIDTH - GATES_OFF), F32),
                   jax.ShapeDtypeStruct((s_len, D_MODEL), BF16),
                   jax.ShapeDtypeStruct((FOX_HEADS, s_len, 128), BF16), jax.ShapeDtypeStruct((FOX_HEADS, s_len, 128), BF16),
                   jax.ShapeDtypeStruct((FOX_HEADS, s_len, HEAD_DIM), BF16), tiled, tiled],
        scratch_shapes=[pltpu.VMEM((TMM, kept), F32), pltpu.VMEM((TMM, kept), F32), pltpu.VMEM((1, 128), F32),
                        pltpu.VMEM((3, TMM, D_MODEL), F32), pltpu.SemaphoreType.DMA((3,))],
        compiler_params=_params(("arbitrary",)),
    )(x, norm_g, w_my, bf_pad, gq, gk, seg, to_q, to_k)


def _mem_prep(mem, g, w_kv, gk):
    def body(mem_ref, g_ref, w_ref, gk_ref, km_ref, vm_ref):
        m = mem_ref[...]
        kv = _dot((m * _rms(m) * g_ref[...]).astype(BF16), w_ref[...])
        for h in range(MEM_HEADS):
            sl = slice(HEAD_DIM * h, HEAD_DIM * (h + 1))
            kh = kv[:, sl]
            km_ref[:, sl] = (kh * _rms(kh) * gk_ref[...]).astype(BF16)
        vm_ref[...] = kv[:, 256:512].astype(BF16)

    return pl.pallas_call(
        body,
        name="mem_prep",
        out_shape=(jax.ShapeDtypeStruct((N_MEM, 256), BF16),) * 2,
        in_specs=[pl.BlockSpec(memory_space=pltpu.VMEM)] * 4,
        out_specs=(pl.BlockSpec(memory_space=pltpu.VMEM),) * 2,
        compiler_params=pltpu.CompilerParams(vmem_limit_bytes=VMEM_LIMIT),
    )(mem, g, w_kv, gk)


def _causal_mask_t():
    row = lax.broadcasted_iota(jnp.int32, (TA, TA), 0)
    col = lax.broadcasted_iota(jnp.int32, (TA, TA), 1)
    return row <= col


def _fox_fwd(q_aug, k_aug, vt_aug, w_kv16, w_out16):
    s_len = q_aug.shape[1]
    n_tiles = s_len // TA
    hb = HB_FWD
    n_groups = FOX_HEADS // hb

    def body(q_ref, k_new, vt_new, kv16_ref, out16_ref, o_ref, st_ref, kv_all, out_all, k_ref, vt_ref, send_sems, recv_sems, local_sems):
        qi = pl.program_id(1)
        for h in range(hb):
            k_ref[h, pl.ds(pl.multiple_of(qi * TA, TA), TA), :] = k_new[h]
            vt_ref[h, qi] = vt_new[h, 0]
        remote, local = _row_block_exchange(kv16_ref, out16_ref, kv_all, out_all, send_sems, recv_sems, local_sems, gather=True)

        @pl.when((pl.program_id(0) == 0) & (qi == 0))
        def _():
            for cp in remote + local:
                cp.start()

        qs = [q_ref[h] for h in range(hb)]

        def step(t0, carry, masked, width):
            rows = pl.ds(pl.multiple_of(t0 * TA, TA), width * TA)
            scores = [_dot_nt(k_ref[h, rows, :], qs[h]) for h in range(hb)]
            soft = []
            for h in range(hb):
                m, _ = carry[h]
                s = jnp.where(_causal_mask_t(), scores[h], -1e30) if masked else scores[h]
                m_new = jnp.maximum(m, jnp.max(s, axis=0, keepdims=True))
                soft.append((m_new, jnp.exp(m - m_new), jnp.exp(s - m_new).astype(BF16)))
            out = []
            for h, (m_new, alpha, p) in enumerate(soft):
                acc = alpha * carry[h][1]
                for t in range(width):
                    acc = acc + _dot(vt_ref[h, t0 + t, 0:AUG_ROWS, :], p[t * TA:(t + 1) * TA])
                out.append((m_new, acc))
            return tuple(out)

        init = tuple((jnp.full((1, TA), -1e30, F32), jnp.zeros((AUG_ROWS, TA), F32)) for _ in range(hb))
        carry = lax.fori_loop(0, qi // 2, lambda j, cr: step(2 * j, cr, False, 2), init)
        carry = lax.fori_loop(2 * (qi // 2), qi, lambda j, cr: step(j, cr, False, 1), carry)
        carry = step(qi, carry, True, 1)
        for h in range(hb):
            m, acc = carry[h]
            l = acc[HEAD_DIM:HEAD_DIM + 1, :]
            lse = m + jnp.log(l)
            o_ref[h] = jnp.concatenate([acc[0:HEAD_DIM] / l, jnp.broadcast_to(lse, (HEAD_DIM, TA))], axis=0).T
            st_ref[h, 0] = jnp.broadcast_to(lse, (8, TA))

        @pl.when((pl.program_id(0) == n_groups - 1) & (qi == n_tiles - 1))
        def _():
            for cp in remote:
                cp.wait_recv()
            for cp in remote:
                cp.wait_send()
            for cp in local:
                cp.wait()

    hbm = pl.BlockSpec(memory_space=pl.ANY)
    return pl.pallas_call(
        body,
        name="fox_fwd",
        grid=(n_groups, n_tiles),
        in_specs=[pl.BlockSpec((hb, TA, 128), lambda g, i: (g, i, 0)), pl.BlockSpec((hb, TA, 128), lambda g, i: (g, i, 0)),
                  pl.BlockSpec((hb, 1, AUG_KEPT, TA), lambda g, i: (g, i, 0, 0)), hbm, hbm],
        out_specs=[pl.BlockSpec((hb, TA, 128), lambda g, i: (g, i, 0)), pl.BlockSpec((hb, 1, 8, TA), lambda g, i: (g, i, 0, 0)),
                   hbm, hbm],
        out_shape=[jax.ShapeDtypeStruct((FOX_HEADS, s_len, 128), F32), jax.ShapeDtypeStruct((FOX_HEADS, n_tiles, 8, TA), F32),
                   jax.ShapeDtypeStruct((D_MODEL, 512), BF16), jax.ShapeDtypeStruct((D_MODEL, D_MODEL), BF16)],
        scratch_shapes=[pltpu.VMEM((hb, s_len, 128), BF16), pltpu.VMEM((hb, n_tiles, AUG_KEPT, TA), BF16),
                        pltpu.SemaphoreType.DMA((14,)), pltpu.SemaphoreType.DMA((14,)), pltpu.SemaphoreType.DMA((2,))],
        compiler_params=_params(("arbitrary", "arbitrary")),
    )(q_aug, k_aug, vt_aug, w_kv16, w_out16)


def _lane_group(lane, a, b, c, d):
    return jnp.where(lane < 64, a, jnp.where(lane < 128, b, jnp.where(lane < 192, c, d)))


def _pool_count(row0):
    lane = lax.broadcasted_iota(jnp.int32, (TM, POOL_WIDTH), 1)
    t1 = row0 + lax.broadcasted_iota(jnp.int32, (TM, POOL_WIDTH), 0) + 1
    return 1.0 / jnp.minimum(t1, _lane_group(lane, 2, 4, 8, 16)).astype(F32), lane


def _pool_delta(u, halo, cnt, lane):
    xe = jnp.concatenate([halo, u], axis=0)
    s2 = xe + pltpu.roll(xe, 1, 0)
    s4 = s2 + pltpu.roll(s2, 2, 0)
    s8 = s4 + pltpu.roll(s4, 4, 0)
    s16 = s8 + pltpu.roll(s8, 8, 0)
    win = _lane_group(lane, s2[HALO:], s4[HALO:], s8[HALO:], s16[HALO:])
    return win * cnt - u


def _pool_delta_bwd(dd, dd_next, cnt, cnt_next, lane):
    ee = jnp.concatenate([dd * cnt, dd_next * cnt_next], axis=0)
    n = TM + HALO
    r2 = ee + pltpu.roll(ee, n - 1, 0)
    r4 = r2 + pltpu.roll(r2, n - 2, 0)
    r8 = r4 + pltpu.roll(r4, n - 4, 0)
    r16 = r8 + pltpu.roll(r8, n - 8, 0)
    return _lane_group(lane, r2[:TM], r4[:TM], r8[:TM], r16[:TM]) - dd


def _mem_attn(qm, gq, seg, km_ref, vm_ref):
    r = _head_rms(qm, seg)
    a = qm * r
    q16 = (a * gq * 0.125).astype(BF16)
    heads = []
    for h in range(MEM_HEADS):
        sl = slice(HEAD_DIM * h, HEAD_DIM * (h + 1))
        s = _dot_nt(q16[:, sl], km_ref[:, sl])
        e = jnp.exp(s - jnp.max(s, axis=-1, keepdims=True))
        p = e * (1.0 / jnp.sum(e, axis=-1, keepdims=True))
        heads.append((p, _dot(p.astype(BF16), vm_ref[:, sl])))
    return a, r, q16, heads


def _mem_attn_stages(qm, gq, seg, km_ref, vm_ref, result):
    r = _head_rms(qm, seg)
    a = qm * r
    q16 = (a * gq * 0.125).astype(BF16)
    yield
    head_slices = [slice(HEAD_DIM * h, HEAD_DIM * (h + 1)) for h in range(MEM_HEADS)]
    scores = [_dot_nt(q16[:, sl], km_ref[:, sl]) for sl in head_slices]
    yield
    heads = []
    for s, sl in zip(scores, head_slices):
        e = jnp.exp(s - jnp.max(s, axis=-1, keepdims=True))
        p = e * (1.0 / jnp.sum(e, axis=-1, keepdims=True))
        heads.append((p, _dot(p.astype(BF16), vm_ref[:, sl])))
    result.extend([a, r, q16, heads])
    yield


def _mix_out_loss(proj_a, proj_b, olse, km, vm, wp_bd, pool_scale, gq_m, seg, x, target, w_out):
    s_len = x.shape[0]
    n_blocks = s_len // TMM
    halves = TMM // TM

    def body(ua_ref, halo_ref, gb_ref, qm_ref, ol_ref, km_ref, vm_ref, wp_ref, sc_ref, gq_ref, seg_ref, x_ref, t_ref, w_ref,
             dout_ref, dmix_ref, dw16_ref, loss_ref, mixed_even, mixed_odd, d16_ref, dw_ref):
        s = pl.program_id(0)
        chunks = [slice(c0, c0 + 256) for c0 in range(0, D_MODEL, 256)]

        def matmuls(mixed_ref):
            for cols in chunks:
                err = x_ref[:, cols] + _dot(mixed_ref[...], w_ref[:, cols]) - t_ref[:, cols]
                loss_ref[...] += (0.5 / D_MODEL) * jnp.sum(err * err)
                d_out = err / float(D_MODEL)
                dout_ref[:, cols] = d_out
                d16_ref[:, cols] = d_out.astype(BF16)
                yield
            for cols in chunks:
                dmix_ref[:, cols] = _dot_nt(d16_ref[...], w_ref[cols, :])
                yield
            for cols in chunks:
                dw_ref[:, cols] += _dot_tn(mixed_ref[...], d16_ref[:, cols])
                yield

        def concatenation(mixed_ref):
            for half in range(halves):
                r0 = TM * half
                rows = slice(r0, r0 + TM)
                u = ua_ref[rows, 0:256]
                g_a = ua_ref[rows, 256:512]
                halo = jnp.where(s > 0, halo_ref[...], 0.0) if half == 0 else ua_ref[r0 - HALO:r0, 0:256]
                cnt, lane = _pool_count(s * TMM + r0)
                d = _pool_delta(u, halo, cnt, lane).astype(BF16)
                y_a = _dot(d, wp_ref[...]) * sc_ref[...]
                mixed_ref[rows, 0:256] = (y_a * (g_a * _sigmoid(g_a))).astype(BF16)
                yield
                g_b = gb_ref[rows, :]
                y_b = jnp.concatenate([ol_ref[h, rows, 0:HEAD_DIM] for h in range(FOX_HEADS)], axis=1)
                mixed_ref[rows, 256:768] = (y_b * (g_b * _sigmoid(g_b))).astype(BF16)
                yield
                attn = []
                stages = _mem_attn_stages(qm_ref[rows, 0:256], gq_ref[...], seg_ref[0:256, 0:256], km_ref, vm_ref, attn)
                next(stages)
                yield
                next(stages)
                yield
                next(stages)
                g_m = qm_ref[rows, 256:512]
                y_m = jnp.concatenate([y for _, y in attn[3]], axis=1)
                mixed_ref[rows, 768:1024] = (y_m * (g_m * _sigmoid(g_m))).astype(BF16)
                yield

        multiplying = lambda mixed_ref: (matmuls(mixed_ref), 3 * len(chunks))
        building = lambda mixed_ref: (concatenation(mixed_ref), 5 * halves)

        @pl.when(s == 0)
        def _():
            dw_ref[...] = jnp.zeros((D_MODEL, D_MODEL), F32)
            loss_ref[...] = jnp.zeros((8, 128), F32)
            _alternate(building(mixed_even))

        @pl.when((s > 0) & (s < n_blocks) & (s % 2 == 1))
        def _():
            _alternate(multiplying(mixed_even), building(mixed_odd))

        @pl.when((s > 0) & (s < n_blocks) & (s % 2 == 0))
        def _():
            _alternate(multiplying(mixed_odd), building(mixed_even))

        @pl.when(s == n_blocks)
        def _():
            _alternate(multiplying(mixed_odd if n_blocks % 2 == 0 else mixed_even))
            dw16_ref[...] = dw_ref[...].astype(BF16)

    build = lambda s: jnp.minimum(s, n_blocks - 1)
    use = lambda s: jnp.maximum(s - 1, 0)
    piece = lambda j: pl.BlockSpec((TMM, PIECE), lambda s: (build(s), j))
    halo_blk = pl.BlockSpec((HALO, 256), lambda s: (jnp.maximum(build(s) * (TMM // HALO) - 1, 0), 0))
    row = pl.BlockSpec((TMM, D_MODEL), lambda s: (use(s), 0))
    return pl.pallas_call(
        body,
        name="mix_out_loss",
        grid=(n_blocks + 1,),
        in_specs=[piece(0), halo_blk, piece(0), piece(1), pl.BlockSpec((FOX_HEADS, TMM, 128), lambda s: (0, build(s), 0)),
                  _full((N_MEM, 256)), _full((N_MEM, 256)), _full((256, 256)), _full((1, 256)), _full((1, 256)),
                  _full((PIECE, PIECE)), row, row, _full((D_MODEL, D_MODEL))],
        out_specs=[row, row, _full((D_MODEL, D_MODEL)), _full((8, 128))],
        out_shape=[jax.ShapeDtypeStruct((s_len, D_MODEL), F32), jax.ShapeDtypeStruct((s_len, D_MODEL), F32),
                   jax.ShapeDtypeStruct((D_MODEL, D_MODEL), BF16), jax.ShapeDtypeStruct((8, 128), F32)],
        scratch_shapes=[pltpu.VMEM((TMM, D_MODEL), BF16), pltpu.VMEM((TMM, D_MODEL), BF16), pltpu.VMEM((TMM, D_MODEL), BF16),
                        pltpu.VMEM((D_MODEL, D_MODEL), F32)],
        compiler_params=_params(("arbitrary",)),
    )(proj_a, proj_a, proj_b, proj_b, olse, km, vm, wp_bd, pool_scale, gq_m, seg, x, target, w_out)


def _silu_pair(g):
    s = _sigmoid(g)
    return g * s, s * (1.0 + g * (1.0 - s))


def _mix_bwd(proj_a, proj_b, olse, d_mixed, km, vm, wp_bd, pool_scale, gq_m, seg):
    s_len = proj_a.shape[0]
    n_tiles = s_len // TM

    def body(ua_ref, halo_ref, ga_next_ref, gb_ref, qm_ref, ol_ref, dm_ref, dm_next_ref, km_ref, vm_ref, wp_ref, sc_ref, gq_ref,
             seg_ref, dua_ref, dgb_ref, dqm_ref, do_ref, dl_ref, dwp_ref, dsc_ref, dkm_ref, dvm_ref, dgq_ref):
        i = pl.program_id(0)

        @pl.when(i == 0)
        def _():
            dwp_ref[...] = jnp.zeros((256, 256), F32)
            dsc_ref[...] = jnp.zeros((1, 256), F32)
            dkm_ref[...] = jnp.zeros((N_MEM, 256), F32)
            dvm_ref[...] = jnp.zeros((N_MEM, 256), F32)
            dgq_ref[...] = jnp.zeros((1, HEAD_DIM), F32)

        def pooling_mixer():
            u = ua_ref[:, 0:256]
            g_a = ua_ref[:, 256:512]
            halo = jnp.where(i > 0, halo_ref[...], 0.0)
            cnt, lane = _pool_count(i * TM)
            d16 = _pool_delta(u, halo, cnt, lane).astype(BF16)
            t = _dot(d16, wp_ref[...])
            yield
            y_a = t * sc_ref[...]
            silu_a, dsilu_a = _silu_pair(g_a)
            dm_a = dm_ref[:, 0:256]
            dy_a = dm_a * silu_a
            dsc_ref[...] += jnp.sum(dy_a * t, axis=0, keepdims=True)
            dt16 = (dy_a * sc_ref[...]).astype(BF16)
            dwp_ref[...] += _dot_tn(d16, dt16)
            dd = _dot_nt(dt16, wp_ref[...])
            dua_ref[:, 256:512] = (dm_a * y_a * dsilu_a).astype(BF16)
            yield
            g_next = ga_next_ref[...]
            dt_next = (dm_next_ref[...] * (g_next * _sigmoid(g_next)) * sc_ref[...]).astype(BF16)
            dd_next = jnp.where(i < n_tiles - 1, _dot_nt(dt_next, wp_ref[...]), 0.0)
            cnt_next = _pool_count((i + 1) * TM)[0][0:HALO]
            dua_ref[:, 0:256] = _pool_delta_bwd(dd, dd_next, cnt, cnt_next, lane).astype(BF16)
            yield

        def output_gate():
            silu_b, dsilu_b = _silu_pair(gb_ref[...])
            dm_b = dm_ref[:, 256:768]
            o_all = jnp.concatenate([ol_ref[h][:, 0:HEAD_DIM] for h in range(FOX_HEADS)], axis=1)
            d_o = dm_b * silu_b
            dgb_ref[...] = (dm_b * o_all * dsilu_b).astype(BF16)
            d_o16 = d_o.astype(BF16)
            for h in range(FOX_HEADS):
                do_ref[h] = d_o16[:, HEAD_DIM * h:HEAD_DIM * (h + 1)]
            yield
            prod = d_o * o_all
            hi = prod.astype(BF16)
            lo = (prod - hi.astype(F32)).astype(BF16)
            head_of_lane = lax.broadcasted_iota(jnp.int32, (FOX_HEADS, PIECE), 1) // HEAD_DIM
            pick = jnp.where(head_of_lane == lax.broadcasted_iota(jnp.int32, (FOX_HEADS, PIECE), 0), 1.0, 0.0).astype(BF16)
            delta = _dot_nt(pick, hi) + _dot_nt(pick, lo)
            for h in range(FOX_HEADS):
                dl_ref[h, 0] = jnp.broadcast_to(delta[h:h + 1, :], (8, TM))
            yield

        def memory_attention():
            seg = seg_ref[0:256, 0:256]
            a, r, q16, heads = _mem_attn(qm_ref[:, 0:256], gq_ref[...], seg, km_ref, vm_ref)
            yield
            silu_m, dsilu_m = _silu_pair(qm_ref[:, 256:512])
            dm_m = dm_ref[:, 768:1024]
            y_m = jnp.concatenate([y for _, y in heads], axis=1)
            dqm_ref[:, 256:512] = (dm_m * y_m * dsilu_m).astype(BF16)
            dy16_all = (dm_m * silu_m).astype(BF16)
            d_scores = []
            head_slices = [slice(HEAD_DIM * h, HEAD_DIM * (h + 1)) for h in range(MEM_HEADS)]
            dps = [_dot_nt(dy16_all[:, sl], vm_ref[:, sl]) for sl in head_slices]
            for h, sl in enumerate(head_slices):
                dvm_ref[:, sl] += _dot_tn(heads[h][0].astype(BF16), dy16_all[:, sl])
            yield
            for h, sl in enumerate(head_slices):
                p = heads[h][0]
                dp = dps[h]
                ds16 = (p * (dp - jnp.sum(dp * p, axis=-1, keepdims=True))).astype(BF16)
                dkm_ref[:, sl] += _dot_tn(ds16, q16[:, sl])
                d_scores.append(_dot(ds16, km_ref[:, sl]))
                yield
            dq, dg_rows = _head_rms_bwd(a, r, gq_ref[...], jnp.concatenate(d_scores, axis=1) * 0.125, seg)
            dqm_ref[:, 0:256] = dq.astype(BF16)
            dgq_ref[...] += _fold_heads(jnp.sum(dg_rows, axis=0, keepdims=True), MEM_HEADS)
            yield

        _alternate((pooling_mixer(), 3))
        _alternate((memory_attention(), 3 + MEM_HEADS), (output_gate(), 2))

    n_halo = s_len // HALO
    piece = lambda j: pl.BlockSpec((TM, PIECE), lambda i: (i, j))
    before = pl.BlockSpec((HALO, 256), lambda i: (jnp.maximum(i * (TM // HALO) - 1, 0), 0))
    after = lambda j: pl.BlockSpec((HALO, 256), lambda i: (jnp.minimum((i + 1) * (TM // HALO), n_halo - 1), j))
    row = pl.BlockSpec((TM, D_MODEL), lambda i: (i, 0))
    out_piece = pl.BlockSpec((TM, PIECE), lambda i: (i, 0))
    return pl.pallas_call(
        body,
        name="mix_bwd",
        grid=(n_tiles,),
        in_specs=[piece(0), before, after(1), piece(0), piece(1), pl.BlockSpec((FOX_HEADS, TM, 128), lambda i: (0, i, 0)),
                  row, after(0), _full((N_MEM, 256)), _full((N_MEM, 256)), _full((256, 256)), _full((1, 256)), _full((1, 256)),
                  _full((PIECE, PIECE))],
        out_specs=[out_piece, out_piece, out_piece, pl.BlockSpec((FOX_HEADS, TM, HEAD_DIM), lambda i: (0, i, 0)),
                   pl.BlockSpec((FOX_HEADS, 1, 8, TM), lambda i: (0, i, 0, 0)),
                   _full((256, 256)), _full((1, 256)), _full((N_MEM, 256)), _full((N_MEM, 256)), _full((1, HEAD_DIM))],
        out_shape=[jax.ShapeDtypeStruct((s_len, PIECE), BF16)] * 3 + [
            jax.ShapeDtypeStruct((FOX_HEADS, s_len, HEAD_DIM), BF16),
            jax.ShapeDtypeStruct((FOX_HEADS, n_tiles, 8, TM), F32),
            jax.ShapeDtypeStruct((256, 256), F32), jax.ShapeDtypeStruct((1, 256), F32),
            jax.ShapeDtypeStruct((N_MEM, 256), F32), jax.ShapeDtypeStruct((N_MEM, 256), F32),
            jax.ShapeDtypeStruct((1, HEAD_DIM), F32)],
        compiler_params=_params(("arbitrary",)),
    )(proj_a, proj_a, proj_a, proj_b, proj_b, olse, d_mixed, d_mixed, km, vm, wp_bd, pool_scale, gq_m, seg)


def _fox_bwd(q_aug, k_aug, kt_aug, v_h, d_o, lse_rows, delta_rows, dw_kv16, dw_out16):
    s_len = q_aug.shape[1]
    n_tiles = s_len // TA
    n_groups = FOX_HEADS // HB

    def body(q_hbm, k_ref, kt_ref, v_ref, do_hbm, st_ref, dl_ref, dkv16_ref, dout16_ref, dq_ref, dk_ref, dv_ref, land_kv, land_out,
             dqt_ref, q_ref, do_ref, send_sems, recv_sems, local_sems, tile_sems):
        j = pl.program_id(1)
        remote, local = _row_block_exchange(dkv16_ref, dout16_ref, land_kv, land_out, send_sems, recv_sems, local_sems, gather=False)

        def tile_loads(i):
            rows = pl.ds(pl.multiple_of(i * TA, TA), TA)
            return [pltpu.make_async_copy(q_hbm.at[:, rows, :], q_ref.at[:, rows, :], tile_sems.at[2 * i]),
                    pltpu.make_async_copy(do_hbm.at[:, rows, :], do_ref.at[:, rows, :], tile_sems.at[2 * i + 1])]

        @pl.when((pl.program_id(0) == 0) & (j == 0))
        def _():
            for i in range(n_tiles):
                for cp in tile_loads(i):
                    cp.start()
            for cp in remote + local:
                cp.start()

        @pl.when(j == 0)
        def _():
            dqt_ref[...] = jnp.zeros((HB, n_tiles, AUG_ROWS, TA), F32)

        ks = [k_ref[h] for h in range(HB)]
        kts = [kt_ref[h, 0, 0:AUG_ROWS, :] for h in range(HB)]
        vs = [v_ref[h] for h in range(HB)]

        def pair(i0, acc, masked, width):
            @pl.when(j == 0)
            def _():
                for t in range(width):
                    for cp in tile_loads(i0 + t):
                        cp.wait()

            rows = pl.ds(pl.multiple_of(i0 * TA, TA), width * TA)
            qs = [q_ref[h, rows, :] for h in range(HB)]
            d_os = [do_ref[h, rows, :] for h in range(HB)]
            row_stat = lambda ref, h: jnp.concatenate([ref[h, i0 + t, 0:1, :] for t in range(width)], axis=1)
            scores = [(_dot_nt(ks[h], qs[h]), _dot_nt(vs[h], d_os[h])) for h in range(HB)]
            probs = []
            for h in range(HB):
                s, dp = scores[h]
                if masked:
                    s = jnp.where(_causal_mask_t(), s, -1e30)
                p = jnp.exp(s - row_stat(st_ref, h))
                probs.append((p.astype(BF16), (p * (dp - row_stat(dl_ref, h))).astype(BF16)))
            out = []
            for h in range(HB):
                p16, ds16 = probs[h]
                dqt = _dot(kts[h], ds16)
                for t in range(width):
                    dqt_ref[h, i0 + t] += dqt[:, t * TA:(t + 1) * TA]
                out.append((acc[h][0] + _dot(ds16, qs[h]), acc[h][1] + _dot(p16, d_os[h])))
            return tuple(out)

        zero = tuple((jnp.zeros((TA, 128), F32), jnp.zeros((TA, HEAD_DIM), F32)) for _ in range(HB))
        acc = pair(j, zero, True, 1)
        odd = (n_tiles - 1 - j) % 2
        acc = lax.fori_loop(j + 1, j + 1 + odd, lambda i, a: pair(i, a, False, 1), acc)
        acc = lax.fori_loop(0, (n_tiles - 1 - j) // 2, lambda t, a: pair(j + 1 + odd + 2 * t, a, False, 2), acc)
        pad = jnp.zeros((128 - AUG_ROWS, TA), F32)
        for h in range(HB):
            dk_ref[h] = acc[h][0]
            dv_ref[h] = acc[h][1].astype(BF16)
            dq_ref[h] = jnp.concatenate([dqt_ref[h, j], pad], axis=0).T

        @pl.when((pl.program_id(0) == n_groups - 1) & (j == n_tiles - 1))
        def _():
            for cp in remote:
                cp.wait_recv()
            for cp in remote:
                cp.wait_send()
            for cp in local:
                cp.wait()

    assert n_groups == 1
    resident = pl.BlockSpec(memory_space=pltpu.VMEM)
    rows_blk = lambda r: resident
    tile = lambda w: pl.BlockSpec((HB, TA, w), lambda g, j: (g, j, 0))
    hbm = pl.BlockSpec(memory_space=pl.ANY)
    return pl.pallas_call(
        body,
        name="fox_bwd",
        grid=(n_groups, n_tiles),
        in_specs=[hbm, tile(128), pl.BlockSpec((HB, 1, AUG_KEPT, TA), lambda g, j: (g, j, 0, 0)), tile(HEAD_DIM),
                  hbm, rows_blk(8), rows_blk(8), hbm, hbm],
        out_specs=[tile(128), tile(128), tile(HEAD_DIM), hbm, hbm],
        out_shape=[jax.ShapeDtypeStruct((FOX_HEADS, s_len, 128), F32), jax.ShapeDtypeStruct((FOX_HEADS, s_len, 128), F32),
                   jax.ShapeDtypeStruct((FOX_HEADS, s_len, HEAD_DIM), BF16),
                   jax.ShapeDtypeStruct((N_DEV, 128, 512), BF16), jax.ShapeDtypeStruct((N_DEV, 128, D_MODEL), BF16)],
        scratch_shapes=[pltpu.VMEM((HB, n_tiles, AUG_ROWS, TA), F32),
                        pltpu.VMEM((HB, s_len, 128), BF16), pltpu.VMEM((HB, s_len, HEAD_DIM), BF16),
                        pltpu.SemaphoreType.DMA((14,)), pltpu.SemaphoreType.DMA((14,)), pltpu.SemaphoreType.DMA((2,)),
                        pltpu.SemaphoreType.DMA((2 * n_tiles,))],
        compiler_params=_params(("arbitrary", "arbitrary")),
    )(q_aug, k_aug, kt_aug, v_h, d_o, lse_rows, delta_rows, dw_kv16, dw_out16)


def _fox_post(proj_a, proj_b, bf_pad, gq, gk, seg, dq_aug, dk_aug, dv_h):
    s_len = proj_a.shape[0]
    n_tiles = s_len // TM

    def body(q_ref, k_ref, f_ref, bf_ref, gq_ref, gk_ref, seg_ref, dqa_ref, dka_ref, dvh_ref,
             dq_ref, dk_ref, dv_ref, df_ref, dgq_ref, dgk_ref, dbf_ref, carry_ref):
        @pl.when(pl.program_id(0) == 0)
        def _():
            carry_ref[...] = jnp.zeros((1, 128), F32)
            dgq_ref[...] = jnp.zeros((1, HEAD_DIM), F32)
            dgk_ref[...] = jnp.zeros((1, HEAD_DIM), F32)
            dbf_ref[...] = jnp.zeros((1, 128), F32)

        lane = lax.broadcasted_iota(jnp.int32, (TM, 128), 1)
        seg = seg_ref[...]
        dqas = [dqa_ref[h] for h in range(FOX_HEADS)]
        dkas = [dka_ref[h] for h in range(FOX_HEADS)]
        def head_norms():
            q_all = q_ref[...]
            k_all = k_ref[...]
            rq = _head_rms(q_all, seg)
            rk = _head_rms(k_all, seg)
            yield
            dy_q = jnp.concatenate([t[:, 0:HEAD_DIM] for t in dqas], axis=1) * 0.125
            dq, dgq_rows = _head_rms_bwd(q_all * rq, rq, gq_ref[...], dy_q, seg)
            dq_ref[...] = dq.astype(BF16)
            dgq_ref[...] += _fold_heads(jnp.sum(dgq_rows, axis=0, keepdims=True), FOX_HEADS)
            yield
            dy_k = jnp.concatenate([t[:, 0:HEAD_DIM] for t in dkas], axis=1)
            dk, dgk_rows = _head_rms_bwd(k_all * rk, rk, gk_ref[...], dy_k, seg)
            dk_ref[...] = dk.astype(BF16)
            dgk_ref[...] += _fold_heads(jnp.sum(dgk_rows, axis=0, keepdims=True), FOX_HEADS)
            dv_ref[...] = jnp.concatenate([dvh_ref[h] for h in range(FOX_HEADS)], axis=1)
            yield

        def forget_gates():
            d_cum = jnp.zeros((TM, 128), F32)
            for h in range(FOX_HEADS):
                d_cum = jnp.where(lane == h, dqas[h][:, 64:65] - dkas[h][:, 67:68], d_cum)
            row = lax.broadcasted_iota(jnp.int32, (TM, TM), 0)
            col = lax.broadcasted_iota(jnp.int32, (TM, TM), 1)
            tri = jnp.where(col >= row, 1.0, 0.0).astype(BF16)
            hi, mid, lo = _split3(d_cum)
            d_logf = _dot(tri, hi) + _dot(tri, mid) + _dot(tri, lo) + carry_ref[...]
            yield
            carry_ref[...] = d_logf[0:1, :]
            z = f_ref[...] + bf_ref[...]
            d_f = jnp.where(lane < FOX_HEADS, d_logf / (1.0 + jnp.exp(z)), 0.0)
            df_ref[...] = d_f.astype(BF16)
            dbf_ref[...] += jnp.sum(d_f, axis=0, keepdims=True)
            yield

        _alternate((head_norms(), 3), (forget_gates(), 2))

    rev = lambda i: n_tiles - 1 - i
    col_blk = lambda j: pl.BlockSpec((TM, PIECE), lambda i: (rev(i), j))
    head_blk = lambda w: pl.BlockSpec((FOX_HEADS, TM, w), lambda i: (0, rev(i), 0))
    out_piece = pl.BlockSpec((TM, PIECE), lambda i: (rev(i), 0))
    return pl.pallas_call(
        body,
        name="fox_post",
        grid=(n_tiles,),
        in_specs=[col_blk(1), col_blk(2), pl.BlockSpec((TM, 128), lambda i: (rev(i), (MY_F_OFF - GATES_OFF) // 128)),
                  _full((1, 128)), _full((1, PIECE)), _full((1, PIECE)), _full((PIECE, PIECE)),
                  head_blk(128), head_blk(128), head_blk(HEAD_DIM)],
        out_specs=[out_piece, out_piece, out_piece, pl.BlockSpec((TM, 128), lambda i: (rev(i), 0)),
                   _full((1, HEAD_DIM)), _full((1, HEAD_DIM)), _full((1, 128))],
        out_shape=[jax.ShapeDtypeStruct((s_len, PIECE), BF16)] * 3 + [
            jax.ShapeDtypeStruct((s_len, 128), BF16), jax.ShapeDtypeStruct((1, HEAD_DIM), F32),
            jax.ShapeDtypeStruct((1, HEAD_DIM), F32), jax.ShapeDtypeStruct((1, 128), F32)],
        scratch_shapes=[pltpu.VMEM((1, 128), F32)],
        compiler_params=_params(("arbitrary",)),
    )(proj_a, proj_a, proj_b, bf_pad, gq, gk, seg, dq_aug, dk_aug, dv_h)


def _mem_bwd(mem, g, w_kv, gk, dkm, dvm):
    def body(mem_ref, g_ref, w_ref, gk_ref, dkm_ref, dvm_ref, dw_ref, dg_ref, dgk_ref, dkv_ref):
        m = mem_ref[...]
        r = _rms(m)
        a = m * r
        mn16 = (a * g_ref[...]).astype(BF16)
        kv = _dot(mn16, w_ref[...])
        dgk = jnp.zeros((N_MEM, HEAD_DIM), F32)
        for h in range(MEM_HEADS):
            sl = slice(HEAD_DIM * h, HEAD_DIM * (h + 1))
            kh = kv[:, sl]
            rk = _rms(kh)
            dk, dg_rows = _rms_bwd(kh * rk, rk, gk_ref[...], dkm_ref[:, sl])
            dkv_ref[:, sl] = dk.astype(BF16)
            dgk = dgk + dg_rows
        dkv_ref[:, 256:512] = dvm_ref[...].astype(BF16)
        dgk_ref[...] = jnp.sum(dgk, axis=0, keepdims=True)
        dkv16 = dkv_ref[...]
        dw_ref[...] = _dot_tn(mn16, dkv16).astype(BF16)
        dg_ref[...] = jnp.sum(_dot_nt(dkv16, w_ref[...]) * a, axis=0, keepdims=True)

    return pl.pallas_call(
        body,
        name="mem_bwd",
        out_shape=(jax.ShapeDtypeStruct((D_MODEL, 512), BF16), jax.ShapeDtypeStruct((1, D_MODEL), F32),
                   jax.ShapeDtypeStruct((1, HEAD_DIM), F32)),
        in_specs=[pl.BlockSpec(memory_space=pltpu.VMEM)] * 6,
        out_specs=(pl.BlockSpec(memory_space=pltpu.VMEM),) * 3,
        scratch_shapes=[pltpu.VMEM((N_MEM, 512), BF16)],
        compiler_params=pltpu.CompilerParams(vmem_limit_bytes=VMEM_LIMIT),
    )(mem, g, w_kv, gk, dkm, dvm)


def _grad_x_exchange(pieces, d_f, w_my, x, norm_g, d_out, dw_in, land_kv, land_out, d_mem_g, d_scale, d_bf, d_gq, d_gk, d_gqm,
                     d_gkm, dwp_bd, loss_blk):
    s_len = x.shape[0]
    n_steps = s_len // TM
    step2, step3 = n_steps // 4, (11 * n_steps) // 16
    half = D_MODEL // 2

    def body(p0, p1, p2, p3, p4, p5, df_ref, x_ref, dout_ref, w_ref, g_ref, in_ref, lkv_ref, lout_ref,
             dmg, dsc, dbf, dgq, dgk, dgqm, dgkm, dwp, loss_ref,
             gx_ref, rin_ref, rkv_ref, rout_ref, sred_ref,
             dng, land1, send2a, send2b, keep2a, keep2b, land2a, land2b, send3a, send3b, land3a, land3b, sb_ref, sland,
             own4, lkv_v, lout_v, send_sems, recv_sems, load_sems):
        i = pl.program_id(0)
        x_, y_, c_, me = _my_place()
        sibling, x_nbr, y_nbr = (x_, y_, 1 - c_), (1 - x_, y_, c_), (x_, 1 - y_, c_)
        loads = [pltpu.make_async_copy(in_ref.at[2 * k + c_], own4.at[k], load_sems.at[k]) for k in range(4)]
        loads += [pltpu.make_async_copy(lkv_ref, lkv_v, load_sems.at[4]), pltpu.make_async_copy(lout_ref, lout_v, load_sems.at[5])]

        def rcopy(src, dst, sem, to):
            return pltpu.make_async_remote_copy(src_ref=src, dst_ref=dst, send_sem=send_sems.at[sem], recv_sem=recv_sems.at[sem],
                                                device_id=to, device_id_type=MESH)

        early_rows, late_rows = pl.ds(8, SB_ROWS - 8), pl.ds(0, 8)
        small_early, small_late = [], []
        for k in range(1, N_DEV):
            peer = (x_ ^ (k >> 2), y_ ^ ((k >> 1) & 1), c_ ^ (k & 1))
            small_early.append(rcopy(sb_ref.at[early_rows], sland.at[me, early_rows], k - 1, peer))
            small_late.append(rcopy(sb_ref.at[late_rows], sland.at[me, late_rows], 16 + k, peer))
        small = small_early + small_late
        stage1 = [rcopy(in_ref.at[2 * k + 1 - c_], land1.at[k], 7 + k, sibling) for k in range(4)]
        stage2 = [rcopy(send2a.at[j], land2a.at[j], 11 + j, x_nbr) for j in range(2)]
        stage2 += [rcopy(send2b.at[j], land2b.at[j], 13 + j, y_nbr) for j in range(2)]
        stage3 = [rcopy(send3a, land3a, 15, y_nbr), rcopy(send3b, land3b, 16, x_nbr)]
        top, bot = slice(0, half), slice(half, D_MODEL)

        @pl.when(i == 0)
        def _():
            dng[...] = jnp.zeros((1, D_MODEL), F32)
            for cp in stage1 + loads:
                cp.start()
            sb_ref[...] = jnp.zeros((SB_ROWS, SB_W), F32)
            sb_ref[SB_POOL_SCALE:SB_POOL_SCALE + 1, :] = dsc[...]
            sb_ref[SB_B_F:SB_B_F + 1, 0:128] = dbf[...]
            for row, ref in ((SB_FOX_Q, dgq), (SB_FOX_K, dgk), (SB_MEM_Q, dgqm), (SB_MEM_K, dgkm)):
                sb_ref[row:row + 1, 0:HEAD_DIM] = ref[...]
            sb_ref[SB_LOSS:SB_LOSS + 1, 0:128] = loss_ref[0:1, :]
            for grp in range(4):
                sl = slice(64 * grp, 64 * (grp + 1))
                sb_ref[SB_W_POOL:SB_W_POOL + 64, sl] = dwp[sl, sl]
            for cp in small_early:
                cp.start()

        @pl.when(i == step2)
        def _():
            for cp in stage1:
                cp.wait_recv()
            for cp in loads[0:4]:
                cp.wait()

            def chip_sum(k, cols):
                return own4[k, :, cols].astype(F32) + land1[k, :, cols].astype(F32)

            for j in range(2):
                send2a[j] = chip_sum(2 * (1 - x_) + j, top).astype(BF16)
                keep2a[j] = chip_sum(2 * x_ + j, top)
                send2b[j] = chip_sum(2 * j + 1 - y_, bot).astype(BF16)
                keep2b[j] = chip_sum(2 * j + y_, bot)
            for cp in stage2:
                cp.start()

        @pl.when(i == step3)
        def _():
            for cp in stage2:
                cp.wait_recv()
            for j in range(2):
                keep2a[j] = keep2a[j] + land2a[j].astype(F32)
                keep2b[j] = keep2b[j] + land2b[j].astype(F32)
            send3a[...] = keep2a[1 - y_].astype(BF16)
            send3b[...] = keep2b[1 - x_].astype(BF16)
            for cp in stage3:
                cp.start()

        dh = _dot_nt(df_ref[...], w_ref[:, MY_F_OFF:MY_WIDTH])
        for j, p in enumerate((p0, p1, p2, p3, p4, p5)):
            dh = dh + _dot_nt(p[...], w_ref[:, PIECE * j:PIECE * (j + 1)])
        xv = x_ref[...]
        r = _rms(xv)
        dx, dg_rows = _rms_bwd(xv * r, r, g_ref[...], dh)
        gx_ref[...] = dout_ref[...] + dx
        dng[...] += jnp.sum(dg_rows, axis=0, keepdims=True)

        @pl.when(i == n_steps - 1)
        def _():
            for k in range(4):
                sb_ref[SB_NORM_G + k:SB_NORM_G + k + 1, :] = dng[:, SB_W * k:SB_W * (k + 1)]
                sb_ref[SB_MEM_NORM_G + k:SB_MEM_NORM_G + k + 1, :] = dmg[:, SB_W * k:SB_W * (k + 1)]
            for cp in small_late:
                cp.start()
            sland[me] = sb_ref[...]
            for cp in stage3:
                cp.wait_recv()
            rin_ref[:, top] = keep2a[y_] + land3a[...].astype(F32)
            rin_ref[:, bot] = keep2b[x_] + land3b[...].astype(F32)
            for cp in small:
                cp.wait_recv()
            for cp in small + stage1 + stage2 + stage3:
                cp.wait_send()
            for cp in loads[4:6]:
                cp.wait()
            for land, red in ((lkv_v, rkv_ref), (lout_v, rout_ref), (sland, sred_ref)):
                acc = land[0].astype(F32)
                for d in range(1, N_DEV):
                    acc = acc + land[d].astype(F32)
                red[...] = acc

    piece = pl.BlockSpec((TM, PIECE), lambda i: (i, 0))
    row = pl.BlockSpec((TM, D_MODEL), lambda i: (i, 0))
    vmem = pl.BlockSpec(memory_space=pltpu.VMEM)
    half_chunk = lambda n, dt: pltpu.VMEM((n, CHUNK_ROWS, half), dt)
    whole = (w_my, norm_g, dw_in, land_kv, land_out, d_mem_g, d_scale, d_bf, d_gq, d_gk, d_gqm, d_gkm, dwp_bd, loss_blk)
    return pl.pallas_call(
        body,
        name="grad_x_exchange",
        grid=(n_steps,),
        in_specs=[piece] * 6 + [pl.BlockSpec((TM, 128), lambda i: (i, 0)), row, row, vmem, vmem]
        + [pl.BlockSpec(memory_space=pl.ANY)] * 3 + [vmem] * (len(whole) - 5),
        out_specs=[row, vmem, vmem, vmem, vmem],
        out_shape=[jax.ShapeDtypeStruct((s_len, D_MODEL), F32), jax.ShapeDtypeStruct((CHUNK_ROWS, D_MODEL), F32),
                   jax.ShapeDtypeStruct((128, 512), F32), jax.ShapeDtypeStruct((128, D_MODEL), F32),
                   jax.ShapeDtypeStruct((SB_ROWS, SB_W), F32)],
        scratch_shapes=[
            pltpu.VMEM((1, D_MODEL), F32),
            pltpu.VMEM((4, CHUNK_ROWS, D_MODEL), BF16),
            half_chunk(2, BF16), half_chunk(2, BF16), half_chunk(2, F32), half_chunk(2, F32), half_chunk(2, BF16), half_chunk(2, BF16),
            pltpu.VMEM((CHUNK_ROWS, half), BF16), pltpu.VMEM((CHUNK_ROWS, half), BF16),
            pltpu.VMEM((CHUNK_ROWS, half), BF16), pltpu.VMEM((CHUNK_ROWS, half), BF16),
            pltpu.VMEM((SB_ROWS, SB_W), F32),
            pltpu.VMEM((N_DEV, SB_ROWS, SB_W), F32),
            pltpu.VMEM((4, CHUNK_ROWS, D_MODEL), BF16),
            pltpu.VMEM((N_DEV, 128, 512), BF16),
            pltpu.VMEM((N_DEV, 128, D_MODEL), BF16),
            pltpu.SemaphoreType.DMA((24,)),
            pltpu.SemaphoreType.DMA((24,)),
            pltpu.SemaphoreType.DMA((6,)),
        ],
        compiler_params=_params(("arbitrary",)),
    )(*pieces, d_f, x, d_out, *whole)


def _grad_w_in(h16, pieces, d_f):
    s_len = h16.shape[0]
    tk = 512

    def body(h_ref, p0, p1, p2, p3, p4, p5, df_ref, out_ref, dw_ref):
        @pl.when(pl.program_id(0) == 0)
        def _():
            dw_ref[...] = jnp.zeros((D_MODEL, MY_WIDTH), F32)

        h = h_ref[...]
        for j, p in enumerate((p0, p1, p2, p3, p4, p5)):
            dw_ref[:, PIECE * j:PIECE * (j + 1)] += _dot_tn(h, p[...])
        dw_ref[:, MY_F_OFF:MY_WIDTH] += _dot_tn(h, df_ref[...])

        @pl.when(pl.program_id(0) == pl.num_programs(0) - 1)
        def _():
            for d in range(N_DEV):
                parts = [dw_ref[:, start:start + width] for start, width in _chunk_segments(d)]
                parts.append(jnp.zeros((D_MODEL, 512 - IN_CHUNK), F32))
                out_ref[d] = jnp.concatenate(parts, axis=1).T[0:CHUNK_ROWS].astype(BF16)

    piece = pl.BlockSpec((tk, PIECE), lambda i: (i, 0))
    return pl.pallas_call(
        body,
        name="grad_w_in",
        grid=(s_len // tk,),
        in_specs=[pl.BlockSpec((tk, D_MODEL), lambda i: (i, 0))] + [piece] * 6 + [pl.BlockSpec((tk, 128), lambda i: (i, 0))],
        out_specs=_full((N_DEV, CHUNK_ROWS, D_MODEL)),
        out_shape=jax.ShapeDtypeStruct((N_DEV, CHUNK_ROWS, D_MODEL), BF16),
        scratch_shapes=[pltpu.VMEM((D_MODEL, MY_WIDTH), F32)],
        compiler_params=_params(("arbitrary",)),
    )(h16, *pieces, d_f)


def _adamw(w, g, m, v):
    m = ADAM_B1 * m + (1.0 - ADAM_B1) * g
    v = ADAM_B2 * v + (1.0 - ADAM_B2) * (g * g)
    m_hat = m / (1.0 - ADAM_B1 ** ADAM_STEP)
    v_hat = v / (1.0 - ADAM_B2 ** ADAM_STEP)
    return -ADAM_LR * (m_hat / (jnp.sqrt(v_hat) + ADAM_EPS) + ADAM_WD * w), m, v


PARAM_ORDER = ("norm_g", "w_in", "b_f", "w_pool", "pool_scale", "fox_q_g", "fox_k_g", "mem_norm_g", "w_mem_kv", "mem_q_g", "mem_k_g", "w_out")


def _update(red_in, red_kv, red_out, sred, params):
    def body(rin_ref, rkv_ref, rout_ref, sred_ref, *refs):
        ins, outs = refs[:3 * len(PARAM_ORDER)], refs[3 * len(PARAM_ORDER):]
        wide = lambda row: jnp.concatenate([sred_ref[row + k:row + k + 1, :] for k in range(4)], axis=1)
        head = lambda row: sred_ref[row:row + 1, 0:HEAD_DIM]
        grads = {
            "norm_g": lambda: wide(SB_NORM_G), "w_in": lambda: rin_ref[...], "b_f": lambda: sred_ref[SB_B_F:SB_B_F + 1, 0:FOX_HEADS],
            "pool_scale": lambda: sred_ref[SB_POOL_SCALE:SB_POOL_SCALE + 1, :], "fox_q_g": lambda: head(SB_FOX_Q),
            "fox_k_g": lambda: head(SB_FOX_K), "mem_norm_g": lambda: wide(SB_MEM_NORM_G), "w_mem_kv": lambda: rkv_ref[...],
            "mem_q_g": lambda: head(SB_MEM_Q), "mem_k_g": lambda: head(SB_MEM_K), "w_out": lambda: rout_ref[...],
        }
        for p, name in enumerate(PARAM_ORDER):
            w_ref, m_ref, v_ref = ins[3 * p:3 * p + 3]
            g_out, d_out, m_out, v_out = outs[4 * p:4 * p + 4]
            if name == "w_pool":
                for grp in range(4):
                    g = sred_ref[SB_W_POOL:SB_W_POOL + 64, 64 * grp:64 * (grp + 1)]
                    delta, m_new, v_new = _adamw(w_ref[grp], g, m_ref[grp], v_ref[grp])
                    g_out[grp], d_out[grp], m_out[grp], v_out[grp] = g, delta, m_new, v_new
            elif name == "w_in":
                for j in range(8):
                    rows = pl.ds(j, IN_CHUNK, stride=8)
                    g = rin_ref[0:IN_CHUNK, 128 * j:128 * (j + 1)]
                    delta, m_new, v_new = _adamw(w_ref[rows, :], g, m_ref[rows, :], v_ref[rows, :])
                    g_out[rows, :], d_out[rows, :], m_out[rows, :], v_out[rows, :] = g, delta, m_new, v_new
            else:
                g = grads[name]()
                delta, m_new, v_new = _adamw(w_ref[...], g, m_ref[...], v_ref[...])
                g_out[...], d_out[...], m_out[...], v_out[...] = g, delta, m_new, v_new

    flat = [t for name in PARAM_ORDER for t in params[name]]
    shapes = [params[name][0].shape for name in PARAM_ORDER for _ in range(4)]
    outs = pl.pallas_call(
        body,
        name="adamw_update",
        out_shape=tuple(jax.ShapeDtypeStruct(s, F32) for s in shapes),
        in_specs=[pl.BlockSpec(memory_space=pltpu.VMEM)] * (4 + len(flat)),
        out_specs=(pl.BlockSpec(memory_space=pltpu.VMEM),) * len(shapes),
        compiler_params=pltpu.CompilerParams(vmem_limit_bytes=VMEM_LIMIT),
    )(red_in, red_kv, red_out, sred, *flat)
    return {name: outs[4 * p:4 * p + 4] for p, name in enumerate(PARAM_ORDER)}


def kernel(x, mem, norm_g, w_in, b_f, w_pool, pool_scale, fox_q_g, fox_k_g, mem_norm_g, w_mem_kv, mem_q_g, mem_k_g, w_out, loss_target, m_norm_g, m_w_in, m_b_f, m_w_pool, m_pool_scale, m_fox_q_g, m_fox_k_g, m_mem_norm_g, m_w_mem_kv, m_mem_q_g, m_mem_k_g, m_w_out, v_norm_g, v_w_in, v_b_f, v_w_pool, v_pool_scale, v_fox_q_g, v_fox_k_g, v_mem_norm_g, v_w_mem_kv, v_mem_q_g, v_mem_k_g, v_w_out):
    given = dict(norm_g=(norm_g, m_norm_g, v_norm_g), w_in=(w_in, m_w_in, v_w_in), b_f=(b_f, m_b_f, v_b_f),
                 w_pool=(w_pool, m_w_pool, v_w_pool), pool_scale=(pool_scale, m_pool_scale, v_pool_scale),
                 fox_q_g=(fox_q_g, m_fox_q_g, v_fox_q_g), fox_k_g=(fox_k_g, m_fox_k_g, v_fox_k_g),
                 mem_norm_g=(mem_norm_g, m_mem_norm_g, v_mem_norm_g), w_mem_kv=(w_mem_kv, m_w_mem_kv, v_w_mem_kv),
                 mem_q_g=(mem_q_g, m_mem_q_g, v_mem_q_g), mem_k_g=(mem_k_g, m_mem_k_g, v_mem_k_g), w_out=(w_out, m_w_out, v_w_out))
    params = {name: tuple(t[0] if t.ndim > 2 else t for t in wmv) for name, wmv in given.items()}
    params["w_in"] = tuple(t.T.reshape(IN_CHUNK * 8, 128) for t in params["w_in"])
    x2, mem2, target2 = x[0], mem[0], loss_target[0]

    seg = jnp.asarray(np.kron(np.eye(FOX_HEADS), np.full((HEAD_DIM, HEAD_DIM), 1.0 / HEAD_DIM)), BF16)
    w_my, w_kv16, w_out16, wp_bd, bf_pad, gq8, gk8, gqm4 = _gather_w_in(
        params["w_in"][0], params["w_mem_kv"][0], params["w_out"][0], params["w_pool"][0], b_f, fox_q_g, fox_k_g, mem_q_g)
    proj_a, proj_b, h16, q_aug, k_aug, v_h, kt_aug, vt_aug = _proj_prep(x2, norm_g, w_my, bf_pad, gq8, gk8, seg)
    olse, lse_rows, w_kv_all, w_out_all = _fox_fwd(q_aug, k_aug, vt_aug, w_kv16, w_out16)
    km, vm = _mem_prep(mem2, mem_norm_g, w_kv_all, mem_k_g)
    d_out, d_mixed, dw_out, loss_blk = _mix_out_loss(proj_a, proj_b, olse, km, vm, wp_bd, pool_scale, gqm4, seg, x2, target2, w_out_all)

    d_ua, d_gb, d_qm, d_o, delta_rows, dwp_bd, d_scale, dkm, dvm, d_gqm = _mix_bwd(proj_a, proj_b, olse, d_mixed, km, vm, wp_bd,
                                                                                    pool_scale, gqm4, seg)
    dw_kv, d_mem_g, d_gkm = _mem_bwd(mem2, mem_norm_g, w_kv_all, mem_k_g, dkm, dvm)
    dq_aug, dk_aug, dv_h, land_kv, land_out = _fox_bwd(q_aug, k_aug, kt_aug, v_h, d_o, lse_rows, delta_rows, dw_kv, dw_out)
    d_q, d_k, d_v, d_f, d_gq, d_gk, d_bf = _fox_post(proj_a, proj_b, bf_pad, gq8, gk8, seg, dq_aug, dk_aug, dv_h)
    pieces = (d_ua, d_q, d_k, d_v, d_gb, d_qm)
    dw_in = _grad_w_in(h16, pieces, d_f)
    grad_x, red_in, red_kv, red_out, sred = _grad_x_exchange(pieces, d_f, w_my, x2, norm_g, d_out, dw_in, land_kv, land_out, d_mem_g,
                                                             d_scale, d_bf, d_gq, d_gk, d_gqm, d_gkm, dwp_bd, loss_blk)
    new = _update(red_in, red_kv, red_out, sred, params)
    result = [sred[SB_LOSS, 0], grad_x[None]]
    for kind in range(4):
        for name in PARAM_ORDER:
            out = new[name][kind]
            if name == "w_in":
                out = out.reshape(IN_CHUNK, D_MODEL).T
            result.append(out[None] if given[name][0].ndim > 2 else out)
    return tuple(result)
```

```python
import functools

import jax
import jax.numpy as jnp
import numpy as np
from jax import lax
from jax.experimental import pallas as pl
from jax.experimental.pallas import tpu as pltpu

F32 = jnp.float32
BF16 = jnp.bfloat16
MESH = pl.DeviceIdType.MESH

N_DEV = 8
D_MODEL = 1024
HEAD_DIM = 64
FOX_HEADS = 8
MEM_HEADS = 4
N_MEM = 256
POOL_WIDTH = 256
EPS = 1e-6
IN_WIDTH = 3080
IN_CHUNK = IN_WIDTH // N_DEV
CHUNK_ROWS = 400
F_OFF = 2048
MY_WIDTH = 3200
MY_F_OFF = 3072
PIECE = 512
GATES_OFF = 4 * PIECE
AUG_KEPT = 80
TM = 256
TMM = 512
TA = 256
HB = 8
HB_FWD = 8
AUG_ROWS = 72
HALO = 16
VMEM_LIMIT = 56 * 1024 * 1024

ADAM_LR = 0.001
ADAM_B1 = 0.9
ADAM_B2 = 0.999
ADAM_EPS = 1e-08
ADAM_WD = 0.01
ADAM_STEP = 10


def _dot(a, b):
    return jnp.dot(a, b, preferred_element_type=F32)


def _dot_nt(a, b):
    return lax.dot_general(a, b, (((1,), (1,)), ((), ())), preferred_element_type=F32)


def _dot_tn(a, b):
    return lax.dot_general(a, b, (((0,), (0,)), ((), ())), preferred_element_type=F32)


def _sigmoid(g):
    return 0.5 + 0.5 * jnp.tanh(0.5 * g)


def _split3(v):
    hi = v.astype(BF16)
    r1 = v - hi.astype(F32)
    mid = r1.astype(BF16)
    lo = (r1 - mid.astype(F32)).astype(BF16)
    return hi, mid, lo


def _rms(v):
    return lax.rsqrt(jnp.mean(v * v, axis=-1, keepdims=True) + EPS)


def _rms_bwd(a, r, g, dy):
    da = dy * g
    return r * (da - a * jnp.mean(da * a, axis=-1, keepdims=True)), dy * a


def _head_mean(z, seg):
    hi = z.astype(BF16)
    lo = (z - hi.astype(F32)).astype(BF16)
    if z.shape[1] == 256:
        return _dot(hi, seg) + _dot(lo, seg)
    half = seg[0:256, 0:256]
    return jnp.concatenate([_dot(hi[:, 0:256], half) + _dot(lo[:, 0:256], half),
                            _dot(hi[:, 256:512], half) + _dot(lo[:, 256:512], half)], axis=1)


def _head_rms(v, seg):
    return lax.rsqrt(_head_mean(v * v, seg) + EPS)


def _head_rms_bwd(a, r, g, dy, seg):
    da = dy * g
    return r * (da - a * _head_mean(da * a, seg)), dy * a


def _fold_heads(row, n_heads):
    out = row[:, 0:HEAD_DIM]
    for h in range(1, n_heads):
        out = out + row[:, HEAD_DIM * h:HEAD_DIM * (h + 1)]
    return out


def _params(sem, limit=VMEM_LIMIT):
    return pltpu.CompilerParams(dimension_semantics=sem, vmem_limit_bytes=limit)


def _full(shape):
    return pl.BlockSpec(shape, lambda *_: (0,) * len(shape))


def _chunk_segments(d):
    runs = []
    for lo, hi, shift in ((0, F_OFF, 0), (F_OFF, F_OFF + FOX_HEADS, MY_F_OFF - F_OFF), (F_OFF + FOX_HEADS, IN_WIDTH, -FOX_HEADS)):
        a, b = max(lo, IN_CHUNK * d), min(hi, IN_CHUNK * (d + 1))
        if a < b:
            runs.append((a + shift, b - a))
    return runs


def _my_place():
    x, y, c = lax.axis_index("x"), lax.axis_index("y"), lax.axis_index("c")
    return x, y, c, 4 * x + 2 * y + c


def _shard_rows(dev):
    return pl.ds(pl.multiple_of(128 * dev, 128), 128)


def _gather_w_in(w_in, w_kv, w_out, w_pool, b_f, gq, gk, gqm):
    def body(win_ref, wkv_ref, wout_ref, wp_ref, bf_ref, gq_ref, gk_ref, gqm_ref, wmy_ref, kv16_ref, out16_ref, wpbd_ref, bfpad_ref,
             gq8_ref, gk8_ref, gqm4_ref, in_all, pay_in, win_rows, send_sems, recv_sems, load_sem):
        x, y, c, me = _my_place()
        here, sibling = (x, y, c), (x, y, 1 - c)
        x_chip, y_chip, far_chip = (1 - x, y), (x, 1 - y), (1 - x, 1 - y)
        relay_from = (x ^ (1 - c), y ^ c)
        relay_to = (x ^ c, y ^ (1 - c))

        load = pltpu.make_async_copy(win_ref, win_rows, load_sem)
        load.start()
        load.wait()
        pay_in[...] = jnp.zeros((CHUNK_ROWS, D_MODEL), BF16)
        for j in range(8):
            pay_in[0:IN_CHUNK, 128 * j:128 * (j + 1)] = win_rows[pl.ds(j, IN_CHUNK, stride=8), :].astype(BF16)

        def copy(k, block, to, own=False):
            px, py, pc = block
            dst = in_all.at[4 * px + 2 * py + pc]
            return pltpu.make_async_remote_copy(src_ref=pay_in if own else dst, dst_ref=dst, send_sem=send_sems.at[k],
                                                recv_sem=recv_sems.at[k], device_id=to, device_id_type=MESH)

        first = [copy(0, here, sibling, own=True), copy(1, here, (*x_chip, c), own=True), copy(2, here, (*y_chip, c), own=True)]
        for cp in first:
            cp.start()
        in_all[me] = pay_in[...]
        kv16_ref[...] = wkv_ref[...].astype(BF16)
        out16_ref[...] = wout_ref[...].astype(BF16)
        wpbd_ref[...] = jnp.zeros((POOL_WIDTH, POOL_WIDTH), BF16)
        for grp in range(4):
            sl = slice(64 * grp, 64 * (grp + 1))
            wpbd_ref[sl, sl] = wp_ref[grp].astype(BF16)
        bfpad_ref[...] = jnp.zeros((1, 128), F32)
        bfpad_ref[:, 0:FOX_HEADS] = bf_ref[...]
        gq8_ref[...] = jnp.concatenate([gq_ref[...]] * FOX_HEADS, axis=1)
        gk8_ref[...] = jnp.concatenate([gk_ref[...]] * FOX_HEADS, axis=1)
        gqm4_ref[...] = jnp.concatenate([gqm_ref[...]] * MEM_HEADS, axis=1)
        copy(1 + c, (*relay_from, c), here).wait_recv()
        passed = [copy(3, (*relay_from, c), (*relay_to, c)), copy(4, (*relay_from, c), sibling)]
        for cp in passed:
            cp.start()
        copy(2 - c, (*relay_to, c), here).wait_recv()
        passed.append(copy(5, (*relay_to, c), sibling))
        passed[-1].start()
        copy(3, (*far_chip, c), here).wait_recv()
        passed.append(copy(6, (*far_chip, c), sibling))
        passed[-1].start()
        copy(0, sibling, here).wait_recv()
        for k, chip in ((4, relay_to), (5, relay_from), (6, far_chip)):
            copy(k, (*chip, 1 - c), here).wait_recv()
        for cp in first + passed:
            cp.wait_send()

        row_pad = jnp.zeros((512 - CHUNK_ROWS, 256), F32)
        for r0 in range(0, D_MODEL, 256):
            ch = [jnp.concatenate([in_all[d, :, r0:r0 + 256].astype(F32), row_pad], axis=0).T[:, 0:IN_CHUNK] for d in range(N_DEV)]
            lo = F_OFF - 5 * IN_CHUNK
            full = jnp.concatenate(ch[:5] + [ch[5][:, :lo], ch[5][:, lo + FOX_HEADS:], ch[6], ch[7], ch[5][:, lo:lo + FOX_HEADS],
                                             jnp.zeros((256, MY_WIDTH - IN_WIDTH), F32)], axis=1)
            wmy_ref[r0:r0 + 256, :] = full.astype(BF16)

    return pl.pallas_call(
        body,
        name="gather_w_in",
        out_shape=(jax.ShapeDtypeStruct((D_MODEL, MY_WIDTH), BF16), jax.ShapeDtypeStruct((128, 512), BF16),
                   jax.ShapeDtypeStruct((128, D_MODEL), BF16), jax.ShapeDtypeStruct((POOL_WIDTH, POOL_WIDTH), BF16),
                   jax.ShapeDtypeStruct((1, 128), F32), jax.ShapeDtypeStruct((1, PIECE), F32), jax.ShapeDtypeStruct((1, PIECE), F32),
                   jax.ShapeDtypeStruct((1, 256), F32)),
        in_specs=[pl.BlockSpec(memory_space=pl.ANY)] + [pl.BlockSpec(memory_space=pltpu.VMEM)] * 7,
        out_specs=(pl.BlockSpec(memory_space=pltpu.VMEM),) * 8,
        scratch_shapes=[
            pltpu.VMEM((N_DEV, CHUNK_ROWS, D_MODEL), BF16),
            pltpu.VMEM((CHUNK_ROWS, D_MODEL), BF16),
            pltpu.VMEM((IN_CHUNK * 8, 128), F32),
            pltpu.SemaphoreType.DMA((7,)),
            pltpu.SemaphoreType.DMA((7,)),
            pltpu.SemaphoreType.DMA,
        ],
        compiler_params=pltpu.CompilerParams(vmem_limit_bytes=VMEM_LIMIT),
    )(w_in, w_kv, w_out, w_pool, b_f, gq, gk, gqm)


def _row_block_exchange(kv_ref, out_ref, kv_dst, out_dst, send_sems, recv_sems, local_sems, gather):
    x, y, c, me = _my_place()
    remote = []
    for k in range(1, N_DEV):
        px, py, pc = x ^ (k >> 2), y ^ ((k >> 1) & 1), c ^ (k & 1)
        peer = 4 * px + 2 * py + pc
        for fam, (src, dst) in enumerate(((kv_ref, kv_dst), (out_ref, out_dst))):
            remote.append(pltpu.make_async_remote_copy(
                src_ref=src if gather else src.at[_shard_rows(peer)], dst_ref=dst.at[_shard_rows(me)] if gather else dst.at[me],
                send_sem=send_sems.at[7 * fam + k - 1], recv_sem=recv_sems.at[7 * fam + k - 1],
                device_id=(px, py, pc), device_id_type=MESH))
    local = [pltpu.make_async_copy(src if gather else src.at[_shard_rows(me)], dst.at[_shard_rows(me)] if gather else dst.at[me],
                                   local_sems.at[fam]) for fam, (src, dst) in enumerate(((kv_ref, kv_dst), (out_ref, out_dst)))]
    return remote, local


SB_ROWS, SB_W = 80, 256
SB_NORM_G, SB_MEM_NORM_G, SB_POOL_SCALE, SB_B_F, SB_FOX_Q, SB_FOX_K, SB_MEM_Q, SB_MEM_K, SB_LOSS, SB_W_POOL = 0, 4, 8, 9, 10, 11, 12, 13, 14, 16


def _alternate(*parts):
    state = [[part, 0, stages] for part, stages in parts]
    while state:
        least = min(state, key=lambda entry: entry[1] / entry[2])
        least[1] += 1
        if next(least[0], "done") == "done":
            state.remove(least)


def _log_sigmoid(z):
    return jnp.minimum(z, 0.0) - jnp.log(1.0 + jnp.exp(-jnp.abs(z)))


def _forget_lane_maps():
    to_q = np.zeros((128, FOX_HEADS * 128), np.float32)
    to_k = np.zeros((128, FOX_HEADS * 128), np.float32)
    for h in range(FOX_HEADS):
        for term in range(3):
            to_q[8 * term + h, 128 * h + 64 + term] = 1.0
            to_q[24, 128 * h + 67 + term] = 1.0
            to_k[24, 128 * h + 64 + term] = 1.0
            to_k[8 * term + h, 128 * h + 67 + term] = -1.0
    return jnp.asarray(to_q, BF16), jnp.asarray(to_k, BF16)


def _proj_prep(x, norm_g, w_my, bf_pad, gq, gk, seg):
    s_len = x.shape[0]
    n_blocks = s_len // TMM
    halves = TMM // TM
    to_q, to_k = _forget_lane_maps()
    q_off, f_off = PIECE, MY_F_OFF
    kept = 3 * PIECE + 128

    def body(x_ref, g_ref, w_ref, bf_ref, gq_ref, gk_ref, seg_ref, eq_ref, ek_ref,
             proj_a_ref, proj_b_ref, h_ref, qa_ref, ka_ref, vh_ref, kt_ref, vt_ref, kept_even, kept_odd, carry_ref):
        s = pl.program_id(0)

        def project(kept_ref):
            xv = x_ref[...]
            h_ref[...] = (xv * _rms(xv) * g_ref[...]).astype(BF16)
            yield
            for c0 in range(0, MY_WIDTH, 256):
                c1 = min(c0 + 256, MY_WIDTH)
                res = _dot(h_ref[...], w_ref[:, c0:c1])
                if c0 < 3 * PIECE:
                    proj_a_ref[:, c0:c1] = res
                if c0 >= GATES_OFF:
                    proj_b_ref[:, c0 - GATES_OFF:c1 - GATES_OFF] = res
                if q_off <= c0 < q_off + 3 * PIECE:
                    kept_ref[:, c0 - q_off:c1 - q_off] = res
                if c0 == f_off:
                    kept_ref[:, 3 * PIECE:kept] = res
                yield

        def operands(kept_ref):
            lane = lax.broadcasted_iota(jnp.int32, (TM, 128), 1)
            row = lax.broadcasted_iota(jnp.int32, (TM, TM), 0)
            col = lax.broadcasted_iota(jnp.int32, (TM, TM), 1)
            tri = jnp.where(col <= row, 1.0, 0.0).astype(BF16)
            one_at_64 = jnp.where(lane == 64, 1.0, 0.0).astype(BF16)
            zeros = jnp.zeros((TM, HEAD_DIM), BF16)
            for half in range(halves):
                rows = slice(TM * half, TM * (half + 1))
                logf = jnp.where(lane < FOX_HEADS, _log_sigmoid(kept_ref[rows, 3 * PIECE:kept] + bf_ref[...]), 0.0)
                hi, mid, lo = _split3(logf)
                cum = _dot(tri, hi) + _dot(tri, mid) + _dot(tri, lo) + carry_ref[...]
                q_all = kept_ref[rows, 0:PIECE]
                k_all = kept_ref[rows, PIECE:2 * PIECE]
                rq = _head_rms(q_all, seg_ref[...])
                rk = _head_rms(k_all, seg_ref[...])
                yield
                carry_ref[...] = cum[TM - 1:TM, :]
                f_hi, f_mid, f_lo = [t.astype(F32) for t in _split3(cum)]
                packed = (f_hi + pltpu.roll(f_mid, 8, 1) + pltpu.roll(f_lo, 16, 1)
                          + jnp.where(lane == 24, 1.0, 0.0)).astype(BF16)
                extra_q = _dot(packed, eq_ref[...]).astype(BF16)
                extra_k = _dot(packed, ek_ref[...]).astype(BF16)
                qn_all = (q_all * rq * gq_ref[...] * 0.125).astype(BF16)
                kn_all = (k_all * rk * gk_ref[...]).astype(BF16)
                v_all = kept_ref[rows, 2 * PIECE:3 * PIECE].astype(BF16)
                yield
                for h in range(FOX_HEADS):
                    sl = slice(HEAD_DIM * h, HEAD_DIM * (h + 1))
                    tile = slice(128 * h, 128 * (h + 1))
                    qa = jnp.where(lane < 64, jnp.concatenate([qn_all[:, sl], zeros], axis=1), extra_q[:, tile])
                    ka = jnp.where(lane < 64, jnp.concatenate([kn_all[:, sl], zeros], axis=1), extra_k[:, tile])
                    vh = v_all[:, sl]
                    v_aug = jnp.where(lane < 64, jnp.concatenate([vh, zeros], axis=1), one_at_64)
                    qa_ref[h, rows, :] = qa
                    ka_ref[h, rows, :] = ka
                    vh_ref[h, rows, :] = vh
                    kt_ref[h, half] = ka.T[0:AUG_KEPT]
                    vt_ref[h, half] = v_aug.T[0:AUG_KEPT]
                    if h % 2 == 1:
                        yield

        projecting = lambda kept_ref: (project(kept_ref), 1 + pl.cdiv(MY_WIDTH, 256))
        making = lambda kept_ref: (operands(kept_ref), halves * (2 + FOX_HEADS // 2))

        @pl.when(s == 0)
        def _():
            carry_ref[...] = jnp.zeros((1, 128), F32)
            _alternate(projecting(kept_even))

        @pl.when((s > 0) & (s < n_blocks) & (s % 2 == 1))
        def _():
            _alternate(projecting(kept_odd), making(kept_even))

        @pl.when((s > 0) & (s < n_blocks) & (s % 2 == 0))
        def _():
            _alternate(projecting(kept_even), making(kept_odd))

        @pl.when(s == n_blocks)
        def _():
            _alternate(making(kept_odd if n_blocks % 2 == 0 else kept_even))

    made = lambda s: jnp.minimum(s, n_blocks - 1)
    used = lambda s: jnp.maximum(s - 1, 0)
    head_blk = lambda w: pl.BlockSpec((FOX_HEADS, TMM, w), lambda s: (0, used(s), 0))
    tile_blk = pl.BlockSpec((FOX_HEADS, halves, AUG_KEPT, TM), lambda s: (0, used(s), 0, 0))
    tiled = jax.ShapeDtypeStruct((FOX_HEADS, s_len // TM, AUG_KEPT, TM), BF16)
    made_blk = lambda w: pl.BlockSpec((TMM, w), lambda s: (made(s), 0))
    return pl.pallas_call(
        body,
        name="proj_prep",
        grid=(n_blocks + 1,),
        in_specs=[pl.BlockSpec((TMM, D_MODEL), lambda s: (made(s), 0)), _full((1, D_MODEL)), _full((D_MODEL, MY_WIDTH)),
                  _full((1, 128)), _full((1, PIECE)), _full((1, PIECE)), _full((PIECE, PIECE)),
                  _full((128, FOX_HEADS * 128)), _full((128, FOX_HEADS * 128))],
        out_specs=[made_blk(3 * PIECE), made_blk(MY_WIDTH - GATES_OFF), made_blk(D_MODEL),
                   head_blk(128), head_blk(128), head_blk(HEAD_DIM), tile_blk, tile_blk],
        out_shape=[jax.ShapeDtypeStruct((s_len, 3 * PIECE), F32), jax.ShapeDtypeStruct((s_len, MY_WIDTH - GATES_OFF), F32),
                   jax.ShapeDtypeStruct((s_len, D_MODEL), BF16),
                   jax.ShapeDtypeStruct((FOX_HEADS, s_len, 128), BF16), jax.ShapeDtypeStruct((FOX_HEADS, s_len, 128), BF16),
                   jax.ShapeDtypeStruct((FOX_HEADS, s_len, HEAD_DIM), BF16), tiled, tiled],
        scratch_shapes=[pltpu.VMEM((TMM, kept), F32), pltpu.VMEM((TMM, kept), F32), pltpu.VMEM((1, 128), F32)],
        compiler_params=_params(("arbitrary",)),
    )(x, norm_g, w_my, bf_pad, gq, gk, seg, to_q, to_k)


def _mem_prep(mem, g, w_kv, gk):
    def body(mem_ref, g_ref, w_ref, gk_ref, km_ref, vm_ref):
        m = mem_ref[...]
        kv = _dot((m * _rms(m) * g_ref[...]).astype(BF16), w_ref[...])
        for h in range(MEM_HEADS):
            sl = slice(HEAD_DIM * h, HEAD_DIM * (h + 1))
            kh = kv[:, sl]
            km_ref[:, sl] = (kh * _rms(kh) * gk_ref[...]).astype(BF16)
        vm_ref[...] = kv[:, 256:512].astype(BF16)

    return pl.pallas_call(
        body,
        name="mem_prep",
        out_shape=(jax.ShapeDtypeStruct((N_MEM, 256), BF16),) * 2,
        in_specs=[pl.BlockSpec(memory_space=pltpu.VMEM)] * 4,
        out_specs=(pl.BlockSpec(memory_space=pltpu.VMEM),) * 2,
        compiler_params=pltpu.CompilerParams(vmem_limit_bytes=VMEM_LIMIT),
    )(mem, g, w_kv, gk)


def _causal_mask_t():
    row = lax.broadcasted_iota(jnp.int32, (TA, TA), 0)
    col = lax.broadcasted_iota(jnp.int32, (TA, TA), 1)
    return row <= col


def _fox_fwd(q_aug, k_aug, vt_aug, w_kv16, w_out16):
    s_len = q_aug.shape[1]
    n_tiles = s_len // TA
    hb = HB_FWD
    n_groups = FOX_HEADS // hb

    def body(q_ref, k_new, vt_new, kv16_ref, out16_ref, o_ref, st_ref, kv_all, out_all, k_ref, vt_ref, send_sems, recv_sems, local_sems):
        qi = pl.program_id(1)
        for h in range(hb):
            k_ref[h, pl.ds(pl.multiple_of(qi * TA, TA), TA), :] = k_new[h]
            vt_ref[h, qi] = vt_new[h, 0]
        remote, local = _row_block_exchange(kv16_ref, out16_ref, kv_all, out_all, send_sems, recv_sems, local_sems, gather=True)

        @pl.when((pl.program_id(0) == 0) & (qi == 0))
        def _():
            for cp in remote + local:
                cp.start()

        qs = [q_ref[h] for h in range(hb)]

        def step(t0, carry, masked, width):
            rows = pl.ds(pl.multiple_of(t0 * TA, TA), width * TA)
            scores = [_dot_nt(k_ref[h, rows, :], qs[h]) for h in range(hb)]
            soft = []
            for h in range(hb):
                m, _ = carry[h]
                s = jnp.where(_causal_mask_t(), scores[h], -1e30) if masked else scores[h]
                m_new = jnp.maximum(m, jnp.max(s, axis=0, keepdims=True))
                soft.append((m_new, jnp.exp(m - m_new), jnp.exp(s - m_new).astype(BF16)))
            out = []
            for h, (m_new, alpha, p) in enumerate(soft):
                acc = alpha * carry[h][1]
                for t in range(width):
                    acc = acc + _dot(vt_ref[h, t0 + t, 0:AUG_ROWS, :], p[t * TA:(t + 1) * TA])
                out.append((m_new, acc))
            return tuple(out)

        init = tuple((jnp.full((1, TA), -1e30, F32), jnp.zeros((AUG_ROWS, TA), F32)) for _ in range(hb))
        carry = lax.fori_loop(0, qi // 2, lambda j, cr: step(2 * j, cr, False, 2), init)
        carry = lax.fori_loop(2 * (qi // 2), qi, lambda j, cr: step(j, cr, False, 1), carry)
        carry = step(qi, carry, True, 1)
        for h in range(hb):
            m, acc = carry[h]
            l = acc[HEAD_DIM:HEAD_DIM + 1, :]
            lse = m + jnp.log(l)
            o_ref[h] = jnp.concatenate([acc[0:HEAD_DIM] / l, jnp.broadcast_to(lse, (HEAD_DIM, TA))], axis=0).T
            st_ref[h, 0] = jnp.broadcast_to(lse, (8, TA))

        @pl.when((pl.program_id(0) == n_groups - 1) & (qi == n_tiles - 1))
        def _():
            for cp in remote:
                cp.wait_recv()
            for cp in remote:
                cp.wait_send()
            for cp in local:
                cp.wait()

    hbm = pl.BlockSpec(memory_space=pl.ANY)
    return pl.pallas_call(
        body,
        name="fox_fwd",
        grid=(n_groups, n_tiles),
        in_specs=[pl.BlockSpec((hb, TA, 128), lambda g, i: (g, i, 0)), pl.BlockSpec((hb, TA, 128), lambda g, i: (g, i, 0)),
                  pl.BlockSpec((hb, 1, AUG_KEPT, TA), lambda g, i: (g, i, 0, 0)), hbm, hbm],
        out_specs=[pl.BlockSpec((hb, TA, 128), lambda g, i: (g, i, 0)), pl.BlockSpec((hb, 1, 8, TA), lambda g, i: (g, i, 0, 0)),
                   hbm, hbm],
        out_shape=[jax.ShapeDtypeStruct((FOX_HEADS, s_len, 128), F32), jax.ShapeDtypeStruct((FOX_HEADS, n_tiles, 8, TA), F32),
                   jax.ShapeDtypeStruct((D_MODEL, 512), BF16), jax.ShapeDtypeStruct((D_MODEL, D_MODEL), BF16)],
        scratch_shapes=[pltpu.VMEM((hb, s_len, 128), BF16), pltpu.VMEM((hb, n_tiles, AUG_KEPT, TA), BF16),
                        pltpu.SemaphoreType.DMA((14,)), pltpu.SemaphoreType.DMA((14,)), pltpu.SemaphoreType.DMA((2,))],
        compiler_params=_params(("arbitrary", "arbitrary")),
    )(q_aug, k_aug, vt_aug, w_kv16, w_out16)


def _lane_group(lane, a, b, c, d):
    return jnp.where(lane < 64, a, jnp.where(lane < 128, b, jnp.where(lane < 192, c, d)))


def _pool_count(row0):
    lane = lax.broadcasted_iota(jnp.int32, (TM, POOL_WIDTH), 1)
    t1 = row0 + lax.broadcasted_iota(jnp.int32, (TM, POOL_WIDTH), 0) + 1
    return 1.0 / jnp.minimum(t1, _lane_group(lane, 2, 4, 8, 16)).astype(F32), lane


def _pool_delta(u, halo, cnt, lane):
    xe = jnp.concatenate([halo, u], axis=0)
    s2 = xe + pltpu.roll(xe, 1, 0)
    s4 = s2 + pltpu.roll(s2, 2, 0)
    s8 = s4 + pltpu.roll(s4, 4, 0)
    s16 = s8 + pltpu.roll(s8, 8, 0)
    win = _lane_group(lane, s2[HALO:], s4[HALO:], s8[HALO:], s16[HALO:])
    return win * cnt - u


def _pool_delta_bwd(dd, dd_next, cnt, cnt_next, lane):
    ee = jnp.concatenate([dd * cnt, dd_next * cnt_next], axis=0)
    n = TM + HALO
    r2 = ee + pltpu.roll(ee, n - 1, 0)
    r4 = r2 + pltpu.roll(r2, n - 2, 0)
    r8 = r4 + pltpu.roll(r4, n - 4, 0)
    r16 = r8 + pltpu.roll(r8, n - 8, 0)
    return _lane_group(lane, r2[:TM], r4[:TM], r8[:TM], r16[:TM]) - dd


def _mem_attn(qm, gq, seg, km_ref, vm_ref):
    r = _head_rms(qm, seg)
    a = qm * r
    q16 = (a * gq * 0.125).astype(BF16)
    heads = []
    for h in range(MEM_HEADS):
        sl = slice(HEAD_DIM * h, HEAD_DIM * (h + 1))
        s = _dot_nt(q16[:, sl], km_ref[:, sl])
        e = jnp.exp(s - jnp.max(s, axis=-1, keepdims=True))
        p = e * (1.0 / jnp.sum(e, axis=-1, keepdims=True))
        heads.append((p, _dot(p.astype(BF16), vm_ref[:, sl])))
    return a, r, q16, heads


def _mem_attn_stages(qm, gq, seg, km_ref, vm_ref, result):
    r = _head_rms(qm, seg)
    a = qm * r
    q16 = (a * gq * 0.125).astype(BF16)
    yield
    head_slices = [slice(HEAD_DIM * h, HEAD_DIM * (h + 1)) for h in range(MEM_HEADS)]
    scores = [_dot_nt(q16[:, sl], km_ref[:, sl]) for sl in head_slices]
    yield
    heads = []
    for s, sl in zip(scores, head_slices):
        e = jnp.exp(s - jnp.max(s, axis=-1, keepdims=True))
        p = e * (1.0 / jnp.sum(e, axis=-1, keepdims=True))
        heads.append((p, _dot(p.astype(BF16), vm_ref[:, sl])))
    result.extend([a, r, q16, heads])
    yield


def _mix_out_loss(proj_a, proj_b, olse, km, vm, wp_bd, pool_scale, gq_m, seg, x, target, w_out):
    s_len = x.shape[0]
    n_blocks = s_len // TMM
    halves = TMM // TM

    def body(ua_ref, halo_ref, gb_ref, qm_ref, ol_ref, km_ref, vm_ref, wp_ref, sc_ref, gq_ref, seg_ref, x_ref, t_ref, w_ref,
             dout_ref, dmix_ref, dw16_ref, loss_ref, mixed_even, mixed_odd, d16_ref, dw_ref):
        s = pl.program_id(0)
        chunks = [slice(c0, c0 + 256) for c0 in range(0, D_MODEL, 256)]

        def matmuls(mixed_ref):
            for cols in chunks:
                err = x_ref[:, cols] + _dot(mixed_ref[...], w_ref[:, cols]) - t_ref[:, cols]
                loss_ref[...] += (0.5 / D_MODEL) * jnp.sum(err * err)
                d_out = err / float(D_MODEL)
                dout_ref[:, cols] = d_out
                d16_ref[:, cols] = d_out.astype(BF16)
                yield
            for cols in chunks:
                dmix_ref[:, cols] = _dot_nt(d16_ref[...], w_ref[cols, :])
                yield
            for cols in chunks:
                dw_ref[:, cols] += _dot_tn(mixed_ref[...], d16_ref[:, cols])
                yield

        def concatenation(mixed_ref):
            for half in range(halves):
                r0 = TM * half
                rows = slice(r0, r0 + TM)
                u = ua_ref[rows, 0:256]
                g_a = ua_ref[rows, 256:512]
                halo = jnp.where(s > 0, halo_ref[...], 0.0) if half == 0 else ua_ref[r0 - HALO:r0, 0:256]
                cnt, lane = _pool_count(s * TMM + r0)
                d = _pool_delta(u, halo, cnt, lane).astype(BF16)
                y_a = _dot(d, wp_ref[...]) * sc_ref[...]
                mixed_ref[rows, 0:256] = (y_a * (g_a * _sigmoid(g_a))).astype(BF16)
                yield
                g_b = gb_ref[rows, :]
                y_b = jnp.concatenate([ol_ref[h, rows, 0:HEAD_DIM] for h in range(FOX_HEADS)], axis=1)
                mixed_ref[rows, 256:768] = (y_b * (g_b * _sigmoid(g_b))).astype(BF16)
                yield
                attn = []
                stages = _mem_attn_stages(qm_ref[rows, 0:256], gq_ref[...], seg_ref[0:256, 0:256], km_ref, vm_ref, attn)
                next(stages)
                yield
                next(stages)
                yield
                next(stages)
                g_m = qm_ref[rows, 256:512]
                y_m = jnp.concatenate([y for _, y in attn[3]], axis=1)
                mixed_ref[rows, 768:1024] = (y_m * (g_m * _sigmoid(g_m))).astype(BF16)
                yield

        multiplying = lambda mixed_ref: (matmuls(mixed_ref), 3 * len(chunks))
        building = lambda mixed_ref: (concatenation(mixed_ref), 4 * halves)

        @pl.when(s == 0)
        def _():
            dw_ref[...] = jnp.zeros((D_MODEL, D_MODEL), F32)
            loss_ref[...] = jnp.zeros((8, 128), F32)
            _alternate(building(mixed_even))

        @pl.when((s > 0) & (s < n_blocks) & (s % 2 == 1))
        def _():
            _alternate(multiplying(mixed_even), building(mixed_odd))

        @pl.when((s > 0) & (s < n_blocks) & (s % 2 == 0))
        def _():
            _alternate(multiplying(mixed_odd), building(mixed_even))

        @pl.when(s == n_blocks)
        def _():
            _alternate(multiplying(mixed_odd if n_blocks % 2 == 0 else mixed_even))
            dw16_ref[...] = dw_ref[...].astype(BF16)

    build = lambda s: jnp.minimum(s, n_blocks - 1)
    use = lambda s: jnp.maximum(s - 1, 0)
    piece = lambda j: pl.BlockSpec((TMM, PIECE), lambda s: (build(s), j))
    halo_blk = pl.BlockSpec((HALO, 256), lambda s: (jnp.maximum(build(s) * (TMM // HALO) - 1, 0), 0))
    row = pl.BlockSpec((TMM, D_MODEL), lambda s: (use(s), 0))
    return pl.pallas_call(
        body,
        name="mix_out_loss",
        grid=(n_blocks + 1,),
        in_specs=[piece(0), halo_blk, piece(0), piece(1), pl.BlockSpec((FOX_HEADS, TMM, 128), lambda s: (0, build(s), 0)),
                  _full((N_MEM, 256)), _full((N_MEM, 256)), _full((256, 256)), _full((1, 256)), _full((1, 256)),
                  _full((PIECE, PIECE)), row, row, _full((D_MODEL, D_MODEL))],
        out_specs=[row, row, _full((D_MODEL, D_MODEL)), _full((8, 128))],
        out_shape=[jax.ShapeDtypeStruct((s_len, D_MODEL), F32), jax.ShapeDtypeStruct((s_len, D_MODEL), F32),
                   jax.ShapeDtypeStruct((D_MODEL, D_MODEL), BF16), jax.ShapeDtypeStruct((8, 128), F32)],
        scratch_shapes=[pltpu.VMEM((TMM, D_MODEL), BF16), pltpu.VMEM((TMM, D_MODEL), BF16), pltpu.VMEM((TMM, D_MODEL), BF16),
                        pltpu.VMEM((D_MODEL, D_MODEL), F32)],
        compiler_params=_params(("arbitrary",)),
    )(proj_a, proj_a, proj_b, proj_b, olse, km, vm, wp_bd, pool_scale, gq_m, seg, x, target, w_out)


def _silu_pair(g):
    s = _sigmoid(g)
    return g * s, s * (1.0 + g * (1.0 - s))


def _mix_bwd(proj_a, proj_b, olse, d_mixed, km, vm, wp_bd, pool_scale, gq_m, seg):
    s_len = proj_a.shape[0]
    n_tiles = s_len // TM

    def body(ua_ref, halo_ref, ga_next_ref, gb_ref, qm_ref, ol_ref, dm_ref, dm_next_ref, km_ref, vm_ref, wp_ref, sc_ref, gq_ref,
             seg_ref, dua_ref, dgb_ref, dqm_ref, do_ref, dl_ref, dwp_ref, dsc_ref, dkm_ref, dvm_ref, dgq_ref):
        i = pl.program_id(0)

        @pl.when(i == 0)
        def _():
            dwp_ref[...] = jnp.zeros((256, 256), F32)
            dsc_ref[...] = jnp.zeros((1, 256), F32)
            dkm_ref[...] = jnp.zeros((N_MEM, 256), F32)
            dvm_ref[...] = jnp.zeros((N_MEM, 256), F32)
            dgq_ref[...] = jnp.zeros((1, HEAD_DIM), F32)

        def pooling_mixer():
            u = ua_ref[:, 0:256]
            g_a = ua_ref[:, 256:512]
            halo = jnp.where(i > 0, halo_ref[...], 0.0)
            cnt, lane = _pool_count(i * TM)
            d16 = _pool_delta(u, halo, cnt, lane).astype(BF16)
            t = _dot(d16, wp_ref[...])
            yield
            y_a = t * sc_ref[...]
            silu_a, dsilu_a = _silu_pair(g_a)
            dm_a = dm_ref[:, 0:256]
            dy_a = dm_a * silu_a
            dsc_ref[...] += jnp.sum(dy_a * t, axis=0, keepdims=True)
            dt16 = (dy_a * sc_ref[...]).astype(BF16)
            dwp_ref[...] += _dot_tn(d16, dt16)
            dd = _dot_nt(dt16, wp_ref[...])
            dua_ref[:, 256:512] = (dm_a * y_a * dsilu_a).astype(BF16)
            yield
            g_next = ga_next_ref[...]
            dt_next = (dm_next_ref[...] * (g_next * _sigmoid(g_next)) * sc_ref[...]).astype(BF16)
            dd_next = jnp.where(i < n_tiles - 1, _dot_nt(dt_next, wp_ref[...]), 0.0)
            cnt_next = _pool_count((i + 1) * TM)[0][0:HALO]
            dua_ref[:, 0:256] = _pool_delta_bwd(dd, dd_next, cnt, cnt_next, lane).astype(BF16)
            yield

        def output_gate():
            silu_b, dsilu_b = _silu_pair(gb_ref[...])
            dm_b = dm_ref[:, 256:768]
            o_all = jnp.concatenate([ol_ref[h][:, 0:HEAD_DIM] for h in range(FOX_HEADS)], axis=1)
            d_o = dm_b * silu_b
            dgb_ref[...] = (dm_b * o_all * dsilu_b).astype(BF16)
            d_o16 = d_o.astype(BF16)
            for h in range(FOX_HEADS):
                do_ref[h] = d_o16[:, HEAD_DIM * h:HEAD_DIM * (h + 1)]
            yield
            prod = d_o * o_all
            hi = prod.astype(BF16)
            lo = (prod - hi.astype(F32)).astype(BF16)
            head_of_lane = lax.broadcasted_iota(jnp.int32, (FOX_HEADS, PIECE), 1) // HEAD_DIM
            pick = jnp.where(head_of_lane == lax.broadcasted_iota(jnp.int32, (FOX_HEADS, PIECE), 0), 1.0, 0.0).astype(BF16)
            delta = _dot_nt(pick, hi) + _dot_nt(pick, lo)
            for h in range(FOX_HEADS):
                dl_ref[h, 0] = jnp.broadcast_to(delta[h:h + 1, :], (8, TM))
            yield

        def memory_attention():
            seg = seg_ref[0:256, 0:256]
            a, r, q16, heads = _mem_attn(qm_ref[:, 0:256], gq_ref[...], seg, km_ref, vm_ref)
            yield
            silu_m, dsilu_m = _silu_pair(qm_ref[:, 256:512])
            dm_m = dm_ref[:, 768:1024]
            y_m = jnp.concatenate([y for _, y in heads], axis=1)
            dqm_ref[:, 256:512] = (dm_m * y_m * dsilu_m).astype(BF16)
            dy16_all = (dm_m * silu_m).astype(BF16)
            d_scores = []
            head_slices = [slice(HEAD_DIM * h, HEAD_DIM * (h + 1)) for h in range(MEM_HEADS)]
            dps = [_dot_nt(dy16_all[:, sl], vm_ref[:, sl]) for sl in head_slices]
            for h, sl in enumerate(head_slices):
                dvm_ref[:, sl] += _dot_tn(heads[h][0].astype(BF16), dy16_all[:, sl])
            yield
            for h, sl in enumerate(head_slices):
                p = heads[h][0]
                dp = dps[h]
                ds16 = (p * (dp - jnp.sum(dp * p, axis=-1, keepdims=True))).astype(BF16)
                dkm_ref[:, sl] += _dot_tn(ds16, q16[:, sl])
                d_scores.append(_dot(ds16, km_ref[:, sl]))
                yield
            dq, dg_rows = _head_rms_bwd(a, r, gq_ref[...], jnp.concatenate(d_scores, axis=1) * 0.125, seg)
            dqm_ref[:, 0:256] = dq.astype(BF16)
            dgq_ref[...] += _fold_heads(jnp.sum(dg_rows, axis=0, keepdims=True), MEM_HEADS)
            yield

        _alternate((pooling_mixer(), 3))
        _alternate((memory_attention(), 3 + MEM_HEADS), (output_gate(), 2))

    n_halo = s_len // HALO
    piece = lambda j: pl.BlockSpec((TM, PIECE), lambda i: (i, j))
    before = pl.BlockSpec((HALO, 256), lambda i: (jnp.maximum(i * (TM // HALO) - 1, 0), 0))
    after = lambda j: pl.BlockSpec((HALO, 256), lambda i: (jnp.minimum((i + 1) * (TM // HALO), n_halo - 1), j))
    row = pl.BlockSpec((TM, D_MODEL), lambda i: (i, 0))
    out_piece = pl.BlockSpec((TM, PIECE), lambda i: (i, 0))
    return pl.pallas_call(
        body,
        name="mix_bwd",
        grid=(n_tiles,),
        in_specs=[piece(0), before, after(1), piece(0), piece(1), pl.BlockSpec((FOX_HEADS, TM, 128), lambda i: (0, i, 0)),
                  row, after(0), _full((N_MEM, 256)), _full((N_MEM, 256)), _full((256, 256)), _full((1, 256)), _full((1, 256)),
                  _full((PIECE, PIECE))],
        out_specs=[out_piece, out_piece, out_piece, pl.BlockSpec((FOX_HEADS, TM, HEAD_DIM), lambda i: (0, i, 0)),
                   pl.BlockSpec((FOX_HEADS, 1, 8, TM), lambda i: (0, i, 0, 0)),
                   _full((256, 256)), _full((1, 256)), _full((N_MEM, 256)), _full((N_MEM, 256)), _full((1, HEAD_DIM))],
        out_shape=[jax.ShapeDtypeStruct((s_len, PIECE), BF16)] * 3 + [
            jax.ShapeDtypeStruct((FOX_HEADS, s_len, HEAD_DIM), BF16),
            jax.ShapeDtypeStruct((FOX_HEADS, n_tiles, 8, TM), F32),
            jax.ShapeDtypeStruct((256, 256), F32), jax.ShapeDtypeStruct((1, 256), F32),
            jax.ShapeDtypeStruct((N_MEM, 256), F32), jax.ShapeDtypeStruct((N_MEM, 256), F32),
            jax.ShapeDtypeStruct((1, HEAD_DIM), F32)],
        compiler_params=_params(("arbitrary",)),
    )(proj_a, proj_a, proj_a, proj_b, proj_b, olse, d_mixed, d_mixed, km, vm, wp_bd, pool_scale, gq_m, seg)


def _fox_bwd(q_aug, k_aug, kt_aug, v_h, d_o, lse_rows, delta_rows, dw_kv16, dw_out16):
    s_len = q_aug.shape[1]
    n_tiles = s_len // TA
    n_groups = FOX_HEADS // HB

    def body(q_hbm, k_ref, kt_ref, v_ref, do_hbm, st_ref, dl_ref, dkv16_ref, dout16_ref, dq_ref, dk_ref, dv_ref, land_kv, land_out,
             dqt_ref, q_ref, do_ref, send_sems, recv_sems, local_sems, tile_sems):
        j = pl.program_id(1)
        remote, local = _row_block_exchange(dkv16_ref, dout16_ref, land_kv, land_out, send_sems, recv_sems, local_sems, gather=False)

        def tile_loads(i):
            rows = pl.ds(pl.multiple_of(i * TA, TA), TA)
            return [pltpu.make_async_copy(q_hbm.at[:, rows, :], q_ref.at[:, rows, :], tile_sems.at[2 * i]),
                    pltpu.make_async_copy(do_hbm.at[:, rows, :], do_ref.at[:, rows, :], tile_sems.at[2 * i + 1])]

        @pl.when((pl.program_id(0) == 0) & (j == 0))
        def _():
            for i in range(n_tiles):
                for cp in tile_loads(i):
                    cp.start()
            for cp in remote + local:
                cp.start()

        @pl.when(j == 0)
        def _():
            dqt_ref[...] = jnp.zeros((HB, n_tiles, AUG_ROWS, TA), F32)

        ks = [k_ref[h] for h in range(HB)]
        kts = [kt_ref[h, 0, 0:AUG_ROWS, :] for h in range(HB)]
        vs = [v_ref[h] for h in range(HB)]

        def pair(i0, acc, masked, width):
            @pl.when(j == 0)
            def _():
                for t in range(width):
                    for cp in tile_loads(i0 + t):
                        cp.wait()

            rows = pl.ds(pl.multiple_of(i0 * TA, TA), width * TA)
            qs = [q_ref[h, rows, :] for h in range(HB)]
            d_os = [do_ref[h, rows, :] for h in range(HB)]
            row_stat = lambda ref, h: jnp.concatenate([ref[h, i0 + t, 0:1, :] for t in range(width)], axis=1)
            scores = [(_dot_nt(ks[h], qs[h]), _dot_nt(vs[h], d_os[h])) for h in range(HB)]
            probs = []
            for h in range(HB):
                s, dp = scores[h]
                if masked:
                    s = jnp.where(_causal_mask_t(), s, -1e30)
                p = jnp.exp(s - row_stat(st_ref, h))
                probs.append((p.astype(BF16), (p * (dp - row_stat(dl_ref, h))).astype(BF16)))
            out = []
            for h in range(HB):
                p16, ds16 = probs[h]
                dqt = _dot(kts[h], ds16)
                for t in range(width):
                    dqt_ref[h, i0 + t] += dqt[:, t * TA:(t + 1) * TA]
                out.append((acc[h][0] + _dot(ds16, qs[h]), acc[h][1] + _dot(p16, d_os[h])))
            return tuple(out)

        zero = tuple((jnp.zeros((TA, 128), F32), jnp.zeros((TA, HEAD_DIM), F32)) for _ in range(HB))
        acc = pair(j, zero, True, 1)
        odd = (n_tiles - 1 - j) % 2
        acc = lax.fori_loop(j + 1, j + 1 + odd, lambda i, a: pair(i, a, False, 1), acc)
        acc = lax.fori_loop(0, (n_tiles - 1 - j) // 2, lambda t, a: pair(j + 1 + odd + 2 * t, a, False, 2), acc)
        pad = jnp.zeros((128 - AUG_ROWS, TA), F32)
        for h in range(HB):
            dk_ref[h] = acc[h][0]
            dv_ref[h] = acc[h][1].astype(BF16)
            dq_ref[h] = jnp.concatenate([dqt_ref[h, j], pad], axis=0).T

        @pl.when((pl.program_id(0) == n_groups - 1) & (j == n_tiles - 1))
        def _():
            for cp in remote:
                cp.wait_recv()
            for cp in remote:
                cp.wait_send()
            for cp in local:
                cp.wait()

    assert n_groups == 1
    resident = pl.BlockSpec(memory_space=pltpu.VMEM)
    rows_blk = lambda r: resident
    tile = lambda w: pl.BlockSpec((HB, TA, w), lambda g, j: (g, j, 0))
    hbm = pl.BlockSpec(memory_space=pl.ANY)
    return pl.pallas_call(
        body,
        name="fox_bwd",
        grid=(n_groups, n_tiles),
        in_specs=[hbm, tile(128), pl.BlockSpec((HB, 1, AUG_KEPT, TA), lambda g, j: (g, j, 0, 0)), tile(HEAD_DIM),
                  hbm, rows_blk(8), rows_blk(8), hbm, hbm],
        out_specs=[tile(128), tile(128), tile(HEAD_DIM), hbm, hbm],
        out_shape=[jax.ShapeDtypeStruct((FOX_HEADS, s_len, 128), F32), jax.ShapeDtypeStruct((FOX_HEADS, s_len, 128), F32),
                   jax.ShapeDtypeStruct((FOX_HEADS, s_len, HEAD_DIM), BF16),
                   jax.ShapeDtypeStruct((N_DEV, 128, 512), BF16), jax.ShapeDtypeStruct((N_DEV, 128, D_MODEL), BF16)],
        scratch_shapes=[pltpu.VMEM((HB, n_tiles, AUG_ROWS, TA), F32),
                        pltpu.VMEM((HB, s_len, 128), BF16), pltpu.VMEM((HB, s_len, HEAD_DIM), BF16),
                        pltpu.SemaphoreType.DMA((14,)), pltpu.SemaphoreType.DMA((14,)), pltpu.SemaphoreType.DMA((2,)),
                        pltpu.SemaphoreType.DMA((2 * n_tiles,))],
        compiler_params=_params(("arbitrary", "arbitrary")),
    )(q_aug, k_aug, kt_aug, v_h, d_o, lse_rows, delta_rows, dw_kv16, dw_out16)


def _fox_post(proj_a, proj_b, bf_pad, gq, gk, seg, dq_aug, dk_aug, dv_h):
    s_len = proj_a.shape[0]
    n_tiles = s_len // TM

    def body(q_ref, k_ref, f_ref, bf_ref, gq_ref, gk_ref, seg_ref, dqa_ref, dka_ref, dvh_ref,
             dq_ref, dk_ref, dv_ref, df_ref, dgq_ref, dgk_ref, dbf_ref, carry_ref):
        @pl.when(pl.program_id(0) == 0)
        def _():
            carry_ref[...] = jnp.zeros((1, 128), F32)
            dgq_ref[...] = jnp.zeros((1, HEAD_DIM), F32)
            dgk_ref[...] = jnp.zeros((1, HEAD_DIM), F32)
            dbf_ref[...] = jnp.zeros((1, 128), F32)

        lane = lax.broadcasted_iota(jnp.int32, (TM, 128), 1)
        seg = seg_ref[...]
        def head_norms():
            q_all = q_ref[...]
            k_all = k_ref[...]
            rq = _head_rms(q_all, seg)
            rk = _head_rms(k_all, seg)
            yield
            dy_q = jnp.concatenate([dqa_ref[h, :, 0:HEAD_DIM] for h in range(FOX_HEADS)], axis=1) * 0.125
            dq, dgq_rows = _head_rms_bwd(q_all * rq, rq, gq_ref[...], dy_q, seg)
            dq_ref[...] = dq.astype(BF16)
            dgq_ref[...] += _fold_heads(jnp.sum(dgq_rows, axis=0, keepdims=True), FOX_HEADS)
            yield
            dy_k = jnp.concatenate([dka_ref[h, :, 0:HEAD_DIM] for h in range(FOX_HEADS)], axis=1)
            dk, dgk_rows = _head_rms_bwd(k_all * rk, rk, gk_ref[...], dy_k, seg)
            dk_ref[...] = dk.astype(BF16)
            dgk_ref[...] += _fold_heads(jnp.sum(dgk_rows, axis=0, keepdims=True), FOX_HEADS)
            dv_ref[...] = jnp.concatenate([dvh_ref[h] for h in range(FOX_HEADS)], axis=1)
            yield

        def forget_gates():
            d_cum = jnp.zeros((TM, 128), F32)
            for h in range(FOX_HEADS):
                d_cum = jnp.where(lane == h, dqa_ref[h, :, 64:65] - dka_ref[h, :, 67:68], d_cum)
            row = lax.broadcasted_iota(jnp.int32, (TM, TM), 0)
            col = lax.broadcasted_iota(jnp.int32, (TM, TM), 1)
            tri = jnp.where(col >= row, 1.0, 0.0).astype(BF16)
            hi, mid, lo = _split3(d_cum)
            d_logf = _dot(tri, hi) + _dot(tri, mid) + _dot(tri, lo) + carry_ref[...]
            yield
            carry_ref[...] = d_logf[0:1, :]
            z = f_ref[...] + bf_ref[...]
            d_f = jnp.where(lane < FOX_HEADS, d_logf / (1.0 + jnp.exp(z)), 0.0)
            df_ref[...] = d_f.astype(BF16)
            dbf_ref[...] += jnp.sum(d_f, axis=0, keepdims=True)
            yield

        _alternate((head_norms(), 3), (forget_gates(), 2))

    rev = lambda i: n_tiles - 1 - i
    col_blk = lambda j: pl.BlockSpec((TM, PIECE), lambda i: (rev(i), j))
    head_blk = lambda w: pl.BlockSpec((FOX_HEADS, TM, w), lambda i: (0, rev(i), 0))
    out_piece = pl.BlockSpec((TM, PIECE), lambda i: (rev(i), 0))
    return pl.pallas_call(
        body,
        name="fox_post",
        grid=(n_tiles,),
        in_specs=[col_blk(1), col_blk(2), pl.BlockSpec((TM, 128), lambda i: (rev(i), (MY_F_OFF - GATES_OFF) // 128)),
                  _full((1, 128)), _full((1, PIECE)), _full((1, PIECE)), _full((PIECE, PIECE)),
                  head_blk(128), head_blk(128), head_blk(HEAD_DIM)],
        out_specs=[out_piece, out_piece, out_piece, pl.BlockSpec((TM, 128), lambda i: (rev(i), 0)),
                   _full((1, HEAD_DIM)), _full((1, HEAD_DIM)), _full((1, 128))],
        out_shape=[jax.ShapeDtypeStruct((s_len, PIECE), BF16)] * 3 + [
            jax.ShapeDtypeStruct((s_len, 128), BF16), jax.ShapeDtypeStruct((1, HEAD_DIM), F32),
            jax.ShapeDtypeStruct((1, HEAD_DIM), F32), jax.ShapeDtypeStruct((1, 128), F32)],
        scratch_shapes=[pltpu.VMEM((1, 128), F32)],
        compiler_params=_params(("arbitrary",)),
    )(proj_a, proj_a, proj_b, bf_pad, gq, gk, seg, dq_aug, dk_aug, dv_h)


def _mem_bwd(mem, g, w_kv, gk, dkm, dvm):
    def body(mem_ref, g_ref, w_ref, gk_ref, dkm_ref, dvm_ref, dw_ref, dg_ref, dgk_ref, dkv_ref):
        m = mem_ref[...]
        r = _rms(m)
        a = m * r
        mn16 = (a * g_ref[...]).astype(BF16)
        kv = _dot(mn16, w_ref[...])
        dgk = jnp.zeros((N_MEM, HEAD_DIM), F32)
        for h in range(MEM_HEADS):
            sl = slice(HEAD_DIM * h, HEAD_DIM * (h + 1))
            kh = kv[:, sl]
            rk = _rms(kh)
            dk, dg_rows = _rms_bwd(kh * rk, rk, gk_ref[...], dkm_ref[:, sl])
            dkv_ref[:, sl] = dk.astype(BF16)
            dgk = dgk + dg_rows
        dkv_ref[:, 256:512] = dvm_ref[...].astype(BF16)
        dgk_ref[...] = jnp.sum(dgk, axis=0, keepdims=True)
        dkv16 = dkv_ref[...]
        dw_ref[...] = _dot_tn(mn16, dkv16).astype(BF16)
        dg_ref[...] = jnp.sum(_dot_nt(dkv16, w_ref[...]) * a, axis=0, keepdims=True)

    return pl.pallas_call(
        body,
        name="mem_bwd",
        out_shape=(jax.ShapeDtypeStruct((D_MODEL, 512), BF16), jax.ShapeDtypeStruct((1, D_MODEL), F32),
                   jax.ShapeDtypeStruct((1, HEAD_DIM), F32)),
        in_specs=[pl.BlockSpec(memory_space=pltpu.VMEM)] * 6,
        out_specs=(pl.BlockSpec(memory_space=pltpu.VMEM),) * 3,
        scratch_shapes=[pltpu.VMEM((N_MEM, 512), BF16)],
        compiler_params=pltpu.CompilerParams(vmem_limit_bytes=VMEM_LIMIT),
    )(mem, g, w_kv, gk, dkm, dvm)


def _grad_x_exchange(pieces, d_f, w_my, x, norm_g, d_out, dw_in, land_kv, land_out, d_mem_g, d_scale, d_bf, d_gq, d_gk, d_gqm,
                     d_gkm, dwp_bd, loss_blk):
    s_len = x.shape[0]
    n_steps = s_len // TM
    step2, step3 = n_steps // 4, (11 * n_steps) // 16
    half = D_MODEL // 2

    def body(p0, p1, p2, p3, p4, p5, df_ref, x_ref, dout_ref, w_ref, g_ref, in_ref, lkv_ref, lout_ref,
             dmg, dsc, dbf, dgq, dgk, dgqm, dgkm, dwp, loss_ref,
             gx_ref, rin_ref, rkv_ref, rout_ref, sred_ref,
             dng, land1, send2a, send2b, keep2a, keep2b, land2a, land2b, send3a, send3b, land3a, land3b, sb_ref, sland,
             own4, lkv_v, lout_v, send_sems, recv_sems, load_sems):
        i = pl.program_id(0)
        x_, y_, c_, me = _my_place()
        sibling, x_nbr, y_nbr = (x_, y_, 1 - c_), (1 - x_, y_, c_), (x_, 1 - y_, c_)
        loads = [pltpu.make_async_copy(in_ref.at[2 * k + c_], own4.at[k], load_sems.at[k]) for k in range(4)]
        loads += [pltpu.make_async_copy(lkv_ref, lkv_v, load_sems.at[4]), pltpu.make_async_copy(lout_ref, lout_v, load_sems.at[5])]

        def rcopy(src, dst, sem, to):
            return pltpu.make_async_remote_copy(src_ref=src, dst_ref=dst, send_sem=send_sems.at[sem], recv_sem=recv_sems.at[sem],
                                                device_id=to, device_id_type=MESH)

        early_rows, late_rows = pl.ds(8, SB_ROWS - 8), pl.ds(0, 8)
        small_early, small_late = [], []
        for k in range(1, N_DEV):
            peer = (x_ ^ (k >> 2), y_ ^ ((k >> 1) & 1), c_ ^ (k & 1))
            small_early.append(rcopy(sb_ref.at[early_rows], sland.at[me, early_rows], k - 1, peer))
            small_late.append(rcopy(sb_ref.at[late_rows], sland.at[me, late_rows], 16 + k, peer))
        small = small_early + small_late
        stage1 = [rcopy(in_ref.at[2 * k + 1 - c_], land1.at[k], 7 + k, sibling) for k in range(4)]
        stage2 = [rcopy(send2a.at[j], land2a.at[j], 11 + j, x_nbr) for j in range(2)]
        stage2 += [rcopy(send2b.at[j], land2b.at[j], 13 + j, y_nbr) for j in range(2)]
        stage3 = [rcopy(send3a, land3a, 15, y_nbr), rcopy(send3b, land3b, 16, x_nbr)]
        top, bot = slice(0, half), slice(half, D_MODEL)

        @pl.when(i == 0)
        def _():
            dng[...] = jnp.zeros((1, D_MODEL), F32)
            for cp in stage1 + loads:
                cp.start()
            sb_ref[...] = jnp.zeros((SB_ROWS, SB_W), F32)
            sb_ref[SB_POOL_SCALE:SB_POOL_SCALE + 1, :] = dsc[...]
            sb_ref[SB_B_F:SB_B_F + 1, 0:128] = dbf[...]
            for row, ref in ((SB_FOX_Q, dgq), (SB_FOX_K, dgk), (SB_MEM_Q, dgqm), (SB_MEM_K, dgkm)):
                sb_ref[row:row + 1, 0:HEAD_DIM] = ref[...]
            sb_ref[SB_LOSS:SB_LOSS + 1, 0:128] = loss_ref[0:1, :]
            for grp in range(4):
                sl = slice(64 * grp, 64 * (grp + 1))
                sb_ref[SB_W_POOL:SB_W_POOL + 64, sl] = dwp[sl, sl]
            for cp in small_early:
                cp.start()

        @pl.when(i == step2)
        def _():
            for cp in stage1:
                cp.wait_recv()
            for cp in loads[0:4]:
                cp.wait()

            def chip_sum(k, cols):
                return own4[k, :, cols].astype(F32) + land1[k, :, cols].astype(F32)

            for j in range(2):
                send2a[j] = chip_sum(2 * (1 - x_) + j, top).astype(BF16)
                keep2a[j] = chip_sum(2 * x_ + j, top)
                send2b[j] = chip_sum(2 * j + 1 - y_, bot).astype(BF16)
                keep2b[j] = chip_sum(2 * j + y_, bot)
            for cp in stage2:
                cp.start()

        @pl.when(i == step3)
        def _():
            for cp in stage2:
                cp.wait_recv()
            for j in range(2):
                keep2a[j] = keep2a[j] + land2a[j].astype(F32)
                keep2b[j] = keep2b[j] + land2b[j].astype(F32)
            send3a[...] = keep2a[1 - y_].astype(BF16)
            send3b[...] = keep2b[1 - x_].astype(BF16)
            for cp in stage3:
                cp.start()

        dh = _dot_nt(df_ref[...], w_ref[:, MY_F_OFF:MY_WIDTH])
        for j, p in enumerate((p0, p1, p2, p3, p4, p5)):
            dh = dh + _dot_nt(p[...], w_ref[:, PIECE * j:PIECE * (j + 1)])
        xv = x_ref[...]
        r = _rms(xv)
        dx, dg_rows = _rms_bwd(xv * r, r, g_ref[...], dh)
        gx_ref[...] = dout_ref[...] + dx
        dng[...] += jnp.sum(dg_rows, axis=0, keepdims=True)

        @pl.when(i == n_steps - 1)
        def _():
            for k in range(4):
                sb_ref[SB_NORM_G + k:SB_NORM_G + k + 1, :] = dng[:, SB_W * k:SB_W * (k + 1)]
                sb_ref[SB_MEM_NORM_G + k:SB_MEM_NORM_G + k + 1, :] = dmg[:, SB_W * k:SB_W * (k + 1)]
            for cp in small_late:
                cp.start()
            sland[me] = sb_ref[...]
            for cp in stage3:
                cp.wait_recv()
            rin_ref[:, top] = keep2a[y_] + land3a[...].astype(F32)
            rin_ref[:, bot] = keep2b[x_] + land3b[...].astype(F32)
            for cp in small:
                cp.wait_recv()
            for cp in small + stage1 + stage2 + stage3:
                cp.wait_send()
            for cp in loads[4:6]:
                cp.wait()
            for land, red in ((lkv_v, rkv_ref), (lout_v, rout_ref), (sland, sred_ref)):
                acc = land[0].astype(F32)
                for d in range(1, N_DEV):
                    acc = acc + land[d].astype(F32)
                red[...] = acc

    piece = pl.BlockSpec((TM, PIECE), lambda i: (i, 0))
    row = pl.BlockSpec((TM, D_MODEL), lambda i: (i, 0))
    vmem = pl.BlockSpec(memory_space=pltpu.VMEM)
    half_chunk = lambda n, dt: pltpu.VMEM((n, CHUNK_ROWS, half), dt)
    whole = (w_my, norm_g, dw_in, land_kv, land_out, d_mem_g, d_scale, d_bf, d_gq, d_gk, d_gqm, d_gkm, dwp_bd, loss_blk)
    return pl.pallas_call(
        body,
        name="grad_x_exchange",
        grid=(n_steps,),
        in_specs=[piece] * 6 + [pl.BlockSpec((TM, 128), lambda i: (i, 0)), row, row, vmem, vmem]
        + [pl.BlockSpec(memory_space=pl.ANY)] * 3 + [vmem] * (len(whole) - 5),
        out_specs=[row, vmem, vmem, vmem, vmem],
        out_shape=[jax.ShapeDtypeStruct((s_len, D_MODEL), F32), jax.ShapeDtypeStruct((CHUNK_ROWS, D_MODEL), F32),
                   jax.ShapeDtypeStruct((128, 512), F32), jax.ShapeDtypeStruct((128, D_MODEL), F32),
                   jax.ShapeDtypeStruct((SB_ROWS, SB_W), F32)],
        scratch_shapes=[
            pltpu.VMEM((1, D_MODEL), F32),
            pltpu.VMEM((4, CHUNK_ROWS, D_MODEL), BF16),
            half_chunk(2, BF16), half_chunk(2, BF16), half_chunk(2, F32), half_chunk(2, F32), half_chunk(2, BF16), half_chunk(2, BF16),
            pltpu.VMEM((CHUNK_ROWS, half), BF16), pltpu.VMEM((CHUNK_ROWS, half), BF16),
            pltpu.VMEM((CHUNK_ROWS, half), BF16), pltpu.VMEM((CHUNK_ROWS, half), BF16),
            pltpu.VMEM((SB_ROWS, SB_W), F32),
            pltpu.VMEM((N_DEV, SB_ROWS, SB_W), F32),
            pltpu.VMEM((4, CHUNK_ROWS, D_MODEL), BF16),
            pltpu.VMEM((N_DEV, 128, 512), BF16),
            pltpu.VMEM((N_DEV, 128, D_MODEL), BF16),
            pltpu.SemaphoreType.DMA((24,)),
            pltpu.SemaphoreType.DMA((24,)),
            pltpu.SemaphoreType.DMA((6,)),
        ],
        compiler_params=_params(("arbitrary",)),
    )(*pieces, d_f, x, d_out, *whole)


def _grad_w_in(h16, pieces, d_f):
    s_len = h16.shape[0]
    tk = 512

    def body(h_ref, p0, p1, p2, p3, p4, p5, df_ref, out_ref, dw_ref):
        @pl.when(pl.program_id(0) == 0)
        def _():
            dw_ref[...] = jnp.zeros((D_MODEL, MY_WIDTH), F32)

        h = h_ref[...]
        for j, p in enumerate((p0, p1, p2, p3, p4, p5)):
            dw_ref[:, PIECE * j:PIECE * (j + 1)] += _dot_tn(h, p[...])
        dw_ref[:, MY_F_OFF:MY_WIDTH] += _dot_tn(h, df_ref[...])

        @pl.when(pl.program_id(0) == pl.num_programs(0) - 1)
        def _():
            for d in range(N_DEV):
                parts = [dw_ref[:, start:start + width] for start, width in _chunk_segments(d)]
                parts.append(jnp.zeros((D_MODEL, 512 - IN_CHUNK), F32))
                out_ref[d] = jnp.concatenate(parts, axis=1).T[0:CHUNK_ROWS].astype(BF16)

    piece = pl.BlockSpec((tk, PIECE), lambda i: (i, 0))
    return pl.pallas_call(
        body,
        name="grad_w_in",
        grid=(s_len // tk,),
        in_specs=[pl.BlockSpec((tk, D_MODEL), lambda i: (i, 0))] + [piece] * 6 + [pl.BlockSpec((tk, 128), lambda i: (i, 0))],
        out_specs=_full((N_DEV, CHUNK_ROWS, D_MODEL)),
        out_shape=jax.ShapeDtypeStruct((N_DEV, CHUNK_ROWS, D_MODEL), BF16),
        scratch_shapes=[pltpu.VMEM((D_MODEL, MY_WIDTH), F32)],
        compiler_params=_params(("arbitrary",)),
    )(h16, *pieces, d_f)


def _adamw(w, g, m, v):
    m = ADAM_B1 * m + (1.0 - ADAM_B1) * g
    v = ADAM_B2 * v + (1.0 - ADAM_B2) * (g * g)
    m_hat = m / (1.0 - ADAM_B1 ** ADAM_STEP)
    v_hat = v / (1.0 - ADAM_B2 ** ADAM_STEP)
    return -ADAM_LR * (m_hat / (jnp.sqrt(v_hat) + ADAM_EPS) + ADAM_WD * w), m, v


PARAM_ORDER = ("norm_g", "w_in", "b_f", "w_pool", "pool_scale", "fox_q_g", "fox_k_g", "mem_norm_g", "w_mem_kv", "mem_q_g", "mem_k_g", "w_out")


def _update(red_in, red_kv, red_out, sred, params):
    def body(rin_ref, rkv_ref, rout_ref, sred_ref, *refs):
        ins, outs = refs[:3 * len(PARAM_ORDER)], refs[3 * len(PARAM_ORDER):]
        wide = lambda row: jnp.concatenate([sred_ref[row + k:row + k + 1, :] for k in range(4)], axis=1)
        head = lambda row: sred_ref[row:row + 1, 0:HEAD_DIM]
        grads = {
            "norm_g": lambda: wide(SB_NORM_G), "w_in": lambda: rin_ref[...], "b_f": lambda: sred_ref[SB_B_F:SB_B_F + 1, 0:FOX_HEADS],
            "pool_scale": lambda: sred_ref[SB_POOL_SCALE:SB_POOL_SCALE + 1, :], "fox_q_g": lambda: head(SB_FOX_Q),
            "fox_k_g": lambda: head(SB_FOX_K), "mem_norm_g": lambda: wide(SB_MEM_NORM_G), "w_mem_kv": lambda: rkv_ref[...],
            "mem_q_g": lambda: head(SB_MEM_Q), "mem_k_g": lambda: head(SB_MEM_K), "w_out": lambda: rout_ref[...],
        }
        for p, name in enumerate(PARAM_ORDER):
            w_ref, m_ref, v_ref = ins[3 * p:3 * p + 3]
            g_out, d_out, m_out, v_out = outs[4 * p:4 * p + 4]
            if name == "w_pool":
                for grp in range(4):
                    g = sred_ref[SB_W_POOL:SB_W_POOL + 64, 64 * grp:64 * (grp + 1)]
                    delta, m_new, v_new = _adamw(w_ref[grp], g, m_ref[grp], v_ref[grp])
                    g_out[grp], d_out[grp], m_out[grp], v_out[grp] = g, delta, m_new, v_new
            elif name == "w_in":
                for j in range(8):
                    rows = pl.ds(j, IN_CHUNK, stride=8)
                    g = rin_ref[0:IN_CHUNK, 128 * j:128 * (j + 1)]
                    delta, m_new, v_new = _adamw(w_ref[rows, :], g, m_ref[rows, :], v_ref[rows, :])
                    g_out[rows, :], d_out[rows, :], m_out[rows, :], v_out[rows, :] = g, delta, m_new, v_new
            else:
                g = grads[name]()
                delta, m_new, v_new = _adamw(w_ref[...], g, m_ref[...], v_ref[...])
                g_out[...], d_out[...], m_out[...], v_out[...] = g, delta, m_new, v_new

    flat = [t for name in PARAM_ORDER for t in params[name]]
    shapes = [params[name][0].shape for name in PARAM_ORDER for _ in range(4)]
    outs = pl.pallas_call(
        body,
        name="adamw_update",
        out_shape=tuple(jax.ShapeDtypeStruct(s, F32) for s in shapes),
        in_specs=[pl.BlockSpec(memory_space=pltpu.VMEM)] * (4 + len(flat)),
        out_specs=(pl.BlockSpec(memory_space=pltpu.VMEM),) * len(shapes),
        compiler_params=pltpu.CompilerParams(vmem_limit_bytes=VMEM_LIMIT),
    )(red_in, red_kv, red_out, sred, *flat)
    return {name: outs[4 * p:4 * p + 4] for p, name in enumerate(PARAM_ORDER)}


def kernel(x, mem, norm_g, w_in, b_f, w_pool, pool_scale, fox_q_g, fox_k_g, mem_norm_g, w_mem_kv, mem_q_g, mem_k_g, w_out, loss_target, m_norm_g, m_w_in, m_b_f, m_w_pool, m_pool_scale, m_fox_q_g, m_fox_k_g, m_mem_norm_g, m_w_mem_kv, m_mem_q_g, m_mem_k_g, m_w_out, v_norm_g, v_w_in, v_b_f, v_w_pool, v_pool_scale, v_fox_q_g, v_fox_k_g, v_mem_norm_g, v_w_mem_kv, v_mem_q_g, v_mem_k_g, v_w_out):
    given = dict(norm_g=(norm_g, m_norm_g, v_norm_g), w_in=(w_in, m_w_in, v_w_in), b_f=(b_f, m_b_f, v_b_f),
                 w_pool=(w_pool, m_w_pool, v_w_pool), pool_scale=(pool_scale, m_pool_scale, v_pool_scale),
                 fox_q_g=(fox_q_g, m_fox_q_g, v_fox_q_g), fox_k_g=(fox_k_g, m_fox_k_g, v_fox_k_g),
                 mem_norm_g=(mem_norm_g, m_mem_norm_g, v_mem_norm_g), w_mem_kv=(w_mem_kv, m_w_mem_kv, v_w_mem_kv),
                 mem_q_g=(mem_q_g, m_mem_q_g, v_mem_q_g), mem_k_g=(mem_k_g, m_mem_k_g, v_mem_k_g), w_out=(w_out, m_w_out, v_w_out))
    params = {name: tuple(t[0] if t.ndim > 2 else t for t in wmv) for name, wmv in given.items()}
    params["w_in"] = tuple(t.T.reshape(IN_CHUNK * 8, 128) for t in params["w_in"])
    x2, mem2, target2 = x[0], mem[0], loss_target[0]

    seg = jnp.asarray(np.kron(np.eye(FOX_HEADS), np.full((HEAD_DIM, HEAD_DIM), 1.0 / HEAD_DIM)), BF16)
    w_my, w_kv16, w_out16, wp_bd, bf_pad, gq8, gk8, gqm4 = _gather_w_in(
        params["w_in"][0], params["w_mem_kv"][0], params["w_out"][0], params["w_pool"][0], b_f, fox_q_g, fox_k_g, mem_q_g)
    proj_a, proj_b, h16, q_aug, k_aug, v_h, kt_aug, vt_aug = _proj_prep(x2, norm_g, w_my, bf_pad, gq8, gk8, seg)
    olse, lse_rows, w_kv_all, w_out_all = _fox_fwd(q_aug, k_aug, vt_aug, w_kv16, w_out16)
    km, vm = _mem_prep(mem2, mem_norm_g, w_kv_all, mem_k_g)
    d_out, d_mixed, dw_out, loss_blk = _mix_out_loss(proj_a, proj_b, olse, km, vm, wp_bd, pool_scale, gqm4, seg, x2, target2, w_out_all)

    d_ua, d_gb, d_qm, d_o, delta_rows, dwp_bd, d_scale, dkm, dvm, d_gqm = _mix_bwd(proj_a, proj_b, olse, d_mixed, km, vm, wp_bd,
                                                                                    pool_scale, gqm4, seg)
    dw_kv, d_mem_g, d_gkm = _mem_bwd(mem2, mem_norm_g, w_kv_all, mem_k_g, dkm, dvm)
    dq_aug, dk_aug, dv_h, land_kv, land_out = _fox_bwd(q_aug, k_aug, kt_aug, v_h, d_o, lse_rows, delta_rows, dw_kv, dw_out)
    d_q, d_k, d_v, d_f, d_gq, d_gk, d_bf = _fox_post(proj_a, proj_b, bf_pad, gq8, gk8, seg, dq_aug, dk_aug, dv_h)
    pieces = (d_ua, d_q, d_k, d_v, d_gb, d_qm)
    dw_in = _grad_w_in(h16, pieces, d_f)
    grad_x, red_in, red_kv, red_out, sred = _grad_x_exchange(pieces, d_f, w_my, x2, norm_g, d_out, dw_in, land_kv, land_out, d_mem_g,
                                                             d_scale, d_bf, d_gq, d_gk, d_gqm, d_gkm, dwp_bd, loss_blk)
    new = _update(red_in, red_kv, red_out, sred, params)
    result = [sred[SB_LOSS, 0], grad_x[None]]
    for kind in range(4):
        for name in PARAM_ORDER:
            out = new[name][kind]
            if name == "w_in":
                out = out.reshape(IN_CHUNK, D_MODEL).T
            result.append(out[None] if given[name][0].ndim > 2 else out)
    return tuple(result)
```
